```python
import math
import jax, jax.numpy as jnp
from jax import lax
import numpy as np

D_MODEL = 1024
BATCH = 8
SEQ = 4096
DEPTH = 1

N_HEADS = 8
QK_NOPE_DIM = 64
QK_ROPE_DIM = 32
QK_HEAD_DIM = QK_NOPE_DIM + QK_ROPE_DIM
V_HEAD_DIM = 64
Q_LORA_RANK = 384
KV_LORA_RANK = 256
ROPE_THETA = 10000.0
Q_BLOCK = 128
SSM_WIDTH = 512
SSM_GROUP = 16
SSM_GROUPS = SSM_WIDTH // SSM_GROUP
SSM_STATE = 64
DT_MIN = 0.001
DT_MAX = 0.1
N_BRANCHES = 2
D_FF = 2816
CONV_WIDTH = 3
EPS = 1e-6

OFF_CQ = 0
OFF_CKV = OFF_CQ + Q_LORA_RANK
OFF_KR = OFF_CKV + KV_LORA_RANK
OFF_U = OFF_KR + QK_ROPE_DIM
OFF_GATE = OFF_U + SSM_WIDTH
D_IN = OFF_GATE + N_BRANCHES * D_MODEL

kernel_name = "hybrid_mla_s5_gated_convffn"


def _rmsnorm(x, g):
    xf = x.astype(jnp.float32)
    y = xf * lax.rsqrt(jnp.mean(xf * xf, axis=-1, keepdims=True) + EPS)
    return (y * g.astype(jnp.float32)).astype(x.dtype)


def _rope(x, cos, sin):
    half = x.shape[-1] // 2
    x1, x2 = x[..., :half], x[..., half:]
    return jnp.concatenate([x1 * cos - x2 * sin, x2 * cos + x1 * sin], axis=-1)


def _rope_tables(positions, dtype):
    inv_freq = ROPE_THETA ** (-jnp.arange(0, QK_ROPE_DIM, 2, dtype=jnp.float32) / QK_ROPE_DIM)
    ang = positions.astype(jnp.float32)[..., None] * inv_freq
    return jnp.cos(ang).astype(dtype), jnp.sin(ang).astype(dtype)


def _mla(c_q, c_kv, k_r, cos, sin, q_norm, w_uq, kv_norm, w_uk, w_uv):
    B, L, _ = c_q.shape
    q = (_rmsnorm(c_q, q_norm) @ w_uq).reshape(B, L, N_HEADS, QK_HEAD_DIM)
    q_nope = q[..., :QK_NOPE_DIM]
    q_rope = _rope(q[..., QK_NOPE_DIM:], cos[:, :, None, :], sin[:, :, None, :])
    ckv = _rmsnorm(c_kv, kv_norm)
    k_nope = (ckv @ w_uk).reshape(B, L, N_HEADS, QK_NOPE_DIM)
    v = (ckv @ w_uv).reshape(B, L, N_HEADS, V_HEAD_DIM)
    k_rope = _rope(k_r, cos, sin)
    scale = 1.0 / math.sqrt(QK_HEAD_DIM)
    nblk = L // Q_BLOCK
    qn_blk = q_nope.reshape(B, nblk, Q_BLOCK, N_HEADS, QK_NOPE_DIM).transpose(1, 0, 2, 3, 4)
    qr_blk = q_rope.reshape(B, nblk, Q_BLOCK, N_HEADS, QK_ROPE_DIM).transpose(1, 0, 2, 3, 4)
    k_idx = jnp.arange(L)

    def block(args):
        qn, qr, i = args
        s = jnp.einsum('bqhd,bkhd->bhqk', qn, k_nope) + jnp.einsum('bqhd,bkd->bhqk', qr, k_rope)
        s = s.astype(jnp.float32) * scale
        q_idx = i * Q_BLOCK + jnp.arange(Q_BLOCK)
        s = jnp.where(k_idx[None, :] <= q_idx[:, None], s, -jnp.inf)
        p = jax.nn.softmax(s, axis=-1).astype(v.dtype)
        return jnp.einsum('bhqk,bkhd->bqhd', p, v)

    out = lax.map(block, (qn_blk, qr_blk, jnp.arange(nblk)))
    return out.transpose(1, 0, 2, 3, 4).reshape(B, L, N_HEADS * V_HEAD_DIM)


def _ssm_combine(e1, e2):
    a1r, a1i, b1r, b1i = e1
    a2r, a2i, b2r, b2i = e2
    return (a2r * a1r - a2i * a1i,
            a2r * a1i + a2i * a1r,
            a2r * b1r - a2i * b1i + b2r,
            a2r * b1i + a2i * b1r + b2i)


def _s5(u, lam_re, lam_im, log_dt, b_re, b_im, c_re, c_im, d_skip, w_glu, b_glu):
    B, L, _ = u.shape
    uf = u.astype(jnp.float32).reshape(B, L, SSM_GROUPS, SSM_GROUP)
    dt = jnp.exp(log_dt.astype(jnp.float32))[:, None]
    lr = lam_re.astype(jnp.float32)
    li = lam_im.astype(jnp.float32)
    mag = jnp.exp(lr * dt)
    ang = li * dt
    a_re, a_im = mag * jnp.cos(ang), mag * jnp.sin(ang)
    den = lr * lr + li * li
    n_re, n_im = a_re - 1.0, a_im
    z_re = (n_re * lr + n_im * li) / den
    z_im = (n_im * lr - n_re * li) / den
    br, bi = b_re.astype(jnp.float32), b_im.astype(jnp.float32)
    bb_re = z_re[..., None] * br - z_im[..., None] * bi
    bb_im = z_re[..., None] * bi + z_im[..., None] * br
    bu_re = jnp.einsum('blgh,gph->blgp', uf, bb_re)
    bu_im = jnp.einsum('blgh,gph->blgp', uf, bb_im)
    a_seq_re = jnp.broadcast_to(a_re, (L, SSM_GROUPS, SSM_STATE))
    a_seq_im = jnp.broadcast_to(a_im, (L, SSM_GROUPS, SSM_STATE))

    def scan_one(bur, bui):
        _, _, sr, si = lax.associative_scan(_ssm_combine, (a_seq_re, a_seq_im, bur, bui), axis=0)
        return sr, si

    s_re, s_im = jax.vmap(scan_one)(bu_re, bu_im)
    y = (jnp.einsum('blgp,ghp->blgh', s_re, c_re.astype(jnp.float32))
         - jnp.einsum('blgp,ghp->blgh', s_im, c_im.astype(jnp.float32))
         + d_skip.astype(jnp.float32) * uf)
    y = jax.nn.gelu(y.reshape(B, L, SSM_WIDTH), approximate=True)
    y = y * jax.nn.sigmoid(y @ w_glu.astype(jnp.float32) + b_glu.astype(jnp.float32))
    return y.astype(u.dtype)


def _causal_dwconv(h, w, b):
    L = h.shape[1]
    hp = jnp.pad(h, ((0, 0), (CONV_WIDTH - 1, 0), (0, 0)))
    out = b
    for k in range(CONV_WIDTH):
        out = out + w[k] * hp[:, k:k + L, :]
    return out


def _conv_ffn(x, w_up, conv_w, conv_b, w_down):
    h = _causal_dwconv(x @ w_up, conv_w, conv_b)
    gate, val = h[..., :D_FF], h[..., D_FF:]
    return (jax.nn.gelu(gate, approximate=True) * val) @ w_down


def _fwd_setup_inputs(seed: int = 0) -> dict:
    key = jax.random.key(seed)
    ks = iter(jax.random.split(key, 40))

    def nrm(shape, scale):
        return jax.random.normal(next(ks), shape, jnp.float32) * scale

    def gain(n):
        return 1.0 + nrm((DEPTH, n), 0.02)

    x = jax.random.normal(next(ks), (BATCH, SEQ, D_MODEL), jnp.float32)
    offs = jax.random.randint(next(ks), (BATCH, 1), 0, 1024, dtype=jnp.int32)
    positions = offs + jnp.arange(SEQ, dtype=jnp.int32)[None, :]
    lam_im0 = math.pi * jnp.arange(SSM_STATE, dtype=jnp.float32)
    return {
        "x": x,
        "positions": positions,
        "mix_norm_pre": gain(D_MODEL),
        "w_in": nrm((DEPTH, D_MODEL, D_IN), D_MODEL ** -0.5),
        "q_norm": gain(Q_LORA_RANK),
        "w_uq": nrm((DEPTH, Q_LORA_RANK, N_HEADS * QK_HEAD_DIM), Q_LORA_RANK ** -0.5),
        "kv_norm": gain(KV_LORA_RANK),
        "w_uk": nrm((DEPTH, KV_LORA_RANK, N_HEADS * QK_NOPE_DIM), KV_LORA_RANK ** -0.5),
        "w_uv": nrm((DEPTH, KV_LORA_RANK, N_HEADS * V_HEAD_DIM), KV_LORA_RANK ** -0.5),
        "ssm_lambda_re": -0.5 + nrm((DEPTH, SSM_GROUPS, SSM_STATE), 0.01),
        "ssm_lambda_im": lam_im0 + nrm((DEPTH, SSM_GROUPS, SSM_STATE), 0.01),
        "ssm_log_dt": jax.random.uniform(next(ks), (DEPTH, SSM_GROUPS), jnp.float32,
                                          math.log(DT_MIN), math.log(DT_MAX)),
        "ssm_b_re": nrm((DEPTH, SSM_GROUPS, SSM_STATE, SSM_GROUP), (2.0 * SSM_GROUP) ** -0.5),
        "ssm_b_im": nrm((DEPTH, SSM_GROUPS, SSM_STATE, SSM_GROUP), (2.0 * SSM_GROUP) ** -0.5),
        "ssm_c_re": nrm((DEPTH, SSM_GROUPS, SSM_GROUP, SSM_STATE), (2.0 * SSM_STATE) ** -0.5),
        "ssm_c_im": nrm((DEPTH, SSM_GROUPS, SSM_GROUP, SSM_STATE), (2.0 * SSM_STATE) ** -0.5),
        "ssm_d": nrm((DEPTH, SSM_GROUPS, SSM_GROUP), 1.0),
        "w_glu": nrm((DEPTH, SSM_WIDTH, SSM_WIDTH), SSM_WIDTH ** -0.5),
        "b_glu": nrm((DEPTH, SSM_WIDTH), 0.01),
        "w_branch_attn": nrm((DEPTH, N_HEADS * V_HEAD_DIM, D_MODEL), (N_HEADS * V_HEAD_DIM) ** -0.5),
        "w_branch_ssm": nrm((DEPTH, SSM_WIDTH, D_MODEL), SSM_WIDTH ** -0.5),
        "b_gate": nrm((DEPTH, N_BRANCHES * D_MODEL), 0.01),
        "w_out": nrm((DEPTH, D_MODEL, D_MODEL), D_MODEL ** -0.5),
        "mix_norm_post": gain(D_MODEL),
        "ffn_norm_pre": gain(D_MODEL),
        "w_up": nrm((DEPTH, D_MODEL, 2 * D_FF), D_MODEL ** -0.5),
        "conv_w": nrm((DEPTH, CONV_WIDTH, 2 * D_FF), CONV_WIDTH ** -0.5),
        "conv_b": nrm((DEPTH, 2 * D_FF), 0.01),
        "w_down": nrm((DEPTH, D_FF, D_MODEL), D_FF ** -0.5),
        "ffn_norm_post": gain(D_MODEL),
    }


def _fwd_reference(x, positions, mix_norm_pre, w_in, q_norm, w_uq, kv_norm, w_uk, w_uv,
              ssm_lambda_re, ssm_lambda_im, ssm_log_dt, ssm_b_re, ssm_b_im, ssm_c_re, ssm_c_im,
              ssm_d, w_glu, b_glu, w_branch_attn, w_branch_ssm, b_gate, w_out, mix_norm_post,
              ffn_norm_pre, w_up, conv_w, conv_b, w_down, ffn_norm_post):
    B, L, _ = x.shape
    cos, sin = _rope_tables(positions, x.dtype)
    for layer in range(DEPTH):
        hn = _rmsnorm(x, mix_norm_pre[layer])
        proj = hn @ w_in[layer]
        c_q = proj[..., OFF_CQ:OFF_CKV]
        c_kv = proj[..., OFF_CKV:OFF_KR]
        k_r = proj[..., OFF_KR:OFF_U]
        u = proj[..., OFF_U:OFF_GATE]
        gate_logits = proj[..., OFF_GATE:] + b_gate[layer]
        attn = _mla(c_q, c_kv, k_r, cos, sin, q_norm[layer], w_uq[layer],
                    kv_norm[layer], w_uk[layer], w_uv[layer])
        ssm = _s5(u, ssm_lambda_re[layer], ssm_lambda_im[layer], ssm_log_dt[layer],
                  ssm_b_re[layer], ssm_b_im[layer], ssm_c_re[layer], ssm_c_im[layer],
                  ssm_d[layer], w_glu[layer], b_glu[layer])
        gates = jax.nn.sigmoid(gate_logits.astype(jnp.float32)).astype(x.dtype)
        gates = gates.reshape(B, L, N_BRANCHES, D_MODEL)
        merged = (gates[:, :, 0] * (attn @ w_branch_attn[layer])
                  + gates[:, :, 1] * (ssm @ w_branch_ssm[layer]))
        x = x + _rmsnorm(merged @ w_out[layer], mix_norm_post[layer])
        hn = _rmsnorm(x, ffn_norm_pre[layer])
        ff = _conv_ffn(hn, w_up[layer], conv_w[layer], conv_b[layer], w_down[layer])
        x = x + _rmsnorm(ff, ffn_norm_post[layer])
    return x


import jax as _jax
import jax.numpy as _jnp

TWIN_FORMAT = 'train_step'
FWD_PARAMS = ['x', 'positions', 'mix_norm_pre', 'w_in', 'q_norm', 'w_uq', 'kv_norm', 'w_uk', 'w_uv', 'ssm_lambda_re', 'ssm_lambda_im', 'ssm_log_dt', 'ssm_b_re', 'ssm_b_im', 'ssm_c_re', 'ssm_c_im', 'ssm_d', 'w_glu', 'b_glu', 'w_branch_attn', 'w_branch_ssm', 'b_gate', 'w_out', 'mix_norm_post', 'ffn_norm_pre', 'w_up', 'conv_w', 'conv_b', 'w_down', 'ffn_norm_post']
TWIN_WEIGHTS = ['mix_norm_pre', 'w_in', 'q_norm', 'w_uq', 'kv_norm', 'w_uk', 'w_uv', 'ssm_lambda_re', 'ssm_lambda_im', 'ssm_log_dt', 'ssm_b_re', 'ssm_b_im', 'ssm_c_re', 'ssm_c_im', 'ssm_d', 'w_glu', 'b_glu', 'w_branch_attn', 'w_branch_ssm', 'b_gate', 'w_out', 'mix_norm_post', 'ffn_norm_pre', 'w_up', 'conv_w', 'conv_b', 'w_down', 'ffn_norm_post']
TWIN_DIFF_INPUT = 'x'
TWIN_INPUTS = ['x', 'positions', 'mix_norm_pre', 'w_in', 'q_norm', 'w_uq', 'kv_norm', 'w_uk', 'w_uv', 'ssm_lambda_re', 'ssm_lambda_im', 'ssm_log_dt', 'ssm_b_re', 'ssm_b_im', 'ssm_c_re', 'ssm_c_im', 'ssm_d', 'w_glu', 'b_glu', 'w_branch_attn', 'w_branch_ssm', 'b_gate', 'w_out', 'mix_norm_post', 'ffn_norm_pre', 'w_up', 'conv_w', 'conv_b', 'w_down', 'ffn_norm_post', 'loss_target', 'm_mix_norm_pre', 'm_w_in', 'm_q_norm', 'm_w_uq', 'm_kv_norm', 'm_w_uk', 'm_w_uv', 'm_ssm_lambda_re', 'm_ssm_lambda_im', 'm_ssm_log_dt', 'm_ssm_b_re', 'm_ssm_b_im', 'm_ssm_c_re', 'm_ssm_c_im', 'm_ssm_d', 'm_w_glu', 'm_b_glu', 'm_w_branch_attn', 'm_w_branch_ssm', 'm_b_gate', 'm_w_out', 'm_mix_norm_post', 'm_ffn_norm_pre', 'm_w_up', 'm_conv_w', 'm_conv_b', 'm_w_down', 'm_ffn_norm_post', 'v_mix_norm_pre', 'v_w_in', 'v_q_norm', 'v_w_uq', 'v_kv_norm', 'v_w_uk', 'v_w_uv', 'v_ssm_lambda_re', 'v_ssm_lambda_im', 'v_ssm_log_dt', 'v_ssm_b_re', 'v_ssm_b_im', 'v_ssm_c_re', 'v_ssm_c_im', 'v_ssm_d', 'v_w_glu', 'v_b_glu', 'v_w_branch_attn', 'v_w_branch_ssm', 'v_b_gate', 'v_w_out', 'v_mix_norm_post', 'v_ffn_norm_pre', 'v_w_up', 'v_conv_w', 'v_conv_b', 'v_w_down', 'v_ffn_norm_post']
TWIN_OUTPUTS = ['loss', 'grad_x', 'grad_mix_norm_pre', 'grad_w_in', 'grad_q_norm', 'grad_w_uq', 'grad_kv_norm', 'grad_w_uk', 'grad_w_uv', 'grad_ssm_lambda_re', 'grad_ssm_lambda_im', 'grad_ssm_log_dt', 'grad_ssm_b_re', 'grad_ssm_b_im', 'grad_ssm_c_re', 'grad_ssm_c_im', 'grad_ssm_d', 'grad_w_glu', 'grad_b_glu', 'grad_w_branch_attn', 'grad_w_branch_ssm', 'grad_b_gate', 'grad_w_out', 'grad_mix_norm_post', 'grad_ffn_norm_pre', 'grad_w_up', 'grad_conv_w', 'grad_conv_b', 'grad_w_down', 'grad_ffn_norm_post', 'delta_mix_norm_pre', 'delta_w_in', 'delta_q_norm', 'delta_w_uq', 'delta_kv_norm', 'delta_w_uk', 'delta_w_uv', 'delta_ssm_lambda_re', 'delta_ssm_lambda_im', 'delta_ssm_log_dt', 'delta_ssm_b_re', 'delta_ssm_b_im', 'delta_ssm_c_re', 'delta_ssm_c_im', 'delta_ssm_d', 'delta_w_glu', 'delta_b_glu', 'delta_w_branch_attn', 'delta_w_branch_ssm', 'delta_b_gate', 'delta_w_out', 'delta_mix_norm_post', 'delta_ffn_norm_pre', 'delta_w_up', 'delta_conv_w', 'delta_conv_b', 'delta_w_down', 'delta_ffn_norm_post', 'new_m_mix_norm_pre', 'new_m_w_in', 'new_m_q_norm', 'new_m_w_uq', 'new_m_kv_norm', 'new_m_w_uk', 'new_m_w_uv', 'new_m_ssm_lambda_re', 'new_m_ssm_lambda_im', 'new_m_ssm_log_dt', 'new_m_ssm_b_re', 'new_m_ssm_b_im', 'new_m_ssm_c_re', 'new_m_ssm_c_im', 'new_m_ssm_d', 'new_m_w_glu', 'new_m_b_glu', 'new_m_w_branch_attn', 'new_m_w_branch_ssm', 'new_m_b_gate', 'new_m_w_out', 'new_m_mix_norm_post', 'new_m_ffn_norm_pre', 'new_m_w_up', 'new_m_conv_w', 'new_m_conv_b', 'new_m_w_down', 'new_m_ffn_norm_post', 'new_v_mix_norm_pre', 'new_v_w_in', 'new_v_q_norm', 'new_v_w_uq', 'new_v_kv_norm', 'new_v_w_uk', 'new_v_w_uv', 'new_v_ssm_lambda_re', 'new_v_ssm_lambda_im', 'new_v_ssm_log_dt', 'new_v_ssm_b_re', 'new_v_ssm_b_im', 'new_v_ssm_c_re', 'new_v_ssm_c_im', 'new_v_ssm_d', 'new_v_w_glu', 'new_v_b_glu', 'new_v_w_branch_attn', 'new_v_w_branch_ssm', 'new_v_b_gate', 'new_v_w_out', 'new_v_mix_norm_post', 'new_v_ffn_norm_pre', 'new_v_w_up', 'new_v_conv_w', 'new_v_conv_b', 'new_v_w_down', 'new_v_ffn_norm_post']
TWIN_LEAF_KINDS = {'loss': 'loss', 'grad_x': 'grad_x', 'grad_mix_norm_pre': 'grad_w', 'grad_w_in': 'grad_w', 'grad_q_norm': 'grad_w', 'grad_w_uq': 'grad_w', 'grad_kv_norm': 'grad_w', 'grad_w_uk': 'grad_w', 'grad_w_uv': 'grad_w', 'grad_ssm_lambda_re': 'grad_w', 'grad_ssm_lambda_im': 'grad_w', 'grad_ssm_log_dt': 'grad_w', 'grad_ssm_b_re': 'grad_w', 'grad_ssm_b_im': 'grad_w', 'grad_ssm_c_re': 'grad_w', 'grad_ssm_c_im': 'grad_w', 'grad_ssm_d': 'grad_w', 'grad_w_glu': 'grad_w', 'grad_b_glu': 'grad_w', 'grad_w_branch_attn': 'grad_w', 'grad_w_branch_ssm': 'grad_w', 'grad_b_gate': 'grad_w', 'grad_w_out': 'grad_w', 'grad_mix_norm_post': 'grad_w', 'grad_ffn_norm_pre': 'grad_w', 'grad_w_up': 'grad_w', 'grad_conv_w': 'grad_w', 'grad_conv_b': 'grad_w', 'grad_w_down': 'grad_w', 'grad_ffn_norm_post': 'grad_w', 'delta_mix_norm_pre': 'delta_w', 'delta_w_in': 'delta_w', 'delta_q_norm': 'delta_w', 'delta_w_uq': 'delta_w', 'delta_kv_norm': 'delta_w', 'delta_w_uk': 'delta_w', 'delta_w_uv': 'delta_w', 'delta_ssm_lambda_re': 'delta_w', 'delta_ssm_lambda_im': 'delta_w', 'delta_ssm_log_dt': 'delta_w', 'delta_ssm_b_re': 'delta_w', 'delta_ssm_b_im': 'delta_w', 'delta_ssm_c_re': 'delta_w', 'delta_ssm_c_im': 'delta_w', 'delta_ssm_d': 'delta_w', 'delta_w_glu': 'delta_w', 'delta_b_glu': 'delta_w', 'delta_w_branch_attn': 'delta_w', 'delta_w_branch_ssm': 'delta_w', 'delta_b_gate': 'delta_w', 'delta_w_out': 'delta_w', 'delta_mix_norm_post': 'delta_w', 'delta_ffn_norm_pre': 'delta_w', 'delta_w_up': 'delta_w', 'delta_conv_w': 'delta_w', 'delta_conv_b': 'delta_w', 'delta_w_down': 'delta_w', 'delta_ffn_norm_post': 'delta_w', 'new_m_mix_norm_pre': 'new_m', 'new_m_w_in': 'new_m', 'new_m_q_norm': 'new_m', 'new_m_w_uq': 'new_m', 'new_m_kv_norm': 'new_m', 'new_m_w_uk': 'new_m', 'new_m_w_uv': 'new_m', 'new_m_ssm_lambda_re': 'new_m', 'new_m_ssm_lambda_im': 'new_m', 'new_m_ssm_log_dt': 'new_m', 'new_m_ssm_b_re': 'new_m', 'new_m_ssm_b_im': 'new_m', 'new_m_ssm_c_re': 'new_m', 'new_m_ssm_c_im': 'new_m', 'new_m_ssm_d': 'new_m', 'new_m_w_glu': 'new_m', 'new_m_b_glu': 'new_m', 'new_m_w_branch_attn': 'new_m', 'new_m_w_branch_ssm': 'new_m', 'new_m_b_gate': 'new_m', 'new_m_w_out': 'new_m', 'new_m_mix_norm_post': 'new_m', 'new_m_ffn_norm_pre': 'new_m', 'new_m_w_up': 'new_m', 'new_m_conv_w': 'new_m', 'new_m_conv_b': 'new_m', 'new_m_w_down': 'new_m', 'new_m_ffn_norm_post': 'new_m', 'new_v_mix_norm_pre': 'new_v', 'new_v_w_in': 'new_v', 'new_v_q_norm': 'new_v', 'new_v_w_uq': 'new_v', 'new_v_kv_norm': 'new_v', 'new_v_w_uk': 'new_v', 'new_v_w_uv': 'new_v', 'new_v_ssm_lambda_re': 'new_v', 'new_v_ssm_lambda_im': 'new_v', 'new_v_ssm_log_dt': 'new_v', 'new_v_ssm_b_re': 'new_v', 'new_v_ssm_b_im': 'new_v', 'new_v_ssm_c_re': 'new_v', 'new_v_ssm_c_im': 'new_v', 'new_v_ssm_d': 'new_v', 'new_v_w_glu': 'new_v', 'new_v_b_glu': 'new_v', 'new_v_w_branch_attn': 'new_v', 'new_v_w_branch_ssm': 'new_v', 'new_v_b_gate': 'new_v', 'new_v_w_out': 'new_v', 'new_v_mix_norm_post': 'new_v', 'new_v_ffn_norm_pre': 'new_v', 'new_v_w_up': 'new_v', 'new_v_conv_w': 'new_v', 'new_v_conv_b': 'new_v', 'new_v_w_down': 'new_v', 'new_v_ffn_norm_post': 'new_v'}


def _forward(args):
    return _fwd_reference(*[args[k] for k in FWD_PARAMS])


def _output_shape():
    out = _jax.eval_shape(lambda: _forward(_fwd_setup_inputs(0)))
    return out.shape, out.dtype

N_MICROBATCH = 1
ADAM_LR = 0.001
ADAM_B1 = 0.9
ADAM_B2 = 0.999
ADAM_EPS = 1e-08
ADAM_WD = 0.01
ADAM_STEP = 10
PER_EXAMPLE_BATCH_AXIS = {'x': 0, 'positions': 0, 'loss_target': 0}
SHARED_INPUTS = []
_WEIGHT_DTYPES = {'mix_norm_pre': _jnp.float32, 'w_in': _jnp.float32, 'q_norm': _jnp.float32, 'w_uq': _jnp.float32, 'kv_norm': _jnp.float32, 'w_uk': _jnp.float32, 'w_uv': _jnp.float32, 'ssm_lambda_re': _jnp.float32, 'ssm_lambda_im': _jnp.float32, 'ssm_log_dt': _jnp.float32, 'ssm_b_re': _jnp.float32, 'ssm_b_im': _jnp.float32, 'ssm_c_re': _jnp.float32, 'ssm_c_im': _jnp.float32, 'ssm_d': _jnp.float32, 'w_glu': _jnp.float32, 'b_glu': _jnp.float32, 'w_branch_attn': _jnp.float32, 'w_branch_ssm': _jnp.float32, 'b_gate': _jnp.float32, 'w_out': _jnp.float32, 'mix_norm_post': _jnp.float32, 'ffn_norm_pre': _jnp.float32, 'w_up': _jnp.float32, 'conv_w': _jnp.float32, 'conv_b': _jnp.float32, 'w_down': _jnp.float32, 'ffn_norm_post': _jnp.float32}
MOMENT_SCALE = {'mix_norm_pre': 6.518005e-01, 'w_in': 3.630278e-01, 'q_norm': 3.362899e-01, 'w_uq': 2.253212e-01, 'kv_norm': 6.404072e-01, 'w_uk': 2.307212e-01, 'w_uv': 3.568989e-01, 'ssm_lambda_re': 4.945748e-02, 'ssm_lambda_im': 3.502747e-02, 'ssm_log_dt': 4.944086e+01, 'ssm_b_re': 2.489363e-02, 'ssm_b_im': 2.364110e-02, 'ssm_c_re': 5.002212e-02, 'ssm_c_im': 5.170638e-02, 'ssm_d': 3.836940e+00, 'w_glu': 4.486723e-01, 'b_glu': 1.553602e+00, 'w_branch_attn': 2.534563e-01, 'w_branch_ssm': 2.708406e+00, 'b_gate': 7.379723e-01, 'w_out': 2.379665e+00, 'mix_norm_post': 3.290214e+01, 'ffn_norm_pre': 1.886753e+00, 'w_up': 7.156300e-01, 'conv_w': 9.695747e-01, 'conv_b': 3.055728e+00, 'w_down': 1.690135e+00, 'ffn_norm_post': 3.202713e+01}


def _to_microbatches(a, axis):
    t = _jnp.moveaxis(a, axis, 0)
    t = t.reshape((N_MICROBATCH, t.shape[0] // N_MICROBATCH) + t.shape[1:])
    return _jnp.moveaxis(t, 1, axis + 1)


def setup_inputs(seed: int = 0) -> dict:
    inp = _fwd_setup_inputs(seed)
    key = _jax.random.fold_in(_jax.random.key(seed), 7919)
    shape, _ = _output_shape()
    out = dict(inp)
    out["loss_target"] = _jax.random.normal(_jax.random.fold_in(key, 0), shape, _jnp.float32)
    for i, name in enumerate(TWIN_WEIGHTS):
        w = inp[name].astype(_jnp.float32)
        if MOMENT_SCALE is None:
            s = _jnp.sqrt(_jnp.mean(_jnp.square(w)) + 1e-30)
        else:
            s = MOMENT_SCALE[name]
        km, kv = _jax.random.split(_jax.random.fold_in(key, i + 1))
        out[name] = w
        out["m_" + name] = s * _jax.random.normal(km, w.shape, _jnp.float32)
        out["v_" + name] = (s * s) * _jax.random.uniform(kv, w.shape, _jnp.float32, 0.5, 1.5)
    if N_MICROBATCH > 1:
        for name, axis in PER_EXAMPLE_BATCH_AXIS.items():
            out[name] = _to_microbatches(out[name], axis)
    return {'x': out['x'], 'positions': out['positions'], 'mix_norm_pre': out['mix_norm_pre'], 'w_in': out['w_in'], 'q_norm': out['q_norm'], 'w_uq': out['w_uq'], 'kv_norm': out['kv_norm'], 'w_uk': out['w_uk'], 'w_uv': out['w_uv'], 'ssm_lambda_re': out['ssm_lambda_re'], 'ssm_lambda_im': out['ssm_lambda_im'], 'ssm_log_dt': out['ssm_log_dt'], 'ssm_b_re': out['ssm_b_re'], 'ssm_b_im': out['ssm_b_im'], 'ssm_c_re': out['ssm_c_re'], 'ssm_c_im': out['ssm_c_im'], 'ssm_d': out['ssm_d'], 'w_glu': out['w_glu'], 'b_glu': out['b_glu'], 'w_branch_attn': out['w_branch_attn'], 'w_branch_ssm': out['w_branch_ssm'], 'b_gate': out['b_gate'], 'w_out': out['w_out'], 'mix_norm_post': out['mix_norm_post'], 'ffn_norm_pre': out['ffn_norm_pre'], 'w_up': out['w_up'], 'conv_w': out['conv_w'], 'conv_b': out['conv_b'], 'w_down': out['w_down'], 'ffn_norm_post': out['ffn_norm_post'], 'loss_target': out['loss_target'], 'm_mix_norm_pre': out['m_mix_norm_pre'], 'm_w_in': out['m_w_in'], 'm_q_norm': out['m_q_norm'], 'm_w_uq': out['m_w_uq'], 'm_kv_norm': out['m_kv_norm'], 'm_w_uk': out['m_w_uk'], 'm_w_uv': out['m_w_uv'], 'm_ssm_lambda_re': out['m_ssm_lambda_re'], 'm_ssm_lambda_im': out['m_ssm_lambda_im'], 'm_ssm_log_dt': out['m_ssm_log_dt'], 'm_ssm_b_re': out['m_ssm_b_re'], 'm_ssm_b_im': out['m_ssm_b_im'], 'm_ssm_c_re': out['m_ssm_c_re'], 'm_ssm_c_im': out['m_ssm_c_im'], 'm_ssm_d': out['m_ssm_d'], 'm_w_glu': out['m_w_glu'], 'm_b_glu': out['m_b_glu'], 'm_w_branch_attn': out['m_w_branch_attn'], 'm_w_branch_ssm': out['m_w_branch_ssm'], 'm_b_gate': out['m_b_gate'], 'm_w_out': out['m_w_out'], 'm_mix_norm_post': out['m_mix_norm_post'], 'm_ffn_norm_pre': out['m_ffn_norm_pre'], 'm_w_up': out['m_w_up'], 'm_conv_w': out['m_conv_w'], 'm_conv_b': out['m_conv_b'], 'm_w_down': out['m_w_down'], 'm_ffn_norm_post': out['m_ffn_norm_post'], 'v_mix_norm_pre': out['v_mix_norm_pre'], 'v_w_in': out['v_w_in'], 'v_q_norm': out['v_q_norm'], 'v_w_uq': out['v_w_uq'], 'v_kv_norm': out['v_kv_norm'], 'v_w_uk': out['v_w_uk'], 'v_w_uv': out['v_w_uv'], 'v_ssm_lambda_re': out['v_ssm_lambda_re'], 'v_ssm_lambda_im': out['v_ssm_lambda_im'], 'v_ssm_log_dt': out['v_ssm_log_dt'], 'v_ssm_b_re': out['v_ssm_b_re'], 'v_ssm_b_im': out['v_ssm_b_im'], 'v_ssm_c_re': out['v_ssm_c_re'], 'v_ssm_c_im': out['v_ssm_c_im'], 'v_ssm_d': out['v_ssm_d'], 'v_w_glu': out['v_w_glu'], 'v_b_glu': out['v_b_glu'], 'v_w_branch_attn': out['v_w_branch_attn'], 'v_w_branch_ssm': out['v_w_branch_ssm'], 'v_b_gate': out['v_b_gate'], 'v_w_out': out['v_w_out'], 'v_mix_norm_post': out['v_mix_norm_post'], 'v_ffn_norm_pre': out['v_ffn_norm_pre'], 'v_w_up': out['v_w_up'], 'v_conv_w': out['v_conv_w'], 'v_conv_b': out['v_conv_b'], 'v_w_down': out['v_w_down'], 'v_ffn_norm_post': out['v_ffn_norm_post']}


def _loss(weights, diff, rest, loss_target):
    with _jax.named_scope("forward"):
        args = {**rest, TWIN_DIFF_INPUT: diff, **{k: w.astype(_WEIGHT_DTYPES[k]) for k, w in weights.items()}}
        y = _forward(args)
    with _jax.named_scope("loss_head"):
        err = _jnp.square(y.astype(_jnp.float32) - loss_target)
        return 0.5 * _jnp.sum(_jnp.mean(err, axis=-1)) if err.ndim else 0.5 * err


def _adamw(w, g, m, v):
    m = ADAM_B1 * m + (1.0 - ADAM_B1) * g
    v = ADAM_B2 * v + (1.0 - ADAM_B2) * _jnp.square(g)
    m_hat = m / (1.0 - ADAM_B1 ** ADAM_STEP)
    v_hat = v / (1.0 - ADAM_B2 ** ADAM_STEP)
    delta = -ADAM_LR * (m_hat / (_jnp.sqrt(v_hat) + ADAM_EPS) + ADAM_WD * w)
    return delta, m, v


def reference(x, positions, mix_norm_pre, w_in, q_norm, w_uq, kv_norm, w_uk, w_uv, ssm_lambda_re, ssm_lambda_im, ssm_log_dt, ssm_b_re, ssm_b_im, ssm_c_re, ssm_c_im, ssm_d, w_glu, b_glu, w_branch_attn, w_branch_ssm, b_gate, w_out, mix_norm_post, ffn_norm_pre, w_up, conv_w, conv_b, w_down, ffn_norm_post, loss_target, m_mix_norm_pre, m_w_in, m_q_norm, m_w_uq, m_kv_norm, m_w_uk, m_w_uv, m_ssm_lambda_re, m_ssm_lambda_im, m_ssm_log_dt, m_ssm_b_re, m_ssm_b_im, m_ssm_c_re, m_ssm_c_im, m_ssm_d, m_w_glu, m_b_glu, m_w_branch_attn, m_w_branch_ssm, m_b_gate, m_w_out, m_mix_norm_post, m_ffn_norm_pre, m_w_up, m_conv_w, m_conv_b, m_w_down, m_ffn_norm_post, v_mix_norm_pre, v_w_in, v_q_norm, v_w_uq, v_kv_norm, v_w_uk, v_w_uv, v_ssm_lambda_re, v_ssm_lambda_im, v_ssm_log_dt, v_ssm_b_re, v_ssm_b_im, v_ssm_c_re, v_ssm_c_im, v_ssm_d, v_w_glu, v_b_glu, v_w_branch_attn, v_w_branch_ssm, v_b_gate, v_w_out, v_mix_norm_post, v_ffn_norm_pre, v_w_up, v_conv_w, v_conv_b, v_w_down, v_ffn_norm_post):
    given = dict(x=x, positions=positions, mix_norm_pre=mix_norm_pre, w_in=w_in, q_norm=q_norm, w_uq=w_uq, kv_norm=kv_norm, w_uk=w_uk, w_uv=w_uv, ssm_lambda_re=ssm_lambda_re, ssm_lambda_im=ssm_lambda_im, ssm_log_dt=ssm_log_dt, ssm_b_re=ssm_b_re, ssm_b_im=ssm_b_im, ssm_c_re=ssm_c_re, ssm_c_im=ssm_c_im, ssm_d=ssm_d, w_glu=w_glu, b_glu=b_glu, w_branch_attn=w_branch_attn, w_branch_ssm=w_branch_ssm, b_gate=b_gate, w_out=w_out, mix_norm_post=mix_norm_post, ffn_norm_pre=ffn_norm_pre, w_up=w_up, conv_w=conv_w, conv_b=conv_b, w_down=w_down, ffn_norm_post=ffn_norm_post, loss_target=loss_target, m_mix_norm_pre=m_mix_norm_pre, m_w_in=m_w_in, m_q_norm=m_q_norm, m_w_uq=m_w_uq, m_kv_norm=m_kv_norm, m_w_uk=m_w_uk, m_w_uv=m_w_uv, m_ssm_lambda_re=m_ssm_lambda_re, m_ssm_lambda_im=m_ssm_lambda_im, m_ssm_log_dt=m_ssm_log_dt, m_ssm_b_re=m_ssm_b_re, m_ssm_b_im=m_ssm_b_im, m_ssm_c_re=m_ssm_c_re, m_ssm_c_im=m_ssm_c_im, m_ssm_d=m_ssm_d, m_w_glu=m_w_glu, m_b_glu=m_b_glu, m_w_branch_attn=m_w_branch_attn, m_w_branch_ssm=m_w_branch_ssm, m_b_gate=m_b_gate, m_w_out=m_w_out, m_mix_norm_post=m_mix_norm_post, m_ffn_norm_pre=m_ffn_norm_pre, m_w_up=m_w_up, m_conv_w=m_conv_w, m_conv_b=m_conv_b, m_w_down=m_w_down, m_ffn_norm_post=m_ffn_norm_post, v_mix_norm_pre=v_mix_norm_pre, v_w_in=v_w_in, v_q_norm=v_q_norm, v_w_uq=v_w_uq, v_kv_norm=v_kv_norm, v_w_uk=v_w_uk, v_w_uv=v_w_uv, v_ssm_lambda_re=v_ssm_lambda_re, v_ssm_lambda_im=v_ssm_lambda_im, v_ssm_log_dt=v_ssm_log_dt, v_ssm_b_re=v_ssm_b_re, v_ssm_b_im=v_ssm_b_im, v_ssm_c_re=v_ssm_c_re, v_ssm_c_im=v_ssm_c_im, v_ssm_d=v_ssm_d, v_w_glu=v_w_glu, v_b_glu=v_b_glu, v_w_branch_attn=v_w_branch_attn, v_w_branch_ssm=v_w_branch_ssm, v_b_gate=v_b_gate, v_w_out=v_w_out, v_mix_norm_post=v_mix_norm_post, v_ffn_norm_pre=v_ffn_norm_pre, v_w_up=v_w_up, v_conv_w=v_conv_w, v_conv_b=v_conv_b, v_w_down=v_w_down, v_ffn_norm_post=v_ffn_norm_post)
    weights = {n: given[n] for n in TWIN_WEIGHTS}
    shared = {n: given[n] for n in SHARED_INPUTS}
    per_example = {n: given[n] for n in ['x', 'positions']}
    grad_fn = _jax.value_and_grad(_loss, argnums=(0, 1))

    def one_microbatch(ex, loss_target):
        ex = dict(ex)
        diff = ex.pop(TWIN_DIFF_INPUT)
        return grad_fn(weights, diff, {**shared, **ex}, loss_target)

    if N_MICROBATCH == 1:
        loss, (grad_w, grad_x) = one_microbatch(per_example, given["loss_target"])
    else:
        def body(carry, xs):
            loss_sum, grad_sum = carry
            l_k, (gw_k, gx_k) = one_microbatch(xs[0], xs[1])
            with _jax.named_scope("update"):
                return (loss_sum + l_k, _jax.tree.map(_jnp.add, grad_sum, gw_k)), gx_k

        init = (_jnp.zeros((), _jnp.float32), _jax.tree.map(_jnp.zeros_like, weights))
        (loss, grad_w), grad_x = _jax.lax.scan(body, init, (per_example, given["loss_target"]))
    with _jax.named_scope("update"):
        delta_w, new_m, new_v = {}, {}, {}
        for n in TWIN_WEIGHTS:
            delta_w[n], new_m[n], new_v[n] = _adamw(weights[n], grad_w[n], given["m_" + n], given["v_" + n])
    return (loss, grad_x, *[grad_w[n] for n in TWIN_WEIGHTS], *[delta_w[n] for n in TWIN_WEIGHTS],
            *[new_m[n] for n in TWIN_WEIGHTS], *[new_v[n] for n in TWIN_WEIGHTS])
```

```python
import math

import jax
import jax.numpy as jnp
from jax import lax
from jax.experimental import pallas as pl
from jax.experimental.pallas import tpu as pltpu

F32 = jnp.float32
BF16 = jnp.bfloat16
MESH = pl.DeviceIdType.MESH

D_MODEL = 1024
N_HEADS = 8
QK_NOPE = 64
QK_ROPE = 32
QK_HEAD = QK_NOPE + QK_ROPE
V_HEAD = 64
Q_RANK = 384
KV_RANK = 256
ROPE_THETA = 10000.0
SSM_W = 512
SSM_H = 16
SSM_G = 32
SSM_P = 64
SSM_CH = SSM_G * SSM_P
D_FF = 2816
EPS = 1e-6
ADAM_LR = 0.001
ADAM_B1 = 0.9
ADAM_B2 = 0.999
ADAM_EPS = 1e-08
ADAM_WD = 0.01
ADAM_STEP = 10

LANES = 128
SUBLANES = 8
VMEM_LIMIT = 56 * 1024 * 1024

HEAD_SLOT = LANES
HP = N_HEADS * HEAD_SLOT
P_CQ, P_CKV, P_KR, P_U, P_GL, P_END = 0, 384, 640, 768, 1280, 3328
KR_LANE = 64

FLAT_W = 1024
N_CHIPS = 4


def _tile(n, cap):
    if n <= cap:
        return n
    best = None
    for t in range(LANES, cap + 1, LANES):
        if n % t == 0:
            best = t
    assert best is not None, (n, cap)
    return best


def _params(sem):
    return pltpu.CompilerParams(dimension_semantics=sem, vmem_limit_bytes=VMEM_LIMIT)


def _dot(a, b):
    return jnp.dot(a, b, preferred_element_type=F32)


def _dot_nt(a, b):
    return lax.dot_general(a, b, (((1,), (1,)), ((), ())), preferred_element_type=F32)


def _dot_tn(a, b):
    return lax.dot_general(a, b, (((0,), (0,)), ((), ())), preferred_element_type=F32)


def _rms(x, g):
    r = lax.rsqrt(jnp.mean(x * x, axis=-1, keepdims=True) + EPS)
    return x * r * g, r


def _rms_bwd(dy, x, g):
    r = lax.rsqrt(jnp.mean(x * x, axis=-1, keepdims=True) + EPS)
    dyg = dy * g
    dx = r * dyg - x * (r * r * r) * jnp.mean(dyg * x, axis=-1, keepdims=True)
    dg = jnp.sum(dy * x * r, axis=0, keepdims=True)
    return dx, dg


_GELU_K0 = math.sqrt(2.0 / math.pi)
_GELU_K1 = 0.044715


def _gelu(x):
    th = jnp.tanh(_GELU_K0 * (x + _GELU_K1 * x * x * x))
    return 0.5 * x * (1.0 + th)


def _gelu_grad(x):
    th = jnp.tanh(_GELU_K0 * (x + _GELU_K1 * x * x * x))
    return 0.5 * (1.0 + th) + 0.5 * x * (1.0 - th * th) * _GELU_K0 * (1.0 + 3.0 * _GELU_K1 * x * x)


def _sigmoid(x):
    return 1.0 / (1.0 + jnp.exp(-x))


def _rope(q, c, s):
    n = q.shape[1]
    lane = lax.broadcasted_iota(jnp.int32, q.shape, 1) % HEAD_SLOT
    sw = jnp.where(lane < KR_LANE + QK_ROPE // 2, pltpu.roll(q, n - QK_ROPE // 2, 1), pltpu.roll(q, QK_ROPE // 2, 1))
    return q * c + sw * s


def _rope_bwd(dy, c, s):
    n = dy.shape[1]
    t = dy * s
    lane = lax.broadcasted_iota(jnp.int32, dy.shape, 1) % HEAD_SLOT
    sw = jnp.where(lane < KR_LANE + QK_ROPE // 2, pltpu.roll(t, n - QK_ROPE // 2, 1), pltpu.roll(t, QK_ROPE // 2, 1))
    rope_lane = jnp.logical_and(lane >= KR_LANE, lane < KR_LANE + QK_ROPE)
    return dy * c + jnp.where(rope_lane, sw, 0.0)


def _shift_down(x, k, halo):
    xs = pltpu.roll(x, k, 0)
    hs = pltpu.roll(halo, k, 0)
    rows = lax.broadcasted_iota(jnp.int32, halo.shape, 0)
    top = jnp.where(rows < k, hs, xs[0:SUBLANES])
    return jnp.concatenate([top, xs[SUBLANES:]], axis=0)


def _shift_up(x, k, halo):
    t = x.shape[0]
    xs = pltpu.roll(x, t - k, 0)
    hs = pltpu.roll(halo, SUBLANES - k, 0)
    rows = lax.broadcasted_iota(jnp.int32, halo.shape, 0)
    bot = jnp.where(rows >= SUBLANES - k, hs, xs[t - SUBLANES:])
    return jnp.concatenate([xs[:t - SUBLANES], bot], axis=0)


def _mm(a, b, name, out_dtype=F32, bt=False, b_col0=0, n=None, tm_cap=512, tn_cap=1408):
    m, k = a.shape
    if bt:
        n_full = b.shape[0]
        n = n_full
    else:
        n = b.shape[1] if n is None else n
    tm = min(tm_cap, m)
    tn = _tile(n, tn_cap)

    def body(a_ref, b_ref, o_ref):
        if bt:
            o_ref[...] = _dot_nt(a_ref[...], b_ref[...]).astype(out_dtype)
        else:
            o_ref[...] = _dot(a_ref[...], b_ref[...]).astype(out_dtype)

    if bt:
        b_spec = pl.BlockSpec((tn, k), lambda j, i: (j, b_col0))
    else:
        off = b_col0 * (n // tn)
        b_spec = pl.BlockSpec((k, tn), lambda j, i: (0, off + j))
    return pl.pallas_call(
        body, name=name, grid=(n // tn, m // tm),
        in_specs=[pl.BlockSpec((tm, k), lambda j, i: (i, 0)), b_spec],
        out_specs=pl.BlockSpec((tm, tn), lambda j, i: (i, j)),
        out_shape=jax.ShapeDtypeStruct((m, n), out_dtype),
        compiler_params=_params(("parallel", "parallel")),
    )(a, b)


def _mm_tn(a, b, name, tk_cap=512, tn_cap=1664, tl_cap=1024):
    l, k = a.shape
    n = b.shape[1]
    tk = _tile(k, tk_cap)
    tn = _tile(n, tn_cap)
    tl = min(tl_cap, l)

    def body(a_ref, b_ref, o_ref):
        @pl.when(pl.program_id(2) == 0)
        def _():
            o_ref[...] = jnp.zeros_like(o_ref)

        o_ref[...] += _dot_tn(a_ref[...], b_ref[...])

    return pl.pallas_call(
        body, name=name, grid=(k // tk, n // tn, l // tl),
        in_specs=[pl.BlockSpec((tl, tk), lambda i, j, r: (r, i)), pl.BlockSpec((tl, tn), lambda i, j, r: (r, j))],
        out_specs=pl.BlockSpec((tk, tn), lambda i, j, r: (i, j)),
        out_shape=jax.ShapeDtypeStruct((k, n), F32),
        compiler_params=_params(("parallel", "parallel", "arbitrary")),
    )(a, b)


def _row(tl, n):
    return pl.BlockSpec((tl, n), lambda i: (i, 0))


def _const(shape):
    return pl.BlockSpec(shape, lambda i: tuple(0 for _ in shape))


def _proj_fwd(x, g1, win, gq, wuq, gkv, wukv, rc, rs, bg, tl):
    l = x.shape[0]

    def body(x_ref, g1_ref, win_ref, gq_ref, wuq_ref, gkv_ref, wukv_ref, rc_ref, rs_ref, bg_ref,
             hn_ref, cq_ref, ckv_ref, q_ref, k_ref, v_ref, u_ref, gl_ref):
        hn, _ = _rms(x_ref[...], g1_ref[...])
        hnb = hn.astype(BF16)
        hn_ref[...] = hnb
        proj = _dot(hnb, win_ref[...])
        cq = proj[:, P_CQ:P_CKV]
        ckv = proj[:, P_CKV:P_KR]
        kr = proj[:, P_KR:P_U]
        cq_ref[...] = cq
        ckv_ref[...] = ckv
        u_ref[...] = proj[:, P_U:P_GL]
        gl_ref[...] = proj[:, P_GL:P_END] + bg_ref[...]
        qn, _ = _rms(cq, gq_ref[...])
        q = _dot(qn.astype(BF16), wuq_ref[...])
        c1 = rc_ref[...]
        s1 = rs_ref[...]
        q_ref[...] = _rope(q, jnp.tile(c1, (1, N_HEADS)), jnp.tile(s1, (1, N_HEADS))).astype(BF16)
        ckvn, _ = _rms(ckv, gkv_ref[...])
        kv = _dot(ckvn.astype(BF16), wukv_ref[...])
        krr = _rope(kr, c1, s1)
        k_ref[...] = (kv[:, :HP] + jnp.tile(krr, (1, N_HEADS))).astype(BF16)
        v_ref[...] = kv[:, HP:].astype(BF16)

    outs = [(D_MODEL, BF16), (Q_RANK, F32), (KV_RANK, F32), (HP, BF16), (HP, BF16), (HP, BF16),
            (SSM_W, F32), (2 * D_MODEL, F32)]
    return pl.pallas_call(
        body, name="proj_fwd", grid=(l // tl,),
        in_specs=[_row(tl, D_MODEL), _const((1, D_MODEL)), _const((D_MODEL, P_END)), _const((1, Q_RANK)),
                  _const((Q_RANK, HP)), _const((1, KV_RANK)), _const((KV_RANK, 2 * HP)),
                  _row(tl, HEAD_SLOT), _row(tl, HEAD_SLOT), _const((1, 2 * D_MODEL))],
        out_specs=[_row(tl, n) for n, _ in outs],
        out_shape=[jax.ShapeDtypeStruct((l, n), dt) for n, dt in outs],
        compiler_params=_params(("parallel",)),
    )(x, g1, win, gq, wuq, gkv, wukv, rc, rs, bg)


_NEG = -1e30


def _attn_fwd(q, k, v, tq):
    l = q.shape[0]
    nq = l // tq
    scale = 1.0 / math.sqrt(QK_HEAD)

    def body(q_ref, k_ref, v_ref, o_ref, lse_ref, m_ref, l_ref, acc_ref):
        i = pl.program_id(1)
        j = pl.program_id(2)

        @pl.when(j == 0)
        def _():
            m_ref[...] = jnp.full_like(m_ref, _NEG)
            l_ref[...] = jnp.zeros_like(l_ref)
            acc_ref[...] = jnp.zeros_like(acc_ref)

        @pl.when(j <= i)
        def _():
            s = _dot_nt(q_ref[...], k_ref[...]) * scale
            rows = i * tq + lax.broadcasted_iota(jnp.int32, s.shape, 0)
            cols = j * tq + lax.broadcasted_iota(jnp.int32, s.shape, 1)
            s = jnp.where(cols <= rows, s, _NEG)
            m_old = m_ref[...]
            m_new = jnp.maximum(m_old, jnp.max(s, axis=-1, keepdims=True))
            p = jnp.exp(s - m_new)
            alpha = jnp.exp(m_old - m_new)
            l_ref[...] = alpha * l_ref[...] + jnp.sum(p, axis=-1, keepdims=True)
            acc_ref[...] = alpha * acc_ref[...] + _dot(p.astype(BF16), v_ref[...])
            m_ref[...] = m_new

        @pl.when(j == i)
        def _():
            o_ref[...] = (acc_ref[...] / l_ref[...]).astype(BF16)
            lse_ref[...] = jnp.broadcast_to(m_ref[...] + jnp.log(l_ref[...]), lse_ref.shape)

    blk = (tq, HEAD_SLOT)
    return pl.pallas_call(
        body, name="attn_fwd", grid=(N_HEADS, nq, nq),
        in_specs=[pl.BlockSpec(blk, lambda h, i, j: (i, h)),
                  pl.BlockSpec(blk, lambda h, i, j: (jnp.minimum(j, i), h)),
                  pl.BlockSpec(blk, lambda h, i, j: (jnp.minimum(j, i), h))],
        out_specs=[pl.BlockSpec(blk, lambda h, i, j: (i, h)), pl.BlockSpec(blk, lambda h, i, j: (i, h))],
        out_shape=[jax.ShapeDtypeStruct((l, HP), BF16), jax.ShapeDtypeStruct((l, HP), F32)],
        scratch_shapes=[pltpu.VMEM((tq, 1), F32), pltpu.VMEM((tq, 1), F32), pltpu.VMEM((tq, HEAD_SLOT), F32)],
        compiler_params=_params(("parallel", "parallel", "arbitrary")),
    )(q, k, v)


def _attn_bwd(q, k, v, o, do, lse, tq):
    l = q.shape[0]
    nq = l // tq
    scale = 1.0 / math.sqrt(QK_HEAD)

    def body(q_ref, k_ref, v_ref, o_ref, do_ref, lse_ref, dq_ref, dk_ref, dv_ref, dka_ref, dva_ref):
        j = pl.program_id(1)
        i = pl.program_id(2)

        @pl.when(jnp.logical_and(j == 0, i == 0))
        def _():
            dq_ref[...] = jnp.zeros_like(dq_ref)

        @pl.when(i == 0)
        def _():
            dka_ref[...] = jnp.zeros_like(dka_ref)
            dva_ref[...] = jnp.zeros_like(dva_ref)

        @pl.when(i >= j)
        def _():
            qb = q_ref[...]
            kb = k_ref[...]
            dob = do_ref[...]
            s = _dot_nt(qb, kb) * scale
            rows = i * tq + lax.broadcasted_iota(jnp.int32, s.shape, 0)
            cols = j * tq + lax.broadcasted_iota(jnp.int32, s.shape, 1)
            s = jnp.where(cols <= rows, s, _NEG)
            p = jnp.exp(s - lse_ref[:, 0:1])
            pb = p.astype(BF16)
            dva_ref[...] += _dot_tn(pb, dob)
            dp = _dot_nt(dob, v_ref[...])
            delta = jnp.sum(dob.astype(F32) * o_ref[...].astype(F32), axis=-1, keepdims=True)
            ds = (p * (dp - delta) * scale).astype(BF16)
            dka_ref[...] += _dot_tn(ds, qb)
            r0 = pl.multiple_of(i * tq, tq)
            dq_ref[pl.ds(r0, tq), :] += _dot(ds, kb)

        @pl.when(i == nq - 1)
        def _():
            dk_ref[...] = dka_ref[...].astype(BF16)
            dv_ref[...] = dva_ref[...].astype(BF16)

    blk = (tq, HEAD_SLOT)
    qmap = lambda h, j, i: (jnp.maximum(i, j), h)
    kmap = lambda h, j, i: (j, h)
    return pl.pallas_call(
        body, name="attn_bwd", grid=(N_HEADS, nq, nq),
        in_specs=[pl.BlockSpec(blk, qmap), pl.BlockSpec(blk, kmap), pl.BlockSpec(blk, kmap),
                  pl.BlockSpec(blk, qmap), pl.BlockSpec(blk, qmap), pl.BlockSpec(blk, qmap)],
        out_specs=[pl.BlockSpec((l, HEAD_SLOT), lambda h, j, i: (0, h)), pl.BlockSpec(blk, kmap), pl.BlockSpec(blk, kmap)],
        out_shape=[jax.ShapeDtypeStruct((l, HP), F32), jax.ShapeDtypeStruct((l, HP), BF16),
                   jax.ShapeDtypeStruct((l, HP), BF16)],
        scratch_shapes=[pltpu.VMEM(blk, F32), pltpu.VMEM(blk, F32)],
        compiler_params=_params(("parallel", "arbitrary", "arbitrary")),
    )(q, k, v, o, do, lse)


SSM_CB = 512
SSM_UB = 128
SSM_NB = SSM_CH // SSM_CB


def _scan_tiles(re_ref, im_ref, tab, carry, n_tiles, reverse):
    def tile(n, c):
        cr, ci = c
        idx = (n_tiles - 1 - n) if reverse else n
        r0 = pl.multiple_of(idx * SUBLANES, SUBLANES)
        sr = re_ref[pl.ds(r0, SUBLANES), :]
        si = im_ref[pl.ds(r0, SUBLANES), :]
        for step, k in enumerate((1, 2, 4)):
            mr, mi = tab[2 * step], tab[2 * step + 1]
            sh = (SUBLANES - k) if reverse else k
            rr = pltpu.roll(sr, sh, 0)
            ri = pltpu.roll(si, sh, 0)
            sr, si = sr + mr * rr - mi * ri, si + mr * ri + mi * rr
        pr, pi = tab[6], tab[7]
        sr, si = sr + pr * cr - pi * ci, si + pr * ci + pi * cr
        re_ref[pl.ds(r0, SUBLANES), :] = sr
        im_ref[pl.ds(r0, SUBLANES), :] = si
        if reverse:
            return sr[0:1, :], si[0:1, :]
        return sr[SUBLANES - 1:SUBLANES, :], si[SUBLANES - 1:SUBLANES, :]

    return lax.fori_loop(0, n_tiles, tile, carry, unroll=2)


def _ssm_fwd(u, bre, bim, cre, cim, dvec, tab, tt):
    l = u.shape[0]
    nt = l // tt

    def body(u_ref, bre_ref, bim_ref, cre_ref, cim_ref, d_ref, tab_ref, y_ref, sre_ref, sim_ref, car_ref):
        @pl.when(pl.program_id(1) == 0)
        def _():
            car_ref[...] = jnp.zeros_like(car_ref)

        uf = u_ref[...]
        ub = uf.astype(BF16)
        sre_ref[...] = _dot(ub, bre_ref[0])
        sim_ref[...] = _dot(ub, bim_ref[0])
        tab_v = [tab_ref[n] for n in range(8)]
        cr, ci = _scan_tiles(sre_ref, sim_ref, tab_v, (car_ref[0:1, :], car_ref[8:9, :]), tt // SUBLANES, False)
        car_ref[0:1, :] = cr
        car_ref[8:9, :] = ci
        y_ref[...] = (_dot(sre_ref[...].astype(BF16), cre_ref[0]) - _dot(sim_ref[...].astype(BF16), cim_ref[0])
                      + d_ref[...] * uf)

    return pl.pallas_call(
        body, name="ssm_fwd", grid=(SSM_NB, nt),
        in_specs=[pl.BlockSpec((tt, SSM_UB), lambda m, t: (t, m)),
                  pl.BlockSpec((1, SSM_UB, SSM_CB), lambda m, t: (m, 0, 0)),
                  pl.BlockSpec((1, SSM_UB, SSM_CB), lambda m, t: (m, 0, 0)),
                  pl.BlockSpec((1, SSM_CB, SSM_UB), lambda m, t: (m, 0, 0)),
                  pl.BlockSpec((1, SSM_CB, SSM_UB), lambda m, t: (m, 0, 0)),
                  pl.BlockSpec((1, SSM_UB), lambda m, t: (0, m)),
                  pl.BlockSpec((8, SUBLANES, SSM_CB), lambda m, t: (0, 0, m))],
        out_specs=[pl.BlockSpec((tt, SSM_UB), lambda m, t: (t, m)),
                   pl.BlockSpec((tt, SSM_CB), lambda m, t: (t, m)),
                   pl.BlockSpec((tt, SSM_CB), lambda m, t: (t, m))],
        out_shape=[jax.ShapeDtypeStruct((l, SSM_W), F32), jax.ShapeDtypeStruct((l, SSM_CH), F32),
                   jax.ShapeDtypeStruct((l, SSM_CH), F32)],
        scratch_shapes=[pltpu.VMEM((2 * SUBLANES, SSM_CB), F32)],
        compiler_params=_params(("parallel", "arbitrary")),
    )(u, bre, bim, cre, cim, dvec, tab)


def _ssm_bwd(dy, u, sre, sim, bre, bim, cre, cim, dvec, tab, tt):
    l = u.shape[0]
    nt = l // tt
    tpb = tt // SUBLANES

    def body(dy_ref, u_ref, sre_ref, sim_ref, hre_ref, him_ref, bre_ref, bim_ref, cre_ref, cim_ref, d_ref, tab_ref,
             du_ref, dbre_ref, dbim_ref, dcre_ref, dcim_ref, dare_ref, daim_ref, dd_ref, lr_ref, li_ref, car_ref):
        t = pl.program_id(1)

        @pl.when(t == 0)
        def _():
            car_ref[...] = jnp.zeros_like(car_ref)
            for ref in (dbre_ref, dbim_ref, dcre_ref, dcim_ref, dare_ref, daim_ref, dd_ref):
                ref[...] = jnp.zeros_like(ref)

        dyf = dy_ref[...]
        dyb = dyf.astype(BF16)
        uf = u_ref[...]
        s_re = sre_ref[...]
        s_im = sim_ref[...]
        lr_ref[...] = _dot_nt(dyb, cre_ref[0])
        li_ref[...] = -_dot_nt(dyb, cim_ref[0])
        dcre_ref[0] += _dot_tn(s_re.astype(BF16), dyb)
        dcim_ref[0] -= _dot_tn(s_im.astype(BF16), dyb)
        tab_v = [tab_ref[n] for n in range(8)]
        cr, ci = _scan_tiles(lr_ref, li_ref, tab_v, (car_ref[0:1, :], car_ref[8:9, :]), tpb, True)
        car_ref[0:1, :] = cr
        car_ref[8:9, :] = ci
        lam_r = lr_ref[...]
        lam_i = li_ref[...]
        keep = jnp.where(t == nt - 1, 0.0, 1.0)
        sp_r = _shift_down(s_re, 1, hre_ref[...] * keep)
        sp_i = _shift_down(s_im, 1, him_ref[...] * keep)
        dare_ref[...] += jnp.sum(lam_r * sp_r + lam_i * sp_i, axis=0, keepdims=True)
        daim_ref[...] += jnp.sum(lam_i * sp_r - lam_r * sp_i, axis=0, keepdims=True)
        lrb = lam_r.astype(BF16)
        lib = lam_i.astype(BF16)
        du_ref[...] = _dot_nt(lrb, bre_ref[0]) + _dot_nt(lib, bim_ref[0]) + dyf * d_ref[...]
        ub = uf.astype(BF16)
        dbre_ref[0] += _dot_tn(ub, lrb)
        dbim_ref[0] += _dot_tn(ub, lib)
        dd_ref[...] += jnp.sum(dyf * uf, axis=0, keepdims=True)

    rev = lambda m, t: (nt - 1 - t, m)
    halo = lambda m, t: (jnp.maximum((nt - 1 - t) * tpb - 1, 0), m)
    wb = pl.BlockSpec((1, SSM_UB, SSM_CB), lambda m, t: (m, 0, 0))
    wc = pl.BlockSpec((1, SSM_CB, SSM_UB), lambda m, t: (m, 0, 0))
    vec_c = pl.BlockSpec((1, SSM_CB), lambda m, t: (0, m))
    vec_u = pl.BlockSpec((1, SSM_UB), lambda m, t: (0, m))
    return pl.pallas_call(
        body, name="ssm_bwd", grid=(SSM_NB, nt),
        in_specs=[pl.BlockSpec((tt, SSM_UB), rev), pl.BlockSpec((tt, SSM_UB), rev),
                  pl.BlockSpec((tt, SSM_CB), rev), pl.BlockSpec((tt, SSM_CB), rev),
                  pl.BlockSpec((SUBLANES, SSM_CB), halo), pl.BlockSpec((SUBLANES, SSM_CB), halo),
                  wb, wb, wc, wc, vec_u,
                  pl.BlockSpec((8, SUBLANES, SSM_CB), lambda m, t: (0, 0, m))],
        out_specs=[pl.BlockSpec((tt, SSM_UB), rev), wb, wb, wc, wc, vec_c, vec_c, vec_u],
        out_shape=[jax.ShapeDtypeStruct((l, SSM_W), F32),
                   jax.ShapeDtypeStruct((SSM_NB, SSM_UB, SSM_CB), F32), jax.ShapeDtypeStruct((SSM_NB, SSM_UB, SSM_CB), F32),
                   jax.ShapeDtypeStruct((SSM_NB, SSM_CB, SSM_UB), F32), jax.ShapeDtypeStruct((SSM_NB, SSM_CB, SSM_UB), F32),
                   jax.ShapeDtypeStruct((1, SSM_CH), F32), jax.ShapeDtypeStruct((1, SSM_CH), F32),
                   jax.ShapeDtypeStruct((1, SSM_W), F32)],
        scratch_shapes=[pltpu.VMEM((tt, SSM_CB), F32), pltpu.VMEM((tt, SSM_CB), F32),
                        pltpu.VMEM((2 * SUBLANES, SSM_CB), F32)],
        compiler_params=_params(("parallel", "arbitrary")),
    )(dy, u, sre, sim, sre, sim, bre, bim, cre, cim, dvec, tab)


def _merge_fwd(x, gl, attn, y1, wba, wbs, wglu, bglu, wout, gpost, gpre, tl):
    l = x.shape[0]

    def body(x_ref, gl_ref, at_ref, y1_ref, wba_ref, wbs_ref, wglu_ref, bglu_ref, wout_ref, gpost_ref, gpre_ref,
             a_ref, sm_ref, mg_ref, z_ref, x1_ref, hn2_ref, y3_ref):
        y2 = _gelu(y1_ref[...])
        sg = _sigmoid(_dot(y2.astype(BF16), wglu_ref[...]) + bglu_ref[...])
        y3 = (y2 * sg).astype(BF16)
        y3_ref[...] = y3
        a = _dot(at_ref[...], wba_ref[...])
        sm = _dot(y3, wbs_ref[...])
        a_ref[...] = a
        sm_ref[...] = sm
        g = _sigmoid(gl_ref[...])
        merged = (g[:, :D_MODEL] * a + g[:, D_MODEL:] * sm).astype(BF16)
        mg_ref[...] = merged
        z = _dot(merged, wout_ref[...])
        z_ref[...] = z
        n, _ = _rms(z, gpost_ref[...])
        x1 = x_ref[...] + n
        x1_ref[...] = x1
        hn2, _ = _rms(x1, gpre_ref[...])
        hn2_ref[...] = hn2.astype(BF16)

    outs = [(D_MODEL, F32), (D_MODEL, F32), (D_MODEL, BF16), (D_MODEL, F32), (D_MODEL, F32), (D_MODEL, BF16),
            (SSM_W, BF16)]
    return pl.pallas_call(
        body, name="merge_fwd", grid=(l // tl,),
        in_specs=[_row(tl, D_MODEL), _row(tl, 2 * D_MODEL), _row(tl, HP), _row(tl, SSM_W),
                  _const((HP, D_MODEL)), _const((SSM_W, D_MODEL)), _const((SSM_W, SSM_W)), _const((1, SSM_W)),
                  _const((D_MODEL, D_MODEL)), _const((1, D_MODEL)), _const((1, D_MODEL))],
        out_specs=[_row(tl, n) for n, _ in outs],
        out_shape=[jax.ShapeDtypeStruct((l, n), dt) for n, dt in outs],
        compiler_params=_params(("parallel",)),
    )(x, gl, attn, y1, wba, wbs, wglu, bglu, wout, gpost, gpre)


def _merge_bwd(dhn2a, dhn2b, x1, dx2, z, gl, a, sm, y1, wba, wbs, wglu, bglu, wout, gpost, gpre, tl):
    l = x1.shape[0]

    def body(da_ref, db_ref, x1_ref, dx2_ref, z_ref, gl_ref, a_ref, sm_ref, y1_ref,
             wba_ref, wbs_ref, wglu_ref, bglu_ref, wout_ref, gpost_ref, gpre_ref,
             dx1_ref, dz_ref, dbra_ref, dbrs_ref, dgl_ref, dat_ref, dy1_ref, dt_ref, y2_ref,
             dgpre_ref, dgpost_ref, dbg_ref, dbglu_ref):
        @pl.when(pl.program_id(0) == 0)
        def _():
            for ref in (dgpre_ref, dgpost_ref, dbg_ref, dbglu_ref):
                ref[...] = jnp.zeros_like(ref)

        dhn2 = da_ref[...] + db_ref[...]
        dx1a, dgpre = _rms_bwd(dhn2, x1_ref[...], gpre_ref[...])
        dgpre_ref[...] += dgpre
        dx1 = dx2_ref[...] + dx1a
        dx1_ref[...] = dx1
        dz, dgpost = _rms_bwd(dx1, z_ref[...], gpost_ref[...])
        dgpost_ref[...] += dgpost
        dzb = dz.astype(BF16)
        dz_ref[...] = dzb
        dm = _dot_nt(dzb, wout_ref[...])
        g = _sigmoid(gl_ref[...])
        g0 = g[:, :D_MODEL]
        g1 = g[:, D_MODEL:]
        dbra = (dm * g0).astype(BF16)
        dbrs = (dm * g1).astype(BF16)
        dbra_ref[...] = dbra
        dbrs_ref[...] = dbrs
        dgl0 = dm * a_ref[...] * g0 * (1.0 - g0)
        dgl1 = dm * sm_ref[...] * g1 * (1.0 - g1)
        dgl_ref[:, :D_MODEL] = dgl0.astype(BF16)
        dgl_ref[:, D_MODEL:] = dgl1.astype(BF16)
        dbg_ref[:, :D_MODEL] += jnp.sum(dgl0, axis=0, keepdims=True)
        dbg_ref[:, D_MODEL:] += jnp.sum(dgl1, axis=0, keepdims=True)
        dat_ref[...] = _dot_nt(dbra, wba_ref[...]).astype(BF16)
        dy3 = _dot_nt(dbrs, wbs_ref[...])
        y1v = y1_ref[...]
        y2 = _gelu(y1v)
        y2b = y2.astype(BF16)
        y2_ref[...] = y2b
        sg = _sigmoid(_dot(y2b, wglu_ref[...]) + bglu_ref[...])
        dt = dy3 * y2 * sg * (1.0 - sg)
        dtb = dt.astype(BF16)
        dt_ref[...] = dtb
        dbglu_ref[...] += jnp.sum(dt, axis=0, keepdims=True)
        dy2 = dy3 * sg + _dot_nt(dtb, wglu_ref[...])
        dy1_ref[...] = dy2 * _gelu_grad(y1v)

    outs = [(D_MODEL, F32), (D_MODEL, BF16), (D_MODEL, BF16), (D_MODEL, BF16), (2 * D_MODEL, BF16), (HP, BF16),
            (SSM_W, F32), (SSM_W, BF16), (SSM_W, BF16)]
    accs = [D_MODEL, D_MODEL, 2 * D_MODEL, SSM_W]
    return pl.pallas_call(
        body, name="merge_bwd", grid=(l // tl,),
        in_specs=[_row(tl, D_MODEL), _row(tl, D_MODEL), _row(tl, D_MODEL), _row(tl, D_MODEL), _row(tl, D_MODEL),
                  _row(tl, 2 * D_MODEL), _row(tl, D_MODEL), _row(tl, D_MODEL), _row(tl, SSM_W),
                  _const((HP, D_MODEL)), _const((SSM_W, D_MODEL)), _const((SSM_W, SSM_W)), _const((1, SSM_W)),
                  _const((D_MODEL, D_MODEL)), _const((1, D_MODEL)), _const((1, D_MODEL))],
        out_specs=[_row(tl, n) for n, _ in outs] + [_const((1, n)) for n in accs],
        out_shape=[jax.ShapeDtypeStruct((l, n), dt) for n, dt in outs]
        + [jax.ShapeDtypeStruct((1, n), F32) for n in accs],
        compiler_params=_params(("arbitrary",)),
    )(dhn2a, dhn2b, x1, dx2, z, gl, a, sm, y1, wba, wbs, wglu, bglu, wout, gpost, gpre)


def _proj_bwd(x, dx1, cq, ckv, dq, dk, dv, du, dgl, g1, win, gq, wuq, gkv, wukv, rc, rs, tl):
    l = x.shape[0]

    def body(x_ref, dx1_ref, cq_ref, ckv_ref, dq_ref, dk_ref, dv_ref, du_ref, dgl_ref,
             g1_ref, win_ref, gq_ref, wuq_ref, gkv_ref, wukv_ref, rc_ref, rs_ref,
             gx_ref, dql_ref, qn_ref, ckvn_ref, dproj_ref, dg1_ref, dgq_ref, dgkv_ref):
        @pl.when(pl.program_id(0) == 0)
        def _():
            for ref in (dg1_ref, dgq_ref, dgkv_ref):
                ref[...] = jnp.zeros_like(ref)

        c1 = rc_ref[...]
        s1 = rs_ref[...]
        dql = _rope_bwd(dq_ref[...], jnp.tile(c1, (1, N_HEADS)), jnp.tile(s1, (1, N_HEADS))).astype(BF16)
        dql_ref[...] = dql
        dqn = _dot_nt(dql, wuq_ref[...])
        cq = cq_ref[...]
        qn, _ = _rms(cq, gq_ref[...])
        qn_ref[...] = qn.astype(BF16)
        dcq, dgq = _rms_bwd(dqn, cq, gq_ref[...])
        dgq_ref[...] += dgq
        dkb = dk_ref[...]
        dvb = dv_ref[...]
        dkf = dkb.astype(F32)
        dkr = dkf[:, 0:HEAD_SLOT]
        for h in range(1, N_HEADS):
            dkr = dkr + dkf[:, h * HEAD_SLOT:(h + 1) * HEAD_SLOT]
        dkr = _rope_bwd(dkr, c1, s1)
        dckvn = _dot_nt(dkb, wukv_ref[:, :HP]) + _dot_nt(dvb, wukv_ref[:, HP:])
        ckv = ckv_ref[...]
        ckvn, _ = _rms(ckv, gkv_ref[...])
        ckvn_ref[...] = ckvn.astype(BF16)
        dckv, dgkv = _rms_bwd(dckvn, ckv, gkv_ref[...])
        dgkv_ref[...] += dgkv
        dproj_ref[:, P_CQ:P_CKV] = dcq.astype(BF16)
        dproj_ref[:, P_CKV:P_KR] = dckv.astype(BF16)
        dproj_ref[:, P_KR:P_U] = dkr.astype(BF16)
        dproj_ref[:, P_U:P_GL] = du_ref[...].astype(BF16)
        dproj_ref[:, P_GL:P_END] = dgl_ref[...]
        dhn = _dot_nt(dproj_ref[...], win_ref[...])
        dxa, dg1 = _rms_bwd(dhn, x_ref[...], g1_ref[...])
        dg1_ref[...] += dg1
        gx_ref[...] = dx1_ref[...] + dxa

    outs = [(D_MODEL, F32), (HP, BF16), (Q_RANK, BF16), (KV_RANK, BF16), (P_END, BF16)]
    accs = [D_MODEL, Q_RANK, KV_RANK]
    return pl.pallas_call(
        body, name="proj_bwd", grid=(l // tl,),
        in_specs=[_row(tl, D_MODEL), _row(tl, D_MODEL), _row(tl, Q_RANK), _row(tl, KV_RANK), _row(tl, HP),
                  _row(tl, HP), _row(tl, HP), _row(tl, SSM_W), _row(tl, 2 * D_MODEL),
                  _const((1, D_MODEL)), _const((D_MODEL, P_END)), _const((1, Q_RANK)), _const((Q_RANK, HP)),
                  _const((1, KV_RANK)), _const((KV_RANK, 2 * HP)), _row(tl, HEAD_SLOT), _row(tl, HEAD_SLOT)],
        out_specs=[_row(tl, n) for n, _ in outs] + [_const((1, n)) for n in accs],
        out_shape=[jax.ShapeDtypeStruct((l, n), dt) for n, dt in outs]
        + [jax.ShapeDtypeStruct((1, n), F32) for n in accs],
        compiler_params=_params(("arbitrary",)),
    )(x, dx1, cq, ckv, dq, dk, dv, du, dgl, g1, win, gq, wuq, gkv, wukv, rc, rs)


CONV_CB = 256
CONV_NB = D_FF // CONV_CB


def _conv3(h, halo, w, b):
    return b + w[0:1, :] * _shift_down(h, 2, halo) + w[1:2, :] * _shift_down(h, 1, halo) + w[2:3, :] * h


def _conv_fwd(h, cw, cb, tl):
    l = h.shape[0]

    def body(hg_ref, hv_ref, wg_ref, wv_ref, bg_ref, bv_ref, act_ref, halo_ref):
        @pl.when(pl.program_id(1) == 0)
        def _():
            halo_ref[...] = jnp.zeros_like(halo_ref)

        hg = hg_ref[...]
        hv = hv_ref[...]
        cg = _conv3(hg, halo_ref[0:SUBLANES, :], wg_ref[...], bg_ref[...])
        cv = _conv3(hv, halo_ref[SUBLANES:, :], wv_ref[...], bv_ref[...])
        act_ref[...] = (_gelu(cg) * cv).astype(BF16)
        halo_ref[0:SUBLANES, :] = hg[tl - SUBLANES:, :]
        halo_ref[SUBLANES:, :] = hv[tl - SUBLANES:, :]

    gmap = lambda c, r: (r, c)
    vmap = lambda c, r: (r, CONV_NB + c)
    return pl.pallas_call(
        body, name="conv_fwd", grid=(CONV_NB, l // tl),
        in_specs=[pl.BlockSpec((tl, CONV_CB), gmap), pl.BlockSpec((tl, CONV_CB), vmap),
                  pl.BlockSpec((3, CONV_CB), lambda c, r: (0, c)), pl.BlockSpec((3, CONV_CB), lambda c, r: (0, CONV_NB + c)),
                  pl.BlockSpec((1, CONV_CB), lambda c, r: (0, c)), pl.BlockSpec((1, CONV_CB), lambda c, r: (0, CONV_NB + c))],
        out_specs=pl.BlockSpec((tl, CONV_CB), gmap),
        out_shape=jax.ShapeDtypeStruct((l, D_FF), BF16),
        scratch_shapes=[pltpu.VMEM((2 * SUBLANES, CONV_CB), F32)],
        compiler_params=_params(("parallel", "arbitrary")),
    )(h, h, cw, cw, cb, cb)


def _conv_bwd(h, dact, cw, cb, tl):
    l = h.shape[0]
    nr = l // tl
    tpb = tl // SUBLANES

    def body(hg_ref, hv_ref, hgh_ref, hvh_ref, da_ref, wg_ref, wv_ref, bg_ref, bv_ref,
             dhg_ref, dhv_ref, dwg_ref, dwv_ref, dbg_ref, dbv_ref, car_ref):
        r = pl.program_id(1)

        @pl.when(r == 0)
        def _():
            for ref in (car_ref, dwg_ref, dwv_ref, dbg_ref, dbv_ref):
                ref[...] = jnp.zeros_like(ref)

        keep = jnp.where(r == nr - 1, 0.0, 1.0)
        da = da_ref[...].astype(F32)

        def half(h_ref, halo, w, b):
            hh = h_ref[...]
            h1 = _shift_down(hh, 1, halo)
            h2 = _shift_down(hh, 2, halo)
            return hh, h1, h2, b + w[0:1, :] * h2 + w[1:2, :] * h1 + w[2:3, :] * hh

        wg = wg_ref[...]
        wv = wv_ref[...]
        hg, hg1, hg2, cg = half(hg_ref, hgh_ref[...] * keep, wg, bg_ref[...])
        hv, hv1, hv2, cv = half(hv_ref, hvh_ref[...] * keep, wv, bv_ref[...])
        dcg = da * cv * _gelu_grad(cg)
        dcv = da * _gelu(cg)

        def back(dc, hh, h1, h2, w, nxt, dw_ref, db_ref, dh_ref):
            db_ref[...] += jnp.sum(dc, axis=0, keepdims=True)
            dw_ref[0:1, :] += jnp.sum(dc * h2, axis=0, keepdims=True)
            dw_ref[1:2, :] += jnp.sum(dc * h1, axis=0, keepdims=True)
            dw_ref[2:3, :] += jnp.sum(dc * hh, axis=0, keepdims=True)
            dh = w[2:3, :] * dc + w[1:2, :] * _shift_up(dc, 1, nxt) + w[0:1, :] * _shift_up(dc, 2, nxt)
            dh_ref[...] = dh.astype(BF16)

        back(dcg, hg, hg1, hg2, wg, car_ref[0:SUBLANES, :], dwg_ref, dbg_ref, dhg_ref)
        back(dcv, hv, hv1, hv2, wv, car_ref[SUBLANES:, :], dwv_ref, dbv_ref, dhv_ref)
        car_ref[0:SUBLANES, :] = dcg[0:SUBLANES, :]
        car_ref[SUBLANES:, :] = dcv[0:SUBLANES, :]

    grev = lambda c, r: (nr - 1 - r, c)
    vrev = lambda c, r: (nr - 1 - r, CONV_NB + c)
    ghalo = lambda c, r: (jnp.maximum((nr - 1 - r) * tpb - 1, 0), c)
    vhalo = lambda c, r: (jnp.maximum((nr - 1 - r) * tpb - 1, 0), CONV_NB + c)
    colg = lambda c, r: (0, c)
    colv = lambda c, r: (0, CONV_NB + c)
    return pl.pallas_call(
        body, name="conv_bwd", grid=(CONV_NB, nr),
        in_specs=[pl.BlockSpec((tl, CONV_CB), grev), pl.BlockSpec((tl, CONV_CB), vrev),
                  pl.BlockSpec((SUBLANES, CONV_CB), ghalo), pl.BlockSpec((SUBLANES, CONV_CB), vhalo),
                  pl.BlockSpec((tl, CONV_CB), grev),
                  pl.BlockSpec((3, CONV_CB), colg), pl.BlockSpec((3, CONV_CB), colv),
                  pl.BlockSpec((1, CONV_CB), colg), pl.BlockSpec((1, CONV_CB), colv)],
        out_specs=[pl.BlockSpec((tl, CONV_CB), grev), pl.BlockSpec((tl, CONV_CB), grev),
                   pl.BlockSpec((3, CONV_CB), colg), pl.BlockSpec((3, CONV_CB), colg),
                   pl.BlockSpec((1, CONV_CB), colg), pl.BlockSpec((1, CONV_CB), colg)],
        out_shape=[jax.ShapeDtypeStruct((l, D_FF), BF16), jax.ShapeDtypeStruct((l, D_FF), BF16),
                   jax.ShapeDtypeStruct((3, D_FF), F32), jax.ShapeDtypeStruct((3, D_FF), F32),
                   jax.ShapeDtypeStruct((1, D_FF), F32), jax.ShapeDtypeStruct((1, D_FF), F32)],
        scratch_shapes=[pltpu.VMEM((2 * SUBLANES, CONV_CB), F32)],
        compiler_params=_params(("parallel", "arbitrary")),
    )(h, h, h, h, dact, cw, cw, cb, cb)


def _loss_head(ff, x1, tgt, g, tl):
    l = ff.shape[0]

    def body(ff_ref, x1_ref, tg_ref, g_ref, loss_ref, dx2_ref, dff_ref, dg_ref):
        @pl.when(pl.program_id(0) == 0)
        def _():
            loss_ref[...] = jnp.zeros_like(loss_ref)
            dg_ref[...] = jnp.zeros_like(dg_ref)

        f = ff_ref[...]
        gv = g_ref[...]
        n, _ = _rms(f, gv)
        e = x1_ref[...] + n - tg_ref[...]
        loss_ref[...] += 0.5 * jnp.sum(jnp.mean(e * e, axis=-1, keepdims=True), axis=0, keepdims=True)
        dx2 = e * (1.0 / D_MODEL)
        dx2_ref[...] = dx2
        dff, dg = _rms_bwd(dx2, f, gv)
        dff_ref[...] = dff.astype(BF16)
        dg_ref[...] += dg

    return pl.pallas_call(
        body, name="loss_head", grid=(l // tl,),
        in_specs=[_row(tl, D_MODEL), _row(tl, D_MODEL), _row(tl, D_MODEL), _const((1, D_MODEL))],
        out_specs=[_const((1, LANES)), _row(tl, D_MODEL), _row(tl, D_MODEL), _const((1, D_MODEL))],
        out_shape=[jax.ShapeDtypeStruct((1, LANES), F32), jax.ShapeDtypeStruct((l, D_MODEL), F32),
                   jax.ShapeDtypeStruct((l, D_MODEL), BF16), jax.ShapeDtypeStruct((1, D_MODEL), F32)],
        compiler_params=_params(("arbitrary",)),
    )(ff, x1, tgt, g)


def _ssm_disc(lam_re, lam_im, log_dt, b_re, b_im):
    dt = jnp.exp(log_dt)[:, None]
    mag = jnp.exp(lam_re * dt)
    ang = lam_im * dt
    a_re, a_im = mag * jnp.cos(ang), mag * jnp.sin(ang)
    den = lam_re * lam_re + lam_im * lam_im
    n_re, n_im = a_re - 1.0, a_im
    z_re = (n_re * lam_re + n_im * lam_im) / den
    z_im = (n_im * lam_re - n_re * lam_im) / den
    bb_re = z_re[..., None] * b_re - z_im[..., None] * b_im
    bb_im = z_re[..., None] * b_im + z_im[..., None] * b_re
    return a_re, a_im, bb_re, bb_im


_GPB = SSM_CB // SSM_P


def _embed_b(bb):
    t = bb.transpose(0, 2, 1).reshape(SSM_NB, _GPB, SSM_H, SSM_P)
    return jnp.einsum('mjhp,jk->mjhkp', t, jnp.eye(_GPB, dtype=bb.dtype)).reshape(SSM_NB, SSM_UB, SSM_CB)


def _extract_b(d):
    t = d.reshape(SSM_NB, _GPB, SSM_H, _GPB, SSM_P)
    t = jnp.einsum('mjhkp,jk->mjhp', t, jnp.eye(_GPB, dtype=d.dtype))
    return t.reshape(SSM_G, SSM_H, SSM_P).transpose(0, 2, 1)


def _embed_c(c):
    t = c.transpose(0, 2, 1).reshape(SSM_NB, _GPB, SSM_P, SSM_H)
    return jnp.einsum('mjph,jk->mjpkh', t, jnp.eye(_GPB, dtype=c.dtype)).reshape(SSM_NB, SSM_CB, SSM_UB)


def _extract_c(d):
    t = d.reshape(SSM_NB, _GPB, SSM_P, _GPB, SSM_H)
    t = jnp.einsum('mjpkh,jk->mjph', t, jnp.eye(_GPB, dtype=d.dtype))
    return t.reshape(SSM_G, SSM_P, SSM_H).transpose(0, 2, 1)


def _scan_tables(a_re, a_im, reverse):
    ar = a_re.reshape(1, SSM_CH)
    ai = (-a_im if reverse else a_im).reshape(1, SSM_CH)
    pr, pi = [ar], [ai]
    for _ in range(SUBLANES - 1):
        pr, pi = pr + [pr[-1] * ar - pi[-1] * ai], pi + [pr[-1] * ai + pi[-1] * ar]
    rows = jnp.arange(SUBLANES)[:, None]
    out = []
    for k in (1, 2, 4):
        valid = (rows + k <= SUBLANES - 1) if reverse else (rows >= k)
        out += [jnp.where(valid, pr[k - 1], 0.0), jnp.where(valid, pi[k - 1], 0.0)]
    order = list(range(SUBLANES - 1, -1, -1)) if reverse else list(range(SUBLANES))
    out += [jnp.concatenate([pr[n] for n in order], axis=0), jnp.concatenate([pi[n] for n in order], axis=0)]
    return jnp.stack(out).astype(F32)


def _pad_heads(w, d):
    lead = w.shape[:-1]
    w = w.reshape(lead + (N_HEADS, d))
    w = jnp.pad(w, [(0, 0)] * len(lead) + [(0, 0), (0, HEAD_SLOT - d)])
    return w.reshape(lead + (HP,))


def _unpad_heads(w, d):
    lead = w.shape[:-1]
    return w.reshape(lead + (N_HEADS, HEAD_SLOT))[..., :d].reshape(lead + (N_HEADS * d,))


def _pad_w_in(w):
    z = lambda n: jnp.zeros((w.shape[0], n), w.dtype)
    return jnp.concatenate([w[:, :640], z(KR_LANE), w[:, 640:672], z(HEAD_SLOT - KR_LANE - QK_ROPE), w[:, 672:]], axis=1)


def _unpad_w_in(w):
    return jnp.concatenate([w[:, :640], w[:, P_KR + KR_LANE:P_KR + KR_LANE + QK_ROPE], w[:, P_U:]], axis=1)


def _local_step(x, positions, tgt, wts, sp):
    l = x.shape[0]
    tl = min(256, l)
    ta = min(512, l)

    inv_freq = ROPE_THETA ** (-jnp.arange(0, QK_ROPE, 2, dtype=F32) / QK_ROPE)
    ang = positions.astype(F32)[:, None] * inv_freq
    cos, sin = jnp.cos(ang), jnp.sin(ang)
    one = jnp.ones((l, KR_LANE), F32)
    rc = jnp.concatenate([one, cos, cos, jnp.ones((l, HEAD_SLOT - KR_LANE - QK_ROPE), F32)], axis=1)
    rs = jnp.concatenate([0 * one, -sin, sin, jnp.zeros((l, HEAD_SLOT - KR_LANE - QK_ROPE), F32)], axis=1)

    win = _pad_w_in(wts["w_in"])
    wuq = _pad_heads(wts["w_uq"], QK_HEAD)
    wukv = jnp.concatenate([_pad_heads(wts["w_uk"], QK_NOPE), _pad_heads(wts["w_uv"], V_HEAD)], axis=1)
    wba = jnp.pad(wts["w_branch_attn"].reshape(N_HEADS, V_HEAD, D_MODEL),
                  ((0, 0), (0, HEAD_SLOT - V_HEAD), (0, 0))).reshape(HP, D_MODEL)
    wbs, wglu, wout, wup, wdown = wts["w_branch_ssm"], wts["w_glu"], wts["w_out"], wts["w_up"], wts["w_down"]

    disc_in = (sp["ssm_lambda_re"], sp["ssm_lambda_im"], sp["ssm_log_dt"], sp["ssm_b_re"], sp["ssm_b_im"])
    (a_re, a_im, bb_re, bb_im), disc_vjp = jax.vjp(_ssm_disc, *disc_in)
    bre, bim = _embed_b(bb_re).astype(BF16), _embed_b(bb_im).astype(BF16)
    cre, cim = _embed_c(sp["ssm_c_re"]).astype(BF16), _embed_c(sp["ssm_c_im"]).astype(BF16)
    dvec = sp["ssm_d"].reshape(1, SSM_W)
    tab_f = _scan_tables(a_re, a_im, False)
    tab_r = _scan_tables(a_re, a_im, True)

    g1, gq, gkv = sp["mix_norm_pre"], sp["q_norm"], sp["kv_norm"]
    gpost, gpre, gfin = sp["mix_norm_post"], sp["ffn_norm_pre"], sp["ffn_norm_post"]
    bgate, bglu, convb = sp["b_gate"], sp["b_glu"], sp["conv_b"]
    convw = sp["conv_w"]

    hn, cq, ckv, q, k, v, u, gl = _proj_fwd(x, g1, win, gq, wuq, gkv, wukv, rc, rs, bgate, tl)
    attn, lse = _attn_fwd(q, k, v, ta)
    y1, sre, sim = _ssm_fwd(u, bre, bim, cre, cim, dvec, tab_f, ta)
    a, sm, merged, z, x1, hn2, y3 = _merge_fwd(x, gl, attn, y1, wba, wbs, wglu, bglu, wout, gpost, gpre, tl)
    h = _mm(hn2, wup, "ffn_up")
    act = _conv_fwd(h, convw, convb, ta)
    ff = _mm(act, wdown, "ffn_down")
    loss, dx2, dff, dgfin = _loss_head(ff, x1, tgt, gfin, tl)

    dact = _mm(dff, wdown, "ffn_down_dx", out_dtype=BF16, bt=True)
    d_wdown = _mm_tn(act, dff, "ffn_down_dw")
    dhg, dhv, dwg, dwv, dbg, dbv = _conv_bwd(h, dact, convw, convb, ta)
    d_convw = jnp.concatenate([dwg, dwv], axis=1)
    d_convb = jnp.concatenate([dbg, dbv], axis=1)
    dhn2a = _mm(dhg, wup, "ffn_up_dx_gate", bt=True, b_col0=0)
    dhn2b = _mm(dhv, wup, "ffn_up_dx_val", bt=True, b_col0=1)
    d_wup = jnp.concatenate([_mm_tn(hn2, dhg, "ffn_up_dw_gate"), _mm_tn(hn2, dhv, "ffn_up_dw_val")], axis=1)
    (dx1, dz, dbra, dbrs, dgl, dattn, dy1, dt, y2, dgpre, dgpost, dbgate, dbglu) = _merge_bwd(
        dhn2a, dhn2b, x1, dx2, z, gl, a, sm, y1, wba, wbs, wglu, bglu, wout, gpost, gpre, tl)
    d_wout = _mm_tn(merged, dz, "w_out_dw")
    d_wba = _mm_tn(attn, dbra, "w_branch_attn_dw")
    d_wbs = _mm_tn(y3, dbrs, "w_branch_ssm_dw")
    d_wglu = _mm_tn(y2, dt, "w_glu_dw")
    dq, dk, dv = _attn_bwd(q, k, v, attn, dattn, lse, ta)
    du, dbre, dbim, dcre, dcim, dare, daim, dd = _ssm_bwd(dy1, u, sre, sim, bre, bim, cre, cim, dvec, tab_r, ta)
    gx, dql, qn, ckvn, dproj, dg1, dgq, dgkv = _proj_bwd(
        x, dx1, cq, ckv, dq, dk, dv, du, dgl, g1, win, gq, wuq, gkv, wukv, rc, rs, tl)
    d_win = _mm_tn(hn, dproj, "w_in_dw")
    d_wuq = _mm_tn(qn, dql, "w_uq_dw")
    d_wuk = _mm_tn(ckvn, dk, "w_uk_dw")
    d_wuv = _mm_tn(ckvn, dv, "w_uv_dw")

    d_lre, d_lim, d_ldt, d_bre, d_bim = disc_vjp((dare.reshape(SSM_G, SSM_P), daim.reshape(SSM_G, SSM_P),
                                                  _extract_b(dbre), _extract_b(dbim)))
    big = {
        "w_in": _unpad_w_in(d_win),
        "w_uq": _unpad_heads(d_wuq, QK_HEAD),
        "w_uk": _unpad_heads(d_wuk, QK_NOPE),
        "w_uv": _unpad_heads(d_wuv, V_HEAD),
        "w_glu": d_wglu,
        "w_branch_attn": d_wba.reshape(N_HEADS, HEAD_SLOT, D_MODEL)[:, :V_HEAD].reshape(N_HEADS * V_HEAD, D_MODEL),
        "w_branch_ssm": d_wbs,
        "w_out": d_wout,
        "w_up": d_wup,
        "conv_w": d_convw,
        "w_down": d_wdown,
    }
    small = {
        "mix_norm_pre": dg1, "q_norm": dgq, "kv_norm": dgkv,
        "ssm_lambda_re": d_lre, "ssm_lambda_im": d_lim, "ssm_log_dt": d_ldt,
        "ssm_b_re": d_bre, "ssm_b_im": d_bim,
        "ssm_c_re": _extract_c(dcre), "ssm_c_im": _extract_c(dcim),
        "ssm_d": dd.reshape(SSM_G, SSM_H), "b_glu": dbglu, "b_gate": dbgate,
        "mix_norm_post": dgpost, "ffn_norm_pre": dgpre, "conv_b": d_convb, "ffn_norm_post": dgfin,
    }
    return loss[0, 0], gx, big, small


_ANY = pl.BlockSpec(memory_space=pl.ANY)


def _chip_exchange(a, name, gather):
    blk = a.shape if gather else a.shape[1:]

    def body(a_ref, out_ref, send_sems, recv_sems, local_sem):
        x, y, c = lax.axis_index("x"), lax.axis_index("y"), lax.axis_index("c")
        me = 2 * x + y
        peers = [(1 - x, y), (x, 1 - y), (1 - x, 1 - y)]

        def src(px, py):
            return a_ref if gather else a_ref.at[2 * px + py]

        mine = pltpu.make_async_copy(src(x, y), out_ref.at[me], local_sem)
        mine.start()
        sent = []
        for n, (px, py) in enumerate(peers):
            cp = pltpu.make_async_remote_copy(src_ref=src(px, py), dst_ref=out_ref.at[me], send_sem=send_sems.at[n],
                                              recv_sem=recv_sems.at[n], device_id=(px, py, c), device_id_type=MESH)
            cp.start()
            sent.append(cp)
        for n, (px, py) in enumerate(peers):
            pltpu.make_async_remote_copy(src_ref=src(px, py), dst_ref=out_ref.at[2 * px + py], send_sem=send_sems.at[n],
                                         recv_sem=recv_sems.at[n], device_id=(px, py, c), device_id_type=MESH).wait_recv()
        for cp in sent:
            cp.wait_send()
        mine.wait()

    return pl.pallas_call(
        body, name=name, in_specs=[_ANY], out_specs=_ANY,
        out_shape=jax.ShapeDtypeStruct((N_CHIPS,) + tuple(blk), a.dtype),
        scratch_shapes=[pltpu.SemaphoreType.DMA((3,)), pltpu.SemaphoreType.DMA((3,)), pltpu.SemaphoreType.DMA],
    )(a)


def _core_swap(a, name):
    def body(a_ref, out_ref, send_sem, recv_sem):
        x, y, c = lax.axis_index("x"), lax.axis_index("y"), lax.axis_index("c")
        cp = pltpu.make_async_remote_copy(src_ref=a_ref, dst_ref=out_ref, send_sem=send_sem, recv_sem=recv_sem,
                                          device_id=(x, y, 1 - c), device_id_type=MESH)
        cp.start()
        cp.wait()

    return pl.pallas_call(
        body, name=name, in_specs=[_ANY], out_specs=_ANY,
        out_shape=jax.ShapeDtypeStruct(a.shape, a.dtype),
        scratch_shapes=[pltpu.SemaphoreType.DMA, pltpu.SemaphoreType.DMA],
    )(a)


def _all_reduce_small(v, name):
    rows, w = v.shape

    def body(v_ref, out_ref, buf_ref, send_sems, recv_sems):
        x, y, c = lax.axis_index("x"), lax.axis_index("y"), lax.axis_index("c")
        me = 4 * x + 2 * y + c

        def flip(n):
            return (1 - x if n & 4 else x, 1 - y if n & 2 else y, 1 - c if n & 1 else c)

        sent = []
        for n in range(1, 8):
            cp = pltpu.make_async_remote_copy(src_ref=v_ref, dst_ref=buf_ref.at[me], send_sem=send_sems.at[n - 1],
                                              recv_sem=recv_sems.at[n - 1], device_id=flip(n), device_id_type=MESH)
            cp.start()
            sent.append(cp)
        buf_ref[me] = v_ref[...]
        for n in range(1, 8):
            px, py, pc = flip(n)
            pltpu.make_async_remote_copy(src_ref=v_ref, dst_ref=buf_ref.at[4 * px + 2 * py + pc],
                                         send_sem=send_sems.at[n - 1], recv_sem=recv_sems.at[n - 1],
                                         device_id=flip(n), device_id_type=MESH).wait_recv()
        for cp in sent:
            cp.wait_send()
        acc = buf_ref[0]
        for d in range(1, 8):
            acc = acc + buf_ref[d]
        out_ref[...] = acc

    vm = pl.BlockSpec(memory_space=pltpu.VMEM)
    return pl.pallas_call(
        body, name=name, in_specs=[vm], out_specs=vm,
        out_shape=jax.ShapeDtypeStruct((rows, w), F32),
        scratch_shapes=[pltpu.VMEM((8, rows, w), F32), pltpu.SemaphoreType.DMA((7,)), pltpu.SemaphoreType.DMA((7,))],
        compiler_params=pltpu.CompilerParams(vmem_limit_bytes=VMEM_LIMIT),
    )(v)


def _add_pair(a, b, name):
    nb, n, w = a.shape
    tr = _rows_tile(n)
    spec = pl.BlockSpec((1, tr, w), lambda j, i: (j, i, 0))

    def body(a_ref, b_ref, o_ref):
        o_ref[...] = a_ref[...] + b_ref[...]

    return pl.pallas_call(body, name=name, grid=(nb, n // tr), in_specs=[spec, spec], out_specs=spec,
                          out_shape=jax.ShapeDtypeStruct(a.shape, F32),
                          compiler_params=_params(("parallel", "parallel")))(a, b)


def _add_chips(r, name):
    nb, n, w = r.shape
    tr = _rows_tile(n)

    def body(r_ref, o_ref):
        o_ref[...] = ((r_ref[0] + r_ref[1]) + r_ref[2]) + r_ref[3]

    return pl.pallas_call(body, name=name, grid=(n // tr,),
                          in_specs=[pl.BlockSpec((nb, tr, w), lambda i: (0, i, 0))],
                          out_specs=pl.BlockSpec((tr, w), lambda i: (i, 0)),
                          out_shape=jax.ShapeDtypeStruct((n, w), F32),
                          compiler_params=_params(("parallel",)))(r)


def _rows_tile(n):
    for t in (512, 368, 256, 184, 128, 80, 64, 40, 32, 16, 8):
        if n % t == 0:
            return t
    return n


def _adamw(w, g, m, v, name):
    rows, wd = w.shape
    tr = _rows_tile(rows)
    c1 = 1.0 - ADAM_B1 ** ADAM_STEP
    c2 = 1.0 - ADAM_B2 ** ADAM_STEP

    def body(w_ref, g_ref, m_ref, v_ref, d_ref, mo_ref, vo_ref):
        gv = g_ref[...]
        m2 = ADAM_B1 * m_ref[...] + (1.0 - ADAM_B1) * gv
        v2 = ADAM_B2 * v_ref[...] + (1.0 - ADAM_B2) * (gv * gv)
        mo_ref[...] = m2
        vo_ref[...] = v2
        d_ref[...] = -ADAM_LR * ((m2 / c1) / (jnp.sqrt(v2 / c2) + ADAM_EPS) + ADAM_WD * w_ref[...])

    spec = pl.BlockSpec((tr, wd), lambda i: (i, 0))
    shp = jax.ShapeDtypeStruct((rows, wd), F32)
    return pl.pallas_call(body, name=name, grid=(rows // tr,), in_specs=[spec] * 4, out_specs=[spec] * 3,
                          out_shape=[shp] * 3, compiler_params=_params(("parallel",)))(w, g, m, v)


BIG = [("w_in", (1024, 3232), 1), ("w_uq", (384, 768), 1), ("w_uk", (256, 512), 1), ("w_uv", (256, 512), 1),
       ("w_glu", (512, 512), 0), ("w_branch_attn", (512, 1024), 1), ("w_branch_ssm", (512, 1024), 1),
       ("w_out", (1024, 1024), 0), ("w_up", (1024, 5632), 1), ("conv_w", (3, 5632), 1), ("w_down", (2816, 1024), 0)]
SMALL = [("mix_norm_pre", (1024,)), ("q_norm", (384,)), ("kv_norm", (256,)), ("ssm_lambda_re", (32, 64)),
         ("ssm_lambda_im", (32, 64)), ("ssm_log_dt", (32,)), ("ssm_b_re", (32, 64, 16)), ("ssm_b_im", (32, 64, 16)),
         ("ssm_c_re", (32, 16, 64)), ("ssm_c_im", (32, 16, 64)), ("ssm_d", (32, 16)), ("b_glu", (512,)),
         ("b_gate", (2048,)), ("mix_norm_post", (1024,)), ("ffn_norm_pre", (1024,)), ("conv_b", (5632,)),
         ("ffn_norm_post", (1024,))]
ROW_ALIGN = 16


def _shard_shape(shape, axis):
    s = list(shape)
    s[axis] //= N_CHIPS
    return tuple(s)


def _part_rows(shape, axis):
    n = math.prod(_shard_shape(shape, axis))
    rows = -(-n // FLAT_W)
    return -(-rows // ROW_ALIGN) * ROW_ALIGN


BIG_ROWS = -(-sum(_part_rows(s, a) for _, s, a in BIG) // 32) * 32


def _pack_big(shards, dtype):
    parts = []
    used = 0
    for name, shape, axis in BIG:
        rows = _part_rows(shape, axis)
        flat = shards[name].astype(dtype).reshape(-1)
        flat = jnp.pad(flat, (0, rows * FLAT_W - flat.shape[0]))
        parts.append(flat.reshape(rows, FLAT_W))
        used += rows
    if BIG_ROWS > used:
        parts.append(jnp.zeros((BIG_ROWS - used, FLAT_W), dtype))
    return jnp.concatenate(parts, axis=0)


def _unpack_big(flat):
    out = {}
    r0 = 0
    for name, shape, axis in BIG:
        rows = _part_rows(shape, axis)
        ss = _shard_shape(shape, axis)
        out[name] = flat[r0:r0 + rows].reshape(-1)[:math.prod(ss)].reshape(ss)
        r0 += rows
    return out


def _split_big(full, j):
    out = {}
    for name, shape, axis in BIG:
        n = shape[axis] // N_CHIPS
        out[name] = lax.slice_in_dim(full[name], j * n, (j + 1) * n, axis=axis)
    return out


SMALL_N = 1 + sum(math.prod(s) for _, s in SMALL)
SMALL_ROWS = -(-(-(-SMALL_N // FLAT_W)) // SUBLANES) * SUBLANES


def _pack_small(first, vals):
    flat = jnp.concatenate([jnp.reshape(first, (1,)).astype(F32)] + [vals[n].astype(F32).reshape(-1) for n, _ in SMALL])
    return jnp.pad(flat, (0, SMALL_ROWS * FLAT_W - SMALL_N)).reshape(SMALL_ROWS, FLAT_W)


def _unpack_small(flat):
    flat = flat.reshape(-1)
    out = {}
    o = 1
    for name, shape in SMALL:
        n = math.prod(shape)
        out[name] = flat[o:o + n].reshape(shape)
        o += n
    return flat[0], out


_ARG_NAMES = ["x", "positions"] + [n for n in (
    "mix_norm_pre", "w_in", "q_norm", "w_uq", "kv_norm", "w_uk", "w_uv", "ssm_lambda_re", "ssm_lambda_im", "ssm_log_dt",
    "ssm_b_re", "ssm_b_im", "ssm_c_re", "ssm_c_im", "ssm_d", "w_glu", "b_glu", "w_branch_attn", "w_branch_ssm",
    "b_gate", "w_out", "mix_norm_post", "ffn_norm_pre", "w_up", "conv_w", "conv_b", "w_down", "ffn_norm_post")]
_WEIGHTS = _ARG_NAMES[2:]


def _gather_weights(w):
    c = lax.axis_index("c")
    half = BIG_ROWS // 2
    w_flat = _pack_big(w, BF16)
    mine = lax.dynamic_slice_in_dim(w_flat, c * half, half, axis=0)
    got_ici = _chip_exchange(mine, "gather_weights_ici", gather=True)
    got_d2d = _core_swap(got_ici, "gather_weights_d2d")
    gathered = jnp.zeros((N_CHIPS, BIG_ROWS, FLAT_W), BF16)
    gathered = lax.dynamic_update_slice_in_dim(gathered, got_ici, c * half, axis=1)
    gathered = lax.dynamic_update_slice_in_dim(gathered, got_d2d, (1 - c) * half, axis=1)
    shards = [_unpack_big(gathered[j]) for j in range(N_CHIPS)]
    full = {name: jnp.concatenate([s[name] for s in shards], axis=axis) for name, _, axis in BIG}

    chip = 2 * lax.axis_index("x") + lax.axis_index("y")
    ncol = 2 * D_FF // N_CHIPS
    placed = lax.dynamic_update_slice_in_dim(jnp.zeros((3, 2 * D_FF), F32), w["conv_w"], chip * ncol, axis=1)
    placed = placed * jnp.where(c == 0, 1.0, 0.0)
    cw_rows = -(-(-(-3 * 2 * D_FF // FLAT_W)) // SUBLANES) * SUBLANES
    placed = jnp.pad(placed.reshape(-1), (0, cw_rows * FLAT_W - 3 * 2 * D_FF)).reshape(cw_rows, FLAT_W)
    conv_w_full = _all_reduce_small(placed, "gather_conv_w").reshape(-1)[:3 * 2 * D_FF].reshape(3, 2 * D_FF)
    return full, conv_w_full


def _reduce_grads(gbig, loss, gsmall):
    c = lax.axis_index("c")
    half = BIG_ROWS // 2
    g_flat = jnp.stack([_pack_big(_split_big(gbig, j), F32) for j in range(N_CHIPS)])
    g_mine = lax.dynamic_slice_in_dim(g_flat, c * half, half, axis=1)
    g_other = lax.dynamic_slice_in_dim(g_flat, (1 - c) * half, half, axis=1)
    pair = _add_pair(g_mine, _core_swap(g_other, "reduce_grads_d2d"), "reduce_grads_pair")
    total = _add_chips(_chip_exchange(pair, "reduce_grads_ici", gather=False), "reduce_grads_chips")
    other = _core_swap(total, "reduce_grads_back")
    g_red = jnp.zeros((BIG_ROWS, FLAT_W), F32)
    g_red = lax.dynamic_update_slice_in_dim(g_red, total, c * half, axis=0)
    g_red = lax.dynamic_update_slice_in_dim(g_red, other, (1 - c) * half, axis=0)

    small_red = _all_reduce_small(_pack_small(loss, {n: gsmall[n] for n, _ in SMALL}), "reduce_small")
    return g_red, small_red


def _step(args):
    x = args["x"][0]
    positions = args["positions"][0]
    tgt = args["loss_target"][0]
    w = {n: args[n][0] for n in _WEIGHTS}
    m = {n: args["m_" + n][0] for n in _WEIGHTS}
    v = {n: args["v_" + n][0] for n in _WEIGHTS}

    full, conv_w_full = _gather_weights(w)
    sp = {n: w[n].reshape(s) for n, s in SMALL}
    for n in ("mix_norm_pre", "q_norm", "kv_norm", "b_glu", "b_gate", "mix_norm_post", "ffn_norm_pre", "conv_b",
              "ffn_norm_post"):
        sp[n] = sp[n].reshape(1, -1)
    sp["conv_w"] = conv_w_full
    loss, gx, gbig, gsmall = _local_step(x, positions, tgt, full, sp)
    g_red, small_red = _reduce_grads(gbig, loss, gsmall)
    loss_all, gsmall_all = _unpack_small(small_red)

    d_big, m_big, v_big = _adamw(_pack_big(w, F32), g_red, _pack_big(m, F32), _pack_big(v, F32), "adamw_big")
    zero = jnp.zeros((), F32)
    d_sm, m_sm, v_sm = _adamw(_pack_small(zero, w), small_red, _pack_small(zero, m), _pack_small(zero, v), "adamw_small")

    outs = {}
    for prefix, big_flat, small_flat in (("grad_", g_red, None), ("delta_", d_big, d_sm), ("new_m_", m_big, m_sm),
                                         ("new_v_", v_big, v_sm)):
        bigs = _unpack_big(big_flat)
        smalls = gsmall_all if small_flat is None else _unpack_small(small_flat)[1]
        for n in _WEIGHTS:
            val = bigs[n] if n in bigs else smalls[n]
            outs[prefix + n] = val.reshape(args[n].shape)
    res = [loss_all, gx[None]]
    for prefix in ("grad_", "delta_", "new_m_", "new_v_"):
        res += [outs[prefix + n] for n in _WEIGHTS]
    return tuple(res)


def kernel(x, positions, mix_norm_pre, w_in, q_norm, w_uq, kv_norm, w_uk, w_uv, ssm_lambda_re, ssm_lambda_im, ssm_log_dt, ssm_b_re, ssm_b_im, ssm_c_re, ssm_c_im, ssm_d, w_glu, b_glu, w_branch_attn, w_branch_ssm, b_gate, w_out, mix_norm_post, ffn_norm_pre, w_up, conv_w, conv_b, w_down, ffn_norm_post, loss_target, m_mix_norm_pre, m_w_in, m_q_norm, m_w_uq, m_kv_norm, m_w_uk, m_w_uv, m_ssm_lambda_re, m_ssm_lambda_im, m_ssm_log_dt, m_ssm_b_re, m_ssm_b_im, m_ssm_c_re, m_ssm_c_im, m_ssm_d, m_w_glu, m_b_glu, m_w_branch_attn, m_w_branch_ssm, m_b_gate, m_w_out, m_mix_norm_post, m_ffn_norm_pre, m_w_up, m_conv_w, m_conv_b, m_w_down, m_ffn_norm_post, v_mix_norm_pre, v_w_in, v_q_norm, v_w_uq, v_kv_norm, v_w_uk, v_w_uv, v_ssm_lambda_re, v_ssm_lambda_im, v_ssm_log_dt, v_ssm_b_re, v_ssm_b_im, v_ssm_c_re, v_ssm_c_im, v_ssm_d, v_w_glu, v_b_glu, v_w_branch_attn, v_w_branch_ssm, v_b_gate, v_w_out, v_mix_norm_post, v_ffn_norm_pre, v_w_up, v_conv_w, v_conv_b, v_w_down, v_ffn_norm_post):
    given = dict(locals())
    return _step(given)
```

```python
import math

import jax
import jax.numpy as jnp
from jax import lax
from jax.experimental import pallas as pl
from jax.experimental.pallas import tpu as pltpu

F32 = jnp.float32
BF16 = jnp.bfloat16
MESH = pl.DeviceIdType.MESH

D_MODEL = 1024
N_HEADS = 8
QK_NOPE = 64
QK_ROPE = 32
QK_HEAD = QK_NOPE + QK_ROPE
V_HEAD = 64
Q_RANK = 384
KV_RANK = 256
ROPE_THETA = 10000.0
SSM_W = 512
SSM_H = 16
SSM_G = 32
SSM_P = 64
SSM_CH = SSM_G * SSM_P
D_FF = 2816
EPS = 1e-6
ADAM_LR = 0.001
ADAM_B1 = 0.9
ADAM_B2 = 0.999
ADAM_EPS = 1e-08
ADAM_WD = 0.01
ADAM_STEP = 10

LANES = 128
SUBLANES = 8
VMEM_LIMIT = 56 * 1024 * 1024

HEAD_SLOT = LANES
HP = N_HEADS * HEAD_SLOT
P_CQ, P_CKV, P_KR, P_U, P_GL, P_END = 0, 384, 640, 768, 1280, 3328
KR_LANE = 64

FLAT_W = 1024
N_CHIPS = 4


def _tile(n, cap):
    if n <= cap:
        return n
    best = None
    for t in range(LANES, cap + 1, LANES):
        if n % t == 0:
            best = t
    assert best is not None, (n, cap)
    return best


def _params(sem):
    return pltpu.CompilerParams(dimension_semantics=sem, vmem_limit_bytes=VMEM_LIMIT)


def _dot(a, b):
    return jnp.dot(a, b, preferred_element_type=F32)


def _dot_nt(a, b):
    return lax.dot_general(a, b, (((1,), (1,)), ((), ())), preferred_element_type=F32)


def _dot_tn(a, b):
    return lax.dot_general(a, b, (((0,), (0,)), ((), ())), preferred_element_type=F32)


def _rms(x, g):
    r = lax.rsqrt(jnp.mean(x * x, axis=-1, keepdims=True) + EPS)
    return x * r * g, r


def _rms_bwd(dy, x, g):
    r = lax.rsqrt(jnp.mean(x * x, axis=-1, keepdims=True) + EPS)
    dyg = dy * g
    dx = r * dyg - x * (r * r * r) * jnp.mean(dyg * x, axis=-1, keepdims=True)
    dg = jnp.sum(dy * x * r, axis=0, keepdims=True)
    return dx, dg


_GELU_K0 = math.sqrt(2.0 / math.pi)
_GELU_K1 = 0.044715


def _gelu(x):
    th = jnp.tanh(_GELU_K0 * (x + _GELU_K1 * x * x * x))
    return 0.5 * x * (1.0 + th)


def _gelu_grad(x):
    th = jnp.tanh(_GELU_K0 * (x + _GELU_K1 * x * x * x))
    return 0.5 * (1.0 + th) + 0.5 * x * (1.0 - th * th) * _GELU_K0 * (1.0 + 3.0 * _GELU_K1 * x * x)


def _sigmoid(x):
    return 1.0 / (1.0 + jnp.exp(-x))


def _rope(q, c, s):
    n = q.shape[1]
    lane = lax.broadcasted_iota(jnp.int32, q.shape, 1) % HEAD_SLOT
    sw = jnp.where(lane < KR_LANE + QK_ROPE // 2, pltpu.roll(q, n - QK_ROPE // 2, 1), pltpu.roll(q, QK_ROPE // 2, 1))
    return q * c + sw * s


def _rope_bwd(dy, c, s):
    n = dy.shape[1]
    t = dy * s
    lane = lax.broadcasted_iota(jnp.int32, dy.shape, 1) % HEAD_SLOT
    sw = jnp.where(lane < KR_LANE + QK_ROPE // 2, pltpu.roll(t, n - QK_ROPE // 2, 1), pltpu.roll(t, QK_ROPE // 2, 1))
    rope_lane = jnp.logical_and(lane >= KR_LANE, lane < KR_LANE + QK_ROPE)
    return dy * c + jnp.where(rope_lane, sw, 0.0)


def _shift_down(x, k, halo):
    xs = pltpu.roll(x, k, 0)
    hs = pltpu.roll(halo, k, 0)
    rows = lax.broadcasted_iota(jnp.int32, halo.shape, 0)
    top = jnp.where(rows < k, hs, xs[0:SUBLANES])
    return jnp.concatenate([top, xs[SUBLANES:]], axis=0)


def _shift_up(x, k, halo):
    t = x.shape[0]
    xs = pltpu.roll(x, t - k, 0)
    hs = pltpu.roll(halo, SUBLANES - k, 0)
    rows = lax.broadcasted_iota(jnp.int32, halo.shape, 0)
    bot = jnp.where(rows >= SUBLANES - k, hs, xs[t - SUBLANES:])
    return jnp.concatenate([xs[:t - SUBLANES], bot], axis=0)


def _mm(a, b, name, out_dtype=F32, bt=False, b_col0=0, n=None, tm_cap=512, tn_cap=1408):
    m, k = a.shape
    if bt:
        n_full = b.shape[0]
        n = n_full
    else:
        n = b.shape[1] if n is None else n
    tm = min(tm_cap, m)
    tn = _tile(n, tn_cap)

    def body(a_ref, b_ref, o_ref):
        if bt:
            o_ref[...] = _dot_nt(a_ref[...], b_ref[...]).astype(out_dtype)
        else:
            o_ref[...] = _dot(a_ref[...], b_ref[...]).astype(out_dtype)

    if bt:
        b_spec = pl.BlockSpec((tn, k), lambda j, i: (j, b_col0))
    else:
        off = b_col0 * (n // tn)
        b_spec = pl.BlockSpec((k, tn), lambda j, i: (0, off + j))
    return pl.pallas_call(
        body, name=name, grid=(n // tn, m // tm),
        in_specs=[pl.BlockSpec((tm, k), lambda j, i: (i, 0)), b_spec],
        out_specs=pl.BlockSpec((tm, tn), lambda j, i: (i, j)),
        out_shape=jax.ShapeDtypeStruct((m, n), out_dtype),
        compiler_params=_params(("parallel", "parallel")),
    )(a, b)


def _mm_tn(a, b, name, tk_cap=512, tn_cap=1664, tl_cap=1024):
    l, k = a.shape
    n = b.shape[1]
    tk = _tile(k, tk_cap)
    tn = _tile(n, tn_cap)
    tl = min(tl_cap, l)

    def body(a_ref, b_ref, o_ref):
        @pl.when(pl.program_id(2) == 0)
        def _():
            o_ref[...] = jnp.zeros_like(o_ref)

        o_ref[...] += _dot_tn(a_ref[...], b_ref[...])

    return pl.pallas_call(
        body, name=name, grid=(k // tk, n // tn, l // tl),
        in_specs=[pl.BlockSpec((tl, tk), lambda i, j, r: (r, i)), pl.BlockSpec((tl, tn), lambda i, j, r: (r, j))],
        out_specs=pl.BlockSpec((tk, tn), lambda i, j, r: (i, j)),
        out_shape=jax.ShapeDtypeStruct((k, n), F32),
        compiler_params=_params(("parallel", "parallel", "arbitrary")),
    )(a, b)


def _row(tl, n):
    return pl.BlockSpec((tl, n), lambda i: (i, 0))


def _const(shape):
    return pl.BlockSpec(shape, lambda i: tuple(0 for _ in shape))


def _proj_fwd(x, g1, win, gq, wuq, gkv, wukv, rc, rs, bg, tl):
    l = x.shape[0]

    def body(x_ref, g1_ref, win_ref, gq_ref, wuq_ref, gkv_ref, wukv_ref, rc_ref, rs_ref, bg_ref,
             hn_ref, cq_ref, ckv_ref, q_ref, k_ref, v_ref, u_ref, gl_ref):
        hn, _ = _rms(x_ref[...], g1_ref[...])
        hnb = hn.astype(BF16)
        hn_ref[...] = hnb
        proj = _dot(hnb, win_ref[...])
        cq = proj[:, P_CQ:P_CKV]
        ckv = proj[:, P_CKV:P_KR]
        kr = proj[:, P_KR:P_U]
        cq_ref[...] = cq
        ckv_ref[...] = ckv
        u_ref[...] = proj[:, P_U:P_GL]
        gl_ref[...] = proj[:, P_GL:P_END] + bg_ref[...]
        qn, _ = _rms(cq, gq_ref[...])
        q = _dot(qn.astype(BF16), wuq_ref[...])
        c1 = rc_ref[...]
        s1 = rs_ref[...]
        q_ref[...] = (_rope(q, jnp.tile(c1, (1, N_HEADS)), jnp.tile(s1, (1, N_HEADS))) * Q_PRESCALE).astype(BF16)
        ckvn, _ = _rms(ckv, gkv_ref[...])
        kv = _dot(ckvn.astype(BF16), wukv_ref[...])
        krr = _rope(kr, c1, s1)
        k_ref[...] = (kv[:, :HP] + jnp.tile(krr, (1, N_HEADS))).astype(BF16)
        v_ref[...] = kv[:, HP:].astype(BF16)

    outs = [(D_MODEL, BF16), (Q_RANK, F32), (KV_RANK, F32), (HP, BF16), (HP, BF16), (HP, BF16),
            (SSM_W, F32), (2 * D_MODEL, F32)]
    return pl.pallas_call(
        body, name="proj_fwd", grid=(l // tl,),
        in_specs=[_row(tl, D_MODEL), _const((1, D_MODEL)), _const((D_MODEL, P_END)), _const((1, Q_RANK)),
                  _const((Q_RANK, HP)), _const((1, KV_RANK)), _const((KV_RANK, 2 * HP)),
                  _row(tl, HEAD_SLOT), _row(tl, HEAD_SLOT), _const((1, 2 * D_MODEL))],
        out_specs=[_row(tl, n) for n, _ in outs],
        out_shape=[jax.ShapeDtypeStruct((l, n), dt) for n, dt in outs],
        compiler_params=_params(("parallel",)),
    )(x, g1, win, gq, wuq, gkv, wukv, rc, rs, bg)


_NEG = -1e30


LOG2E = 1.0 / math.log(2.0)
LN2 = math.log(2.0)
ATTN_SCALE = 1.0 / math.sqrt(QK_HEAD)
Q_PRESCALE = ATTN_SCALE * LOG2E
HEADS_PER_STEP = 2
PAIR_W = HEADS_PER_STEP * HEAD_SLOT


def _causal_pairs(nq, by_query):
    if by_query:
        pairs = [(i, j) for i in range(nq) for j in range(i + 1)]
    else:
        pairs = [(i, j) for j in range(nq) for i in range(j, nq)]
    return jnp.array([p[0] for p in pairs], jnp.int32), jnp.array([p[1] for p in pairs], jnp.int32)


def _diag_mask(s):
    rows = lax.broadcasted_iota(jnp.int32, s.shape, 0)
    cols = lax.broadcasted_iota(jnp.int32, s.shape, 1)
    return jnp.where(cols <= rows, s, _NEG)


def _attn_fwd(q, k, v, tq):
    l = q.shape[0]
    nq = l // tq
    it, jt = _causal_pairs(nq, True)

    def body(it_ref, jt_ref, q_ref, k_ref, v_ref, o_ref, lse_ref, m_ref, l_ref, acc_ref):
        t = pl.program_id(1)
        i = it_ref[t]
        j = jt_ref[t]

        @pl.when(j == 0)
        def _():
            m_ref[...] = jnp.full_like(m_ref, _NEG)
            l_ref[...] = jnp.zeros_like(l_ref)
            acc_ref[...] = jnp.zeros_like(acc_ref)

        def update(on_diagonal):
            for hh in range(HEADS_PER_STEP):
                sl = slice(hh * HEAD_SLOT, (hh + 1) * HEAD_SLOT)
                s = _dot_nt(q_ref[:, sl], k_ref[:, sl])
                if on_diagonal:
                    s = _diag_mask(s)
                m_old = m_ref[hh]
                m_new = jnp.maximum(m_old, jnp.max(s, axis=-1, keepdims=True))
                p = jnp.exp2(s - m_new)
                alpha = jnp.exp2(m_old - m_new)
                l_ref[hh] = alpha * l_ref[hh] + jnp.sum(p, axis=-1, keepdims=True)
                acc_ref[:, sl] = alpha * acc_ref[:, sl] + _dot(p.astype(BF16), v_ref[:, sl])
                m_ref[hh] = m_new

        @pl.when(j < i)
        def _():
            update(False)

        @pl.when(j == i)
        def _():
            update(True)
            for hh in range(HEADS_PER_STEP):
                sl = slice(hh * HEAD_SLOT, (hh + 1) * HEAD_SLOT)
                o_ref[:, sl] = (acc_ref[:, sl] / l_ref[hh]).astype(BF16)
                lse_ref[:, sl] = jnp.broadcast_to(m_ref[hh] + jnp.log(l_ref[hh]) * LOG2E, (tq, HEAD_SLOT))

    blk = (tq, PAIR_W)
    qmap = lambda h, t, it_ref, jt_ref: (it_ref[t], h)
    kmap = lambda h, t, it_ref, jt_ref: (jt_ref[t], h)
    return pl.pallas_call(
        body, name="attn_fwd",
        grid_spec=pltpu.PrefetchScalarGridSpec(
            num_scalar_prefetch=2, grid=(N_HEADS // HEADS_PER_STEP, it.shape[0]),
            in_specs=[pl.BlockSpec(blk, qmap), pl.BlockSpec(blk, kmap), pl.BlockSpec(blk, kmap)],
            out_specs=[pl.BlockSpec(blk, qmap), pl.BlockSpec(blk, qmap)],
            scratch_shapes=[pltpu.VMEM((HEADS_PER_STEP, tq, 1), F32), pltpu.VMEM((HEADS_PER_STEP, tq, 1), F32),
                            pltpu.VMEM((tq, PAIR_W), F32)]),
        out_shape=[jax.ShapeDtypeStruct((l, HP), BF16), jax.ShapeDtypeStruct((l, HP), F32)],
        compiler_params=_params(("parallel", "arbitrary")),
    )(it, jt, q, k, v)


def _attn_bwd(q, k, v, o, do, lse, tq):
    l = q.shape[0]
    nq = l // tq
    it, jt = _causal_pairs(nq, False)

    def body(it_ref, jt_ref, q_ref, k_ref, v_ref, o_ref, do_ref, lse_ref, dq_ref, dk_ref, dv_ref, dka_ref, dva_ref):
        t = pl.program_id(1)
        i = it_ref[t]
        j = jt_ref[t]

        @pl.when(t == 0)
        def _():
            dq_ref[...] = jnp.zeros_like(dq_ref)

        @pl.when(i == j)
        def _():
            dka_ref[...] = jnp.zeros_like(dka_ref)
            dva_ref[...] = jnp.zeros_like(dva_ref)

        def update(on_diagonal):
            r0 = pl.multiple_of(i * tq, tq)
            for hh in range(HEADS_PER_STEP):
                sl = slice(hh * HEAD_SLOT, (hh + 1) * HEAD_SLOT)
                qb = q_ref[:, sl]
                kb = k_ref[:, sl]
                dob = do_ref[:, sl]
                s = _dot_nt(qb, kb)
                if on_diagonal:
                    s = _diag_mask(s)
                p = jnp.exp2(s - lse_ref[:, hh * HEAD_SLOT:hh * HEAD_SLOT + 1])
                dva_ref[:, sl] += _dot_tn(p.astype(BF16), dob)
                dp = _dot_nt(dob, v_ref[:, sl])
                delta = jnp.sum(dob.astype(F32) * o_ref[:, sl].astype(F32), axis=-1, keepdims=True)
                ds = (p * (dp - delta)).astype(BF16)
                dka_ref[:, sl] += _dot_tn(ds, qb)
                dq_ref[pl.ds(r0, tq), sl] += ATTN_SCALE * _dot(ds, kb)

        @pl.when(j < i)
        def _():
            update(False)

        @pl.when(j == i)
        def _():
            update(True)

        @pl.when(i == nq - 1)
        def _():
            dk_ref[...] = (dka_ref[...] * LN2).astype(BF16)
            dv_ref[...] = dva_ref[...].astype(BF16)

    blk = (tq, PAIR_W)
    qmap = lambda h, t, it_ref, jt_ref: (it_ref[t], h)
    kmap = lambda h, t, it_ref, jt_ref: (jt_ref[t], h)
    return pl.pallas_call(
        body, name="attn_bwd",
        grid_spec=pltpu.PrefetchScalarGridSpec(
            num_scalar_prefetch=2, grid=(N_HEADS // HEADS_PER_STEP, it.shape[0]),
            in_specs=[pl.BlockSpec(blk, qmap), pl.BlockSpec(blk, kmap), pl.BlockSpec(blk, kmap),
                      pl.BlockSpec(blk, qmap), pl.BlockSpec(blk, qmap), pl.BlockSpec(blk, qmap)],
            out_specs=[pl.BlockSpec((l, PAIR_W), lambda h, t, it_ref, jt_ref: (0, h)), pl.BlockSpec(blk, kmap),
                       pl.BlockSpec(blk, kmap)],
            scratch_shapes=[pltpu.VMEM(blk, F32), pltpu.VMEM(blk, F32)]),
        out_shape=[jax.ShapeDtypeStruct((l, HP), F32), jax.ShapeDtypeStruct((l, HP), BF16),
                   jax.ShapeDtypeStruct((l, HP), BF16)],
        compiler_params=_params(("parallel", "arbitrary")),
    )(it, jt, q, k, v, o, do, lse)


SSM_CB = 512
SSM_UB = 128
SSM_NB = SSM_CH // SSM_CB


def _scan_tiles(re_ref, im_ref, tab, carry, n_tiles, reverse):
    def tile(n, c):
        cr, ci = c
        idx = (n_tiles - 1 - n) if reverse else n
        r0 = pl.multiple_of(idx * SUBLANES, SUBLANES)
        sr = re_ref[pl.ds(r0, SUBLANES), :]
        si = im_ref[pl.ds(r0, SUBLANES), :]
        for step, k in enumerate((1, 2, 4)):
            mr, mi = tab[2 * step], tab[2 * step + 1]
            sh = (SUBLANES - k) if reverse else k
            rr = pltpu.roll(sr, sh, 0)
            ri = pltpu.roll(si, sh, 0)
            sr, si = sr + mr * rr - mi * ri, si + mr * ri + mi * rr
        pr, pi = tab[6], tab[7]
        sr, si = sr + pr * cr - pi * ci, si + pr * ci + pi * cr
        re_ref[pl.ds(r0, SUBLANES), :] = sr
        im_ref[pl.ds(r0, SUBLANES), :] = si
        if reverse:
            return sr[0:1, :], si[0:1, :]
        return sr[SUBLANES - 1:SUBLANES, :], si[SUBLANES - 1:SUBLANES, :]

    return lax.fori_loop(0, n_tiles, tile, carry, unroll=2)


def _ssm_fwd(u, bre, bim, cre, cim, dvec, tab, tt):
    l = u.shape[0]
    nt = l // tt

    def body(u_ref, bre_ref, bim_ref, cre_ref, cim_ref, d_ref, tab_ref, y_ref, sre_ref, sim_ref, car_ref):
        @pl.when(pl.program_id(1) == 0)
        def _():
            car_ref[...] = jnp.zeros_like(car_ref)

        uf = u_ref[...]
        ub = uf.astype(BF16)
        sre_ref[...] = _dot(ub, bre_ref[0])
        sim_ref[...] = _dot(ub, bim_ref[0])
        tab_v = [tab_ref[n] for n in range(8)]
        cr, ci = _scan_tiles(sre_ref, sim_ref, tab_v, (car_ref[0:1, :], car_ref[8:9, :]), tt // SUBLANES, False)
        car_ref[0:1, :] = cr
        car_ref[8:9, :] = ci
        y_ref[...] = (_dot(sre_ref[...].astype(BF16), cre_ref[0]) - _dot(sim_ref[...].astype(BF16), cim_ref[0])
                      + d_ref[...] * uf)

    return pl.pallas_call(
        body, name="ssm_fwd", grid=(SSM_NB, nt),
        in_specs=[pl.BlockSpec((tt, SSM_UB), lambda m, t: (t, m)),
                  pl.BlockSpec((1, SSM_UB, SSM_CB), lambda m, t: (m, 0, 0)),
                  pl.BlockSpec((1, SSM_UB, SSM_CB), lambda m, t: (m, 0, 0)),
                  pl.BlockSpec((1, SSM_CB, SSM_UB), lambda m, t: (m, 0, 0)),
                  pl.BlockSpec((1, SSM_CB, SSM_UB), lambda m, t: (m, 0, 0)),
                  pl.BlockSpec((1, SSM_UB), lambda m, t: (0, m)),
                  pl.BlockSpec((8, SUBLANES, SSM_CB), lambda m, t: (0, 0, m))],
        out_specs=[pl.BlockSpec((tt, SSM_UB), lambda m, t: (t, m)),
                   pl.BlockSpec((tt, SSM_CB), lambda m, t: (t, m)),
                   pl.BlockSpec((tt, SSM_CB), lambda m, t: (t, m))],
        out_shape=[jax.ShapeDtypeStruct((l, SSM_W), F32), jax.ShapeDtypeStruct((l, SSM_CH), F32),
                   jax.ShapeDtypeStruct((l, SSM_CH), F32)],
        scratch_shapes=[pltpu.VMEM((2 * SUBLANES, SSM_CB), F32)],
        compiler_params=_params(("parallel", "arbitrary")),
    )(u, bre, bim, cre, cim, dvec, tab)


def _ssm_bwd(dy, u, sre, sim, bre, bim, cre, cim, dvec, tab, tt):
    l = u.shape[0]
    nt = l // tt
    tpb = tt // SUBLANES

    def body(dy_ref, u_ref, sre_ref, sim_ref, hre_ref, him_ref, bre_ref, bim_ref, cre_ref, cim_ref, d_ref, tab_ref,
             du_ref, dbre_ref, dbim_ref, dcre_ref, dcim_ref, dare_ref, daim_ref, dd_ref, lr_ref, li_ref, car_ref):
        t = pl.program_id(1)

        @pl.when(t == 0)
        def _():
            car_ref[...] = jnp.zeros_like(car_ref)
            for ref in (dbre_ref, dbim_ref, dcre_ref, dcim_ref, dare_ref, daim_ref, dd_ref):
                ref[...] = jnp.zeros_like(ref)

        dyf = dy_ref[...]
        dyb = dyf.astype(BF16)
        uf = u_ref[...]
        s_re = sre_ref[...]
        s_im = sim_ref[...]
        lr_ref[...] = _dot_nt(dyb, cre_ref[0])
        li_ref[...] = -_dot_nt(dyb, cim_ref[0])
        dcre_ref[0] += _dot_tn(s_re.astype(BF16), dyb)
        dcim_ref[0] -= _dot_tn(s_im.astype(BF16), dyb)
        tab_v = [tab_ref[n] for n in range(8)]
        cr, ci = _scan_tiles(lr_ref, li_ref, tab_v, (car_ref[0:1, :], car_ref[8:9, :]), tpb, True)
        car_ref[0:1, :] = cr
        car_ref[8:9, :] = ci
        lam_r = lr_ref[...]
        lam_i = li_ref[...]
        keep = jnp.where(t == nt - 1, 0.0, 1.0)
        sp_r = _shift_down(s_re, 1, hre_ref[...] * keep)
        sp_i = _shift_down(s_im, 1, him_ref[...] * keep)
        dare_ref[...] += jnp.sum(lam_r * sp_r + lam_i * sp_i, axis=0, keepdims=True)
        daim_ref[...] += jnp.sum(lam_i * sp_r - lam_r * sp_i, axis=0, keepdims=True)
        lrb = lam_r.astype(BF16)
        lib = lam_i.astype(BF16)
        du_ref[...] = _dot_nt(lrb, bre_ref[0]) + _dot_nt(lib, bim_ref[0]) + dyf * d_ref[...]
        ub = uf.astype(BF16)
        dbre_ref[0] += _dot_tn(ub, lrb)
        dbim_ref[0] += _dot_tn(ub, lib)
        dd_ref[...] += jnp.sum(dyf * uf, axis=0, keepdims=True)

    rev = lambda m, t: (nt - 1 - t, m)
    halo = lambda m, t: (jnp.maximum((nt - 1 - t) * tpb - 1, 0), m)
    wb = pl.BlockSpec((1, SSM_UB, SSM_CB), lambda m, t: (m, 0, 0))
    wc = pl.BlockSpec((1, SSM_CB, SSM_UB), lambda m, t: (m, 0, 0))
    vec_c = pl.BlockSpec((1, SSM_CB), lambda m, t: (0, m))
    vec_u = pl.BlockSpec((1, SSM_UB), lambda m, t: (0, m))
    return pl.pallas_call(
        body, name="ssm_bwd", grid=(SSM_NB, nt),
        in_specs=[pl.BlockSpec((tt, SSM_UB), rev), pl.BlockSpec((tt, SSM_UB), rev),
                  pl.BlockSpec((tt, SSM_CB), rev), pl.BlockSpec((tt, SSM_CB), rev),
                  pl.BlockSpec((SUBLANES, SSM_CB), halo), pl.BlockSpec((SUBLANES, SSM_CB), halo),
                  wb, wb, wc, wc, vec_u,
                  pl.BlockSpec((8, SUBLANES, SSM_CB), lambda m, t: (0, 0, m))],
        out_specs=[pl.BlockSpec((tt, SSM_UB), rev), wb, wb, wc, wc, vec_c, vec_c, vec_u],
        out_shape=[jax.ShapeDtypeStruct((l, SSM_W), F32),
                   jax.ShapeDtypeStruct((SSM_NB, SSM_UB, SSM_CB), F32), jax.ShapeDtypeStruct((SSM_NB, SSM_UB, SSM_CB), F32),
                   jax.ShapeDtypeStruct((SSM_NB, SSM_CB, SSM_UB), F32), jax.ShapeDtypeStruct((SSM_NB, SSM_CB, SSM_UB), F32),
                   jax.ShapeDtypeStruct((1, SSM_CH), F32), jax.ShapeDtypeStruct((1, SSM_CH), F32),
                   jax.ShapeDtypeStruct((1, SSM_W), F32)],
        scratch_shapes=[pltpu.VMEM((tt, SSM_CB), F32), pltpu.VMEM((tt, SSM_CB), F32),
                        pltpu.VMEM((2 * SUBLANES, SSM_CB), F32)],
        compiler_params=_params(("parallel", "arbitrary")),
    )(dy, u, sre, sim, sre, sim, bre, bim, cre, cim, dvec, tab)


def _merge_fwd(x, gl, attn, y1, wba, wbs, wglu, bglu, wout, gpost, gpre, tl):
    l = x.shape[0]

    def body(x_ref, gl_ref, at_ref, y1_ref, wba_ref, wbs_ref, wglu_ref, bglu_ref, wout_ref, gpost_ref, gpre_ref,
             a_ref, sm_ref, mg_ref, z_ref, x1_ref, hn2_ref, y3_ref):
        y2 = _gelu(y1_ref[...])
        sg = _sigmoid(_dot(y2.astype(BF16), wglu_ref[...]) + bglu_ref[...])
        y3 = (y2 * sg).astype(BF16)
        y3_ref[...] = y3
        a = _dot(at_ref[...], wba_ref[...])
        sm = _dot(y3, wbs_ref[...])
        a_ref[...] = a
        sm_ref[...] = sm
        g = _sigmoid(gl_ref[...])
        merged = (g[:, :D_MODEL] * a + g[:, D_MODEL:] * sm).astype(BF16)
        mg_ref[...] = merged
        z = _dot(merged, wout_ref[...])
        z_ref[...] = z
        n, _ = _rms(z, gpost_ref[...])
        x1 = x_ref[...] + n
        x1_ref[...] = x1
        hn2, _ = _rms(x1, gpre_ref[...])
        hn2_ref[...] = hn2.astype(BF16)

    outs = [(D_MODEL, F32), (D_MODEL, F32), (D_MODEL, BF16), (D_MODEL, F32), (D_MODEL, F32), (D_MODEL, BF16),
            (SSM_W, BF16)]
    return pl.pallas_call(
        body, name="merge_fwd", grid=(l // tl,),
        in_specs=[_row(tl, D_MODEL), _row(tl, 2 * D_MODEL), _row(tl, HP), _row(tl, SSM_W),
                  _const((HP, D_MODEL)), _const((SSM_W, D_MODEL)), _const((SSM_W, SSM_W)), _const((1, SSM_W)),
                  _const((D_MODEL, D_MODEL)), _const((1, D_MODEL)), _const((1, D_MODEL))],
        out_specs=[_row(tl, n) for n, _ in outs],
        out_shape=[jax.ShapeDtypeStruct((l, n), dt) for n, dt in outs],
        compiler_params=_params(("parallel",)),
    )(x, gl, attn, y1, wba, wbs, wglu, bglu, wout, gpost, gpre)


def _merge_bwd(dhn2a, dhn2b, x1, dx2, z, gl, a, sm, y1, wba, wbs, wglu, bglu, wout, gpost, gpre, tl):
    l = x1.shape[0]

    def body(da_ref, db_ref, x1_ref, dx2_ref, z_ref, gl_ref, a_ref, sm_ref, y1_ref,
             wba_ref, wbs_ref, wglu_ref, bglu_ref, wout_ref, gpost_ref, gpre_ref,
             dx1_ref, dz_ref, dbra_ref, dbrs_ref, dgl_ref, dat_ref, dy1_ref, dt_ref, y2_ref,
             dgpre_ref, dgpost_ref, dbg_ref, dbglu_ref):
        @pl.when(pl.program_id(0) == 0)
        def _():
            for ref in (dgpre_ref, dgpost_ref, dbg_ref, dbglu_ref):
                ref[...] = jnp.zeros_like(ref)

        dhn2 = da_ref[...] + db_ref[...]
        dx1a, dgpre = _rms_bwd(dhn2, x1_ref[...], gpre_ref[...])
        dgpre_ref[...] += dgpre
        dx1 = dx2_ref[...] + dx1a
        dx1_ref[...] = dx1
        dz, dgpost = _rms_bwd(dx1, z_ref[...], gpost_ref[...])
        dgpost_ref[...] += dgpost
        dzb = dz.astype(BF16)
        dz_ref[...] = dzb
        dm = _dot_nt(dzb, wout_ref[...])
        g = _sigmoid(gl_ref[...])
        g0 = g[:, :D_MODEL]
        g1 = g[:, D_MODEL:]
        dbra = (dm * g0).astype(BF16)
        dbrs = (dm * g1).astype(BF16)
        dbra_ref[...] = dbra
        dbrs_ref[...] = dbrs
        dgl0 = dm * a_ref[...] * g0 * (1.0 - g0)
        dgl1 = dm * sm_ref[...] * g1 * (1.0 - g1)
        dgl_ref[:, :D_MODEL] = dgl0.astype(BF16)
        dgl_ref[:, D_MODEL:] = dgl1.astype(BF16)
        dbg_ref[:, :D_MODEL] += jnp.sum(dgl0, axis=0, keepdims=True)
        dbg_ref[:, D_MODEL:] += jnp.sum(dgl1, axis=0, keepdims=True)
        dat_ref[...] = _dot_nt(dbra, wba_ref[...]).astype(BF16)
        dy3 = _dot_nt(dbrs, wbs_ref[...])
        y1v = y1_ref[...]
        y2 = _gelu(y1v)
        y2b = y2.astype(BF16)
        y2_ref[...] = y2b
        sg = _sigmoid(_dot(y2b, wglu_ref[...]) + bglu_ref[...])
        dt = dy3 * y2 * sg * (1.0 - sg)
        dtb = dt.astype(BF16)
        dt_ref[...] = dtb
        dbglu_ref[...] += jnp.sum(dt, axis=0, keepdims=True)
        dy2 = dy3 * sg + _dot_nt(dtb, wglu_ref[...])
        dy1_ref[...] = dy2 * _gelu_grad(y1v)

    outs = [(D_MODEL, F32), (D_MODEL, BF16), (D_MODEL, BF16), (D_MODEL, BF16), (2 * D_MODEL, BF16), (HP, BF16),
            (SSM_W, F32), (SSM_W, BF16), (SSM_W, BF16)]
    accs = [D_MODEL, D_MODEL, 2 * D_MODEL, SSM_W]
    return pl.pallas_call(
        body, name="merge_bwd", grid=(l // tl,),
        in_specs=[_row(tl, D_MODEL), _row(tl, D_MODEL), _row(tl, D_MODEL), _row(tl, D_MODEL), _row(tl, D_MODEL),
                  _row(tl, 2 * D_MODEL), _row(tl, D_MODEL), _row(tl, D_MODEL), _row(tl, SSM_W),
                  _const((HP, D_MODEL)), _const((SSM_W, D_MODEL)), _const((SSM_W, SSM_W)), _const((1, SSM_W)),
                  _const((D_MODEL, D_MODEL)), _const((1, D_MODEL)), _const((1, D_MODEL))],
        out_specs=[_row(tl, n) for n, _ in outs] + [_const((1, n)) for n in accs],
        out_shape=[jax.ShapeDtypeStruct((l, n), dt) for n, dt in outs]
        + [jax.ShapeDtypeStruct((1, n), F32) for n in accs],
        compiler_params=_params(("arbitrary",)),
    )(dhn2a, dhn2b, x1, dx2, z, gl, a, sm, y1, wba, wbs, wglu, bglu, wout, gpost, gpre)


def _proj_bwd(x, dx1, cq, ckv, dq, dk, dv, du, dgl, g1, win, gq, wuq, gkv, wukv, rc, rs, tl):
    l = x.shape[0]

    def body(x_ref, dx1_ref, cq_ref, ckv_ref, dq_ref, dk_ref, dv_ref, du_ref, dgl_ref,
             g1_ref, win_ref, gq_ref, wuq_ref, gkv_ref, wukv_ref, rc_ref, rs_ref,
             gx_ref, dql_ref, qn_ref, ckvn_ref, dproj_ref, dg1_ref, dgq_ref, dgkv_ref):
        @pl.when(pl.program_id(0) == 0)
        def _():
            for ref in (dg1_ref, dgq_ref, dgkv_ref):
                ref[...] = jnp.zeros_like(ref)

        c1 = rc_ref[...]
        s1 = rs_ref[...]
        dql = _rope_bwd(dq_ref[...], jnp.tile(c1, (1, N_HEADS)), jnp.tile(s1, (1, N_HEADS))).astype(BF16)
        dql_ref[...] = dql
        dqn = _dot_nt(dql, wuq_ref[...])
        cq = cq_ref[...]
        qn, _ = _rms(cq, gq_ref[...])
        qn_ref[...] = qn.astype(BF16)
        dcq, dgq = _rms_bwd(dqn, cq, gq_ref[...])
        dgq_ref[...] += dgq
        dkb = dk_ref[...]
        dvb = dv_ref[...]
        dkf = dkb.astype(F32)
        dkr = dkf[:, 0:HEAD_SLOT]
        for h in range(1, N_HEADS):
            dkr = dkr + dkf[:, h * HEAD_SLOT:(h + 1) * HEAD_SLOT]
        dkr = _rope_bwd(dkr, c1, s1)
        dckvn = _dot_nt(dkb, wukv_ref[:, :HP]) + _dot_nt(dvb, wukv_ref[:, HP:])
        ckv = ckv_ref[...]
        ckvn, _ = _rms(ckv, gkv_ref[...])
        ckvn_ref[...] = ckvn.astype(BF16)
        dckv, dgkv = _rms_bwd(dckvn, ckv, gkv_ref[...])
        dgkv_ref[...] += dgkv
        dproj_ref[:, P_CQ:P_CKV] = dcq.astype(BF16)
        dproj_ref[:, P_CKV:P_KR] = dckv.astype(BF16)
        dproj_ref[:, P_KR:P_U] = dkr.astype(BF16)
        dproj_ref[:, P_U:P_GL] = du_ref[...].astype(BF16)
        dproj_ref[:, P_GL:P_END] = dgl_ref[...]
        dhn = _dot_nt(dproj_ref[...], win_ref[...])
        dxa, dg1 = _rms_bwd(dhn, x_ref[...], g1_ref[...])
        dg1_ref[...] += dg1
        gx_ref[...] = dx1_ref[...] + dxa

    outs = [(D_MODEL, F32), (HP, BF16), (Q_RANK, BF16), (KV_RANK, BF16), (P_END, BF16)]
    accs = [D_MODEL, Q_RANK, KV_RANK]
    return pl.pallas_call(
        body, name="proj_bwd", grid=(l // tl,),
        in_specs=[_row(tl, D_MODEL), _row(tl, D_MODEL), _row(tl, Q_RANK), _row(tl, KV_RANK), _row(tl, HP),
                  _row(tl, HP), _row(tl, HP), _row(tl, SSM_W), _row(tl, 2 * D_MODEL),
                  _const((1, D_MODEL)), _const((D_MODEL, P_END)), _const((1, Q_RANK)), _const((Q_RANK, HP)),
                  _const((1, KV_RANK)), _const((KV_RANK, 2 * HP)), _row(tl, HEAD_SLOT), _row(tl, HEAD_SLOT)],
        out_specs=[_row(tl, n) for n, _ in outs] + [_const((1, n)) for n in accs],
        out_shape=[jax.ShapeDtypeStruct((l, n), dt) for n, dt in outs]
        + [jax.ShapeDtypeStruct((1, n), F32) for n in accs],
        compiler_params=_params(("arbitrary",)),
    )(x, dx1, cq, ckv, dq, dk, dv, du, dgl, g1, win, gq, wuq, gkv, wukv, rc, rs)


CONV_CB = 256
CONV_NB = D_FF // CONV_CB


def _conv3(h, halo, w, b):
    return b + w[0:1, :] * _shift_down(h, 2, halo) + w[1:2, :] * _shift_down(h, 1, halo) + w[2:3, :] * h


def _conv_fwd(h, cw, cb, tl):
    l = h.shape[0]

    def body(hg_ref, hv_ref, wg_ref, wv_ref, bg_ref, bv_ref, act_ref, halo_ref):
        @pl.when(pl.program_id(1) == 0)
        def _():
            halo_ref[...] = jnp.zeros_like(halo_ref)

        hg = hg_ref[...]
        hv = hv_ref[...]
        cg = _conv3(hg, halo_ref[0:SUBLANES, :], wg_ref[...], bg_ref[...])
        cv = _conv3(hv, halo_ref[SUBLANES:, :], wv_ref[...], bv_ref[...])
        act_ref[...] = (_gelu(cg) * cv).astype(BF16)
        halo_ref[0:SUBLANES, :] = hg[tl - SUBLANES:, :]
        halo_ref[SUBLANES:, :] = hv[tl - SUBLANES:, :]

    gmap = lambda c, r: (r, c)
    vmap = lambda c, r: (r, CONV_NB + c)
    return pl.pallas_call(
        body, name="conv_fwd", grid=(CONV_NB, l // tl),
        in_specs=[pl.BlockSpec((tl, CONV_CB), gmap), pl.BlockSpec((tl, CONV_CB), vmap),
                  pl.BlockSpec((3, CONV_CB), lambda c, r: (0, c)), pl.BlockSpec((3, CONV_CB), lambda c, r: (0, CONV_NB + c)),
                  pl.BlockSpec((1, CONV_CB), lambda c, r: (0, c)), pl.BlockSpec((1, CONV_CB), lambda c, r: (0, CONV_NB + c))],
        out_specs=pl.BlockSpec((tl, CONV_CB), gmap),
        out_shape=jax.ShapeDtypeStruct((l, D_FF), BF16),
        scratch_shapes=[pltpu.VMEM((2 * SUBLANES, CONV_CB), F32)],
        compiler_params=_params(("parallel", "arbitrary")),
    )(h, h, cw, cw, cb, cb)


def _conv_bwd(h, dact, cw, cb, tl):
    l = h.shape[0]
    nr = l // tl
    tpb = tl // SUBLANES

    def body(hg_ref, hv_ref, hgh_ref, hvh_ref, da_ref, wg_ref, wv_ref, bg_ref, bv_ref,
             dhg_ref, dhv_ref, dwg_ref, dwv_ref, dbg_ref, dbv_ref, car_ref):
        r = pl.program_id(1)

        @pl.when(r == 0)
        def _():
            for ref in (car_ref, dwg_ref, dwv_ref, dbg_ref, dbv_ref):
                ref[...] = jnp.zeros_like(ref)

        keep = jnp.where(r == nr - 1, 0.0, 1.0)
        da = da_ref[...].astype(F32)

        def half(h_ref, halo, w, b):
            hh = h_ref[...]
            h1 = _shift_down(hh, 1, halo)
            h2 = _shift_down(hh, 2, halo)
            return hh, h1, h2, b + w[0:1, :] * h2 + w[1:2, :] * h1 + w[2:3, :] * hh

        wg = wg_ref[...]
        wv = wv_ref[...]
        hg, hg1, hg2, cg = half(hg_ref, hgh_ref[...] * keep, wg, bg_ref[...])
        hv, hv1, hv2, cv = half(hv_ref, hvh_ref[...] * keep, wv, bv_ref[...])
        dcg = da * cv * _gelu_grad(cg)
        dcv = da * _gelu(cg)

        def back(dc, hh, h1, h2, w, nxt, dw_ref, db_ref, dh_ref):
            db_ref[...] += jnp.sum(dc, axis=0, keepdims=True)
            dw_ref[0:1, :] += jnp.sum(dc * h2, axis=0, keepdims=True)
            dw_ref[1:2, :] += jnp.sum(dc * h1, axis=0, keepdims=True)
            dw_ref[2:3, :] += jnp.sum(dc * hh, axis=0, keepdims=True)
            dh = w[2:3, :] * dc + w[1:2, :] * _shift_up(dc, 1, nxt) + w[0:1, :] * _shift_up(dc, 2, nxt)
            dh_ref[...] = dh.astype(BF16)

        back(dcg, hg, hg1, hg2, wg, car_ref[0:SUBLANES, :], dwg_ref, dbg_ref, dhg_ref)
        back(dcv, hv, hv1, hv2, wv, car_ref[SUBLANES:, :], dwv_ref, dbv_ref, dhv_ref)
        car_ref[0:SUBLANES, :] = dcg[0:SUBLANES, :]
        car_ref[SUBLANES:, :] = dcv[0:SUBLANES, :]

    grev = lambda c, r: (nr - 1 - r, c)
    vrev = lambda c, r: (nr - 1 - r, CONV_NB + c)
    ghalo = lambda c, r: (jnp.maximum((nr - 1 - r) * tpb - 1, 0), c)
    vhalo = lambda c, r: (jnp.maximum((nr - 1 - r) * tpb - 1, 0), CONV_NB + c)
    colg = lambda c, r: (0, c)
    colv = lambda c, r: (0, CONV_NB + c)
    return pl.pallas_call(
        body, name="conv_bwd", grid=(CONV_NB, nr),
        in_specs=[pl.BlockSpec((tl, CONV_CB), grev), pl.BlockSpec((tl, CONV_CB), vrev),
                  pl.BlockSpec((SUBLANES, CONV_CB), ghalo), pl.BlockSpec((SUBLANES, CONV_CB), vhalo),
                  pl.BlockSpec((tl, CONV_CB), grev),
                  pl.BlockSpec((3, CONV_CB), colg), pl.BlockSpec((3, CONV_CB), colv),
                  pl.BlockSpec((1, CONV_CB), colg), pl.BlockSpec((1, CONV_CB), colv)],
        out_specs=[pl.BlockSpec((tl, CONV_CB), grev), pl.BlockSpec((tl, CONV_CB), grev),
                   pl.BlockSpec((3, CONV_CB), colg), pl.BlockSpec((3, CONV_CB), colg),
                   pl.BlockSpec((1, CONV_CB), colg), pl.BlockSpec((1, CONV_CB), colg)],
        out_shape=[jax.ShapeDtypeStruct((l, D_FF), BF16), jax.ShapeDtypeStruct((l, D_FF), BF16),
                   jax.ShapeDtypeStruct((3, D_FF), F32), jax.ShapeDtypeStruct((3, D_FF), F32),
                   jax.ShapeDtypeStruct((1, D_FF), F32), jax.ShapeDtypeStruct((1, D_FF), F32)],
        scratch_shapes=[pltpu.VMEM((2 * SUBLANES, CONV_CB), F32)],
        compiler_params=_params(("parallel", "arbitrary")),
    )(h, h, h, h, dact, cw, cw, cb, cb)


def _loss_head(ff, x1, tgt, g, tl):
    l = ff.shape[0]

    def body(ff_ref, x1_ref, tg_ref, g_ref, loss_ref, dx2_ref, dff_ref, dg_ref):
        @pl.when(pl.program_id(0) == 0)
        def _():
            loss_ref[...] = jnp.zeros_like(loss_ref)
            dg_ref[...] = jnp.zeros_like(dg_ref)

        f = ff_ref[...]
        gv = g_ref[...]
        n, _ = _rms(f, gv)
        e = x1_ref[...] + n - tg_ref[...]
        loss_ref[...] += 0.5 * jnp.sum(jnp.mean(e * e, axis=-1, keepdims=True), axis=0, keepdims=True)
        dx2 = e * (1.0 / D_MODEL)
        dx2_ref[...] = dx2
        dff, dg = _rms_bwd(dx2, f, gv)
        dff_ref[...] = dff.astype(BF16)
        dg_ref[...] += dg

    return pl.pallas_call(
        body, name="loss_head", grid=(l // tl,),
        in_specs=[_row(tl, D_MODEL), _row(tl, D_MODEL), _row(tl, D_MODEL), _const((1, D_MODEL))],
        out_specs=[_const((1, LANES)), _row(tl, D_MODEL), _row(tl, D_MODEL), _const((1, D_MODEL))],
        out_shape=[jax.ShapeDtypeStruct((1, LANES), F32), jax.ShapeDtypeStruct((l, D_MODEL), F32),
                   jax.ShapeDtypeStruct((l, D_MODEL), BF16), jax.ShapeDtypeStruct((1, D_MODEL), F32)],
        compiler_params=_params(("arbitrary",)),
    )(ff, x1, tgt, g)


def _ssm_disc(lam_re, lam_im, log_dt, b_re, b_im):
    dt = jnp.exp(log_dt)[:, None]
    mag = jnp.exp(lam_re * dt)
    ang = lam_im * dt
    a_re, a_im = mag * jnp.cos(ang), mag * jnp.sin(ang)
    den = lam_re * lam_re + lam_im * lam_im
    n_re, n_im = a_re - 1.0, a_im
    z_re = (n_re * lam_re + n_im * lam_im) / den
    z_im = (n_im * lam_re - n_re * lam_im) / den
    bb_re = z_re[..., None] * b_re - z_im[..., None] * b_im
    bb_im = z_re[..., None] * b_im + z_im[..., None] * b_re
    return a_re, a_im, bb_re, bb_im


_GPB = SSM_CB // SSM_P


def _embed_b(bb):
    t = bb.transpose(0, 2, 1).reshape(SSM_NB, _GPB, SSM_H, SSM_P)
    return jnp.einsum('mjhp,jk->mjhkp', t, jnp.eye(_GPB, dtype=bb.dtype)).reshape(SSM_NB, SSM_UB, SSM_CB)


def _extract_b(d):
    t = d.reshape(SSM_NB, _GPB, SSM_H, _GPB, SSM_P)
    t = jnp.einsum('mjhkp,jk->mjhp', t, jnp.eye(_GPB, dtype=d.dtype))
    return t.reshape(SSM_G, SSM_H, SSM_P).transpose(0, 2, 1)


def _embed_c(c):
    t = c.transpose(0, 2, 1).reshape(SSM_NB, _GPB, SSM_P, SSM_H)
    return jnp.einsum('mjph,jk->mjpkh', t, jnp.eye(_GPB, dtype=c.dtype)).reshape(SSM_NB, SSM_CB, SSM_UB)


def _extract_c(d):
    t = d.reshape(SSM_NB, _GPB, SSM_P, _GPB, SSM_H)
    t = jnp.einsum('mjpkh,jk->mjph', t, jnp.eye(_GPB, dtype=d.dtype))
    return t.reshape(SSM_G, SSM_P, SSM_H).transpose(0, 2, 1)


def _scan_tables(a_re, a_im, reverse):
    ar = a_re.reshape(1, SSM_CH)
    ai = (-a_im if reverse else a_im).reshape(1, SSM_CH)
    pr, pi = [ar], [ai]
    for _ in range(SUBLANES - 1):
        pr, pi = pr + [pr[-1] * ar - pi[-1] * ai], pi + [pr[-1] * ai + pi[-1] * ar]
    rows = jnp.arange(SUBLANES)[:, None]
    out = []
    for k in (1, 2, 4):
        valid = (rows + k <= SUBLANES - 1) if reverse else (rows >= k)
        out += [jnp.where(valid, pr[k - 1], 0.0), jnp.where(valid, pi[k - 1], 0.0)]
    order = list(range(SUBLANES - 1, -1, -1)) if reverse else list(range(SUBLANES))
    out += [jnp.concatenate([pr[n] for n in order], axis=0), jnp.concatenate([pi[n] for n in order], axis=0)]
    return jnp.stack(out).astype(F32)


def _pad_heads(w, d):
    lead = w.shape[:-1]
    w = w.reshape(lead + (N_HEADS, d))
    w = jnp.pad(w, [(0, 0)] * len(lead) + [(0, 0), (0, HEAD_SLOT - d)])
    return w.reshape(lead + (HP,))


def _unpad_heads(w, d):
    lead = w.shape[:-1]
    return w.reshape(lead + (N_HEADS, HEAD_SLOT))[..., :d].reshape(lead + (N_HEADS * d,))


def _pad_w_in(w):
    z = lambda n: jnp.zeros((w.shape[0], n), w.dtype)
    return jnp.concatenate([w[:, :640], z(KR_LANE), w[:, 640:672], z(HEAD_SLOT - KR_LANE - QK_ROPE), w[:, 672:]], axis=1)


def _unpad_w_in(w):
    return jnp.concatenate([w[:, :640], w[:, P_KR + KR_LANE:P_KR + KR_LANE + QK_ROPE], w[:, P_U:]], axis=1)


def _local_step(x, positions, tgt, wts, sp):
    l = x.shape[0]
    tl = min(256, l)
    ta = min(512, l)

    inv_freq = ROPE_THETA ** (-jnp.arange(0, QK_ROPE, 2, dtype=F32) / QK_ROPE)
    ang = positions.astype(F32)[:, None] * inv_freq
    cos, sin = jnp.cos(ang), jnp.sin(ang)
    one = jnp.ones((l, KR_LANE), F32)
    rc = jnp.concatenate([one, cos, cos, jnp.ones((l, HEAD_SLOT - KR_LANE - QK_ROPE), F32)], axis=1)
    rs = jnp.concatenate([0 * one, -sin, sin, jnp.zeros((l, HEAD_SLOT - KR_LANE - QK_ROPE), F32)], axis=1)

    win = _pad_w_in(wts["w_in"])
    wuq = _pad_heads(wts["w_uq"], QK_HEAD)
    wukv = jnp.concatenate([_pad_heads(wts["w_uk"], QK_NOPE), _pad_heads(wts["w_uv"], V_HEAD)], axis=1)
    wba = jnp.pad(wts["w_branch_attn"].reshape(N_HEADS, V_HEAD, D_MODEL),
                  ((0, 0), (0, HEAD_SLOT - V_HEAD), (0, 0))).reshape(HP, D_MODEL)
    wbs, wglu, wout, wup, wdown = wts["w_branch_ssm"], wts["w_glu"], wts["w_out"], wts["w_up"], wts["w_down"]

    disc_in = (sp["ssm_lambda_re"], sp["ssm_lambda_im"], sp["ssm_log_dt"], sp["ssm_b_re"], sp["ssm_b_im"])
    (a_re, a_im, bb_re, bb_im), disc_vjp = jax.vjp(_ssm_disc, *disc_in)
    bre, bim = _embed_b(bb_re).astype(BF16), _embed_b(bb_im).astype(BF16)
    cre, cim = _embed_c(sp["ssm_c_re"]).astype(BF16), _embed_c(sp["ssm_c_im"]).astype(BF16)
    dvec = sp["ssm_d"].reshape(1, SSM_W)
    tab_f = _scan_tables(a_re, a_im, False)
    tab_r = _scan_tables(a_re, a_im, True)

    g1, gq, gkv = sp["mix_norm_pre"], sp["q_norm"], sp["kv_norm"]
    gpost, gpre, gfin = sp["mix_norm_post"], sp["ffn_norm_pre"], sp["ffn_norm_post"]
    bgate, bglu, convb = sp["b_gate"], sp["b_glu"], sp["conv_b"]
    convw = sp["conv_w"]

    hn, cq, ckv, q, k, v, u, gl = _proj_fwd(x, g1, win, gq, wuq, gkv, wukv, rc, rs, bgate, tl)
    attn, lse = _attn_fwd(q, k, v, ta)
    y1, sre, sim = _ssm_fwd(u, bre, bim, cre, cim, dvec, tab_f, ta)
    a, sm, merged, z, x1, hn2, y3 = _merge_fwd(x, gl, attn, y1, wba, wbs, wglu, bglu, wout, gpost, gpre, tl)
    h = _mm(hn2, wup, "ffn_up")
    act = _conv_fwd(h, convw, convb, ta)
    ff = _mm(act, wdown, "ffn_down")
    loss, dx2, dff, dgfin = _loss_head(ff, x1, tgt, gfin, tl)

    dact = _mm(dff, wdown, "ffn_down_dx", out_dtype=BF16, bt=True)
    d_wdown = _mm_tn(act, dff, "ffn_down_dw")
    dhg, dhv, dwg, dwv, dbg, dbv = _conv_bwd(h, dact, convw, convb, ta)
    d_convw = jnp.concatenate([dwg, dwv], axis=1)
    d_convb = jnp.concatenate([dbg, dbv], axis=1)
    dhn2a = _mm(dhg, wup, "ffn_up_dx_gate", bt=True, b_col0=0)
    dhn2b = _mm(dhv, wup, "ffn_up_dx_val", bt=True, b_col0=1)
    d_wup = jnp.concatenate([_mm_tn(hn2, dhg, "ffn_up_dw_gate"), _mm_tn(hn2, dhv, "ffn_up_dw_val")], axis=1)
    (dx1, dz, dbra, dbrs, dgl, dattn, dy1, dt, y2, dgpre, dgpost, dbgate, dbglu) = _merge_bwd(
        dhn2a, dhn2b, x1, dx2, z, gl, a, sm, y1, wba, wbs, wglu, bglu, wout, gpost, gpre, tl)
    d_wout = _mm_tn(merged, dz, "w_out_dw")
    d_wba = _mm_tn(attn, dbra, "w_branch_attn_dw")
    d_wbs = _mm_tn(y3, dbrs, "w_branch_ssm_dw")
    d_wglu = _mm_tn(y2, dt, "w_glu_dw")
    dq, dk, dv = _attn_bwd(q, k, v, attn, dattn, lse, ta)
    du, dbre, dbim, dcre, dcim, dare, daim, dd = _ssm_bwd(dy1, u, sre, sim, bre, bim, cre, cim, dvec, tab_r, ta)
    gx, dql, qn, ckvn, dproj, dg1, dgq, dgkv = _proj_bwd(
        x, dx1, cq, ckv, dq, dk, dv, du, dgl, g1, win, gq, wuq, gkv, wukv, rc, rs, tl)
    d_win = _mm_tn(hn, dproj, "w_in_dw")
    d_wuq = _mm_tn(qn, dql, "w_uq_dw")
    d_wuk = _mm_tn(ckvn, dk, "w_uk_dw")
    d_wuv = _mm_tn(ckvn, dv, "w_uv_dw")

    d_lre, d_lim, d_ldt, d_bre, d_bim = disc_vjp((dare.reshape(SSM_G, SSM_P), daim.reshape(SSM_G, SSM_P),
                                                  _extract_b(dbre), _extract_b(dbim)))
    big = {
        "w_in": _unpad_w_in(d_win),
        "w_uq": _unpad_heads(d_wuq, QK_HEAD),
        "w_uk": _unpad_heads(d_wuk, QK_NOPE),
        "w_uv": _unpad_heads(d_wuv, V_HEAD),
        "w_glu": d_wglu,
        "w_branch_attn": d_wba.reshape(N_HEADS, HEAD_SLOT, D_MODEL)[:, :V_HEAD].reshape(N_HEADS * V_HEAD, D_MODEL),
        "w_branch_ssm": d_wbs,
        "w_out": d_wout,
        "w_up": d_wup,
        "conv_w": d_convw,
        "w_down": d_wdown,
    }
    small = {
        "mix_norm_pre": dg1, "q_norm": dgq, "kv_norm": dgkv,
        "ssm_lambda_re": d_lre, "ssm_lambda_im": d_lim, "ssm_log_dt": d_ldt,
        "ssm_b_re": d_bre, "ssm_b_im": d_bim,
        "ssm_c_re": _extract_c(dcre), "ssm_c_im": _extract_c(dcim),
        "ssm_d": dd.reshape(SSM_G, SSM_H), "b_glu": dbglu, "b_gate": dbgate,
        "mix_norm_post": dgpost, "ffn_norm_pre": dgpre, "conv_b": d_convb, "ffn_norm_post": dgfin,
    }
    return loss[0, 0], gx, big, small


_ANY = pl.BlockSpec(memory_space=pl.ANY)


def _chip_exchange(a, name, gather):
    blk = a.shape if gather else a.shape[1:]

    def body(a_ref, out_ref, send_sems, recv_sems, local_sem):
        x, y, c = lax.axis_index("x"), lax.axis_index("y"), lax.axis_index("c")
        me = 2 * x + y
        peers = [(1 - x, y), (x, 1 - y), (1 - x, 1 - y)]

        def src(px, py):
            return a_ref if gather else a_ref.at[2 * px + py]

        mine = pltpu.make_async_copy(src(x, y), out_ref.at[me], local_sem)
        mine.start()
        sent = []
        for n, (px, py) in enumerate(peers):
            cp = pltpu.make_async_remote_copy(src_ref=src(px, py), dst_ref=out_ref.at[me], send_sem=send_sems.at[n],
                                              recv_sem=recv_sems.at[n], device_id=(px, py, c), device_id_type=MESH)
            cp.start()
            sent.append(cp)
        for n, (px, py) in enumerate(peers):
            pltpu.make_async_remote_copy(src_ref=src(px, py), dst_ref=out_ref.at[2 * px + py], send_sem=send_sems.at[n],
                                         recv_sem=recv_sems.at[n], device_id=(px, py, c), device_id_type=MESH).wait_recv()
        for cp in sent:
            cp.wait_send()
        mine.wait()

    return pl.pallas_call(
        body, name=name, in_specs=[_ANY], out_specs=_ANY,
        out_shape=jax.ShapeDtypeStruct((N_CHIPS,) + tuple(blk), a.dtype),
        scratch_shapes=[pltpu.SemaphoreType.DMA((3,)), pltpu.SemaphoreType.DMA((3,)), pltpu.SemaphoreType.DMA],
    )(a)


def _core_swap(a, name):
    def body(a_ref, out_ref, send_sem, recv_sem):
        x, y, c = lax.axis_index("x"), lax.axis_index("y"), lax.axis_index("c")
        cp = pltpu.make_async_remote_copy(src_ref=a_ref, dst_ref=out_ref, send_sem=send_sem, recv_sem=recv_sem,
                                          device_id=(x, y, 1 - c), device_id_type=MESH)
        cp.start()
        cp.wait()

    return pl.pallas_call(
        body, name=name, in_specs=[_ANY], out_specs=_ANY,
        out_shape=jax.ShapeDtypeStruct(a.shape, a.dtype),
        scratch_shapes=[pltpu.SemaphoreType.DMA, pltpu.SemaphoreType.DMA],
    )(a)


def _all_reduce_small(v, name):
    rows, w = v.shape

    def body(v_ref, out_ref, buf_ref, send_sems, recv_sems):
        x, y, c = lax.axis_index("x"), lax.axis_index("y"), lax.axis_index("c")
        me = 4 * x + 2 * y + c

        def flip(n):
            return (1 - x if n & 4 else x, 1 - y if n & 2 else y, 1 - c if n & 1 else c)

        sent = []
        for n in range(1, 8):
            cp = pltpu.make_async_remote_copy(src_ref=v_ref, dst_ref=buf_ref.at[me], send_sem=send_sems.at[n - 1],
                                              recv_sem=recv_sems.at[n - 1], device_id=flip(n), device_id_type=MESH)
            cp.start()
            sent.append(cp)
        buf_ref[me] = v_ref[...]
        for n in range(1, 8):
            px, py, pc = flip(n)
            pltpu.make_async_remote_copy(src_ref=v_ref, dst_ref=buf_ref.at[4 * px + 2 * py + pc],
                                         send_sem=send_sems.at[n - 1], recv_sem=recv_sems.at[n - 1],
                                         device_id=flip(n), device_id_type=MESH).wait_recv()
        for cp in sent:
            cp.wait_send()
        acc = buf_ref[0]
        for d in range(1, 8):
            acc = acc + buf_ref[d]
        out_ref[...] = acc

    vm = pl.BlockSpec(memory_space=pltpu.VMEM)
    return pl.pallas_call(
        body, name=name, in_specs=[vm], out_specs=vm,
        out_shape=jax.ShapeDtypeStruct((rows, w), F32),
        scratch_shapes=[pltpu.VMEM((8, rows, w), F32), pltpu.SemaphoreType.DMA((7,)), pltpu.SemaphoreType.DMA((7,))],
        compiler_params=pltpu.CompilerParams(vmem_limit_bytes=VMEM_LIMIT),
    )(v)


def _add_pair(a, b, name):
    nb, n, w = a.shape
    tr = _rows_tile(n)
    spec = pl.BlockSpec((1, tr, w), lambda j, i: (j, i, 0))

    def body(a_ref, b_ref, o_ref):
        o_ref[...] = a_ref[...] + b_ref[...]

    return pl.pallas_call(body, name=name, grid=(nb, n // tr), in_specs=[spec, spec], out_specs=spec,
                          out_shape=jax.ShapeDtypeStruct(a.shape, F32),
                          compiler_params=_params(("parallel", "parallel")))(a, b)


def _add_chips(r, name):
    nb, n, w = r.shape
    tr = _rows_tile(n)

    def body(r_ref, o_ref):
        o_ref[...] = ((r_ref[0] + r_ref[1]) + r_ref[2]) + r_ref[3]

    return pl.pallas_call(body, name=name, grid=(n // tr,),
                          in_specs=[pl.BlockSpec((nb, tr, w), lambda i: (0, i, 0))],
                          out_specs=pl.BlockSpec((tr, w), lambda i: (i, 0)),
                          out_shape=jax.ShapeDtypeStruct((n, w), F32),
                          compiler_params=_params(("parallel",)))(r)


def _rows_tile(n):
    for t in (512, 368, 256, 184, 128, 80, 64, 40, 32, 16, 8):
        if n % t == 0:
            return t
    return n


def _adamw(w, g, m, v, name):
    rows, wd = w.shape
    tr = _rows_tile(rows)
    c1 = 1.0 - ADAM_B1 ** ADAM_STEP
    c2 = 1.0 - ADAM_B2 ** ADAM_STEP

    def body(w_ref, g_ref, m_ref, v_ref, d_ref, mo_ref, vo_ref):
        gv = g_ref[...]
        m2 = ADAM_B1 * m_ref[...] + (1.0 - ADAM_B1) * gv
        v2 = ADAM_B2 * v_ref[...] + (1.0 - ADAM_B2) * (gv * gv)
        mo_ref[...] = m2
        vo_ref[...] = v2
        d_ref[...] = -ADAM_LR * ((m2 / c1) / (jnp.sqrt(v2 / c2) + ADAM_EPS) + ADAM_WD * w_ref[...])

    spec = pl.BlockSpec((tr, wd), lambda i: (i, 0))
    shp = jax.ShapeDtypeStruct((rows, wd), F32)
    return pl.pallas_call(body, name=name, grid=(rows // tr,), in_specs=[spec] * 4, out_specs=[spec] * 3,
                          out_shape=[shp] * 3, compiler_params=_params(("parallel",)))(w, g, m, v)


BIG = [("w_in", (1024, 3232), 1), ("w_uq", (384, 768), 1), ("w_uk", (256, 512), 1), ("w_uv", (256, 512), 1),
       ("w_glu", (512, 512), 0), ("w_branch_attn", (512, 1024), 1), ("w_branch_ssm", (512, 1024), 1),
       ("w_out", (1024, 1024), 0), ("w_up", (1024, 5632), 1), ("conv_w", (3, 5632), 1), ("w_down", (2816, 1024), 0)]
SMALL = [("mix_norm_pre", (1024,)), ("q_norm", (384,)), ("kv_norm", (256,)), ("ssm_lambda_re", (32, 64)),
         ("ssm_lambda_im", (32, 64)), ("ssm_log_dt", (32,)), ("ssm_b_re", (32, 64, 16)), ("ssm_b_im", (32, 64, 16)),
         ("ssm_c_re", (32, 16, 64)), ("ssm_c_im", (32, 16, 64)), ("ssm_d", (32, 16)), ("b_glu", (512,)),
         ("b_gate", (2048,)), ("mix_norm_post", (1024,)), ("ffn_norm_pre", (1024,)), ("conv_b", (5632,)),
         ("ffn_norm_post", (1024,))]
ROW_ALIGN = 16


def _shard_shape(shape, axis):
    s = list(shape)
    s[axis] //= N_CHIPS
    return tuple(s)


def _part_rows(shape, axis):
    n = math.prod(_shard_shape(shape, axis))
    rows = -(-n // FLAT_W)
    return -(-rows // ROW_ALIGN) * ROW_ALIGN


BIG_ROWS = -(-sum(_part_rows(s, a) for _, s, a in BIG) // 32) * 32


def _pack_big(shards, dtype):
    parts = []
    used = 0
    for name, shape, axis in BIG:
        rows = _part_rows(shape, axis)
        flat = shards[name].astype(dtype).reshape(-1)
        flat = jnp.pad(flat, (0, rows * FLAT_W - flat.shape[0]))
        parts.append(flat.reshape(rows, FLAT_W))
        used += rows
    if BIG_ROWS > used:
        parts.append(jnp.zeros((BIG_ROWS - used, FLAT_W), dtype))
    return jnp.concatenate(parts, axis=0)


def _unpack_big(flat):
    out = {}
    r0 = 0
    for name, shape, axis in BIG:
        rows = _part_rows(shape, axis)
        ss = _shard_shape(shape, axis)
        out[name] = flat[r0:r0 + rows].reshape(-1)[:math.prod(ss)].reshape(ss)
        r0 += rows
    return out


def _split_big(full, j):
    out = {}
    for name, shape, axis in BIG:
        n = shape[axis] // N_CHIPS
        out[name] = lax.slice_in_dim(full[name], j * n, (j + 1) * n, axis=axis)
    return out


SMALL_N = 1 + sum(math.prod(s) for _, s in SMALL)
SMALL_ROWS = -(-(-(-SMALL_N // FLAT_W)) // SUBLANES) * SUBLANES


def _pack_small(first, vals):
    flat = jnp.concatenate([jnp.reshape(first, (1,)).astype(F32)] + [vals[n].astype(F32).reshape(-1) for n, _ in SMALL])
    return jnp.pad(flat, (0, SMALL_ROWS * FLAT_W - SMALL_N)).reshape(SMALL_ROWS, FLAT_W)


def _unpack_small(flat):
    flat = flat.reshape(-1)
    out = {}
    o = 1
    for name, shape in SMALL:
        n = math.prod(shape)
        out[name] = flat[o:o + n].reshape(shape)
        o += n
    return flat[0], out


_ARG_NAMES = ["x", "positions"] + [n for n in (
    "mix_norm_pre", "w_in", "q_norm", "w_uq", "kv_norm", "w_uk", "w_uv", "ssm_lambda_re", "ssm_lambda_im", "ssm_log_dt",
    "ssm_b_re", "ssm_b_im", "ssm_c_re", "ssm_c_im", "ssm_d", "w_glu", "b_glu", "w_branch_attn", "w_branch_ssm",
    "b_gate", "w_out", "mix_norm_post", "ffn_norm_pre", "w_up", "conv_w", "conv_b", "w_down", "ffn_norm_post")]
_WEIGHTS = _ARG_NAMES[2:]


def _gather_weights(w):
    c = lax.axis_index("c")
    half = BIG_ROWS // 2
    w_flat = _pack_big(w, BF16)
    mine = lax.dynamic_slice_in_dim(w_flat, c * half, half, axis=0)
    got_ici = _chip_exchange(mine, "gather_weights_ici", gather=True)
    got_d2d = _core_swap(got_ici, "gather_weights_d2d")
    gathered = jnp.zeros((N_CHIPS, BIG_ROWS, FLAT_W), BF16)
    gathered = lax.dynamic_update_slice_in_dim(gathered, got_ici, c * half, axis=1)
    gathered = lax.dynamic_update_slice_in_dim(gathered, got_d2d, (1 - c) * half, axis=1)
    shards = [_unpack_big(gathered[j]) for j in range(N_CHIPS)]
    full = {name: jnp.concatenate([s[name] for s in shards], axis=axis) for name, _, axis in BIG}

    chip = 2 * lax.axis_index("x") + lax.axis_index("y")
    ncol = 2 * D_FF // N_CHIPS
    placed = lax.dynamic_update_slice_in_dim(jnp.zeros((3, 2 * D_FF), F32), w["conv_w"], chip * ncol, axis=1)
    placed = placed * jnp.where(c == 0, 1.0, 0.0)
    cw_rows = -(-(-(-3 * 2 * D_FF // FLAT_W)) // SUBLANES) * SUBLANES
    placed = jnp.pad(placed.reshape(-1), (0, cw_rows * FLAT_W - 3 * 2 * D_FF)).reshape(cw_rows, FLAT_W)
    conv_w_full = _all_reduce_small(placed, "gather_conv_w").reshape(-1)[:3 * 2 * D_FF].reshape(3, 2 * D_FF)
    return full, conv_w_full


def _reduce_grads(gbig, loss, gsmall):
    c = lax.axis_index("c")
    half = BIG_ROWS // 2
    g_flat = jnp.stack([_pack_big(_split_big(gbig, j), F32) for j in range(N_CHIPS)])
    g_mine = lax.dynamic_slice_in_dim(g_flat, c * half, half, axis=1)
    g_other = lax.dynamic_slice_in_dim(g_flat, (1 - c) * half, half, axis=1)
    pair = _add_pair(g_mine, _core_swap(g_other, "reduce_grads_d2d"), "reduce_grads_pair")
    total = _add_chips(_chip_exchange(pair, "reduce_grads_ici", gather=False), "reduce_grads_chips")
    other = _core_swap(total, "reduce_grads_back")
    g_red = jnp.zeros((BIG_ROWS, FLAT_W), F32)
    g_red = lax.dynamic_update_slice_in_dim(g_red, total, c * half, axis=0)
    g_red = lax.dynamic_update_slice_in_dim(g_red, other, (1 - c) * half, axis=0)

    small_red = _all_reduce_small(_pack_small(loss, {n: gsmall[n] for n, _ in SMALL}), "reduce_small")
    return g_red, small_red


def _step(args):
    x = args["x"][0]
    positions = args["positions"][0]
    tgt = args["loss_target"][0]
    w = {n: args[n][0] for n in _WEIGHTS}
    m = {n: args["m_" + n][0] for n in _WEIGHTS}
    v = {n: args["v_" + n][0] for n in _WEIGHTS}

    full, conv_w_full = _gather_weights(w)
    sp = {n: w[n].reshape(s) for n, s in SMALL}
    for n in ("mix_norm_pre", "q_norm", "kv_norm", "b_glu", "b_gate", "mix_norm_post", "ffn_norm_pre", "conv_b",
              "ffn_norm_post"):
        sp[n] = sp[n].reshape(1, -1)
    sp["conv_w"] = conv_w_full
    loss, gx, gbig, gsmall = _local_step(x, positions, tgt, full, sp)
    g_red, small_red = _reduce_grads(gbig, loss, gsmall)
    loss_all, gsmall_all = _unpack_small(small_red)

    d_big, m_big, v_big = _adamw(_pack_big(w, F32), g_red, _pack_big(m, F32), _pack_big(v, F32), "adamw_big")
    zero = jnp.zeros((), F32)
    d_sm, m_sm, v_sm = _adamw(_pack_small(zero, w), small_red, _pack_small(zero, m), _pack_small(zero, v), "adamw_small")

    outs = {}
    for prefix, big_flat, small_flat in (("grad_", g_red, None), ("delta_", d_big, d_sm), ("new_m_", m_big, m_sm),
                                         ("new_v_", v_big, v_sm)):
        bigs = _unpack_big(big_flat)
        smalls = gsmall_all if small_flat is None else _unpack_small(small_flat)[1]
        for n in _WEIGHTS:
            val = bigs[n] if n in bigs else smalls[n]
            outs[prefix + n] = val.reshape(args[n].shape)
    res = [loss_all, gx[None]]
    for prefix in ("grad_", "delta_", "new_m_", "new_v_"):
        res += [outs[prefix + n] for n in _WEIGHTS]
    return tuple(res)


def kernel(x, positions, mix_norm_pre, w_in, q_norm, w_uq, kv_norm, w_uk, w_uv, ssm_lambda_re, ssm_lambda_im, ssm_log_dt, ssm_b_re, ssm_b_im, ssm_c_re, ssm_c_im, ssm_d, w_glu, b_glu, w_branch_attn, w_branch_ssm, b_gate, w_out, mix_norm_post, ffn_norm_pre, w_up, conv_w, conv_b, w_down, ffn_norm_post, loss_target, m_mix_norm_pre, m_w_in, m_q_norm, m_w_uq, m_kv_norm, m_w_uk, m_w_uv, m_ssm_lambda_re, m_ssm_lambda_im, m_ssm_log_dt, m_ssm_b_re, m_ssm_b_im, m_ssm_c_re, m_ssm_c_im, m_ssm_d, m_w_glu, m_b_glu, m_w_branch_attn, m_w_branch_ssm, m_b_gate, m_w_out, m_mix_norm_post, m_ffn_norm_pre, m_w_up, m_conv_w, m_conv_b, m_w_down, m_ffn_norm_post, v_mix_norm_pre, v_w_in, v_q_norm, v_w_uq, v_kv_norm, v_w_uk, v_w_uv, v_ssm_lambda_re, v_ssm_lambda_im, v_ssm_log_dt, v_ssm_b_re, v_ssm_b_im, v_ssm_c_re, v_ssm_c_im, v_ssm_d, v_w_glu, v_b_glu, v_w_branch_attn, v_w_branch_ssm, v_b_gate, v_w_out, v_mix_norm_post, v_ffn_norm_pre, v_w_up, v_conv_w, v_conv_b, v_w_down, v_ffn_norm_post):
    given = dict(locals())
    return _step(given)
```

```python
import math

import jax
import jax.numpy as jnp
from jax import lax
from jax.experimental import pallas as pl
from jax.experimental.pallas import tpu as pltpu

F32 = jnp.float32
BF16 = jnp.bfloat16
MESH = pl.DeviceIdType.MESH

D_MODEL = 1024
N_HEADS = 8
QK_NOPE = 64
QK_ROPE = 32
QK_HEAD = QK_NOPE + QK_ROPE
V_HEAD = 64
Q_RANK = 384
KV_RANK = 256
ROPE_THETA = 10000.0
SSM_W = 512
SSM_H = 16
SSM_G = 32
SSM_P = 64
SSM_CH = SSM_G * SSM_P
D_FF = 2816
EPS = 1e-6
ADAM_LR = 0.001
ADAM_B1 = 0.9
ADAM_B2 = 0.999
ADAM_EPS = 1e-08
ADAM_WD = 0.01
ADAM_STEP = 10

LANES = 128
SUBLANES = 8
VMEM_LIMIT = 56 * 1024 * 1024

HEAD_SLOT = LANES
HP = N_HEADS * HEAD_SLOT
P_CQ, P_CKV, P_KR, P_U, P_GL, P_END = 0, 384, 640, 768, 1280, 3328
KR_LANE = 64

FLAT_W = 1024
N_CHIPS = 4


def _tile(n, cap):
    if n <= cap:
        return n
    best = None
    for t in range(LANES, cap + 1, LANES):
        if n % t == 0:
            best = t
    assert best is not None, (n, cap)
    return best


def _params(sem):
    return pltpu.CompilerParams(dimension_semantics=sem, vmem_limit_bytes=VMEM_LIMIT)


def _dot(a, b):
    return jnp.dot(a, b, preferred_element_type=F32)


def _dot_nt(a, b):
    return lax.dot_general(a, b, (((1,), (1,)), ((), ())), preferred_element_type=F32)


def _dot_tn(a, b):
    return lax.dot_general(a, b, (((0,), (0,)), ((), ())), preferred_element_type=F32)


def _rms(x, g):
    r = lax.rsqrt(jnp.mean(x * x, axis=-1, keepdims=True) + EPS)
    return x * r * g, r


def _rms_bwd(dy, x, g):
    r = lax.rsqrt(jnp.mean(x * x, axis=-1, keepdims=True) + EPS)
    dyg = dy * g
    dx = r * dyg - x * (r * r * r) * jnp.mean(dyg * x, axis=-1, keepdims=True)
    dg = jnp.sum(dy * x * r, axis=0, keepdims=True)
    return dx, dg


_GELU_K0 = math.sqrt(2.0 / math.pi)
_GELU_K1 = 0.044715


def _gelu(x):
    th = jnp.tanh(_GELU_K0 * (x + _GELU_K1 * x * x * x))
    return 0.5 * x * (1.0 + th)


def _gelu_grad(x):
    th = jnp.tanh(_GELU_K0 * (x + _GELU_K1 * x * x * x))
    return 0.5 * (1.0 + th) + 0.5 * x * (1.0 - th * th) * _GELU_K0 * (1.0 + 3.0 * _GELU_K1 * x * x)


def _sigmoid(x):
    return 1.0 / (1.0 + jnp.exp(-x))


def _rope(q, c, s):
    n = q.shape[1]
    lane = lax.broadcasted_iota(jnp.int32, q.shape, 1) % HEAD_SLOT
    sw = jnp.where(lane < KR_LANE + QK_ROPE // 2, pltpu.roll(q, n - QK_ROPE // 2, 1), pltpu.roll(q, QK_ROPE // 2, 1))
    return q * c + sw * s


def _rope_bwd(dy, c, s):
    n = dy.shape[1]
    t = dy * s
    lane = lax.broadcasted_iota(jnp.int32, dy.shape, 1) % HEAD_SLOT
    sw = jnp.where(lane < KR_LANE + QK_ROPE // 2, pltpu.roll(t, n - QK_ROPE // 2, 1), pltpu.roll(t, QK_ROPE // 2, 1))
    rope_lane = jnp.logical_and(lane >= KR_LANE, lane < KR_LANE + QK_ROPE)
    return dy * c + jnp.where(rope_lane, sw, 0.0)


def _shift_down(x, k, halo):
    xs = pltpu.roll(x, k, 0)
    hs = pltpu.roll(halo, k, 0)
    rows = lax.broadcasted_iota(jnp.int32, halo.shape, 0)
    top = jnp.where(rows < k, hs, xs[0:SUBLANES])
    return jnp.concatenate([top, xs[SUBLANES:]], axis=0)


def _shift_up(x, k, halo):
    t = x.shape[0]
    xs = pltpu.roll(x, t - k, 0)
    hs = pltpu.roll(halo, SUBLANES - k, 0)
    rows = lax.broadcasted_iota(jnp.int32, halo.shape, 0)
    bot = jnp.where(rows >= SUBLANES - k, hs, xs[t - SUBLANES:])
    return jnp.concatenate([xs[:t - SUBLANES], bot], axis=0)


def _mm(a, b, name, out_dtype=F32, bt=False, b_col0=0, n=None, tm_cap=512, tn_cap=1408, a_lead=None):
    m, k = a.shape[-2:]
    if bt:
        n_full = b.shape[0]
        n = n_full
    else:
        n = b.shape[1] if n is None else n
    tm = min(tm_cap, m)
    tn = _tile(n, tn_cap)

    def body(a_ref, b_ref, o_ref):
        if bt:
            o_ref[...] = _dot_nt(a_ref[...], b_ref[...]).astype(out_dtype)
        else:
            o_ref[...] = _dot(a_ref[...], b_ref[...]).astype(out_dtype)

    if bt:
        b_spec = pl.BlockSpec((tn, k), lambda j, i: (j, b_col0))
    else:
        off = b_col0 * (n // tn)
        b_spec = pl.BlockSpec((k, tn), lambda j, i: (0, off + j))
    if a_lead is None:
        a_spec = pl.BlockSpec((tm, k), lambda j, i: (i, 0))
    else:
        a_spec = pl.BlockSpec((None, tm, k), lambda j, i: (a_lead, i, 0))
    return pl.pallas_call(
        body, name=name, grid=(n // tn, m // tm),
        in_specs=[a_spec, b_spec],
        out_specs=pl.BlockSpec((tm, tn), lambda j, i: (i, j)),
        out_shape=jax.ShapeDtypeStruct((m, n), out_dtype),
        compiler_params=_params(("parallel", "parallel")),
    )(a, b)


def _mm_tn(a, b, name, tk_cap=512, tn_cap=1664, tl_cap=1024, chips=False):
    l, k = a.shape
    tk = _tile(k, tk_cap)
    tl = min(tl_cap, l)

    def body(a_ref, b_ref, o_ref):
        @pl.when(pl.program_id(2) == 0)
        def _():
            o_ref[...] = jnp.zeros_like(o_ref)

        o_ref[...] += _dot_tn(a_ref[...], b_ref[...])

    if chips:
        n = b.shape[-1] * (b.shape[0] if b.ndim == 3 else 1)
        tn = n // N_CHIPS
        assert tn % LANES == 0
        if b.ndim == 3:
            per = N_CHIPS // b.shape[0]
            b_spec = pl.BlockSpec((None, tl, tn), lambda i, j, r: (j // per, r, j % per))
        else:
            b_spec = pl.BlockSpec((tl, tn), lambda i, j, r: (r, j))
        out_spec = pl.BlockSpec((None, tk, tn), lambda i, j, r: (j, i, 0))
        out_shape = jax.ShapeDtypeStruct((N_CHIPS, k, tn), F32)
    else:
        n = b.shape[1]
        tn = _tile(n, tn_cap)
        b_spec = pl.BlockSpec((tl, tn), lambda i, j, r: (r, j))
        out_spec = pl.BlockSpec((tk, tn), lambda i, j, r: (i, j))
        out_shape = jax.ShapeDtypeStruct((k, n), F32)
    return pl.pallas_call(
        body, name=name, grid=(k // tk, n // tn, l // tl),
        in_specs=[pl.BlockSpec((tl, tk), lambda i, j, r: (r, i)), b_spec],
        out_specs=out_spec, out_shape=out_shape,
        compiler_params=_params(("parallel", "parallel", "arbitrary")),
    )(a, b)


def _row(tl, n):
    return pl.BlockSpec((tl, n), lambda i: (i, 0))


def _const(shape):
    return pl.BlockSpec(shape, lambda i: tuple(0 for _ in shape))


def _proj_fwd(x, g1, win, gq, wuq, gkv, wukv, rc, rs, bg, tl):
    l = x.shape[0]

    def body(x_ref, g1_ref, win_ref, gq_ref, wuq_ref, gkv_ref, wukv_ref, rc_ref, rs_ref, bg_ref,
             hn_ref, cq_ref, ckv_ref, q_ref, k_ref, v_ref, u_ref, gl_ref):
        hn, _ = _rms(x_ref[...], g1_ref[...])
        hnb = hn.astype(BF16)
        hn_ref[...] = hnb
        proj = _dot(hnb, win_ref[...])
        cq = proj[:, P_CQ:P_CKV]
        ckv = proj[:, P_CKV:P_KR]
        kr = proj[:, P_KR:P_U]
        cq_ref[...] = cq
        ckv_ref[...] = ckv
        u_ref[...] = proj[:, P_U:P_GL]
        gl_ref[...] = proj[:, P_GL:P_END] + bg_ref[...]
        qn, _ = _rms(cq, gq_ref[...])
        q = _dot(qn.astype(BF16), wuq_ref[...])
        c1 = rc_ref[...]
        s1 = rs_ref[...]
        q_ref[...] = (_rope(q, jnp.tile(c1, (1, N_HEADS)), jnp.tile(s1, (1, N_HEADS))) * Q_PRESCALE).astype(BF16)
        ckvn, _ = _rms(ckv, gkv_ref[...])
        kv = _dot(ckvn.astype(BF16), wukv_ref[...])
        krr = _rope(kr, c1, s1)
        k_ref[...] = (kv[:, :HP] + jnp.tile(krr, (1, N_HEADS))).astype(BF16)
        v_ref[...] = kv[:, HP:].astype(BF16)

    outs = [(D_MODEL, BF16), (Q_RANK, F32), (KV_RANK, F32), (HP, BF16), (HP, BF16), (HP, BF16),
            (SSM_W, F32), (2 * D_MODEL, F32)]
    return pl.pallas_call(
        body, name="proj_fwd", grid=(l // tl,),
        in_specs=[_row(tl, D_MODEL), _const((1, D_MODEL)), _const((D_MODEL, P_END)), _const((1, Q_RANK)),
                  _const((Q_RANK, HP)), _const((1, KV_RANK)), _const((KV_RANK, 2 * HP)),
                  _row(tl, HEAD_SLOT), _row(tl, HEAD_SLOT), _const((1, 2 * D_MODEL))],
        out_specs=[_row(tl, n) for n, _ in outs],
        out_shape=[jax.ShapeDtypeStruct((l, n), dt) for n, dt in outs],
        compiler_params=_params(("parallel",)),
    )(x, g1, win, gq, wuq, gkv, wukv, rc, rs, bg)


_NEG = -1e30


LOG2E = 1.0 / math.log(2.0)
LN2 = math.log(2.0)
ATTN_SCALE = 1.0 / math.sqrt(QK_HEAD)
Q_PRESCALE = ATTN_SCALE * LOG2E
HEADS_PER_STEP = 2
PAIR_W = HEADS_PER_STEP * HEAD_SLOT


def _causal_pairs(nq, by_query):
    if by_query:
        pairs = [(i, j) for i in range(nq) for j in range(i + 1)]
    else:
        pairs = [(i, j) for j in range(nq) for i in range(j, nq)]
    return jnp.array([p[0] for p in pairs], jnp.int32), jnp.array([p[1] for p in pairs], jnp.int32)


def _diag_mask(s):
    rows = lax.broadcasted_iota(jnp.int32, s.shape, 0)
    cols = lax.broadcasted_iota(jnp.int32, s.shape, 1)
    return jnp.where(cols <= rows, s, _NEG)


def _attn_fwd(q, k, v, tq):
    l = q.shape[0]
    nq = l // tq
    it, jt = _causal_pairs(nq, True)

    def body(it_ref, jt_ref, q_ref, k_ref, v_ref, o_ref, lse_ref, m_ref, l_ref, acc_ref):
        t = pl.program_id(1)
        i = it_ref[t]
        j = jt_ref[t]

        @pl.when(j == 0)
        def _():
            m_ref[...] = jnp.full_like(m_ref, _NEG)
            l_ref[...] = jnp.zeros_like(l_ref)
            acc_ref[...] = jnp.zeros_like(acc_ref)

        def update(on_diagonal):
            for hh in range(HEADS_PER_STEP):
                sl = slice(hh * HEAD_SLOT, (hh + 1) * HEAD_SLOT)
                s = _dot_nt(q_ref[:, sl], k_ref[:, sl])
                if on_diagonal:
                    s = _diag_mask(s)
                m_old = m_ref[hh]
                m_new = jnp.maximum(m_old, jnp.max(s, axis=-1, keepdims=True))
                p = jnp.exp2(s - m_new)
                alpha = jnp.exp2(m_old - m_new)
                l_ref[hh] = alpha * l_ref[hh] + jnp.sum(p, axis=-1, keepdims=True)
                acc_ref[:, sl] = alpha * acc_ref[:, sl] + _dot(p.astype(BF16), v_ref[:, sl])
                m_ref[hh] = m_new

        @pl.when(j < i)
        def _():
            update(False)

        @pl.when(j == i)
        def _():
            update(True)
            for hh in range(HEADS_PER_STEP):
                sl = slice(hh * HEAD_SLOT, (hh + 1) * HEAD_SLOT)
                o_ref[:, sl] = (acc_ref[:, sl] / l_ref[hh]).astype(BF16)
                lse_ref[:, sl] = jnp.broadcast_to(m_ref[hh] + jnp.log(l_ref[hh]) * LOG2E, (tq, HEAD_SLOT))

    blk = (tq, PAIR_W)
    qmap = lambda h, t, it_ref, jt_ref: (it_ref[t], h)
    kmap = lambda h, t, it_ref, jt_ref: (jt_ref[t], h)
    return pl.pallas_call(
        body, name="attn_fwd",
        grid_spec=pltpu.PrefetchScalarGridSpec(
            num_scalar_prefetch=2, grid=(N_HEADS // HEADS_PER_STEP, it.shape[0]),
            in_specs=[pl.BlockSpec(blk, qmap), pl.BlockSpec(blk, kmap), pl.BlockSpec(blk, kmap)],
            out_specs=[pl.BlockSpec(blk, qmap), pl.BlockSpec(blk, qmap)],
            scratch_shapes=[pltpu.VMEM((HEADS_PER_STEP, tq, 1), F32), pltpu.VMEM((HEADS_PER_STEP, tq, 1), F32),
                            pltpu.VMEM((tq, PAIR_W), F32)]),
        out_shape=[jax.ShapeDtypeStruct((l, HP), BF16), jax.ShapeDtypeStruct((l, HP), F32)],
        compiler_params=_params(("parallel", "arbitrary")),
    )(it, jt, q, k, v)


def _attn_bwd(q, k, v, o, do, lse, tq):
    l = q.shape[0]
    nq = l // tq
    it, jt = _causal_pairs(nq, False)

    def body(it_ref, jt_ref, q_ref, k_ref, v_ref, o_ref, do_ref, lse_ref, dq_ref, dk_ref, dv_ref, dka_ref, dva_ref):
        t = pl.program_id(1)
        i = it_ref[t]
        j = jt_ref[t]

        @pl.when(t == 0)
        def _():
            dq_ref[...] = jnp.zeros_like(dq_ref)

        @pl.when(i == j)
        def _():
            dka_ref[...] = jnp.zeros_like(dka_ref)
            dva_ref[...] = jnp.zeros_like(dva_ref)

        def update(on_diagonal):
            r0 = pl.multiple_of(i * tq, tq)
            for hh in range(HEADS_PER_STEP):
                sl = slice(hh * HEAD_SLOT, (hh + 1) * HEAD_SLOT)
                qb = q_ref[:, sl]
                kb = k_ref[:, sl]
                dob = do_ref[:, sl]
                s = _dot_nt(qb, kb)
                if on_diagonal:
                    s = _diag_mask(s)
                p = jnp.exp2(s - lse_ref[:, hh * HEAD_SLOT:hh * HEAD_SLOT + 1])
                dva_ref[:, sl] += _dot_tn(p.astype(BF16), dob)
                dp = _dot_nt(dob, v_ref[:, sl])
                delta = jnp.sum(dob.astype(F32) * o_ref[:, sl].astype(F32), axis=-1, keepdims=True)
                ds = (p * (dp - delta)).astype(BF16)
                dka_ref[:, sl] += _dot_tn(ds, qb)
                dq_ref[pl.ds(r0, tq), sl] += ATTN_SCALE * _dot(ds, kb)

        @pl.when(j < i)
        def _():
            update(False)

        @pl.when(j == i)
        def _():
            update(True)

        @pl.when(i == nq - 1)
        def _():
            dk_ref[...] = (dka_ref[...] * LN2).astype(BF16)
            dv_ref[...] = dva_ref[...].astype(BF16)

    blk = (tq, PAIR_W)
    qmap = lambda h, t, it_ref, jt_ref: (it_ref[t], h)
    kmap = lambda h, t, it_ref, jt_ref: (jt_ref[t], h)
    return pl.pallas_call(
        body, name="attn_bwd",
        grid_spec=pltpu.PrefetchScalarGridSpec(
            num_scalar_prefetch=2, grid=(N_HEADS // HEADS_PER_STEP, it.shape[0]),
            in_specs=[pl.BlockSpec(blk, qmap), pl.BlockSpec(blk, kmap), pl.BlockSpec(blk, kmap),
                      pl.BlockSpec(blk, qmap), pl.BlockSpec(blk, qmap), pl.BlockSpec(blk, qmap)],
            out_specs=[pl.BlockSpec((l, PAIR_W), lambda h, t, it_ref, jt_ref: (0, h)), pl.BlockSpec(blk, kmap),
                       pl.BlockSpec(blk, kmap)],
            scratch_shapes=[pltpu.VMEM(blk, F32), pltpu.VMEM(blk, F32)]),
        out_shape=[jax.ShapeDtypeStruct((l, HP), F32), jax.ShapeDtypeStruct((l, HP), BF16),
                   jax.ShapeDtypeStruct((l, HP), BF16)],
        compiler_params=_params(("parallel", "arbitrary")),
    )(it, jt, q, k, v, o, do, lse)


SSM_CB = 512
SSM_UB = 128
SSM_NB = SSM_CH // SSM_CB


def _scan_tiles(re_ref, im_ref, tab, carry, n_tiles, reverse):
    def tile(n, c):
        cr, ci = c
        idx = (n_tiles - 1 - n) if reverse else n
        r0 = pl.multiple_of(idx * SUBLANES, SUBLANES)
        sr = re_ref[pl.ds(r0, SUBLANES), :]
        si = im_ref[pl.ds(r0, SUBLANES), :]
        for step, k in enumerate((1, 2, 4)):
            mr, mi = tab[2 * step], tab[2 * step + 1]
            sh = (SUBLANES - k) if reverse else k
            rr = pltpu.roll(sr, sh, 0)
            ri = pltpu.roll(si, sh, 0)
            sr, si = sr + mr * rr - mi * ri, si + mr * ri + mi * rr
        pr, pi = tab[6], tab[7]
        sr, si = sr + pr * cr - pi * ci, si + pr * ci + pi * cr
        re_ref[pl.ds(r0, SUBLANES), :] = sr
        im_ref[pl.ds(r0, SUBLANES), :] = si
        if reverse:
            return sr[0:1, :], si[0:1, :]
        return sr[SUBLANES - 1:SUBLANES, :], si[SUBLANES - 1:SUBLANES, :]

    return lax.fori_loop(0, n_tiles, tile, carry, unroll=2)


def _ssm_fwd(u, bre, bim, cre, cim, dvec, tab, tt):
    l = u.shape[0]
    nt = l // tt

    def body(u_ref, bre_ref, bim_ref, cre_ref, cim_ref, d_ref, tab_ref, y_ref, sre_ref, sim_ref, car_ref):
        @pl.when(pl.program_id(1) == 0)
        def _():
            car_ref[...] = jnp.zeros_like(car_ref)

        uf = u_ref[...]
        ub = uf.astype(BF16)
        sre_ref[...] = _dot(ub, bre_ref[0])
        sim_ref[...] = _dot(ub, bim_ref[0])
        tab_v = [tab_ref[n] for n in range(8)]
        cr, ci = _scan_tiles(sre_ref, sim_ref, tab_v, (car_ref[0:1, :], car_ref[8:9, :]), tt // SUBLANES, False)
        car_ref[0:1, :] = cr
        car_ref[8:9, :] = ci
        y_ref[...] = (_dot(sre_ref[...].astype(BF16), cre_ref[0]) - _dot(sim_ref[...].astype(BF16), cim_ref[0])
                      + d_ref[...] * uf)

    return pl.pallas_call(
        body, name="ssm_fwd", grid=(SSM_NB, nt),
        in_specs=[pl.BlockSpec((tt, SSM_UB), lambda m, t: (t, m)),
                  pl.BlockSpec((1, SSM_UB, SSM_CB), lambda m, t: (m, 0, 0)),
                  pl.BlockSpec((1, SSM_UB, SSM_CB), lambda m, t: (m, 0, 0)),
                  pl.BlockSpec((1, SSM_CB, SSM_UB), lambda m, t: (m, 0, 0)),
                  pl.BlockSpec((1, SSM_CB, SSM_UB), lambda m, t: (m, 0, 0)),
                  pl.BlockSpec((1, SSM_UB), lambda m, t: (0, m)),
                  pl.BlockSpec((8, SUBLANES, SSM_CB), lambda m, t: (0, 0, m))],
        out_specs=[pl.BlockSpec((tt, SSM_UB), lambda m, t: (t, m)),
                   pl.BlockSpec((tt, SSM_CB), lambda m, t: (t, m)),
                   pl.BlockSpec((tt, SSM_CB), lambda m, t: (t, m))],
        out_shape=[jax.ShapeDtypeStruct((l, SSM_W), F32), jax.ShapeDtypeStruct((l, SSM_CH), F32),
                   jax.ShapeDtypeStruct((l, SSM_CH), F32)],
        scratch_shapes=[pltpu.VMEM((2 * SUBLANES, SSM_CB), F32)],
        compiler_params=_params(("parallel", "arbitrary")),
    )(u, bre, bim, cre, cim, dvec, tab)


def _ssm_bwd(dy, u, sre, sim, bre, bim, cre, cim, dvec, tab, tt):
    l = u.shape[0]
    nt = l // tt
    tpb = tt // SUBLANES

    def body(dy_ref, u_ref, sre_ref, sim_ref, hre_ref, him_ref, bre_ref, bim_ref, cre_ref, cim_ref, d_ref, tab_ref,
             du_ref, dbre_ref, dbim_ref, dcre_ref, dcim_ref, dare_ref, daim_ref, dd_ref, lr_ref, li_ref, car_ref):
        t = pl.program_id(1)

        @pl.when(t == 0)
        def _():
            car_ref[...] = jnp.zeros_like(car_ref)
            for ref in (dbre_ref, dbim_ref, dcre_ref, dcim_ref, dare_ref, daim_ref, dd_ref):
                ref[...] = jnp.zeros_like(ref)

        dyf = dy_ref[...]
        dyb = dyf.astype(BF16)
        uf = u_ref[...]
        s_re = sre_ref[...]
        s_im = sim_ref[...]
        lr_ref[...] = _dot_nt(dyb, cre_ref[0])
        li_ref[...] = -_dot_nt(dyb, cim_ref[0])
        dcre_ref[0] += _dot_tn(s_re.astype(BF16), dyb)
        dcim_ref[0] -= _dot_tn(s_im.astype(BF16), dyb)
        tab_v = [tab_ref[n] for n in range(8)]
        cr, ci = _scan_tiles(lr_ref, li_ref, tab_v, (car_ref[0:1, :], car_ref[8:9, :]), tpb, True)
        car_ref[0:1, :] = cr
        car_ref[8:9, :] = ci
        lam_r = lr_ref[...]
        lam_i = li_ref[...]
        keep = jnp.where(t == nt - 1, 0.0, 1.0)
        sp_r = _shift_down(s_re, 1, hre_ref[...] * keep)
        sp_i = _shift_down(s_im, 1, him_ref[...] * keep)
        dare_ref[...] += jnp.sum(lam_r * sp_r + lam_i * sp_i, axis=0, keepdims=True)
        daim_ref[...] += jnp.sum(lam_i * sp_r - lam_r * sp_i, axis=0, keepdims=True)
        lrb = lam_r.astype(BF16)
        lib = lam_i.astype(BF16)
        du_ref[...] = _dot_nt(lrb, bre_ref[0]) + _dot_nt(lib, bim_ref[0]) + dyf * d_ref[...]
        ub = uf.astype(BF16)
        dbre_ref[0] += _dot_tn(ub, lrb)
        dbim_ref[0] += _dot_tn(ub, lib)
        dd_ref[...] += jnp.sum(dyf * uf, axis=0, keepdims=True)

    rev = lambda m, t: (nt - 1 - t, m)
    halo = lambda m, t: (jnp.maximum((nt - 1 - t) * tpb - 1, 0), m)
    wb = pl.BlockSpec((1, SSM_UB, SSM_CB), lambda m, t: (m, 0, 0))
    wc = pl.BlockSpec((1, SSM_CB, SSM_UB), lambda m, t: (m, 0, 0))
    vec_c = pl.BlockSpec((1, SSM_CB), lambda m, t: (0, m))
    vec_u = pl.BlockSpec((1, SSM_UB), lambda m, t: (0, m))
    return pl.pallas_call(
        body, name="ssm_bwd", grid=(SSM_NB, nt),
        in_specs=[pl.BlockSpec((tt, SSM_UB), rev), pl.BlockSpec((tt, SSM_UB), rev),
                  pl.BlockSpec((tt, SSM_CB), rev), pl.BlockSpec((tt, SSM_CB), rev),
                  pl.BlockSpec((SUBLANES, SSM_CB), halo), pl.BlockSpec((SUBLANES, SSM_CB), halo),
                  wb, wb, wc, wc, vec_u,
                  pl.BlockSpec((8, SUBLANES, SSM_CB), lambda m, t: (0, 0, m))],
        out_specs=[pl.BlockSpec((tt, SSM_UB), rev), wb, wb, wc, wc, vec_c, vec_c, vec_u],
        out_shape=[jax.ShapeDtypeStruct((l, SSM_W), F32),
                   jax.ShapeDtypeStruct((SSM_NB, SSM_UB, SSM_CB), F32), jax.ShapeDtypeStruct((SSM_NB, SSM_UB, SSM_CB), F32),
                   jax.ShapeDtypeStruct((SSM_NB, SSM_CB, SSM_UB), F32), jax.ShapeDtypeStruct((SSM_NB, SSM_CB, SSM_UB), F32),
                   jax.ShapeDtypeStruct((1, SSM_CH), F32), jax.ShapeDtypeStruct((1, SSM_CH), F32),
                   jax.ShapeDtypeStruct((1, SSM_W), F32)],
        scratch_shapes=[pltpu.VMEM((tt, SSM_CB), F32), pltpu.VMEM((tt, SSM_CB), F32),
                        pltpu.VMEM((2 * SUBLANES, SSM_CB), F32)],
        compiler_params=_params(("parallel", "arbitrary")),
    )(dy, u, sre, sim, sre, sim, bre, bim, cre, cim, dvec, tab)


def _merge_fwd(x, gl, attn, y1, wba, wbs, wglu, bglu, wout, gpost, gpre, tl):
    l = x.shape[0]

    def body(x_ref, gl_ref, at_ref, y1_ref, wba_ref, wbs_ref, wglu_ref, bglu_ref, wout_ref, gpost_ref, gpre_ref,
             a_ref, sm_ref, mg_ref, z_ref, x1_ref, hn2_ref, y3_ref):
        y2 = _gelu(y1_ref[...])
        sg = _sigmoid(_dot(y2.astype(BF16), wglu_ref[...]) + bglu_ref[...])
        y3 = (y2 * sg).astype(BF16)
        y3_ref[...] = y3
        a = _dot(at_ref[...], wba_ref[...])
        sm = _dot(y3, wbs_ref[...])
        a_ref[...] = a
        sm_ref[...] = sm
        g = _sigmoid(gl_ref[...])
        merged = (g[:, :D_MODEL] * a + g[:, D_MODEL:] * sm).astype(BF16)
        mg_ref[...] = merged
        z = _dot(merged, wout_ref[...])
        z_ref[...] = z
        n, _ = _rms(z, gpost_ref[...])
        x1 = x_ref[...] + n
        x1_ref[...] = x1
        hn2, _ = _rms(x1, gpre_ref[...])
        hn2_ref[...] = hn2.astype(BF16)

    outs = [(D_MODEL, F32), (D_MODEL, F32), (D_MODEL, BF16), (D_MODEL, F32), (D_MODEL, F32), (D_MODEL, BF16),
            (SSM_W, BF16)]
    return pl.pallas_call(
        body, name="merge_fwd", grid=(l // tl,),
        in_specs=[_row(tl, D_MODEL), _row(tl, 2 * D_MODEL), _row(tl, HP), _row(tl, SSM_W),
                  _const((HP, D_MODEL)), _const((SSM_W, D_MODEL)), _const((SSM_W, SSM_W)), _const((1, SSM_W)),
                  _const((D_MODEL, D_MODEL)), _const((1, D_MODEL)), _const((1, D_MODEL))],
        out_specs=[_row(tl, n) for n, _ in outs],
        out_shape=[jax.ShapeDtypeStruct((l, n), dt) for n, dt in outs],
        compiler_params=_params(("parallel",)),
    )(x, gl, attn, y1, wba, wbs, wglu, bglu, wout, gpost, gpre)


def _merge_bwd(dhn2a, dhn2b, x1, dx2, z, gl, a, sm, y1, wba, wbs, wglu, bglu, wout, gpost, gpre, tl):
    l = x1.shape[0]

    def body(da_ref, db_ref, x1_ref, dx2_ref, z_ref, gl_ref, a_ref, sm_ref, y1_ref,
             wba_ref, wbs_ref, wglu_ref, bglu_ref, wout_ref, gpost_ref, gpre_ref,
             dx1_ref, dz_ref, dbra_ref, dbrs_ref, dgl_ref, dat_ref, dy1_ref, dt_ref, y2_ref,
             dgpre_ref, dgpost_ref, dbg_ref, dbglu_ref):
        @pl.when(pl.program_id(0) == 0)
        def _():
            for ref in (dgpre_ref, dgpost_ref, dbg_ref, dbglu_ref):
                ref[...] = jnp.zeros_like(ref)

        dhn2 = da_ref[...] + db_ref[...]
        dx1a, dgpre = _rms_bwd(dhn2, x1_ref[...], gpre_ref[...])
        dgpre_ref[...] += dgpre
        dx1 = dx2_ref[...] + dx1a
        dx1_ref[...] = dx1
        dz, dgpost = _rms_bwd(dx1, z_ref[...], gpost_ref[...])
        dgpost_ref[...] += dgpost
        dzb = dz.astype(BF16)
        dz_ref[...] = dzb
        dm = _dot_nt(dzb, wout_ref[...])
        g = _sigmoid(gl_ref[...])
        g0 = g[:, :D_MODEL]
        g1 = g[:, D_MODEL:]
        dbra = (dm * g0).astype(BF16)
        dbrs = (dm * g1).astype(BF16)
        dbra_ref[...] = dbra
        dbrs_ref[...] = dbrs
        dgl0 = dm * a_ref[...] * g0 * (1.0 - g0)
        dgl1 = dm * sm_ref[...] * g1 * (1.0 - g1)
        dgl_ref[:, :D_MODEL] = dgl0.astype(BF16)
        dgl_ref[:, D_MODEL:] = dgl1.astype(BF16)
        dbg_ref[:, :D_MODEL] += jnp.sum(dgl0, axis=0, keepdims=True)
        dbg_ref[:, D_MODEL:] += jnp.sum(dgl1, axis=0, keepdims=True)
        dat_ref[...] = _dot_nt(dbra, wba_ref[...]).astype(BF16)
        dy3 = _dot_nt(dbrs, wbs_ref[...])
        y1v = y1_ref[...]
        y2 = _gelu(y1v)
        y2b = y2.astype(BF16)
        y2_ref[...] = y2b
        sg = _sigmoid(_dot(y2b, wglu_ref[...]) + bglu_ref[...])
        dt = dy3 * y2 * sg * (1.0 - sg)
        dtb = dt.astype(BF16)
        dt_ref[...] = dtb
        dbglu_ref[...] += jnp.sum(dt, axis=0, keepdims=True)
        dy2 = dy3 * sg + _dot_nt(dtb, wglu_ref[...])
        dy1_ref[...] = dy2 * _gelu_grad(y1v)

    outs = [(D_MODEL, F32), (D_MODEL, BF16), (D_MODEL, BF16), (D_MODEL, BF16), (2 * D_MODEL, BF16), (HP, BF16),
            (SSM_W, F32), (SSM_W, BF16), (SSM_W, BF16)]
    accs = [D_MODEL, D_MODEL, 2 * D_MODEL, SSM_W]
    return pl.pallas_call(
        body, name="merge_bwd", grid=(l // tl,),
        in_specs=[_row(tl, D_MODEL), _row(tl, D_MODEL), _row(tl, D_MODEL), _row(tl, D_MODEL), _row(tl, D_MODEL),
                  _row(tl, 2 * D_MODEL), _row(tl, D_MODEL), _row(tl, D_MODEL), _row(tl, SSM_W),
                  _const((HP, D_MODEL)), _const((SSM_W, D_MODEL)), _const((SSM_W, SSM_W)), _const((1, SSM_W)),
                  _const((D_MODEL, D_MODEL)), _const((1, D_MODEL)), _const((1, D_MODEL))],
        out_specs=[_row(tl, n) for n, _ in outs] + [_const((1, n)) for n in accs],
        out_shape=[jax.ShapeDtypeStruct((l, n), dt) for n, dt in outs]
        + [jax.ShapeDtypeStruct((1, n), F32) for n in accs],
        compiler_params=_params(("arbitrary",)),
    )(dhn2a, dhn2b, x1, dx2, z, gl, a, sm, y1, wba, wbs, wglu, bglu, wout, gpost, gpre)


def _proj_bwd(x, dx1, cq, ckv, dq, dk, dv, du, dgl, g1, win, gq, wuq, gkv, wukv, rc, rs, tl):
    l = x.shape[0]

    def body(x_ref, dx1_ref, cq_ref, ckv_ref, dq_ref, dk_ref, dv_ref, du_ref, dgl_ref,
             g1_ref, win_ref, gq_ref, wuq_ref, gkv_ref, wukv_ref, rc_ref, rs_ref,
             gx_ref, dql_ref, qn_ref, ckvn_ref, dproj_ref, dg1_ref, dgq_ref, dgkv_ref):
        @pl.when(pl.program_id(0) == 0)
        def _():
            for ref in (dg1_ref, dgq_ref, dgkv_ref):
                ref[...] = jnp.zeros_like(ref)

        c1 = rc_ref[...]
        s1 = rs_ref[...]
        dql = _rope_bwd(dq_ref[...], jnp.tile(c1, (1, N_HEADS)), jnp.tile(s1, (1, N_HEADS))).astype(BF16)
        dql_ref[...] = dql
        dqn = _dot_nt(dql, wuq_ref[...])
        cq = cq_ref[...]
        qn, _ = _rms(cq, gq_ref[...])
        qn_ref[...] = qn.astype(BF16)
        dcq, dgq = _rms_bwd(dqn, cq, gq_ref[...])
        dgq_ref[...] += dgq
        dkb = dk_ref[...]
        dvb = dv_ref[...]
        dkf = dkb.astype(F32)
        dkr = dkf[:, 0:HEAD_SLOT]
        for h in range(1, N_HEADS):
            dkr = dkr + dkf[:, h * HEAD_SLOT:(h + 1) * HEAD_SLOT]
        dkr = _rope_bwd(dkr, c1, s1)
        dckvn = _dot_nt(dkb, wukv_ref[:, :HP]) + _dot_nt(dvb, wukv_ref[:, HP:])
        ckv = ckv_ref[...]
        ckvn, _ = _rms(ckv, gkv_ref[...])
        ckvn_ref[...] = ckvn.astype(BF16)
        dckv, dgkv = _rms_bwd(dckvn, ckv, gkv_ref[...])
        dgkv_ref[...] += dgkv
        dproj_ref[:, P_CQ:P_CKV] = dcq.astype(BF16)
        dproj_ref[:, P_CKV:P_KR] = dckv.astype(BF16)
        dproj_ref[:, P_KR:P_U] = dkr.astype(BF16)
        dproj_ref[:, P_U:P_GL] = du_ref[...].astype(BF16)
        dproj_ref[:, P_GL:P_END] = dgl_ref[...]
        dhn = _dot_nt(dproj_ref[...], win_ref[...])
        dxa, dg1 = _rms_bwd(dhn, x_ref[...], g1_ref[...])
        dg1_ref[...] += dg1
        gx_ref[...] = dx1_ref[...] + dxa

    outs = [(D_MODEL, F32), (HP, BF16), (Q_RANK, BF16), (KV_RANK, BF16), (P_END, BF16)]
    accs = [D_MODEL, Q_RANK, KV_RANK]
    return pl.pallas_call(
        body, name="proj_bwd", grid=(l // tl,),
        in_specs=[_row(tl, D_MODEL), _row(tl, D_MODEL), _row(tl, Q_RANK), _row(tl, KV_RANK), _row(tl, HP),
                  _row(tl, HP), _row(tl, HP), _row(tl, SSM_W), _row(tl, 2 * D_MODEL),
                  _const((1, D_MODEL)), _const((D_MODEL, P_END)), _const((1, Q_RANK)), _const((Q_RANK, HP)),
                  _const((1, KV_RANK)), _const((KV_RANK, 2 * HP)), _row(tl, HEAD_SLOT), _row(tl, HEAD_SLOT)],
        out_specs=[_row(tl, n) for n, _ in outs] + [_const((1, n)) for n in accs],
        out_shape=[jax.ShapeDtypeStruct((l, n), dt) for n, dt in outs]
        + [jax.ShapeDtypeStruct((1, n), F32) for n in accs],
        compiler_params=_params(("arbitrary",)),
    )(x, dx1, cq, ckv, dq, dk, dv, du, dgl, g1, win, gq, wuq, gkv, wukv, rc, rs)


CONV_CB = 256
CONV_NB = D_FF // CONV_CB


def _conv3(h, halo, w, b):
    return b + w[0:1, :] * _shift_down(h, 2, halo) + w[1:2, :] * _shift_down(h, 1, halo) + w[2:3, :] * h


def _conv_fwd(h, cw, cb, tl):
    l = h.shape[0]

    def body(hg_ref, hv_ref, wg_ref, wv_ref, bg_ref, bv_ref, act_ref, halo_ref):
        @pl.when(pl.program_id(1) == 0)
        def _():
            halo_ref[...] = jnp.zeros_like(halo_ref)

        hg = hg_ref[...]
        hv = hv_ref[...]
        cg = _conv3(hg, halo_ref[0:SUBLANES, :], wg_ref[...], bg_ref[...])
        cv = _conv3(hv, halo_ref[SUBLANES:, :], wv_ref[...], bv_ref[...])
        act_ref[...] = (_gelu(cg) * cv).astype(BF16)
        halo_ref[0:SUBLANES, :] = hg[tl - SUBLANES:, :]
        halo_ref[SUBLANES:, :] = hv[tl - SUBLANES:, :]

    gmap = lambda c, r: (r, c)
    vmap = lambda c, r: (r, CONV_NB + c)
    return pl.pallas_call(
        body, name="conv_fwd", grid=(CONV_NB, l // tl),
        in_specs=[pl.BlockSpec((tl, CONV_CB), gmap), pl.BlockSpec((tl, CONV_CB), vmap),
                  pl.BlockSpec((3, CONV_CB), lambda c, r: (0, c)), pl.BlockSpec((3, CONV_CB), lambda c, r: (0, CONV_NB + c)),
                  pl.BlockSpec((1, CONV_CB), lambda c, r: (0, c)), pl.BlockSpec((1, CONV_CB), lambda c, r: (0, CONV_NB + c))],
        out_specs=pl.BlockSpec((tl, CONV_CB), gmap),
        out_shape=jax.ShapeDtypeStruct((l, D_FF), BF16),
        scratch_shapes=[pltpu.VMEM((2 * SUBLANES, CONV_CB), F32)],
        compiler_params=_params(("parallel", "arbitrary")),
    )(h, h, cw, cw, cb, cb)


def _conv_bwd(h, dact, cw, cb, tl):
    l = h.shape[0]
    nr = l // tl
    tpb = tl // SUBLANES

    def body(hg_ref, hv_ref, hgh_ref, hvh_ref, da_ref, wg_ref, wv_ref, bg_ref, bv_ref,
             dh_ref, dwg_ref, dwv_ref, dbg_ref, dbv_ref, car_ref):
        r = pl.program_id(1)

        @pl.when(r == 0)
        def _():
            for ref in (car_ref, dwg_ref, dwv_ref, dbg_ref, dbv_ref):
                ref[...] = jnp.zeros_like(ref)

        keep = jnp.where(r == nr - 1, 0.0, 1.0)
        da = da_ref[...].astype(F32)

        def half(h_ref, halo, w, b):
            hh = h_ref[...]
            h1 = _shift_down(hh, 1, halo)
            h2 = _shift_down(hh, 2, halo)
            return hh, h1, h2, b + w[0:1, :] * h2 + w[1:2, :] * h1 + w[2:3, :] * hh

        wg = wg_ref[...]
        wv = wv_ref[...]
        hg, hg1, hg2, cg = half(hg_ref, hgh_ref[...] * keep, wg, bg_ref[...])
        hv, hv1, hv2, cv = half(hv_ref, hvh_ref[...] * keep, wv, bv_ref[...])
        dcg = da * cv * _gelu_grad(cg)
        dcv = da * _gelu(cg)

        def back(dc, hh, h1, h2, w, nxt, dw_ref, db_ref, part):
            db_ref[...] += jnp.sum(dc, axis=0, keepdims=True)
            dw_ref[0:1, :] += jnp.sum(dc * h2, axis=0, keepdims=True)
            dw_ref[1:2, :] += jnp.sum(dc * h1, axis=0, keepdims=True)
            dw_ref[2:3, :] += jnp.sum(dc * hh, axis=0, keepdims=True)
            dh = w[2:3, :] * dc + w[1:2, :] * _shift_up(dc, 1, nxt) + w[0:1, :] * _shift_up(dc, 2, nxt)
            dh_ref[part] = dh.astype(BF16)

        back(dcg, hg, hg1, hg2, wg, car_ref[0:SUBLANES, :], dwg_ref, dbg_ref, 0)
        back(dcv, hv, hv1, hv2, wv, car_ref[SUBLANES:, :], dwv_ref, dbv_ref, 1)
        car_ref[0:SUBLANES, :] = dcg[0:SUBLANES, :]
        car_ref[SUBLANES:, :] = dcv[0:SUBLANES, :]

    grev = lambda c, r: (nr - 1 - r, c)
    vrev = lambda c, r: (nr - 1 - r, CONV_NB + c)
    ghalo = lambda c, r: (jnp.maximum((nr - 1 - r) * tpb - 1, 0), c)
    vhalo = lambda c, r: (jnp.maximum((nr - 1 - r) * tpb - 1, 0), CONV_NB + c)
    colg = lambda c, r: (0, c)
    colv = lambda c, r: (0, CONV_NB + c)
    return pl.pallas_call(
        body, name="conv_bwd", grid=(CONV_NB, nr),
        in_specs=[pl.BlockSpec((tl, CONV_CB), grev), pl.BlockSpec((tl, CONV_CB), vrev),
                  pl.BlockSpec((SUBLANES, CONV_CB), ghalo), pl.BlockSpec((SUBLANES, CONV_CB), vhalo),
                  pl.BlockSpec((tl, CONV_CB), grev),
                  pl.BlockSpec((3, CONV_CB), colg), pl.BlockSpec((3, CONV_CB), colv),
                  pl.BlockSpec((1, CONV_CB), colg), pl.BlockSpec((1, CONV_CB), colv)],
        out_specs=[pl.BlockSpec((2, tl, CONV_CB), lambda c, r: (0, nr - 1 - r, c)),
                   pl.BlockSpec((3, CONV_CB), colg), pl.BlockSpec((3, CONV_CB), colg),
                   pl.BlockSpec((1, CONV_CB), colg), pl.BlockSpec((1, CONV_CB), colg)],
        out_shape=[jax.ShapeDtypeStruct((2, l, D_FF), BF16),
                   jax.ShapeDtypeStruct((3, D_FF), F32), jax.ShapeDtypeStruct((3, D_FF), F32),
                   jax.ShapeDtypeStruct((1, D_FF), F32), jax.ShapeDtypeStruct((1, D_FF), F32)],
        scratch_shapes=[pltpu.VMEM((2 * SUBLANES, CONV_CB), F32)],
        compiler_params=_params(("parallel", "arbitrary")),
    )(h, h, h, h, dact, cw, cw, cb, cb)


def _loss_head(ff, x1, tgt, g, tl):
    l = ff.shape[0]

    def body(ff_ref, x1_ref, tg_ref, g_ref, loss_ref, dx2_ref, dff_ref, dg_ref):
        @pl.when(pl.program_id(0) == 0)
        def _():
            loss_ref[...] = jnp.zeros_like(loss_ref)
            dg_ref[...] = jnp.zeros_like(dg_ref)

        f = ff_ref[...]
        gv = g_ref[...]
        n, _ = _rms(f, gv)
        e = x1_ref[...] + n - tg_ref[...]
        loss_ref[...] += 0.5 * jnp.sum(jnp.mean(e * e, axis=-1, keepdims=True), axis=0, keepdims=True)
        dx2 = e * (1.0 / D_MODEL)
        dx2_ref[...] = dx2
        dff, dg = _rms_bwd(dx2, f, gv)
        dff_ref[...] = dff.astype(BF16)
        dg_ref[...] += dg

    return pl.pallas_call(
        body, name="loss_head", grid=(l // tl,),
        in_specs=[_row(tl, D_MODEL), _row(tl, D_MODEL), _row(tl, D_MODEL), _const((1, D_MODEL))],
        out_specs=[_const((1, LANES)), _row(tl, D_MODEL), _row(tl, D_MODEL), _const((1, D_MODEL))],
        out_shape=[jax.ShapeDtypeStruct((1, LANES), F32), jax.ShapeDtypeStruct((l, D_MODEL), F32),
                   jax.ShapeDtypeStruct((l, D_MODEL), BF16), jax.ShapeDtypeStruct((1, D_MODEL), F32)],
        compiler_params=_params(("arbitrary",)),
    )(ff, x1, tgt, g)


def _ssm_disc(lam_re, lam_im, log_dt, b_re, b_im):
    dt = jnp.exp(log_dt)[:, None]
    mag = jnp.exp(lam_re * dt)
    ang = lam_im * dt
    a_re, a_im = mag * jnp.cos(ang), mag * jnp.sin(ang)
    den = lam_re * lam_re + lam_im * lam_im
    n_re, n_im = a_re - 1.0, a_im
    z_re = (n_re * lam_re + n_im * lam_im) / den
    z_im = (n_im * lam_re - n_re * lam_im) / den
    bb_re = z_re[..., None] * b_re - z_im[..., None] * b_im
    bb_im = z_re[..., None] * b_im + z_im[..., None] * b_re
    return a_re, a_im, bb_re, bb_im


_GPB = SSM_CB // SSM_P


def _embed_b(bb):
    t = bb.transpose(0, 2, 1).reshape(SSM_NB, _GPB, SSM_H, SSM_P)
    return jnp.einsum('mjhp,jk->mjhkp', t, jnp.eye(_GPB, dtype=bb.dtype)).reshape(SSM_NB, SSM_UB, SSM_CB)


def _extract_b(d):
    t = d.reshape(SSM_NB, _GPB, SSM_H, _GPB, SSM_P)
    t = jnp.einsum('mjhkp,jk->mjhp', t, jnp.eye(_GPB, dtype=d.dtype))
    return t.reshape(SSM_G, SSM_H, SSM_P).transpose(0, 2, 1)


def _embed_c(c):
    t = c.transpose(0, 2, 1).reshape(SSM_NB, _GPB, SSM_P, SSM_H)
    return jnp.einsum('mjph,jk->mjpkh', t, jnp.eye(_GPB, dtype=c.dtype)).reshape(SSM_NB, SSM_CB, SSM_UB)


def _extract_c(d):
    t = d.reshape(SSM_NB, _GPB, SSM_P, _GPB, SSM_H)
    t = jnp.einsum('mjpkh,jk->mjph', t, jnp.eye(_GPB, dtype=d.dtype))
    return t.reshape(SSM_G, SSM_P, SSM_H).transpose(0, 2, 1)


def _scan_tables(a_re, a_im, reverse):
    ar = a_re.reshape(1, SSM_CH)
    ai = (-a_im if reverse else a_im).reshape(1, SSM_CH)
    pr, pi = [ar], [ai]
    for _ in range(SUBLANES - 1):
        pr, pi = pr + [pr[-1] * ar - pi[-1] * ai], pi + [pr[-1] * ai + pi[-1] * ar]
    rows = jnp.arange(SUBLANES)[:, None]
    out = []
    for k in (1, 2, 4):
        valid = (rows + k <= SUBLANES - 1) if reverse else (rows >= k)
        out += [jnp.where(valid, pr[k - 1], 0.0), jnp.where(valid, pi[k - 1], 0.0)]
    order = list(range(SUBLANES - 1, -1, -1)) if reverse else list(range(SUBLANES))
    out += [jnp.concatenate([pr[n] for n in order], axis=0), jnp.concatenate([pi[n] for n in order], axis=0)]
    return jnp.stack(out).astype(F32)


def _pad_heads(w, d):
    lead = w.shape[:-1]
    w = w.reshape(lead + (N_HEADS, d))
    w = jnp.pad(w, [(0, 0)] * len(lead) + [(0, 0), (0, HEAD_SLOT - d)])
    return w.reshape(lead + (HP,))


def _unpad_heads(w, d):
    lead = w.shape[:-1]
    return w.reshape(lead + (N_HEADS, HEAD_SLOT))[..., :d].reshape(lead + (N_HEADS * d,))


def _chip_major(w, axis):
    k, n = w.shape
    if axis == 0:
        return w.reshape(N_CHIPS, k // N_CHIPS, n)
    return w.reshape(k, N_CHIPS, n // N_CHIPS).transpose(1, 0, 2)


def _from_chip_major(w, axis):
    if axis == 0:
        return w.reshape(-1, w.shape[2])
    return w.transpose(1, 0, 2).reshape(w.shape[1], -1)


def _pad_w_in(w):
    z = lambda n: jnp.zeros((w.shape[0], n), w.dtype)
    return jnp.concatenate([w[:, :640], z(KR_LANE), w[:, 640:672], z(HEAD_SLOT - KR_LANE - QK_ROPE), w[:, 672:]], axis=1)


def _unpad_w_in(w):
    return jnp.concatenate([w[:, :640], w[:, P_KR + KR_LANE:P_KR + KR_LANE + QK_ROPE], w[:, P_U:]], axis=1)


def _local_step(x, positions, tgt, wts, sp):
    l = x.shape[0]
    tl = min(256, l)
    ta = min(512, l)

    inv_freq = ROPE_THETA ** (-jnp.arange(0, QK_ROPE, 2, dtype=F32) / QK_ROPE)
    ang = positions.astype(F32)[:, None] * inv_freq
    cos, sin = jnp.cos(ang), jnp.sin(ang)
    one = jnp.ones((l, KR_LANE), F32)
    rc = jnp.concatenate([one, cos, cos, jnp.ones((l, HEAD_SLOT - KR_LANE - QK_ROPE), F32)], axis=1)
    rs = jnp.concatenate([0 * one, -sin, sin, jnp.zeros((l, HEAD_SLOT - KR_LANE - QK_ROPE), F32)], axis=1)

    win = _pad_w_in(wts["w_in"])
    wuq = _pad_heads(wts["w_uq"], QK_HEAD)
    wukv = jnp.concatenate([_pad_heads(wts["w_uk"], QK_NOPE), _pad_heads(wts["w_uv"], V_HEAD)], axis=1)
    wba = jnp.pad(wts["w_branch_attn"].reshape(N_HEADS, V_HEAD, D_MODEL),
                  ((0, 0), (0, HEAD_SLOT - V_HEAD), (0, 0))).reshape(HP, D_MODEL)
    wbs, wglu, wout, wup, wdown = wts["w_branch_ssm"], wts["w_glu"], wts["w_out"], wts["w_up"], wts["w_down"]

    disc_in = (sp["ssm_lambda_re"], sp["ssm_lambda_im"], sp["ssm_log_dt"], sp["ssm_b_re"], sp["ssm_b_im"])
    (a_re, a_im, bb_re, bb_im), disc_vjp = jax.vjp(_ssm_disc, *disc_in)
    bre, bim = _embed_b(bb_re).astype(BF16), _embed_b(bb_im).astype(BF16)
    cre, cim = _embed_c(sp["ssm_c_re"]).astype(BF16), _embed_c(sp["ssm_c_im"]).astype(BF16)
    dvec = sp["ssm_d"].reshape(1, SSM_W)
    tab_f = _scan_tables(a_re, a_im, False)
    tab_r = _scan_tables(a_re, a_im, True)

    g1, gq, gkv = sp["mix_norm_pre"], sp["q_norm"], sp["kv_norm"]
    gpost, gpre, gfin = sp["mix_norm_post"], sp["ffn_norm_pre"], sp["ffn_norm_post"]
    bgate, bglu, convb = sp["b_gate"], sp["b_glu"], sp["conv_b"]
    convw = sp["conv_w"]

    hn, cq, ckv, q, k, v, u, gl = _proj_fwd(x, g1, win, gq, wuq, gkv, wukv, rc, rs, bgate, tl)
    attn, lse = _attn_fwd(q, k, v, ta)
    y1, sre, sim = _ssm_fwd(u, bre, bim, cre, cim, dvec, tab_f, ta)
    a, sm, merged, z, x1, hn2, y3 = _merge_fwd(x, gl, attn, y1, wba, wbs, wglu, bglu, wout, gpost, gpre, tl)
    h = _mm(hn2, wup, "ffn_up")
    act = _conv_fwd(h, convw, convb, ta)
    ff = _mm(act, wdown, "ffn_down")
    loss, dx2, dff, dgfin = _loss_head(ff, x1, tgt, gfin, tl)

    dact = _mm(dff, wdown, "ffn_down_dx", out_dtype=BF16, bt=True)
    d_wdown = _mm_tn(act, dff, "ffn_down_dw")
    dh, dwg, dwv, dbg, dbv = _conv_bwd(h, dact, convw, convb, ta)
    d_convw = jnp.concatenate([dwg, dwv], axis=1)
    d_convb = jnp.concatenate([dbg, dbv], axis=1)
    dhn2a = _mm(dh, wup, "ffn_up_dx_gate", bt=True, b_col0=0, a_lead=0)
    dhn2b = _mm(dh, wup, "ffn_up_dx_val", bt=True, b_col0=1, a_lead=1)
    d_wup = _mm_tn(hn2, dh, "ffn_up_dw", chips=True)
    (dx1, dz, dbra, dbrs, dgl, dattn, dy1, dt, y2, dgpre, dgpost, dbgate, dbglu) = _merge_bwd(
        dhn2a, dhn2b, x1, dx2, z, gl, a, sm, y1, wba, wbs, wglu, bglu, wout, gpost, gpre, tl)
    d_wout = _mm_tn(merged, dz, "w_out_dw")
    d_wba = _mm_tn(attn, dbra, "w_branch_attn_dw", chips=True)
    d_wbs = _mm_tn(y3, dbrs, "w_branch_ssm_dw", chips=True)
    d_wglu = _mm_tn(y2, dt, "w_glu_dw")
    dq, dk, dv = _attn_bwd(q, k, v, attn, dattn, lse, ta)
    du, dbre, dbim, dcre, dcim, dare, daim, dd = _ssm_bwd(dy1, u, sre, sim, bre, bim, cre, cim, dvec, tab_r, ta)
    gx, dql, qn, ckvn, dproj, dg1, dgq, dgkv = _proj_bwd(
        x, dx1, cq, ckv, dq, dk, dv, du, dgl, g1, win, gq, wuq, gkv, wukv, rc, rs, tl)
    d_win = _mm_tn(hn, dproj, "w_in_dw")
    d_wuq = _mm_tn(qn, dql, "w_uq_dw")
    d_wuk = _mm_tn(ckvn, dk, "w_uk_dw")
    d_wuv = _mm_tn(ckvn, dv, "w_uv_dw")

    d_lre, d_lim, d_ldt, d_bre, d_bim = disc_vjp((dare.reshape(SSM_G, SSM_P), daim.reshape(SSM_G, SSM_P),
                                                  _extract_b(dbre), _extract_b(dbim)))
    ncol = D_MODEL // N_CHIPS
    big = {
        "w_in": _chip_major(_unpad_w_in(d_win), 1),
        "w_uq": _chip_major(_unpad_heads(d_wuq, QK_HEAD), 1),
        "w_uk": _chip_major(_unpad_heads(d_wuk, QK_NOPE), 1),
        "w_uv": _chip_major(_unpad_heads(d_wuv, V_HEAD), 1),
        "w_glu": _chip_major(d_wglu, 0),
        "w_branch_attn": d_wba.reshape(N_CHIPS, N_HEADS, HEAD_SLOT, ncol)[:, :, :V_HEAD].reshape(
            N_CHIPS, N_HEADS * V_HEAD, ncol),
        "w_branch_ssm": d_wbs,
        "w_out": _chip_major(d_wout, 0),
        "w_up": d_wup,
        "w_down": _chip_major(d_wdown, 0),
    }
    small = {
        "conv_w": d_convw,
        "mix_norm_pre": dg1, "q_norm": dgq, "kv_norm": dgkv,
        "ssm_lambda_re": d_lre, "ssm_lambda_im": d_lim, "ssm_log_dt": d_ldt,
        "ssm_b_re": d_bre, "ssm_b_im": d_bim,
        "ssm_c_re": _extract_c(dcre), "ssm_c_im": _extract_c(dcim),
        "ssm_d": dd.reshape(SSM_G, SSM_H), "b_glu": dbglu, "b_gate": dbgate,
        "mix_norm_post": dgpost, "ffn_norm_pre": dgpre, "conv_b": d_convb, "ffn_norm_post": dgfin,
    }
    return loss[0, 0], gx, big, small


_ANY = pl.BlockSpec(memory_space=pl.ANY)


ROW_TILE = 16


def _place():
    x, y, c = lax.axis_index("x"), lax.axis_index("y"), lax.axis_index("c")
    return x, y, c, 2 * x + y, [(1 - x, y), (x, 1 - y), (1 - x, 1 - y)]


def _half(rows, which):
    hr = rows // 2
    return pl.ds(pl.multiple_of(which * hr, ROW_TILE), hr)


def _remote(src, dst, send_sems, recv_sems, n, dev):
    return pltpu.make_async_remote_copy(src_ref=src, dst_ref=dst, send_sem=send_sems.at[n], recv_sem=recv_sems.at[n],
                                        device_id=dev, device_id_type=MESH)


def _gather_big(shards):
    nw = len(shards)
    rows = [s.shape[0] for s in shards]

    def body(*refs):
        ins, outs = refs[:nw], refs[nw:2 * nw]
        ici_send, ici_recv, d2d_send, d2d_recv, local_sems = refs[2 * nw:]
        x, y, c, me, peers = _place()
        local = []
        for i in range(nw):
            cp = pltpu.make_async_copy(ins[i], outs[i].at[me], local_sems.at[i])
            cp.start()
            local.append(cp)
        sent = []
        for i in range(nw):
            for p, (px, py) in enumerate(peers):
                cp = _remote(ins[i].at[_half(rows[i], c)], outs[i].at[me, _half(rows[i], c)], ici_send, ici_recv,
                             3 * i + p, (px, py, c))
                cp.start()
                sent.append(cp)
        for p, (px, py) in enumerate(peers):
            for i in range(nw):
                blk = outs[i].at[2 * px + py, _half(rows[i], c)]
                _remote(blk, blk, ici_send, ici_recv, 3 * i + p, (px, py, c)).wait_recv()
                cp = _remote(blk, blk, d2d_send, d2d_recv, 3 * i + p, (x, y, 1 - c))
                cp.start()
                sent.append(cp)
        for p, (px, py) in enumerate(peers):
            for i in range(nw):
                blk = outs[i].at[2 * px + py, _half(rows[i], 1 - c)]
                _remote(blk, blk, d2d_send, d2d_recv, 3 * i + p, (x, y, 1 - c)).wait_recv()
        for cp in sent:
            cp.wait_send()
        for cp in local:
            cp.wait()

    dma = pltpu.SemaphoreType.DMA
    return pl.pallas_call(
        body, name="gather_weights", in_specs=[_ANY] * nw, out_specs=[_ANY] * nw,
        out_shape=[jax.ShapeDtypeStruct((N_CHIPS,) + s.shape, s.dtype) for s in shards],
        scratch_shapes=[dma((3 * nw,)), dma((3 * nw,)), dma((3 * nw,)), dma((3 * nw,)), dma((nw,))],
    )(*shards)


def _reduce_to_sibling(grads):
    nw = len(grads)

    def body(*refs):
        ins, outs = refs[:nw], refs[nw:2 * nw]
        send_sems, recv_sems = refs[2 * nw:]
        x, y, c, _, _ = _place()
        sent = []
        for i in range(nw):
            cp = _remote(ins[i].at[pl.ds(0, N_CHIPS), _half(grads[i].shape[1], 1 - c)], outs[i], send_sems, recv_sems,
                         i, (x, y, 1 - c))
            cp.start()
            sent.append(cp)
        for cp in sent:
            cp.wait()

    dma = pltpu.SemaphoreType.DMA
    return pl.pallas_call(
        body, name="reduce_grads_d2d", in_specs=[_ANY] * nw, out_specs=[_ANY] * nw,
        out_shape=[jax.ShapeDtypeStruct((N_CHIPS, g.shape[1] // 2, g.shape[2]), g.dtype) for g in grads],
        scratch_shapes=[dma((nw,)), dma((nw,))],
    )(*grads)


def _reduce_between_chips(pairs):
    nw = len(pairs)

    def body(*refs):
        ins, outs = refs[:nw], refs[nw:2 * nw]
        send_sems, recv_sems, local_sems = refs[2 * nw:]
        x, y, c, me, peers = _place()
        local = []
        for i in range(nw):
            cp = pltpu.make_async_copy(ins[i].at[me], outs[i].at[me], local_sems.at[i])
            cp.start()
            local.append(cp)
        sent = []
        for i in range(nw):
            for p, (px, py) in enumerate(peers):
                cp = _remote(ins[i].at[2 * px + py], outs[i].at[me], send_sems, recv_sems, 3 * i + p, (px, py, c))
                cp.start()
                sent.append(cp)
        for i in range(nw):
            for p, (px, py) in enumerate(peers):
                blk = outs[i].at[2 * px + py]
                _remote(blk, blk, send_sems, recv_sems, 3 * i + p, (px, py, c)).wait_recv()
        for cp in sent:
            cp.wait_send()
        for cp in local:
            cp.wait()

    dma = pltpu.SemaphoreType.DMA
    return pl.pallas_call(
        body, name="reduce_grads_ici", in_specs=[_ANY] * nw, out_specs=[_ANY] * nw,
        out_shape=[jax.ShapeDtypeStruct(p.shape, p.dtype) for p in pairs],
        scratch_shapes=[dma((3 * nw,)), dma((3 * nw,)), dma((nw,))],
    )(*pairs)


def _reduce_back(totals):
    nw = len(totals)

    def body(*refs):
        ins, outs = refs[:nw], refs[nw:2 * nw]
        send_sems, recv_sems, local_sems = refs[2 * nw:]
        x, y, c, _, _ = _place()
        moved = []
        for i in range(nw):
            rows = 2 * totals[i].shape[0]
            cp = pltpu.make_async_copy(ins[i], outs[i].at[_half(rows, c)], local_sems.at[i])
            cp.start()
            moved.append(cp)
            cp = _remote(ins[i], outs[i].at[_half(rows, c)], send_sems, recv_sems, i, (x, y, 1 - c))
            cp.start()
            moved.append(cp)
        for i in range(nw):
            blk = outs[i].at[_half(2 * totals[i].shape[0], 1 - c)]
            _remote(blk, blk, send_sems, recv_sems, i, (x, y, 1 - c)).wait_recv()
        for n, cp in enumerate(moved):
            if n % 2:
                cp.wait_send()
            else:
                cp.wait()

    dma = pltpu.SemaphoreType.DMA
    return pl.pallas_call(
        body, name="reduce_grads_back", in_specs=[_ANY] * nw, out_specs=[_ANY] * nw,
        out_shape=[jax.ShapeDtypeStruct((2 * t.shape[0], t.shape[1]), t.dtype) for t in totals],
        scratch_shapes=[dma((nw,)), dma((nw,)), dma((nw,))],
    )(*totals)


def _all_reduce_small(v, name):
    rows, w = v.shape

    def body(v_ref, out_ref, buf_ref, send_sems, recv_sems):
        x, y, c = lax.axis_index("x"), lax.axis_index("y"), lax.axis_index("c")
        me = 4 * x + 2 * y + c

        def flip(n):
            return (1 - x if n & 4 else x, 1 - y if n & 2 else y, 1 - c if n & 1 else c)

        sent = []
        for n in range(1, 8):
            cp = pltpu.make_async_remote_copy(src_ref=v_ref, dst_ref=buf_ref.at[me], send_sem=send_sems.at[n - 1],
                                              recv_sem=recv_sems.at[n - 1], device_id=flip(n), device_id_type=MESH)
            cp.start()
            sent.append(cp)
        buf_ref[me] = v_ref[...]
        for n in range(1, 8):
            px, py, pc = flip(n)
            pltpu.make_async_remote_copy(src_ref=v_ref, dst_ref=buf_ref.at[4 * px + 2 * py + pc],
                                         send_sem=send_sems.at[n - 1], recv_sem=recv_sems.at[n - 1],
                                         device_id=flip(n), device_id_type=MESH).wait_recv()
        for cp in sent:
            cp.wait_send()
        acc = buf_ref[0]
        for d in range(1, 8):
            acc = acc + buf_ref[d]
        out_ref[...] = acc

    vm = pl.BlockSpec(memory_space=pltpu.VMEM)
    return pl.pallas_call(
        body, name=name, in_specs=[vm], out_specs=vm,
        out_shape=jax.ShapeDtypeStruct((rows, w), F32),
        scratch_shapes=[pltpu.VMEM((8, rows, w), F32), pltpu.SemaphoreType.DMA((7,)), pltpu.SemaphoreType.DMA((7,))],
        compiler_params=pltpu.CompilerParams(vmem_limit_bytes=VMEM_LIMIT),
    )(v)


ELEMENTWISE_BLOCK = 256 * 1024


def _rows_tile(rows, cols):
    best = None
    for t in range(SUBLANES, rows + 1, SUBLANES):
        if rows % t == 0 and t * cols <= ELEMENTWISE_BLOCK:
            best = t
    return rows if best is None else best


def _add_pair(g, t, core, name):
    nb, n, w = t.shape
    tr = _rows_tile(n, w)
    steps = n // tr

    def body(core_ref, g_ref, t_ref, o_ref):
        o_ref[...] = g_ref[...] + t_ref[...]

    spec = pl.BlockSpec((1, tr, w), lambda j, i, core_ref: (j, i, 0))
    return pl.pallas_call(
        body, name=name,
        grid_spec=pltpu.PrefetchScalarGridSpec(
            num_scalar_prefetch=1, grid=(nb, steps),
            in_specs=[pl.BlockSpec((1, tr, w), lambda j, i, core_ref: (j, core_ref[0] * steps + i, 0)), spec],
            out_specs=spec),
        out_shape=jax.ShapeDtypeStruct(t.shape, F32),
        compiler_params=_params(("parallel", "parallel")))(core, g, t)


def _add_chips(r, name):
    nb, n, w = r.shape
    tr = _rows_tile(n, w)

    def body(r_ref, o_ref):
        o_ref[...] = ((r_ref[0] + r_ref[1]) + r_ref[2]) + r_ref[3]

    return pl.pallas_call(body, name=name, grid=(n // tr,),
                          in_specs=[pl.BlockSpec((nb, tr, w), lambda i: (0, i, 0))],
                          out_specs=pl.BlockSpec((tr, w), lambda i: (i, 0)),
                          out_shape=jax.ShapeDtypeStruct((n, w), F32),
                          compiler_params=_params(("parallel",)))(r)


def _adamw(w, g, m, v, name):
    rows, wd = w.shape
    tr = _rows_tile(rows, wd)
    c1 = 1.0 - ADAM_B1 ** ADAM_STEP
    c2 = 1.0 - ADAM_B2 ** ADAM_STEP

    def body(w_ref, g_ref, m_ref, v_ref, d_ref, mo_ref, vo_ref):
        gv = g_ref[...]
        m2 = ADAM_B1 * m_ref[...] + (1.0 - ADAM_B1) * gv
        v2 = ADAM_B2 * v_ref[...] + (1.0 - ADAM_B2) * (gv * gv)
        mo_ref[...] = m2
        vo_ref[...] = v2
        d_ref[...] = -ADAM_LR * ((m2 / c1) / (jnp.sqrt(v2 / c2) + ADAM_EPS) + ADAM_WD * w_ref[...])

    spec = pl.BlockSpec((tr, wd), lambda i: (i, 0))
    shp = jax.ShapeDtypeStruct((rows, wd), F32)
    return pl.pallas_call(body, name=name, grid=(rows // tr,), in_specs=[spec] * 4, out_specs=[spec] * 3,
                          out_shape=[shp] * 3, compiler_params=_params(("parallel",)))(w, g, m, v)


BIG = [("w_in", (1024, 3232), 1), ("w_uq", (384, 768), 1), ("w_uk", (256, 512), 1), ("w_uv", (256, 512), 1),
       ("w_glu", (512, 512), 0), ("w_branch_attn", (512, 1024), 1), ("w_branch_ssm", (512, 1024), 1),
       ("w_out", (1024, 1024), 0), ("w_up", (1024, 5632), 1), ("conv_w", (3, 5632), 1), ("w_down", (2816, 1024), 0)]
SMALL = [("mix_norm_pre", (1024,)), ("q_norm", (384,)), ("kv_norm", (256,)), ("ssm_lambda_re", (32, 64)),
         ("ssm_lambda_im", (32, 64)), ("ssm_log_dt", (32,)), ("ssm_b_re", (32, 64, 16)), ("ssm_b_im", (32, 64, 16)),
         ("ssm_c_re", (32, 16, 64)), ("ssm_c_im", (32, 16, 64)), ("ssm_d", (32, 16)), ("b_glu", (512,)),
         ("b_gate", (2048,)), ("mix_norm_post", (1024,)), ("ffn_norm_pre", (1024,)), ("conv_b", (5632,)),
         ("ffn_norm_post", (1024,))]
MATMUL_W = [b for b in BIG if b[0] != "conv_w"]
CONV_W_SHAPE = (3, 2 * D_FF)
CONV_W_SHARD = (3, 2 * D_FF // N_CHIPS)
SMALL_SUM = [("loss", (1,))] + SMALL + [("conv_w", CONV_W_SHAPE)]
SMALL_ADAM = SMALL + [("conv_w", CONV_W_SHARD)]


def _pack_flat(layout, vals):
    flat = jnp.concatenate([vals[n].astype(F32).reshape(-1) for n, _ in layout])
    rows = -(-(-(-flat.shape[0] // FLAT_W)) // SUBLANES) * SUBLANES
    return jnp.pad(flat, (0, rows * FLAT_W - flat.shape[0])).reshape(rows, FLAT_W)


def _unpack_flat(layout, flat):
    flat = flat.reshape(-1)
    out = {}
    o = 0
    for name, shape in layout:
        n = math.prod(shape)
        out[name] = flat[o:o + n].reshape(shape)
        o += n
    return out


_ARG_NAMES = ["x", "positions"] + [n for n in (
    "mix_norm_pre", "w_in", "q_norm", "w_uq", "kv_norm", "w_uk", "w_uv", "ssm_lambda_re", "ssm_lambda_im", "ssm_log_dt",
    "ssm_b_re", "ssm_b_im", "ssm_c_re", "ssm_c_im", "ssm_d", "w_glu", "b_glu", "w_branch_attn", "w_branch_ssm",
    "b_gate", "w_out", "mix_norm_post", "ffn_norm_pre", "w_up", "conv_w", "conv_b", "w_down", "ffn_norm_post")]
_WEIGHTS = _ARG_NAMES[2:]


def _gather_weights(w):
    c = lax.axis_index("c")
    gathered = _gather_big([w[name].astype(BF16) for name, _, _ in MATMUL_W])
    full = {name: _from_chip_major(g, axis) for (name, _, axis), g in zip(MATMUL_W, gathered)}

    chip = 2 * lax.axis_index("x") + lax.axis_index("y")
    ncol = 2 * D_FF // N_CHIPS
    placed = lax.dynamic_update_slice_in_dim(jnp.zeros((3, 2 * D_FF), F32), w["conv_w"], chip * ncol, axis=1)
    placed = placed * jnp.where(c == 0, 1.0, 0.0)
    cw_rows = -(-(-(-3 * 2 * D_FF // FLAT_W)) // SUBLANES) * SUBLANES
    placed = jnp.pad(placed.reshape(-1), (0, cw_rows * FLAT_W - 3 * 2 * D_FF)).reshape(cw_rows, FLAT_W)
    conv_w_full = _all_reduce_small(placed, "gather_conv_w").reshape(-1)[:3 * 2 * D_FF].reshape(3, 2 * D_FF)
    return full, conv_w_full


def _reduce_grads(gbig, loss, gsmall):
    core = lax.axis_index("c").astype(jnp.int32).reshape(1)
    names = [name for name, _, _ in MATMUL_W]
    grads = [gbig[n] for n in names]
    theirs = _reduce_to_sibling(grads)
    pairs = [_add_pair(g, t, core, "reduce_pair_" + n) for n, g, t in zip(names, grads, theirs)]
    landed = _reduce_between_chips(pairs)
    totals = [_add_chips(r, "reduce_chips_" + n) for n, r in zip(names, landed)]
    g_red = dict(zip(names, _reduce_back(totals)))

    vals = dict(gsmall)
    vals["loss"] = loss
    small_red = _unpack_flat(SMALL_SUM, _all_reduce_small(_pack_flat(SMALL_SUM, vals), "reduce_small"))
    return g_red, small_red


def _step(args):
    x = args["x"][0]
    positions = args["positions"][0]
    tgt = args["loss_target"][0]
    w = {n: args[n][0] for n in _WEIGHTS}
    m = {n: args["m_" + n][0] for n in _WEIGHTS}
    v = {n: args["v_" + n][0] for n in _WEIGHTS}

    full, conv_w_full = _gather_weights(w)
    sp = {n: w[n].reshape(s) for n, s in SMALL}
    for n in ("mix_norm_pre", "q_norm", "kv_norm", "b_glu", "b_gate", "mix_norm_post", "ffn_norm_pre", "conv_b",
              "ffn_norm_post"):
        sp[n] = sp[n].reshape(1, -1)
    sp["conv_w"] = conv_w_full
    loss, gx, gbig, gsmall = _local_step(x, positions, tgt, full, sp)
    g_red, small_red = _reduce_grads(gbig, loss, gsmall)

    chip = 2 * lax.axis_index("x") + lax.axis_index("y")
    grads = dict(small_red)
    grads["conv_w"] = lax.dynamic_slice_in_dim(small_red["conv_w"], chip * CONV_W_SHARD[1], CONV_W_SHARD[1], axis=1)
    grads.update(g_red)

    outs = {"grad_" + n: grads[n] for n in _WEIGHTS}
    for name, _, _ in MATMUL_W:
        d, m2, v2 = _adamw(w[name], grads[name], m[name], v[name], "adamw_" + name)
        outs["delta_" + name], outs["new_m_" + name], outs["new_v_" + name] = d, m2, v2
    d_sm, m_sm, v_sm = _adamw(_pack_flat(SMALL_ADAM, w), _pack_flat(SMALL_ADAM, grads), _pack_flat(SMALL_ADAM, m),
                              _pack_flat(SMALL_ADAM, v), "adamw_small")
    for prefix, flat in (("delta_", d_sm), ("new_m_", m_sm), ("new_v_", v_sm)):
        for n, val in _unpack_flat(SMALL_ADAM, flat).items():
            outs[prefix + n] = val
    outs = {n: val.reshape(args[n.split("_", 1)[1] if not n.startswith("new_") else n[6:]].shape)
            for n, val in outs.items()}
    res = [small_red["loss"][0], gx[None]]
    for prefix in ("grad_", "delta_", "new_m_", "new_v_"):
        res += [outs[prefix + n] for n in _WEIGHTS]
    return tuple(res)


def kernel(x, positions, mix_norm_pre, w_in, q_norm, w_uq, kv_norm, w_uk, w_uv, ssm_lambda_re, ssm_lambda_im, ssm_log_dt, ssm_b_re, ssm_b_im, ssm_c_re, ssm_c_im, ssm_d, w_glu, b_glu, w_branch_attn, w_branch_ssm, b_gate, w_out, mix_norm_post, ffn_norm_pre, w_up, conv_w, conv_b, w_down, ffn_norm_post, loss_target, m_mix_norm_pre, m_w_in, m_q_norm, m_w_uq, m_kv_norm, m_w_uk, m_w_uv, m_ssm_lambda_re, m_ssm_lambda_im, m_ssm_log_dt, m_ssm_b_re, m_ssm_b_im, m_ssm_c_re, m_ssm_c_im, m_ssm_d, m_w_glu, m_b_glu, m_w_branch_attn, m_w_branch_ssm, m_b_gate, m_w_out, m_mix_norm_post, m_ffn_norm_pre, m_w_up, m_conv_w, m_conv_b, m_w_down, m_ffn_norm_post, v_mix_norm_pre, v_w_in, v_q_norm, v_w_uq, v_kv_norm, v_w_uk, v_w_uv, v_ssm_lambda_re, v_ssm_lambda_im, v_ssm_log_dt, v_ssm_b_re, v_ssm_b_im, v_ssm_c_re, v_ssm_c_im, v_ssm_d, v_w_glu, v_b_glu, v_w_branch_attn, v_w_branch_ssm, v_b_gate, v_w_out, v_mix_norm_post, v_ffn_norm_pre, v_w_up, v_conv_w, v_conv_b, v_w_down, v_ffn_norm_post):
    given = dict(locals())
    return _step(given)
```

```python
import math

import jax
import jax.numpy as jnp
from jax import lax
from jax.experimental import pallas as pl
from jax.experimental.pallas import tpu as pltpu

F32 = jnp.float32
BF16 = jnp.bfloat16
MESH = pl.DeviceIdType.MESH

D_MODEL = 1024
N_HEADS = 8
QK_NOPE = 64
QK_ROPE = 32
QK_HEAD = QK_NOPE + QK_ROPE
V_HEAD = 64
Q_RANK = 384
KV_RANK = 256
ROPE_THETA = 10000.0
SSM_W = 512
SSM_H = 16
SSM_G = 32
SSM_P = 64
SSM_CH = SSM_G * SSM_P
D_FF = 2816
EPS = 1e-6
ADAM_LR = 0.001
ADAM_B1 = 0.9
ADAM_B2 = 0.999
ADAM_EPS = 1e-08
ADAM_WD = 0.01
ADAM_STEP = 10

LANES = 128
SUBLANES = 8
VMEM_LIMIT = 56 * 1024 * 1024

HEAD_SLOT = LANES
HP = N_HEADS * HEAD_SLOT
P_CQ, P_CKV, P_KR, P_U, P_GL, P_END = 0, 384, 640, 768, 1280, 3328
KR_LANE = 64

FLAT_W = 1024
N_CHIPS = 4


def _tile(n, cap):
    if n <= cap:
        return n
    best = None
    for t in range(LANES, cap + 1, LANES):
        if n % t == 0:
            best = t
    assert best is not None, (n, cap)
    return best


def _params(sem):
    return pltpu.CompilerParams(dimension_semantics=sem, vmem_limit_bytes=VMEM_LIMIT)


def _dot(a, b):
    return jnp.dot(a, b, preferred_element_type=F32)


def _dot_nt(a, b):
    return lax.dot_general(a, b, (((1,), (1,)), ((), ())), preferred_element_type=F32)


def _dot_tn(a, b):
    return lax.dot_general(a, b, (((0,), (0,)), ((), ())), preferred_element_type=F32)


def _rms(x, g):
    r = lax.rsqrt(jnp.mean(x * x, axis=-1, keepdims=True) + EPS)
    return x * r * g, r


def _rms_bwd(dy, x, g):
    r = lax.rsqrt(jnp.mean(x * x, axis=-1, keepdims=True) + EPS)
    dyg = dy * g
    dx = r * dyg - x * (r * r * r) * jnp.mean(dyg * x, axis=-1, keepdims=True)
    dg = jnp.sum(dy * x * r, axis=0, keepdims=True)
    return dx, dg


_GELU_K0 = math.sqrt(2.0 / math.pi)
_GELU_K1 = 0.044715


def _gelu(x):
    th = jnp.tanh(_GELU_K0 * (x + _GELU_K1 * x * x * x))
    return 0.5 * x * (1.0 + th)


def _gelu_grad(x):
    th = jnp.tanh(_GELU_K0 * (x + _GELU_K1 * x * x * x))
    return 0.5 * (1.0 + th) + 0.5 * x * (1.0 - th * th) * _GELU_K0 * (1.0 + 3.0 * _GELU_K1 * x * x)


def _sigmoid(x):
    return 1.0 / (1.0 + jnp.exp(-x))


def _rope(q, c, s):
    n = q.shape[1]
    lane = lax.broadcasted_iota(jnp.int32, q.shape, 1) % HEAD_SLOT
    sw = jnp.where(lane < KR_LANE + QK_ROPE // 2, pltpu.roll(q, n - QK_ROPE // 2, 1), pltpu.roll(q, QK_ROPE // 2, 1))
    return q * c + sw * s


def _rope_bwd(dy, c, s):
    n = dy.shape[1]
    t = dy * s
    lane = lax.broadcasted_iota(jnp.int32, dy.shape, 1) % HEAD_SLOT
    sw = jnp.where(lane < KR_LANE + QK_ROPE // 2, pltpu.roll(t, n - QK_ROPE // 2, 1), pltpu.roll(t, QK_ROPE // 2, 1))
    rope_lane = jnp.logical_and(lane >= KR_LANE, lane < KR_LANE + QK_ROPE)
    return dy * c + jnp.where(rope_lane, sw, 0.0)


def _shift_down(x, k, halo):
    xs = pltpu.roll(x, k, 0)
    hs = pltpu.roll(halo, k, 0)
    rows = lax.broadcasted_iota(jnp.int32, halo.shape, 0)
    top = jnp.where(rows < k, hs, xs[0:SUBLANES])
    return jnp.concatenate([top, xs[SUBLANES:]], axis=0)


def _shift_up(x, k, halo):
    t = x.shape[0]
    xs = pltpu.roll(x, t - k, 0)
    hs = pltpu.roll(halo, SUBLANES - k, 0)
    rows = lax.broadcasted_iota(jnp.int32, halo.shape, 0)
    bot = jnp.where(rows >= SUBLANES - k, hs, xs[t - SUBLANES:])
    return jnp.concatenate([xs[:t - SUBLANES], bot], axis=0)


def _mm(a, b, name, out_dtype=F32, bt=False, b_col0=0, n=None, tm_cap=512, tn_cap=1408, a_lead=None):
    m, k = a.shape[-2:]
    if bt:
        n_full = b.shape[0]
        n = n_full
    else:
        n = b.shape[1] if n is None else n
    tm = min(tm_cap, m)
    tn = _tile(n, tn_cap)

    def body(a_ref, b_ref, o_ref):
        if bt:
            o_ref[...] = _dot_nt(a_ref[...], b_ref[...]).astype(out_dtype)
        else:
            o_ref[...] = _dot(a_ref[...], b_ref[...]).astype(out_dtype)

    if bt:
        b_spec = pl.BlockSpec((tn, k), lambda j, i: (j, b_col0))
    else:
        off = b_col0 * (n // tn)
        b_spec = pl.BlockSpec((k, tn), lambda j, i: (0, off + j))
    if a_lead is None:
        a_spec = pl.BlockSpec((tm, k), lambda j, i: (i, 0))
    else:
        a_spec = pl.BlockSpec((None, tm, k), lambda j, i: (a_lead, i, 0))
    return pl.pallas_call(
        body, name=name, grid=(n // tn, m // tm),
        in_specs=[a_spec, b_spec],
        out_specs=pl.BlockSpec((tm, tn), lambda j, i: (i, j)),
        out_shape=jax.ShapeDtypeStruct((m, n), out_dtype),
        compiler_params=_params(("parallel", "parallel")),
    )(a, b)


def _mm_tn(a, b, name, tk_cap=512, tn_cap=1664, tl_cap=1024, chips=False):
    l, k = a.shape
    tk = _tile(k, tk_cap)
    tl = min(tl_cap, l)

    def body(a_ref, b_ref, o_ref):
        @pl.when(pl.program_id(2) == 0)
        def _():
            o_ref[...] = jnp.zeros_like(o_ref)

        o_ref[...] += _dot_tn(a_ref[...], b_ref[...])

    if chips:
        n = b.shape[-1] * (b.shape[0] if b.ndim == 3 else 1)
        tn = n // N_CHIPS
        assert tn % LANES == 0
        if b.ndim == 3:
            per = N_CHIPS // b.shape[0]
            b_spec = pl.BlockSpec((None, tl, tn), lambda i, j, r: (j // per, r, j % per))
        else:
            b_spec = pl.BlockSpec((tl, tn), lambda i, j, r: (r, j))
        out_spec = pl.BlockSpec((None, tk, tn), lambda i, j, r: (j, i, 0))
        out_shape = jax.ShapeDtypeStruct((N_CHIPS, k, tn), F32)
    else:
        n = b.shape[1]
        tn = _tile(n, tn_cap)
        b_spec = pl.BlockSpec((tl, tn), lambda i, j, r: (r, j))
        out_spec = pl.BlockSpec((tk, tn), lambda i, j, r: (i, j))
        out_shape = jax.ShapeDtypeStruct((k, n), F32)
    return pl.pallas_call(
        body, name=name, grid=(k // tk, n // tn, l // tl),
        in_specs=[pl.BlockSpec((tl, tk), lambda i, j, r: (r, i)), b_spec],
        out_specs=out_spec, out_shape=out_shape,
        compiler_params=_params(("parallel", "parallel", "arbitrary")),
    )(a, b)


def _row(tl, n):
    return pl.BlockSpec((tl, n), lambda i: (i, 0))


def _const(shape):
    return pl.BlockSpec(shape, lambda i: tuple(0 for _ in shape))


def _proj_fwd(x, g1, win, gq, wuq, gkv, wukv, rc, rs, bg, tl):
    l = x.shape[0]

    def body(x_ref, g1_ref, win_ref, gq_ref, wuq_ref, gkv_ref, wukv_ref, rc_ref, rs_ref, bg_ref,
             hn_ref, cq_ref, ckv_ref, q_ref, k_ref, v_ref, u_ref, gl_ref):
        hn, _ = _rms(x_ref[...], g1_ref[...])
        hnb = hn.astype(BF16)
        hn_ref[...] = hnb
        proj = _dot(hnb, win_ref[...])
        cq = proj[:, P_CQ:P_CKV]
        ckv = proj[:, P_CKV:P_KR]
        kr = proj[:, P_KR:P_U]
        cq_ref[...] = cq
        ckv_ref[...] = ckv
        u_ref[...] = proj[:, P_U:P_GL]
        gl_ref[...] = proj[:, P_GL:P_END] + bg_ref[...]
        qn, _ = _rms(cq, gq_ref[...])
        q = _dot(qn.astype(BF16), wuq_ref[...])
        c1 = rc_ref[...]
        s1 = rs_ref[...]
        q_ref[...] = (_rope(q, jnp.tile(c1, (1, N_HEADS)), jnp.tile(s1, (1, N_HEADS))) * Q_PRESCALE).astype(BF16)
        ckvn, _ = _rms(ckv, gkv_ref[...])
        kv = _dot(ckvn.astype(BF16), wukv_ref[...])
        krr = _rope(kr, c1, s1)
        k_ref[...] = (kv[:, :HP] + jnp.tile(krr, (1, N_HEADS))).astype(BF16)
        v_ref[...] = kv[:, HP:].astype(BF16)

    outs = [(D_MODEL, BF16), (Q_RANK, F32), (KV_RANK, F32), (HP, BF16), (HP, BF16), (HP, BF16),
            (SSM_W, F32), (2 * D_MODEL, F32)]
    return pl.pallas_call(
        body, name="proj_fwd", grid=(l // tl,),
        in_specs=[_row(tl, D_MODEL), _const((1, D_MODEL)), _const((D_MODEL, P_END)), _const((1, Q_RANK)),
                  _const((Q_RANK, HP)), _const((1, KV_RANK)), _const((KV_RANK, 2 * HP)),
                  _row(tl, HEAD_SLOT), _row(tl, HEAD_SLOT), _const((1, 2 * D_MODEL))],
        out_specs=[_row(tl, n) for n, _ in outs],
        out_shape=[jax.ShapeDtypeStruct((l, n), dt) for n, dt in outs],
        compiler_params=_params(("parallel",)),
    )(x, g1, win, gq, wuq, gkv, wukv, rc, rs, bg)


_NEG = -1e30


LOG2E = 1.0 / math.log(2.0)
LN2 = math.log(2.0)
ATTN_SCALE = 1.0 / math.sqrt(QK_HEAD)
Q_PRESCALE = ATTN_SCALE * LOG2E
HEADS_PER_STEP = 2
PAIR_W = HEADS_PER_STEP * HEAD_SLOT


def _causal_pairs(nq, by_query):
    if by_query:
        pairs = [(i, j) for i in range(nq) for j in range(i + 1)]
    else:
        pairs = [(i, j) for j in range(nq) for i in range(j, nq)]
    return jnp.array([p[0] for p in pairs], jnp.int32), jnp.array([p[1] for p in pairs], jnp.int32)


def _diag_mask(s):
    rows = lax.broadcasted_iota(jnp.int32, s.shape, 0)
    cols = lax.broadcasted_iota(jnp.int32, s.shape, 1)
    return jnp.where(cols <= rows, s, _NEG)


def _attn_fwd(q, k, v, tq):
    l = q.shape[0]
    nq = l // tq
    it, jt = _causal_pairs(nq, True)

    def body(it_ref, jt_ref, q_ref, k_ref, v_ref, o_ref, lse_ref, m_ref, l_ref, acc_ref):
        t = pl.program_id(1)
        i = it_ref[t]
        j = jt_ref[t]

        @pl.when(j == 0)
        def _():
            m_ref[...] = jnp.full_like(m_ref, _NEG)
            l_ref[...] = jnp.zeros_like(l_ref)
            acc_ref[...] = jnp.zeros_like(acc_ref)

        def update(on_diagonal):
            for hh in range(HEADS_PER_STEP):
                sl = slice(hh * HEAD_SLOT, (hh + 1) * HEAD_SLOT)
                s = _dot_nt(q_ref[:, sl], k_ref[:, sl])
                if on_diagonal:
                    s = _diag_mask(s)
                m_old = m_ref[hh]
                m_new = jnp.maximum(m_old, jnp.max(s, axis=-1, keepdims=True))
                p = jnp.exp2(s - m_new)
                alpha = jnp.exp2(m_old - m_new)
                l_ref[hh] = alpha * l_ref[hh] + jnp.sum(p, axis=-1, keepdims=True)
                acc_ref[:, sl] = alpha * acc_ref[:, sl] + _dot(p.astype(BF16), v_ref[:, sl])
                m_ref[hh] = m_new

        @pl.when(j < i)
        def _():
            update(False)

        @pl.when(j == i)
        def _():
            update(True)
            for hh in range(HEADS_PER_STEP):
                sl = slice(hh * HEAD_SLOT, (hh + 1) * HEAD_SLOT)
                o_ref[:, sl] = (acc_ref[:, sl] / l_ref[hh]).astype(BF16)
                lse_ref[:, sl] = jnp.broadcast_to(m_ref[hh] + jnp.log(l_ref[hh]) * LOG2E, (tq, HEAD_SLOT))

    blk = (tq, PAIR_W)
    qmap = lambda h, t, it_ref, jt_ref: (it_ref[t], h)
    kmap = lambda h, t, it_ref, jt_ref: (jt_ref[t], h)
    return pl.pallas_call(
        body, name="attn_fwd",
        grid_spec=pltpu.PrefetchScalarGridSpec(
            num_scalar_prefetch=2, grid=(N_HEADS // HEADS_PER_STEP, it.shape[0]),
            in_specs=[pl.BlockSpec(blk, qmap), pl.BlockSpec(blk, kmap), pl.BlockSpec(blk, kmap)],
            out_specs=[pl.BlockSpec(blk, qmap), pl.BlockSpec(blk, qmap)],
            scratch_shapes=[pltpu.VMEM((HEADS_PER_STEP, tq, 1), F32), pltpu.VMEM((HEADS_PER_STEP, tq, 1), F32),
                            pltpu.VMEM((tq, PAIR_W), F32)]),
        out_shape=[jax.ShapeDtypeStruct((l, HP), BF16), jax.ShapeDtypeStruct((l, HP), F32)],
        compiler_params=_params(("parallel", "arbitrary")),
    )(it, jt, q, k, v)


def _attn_bwd(q, k, v, o, do, lse, tq):
    l = q.shape[0]
    nq = l // tq
    it, jt = _causal_pairs(nq, False)

    def body(it_ref, jt_ref, q_ref, k_ref, v_ref, o_ref, do_ref, lse_ref, dq_ref, dk_ref, dv_ref, dka_ref, dva_ref):
        t = pl.program_id(1)
        i = it_ref[t]
        j = jt_ref[t]

        @pl.when(t == 0)
        def _():
            dq_ref[...] = jnp.zeros_like(dq_ref)

        @pl.when(i == j)
        def _():
            dka_ref[...] = jnp.zeros_like(dka_ref)
            dva_ref[...] = jnp.zeros_like(dva_ref)

        def update(on_diagonal):
            r0 = pl.multiple_of(i * tq, tq)
            for hh in range(HEADS_PER_STEP):
                sl = slice(hh * HEAD_SLOT, (hh + 1) * HEAD_SLOT)
                qb = q_ref[:, sl]
                kb = k_ref[:, sl]
                dob = do_ref[:, sl]
                s = _dot_nt(qb, kb)
                if on_diagonal:
                    s = _diag_mask(s)
                p = jnp.exp2(s - lse_ref[:, hh * HEAD_SLOT:hh * HEAD_SLOT + 1])
                dva_ref[:, sl] += _dot_tn(p.astype(BF16), dob)
                dp = _dot_nt(dob, v_ref[:, sl])
                delta = jnp.sum(dob.astype(F32) * o_ref[:, sl].astype(F32), axis=-1, keepdims=True)
                ds = (p * (dp - delta)).astype(BF16)
                dka_ref[:, sl] += _dot_tn(ds, qb)
                dq_ref[pl.ds(r0, tq), sl] += ATTN_SCALE * _dot(ds, kb)

        @pl.when(j < i)
        def _():
            update(False)

        @pl.when(j == i)
        def _():
            update(True)

        @pl.when(i == nq - 1)
        def _():
            dk_ref[...] = (dka_ref[...] * LN2).astype(BF16)
            dv_ref[...] = dva_ref[...].astype(BF16)

    blk = (tq, PAIR_W)
    qmap = lambda h, t, it_ref, jt_ref: (it_ref[t], h)
    kmap = lambda h, t, it_ref, jt_ref: (jt_ref[t], h)
    return pl.pallas_call(
        body, name="attn_bwd",
        grid_spec=pltpu.PrefetchScalarGridSpec(
            num_scalar_prefetch=2, grid=(N_HEADS // HEADS_PER_STEP, it.shape[0]),
            in_specs=[pl.BlockSpec(blk, qmap), pl.BlockSpec(blk, kmap), pl.BlockSpec(blk, kmap),
                      pl.BlockSpec(blk, qmap), pl.BlockSpec(blk, qmap), pl.BlockSpec(blk, qmap)],
            out_specs=[pl.BlockSpec((l, PAIR_W), lambda h, t, it_ref, jt_ref: (0, h)), pl.BlockSpec(blk, kmap),
                       pl.BlockSpec(blk, kmap)],
            scratch_shapes=[pltpu.VMEM(blk, F32), pltpu.VMEM(blk, F32)]),
        out_shape=[jax.ShapeDtypeStruct((l, HP), F32), jax.ShapeDtypeStruct((l, HP), BF16),
                   jax.ShapeDtypeStruct((l, HP), BF16)],
        compiler_params=_params(("parallel", "arbitrary")),
    )(it, jt, q, k, v, o, do, lse)


SSM_CB = 512
SSM_UB = 128
SSM_NB = SSM_CH // SSM_CB


def _scan_tiles(re_ref, im_ref, tab, carry, n_tiles, reverse):
    def tile(n, c):
        cr, ci = c
        idx = (n_tiles - 1 - n) if reverse else n
        r0 = pl.multiple_of(idx * SUBLANES, SUBLANES)
        sr = re_ref[pl.ds(r0, SUBLANES), :]
        si = im_ref[pl.ds(r0, SUBLANES), :]
        for step, k in enumerate((1, 2, 4)):
            mr, mi = tab[2 * step], tab[2 * step + 1]
            sh = (SUBLANES - k) if reverse else k
            rr = pltpu.roll(sr, sh, 0)
            ri = pltpu.roll(si, sh, 0)
            sr, si = sr + mr * rr - mi * ri, si + mr * ri + mi * rr
        pr, pi = tab[6], tab[7]
        sr, si = sr + pr * cr - pi * ci, si + pr * ci + pi * cr
        re_ref[pl.ds(r0, SUBLANES), :] = sr
        im_ref[pl.ds(r0, SUBLANES), :] = si
        if reverse:
            return sr[0:1, :], si[0:1, :]
        return sr[SUBLANES - 1:SUBLANES, :], si[SUBLANES - 1:SUBLANES, :]

    return lax.fori_loop(0, n_tiles, tile, carry, unroll=2)


def _ssm_fwd(u, bre, bim, cre, cim, dvec, tab, tt):
    l = u.shape[0]
    nt = l // tt

    def body(u_ref, bre_ref, bim_ref, cre_ref, cim_ref, d_ref, tab_ref, y_ref, sre_ref, sim_ref, car_ref):
        @pl.when(pl.program_id(1) == 0)
        def _():
            car_ref[...] = jnp.zeros_like(car_ref)

        uf = u_ref[...]
        ub = uf.astype(BF16)
        sre_ref[...] = _dot(ub, bre_ref[0])
        sim_ref[...] = _dot(ub, bim_ref[0])
        tab_v = [tab_ref[n] for n in range(8)]
        cr, ci = _scan_tiles(sre_ref, sim_ref, tab_v, (car_ref[0:1, :], car_ref[8:9, :]), tt // SUBLANES, False)
        car_ref[0:1, :] = cr
        car_ref[8:9, :] = ci
        y_ref[...] = (_dot(sre_ref[...].astype(BF16), cre_ref[0]) - _dot(sim_ref[...].astype(BF16), cim_ref[0])
                      + d_ref[...] * uf)

    return pl.pallas_call(
        body, name="ssm_fwd", grid=(SSM_NB, nt),
        in_specs=[pl.BlockSpec((tt, SSM_UB), lambda m, t: (t, m)),
                  pl.BlockSpec((1, SSM_UB, SSM_CB), lambda m, t: (m, 0, 0)),
                  pl.BlockSpec((1, SSM_UB, SSM_CB), lambda m, t: (m, 0, 0)),
                  pl.BlockSpec((1, SSM_CB, SSM_UB), lambda m, t: (m, 0, 0)),
                  pl.BlockSpec((1, SSM_CB, SSM_UB), lambda m, t: (m, 0, 0)),
                  pl.BlockSpec((1, SSM_UB), lambda m, t: (0, m)),
                  pl.BlockSpec((8, SUBLANES, SSM_CB), lambda m, t: (0, 0, m))],
        out_specs=[pl.BlockSpec((tt, SSM_UB), lambda m, t: (t, m)),
                   pl.BlockSpec((tt, SSM_CB), lambda m, t: (t, m)),
                   pl.BlockSpec((tt, SSM_CB), lambda m, t: (t, m))],
        out_shape=[jax.ShapeDtypeStruct((l, SSM_W), F32), jax.ShapeDtypeStruct((l, SSM_CH), F32),
                   jax.ShapeDtypeStruct((l, SSM_CH), F32)],
        scratch_shapes=[pltpu.VMEM((2 * SUBLANES, SSM_CB), F32)],
        compiler_params=_params(("parallel", "arbitrary")),
    )(u, bre, bim, cre, cim, dvec, tab)


def _ssm_bwd(dy, u, sre, sim, bre, bim, cre, cim, dvec, tab, tt):
    l = u.shape[0]
    nt = l // tt
    tpb = tt // SUBLANES

    def body(dy_ref, u_ref, sre_ref, sim_ref, hre_ref, him_ref, bre_ref, bim_ref, cre_ref, cim_ref, d_ref, tab_ref,
             du_ref, dbre_ref, dbim_ref, dcre_ref, dcim_ref, dare_ref, daim_ref, dd_ref, lr_ref, li_ref, car_ref):
        t = pl.program_id(1)

        @pl.when(t == 0)
        def _():
            car_ref[...] = jnp.zeros_like(car_ref)
            for ref in (dbre_ref, dbim_ref, dcre_ref, dcim_ref, dare_ref, daim_ref, dd_ref):
                ref[...] = jnp.zeros_like(ref)

        dyf = dy_ref[...]
        dyb = dyf.astype(BF16)
        uf = u_ref[...]
        s_re = sre_ref[...]
        s_im = sim_ref[...]
        lr_ref[...] = _dot_nt(dyb, cre_ref[0])
        li_ref[...] = -_dot_nt(dyb, cim_ref[0])
        dcre_ref[0] += _dot_tn(s_re.astype(BF16), dyb)
        dcim_ref[0] -= _dot_tn(s_im.astype(BF16), dyb)
        tab_v = [tab_ref[n] for n in range(8)]
        cr, ci = _scan_tiles(lr_ref, li_ref, tab_v, (car_ref[0:1, :], car_ref[8:9, :]), tpb, True)
        car_ref[0:1, :] = cr
        car_ref[8:9, :] = ci
        lam_r = lr_ref[...]
        lam_i = li_ref[...]
        keep = jnp.where(t == nt - 1, 0.0, 1.0)
        sp_r = _shift_down(s_re, 1, hre_ref[...] * keep)
        sp_i = _shift_down(s_im, 1, him_ref[...] * keep)
        dare_ref[...] += jnp.sum(lam_r * sp_r + lam_i * sp_i, axis=0, keepdims=True)
        daim_ref[...] += jnp.sum(lam_i * sp_r - lam_r * sp_i, axis=0, keepdims=True)
        lrb = lam_r.astype(BF16)
        lib = lam_i.astype(BF16)
        du_ref[...] = _dot_nt(lrb, bre_ref[0]) + _dot_nt(lib, bim_ref[0]) + dyf * d_ref[...]
        ub = uf.astype(BF16)
        dbre_ref[0] += _dot_tn(ub, lrb)
        dbim_ref[0] += _dot_tn(ub, lib)
        dd_ref[...] += jnp.sum(dyf * uf, axis=0, keepdims=True)

    rev = lambda m, t: (nt - 1 - t, m)
    halo = lambda m, t: (jnp.maximum((nt - 1 - t) * tpb - 1, 0), m)
    wb = pl.BlockSpec((1, SSM_UB, SSM_CB), lambda m, t: (m, 0, 0))
    wc = pl.BlockSpec((1, SSM_CB, SSM_UB), lambda m, t: (m, 0, 0))
    vec_c = pl.BlockSpec((1, SSM_CB), lambda m, t: (0, m))
    vec_u = pl.BlockSpec((1, SSM_UB), lambda m, t: (0, m))
    return pl.pallas_call(
        body, name="ssm_bwd", grid=(SSM_NB, nt),
        in_specs=[pl.BlockSpec((tt, SSM_UB), rev), pl.BlockSpec((tt, SSM_UB), rev),
                  pl.BlockSpec((tt, SSM_CB), rev), pl.BlockSpec((tt, SSM_CB), rev),
                  pl.BlockSpec((SUBLANES, SSM_CB), halo), pl.BlockSpec((SUBLANES, SSM_CB), halo),
                  wb, wb, wc, wc, vec_u,
                  pl.BlockSpec((8, SUBLANES, SSM_CB), lambda m, t: (0, 0, m))],
        out_specs=[pl.BlockSpec((tt, SSM_UB), rev), wb, wb, wc, wc, vec_c, vec_c, vec_u],
        out_shape=[jax.ShapeDtypeStruct((l, SSM_W), F32),
                   jax.ShapeDtypeStruct((SSM_NB, SSM_UB, SSM_CB), F32), jax.ShapeDtypeStruct((SSM_NB, SSM_UB, SSM_CB), F32),
                   jax.ShapeDtypeStruct((SSM_NB, SSM_CB, SSM_UB), F32), jax.ShapeDtypeStruct((SSM_NB, SSM_CB, SSM_UB), F32),
                   jax.ShapeDtypeStruct((1, SSM_CH), F32), jax.ShapeDtypeStruct((1, SSM_CH), F32),
                   jax.ShapeDtypeStruct((1, SSM_W), F32)],
        scratch_shapes=[pltpu.VMEM((tt, SSM_CB), F32), pltpu.VMEM((tt, SSM_CB), F32),
                        pltpu.VMEM((2 * SUBLANES, SSM_CB), F32)],
        compiler_params=_params(("parallel", "arbitrary")),
    )(dy, u, sre, sim, sre, sim, bre, bim, cre, cim, dvec, tab)


def _merge_fwd(x, gl, attn, y1, wba, wbs, wglu, bglu, wout, gpost, gpre, tl):
    l = x.shape[0]

    def body(x_ref, gl_ref, at_ref, y1_ref, wba_ref, wbs_ref, wglu_ref, bglu_ref, wout_ref, gpost_ref, gpre_ref,
             a_ref, sm_ref, mg_ref, z_ref, x1_ref, hn2_ref, y3_ref):
        y2 = _gelu(y1_ref[...])
        sg = _sigmoid(_dot(y2.astype(BF16), wglu_ref[...]) + bglu_ref[...])
        y3 = (y2 * sg).astype(BF16)
        y3_ref[...] = y3
        a = _dot(at_ref[...], wba_ref[...])
        sm = _dot(y3, wbs_ref[...])
        a_ref[...] = a
        sm_ref[...] = sm
        g = _sigmoid(gl_ref[...])
        merged = (g[:, :D_MODEL] * a + g[:, D_MODEL:] * sm).astype(BF16)
        mg_ref[...] = merged
        z = _dot(merged, wout_ref[...])
        z_ref[...] = z
        n, _ = _rms(z, gpost_ref[...])
        x1 = x_ref[...] + n
        x1_ref[...] = x1
        hn2, _ = _rms(x1, gpre_ref[...])
        hn2_ref[...] = hn2.astype(BF16)

    outs = [(D_MODEL, F32), (D_MODEL, F32), (D_MODEL, BF16), (D_MODEL, F32), (D_MODEL, F32), (D_MODEL, BF16),
            (SSM_W, BF16)]
    return pl.pallas_call(
        body, name="merge_fwd", grid=(l // tl,),
        in_specs=[_row(tl, D_MODEL), _row(tl, 2 * D_MODEL), _row(tl, HP), _row(tl, SSM_W),
                  _const((HP, D_MODEL)), _const((SSM_W, D_MODEL)), _const((SSM_W, SSM_W)), _const((1, SSM_W)),
                  _const((D_MODEL, D_MODEL)), _const((1, D_MODEL)), _const((1, D_MODEL))],
        out_specs=[_row(tl, n) for n, _ in outs],
        out_shape=[jax.ShapeDtypeStruct((l, n), dt) for n, dt in outs],
        compiler_params=_params(("parallel",)),
    )(x, gl, attn, y1, wba, wbs, wglu, bglu, wout, gpost, gpre)


def _merge_bwd(dhn2a, dhn2b, x1, dx2, z, gl, a, sm, y1, wba, wbs, wglu, bglu, wout, gpost, gpre, tl):
    l = x1.shape[0]

    def body(da_ref, db_ref, x1_ref, dx2_ref, z_ref, gl_ref, a_ref, sm_ref, y1_ref,
             wba_ref, wbs_ref, wglu_ref, bglu_ref, wout_ref, gpost_ref, gpre_ref,
             dx1_ref, dz_ref, dbra_ref, dbrs_ref, dgl_ref, dat_ref, dy1_ref, dt_ref, y2_ref,
             dgpre_ref, dgpost_ref, dbg_ref, dbglu_ref):
        @pl.when(pl.program_id(0) == 0)
        def _():
            for ref in (dgpre_ref, dgpost_ref, dbg_ref, dbglu_ref):
                ref[...] = jnp.zeros_like(ref)

        dhn2 = da_ref[...] + db_ref[...]
        dx1a, dgpre = _rms_bwd(dhn2, x1_ref[...], gpre_ref[...])
        dgpre_ref[...] += dgpre
        dx1 = dx2_ref[...] + dx1a
        dx1_ref[...] = dx1
        dz, dgpost = _rms_bwd(dx1, z_ref[...], gpost_ref[...])
        dgpost_ref[...] += dgpost
        dzb = dz.astype(BF16)
        dz_ref[...] = dzb
        dm = _dot_nt(dzb, wout_ref[...])
        g = _sigmoid(gl_ref[...])
        g0 = g[:, :D_MODEL]
        g1 = g[:, D_MODEL:]
        dbra = (dm * g0).astype(BF16)
        dbrs = (dm * g1).astype(BF16)
        dbra_ref[...] = dbra
        dbrs_ref[...] = dbrs
        dgl0 = dm * a_ref[...] * g0 * (1.0 - g0)
        dgl1 = dm * sm_ref[...] * g1 * (1.0 - g1)
        dgl_ref[:, :D_MODEL] = dgl0.astype(BF16)
        dgl_ref[:, D_MODEL:] = dgl1.astype(BF16)
        dbg_ref[:, :D_MODEL] += jnp.sum(dgl0, axis=0, keepdims=True)
        dbg_ref[:, D_MODEL:] += jnp.sum(dgl1, axis=0, keepdims=True)
        dat_ref[...] = _dot_nt(dbra, wba_ref[...]).astype(BF16)
        dy3 = _dot_nt(dbrs, wbs_ref[...])
        y1v = y1_ref[...]
        y2 = _gelu(y1v)
        y2b = y2.astype(BF16)
        y2_ref[...] = y2b
        sg = _sigmoid(_dot(y2b, wglu_ref[...]) + bglu_ref[...])
        dt = dy3 * y2 * sg * (1.0 - sg)
        dtb = dt.astype(BF16)
        dt_ref[...] = dtb
        dbglu_ref[...] += jnp.sum(dt, axis=0, keepdims=True)
        dy2 = dy3 * sg + _dot_nt(dtb, wglu_ref[...])
        dy1_ref[...] = dy2 * _gelu_grad(y1v)

    outs = [(D_MODEL, F32), (D_MODEL, BF16), (D_MODEL, BF16), (D_MODEL, BF16), (2 * D_MODEL, BF16), (HP, BF16),
            (SSM_W, F32), (SSM_W, BF16), (SSM_W, BF16)]
    accs = [D_MODEL, D_MODEL, 2 * D_MODEL, SSM_W]
    return pl.pallas_call(
        body, name="merge_bwd", grid=(l // tl,),
        in_specs=[_row(tl, D_MODEL), _row(tl, D_MODEL), _row(tl, D_MODEL), _row(tl, D_MODEL), _row(tl, D_MODEL),
                  _row(tl, 2 * D_MODEL), _row(tl, D_MODEL), _row(tl, D_MODEL), _row(tl, SSM_W),
                  _const((HP, D_MODEL)), _const((SSM_W, D_MODEL)), _const((SSM_W, SSM_W)), _const((1, SSM_W)),
                  _const((D_MODEL, D_MODEL)), _const((1, D_MODEL)), _const((1, D_MODEL))],
        out_specs=[_row(tl, n) for n, _ in outs] + [_const((1, n)) for n in accs],
        out_shape=[jax.ShapeDtypeStruct((l, n), dt) for n, dt in outs]
        + [jax.ShapeDtypeStruct((1, n), F32) for n in accs],
        compiler_params=_params(("arbitrary",)),
    )(dhn2a, dhn2b, x1, dx2, z, gl, a, sm, y1, wba, wbs, wglu, bglu, wout, gpost, gpre)


def _proj_bwd(x, dx1, cq, ckv, dq, dk, dv, du, dgl, g1, win, gq, wuq, gkv, wukv, rc, rs, tl):
    l = x.shape[0]

    def body(x_ref, dx1_ref, cq_ref, ckv_ref, dq_ref, dk_ref, dv_ref, du_ref, dgl_ref,
             g1_ref, win_ref, gq_ref, wuq_ref, gkv_ref, wukv_ref, rc_ref, rs_ref,
             gx_ref, dql_ref, qn_ref, ckvn_ref, dproj_ref, dg1_ref, dgq_ref, dgkv_ref):
        @pl.when(pl.program_id(0) == 0)
        def _():
            for ref in (dg1_ref, dgq_ref, dgkv_ref):
                ref[...] = jnp.zeros_like(ref)

        c1 = rc_ref[...]
        s1 = rs_ref[...]
        dql = _rope_bwd(dq_ref[...], jnp.tile(c1, (1, N_HEADS)), jnp.tile(s1, (1, N_HEADS))).astype(BF16)
        dql_ref[...] = dql
        dqn = _dot_nt(dql, wuq_ref[...])
        cq = cq_ref[...]
        qn, _ = _rms(cq, gq_ref[...])
        qn_ref[...] = qn.astype(BF16)
        dcq, dgq = _rms_bwd(dqn, cq, gq_ref[...])
        dgq_ref[...] += dgq
        dkb = dk_ref[...]
        dvb = dv_ref[...]
        dkf = dkb.astype(F32)
        dkr = dkf[:, 0:HEAD_SLOT]
        for h in range(1, N_HEADS):
            dkr = dkr + dkf[:, h * HEAD_SLOT:(h + 1) * HEAD_SLOT]
        dkr = _rope_bwd(dkr, c1, s1)
        dckvn = _dot_nt(dkb, wukv_ref[:, :HP]) + _dot_nt(dvb, wukv_ref[:, HP:])
        ckv = ckv_ref[...]
        ckvn, _ = _rms(ckv, gkv_ref[...])
        ckvn_ref[...] = ckvn.astype(BF16)
        dckv, dgkv = _rms_bwd(dckvn, ckv, gkv_ref[...])
        dgkv_ref[...] += dgkv
        dproj_ref[:, P_CQ:P_CKV] = dcq.astype(BF16)
        dproj_ref[:, P_CKV:P_KR] = dckv.astype(BF16)
        dproj_ref[:, P_KR:P_U] = dkr.astype(BF16)
        dproj_ref[:, P_U:P_GL] = du_ref[...].astype(BF16)
        dproj_ref[:, P_GL:P_END] = dgl_ref[...]
        dhn = _dot_nt(dproj_ref[...], win_ref[...])
        dxa, dg1 = _rms_bwd(dhn, x_ref[...], g1_ref[...])
        dg1_ref[...] += dg1
        gx_ref[...] = dx1_ref[...] + dxa

    outs = [(D_MODEL, F32), (HP, BF16), (Q_RANK, BF16), (KV_RANK, BF16), (P_END, BF16)]
    accs = [D_MODEL, Q_RANK, KV_RANK]
    return pl.pallas_call(
        body, name="proj_bwd", grid=(l // tl,),
        in_specs=[_row(tl, D_MODEL), _row(tl, D_MODEL), _row(tl, Q_RANK), _row(tl, KV_RANK), _row(tl, HP),
                  _row(tl, HP), _row(tl, HP), _row(tl, SSM_W), _row(tl, 2 * D_MODEL),
                  _const((1, D_MODEL)), _const((D_MODEL, P_END)), _const((1, Q_RANK)), _const((Q_RANK, HP)),
                  _const((1, KV_RANK)), _const((KV_RANK, 2 * HP)), _row(tl, HEAD_SLOT), _row(tl, HEAD_SLOT)],
        out_specs=[_row(tl, n) for n, _ in outs] + [_const((1, n)) for n in accs],
        out_shape=[jax.ShapeDtypeStruct((l, n), dt) for n, dt in outs]
        + [jax.ShapeDtypeStruct((1, n), F32) for n in accs],
        compiler_params=_params(("arbitrary",)),
    )(x, dx1, cq, ckv, dq, dk, dv, du, dgl, g1, win, gq, wuq, gkv, wukv, rc, rs)


CONV_CB = 256
CONV_NB = D_FF // CONV_CB


def _conv3(h, halo, w, b):
    return b + w[0:1, :] * _shift_down(h, 2, halo) + w[1:2, :] * _shift_down(h, 1, halo) + w[2:3, :] * h


def _conv_fwd(h, cw, cb, tl):
    l = h.shape[0]

    def body(hg_ref, hv_ref, wg_ref, wv_ref, bg_ref, bv_ref, act_ref, halo_ref):
        @pl.when(pl.program_id(1) == 0)
        def _():
            halo_ref[...] = jnp.zeros_like(halo_ref)

        hg = hg_ref[...]
        hv = hv_ref[...]
        cg = _conv3(hg, halo_ref[0:SUBLANES, :], wg_ref[...], bg_ref[...])
        cv = _conv3(hv, halo_ref[SUBLANES:, :], wv_ref[...], bv_ref[...])
        act_ref[...] = (_gelu(cg) * cv).astype(BF16)
        halo_ref[0:SUBLANES, :] = hg[tl - SUBLANES:, :]
        halo_ref[SUBLANES:, :] = hv[tl - SUBLANES:, :]

    gmap = lambda c, r: (r, c)
    vmap = lambda c, r: (r, CONV_NB + c)
    return pl.pallas_call(
        body, name="conv_fwd", grid=(CONV_NB, l // tl),
        in_specs=[pl.BlockSpec((tl, CONV_CB), gmap), pl.BlockSpec((tl, CONV_CB), vmap),
                  pl.BlockSpec((3, CONV_CB), lambda c, r: (0, c)), pl.BlockSpec((3, CONV_CB), lambda c, r: (0, CONV_NB + c)),
                  pl.BlockSpec((1, CONV_CB), lambda c, r: (0, c)), pl.BlockSpec((1, CONV_CB), lambda c, r: (0, CONV_NB + c))],
        out_specs=pl.BlockSpec((tl, CONV_CB), gmap),
        out_shape=jax.ShapeDtypeStruct((l, D_FF), BF16),
        scratch_shapes=[pltpu.VMEM((2 * SUBLANES, CONV_CB), F32)],
        compiler_params=_params(("parallel", "arbitrary")),
    )(h, h, cw, cw, cb, cb)


def _conv_bwd(h, dact, cw, cb, tl):
    l = h.shape[0]
    nr = l // tl
    tpb = tl // SUBLANES

    def body(hg_ref, hv_ref, hgh_ref, hvh_ref, da_ref, wg_ref, wv_ref, bg_ref, bv_ref,
             dh_ref, dwg_ref, dwv_ref, dbg_ref, dbv_ref, car_ref):
        r = pl.program_id(1)

        @pl.when(r == 0)
        def _():
            for ref in (car_ref, dwg_ref, dwv_ref, dbg_ref, dbv_ref):
                ref[...] = jnp.zeros_like(ref)

        keep = jnp.where(r == nr - 1, 0.0, 1.0)
        da = da_ref[...].astype(F32)

        def half(h_ref, halo, w, b):
            hh = h_ref[...]
            h1 = _shift_down(hh, 1, halo)
            h2 = _shift_down(hh, 2, halo)
            return hh, h1, h2, b + w[0:1, :] * h2 + w[1:2, :] * h1 + w[2:3, :] * hh

        wg = wg_ref[...]
        wv = wv_ref[...]
        hg, hg1, hg2, cg = half(hg_ref, hgh_ref[...] * keep, wg, bg_ref[...])
        hv, hv1, hv2, cv = half(hv_ref, hvh_ref[...] * keep, wv, bv_ref[...])
        dcg = da * cv * _gelu_grad(cg)
        dcv = da * _gelu(cg)

        def back(dc, hh, h1, h2, w, nxt, dw_ref, db_ref, part):
            db_ref[...] += jnp.sum(dc, axis=0, keepdims=True)
            dw_ref[0:1, :] += jnp.sum(dc * h2, axis=0, keepdims=True)
            dw_ref[1:2, :] += jnp.sum(dc * h1, axis=0, keepdims=True)
            dw_ref[2:3, :] += jnp.sum(dc * hh, axis=0, keepdims=True)
            dh = w[2:3, :] * dc + w[1:2, :] * _shift_up(dc, 1, nxt) + w[0:1, :] * _shift_up(dc, 2, nxt)
            dh_ref[part] = dh.astype(BF16)

        back(dcg, hg, hg1, hg2, wg, car_ref[0:SUBLANES, :], dwg_ref, dbg_ref, 0)
        back(dcv, hv, hv1, hv2, wv, car_ref[SUBLANES:, :], dwv_ref, dbv_ref, 1)
        car_ref[0:SUBLANES, :] = dcg[0:SUBLANES, :]
        car_ref[SUBLANES:, :] = dcv[0:SUBLANES, :]

    grev = lambda c, r: (nr - 1 - r, c)
    vrev = lambda c, r: (nr - 1 - r, CONV_NB + c)
    ghalo = lambda c, r: (jnp.maximum((nr - 1 - r) * tpb - 1, 0), c)
    vhalo = lambda c, r: (jnp.maximum((nr - 1 - r) * tpb - 1, 0), CONV_NB + c)
    colg = lambda c, r: (0, c)
    colv = lambda c, r: (0, CONV_NB + c)
    return pl.pallas_call(
        body, name="conv_bwd", grid=(CONV_NB, nr),
        in_specs=[pl.BlockSpec((tl, CONV_CB), grev), pl.BlockSpec((tl, CONV_CB), vrev),
                  pl.BlockSpec((SUBLANES, CONV_CB), ghalo), pl.BlockSpec((SUBLANES, CONV_CB), vhalo),
                  pl.BlockSpec((tl, CONV_CB), grev),
                  pl.BlockSpec((3, CONV_CB), colg), pl.BlockSpec((3, CONV_CB), colv),
                  pl.BlockSpec((1, CONV_CB), colg), pl.BlockSpec((1, CONV_CB), colv)],
        out_specs=[pl.BlockSpec((2, tl, CONV_CB), lambda c, r: (0, nr - 1 - r, c)),
                   pl.BlockSpec((3, CONV_CB), colg), pl.BlockSpec((3, CONV_CB), colg),
                   pl.BlockSpec((1, CONV_CB), colg), pl.BlockSpec((1, CONV_CB), colg)],
        out_shape=[jax.ShapeDtypeStruct((2, l, D_FF), BF16),
                   jax.ShapeDtypeStruct((3, D_FF), F32), jax.ShapeDtypeStruct((3, D_FF), F32),
                   jax.ShapeDtypeStruct((1, D_FF), F32), jax.ShapeDtypeStruct((1, D_FF), F32)],
        scratch_shapes=[pltpu.VMEM((2 * SUBLANES, CONV_CB), F32)],
        compiler_params=_params(("parallel", "arbitrary")),
    )(h, h, h, h, dact, cw, cw, cb, cb)


def _loss_head(ff, x1, tgt, g, tl):
    l = ff.shape[0]

    def body(ff_ref, x1_ref, tg_ref, g_ref, loss_ref, dx2_ref, dff_ref, dg_ref):
        @pl.when(pl.program_id(0) == 0)
        def _():
            loss_ref[...] = jnp.zeros_like(loss_ref)
            dg_ref[...] = jnp.zeros_like(dg_ref)

        f = ff_ref[...]
        gv = g_ref[...]
        n, _ = _rms(f, gv)
        e = x1_ref[...] + n - tg_ref[...]
        loss_ref[...] += 0.5 * jnp.sum(jnp.mean(e * e, axis=-1, keepdims=True), axis=0, keepdims=True)
        dx2 = e * (1.0 / D_MODEL)
        dx2_ref[...] = dx2
        dff, dg = _rms_bwd(dx2, f, gv)
        dff_ref[...] = dff.astype(BF16)
        dg_ref[...] += dg

    return pl.pallas_call(
        body, name="loss_head", grid=(l // tl,),
        in_specs=[_row(tl, D_MODEL), _row(tl, D_MODEL), _row(tl, D_MODEL), _const((1, D_MODEL))],
        out_specs=[_const((1, LANES)), _row(tl, D_MODEL), _row(tl, D_MODEL), _const((1, D_MODEL))],
        out_shape=[jax.ShapeDtypeStruct((1, LANES), F32), jax.ShapeDtypeStruct((l, D_MODEL), F32),
                   jax.ShapeDtypeStruct((l, D_MODEL), BF16), jax.ShapeDtypeStruct((1, D_MODEL), F32)],
        compiler_params=_params(("arbitrary",)),
    )(ff, x1, tgt, g)


def _ssm_disc(lam_re, lam_im, log_dt, b_re, b_im):
    dt = jnp.exp(log_dt)[:, None]
    mag = jnp.exp(lam_re * dt)
    ang = lam_im * dt
    a_re, a_im = mag * jnp.cos(ang), mag * jnp.sin(ang)
    den = lam_re * lam_re + lam_im * lam_im
    n_re, n_im = a_re - 1.0, a_im
    z_re = (n_re * lam_re + n_im * lam_im) / den
    z_im = (n_im * lam_re - n_re * lam_im) / den
    bb_re = z_re[..., None] * b_re - z_im[..., None] * b_im
    bb_im = z_re[..., None] * b_im + z_im[..., None] * b_re
    return a_re, a_im, bb_re, bb_im


_GPB = SSM_CB // SSM_P


def _embed_b(bb):
    t = bb.transpose(0, 2, 1).reshape(SSM_NB, _GPB, SSM_H, SSM_P)
    return jnp.einsum('mjhp,jk->mjhkp', t, jnp.eye(_GPB, dtype=bb.dtype)).reshape(SSM_NB, SSM_UB, SSM_CB)


def _extract_b(d):
    t = d.reshape(SSM_NB, _GPB, SSM_H, _GPB, SSM_P)
    t = jnp.einsum('mjhkp,jk->mjhp', t, jnp.eye(_GPB, dtype=d.dtype))
    return t.reshape(SSM_G, SSM_H, SSM_P).transpose(0, 2, 1)


def _embed_c(c):
    t = c.transpose(0, 2, 1).reshape(SSM_NB, _GPB, SSM_P, SSM_H)
    return jnp.einsum('mjph,jk->mjpkh', t, jnp.eye(_GPB, dtype=c.dtype)).reshape(SSM_NB, SSM_CB, SSM_UB)


def _extract_c(d):
    t = d.reshape(SSM_NB, _GPB, SSM_P, _GPB, SSM_H)
    t = jnp.einsum('mjpkh,jk->mjph', t, jnp.eye(_GPB, dtype=d.dtype))
    return t.reshape(SSM_G, SSM_P, SSM_H).transpose(0, 2, 1)


def _scan_tables(a_re, a_im, reverse):
    ar = a_re.reshape(1, SSM_CH)
    ai = (-a_im if reverse else a_im).reshape(1, SSM_CH)
    pr, pi = [ar], [ai]
    for _ in range(SUBLANES - 1):
        pr, pi = pr + [pr[-1] * ar - pi[-1] * ai], pi + [pr[-1] * ai + pi[-1] * ar]
    rows = jnp.arange(SUBLANES)[:, None]
    out = []
    for k in (1, 2, 4):
        valid = (rows + k <= SUBLANES - 1) if reverse else (rows >= k)
        out += [jnp.where(valid, pr[k - 1], 0.0), jnp.where(valid, pi[k - 1], 0.0)]
    order = list(range(SUBLANES - 1, -1, -1)) if reverse else list(range(SUBLANES))
    out += [jnp.concatenate([pr[n] for n in order], axis=0), jnp.concatenate([pi[n] for n in order], axis=0)]
    return jnp.stack(out).astype(F32)


def _pad_heads(w, d):
    lead = w.shape[:-1]
    w = w.reshape(lead + (N_HEADS, d))
    w = jnp.pad(w, [(0, 0)] * len(lead) + [(0, 0), (0, HEAD_SLOT - d)])
    return w.reshape(lead + (HP,))


def _unpad_heads(w, d):
    lead = w.shape[:-1]
    return w.reshape(lead + (N_HEADS, HEAD_SLOT))[..., :d].reshape(lead + (N_HEADS * d,))


def _chip_major(w, axis):
    k, n = w.shape
    if axis == 0:
        return w.reshape(N_CHIPS, k // N_CHIPS, n)
    return w.reshape(k, N_CHIPS, n // N_CHIPS).transpose(1, 0, 2)


def _from_chip_major(w, axis):
    if axis == 0:
        return w.reshape(-1, w.shape[2])
    return w.transpose(1, 0, 2).reshape(w.shape[1], -1)


def _pad_w_in(w):
    z = lambda n: jnp.zeros((w.shape[0], n), w.dtype)
    return jnp.concatenate([w[:, :640], z(KR_LANE), w[:, 640:672], z(HEAD_SLOT - KR_LANE - QK_ROPE), w[:, 672:]], axis=1)


def _unpad_w_in(w):
    return jnp.concatenate([w[:, :640], w[:, P_KR + KR_LANE:P_KR + KR_LANE + QK_ROPE], w[:, P_U:]], axis=1)


def _local_step(x, positions, tgt, wts, sp):
    l = x.shape[0]
    tl = min(256, l)
    ta = min(512, l)

    inv_freq = ROPE_THETA ** (-jnp.arange(0, QK_ROPE, 2, dtype=F32) / QK_ROPE)
    ang = positions.astype(F32)[:, None] * inv_freq
    cos, sin = jnp.cos(ang), jnp.sin(ang)
    one = jnp.ones((l, KR_LANE), F32)
    rc = jnp.concatenate([one, cos, cos, jnp.ones((l, HEAD_SLOT - KR_LANE - QK_ROPE), F32)], axis=1)
    rs = jnp.concatenate([0 * one, -sin, sin, jnp.zeros((l, HEAD_SLOT - KR_LANE - QK_ROPE), F32)], axis=1)

    win = _pad_w_in(wts["w_in"])
    wuq = _pad_heads(wts["w_uq"], QK_HEAD)
    wukv = jnp.concatenate([_pad_heads(wts["w_uk"], QK_NOPE), _pad_heads(wts["w_uv"], V_HEAD)], axis=1)
    wba = jnp.pad(wts["w_branch_attn"].reshape(N_HEADS, V_HEAD, D_MODEL),
                  ((0, 0), (0, HEAD_SLOT - V_HEAD), (0, 0))).reshape(HP, D_MODEL)
    wbs, wglu, wout, wup, wdown = wts["w_branch_ssm"], wts["w_glu"], wts["w_out"], wts["w_up"], wts["w_down"]

    disc_in = (sp["ssm_lambda_re"], sp["ssm_lambda_im"], sp["ssm_log_dt"], sp["ssm_b_re"], sp["ssm_b_im"])
    (a_re, a_im, bb_re, bb_im), disc_vjp = jax.vjp(_ssm_disc, *disc_in)
    bre, bim = _embed_b(bb_re).astype(BF16), _embed_b(bb_im).astype(BF16)
    cre, cim = _embed_c(sp["ssm_c_re"]).astype(BF16), _embed_c(sp["ssm_c_im"]).astype(BF16)
    dvec = sp["ssm_d"].reshape(1, SSM_W)
    tab_f = _scan_tables(a_re, a_im, False)
    tab_r = _scan_tables(a_re, a_im, True)

    g1, gq, gkv = sp["mix_norm_pre"], sp["q_norm"], sp["kv_norm"]
    gpost, gpre, gfin = sp["mix_norm_post"], sp["ffn_norm_pre"], sp["ffn_norm_post"]
    bgate, bglu, convb = sp["b_gate"], sp["b_glu"], sp["conv_b"]
    convw = sp["conv_w"]

    hn, cq, ckv, q, k, v, u, gl = _proj_fwd(x, g1, win, gq, wuq, gkv, wukv, rc, rs, bgate, tl)
    attn, lse = _attn_fwd(q, k, v, ta)
    y1, sre, sim = _ssm_fwd(u, bre, bim, cre, cim, dvec, tab_f, ta)
    a, sm, merged, z, x1, hn2, y3 = _merge_fwd(x, gl, attn, y1, wba, wbs, wglu, bglu, wout, gpost, gpre, tl)
    h = _mm(hn2, wup, "ffn_up")
    act = _conv_fwd(h, convw, convb, ta)
    ff = _mm(act, wdown, "ffn_down")
    loss, dx2, dff, dgfin = _loss_head(ff, x1, tgt, gfin, tl)

    dact = _mm(dff, wdown, "ffn_down_dx", out_dtype=BF16, bt=True)
    d_wdown = _mm_tn(act, dff, "ffn_down_dw")
    dh, dwg, dwv, dbg, dbv = _conv_bwd(h, dact, convw, convb, ta)
    d_convw = jnp.concatenate([dwg, dwv], axis=1)
    d_convb = jnp.concatenate([dbg, dbv], axis=1)
    dhn2a = _mm(dh, wup, "ffn_up_dx_gate", bt=True, b_col0=0, a_lead=0)
    dhn2b = _mm(dh, wup, "ffn_up_dx_val", bt=True, b_col0=1, a_lead=1)
    d_wup = _mm_tn(hn2, dh, "ffn_up_dw", chips=True)
    (dx1, dz, dbra, dbrs, dgl, dattn, dy1, dt, y2, dgpre, dgpost, dbgate, dbglu) = _merge_bwd(
        dhn2a, dhn2b, x1, dx2, z, gl, a, sm, y1, wba, wbs, wglu, bglu, wout, gpost, gpre, tl)
    d_wout = _mm_tn(merged, dz, "w_out_dw")
    d_wba = _mm_tn(attn, dbra, "w_branch_attn_dw", chips=True)
    d_wbs = _mm_tn(y3, dbrs, "w_branch_ssm_dw", chips=True)
    d_wglu = _mm_tn(y2, dt, "w_glu_dw")
    dq, dk, dv = _attn_bwd(q, k, v, attn, dattn, lse, ta)
    du, dbre, dbim, dcre, dcim, dare, daim, dd = _ssm_bwd(dy1, u, sre, sim, bre, bim, cre, cim, dvec, tab_r, ta)
    gx, dql, qn, ckvn, dproj, dg1, dgq, dgkv = _proj_bwd(
        x, dx1, cq, ckv, dq, dk, dv, du, dgl, g1, win, gq, wuq, gkv, wukv, rc, rs, tl)
    d_win = _mm_tn(hn, dproj, "w_in_dw")
    d_wuq = _mm_tn(qn, dql, "w_uq_dw")
    d_wuk = _mm_tn(ckvn, dk, "w_uk_dw")
    d_wuv = _mm_tn(ckvn, dv, "w_uv_dw")

    d_lre, d_lim, d_ldt, d_bre, d_bim = disc_vjp((dare.reshape(SSM_G, SSM_P), daim.reshape(SSM_G, SSM_P),
                                                  _extract_b(dbre), _extract_b(dbim)))
    ncol = D_MODEL // N_CHIPS
    big = {
        "w_in": _chip_major(_unpad_w_in(d_win), 1),
        "w_uq": _chip_major(_unpad_heads(d_wuq, QK_HEAD), 1),
        "w_uk": _chip_major(_unpad_heads(d_wuk, QK_NOPE), 1),
        "w_uv": _chip_major(_unpad_heads(d_wuv, V_HEAD), 1),
        "w_glu": _chip_major(d_wglu, 0),
        "w_branch_attn": d_wba.reshape(N_CHIPS, N_HEADS, HEAD_SLOT, ncol)[:, :, :V_HEAD].reshape(
            N_CHIPS, N_HEADS * V_HEAD, ncol),
        "w_branch_ssm": d_wbs,
        "w_out": _chip_major(d_wout, 0),
        "w_up": d_wup,
        "w_down": _chip_major(d_wdown, 0),
    }
    small = {
        "conv_w": d_convw,
        "mix_norm_pre": dg1, "q_norm": dgq, "kv_norm": dgkv,
        "ssm_lambda_re": d_lre, "ssm_lambda_im": d_lim, "ssm_log_dt": d_ldt,
        "ssm_b_re": d_bre, "ssm_b_im": d_bim,
        "ssm_c_re": _extract_c(dcre), "ssm_c_im": _extract_c(dcim),
        "ssm_d": dd.reshape(SSM_G, SSM_H), "b_glu": dbglu, "b_gate": dbgate,
        "mix_norm_post": dgpost, "ffn_norm_pre": dgpre, "conv_b": d_convb, "ffn_norm_post": dgfin,
    }
    return loss[0, 0], gx, big, small


_ANY = pl.BlockSpec(memory_space=pl.ANY)


ROW_TILE = 16


def _place():
    x, y, c = lax.axis_index("x"), lax.axis_index("y"), lax.axis_index("c")
    return x, y, c, 2 * x + y, [(1 - x, y), (x, 1 - y), (1 - x, 1 - y)]


def _half(rows, which):
    hr = rows // 2
    return pl.ds(pl.multiple_of(which * hr, ROW_TILE), hr)


def _remote(src, dst, send_sems, recv_sems, n, dev):
    return pltpu.make_async_remote_copy(src_ref=src, dst_ref=dst, send_sem=send_sems.at[n], recv_sem=recv_sems.at[n],
                                        device_id=dev, device_id_type=MESH)


def _gather_big(shards):
    nw = len(shards)
    rows = [s.shape[0] for s in shards]

    def body(*refs):
        ins, outs = refs[:nw], refs[nw:2 * nw]
        ici_send, ici_recv, d2d_send, d2d_recv = refs[2 * nw:]
        x, y, c, me, peers = _place()
        sent = []
        for i in range(nw):
            for p, (px, py) in enumerate(peers):
                cp = _remote(ins[i].at[_half(rows[i], c)], outs[i].at[me, _half(rows[i], c)], ici_send, ici_recv,
                             3 * i + p, (px, py, c))
                cp.start()
                sent.append(cp)
        for p, (px, py) in enumerate(peers):
            for i in range(nw):
                blk = outs[i].at[2 * px + py, _half(rows[i], c)]
                _remote(blk, blk, ici_send, ici_recv, 3 * i + p, (px, py, c)).wait_recv()
                cp = _remote(blk, blk, d2d_send, d2d_recv, 3 * i + p, (x, y, 1 - c))
                cp.start()
                sent.append(cp)
        for p, (px, py) in enumerate(peers):
            for i in range(nw):
                blk = outs[i].at[2 * px + py, _half(rows[i], 1 - c)]
                _remote(blk, blk, d2d_send, d2d_recv, 3 * i + p, (x, y, 1 - c)).wait_recv()
        for cp in sent:
            cp.wait_send()

    dma = pltpu.SemaphoreType.DMA
    return pl.pallas_call(
        body, name="gather_weights", in_specs=[_ANY] * nw, out_specs=[_ANY] * nw,
        out_shape=[jax.ShapeDtypeStruct((N_CHIPS,) + s.shape, s.dtype) for s in shards],
        scratch_shapes=[dma((3 * nw,)), dma((3 * nw,)), dma((3 * nw,)), dma((3 * nw,))],
    )(*shards)


def _reduce_to_sibling(grads):
    nw = len(grads)

    def body(*refs):
        ins, outs = refs[:nw], refs[nw:2 * nw]
        send_sems, recv_sems = refs[2 * nw:]
        x, y, c, _, _ = _place()
        sent = []
        for i in range(nw):
            cp = _remote(ins[i].at[pl.ds(0, N_CHIPS), _half(grads[i].shape[1], 1 - c)], outs[i], send_sems, recv_sems,
                         i, (x, y, 1 - c))
            cp.start()
            sent.append(cp)
        for cp in sent:
            cp.wait()

    dma = pltpu.SemaphoreType.DMA
    return pl.pallas_call(
        body, name="reduce_grads_d2d", in_specs=[_ANY] * nw, out_specs=[_ANY] * nw,
        out_shape=[jax.ShapeDtypeStruct((N_CHIPS, g.shape[1] // 2, g.shape[2]), g.dtype) for g in grads],
        scratch_shapes=[dma((nw,)), dma((nw,))],
    )(*grads)


def _reduce_between_chips(pairs):
    nw = len(pairs)

    def body(*refs):
        ins, outs = refs[:nw], refs[nw:2 * nw]
        send_sems, recv_sems = refs[2 * nw:]
        x, y, c, me, peers = _place()
        sent = []
        for i in range(nw):
            for p, (px, py) in enumerate(peers):
                cp = _remote(ins[i].at[2 * px + py], outs[i].at[me], send_sems, recv_sems, 3 * i + p, (px, py, c))
                cp.start()
                sent.append(cp)
        for i in range(nw):
            for p, (px, py) in enumerate(peers):
                blk = outs[i].at[2 * px + py]
                _remote(blk, blk, send_sems, recv_sems, 3 * i + p, (px, py, c)).wait_recv()
        for cp in sent:
            cp.wait_send()

    dma = pltpu.SemaphoreType.DMA
    return pl.pallas_call(
        body, name="reduce_grads_ici", in_specs=[_ANY] * nw, out_specs=[_ANY] * nw,
        out_shape=[jax.ShapeDtypeStruct(p.shape, p.dtype) for p in pairs],
        scratch_shapes=[dma((3 * nw,)), dma((3 * nw,))],
    )(*pairs)


def _reduce_back(totals):
    nw = len(totals)

    def body(*refs):
        outs = refs[nw:2 * nw]
        send_sems, recv_sems = refs[2 * nw:]
        x, y, c, _, _ = _place()
        sent = []
        for i in range(nw):
            blk = outs[i].at[_half(totals[i].shape[0], c)]
            cp = _remote(blk, blk, send_sems, recv_sems, i, (x, y, 1 - c))
            cp.start()
            sent.append(cp)
        for i in range(nw):
            blk = outs[i].at[_half(totals[i].shape[0], 1 - c)]
            _remote(blk, blk, send_sems, recv_sems, i, (x, y, 1 - c)).wait_recv()
        for cp in sent:
            cp.wait_send()

    dma = pltpu.SemaphoreType.DMA
    return pl.pallas_call(
        body, name="reduce_grads_back", in_specs=[_ANY] * nw, out_specs=[_ANY] * nw,
        out_shape=[jax.ShapeDtypeStruct(t.shape, t.dtype) for t in totals],
        input_output_aliases={i: i for i in range(nw)},
        scratch_shapes=[dma((nw,)), dma((nw,))],
    )(*totals)


def _all_reduce_small(v, name):
    rows, w = v.shape

    def body(v_ref, out_ref, buf_ref, send_sems, recv_sems):
        x, y, c = lax.axis_index("x"), lax.axis_index("y"), lax.axis_index("c")
        me = 4 * x + 2 * y + c

        def flip(n):
            return (1 - x if n & 4 else x, 1 - y if n & 2 else y, 1 - c if n & 1 else c)

        sent = []
        for n in range(1, 8):
            cp = pltpu.make_async_remote_copy(src_ref=v_ref, dst_ref=buf_ref.at[me], send_sem=send_sems.at[n - 1],
                                              recv_sem=recv_sems.at[n - 1], device_id=flip(n), device_id_type=MESH)
            cp.start()
            sent.append(cp)
        buf_ref[me] = v_ref[...]
        for n in range(1, 8):
            px, py, pc = flip(n)
            pltpu.make_async_remote_copy(src_ref=v_ref, dst_ref=buf_ref.at[4 * px + 2 * py + pc],
                                         send_sem=send_sems.at[n - 1], recv_sem=recv_sems.at[n - 1],
                                         device_id=flip(n), device_id_type=MESH).wait_recv()
        for cp in sent:
            cp.wait_send()
        acc = buf_ref[0]
        for d in range(1, 8):
            acc = acc + buf_ref[d]
        out_ref[...] = acc

    vm = pl.BlockSpec(memory_space=pltpu.VMEM)
    return pl.pallas_call(
        body, name=name, in_specs=[vm], out_specs=vm,
        out_shape=jax.ShapeDtypeStruct((rows, w), F32),
        scratch_shapes=[pltpu.VMEM((8, rows, w), F32), pltpu.SemaphoreType.DMA((7,)), pltpu.SemaphoreType.DMA((7,))],
        compiler_params=pltpu.CompilerParams(vmem_limit_bytes=VMEM_LIMIT),
    )(v)


ELEMENTWISE_BLOCK = 256 * 1024


def _rows_tile(rows, cols):
    best = None
    for t in range(SUBLANES, rows + 1, SUBLANES):
        if rows % t == 0 and t * cols <= ELEMENTWISE_BLOCK:
            best = t
    return rows if best is None else best


def _add_pair(g, t, core, name):
    nb, n, w = t.shape
    tr = _rows_tile(n, w)
    steps = n // tr

    def body(core_ref, g_ref, t_ref, o_ref):
        o_ref[...] = (g_ref[...] + t_ref[...]).astype(BF16)

    spec = pl.BlockSpec((1, tr, w), lambda j, i, core_ref: (j, i, 0))
    return pl.pallas_call(
        body, name=name,
        grid_spec=pltpu.PrefetchScalarGridSpec(
            num_scalar_prefetch=1, grid=(nb, steps),
            in_specs=[pl.BlockSpec((1, tr, w), lambda j, i, core_ref: (j, core_ref[0] * steps + i, 0)), spec],
            out_specs=spec),
        out_shape=jax.ShapeDtypeStruct(t.shape, BF16),
        compiler_params=_params(("parallel", "parallel")))(core, g, t)


def _add_chips(landed, pairs, place, name):
    nb, n, w = landed.shape
    tr = _rows_tile(n, w)
    steps = n // tr

    def body(place_ref, r_ref, own_ref, o_ref):
        me = place_ref[0]
        acc = None
        for k in range(nb):
            blk = jnp.where(me == k, own_ref[0], r_ref[k]).astype(F32)
            acc = blk if acc is None else acc + blk
        o_ref[...] = acc

    return pl.pallas_call(
        body, name=name,
        grid_spec=pltpu.PrefetchScalarGridSpec(
            num_scalar_prefetch=1, grid=(steps,),
            in_specs=[pl.BlockSpec((nb, tr, w), lambda i, place_ref: (0, i, 0)),
                      pl.BlockSpec((1, tr, w), lambda i, place_ref: (place_ref[0], i, 0))],
            out_specs=pl.BlockSpec((tr, w), lambda i, place_ref: (place_ref[1] * steps + i, 0))),
        out_shape=jax.ShapeDtypeStruct((2 * n, w), F32),
        compiler_params=_params(("parallel",)))(place, landed, pairs)


def _adamw(w, g, m, v, name):
    rows, wd = w.shape
    tr = _rows_tile(rows, wd)
    c1 = 1.0 - ADAM_B1 ** ADAM_STEP
    c2 = 1.0 - ADAM_B2 ** ADAM_STEP

    def body(w_ref, g_ref, m_ref, v_ref, d_ref, mo_ref, vo_ref):
        gv = g_ref[...]
        m2 = ADAM_B1 * m_ref[...] + (1.0 - ADAM_B1) * gv
        v2 = ADAM_B2 * v_ref[...] + (1.0 - ADAM_B2) * (gv * gv)
        mo_ref[...] = m2
        vo_ref[...] = v2
        d_ref[...] = -ADAM_LR * ((m2 / c1) / (jnp.sqrt(v2 / c2) + ADAM_EPS) + ADAM_WD * w_ref[...])

    spec = pl.BlockSpec((tr, wd), lambda i: (i, 0))
    shp = jax.ShapeDtypeStruct((rows, wd), F32)
    return pl.pallas_call(body, name=name, grid=(rows // tr,), in_specs=[spec] * 4, out_specs=[spec] * 3,
                          out_shape=[shp] * 3, compiler_params=_params(("parallel",)))(w, g, m, v)


BIG = [("w_in", (1024, 3232), 1), ("w_uq", (384, 768), 1), ("w_uk", (256, 512), 1), ("w_uv", (256, 512), 1),
       ("w_glu", (512, 512), 0), ("w_branch_attn", (512, 1024), 1), ("w_branch_ssm", (512, 1024), 1),
       ("w_out", (1024, 1024), 0), ("w_up", (1024, 5632), 1), ("conv_w", (3, 5632), 1), ("w_down", (2816, 1024), 0)]
SMALL = [("mix_norm_pre", (1024,)), ("q_norm", (384,)), ("kv_norm", (256,)), ("ssm_lambda_re", (32, 64)),
         ("ssm_lambda_im", (32, 64)), ("ssm_log_dt", (32,)), ("ssm_b_re", (32, 64, 16)), ("ssm_b_im", (32, 64, 16)),
         ("ssm_c_re", (32, 16, 64)), ("ssm_c_im", (32, 16, 64)), ("ssm_d", (32, 16)), ("b_glu", (512,)),
         ("b_gate", (2048,)), ("mix_norm_post", (1024,)), ("ffn_norm_pre", (1024,)), ("conv_b", (5632,)),
         ("ffn_norm_post", (1024,))]
MATMUL_W = [b for b in BIG if b[0] != "conv_w"]
CONV_W_SHAPE = (3, 2 * D_FF)
CONV_W_SHARD = (3, 2 * D_FF // N_CHIPS)
SMALL_SUM = [("loss", (1,))] + SMALL + [("conv_w", CONV_W_SHAPE)]
SMALL_ADAM = SMALL + [("conv_w", CONV_W_SHARD)]


def _pack_flat(layout, vals):
    flat = jnp.concatenate([vals[n].astype(F32).reshape(-1) for n, _ in layout])
    rows = -(-(-(-flat.shape[0] // FLAT_W)) // SUBLANES) * SUBLANES
    return jnp.pad(flat, (0, rows * FLAT_W - flat.shape[0])).reshape(rows, FLAT_W)


def _unpack_flat(layout, flat):
    flat = flat.reshape(-1)
    out = {}
    o = 0
    for name, shape in layout:
        n = math.prod(shape)
        out[name] = flat[o:o + n].reshape(shape)
        o += n
    return out


_ARG_NAMES = ["x", "positions"] + [n for n in (
    "mix_norm_pre", "w_in", "q_norm", "w_uq", "kv_norm", "w_uk", "w_uv", "ssm_lambda_re", "ssm_lambda_im", "ssm_log_dt",
    "ssm_b_re", "ssm_b_im", "ssm_c_re", "ssm_c_im", "ssm_d", "w_glu", "b_glu", "w_branch_attn", "w_branch_ssm",
    "b_gate", "w_out", "mix_norm_post", "ffn_norm_pre", "w_up", "conv_w", "conv_b", "w_down", "ffn_norm_post")]
_WEIGHTS = _ARG_NAMES[2:]


def _gather_weights(w):
    c = lax.axis_index("c")
    mine = [w[name].astype(BF16) for name, _, _ in MATMUL_W]
    gathered = _gather_big(mine)
    own = (jnp.arange(N_CHIPS) == 2 * lax.axis_index("x") + lax.axis_index("y"))[:, None, None]
    full = {name: _from_chip_major(jnp.where(own, s[None], g), axis)
            for (name, _, axis), s, g in zip(MATMUL_W, mine, gathered)}

    chip = 2 * lax.axis_index("x") + lax.axis_index("y")
    ncol = 2 * D_FF // N_CHIPS
    placed = lax.dynamic_update_slice_in_dim(jnp.zeros((3, 2 * D_FF), F32), w["conv_w"], chip * ncol, axis=1)
    placed = placed * jnp.where(c == 0, 1.0, 0.0)
    cw_rows = -(-(-(-3 * 2 * D_FF // FLAT_W)) // SUBLANES) * SUBLANES
    placed = jnp.pad(placed.reshape(-1), (0, cw_rows * FLAT_W - 3 * 2 * D_FF)).reshape(cw_rows, FLAT_W)
    conv_w_full = _all_reduce_small(placed, "gather_conv_w").reshape(-1)[:3 * 2 * D_FF].reshape(3, 2 * D_FF)
    return full, conv_w_full


def _reduce_grads(gbig, loss, gsmall):
    core = lax.axis_index("c").astype(jnp.int32).reshape(1)
    chip = (2 * lax.axis_index("x") + lax.axis_index("y")).astype(jnp.int32).reshape(1)
    place = jnp.concatenate([chip, core])
    names = [name for name, _, _ in MATMUL_W]
    grads = [gbig[n] for n in names]
    theirs = _reduce_to_sibling(grads)
    pairs = [_add_pair(g, t, core, "reduce_pair_" + n) for n, g, t in zip(names, grads, theirs)]
    landed = _reduce_between_chips(pairs)
    totals = [_add_chips(r, p, place, "reduce_chips_" + n) for n, r, p in zip(names, landed, pairs)]
    g_red = dict(zip(names, _reduce_back(totals)))

    vals = dict(gsmall)
    vals["loss"] = loss
    small_red = _unpack_flat(SMALL_SUM, _all_reduce_small(_pack_flat(SMALL_SUM, vals), "reduce_small"))
    return g_red, small_red


def _step(args):
    x = args["x"][0]
    positions = args["positions"][0]
    tgt = args["loss_target"][0]
    w = {n: args[n][0] for n in _WEIGHTS}
    m = {n: args["m_" + n][0] for n in _WEIGHTS}
    v = {n: args["v_" + n][0] for n in _WEIGHTS}

    full, conv_w_full = _gather_weights(w)
    sp = {n: w[n].reshape(s) for n, s in SMALL}
    for n in ("mix_norm_pre", "q_norm", "kv_norm", "b_glu", "b_gate", "mix_norm_post", "ffn_norm_pre", "conv_b",
              "ffn_norm_post"):
        sp[n] = sp[n].reshape(1, -1)
    sp["conv_w"] = conv_w_full
    loss, gx, gbig, gsmall = _local_step(x, positions, tgt, full, sp)
    g_red, small_red = _reduce_grads(gbig, loss, gsmall)

    chip = 2 * lax.axis_index("x") + lax.axis_index("y")
    grads = dict(small_red)
    grads["conv_w"] = lax.dynamic_slice_in_dim(small_red["conv_w"], chip * CONV_W_SHARD[1], CONV_W_SHARD[1], axis=1)
    grads.update(g_red)

    outs = {"grad_" + n: grads[n] for n in _WEIGHTS}
    for name, _, _ in MATMUL_W:
        d, m2, v2 = _adamw(w[name], grads[name], m[name], v[name], "adamw_" + name)
        outs["delta_" + name], outs["new_m_" + name], outs["new_v_" + name] = d, m2, v2
    d_sm, m_sm, v_sm = _adamw(_pack_flat(SMALL_ADAM, w), _pack_flat(SMALL_ADAM, grads), _pack_flat(SMALL_ADAM, m),
                              _pack_flat(SMALL_ADAM, v), "adamw_small")
    for prefix, flat in (("delta_", d_sm), ("new_m_", m_sm), ("new_v_", v_sm)):
        for n, val in _unpack_flat(SMALL_ADAM, flat).items():
            outs[prefix + n] = val
    outs = {n: val.reshape(args[n.split("_", 1)[1] if not n.startswith("new_") else n[6:]].shape)
            for n, val in outs.items()}
    res = [small_red["loss"][0], gx[None]]
    for prefix in ("grad_", "delta_", "new_m_", "new_v_"):
        res += [outs[prefix + n] for n in _WEIGHTS]
    return tuple(res)


def kernel(x, positions, mix_norm_pre, w_in, q_norm, w_uq, kv_norm, w_uk, w_uv, ssm_lambda_re, ssm_lambda_im, ssm_log_dt, ssm_b_re, ssm_b_im, ssm_c_re, ssm_c_im, ssm_d, w_glu, b_glu, w_branch_attn, w_branch_ssm, b_gate, w_out, mix_norm_post, ffn_norm_pre, w_up, conv_w, conv_b, w_down, ffn_norm_post, loss_target, m_mix_norm_pre, m_w_in, m_q_norm, m_w_uq, m_kv_norm, m_w_uk, m_w_uv, m_ssm_lambda_re, m_ssm_lambda_im, m_ssm_log_dt, m_ssm_b_re, m_ssm_b_im, m_ssm_c_re, m_ssm_c_im, m_ssm_d, m_w_glu, m_b_glu, m_w_branch_attn, m_w_branch_ssm, m_b_gate, m_w_out, m_mix_norm_post, m_ffn_norm_pre, m_w_up, m_conv_w, m_conv_b, m_w_down, m_ffn_norm_post, v_mix_norm_pre, v_w_in, v_q_norm, v_w_uq, v_kv_norm, v_w_uk, v_w_uv, v_ssm_lambda_re, v_ssm_lambda_im, v_ssm_log_dt, v_ssm_b_re, v_ssm_b_im, v_ssm_c_re, v_ssm_c_im, v_ssm_d, v_w_glu, v_b_glu, v_w_branch_attn, v_w_branch_ssm, v_b_gate, v_w_out, v_mix_norm_post, v_ffn_norm_pre, v_w_up, v_conv_w, v_conv_b, v_w_down, v_ffn_norm_post):
    given = dict(locals())
    return _step(given)
```

```python
import math

import jax
import jax.numpy as jnp
from jax import lax
from jax.experimental import pallas as pl
from jax.experimental.pallas import tpu as pltpu

F32 = jnp.float32
BF16 = jnp.bfloat16
MESH = pl.DeviceIdType.MESH

D_MODEL = 1024
N_HEADS = 8
QK_NOPE = 64
QK_ROPE = 32
QK_HEAD = QK_NOPE + QK_ROPE
V_HEAD = 64
Q_RANK = 384
KV_RANK = 256
ROPE_THETA = 10000.0
SSM_W = 512
SSM_H = 16
SSM_G = 32
SSM_P = 64
SSM_CH = SSM_G * SSM_P
D_FF = 2816
EPS = 1e-6
ADAM_LR = 0.001
ADAM_B1 = 0.9
ADAM_B2 = 0.999
ADAM_EPS = 1e-08
ADAM_WD = 0.01
ADAM_STEP = 10

LANES = 128
SUBLANES = 8
VMEM_LIMIT = 56 * 1024 * 1024

HEAD_SLOT = LANES
HP = N_HEADS * HEAD_SLOT
P_CQ, P_CKV, P_KR, P_U, P_GL, P_END = 0, 384, 640, 768, 1280, 3328
KR_LANE = 64

FLAT_W = 1024
N_CHIPS = 4


def _tile(n, cap):
    if n <= cap:
        return n
    best = None
    for t in range(LANES, cap + 1, LANES):
        if n % t == 0:
            best = t
    assert best is not None, (n, cap)
    return best


def _params(sem):
    return pltpu.CompilerParams(dimension_semantics=sem, vmem_limit_bytes=VMEM_LIMIT)


def _dot(a, b):
    return jnp.dot(a, b, preferred_element_type=F32)


def _dot_nt(a, b):
    return lax.dot_general(a, b, (((1,), (1,)), ((), ())), preferred_element_type=F32)


def _dot_tn(a, b):
    return lax.dot_general(a, b, (((0,), (0,)), ((), ())), preferred_element_type=F32)


def _rms(x, g):
    r = lax.rsqrt(jnp.mean(x * x, axis=-1, keepdims=True) + EPS)
    return x * r * g, r


def _rms_bwd(dy, x, g):
    r = lax.rsqrt(jnp.mean(x * x, axis=-1, keepdims=True) + EPS)
    dyg = dy * g
    dx = r * dyg - x * (r * r * r) * jnp.mean(dyg * x, axis=-1, keepdims=True)
    dg = jnp.sum(dy * x * r, axis=0, keepdims=True)
    return dx, dg


_GELU_K0 = math.sqrt(2.0 / math.pi)
_GELU_K1 = 0.044715


def _gelu(x):
    th = jnp.tanh(_GELU_K0 * (x + _GELU_K1 * x * x * x))
    return 0.5 * x * (1.0 + th)


def _gelu_grad(x):
    th = jnp.tanh(_GELU_K0 * (x + _GELU_K1 * x * x * x))
    return 0.5 * (1.0 + th) + 0.5 * x * (1.0 - th * th) * _GELU_K0 * (1.0 + 3.0 * _GELU_K1 * x * x)


def _sigmoid(x):
    return 1.0 / (1.0 + jnp.exp(-x))


def _rope(q, c, s):
    n = q.shape[1]
    lane = lax.broadcasted_iota(jnp.int32, q.shape, 1) % HEAD_SLOT
    sw = jnp.where(lane < KR_LANE + QK_ROPE // 2, pltpu.roll(q, n - QK_ROPE // 2, 1), pltpu.roll(q, QK_ROPE // 2, 1))
    return q * c + sw * s


def _rope_bwd(dy, c, s):
    n = dy.shape[1]
    t = dy * s
    lane = lax.broadcasted_iota(jnp.int32, dy.shape, 1) % HEAD_SLOT
    sw = jnp.where(lane < KR_LANE + QK_ROPE // 2, pltpu.roll(t, n - QK_ROPE // 2, 1), pltpu.roll(t, QK_ROPE // 2, 1))
    rope_lane = jnp.logical_and(lane >= KR_LANE, lane < KR_LANE + QK_ROPE)
    return dy * c + jnp.where(rope_lane, sw, 0.0)


def _shift_down(x, k, halo):
    xs = pltpu.roll(x, k, 0)
    hs = pltpu.roll(halo, k, 0)
    rows = lax.broadcasted_iota(jnp.int32, halo.shape, 0)
    top = jnp.where(rows < k, hs, xs[0:SUBLANES])
    return jnp.concatenate([top, xs[SUBLANES:]], axis=0)


def _shift_up(x, k, halo):
    t = x.shape[0]
    xs = pltpu.roll(x, t - k, 0)
    hs = pltpu.roll(halo, SUBLANES - k, 0)
    rows = lax.broadcasted_iota(jnp.int32, halo.shape, 0)
    bot = jnp.where(rows >= SUBLANES - k, hs, xs[t - SUBLANES:])
    return jnp.concatenate([xs[:t - SUBLANES], bot], axis=0)


def _mm(a, b, name, out_dtype=F32, bt=False, b_col0=0, n=None, tm_cap=512, tn_cap=1408, a_lead=None):
    m, k = a.shape[-2:]
    if bt:
        n_full = b.shape[0]
        n = n_full
    else:
        n = b.shape[1] if n is None else n
    tm = min(tm_cap, m)
    tn = _tile(n, tn_cap)

    def body(a_ref, b_ref, o_ref):
        if bt:
            o_ref[...] = _dot_nt(a_ref[...], b_ref[...]).astype(out_dtype)
        else:
            o_ref[...] = _dot(a_ref[...], b_ref[...]).astype(out_dtype)

    if bt:
        b_spec = pl.BlockSpec((tn, k), lambda j, i: (j, b_col0))
    else:
        off = b_col0 * (n // tn)
        b_spec = pl.BlockSpec((k, tn), lambda j, i: (0, off + j))
    if a_lead is None:
        a_spec = pl.BlockSpec((tm, k), lambda j, i: (i, 0))
    else:
        a_spec = pl.BlockSpec((None, tm, k), lambda j, i: (a_lead, i, 0))
    return pl.pallas_call(
        body, name=name, grid=(n // tn, m // tm),
        in_specs=[a_spec, b_spec],
        out_specs=pl.BlockSpec((tm, tn), lambda j, i: (i, j)),
        out_shape=jax.ShapeDtypeStruct((m, n), out_dtype),
        compiler_params=_params(("parallel", "parallel")),
    )(a, b)


def _mm_tn(a, b, name, tk_cap=512, tn_cap=1664, tl_cap=1024, chips=False):
    l, k = a.shape
    tk = _tile(k, tk_cap)
    tl = min(tl_cap, l)

    def body(a_ref, b_ref, o_ref):
        @pl.when(pl.program_id(2) == 0)
        def _():
            o_ref[...] = jnp.zeros_like(o_ref)

        o_ref[...] += _dot_tn(a_ref[...], b_ref[...])

    if chips:
        n = b.shape[-1] * (b.shape[0] if b.ndim == 3 else 1)
        tn = n // N_CHIPS
        assert tn % LANES == 0
        if b.ndim == 3:
            per = N_CHIPS // b.shape[0]
            b_spec = pl.BlockSpec((None, tl, tn), lambda i, j, r: (j // per, r, j % per))
        else:
            b_spec = pl.BlockSpec((tl, tn), lambda i, j, r: (r, j))
        out_spec = pl.BlockSpec((None, tk, tn), lambda i, j, r: (j, i, 0))
        out_shape = jax.ShapeDtypeStruct((N_CHIPS, k, tn), F32)
    else:
        n = b.shape[1]
        tn = _tile(n, tn_cap)
        b_spec = pl.BlockSpec((tl, tn), lambda i, j, r: (r, j))
        out_spec = pl.BlockSpec((tk, tn), lambda i, j, r: (i, j))
        out_shape = jax.ShapeDtypeStruct((k, n), F32)
    return pl.pallas_call(
        body, name=name, grid=(k // tk, n // tn, l // tl),
        in_specs=[pl.BlockSpec((tl, tk), lambda i, j, r: (r, i)), b_spec],
        out_specs=out_spec, out_shape=out_shape,
        compiler_params=_params(("parallel", "parallel", "arbitrary")),
    )(a, b)


def _row(tl, n):
    return pl.BlockSpec((tl, n), lambda i: (i, 0))


def _const(shape):
    return pl.BlockSpec(shape, lambda i: tuple(0 for _ in shape))


def _proj_fwd(x, g1, win, gq, wuq, gkv, wukv, rc, rs, bg, tl):
    l = x.shape[0]

    def body(x_ref, g1_ref, win_ref, gq_ref, wuq_ref, gkv_ref, wukv_ref, rc_ref, rs_ref, bg_ref,
             hn_ref, cq_ref, ckv_ref, q_ref, k_ref, v_ref, u_ref, gl_ref):
        hn, _ = _rms(x_ref[...], g1_ref[...])
        hnb = hn.astype(BF16)
        hn_ref[...] = hnb
        proj = _dot(hnb, win_ref[...])
        cq = proj[:, P_CQ:P_CKV]
        ckv = proj[:, P_CKV:P_KR]
        kr = proj[:, P_KR:P_U]
        cq_ref[...] = cq
        ckv_ref[...] = ckv
        u_ref[...] = proj[:, P_U:P_GL]
        gl_ref[...] = proj[:, P_GL:P_END] + bg_ref[...]
        qn, _ = _rms(cq, gq_ref[...])
        q = _dot(qn.astype(BF16), wuq_ref[...])
        c1 = rc_ref[...]
        s1 = rs_ref[...]
        q_ref[...] = (_rope(q, jnp.tile(c1, (1, N_HEADS)), jnp.tile(s1, (1, N_HEADS))) * Q_PRESCALE).astype(BF16)
        ckvn, _ = _rms(ckv, gkv_ref[...])
        kv = _dot(ckvn.astype(BF16), wukv_ref[...])
        krr = _rope(kr, c1, s1)
        k_ref[...] = (kv[:, :HP] + jnp.tile(krr, (1, N_HEADS))).astype(BF16)
        v_ref[...] = kv[:, HP:].astype(BF16)

    outs = [(D_MODEL, BF16), (Q_RANK, F32), (KV_RANK, F32), (HP, BF16), (HP, BF16), (HP, BF16),
            (SSM_W, F32), (2 * D_MODEL, F32)]
    return pl.pallas_call(
        body, name="proj_fwd", grid=(l // tl,),
        in_specs=[_row(tl, D_MODEL), _const((1, D_MODEL)), _const((D_MODEL, P_END)), _const((1, Q_RANK)),
                  _const((Q_RANK, HP)), _const((1, KV_RANK)), _const((KV_RANK, 2 * HP)),
                  _row(tl, HEAD_SLOT), _row(tl, HEAD_SLOT), _const((1, 2 * D_MODEL))],
        out_specs=[_row(tl, n) for n, _ in outs],
        out_shape=[jax.ShapeDtypeStruct((l, n), dt) for n, dt in outs],
        compiler_params=_params(("parallel",)),
    )(x, g1, win, gq, wuq, gkv, wukv, rc, rs, bg)


_NEG = -1e30


LOG2E = 1.0 / math.log(2.0)
LN2 = math.log(2.0)
ATTN_SCALE = 1.0 / math.sqrt(QK_HEAD)
Q_PRESCALE = ATTN_SCALE * LOG2E
HEADS_PER_STEP = 2
PAIR_W = HEADS_PER_STEP * HEAD_SLOT


def _causal_pairs(nq, by_query):
    if by_query:
        pairs = [(i, j) for i in range(nq) for j in range(i + 1)]
    else:
        pairs = [(i, j) for j in range(nq) for i in range(j, nq)]
    return jnp.array([p[0] for p in pairs], jnp.int32), jnp.array([p[1] for p in pairs], jnp.int32)


def _diag_mask_t(s):
    rows = lax.broadcasted_iota(jnp.int32, s.shape, 0)
    cols = lax.broadcasted_iota(jnp.int32, s.shape, 1)
    return jnp.where(rows <= cols, s, _NEG)


def _attn_fwd(q, k, v, tq):
    l = q.shape[0]
    nq = l // tq
    it, jt = _causal_pairs(nq, True)

    def body(it_ref, jt_ref, q_ref, k_ref, v_ref, o_ref, lse_ref, m_ref, l_ref, acc_ref):
        t = pl.program_id(1)
        i = it_ref[t]
        j = jt_ref[t]

        @pl.when(j == 0)
        def _():
            m_ref[...] = jnp.full_like(m_ref, _NEG)
            l_ref[...] = jnp.zeros_like(l_ref)
            acc_ref[...] = jnp.zeros_like(acc_ref)

        def update(on_diagonal):
            for hh in range(HEADS_PER_STEP):
                sl = slice(hh * HEAD_SLOT, (hh + 1) * HEAD_SLOT)
                s = _dot_nt(k_ref[:, sl], q_ref[:, sl])
                if on_diagonal:
                    s = _diag_mask_t(s)
                m_old = m_ref[hh]
                m_new = jnp.maximum(m_old, jnp.max(s, axis=0, keepdims=True))
                p = jnp.exp2(s - m_new)
                alpha = jnp.exp2(m_old - m_new)
                l_ref[hh] = alpha * l_ref[hh] + jnp.sum(p, axis=0, keepdims=True)
                acc_ref[hh] = alpha * acc_ref[hh] + _dot_tn(v_ref[:, sl], p.astype(BF16))
                m_ref[hh] = m_new

        @pl.when(j < i)
        def _():
            update(False)

        @pl.when(j == i)
        def _():
            update(True)
            for hh in range(HEADS_PER_STEP):
                sl = slice(hh * HEAD_SLOT, (hh + 1) * HEAD_SLOT)
                o_ref[:, sl] = (acc_ref[hh] / l_ref[hh]).T.astype(BF16)
                lse_ref[hh] = m_ref[hh] + jnp.log(l_ref[hh]) * LOG2E

    blk = (tq, PAIR_W)
    qmap = lambda h, t, it_ref, jt_ref: (it_ref[t], h)
    kmap = lambda h, t, it_ref, jt_ref: (jt_ref[t], h)
    row = pl.BlockSpec((HEADS_PER_STEP, 1, tq), lambda h, t, it_ref, jt_ref: (h, 0, it_ref[t]))
    return pl.pallas_call(
        body, name="attn_fwd",
        grid_spec=pltpu.PrefetchScalarGridSpec(
            num_scalar_prefetch=2, grid=(N_HEADS // HEADS_PER_STEP, it.shape[0]),
            in_specs=[pl.BlockSpec(blk, qmap), pl.BlockSpec(blk, kmap), pl.BlockSpec(blk, kmap)],
            out_specs=[pl.BlockSpec(blk, qmap), row],
            scratch_shapes=[pltpu.VMEM((HEADS_PER_STEP, 1, tq), F32), pltpu.VMEM((HEADS_PER_STEP, 1, tq), F32),
                            pltpu.VMEM((HEADS_PER_STEP, HEAD_SLOT, tq), F32)]),
        out_shape=[jax.ShapeDtypeStruct((l, HP), BF16), jax.ShapeDtypeStruct((N_HEADS, 1, l), F32)],
        compiler_params=_params(("parallel", "arbitrary")),
    )(it, jt, q, k, v)


def _attn_delta(o, do, tq):
    l = o.shape[0]

    def body(o_ref, do_ref, d_ref):
        prod = o_ref[...].astype(F32) * do_ref[...].astype(F32)
        d_ref[0] = jnp.sum(prod.T, axis=0, keepdims=True)

    blk = pl.BlockSpec((tq, HEAD_SLOT), lambda h, i: (i, h))
    return pl.pallas_call(
        body, name="attn_delta", grid=(N_HEADS, l // tq), in_specs=[blk, blk],
        out_specs=pl.BlockSpec((1, 1, tq), lambda h, i: (h, 0, i)),
        out_shape=jax.ShapeDtypeStruct((N_HEADS, 1, l), F32),
        compiler_params=_params(("parallel", "parallel")),
    )(o, do)


def _attn_bwd(q, k, v, do, lse, delta, tq):
    l = q.shape[0]
    nq = l // tq
    it, jt = _causal_pairs(nq, False)

    def body(it_ref, jt_ref, q_ref, k_ref, v_ref, do_ref, lse_ref, dl_ref, dq_ref, dk_ref, dv_ref, dka_ref, dva_ref):
        t = pl.program_id(1)
        i = it_ref[t]
        j = jt_ref[t]

        @pl.when(t == 0)
        def _():
            dq_ref[...] = jnp.zeros_like(dq_ref)

        @pl.when(i == j)
        def _():
            dka_ref[...] = jnp.zeros_like(dka_ref)
            dva_ref[...] = jnp.zeros_like(dva_ref)

        def update(on_diagonal):
            r0 = pl.multiple_of(i * tq, tq)
            for hh in range(HEADS_PER_STEP):
                sl = slice(hh * HEAD_SLOT, (hh + 1) * HEAD_SLOT)
                qb = q_ref[:, sl]
                kb = k_ref[:, sl]
                dob = do_ref[:, sl]
                s = _dot_nt(kb, qb)
                if on_diagonal:
                    s = _diag_mask_t(s)
                p = jnp.exp2(s - lse_ref[hh])
                dva_ref[:, sl] += _dot(p.astype(BF16), dob)
                dp = _dot_nt(v_ref[:, sl], dob)
                ds = (p * (dp - dl_ref[hh])).astype(BF16)
                dka_ref[:, sl] += _dot(ds, qb)
                dq_ref[pl.ds(r0, tq), sl] += ATTN_SCALE * _dot_tn(ds, kb)

        @pl.when(j < i)
        def _():
            update(False)

        @pl.when(j == i)
        def _():
            update(True)

        @pl.when(i == nq - 1)
        def _():
            dk_ref[...] = (dka_ref[...] * LN2).astype(BF16)
            dv_ref[...] = dva_ref[...].astype(BF16)

    blk = (tq, PAIR_W)
    qmap = lambda h, t, it_ref, jt_ref: (it_ref[t], h)
    kmap = lambda h, t, it_ref, jt_ref: (jt_ref[t], h)
    row = pl.BlockSpec((HEADS_PER_STEP, 1, tq), lambda h, t, it_ref, jt_ref: (h, 0, it_ref[t]))
    return pl.pallas_call(
        body, name="attn_bwd",
        grid_spec=pltpu.PrefetchScalarGridSpec(
            num_scalar_prefetch=2, grid=(N_HEADS // HEADS_PER_STEP, it.shape[0]),
            in_specs=[pl.BlockSpec(blk, qmap), pl.BlockSpec(blk, kmap), pl.BlockSpec(blk, kmap),
                      pl.BlockSpec(blk, qmap), row, row],
            out_specs=[pl.BlockSpec((l, PAIR_W), lambda h, t, it_ref, jt_ref: (0, h)), pl.BlockSpec(blk, kmap),
                       pl.BlockSpec(blk, kmap)],
            scratch_shapes=[pltpu.VMEM(blk, F32), pltpu.VMEM(blk, F32)]),
        out_shape=[jax.ShapeDtypeStruct((l, HP), F32), jax.ShapeDtypeStruct((l, HP), BF16),
                   jax.ShapeDtypeStruct((l, HP), BF16)],
        compiler_params=_params(("parallel", "arbitrary")),
    )(it, jt, q, k, v, do, lse, delta)


SSM_CB = 512
SSM_UB = 128
SSM_NB = SSM_CH // SSM_CB


def _scan_tiles(re_ref, im_ref, tab, carry, n_tiles, reverse):
    def tile(n, c):
        cr, ci = c
        idx = (n_tiles - 1 - n) if reverse else n
        r0 = pl.multiple_of(idx * SUBLANES, SUBLANES)
        sr = re_ref[pl.ds(r0, SUBLANES), :]
        si = im_ref[pl.ds(r0, SUBLANES), :]
        for step, k in enumerate((1, 2, 4)):
            mr, mi = tab[2 * step], tab[2 * step + 1]
            sh = (SUBLANES - k) if reverse else k
            rr = pltpu.roll(sr, sh, 0)
            ri = pltpu.roll(si, sh, 0)
            sr, si = sr + mr * rr - mi * ri, si + mr * ri + mi * rr
        pr, pi = tab[6], tab[7]
        sr, si = sr + pr * cr - pi * ci, si + pr * ci + pi * cr
        re_ref[pl.ds(r0, SUBLANES), :] = sr
        im_ref[pl.ds(r0, SUBLANES), :] = si
        if reverse:
            return sr[0:1, :], si[0:1, :]
        return sr[SUBLANES - 1:SUBLANES, :], si[SUBLANES - 1:SUBLANES, :]

    return lax.fori_loop(0, n_tiles, tile, carry, unroll=2)


def _ssm_fwd(u, bre, bim, cre, cim, dvec, tab, tt):
    l = u.shape[0]
    nt = l // tt

    def body(u_ref, bre_ref, bim_ref, cre_ref, cim_ref, d_ref, tab_ref, y_ref, sre_ref, sim_ref, car_ref):
        @pl.when(pl.program_id(1) == 0)
        def _():
            car_ref[...] = jnp.zeros_like(car_ref)

        uf = u_ref[...]
        ub = uf.astype(BF16)
        sre_ref[...] = _dot(ub, bre_ref[0])
        sim_ref[...] = _dot(ub, bim_ref[0])
        tab_v = [tab_ref[n] for n in range(8)]
        cr, ci = _scan_tiles(sre_ref, sim_ref, tab_v, (car_ref[0:1, :], car_ref[8:9, :]), tt // SUBLANES, False)
        car_ref[0:1, :] = cr
        car_ref[8:9, :] = ci
        y_ref[...] = (_dot(sre_ref[...].astype(BF16), cre_ref[0]) - _dot(sim_ref[...].astype(BF16), cim_ref[0])
                      + d_ref[...] * uf)

    return pl.pallas_call(
        body, name="ssm_fwd", grid=(SSM_NB, nt),
        in_specs=[pl.BlockSpec((tt, SSM_UB), lambda m, t: (t, m)),
                  pl.BlockSpec((1, SSM_UB, SSM_CB), lambda m, t: (m, 0, 0)),
                  pl.BlockSpec((1, SSM_UB, SSM_CB), lambda m, t: (m, 0, 0)),
                  pl.BlockSpec((1, SSM_CB, SSM_UB), lambda m, t: (m, 0, 0)),
                  pl.BlockSpec((1, SSM_CB, SSM_UB), lambda m, t: (m, 0, 0)),
                  pl.BlockSpec((1, SSM_UB), lambda m, t: (0, m)),
                  pl.BlockSpec((8, SUBLANES, SSM_CB), lambda m, t: (0, 0, m))],
        out_specs=[pl.BlockSpec((tt, SSM_UB), lambda m, t: (t, m)),
                   pl.BlockSpec((tt, SSM_CB), lambda m, t: (t, m)),
                   pl.BlockSpec((tt, SSM_CB), lambda m, t: (t, m))],
        out_shape=[jax.ShapeDtypeStruct((l, SSM_W), F32), jax.ShapeDtypeStruct((l, SSM_CH), F32),
                   jax.ShapeDtypeStruct((l, SSM_CH), F32)],
        scratch_shapes=[pltpu.VMEM((2 * SUBLANES, SSM_CB), F32)],
        compiler_params=_params(("parallel", "arbitrary")),
    )(u, bre, bim, cre, cim, dvec, tab)


def _ssm_bwd(dy, u, sre, sim, bre, bim, cre, cim, dvec, tab, tt):
    l = u.shape[0]
    nt = l // tt
    tpb = tt // SUBLANES

    def body(dy_ref, u_ref, sre_ref, sim_ref, hre_ref, him_ref, bre_ref, bim_ref, cre_ref, cim_ref, d_ref, tab_ref,
             du_ref, dbre_ref, dbim_ref, dcre_ref, dcim_ref, dare_ref, daim_ref, dd_ref, lr_ref, li_ref, car_ref):
        t = pl.program_id(1)

        @pl.when(t == 0)
        def _():
            car_ref[...] = jnp.zeros_like(car_ref)
            for ref in (dbre_ref, dbim_ref, dcre_ref, dcim_ref, dare_ref, daim_ref, dd_ref):
                ref[...] = jnp.zeros_like(ref)

        dyf = dy_ref[...]
        dyb = dyf.astype(BF16)
        uf = u_ref[...]
        s_re = sre_ref[...]
        s_im = sim_ref[...]
        lr_ref[...] = _dot_nt(dyb, cre_ref[0])
        li_ref[...] = -_dot_nt(dyb, cim_ref[0])
        dcre_ref[0] += _dot_tn(s_re.astype(BF16), dyb)
        dcim_ref[0] -= _dot_tn(s_im.astype(BF16), dyb)
        tab_v = [tab_ref[n] for n in range(8)]
        cr, ci = _scan_tiles(lr_ref, li_ref, tab_v, (car_ref[0:1, :], car_ref[8:9, :]), tpb, True)
        car_ref[0:1, :] = cr
        car_ref[8:9, :] = ci
        lam_r = lr_ref[...]
        lam_i = li_ref[...]
        keep = jnp.where(t == nt - 1, 0.0, 1.0)
        sp_r = _shift_down(s_re, 1, hre_ref[...] * keep)
        sp_i = _shift_down(s_im, 1, him_ref[...] * keep)
        dare_ref[...] += jnp.sum(lam_r * sp_r + lam_i * sp_i, axis=0, keepdims=True)
        daim_ref[...] += jnp.sum(lam_i * sp_r - lam_r * sp_i, axis=0, keepdims=True)
        lrb = lam_r.astype(BF16)
        lib = lam_i.astype(BF16)
        du_ref[...] = _dot_nt(lrb, bre_ref[0]) + _dot_nt(lib, bim_ref[0]) + dyf * d_ref[...]
        ub = uf.astype(BF16)
        dbre_ref[0] += _dot_tn(ub, lrb)
        dbim_ref[0] += _dot_tn(ub, lib)
        dd_ref[...] += jnp.sum(dyf * uf, axis=0, keepdims=True)

    rev = lambda m, t: (nt - 1 - t, m)
    halo = lambda m, t: (jnp.maximum((nt - 1 - t) * tpb - 1, 0), m)
    wb = pl.BlockSpec((1, SSM_UB, SSM_CB), lambda m, t: (m, 0, 0))
    wc = pl.BlockSpec((1, SSM_CB, SSM_UB), lambda m, t: (m, 0, 0))
    vec_c = pl.BlockSpec((1, SSM_CB), lambda m, t: (0, m))
    vec_u = pl.BlockSpec((1, SSM_UB), lambda m, t: (0, m))
    return pl.pallas_call(
        body, name="ssm_bwd", grid=(SSM_NB, nt),
        in_specs=[pl.BlockSpec((tt, SSM_UB), rev), pl.BlockSpec((tt, SSM_UB), rev),
                  pl.BlockSpec((tt, SSM_CB), rev), pl.BlockSpec((tt, SSM_CB), rev),
                  pl.BlockSpec((SUBLANES, SSM_CB), halo), pl.BlockSpec((SUBLANES, SSM_CB), halo),
                  wb, wb, wc, wc, vec_u,
                  pl.BlockSpec((8, SUBLANES, SSM_CB), lambda m, t: (0, 0, m))],
        out_specs=[pl.BlockSpec((tt, SSM_UB), rev), wb, wb, wc, wc, vec_c, vec_c, vec_u],
        out_shape=[jax.ShapeDtypeStruct((l, SSM_W), F32),
                   jax.ShapeDtypeStruct((SSM_NB, SSM_UB, SSM_CB), F32), jax.ShapeDtypeStruct((SSM_NB, SSM_UB, SSM_CB), F32),
                   jax.ShapeDtypeStruct((SSM_NB, SSM_CB, SSM_UB), F32), jax.ShapeDtypeStruct((SSM_NB, SSM_CB, SSM_UB), F32),
                   jax.ShapeDtypeStruct((1, SSM_CH), F32), jax.ShapeDtypeStruct((1, SSM_CH), F32),
                   jax.ShapeDtypeStruct((1, SSM_W), F32)],
        scratch_shapes=[pltpu.VMEM((tt, SSM_CB), F32), pltpu.VMEM((tt, SSM_CB), F32),
                        pltpu.VMEM((2 * SUBLANES, SSM_CB), F32)],
        compiler_params=_params(("parallel", "arbitrary")),
    )(dy, u, sre, sim, sre, sim, bre, bim, cre, cim, dvec, tab)


def _merge_fwd(x, gl, attn, y1, wba, wbs, wglu, bglu, wout, gpost, gpre, tl):
    l = x.shape[0]

    def body(x_ref, gl_ref, at_ref, y1_ref, wba_ref, wbs_ref, wglu_ref, bglu_ref, wout_ref, gpost_ref, gpre_ref,
             a_ref, sm_ref, mg_ref, z_ref, x1_ref, hn2_ref, y3_ref):
        y2 = _gelu(y1_ref[...])
        sg = _sigmoid(_dot(y2.astype(BF16), wglu_ref[...]) + bglu_ref[...])
        y3 = (y2 * sg).astype(BF16)
        y3_ref[...] = y3
        a = _dot(at_ref[...], wba_ref[...])
        sm = _dot(y3, wbs_ref[...])
        a_ref[...] = a
        sm_ref[...] = sm
        g = _sigmoid(gl_ref[...])
        merged = (g[:, :D_MODEL] * a + g[:, D_MODEL:] * sm).astype(BF16)
        mg_ref[...] = merged
        z = _dot(merged, wout_ref[...])
        z_ref[...] = z
        n, _ = _rms(z, gpost_ref[...])
        x1 = x_ref[...] + n
        x1_ref[...] = x1
        hn2, _ = _rms(x1, gpre_ref[...])
        hn2_ref[...] = hn2.astype(BF16)

    outs = [(D_MODEL, F32), (D_MODEL, F32), (D_MODEL, BF16), (D_MODEL, F32), (D_MODEL, F32), (D_MODEL, BF16),
            (SSM_W, BF16)]
    return pl.pallas_call(
        body, name="merge_fwd", grid=(l // tl,),
        in_specs=[_row(tl, D_MODEL), _row(tl, 2 * D_MODEL), _row(tl, HP), _row(tl, SSM_W),
                  _const((HP, D_MODEL)), _const((SSM_W, D_MODEL)), _const((SSM_W, SSM_W)), _const((1, SSM_W)),
                  _const((D_MODEL, D_MODEL)), _const((1, D_MODEL)), _const((1, D_MODEL))],
        out_specs=[_row(tl, n) for n, _ in outs],
        out_shape=[jax.ShapeDtypeStruct((l, n), dt) for n, dt in outs],
        compiler_params=_params(("parallel",)),
    )(x, gl, attn, y1, wba, wbs, wglu, bglu, wout, gpost, gpre)


def _merge_bwd(dhn2a, dhn2b, x1, dx2, z, gl, a, sm, y1, wba, wbs, wglu, bglu, wout, gpost, gpre, tl):
    l = x1.shape[0]

    def body(da_ref, db_ref, x1_ref, dx2_ref, z_ref, gl_ref, a_ref, sm_ref, y1_ref,
             wba_ref, wbs_ref, wglu_ref, bglu_ref, wout_ref, gpost_ref, gpre_ref,
             dx1_ref, dz_ref, dbra_ref, dbrs_ref, dgl_ref, dat_ref, dy1_ref, dt_ref, y2_ref,
             dgpre_ref, dgpost_ref, dbg_ref, dbglu_ref):
        @pl.when(pl.program_id(0) == 0)
        def _():
            for ref in (dgpre_ref, dgpost_ref, dbg_ref, dbglu_ref):
                ref[...] = jnp.zeros_like(ref)

        dhn2 = da_ref[...] + db_ref[...]
        dx1a, dgpre = _rms_bwd(dhn2, x1_ref[...], gpre_ref[...])
        dgpre_ref[...] += dgpre
        dx1 = dx2_ref[...] + dx1a
        dx1_ref[...] = dx1
        dz, dgpost = _rms_bwd(dx1, z_ref[...], gpost_ref[...])
        dgpost_ref[...] += dgpost
        dzb = dz.astype(BF16)
        dz_ref[...] = dzb
        dm = _dot_nt(dzb, wout_ref[...])
        g = _sigmoid(gl_ref[...])
        g0 = g[:, :D_MODEL]
        g1 = g[:, D_MODEL:]
        dbra = (dm * g0).astype(BF16)
        dbrs = (dm * g1).astype(BF16)
        dbra_ref[...] = dbra
        dbrs_ref[...] = dbrs
        dgl0 = dm * a_ref[...] * g0 * (1.0 - g0)
        dgl1 = dm * sm_ref[...] * g1 * (1.0 - g1)
        dgl_ref[:, :D_MODEL] = dgl0.astype(BF16)
        dgl_ref[:, D_MODEL:] = dgl1.astype(BF16)
        dbg_ref[:, :D_MODEL] += jnp.sum(dgl0, axis=0, keepdims=True)
        dbg_ref[:, D_MODEL:] += jnp.sum(dgl1, axis=0, keepdims=True)
        dat_ref[...] = _dot_nt(dbra, wba_ref[...]).astype(BF16)
        dy3 = _dot_nt(dbrs, wbs_ref[...])
        y1v = y1_ref[...]
        y2 = _gelu(y1v)
        y2b = y2.astype(BF16)
        y2_ref[...] = y2b
        sg = _sigmoid(_dot(y2b, wglu_ref[...]) + bglu_ref[...])
        dt = dy3 * y2 * sg * (1.0 - sg)
        dtb = dt.astype(BF16)
        dt_ref[...] = dtb
        dbglu_ref[...] += jnp.sum(dt, axis=0, keepdims=True)
        dy2 = dy3 * sg + _dot_nt(dtb, wglu_ref[...])
        dy1_ref[...] = dy2 * _gelu_grad(y1v)

    outs = [(D_MODEL, F32), (D_MODEL, BF16), (D_MODEL, BF16), (D_MODEL, BF16), (2 * D_MODEL, BF16), (HP, BF16),
            (SSM_W, F32), (SSM_W, BF16), (SSM_W, BF16)]
    accs = [D_MODEL, D_MODEL, 2 * D_MODEL, SSM_W]
    return pl.pallas_call(
        body, name="merge_bwd", grid=(l // tl,),
        in_specs=[_row(tl, D_MODEL), _row(tl, D_MODEL), _row(tl, D_MODEL), _row(tl, D_MODEL), _row(tl, D_MODEL),
                  _row(tl, 2 * D_MODEL), _row(tl, D_MODEL), _row(tl, D_MODEL), _row(tl, SSM_W),
                  _const((HP, D_MODEL)), _const((SSM_W, D_MODEL)), _const((SSM_W, SSM_W)), _const((1, SSM_W)),
                  _const((D_MODEL, D_MODEL)), _const((1, D_MODEL)), _const((1, D_MODEL))],
        out_specs=[_row(tl, n) for n, _ in outs] + [_const((1, n)) for n in accs],
        out_shape=[jax.ShapeDtypeStruct((l, n), dt) for n, dt in outs]
        + [jax.ShapeDtypeStruct((1, n), F32) for n in accs],
        compiler_params=_params(("arbitrary",)),
    )(dhn2a, dhn2b, x1, dx2, z, gl, a, sm, y1, wba, wbs, wglu, bglu, wout, gpost, gpre)


def _proj_bwd(x, dx1, cq, ckv, dq, dk, dv, du, dgl, g1, win, gq, wuq, gkv, wukv, rc, rs, tl):
    l = x.shape[0]

    def body(x_ref, dx1_ref, cq_ref, ckv_ref, dq_ref, dk_ref, dv_ref, du_ref, dgl_ref,
             g1_ref, win_ref, gq_ref, wuq_ref, gkv_ref, wukv_ref, rc_ref, rs_ref,
             gx_ref, dql_ref, qn_ref, ckvn_ref, dproj_ref, dg1_ref, dgq_ref, dgkv_ref):
        @pl.when(pl.program_id(0) == 0)
        def _():
            for ref in (dg1_ref, dgq_ref, dgkv_ref):
                ref[...] = jnp.zeros_like(ref)

        c1 = rc_ref[...]
        s1 = rs_ref[...]
        dql = _rope_bwd(dq_ref[...], jnp.tile(c1, (1, N_HEADS)), jnp.tile(s1, (1, N_HEADS))).astype(BF16)
        dql_ref[...] = dql
        dqn = _dot_nt(dql, wuq_ref[...])
        cq = cq_ref[...]
        qn, _ = _rms(cq, gq_ref[...])
        qn_ref[...] = qn.astype(BF16)
        dcq, dgq = _rms_bwd(dqn, cq, gq_ref[...])
        dgq_ref[...] += dgq
        dkb = dk_ref[...]
        dvb = dv_ref[...]
        dkf = dkb.astype(F32)
        dkr = dkf[:, 0:HEAD_SLOT]
        for h in range(1, N_HEADS):
            dkr = dkr + dkf[:, h * HEAD_SLOT:(h + 1) * HEAD_SLOT]
        dkr = _rope_bwd(dkr, c1, s1)
        dckvn = _dot_nt(dkb, wukv_ref[:, :HP]) + _dot_nt(dvb, wukv_ref[:, HP:])
        ckv = ckv_ref[...]
        ckvn, _ = _rms(ckv, gkv_ref[...])
        ckvn_ref[...] = ckvn.astype(BF16)
        dckv, dgkv = _rms_bwd(dckvn, ckv, gkv_ref[...])
        dgkv_ref[...] += dgkv
        dproj_ref[:, P_CQ:P_CKV] = dcq.astype(BF16)
        dproj_ref[:, P_CKV:P_KR] = dckv.astype(BF16)
        dproj_ref[:, P_KR:P_U] = dkr.astype(BF16)
        dproj_ref[:, P_U:P_GL] = du_ref[...].astype(BF16)
        dproj_ref[:, P_GL:P_END] = dgl_ref[...]
        dhn = _dot_nt(dproj_ref[...], win_ref[...])
        dxa, dg1 = _rms_bwd(dhn, x_ref[...], g1_ref[...])
        dg1_ref[...] += dg1
        gx_ref[...] = dx1_ref[...] + dxa

    outs = [(D_MODEL, F32), (HP, BF16), (Q_RANK, BF16), (KV_RANK, BF16), (P_END, BF16)]
    accs = [D_MODEL, Q_RANK, KV_RANK]
    return pl.pallas_call(
        body, name="proj_bwd", grid=(l // tl,),
        in_specs=[_row(tl, D_MODEL), _row(tl, D_MODEL), _row(tl, Q_RANK), _row(tl, KV_RANK), _row(tl, HP),
                  _row(tl, HP), _row(tl, HP), _row(tl, SSM_W), _row(tl, 2 * D_MODEL),
                  _const((1, D_MODEL)), _const((D_MODEL, P_END)), _const((1, Q_RANK)), _const((Q_RANK, HP)),
                  _const((1, KV_RANK)), _const((KV_RANK, 2 * HP)), _row(tl, HEAD_SLOT), _row(tl, HEAD_SLOT)],
        out_specs=[_row(tl, n) for n, _ in outs] + [_const((1, n)) for n in accs],
        out_shape=[jax.ShapeDtypeStruct((l, n), dt) for n, dt in outs]
        + [jax.ShapeDtypeStruct((1, n), F32) for n in accs],
        compiler_params=_params(("arbitrary",)),
    )(x, dx1, cq, ckv, dq, dk, dv, du, dgl, g1, win, gq, wuq, gkv, wukv, rc, rs)


CONV_CB = 256
CONV_NB = D_FF // CONV_CB


def _conv3(h, halo, w, b):
    return b + w[0:1, :] * _shift_down(h, 2, halo) + w[1:2, :] * _shift_down(h, 1, halo) + w[2:3, :] * h


def _conv_fwd(h, cw, cb, tl):
    l = h.shape[0]

    def body(hg_ref, hv_ref, wg_ref, wv_ref, bg_ref, bv_ref, act_ref, halo_ref):
        @pl.when(pl.program_id(1) == 0)
        def _():
            halo_ref[...] = jnp.zeros_like(halo_ref)

        hg = hg_ref[...]
        hv = hv_ref[...]
        cg = _conv3(hg, halo_ref[0:SUBLANES, :], wg_ref[...], bg_ref[...])
        cv = _conv3(hv, halo_ref[SUBLANES:, :], wv_ref[...], bv_ref[...])
        act_ref[...] = (_gelu(cg) * cv).astype(BF16)
        halo_ref[0:SUBLANES, :] = hg[tl - SUBLANES:, :]
        halo_ref[SUBLANES:, :] = hv[tl - SUBLANES:, :]

    gmap = lambda c, r: (r, c)
    vmap = lambda c, r: (r, CONV_NB + c)
    return pl.pallas_call(
        body, name="conv_fwd", grid=(CONV_NB, l // tl),
        in_specs=[pl.BlockSpec((tl, CONV_CB), gmap), pl.BlockSpec((tl, CONV_CB), vmap),
                  pl.BlockSpec((3, CONV_CB), lambda c, r: (0, c)), pl.BlockSpec((3, CONV_CB), lambda c, r: (0, CONV_NB + c)),
                  pl.BlockSpec((1, CONV_CB), lambda c, r: (0, c)), pl.BlockSpec((1, CONV_CB), lambda c, r: (0, CONV_NB + c))],
        out_specs=pl.BlockSpec((tl, CONV_CB), gmap),
        out_shape=jax.ShapeDtypeStruct((l, D_FF), BF16),
        scratch_shapes=[pltpu.VMEM((2 * SUBLANES, CONV_CB), F32)],
        compiler_params=_params(("parallel", "arbitrary")),
    )(h, h, cw, cw, cb, cb)


def _conv_bwd(h, dact, cw, cb, tl):
    l = h.shape[0]
    nr = l // tl
    tpb = tl // SUBLANES

    def body(hg_ref, hv_ref, hgh_ref, hvh_ref, da_ref, wg_ref, wv_ref, bg_ref, bv_ref,
             dh_ref, dwg_ref, dwv_ref, dbg_ref, dbv_ref, car_ref):
        r = pl.program_id(1)

        @pl.when(r == 0)
        def _():
            for ref in (car_ref, dwg_ref, dwv_ref, dbg_ref, dbv_ref):
                ref[...] = jnp.zeros_like(ref)

        keep = jnp.where(r == nr - 1, 0.0, 1.0)
        da = da_ref[...].astype(F32)

        def half(h_ref, halo, w, b):
            hh = h_ref[...]
            h1 = _shift_down(hh, 1, halo)
            h2 = _shift_down(hh, 2, halo)
            return hh, h1, h2, b + w[0:1, :] * h2 + w[1:2, :] * h1 + w[2:3, :] * hh

        wg = wg_ref[...]
        wv = wv_ref[...]
        hg, hg1, hg2, cg = half(hg_ref, hgh_ref[...] * keep, wg, bg_ref[...])
        hv, hv1, hv2, cv = half(hv_ref, hvh_ref[...] * keep, wv, bv_ref[...])
        dcg = da * cv * _gelu_grad(cg)
        dcv = da * _gelu(cg)

        def back(dc, hh, h1, h2, w, nxt, dw_ref, db_ref, part):
            db_ref[...] += jnp.sum(dc, axis=0, keepdims=True)
            dw_ref[0:1, :] += jnp.sum(dc * h2, axis=0, keepdims=True)
            dw_ref[1:2, :] += jnp.sum(dc * h1, axis=0, keepdims=True)
            dw_ref[2:3, :] += jnp.sum(dc * hh, axis=0, keepdims=True)
            dh = w[2:3, :] * dc + w[1:2, :] * _shift_up(dc, 1, nxt) + w[0:1, :] * _shift_up(dc, 2, nxt)
            dh_ref[part] = dh.astype(BF16)

        back(dcg, hg, hg1, hg2, wg, car_ref[0:SUBLANES, :], dwg_ref, dbg_ref, 0)
        back(dcv, hv, hv1, hv2, wv, car_ref[SUBLANES:, :], dwv_ref, dbv_ref, 1)
        car_ref[0:SUBLANES, :] = dcg[0:SUBLANES, :]
        car_ref[SUBLANES:, :] = dcv[0:SUBLANES, :]

    grev = lambda c, r: (nr - 1 - r, c)
    vrev = lambda c, r: (nr - 1 - r, CONV_NB + c)
    ghalo = lambda c, r: (jnp.maximum((nr - 1 - r) * tpb - 1, 0), c)
    vhalo = lambda c, r: (jnp.maximum((nr - 1 - r) * tpb - 1, 0), CONV_NB + c)
    colg = lambda c, r: (0, c)
    colv = lambda c, r: (0, CONV_NB + c)
    return pl.pallas_call(
        body, name="conv_bwd", grid=(CONV_NB, nr),
        in_specs=[pl.BlockSpec((tl, CONV_CB), grev), pl.BlockSpec((tl, CONV_CB), vrev),
                  pl.BlockSpec((SUBLANES, CONV_CB), ghalo), pl.BlockSpec((SUBLANES, CONV_CB), vhalo),
                  pl.BlockSpec((tl, CONV_CB), grev),
                  pl.BlockSpec((3, CONV_CB), colg), pl.BlockSpec((3, CONV_CB), colv),
                  pl.BlockSpec((1, CONV_CB), colg), pl.BlockSpec((1, CONV_CB), colv)],
        out_specs=[pl.BlockSpec((2, tl, CONV_CB), lambda c, r: (0, nr - 1 - r, c)),
                   pl.BlockSpec((3, CONV_CB), colg), pl.BlockSpec((3, CONV_CB), colg),
                   pl.BlockSpec((1, CONV_CB), colg), pl.BlockSpec((1, CONV_CB), colg)],
        out_shape=[jax.ShapeDtypeStruct((2, l, D_FF), BF16),
                   jax.ShapeDtypeStruct((3, D_FF), F32), jax.ShapeDtypeStruct((3, D_FF), F32),
                   jax.ShapeDtypeStruct((1, D_FF), F32), jax.ShapeDtypeStruct((1, D_FF), F32)],
        scratch_shapes=[pltpu.VMEM((2 * SUBLANES, CONV_CB), F32)],
        compiler_params=_params(("parallel", "arbitrary")),
    )(h, h, h, h, dact, cw, cw, cb, cb)


def _loss_head(ff, x1, tgt, g, tl):
    l = ff.shape[0]

    def body(ff_ref, x1_ref, tg_ref, g_ref, loss_ref, dx2_ref, dff_ref, dg_ref):
        @pl.when(pl.program_id(0) == 0)
        def _():
            loss_ref[...] = jnp.zeros_like(loss_ref)
            dg_ref[...] = jnp.zeros_like(dg_ref)

        f = ff_ref[...]
        gv = g_ref[...]
        n, _ = _rms(f, gv)
        e = x1_ref[...] + n - tg_ref[...]
        loss_ref[...] += 0.5 * jnp.sum(jnp.mean(e * e, axis=-1, keepdims=True), axis=0, keepdims=True)
        dx2 = e * (1.0 / D_MODEL)
        dx2_ref[...] = dx2
        dff, dg = _rms_bwd(dx2, f, gv)
        dff_ref[...] = dff.astype(BF16)
        dg_ref[...] += dg

    return pl.pallas_call(
        body, name="loss_head", grid=(l // tl,),
        in_specs=[_row(tl, D_MODEL), _row(tl, D_MODEL), _row(tl, D_MODEL), _const((1, D_MODEL))],
        out_specs=[_const((1, LANES)), _row(tl, D_MODEL), _row(tl, D_MODEL), _const((1, D_MODEL))],
        out_shape=[jax.ShapeDtypeStruct((1, LANES), F32), jax.ShapeDtypeStruct((l, D_MODEL), F32),
                   jax.ShapeDtypeStruct((l, D_MODEL), BF16), jax.ShapeDtypeStruct((1, D_MODEL), F32)],
        compiler_params=_params(("arbitrary",)),
    )(ff, x1, tgt, g)


def _ssm_disc(lam_re, lam_im, log_dt, b_re, b_im):
    dt = jnp.exp(log_dt)[:, None]
    mag = jnp.exp(lam_re * dt)
    ang = lam_im * dt
    a_re, a_im = mag * jnp.cos(ang), mag * jnp.sin(ang)
    den = lam_re * lam_re + lam_im * lam_im
    n_re, n_im = a_re - 1.0, a_im
    z_re = (n_re * lam_re + n_im * lam_im) / den
    z_im = (n_im * lam_re - n_re * lam_im) / den
    bb_re = z_re[..., None] * b_re - z_im[..., None] * b_im
    bb_im = z_re[..., None] * b_im + z_im[..., None] * b_re
    return a_re, a_im, bb_re, bb_im


_GPB = SSM_CB // SSM_P


def _embed_b(bb):
    t = bb.transpose(0, 2, 1).reshape(SSM_NB, _GPB, SSM_H, SSM_P)
    return jnp.einsum('mjhp,jk->mjhkp', t, jnp.eye(_GPB, dtype=bb.dtype)).reshape(SSM_NB, SSM_UB, SSM_CB)


def _extract_b(d):
    t = d.reshape(SSM_NB, _GPB, SSM_H, _GPB, SSM_P)
    t = jnp.einsum('mjhkp,jk->mjhp', t, jnp.eye(_GPB, dtype=d.dtype))
    return t.reshape(SSM_G, SSM_H, SSM_P).transpose(0, 2, 1)


def _embed_c(c):
    t = c.transpose(0, 2, 1).reshape(SSM_NB, _GPB, SSM_P, SSM_H)
    return jnp.einsum('mjph,jk->mjpkh', t, jnp.eye(_GPB, dtype=c.dtype)).reshape(SSM_NB, SSM_CB, SSM_UB)


def _extract_c(d):
    t = d.reshape(SSM_NB, _GPB, SSM_P, _GPB, SSM_H)
    t = jnp.einsum('mjpkh,jk->mjph', t, jnp.eye(_GPB, dtype=d.dtype))
    return t.reshape(SSM_G, SSM_P, SSM_H).transpose(0, 2, 1)


def _scan_tables(a_re, a_im, reverse):
    ar = a_re.reshape(1, SSM_CH)
    ai = (-a_im if reverse else a_im).reshape(1, SSM_CH)
    pr, pi = [ar], [ai]
    for _ in range(SUBLANES - 1):
        pr, pi = pr + [pr[-1] * ar - pi[-1] * ai], pi + [pr[-1] * ai + pi[-1] * ar]
    rows = jnp.arange(SUBLANES)[:, None]
    out = []
    for k in (1, 2, 4):
        valid = (rows + k <= SUBLANES - 1) if reverse else (rows >= k)
        out += [jnp.where(valid, pr[k - 1], 0.0), jnp.where(valid, pi[k - 1], 0.0)]
    order = list(range(SUBLANES - 1, -1, -1)) if reverse else list(range(SUBLANES))
    out += [jnp.concatenate([pr[n] for n in order], axis=0), jnp.concatenate([pi[n] for n in order], axis=0)]
    return jnp.stack(out).astype(F32)


def _pad_heads(w, d):
    lead = w.shape[:-1]
    w = w.reshape(lead + (N_HEADS, d))
    w = jnp.pad(w, [(0, 0)] * len(lead) + [(0, 0), (0, HEAD_SLOT - d)])
    return w.reshape(lead + (HP,))


def _unpad_heads(w, d):
    lead = w.shape[:-1]
    return w.reshape(lead + (N_HEADS, HEAD_SLOT))[..., :d].reshape(lead + (N_HEADS * d,))


def _chip_major(w, axis):
    k, n = w.shape
    if axis == 0:
        return w.reshape(N_CHIPS, k // N_CHIPS, n)
    return w.reshape(k, N_CHIPS, n // N_CHIPS).transpose(1, 0, 2)


def _from_chip_major(w, axis):
    if axis == 0:
        return w.reshape(-1, w.shape[2])
    return w.transpose(1, 0, 2).reshape(w.shape[1], -1)


def _pad_w_in(w):
    z = lambda n: jnp.zeros((w.shape[0], n), w.dtype)
    return jnp.concatenate([w[:, :640], z(KR_LANE), w[:, 640:672], z(HEAD_SLOT - KR_LANE - QK_ROPE), w[:, 672:]], axis=1)


def _unpad_w_in(w):
    return jnp.concatenate([w[:, :640], w[:, P_KR + KR_LANE:P_KR + KR_LANE + QK_ROPE], w[:, P_U:]], axis=1)


def _local_step(x, positions, tgt, wts, sp):
    l = x.shape[0]
    tl = min(256, l)
    ta = min(512, l)

    inv_freq = ROPE_THETA ** (-jnp.arange(0, QK_ROPE, 2, dtype=F32) / QK_ROPE)
    ang = positions.astype(F32)[:, None] * inv_freq
    cos, sin = jnp.cos(ang), jnp.sin(ang)
    one = jnp.ones((l, KR_LANE), F32)
    rc = jnp.concatenate([one, cos, cos, jnp.ones((l, HEAD_SLOT - KR_LANE - QK_ROPE), F32)], axis=1)
    rs = jnp.concatenate([0 * one, -sin, sin, jnp.zeros((l, HEAD_SLOT - KR_LANE - QK_ROPE), F32)], axis=1)

    win = _pad_w_in(wts["w_in"])
    wuq = _pad_heads(wts["w_uq"], QK_HEAD)
    wukv = jnp.concatenate([_pad_heads(wts["w_uk"], QK_NOPE), _pad_heads(wts["w_uv"], V_HEAD)], axis=1)
    wba = jnp.pad(wts["w_branch_attn"].reshape(N_HEADS, V_HEAD, D_MODEL),
                  ((0, 0), (0, HEAD_SLOT - V_HEAD), (0, 0))).reshape(HP, D_MODEL)
    wbs, wglu, wout, wup, wdown = wts["w_branch_ssm"], wts["w_glu"], wts["w_out"], wts["w_up"], wts["w_down"]

    disc_in = (sp["ssm_lambda_re"], sp["ssm_lambda_im"], sp["ssm_log_dt"], sp["ssm_b_re"], sp["ssm_b_im"])
    (a_re, a_im, bb_re, bb_im), disc_vjp = jax.vjp(_ssm_disc, *disc_in)
    bre, bim = _embed_b(bb_re).astype(BF16), _embed_b(bb_im).astype(BF16)
    cre, cim = _embed_c(sp["ssm_c_re"]).astype(BF16), _embed_c(sp["ssm_c_im"]).astype(BF16)
    dvec = sp["ssm_d"].reshape(1, SSM_W)
    tab_f = _scan_tables(a_re, a_im, False)
    tab_r = _scan_tables(a_re, a_im, True)

    g1, gq, gkv = sp["mix_norm_pre"], sp["q_norm"], sp["kv_norm"]
    gpost, gpre, gfin = sp["mix_norm_post"], sp["ffn_norm_pre"], sp["ffn_norm_post"]
    bgate, bglu, convb = sp["b_gate"], sp["b_glu"], sp["conv_b"]
    convw = sp["conv_w"]

    hn, cq, ckv, q, k, v, u, gl = _proj_fwd(x, g1, win, gq, wuq, gkv, wukv, rc, rs, bgate, tl)
    attn, lse = _attn_fwd(q, k, v, ta)
    y1, sre, sim = _ssm_fwd(u, bre, bim, cre, cim, dvec, tab_f, ta)
    a, sm, merged, z, x1, hn2, y3 = _merge_fwd(x, gl, attn, y1, wba, wbs, wglu, bglu, wout, gpost, gpre, tl)
    h = _mm(hn2, wup, "ffn_up")
    act = _conv_fwd(h, convw, convb, ta)
    ff = _mm(act, wdown, "ffn_down")
    loss, dx2, dff, dgfin = _loss_head(ff, x1, tgt, gfin, tl)

    dact = _mm(dff, wdown, "ffn_down_dx", out_dtype=BF16, bt=True)
    d_wdown = _mm_tn(act, dff, "ffn_down_dw")
    dh, dwg, dwv, dbg, dbv = _conv_bwd(h, dact, convw, convb, ta)
    d_convw = jnp.concatenate([dwg, dwv], axis=1)
    d_convb = jnp.concatenate([dbg, dbv], axis=1)
    dhn2a = _mm(dh, wup, "ffn_up_dx_gate", bt=True, b_col0=0, a_lead=0)
    dhn2b = _mm(dh, wup, "ffn_up_dx_val", bt=True, b_col0=1, a_lead=1)
    d_wup = _mm_tn(hn2, dh, "ffn_up_dw", chips=True)
    (dx1, dz, dbra, dbrs, dgl, dattn, dy1, dt, y2, dgpre, dgpost, dbgate, dbglu) = _merge_bwd(
        dhn2a, dhn2b, x1, dx2, z, gl, a, sm, y1, wba, wbs, wglu, bglu, wout, gpost, gpre, tl)
    d_wout = _mm_tn(merged, dz, "w_out_dw")
    d_wba = _mm_tn(attn, dbra, "w_branch_attn_dw", chips=True)
    d_wbs = _mm_tn(y3, dbrs, "w_branch_ssm_dw", chips=True)
    d_wglu = _mm_tn(y2, dt, "w_glu_dw")
    dq, dk, dv = _attn_bwd(q, k, v, dattn, lse, _attn_delta(attn, dattn, ta), ta)
    du, dbre, dbim, dcre, dcim, dare, daim, dd = _ssm_bwd(dy1, u, sre, sim, bre, bim, cre, cim, dvec, tab_r, ta)
    gx, dql, qn, ckvn, dproj, dg1, dgq, dgkv = _proj_bwd(
        x, dx1, cq, ckv, dq, dk, dv, du, dgl, g1, win, gq, wuq, gkv, wukv, rc, rs, tl)
    d_win = _mm_tn(hn, dproj, "w_in_dw")
    d_wuq = _mm_tn(qn, dql, "w_uq_dw")
    d_wuk = _mm_tn(ckvn, dk, "w_uk_dw")
    d_wuv = _mm_tn(ckvn, dv, "w_uv_dw")

    d_lre, d_lim, d_ldt, d_bre, d_bim = disc_vjp((dare.reshape(SSM_G, SSM_P), daim.reshape(SSM_G, SSM_P),
                                                  _extract_b(dbre), _extract_b(dbim)))
    ncol = D_MODEL // N_CHIPS
    big = {
        "w_in": _chip_major(_unpad_w_in(d_win), 1),
        "w_uq": _chip_major(_unpad_heads(d_wuq, QK_HEAD), 1),
        "w_uk": _chip_major(_unpad_heads(d_wuk, QK_NOPE), 1),
        "w_uv": _chip_major(_unpad_heads(d_wuv, V_HEAD), 1),
        "w_glu": _chip_major(d_wglu, 0),
        "w_branch_attn": d_wba.reshape(N_CHIPS, N_HEADS, HEAD_SLOT, ncol)[:, :, :V_HEAD].reshape(
            N_CHIPS, N_HEADS * V_HEAD, ncol),
        "w_branch_ssm": d_wbs,
        "w_out": _chip_major(d_wout, 0),
        "w_up": d_wup,
        "w_down": _chip_major(d_wdown, 0),
    }
    small = {
        "conv_w": d_convw,
        "mix_norm_pre": dg1, "q_norm": dgq, "kv_norm": dgkv,
        "ssm_lambda_re": d_lre, "ssm_lambda_im": d_lim, "ssm_log_dt": d_ldt,
        "ssm_b_re": d_bre, "ssm_b_im": d_bim,
        "ssm_c_re": _extract_c(dcre), "ssm_c_im": _extract_c(dcim),
        "ssm_d": dd.reshape(SSM_G, SSM_H), "b_glu": dbglu, "b_gate": dbgate,
        "mix_norm_post": dgpost, "ffn_norm_pre": dgpre, "conv_b": d_convb, "ffn_norm_post": dgfin,
    }
    return loss[0, 0], gx, big, small


_ANY = pl.BlockSpec(memory_space=pl.ANY)


ROW_TILE = 16


def _place():
    x, y, c = lax.axis_index("x"), lax.axis_index("y"), lax.axis_index("c")
    return x, y, c, 2 * x + y, [(1 - x, y), (x, 1 - y), (1 - x, 1 - y)]


def _half(rows, which):
    hr = rows // 2
    return pl.ds(pl.multiple_of(which * hr, ROW_TILE), hr)


def _remote(src, dst, send_sems, recv_sems, n, dev):
    return pltpu.make_async_remote_copy(src_ref=src, dst_ref=dst, send_sem=send_sems.at[n], recv_sem=recv_sems.at[n],
                                        device_id=dev, device_id_type=MESH)


def _gather_big(shards):
    nw = len(shards)
    rows = [s.shape[0] for s in shards]

    def body(*refs):
        ins, outs = refs[:nw], refs[nw:2 * nw]
        ici_send, ici_recv, d2d_send, d2d_recv = refs[2 * nw:]
        x, y, c, me, peers = _place()
        sent = []
        for i in range(nw):
            for p, (px, py) in enumerate(peers):
                cp = _remote(ins[i].at[_half(rows[i], c)], outs[i].at[me, _half(rows[i], c)], ici_send, ici_recv,
                             3 * i + p, (px, py, c))
                cp.start()
                sent.append(cp)
        for p, (px, py) in enumerate(peers):
            for i in range(nw):
                blk = outs[i].at[2 * px + py, _half(rows[i], c)]
                _remote(blk, blk, ici_send, ici_recv, 3 * i + p, (px, py, c)).wait_recv()
                cp = _remote(blk, blk, d2d_send, d2d_recv, 3 * i + p, (x, y, 1 - c))
                cp.start()
                sent.append(cp)
        for p, (px, py) in enumerate(peers):
            for i in range(nw):
                blk = outs[i].at[2 * px + py, _half(rows[i], 1 - c)]
                _remote(blk, blk, d2d_send, d2d_recv, 3 * i + p, (x, y, 1 - c)).wait_recv()
        for cp in sent:
            cp.wait_send()

    dma = pltpu.SemaphoreType.DMA
    return pl.pallas_call(
        body, name="gather_weights", in_specs=[_ANY] * nw, out_specs=[_ANY] * nw,
        out_shape=[jax.ShapeDtypeStruct((N_CHIPS,) + s.shape, s.dtype) for s in shards],
        scratch_shapes=[dma((3 * nw,)), dma((3 * nw,)), dma((3 * nw,)), dma((3 * nw,))],
    )(*shards)


def _reduce_to_sibling(grads):
    nw = len(grads)

    def body(*refs):
        ins, outs = refs[:nw], refs[nw:2 * nw]
        send_sems, recv_sems = refs[2 * nw:]
        x, y, c, _, _ = _place()
        sent = []
        for i in range(nw):
            cp = _remote(ins[i].at[pl.ds(0, N_CHIPS), _half(grads[i].shape[1], 1 - c)], outs[i], send_sems, recv_sems,
                         i, (x, y, 1 - c))
            cp.start()
            sent.append(cp)
        for cp in sent:
            cp.wait()

    dma = pltpu.SemaphoreType.DMA
    return pl.pallas_call(
        body, name="reduce_grads_d2d", in_specs=[_ANY] * nw, out_specs=[_ANY] * nw,
        out_shape=[jax.ShapeDtypeStruct((N_CHIPS, g.shape[1] // 2, g.shape[2]), g.dtype) for g in grads],
        scratch_shapes=[dma((nw,)), dma((nw,))],
    )(*grads)


def _reduce_between_chips(pairs):
    nw = len(pairs)

    def body(*refs):
        ins, outs = refs[:nw], refs[nw:2 * nw]
        send_sems, recv_sems = refs[2 * nw:]
        x, y, c, me, peers = _place()
        sent = []
        for i in range(nw):
            for p, (px, py) in enumerate(peers):
                cp = _remote(ins[i].at[2 * px + py], outs[i].at[me], send_sems, recv_sems, 3 * i + p, (px, py, c))
                cp.start()
                sent.append(cp)
        for i in range(nw):
            for p, (px, py) in enumerate(peers):
                blk = outs[i].at[2 * px + py]
                _remote(blk, blk, send_sems, recv_sems, 3 * i + p, (px, py, c)).wait_recv()
        for cp in sent:
            cp.wait_send()

    dma = pltpu.SemaphoreType.DMA
    return pl.pallas_call(
        body, name="reduce_grads_ici", in_specs=[_ANY] * nw, out_specs=[_ANY] * nw,
        out_shape=[jax.ShapeDtypeStruct(p.shape, p.dtype) for p in pairs],
        scratch_shapes=[dma((3 * nw,)), dma((3 * nw,))],
    )(*pairs)


def _reduce_back(totals):
    nw = len(totals)

    def body(*refs):
        outs = refs[nw:2 * nw]
        send_sems, recv_sems = refs[2 * nw:]
        x, y, c, _, _ = _place()
        sent = []
        for i in range(nw):
            blk = outs[i].at[_half(totals[i].shape[0], c)]
            cp = _remote(blk, blk, send_sems, recv_sems, i, (x, y, 1 - c))
            cp.start()
            sent.append(cp)
        for i in range(nw):
            blk = outs[i].at[_half(totals[i].shape[0], 1 - c)]
            _remote(blk, blk, send_sems, recv_sems, i, (x, y, 1 - c)).wait_recv()
        for cp in sent:
            cp.wait_send()

    dma = pltpu.SemaphoreType.DMA
    return pl.pallas_call(
        body, name="reduce_grads_back", in_specs=[_ANY] * nw, out_specs=[_ANY] * nw,
        out_shape=[jax.ShapeDtypeStruct(t.shape, t.dtype) for t in totals],
        input_output_aliases={i: i for i in range(nw)},
        scratch_shapes=[dma((nw,)), dma((nw,))],
    )(*totals)


def _all_reduce_small(v, name):
    rows, w = v.shape

    def body(v_ref, out_ref, buf_ref, send_sems, recv_sems):
        x, y, c = lax.axis_index("x"), lax.axis_index("y"), lax.axis_index("c")
        me = 4 * x + 2 * y + c

        def flip(n):
            return (1 - x if n & 4 else x, 1 - y if n & 2 else y, 1 - c if n & 1 else c)

        sent = []
        for n in range(1, 8):
            cp = pltpu.make_async_remote_copy(src_ref=v_ref, dst_ref=buf_ref.at[me], send_sem=send_sems.at[n - 1],
                                              recv_sem=recv_sems.at[n - 1], device_id=flip(n), device_id_type=MESH)
            cp.start()
            sent.append(cp)
        buf_ref[me] = v_ref[...]
        for n in range(1, 8):
            px, py, pc = flip(n)
            pltpu.make_async_remote_copy(src_ref=v_ref, dst_ref=buf_ref.at[4 * px + 2 * py + pc],
                                         send_sem=send_sems.at[n - 1], recv_sem=recv_sems.at[n - 1],
                                         device_id=flip(n), device_id_type=MESH).wait_recv()
        for cp in sent:
            cp.wait_send()
        acc = buf_ref[0]
        for d in range(1, 8):
            acc = acc + buf_ref[d]
        out_ref[...] = acc

    vm = pl.BlockSpec(memory_space=pltpu.VMEM)
    return pl.pallas_call(
        body, name=name, in_specs=[vm], out_specs=vm,
        out_shape=jax.ShapeDtypeStruct((rows, w), F32),
        scratch_shapes=[pltpu.VMEM((8, rows, w), F32), pltpu.SemaphoreType.DMA((7,)), pltpu.SemaphoreType.DMA((7,))],
        compiler_params=pltpu.CompilerParams(vmem_limit_bytes=VMEM_LIMIT),
    )(v)


ELEMENTWISE_BLOCK = 256 * 1024


def _rows_tile(rows, cols):
    best = None
    for t in range(SUBLANES, rows + 1, SUBLANES):
        if rows % t == 0 and t * cols <= ELEMENTWISE_BLOCK:
            best = t
    return rows if best is None else best


def _add_pair(g, t, core, name):
    nb, n, w = t.shape
    tr = _rows_tile(n, w)
    steps = n // tr

    def body(core_ref, g_ref, t_ref, o_ref):
        o_ref[...] = (g_ref[...] + t_ref[...]).astype(BF16)

    spec = pl.BlockSpec((1, tr, w), lambda j, i, core_ref: (j, i, 0))
    return pl.pallas_call(
        body, name=name,
        grid_spec=pltpu.PrefetchScalarGridSpec(
            num_scalar_prefetch=1, grid=(nb, steps),
            in_specs=[pl.BlockSpec((1, tr, w), lambda j, i, core_ref: (j, core_ref[0] * steps + i, 0)), spec],
            out_specs=spec),
        out_shape=jax.ShapeDtypeStruct(t.shape, BF16),
        compiler_params=_params(("parallel", "parallel")))(core, g, t)


def _add_chips(landed, pairs, place, name):
    nb, n, w = landed.shape
    tr = _rows_tile(n, w)
    steps = n // tr

    def body(place_ref, r_ref, own_ref, o_ref):
        me = place_ref[0]
        acc = None
        for k in range(nb):
            blk = jnp.where(me == k, own_ref[0], r_ref[k]).astype(F32)
            acc = blk if acc is None else acc + blk
        o_ref[...] = acc

    return pl.pallas_call(
        body, name=name,
        grid_spec=pltpu.PrefetchScalarGridSpec(
            num_scalar_prefetch=1, grid=(steps,),
            in_specs=[pl.BlockSpec((nb, tr, w), lambda i, place_ref: (0, i, 0)),
                      pl.BlockSpec((1, tr, w), lambda i, place_ref: (place_ref[0], i, 0))],
            out_specs=pl.BlockSpec((tr, w), lambda i, place_ref: (place_ref[1] * steps + i, 0))),
        out_shape=jax.ShapeDtypeStruct((2 * n, w), F32),
        compiler_params=_params(("parallel",)))(place, landed, pairs)


def _adamw(w, g, m, v, name):
    rows, wd = w.shape
    tr = _rows_tile(rows, wd)
    c1 = 1.0 - ADAM_B1 ** ADAM_STEP
    c2 = 1.0 - ADAM_B2 ** ADAM_STEP

    def body(w_ref, g_ref, m_ref, v_ref, d_ref, mo_ref, vo_ref):
        gv = g_ref[...]
        m2 = ADAM_B1 * m_ref[...] + (1.0 - ADAM_B1) * gv
        v2 = ADAM_B2 * v_ref[...] + (1.0 - ADAM_B2) * (gv * gv)
        mo_ref[...] = m2
        vo_ref[...] = v2
        d_ref[...] = -ADAM_LR * ((m2 / c1) / (jnp.sqrt(v2 / c2) + ADAM_EPS) + ADAM_WD * w_ref[...])

    spec = pl.BlockSpec((tr, wd), lambda i: (i, 0))
    shp = jax.ShapeDtypeStruct((rows, wd), F32)
    return pl.pallas_call(body, name=name, grid=(rows // tr,), in_specs=[spec] * 4, out_specs=[spec] * 3,
                          out_shape=[shp] * 3, compiler_params=_params(("parallel",)))(w, g, m, v)


BIG = [("w_in", (1024, 3232), 1), ("w_uq", (384, 768), 1), ("w_uk", (256, 512), 1), ("w_uv", (256, 512), 1),
       ("w_glu", (512, 512), 0), ("w_branch_attn", (512, 1024), 1), ("w_branch_ssm", (512, 1024), 1),
       ("w_out", (1024, 1024), 0), ("w_up", (1024, 5632), 1), ("conv_w", (3, 5632), 1), ("w_down", (2816, 1024), 0)]
SMALL = [("mix_norm_pre", (1024,)), ("q_norm", (384,)), ("kv_norm", (256,)), ("ssm_lambda_re", (32, 64)),
         ("ssm_lambda_im", (32, 64)), ("ssm_log_dt", (32,)), ("ssm_b_re", (32, 64, 16)), ("ssm_b_im", (32, 64, 16)),
         ("ssm_c_re", (32, 16, 64)), ("ssm_c_im", (32, 16, 64)), ("ssm_d", (32, 16)), ("b_glu", (512,)),
         ("b_gate", (2048,)), ("mix_norm_post", (1024,)), ("ffn_norm_pre", (1024,)), ("conv_b", (5632,)),
         ("ffn_norm_post", (1024,))]
MATMUL_W = [b for b in BIG if b[0] != "conv_w"]
CONV_W_SHAPE = (3, 2 * D_FF)
CONV_W_SHARD = (3, 2 * D_FF // N_CHIPS)
SMALL_SUM = [("loss", (1,))] + SMALL + [("conv_w", CONV_W_SHAPE)]
SMALL_ADAM = SMALL + [("conv_w", CONV_W_SHARD)]


def _pack_flat(layout, vals):
    flat = jnp.concatenate([vals[n].astype(F32).reshape(-1) for n, _ in layout])
    rows = -(-(-(-flat.shape[0] // FLAT_W)) // SUBLANES) * SUBLANES
    return jnp.pad(flat, (0, rows * FLAT_W - flat.shape[0])).reshape(rows, FLAT_W)


def _unpack_flat(layout, flat):
    flat = flat.reshape(-1)
    out = {}
    o = 0
    for name, shape in layout:
        n = math.prod(shape)
        out[name] = flat[o:o + n].reshape(shape)
        o += n
    return out


_ARG_NAMES = ["x", "positions"] + [n for n in (
    "mix_norm_pre", "w_in", "q_norm", "w_uq", "kv_norm", "w_uk", "w_uv", "ssm_lambda_re", "ssm_lambda_im", "ssm_log_dt",
    "ssm_b_re", "ssm_b_im", "ssm_c_re", "ssm_c_im", "ssm_d", "w_glu", "b_glu", "w_branch_attn", "w_branch_ssm",
    "b_gate", "w_out", "mix_norm_post", "ffn_norm_pre", "w_up", "conv_w", "conv_b", "w_down", "ffn_norm_post")]
_WEIGHTS = _ARG_NAMES[2:]


def _gather_weights(w):
    c = lax.axis_index("c")
    mine = [w[name].astype(BF16) for name, _, _ in MATMUL_W]
    gathered = _gather_big(mine)
    own = (jnp.arange(N_CHIPS) == 2 * lax.axis_index("x") + lax.axis_index("y"))[:, None, None]
    full = {name: _from_chip_major(jnp.where(own, s[None], g), axis)
            for (name, _, axis), s, g in zip(MATMUL_W, mine, gathered)}

    chip = 2 * lax.axis_index("x") + lax.axis_index("y")
    ncol = 2 * D_FF // N_CHIPS
    placed = lax.dynamic_update_slice_in_dim(jnp.zeros((3, 2 * D_FF), F32), w["conv_w"], chip * ncol, axis=1)
    placed = placed * jnp.where(c == 0, 1.0, 0.0)
    cw_rows = -(-(-(-3 * 2 * D_FF // FLAT_W)) // SUBLANES) * SUBLANES
    placed = jnp.pad(placed.reshape(-1), (0, cw_rows * FLAT_W - 3 * 2 * D_FF)).reshape(cw_rows, FLAT_W)
    conv_w_full = _all_reduce_small(placed, "gather_conv_w").reshape(-1)[:3 * 2 * D_FF].reshape(3, 2 * D_FF)
    return full, conv_w_full


def _reduce_grads(gbig, loss, gsmall):
    core = lax.axis_index("c").astype(jnp.int32).reshape(1)
    chip = (2 * lax.axis_index("x") + lax.axis_index("y")).astype(jnp.int32).reshape(1)
    place = jnp.concatenate([chip, core])
    names = [name for name, _, _ in MATMUL_W]
    grads = [gbig[n] for n in names]
    theirs = _reduce_to_sibling(grads)
    pairs = [_add_pair(g, t, core, "reduce_pair_" + n) for n, g, t in zip(names, grads, theirs)]
    landed = _reduce_between_chips(pairs)
    totals = [_add_chips(r, p, place, "reduce_chips_" + n) for n, r, p in zip(names, landed, pairs)]
    g_red = dict(zip(names, _reduce_back(totals)))

    vals = dict(gsmall)
    vals["loss"] = loss
    small_red = _unpack_flat(SMALL_SUM, _all_reduce_small(_pack_flat(SMALL_SUM, vals), "reduce_small"))
    return g_red, small_red


def _step(args):
    x = args["x"][0]
    positions = args["positions"][0]
    tgt = args["loss_target"][0]
    w = {n: args[n][0] for n in _WEIGHTS}
    m = {n: args["m_" + n][0] for n in _WEIGHTS}
    v = {n: args["v_" + n][0] for n in _WEIGHTS}

    full, conv_w_full = _gather_weights(w)
    sp = {n: w[n].reshape(s) for n, s in SMALL}
    for n in ("mix_norm_pre", "q_norm", "kv_norm", "b_glu", "b_gate", "mix_norm_post", "ffn_norm_pre", "conv_b",
              "ffn_norm_post"):
        sp[n] = sp[n].reshape(1, -1)
    sp["conv_w"] = conv_w_full
    loss, gx, gbig, gsmall = _local_step(x, positions, tgt, full, sp)
    g_red, small_red = _reduce_grads(gbig, loss, gsmall)

    chip = 2 * lax.axis_index("x") + lax.axis_index("y")
    grads = dict(small_red)
    grads["conv_w"] = lax.dynamic_slice_in_dim(small_red["conv_w"], chip * CONV_W_SHARD[1], CONV_W_SHARD[1], axis=1)
    grads.update(g_red)

    outs = {"grad_" + n: grads[n] for n in _WEIGHTS}
    for name, _, _ in MATMUL_W:
        d, m2, v2 = _adamw(w[name], grads[name], m[name], v[name], "adamw_" + name)
        outs["delta_" + name], outs["new_m_" + name], outs["new_v_" + name] = d, m2, v2
    d_sm, m_sm, v_sm = _adamw(_pack_flat(SMALL_ADAM, w), _pack_flat(SMALL_ADAM, grads), _pack_flat(SMALL_ADAM, m),
                              _pack_flat(SMALL_ADAM, v), "adamw_small")
    for prefix, flat in (("delta_", d_sm), ("new_m_", m_sm), ("new_v_", v_sm)):
        for n, val in _unpack_flat(SMALL_ADAM, flat).items():
            outs[prefix + n] = val
    outs = {n: val.reshape(args[n.split("_", 1)[1] if not n.startswith("new_") else n[6:]].shape)
            for n, val in outs.items()}
    res = [small_red["loss"][0], gx[None]]
    for prefix in ("grad_", "delta_", "new_m_", "new_v_"):
        res += [outs[prefix + n] for n in _WEIGHTS]
    return tuple(res)


def kernel(x, positions, mix_norm_pre, w_in, q_norm, w_uq, kv_norm, w_uk, w_uv, ssm_lambda_re, ssm_lambda_im, ssm_log_dt, ssm_b_re, ssm_b_im, ssm_c_re, ssm_c_im, ssm_d, w_glu, b_glu, w_branch_attn, w_branch_ssm, b_gate, w_out, mix_norm_post, ffn_norm_pre, w_up, conv_w, conv_b, w_down, ffn_norm_post, loss_target, m_mix_norm_pre, m_w_in, m_q_norm, m_w_uq, m_kv_norm, m_w_uk, m_w_uv, m_ssm_lambda_re, m_ssm_lambda_im, m_ssm_log_dt, m_ssm_b_re, m_ssm_b_im, m_ssm_c_re, m_ssm_c_im, m_ssm_d, m_w_glu, m_b_glu, m_w_branch_attn, m_w_branch_ssm, m_b_gate, m_w_out, m_mix_norm_post, m_ffn_norm_pre, m_w_up, m_conv_w, m_conv_b, m_w_down, m_ffn_norm_post, v_mix_norm_pre, v_w_in, v_q_norm, v_w_uq, v_kv_norm, v_w_uk, v_w_uv, v_ssm_lambda_re, v_ssm_lambda_im, v_ssm_log_dt, v_ssm_b_re, v_ssm_b_im, v_ssm_c_re, v_ssm_c_im, v_ssm_d, v_w_glu, v_b_glu, v_w_branch_attn, v_w_branch_ssm, v_b_gate, v_w_out, v_mix_norm_post, v_ffn_norm_pre, v_w_up, v_conv_w, v_conv_b, v_w_down, v_ffn_norm_post):
    given = dict(locals())
    return _step(given)
```

```python
import math

import jax
import jax.numpy as jnp
from jax import lax
from jax.experimental import pallas as pl
from jax.experimental.pallas import tpu as pltpu

F32 = jnp.float32
BF16 = jnp.bfloat16
MESH = pl.DeviceIdType.MESH

D_MODEL = 1024
N_HEADS = 8
QK_NOPE = 64
QK_ROPE = 32
QK_HEAD = QK_NOPE + QK_ROPE
V_HEAD = 64
Q_RANK = 384
KV_RANK = 256
ROPE_THETA = 10000.0
SSM_W = 512
SSM_H = 16
SSM_G = 32
SSM_P = 64
SSM_CH = SSM_G * SSM_P
D_FF = 2816
EPS = 1e-6
ADAM_LR = 0.001
ADAM_B1 = 0.9
ADAM_B2 = 0.999
ADAM_EPS = 1e-08
ADAM_WD = 0.01
ADAM_STEP = 10

LANES = 128
SUBLANES = 8
VMEM_LIMIT = 56 * 1024 * 1024

HEAD_SLOT = LANES
HP = N_HEADS * HEAD_SLOT
P_CQ, P_CKV, P_KR, P_U, P_GL, P_END = 0, 384, 640, 768, 1280, 3328
KR_LANE = 64

FLAT_W = 1024
N_CHIPS = 4


def _tile(n, cap):
    if n <= cap:
        return n
    best = None
    for t in range(LANES, cap + 1, LANES):
        if n % t == 0:
            best = t
    assert best is not None, (n, cap)
    return best


def _params(sem):
    return pltpu.CompilerParams(dimension_semantics=sem, vmem_limit_bytes=VMEM_LIMIT)


def _dot(a, b):
    return jnp.dot(a, b, preferred_element_type=F32)


def _dot_nt(a, b):
    return lax.dot_general(a, b, (((1,), (1,)), ((), ())), preferred_element_type=F32)


def _dot_tn(a, b):
    return lax.dot_general(a, b, (((0,), (0,)), ((), ())), preferred_element_type=F32)


def _rms(x, g):
    r = lax.rsqrt(jnp.mean(x * x, axis=-1, keepdims=True) + EPS)
    return x * r * g, r


def _rms_bwd(dy, x, g):
    r = lax.rsqrt(jnp.mean(x * x, axis=-1, keepdims=True) + EPS)
    dyg = dy * g
    dx = r * dyg - x * (r * r * r) * jnp.mean(dyg * x, axis=-1, keepdims=True)
    dg = jnp.sum(dy * x * r, axis=0, keepdims=True)
    return dx, dg


_GELU_K0 = math.sqrt(2.0 / math.pi)
_GELU_K1 = 0.044715


def _gelu(x):
    th = jnp.tanh(_GELU_K0 * (x + _GELU_K1 * x * x * x))
    return 0.5 * x * (1.0 + th)


def _gelu_grad(x):
    th = jnp.tanh(_GELU_K0 * (x + _GELU_K1 * x * x * x))
    return 0.5 * (1.0 + th) + 0.5 * x * (1.0 - th * th) * _GELU_K0 * (1.0 + 3.0 * _GELU_K1 * x * x)


def _sigmoid(x):
    return 1.0 / (1.0 + jnp.exp(-x))


def _rope(q, c, s):
    n = q.shape[1]
    lane = lax.broadcasted_iota(jnp.int32, q.shape, 1) % HEAD_SLOT
    sw = jnp.where(lane < KR_LANE + QK_ROPE // 2, pltpu.roll(q, n - QK_ROPE // 2, 1), pltpu.roll(q, QK_ROPE // 2, 1))
    return q * c + sw * s


def _rope_bwd(dy, c, s):
    n = dy.shape[1]
    t = dy * s
    lane = lax.broadcasted_iota(jnp.int32, dy.shape, 1) % HEAD_SLOT
    sw = jnp.where(lane < KR_LANE + QK_ROPE // 2, pltpu.roll(t, n - QK_ROPE // 2, 1), pltpu.roll(t, QK_ROPE // 2, 1))
    rope_lane = jnp.logical_and(lane >= KR_LANE, lane < KR_LANE + QK_ROPE)
    return dy * c + jnp.where(rope_lane, sw, 0.0)


def _shift_down(x, k, halo):
    xs = pltpu.roll(x, k, 0)
    hs = pltpu.roll(halo, k, 0)
    rows = lax.broadcasted_iota(jnp.int32, halo.shape, 0)
    top = jnp.where(rows < k, hs, xs[0:SUBLANES])
    return jnp.concatenate([top, xs[SUBLANES:]], axis=0)


def _shift_up(x, k, halo):
    t = x.shape[0]
    xs = pltpu.roll(x, t - k, 0)
    hs = pltpu.roll(halo, SUBLANES - k, 0)
    rows = lax.broadcasted_iota(jnp.int32, halo.shape, 0)
    bot = jnp.where(rows >= SUBLANES - k, hs, xs[t - SUBLANES:])
    return jnp.concatenate([xs[:t - SUBLANES], bot], axis=0)


def _mm(a, b, name, out_dtype=F32, bt=False, b_col0=0, n=None, tm_cap=512, tn_cap=1408, a_lead=None):
    m, k = a.shape[-2:]
    if bt:
        n_full = b.shape[0]
        n = n_full
    else:
        n = b.shape[1] if n is None else n
    tm = min(tm_cap, m)
    tn = _tile(n, tn_cap)

    def body(a_ref, b_ref, o_ref):
        if bt:
            o_ref[...] = _dot_nt(a_ref[...], b_ref[...]).astype(out_dtype)
        else:
            o_ref[...] = _dot(a_ref[...], b_ref[...]).astype(out_dtype)

    if bt:
        b_spec = pl.BlockSpec((tn, k), lambda j, i: (j, b_col0))
    else:
        off = b_col0 * (n // tn)
        b_spec = pl.BlockSpec((k, tn), lambda j, i: (0, off + j))
    if a_lead is None:
        a_spec = pl.BlockSpec((tm, k), lambda j, i: (i, 0))
    else:
        a_spec = pl.BlockSpec((None, tm, k), lambda j, i: (a_lead, i, 0))
    return pl.pallas_call(
        body, name=name, grid=(n // tn, m // tm),
        in_specs=[a_spec, b_spec],
        out_specs=pl.BlockSpec((tm, tn), lambda j, i: (i, j)),
        out_shape=jax.ShapeDtypeStruct((m, n), out_dtype),
        compiler_params=_params(("parallel", "parallel")),
    )(a, b)


def _mm_tn(a, b, name, tk_cap=512, tn_cap=1664, tl_cap=1024, chips=False):
    l, k = a.shape
    tk = _tile(k, tk_cap)
    tl = min(tl_cap, l)

    def body(a_ref, b_ref, o_ref):
        @pl.when(pl.program_id(2) == 0)
        def _():
            o_ref[...] = jnp.zeros_like(o_ref)

        o_ref[...] += _dot_tn(a_ref[...], b_ref[...])

    if chips:
        n = b.shape[-1] * (b.shape[0] if b.ndim == 3 else 1)
        tn = n // N_CHIPS
        assert tn % LANES == 0
        if b.ndim == 3:
            per = N_CHIPS // b.shape[0]
            b_spec = pl.BlockSpec((None, tl, tn), lambda i, j, r: (j // per, r, j % per))
        else:
            b_spec = pl.BlockSpec((tl, tn), lambda i, j, r: (r, j))
        out_spec = pl.BlockSpec((None, tk, tn), lambda i, j, r: (j, i, 0))
        out_shape = jax.ShapeDtypeStruct((N_CHIPS, k, tn), F32)
    else:
        n = b.shape[1]
        tn = _tile(n, tn_cap)
        b_spec = pl.BlockSpec((tl, tn), lambda i, j, r: (r, j))
        out_spec = pl.BlockSpec((tk, tn), lambda i, j, r: (i, j))
        out_shape = jax.ShapeDtypeStruct((k, n), F32)
    return pl.pallas_call(
        body, name=name, grid=(k // tk, n // tn, l // tl),
        in_specs=[pl.BlockSpec((tl, tk), lambda i, j, r: (r, i)), b_spec],
        out_specs=out_spec, out_shape=out_shape,
        compiler_params=_params(("parallel", "parallel", "arbitrary")),
    )(a, b)


def _row(tl, n):
    return pl.BlockSpec((tl, n), lambda i: (i, 0))


def _const(shape):
    return pl.BlockSpec(shape, lambda i: tuple(0 for _ in shape))


def _proj_fwd(x, g1, win, gq, wuq, gkv, wukv, rc, rs, bg, tl):
    l = x.shape[0]

    def body(x_ref, g1_ref, win_ref, gq_ref, wuq_ref, gkv_ref, wukv_ref, rc_ref, rs_ref, bg_ref,
             hn_ref, cq_ref, ckv_ref, q_ref, k_ref, v_ref, u_ref, gl_ref):
        hn, _ = _rms(x_ref[...], g1_ref[...])
        hnb = hn.astype(BF16)
        hn_ref[...] = hnb
        proj = _dot(hnb, win_ref[...])
        cq = proj[:, P_CQ:P_CKV]
        ckv = proj[:, P_CKV:P_KR]
        kr = proj[:, P_KR:P_U]
        cq_ref[...] = cq
        ckv_ref[...] = ckv
        u_ref[...] = proj[:, P_U:P_GL]
        gl_ref[...] = proj[:, P_GL:P_END] + bg_ref[...]
        qn, _ = _rms(cq, gq_ref[...])
        q = _dot(qn.astype(BF16), wuq_ref[...])
        c1 = rc_ref[...]
        s1 = rs_ref[...]
        q_ref[...] = (_rope(q, jnp.tile(c1, (1, N_HEADS)), jnp.tile(s1, (1, N_HEADS))) * Q_PRESCALE).astype(BF16)
        ckvn, _ = _rms(ckv, gkv_ref[...])
        kv = _dot(ckvn.astype(BF16), wukv_ref[...])
        krr = _rope(kr, c1, s1)
        k_ref[...] = (kv[:, :HP] + jnp.tile(krr, (1, N_HEADS))).astype(BF16)
        v_ref[...] = kv[:, HP:].astype(BF16)

    outs = [(D_MODEL, BF16), (Q_RANK, F32), (KV_RANK, F32), (HP, BF16), (HP, BF16), (HP, BF16),
            (SSM_W, F32), (2 * D_MODEL, F32)]
    return pl.pallas_call(
        body, name="proj_fwd", grid=(l // tl,),
        in_specs=[_row(tl, D_MODEL), _const((1, D_MODEL)), _const((D_MODEL, P_END)), _const((1, Q_RANK)),
                  _const((Q_RANK, HP)), _const((1, KV_RANK)), _const((KV_RANK, 2 * HP)),
                  _row(tl, HEAD_SLOT), _row(tl, HEAD_SLOT), _const((1, 2 * D_MODEL))],
        out_specs=[_row(tl, n) for n, _ in outs],
        out_shape=[jax.ShapeDtypeStruct((l, n), dt) for n, dt in outs],
        compiler_params=_params(("parallel",)),
    )(x, g1, win, gq, wuq, gkv, wukv, rc, rs, bg)


_NEG = -1e30


LOG2E = 1.0 / math.log(2.0)
LN2 = math.log(2.0)
ATTN_SCALE = 1.0 / math.sqrt(QK_HEAD)
Q_PRESCALE = ATTN_SCALE * LOG2E
HEADS_PER_STEP = 2
PAIR_W = HEADS_PER_STEP * HEAD_SLOT


def _causal_pairs(nq, by_query):
    if by_query:
        pairs = [(i, j) for i in range(nq) for j in range(i + 1)]
    else:
        pairs = [(i, j) for j in range(nq) for i in range(j, nq)]
    return jnp.array([p[0] for p in pairs], jnp.int32), jnp.array([p[1] for p in pairs], jnp.int32)


def _diag_mask_t(s):
    rows = lax.broadcasted_iota(jnp.int32, s.shape, 0)
    cols = lax.broadcasted_iota(jnp.int32, s.shape, 1)
    return jnp.where(rows <= cols, s, _NEG)


def _attn_fwd(q, k, v, tq):
    l = q.shape[0]
    nq = l // tq
    it, jt = _causal_pairs(nq, True)

    def body(it_ref, jt_ref, q_ref, k_ref, v_ref, o_ref, lse_ref, m_ref, l_ref, acc_ref):
        t = pl.program_id(1)
        i = it_ref[t]
        j = jt_ref[t]

        @pl.when(j == 0)
        def _():
            m_ref[...] = jnp.full_like(m_ref, _NEG)
            l_ref[...] = jnp.zeros_like(l_ref)
            acc_ref[...] = jnp.zeros_like(acc_ref)

        def update(on_diagonal):
            for hh in range(HEADS_PER_STEP):
                sl = slice(hh * HEAD_SLOT, (hh + 1) * HEAD_SLOT)
                s = _dot_nt(k_ref[:, sl], q_ref[:, sl])
                if on_diagonal:
                    s = _diag_mask_t(s)
                m_old = m_ref[hh]
                m_new = jnp.maximum(m_old, jnp.max(s, axis=0, keepdims=True))
                p = jnp.exp2(s - m_new)
                alpha = jnp.exp2(m_old - m_new)
                l_ref[hh] = alpha * l_ref[hh] + jnp.sum(p, axis=0, keepdims=True)
                acc_ref[hh] = alpha * acc_ref[hh] + _dot_tn(v_ref[:, sl], p.astype(BF16))
                m_ref[hh] = m_new

        @pl.when(j < i)
        def _():
            update(False)

        @pl.when(j == i)
        def _():
            update(True)
            for hh in range(HEADS_PER_STEP):
                sl = slice(hh * HEAD_SLOT, (hh + 1) * HEAD_SLOT)
                o_ref[:, sl] = (acc_ref[hh] / l_ref[hh]).T.astype(BF16)
                lse_ref[hh] = m_ref[hh] + jnp.log(l_ref[hh]) * LOG2E

    blk = (tq, PAIR_W)
    qmap = lambda h, t, it_ref, jt_ref: (it_ref[t], h)
    kmap = lambda h, t, it_ref, jt_ref: (jt_ref[t], h)
    row = pl.BlockSpec((HEADS_PER_STEP, 1, tq), lambda h, t, it_ref, jt_ref: (h, 0, it_ref[t]))
    return pl.pallas_call(
        body, name="attn_fwd",
        grid_spec=pltpu.PrefetchScalarGridSpec(
            num_scalar_prefetch=2, grid=(N_HEADS // HEADS_PER_STEP, it.shape[0]),
            in_specs=[pl.BlockSpec(blk, qmap), pl.BlockSpec(blk, kmap), pl.BlockSpec(blk, kmap)],
            out_specs=[pl.BlockSpec(blk, qmap), row],
            scratch_shapes=[pltpu.VMEM((HEADS_PER_STEP, 1, tq), F32), pltpu.VMEM((HEADS_PER_STEP, 1, tq), F32),
                            pltpu.VMEM((HEADS_PER_STEP, HEAD_SLOT, tq), F32)]),
        out_shape=[jax.ShapeDtypeStruct((l, HP), BF16), jax.ShapeDtypeStruct((N_HEADS, 1, l), F32)],
        compiler_params=_params(("parallel", "arbitrary")),
    )(it, jt, q, k, v)


def _attn_delta(o, do, tq):
    l = o.shape[0]

    def body(o_ref, do_ref, d_ref):
        prod = o_ref[...].astype(F32) * do_ref[...].astype(F32)
        for hh in range(HEADS_PER_STEP):
            d_ref[hh] = jnp.sum(prod[:, hh * HEAD_SLOT:(hh + 1) * HEAD_SLOT].T, axis=0, keepdims=True)

    blk = pl.BlockSpec((tq, PAIR_W), lambda h, i: (i, h))
    return pl.pallas_call(
        body, name="attn_delta", grid=(N_HEADS // HEADS_PER_STEP, l // tq), in_specs=[blk, blk],
        out_specs=pl.BlockSpec((HEADS_PER_STEP, 1, tq), lambda h, i: (h, 0, i)),
        out_shape=jax.ShapeDtypeStruct((N_HEADS, 1, l), F32),
        compiler_params=_params(("parallel", "parallel")),
    )(o, do)


def _attn_bwd(q, k, v, do, lse, delta, tq):
    l = q.shape[0]
    nq = l // tq
    it, jt = _causal_pairs(nq, False)

    def body(it_ref, jt_ref, q_ref, k_ref, v_ref, do_ref, lse_ref, dl_ref, dq_ref, dk_ref, dv_ref, dka_ref, dva_ref):
        t = pl.program_id(1)
        i = it_ref[t]
        j = jt_ref[t]

        @pl.when(t == 0)
        def _():
            dq_ref[...] = jnp.zeros_like(dq_ref)

        @pl.when(i == j)
        def _():
            dka_ref[...] = jnp.zeros_like(dka_ref)
            dva_ref[...] = jnp.zeros_like(dva_ref)

        def update(on_diagonal):
            r0 = pl.multiple_of(i * tq, tq)
            for hh in range(HEADS_PER_STEP):
                sl = slice(hh * HEAD_SLOT, (hh + 1) * HEAD_SLOT)
                qb = q_ref[:, sl]
                kb = k_ref[:, sl]
                dob = do_ref[:, sl]
                s = _dot_nt(kb, qb)
                if on_diagonal:
                    s = _diag_mask_t(s)
                p = jnp.exp2(s - lse_ref[hh])
                dva_ref[:, sl] += _dot(p.astype(BF16), dob)
                dp = _dot_nt(v_ref[:, sl], dob)
                ds = (p * (dp - dl_ref[hh])).astype(BF16)
                dka_ref[:, sl] += _dot(ds, qb)
                dq_ref[pl.ds(r0, tq), sl] += ATTN_SCALE * _dot_tn(ds, kb)

        @pl.when(j < i)
        def _():
            update(False)

        @pl.when(j == i)
        def _():
            update(True)

        @pl.when(i == nq - 1)
        def _():
            dk_ref[...] = (dka_ref[...] * LN2).astype(BF16)
            dv_ref[...] = dva_ref[...].astype(BF16)

    blk = (tq, PAIR_W)
    qmap = lambda h, t, it_ref, jt_ref: (it_ref[t], h)
    kmap = lambda h, t, it_ref, jt_ref: (jt_ref[t], h)
    row = pl.BlockSpec((HEADS_PER_STEP, 1, tq), lambda h, t, it_ref, jt_ref: (h, 0, it_ref[t]))
    return pl.pallas_call(
        body, name="attn_bwd",
        grid_spec=pltpu.PrefetchScalarGridSpec(
            num_scalar_prefetch=2, grid=(N_HEADS // HEADS_PER_STEP, it.shape[0]),
            in_specs=[pl.BlockSpec(blk, qmap), pl.BlockSpec(blk, kmap), pl.BlockSpec(blk, kmap),
                      pl.BlockSpec(blk, qmap), row, row],
            out_specs=[pl.BlockSpec((l, PAIR_W), lambda h, t, it_ref, jt_ref: (0, h)), pl.BlockSpec(blk, kmap),
                       pl.BlockSpec(blk, kmap)],
            scratch_shapes=[pltpu.VMEM(blk, F32), pltpu.VMEM(blk, F32)]),
        out_shape=[jax.ShapeDtypeStruct((l, HP), F32), jax.ShapeDtypeStruct((l, HP), BF16),
                   jax.ShapeDtypeStruct((l, HP), BF16)],
        compiler_params=_params(("parallel", "arbitrary")),
    )(it, jt, q, k, v, do, lse, delta)


SSM_CB = 512
SSM_UB = 128
SSM_NB = SSM_CH // SSM_CB


def _scan_tiles(re_ref, im_ref, tab, carry, n_tiles, reverse):
    def tile(n, c):
        cr, ci = c
        idx = (n_tiles - 1 - n) if reverse else n
        r0 = pl.multiple_of(idx * SUBLANES, SUBLANES)
        sr = re_ref[pl.ds(r0, SUBLANES), :]
        si = im_ref[pl.ds(r0, SUBLANES), :]
        for step, k in enumerate((1, 2, 4)):
            mr, mi = tab[2 * step], tab[2 * step + 1]
            sh = (SUBLANES - k) if reverse else k
            rr = pltpu.roll(sr, sh, 0)
            ri = pltpu.roll(si, sh, 0)
            sr, si = sr + mr * rr - mi * ri, si + mr * ri + mi * rr
        pr, pi = tab[6], tab[7]
        sr, si = sr + pr * cr - pi * ci, si + pr * ci + pi * cr
        re_ref[pl.ds(r0, SUBLANES), :] = sr
        im_ref[pl.ds(r0, SUBLANES), :] = si
        if reverse:
            return sr[0:1, :], si[0:1, :]
        return sr[SUBLANES - 1:SUBLANES, :], si[SUBLANES - 1:SUBLANES, :]

    return lax.fori_loop(0, n_tiles, tile, carry, unroll=2)


def _ssm_fwd(u, bre, bim, cre, cim, dvec, tab, tt):
    l = u.shape[0]
    nt = l // tt

    def body(u_ref, bre_ref, bim_ref, cre_ref, cim_ref, d_ref, tab_ref, y_ref, sre_ref, sim_ref, car_ref):
        @pl.when(pl.program_id(1) == 0)
        def _():
            car_ref[...] = jnp.zeros_like(car_ref)

        uf = u_ref[...]
        ub = uf.astype(BF16)
        sre_ref[...] = _dot(ub, bre_ref[0])
        sim_ref[...] = _dot(ub, bim_ref[0])
        tab_v = [tab_ref[n] for n in range(8)]
        cr, ci = _scan_tiles(sre_ref, sim_ref, tab_v, (car_ref[0:1, :], car_ref[8:9, :]), tt // SUBLANES, False)
        car_ref[0:1, :] = cr
        car_ref[8:9, :] = ci
        y_ref[...] = (_dot(sre_ref[...].astype(BF16), cre_ref[0]) - _dot(sim_ref[...].astype(BF16), cim_ref[0])
                      + d_ref[...] * uf)

    return pl.pallas_call(
        body, name="ssm_fwd", grid=(SSM_NB, nt),
        in_specs=[pl.BlockSpec((tt, SSM_UB), lambda m, t: (t, m)),
                  pl.BlockSpec((1, SSM_UB, SSM_CB), lambda m, t: (m, 0, 0)),
                  pl.BlockSpec((1, SSM_UB, SSM_CB), lambda m, t: (m, 0, 0)),
                  pl.BlockSpec((1, SSM_CB, SSM_UB), lambda m, t: (m, 0, 0)),
                  pl.BlockSpec((1, SSM_CB, SSM_UB), lambda m, t: (m, 0, 0)),
                  pl.BlockSpec((1, SSM_UB), lambda m, t: (0, m)),
                  pl.BlockSpec((8, SUBLANES, SSM_CB), lambda m, t: (0, 0, m))],
        out_specs=[pl.BlockSpec((tt, SSM_UB), lambda m, t: (t, m)),
                   pl.BlockSpec((tt, SSM_CB), lambda m, t: (t, m)),
                   pl.BlockSpec((tt, SSM_CB), lambda m, t: (t, m))],
        out_shape=[jax.ShapeDtypeStruct((l, SSM_W), F32), jax.ShapeDtypeStruct((l, SSM_CH), F32),
                   jax.ShapeDtypeStruct((l, SSM_CH), F32)],
        scratch_shapes=[pltpu.VMEM((2 * SUBLANES, SSM_CB), F32)],
        compiler_params=_params(("parallel", "arbitrary")),
    )(u, bre, bim, cre, cim, dvec, tab)


def _ssm_bwd(dy, u, sre, sim, bre, bim, cre, cim, dvec, tab, tt):
    l = u.shape[0]
    nt = l // tt
    tpb = tt // SUBLANES

    def body(dy_ref, u_ref, sre_ref, sim_ref, hre_ref, him_ref, bre_ref, bim_ref, cre_ref, cim_ref, d_ref, tab_ref,
             du_ref, dbre_ref, dbim_ref, dcre_ref, dcim_ref, dare_ref, daim_ref, dd_ref, lr_ref, li_ref, car_ref):
        t = pl.program_id(1)

        @pl.when(t == 0)
        def _():
            car_ref[...] = jnp.zeros_like(car_ref)
            for ref in (dbre_ref, dbim_ref, dcre_ref, dcim_ref, dare_ref, daim_ref, dd_ref):
                ref[...] = jnp.zeros_like(ref)

        dyf = dy_ref[...]
        dyb = dyf.astype(BF16)
        uf = u_ref[...]
        s_re = sre_ref[...]
        s_im = sim_ref[...]
        lr_ref[...] = _dot_nt(dyb, cre_ref[0])
        li_ref[...] = -_dot_nt(dyb, cim_ref[0])
        dcre_ref[0] += _dot_tn(s_re.astype(BF16), dyb)
        dcim_ref[0] -= _dot_tn(s_im.astype(BF16), dyb)
        tab_v = [tab_ref[n] for n in range(8)]
        cr, ci = _scan_tiles(lr_ref, li_ref, tab_v, (car_ref[0:1, :], car_ref[8:9, :]), tpb, True)
        car_ref[0:1, :] = cr
        car_ref[8:9, :] = ci
        lam_r = lr_ref[...]
        lam_i = li_ref[...]
        keep = jnp.where(t == nt - 1, 0.0, 1.0)
        sp_r = _shift_down(s_re, 1, hre_ref[...] * keep)
        sp_i = _shift_down(s_im, 1, him_ref[...] * keep)
        dare_ref[...] += jnp.sum(lam_r * sp_r + lam_i * sp_i, axis=0, keepdims=True)
        daim_ref[...] += jnp.sum(lam_i * sp_r - lam_r * sp_i, axis=0, keepdims=True)
        lrb = lam_r.astype(BF16)
        lib = lam_i.astype(BF16)
        du_ref[...] = _dot_nt(lrb, bre_ref[0]) + _dot_nt(lib, bim_ref[0]) + dyf * d_ref[...]
        ub = uf.astype(BF16)
        dbre_ref[0] += _dot_tn(ub, lrb)
        dbim_ref[0] += _dot_tn(ub, lib)
        dd_ref[...] += jnp.sum(dyf * uf, axis=0, keepdims=True)

    rev = lambda m, t: (nt - 1 - t, m)
    halo = lambda m, t: (jnp.maximum((nt - 1 - t) * tpb - 1, 0), m)
    wb = pl.BlockSpec((1, SSM_UB, SSM_CB), lambda m, t: (m, 0, 0))
    wc = pl.BlockSpec((1, SSM_CB, SSM_UB), lambda m, t: (m, 0, 0))
    vec_c = pl.BlockSpec((1, SSM_CB), lambda m, t: (0, m))
    vec_u = pl.BlockSpec((1, SSM_UB), lambda m, t: (0, m))
    return pl.pallas_call(
        body, name="ssm_bwd", grid=(SSM_NB, nt),
        in_specs=[pl.BlockSpec((tt, SSM_UB), rev), pl.BlockSpec((tt, SSM_UB), rev),
                  pl.BlockSpec((tt, SSM_CB), rev), pl.BlockSpec((tt, SSM_CB), rev),
                  pl.BlockSpec((SUBLANES, SSM_CB), halo), pl.BlockSpec((SUBLANES, SSM_CB), halo),
                  wb, wb, wc, wc, vec_u,
                  pl.BlockSpec((8, SUBLANES, SSM_CB), lambda m, t: (0, 0, m))],
        out_specs=[pl.BlockSpec((tt, SSM_UB), rev), wb, wb, wc, wc, vec_c, vec_c, vec_u],
        out_shape=[jax.ShapeDtypeStruct((l, SSM_W), F32),
                   jax.ShapeDtypeStruct((SSM_NB, SSM_UB, SSM_CB), F32), jax.ShapeDtypeStruct((SSM_NB, SSM_UB, SSM_CB), F32),
                   jax.ShapeDtypeStruct((SSM_NB, SSM_CB, SSM_UB), F32), jax.ShapeDtypeStruct((SSM_NB, SSM_CB, SSM_UB), F32),
                   jax.ShapeDtypeStruct((1, SSM_CH), F32), jax.ShapeDtypeStruct((1, SSM_CH), F32),
                   jax.ShapeDtypeStruct((1, SSM_W), F32)],
        scratch_shapes=[pltpu.VMEM((tt, SSM_CB), F32), pltpu.VMEM((tt, SSM_CB), F32),
                        pltpu.VMEM((2 * SUBLANES, SSM_CB), F32)],
        compiler_params=_params(("parallel", "arbitrary")),
    )(dy, u, sre, sim, sre, sim, bre, bim, cre, cim, dvec, tab)


def _merge_fwd(x, gl, attn, y1, wba, wbs, wglu, bglu, wout, gpost, gpre, tl):
    l = x.shape[0]

    def body(x_ref, gl_ref, at_ref, y1_ref, wba_ref, wbs_ref, wglu_ref, bglu_ref, wout_ref, gpost_ref, gpre_ref,
             a_ref, sm_ref, mg_ref, z_ref, x1_ref, hn2_ref, y3_ref):
        y2 = _gelu(y1_ref[...])
        sg = _sigmoid(_dot(y2.astype(BF16), wglu_ref[...]) + bglu_ref[...])
        y3 = (y2 * sg).astype(BF16)
        y3_ref[...] = y3
        a = _dot(at_ref[...], wba_ref[...])
        sm = _dot(y3, wbs_ref[...])
        a_ref[...] = a
        sm_ref[...] = sm
        g = _sigmoid(gl_ref[...])
        merged = (g[:, :D_MODEL] * a + g[:, D_MODEL:] * sm).astype(BF16)
        mg_ref[...] = merged
        z = _dot(merged, wout_ref[...])
        z_ref[...] = z
        n, _ = _rms(z, gpost_ref[...])
        x1 = x_ref[...] + n
        x1_ref[...] = x1
        hn2, _ = _rms(x1, gpre_ref[...])
        hn2_ref[...] = hn2.astype(BF16)

    outs = [(D_MODEL, F32), (D_MODEL, F32), (D_MODEL, BF16), (D_MODEL, F32), (D_MODEL, F32), (D_MODEL, BF16),
            (SSM_W, BF16)]
    return pl.pallas_call(
        body, name="merge_fwd", grid=(l // tl,),
        in_specs=[_row(tl, D_MODEL), _row(tl, 2 * D_MODEL), _row(tl, HP), _row(tl, SSM_W),
                  _const((HP, D_MODEL)), _const((SSM_W, D_MODEL)), _const((SSM_W, SSM_W)), _const((1, SSM_W)),
                  _const((D_MODEL, D_MODEL)), _const((1, D_MODEL)), _const((1, D_MODEL))],
        out_specs=[_row(tl, n) for n, _ in outs],
        out_shape=[jax.ShapeDtypeStruct((l, n), dt) for n, dt in outs],
        compiler_params=_params(("parallel",)),
    )(x, gl, attn, y1, wba, wbs, wglu, bglu, wout, gpost, gpre)


def _merge_bwd(dhn2a, dhn2b, x1, dx2, z, gl, a, sm, y1, wba, wbs, wglu, bglu, wout, gpost, gpre, tl):
    l = x1.shape[0]

    def body(da_ref, db_ref, x1_ref, dx2_ref, z_ref, gl_ref, a_ref, sm_ref, y1_ref,
             wba_ref, wbs_ref, wglu_ref, bglu_ref, wout_ref, gpost_ref, gpre_ref,
             dx1_ref, dz_ref, dbra_ref, dbrs_ref, dgl_ref, dat_ref, dy1_ref, dt_ref, y2_ref,
             dgpre_ref, dgpost_ref, dbg_ref, dbglu_ref):
        @pl.when(pl.program_id(0) == 0)
        def _():
            for ref in (dgpre_ref, dgpost_ref, dbg_ref, dbglu_ref):
                ref[...] = jnp.zeros_like(ref)

        dhn2 = da_ref[...] + db_ref[...]
        dx1a, dgpre = _rms_bwd(dhn2, x1_ref[...], gpre_ref[...])
        dgpre_ref[...] += dgpre
        dx1 = dx2_ref[...] + dx1a
        dx1_ref[...] = dx1
        dz, dgpost = _rms_bwd(dx1, z_ref[...], gpost_ref[...])
        dgpost_ref[...] += dgpost
        dzb = dz.astype(BF16)
        dz_ref[...] = dzb
        dm = _dot_nt(dzb, wout_ref[...])
        g = _sigmoid(gl_ref[...])
        g0 = g[:, :D_MODEL]
        g1 = g[:, D_MODEL:]
        dbra = (dm * g0).astype(BF16)
        dbrs = (dm * g1).astype(BF16)
        dbra_ref[...] = dbra
        dbrs_ref[...] = dbrs
        dgl0 = dm * a_ref[...] * g0 * (1.0 - g0)
        dgl1 = dm * sm_ref[...] * g1 * (1.0 - g1)
        dgl_ref[:, :D_MODEL] = dgl0.astype(BF16)
        dgl_ref[:, D_MODEL:] = dgl1.astype(BF16)
        dbg_ref[:, :D_MODEL] += jnp.sum(dgl0, axis=0, keepdims=True)
        dbg_ref[:, D_MODEL:] += jnp.sum(dgl1, axis=0, keepdims=True)
        dat_ref[...] = _dot_nt(dbra, wba_ref[...]).astype(BF16)
        dy3 = _dot_nt(dbrs, wbs_ref[...])
        y1v = y1_ref[...]
        y2 = _gelu(y1v)
        y2b = y2.astype(BF16)
        y2_ref[...] = y2b
        sg = _sigmoid(_dot(y2b, wglu_ref[...]) + bglu_ref[...])
        dt = dy3 * y2 * sg * (1.0 - sg)
        dtb = dt.astype(BF16)
        dt_ref[...] = dtb
        dbglu_ref[...] += jnp.sum(dt, axis=0, keepdims=True)
        dy2 = dy3 * sg + _dot_nt(dtb, wglu_ref[...])
        dy1_ref[...] = dy2 * _gelu_grad(y1v)

    outs = [(D_MODEL, F32), (D_MODEL, BF16), (D_MODEL, BF16), (D_MODEL, BF16), (2 * D_MODEL, BF16), (HP, BF16),
            (SSM_W, F32), (SSM_W, BF16), (SSM_W, BF16)]
    accs = [D_MODEL, D_MODEL, 2 * D_MODEL, SSM_W]
    return pl.pallas_call(
        body, name="merge_bwd", grid=(l // tl,),
        in_specs=[_row(tl, D_MODEL), _row(tl, D_MODEL), _row(tl, D_MODEL), _row(tl, D_MODEL), _row(tl, D_MODEL),
                  _row(tl, 2 * D_MODEL), _row(tl, D_MODEL), _row(tl, D_MODEL), _row(tl, SSM_W),
                  _const((HP, D_MODEL)), _const((SSM_W, D_MODEL)), _const((SSM_W, SSM_W)), _const((1, SSM_W)),
                  _const((D_MODEL, D_MODEL)), _const((1, D_MODEL)), _const((1, D_MODEL))],
        out_specs=[_row(tl, n) for n, _ in outs] + [_const((1, n)) for n in accs],
        out_shape=[jax.ShapeDtypeStruct((l, n), dt) for n, dt in outs]
        + [jax.ShapeDtypeStruct((1, n), F32) for n in accs],
        compiler_params=_params(("arbitrary",)),
    )(dhn2a, dhn2b, x1, dx2, z, gl, a, sm, y1, wba, wbs, wglu, bglu, wout, gpost, gpre)


def _proj_bwd(x, dx1, cq, ckv, dq, dk, dv, du, dgl, g1, win, gq, wuq, gkv, wukv, rc, rs, tl):
    l = x.shape[0]

    def body(x_ref, dx1_ref, cq_ref, ckv_ref, dq_ref, dk_ref, dv_ref, du_ref, dgl_ref,
             g1_ref, win_ref, gq_ref, wuq_ref, gkv_ref, wukv_ref, rc_ref, rs_ref,
             gx_ref, dql_ref, qn_ref, ckvn_ref, dproj_ref, dg1_ref, dgq_ref, dgkv_ref):
        @pl.when(pl.program_id(0) == 0)
        def _():
            for ref in (dg1_ref, dgq_ref, dgkv_ref):
                ref[...] = jnp.zeros_like(ref)

        c1 = rc_ref[...]
        s1 = rs_ref[...]
        dql = _rope_bwd(dq_ref[...], jnp.tile(c1, (1, N_HEADS)), jnp.tile(s1, (1, N_HEADS))).astype(BF16)
        dql_ref[...] = dql
        dqn = _dot_nt(dql, wuq_ref[...])
        cq = cq_ref[...]
        qn, _ = _rms(cq, gq_ref[...])
        qn_ref[...] = qn.astype(BF16)
        dcq, dgq = _rms_bwd(dqn, cq, gq_ref[...])
        dgq_ref[...] += dgq
        dkb = dk_ref[...]
        dvb = dv_ref[...]
        dkf = dkb.astype(F32)
        dkr = dkf[:, 0:HEAD_SLOT]
        for h in range(1, N_HEADS):
            dkr = dkr + dkf[:, h * HEAD_SLOT:(h + 1) * HEAD_SLOT]
        dkr = _rope_bwd(dkr, c1, s1)
        dckvn = _dot_nt(dkb, wukv_ref[:, :HP]) + _dot_nt(dvb, wukv_ref[:, HP:])
        ckv = ckv_ref[...]
        ckvn, _ = _rms(ckv, gkv_ref[...])
        ckvn_ref[...] = ckvn.astype(BF16)
        dckv, dgkv = _rms_bwd(dckvn, ckv, gkv_ref[...])
        dgkv_ref[...] += dgkv
        dproj_ref[:, P_CQ:P_CKV] = dcq.astype(BF16)
        dproj_ref[:, P_CKV:P_KR] = dckv.astype(BF16)
        dproj_ref[:, P_KR:P_U] = dkr.astype(BF16)
        dproj_ref[:, P_U:P_GL] = du_ref[...].astype(BF16)
        dproj_ref[:, P_GL:P_END] = dgl_ref[...]
        dhn = _dot_nt(dproj_ref[...], win_ref[...])
        dxa, dg1 = _rms_bwd(dhn, x_ref[...], g1_ref[...])
        dg1_ref[...] += dg1
        gx_ref[...] = dx1_ref[...] + dxa

    outs = [(D_MODEL, F32), (HP, BF16), (Q_RANK, BF16), (KV_RANK, BF16), (P_END, BF16)]
    accs = [D_MODEL, Q_RANK, KV_RANK]
    return pl.pallas_call(
        body, name="proj_bwd", grid=(l // tl,),
        in_specs=[_row(tl, D_MODEL), _row(tl, D_MODEL), _row(tl, Q_RANK), _row(tl, KV_RANK), _row(tl, HP),
                  _row(tl, HP), _row(tl, HP), _row(tl, SSM_W), _row(tl, 2 * D_MODEL),
                  _const((1, D_MODEL)), _const((D_MODEL, P_END)), _const((1, Q_RANK)), _const((Q_RANK, HP)),
                  _const((1, KV_RANK)), _const((KV_RANK, 2 * HP)), _row(tl, HEAD_SLOT), _row(tl, HEAD_SLOT)],
        out_specs=[_row(tl, n) for n, _ in outs] + [_const((1, n)) for n in accs],
        out_shape=[jax.ShapeDtypeStruct((l, n), dt) for n, dt in outs]
        + [jax.ShapeDtypeStruct((1, n), F32) for n in accs],
        compiler_params=_params(("arbitrary",)),
    )(x, dx1, cq, ckv, dq, dk, dv, du, dgl, g1, win, gq, wuq, gkv, wukv, rc, rs)


CONV_CB = 256
CONV_NB = D_FF // CONV_CB


def _conv3(h, halo, w, b):
    return b + w[0:1, :] * _shift_down(h, 2, halo) + w[1:2, :] * _shift_down(h, 1, halo) + w[2:3, :] * h


def _conv_fwd(h, cw, cb, tl):
    l = h.shape[0]

    def body(hg_ref, hv_ref, wg_ref, wv_ref, bg_ref, bv_ref, act_ref, halo_ref):
        @pl.when(pl.program_id(1) == 0)
        def _():
            halo_ref[...] = jnp.zeros_like(halo_ref)

        hg = hg_ref[...]
        hv = hv_ref[...]
        cg = _conv3(hg, halo_ref[0:SUBLANES, :], wg_ref[...], bg_ref[...])
        cv = _conv3(hv, halo_ref[SUBLANES:, :], wv_ref[...], bv_ref[...])
        act_ref[...] = (_gelu(cg) * cv).astype(BF16)
        halo_ref[0:SUBLANES, :] = hg[tl - SUBLANES:, :]
        halo_ref[SUBLANES:, :] = hv[tl - SUBLANES:, :]

    gmap = lambda c, r: (r, c)
    vmap = lambda c, r: (r, CONV_NB + c)
    return pl.pallas_call(
        body, name="conv_fwd", grid=(CONV_NB, l // tl),
        in_specs=[pl.BlockSpec((tl, CONV_CB), gmap), pl.BlockSpec((tl, CONV_CB), vmap),
                  pl.BlockSpec((3, CONV_CB), lambda c, r: (0, c)), pl.BlockSpec((3, CONV_CB), lambda c, r: (0, CONV_NB + c)),
                  pl.BlockSpec((1, CONV_CB), lambda c, r: (0, c)), pl.BlockSpec((1, CONV_CB), lambda c, r: (0, CONV_NB + c))],
        out_specs=pl.BlockSpec((tl, CONV_CB), gmap),
        out_shape=jax.ShapeDtypeStruct((l, D_FF), BF16),
        scratch_shapes=[pltpu.VMEM((2 * SUBLANES, CONV_CB), F32)],
        compiler_params=_params(("parallel", "arbitrary")),
    )(h, h, cw, cw, cb, cb)


def _conv_bwd(h, dact, cw, cb, tl):
    l = h.shape[0]
    nr = l // tl
    tpb = tl // SUBLANES

    def body(hg_ref, hv_ref, hgh_ref, hvh_ref, da_ref, wg_ref, wv_ref, bg_ref, bv_ref,
             dh_ref, dwg_ref, dwv_ref, dbg_ref, dbv_ref, car_ref):
        r = pl.program_id(1)

        @pl.when(r == 0)
        def _():
            for ref in (car_ref, dwg_ref, dwv_ref, dbg_ref, dbv_ref):
                ref[...] = jnp.zeros_like(ref)

        keep = jnp.where(r == nr - 1, 0.0, 1.0)
        da = da_ref[...].astype(F32)

        def half(h_ref, halo, w, b):
            hh = h_ref[...]
            h1 = _shift_down(hh, 1, halo)
            h2 = _shift_down(hh, 2, halo)
            return hh, h1, h2, b + w[0:1, :] * h2 + w[1:2, :] * h1 + w[2:3, :] * hh

        wg = wg_ref[...]
        wv = wv_ref[...]
        hg, hg1, hg2, cg = half(hg_ref, hgh_ref[...] * keep, wg, bg_ref[...])
        hv, hv1, hv2, cv = half(hv_ref, hvh_ref[...] * keep, wv, bv_ref[...])
        dcg = da * cv * _gelu_grad(cg)
        dcv = da * _gelu(cg)

        def back(dc, hh, h1, h2, w, nxt, dw_ref, db_ref, part):
            db_ref[...] += jnp.sum(dc, axis=0, keepdims=True)
            dw_ref[0:1, :] += jnp.sum(dc * h2, axis=0, keepdims=True)
            dw_ref[1:2, :] += jnp.sum(dc * h1, axis=0, keepdims=True)
            dw_ref[2:3, :] += jnp.sum(dc * hh, axis=0, keepdims=True)
            dh = w[2:3, :] * dc + w[1:2, :] * _shift_up(dc, 1, nxt) + w[0:1, :] * _shift_up(dc, 2, nxt)
            dh_ref[part] = dh.astype(BF16)

        back(dcg, hg, hg1, hg2, wg, car_ref[0:SUBLANES, :], dwg_ref, dbg_ref, 0)
        back(dcv, hv, hv1, hv2, wv, car_ref[SUBLANES:, :], dwv_ref, dbv_ref, 1)
        car_ref[0:SUBLANES, :] = dcg[0:SUBLANES, :]
        car_ref[SUBLANES:, :] = dcv[0:SUBLANES, :]

    grev = lambda c, r: (nr - 1 - r, c)
    vrev = lambda c, r: (nr - 1 - r, CONV_NB + c)
    ghalo = lambda c, r: (jnp.maximum((nr - 1 - r) * tpb - 1, 0), c)
    vhalo = lambda c, r: (jnp.maximum((nr - 1 - r) * tpb - 1, 0), CONV_NB + c)
    colg = lambda c, r: (0, c)
    colv = lambda c, r: (0, CONV_NB + c)
    return pl.pallas_call(
        body, name="conv_bwd", grid=(CONV_NB, nr),
        in_specs=[pl.BlockSpec((tl, CONV_CB), grev), pl.BlockSpec((tl, CONV_CB), vrev),
                  pl.BlockSpec((SUBLANES, CONV_CB), ghalo), pl.BlockSpec((SUBLANES, CONV_CB), vhalo),
                  pl.BlockSpec((tl, CONV_CB), grev),
                  pl.BlockSpec((3, CONV_CB), colg), pl.BlockSpec((3, CONV_CB), colv),
                  pl.BlockSpec((1, CONV_CB), colg), pl.BlockSpec((1, CONV_CB), colv)],
        out_specs=[pl.BlockSpec((2, tl, CONV_CB), lambda c, r: (0, nr - 1 - r, c)),
                   pl.BlockSpec((3, CONV_CB), colg), pl.BlockSpec((3, CONV_CB), colg),
                   pl.BlockSpec((1, CONV_CB), colg), pl.BlockSpec((1, CONV_CB), colg)],
        out_shape=[jax.ShapeDtypeStruct((2, l, D_FF), BF16),
                   jax.ShapeDtypeStruct((3, D_FF), F32), jax.ShapeDtypeStruct((3, D_FF), F32),
                   jax.ShapeDtypeStruct((1, D_FF), F32), jax.ShapeDtypeStruct((1, D_FF), F32)],
        scratch_shapes=[pltpu.VMEM((2 * SUBLANES, CONV_CB), F32)],
        compiler_params=_params(("parallel", "arbitrary")),
    )(h, h, h, h, dact, cw, cw, cb, cb)


def _loss_head(ff, x1, tgt, g, tl):
    l = ff.shape[0]

    def body(ff_ref, x1_ref, tg_ref, g_ref, loss_ref, dx2_ref, dff_ref, dg_ref):
        @pl.when(pl.program_id(0) == 0)
        def _():
            loss_ref[...] = jnp.zeros_like(loss_ref)
            dg_ref[...] = jnp.zeros_like(dg_ref)

        f = ff_ref[...]
        gv = g_ref[...]
        n, _ = _rms(f, gv)
        e = x1_ref[...] + n - tg_ref[...]
        loss_ref[...] += 0.5 * jnp.sum(jnp.mean(e * e, axis=-1, keepdims=True), axis=0, keepdims=True)
        dx2 = e * (1.0 / D_MODEL)
        dx2_ref[...] = dx2
        dff, dg = _rms_bwd(dx2, f, gv)
        dff_ref[...] = dff.astype(BF16)
        dg_ref[...] += dg

    return pl.pallas_call(
        body, name="loss_head", grid=(l // tl,),
        in_specs=[_row(tl, D_MODEL), _row(tl, D_MODEL), _row(tl, D_MODEL), _const((1, D_MODEL))],
        out_specs=[_const((1, LANES)), _row(tl, D_MODEL), _row(tl, D_MODEL), _const((1, D_MODEL))],
        out_shape=[jax.ShapeDtypeStruct((1, LANES), F32), jax.ShapeDtypeStruct((l, D_MODEL), F32),
                   jax.ShapeDtypeStruct((l, D_MODEL), BF16), jax.ShapeDtypeStruct((1, D_MODEL), F32)],
        compiler_params=_params(("arbitrary",)),
    )(ff, x1, tgt, g)


def _ssm_disc(lam_re, lam_im, log_dt, b_re, b_im):
    dt = jnp.exp(log_dt)[:, None]
    mag = jnp.exp(lam_re * dt)
    ang = lam_im * dt
    a_re, a_im = mag * jnp.cos(ang), mag * jnp.sin(ang)
    den = lam_re * lam_re + lam_im * lam_im
    n_re, n_im = a_re - 1.0, a_im
    z_re = (n_re * lam_re + n_im * lam_im) / den
    z_im = (n_im * lam_re - n_re * lam_im) / den
    bb_re = z_re[..., None] * b_re - z_im[..., None] * b_im
    bb_im = z_re[..., None] * b_im + z_im[..., None] * b_re
    return a_re, a_im, bb_re, bb_im


_GPB = SSM_CB // SSM_P


def _embed_b(bb):
    t = bb.transpose(0, 2, 1).reshape(SSM_NB, _GPB, SSM_H, SSM_P)
    return jnp.einsum('mjhp,jk->mjhkp', t, jnp.eye(_GPB, dtype=bb.dtype)).reshape(SSM_NB, SSM_UB, SSM_CB)


def _extract_b(d):
    t = d.reshape(SSM_NB, _GPB, SSM_H, _GPB, SSM_P)
    t = jnp.einsum('mjhkp,jk->mjhp', t, jnp.eye(_GPB, dtype=d.dtype))
    return t.reshape(SSM_G, SSM_H, SSM_P).transpose(0, 2, 1)


def _embed_c(c):
    t = c.transpose(0, 2, 1).reshape(SSM_NB, _GPB, SSM_P, SSM_H)
    return jnp.einsum('mjph,jk->mjpkh', t, jnp.eye(_GPB, dtype=c.dtype)).reshape(SSM_NB, SSM_CB, SSM_UB)


def _extract_c(d):
    t = d.reshape(SSM_NB, _GPB, SSM_P, _GPB, SSM_H)
    t = jnp.einsum('mjpkh,jk->mjph', t, jnp.eye(_GPB, dtype=d.dtype))
    return t.reshape(SSM_G, SSM_P, SSM_H).transpose(0, 2, 1)


def _scan_tables(a_re, a_im, reverse):
    ar = a_re.reshape(1, SSM_CH)
    ai = (-a_im if reverse else a_im).reshape(1, SSM_CH)
    pr, pi = [ar], [ai]
    for _ in range(SUBLANES - 1):
        pr, pi = pr + [pr[-1] * ar - pi[-1] * ai], pi + [pr[-1] * ai + pi[-1] * ar]
    rows = jnp.arange(SUBLANES)[:, None]
    out = []
    for k in (1, 2, 4):
        valid = (rows + k <= SUBLANES - 1) if reverse else (rows >= k)
        out += [jnp.where(valid, pr[k - 1], 0.0), jnp.where(valid, pi[k - 1], 0.0)]
    order = list(range(SUBLANES - 1, -1, -1)) if reverse else list(range(SUBLANES))
    out += [jnp.concatenate([pr[n] for n in order], axis=0), jnp.concatenate([pi[n] for n in order], axis=0)]
    return jnp.stack(out).astype(F32)


def _pad_heads(w, d):
    lead = w.shape[:-1]
    w = w.reshape(lead + (N_HEADS, d))
    w = jnp.pad(w, [(0, 0)] * len(lead) + [(0, 0), (0, HEAD_SLOT - d)])
    return w.reshape(lead + (HP,))


def _unpad_heads(w, d):
    lead = w.shape[:-1]
    return w.reshape(lead + (N_HEADS, HEAD_SLOT))[..., :d].reshape(lead + (N_HEADS * d,))


def _chip_major(w, axis):
    k, n = w.shape
    if axis == 0:
        return w.reshape(N_CHIPS, k // N_CHIPS, n)
    return w.reshape(k, N_CHIPS, n // N_CHIPS).transpose(1, 0, 2)


def _from_chip_major(w, axis):
    if axis == 0:
        return w.reshape(-1, w.shape[2])
    return w.transpose(1, 0, 2).reshape(w.shape[1], -1)


def _pad_w_in(w):
    z = lambda n: jnp.zeros((w.shape[0], n), w.dtype)
    return jnp.concatenate([w[:, :640], z(KR_LANE), w[:, 640:672], z(HEAD_SLOT - KR_LANE - QK_ROPE), w[:, 672:]], axis=1)


def _unpad_w_in(w):
    return jnp.concatenate([w[:, :640], w[:, P_KR + KR_LANE:P_KR + KR_LANE + QK_ROPE], w[:, P_U:]], axis=1)


def _local_step(x, positions, tgt, wts, sp):
    l = x.shape[0]
    tl = min(256, l)
    ta = min(512, l)

    inv_freq = ROPE_THETA ** (-jnp.arange(0, QK_ROPE, 2, dtype=F32) / QK_ROPE)
    ang = positions.astype(F32)[:, None] * inv_freq
    cos, sin = jnp.cos(ang), jnp.sin(ang)
    one = jnp.ones((l, KR_LANE), F32)
    rc = jnp.concatenate([one, cos, cos, jnp.ones((l, HEAD_SLOT - KR_LANE - QK_ROPE), F32)], axis=1)
    rs = jnp.concatenate([0 * one, -sin, sin, jnp.zeros((l, HEAD_SLOT - KR_LANE - QK_ROPE), F32)], axis=1)

    win = _pad_w_in(wts["w_in"])
    wuq = _pad_heads(wts["w_uq"], QK_HEAD)
    wukv = jnp.concatenate([_pad_heads(wts["w_uk"], QK_NOPE), _pad_heads(wts["w_uv"], V_HEAD)], axis=1)
    wba = jnp.pad(wts["w_branch_attn"].reshape(N_HEADS, V_HEAD, D_MODEL),
                  ((0, 0), (0, HEAD_SLOT - V_HEAD), (0, 0))).reshape(HP, D_MODEL)
    wbs, wglu, wout = wts["w_branch_ssm"], wts["w_glu"], wts["w_out"]

    disc_in = (sp["ssm_lambda_re"], sp["ssm_lambda_im"], sp["ssm_log_dt"], sp["ssm_b_re"], sp["ssm_b_im"])
    (a_re, a_im, bb_re, bb_im), disc_vjp = jax.vjp(_ssm_disc, *disc_in)
    bre, bim = _embed_b(bb_re).astype(BF16), _embed_b(bb_im).astype(BF16)
    cre, cim = _embed_c(sp["ssm_c_re"]).astype(BF16), _embed_c(sp["ssm_c_im"]).astype(BF16)
    dvec = sp["ssm_d"].reshape(1, SSM_W)
    tab_f = _scan_tables(a_re, a_im, False)
    tab_r = _scan_tables(a_re, a_im, True)

    g1, gq, gkv = sp["mix_norm_pre"], sp["q_norm"], sp["kv_norm"]
    gpost, gpre, gfin = sp["mix_norm_post"], sp["ffn_norm_pre"], sp["ffn_norm_post"]
    bgate, bglu, convb = sp["b_gate"], sp["b_glu"], sp["conv_b"]
    convw = sp["conv_w"]

    hn, cq, ckv, q, k, v, u, gl = _proj_fwd(x, g1, win, gq, wuq, gkv, wukv, rc, rs, bgate, tl)
    attn, lse = _attn_fwd(q, k, v, ta)
    y1, sre, sim = _ssm_fwd(u, bre, bim, cre, cim, dvec, tab_f, ta)
    a, sm, merged, z, x1, hn2, y3 = _merge_fwd(x, gl, attn, y1, wba, wbs, wglu, bglu, wout, gpost, gpre, tl)
    wup, wdown = wts["ffn"](x1)
    h = _mm(hn2, wup, "ffn_up")
    act = _conv_fwd(h, convw, convb, ta)
    ff = _mm(act, wdown, "ffn_down")
    loss, dx2, dff, dgfin = _loss_head(ff, x1, tgt, gfin, tl)

    dact = _mm(dff, wdown, "ffn_down_dx", out_dtype=BF16, bt=True)
    d_wdown = _mm_tn(act, dff, "ffn_down_dw")
    dh, dwg, dwv, dbg, dbv = _conv_bwd(h, dact, convw, convb, ta)
    d_convw = jnp.concatenate([dwg, dwv], axis=1)
    d_convb = jnp.concatenate([dbg, dbv], axis=1)
    dhn2a = _mm(dh, wup, "ffn_up_dx_gate", bt=True, b_col0=0, a_lead=0)
    dhn2b = _mm(dh, wup, "ffn_up_dx_val", bt=True, b_col0=1, a_lead=1)
    d_wup = _mm_tn(hn2, dh, "ffn_up_dw", chips=True)
    (dx1, dz, dbra, dbrs, dgl, dattn, dy1, dt, y2, dgpre, dgpost, dbgate, dbglu) = _merge_bwd(
        dhn2a, dhn2b, x1, dx2, z, gl, a, sm, y1, wba, wbs, wglu, bglu, wout, gpost, gpre, tl)
    d_wout = _mm_tn(merged, dz, "w_out_dw")
    d_wba = _mm_tn(attn, dbra, "w_branch_attn_dw", chips=True)
    d_wbs = _mm_tn(y3, dbrs, "w_branch_ssm_dw", chips=True)
    d_wglu = _mm_tn(y2, dt, "w_glu_dw")
    dq, dk, dv = _attn_bwd(q, k, v, dattn, lse, _attn_delta(attn, dattn, min(2048, l)), ta)
    du, dbre, dbim, dcre, dcim, dare, daim, dd = _ssm_bwd(dy1, u, sre, sim, bre, bim, cre, cim, dvec, tab_r, ta)
    gx, dql, qn, ckvn, dproj, dg1, dgq, dgkv = _proj_bwd(
        x, dx1, cq, ckv, dq, dk, dv, du, dgl, g1, win, gq, wuq, gkv, wukv, rc, rs, tl)
    d_win = _mm_tn(hn, dproj, "w_in_dw")
    d_wuq = _mm_tn(qn, dql, "w_uq_dw")
    d_wuk = _mm_tn(ckvn, dk, "w_uk_dw")
    d_wuv = _mm_tn(ckvn, dv, "w_uv_dw")

    d_lre, d_lim, d_ldt, d_bre, d_bim = disc_vjp((dare.reshape(SSM_G, SSM_P), daim.reshape(SSM_G, SSM_P),
                                                  _extract_b(dbre), _extract_b(dbim)))
    ncol = D_MODEL // N_CHIPS
    big = {
        "w_in": _chip_major(_unpad_w_in(d_win), 1),
        "w_uq": _chip_major(_unpad_heads(d_wuq, QK_HEAD), 1),
        "w_uk": _chip_major(_unpad_heads(d_wuk, QK_NOPE), 1),
        "w_uv": _chip_major(_unpad_heads(d_wuv, V_HEAD), 1),
        "w_glu": _chip_major(d_wglu, 0),
        "w_branch_attn": d_wba.reshape(N_CHIPS, N_HEADS, HEAD_SLOT, ncol)[:, :, :V_HEAD].reshape(
            N_CHIPS, N_HEADS * V_HEAD, ncol),
        "w_branch_ssm": d_wbs,
        "w_out": _chip_major(d_wout, 0),
        "w_up": d_wup,
        "w_down": _chip_major(d_wdown, 0),
    }
    small = {
        "conv_w": d_convw,
        "mix_norm_pre": dg1, "q_norm": dgq, "kv_norm": dgkv,
        "ssm_lambda_re": d_lre, "ssm_lambda_im": d_lim, "ssm_log_dt": d_ldt,
        "ssm_b_re": d_bre, "ssm_b_im": d_bim,
        "ssm_c_re": _extract_c(dcre), "ssm_c_im": _extract_c(dcim),
        "ssm_d": dd.reshape(SSM_G, SSM_H), "b_glu": dbglu, "b_gate": dbgate,
        "mix_norm_post": dgpost, "ffn_norm_pre": dgpre, "conv_b": d_convb, "ffn_norm_post": dgfin,
    }
    return loss[0, 0], gx, big, small


_ANY = pl.BlockSpec(memory_space=pl.ANY)


ROW_TILE = 16


def _place():
    x, y, c = lax.axis_index("x"), lax.axis_index("y"), lax.axis_index("c")
    return x, y, c, 2 * x + y, [(1 - x, y), (x, 1 - y), (1 - x, 1 - y)]


def _half(rows, which):
    hr = rows // 2
    return pl.ds(pl.multiple_of(which * hr, ROW_TILE), hr)


def _remote(src, dst, send_sems, recv_sems, n, dev):
    return pltpu.make_async_remote_copy(src_ref=src, dst_ref=dst, send_sem=send_sems.at[n], recv_sem=recv_sems.at[n],
                                        device_id=dev, device_id_type=MESH)


def _gather_big(shards):
    nw = len(shards)
    rows = [s.shape[0] for s in shards]

    def body(*refs):
        ins, outs = refs[:nw], refs[nw:2 * nw]
        ici_send, ici_recv, d2d_send, d2d_recv = refs[2 * nw:]
        x, y, c, me, peers = _place()
        sent = []
        for i in range(nw):
            for p, (px, py) in enumerate(peers):
                cp = _remote(ins[i].at[_half(rows[i], c)], outs[i].at[me, _half(rows[i], c)], ici_send, ici_recv,
                             3 * i + p, (px, py, c))
                cp.start()
                sent.append(cp)
        for p, (px, py) in enumerate(peers):
            for i in range(nw):
                blk = outs[i].at[2 * px + py, _half(rows[i], c)]
                _remote(blk, blk, ici_send, ici_recv, 3 * i + p, (px, py, c)).wait_recv()
                cp = _remote(blk, blk, d2d_send, d2d_recv, 3 * i + p, (x, y, 1 - c))
                cp.start()
                sent.append(cp)
        for p, (px, py) in enumerate(peers):
            for i in range(nw):
                blk = outs[i].at[2 * px + py, _half(rows[i], 1 - c)]
                _remote(blk, blk, d2d_send, d2d_recv, 3 * i + p, (x, y, 1 - c)).wait_recv()
        for cp in sent:
            cp.wait_send()

    dma = pltpu.SemaphoreType.DMA
    return pl.pallas_call(
        body, name="gather_weights", in_specs=[_ANY] * nw, out_specs=[_ANY] * nw,
        out_shape=[jax.ShapeDtypeStruct((N_CHIPS,) + s.shape, s.dtype) for s in shards],
        scratch_shapes=[dma((3 * nw,)), dma((3 * nw,)), dma((3 * nw,)), dma((3 * nw,))],
    )(*shards)


_HBM = pl.BlockSpec(memory_space=pltpu.HBM)
_SEM = pl.BlockSpec(memory_space=pltpu.SEMAPHORE)
_DATAFLOW = pltpu.SideEffectType.DATAFLOW_SIDE_EFFECTING


def _gather_start(shards, name):
    nw = len(shards)
    lands = [lax.empty((N_CHIPS,) + s.shape, s.dtype) for s in shards]

    def body(*refs):
        ins, zones = refs[:nw], refs[nw:2 * nw]
        send_sems, recv_sems, token = refs[2 * nw], refs[2 * nw + 1], refs[-1]
        x, y, c, me, peers = _place()
        for i in range(nw):
            for p, (px, py) in enumerate(peers):
                _remote(ins[i], zones[i].at[me], send_sems, recv_sems, 3 * i + p, (px, py, c)).start()
        token[...] = jnp.zeros_like(token)

    thru = [pltpu.HBM(a.shape, a.dtype) for a in list(shards) + lands]
    dma = pltpu.SemaphoreType.DMA
    outs = pl.pallas_call(
        body, name=name,
        out_shape=(dma((3 * nw,)), dma((3 * nw,)), *thru, jax.ShapeDtypeStruct((SUBLANES, LANES), F32)),
        in_specs=[_HBM] * (2 * nw),
        out_specs=(_SEM, _SEM, *([_HBM] * (2 * nw)), pl.BlockSpec(memory_space=pltpu.VMEM)),
        input_output_aliases={i: 2 + i for i in range(2 * nw)},
        compiler_params=pltpu.CompilerParams(has_side_effects=_DATAFLOW),
    )(*[pltpu.with_memory_space_constraint(a, pltpu.HBM) for a in list(shards) + lands])
    return outs[0], outs[1], list(outs[2:2 + nw]), list(outs[2 + nw:2 + 2 * nw]), outs[-1]


def _gather_wait(send_sems, recv_sems, shards, lands, after, name):
    nw = len(shards)

    def body(*refs):
        ins, zones = refs[:nw], refs[nw:2 * nw]
        send_sems, recv_sems = refs[2 * nw], refs[2 * nw + 1]
        x, y, c, me, peers = _place()
        for i in range(nw):
            for p, (px, py) in enumerate(peers):
                cp = _remote(ins[i], zones[i].at[2 * px + py], send_sems, recv_sems, 3 * i + p, (px, py, c))
                cp.wait_send()
                cp.wait_recv()

    both = list(shards) + list(lands)
    outs = pl.pallas_call(
        body, name=name,
        out_shape=tuple(pltpu.HBM(a.shape, a.dtype) for a in both),
        in_specs=(*([_HBM] * (2 * nw)), _SEM, _SEM, _ANY), out_specs=[_HBM] * (2 * nw),
        input_output_aliases={i: i for i in range(2 * nw)},
        compiler_params=pltpu.CompilerParams(has_side_effects=_DATAFLOW),
    )(*both, send_sems, recv_sems, after)
    return list(outs[:nw]), list(outs[nw:])


def _reduce_to_sibling(grads):
    nw = len(grads)

    def body(*refs):
        ins, outs = refs[:nw], refs[nw:2 * nw]
        send_sems, recv_sems = refs[2 * nw:]
        x, y, c, _, _ = _place()
        sent = []
        for i in range(nw):
            cp = _remote(ins[i].at[pl.ds(0, N_CHIPS), _half(grads[i].shape[1], 1 - c)], outs[i], send_sems, recv_sems,
                         i, (x, y, 1 - c))
            cp.start()
            sent.append(cp)
        for cp in sent:
            cp.wait()

    dma = pltpu.SemaphoreType.DMA
    return pl.pallas_call(
        body, name="reduce_grads_d2d", in_specs=[_ANY] * nw, out_specs=[_ANY] * nw,
        out_shape=[jax.ShapeDtypeStruct((N_CHIPS, g.shape[1] // 2, g.shape[2]), g.dtype) for g in grads],
        scratch_shapes=[dma((nw,)), dma((nw,))],
    )(*grads)


def _reduce_between_chips(pairs):
    nw = len(pairs)

    def body(*refs):
        ins, outs = refs[:nw], refs[nw:2 * nw]
        send_sems, recv_sems = refs[2 * nw:]
        x, y, c, me, peers = _place()
        sent = []
        for i in range(nw):
            for p, (px, py) in enumerate(peers):
                cp = _remote(ins[i].at[2 * px + py], outs[i].at[me], send_sems, recv_sems, 3 * i + p, (px, py, c))
                cp.start()
                sent.append(cp)
        for i in range(nw):
            for p, (px, py) in enumerate(peers):
                blk = outs[i].at[2 * px + py]
                _remote(blk, blk, send_sems, recv_sems, 3 * i + p, (px, py, c)).wait_recv()
        for cp in sent:
            cp.wait_send()

    dma = pltpu.SemaphoreType.DMA
    return pl.pallas_call(
        body, name="reduce_grads_ici", in_specs=[_ANY] * nw, out_specs=[_ANY] * nw,
        out_shape=[jax.ShapeDtypeStruct(p.shape, p.dtype) for p in pairs],
        scratch_shapes=[dma((3 * nw,)), dma((3 * nw,))],
    )(*pairs)


def _reduce_back(totals):
    nw = len(totals)

    def body(*refs):
        outs = refs[nw:2 * nw]
        send_sems, recv_sems = refs[2 * nw:]
        x, y, c, _, _ = _place()
        sent = []
        for i in range(nw):
            blk = outs[i].at[_half(totals[i].shape[0], c)]
            cp = _remote(blk, blk, send_sems, recv_sems, i, (x, y, 1 - c))
            cp.start()
            sent.append(cp)
        for i in range(nw):
            blk = outs[i].at[_half(totals[i].shape[0], 1 - c)]
            _remote(blk, blk, send_sems, recv_sems, i, (x, y, 1 - c)).wait_recv()
        for cp in sent:
            cp.wait_send()

    dma = pltpu.SemaphoreType.DMA
    return pl.pallas_call(
        body, name="reduce_grads_back", in_specs=[_ANY] * nw, out_specs=[_ANY] * nw,
        out_shape=[jax.ShapeDtypeStruct(t.shape, t.dtype) for t in totals],
        input_output_aliases={i: i for i in range(nw)},
        scratch_shapes=[dma((nw,)), dma((nw,))],
    )(*totals)


def _all_reduce_small(v, name):
    rows, w = v.shape

    def body(v_ref, out_ref, buf_ref, send_sems, recv_sems):
        x, y, c = lax.axis_index("x"), lax.axis_index("y"), lax.axis_index("c")
        me = 4 * x + 2 * y + c

        def flip(n):
            return (1 - x if n & 4 else x, 1 - y if n & 2 else y, 1 - c if n & 1 else c)

        sent = []
        for n in range(1, 8):
            cp = pltpu.make_async_remote_copy(src_ref=v_ref, dst_ref=buf_ref.at[me], send_sem=send_sems.at[n - 1],
                                              recv_sem=recv_sems.at[n - 1], device_id=flip(n), device_id_type=MESH)
            cp.start()
            sent.append(cp)
        buf_ref[me] = v_ref[...]
        for n in range(1, 8):
            px, py, pc = flip(n)
            pltpu.make_async_remote_copy(src_ref=v_ref, dst_ref=buf_ref.at[4 * px + 2 * py + pc],
                                         send_sem=send_sems.at[n - 1], recv_sem=recv_sems.at[n - 1],
                                         device_id=flip(n), device_id_type=MESH).wait_recv()
        for cp in sent:
            cp.wait_send()
        acc = buf_ref[0]
        for d in range(1, 8):
            acc = acc + buf_ref[d]
        out_ref[...] = acc

    vm = pl.BlockSpec(memory_space=pltpu.VMEM)
    return pl.pallas_call(
        body, name=name, in_specs=[vm], out_specs=vm,
        out_shape=jax.ShapeDtypeStruct((rows, w), F32),
        scratch_shapes=[pltpu.VMEM((8, rows, w), F32), pltpu.SemaphoreType.DMA((7,)), pltpu.SemaphoreType.DMA((7,))],
        compiler_params=pltpu.CompilerParams(vmem_limit_bytes=VMEM_LIMIT),
    )(v)


ELEMENTWISE_BLOCK = 256 * 1024


def _rows_tile(rows, cols):
    best = None
    for t in range(SUBLANES, rows + 1, SUBLANES):
        if rows % t == 0 and t * cols <= ELEMENTWISE_BLOCK:
            best = t
    return rows if best is None else best


def _add_pair(g, t, core, name):
    nb, n, w = t.shape
    tr = _rows_tile(n, w)
    steps = n // tr

    def body(core_ref, g_ref, t_ref, o_ref):
        o_ref[...] = (g_ref[...] + t_ref[...]).astype(BF16)

    spec = pl.BlockSpec((1, tr, w), lambda j, i, core_ref: (j, i, 0))
    return pl.pallas_call(
        body, name=name,
        grid_spec=pltpu.PrefetchScalarGridSpec(
            num_scalar_prefetch=1, grid=(nb, steps),
            in_specs=[pl.BlockSpec((1, tr, w), lambda j, i, core_ref: (j, core_ref[0] * steps + i, 0)), spec],
            out_specs=spec),
        out_shape=jax.ShapeDtypeStruct(t.shape, BF16),
        compiler_params=_params(("parallel", "parallel")))(core, g, t)


def _add_chips(landed, pairs, place, name):
    nb, n, w = landed.shape
    tr = _rows_tile(n, w)
    steps = n // tr

    def body(place_ref, r_ref, own_ref, o_ref):
        me = place_ref[0]
        acc = None
        for k in range(nb):
            blk = jnp.where(me == k, own_ref[0], r_ref[k]).astype(F32)
            acc = blk if acc is None else acc + blk
        o_ref[...] = acc

    return pl.pallas_call(
        body, name=name,
        grid_spec=pltpu.PrefetchScalarGridSpec(
            num_scalar_prefetch=1, grid=(steps,),
            in_specs=[pl.BlockSpec((nb, tr, w), lambda i, place_ref: (0, i, 0)),
                      pl.BlockSpec((1, tr, w), lambda i, place_ref: (place_ref[0], i, 0))],
            out_specs=pl.BlockSpec((tr, w), lambda i, place_ref: (place_ref[1] * steps + i, 0))),
        out_shape=jax.ShapeDtypeStruct((2 * n, w), F32),
        compiler_params=_params(("parallel",)))(place, landed, pairs)


def _adamw(w, g, m, v, name):
    rows, wd = w.shape
    tr = _rows_tile(rows, wd)
    c1 = 1.0 - ADAM_B1 ** ADAM_STEP
    c2 = 1.0 - ADAM_B2 ** ADAM_STEP

    def body(w_ref, g_ref, m_ref, v_ref, d_ref, mo_ref, vo_ref):
        gv = g_ref[...]
        m2 = ADAM_B1 * m_ref[...] + (1.0 - ADAM_B1) * gv
        v2 = ADAM_B2 * v_ref[...] + (1.0 - ADAM_B2) * (gv * gv)
        mo_ref[...] = m2
        vo_ref[...] = v2
        d_ref[...] = -ADAM_LR * ((m2 / c1) / (jnp.sqrt(v2 / c2) + ADAM_EPS) + ADAM_WD * w_ref[...])

    spec = pl.BlockSpec((tr, wd), lambda i: (i, 0))
    shp = jax.ShapeDtypeStruct((rows, wd), F32)
    return pl.pallas_call(body, name=name, grid=(rows // tr,), in_specs=[spec] * 4, out_specs=[spec] * 3,
                          out_shape=[shp] * 3, compiler_params=_params(("parallel",)))(w, g, m, v)


BIG = [("w_in", (1024, 3232), 1), ("w_uq", (384, 768), 1), ("w_uk", (256, 512), 1), ("w_uv", (256, 512), 1),
       ("w_glu", (512, 512), 0), ("w_branch_attn", (512, 1024), 1), ("w_branch_ssm", (512, 1024), 1),
       ("w_out", (1024, 1024), 0), ("w_up", (1024, 5632), 1), ("conv_w", (3, 5632), 1), ("w_down", (2816, 1024), 0)]
SMALL = [("mix_norm_pre", (1024,)), ("q_norm", (384,)), ("kv_norm", (256,)), ("ssm_lambda_re", (32, 64)),
         ("ssm_lambda_im", (32, 64)), ("ssm_log_dt", (32,)), ("ssm_b_re", (32, 64, 16)), ("ssm_b_im", (32, 64, 16)),
         ("ssm_c_re", (32, 16, 64)), ("ssm_c_im", (32, 16, 64)), ("ssm_d", (32, 16)), ("b_glu", (512,)),
         ("b_gate", (2048,)), ("mix_norm_post", (1024,)), ("ffn_norm_pre", (1024,)), ("conv_b", (5632,)),
         ("ffn_norm_post", (1024,))]
MATMUL_W = [b for b in BIG if b[0] != "conv_w"]
FFN_W = ("w_up", "w_down")
CONV_W_SHAPE = (3, 2 * D_FF)
CONV_W_SHARD = (3, 2 * D_FF // N_CHIPS)
SMALL_SUM = [("loss", (1,))] + SMALL + [("conv_w", CONV_W_SHAPE)]
SMALL_ADAM = SMALL + [("conv_w", CONV_W_SHARD)]


def _pack_flat(layout, vals):
    flat = jnp.concatenate([vals[n].astype(F32).reshape(-1) for n, _ in layout])
    rows = -(-(-(-flat.shape[0] // FLAT_W)) // SUBLANES) * SUBLANES
    return jnp.pad(flat, (0, rows * FLAT_W - flat.shape[0])).reshape(rows, FLAT_W)


def _unpack_flat(layout, flat):
    flat = flat.reshape(-1)
    out = {}
    o = 0
    for name, shape in layout:
        n = math.prod(shape)
        out[name] = flat[o:o + n].reshape(shape)
        o += n
    return out


_ARG_NAMES = ["x", "positions"] + [n for n in (
    "mix_norm_pre", "w_in", "q_norm", "w_uq", "kv_norm", "w_uk", "w_uv", "ssm_lambda_re", "ssm_lambda_im", "ssm_log_dt",
    "ssm_b_re", "ssm_b_im", "ssm_c_re", "ssm_c_im", "ssm_d", "w_glu", "b_glu", "w_branch_attn", "w_branch_ssm",
    "b_gate", "w_out", "mix_norm_post", "ffn_norm_pre", "w_up", "conv_w", "conv_b", "w_down", "ffn_norm_post")]
_WEIGHTS = _ARG_NAMES[2:]


def _gather_weights(w):
    c = lax.axis_index("c")
    early = [b for b in MATMUL_W if b[0] not in FFN_W]
    late = [b for b in MATMUL_W if b[0] in FFN_W]
    own = (jnp.arange(N_CHIPS) == 2 * lax.axis_index("x") + lax.axis_index("y"))[:, None, None]

    def whole(layout, mine, gathered):
        return {name: _from_chip_major(jnp.where(own, s[None], g), axis)
                for (name, _, axis), s, g in zip(layout, mine, gathered)}

    mine = [w[name].astype(BF16) for name, _, _ in early]
    full = whole(early, mine, _gather_big(mine))
    mine_late = [w[name].astype(BF16) for name, _, _ in late]
    _, mine_late = lax.optimization_barrier((full["w_in"], mine_late))
    send_sems, recv_sems, shards_thru, lands_thru, token = _gather_start(mine_late, "gather_ffn_start")

    def ffn_weights(after):
        shards, lands = _gather_wait(send_sems, recv_sems, shards_thru, lands_thru, after, "gather_ffn_wait")
        got = whole(late, shards, lands)
        return got["w_up"], got["w_down"]

    full["ffn"] = ffn_weights
    full["token"] = token[0, 0]

    chip = 2 * lax.axis_index("x") + lax.axis_index("y")
    ncol = 2 * D_FF // N_CHIPS
    placed = lax.dynamic_update_slice_in_dim(jnp.zeros((3, 2 * D_FF), F32), w["conv_w"], chip * ncol, axis=1)
    placed = placed * jnp.where(c == 0, 1.0, 0.0)
    cw_rows = -(-(-(-3 * 2 * D_FF // FLAT_W)) // SUBLANES) * SUBLANES
    placed = jnp.pad(placed.reshape(-1), (0, cw_rows * FLAT_W - 3 * 2 * D_FF)).reshape(cw_rows, FLAT_W)
    conv_w_full = _all_reduce_small(placed, "gather_conv_w").reshape(-1)[:3 * 2 * D_FF].reshape(3, 2 * D_FF)
    return full, conv_w_full


def _reduce_grads(gbig, loss, gsmall):
    core = lax.axis_index("c").astype(jnp.int32).reshape(1)
    chip = (2 * lax.axis_index("x") + lax.axis_index("y")).astype(jnp.int32).reshape(1)
    place = jnp.concatenate([chip, core])
    names = [name for name, _, _ in MATMUL_W]
    grads = [gbig[n] for n in names]
    theirs = _reduce_to_sibling(grads)
    pairs = [_add_pair(g, t, core, "reduce_pair_" + n) for n, g, t in zip(names, grads, theirs)]
    landed = _reduce_between_chips(pairs)
    totals = [_add_chips(r, p, place, "reduce_chips_" + n) for n, r, p in zip(names, landed, pairs)]
    g_red = dict(zip(names, _reduce_back(totals)))

    vals = dict(gsmall)
    vals["loss"] = loss
    small_red = _unpack_flat(SMALL_SUM, _all_reduce_small(_pack_flat(SMALL_SUM, vals), "reduce_small"))
    return g_red, small_red


def _step(args):
    x = args["x"][0]
    positions = args["positions"][0]
    tgt = args["loss_target"][0]
    w = {n: args[n][0] for n in _WEIGHTS}
    m = {n: args["m_" + n][0] for n in _WEIGHTS}
    v = {n: args["v_" + n][0] for n in _WEIGHTS}

    full, conv_w_full = _gather_weights(w)
    sp = {n: w[n].reshape(s) for n, s in SMALL}
    for n in ("mix_norm_pre", "q_norm", "kv_norm", "b_glu", "b_gate", "mix_norm_post", "ffn_norm_pre", "conv_b",
              "ffn_norm_post"):
        sp[n] = sp[n].reshape(1, -1)
    sp["conv_w"] = conv_w_full
    sp["mix_norm_pre"] = sp["mix_norm_pre"] + full.pop("token")
    loss, gx, gbig, gsmall = _local_step(x, positions, tgt, full, sp)
    g_red, small_red = _reduce_grads(gbig, loss, gsmall)

    chip = 2 * lax.axis_index("x") + lax.axis_index("y")
    grads = dict(small_red)
    grads["conv_w"] = lax.dynamic_slice_in_dim(small_red["conv_w"], chip * CONV_W_SHARD[1], CONV_W_SHARD[1], axis=1)
    grads.update(g_red)

    outs = {"grad_" + n: grads[n] for n in _WEIGHTS}
    for name, _, _ in MATMUL_W:
        d, m2, v2 = _adamw(w[name], grads[name], m[name], v[name], "adamw_" + name)
        outs["delta_" + name], outs["new_m_" + name], outs["new_v_" + name] = d, m2, v2
    d_sm, m_sm, v_sm = _adamw(_pack_flat(SMALL_ADAM, w), _pack_flat(SMALL_ADAM, grads), _pack_flat(SMALL_ADAM, m),
                              _pack_flat(SMALL_ADAM, v), "adamw_small")
    for prefix, flat in (("delta_", d_sm), ("new_m_", m_sm), ("new_v_", v_sm)):
        for n, val in _unpack_flat(SMALL_ADAM, flat).items():
            outs[prefix + n] = val
    outs = {n: val.reshape(args[n.split("_", 1)[1] if not n.startswith("new_") else n[6:]].shape)
            for n, val in outs.items()}
    res = [small_red["loss"][0], gx[None]]
    for prefix in ("grad_", "delta_", "new_m_", "new_v_"):
        res += [outs[prefix + n] for n in _WEIGHTS]
    return tuple(res)


def kernel(x, positions, mix_norm_pre, w_in, q_norm, w_uq, kv_norm, w_uk, w_uv, ssm_lambda_re, ssm_lambda_im, ssm_log_dt, ssm_b_re, ssm_b_im, ssm_c_re, ssm_c_im, ssm_d, w_glu, b_glu, w_branch_attn, w_branch_ssm, b_gate, w_out, mix_norm_post, ffn_norm_pre, w_up, conv_w, conv_b, w_down, ffn_norm_post, loss_target, m_mix_norm_pre, m_w_in, m_q_norm, m_w_uq, m_kv_norm, m_w_uk, m_w_uv, m_ssm_lambda_re, m_ssm_lambda_im, m_ssm_log_dt, m_ssm_b_re, m_ssm_b_im, m_ssm_c_re, m_ssm_c_im, m_ssm_d, m_w_glu, m_b_glu, m_w_branch_attn, m_w_branch_ssm, m_b_gate, m_w_out, m_mix_norm_post, m_ffn_norm_pre, m_w_up, m_conv_w, m_conv_b, m_w_down, m_ffn_norm_post, v_mix_norm_pre, v_w_in, v_q_norm, v_w_uq, v_kv_norm, v_w_uk, v_w_uv, v_ssm_lambda_re, v_ssm_lambda_im, v_ssm_log_dt, v_ssm_b_re, v_ssm_b_im, v_ssm_c_re, v_ssm_c_im, v_ssm_d, v_w_glu, v_b_glu, v_w_branch_attn, v_w_branch_ssm, v_b_gate, v_w_out, v_mix_norm_post, v_ffn_norm_pre, v_w_up, v_conv_w, v_conv_b, v_w_down, v_ffn_norm_post):
    given = dict(locals())
    return _step(given)
```

```python
import math

import jax
import jax.numpy as jnp
from jax import lax
from jax.experimental import pallas as pl
from jax.experimental.pallas import tpu as pltpu

F32 = jnp.float32
BF16 = jnp.bfloat16
MESH = pl.DeviceIdType.MESH

D_MODEL = 1024
N_HEADS = 8
QK_NOPE = 64
QK_ROPE = 32
QK_HEAD = QK_NOPE + QK_ROPE
V_HEAD = 64
Q_RANK = 384
KV_RANK = 256
ROPE_THETA = 10000.0
SSM_W = 512
SSM_H = 16
SSM_G = 32
SSM_P = 64
SSM_CH = SSM_G * SSM_P
D_FF = 2816
EPS = 1e-6
ADAM_LR = 0.001
ADAM_B1 = 0.9
ADAM_B2 = 0.999
ADAM_EPS = 1e-08
ADAM_WD = 0.01
ADAM_STEP = 10

LANES = 128
SUBLANES = 8
VMEM_LIMIT = 56 * 1024 * 1024

HEAD_SLOT = LANES
HP = N_HEADS * HEAD_SLOT
P_CQ, P_CKV, P_KR, P_U, P_GL, P_END = 0, 384, 640, 768, 1280, 3328
KR_LANE = 64

FLAT_W = 1024
N_CHIPS = 4


def _tile(n, cap):
    if n <= cap:
        return n
    best = None
    for t in range(LANES, cap + 1, LANES):
        if n % t == 0:
            best = t
    assert best is not None, (n, cap)
    return best


def _params(sem):
    return pltpu.CompilerParams(dimension_semantics=sem, vmem_limit_bytes=VMEM_LIMIT)


def _dot(a, b):
    return jnp.dot(a, b, preferred_element_type=F32)


def _dot_nt(a, b):
    return lax.dot_general(a, b, (((1,), (1,)), ((), ())), preferred_element_type=F32)


def _dot_tn(a, b):
    return lax.dot_general(a, b, (((0,), (0,)), ((), ())), preferred_element_type=F32)


def _rms(x, g):
    r = lax.rsqrt(jnp.mean(x * x, axis=-1, keepdims=True) + EPS)
    return x * r * g, r


def _rms_bwd(dy, x, g):
    r = lax.rsqrt(jnp.mean(x * x, axis=-1, keepdims=True) + EPS)
    dyg = dy * g
    dx = r * dyg - x * (r * r * r) * jnp.mean(dyg * x, axis=-1, keepdims=True)
    dg = jnp.sum(dy * x * r, axis=0, keepdims=True)
    return dx, dg


_GELU_K0 = math.sqrt(2.0 / math.pi)
_GELU_K1 = 0.044715


def _gelu(x):
    th = jnp.tanh(_GELU_K0 * (x + _GELU_K1 * x * x * x))
    return 0.5 * x * (1.0 + th)


def _gelu_grad(x):
    th = jnp.tanh(_GELU_K0 * (x + _GELU_K1 * x * x * x))
    return 0.5 * (1.0 + th) + 0.5 * x * (1.0 - th * th) * _GELU_K0 * (1.0 + 3.0 * _GELU_K1 * x * x)


def _sigmoid(x):
    return 1.0 / (1.0 + jnp.exp(-x))


def _rope(q, c, s):
    n = q.shape[1]
    lane = lax.broadcasted_iota(jnp.int32, q.shape, 1) % HEAD_SLOT
    sw = jnp.where(lane < KR_LANE + QK_ROPE // 2, pltpu.roll(q, n - QK_ROPE // 2, 1), pltpu.roll(q, QK_ROPE // 2, 1))
    return q * c + sw * s


def _rope_bwd(dy, c, s):
    n = dy.shape[1]
    t = dy * s
    lane = lax.broadcasted_iota(jnp.int32, dy.shape, 1) % HEAD_SLOT
    sw = jnp.where(lane < KR_LANE + QK_ROPE // 2, pltpu.roll(t, n - QK_ROPE // 2, 1), pltpu.roll(t, QK_ROPE // 2, 1))
    rope_lane = jnp.logical_and(lane >= KR_LANE, lane < KR_LANE + QK_ROPE)
    return dy * c + jnp.where(rope_lane, sw, 0.0)


def _shift_down(x, k, halo):
    xs = pltpu.roll(x, k, 0)
    hs = pltpu.roll(halo, k, 0)
    rows = lax.broadcasted_iota(jnp.int32, halo.shape, 0)
    top = jnp.where(rows < k, hs, xs[0:SUBLANES])
    return jnp.concatenate([top, xs[SUBLANES:]], axis=0)


def _shift_up(x, k, halo):
    t = x.shape[0]
    xs = pltpu.roll(x, t - k, 0)
    hs = pltpu.roll(halo, SUBLANES - k, 0)
    rows = lax.broadcasted_iota(jnp.int32, halo.shape, 0)
    bot = jnp.where(rows >= SUBLANES - k, hs, xs[t - SUBLANES:])
    return jnp.concatenate([xs[:t - SUBLANES], bot], axis=0)


def _mm(a, b, name, out_dtype=F32, bt=False, b_col0=0, n=None, tm_cap=512, tn_cap=1408, a_lead=None):
    m, k = a.shape[-2:]
    if bt:
        n_full = b.shape[0]
        n = n_full
    else:
        n = b.shape[1] if n is None else n
    tm = min(tm_cap, m)
    tn = _tile(n, tn_cap)

    def body(a_ref, b_ref, o_ref):
        if bt:
            o_ref[...] = _dot_nt(a_ref[...], b_ref[...]).astype(out_dtype)
        else:
            o_ref[...] = _dot(a_ref[...], b_ref[...]).astype(out_dtype)

    if bt:
        b_spec = pl.BlockSpec((tn, k), lambda j, i: (j, b_col0))
    else:
        off = b_col0 * (n // tn)
        b_spec = pl.BlockSpec((k, tn), lambda j, i: (0, off + j))
    if a_lead is None:
        a_spec = pl.BlockSpec((tm, k), lambda j, i: (i, 0))
    else:
        a_spec = pl.BlockSpec((None, tm, k), lambda j, i: (a_lead, i, 0))
    return pl.pallas_call(
        body, name=name, grid=(n // tn, m // tm),
        in_specs=[a_spec, b_spec],
        out_specs=pl.BlockSpec((tm, tn), lambda j, i: (i, j)),
        out_shape=jax.ShapeDtypeStruct((m, n), out_dtype),
        compiler_params=_params(("parallel", "parallel")),
    )(a, b)


def _mm_tn(a, b, name, tk_cap=512, tn_cap=1664, tl_cap=1024, chips=False):
    l, k = a.shape
    tk = _tile(k, tk_cap)
    tl = min(tl_cap, l)

    def body(a_ref, b_ref, o_ref):
        @pl.when(pl.program_id(2) == 0)
        def _():
            o_ref[...] = jnp.zeros_like(o_ref)

        o_ref[...] += _dot_tn(a_ref[...], b_ref[...])

    if chips:
        n = b.shape[-1] * (b.shape[0] if b.ndim == 3 else 1)
        tn = n // N_CHIPS
        assert tn % LANES == 0
        if b.ndim == 3:
            per = N_CHIPS // b.shape[0]
            b_spec = pl.BlockSpec((None, tl, tn), lambda i, j, r: (j // per, r, j % per))
        else:
            b_spec = pl.BlockSpec((tl, tn), lambda i, j, r: (r, j))
        out_spec = pl.BlockSpec((None, tk, tn), lambda i, j, r: (j, i, 0))
        out_shape = jax.ShapeDtypeStruct((N_CHIPS, k, tn), F32)
    else:
        n = b.shape[1]
        tn = _tile(n, tn_cap)
        b_spec = pl.BlockSpec((tl, tn), lambda i, j, r: (r, j))
        out_spec = pl.BlockSpec((tk, tn), lambda i, j, r: (i, j))
        out_shape = jax.ShapeDtypeStruct((k, n), F32)
    return pl.pallas_call(
        body, name=name, grid=(k // tk, n // tn, l // tl),
        in_specs=[pl.BlockSpec((tl, tk), lambda i, j, r: (r, i)), b_spec],
        out_specs=out_spec, out_shape=out_shape,
        compiler_params=_params(("parallel", "parallel", "arbitrary")),
    )(a, b)


def _row(tl, n):
    return pl.BlockSpec((tl, n), lambda i: (i, 0))


def _const(shape):
    return pl.BlockSpec(shape, lambda i: tuple(0 for _ in shape))


def _proj_fwd(x, g1, win, gq, wuq, gkv, wukv, rc, rs, bg, tl):
    l = x.shape[0]

    def body(x_ref, g1_ref, win_ref, gq_ref, wuq_ref, gkv_ref, wukv_ref, rc_ref, rs_ref, bg_ref,
             hn_ref, cq_ref, ckv_ref, q_ref, k_ref, v_ref, u_ref, gl_ref):
        hn, _ = _rms(x_ref[...], g1_ref[...])
        hnb = hn.astype(BF16)
        hn_ref[...] = hnb
        proj = _dot(hnb, win_ref[...])
        cq = proj[:, P_CQ:P_CKV]
        ckv = proj[:, P_CKV:P_KR]
        kr = proj[:, P_KR:P_U]
        cq_ref[...] = cq
        ckv_ref[...] = ckv
        u_ref[...] = proj[:, P_U:P_GL]
        gl_ref[...] = proj[:, P_GL:P_END] + bg_ref[...]
        qn, _ = _rms(cq, gq_ref[...])
        q = _dot(qn.astype(BF16), wuq_ref[...])
        c1 = rc_ref[...]
        s1 = rs_ref[...]
        q_ref[...] = (_rope(q, jnp.tile(c1, (1, N_HEADS)), jnp.tile(s1, (1, N_HEADS))) * Q_PRESCALE).astype(BF16)
        ckvn, _ = _rms(ckv, gkv_ref[...])
        kv = _dot(ckvn.astype(BF16), wukv_ref[...])
        krr = _rope(kr, c1, s1)
        k_ref[...] = (kv[:, :HP] + jnp.tile(krr, (1, N_HEADS))).astype(BF16)
        v_ref[...] = kv[:, HP:].astype(BF16)

    outs = [(D_MODEL, BF16), (Q_RANK, F32), (KV_RANK, F32), (HP, BF16), (HP, BF16), (HP, BF16),
            (SSM_W, F32), (2 * D_MODEL, F32)]
    return pl.pallas_call(
        body, name="proj_fwd", grid=(l // tl,),
        in_specs=[_row(tl, D_MODEL), _const((1, D_MODEL)), _const((D_MODEL, P_END)), _const((1, Q_RANK)),
                  _const((Q_RANK, HP)), _const((1, KV_RANK)), _const((KV_RANK, 2 * HP)),
                  _row(tl, HEAD_SLOT), _row(tl, HEAD_SLOT), _const((1, 2 * D_MODEL))],
        out_specs=[_row(tl, n) for n, _ in outs],
        out_shape=[jax.ShapeDtypeStruct((l, n), dt) for n, dt in outs],
        compiler_params=_params(("parallel",)),
    )(x, g1, win, gq, wuq, gkv, wukv, rc, rs, bg)


_NEG = -1e30


LOG2E = 1.0 / math.log(2.0)
LN2 = math.log(2.0)
ATTN_SCALE = 1.0 / math.sqrt(QK_HEAD)
Q_PRESCALE = ATTN_SCALE * LOG2E
HEADS_PER_STEP = 2
PAIR_W = HEADS_PER_STEP * HEAD_SLOT


def _causal_pairs(nq, by_query):
    if by_query:
        pairs = [(i, j) for i in range(nq) for j in range(i + 1)]
    else:
        pairs = [(i, j) for j in range(nq) for i in range(j, nq)]
    return jnp.array([p[0] for p in pairs], jnp.int32), jnp.array([p[1] for p in pairs], jnp.int32)


def _diag_mask_t(s):
    rows = lax.broadcasted_iota(jnp.int32, s.shape, 0)
    cols = lax.broadcasted_iota(jnp.int32, s.shape, 1)
    return jnp.where(rows <= cols, s, _NEG)


def _attn_fwd(q, k, v, tq):
    l = q.shape[0]
    nq = l // tq
    it, jt = _causal_pairs(nq, True)

    def body(it_ref, jt_ref, q_ref, k_ref, v_ref, o_ref, lse_ref, m_ref, l_ref, acc_ref):
        t = pl.program_id(1)
        i = it_ref[t]
        j = jt_ref[t]

        @pl.when(j == 0)
        def _():
            m_ref[...] = jnp.full_like(m_ref, _NEG)
            l_ref[...] = jnp.zeros_like(l_ref)
            acc_ref[...] = jnp.zeros_like(acc_ref)

        def update(on_diagonal):
            for hh in range(HEADS_PER_STEP):
                sl = slice(hh * HEAD_SLOT, (hh + 1) * HEAD_SLOT)
                s = _dot_nt(k_ref[:, sl], q_ref[:, sl])
                if on_diagonal:
                    s = _diag_mask_t(s)
                m_old = m_ref[hh]
                m_new = jnp.maximum(m_old, jnp.max(s, axis=0, keepdims=True))
                p = jnp.exp2(s - m_new)
                alpha = jnp.exp2(m_old - m_new)
                l_ref[hh] = alpha * l_ref[hh] + jnp.sum(p, axis=0, keepdims=True)
                acc_ref[hh] = alpha * acc_ref[hh] + _dot_tn(v_ref[:, sl], p.astype(BF16))
                m_ref[hh] = m_new

        @pl.when(j < i)
        def _():
            update(False)

        @pl.when(j == i)
        def _():
            update(True)
            for hh in range(HEADS_PER_STEP):
                sl = slice(hh * HEAD_SLOT, (hh + 1) * HEAD_SLOT)
                o_ref[:, sl] = (acc_ref[hh] / l_ref[hh]).T.astype(BF16)
                lse_ref[hh] = m_ref[hh] + jnp.log(l_ref[hh]) * LOG2E

    blk = (tq, PAIR_W)
    qmap = lambda h, t, it_ref, jt_ref: (it_ref[t], h)
    kmap = lambda h, t, it_ref, jt_ref: (jt_ref[t], h)
    row = pl.BlockSpec((HEADS_PER_STEP, 1, tq), lambda h, t, it_ref, jt_ref: (h, 0, it_ref[t]))
    return pl.pallas_call(
        body, name="attn_fwd",
        grid_spec=pltpu.PrefetchScalarGridSpec(
            num_scalar_prefetch=2, grid=(N_HEADS // HEADS_PER_STEP, it.shape[0]),
            in_specs=[pl.BlockSpec(blk, qmap), pl.BlockSpec(blk, kmap), pl.BlockSpec(blk, kmap)],
            out_specs=[pl.BlockSpec(blk, qmap), row],
            scratch_shapes=[pltpu.VMEM((HEADS_PER_STEP, 1, tq), F32), pltpu.VMEM((HEADS_PER_STEP, 1, tq), F32),
                            pltpu.VMEM((HEADS_PER_STEP, HEAD_SLOT, tq), F32)]),
        out_shape=[jax.ShapeDtypeStruct((l, HP), BF16), jax.ShapeDtypeStruct((N_HEADS, 1, l), F32)],
        compiler_params=_params(("parallel", "arbitrary")),
    )(it, jt, q, k, v)


def _attn_delta(o, do, tq):
    l = o.shape[0]

    def body(o_ref, do_ref, d_ref):
        prod = o_ref[...].astype(F32) * do_ref[...].astype(F32)
        for hh in range(HEADS_PER_STEP):
            d_ref[hh] = jnp.sum(prod[:, hh * HEAD_SLOT:(hh + 1) * HEAD_SLOT].T, axis=0, keepdims=True)

    blk = pl.BlockSpec((tq, PAIR_W), lambda h, i: (i, h))
    return pl.pallas_call(
        body, name="attn_delta", grid=(N_HEADS // HEADS_PER_STEP, l // tq), in_specs=[blk, blk],
        out_specs=pl.BlockSpec((HEADS_PER_STEP, 1, tq), lambda h, i: (h, 0, i)),
        out_shape=jax.ShapeDtypeStruct((N_HEADS, 1, l), F32),
        compiler_params=_params(("parallel", "parallel")),
    )(o, do)


def _attn_bwd(q, k, v, do, lse, delta, tq):
    l = q.shape[0]
    nq = l // tq
    it, jt = _causal_pairs(nq, False)

    def body(it_ref, jt_ref, q_ref, k_ref, v_ref, do_ref, lse_ref, dl_ref, dq_ref, dk_ref, dv_ref, dka_ref, dva_ref):
        t = pl.program_id(1)
        i = it_ref[t]
        j = jt_ref[t]

        @pl.when(t == 0)
        def _():
            dq_ref[...] = jnp.zeros_like(dq_ref)

        @pl.when(i == j)
        def _():
            dka_ref[...] = jnp.zeros_like(dka_ref)
            dva_ref[...] = jnp.zeros_like(dva_ref)

        def update(on_diagonal):
            r0 = pl.multiple_of(i * tq, tq)
            for hh in range(HEADS_PER_STEP):
                sl = slice(hh * HEAD_SLOT, (hh + 1) * HEAD_SLOT)
                qb = q_ref[:, sl]
                kb = k_ref[:, sl]
                dob = do_ref[:, sl]
                s = _dot_nt(kb, qb)
                if on_diagonal:
                    s = _diag_mask_t(s)
                p = jnp.exp2(s - lse_ref[hh])
                dva_ref[:, sl] += _dot(p.astype(BF16), dob)
                dp = _dot_nt(v_ref[:, sl], dob)
                ds = (p * (dp - dl_ref[hh])).astype(BF16)
                dka_ref[:, sl] += _dot(ds, qb)
                dq_ref[pl.ds(r0, tq), sl] += ATTN_SCALE * _dot_tn(ds, kb)

        @pl.when(j < i)
        def _():
            update(False)

        @pl.when(j == i)
        def _():
            update(True)

        @pl.when(i == nq - 1)
        def _():
            dk_ref[...] = (dka_ref[...] * LN2).astype(BF16)
            dv_ref[...] = dva_ref[...].astype(BF16)

    blk = (tq, PAIR_W)
    qmap = lambda h, t, it_ref, jt_ref: (it_ref[t], h)
    kmap = lambda h, t, it_ref, jt_ref: (jt_ref[t], h)
    row = pl.BlockSpec((HEADS_PER_STEP, 1, tq), lambda h, t, it_ref, jt_ref: (h, 0, it_ref[t]))
    return pl.pallas_call(
        body, name="attn_bwd",
        grid_spec=pltpu.PrefetchScalarGridSpec(
            num_scalar_prefetch=2, grid=(N_HEADS // HEADS_PER_STEP, it.shape[0]),
            in_specs=[pl.BlockSpec(blk, qmap), pl.BlockSpec(blk, kmap), pl.BlockSpec(blk, kmap),
                      pl.BlockSpec(blk, qmap), row, row],
            out_specs=[pl.BlockSpec((l, PAIR_W), lambda h, t, it_ref, jt_ref: (0, h)), pl.BlockSpec(blk, kmap),
                       pl.BlockSpec(blk, kmap)],
            scratch_shapes=[pltpu.VMEM(blk, F32), pltpu.VMEM(blk, F32)]),
        out_shape=[jax.ShapeDtypeStruct((l, HP), F32), jax.ShapeDtypeStruct((l, HP), BF16),
                   jax.ShapeDtypeStruct((l, HP), BF16)],
        compiler_params=_params(("parallel", "arbitrary")),
    )(it, jt, q, k, v, do, lse, delta)


SSM_CB = 512
SSM_UB = 128
SSM_NB = SSM_CH // SSM_CB


def _scan_tiles(re_ref, im_ref, tab, carry, n_tiles, reverse):
    def tile(n, c):
        cr, ci = c
        idx = (n_tiles - 1 - n) if reverse else n
        r0 = pl.multiple_of(idx * SUBLANES, SUBLANES)
        sr = re_ref[pl.ds(r0, SUBLANES), :]
        si = im_ref[pl.ds(r0, SUBLANES), :]
        for step, k in enumerate((1, 2, 4)):
            mr, mi = tab[2 * step], tab[2 * step + 1]
            sh = (SUBLANES - k) if reverse else k
            rr = pltpu.roll(sr, sh, 0)
            ri = pltpu.roll(si, sh, 0)
            sr, si = sr + mr * rr - mi * ri, si + mr * ri + mi * rr
        pr, pi = tab[6], tab[7]
        sr, si = sr + pr * cr - pi * ci, si + pr * ci + pi * cr
        re_ref[pl.ds(r0, SUBLANES), :] = sr
        im_ref[pl.ds(r0, SUBLANES), :] = si
        if reverse:
            return sr[0:1, :], si[0:1, :]
        return sr[SUBLANES - 1:SUBLANES, :], si[SUBLANES - 1:SUBLANES, :]

    return lax.fori_loop(0, n_tiles, tile, carry, unroll=2)


def _ssm_fwd(u, bre, bim, cre, cim, dvec, tab, tt):
    l = u.shape[0]
    nt = l // tt

    def body(u_ref, bre_ref, bim_ref, cre_ref, cim_ref, d_ref, tab_ref, y_ref, sre_ref, sim_ref, car_ref):
        @pl.when(pl.program_id(1) == 0)
        def _():
            car_ref[...] = jnp.zeros_like(car_ref)

        uf = u_ref[...]
        ub = uf.astype(BF16)
        sre_ref[...] = _dot(ub, bre_ref[0])
        sim_ref[...] = _dot(ub, bim_ref[0])
        tab_v = [tab_ref[n] for n in range(8)]
        cr, ci = _scan_tiles(sre_ref, sim_ref, tab_v, (car_ref[0:1, :], car_ref[8:9, :]), tt // SUBLANES, False)
        car_ref[0:1, :] = cr
        car_ref[8:9, :] = ci
        y_ref[...] = (_dot(sre_ref[...].astype(BF16), cre_ref[0]) - _dot(sim_ref[...].astype(BF16), cim_ref[0])
                      + d_ref[...] * uf)

    return pl.pallas_call(
        body, name="ssm_fwd", grid=(SSM_NB, nt),
        in_specs=[pl.BlockSpec((tt, SSM_UB), lambda m, t: (t, m)),
                  pl.BlockSpec((1, SSM_UB, SSM_CB), lambda m, t: (m, 0, 0)),
                  pl.BlockSpec((1, SSM_UB, SSM_CB), lambda m, t: (m, 0, 0)),
                  pl.BlockSpec((1, SSM_CB, SSM_UB), lambda m, t: (m, 0, 0)),
                  pl.BlockSpec((1, SSM_CB, SSM_UB), lambda m, t: (m, 0, 0)),
                  pl.BlockSpec((1, SSM_UB), lambda m, t: (0, m)),
                  pl.BlockSpec((8, SUBLANES, SSM_CB), lambda m, t: (0, 0, m))],
        out_specs=[pl.BlockSpec((tt, SSM_UB), lambda m, t: (t, m)),
                   pl.BlockSpec((tt, SSM_CB), lambda m, t: (t, m)),
                   pl.BlockSpec((tt, SSM_CB), lambda m, t: (t, m))],
        out_shape=[jax.ShapeDtypeStruct((l, SSM_W), F32), jax.ShapeDtypeStruct((l, SSM_CH), F32),
                   jax.ShapeDtypeStruct((l, SSM_CH), F32)],
        scratch_shapes=[pltpu.VMEM((2 * SUBLANES, SSM_CB), F32)],
        compiler_params=_params(("parallel", "arbitrary")),
    )(u, bre, bim, cre, cim, dvec, tab)


def _ssm_bwd(dy, u, sre, sim, bre, bim, cre, cim, dvec, tab, tt):
    l = u.shape[0]
    nt = l // tt
    tpb = tt // SUBLANES

    def body(dy_ref, u_ref, sre_ref, sim_ref, hre_ref, him_ref, bre_ref, bim_ref, cre_ref, cim_ref, d_ref, tab_ref,
             du_ref, dbre_ref, dbim_ref, dcre_ref, dcim_ref, dare_ref, daim_ref, dd_ref, lr_ref, li_ref, car_ref):
        t = pl.program_id(1)

        @pl.when(t == 0)
        def _():
            car_ref[...] = jnp.zeros_like(car_ref)
            for ref in (dbre_ref, dbim_ref, dcre_ref, dcim_ref, dare_ref, daim_ref, dd_ref):
                ref[...] = jnp.zeros_like(ref)

        dyf = dy_ref[...]
        dyb = dyf.astype(BF16)
        uf = u_ref[...]
        s_re = sre_ref[...]
        s_im = sim_ref[...]
        lr_ref[...] = _dot_nt(dyb, cre_ref[0])
        li_ref[...] = -_dot_nt(dyb, cim_ref[0])
        dcre_ref[0] += _dot_tn(s_re.astype(BF16), dyb)
        dcim_ref[0] -= _dot_tn(s_im.astype(BF16), dyb)
        tab_v = [tab_ref[n] for n in range(8)]
        cr, ci = _scan_tiles(lr_ref, li_ref, tab_v, (car_ref[0:1, :], car_ref[8:9, :]), tpb, True)
        car_ref[0:1, :] = cr
        car_ref[8:9, :] = ci
        lam_r = lr_ref[...]
        lam_i = li_ref[...]
        keep = jnp.where(t == nt - 1, 0.0, 1.0)
        sp_r = _shift_down(s_re, 1, hre_ref[...] * keep)
        sp_i = _shift_down(s_im, 1, him_ref[...] * keep)
        dare_ref[...] += jnp.sum(lam_r * sp_r + lam_i * sp_i, axis=0, keepdims=True)
        daim_ref[...] += jnp.sum(lam_i * sp_r - lam_r * sp_i, axis=0, keepdims=True)
        lrb = lam_r.astype(BF16)
        lib = lam_i.astype(BF16)
        du_ref[...] = _dot_nt(lrb, bre_ref[0]) + _dot_nt(lib, bim_ref[0]) + dyf * d_ref[...]
        ub = uf.astype(BF16)
        dbre_ref[0] += _dot_tn(ub, lrb)
        dbim_ref[0] += _dot_tn(ub, lib)
        dd_ref[...] += jnp.sum(dyf * uf, axis=0, keepdims=True)

    rev = lambda m, t: (nt - 1 - t, m)
    halo = lambda m, t: (jnp.maximum((nt - 1 - t) * tpb - 1, 0), m)
    wb = pl.BlockSpec((1, SSM_UB, SSM_CB), lambda m, t: (m, 0, 0))
    wc = pl.BlockSpec((1, SSM_CB, SSM_UB), lambda m, t: (m, 0, 0))
    vec_c = pl.BlockSpec((1, SSM_CB), lambda m, t: (0, m))
    vec_u = pl.BlockSpec((1, SSM_UB), lambda m, t: (0, m))
    return pl.pallas_call(
        body, name="ssm_bwd", grid=(SSM_NB, nt),
        in_specs=[pl.BlockSpec((tt, SSM_UB), rev), pl.BlockSpec((tt, SSM_UB), rev),
                  pl.BlockSpec((tt, SSM_CB), rev), pl.BlockSpec((tt, SSM_CB), rev),
                  pl.BlockSpec((SUBLANES, SSM_CB), halo), pl.BlockSpec((SUBLANES, SSM_CB), halo),
                  wb, wb, wc, wc, vec_u,
                  pl.BlockSpec((8, SUBLANES, SSM_CB), lambda m, t: (0, 0, m))],
        out_specs=[pl.BlockSpec((tt, SSM_UB), rev), wb, wb, wc, wc, vec_c, vec_c, vec_u],
        out_shape=[jax.ShapeDtypeStruct((l, SSM_W), F32),
                   jax.ShapeDtypeStruct((SSM_NB, SSM_UB, SSM_CB), F32), jax.ShapeDtypeStruct((SSM_NB, SSM_UB, SSM_CB), F32),
                   jax.ShapeDtypeStruct((SSM_NB, SSM_CB, SSM_UB), F32), jax.ShapeDtypeStruct((SSM_NB, SSM_CB, SSM_UB), F32),
                   jax.ShapeDtypeStruct((1, SSM_CH), F32), jax.ShapeDtypeStruct((1, SSM_CH), F32),
                   jax.ShapeDtypeStruct((1, SSM_W), F32)],
        scratch_shapes=[pltpu.VMEM((tt, SSM_CB), F32), pltpu.VMEM((tt, SSM_CB), F32),
                        pltpu.VMEM((2 * SUBLANES, SSM_CB), F32)],
        compiler_params=_params(("parallel", "arbitrary")),
    )(dy, u, sre, sim, sre, sim, bre, bim, cre, cim, dvec, tab)


def _merge_fwd(x, gl, attn, y1, wba, wbs, wglu, bglu, wout, gpost, gpre, tl):
    l = x.shape[0]

    def body(x_ref, gl_ref, at_ref, y1_ref, wba_ref, wbs_ref, wglu_ref, bglu_ref, wout_ref, gpost_ref, gpre_ref,
             a_ref, sm_ref, mg_ref, z_ref, x1_ref, hn2_ref, y3_ref):
        y2 = _gelu(y1_ref[...])
        sg = _sigmoid(_dot(y2.astype(BF16), wglu_ref[...]) + bglu_ref[...])
        y3 = (y2 * sg).astype(BF16)
        y3_ref[...] = y3
        a = _dot(at_ref[...], wba_ref[...])
        sm = _dot(y3, wbs_ref[...])
        a_ref[...] = a
        sm_ref[...] = sm
        g = _sigmoid(gl_ref[...])
        merged = (g[:, :D_MODEL] * a + g[:, D_MODEL:] * sm).astype(BF16)
        mg_ref[...] = merged
        z = _dot(merged, wout_ref[...])
        z_ref[...] = z
        n, _ = _rms(z, gpost_ref[...])
        x1 = x_ref[...] + n
        x1_ref[...] = x1
        hn2, _ = _rms(x1, gpre_ref[...])
        hn2_ref[...] = hn2.astype(BF16)

    outs = [(D_MODEL, F32), (D_MODEL, F32), (D_MODEL, BF16), (D_MODEL, F32), (D_MODEL, F32), (D_MODEL, BF16),
            (SSM_W, BF16)]
    return pl.pallas_call(
        body, name="merge_fwd", grid=(l // tl,),
        in_specs=[_row(tl, D_MODEL), _row(tl, 2 * D_MODEL), _row(tl, HP), _row(tl, SSM_W),
                  _const((HP, D_MODEL)), _const((SSM_W, D_MODEL)), _const((SSM_W, SSM_W)), _const((1, SSM_W)),
                  _const((D_MODEL, D_MODEL)), _const((1, D_MODEL)), _const((1, D_MODEL))],
        out_specs=[_row(tl, n) for n, _ in outs],
        out_shape=[jax.ShapeDtypeStruct((l, n), dt) for n, dt in outs],
        compiler_params=_params(("parallel",)),
    )(x, gl, attn, y1, wba, wbs, wglu, bglu, wout, gpost, gpre)


def _merge_bwd(dhn2a, dhn2b, x1, dx2, z, gl, a, sm, y1, wba, wbs, wglu, bglu, wout, gpost, gpre, tl):
    l = x1.shape[0]

    def body(da_ref, db_ref, x1_ref, dx2_ref, z_ref, gl_ref, a_ref, sm_ref, y1_ref,
             wba_ref, wbs_ref, wglu_ref, bglu_ref, wout_ref, gpost_ref, gpre_ref,
             dx1_ref, dz_ref, dbra_ref, dbrs_ref, dgl_ref, dat_ref, dy1_ref, dt_ref, y2_ref,
             dgpre_ref, dgpost_ref, dbg_ref, dbglu_ref):
        @pl.when(pl.program_id(0) == 0)
        def _():
            for ref in (dgpre_ref, dgpost_ref, dbg_ref, dbglu_ref):
                ref[...] = jnp.zeros_like(ref)

        dhn2 = da_ref[...] + db_ref[...]
        dx1a, dgpre = _rms_bwd(dhn2, x1_ref[...], gpre_ref[...])
        dgpre_ref[...] += dgpre
        dx1 = dx2_ref[...] + dx1a
        dx1_ref[...] = dx1
        dz, dgpost = _rms_bwd(dx1, z_ref[...], gpost_ref[...])
        dgpost_ref[...] += dgpost
        dzb = dz.astype(BF16)
        dz_ref[...] = dzb
        dm = _dot_nt(dzb, wout_ref[...])
        g = _sigmoid(gl_ref[...])
        g0 = g[:, :D_MODEL]
        g1 = g[:, D_MODEL:]
        dbra = (dm * g0).astype(BF16)
        dbrs = (dm * g1).astype(BF16)
        dbra_ref[...] = dbra
        dbrs_ref[...] = dbrs
        dgl0 = dm * a_ref[...] * g0 * (1.0 - g0)
        dgl1 = dm * sm_ref[...] * g1 * (1.0 - g1)
        dgl_ref[:, :D_MODEL] = dgl0.astype(BF16)
        dgl_ref[:, D_MODEL:] = dgl1.astype(BF16)
        dbg_ref[:, :D_MODEL] += jnp.sum(dgl0, axis=0, keepdims=True)
        dbg_ref[:, D_MODEL:] += jnp.sum(dgl1, axis=0, keepdims=True)
        dat_ref[...] = _dot_nt(dbra, wba_ref[...]).astype(BF16)
        dy3 = _dot_nt(dbrs, wbs_ref[...])
        y1v = y1_ref[...]
        y2 = _gelu(y1v)
        y2b = y2.astype(BF16)
        y2_ref[...] = y2b
        sg = _sigmoid(_dot(y2b, wglu_ref[...]) + bglu_ref[...])
        dt = dy3 * y2 * sg * (1.0 - sg)
        dtb = dt.astype(BF16)
        dt_ref[...] = dtb
        dbglu_ref[...] += jnp.sum(dt, axis=0, keepdims=True)
        dy2 = dy3 * sg + _dot_nt(dtb, wglu_ref[...])
        dy1_ref[...] = dy2 * _gelu_grad(y1v)

    outs = [(D_MODEL, F32), (D_MODEL, BF16), (D_MODEL, BF16), (D_MODEL, BF16), (2 * D_MODEL, BF16), (HP, BF16),
            (SSM_W, F32), (SSM_W, BF16), (SSM_W, BF16)]
    accs = [D_MODEL, D_MODEL, 2 * D_MODEL, SSM_W]
    return pl.pallas_call(
        body, name="merge_bwd", grid=(l // tl,),
        in_specs=[_row(tl, D_MODEL), _row(tl, D_MODEL), _row(tl, D_MODEL), _row(tl, D_MODEL), _row(tl, D_MODEL),
                  _row(tl, 2 * D_MODEL), _row(tl, D_MODEL), _row(tl, D_MODEL), _row(tl, SSM_W),
                  _const((HP, D_MODEL)), _const((SSM_W, D_MODEL)), _const((SSM_W, SSM_W)), _const((1, SSM_W)),
                  _const((D_MODEL, D_MODEL)), _const((1, D_MODEL)), _const((1, D_MODEL))],
        out_specs=[_row(tl, n) for n, _ in outs] + [_const((1, n)) for n in accs],
        out_shape=[jax.ShapeDtypeStruct((l, n), dt) for n, dt in outs]
        + [jax.ShapeDtypeStruct((1, n), F32) for n in accs],
        compiler_params=_params(("arbitrary",)),
    )(dhn2a, dhn2b, x1, dx2, z, gl, a, sm, y1, wba, wbs, wglu, bglu, wout, gpost, gpre)


def _proj_bwd(x, dx1, cq, ckv, dq, dk, dv, du, dgl, g1, win, gq, wuq, gkv, wukv, rc, rs, tl):
    l = x.shape[0]

    def body(x_ref, dx1_ref, cq_ref, ckv_ref, dq_ref, dk_ref, dv_ref, du_ref, dgl_ref,
             g1_ref, win_ref, gq_ref, wuq_ref, gkv_ref, wukv_ref, rc_ref, rs_ref,
             gx_ref, dql_ref, qn_ref, ckvn_ref, dproj_ref, dg1_ref, dgq_ref, dgkv_ref):
        @pl.when(pl.program_id(0) == 0)
        def _():
            for ref in (dg1_ref, dgq_ref, dgkv_ref):
                ref[...] = jnp.zeros_like(ref)

        c1 = rc_ref[...]
        s1 = rs_ref[...]
        dql = _rope_bwd(dq_ref[...], jnp.tile(c1, (1, N_HEADS)), jnp.tile(s1, (1, N_HEADS))).astype(BF16)
        dql_ref[...] = dql
        dqn = _dot_nt(dql, wuq_ref[...])
        cq = cq_ref[...]
        qn, _ = _rms(cq, gq_ref[...])
        qn_ref[...] = qn.astype(BF16)
        dcq, dgq = _rms_bwd(dqn, cq, gq_ref[...])
        dgq_ref[...] += dgq
        dkb = dk_ref[...]
        dvb = dv_ref[...]
        dkf = dkb.astype(F32)
        dkr = dkf[:, 0:HEAD_SLOT]
        for h in range(1, N_HEADS):
            dkr = dkr + dkf[:, h * HEAD_SLOT:(h + 1) * HEAD_SLOT]
        dkr = _rope_bwd(dkr, c1, s1)
        dckvn = _dot_nt(dkb, wukv_ref[:, :HP]) + _dot_nt(dvb, wukv_ref[:, HP:])
        ckv = ckv_ref[...]
        ckvn, _ = _rms(ckv, gkv_ref[...])
        ckvn_ref[...] = ckvn.astype(BF16)
        dckv, dgkv = _rms_bwd(dckvn, ckv, gkv_ref[...])
        dgkv_ref[...] += dgkv
        dproj_ref[:, P_CQ:P_CKV] = dcq.astype(BF16)
        dproj_ref[:, P_CKV:P_KR] = dckv.astype(BF16)
        dproj_ref[:, P_KR:P_U] = dkr.astype(BF16)
        dproj_ref[:, P_U:P_GL] = du_ref[...].astype(BF16)
        dproj_ref[:, P_GL:P_END] = dgl_ref[...]
        dhn = _dot_nt(dproj_ref[...], win_ref[...])
        dxa, dg1 = _rms_bwd(dhn, x_ref[...], g1_ref[...])
        dg1_ref[...] += dg1
        gx_ref[...] = dx1_ref[...] + dxa

    outs = [(D_MODEL, F32), (HP, BF16), (Q_RANK, BF16), (KV_RANK, BF16), (P_END, BF16)]
    accs = [D_MODEL, Q_RANK, KV_RANK]
    return pl.pallas_call(
        body, name="proj_bwd", grid=(l // tl,),
        in_specs=[_row(tl, D_MODEL), _row(tl, D_MODEL), _row(tl, Q_RANK), _row(tl, KV_RANK), _row(tl, HP),
                  _row(tl, HP), _row(tl, HP), _row(tl, SSM_W), _row(tl, 2 * D_MODEL),
                  _const((1, D_MODEL)), _const((D_MODEL, P_END)), _const((1, Q_RANK)), _const((Q_RANK, HP)),
                  _const((1, KV_RANK)), _const((KV_RANK, 2 * HP)), _row(tl, HEAD_SLOT), _row(tl, HEAD_SLOT)],
        out_specs=[_row(tl, n) for n, _ in outs] + [_const((1, n)) for n in accs],
        out_shape=[jax.ShapeDtypeStruct((l, n), dt) for n, dt in outs]
        + [jax.ShapeDtypeStruct((1, n), F32) for n in accs],
        compiler_params=_params(("arbitrary",)),
    )(x, dx1, cq, ckv, dq, dk, dv, du, dgl, g1, win, gq, wuq, gkv, wukv, rc, rs)


CONV_CB = 256
CONV_NB = D_FF // CONV_CB


def _conv3(h, halo, w, b):
    return b + w[0:1, :] * _shift_down(h, 2, halo) + w[1:2, :] * _shift_down(h, 1, halo) + w[2:3, :] * h


def _conv_fwd(h, cw, cb, tl):
    l = h.shape[0]

    def body(hg_ref, hv_ref, wg_ref, wv_ref, bg_ref, bv_ref, act_ref, halo_ref):
        @pl.when(pl.program_id(1) == 0)
        def _():
            halo_ref[...] = jnp.zeros_like(halo_ref)

        hg = hg_ref[...]
        hv = hv_ref[...]
        cg = _conv3(hg, halo_ref[0:SUBLANES, :], wg_ref[...], bg_ref[...])
        cv = _conv3(hv, halo_ref[SUBLANES:, :], wv_ref[...], bv_ref[...])
        act_ref[...] = (_gelu(cg) * cv).astype(BF16)
        halo_ref[0:SUBLANES, :] = hg[tl - SUBLANES:, :]
        halo_ref[SUBLANES:, :] = hv[tl - SUBLANES:, :]

    gmap = lambda c, r: (r, c)
    vmap = lambda c, r: (r, CONV_NB + c)
    return pl.pallas_call(
        body, name="conv_fwd", grid=(CONV_NB, l // tl),
        in_specs=[pl.BlockSpec((tl, CONV_CB), gmap), pl.BlockSpec((tl, CONV_CB), vmap),
                  pl.BlockSpec((3, CONV_CB), lambda c, r: (0, c)), pl.BlockSpec((3, CONV_CB), lambda c, r: (0, CONV_NB + c)),
                  pl.BlockSpec((1, CONV_CB), lambda c, r: (0, c)), pl.BlockSpec((1, CONV_CB), lambda c, r: (0, CONV_NB + c))],
        out_specs=pl.BlockSpec((tl, CONV_CB), gmap),
        out_shape=jax.ShapeDtypeStruct((l, D_FF), BF16),
        scratch_shapes=[pltpu.VMEM((2 * SUBLANES, CONV_CB), F32)],
        compiler_params=_params(("parallel", "arbitrary")),
    )(h, h, cw, cw, cb, cb)


def _conv_bwd(h, dact, cw, cb, tl):
    l = h.shape[0]
    nr = l // tl
    tpb = tl // SUBLANES

    def body(hg_ref, hv_ref, hgh_ref, hvh_ref, da_ref, wg_ref, wv_ref, bg_ref, bv_ref,
             dh_ref, dwg_ref, dwv_ref, dbg_ref, dbv_ref, car_ref):
        r = pl.program_id(1)

        @pl.when(r == 0)
        def _():
            for ref in (car_ref, dwg_ref, dwv_ref, dbg_ref, dbv_ref):
                ref[...] = jnp.zeros_like(ref)

        keep = jnp.where(r == nr - 1, 0.0, 1.0)
        da = da_ref[...].astype(F32)

        def half(h_ref, halo, w, b):
            hh = h_ref[...]
            h1 = _shift_down(hh, 1, halo)
            h2 = _shift_down(hh, 2, halo)
            return hh, h1, h2, b + w[0:1, :] * h2 + w[1:2, :] * h1 + w[2:3, :] * hh

        wg = wg_ref[...]
        wv = wv_ref[...]
        hg, hg1, hg2, cg = half(hg_ref, hgh_ref[...] * keep, wg, bg_ref[...])
        hv, hv1, hv2, cv = half(hv_ref, hvh_ref[...] * keep, wv, bv_ref[...])
        dcg = da * cv * _gelu_grad(cg)
        dcv = da * _gelu(cg)

        def back(dc, hh, h1, h2, w, nxt, dw_ref, db_ref, part):
            db_ref[...] += jnp.sum(dc, axis=0, keepdims=True)
            dw_ref[0:1, :] += jnp.sum(dc * h2, axis=0, keepdims=True)
            dw_ref[1:2, :] += jnp.sum(dc * h1, axis=0, keepdims=True)
            dw_ref[2:3, :] += jnp.sum(dc * hh, axis=0, keepdims=True)
            dh = w[2:3, :] * dc + w[1:2, :] * _shift_up(dc, 1, nxt) + w[0:1, :] * _shift_up(dc, 2, nxt)
            dh_ref[part] = dh.astype(BF16)

        back(dcg, hg, hg1, hg2, wg, car_ref[0:SUBLANES, :], dwg_ref, dbg_ref, 0)
        back(dcv, hv, hv1, hv2, wv, car_ref[SUBLANES:, :], dwv_ref, dbv_ref, 1)
        car_ref[0:SUBLANES, :] = dcg[0:SUBLANES, :]
        car_ref[SUBLANES:, :] = dcv[0:SUBLANES, :]

    grev = lambda c, r: (nr - 1 - r, c)
    vrev = lambda c, r: (nr - 1 - r, CONV_NB + c)
    ghalo = lambda c, r: (jnp.maximum((nr - 1 - r) * tpb - 1, 0), c)
    vhalo = lambda c, r: (jnp.maximum((nr - 1 - r) * tpb - 1, 0), CONV_NB + c)
    colg = lambda c, r: (0, c)
    colv = lambda c, r: (0, CONV_NB + c)
    return pl.pallas_call(
        body, name="conv_bwd", grid=(CONV_NB, nr),
        in_specs=[pl.BlockSpec((tl, CONV_CB), grev), pl.BlockSpec((tl, CONV_CB), vrev),
                  pl.BlockSpec((SUBLANES, CONV_CB), ghalo), pl.BlockSpec((SUBLANES, CONV_CB), vhalo),
                  pl.BlockSpec((tl, CONV_CB), grev),
                  pl.BlockSpec((3, CONV_CB), colg), pl.BlockSpec((3, CONV_CB), colv),
                  pl.BlockSpec((1, CONV_CB), colg), pl.BlockSpec((1, CONV_CB), colv)],
        out_specs=[pl.BlockSpec((2, tl, CONV_CB), lambda c, r: (0, nr - 1 - r, c)),
                   pl.BlockSpec((3, CONV_CB), colg), pl.BlockSpec((3, CONV_CB), colg),
                   pl.BlockSpec((1, CONV_CB), colg), pl.BlockSpec((1, CONV_CB), colg)],
        out_shape=[jax.ShapeDtypeStruct((2, l, D_FF), BF16),
                   jax.ShapeDtypeStruct((3, D_FF), F32), jax.ShapeDtypeStruct((3, D_FF), F32),
                   jax.ShapeDtypeStruct((1, D_FF), F32), jax.ShapeDtypeStruct((1, D_FF), F32)],
        scratch_shapes=[pltpu.VMEM((2 * SUBLANES, CONV_CB), F32)],
        compiler_params=_params(("parallel", "arbitrary")),
    )(h, h, h, h, dact, cw, cw, cb, cb)


def _loss_head(ff, x1, tgt, g, tl):
    l = ff.shape[0]

    def body(ff_ref, x1_ref, tg_ref, g_ref, loss_ref, dx2_ref, dff_ref, dg_ref):
        @pl.when(pl.program_id(0) == 0)
        def _():
            loss_ref[...] = jnp.zeros_like(loss_ref)
            dg_ref[...] = jnp.zeros_like(dg_ref)

        f = ff_ref[...]
        gv = g_ref[...]
        n, _ = _rms(f, gv)
        e = x1_ref[...] + n - tg_ref[...]
        loss_ref[...] += 0.5 * jnp.sum(jnp.mean(e * e, axis=-1, keepdims=True), axis=0, keepdims=True)
        dx2 = e * (1.0 / D_MODEL)
        dx2_ref[...] = dx2
        dff, dg = _rms_bwd(dx2, f, gv)
        dff_ref[...] = dff.astype(BF16)
        dg_ref[...] += dg

    return pl.pallas_call(
        body, name="loss_head", grid=(l // tl,),
        in_specs=[_row(tl, D_MODEL), _row(tl, D_MODEL), _row(tl, D_MODEL), _const((1, D_MODEL))],
        out_specs=[_const((1, LANES)), _row(tl, D_MODEL), _row(tl, D_MODEL), _const((1, D_MODEL))],
        out_shape=[jax.ShapeDtypeStruct((1, LANES), F32), jax.ShapeDtypeStruct((l, D_MODEL), F32),
                   jax.ShapeDtypeStruct((l, D_MODEL), BF16), jax.ShapeDtypeStruct((1, D_MODEL), F32)],
        compiler_params=_params(("arbitrary",)),
    )(ff, x1, tgt, g)


def _ssm_disc(lam_re, lam_im, log_dt, b_re, b_im):
    dt = jnp.exp(log_dt)[:, None]
    mag = jnp.exp(lam_re * dt)
    ang = lam_im * dt
    a_re, a_im = mag * jnp.cos(ang), mag * jnp.sin(ang)
    den = lam_re * lam_re + lam_im * lam_im
    n_re, n_im = a_re - 1.0, a_im
    z_re = (n_re * lam_re + n_im * lam_im) / den
    z_im = (n_im * lam_re - n_re * lam_im) / den
    bb_re = z_re[..., None] * b_re - z_im[..., None] * b_im
    bb_im = z_re[..., None] * b_im + z_im[..., None] * b_re
    return a_re, a_im, bb_re, bb_im


_GPB = SSM_CB // SSM_P


def _embed_b(bb):
    t = bb.transpose(0, 2, 1).reshape(SSM_NB, _GPB, SSM_H, SSM_P)
    return jnp.einsum('mjhp,jk->mjhkp', t, jnp.eye(_GPB, dtype=bb.dtype)).reshape(SSM_NB, SSM_UB, SSM_CB)


def _extract_b(d):
    t = d.reshape(SSM_NB, _GPB, SSM_H, _GPB, SSM_P)
    t = jnp.einsum('mjhkp,jk->mjhp', t, jnp.eye(_GPB, dtype=d.dtype))
    return t.reshape(SSM_G, SSM_H, SSM_P).transpose(0, 2, 1)


def _embed_c(c):
    t = c.transpose(0, 2, 1).reshape(SSM_NB, _GPB, SSM_P, SSM_H)
    return jnp.einsum('mjph,jk->mjpkh', t, jnp.eye(_GPB, dtype=c.dtype)).reshape(SSM_NB, SSM_CB, SSM_UB)


def _extract_c(d):
    t = d.reshape(SSM_NB, _GPB, SSM_P, _GPB, SSM_H)
    t = jnp.einsum('mjpkh,jk->mjph', t, jnp.eye(_GPB, dtype=d.dtype))
    return t.reshape(SSM_G, SSM_P, SSM_H).transpose(0, 2, 1)


def _scan_tables(a_re, a_im, reverse):
    ar = a_re.reshape(1, SSM_CH)
    ai = (-a_im if reverse else a_im).reshape(1, SSM_CH)
    pr, pi = [ar], [ai]
    for _ in range(SUBLANES - 1):
        pr, pi = pr + [pr[-1] * ar - pi[-1] * ai], pi + [pr[-1] * ai + pi[-1] * ar]
    rows = jnp.arange(SUBLANES)[:, None]
    out = []
    for k in (1, 2, 4):
        valid = (rows + k <= SUBLANES - 1) if reverse else (rows >= k)
        out += [jnp.where(valid, pr[k - 1], 0.0), jnp.where(valid, pi[k - 1], 0.0)]
    order = list(range(SUBLANES - 1, -1, -1)) if reverse else list(range(SUBLANES))
    out += [jnp.concatenate([pr[n] for n in order], axis=0), jnp.concatenate([pi[n] for n in order], axis=0)]
    return jnp.stack(out).astype(F32)


def _pad_heads(w, d):
    lead = w.shape[:-1]
    w = w.reshape(lead + (N_HEADS, d))
    w = jnp.pad(w, [(0, 0)] * len(lead) + [(0, 0), (0, HEAD_SLOT - d)])
    return w.reshape(lead + (HP,))


def _unpad_heads(w, d):
    lead = w.shape[:-1]
    return w.reshape(lead + (N_HEADS, HEAD_SLOT))[..., :d].reshape(lead + (N_HEADS * d,))


def _chip_major(w, axis):
    k, n = w.shape
    if axis == 0:
        return w.reshape(N_CHIPS, k // N_CHIPS, n)
    return w.reshape(k, N_CHIPS, n // N_CHIPS).transpose(1, 0, 2)


def _from_chip_major(w, axis):
    if axis == 0:
        return w.reshape(-1, w.shape[2])
    return w.transpose(1, 0, 2).reshape(w.shape[1], -1)


def _pad_w_in(w):
    z = lambda n: jnp.zeros((w.shape[0], n), w.dtype)
    return jnp.concatenate([w[:, :640], z(KR_LANE), w[:, 640:672], z(HEAD_SLOT - KR_LANE - QK_ROPE), w[:, 672:]], axis=1)


def _unpad_w_in(w):
    return jnp.concatenate([w[:, :640], w[:, P_KR + KR_LANE:P_KR + KR_LANE + QK_ROPE], w[:, P_U:]], axis=1)


def _local_step(x, positions, tgt, wts, sp):
    l = x.shape[0]
    tl = min(256, l)
    ta = min(512, l)

    inv_freq = ROPE_THETA ** (-jnp.arange(0, QK_ROPE, 2, dtype=F32) / QK_ROPE)
    ang = positions.astype(F32)[:, None] * inv_freq
    cos, sin = jnp.cos(ang), jnp.sin(ang)
    one = jnp.ones((l, KR_LANE), F32)
    rc = jnp.concatenate([one, cos, cos, jnp.ones((l, HEAD_SLOT - KR_LANE - QK_ROPE), F32)], axis=1)
    rs = jnp.concatenate([0 * one, -sin, sin, jnp.zeros((l, HEAD_SLOT - KR_LANE - QK_ROPE), F32)], axis=1)

    win = _pad_w_in(wts["w_in"])
    wuq = _pad_heads(wts["w_uq"], QK_HEAD)
    wukv = jnp.concatenate([_pad_heads(wts["w_uk"], QK_NOPE), _pad_heads(wts["w_uv"], V_HEAD)], axis=1)
    wba = jnp.pad(wts["w_branch_attn"].reshape(N_HEADS, V_HEAD, D_MODEL),
                  ((0, 0), (0, HEAD_SLOT - V_HEAD), (0, 0))).reshape(HP, D_MODEL)
    wbs, wglu, wout = wts["w_branch_ssm"], wts["w_glu"], wts["w_out"]

    disc_in = (sp["ssm_lambda_re"], sp["ssm_lambda_im"], sp["ssm_log_dt"], sp["ssm_b_re"], sp["ssm_b_im"])
    (a_re, a_im, bb_re, bb_im), disc_vjp = jax.vjp(_ssm_disc, *disc_in)
    bre, bim = _embed_b(bb_re).astype(BF16), _embed_b(bb_im).astype(BF16)
    cre, cim = _embed_c(sp["ssm_c_re"]).astype(BF16), _embed_c(sp["ssm_c_im"]).astype(BF16)
    dvec = sp["ssm_d"].reshape(1, SSM_W)
    tab_f = _scan_tables(a_re, a_im, False)
    tab_r = _scan_tables(a_re, a_im, True)

    g1, gq, gkv = sp["mix_norm_pre"], sp["q_norm"], sp["kv_norm"]
    gpost, gpre, gfin = sp["mix_norm_post"], sp["ffn_norm_pre"], sp["ffn_norm_post"]
    bgate, bglu, convb = sp["b_gate"], sp["b_glu"], sp["conv_b"]
    convw = sp["conv_w"]

    hn, cq, ckv, q, k, v, u, gl = _proj_fwd(x, g1, win, gq, wuq, gkv, wukv, rc, rs, bgate, tl)
    attn, lse = _attn_fwd(q, k, v, ta)
    y1, sre, sim = _ssm_fwd(u, bre, bim, cre, cim, dvec, tab_f, ta)
    a, sm, merged, z, x1, hn2, y3 = _merge_fwd(x, gl, attn, y1, wba, wbs, wglu, bglu, wout, gpost, gpre, tl)
    wup, wdown = wts["ffn"](x1)
    h = _mm(hn2, wup, "ffn_up")
    act = _conv_fwd(h, convw, convb, ta)
    ff = _mm(act, wdown, "ffn_down")
    loss, dx2, dff, dgfin = _loss_head(ff, x1, tgt, gfin, tl)

    dact = _mm(dff, wdown, "ffn_down_dx", out_dtype=BF16, bt=True)
    d_wdown = _mm_tn(act, dff, "ffn_down_dw")
    dh, dwg, dwv, dbg, dbv = _conv_bwd(h, dact, convw, convb, ta)
    d_convw = jnp.concatenate([dwg, dwv], axis=1)
    d_convb = jnp.concatenate([dbg, dbv], axis=1)
    dhn2a = _mm(dh, wup, "ffn_up_dx_gate", bt=True, b_col0=0, a_lead=0)
    dhn2b = _mm(dh, wup, "ffn_up_dx_val", bt=True, b_col0=1, a_lead=1)
    d_wup = _mm_tn(hn2, dh, "ffn_up_dw", chips=True)
    behind = wts["ffn_grads"](d_wup, _chip_major(d_wdown, 0))
    (dx1, dz, dbra, dbrs, dgl, dattn, dy1, dt, y2, dgpre, dgpost, dbgate, dbglu) = _merge_bwd(
        dhn2a, dhn2b, x1, dx2, z, gl, a, sm, y1, wba, wbs, wglu, bglu, wout, gpost, gpre + behind, tl)
    d_wout = _mm_tn(merged, dz, "w_out_dw")
    d_wba = _mm_tn(attn, dbra, "w_branch_attn_dw", chips=True)
    d_wbs = _mm_tn(y3, dbrs, "w_branch_ssm_dw", chips=True)
    d_wglu = _mm_tn(y2, dt, "w_glu_dw")
    dq, dk, dv = _attn_bwd(q, k, v, dattn, lse, _attn_delta(attn, dattn, min(2048, l)), ta)
    du, dbre, dbim, dcre, dcim, dare, daim, dd = _ssm_bwd(dy1, u, sre, sim, bre, bim, cre, cim, dvec, tab_r, ta)
    gx, dql, qn, ckvn, dproj, dg1, dgq, dgkv = _proj_bwd(
        x, dx1, cq, ckv, dq, dk, dv, du, dgl, g1, win, gq, wuq, gkv, wukv, rc, rs, tl)
    d_win = _mm_tn(hn, dproj, "w_in_dw")
    d_wuq = _mm_tn(qn, dql, "w_uq_dw")
    d_wuk = _mm_tn(ckvn, dk, "w_uk_dw")
    d_wuv = _mm_tn(ckvn, dv, "w_uv_dw")

    d_lre, d_lim, d_ldt, d_bre, d_bim = disc_vjp((dare.reshape(SSM_G, SSM_P), daim.reshape(SSM_G, SSM_P),
                                                  _extract_b(dbre), _extract_b(dbim)))
    ncol = D_MODEL // N_CHIPS
    big = {
        "w_in": _chip_major(_unpad_w_in(d_win), 1),
        "w_uq": _chip_major(_unpad_heads(d_wuq, QK_HEAD), 1),
        "w_uk": _chip_major(_unpad_heads(d_wuk, QK_NOPE), 1),
        "w_uv": _chip_major(_unpad_heads(d_wuv, V_HEAD), 1),
        "w_glu": _chip_major(d_wglu, 0),
        "w_branch_attn": d_wba.reshape(N_CHIPS, N_HEADS, HEAD_SLOT, ncol)[:, :, :V_HEAD].reshape(
            N_CHIPS, N_HEADS * V_HEAD, ncol),
        "w_branch_ssm": d_wbs,
        "w_out": _chip_major(d_wout, 0),
    }
    small = {
        "conv_w": d_convw,
        "mix_norm_pre": dg1, "q_norm": dgq, "kv_norm": dgkv,
        "ssm_lambda_re": d_lre, "ssm_lambda_im": d_lim, "ssm_log_dt": d_ldt,
        "ssm_b_re": d_bre, "ssm_b_im": d_bim,
        "ssm_c_re": _extract_c(dcre), "ssm_c_im": _extract_c(dcim),
        "ssm_d": dd.reshape(SSM_G, SSM_H), "b_glu": dbglu, "b_gate": dbgate,
        "mix_norm_post": dgpost, "ffn_norm_pre": dgpre, "conv_b": d_convb, "ffn_norm_post": dgfin,
    }
    return loss[0, 0], gx, big, small


_ANY = pl.BlockSpec(memory_space=pl.ANY)


ROW_TILE = 16


def _place():
    x, y, c = lax.axis_index("x"), lax.axis_index("y"), lax.axis_index("c")
    return x, y, c, 2 * x + y, [(1 - x, y), (x, 1 - y), (1 - x, 1 - y)]


def _half(rows, which):
    hr = rows // 2
    return pl.ds(pl.multiple_of(which * hr, ROW_TILE), hr)


def _remote(src, dst, send_sems, recv_sems, n, dev):
    return pltpu.make_async_remote_copy(src_ref=src, dst_ref=dst, send_sem=send_sems.at[n], recv_sem=recv_sems.at[n],
                                        device_id=dev, device_id_type=MESH)


def _gather_big(shards):
    nw = len(shards)
    rows = [s.shape[0] for s in shards]

    def body(*refs):
        ins, outs = refs[:nw], refs[nw:2 * nw]
        ici_send, ici_recv, d2d_send, d2d_recv = refs[2 * nw:]
        x, y, c, me, peers = _place()
        sent = []
        for i in range(nw):
            for p, (px, py) in enumerate(peers):
                cp = _remote(ins[i].at[_half(rows[i], c)], outs[i].at[me, _half(rows[i], c)], ici_send, ici_recv,
                             3 * i + p, (px, py, c))
                cp.start()
                sent.append(cp)
        for p, (px, py) in enumerate(peers):
            for i in range(nw):
                blk = outs[i].at[2 * px + py, _half(rows[i], c)]
                _remote(blk, blk, ici_send, ici_recv, 3 * i + p, (px, py, c)).wait_recv()
                cp = _remote(blk, blk, d2d_send, d2d_recv, 3 * i + p, (x, y, 1 - c))
                cp.start()
                sent.append(cp)
        for p, (px, py) in enumerate(peers):
            for i in range(nw):
                blk = outs[i].at[2 * px + py, _half(rows[i], 1 - c)]
                _remote(blk, blk, d2d_send, d2d_recv, 3 * i + p, (x, y, 1 - c)).wait_recv()
        for cp in sent:
            cp.wait_send()

    dma = pltpu.SemaphoreType.DMA
    return pl.pallas_call(
        body, name="gather_weights", in_specs=[_ANY] * nw, out_specs=[_ANY] * nw,
        out_shape=[jax.ShapeDtypeStruct((N_CHIPS,) + s.shape, s.dtype) for s in shards],
        scratch_shapes=[dma((3 * nw,)), dma((3 * nw,)), dma((3 * nw,)), dma((3 * nw,))],
    )(*shards)


_HBM = pl.BlockSpec(memory_space=pltpu.HBM)
_SEM = pl.BlockSpec(memory_space=pltpu.SEMAPHORE)
_DATAFLOW = pltpu.SideEffectType.DATAFLOW_SIDE_EFFECTING


def _exchange_start(shards, name, scatter):
    nw = len(shards)
    lands = [lax.empty(s.shape if scatter else (N_CHIPS,) + s.shape, s.dtype) for s in shards]

    def body(*refs):
        ins, zones = refs[:nw], refs[nw:2 * nw]
        send_sems, recv_sems, token = refs[2 * nw], refs[2 * nw + 1], refs[-1]
        x, y, c, me, peers = _place()
        for i in range(nw):
            for p, (px, py) in enumerate(peers):
                src = ins[i].at[2 * px + py] if scatter else ins[i]
                _remote(src, zones[i].at[me], send_sems, recv_sems, 3 * i + p, (px, py, c)).start()
        token[...] = jnp.zeros_like(token)

    thru = [pltpu.HBM(a.shape, a.dtype) for a in list(shards) + lands]
    dma = pltpu.SemaphoreType.DMA
    outs = pl.pallas_call(
        body, name=name,
        out_shape=(dma((3 * nw,)), dma((3 * nw,)), *thru, jax.ShapeDtypeStruct((SUBLANES, LANES), F32)),
        in_specs=[_HBM] * (2 * nw),
        out_specs=(_SEM, _SEM, *([_HBM] * (2 * nw)), pl.BlockSpec(memory_space=pltpu.VMEM)),
        input_output_aliases={i: 2 + i for i in range(2 * nw)},
        compiler_params=pltpu.CompilerParams(has_side_effects=_DATAFLOW),
    )(*[pltpu.with_memory_space_constraint(a, pltpu.HBM) for a in list(shards) + lands])
    return outs[0], outs[1], list(outs[2:2 + nw]), list(outs[2 + nw:2 + 2 * nw]), outs[-1]


def _exchange_wait(send_sems, recv_sems, shards, lands, after, name, scatter):
    nw = len(shards)

    def body(*refs):
        ins, zones = refs[:nw], refs[nw:2 * nw]
        send_sems, recv_sems = refs[2 * nw], refs[2 * nw + 1]
        x, y, c, me, peers = _place()
        for i in range(nw):
            for p, (px, py) in enumerate(peers):
                src = ins[i].at[2 * px + py] if scatter else ins[i]
                cp = _remote(src, zones[i].at[2 * px + py], send_sems, recv_sems, 3 * i + p, (px, py, c))
                cp.wait_send()
                cp.wait_recv()

    both = list(shards) + list(lands)
    outs = pl.pallas_call(
        body, name=name,
        out_shape=tuple(pltpu.HBM(a.shape, a.dtype) for a in both),
        in_specs=(*([_HBM] * (2 * nw)), _SEM, _SEM, _ANY), out_specs=[_HBM] * (2 * nw),
        input_output_aliases={i: i for i in range(2 * nw)},
        compiler_params=pltpu.CompilerParams(has_side_effects=_DATAFLOW),
    )(*both, send_sems, recv_sems, after)
    return list(outs[:nw]), list(outs[nw:])


def _reduce_to_sibling(grads, name):
    nw = len(grads)

    def body(*refs):
        ins, outs = refs[:nw], refs[nw:2 * nw]
        send_sems, recv_sems = refs[2 * nw:]
        x, y, c, _, _ = _place()
        sent = []
        for i in range(nw):
            cp = _remote(ins[i].at[pl.ds(0, N_CHIPS), _half(grads[i].shape[1], 1 - c)], outs[i], send_sems, recv_sems,
                         i, (x, y, 1 - c))
            cp.start()
            sent.append(cp)
        for cp in sent:
            cp.wait()

    dma = pltpu.SemaphoreType.DMA
    return pl.pallas_call(
        body, name=name, in_specs=[_ANY] * nw, out_specs=[_ANY] * nw,
        out_shape=[jax.ShapeDtypeStruct((N_CHIPS, g.shape[1] // 2, g.shape[2]), g.dtype) for g in grads],
        scratch_shapes=[dma((nw,)), dma((nw,))],
    )(*grads)


def _reduce_between_chips(pairs):
    nw = len(pairs)

    def body(*refs):
        ins, outs = refs[:nw], refs[nw:2 * nw]
        send_sems, recv_sems = refs[2 * nw:]
        x, y, c, me, peers = _place()
        sent = []
        for i in range(nw):
            for p, (px, py) in enumerate(peers):
                cp = _remote(ins[i].at[2 * px + py], outs[i].at[me], send_sems, recv_sems, 3 * i + p, (px, py, c))
                cp.start()
                sent.append(cp)
        for i in range(nw):
            for p, (px, py) in enumerate(peers):
                blk = outs[i].at[2 * px + py]
                _remote(blk, blk, send_sems, recv_sems, 3 * i + p, (px, py, c)).wait_recv()
        for cp in sent:
            cp.wait_send()

    dma = pltpu.SemaphoreType.DMA
    return pl.pallas_call(
        body, name="reduce_grads_ici", in_specs=[_ANY] * nw, out_specs=[_ANY] * nw,
        out_shape=[jax.ShapeDtypeStruct(p.shape, p.dtype) for p in pairs],
        scratch_shapes=[dma((3 * nw,)), dma((3 * nw,))],
    )(*pairs)


def _reduce_back(totals):
    nw = len(totals)

    def body(*refs):
        outs = refs[nw:2 * nw]
        send_sems, recv_sems = refs[2 * nw:]
        x, y, c, _, _ = _place()
        sent = []
        for i in range(nw):
            blk = outs[i].at[_half(totals[i].shape[0], c)]
            cp = _remote(blk, blk, send_sems, recv_sems, i, (x, y, 1 - c))
            cp.start()
            sent.append(cp)
        for i in range(nw):
            blk = outs[i].at[_half(totals[i].shape[0], 1 - c)]
            _remote(blk, blk, send_sems, recv_sems, i, (x, y, 1 - c)).wait_recv()
        for cp in sent:
            cp.wait_send()

    dma = pltpu.SemaphoreType.DMA
    return pl.pallas_call(
        body, name="reduce_grads_back", in_specs=[_ANY] * nw, out_specs=[_ANY] * nw,
        out_shape=[jax.ShapeDtypeStruct(t.shape, t.dtype) for t in totals],
        input_output_aliases={i: i for i in range(nw)},
        scratch_shapes=[dma((nw,)), dma((nw,))],
    )(*totals)


def _all_reduce_small(v, name):
    rows, w = v.shape

    def body(v_ref, out_ref, buf_ref, send_sems, recv_sems):
        x, y, c = lax.axis_index("x"), lax.axis_index("y"), lax.axis_index("c")
        me = 4 * x + 2 * y + c

        def flip(n):
            return (1 - x if n & 4 else x, 1 - y if n & 2 else y, 1 - c if n & 1 else c)

        sent = []
        for n in range(1, 8):
            cp = pltpu.make_async_remote_copy(src_ref=v_ref, dst_ref=buf_ref.at[me], send_sem=send_sems.at[n - 1],
                                              recv_sem=recv_sems.at[n - 1], device_id=flip(n), device_id_type=MESH)
            cp.start()
            sent.append(cp)
        buf_ref[me] = v_ref[...]
        for n in range(1, 8):
            px, py, pc = flip(n)
            pltpu.make_async_remote_copy(src_ref=v_ref, dst_ref=buf_ref.at[4 * px + 2 * py + pc],
                                         send_sem=send_sems.at[n - 1], recv_sem=recv_sems.at[n - 1],
                                         device_id=flip(n), device_id_type=MESH).wait_recv()
        for cp in sent:
            cp.wait_send()
        acc = buf_ref[0]
        for d in range(1, 8):
            acc = acc + buf_ref[d]
        out_ref[...] = acc

    vm = pl.BlockSpec(memory_space=pltpu.VMEM)
    return pl.pallas_call(
        body, name=name, in_specs=[vm], out_specs=vm,
        out_shape=jax.ShapeDtypeStruct((rows, w), F32),
        scratch_shapes=[pltpu.VMEM((8, rows, w), F32), pltpu.SemaphoreType.DMA((7,)), pltpu.SemaphoreType.DMA((7,))],
        compiler_params=pltpu.CompilerParams(vmem_limit_bytes=VMEM_LIMIT),
    )(v)


ELEMENTWISE_BLOCK = 256 * 1024


def _rows_tile(rows, cols):
    best = None
    for t in range(SUBLANES, rows + 1, SUBLANES):
        if rows % t == 0 and t * cols <= ELEMENTWISE_BLOCK:
            best = t
    return rows if best is None else best


def _add_pair(g, t, core, name):
    nb, n, w = t.shape
    tr = _rows_tile(n, w)
    steps = n // tr

    def body(core_ref, g_ref, t_ref, o_ref):
        o_ref[...] = (g_ref[...] + t_ref[...]).astype(BF16)

    spec = pl.BlockSpec((1, tr, w), lambda j, i, core_ref: (j, i, 0))
    return pl.pallas_call(
        body, name=name,
        grid_spec=pltpu.PrefetchScalarGridSpec(
            num_scalar_prefetch=1, grid=(nb, steps),
            in_specs=[pl.BlockSpec((1, tr, w), lambda j, i, core_ref: (j, core_ref[0] * steps + i, 0)), spec],
            out_specs=spec),
        out_shape=jax.ShapeDtypeStruct(t.shape, BF16),
        compiler_params=_params(("parallel", "parallel")))(core, g, t)


def _add_chips(landed, pairs, place, name):
    nb, n, w = landed.shape
    tr = _rows_tile(n, w)
    steps = n // tr

    def body(place_ref, r_ref, own_ref, o_ref):
        me = place_ref[0]
        acc = None
        for k in range(nb):
            blk = jnp.where(me == k, own_ref[0], r_ref[k]).astype(F32)
            acc = blk if acc is None else acc + blk
        o_ref[...] = acc

    return pl.pallas_call(
        body, name=name,
        grid_spec=pltpu.PrefetchScalarGridSpec(
            num_scalar_prefetch=1, grid=(steps,),
            in_specs=[pl.BlockSpec((nb, tr, w), lambda i, place_ref: (0, i, 0)),
                      pl.BlockSpec((1, tr, w), lambda i, place_ref: (place_ref[0], i, 0))],
            out_specs=pl.BlockSpec((tr, w), lambda i, place_ref: (place_ref[1] * steps + i, 0))),
        out_shape=jax.ShapeDtypeStruct((2 * n, w), F32),
        compiler_params=_params(("parallel",)))(place, landed, pairs)


def _adamw(w, g, m, v, name):
    rows, wd = w.shape
    tr = _rows_tile(rows, wd)
    c1 = 1.0 - ADAM_B1 ** ADAM_STEP
    c2 = 1.0 - ADAM_B2 ** ADAM_STEP

    def body(w_ref, g_ref, m_ref, v_ref, d_ref, mo_ref, vo_ref):
        gv = g_ref[...]
        m2 = ADAM_B1 * m_ref[...] + (1.0 - ADAM_B1) * gv
        v2 = ADAM_B2 * v_ref[...] + (1.0 - ADAM_B2) * (gv * gv)
        mo_ref[...] = m2
        vo_ref[...] = v2
        d_ref[...] = -ADAM_LR * ((m2 / c1) / (jnp.sqrt(v2 / c2) + ADAM_EPS) + ADAM_WD * w_ref[...])

    spec = pl.BlockSpec((tr, wd), lambda i: (i, 0))
    shp = jax.ShapeDtypeStruct((rows, wd), F32)
    return pl.pallas_call(body, name=name, grid=(rows // tr,), in_specs=[spec] * 4, out_specs=[spec] * 3,
                          out_shape=[shp] * 3, compiler_params=_params(("parallel",)))(w, g, m, v)


BIG = [("w_in", (1024, 3232), 1), ("w_uq", (384, 768), 1), ("w_uk", (256, 512), 1), ("w_uv", (256, 512), 1),
       ("w_glu", (512, 512), 0), ("w_branch_attn", (512, 1024), 1), ("w_branch_ssm", (512, 1024), 1),
       ("w_out", (1024, 1024), 0), ("w_up", (1024, 5632), 1), ("conv_w", (3, 5632), 1), ("w_down", (2816, 1024), 0)]
SMALL = [("mix_norm_pre", (1024,)), ("q_norm", (384,)), ("kv_norm", (256,)), ("ssm_lambda_re", (32, 64)),
         ("ssm_lambda_im", (32, 64)), ("ssm_log_dt", (32,)), ("ssm_b_re", (32, 64, 16)), ("ssm_b_im", (32, 64, 16)),
         ("ssm_c_re", (32, 16, 64)), ("ssm_c_im", (32, 16, 64)), ("ssm_d", (32, 16)), ("b_glu", (512,)),
         ("b_gate", (2048,)), ("mix_norm_post", (1024,)), ("ffn_norm_pre", (1024,)), ("conv_b", (5632,)),
         ("ffn_norm_post", (1024,))]
MATMUL_W = [b for b in BIG if b[0] != "conv_w"]
FFN_W = ("w_up", "w_down")
CONV_W_SHAPE = (3, 2 * D_FF)
CONV_W_SHARD = (3, 2 * D_FF // N_CHIPS)
SMALL_SUM = [("loss", (1,))] + SMALL + [("conv_w", CONV_W_SHAPE)]
SMALL_ADAM = SMALL + [("conv_w", CONV_W_SHARD)]


def _pack_flat(layout, vals):
    flat = jnp.concatenate([vals[n].astype(F32).reshape(-1) for n, _ in layout])
    rows = -(-(-(-flat.shape[0] // FLAT_W)) // SUBLANES) * SUBLANES
    return jnp.pad(flat, (0, rows * FLAT_W - flat.shape[0])).reshape(rows, FLAT_W)


def _unpack_flat(layout, flat):
    flat = flat.reshape(-1)
    out = {}
    o = 0
    for name, shape in layout:
        n = math.prod(shape)
        out[name] = flat[o:o + n].reshape(shape)
        o += n
    return out


_ARG_NAMES = ["x", "positions"] + [n for n in (
    "mix_norm_pre", "w_in", "q_norm", "w_uq", "kv_norm", "w_uk", "w_uv", "ssm_lambda_re", "ssm_lambda_im", "ssm_log_dt",
    "ssm_b_re", "ssm_b_im", "ssm_c_re", "ssm_c_im", "ssm_d", "w_glu", "b_glu", "w_branch_attn", "w_branch_ssm",
    "b_gate", "w_out", "mix_norm_post", "ffn_norm_pre", "w_up", "conv_w", "conv_b", "w_down", "ffn_norm_post")]
_WEIGHTS = _ARG_NAMES[2:]


def _gather_weights(w):
    c = lax.axis_index("c")
    early = [b for b in MATMUL_W if b[0] not in FFN_W]
    late = [b for b in MATMUL_W if b[0] in FFN_W]
    own = (jnp.arange(N_CHIPS) == 2 * lax.axis_index("x") + lax.axis_index("y"))[:, None, None]

    def whole(layout, mine, gathered):
        return {name: _from_chip_major(jnp.where(own, s[None], g), axis)
                for (name, _, axis), s, g in zip(layout, mine, gathered)}

    mine = [w[name].astype(BF16) for name, _, _ in early]
    full = whole(early, mine, _gather_big(mine))
    mine_late = [w[name].astype(BF16) for name, _, _ in late]
    _, mine_late = lax.optimization_barrier((full["w_in"], mine_late))
    send_sems, recv_sems, shards_thru, lands_thru, token = _exchange_start(mine_late, "gather_ffn_start", scatter=False)

    def ffn_weights(after):
        shards, lands = _exchange_wait(send_sems, recv_sems, shards_thru, lands_thru, after, "gather_ffn_wait",
                                       scatter=False)
        got = whole(late, shards, lands)
        return got["w_up"], got["w_down"]

    full["ffn"] = ffn_weights
    full["token"] = token[0, 0]

    chip = 2 * lax.axis_index("x") + lax.axis_index("y")
    ncol = 2 * D_FF // N_CHIPS
    placed = lax.dynamic_update_slice_in_dim(jnp.zeros((3, 2 * D_FF), F32), w["conv_w"], chip * ncol, axis=1)
    placed = placed * jnp.where(c == 0, 1.0, 0.0)
    cw_rows = -(-(-(-3 * 2 * D_FF // FLAT_W)) // SUBLANES) * SUBLANES
    placed = jnp.pad(placed.reshape(-1), (0, cw_rows * FLAT_W - 3 * 2 * D_FF)).reshape(cw_rows, FLAT_W)
    conv_w_full = _all_reduce_small(placed, "gather_conv_w").reshape(-1)[:3 * 2 * D_FF].reshape(3, 2 * D_FF)
    return full, conv_w_full


def _pair_sums(names, grads, tag):
    core = lax.axis_index("c").astype(jnp.int32).reshape(1)
    theirs = _reduce_to_sibling(grads, "reduce_grads_d2d" + tag)
    return [_add_pair(g, t, core, "reduce_pair_" + n) for n, g, t in zip(names, grads, theirs)]


def _start_ffn_reduce(d_wup, d_wdown):
    pairs = _pair_sums(list(FFN_W), [d_wup, d_wdown], "_ffn")
    send_sems, recv_sems, pairs_thru, lands_thru, token = _exchange_start(pairs, "reduce_ffn_start", scatter=True)
    return (send_sems, recv_sems, pairs_thru, lands_thru), token[0, 0]


def _reduce_grads(gbig, ffn_pending, loss, gsmall):
    core = lax.axis_index("c").astype(jnp.int32).reshape(1)
    chip = (2 * lax.axis_index("x") + lax.axis_index("y")).astype(jnp.int32).reshape(1)
    place = jnp.concatenate([chip, core])
    names = [name for name, _, _ in MATMUL_W if name not in FFN_W]
    pairs = _pair_sums(names, [gbig[n] for n in names], "")
    landed = _reduce_between_chips(pairs)
    ffn_pairs, ffn_landed = _exchange_wait(*ffn_pending, gbig["w_in"], "reduce_ffn_wait", scatter=True)
    names, pairs, landed = names + list(FFN_W), pairs + ffn_pairs, list(landed) + ffn_landed
    totals = [_add_chips(r, p, place, "reduce_chips_" + n) for n, r, p in zip(names, landed, pairs)]
    g_red = dict(zip(names, _reduce_back(totals)))

    vals = dict(gsmall)
    vals["loss"] = loss
    small_red = _unpack_flat(SMALL_SUM, _all_reduce_small(_pack_flat(SMALL_SUM, vals), "reduce_small"))
    return g_red, small_red


def _step(args):
    x = args["x"][0]
    positions = args["positions"][0]
    tgt = args["loss_target"][0]
    w = {n: args[n][0] for n in _WEIGHTS}
    m = {n: args["m_" + n][0] for n in _WEIGHTS}
    v = {n: args["v_" + n][0] for n in _WEIGHTS}

    full, conv_w_full = _gather_weights(w)
    sp = {n: w[n].reshape(s) for n, s in SMALL}
    for n in ("mix_norm_pre", "q_norm", "kv_norm", "b_glu", "b_gate", "mix_norm_post", "ffn_norm_pre", "conv_b",
              "ffn_norm_post"):
        sp[n] = sp[n].reshape(1, -1)
    sp["conv_w"] = conv_w_full
    sp["mix_norm_pre"] = sp["mix_norm_pre"] + full.pop("token")
    pending = []

    def ffn_grads(d_wup, d_wdown):
        state, token = _start_ffn_reduce(d_wup, d_wdown)
        pending.append(state)
        return token

    full["ffn_grads"] = ffn_grads
    loss, gx, gbig, gsmall = _local_step(x, positions, tgt, full, sp)
    g_red, small_red = _reduce_grads(gbig, pending[0], loss, gsmall)

    chip = 2 * lax.axis_index("x") + lax.axis_index("y")
    grads = dict(small_red)
    grads["conv_w"] = lax.dynamic_slice_in_dim(small_red["conv_w"], chip * CONV_W_SHARD[1], CONV_W_SHARD[1], axis=1)
    grads.update(g_red)

    outs = {"grad_" + n: grads[n] for n in _WEIGHTS}
    for name, _, _ in MATMUL_W:
        d, m2, v2 = _adamw(w[name], grads[name], m[name], v[name], "adamw_" + name)
        outs["delta_" + name], outs["new_m_" + name], outs["new_v_" + name] = d, m2, v2
    d_sm, m_sm, v_sm = _adamw(_pack_flat(SMALL_ADAM, w), _pack_flat(SMALL_ADAM, grads), _pack_flat(SMALL_ADAM, m),
                              _pack_flat(SMALL_ADAM, v), "adamw_small")
    for prefix, flat in (("delta_", d_sm), ("new_m_", m_sm), ("new_v_", v_sm)):
        for n, val in _unpack_flat(SMALL_ADAM, flat).items():
            outs[prefix + n] = val
    outs = {n: val.reshape(args[n.split("_", 1)[1] if not n.startswith("new_") else n[6:]].shape)
            for n, val in outs.items()}
    res = [small_red["loss"][0], gx[None]]
    for prefix in ("grad_", "delta_", "new_m_", "new_v_"):
        res += [outs[prefix + n] for n in _WEIGHTS]
    return tuple(res)


def kernel(x, positions, mix_norm_pre, w_in, q_norm, w_uq, kv_norm, w_uk, w_uv, ssm_lambda_re, ssm_lambda_im, ssm_log_dt, ssm_b_re, ssm_b_im, ssm_c_re, ssm_c_im, ssm_d, w_glu, b_glu, w_branch_attn, w_branch_ssm, b_gate, w_out, mix_norm_post, ffn_norm_pre, w_up, conv_w, conv_b, w_down, ffn_norm_post, loss_target, m_mix_norm_pre, m_w_in, m_q_norm, m_w_uq, m_kv_norm, m_w_uk, m_w_uv, m_ssm_lambda_re, m_ssm_lambda_im, m_ssm_log_dt, m_ssm_b_re, m_ssm_b_im, m_ssm_c_re, m_ssm_c_im, m_ssm_d, m_w_glu, m_b_glu, m_w_branch_attn, m_w_branch_ssm, m_b_gate, m_w_out, m_mix_norm_post, m_ffn_norm_pre, m_w_up, m_conv_w, m_conv_b, m_w_down, m_ffn_norm_post, v_mix_norm_pre, v_w_in, v_q_norm, v_w_uq, v_kv_norm, v_w_uk, v_w_uv, v_ssm_lambda_re, v_ssm_lambda_im, v_ssm_log_dt, v_ssm_b_re, v_ssm_b_im, v_ssm_c_re, v_ssm_c_im, v_ssm_d, v_w_glu, v_b_glu, v_w_branch_attn, v_w_branch_ssm, v_b_gate, v_w_out, v_mix_norm_post, v_ffn_norm_pre, v_w_up, v_conv_w, v_conv_b, v_w_down, v_ffn_norm_post):
    given = dict(locals())
    return _step(given)
```

```python
import math

import jax
import jax.numpy as jnp
from jax import lax
from jax.experimental import pallas as pl
from jax.experimental.pallas import tpu as pltpu

F32 = jnp.float32
BF16 = jnp.bfloat16
MESH = pl.DeviceIdType.MESH

D_MODEL = 1024
N_HEADS = 8
QK_NOPE = 64
QK_ROPE = 32
QK_HEAD = QK_NOPE + QK_ROPE
V_HEAD = 64
Q_RANK = 384
KV_RANK = 256
ROPE_THETA = 10000.0
SSM_W = 512
SSM_H = 16
SSM_G = 32
SSM_P = 64
SSM_CH = SSM_G * SSM_P
D_FF = 2816
EPS = 1e-6
ADAM_LR = 0.001
ADAM_B1 = 0.9
ADAM_B2 = 0.999
ADAM_EPS = 1e-08
ADAM_WD = 0.01
ADAM_STEP = 10

LANES = 128
SUBLANES = 8
VMEM_LIMIT = 56 * 1024 * 1024

HEAD_SLOT = LANES
HP = N_HEADS * HEAD_SLOT
P_CQ, P_CKV, P_KR, P_U, P_GL, P_END = 0, 384, 640, 768, 1280, 3328
KR_LANE = 64

FLAT_W = 1024
N_CHIPS = 4


def _tile(n, cap):
    if n <= cap:
        return n
    best = None
    for t in range(LANES, cap + 1, LANES):
        if n % t == 0:
            best = t
    assert best is not None, (n, cap)
    return best


def _params(sem):
    return pltpu.CompilerParams(dimension_semantics=sem, vmem_limit_bytes=VMEM_LIMIT)


def _dot(a, b):
    return jnp.dot(a, b, preferred_element_type=F32)


def _dot_nt(a, b):
    return lax.dot_general(a, b, (((1,), (1,)), ((), ())), preferred_element_type=F32)


def _dot_tn(a, b):
    return lax.dot_general(a, b, (((0,), (0,)), ((), ())), preferred_element_type=F32)


def _rms(x, g):
    r = lax.rsqrt(jnp.mean(x * x, axis=-1, keepdims=True) + EPS)
    return x * r * g, r


def _rms_bwd(dy, x, g):
    r = lax.rsqrt(jnp.mean(x * x, axis=-1, keepdims=True) + EPS)
    dyg = dy * g
    dx = r * dyg - x * (r * r * r) * jnp.mean(dyg * x, axis=-1, keepdims=True)
    dg = jnp.sum(dy * x * r, axis=0, keepdims=True)
    return dx, dg


_GELU_K0 = math.sqrt(2.0 / math.pi)
_GELU_K1 = 0.044715


def _gelu(x):
    th = jnp.tanh(_GELU_K0 * (x + _GELU_K1 * x * x * x))
    return 0.5 * x * (1.0 + th)


def _gelu_grad(x):
    th = jnp.tanh(_GELU_K0 * (x + _GELU_K1 * x * x * x))
    return 0.5 * (1.0 + th) + 0.5 * x * (1.0 - th * th) * _GELU_K0 * (1.0 + 3.0 * _GELU_K1 * x * x)


def _sigmoid(x):
    return 1.0 / (1.0 + jnp.exp(-x))


def _rope(q, c, s):
    n = q.shape[1]
    lane = lax.broadcasted_iota(jnp.int32, q.shape, 1) % HEAD_SLOT
    sw = jnp.where(lane < KR_LANE + QK_ROPE // 2, pltpu.roll(q, n - QK_ROPE // 2, 1), pltpu.roll(q, QK_ROPE // 2, 1))
    return q * c + sw * s


def _rope_bwd(dy, c, s):
    n = dy.shape[1]
    t = dy * s
    lane = lax.broadcasted_iota(jnp.int32, dy.shape, 1) % HEAD_SLOT
    sw = jnp.where(lane < KR_LANE + QK_ROPE // 2, pltpu.roll(t, n - QK_ROPE // 2, 1), pltpu.roll(t, QK_ROPE // 2, 1))
    rope_lane = jnp.logical_and(lane >= KR_LANE, lane < KR_LANE + QK_ROPE)
    return dy * c + jnp.where(rope_lane, sw, 0.0)


def _shift_down(x, k, halo):
    xs = pltpu.roll(x, k, 0)
    hs = pltpu.roll(halo, k, 0)
    rows = lax.broadcasted_iota(jnp.int32, halo.shape, 0)
    top = jnp.where(rows < k, hs, xs[0:SUBLANES])
    return jnp.concatenate([top, xs[SUBLANES:]], axis=0)


def _shift_up(x, k, halo):
    t = x.shape[0]
    xs = pltpu.roll(x, t - k, 0)
    hs = pltpu.roll(halo, SUBLANES - k, 0)
    rows = lax.broadcasted_iota(jnp.int32, halo.shape, 0)
    bot = jnp.where(rows >= SUBLANES - k, hs, xs[t - SUBLANES:])
    return jnp.concatenate([xs[:t - SUBLANES], bot], axis=0)


def _mm(a, b, name, out_dtype=F32, bt=False, b_col0=0, n=None, tm_cap=1024, tn_cap=1408, a_lead=None):
    m, k = a.shape[-2:]
    if bt:
        n_full = b.shape[0]
        n = n_full
    else:
        n = b.shape[1] if n is None else n
    tm = min(tm_cap, m)
    tn = _tile(n, tn_cap)

    def body(a_ref, b_ref, o_ref):
        if bt:
            o_ref[...] = _dot_nt(a_ref[...], b_ref[...]).astype(out_dtype)
        else:
            o_ref[...] = _dot(a_ref[...], b_ref[...]).astype(out_dtype)

    if bt:
        b_spec = pl.BlockSpec((tn, k), lambda j, i: (j, b_col0))
    else:
        off = b_col0 * (n // tn)
        b_spec = pl.BlockSpec((k, tn), lambda j, i: (0, off + j))
    if a_lead is None:
        a_spec = pl.BlockSpec((tm, k), lambda j, i: (i, 0))
    else:
        a_spec = pl.BlockSpec((None, tm, k), lambda j, i: (a_lead, i, 0))
    return pl.pallas_call(
        body, name=name, grid=(n // tn, m // tm),
        in_specs=[a_spec, b_spec],
        out_specs=pl.BlockSpec((tm, tn), lambda j, i: (i, j)),
        out_shape=jax.ShapeDtypeStruct((m, n), out_dtype),
        compiler_params=_params(("parallel", "parallel")),
    )(a, b)


def _mm_tn(a, b, name, tk_cap=1024, tn_cap=1664, tl_cap=1024, chips=False):
    l, k = a.shape
    tk = _tile(k, tk_cap)
    tl = min(tl_cap, l)

    def body(a_ref, b_ref, o_ref):
        @pl.when(pl.program_id(2) == 0)
        def _():
            o_ref[...] = jnp.zeros_like(o_ref)

        o_ref[...] += _dot_tn(a_ref[...], b_ref[...])

    if chips:
        n = b.shape[-1] * (b.shape[0] if b.ndim == 3 else 1)
        tn = n // N_CHIPS
        assert tn % LANES == 0
        if b.ndim == 3:
            per = N_CHIPS // b.shape[0]
            b_spec = pl.BlockSpec((None, tl, tn), lambda i, j, r: (j // per, r, j % per))
        else:
            b_spec = pl.BlockSpec((tl, tn), lambda i, j, r: (r, j))
        out_spec = pl.BlockSpec((None, tk, tn), lambda i, j, r: (j, i, 0))
        out_shape = jax.ShapeDtypeStruct((N_CHIPS, k, tn), F32)
    else:
        n = b.shape[1]
        tn = _tile(n, tn_cap)
        b_spec = pl.BlockSpec((tl, tn), lambda i, j, r: (r, j))
        out_spec = pl.BlockSpec((tk, tn), lambda i, j, r: (i, j))
        out_shape = jax.ShapeDtypeStruct((k, n), F32)
    return pl.pallas_call(
        body, name=name, grid=(k // tk, n // tn, l // tl),
        in_specs=[pl.BlockSpec((tl, tk), lambda i, j, r: (r, i)), b_spec],
        out_specs=out_spec, out_shape=out_shape,
        compiler_params=_params(("parallel", "parallel", "arbitrary")),
    )(a, b)


def _row(tl, n):
    return pl.BlockSpec((tl, n), lambda i: (i, 0))


def _const(shape):
    return pl.BlockSpec(shape, lambda i: tuple(0 for _ in shape))


def _proj_fwd(x, g1, win, gq, wuq, gkv, wukv, rc, rs, bg, tl):
    l = x.shape[0]

    def body(x_ref, g1_ref, win_ref, gq_ref, wuq_ref, gkv_ref, wukv_ref, rc_ref, rs_ref, bg_ref,
             hn_ref, cq_ref, ckv_ref, q_ref, k_ref, v_ref, u_ref, gl_ref):
        hn, _ = _rms(x_ref[...], g1_ref[...])
        hnb = hn.astype(BF16)
        hn_ref[...] = hnb
        proj = _dot(hnb, win_ref[...])
        cq = proj[:, P_CQ:P_CKV]
        ckv = proj[:, P_CKV:P_KR]
        kr = proj[:, P_KR:P_U]
        cq_ref[...] = cq
        ckv_ref[...] = ckv
        u_ref[...] = proj[:, P_U:P_GL]
        gl_ref[...] = proj[:, P_GL:P_END] + bg_ref[...]
        qn, _ = _rms(cq, gq_ref[...])
        q = _dot(qn.astype(BF16), wuq_ref[...])
        c1 = rc_ref[...]
        s1 = rs_ref[...]
        q_ref[...] = (_rope(q, jnp.tile(c1, (1, N_HEADS)), jnp.tile(s1, (1, N_HEADS))) * Q_PRESCALE).astype(BF16)
        ckvn, _ = _rms(ckv, gkv_ref[...])
        kv = _dot(ckvn.astype(BF16), wukv_ref[...])
        krr = _rope(kr, c1, s1)
        k_ref[...] = (kv[:, :HP] + jnp.tile(krr, (1, N_HEADS))).astype(BF16)
        v_ref[...] = kv[:, HP:].astype(BF16)

    outs = [(D_MODEL, BF16), (Q_RANK, F32), (KV_RANK, F32), (HP, BF16), (HP, BF16), (HP, BF16),
            (SSM_W, F32), (2 * D_MODEL, F32)]
    return pl.pallas_call(
        body, name="proj_fwd", grid=(l // tl,),
        in_specs=[_row(tl, D_MODEL), _const((1, D_MODEL)), _const((D_MODEL, P_END)), _const((1, Q_RANK)),
                  _const((Q_RANK, HP)), _const((1, KV_RANK)), _const((KV_RANK, 2 * HP)),
                  _row(tl, HEAD_SLOT), _row(tl, HEAD_SLOT), _const((1, 2 * D_MODEL))],
        out_specs=[_row(tl, n) for n, _ in outs],
        out_shape=[jax.ShapeDtypeStruct((l, n), dt) for n, dt in outs],
        compiler_params=_params(("parallel",)),
    )(x, g1, win, gq, wuq, gkv, wukv, rc, rs, bg)


_NEG = -1e30


LOG2E = 1.0 / math.log(2.0)
LN2 = math.log(2.0)
ATTN_SCALE = 1.0 / math.sqrt(QK_HEAD)
Q_PRESCALE = ATTN_SCALE * LOG2E
HEADS_PER_STEP = 2
PAIR_W = HEADS_PER_STEP * HEAD_SLOT


def _causal_pairs(nq, by_query):
    if by_query:
        pairs = [(i, j) for i in range(nq) for j in range(i + 1)]
    else:
        pairs = [(i, j) for j in range(nq) for i in range(j, nq)]
    return jnp.array([p[0] for p in pairs], jnp.int32), jnp.array([p[1] for p in pairs], jnp.int32)


def _diag_mask_t(s):
    rows = lax.broadcasted_iota(jnp.int32, s.shape, 0)
    cols = lax.broadcasted_iota(jnp.int32, s.shape, 1)
    return jnp.where(rows <= cols, s, _NEG)


def _attn_fwd(q, k, v, tq):
    l = q.shape[0]
    nq = l // tq
    it, jt = _causal_pairs(nq, True)

    def body(it_ref, jt_ref, q_ref, k_ref, v_ref, o_ref, lse_ref, m_ref, l_ref, acc_ref):
        t = pl.program_id(1)
        i = it_ref[t]
        j = jt_ref[t]

        @pl.when(j == 0)
        def _():
            m_ref[...] = jnp.full_like(m_ref, _NEG)
            l_ref[...] = jnp.zeros_like(l_ref)
            acc_ref[...] = jnp.zeros_like(acc_ref)

        def update(on_diagonal):
            for hh in range(HEADS_PER_STEP):
                sl = slice(hh * HEAD_SLOT, (hh + 1) * HEAD_SLOT)
                s = _dot_nt(k_ref[:, sl], q_ref[:, sl])
                if on_diagonal:
                    s = _diag_mask_t(s)
                m_old = m_ref[hh]
                m_new = jnp.maximum(m_old, jnp.max(s, axis=0, keepdims=True))
                p = jnp.exp2(s - m_new)
                alpha = jnp.exp2(m_old - m_new)
                l_ref[hh] = alpha * l_ref[hh] + jnp.sum(p, axis=0, keepdims=True)
                acc_ref[hh] = alpha * acc_ref[hh] + _dot_tn(v_ref[:, sl], p.astype(BF16))
                m_ref[hh] = m_new

        @pl.when(j < i)
        def _():
            update(False)

        @pl.when(j == i)
        def _():
            update(True)
            for hh in range(HEADS_PER_STEP):
                sl = slice(hh * HEAD_SLOT, (hh + 1) * HEAD_SLOT)
                o_ref[:, sl] = (acc_ref[hh] / l_ref[hh]).T.astype(BF16)
                lse_ref[hh] = m_ref[hh] + jnp.log(l_ref[hh]) * LOG2E

    blk = (tq, PAIR_W)
    qmap = lambda h, t, it_ref, jt_ref: (it_ref[t], h)
    kmap = lambda h, t, it_ref, jt_ref: (jt_ref[t], h)
    row = pl.BlockSpec((HEADS_PER_STEP, 1, tq), lambda h, t, it_ref, jt_ref: (h, 0, it_ref[t]))
    return pl.pallas_call(
        body, name="attn_fwd",
        grid_spec=pltpu.PrefetchScalarGridSpec(
            num_scalar_prefetch=2, grid=(N_HEADS // HEADS_PER_STEP, it.shape[0]),
            in_specs=[pl.BlockSpec(blk, qmap), pl.BlockSpec(blk, kmap), pl.BlockSpec(blk, kmap)],
            out_specs=[pl.BlockSpec(blk, qmap), row],
            scratch_shapes=[pltpu.VMEM((HEADS_PER_STEP, 1, tq), F32), pltpu.VMEM((HEADS_PER_STEP, 1, tq), F32),
                            pltpu.VMEM((HEADS_PER_STEP, HEAD_SLOT, tq), F32)]),
        out_shape=[jax.ShapeDtypeStruct((l, HP), BF16), jax.ShapeDtypeStruct((N_HEADS, 1, l), F32)],
        compiler_params=_params(("parallel", "arbitrary")),
    )(it, jt, q, k, v)


def _attn_delta(o, do, tq):
    l = o.shape[0]

    def body(o_ref, do_ref, d_ref):
        prod = o_ref[...].astype(F32) * do_ref[...].astype(F32)
        for hh in range(HEADS_PER_STEP):
            d_ref[hh] = jnp.sum(prod[:, hh * HEAD_SLOT:(hh + 1) * HEAD_SLOT].T, axis=0, keepdims=True)

    blk = pl.BlockSpec((tq, PAIR_W), lambda h, i: (i, h))
    return pl.pallas_call(
        body, name="attn_delta", grid=(N_HEADS // HEADS_PER_STEP, l // tq), in_specs=[blk, blk],
        out_specs=pl.BlockSpec((HEADS_PER_STEP, 1, tq), lambda h, i: (h, 0, i)),
        out_shape=jax.ShapeDtypeStruct((N_HEADS, 1, l), F32),
        compiler_params=_params(("parallel", "parallel")),
    )(o, do)


def _attn_bwd(q, k, v, do, lse, delta, tq):
    l = q.shape[0]
    nq = l // tq
    it, jt = _causal_pairs(nq, False)

    def body(it_ref, jt_ref, q_ref, k_ref, v_ref, do_ref, lse_ref, dl_ref, dq_ref, dk_ref, dv_ref, dka_ref, dva_ref):
        t = pl.program_id(1)
        i = it_ref[t]
        j = jt_ref[t]

        @pl.when(t == 0)
        def _():
            dq_ref[...] = jnp.zeros_like(dq_ref)

        @pl.when(i == j)
        def _():
            dka_ref[...] = jnp.zeros_like(dka_ref)
            dva_ref[...] = jnp.zeros_like(dva_ref)

        def update(on_diagonal):
            r0 = pl.multiple_of(i * tq, tq)
            for hh in range(HEADS_PER_STEP):
                sl = slice(hh * HEAD_SLOT, (hh + 1) * HEAD_SLOT)
                qb = q_ref[:, sl]
                kb = k_ref[:, sl]
                dob = do_ref[:, sl]
                s = _dot_nt(kb, qb)
                if on_diagonal:
                    s = _diag_mask_t(s)
                p = jnp.exp2(s - lse_ref[hh])
                dva_ref[:, sl] += _dot(p.astype(BF16), dob)
                dp = _dot_nt(v_ref[:, sl], dob)
                ds = (p * (dp - dl_ref[hh])).astype(BF16)
                dka_ref[:, sl] += _dot(ds, qb)
                dq_ref[pl.ds(r0, tq), sl] += ATTN_SCALE * _dot_tn(ds, kb)

        @pl.when(j < i)
        def _():
            update(False)

        @pl.when(j == i)
        def _():
            update(True)

        @pl.when(i == nq - 1)
        def _():
            dk_ref[...] = (dka_ref[...] * LN2).astype(BF16)
            dv_ref[...] = dva_ref[...].astype(BF16)

    blk = (tq, PAIR_W)
    qmap = lambda h, t, it_ref, jt_ref: (it_ref[t], h)
    kmap = lambda h, t, it_ref, jt_ref: (jt_ref[t], h)
    row = pl.BlockSpec((HEADS_PER_STEP, 1, tq), lambda h, t, it_ref, jt_ref: (h, 0, it_ref[t]))
    return pl.pallas_call(
        body, name="attn_bwd",
        grid_spec=pltpu.PrefetchScalarGridSpec(
            num_scalar_prefetch=2, grid=(N_HEADS // HEADS_PER_STEP, it.shape[0]),
            in_specs=[pl.BlockSpec(blk, qmap), pl.BlockSpec(blk, kmap), pl.BlockSpec(blk, kmap),
                      pl.BlockSpec(blk, qmap), row, row],
            out_specs=[pl.BlockSpec((l, PAIR_W), lambda h, t, it_ref, jt_ref: (0, h)), pl.BlockSpec(blk, kmap),
                       pl.BlockSpec(blk, kmap)],
            scratch_shapes=[pltpu.VMEM(blk, F32), pltpu.VMEM(blk, F32)]),
        out_shape=[jax.ShapeDtypeStruct((l, HP), F32), jax.ShapeDtypeStruct((l, HP), BF16),
                   jax.ShapeDtypeStruct((l, HP), BF16)],
        compiler_params=_params(("parallel", "arbitrary")),
    )(it, jt, q, k, v, do, lse, delta)


SSM_CB = 512
SSM_UB = 128
SSM_NB = SSM_CH // SSM_CB


def _scan_tiles(re_ref, im_ref, tab, carry, n_tiles, reverse):
    group = 2
    assert n_tiles % group == 0
    pr, pi = tab[6], tab[7]

    def inside(sr, si):
        for step, k in enumerate((1, 2, 4)):
            mr, mi = tab[2 * step], tab[2 * step + 1]
            sh = (SUBLANES - k) if reverse else k
            rr = pltpu.roll(sr, sh, 0)
            ri = pltpu.roll(si, sh, 0)
            sr, si = sr + mr * rr - mi * ri, si + mr * ri + mi * rr
        return sr, si

    def body(n, c):
        cr, ci = c
        first = (n_tiles - group * (n + 1)) if reverse else group * n
        r0 = pl.multiple_of(first * SUBLANES, group * SUBLANES)
        rows = [pl.ds(r0 + g * SUBLANES, SUBLANES) for g in range(group)]
        tiles = [inside(re_ref[r, :], im_ref[r, :]) for r in rows]
        for g in (range(group - 1, -1, -1) if reverse else range(group)):
            sr, si = tiles[g]
            sr, si = sr + pr * cr - pi * ci, si + pr * ci + pi * cr
            re_ref[rows[g], :] = sr
            im_ref[rows[g], :] = si
            edge = slice(0, 1) if reverse else slice(SUBLANES - 1, SUBLANES)
            cr, ci = sr[edge, :], si[edge, :]
        return cr, ci

    return lax.fori_loop(0, n_tiles // group, body, carry)


def _ssm_fwd(u, bre, bim, cre, cim, dvec, tab, tt):
    l = u.shape[0]
    nt = l // tt

    def body(u_ref, bre_ref, bim_ref, cre_ref, cim_ref, d_ref, tab_ref, y_ref, sre_ref, sim_ref, car_ref):
        @pl.when(pl.program_id(1) == 0)
        def _():
            car_ref[...] = jnp.zeros_like(car_ref)

        uf = u_ref[...]
        ub = uf.astype(BF16)
        sre_ref[...] = _dot(ub, bre_ref[0])
        sim_ref[...] = _dot(ub, bim_ref[0])
        tab_v = [tab_ref[n] for n in range(8)]
        cr, ci = _scan_tiles(sre_ref, sim_ref, tab_v, (car_ref[0:1, :], car_ref[8:9, :]), tt // SUBLANES, False)
        car_ref[0:1, :] = cr
        car_ref[8:9, :] = ci
        y_ref[...] = (_dot(sre_ref[...].astype(BF16), cre_ref[0]) - _dot(sim_ref[...].astype(BF16), cim_ref[0])
                      + d_ref[...] * uf)

    return pl.pallas_call(
        body, name="ssm_fwd", grid=(SSM_NB, nt),
        in_specs=[pl.BlockSpec((tt, SSM_UB), lambda m, t: (t, m)),
                  pl.BlockSpec((1, SSM_UB, SSM_CB), lambda m, t: (m, 0, 0)),
                  pl.BlockSpec((1, SSM_UB, SSM_CB), lambda m, t: (m, 0, 0)),
                  pl.BlockSpec((1, SSM_CB, SSM_UB), lambda m, t: (m, 0, 0)),
                  pl.BlockSpec((1, SSM_CB, SSM_UB), lambda m, t: (m, 0, 0)),
                  pl.BlockSpec((1, SSM_UB), lambda m, t: (0, m)),
                  pl.BlockSpec((8, SUBLANES, SSM_CB), lambda m, t: (0, 0, m))],
        out_specs=[pl.BlockSpec((tt, SSM_UB), lambda m, t: (t, m)),
                   pl.BlockSpec((tt, SSM_CB), lambda m, t: (t, m)),
                   pl.BlockSpec((tt, SSM_CB), lambda m, t: (t, m))],
        out_shape=[jax.ShapeDtypeStruct((l, SSM_W), F32), jax.ShapeDtypeStruct((l, SSM_CH), F32),
                   jax.ShapeDtypeStruct((l, SSM_CH), F32)],
        scratch_shapes=[pltpu.VMEM((2 * SUBLANES, SSM_CB), F32)],
        compiler_params=_params(("parallel", "arbitrary")),
    )(u, bre, bim, cre, cim, dvec, tab)


def _ssm_bwd(dy, u, sre, sim, bre, bim, cre, cim, dvec, tab, tt):
    l = u.shape[0]
    nt = l // tt
    tpb = tt // SUBLANES

    def body(dy_ref, u_ref, sre_ref, sim_ref, hre_ref, him_ref, bre_ref, bim_ref, cre_ref, cim_ref, d_ref, tab_ref,
             du_ref, dbre_ref, dbim_ref, dcre_ref, dcim_ref, dare_ref, daim_ref, dd_ref, lr_ref, li_ref, car_ref):
        t = pl.program_id(1)

        @pl.when(t == 0)
        def _():
            car_ref[...] = jnp.zeros_like(car_ref)
            for ref in (dbre_ref, dbim_ref, dcre_ref, dcim_ref, dare_ref, daim_ref, dd_ref):
                ref[...] = jnp.zeros_like(ref)

        dyf = dy_ref[...]
        dyb = dyf.astype(BF16)
        uf = u_ref[...]
        s_re = sre_ref[...]
        s_im = sim_ref[...]
        lr_ref[...] = _dot_nt(dyb, cre_ref[0])
        li_ref[...] = -_dot_nt(dyb, cim_ref[0])
        dcre_ref[0] += _dot_tn(s_re.astype(BF16), dyb)
        dcim_ref[0] -= _dot_tn(s_im.astype(BF16), dyb)
        tab_v = [tab_ref[n] for n in range(8)]
        cr, ci = _scan_tiles(lr_ref, li_ref, tab_v, (car_ref[0:1, :], car_ref[8:9, :]), tpb, True)
        car_ref[0:1, :] = cr
        car_ref[8:9, :] = ci
        lam_r = lr_ref[...]
        lam_i = li_ref[...]
        keep = jnp.where(t == nt - 1, 0.0, 1.0)
        sp_r = _shift_down(s_re, 1, hre_ref[...] * keep)
        sp_i = _shift_down(s_im, 1, him_ref[...] * keep)
        dare_ref[...] += jnp.sum(lam_r * sp_r + lam_i * sp_i, axis=0, keepdims=True)
        daim_ref[...] += jnp.sum(lam_i * sp_r - lam_r * sp_i, axis=0, keepdims=True)
        lrb = lam_r.astype(BF16)
        lib = lam_i.astype(BF16)
        du_ref[...] = _dot_nt(lrb, bre_ref[0]) + _dot_nt(lib, bim_ref[0]) + dyf * d_ref[...]
        ub = uf.astype(BF16)
        dbre_ref[0] += _dot_tn(ub, lrb)
        dbim_ref[0] += _dot_tn(ub, lib)
        dd_ref[...] += jnp.sum(dyf * uf, axis=0, keepdims=True)

    rev = lambda m, t: (nt - 1 - t, m)
    halo = lambda m, t: (jnp.maximum((nt - 1 - t) * tpb - 1, 0), m)
    wb = pl.BlockSpec((1, SSM_UB, SSM_CB), lambda m, t: (m, 0, 0))
    wc = pl.BlockSpec((1, SSM_CB, SSM_UB), lambda m, t: (m, 0, 0))
    vec_c = pl.BlockSpec((1, SSM_CB), lambda m, t: (0, m))
    vec_u = pl.BlockSpec((1, SSM_UB), lambda m, t: (0, m))
    return pl.pallas_call(
        body, name="ssm_bwd", grid=(SSM_NB, nt),
        in_specs=[pl.BlockSpec((tt, SSM_UB), rev), pl.BlockSpec((tt, SSM_UB), rev),
                  pl.BlockSpec((tt, SSM_CB), rev), pl.BlockSpec((tt, SSM_CB), rev),
                  pl.BlockSpec((SUBLANES, SSM_CB), halo), pl.BlockSpec((SUBLANES, SSM_CB), halo),
                  wb, wb, wc, wc, vec_u,
                  pl.BlockSpec((8, SUBLANES, SSM_CB), lambda m, t: (0, 0, m))],
        out_specs=[pl.BlockSpec((tt, SSM_UB), rev), wb, wb, wc, wc, vec_c, vec_c, vec_u],
        out_shape=[jax.ShapeDtypeStruct((l, SSM_W), F32),
                   jax.ShapeDtypeStruct((SSM_NB, SSM_UB, SSM_CB), F32), jax.ShapeDtypeStruct((SSM_NB, SSM_UB, SSM_CB), F32),
                   jax.ShapeDtypeStruct((SSM_NB, SSM_CB, SSM_UB), F32), jax.ShapeDtypeStruct((SSM_NB, SSM_CB, SSM_UB), F32),
                   jax.ShapeDtypeStruct((1, SSM_CH), F32), jax.ShapeDtypeStruct((1, SSM_CH), F32),
                   jax.ShapeDtypeStruct((1, SSM_W), F32)],
        scratch_shapes=[pltpu.VMEM((tt, SSM_CB), F32), pltpu.VMEM((tt, SSM_CB), F32),
                        pltpu.VMEM((2 * SUBLANES, SSM_CB), F32)],
        compiler_params=_params(("parallel", "arbitrary")),
    )(dy, u, sre, sim, sre, sim, bre, bim, cre, cim, dvec, tab)


def _merge_fwd(x, gl, attn, y1, wba, wbs, wglu, bglu, wout, gpost, gpre, tl):
    l = x.shape[0]

    def body(x_ref, gl_ref, at_ref, y1_ref, wba_ref, wbs_ref, wglu_ref, bglu_ref, wout_ref, gpost_ref, gpre_ref,
             a_ref, sm_ref, mg_ref, z_ref, x1_ref, hn2_ref, y3_ref):
        y2 = _gelu(y1_ref[...])
        sg = _sigmoid(_dot(y2.astype(BF16), wglu_ref[...]) + bglu_ref[...])
        y3 = (y2 * sg).astype(BF16)
        y3_ref[...] = y3
        a = _dot(at_ref[...], wba_ref[...])
        sm = _dot(y3, wbs_ref[...])
        a_ref[...] = a
        sm_ref[...] = sm
        g = _sigmoid(gl_ref[...])
        merged = (g[:, :D_MODEL] * a + g[:, D_MODEL:] * sm).astype(BF16)
        mg_ref[...] = merged
        z = _dot(merged, wout_ref[...])
        z_ref[...] = z
        n, _ = _rms(z, gpost_ref[...])
        x1 = x_ref[...] + n
        x1_ref[...] = x1
        hn2, _ = _rms(x1, gpre_ref[...])
        hn2_ref[...] = hn2.astype(BF16)

    outs = [(D_MODEL, F32), (D_MODEL, F32), (D_MODEL, BF16), (D_MODEL, F32), (D_MODEL, F32), (D_MODEL, BF16),
            (SSM_W, BF16)]
    return pl.pallas_call(
        body, name="merge_fwd", grid=(l // tl,),
        in_specs=[_row(tl, D_MODEL), _row(tl, 2 * D_MODEL), _row(tl, HP), _row(tl, SSM_W),
                  _const((HP, D_MODEL)), _const((SSM_W, D_MODEL)), _const((SSM_W, SSM_W)), _const((1, SSM_W)),
                  _const((D_MODEL, D_MODEL)), _const((1, D_MODEL)), _const((1, D_MODEL))],
        out_specs=[_row(tl, n) for n, _ in outs],
        out_shape=[jax.ShapeDtypeStruct((l, n), dt) for n, dt in outs],
        compiler_params=_params(("parallel",)),
    )(x, gl, attn, y1, wba, wbs, wglu, bglu, wout, gpost, gpre)


def _merge_bwd(dhn2a, dhn2b, x1, dx2, z, gl, a, sm, y1, wba, wbs, wglu, bglu, wout, gpost, gpre, tl):
    l = x1.shape[0]

    def body(da_ref, db_ref, x1_ref, dx2_ref, z_ref, gl_ref, a_ref, sm_ref, y1_ref,
             wba_ref, wbs_ref, wglu_ref, bglu_ref, wout_ref, gpost_ref, gpre_ref,
             dx1_ref, dz_ref, dbra_ref, dbrs_ref, dgl_ref, dat_ref, dy1_ref, dt_ref, y2_ref,
             dgpre_ref, dgpost_ref, dbg_ref, dbglu_ref):
        @pl.when(pl.program_id(0) == 0)
        def _():
            for ref in (dgpre_ref, dgpost_ref, dbg_ref, dbglu_ref):
                ref[...] = jnp.zeros_like(ref)

        dhn2 = da_ref[...] + db_ref[...]
        dx1a, dgpre = _rms_bwd(dhn2, x1_ref[...], gpre_ref[...])
        dgpre_ref[...] += dgpre
        dx1 = dx2_ref[...] + dx1a
        dx1_ref[...] = dx1
        dz, dgpost = _rms_bwd(dx1, z_ref[...], gpost_ref[...])
        dgpost_ref[...] += dgpost
        dzb = dz.astype(BF16)
        dz_ref[...] = dzb
        dm = _dot_nt(dzb, wout_ref[...])
        g = _sigmoid(gl_ref[...])
        g0 = g[:, :D_MODEL]
        g1 = g[:, D_MODEL:]
        dbra = (dm * g0).astype(BF16)
        dbrs = (dm * g1).astype(BF16)
        dbra_ref[...] = dbra
        dbrs_ref[...] = dbrs
        dgl0 = dm * a_ref[...] * g0 * (1.0 - g0)
        dgl1 = dm * sm_ref[...] * g1 * (1.0 - g1)
        dgl_ref[:, :D_MODEL] = dgl0.astype(BF16)
        dgl_ref[:, D_MODEL:] = dgl1.astype(BF16)
        dbg_ref[:, :D_MODEL] += jnp.sum(dgl0, axis=0, keepdims=True)
        dbg_ref[:, D_MODEL:] += jnp.sum(dgl1, axis=0, keepdims=True)
        dat_ref[...] = _dot_nt(dbra, wba_ref[...]).astype(BF16)
        dy3 = _dot_nt(dbrs, wbs_ref[...])
        y1v = y1_ref[...]
        y2 = _gelu(y1v)
        y2b = y2.astype(BF16)
        y2_ref[...] = y2b
        sg = _sigmoid(_dot(y2b, wglu_ref[...]) + bglu_ref[...])
        dt = dy3 * y2 * sg * (1.0 - sg)
        dtb = dt.astype(BF16)
        dt_ref[...] = dtb
        dbglu_ref[...] += jnp.sum(dt, axis=0, keepdims=True)
        dy2 = dy3 * sg + _dot_nt(dtb, wglu_ref[...])
        dy1_ref[...] = dy2 * _gelu_grad(y1v)

    outs = [(D_MODEL, F32), (D_MODEL, BF16), (D_MODEL, BF16), (D_MODEL, BF16), (2 * D_MODEL, BF16), (HP, BF16),
            (SSM_W, F32), (SSM_W, BF16), (SSM_W, BF16)]
    accs = [D_MODEL, D_MODEL, 2 * D_MODEL, SSM_W]
    return pl.pallas_call(
        body, name="merge_bwd", grid=(l // tl,),
        in_specs=[_row(tl, D_MODEL), _row(tl, D_MODEL), _row(tl, D_MODEL), _row(tl, D_MODEL), _row(tl, D_MODEL),
                  _row(tl, 2 * D_MODEL), _row(tl, D_MODEL), _row(tl, D_MODEL), _row(tl, SSM_W),
                  _const((HP, D_MODEL)), _const((SSM_W, D_MODEL)), _const((SSM_W, SSM_W)), _const((1, SSM_W)),
                  _const((D_MODEL, D_MODEL)), _const((1, D_MODEL)), _const((1, D_MODEL))],
        out_specs=[_row(tl, n) for n, _ in outs] + [_const((1, n)) for n in accs],
        out_shape=[jax.ShapeDtypeStruct((l, n), dt) for n, dt in outs]
        + [jax.ShapeDtypeStruct((1, n), F32) for n in accs],
        compiler_params=_params(("arbitrary",)),
    )(dhn2a, dhn2b, x1, dx2, z, gl, a, sm, y1, wba, wbs, wglu, bglu, wout, gpost, gpre)


def _proj_bwd(x, dx1, cq, ckv, dq, dk, dv, du, dgl, g1, win, gq, wuq, gkv, wukv, rc, rs, tl):
    l = x.shape[0]

    def body(x_ref, dx1_ref, cq_ref, ckv_ref, dq_ref, dk_ref, dv_ref, du_ref, dgl_ref,
             g1_ref, win_ref, gq_ref, wuq_ref, gkv_ref, wukv_ref, rc_ref, rs_ref,
             gx_ref, dql_ref, qn_ref, ckvn_ref, dproj_ref, dg1_ref, dgq_ref, dgkv_ref):
        @pl.when(pl.program_id(0) == 0)
        def _():
            for ref in (dg1_ref, dgq_ref, dgkv_ref):
                ref[...] = jnp.zeros_like(ref)

        c1 = rc_ref[...]
        s1 = rs_ref[...]
        dql = _rope_bwd(dq_ref[...], jnp.tile(c1, (1, N_HEADS)), jnp.tile(s1, (1, N_HEADS))).astype(BF16)
        dql_ref[...] = dql
        dqn = _dot_nt(dql, wuq_ref[...])
        cq = cq_ref[...]
        qn, _ = _rms(cq, gq_ref[...])
        qn_ref[...] = qn.astype(BF16)
        dcq, dgq = _rms_bwd(dqn, cq, gq_ref[...])
        dgq_ref[...] += dgq
        dkb = dk_ref[...]
        dvb = dv_ref[...]
        dkf = dkb.astype(F32)
        dkr = dkf[:, 0:HEAD_SLOT]
        for h in range(1, N_HEADS):
            dkr = dkr + dkf[:, h * HEAD_SLOT:(h + 1) * HEAD_SLOT]
        dkr = _rope_bwd(dkr, c1, s1)
        dckvn = _dot_nt(dkb, wukv_ref[:, :HP]) + _dot_nt(dvb, wukv_ref[:, HP:])
        ckv = ckv_ref[...]
        ckvn, _ = _rms(ckv, gkv_ref[...])
        ckvn_ref[...] = ckvn.astype(BF16)
        dckv, dgkv = _rms_bwd(dckvn, ckv, gkv_ref[...])
        dgkv_ref[...] += dgkv
        dproj_ref[:, P_CQ:P_CKV] = dcq.astype(BF16)
        dproj_ref[:, P_CKV:P_KR] = dckv.astype(BF16)
        dproj_ref[:, P_KR:P_U] = dkr.astype(BF16)
        dproj_ref[:, P_U:P_GL] = du_ref[...].astype(BF16)
        dproj_ref[:, P_GL:P_END] = dgl_ref[...]
        dhn = _dot_nt(dproj_ref[...], win_ref[...])
        dxa, dg1 = _rms_bwd(dhn, x_ref[...], g1_ref[...])
        dg1_ref[...] += dg1
        gx_ref[...] = dx1_ref[...] + dxa

    outs = [(D_MODEL, F32), (HP, BF16), (Q_RANK, BF16), (KV_RANK, BF16), (P_END, BF16)]
    accs = [D_MODEL, Q_RANK, KV_RANK]
    return pl.pallas_call(
        body, name="proj_bwd", grid=(l // tl,),
        in_specs=[_row(tl, D_MODEL), _row(tl, D_MODEL), _row(tl, Q_RANK), _row(tl, KV_RANK), _row(tl, HP),
                  _row(tl, HP), _row(tl, HP), _row(tl, SSM_W), _row(tl, 2 * D_MODEL),
                  _const((1, D_MODEL)), _const((D_MODEL, P_END)), _const((1, Q_RANK)), _const((Q_RANK, HP)),
                  _const((1, KV_RANK)), _const((KV_RANK, 2 * HP)), _row(tl, HEAD_SLOT), _row(tl, HEAD_SLOT)],
        out_specs=[_row(tl, n) for n, _ in outs] + [_const((1, n)) for n in accs],
        out_shape=[jax.ShapeDtypeStruct((l, n), dt) for n, dt in outs]
        + [jax.ShapeDtypeStruct((1, n), F32) for n in accs],
        compiler_params=_params(("arbitrary",)),
    )(x, dx1, cq, ckv, dq, dk, dv, du, dgl, g1, win, gq, wuq, gkv, wukv, rc, rs)


CONV_CB = 256
CONV_NB = D_FF // CONV_CB


def _conv3(h, halo, w, b):
    return b + w[0:1, :] * _shift_down(h, 2, halo) + w[1:2, :] * _shift_down(h, 1, halo) + w[2:3, :] * h


def _conv_fwd(h, cw, cb, tl):
    l = h.shape[0]

    def body(hg_ref, hv_ref, wg_ref, wv_ref, bg_ref, bv_ref, act_ref, halo_ref):
        @pl.when(pl.program_id(1) == 0)
        def _():
            halo_ref[...] = jnp.zeros_like(halo_ref)

        hg = hg_ref[...]
        hv = hv_ref[...]
        cg = _conv3(hg, halo_ref[0:SUBLANES, :], wg_ref[...], bg_ref[...])
        cv = _conv3(hv, halo_ref[SUBLANES:, :], wv_ref[...], bv_ref[...])
        act_ref[...] = (_gelu(cg) * cv).astype(BF16)
        halo_ref[0:SUBLANES, :] = hg[tl - SUBLANES:, :]
        halo_ref[SUBLANES:, :] = hv[tl - SUBLANES:, :]

    gmap = lambda c, r: (r, c)
    vmap = lambda c, r: (r, CONV_NB + c)
    return pl.pallas_call(
        body, name="conv_fwd", grid=(CONV_NB, l // tl),
        in_specs=[pl.BlockSpec((tl, CONV_CB), gmap), pl.BlockSpec((tl, CONV_CB), vmap),
                  pl.BlockSpec((3, CONV_CB), lambda c, r: (0, c)), pl.BlockSpec((3, CONV_CB), lambda c, r: (0, CONV_NB + c)),
                  pl.BlockSpec((1, CONV_CB), lambda c, r: (0, c)), pl.BlockSpec((1, CONV_CB), lambda c, r: (0, CONV_NB + c))],
        out_specs=pl.BlockSpec((tl, CONV_CB), gmap),
        out_shape=jax.ShapeDtypeStruct((l, D_FF), BF16),
        scratch_shapes=[pltpu.VMEM((2 * SUBLANES, CONV_CB), F32)],
        compiler_params=_params(("parallel", "arbitrary")),
    )(h, h, cw, cw, cb, cb)


def _conv_bwd(h, dact, cw, cb, tl):
    l = h.shape[0]
    nr = l // tl
    tpb = tl // SUBLANES

    def body(hg_ref, hv_ref, hgh_ref, hvh_ref, da_ref, wg_ref, wv_ref, bg_ref, bv_ref,
             dh_ref, dwg_ref, dwv_ref, dbg_ref, dbv_ref, car_ref):
        r = pl.program_id(1)

        @pl.when(r == 0)
        def _():
            for ref in (car_ref, dwg_ref, dwv_ref, dbg_ref, dbv_ref):
                ref[...] = jnp.zeros_like(ref)

        keep = jnp.where(r == nr - 1, 0.0, 1.0)
        da = da_ref[...].astype(F32)

        def half(h_ref, halo, w, b):
            hh = h_ref[...]
            h1 = _shift_down(hh, 1, halo)
            h2 = _shift_down(hh, 2, halo)
            return hh, h1, h2, b + w[0:1, :] * h2 + w[1:2, :] * h1 + w[2:3, :] * hh

        wg = wg_ref[...]
        wv = wv_ref[...]
        hg, hg1, hg2, cg = half(hg_ref, hgh_ref[...] * keep, wg, bg_ref[...])
        hv, hv1, hv2, cv = half(hv_ref, hvh_ref[...] * keep, wv, bv_ref[...])
        dcg = da * cv * _gelu_grad(cg)
        dcv = da * _gelu(cg)

        def back(dc, hh, h1, h2, w, nxt, dw_ref, db_ref, part):
            db_ref[...] += jnp.sum(dc, axis=0, keepdims=True)
            dw_ref[0:1, :] += jnp.sum(dc * h2, axis=0, keepdims=True)
            dw_ref[1:2, :] += jnp.sum(dc * h1, axis=0, keepdims=True)
            dw_ref[2:3, :] += jnp.sum(dc * hh, axis=0, keepdims=True)
            dh = w[2:3, :] * dc + w[1:2, :] * _shift_up(dc, 1, nxt) + w[0:1, :] * _shift_up(dc, 2, nxt)
            dh_ref[part] = dh.astype(BF16)

        back(dcg, hg, hg1, hg2, wg, car_ref[0:SUBLANES, :], dwg_ref, dbg_ref, 0)
        back(dcv, hv, hv1, hv2, wv, car_ref[SUBLANES:, :], dwv_ref, dbv_ref, 1)
        car_ref[0:SUBLANES, :] = dcg[0:SUBLANES, :]
        car_ref[SUBLANES:, :] = dcv[0:SUBLANES, :]

    grev = lambda c, r: (nr - 1 - r, c)
    vrev = lambda c, r: (nr - 1 - r, CONV_NB + c)
    ghalo = lambda c, r: (jnp.maximum((nr - 1 - r) * tpb - 1, 0), c)
    vhalo = lambda c, r: (jnp.maximum((nr - 1 - r) * tpb - 1, 0), CONV_NB + c)
    colg = lambda c, r: (0, c)
    colv = lambda c, r: (0, CONV_NB + c)
    return pl.pallas_call(
        body, name="conv_bwd", grid=(CONV_NB, nr),
        in_specs=[pl.BlockSpec((tl, CONV_CB), grev), pl.BlockSpec((tl, CONV_CB), vrev),
                  pl.BlockSpec((SUBLANES, CONV_CB), ghalo), pl.BlockSpec((SUBLANES, CONV_CB), vhalo),
                  pl.BlockSpec((tl, CONV_CB), grev),
                  pl.BlockSpec((3, CONV_CB), colg), pl.BlockSpec((3, CONV_CB), colv),
                  pl.BlockSpec((1, CONV_CB), colg), pl.BlockSpec((1, CONV_CB), colv)],
        out_specs=[pl.BlockSpec((2, tl, CONV_CB), lambda c, r: (0, nr - 1 - r, c)),
                   pl.BlockSpec((3, CONV_CB), colg), pl.BlockSpec((3, CONV_CB), colg),
                   pl.BlockSpec((1, CONV_CB), colg), pl.BlockSpec((1, CONV_CB), colg)],
        out_shape=[jax.ShapeDtypeStruct((2, l, D_FF), BF16),
                   jax.ShapeDtypeStruct((3, D_FF), F32), jax.ShapeDtypeStruct((3, D_FF), F32),
                   jax.ShapeDtypeStruct((1, D_FF), F32), jax.ShapeDtypeStruct((1, D_FF), F32)],
        scratch_shapes=[pltpu.VMEM((2 * SUBLANES, CONV_CB), F32)],
        compiler_params=_params(("parallel", "arbitrary")),
    )(h, h, h, h, dact, cw, cw, cb, cb)


def _loss_head(ff, x1, tgt, g, tl):
    l = ff.shape[0]

    def body(ff_ref, x1_ref, tg_ref, g_ref, loss_ref, dx2_ref, dff_ref, dg_ref):
        @pl.when(pl.program_id(0) == 0)
        def _():
            loss_ref[...] = jnp.zeros_like(loss_ref)
            dg_ref[...] = jnp.zeros_like(dg_ref)

        f = ff_ref[...]
        gv = g_ref[...]
        n, _ = _rms(f, gv)
        e = x1_ref[...] + n - tg_ref[...]
        loss_ref[...] += 0.5 * jnp.sum(jnp.mean(e * e, axis=-1, keepdims=True), axis=0, keepdims=True)
        dx2 = e * (1.0 / D_MODEL)
        dx2_ref[...] = dx2
        dff, dg = _rms_bwd(dx2, f, gv)
        dff_ref[...] = dff.astype(BF16)
        dg_ref[...] += dg

    return pl.pallas_call(
        body, name="loss_head", grid=(l // tl,),
        in_specs=[_row(tl, D_MODEL), _row(tl, D_MODEL), _row(tl, D_MODEL), _const((1, D_MODEL))],
        out_specs=[_const((1, LANES)), _row(tl, D_MODEL), _row(tl, D_MODEL), _const((1, D_MODEL))],
        out_shape=[jax.ShapeDtypeStruct((1, LANES), F32), jax.ShapeDtypeStruct((l, D_MODEL), F32),
                   jax.ShapeDtypeStruct((l, D_MODEL), BF16), jax.ShapeDtypeStruct((1, D_MODEL), F32)],
        compiler_params=_params(("arbitrary",)),
    )(ff, x1, tgt, g)


def _ssm_disc(lam_re, lam_im, log_dt, b_re, b_im):
    dt = jnp.exp(log_dt)[:, None]
    mag = jnp.exp(lam_re * dt)
    ang = lam_im * dt
    a_re, a_im = mag * jnp.cos(ang), mag * jnp.sin(ang)
    den = lam_re * lam_re + lam_im * lam_im
    n_re, n_im = a_re - 1.0, a_im
    z_re = (n_re * lam_re + n_im * lam_im) / den
    z_im = (n_im * lam_re - n_re * lam_im) / den
    bb_re = z_re[..., None] * b_re - z_im[..., None] * b_im
    bb_im = z_re[..., None] * b_im + z_im[..., None] * b_re
    return a_re, a_im, bb_re, bb_im


_GPB = SSM_CB // SSM_P


def _embed_b(bb):
    t = bb.transpose(0, 2, 1).reshape(SSM_NB, _GPB, SSM_H, SSM_P)
    return jnp.einsum('mjhp,jk->mjhkp', t, jnp.eye(_GPB, dtype=bb.dtype)).reshape(SSM_NB, SSM_UB, SSM_CB)


def _extract_b(d):
    t = d.reshape(SSM_NB, _GPB, SSM_H, _GPB, SSM_P)
    t = jnp.einsum('mjhkp,jk->mjhp', t, jnp.eye(_GPB, dtype=d.dtype))
    return t.reshape(SSM_G, SSM_H, SSM_P).transpose(0, 2, 1)


def _embed_c(c):
    t = c.transpose(0, 2, 1).reshape(SSM_NB, _GPB, SSM_P, SSM_H)
    return jnp.einsum('mjph,jk->mjpkh', t, jnp.eye(_GPB, dtype=c.dtype)).reshape(SSM_NB, SSM_CB, SSM_UB)


def _extract_c(d):
    t = d.reshape(SSM_NB, _GPB, SSM_P, _GPB, SSM_H)
    t = jnp.einsum('mjpkh,jk->mjph', t, jnp.eye(_GPB, dtype=d.dtype))
    return t.reshape(SSM_G, SSM_P, SSM_H).transpose(0, 2, 1)


def _scan_tables(a_re, a_im, reverse):
    ar = a_re.reshape(1, SSM_CH)
    ai = (-a_im if reverse else a_im).reshape(1, SSM_CH)
    pr, pi = [ar], [ai]
    for _ in range(SUBLANES - 1):
        pr, pi = pr + [pr[-1] * ar - pi[-1] * ai], pi + [pr[-1] * ai + pi[-1] * ar]
    rows = jnp.arange(SUBLANES)[:, None]
    out = []
    for k in (1, 2, 4):
        valid = (rows + k <= SUBLANES - 1) if reverse else (rows >= k)
        out += [jnp.where(valid, pr[k - 1], 0.0), jnp.where(valid, pi[k - 1], 0.0)]
    order = list(range(SUBLANES - 1, -1, -1)) if reverse else list(range(SUBLANES))
    out += [jnp.concatenate([pr[n] for n in order], axis=0), jnp.concatenate([pi[n] for n in order], axis=0)]
    return jnp.stack(out).astype(F32)


def _pad_heads(w, d):
    lead = w.shape[:-1]
    w = w.reshape(lead + (N_HEADS, d))
    w = jnp.pad(w, [(0, 0)] * len(lead) + [(0, 0), (0, HEAD_SLOT - d)])
    return w.reshape(lead + (HP,))


def _unpad_heads(w, d):
    lead = w.shape[:-1]
    return w.reshape(lead + (N_HEADS, HEAD_SLOT))[..., :d].reshape(lead + (N_HEADS * d,))


def _chip_major(w, axis):
    k, n = w.shape
    if axis == 0:
        return w.reshape(N_CHIPS, k // N_CHIPS, n)
    return w.reshape(k, N_CHIPS, n // N_CHIPS).transpose(1, 0, 2)


def _from_chip_major(w, axis):
    if axis == 0:
        return w.reshape(-1, w.shape[2])
    return w.transpose(1, 0, 2).reshape(w.shape[1], -1)


def _pad_w_in(w):
    z = lambda n: jnp.zeros((w.shape[0], n), w.dtype)
    return jnp.concatenate([w[:, :640], z(KR_LANE), w[:, 640:672], z(HEAD_SLOT - KR_LANE - QK_ROPE), w[:, 672:]], axis=1)


def _unpad_w_in(w):
    return jnp.concatenate([w[:, :640], w[:, P_KR + KR_LANE:P_KR + KR_LANE + QK_ROPE], w[:, P_U:]], axis=1)


def _local_step(x, positions, tgt, wts, sp):
    l = x.shape[0]
    tl = min(256, l)
    ta = min(512, l)

    inv_freq = ROPE_THETA ** (-jnp.arange(0, QK_ROPE, 2, dtype=F32) / QK_ROPE)
    ang = positions.astype(F32)[:, None] * inv_freq
    cos, sin = jnp.cos(ang), jnp.sin(ang)
    one = jnp.ones((l, KR_LANE), F32)
    rc = jnp.concatenate([one, cos, cos, jnp.ones((l, HEAD_SLOT - KR_LANE - QK_ROPE), F32)], axis=1)
    rs = jnp.concatenate([0 * one, -sin, sin, jnp.zeros((l, HEAD_SLOT - KR_LANE - QK_ROPE), F32)], axis=1)

    win = _pad_w_in(wts["w_in"])
    wuq = _pad_heads(wts["w_uq"], QK_HEAD)
    wukv = jnp.concatenate([_pad_heads(wts["w_uk"], QK_NOPE), _pad_heads(wts["w_uv"], V_HEAD)], axis=1)
    wba = jnp.pad(wts["w_branch_attn"].reshape(N_HEADS, V_HEAD, D_MODEL),
                  ((0, 0), (0, HEAD_SLOT - V_HEAD), (0, 0))).reshape(HP, D_MODEL)
    wbs, wglu, wout = wts["w_branch_ssm"], wts["w_glu"], wts["w_out"]

    disc_in = (sp["ssm_lambda_re"], sp["ssm_lambda_im"], sp["ssm_log_dt"], sp["ssm_b_re"], sp["ssm_b_im"])
    (a_re, a_im, bb_re, bb_im), disc_vjp = jax.vjp(_ssm_disc, *disc_in)
    bre, bim = _embed_b(bb_re).astype(BF16), _embed_b(bb_im).astype(BF16)
    cre, cim = _embed_c(sp["ssm_c_re"]).astype(BF16), _embed_c(sp["ssm_c_im"]).astype(BF16)
    dvec = sp["ssm_d"].reshape(1, SSM_W)
    tab_f = _scan_tables(a_re, a_im, False)
    tab_r = _scan_tables(a_re, a_im, True)

    g1, gq, gkv = sp["mix_norm_pre"], sp["q_norm"], sp["kv_norm"]
    gpost, gpre, gfin = sp["mix_norm_post"], sp["ffn_norm_pre"], sp["ffn_norm_post"]
    bgate, bglu, convb = sp["b_gate"], sp["b_glu"], sp["conv_b"]

    hn, cq, ckv, q, k, v, u, gl = _proj_fwd(x, g1, win, gq, wuq, gkv, wukv, rc, rs, bgate, tl)
    attn, lse = _attn_fwd(q, k, v, ta)
    y1, sre, sim = _ssm_fwd(u, bre, bim, cre, cim, dvec, tab_f, ta)
    a, sm, merged, z, x1, hn2, y3 = _merge_fwd(x, gl, attn, y1, wba, wbs, wglu, bglu, wout, gpost, gpre, tl)
    wup, wdown, convw = wts["ffn"](x1)
    h = _mm(hn2, wup, "ffn_up")
    act = _conv_fwd(h, convw, convb, ta)
    ff = _mm(act, wdown, "ffn_down")
    loss, dx2, dff, dgfin = _loss_head(ff, x1, tgt, gfin, tl)

    dact = _mm(dff, wdown, "ffn_down_dx", out_dtype=BF16, bt=True)
    d_wdown = _mm_tn(act, dff, "ffn_down_dw", tk_cap=D_FF // 2)
    dh, dwg, dwv, dbg, dbv = _conv_bwd(h, dact, convw, convb, ta)
    d_convw = jnp.concatenate([dwg, dwv], axis=1)
    d_convb = jnp.concatenate([dbg, dbv], axis=1)
    dhn2a = _mm(dh, wup, "ffn_up_dx_gate", bt=True, b_col0=0, a_lead=0)
    dhn2b = _mm(dh, wup, "ffn_up_dx_val", bt=True, b_col0=1, a_lead=1)
    d_wup = _mm_tn(hn2, dh, "ffn_up_dw", chips=True)
    behind = wts["ffn_grads"](d_wup, _chip_major(d_wdown, 0))
    (dx1, dz, dbra, dbrs, dgl, dattn, dy1, dt, y2, dgpre, dgpost, dbgate, dbglu) = _merge_bwd(
        dhn2a, dhn2b, x1, dx2, z, gl, a, sm, y1, wba, wbs, wglu, bglu, wout, gpost, gpre + behind, tl)
    d_wout = _mm_tn(merged, dz, "w_out_dw")
    d_wba = _mm_tn(attn, dbra, "w_branch_attn_dw", chips=True)
    d_wbs = _mm_tn(y3, dbrs, "w_branch_ssm_dw", chips=True)
    d_wglu = _mm_tn(y2, dt, "w_glu_dw")
    dq, dk, dv = _attn_bwd(q, k, v, dattn, lse, _attn_delta(attn, dattn, min(2048, l)), ta)
    du, dbre, dbim, dcre, dcim, dare, daim, dd = _ssm_bwd(dy1, u, sre, sim, bre, bim, cre, cim, dvec, tab_r, ta)
    gx, dql, qn, ckvn, dproj, dg1, dgq, dgkv = _proj_bwd(
        x, dx1, cq, ckv, dq, dk, dv, du, dgl, g1, win, gq, wuq, gkv, wukv, rc, rs, tl)
    d_win = _mm_tn(hn, dproj, "w_in_dw")
    d_wuq = _mm_tn(qn, dql, "w_uq_dw")
    d_wuk = _mm_tn(ckvn, dk, "w_uk_dw")
    d_wuv = _mm_tn(ckvn, dv, "w_uv_dw")

    d_lre, d_lim, d_ldt, d_bre, d_bim = disc_vjp((dare.reshape(SSM_G, SSM_P), daim.reshape(SSM_G, SSM_P),
                                                  _extract_b(dbre), _extract_b(dbim)))
    ncol = D_MODEL // N_CHIPS
    big = {
        "w_in": _chip_major(_unpad_w_in(d_win), 1),
        "w_uq": _chip_major(_unpad_heads(d_wuq, QK_HEAD), 1),
        "w_uk": _chip_major(_unpad_heads(d_wuk, QK_NOPE), 1),
        "w_uv": _chip_major(_unpad_heads(d_wuv, V_HEAD), 1),
        "w_glu": _chip_major(d_wglu, 0),
        "w_branch_attn": d_wba.reshape(N_CHIPS, N_HEADS, HEAD_SLOT, ncol)[:, :, :V_HEAD].reshape(
            N_CHIPS, N_HEADS * V_HEAD, ncol),
        "w_branch_ssm": d_wbs,
        "w_out": _chip_major(d_wout, 0),
    }
    small = {
        "conv_w": d_convw,
        "mix_norm_pre": dg1, "q_norm": dgq, "kv_norm": dgkv,
        "ssm_lambda_re": d_lre, "ssm_lambda_im": d_lim, "ssm_log_dt": d_ldt,
        "ssm_b_re": d_bre, "ssm_b_im": d_bim,
        "ssm_c_re": _extract_c(dcre), "ssm_c_im": _extract_c(dcim),
        "ssm_d": dd.reshape(SSM_G, SSM_H), "b_glu": dbglu, "b_gate": dbgate,
        "mix_norm_post": dgpost, "ffn_norm_pre": dgpre, "conv_b": d_convb, "ffn_norm_post": dgfin,
    }
    return loss[0, 0], gx, big, small


_ANY = pl.BlockSpec(memory_space=pl.ANY)


ROW_TILE = 16


def _place():
    x, y, c = lax.axis_index("x"), lax.axis_index("y"), lax.axis_index("c")
    return x, y, c, 2 * x + y, [(1 - x, y), (x, 1 - y), (1 - x, 1 - y)]


def _half(rows, which):
    hr = rows // 2
    return pl.ds(pl.multiple_of(which * hr, ROW_TILE), hr)


def _remote(src, dst, send_sems, recv_sems, n, dev):
    return pltpu.make_async_remote_copy(src_ref=src, dst_ref=dst, send_sem=send_sems.at[n], recv_sem=recv_sems.at[n],
                                        device_id=dev, device_id_type=MESH)


def _gather_big(shards):
    nw = len(shards)
    rows = [s.shape[0] for s in shards]

    def body(*refs):
        ins, outs = refs[:nw], refs[nw:2 * nw]
        ici_send, ici_recv, d2d_send, d2d_recv = refs[2 * nw:]
        x, y, c, me, peers = _place()
        sent = []
        for i in range(nw):
            for p, (px, py) in enumerate(peers):
                cp = _remote(ins[i].at[_half(rows[i], c)], outs[i].at[me, _half(rows[i], c)], ici_send, ici_recv,
                             3 * i + p, (px, py, c))
                cp.start()
                sent.append(cp)
        for p, (px, py) in enumerate(peers):
            for i in range(nw):
                blk = outs[i].at[2 * px + py, _half(rows[i], c)]
                _remote(blk, blk, ici_send, ici_recv, 3 * i + p, (px, py, c)).wait_recv()
                cp = _remote(blk, blk, d2d_send, d2d_recv, 3 * i + p, (x, y, 1 - c))
                cp.start()
                sent.append(cp)
        for p, (px, py) in enumerate(peers):
            for i in range(nw):
                blk = outs[i].at[2 * px + py, _half(rows[i], 1 - c)]
                _remote(blk, blk, d2d_send, d2d_recv, 3 * i + p, (x, y, 1 - c)).wait_recv()
        for cp in sent:
            cp.wait_send()

    dma = pltpu.SemaphoreType.DMA
    return pl.pallas_call(
        body, name="gather_weights", in_specs=[_ANY] * nw, out_specs=[_ANY] * nw,
        out_shape=[jax.ShapeDtypeStruct((N_CHIPS,) + s.shape, s.dtype) for s in shards],
        scratch_shapes=[dma((3 * nw,)), dma((3 * nw,)), dma((3 * nw,)), dma((3 * nw,))],
    )(*shards)


_HBM = pl.BlockSpec(memory_space=pltpu.HBM)
_SEM = pl.BlockSpec(memory_space=pltpu.SEMAPHORE)
_DATAFLOW = pltpu.SideEffectType.DATAFLOW_SIDE_EFFECTING


def _exchange_start(shards, name, scatter):
    nw = len(shards)
    lands = [lax.empty(s.shape if scatter else (N_CHIPS,) + s.shape, s.dtype) for s in shards]

    def body(*refs):
        ins, zones = refs[:nw], refs[nw:2 * nw]
        send_sems, recv_sems, token = refs[2 * nw], refs[2 * nw + 1], refs[-1]
        x, y, c, me, peers = _place()
        for i in range(nw):
            for p, (px, py) in enumerate(peers):
                src = ins[i].at[2 * px + py] if scatter else ins[i]
                _remote(src, zones[i].at[me], send_sems, recv_sems, 3 * i + p, (px, py, c)).start()
        token[...] = jnp.zeros_like(token)

    thru = [pltpu.HBM(a.shape, a.dtype) for a in list(shards) + lands]
    dma = pltpu.SemaphoreType.DMA
    outs = pl.pallas_call(
        body, name=name,
        out_shape=(dma((3 * nw,)), dma((3 * nw,)), *thru, jax.ShapeDtypeStruct((SUBLANES, LANES), F32)),
        in_specs=[_HBM] * (2 * nw),
        out_specs=(_SEM, _SEM, *([_HBM] * (2 * nw)), pl.BlockSpec(memory_space=pltpu.VMEM)),
        input_output_aliases={i: 2 + i for i in range(2 * nw)},
        compiler_params=pltpu.CompilerParams(has_side_effects=_DATAFLOW),
    )(*[pltpu.with_memory_space_constraint(a, pltpu.HBM) for a in list(shards) + lands])
    return outs[0], outs[1], list(outs[2:2 + nw]), list(outs[2 + nw:2 + 2 * nw]), outs[-1]


def _exchange_wait(send_sems, recv_sems, shards, lands, after, name, scatter):
    nw = len(shards)

    def body(*refs):
        ins, zones = refs[:nw], refs[nw:2 * nw]
        send_sems, recv_sems = refs[2 * nw], refs[2 * nw + 1]
        x, y, c, me, peers = _place()
        for i in range(nw):
            for p, (px, py) in enumerate(peers):
                src = ins[i].at[2 * px + py] if scatter else ins[i]
                cp = _remote(src, zones[i].at[2 * px + py], send_sems, recv_sems, 3 * i + p, (px, py, c))
                cp.wait_send()
                cp.wait_recv()

    both = list(shards) + list(lands)
    outs = pl.pallas_call(
        body, name=name,
        out_shape=tuple(pltpu.HBM(a.shape, a.dtype) for a in both),
        in_specs=(*([_HBM] * (2 * nw)), _SEM, _SEM, _ANY), out_specs=[_HBM] * (2 * nw),
        input_output_aliases={i: i for i in range(2 * nw)},
        compiler_params=pltpu.CompilerParams(has_side_effects=_DATAFLOW),
    )(*both, send_sems, recv_sems, after)
    return list(outs[:nw]), list(outs[nw:])


def _reduce_to_sibling(grads, name):
    nw = len(grads)

    def body(*refs):
        ins, outs = refs[:nw], refs[nw:2 * nw]
        send_sems, recv_sems = refs[2 * nw:]
        x, y, c, _, _ = _place()
        sent = []
        for i in range(nw):
            cp = _remote(ins[i].at[pl.ds(0, N_CHIPS), _half(grads[i].shape[1], 1 - c)], outs[i], send_sems, recv_sems,
                         i, (x, y, 1 - c))
            cp.start()
            sent.append(cp)
        for cp in sent:
            cp.wait()

    dma = pltpu.SemaphoreType.DMA
    return pl.pallas_call(
        body, name=name, in_specs=[_ANY] * nw, out_specs=[_ANY] * nw,
        out_shape=[jax.ShapeDtypeStruct((N_CHIPS, g.shape[1] // 2, g.shape[2]), g.dtype) for g in grads],
        scratch_shapes=[dma((nw,)), dma((nw,))],
    )(*grads)


def _reduce_between_chips(pairs):
    nw = len(pairs)

    def body(*refs):
        ins, outs = refs[:nw], refs[nw:2 * nw]
        send_sems, recv_sems = refs[2 * nw:]
        x, y, c, me, peers = _place()
        sent = []
        for i in range(nw):
            for p, (px, py) in enumerate(peers):
                cp = _remote(ins[i].at[2 * px + py], outs[i].at[me], send_sems, recv_sems, 3 * i + p, (px, py, c))
                cp.start()
                sent.append(cp)
        for i in range(nw):
            for p, (px, py) in enumerate(peers):
                blk = outs[i].at[2 * px + py]
                _remote(blk, blk, send_sems, recv_sems, 3 * i + p, (px, py, c)).wait_recv()
        for cp in sent:
            cp.wait_send()

    dma = pltpu.SemaphoreType.DMA
    return pl.pallas_call(
        body, name="reduce_grads_ici", in_specs=[_ANY] * nw, out_specs=[_ANY] * nw,
        out_shape=[jax.ShapeDtypeStruct(p.shape, p.dtype) for p in pairs],
        scratch_shapes=[dma((3 * nw,)), dma((3 * nw,))],
    )(*pairs)


def _reduce_back(totals):
    nw = len(totals)

    def body(*refs):
        outs = refs[nw:2 * nw]
        send_sems, recv_sems = refs[2 * nw:]
        x, y, c, _, _ = _place()
        sent = []
        for i in range(nw):
            blk = outs[i].at[_half(totals[i].shape[0], c)]
            cp = _remote(blk, blk, send_sems, recv_sems, i, (x, y, 1 - c))
            cp.start()
            sent.append(cp)
        for i in range(nw):
            blk = outs[i].at[_half(totals[i].shape[0], 1 - c)]
            _remote(blk, blk, send_sems, recv_sems, i, (x, y, 1 - c)).wait_recv()
        for cp in sent:
            cp.wait_send()

    dma = pltpu.SemaphoreType.DMA
    return pl.pallas_call(
        body, name="reduce_grads_back", in_specs=[_ANY] * nw, out_specs=[_ANY] * nw,
        out_shape=[jax.ShapeDtypeStruct(t.shape, t.dtype) for t in totals],
        input_output_aliases={i: i for i in range(nw)},
        scratch_shapes=[dma((nw,)), dma((nw,))],
    )(*totals)


def _all_reduce_small(v, name):
    rows, w = v.shape
    hr = rows // 2
    assert hr % SUBLANES == 0

    def body(v_ref, out_ref, sib_ref, half_ref, chips_ref, send_sems, recv_sems):
        x, y, c, me, peers = _place()
        sibling = (x, y, 1 - c)
        mine = pl.ds(pl.multiple_of(c * hr, SUBLANES), hr)
        other = pl.ds(pl.multiple_of((1 - c) * hr, SUBLANES), hr)
        cp = _remote(v_ref, sib_ref, send_sems, recv_sems, 0, sibling)
        cp.start()
        cp.wait()
        half_ref[...] = v_ref[mine, :] + sib_ref[mine, :]
        sent = []
        for p, (px, py) in enumerate(peers):
            cp = _remote(half_ref, chips_ref.at[me], send_sems, recv_sems, 1 + p, (px, py, c))
            cp.start()
            sent.append(cp)
        chips_ref[me] = half_ref[...]
        for p, (px, py) in enumerate(peers):
            _remote(half_ref, chips_ref.at[2 * px + py], send_sems, recv_sems, 1 + p, (px, py, c)).wait_recv()
        for cp in sent:
            cp.wait_send()
        out_ref[mine, :] = ((chips_ref[0] + chips_ref[1]) + chips_ref[2]) + chips_ref[3]
        cp = _remote(out_ref.at[mine], out_ref.at[mine], send_sems, recv_sems, 4, sibling)
        cp.start()
        _remote(out_ref.at[other], out_ref.at[other], send_sems, recv_sems, 4, sibling).wait_recv()
        cp.wait_send()

    vm = pl.BlockSpec(memory_space=pltpu.VMEM)
    return pl.pallas_call(
        body, name=name, in_specs=[vm], out_specs=vm,
        out_shape=jax.ShapeDtypeStruct((rows, w), F32),
        scratch_shapes=[pltpu.VMEM((rows, w), F32), pltpu.VMEM((hr, w), F32), pltpu.VMEM((N_CHIPS, hr, w), F32),
                        pltpu.SemaphoreType.DMA((5,)), pltpu.SemaphoreType.DMA((5,))],
        compiler_params=pltpu.CompilerParams(vmem_limit_bytes=VMEM_LIMIT),
    )(v)


ELEMENTWISE_BLOCK = 256 * 1024


def _rows_tile(rows, cols):
    best = None
    for t in range(SUBLANES, rows + 1, SUBLANES):
        if rows % t == 0 and t * cols <= ELEMENTWISE_BLOCK:
            best = t
    return rows if best is None else best


def _add_pair(g, t, core, name):
    nb, n, w = t.shape
    tr = _rows_tile(n, w)
    steps = n // tr

    def body(core_ref, g_ref, t_ref, o_ref):
        o_ref[...] = (g_ref[...] + t_ref[...]).astype(BF16)

    spec = pl.BlockSpec((1, tr, w), lambda j, i, core_ref: (j, i, 0))
    return pl.pallas_call(
        body, name=name,
        grid_spec=pltpu.PrefetchScalarGridSpec(
            num_scalar_prefetch=1, grid=(nb, steps),
            in_specs=[pl.BlockSpec((1, tr, w), lambda j, i, core_ref: (j, core_ref[0] * steps + i, 0)), spec],
            out_specs=spec),
        out_shape=jax.ShapeDtypeStruct(t.shape, BF16),
        compiler_params=_params(("parallel", "parallel")))(core, g, t)


def _add_chips(landed, pairs, place, name):
    nb, n, w = landed.shape
    tr = _rows_tile(n, w)
    steps = n // tr

    def body(place_ref, r_ref, own_ref, o_ref):
        me = place_ref[0]
        acc = None
        for k in range(nb):
            blk = jnp.where(me == k, own_ref[0], r_ref[k]).astype(F32)
            acc = blk if acc is None else acc + blk
        o_ref[...] = acc

    return pl.pallas_call(
        body, name=name,
        grid_spec=pltpu.PrefetchScalarGridSpec(
            num_scalar_prefetch=1, grid=(steps,),
            in_specs=[pl.BlockSpec((nb, tr, w), lambda i, place_ref: (0, i, 0)),
                      pl.BlockSpec((1, tr, w), lambda i, place_ref: (place_ref[0], i, 0))],
            out_specs=pl.BlockSpec((tr, w), lambda i, place_ref: (place_ref[1] * steps + i, 0))),
        out_shape=jax.ShapeDtypeStruct((2 * n, w), F32),
        compiler_params=_params(("parallel",)))(place, landed, pairs)


def _adamw(w, g, m, v, name):
    rows, wd = w.shape
    tr = _rows_tile(rows, wd)
    c1 = 1.0 - ADAM_B1 ** ADAM_STEP
    c2 = 1.0 - ADAM_B2 ** ADAM_STEP

    def body(w_ref, g_ref, m_ref, v_ref, d_ref, mo_ref, vo_ref):
        gv = g_ref[...]
        m2 = ADAM_B1 * m_ref[...] + (1.0 - ADAM_B1) * gv
        v2 = ADAM_B2 * v_ref[...] + (1.0 - ADAM_B2) * (gv * gv)
        mo_ref[...] = m2
        vo_ref[...] = v2
        d_ref[...] = -ADAM_LR * ((m2 / c1) / (jnp.sqrt(v2 / c2) + ADAM_EPS) + ADAM_WD * w_ref[...])

    spec = pl.BlockSpec((tr, wd), lambda i: (i, 0))
    shp = jax.ShapeDtypeStruct((rows, wd), F32)
    return pl.pallas_call(body, name=name, grid=(rows // tr,), in_specs=[spec] * 4, out_specs=[spec] * 3,
                          out_shape=[shp] * 3, compiler_params=_params(("parallel",)))(w, g, m, v)


BIG = [("w_in", (1024, 3232), 1), ("w_uq", (384, 768), 1), ("w_uk", (256, 512), 1), ("w_uv", (256, 512), 1),
       ("w_glu", (512, 512), 0), ("w_branch_attn", (512, 1024), 1), ("w_branch_ssm", (512, 1024), 1),
       ("w_out", (1024, 1024), 0), ("w_up", (1024, 5632), 1), ("conv_w", (3, 5632), 1), ("w_down", (2816, 1024), 0)]
SMALL = [("mix_norm_pre", (1024,)), ("q_norm", (384,)), ("kv_norm", (256,)), ("ssm_lambda_re", (32, 64)),
         ("ssm_lambda_im", (32, 64)), ("ssm_log_dt", (32,)), ("ssm_b_re", (32, 64, 16)), ("ssm_b_im", (32, 64, 16)),
         ("ssm_c_re", (32, 16, 64)), ("ssm_c_im", (32, 16, 64)), ("ssm_d", (32, 16)), ("b_glu", (512,)),
         ("b_gate", (2048,)), ("mix_norm_post", (1024,)), ("ffn_norm_pre", (1024,)), ("conv_b", (5632,)),
         ("ffn_norm_post", (1024,))]
MATMUL_W = [b for b in BIG if b[0] != "conv_w"]
FFN_W = ("w_up", "w_down")
CONV_W_SHAPE = (3, 2 * D_FF)
CONV_W_SHARD = (3, 2 * D_FF // N_CHIPS)
SMALL_SUM = [("loss", (1,))] + SMALL + [("conv_w", CONV_W_SHAPE)]
SMALL_ADAM = SMALL + [("conv_w", CONV_W_SHARD)]


def _pack_flat(layout, vals):
    flat = jnp.concatenate([vals[n].astype(F32).reshape(-1) for n, _ in layout])
    rows = -(-(-(-flat.shape[0] // FLAT_W)) // (2 * SUBLANES)) * 2 * SUBLANES
    return jnp.pad(flat, (0, rows * FLAT_W - flat.shape[0])).reshape(rows, FLAT_W)


def _unpack_flat(layout, flat):
    flat = flat.reshape(-1)
    out = {}
    o = 0
    for name, shape in layout:
        n = math.prod(shape)
        out[name] = flat[o:o + n].reshape(shape)
        o += n
    return out


_ARG_NAMES = ["x", "positions"] + [n for n in (
    "mix_norm_pre", "w_in", "q_norm", "w_uq", "kv_norm", "w_uk", "w_uv", "ssm_lambda_re", "ssm_lambda_im", "ssm_log_dt",
    "ssm_b_re", "ssm_b_im", "ssm_c_re", "ssm_c_im", "ssm_d", "w_glu", "b_glu", "w_branch_attn", "w_branch_ssm",
    "b_gate", "w_out", "mix_norm_post", "ffn_norm_pre", "w_up", "conv_w", "conv_b", "w_down", "ffn_norm_post")]
_WEIGHTS = _ARG_NAMES[2:]


def _gather_weights(w):
    early = [b for b in MATMUL_W if b[0] not in FFN_W]
    late = [b for b in BIG if b[0] in FFN_W or b[0] == "conv_w"]
    own = (jnp.arange(N_CHIPS) == 2 * lax.axis_index("x") + lax.axis_index("y"))[:, None, None]

    def whole(layout, mine, gathered):
        return {name: _from_chip_major(jnp.where(own, s[None], g), axis)
                for (name, _, axis), s, g in zip(layout, mine, gathered)}

    mine = [w[name].astype(BF16) for name, _, _ in early]
    full = whole(early, mine, _gather_big(mine))
    mine_late = [w[name].astype(F32 if name == "conv_w" else BF16) for name, _, _ in late]
    _, mine_late = lax.optimization_barrier((full["w_in"], mine_late))
    send_sems, recv_sems, shards_thru, lands_thru, token = _exchange_start(mine_late, "gather_ffn_start", scatter=False)

    def ffn_weights(after):
        shards, lands = _exchange_wait(send_sems, recv_sems, shards_thru, lands_thru, after, "gather_ffn_wait",
                                       scatter=False)
        got = whole(late, shards, lands)
        return got["w_up"], got["w_down"], got["conv_w"]

    full["ffn"] = ffn_weights
    full["token"] = token[0, 0]
    return full


def _pair_sums(names, grads, tag):
    core = lax.axis_index("c").astype(jnp.int32).reshape(1)
    theirs = _reduce_to_sibling(grads, "reduce_grads_d2d" + tag)
    return [_add_pair(g, t, core, "reduce_pair_" + n) for n, g, t in zip(names, grads, theirs)]


def _start_ffn_reduce(d_wup, d_wdown):
    pairs = _pair_sums(list(FFN_W), [d_wup, d_wdown], "_ffn")
    send_sems, recv_sems, pairs_thru, lands_thru, token = _exchange_start(pairs, "reduce_ffn_start", scatter=True)
    return (send_sems, recv_sems, pairs_thru, lands_thru), token[0, 0]


def _reduce_grads(gbig, ffn_pending, loss, gsmall):
    core = lax.axis_index("c").astype(jnp.int32).reshape(1)
    chip = (2 * lax.axis_index("x") + lax.axis_index("y")).astype(jnp.int32).reshape(1)
    place = jnp.concatenate([chip, core])
    names = [name for name, _, _ in MATMUL_W if name not in FFN_W]
    pairs = _pair_sums(names, [gbig[n] for n in names], "")
    landed = _reduce_between_chips(pairs)
    ffn_pairs, ffn_landed = _exchange_wait(*ffn_pending, gbig["w_in"], "reduce_ffn_wait", scatter=True)
    names, pairs, landed = names + list(FFN_W), pairs + ffn_pairs, list(landed) + ffn_landed
    totals = [_add_chips(r, p, place, "reduce_chips_" + n) for n, r, p in zip(names, landed, pairs)]
    g_red = dict(zip(names, _reduce_back(totals)))

    vals = dict(gsmall)
    vals["loss"] = loss
    small_red = _unpack_flat(SMALL_SUM, _all_reduce_small(_pack_flat(SMALL_SUM, vals), "reduce_small"))
    return g_red, small_red


def _step(args):
    x = args["x"][0]
    positions = args["positions"][0]
    tgt = args["loss_target"][0]
    w = {n: args[n][0] for n in _WEIGHTS}
    m = {n: args["m_" + n][0] for n in _WEIGHTS}
    v = {n: args["v_" + n][0] for n in _WEIGHTS}

    full = _gather_weights(w)
    sp = {n: w[n].reshape(s) for n, s in SMALL}
    for n in ("mix_norm_pre", "q_norm", "kv_norm", "b_glu", "b_gate", "mix_norm_post", "ffn_norm_pre", "conv_b",
              "ffn_norm_post"):
        sp[n] = sp[n].reshape(1, -1)
    sp["mix_norm_pre"] = sp["mix_norm_pre"] + full.pop("token")
    pending = []

    def ffn_grads(d_wup, d_wdown):
        state, token = _start_ffn_reduce(d_wup, d_wdown)
        pending.append(state)
        return token

    full["ffn_grads"] = ffn_grads
    loss, gx, gbig, gsmall = _local_step(x, positions, tgt, full, sp)
    g_red, small_red = _reduce_grads(gbig, pending[0], loss, gsmall)

    chip = 2 * lax.axis_index("x") + lax.axis_index("y")
    grads = dict(small_red)
    grads["conv_w"] = lax.dynamic_slice_in_dim(small_red["conv_w"], chip * CONV_W_SHARD[1], CONV_W_SHARD[1], axis=1)
    grads.update(g_red)

    outs = {"grad_" + n: grads[n] for n in _WEIGHTS}
    for name, _, _ in MATMUL_W:
        d, m2, v2 = _adamw(w[name], grads[name], m[name], v[name], "adamw_" + name)
        outs["delta_" + name], outs["new_m_" + name], outs["new_v_" + name] = d, m2, v2
    d_sm, m_sm, v_sm = _adamw(_pack_flat(SMALL_ADAM, w), _pack_flat(SMALL_ADAM, grads), _pack_flat(SMALL_ADAM, m),
                              _pack_flat(SMALL_ADAM, v), "adamw_small")
    for prefix, flat in (("delta_", d_sm), ("new_m_", m_sm), ("new_v_", v_sm)):
        for n, val in _unpack_flat(SMALL_ADAM, flat).items():
            outs[prefix + n] = val
    outs = {n: val.reshape(args[n.split("_", 1)[1] if not n.startswith("new_") else n[6:]].shape)
            for n, val in outs.items()}
    res = [small_red["loss"][0], gx[None]]
    for prefix in ("grad_", "delta_", "new_m_", "new_v_"):
        res += [outs[prefix + n] for n in _WEIGHTS]
    return tuple(res)


def kernel(x, positions, mix_norm_pre, w_in, q_norm, w_uq, kv_norm, w_uk, w_uv, ssm_lambda_re, ssm_lambda_im, ssm_log_dt, ssm_b_re, ssm_b_im, ssm_c_re, ssm_c_im, ssm_d, w_glu, b_glu, w_branch_attn, w_branch_ssm, b_gate, w_out, mix_norm_post, ffn_norm_pre, w_up, conv_w, conv_b, w_down, ffn_norm_post, loss_target, m_mix_norm_pre, m_w_in, m_q_norm, m_w_uq, m_kv_norm, m_w_uk, m_w_uv, m_ssm_lambda_re, m_ssm_lambda_im, m_ssm_log_dt, m_ssm_b_re, m_ssm_b_im, m_ssm_c_re, m_ssm_c_im, m_ssm_d, m_w_glu, m_b_glu, m_w_branch_attn, m_w_branch_ssm, m_b_gate, m_w_out, m_mix_norm_post, m_ffn_norm_pre, m_w_up, m_conv_w, m_conv_b, m_w_down, m_ffn_norm_post, v_mix_norm_pre, v_w_in, v_q_norm, v_w_uq, v_kv_norm, v_w_uk, v_w_uv, v_ssm_lambda_re, v_ssm_lambda_im, v_ssm_log_dt, v_ssm_b_re, v_ssm_b_im, v_ssm_c_re, v_ssm_c_im, v_ssm_d, v_w_glu, v_b_glu, v_w_branch_attn, v_w_branch_ssm, v_b_gate, v_w_out, v_mix_norm_post, v_ffn_norm_pre, v_w_up, v_conv_w, v_conv_b, v_w_down, v_ffn_norm_post):
    given = dict(locals())
    return _step(given)
```

```python
import math

import jax
import jax.numpy as jnp
from jax import lax
from jax.experimental import pallas as pl
from jax.experimental.pallas import tpu as pltpu

F32 = jnp.float32
BF16 = jnp.bfloat16
MESH = pl.DeviceIdType.MESH

D_MODEL = 1024
N_HEADS = 8
QK_NOPE = 64
QK_ROPE = 32
QK_HEAD = QK_NOPE + QK_ROPE
V_HEAD = 64
Q_RANK = 384
KV_RANK = 256
ROPE_THETA = 10000.0
SSM_W = 512
SSM_H = 16
SSM_G = 32
SSM_P = 64
SSM_CH = SSM_G * SSM_P
D_FF = 2816
EPS = 1e-6
ADAM_LR = 0.001
ADAM_B1 = 0.9
ADAM_B2 = 0.999
ADAM_EPS = 1e-08
ADAM_WD = 0.01
ADAM_STEP = 10

LANES = 128
SUBLANES = 8
VMEM_LIMIT = 56 * 1024 * 1024

HEAD_SLOT = LANES
HP = N_HEADS * HEAD_SLOT
P_CQ, P_CKV, P_KR, P_U, P_GL, P_END = 0, 384, 640, 768, 1280, 3328
KR_LANE = 64

FLAT_W = 1024
N_CHIPS = 4


def _tile(n, cap):
    if n <= cap:
        return n
    best = None
    for t in range(LANES, cap + 1, LANES):
        if n % t == 0:
            best = t
    assert best is not None, (n, cap)
    return best


def _params(sem):
    return pltpu.CompilerParams(dimension_semantics=sem, vmem_limit_bytes=VMEM_LIMIT)


def _dot(a, b):
    return jnp.dot(a, b, preferred_element_type=F32)


def _dot_nt(a, b):
    return lax.dot_general(a, b, (((1,), (1,)), ((), ())), preferred_element_type=F32)


def _dot_tn(a, b):
    return lax.dot_general(a, b, (((0,), (0,)), ((), ())), preferred_element_type=F32)


def _rms(x, g):
    r = lax.rsqrt(jnp.mean(x * x, axis=-1, keepdims=True) + EPS)
    return x * r * g, r


def _rms_bwd(dy, x, g):
    r = lax.rsqrt(jnp.mean(x * x, axis=-1, keepdims=True) + EPS)
    dyg = dy * g
    dx = r * dyg - x * (r * r * r) * jnp.mean(dyg * x, axis=-1, keepdims=True)
    dg = jnp.sum(dy * x * r, axis=0, keepdims=True)
    return dx, dg


_GELU_K0 = math.sqrt(2.0 / math.pi)
_GELU_K1 = 0.044715


def _gelu(x):
    th = jnp.tanh(_GELU_K0 * (x + _GELU_K1 * x * x * x))
    return 0.5 * x * (1.0 + th)


def _gelu_grad(x):
    th = jnp.tanh(_GELU_K0 * (x + _GELU_K1 * x * x * x))
    return 0.5 * (1.0 + th) + 0.5 * x * (1.0 - th * th) * _GELU_K0 * (1.0 + 3.0 * _GELU_K1 * x * x)


def _sigmoid(x):
    return 1.0 / (1.0 + jnp.exp(-x))


def _rope(q, c, s):
    n = q.shape[1]
    lane = lax.broadcasted_iota(jnp.int32, q.shape, 1) % HEAD_SLOT
    sw = jnp.where(lane < KR_LANE + QK_ROPE // 2, pltpu.roll(q, n - QK_ROPE // 2, 1), pltpu.roll(q, QK_ROPE // 2, 1))
    return q * c + sw * s


def _rope_bwd(dy, c, s):
    n = dy.shape[1]
    t = dy * s
    lane = lax.broadcasted_iota(jnp.int32, dy.shape, 1) % HEAD_SLOT
    sw = jnp.where(lane < KR_LANE + QK_ROPE // 2, pltpu.roll(t, n - QK_ROPE // 2, 1), pltpu.roll(t, QK_ROPE // 2, 1))
    rope_lane = jnp.logical_and(lane >= KR_LANE, lane < KR_LANE + QK_ROPE)
    return dy * c + jnp.where(rope_lane, sw, 0.0)


def _shift_down(x, k, halo):
    xs = pltpu.roll(x, k, 0)
    hs = pltpu.roll(halo, k, 0)
    rows = lax.broadcasted_iota(jnp.int32, halo.shape, 0)
    top = jnp.where(rows < k, hs, xs[0:SUBLANES])
    return jnp.concatenate([top, xs[SUBLANES:]], axis=0)


def _shift_up(x, k, halo):
    t = x.shape[0]
    xs = pltpu.roll(x, t - k, 0)
    hs = pltpu.roll(halo, SUBLANES - k, 0)
    rows = lax.broadcasted_iota(jnp.int32, halo.shape, 0)
    bot = jnp.where(rows >= SUBLANES - k, hs, xs[t - SUBLANES:])
    return jnp.concatenate([xs[:t - SUBLANES], bot], axis=0)


def _mm(a, b, name, out_dtype=F32, bt=False, b_col0=0, n=None, tm_cap=1024, tn_cap=1408, a_lead=None):
    m, k = a.shape[-2:]
    if bt:
        n_full = b.shape[0]
        n = n_full
    else:
        n = b.shape[1] if n is None else n
    tm = min(tm_cap, m)
    tn = _tile(n, tn_cap)

    def body(a_ref, b_ref, o_ref):
        if bt:
            o_ref[...] = _dot_nt(a_ref[...], b_ref[...]).astype(out_dtype)
        else:
            o_ref[...] = _dot(a_ref[...], b_ref[...]).astype(out_dtype)

    if bt:
        b_spec = pl.BlockSpec((tn, k), lambda j, i: (j, b_col0))
    else:
        off = b_col0 * (n // tn)
        b_spec = pl.BlockSpec((k, tn), lambda j, i: (0, off + j))
    if a_lead is None:
        a_spec = pl.BlockSpec((tm, k), lambda j, i: (i, 0))
    else:
        a_spec = pl.BlockSpec((None, tm, k), lambda j, i: (a_lead, i, 0))
    return pl.pallas_call(
        body, name=name, grid=(n // tn, m // tm),
        in_specs=[a_spec, b_spec],
        out_specs=pl.BlockSpec((tm, tn), lambda j, i: (i, j)),
        out_shape=jax.ShapeDtypeStruct((m, n), out_dtype),
        compiler_params=_params(("parallel", "parallel")),
    )(a, b)


def _mm_tn(a, b, name, tk_cap=1024, tn_cap=1664, tl_cap=1024, chips=False):
    l, k = a.shape
    tk = _tile(k, tk_cap)
    tl = min(tl_cap, l)

    def body(a_ref, b_ref, o_ref):
        @pl.when(pl.program_id(2) == 0)
        def _():
            o_ref[...] = jnp.zeros_like(o_ref)

        o_ref[...] += _dot_tn(a_ref[...], b_ref[...])

    if chips:
        n = b.shape[-1] * (b.shape[0] if b.ndim == 3 else 1)
        tn = n // N_CHIPS
        assert tn % LANES == 0
        if b.ndim == 3:
            per = N_CHIPS // b.shape[0]
            b_spec = pl.BlockSpec((None, tl, tn), lambda i, j, r: (j // per, r, j % per))
        else:
            b_spec = pl.BlockSpec((tl, tn), lambda i, j, r: (r, j))
        out_spec = pl.BlockSpec((None, tk, tn), lambda i, j, r: (j, i, 0))
        out_shape = jax.ShapeDtypeStruct((N_CHIPS, k, tn), F32)
    else:
        n = b.shape[1]
        tn = _tile(n, tn_cap)
        b_spec = pl.BlockSpec((tl, tn), lambda i, j, r: (r, j))
        out_spec = pl.BlockSpec((tk, tn), lambda i, j, r: (i, j))
        out_shape = jax.ShapeDtypeStruct((k, n), F32)
    return pl.pallas_call(
        body, name=name, grid=(k // tk, n // tn, l // tl),
        in_specs=[pl.BlockSpec((tl, tk), lambda i, j, r: (r, i)), b_spec],
        out_specs=out_spec, out_shape=out_shape,
        compiler_params=_params(("parallel", "parallel", "arbitrary")),
    )(a, b)


def _row(tl, n):
    return pl.BlockSpec((tl, n), lambda i: (i, 0))


def _const(shape):
    return pl.BlockSpec(shape, lambda i: tuple(0 for _ in shape))


def _proj_fwd(x, g1, win, gq, wuq, gkv, wukv, rc, rs, bg, tl):
    l = x.shape[0]

    def body(x_ref, g1_ref, win_ref, gq_ref, wuq_ref, gkv_ref, wukv_ref, rc_ref, rs_ref, bg_ref,
             hn_ref, cq_ref, ckv_ref, q_ref, k_ref, v_ref, u_ref, gl_ref):
        hn, _ = _rms(x_ref[...], g1_ref[...])
        hnb = hn.astype(BF16)
        hn_ref[...] = hnb
        proj = _dot(hnb, win_ref[...])
        cq = proj[:, P_CQ:P_CKV]
        ckv = proj[:, P_CKV:P_KR]
        kr = proj[:, P_KR:P_U]
        cq_ref[...] = cq
        ckv_ref[...] = ckv
        u_ref[...] = proj[:, P_U:P_GL]
        gl_ref[...] = proj[:, P_GL:P_END] + bg_ref[...]
        qn, _ = _rms(cq, gq_ref[...])
        q = _dot(qn.astype(BF16), wuq_ref[...])
        c1 = rc_ref[...]
        s1 = rs_ref[...]
        q_ref[...] = (_rope(q, jnp.tile(c1, (1, N_HEADS)), jnp.tile(s1, (1, N_HEADS))) * Q_PRESCALE).astype(BF16)
        ckvn, _ = _rms(ckv, gkv_ref[...])
        kv = _dot(ckvn.astype(BF16), wukv_ref[...])
        krr = _rope(kr, c1, s1)
        k_ref[...] = (kv[:, :HP] + jnp.tile(krr, (1, N_HEADS))).astype(BF16)
        v_ref[...] = kv[:, HP:].astype(BF16)

    outs = [(D_MODEL, BF16), (Q_RANK, F32), (KV_RANK, F32), (HP, BF16), (HP, BF16), (HP, BF16),
            (SSM_W, F32), (2 * D_MODEL, F32)]
    return pl.pallas_call(
        body, name="proj_fwd", grid=(l // tl,),
        in_specs=[_row(tl, D_MODEL), _const((1, D_MODEL)), _const((D_MODEL, P_END)), _const((1, Q_RANK)),
                  _const((Q_RANK, HP)), _const((1, KV_RANK)), _const((KV_RANK, 2 * HP)),
                  _row(tl, HEAD_SLOT), _row(tl, HEAD_SLOT), _const((1, 2 * D_MODEL))],
        out_specs=[_row(tl, n) for n, _ in outs],
        out_shape=[jax.ShapeDtypeStruct((l, n), dt) for n, dt in outs],
        compiler_params=_params(("parallel",)),
    )(x, g1, win, gq, wuq, gkv, wukv, rc, rs, bg)


_NEG = -1e30


LOG2E = 1.0 / math.log(2.0)
LN2 = math.log(2.0)
ATTN_SCALE = 1.0 / math.sqrt(QK_HEAD)
Q_PRESCALE = ATTN_SCALE * LOG2E
HEADS_PER_STEP = 2
PAIR_W = HEADS_PER_STEP * HEAD_SLOT


def _causal_pairs(nq, by_query):
    if by_query:
        pairs = [(i, j) for i in range(nq) for j in range(i + 1)]
    else:
        pairs = [(i, j) for j in range(nq) for i in range(j, nq)]
    return jnp.array([p[0] for p in pairs], jnp.int32), jnp.array([p[1] for p in pairs], jnp.int32)


def _diag_mask_t(s):
    rows = lax.broadcasted_iota(jnp.int32, s.shape, 0)
    cols = lax.broadcasted_iota(jnp.int32, s.shape, 1)
    return jnp.where(rows <= cols, s, _NEG)


def _attn_fwd(q, k, v, tq):
    l = q.shape[0]
    nq = l // tq
    it, jt = _causal_pairs(nq, True)

    def body(it_ref, jt_ref, q_ref, k_ref, v_ref, o_ref, lse_ref, m_ref, l_ref, acc_ref):
        t = pl.program_id(1)
        i = it_ref[t]
        j = jt_ref[t]

        @pl.when(j == 0)
        def _():
            m_ref[...] = jnp.full_like(m_ref, _NEG)
            l_ref[...] = jnp.zeros_like(l_ref)
            acc_ref[...] = jnp.zeros_like(acc_ref)

        def update(on_diagonal):
            for hh in range(HEADS_PER_STEP):
                sl = slice(hh * HEAD_SLOT, (hh + 1) * HEAD_SLOT)
                s = _dot_nt(k_ref[:, sl], q_ref[:, sl])
                if on_diagonal:
                    s = _diag_mask_t(s)
                m_old = m_ref[hh]
                m_new = jnp.maximum(m_old, jnp.max(s, axis=0, keepdims=True))
                p = jnp.exp2(s - m_new)
                alpha = jnp.exp2(m_old - m_new)
                l_ref[hh] = alpha * l_ref[hh] + jnp.sum(p, axis=0, keepdims=True)
                acc_ref[hh] = alpha * acc_ref[hh] + _dot_tn(v_ref[:, sl], p.astype(BF16))
                m_ref[hh] = m_new

        @pl.when(j < i)
        def _():
            update(False)

        @pl.when(j == i)
        def _():
            update(True)
            for hh in range(HEADS_PER_STEP):
                sl = slice(hh * HEAD_SLOT, (hh + 1) * HEAD_SLOT)
                o_ref[:, sl] = (acc_ref[hh] / l_ref[hh]).T.astype(BF16)
                lse_ref[hh] = m_ref[hh] + jnp.log(l_ref[hh]) * LOG2E

    blk = (tq, PAIR_W)
    qmap = lambda h, t, it_ref, jt_ref: (it_ref[t], h)
    kmap = lambda h, t, it_ref, jt_ref: (jt_ref[t], h)
    row = pl.BlockSpec((HEADS_PER_STEP, 1, tq), lambda h, t, it_ref, jt_ref: (h, 0, it_ref[t]))
    return pl.pallas_call(
        body, name="attn_fwd",
        grid_spec=pltpu.PrefetchScalarGridSpec(
            num_scalar_prefetch=2, grid=(N_HEADS // HEADS_PER_STEP, it.shape[0]),
            in_specs=[pl.BlockSpec(blk, qmap), pl.BlockSpec(blk, kmap), pl.BlockSpec(blk, kmap)],
            out_specs=[pl.BlockSpec(blk, qmap), row],
            scratch_shapes=[pltpu.VMEM((HEADS_PER_STEP, 1, tq), F32), pltpu.VMEM((HEADS_PER_STEP, 1, tq), F32),
                            pltpu.VMEM((HEADS_PER_STEP, HEAD_SLOT, tq), F32)]),
        out_shape=[jax.ShapeDtypeStruct((l, HP), BF16), jax.ShapeDtypeStruct((N_HEADS, 1, l), F32)],
        compiler_params=_params(("parallel", "arbitrary")),
    )(it, jt, q, k, v)


def _attn_delta(o, do, tq):
    l = o.shape[0]

    def body(o_ref, do_ref, d_ref):
        prod = o_ref[...].astype(F32) * do_ref[...].astype(F32)
        for hh in range(HEADS_PER_STEP):
            d_ref[hh] = jnp.sum(prod[:, hh * HEAD_SLOT:(hh + 1) * HEAD_SLOT].T, axis=0, keepdims=True)

    blk = pl.BlockSpec((tq, PAIR_W), lambda h, i: (i, h))
    return pl.pallas_call(
        body, name="attn_delta", grid=(N_HEADS // HEADS_PER_STEP, l // tq), in_specs=[blk, blk],
        out_specs=pl.BlockSpec((HEADS_PER_STEP, 1, tq), lambda h, i: (h, 0, i)),
        out_shape=jax.ShapeDtypeStruct((N_HEADS, 1, l), F32),
        compiler_params=_params(("parallel", "parallel")),
    )(o, do)


def _attn_bwd(q, k, v, do, lse, delta, tq):
    l = q.shape[0]
    nq = l // tq
    it, jt = _causal_pairs(nq, False)

    def body(it_ref, jt_ref, q_ref, k_ref, v_ref, do_ref, lse_ref, dl_ref, dq_ref, dk_ref, dv_ref, dka_ref, dva_ref):
        t = pl.program_id(1)
        i = it_ref[t]
        j = jt_ref[t]

        @pl.when(t == 0)
        def _():
            dq_ref[...] = jnp.zeros_like(dq_ref)

        @pl.when(i == j)
        def _():
            dka_ref[...] = jnp.zeros_like(dka_ref)
            dva_ref[...] = jnp.zeros_like(dva_ref)

        def update(on_diagonal):
            r0 = pl.multiple_of(i * tq, tq)
            for hh in range(HEADS_PER_STEP):
                sl = slice(hh * HEAD_SLOT, (hh + 1) * HEAD_SLOT)
                qb = q_ref[:, sl]
                kb = k_ref[:, sl]
                dob = do_ref[:, sl]
                s = _dot_nt(kb, qb)
                if on_diagonal:
                    s = _diag_mask_t(s)
                p = jnp.exp2(s - lse_ref[hh])
                dva_ref[:, sl] += _dot(p.astype(BF16), dob)
                dp = _dot_nt(v_ref[:, sl], dob)
                ds = (p * (dp - dl_ref[hh])).astype(BF16)
                dka_ref[:, sl] += _dot(ds, qb)
                dq_ref[pl.ds(r0, tq), sl] += ATTN_SCALE * _dot_tn(ds, kb)

        @pl.when(j < i)
        def _():
            update(False)

        @pl.when(j == i)
        def _():
            update(True)

        @pl.when(i == nq - 1)
        def _():
            dk_ref[...] = (dka_ref[...] * LN2).astype(BF16)
            dv_ref[...] = dva_ref[...].astype(BF16)

    blk = (tq, PAIR_W)
    qmap = lambda h, t, it_ref, jt_ref: (it_ref[t], h)
    kmap = lambda h, t, it_ref, jt_ref: (jt_ref[t], h)
    row = pl.BlockSpec((HEADS_PER_STEP, 1, tq), lambda h, t, it_ref, jt_ref: (h, 0, it_ref[t]))
    return pl.pallas_call(
        body, name="attn_bwd",
        grid_spec=pltpu.PrefetchScalarGridSpec(
            num_scalar_prefetch=2, grid=(N_HEADS // HEADS_PER_STEP, it.shape[0]),
            in_specs=[pl.BlockSpec(blk, qmap), pl.BlockSpec(blk, kmap), pl.BlockSpec(blk, kmap),
                      pl.BlockSpec(blk, qmap), row, row],
            out_specs=[pl.BlockSpec((l, PAIR_W), lambda h, t, it_ref, jt_ref: (0, h)), pl.BlockSpec(blk, kmap),
                       pl.BlockSpec(blk, kmap)],
            scratch_shapes=[pltpu.VMEM(blk, F32), pltpu.VMEM(blk, F32)]),
        out_shape=[jax.ShapeDtypeStruct((l, HP), F32), jax.ShapeDtypeStruct((l, HP), BF16),
                   jax.ShapeDtypeStruct((l, HP), BF16)],
        compiler_params=_params(("parallel", "arbitrary")),
    )(it, jt, q, k, v, do, lse, delta)


SSM_CB = 512
SSM_UB = 128
SSM_NB = SSM_CH // SSM_CB


def _scan_tiles(re_ref, im_ref, tab, carry, n_tiles, reverse):
    group = 2
    assert n_tiles % group == 0
    pr, pi = tab[6], tab[7]

    def inside(sr, si):
        for step, k in enumerate((1, 2, 4)):
            mr, mi = tab[2 * step], tab[2 * step + 1]
            sh = (SUBLANES - k) if reverse else k
            rr = pltpu.roll(sr, sh, 0)
            ri = pltpu.roll(si, sh, 0)
            sr, si = sr + mr * rr - mi * ri, si + mr * ri + mi * rr
        return sr, si

    def body(n, c):
        cr, ci = c
        first = (n_tiles - group * (n + 1)) if reverse else group * n
        r0 = pl.multiple_of(first * SUBLANES, group * SUBLANES)
        rows = [pl.ds(r0 + g * SUBLANES, SUBLANES) for g in range(group)]
        tiles = [inside(re_ref[r, :], im_ref[r, :]) for r in rows]
        for g in (range(group - 1, -1, -1) if reverse else range(group)):
            sr, si = tiles[g]
            sr, si = sr + pr * cr - pi * ci, si + pr * ci + pi * cr
            re_ref[rows[g], :] = sr
            im_ref[rows[g], :] = si
            edge = slice(0, 1) if reverse else slice(SUBLANES - 1, SUBLANES)
            cr, ci = sr[edge, :], si[edge, :]
        return cr, ci

    return lax.fori_loop(0, n_tiles // group, body, carry)


def _ssm_fwd(u, bre, bim, cre, cim, dvec, tab, tt):
    l = u.shape[0]
    nt = l // tt

    def body(u_ref, bre_ref, bim_ref, cre_ref, cim_ref, d_ref, tab_ref, y_ref, sre_ref, sim_ref, car_ref):
        @pl.when(pl.program_id(1) == 0)
        def _():
            car_ref[...] = jnp.zeros_like(car_ref)

        uf = u_ref[...]
        ub = uf.astype(BF16)
        sre_ref[...] = _dot(ub, bre_ref[0])
        sim_ref[...] = _dot(ub, bim_ref[0])
        tab_v = [tab_ref[n] for n in range(8)]
        cr, ci = _scan_tiles(sre_ref, sim_ref, tab_v, (car_ref[0:1, :], car_ref[8:9, :]), tt // SUBLANES, False)
        car_ref[0:1, :] = cr
        car_ref[8:9, :] = ci
        y_ref[...] = (_dot(sre_ref[...].astype(BF16), cre_ref[0]) - _dot(sim_ref[...].astype(BF16), cim_ref[0])
                      + d_ref[...] * uf)

    return pl.pallas_call(
        body, name="ssm_fwd", grid=(SSM_NB, nt),
        in_specs=[pl.BlockSpec((tt, SSM_UB), lambda m, t: (t, m)),
                  pl.BlockSpec((1, SSM_UB, SSM_CB), lambda m, t: (m, 0, 0)),
                  pl.BlockSpec((1, SSM_UB, SSM_CB), lambda m, t: (m, 0, 0)),
                  pl.BlockSpec((1, SSM_CB, SSM_UB), lambda m, t: (m, 0, 0)),
                  pl.BlockSpec((1, SSM_CB, SSM_UB), lambda m, t: (m, 0, 0)),
                  pl.BlockSpec((1, SSM_UB), lambda m, t: (0, m)),
                  pl.BlockSpec((8, SUBLANES, SSM_CB), lambda m, t: (0, 0, m))],
        out_specs=[pl.BlockSpec((tt, SSM_UB), lambda m, t: (t, m)),
                   pl.BlockSpec((tt, SSM_CB), lambda m, t: (t, m)),
                   pl.BlockSpec((tt, SSM_CB), lambda m, t: (t, m))],
        out_shape=[jax.ShapeDtypeStruct((l, SSM_W), F32), jax.ShapeDtypeStruct((l, SSM_CH), F32),
                   jax.ShapeDtypeStruct((l, SSM_CH), F32)],
        scratch_shapes=[pltpu.VMEM((2 * SUBLANES, SSM_CB), F32)],
        compiler_params=_params(("parallel", "arbitrary")),
    )(u, bre, bim, cre, cim, dvec, tab)


def _ssm_bwd(dy, u, sre, sim, bre, bim, cre, cim, dvec, tab, tt):
    l = u.shape[0]
    nt = l // tt
    tpb = tt // SUBLANES

    def body(dy_ref, u_ref, sre_ref, sim_ref, hre_ref, him_ref, bre_ref, bim_ref, cre_ref, cim_ref, d_ref, tab_ref,
             du_ref, dbre_ref, dbim_ref, dcre_ref, dcim_ref, dare_ref, daim_ref, dd_ref, lr_ref, li_ref, car_ref):
        t = pl.program_id(1)

        @pl.when(t == 0)
        def _():
            car_ref[...] = jnp.zeros_like(car_ref)
            for ref in (dbre_ref, dbim_ref, dcre_ref, dcim_ref, dare_ref, daim_ref, dd_ref):
                ref[...] = jnp.zeros_like(ref)

        dyf = dy_ref[...]
        dyb = dyf.astype(BF16)
        uf = u_ref[...]
        s_re = sre_ref[...]
        s_im = sim_ref[...]
        lr_ref[...] = _dot_nt(dyb, cre_ref[0])
        li_ref[...] = -_dot_nt(dyb, cim_ref[0])
        dcre_ref[0] += _dot_tn(s_re.astype(BF16), dyb)
        dcim_ref[0] -= _dot_tn(s_im.astype(BF16), dyb)
        tab_v = [tab_ref[n] for n in range(8)]
        cr, ci = _scan_tiles(lr_ref, li_ref, tab_v, (car_ref[0:1, :], car_ref[8:9, :]), tpb, True)
        car_ref[0:1, :] = cr
        car_ref[8:9, :] = ci
        lam_r = lr_ref[...]
        lam_i = li_ref[...]
        keep = jnp.where(t == nt - 1, 0.0, 1.0)
        sp_r = _shift_down(s_re, 1, hre_ref[...] * keep)
        sp_i = _shift_down(s_im, 1, him_ref[...] * keep)
        dare_ref[...] += jnp.sum(lam_r * sp_r + lam_i * sp_i, axis=0, keepdims=True)
        daim_ref[...] += jnp.sum(lam_i * sp_r - lam_r * sp_i, axis=0, keepdims=True)
        lrb = lam_r.astype(BF16)
        lib = lam_i.astype(BF16)
        du_ref[...] = _dot_nt(lrb, bre_ref[0]) + _dot_nt(lib, bim_ref[0]) + dyf * d_ref[...]
        ub = uf.astype(BF16)
        dbre_ref[0] += _dot_tn(ub, lrb)
        dbim_ref[0] += _dot_tn(ub, lib)
        dd_ref[...] += jnp.sum(dyf * uf, axis=0, keepdims=True)

    rev = lambda m, t: (nt - 1 - t, m)
    halo = lambda m, t: (jnp.maximum((nt - 1 - t) * tpb - 1, 0), m)
    wb = pl.BlockSpec((1, SSM_UB, SSM_CB), lambda m, t: (m, 0, 0))
    wc = pl.BlockSpec((1, SSM_CB, SSM_UB), lambda m, t: (m, 0, 0))
    vec_c = pl.BlockSpec((1, SSM_CB), lambda m, t: (0, m))
    vec_u = pl.BlockSpec((1, SSM_UB), lambda m, t: (0, m))
    return pl.pallas_call(
        body, name="ssm_bwd", grid=(SSM_NB, nt),
        in_specs=[pl.BlockSpec((tt, SSM_UB), rev), pl.BlockSpec((tt, SSM_UB), rev),
                  pl.BlockSpec((tt, SSM_CB), rev), pl.BlockSpec((tt, SSM_CB), rev),
                  pl.BlockSpec((SUBLANES, SSM_CB), halo), pl.BlockSpec((SUBLANES, SSM_CB), halo),
                  wb, wb, wc, wc, vec_u,
                  pl.BlockSpec((8, SUBLANES, SSM_CB), lambda m, t: (0, 0, m))],
        out_specs=[pl.BlockSpec((tt, SSM_UB), rev), wb, wb, wc, wc, vec_c, vec_c, vec_u],
        out_shape=[jax.ShapeDtypeStruct((l, SSM_W), F32),
                   jax.ShapeDtypeStruct((SSM_NB, SSM_UB, SSM_CB), F32), jax.ShapeDtypeStruct((SSM_NB, SSM_UB, SSM_CB), F32),
                   jax.ShapeDtypeStruct((SSM_NB, SSM_CB, SSM_UB), F32), jax.ShapeDtypeStruct((SSM_NB, SSM_CB, SSM_UB), F32),
                   jax.ShapeDtypeStruct((1, SSM_CH), F32), jax.ShapeDtypeStruct((1, SSM_CH), F32),
                   jax.ShapeDtypeStruct((1, SSM_W), F32)],
        scratch_shapes=[pltpu.VMEM((tt, SSM_CB), F32), pltpu.VMEM((tt, SSM_CB), F32),
                        pltpu.VMEM((2 * SUBLANES, SSM_CB), F32)],
        compiler_params=_params(("parallel", "arbitrary")),
    )(dy, u, sre, sim, sre, sim, bre, bim, cre, cim, dvec, tab)


def _merge_fwd(x, gl, attn, y1, wba, wbs, wglu, bglu, wout, gpost, gpre, tl):
    l = x.shape[0]

    def body(x_ref, gl_ref, at_ref, y1_ref, wba_ref, wbs_ref, wglu_ref, bglu_ref, wout_ref, gpost_ref, gpre_ref,
             a_ref, sm_ref, mg_ref, z_ref, x1_ref, hn2_ref, y3_ref):
        y2 = _gelu(y1_ref[...])
        sg = _sigmoid(_dot(y2.astype(BF16), wglu_ref[...]) + bglu_ref[...])
        y3 = (y2 * sg).astype(BF16)
        y3_ref[...] = y3
        a = _dot(at_ref[...], wba_ref[...])
        sm = _dot(y3, wbs_ref[...])
        a_ref[...] = a
        sm_ref[...] = sm
        g = _sigmoid(gl_ref[...])
        merged = (g[:, :D_MODEL] * a + g[:, D_MODEL:] * sm).astype(BF16)
        mg_ref[...] = merged
        z = _dot(merged, wout_ref[...])
        z_ref[...] = z
        n, _ = _rms(z, gpost_ref[...])
        x1 = x_ref[...] + n
        x1_ref[...] = x1
        hn2, _ = _rms(x1, gpre_ref[...])
        hn2_ref[...] = hn2.astype(BF16)

    outs = [(D_MODEL, F32), (D_MODEL, F32), (D_MODEL, BF16), (D_MODEL, F32), (D_MODEL, F32), (D_MODEL, BF16),
            (SSM_W, BF16)]
    return pl.pallas_call(
        body, name="merge_fwd", grid=(l // tl,),
        in_specs=[_row(tl, D_MODEL), _row(tl, 2 * D_MODEL), _row(tl, HP), _row(tl, SSM_W),
                  _const((HP, D_MODEL)), _const((SSM_W, D_MODEL)), _const((SSM_W, SSM_W)), _const((1, SSM_W)),
                  _const((D_MODEL, D_MODEL)), _const((1, D_MODEL)), _const((1, D_MODEL))],
        out_specs=[_row(tl, n) for n, _ in outs],
        out_shape=[jax.ShapeDtypeStruct((l, n), dt) for n, dt in outs],
        compiler_params=_params(("parallel",)),
    )(x, gl, attn, y1, wba, wbs, wglu, bglu, wout, gpost, gpre)


def _merge_bwd(dhn2a, dhn2b, x1, dx2, z, gl, a, sm, y1, wba, wbs, wglu, bglu, wout, gpost, gpre, tl):
    l = x1.shape[0]

    def body(da_ref, db_ref, x1_ref, dx2_ref, z_ref, gl_ref, a_ref, sm_ref, y1_ref,
             wba_ref, wbs_ref, wglu_ref, bglu_ref, wout_ref, gpost_ref, gpre_ref,
             dx1_ref, dz_ref, dbra_ref, dbrs_ref, dgl_ref, dat_ref, dy1_ref, dt_ref, y2_ref,
             dgpre_ref, dgpost_ref, dbg_ref, dbglu_ref):
        @pl.when(pl.program_id(0) == 0)
        def _():
            for ref in (dgpre_ref, dgpost_ref, dbg_ref, dbglu_ref):
                ref[...] = jnp.zeros_like(ref)

        dhn2 = da_ref[...] + db_ref[...]
        dx1a, dgpre = _rms_bwd(dhn2, x1_ref[...], gpre_ref[...])
        dgpre_ref[...] += dgpre
        dx1 = dx2_ref[...] + dx1a
        dx1_ref[...] = dx1
        dz, dgpost = _rms_bwd(dx1, z_ref[...], gpost_ref[...])
        dgpost_ref[...] += dgpost
        dzb = dz.astype(BF16)
        dz_ref[...] = dzb
        dm = _dot_nt(dzb, wout_ref[...])
        g = _sigmoid(gl_ref[...])
        g0 = g[:, :D_MODEL]
        g1 = g[:, D_MODEL:]
        dbra = (dm * g0).astype(BF16)
        dbrs = (dm * g1).astype(BF16)
        dbra_ref[...] = dbra
        dbrs_ref[...] = dbrs
        dgl0 = dm * a_ref[...] * g0 * (1.0 - g0)
        dgl1 = dm * sm_ref[...] * g1 * (1.0 - g1)
        dgl_ref[:, :D_MODEL] = dgl0.astype(BF16)
        dgl_ref[:, D_MODEL:] = dgl1.astype(BF16)
        dbg_ref[:, :D_MODEL] += jnp.sum(dgl0, axis=0, keepdims=True)
        dbg_ref[:, D_MODEL:] += jnp.sum(dgl1, axis=0, keepdims=True)
        dat_ref[...] = _dot_nt(dbra, wba_ref[...]).astype(BF16)
        dy3 = _dot_nt(dbrs, wbs_ref[...])
        y1v = y1_ref[...]
        y2 = _gelu(y1v)
        y2b = y2.astype(BF16)
        y2_ref[...] = y2b
        sg = _sigmoid(_dot(y2b, wglu_ref[...]) + bglu_ref[...])
        dt = dy3 * y2 * sg * (1.0 - sg)
        dtb = dt.astype(BF16)
        dt_ref[...] = dtb
        dbglu_ref[...] += jnp.sum(dt, axis=0, keepdims=True)
        dy2 = dy3 * sg + _dot_nt(dtb, wglu_ref[...])
        dy1_ref[...] = dy2 * _gelu_grad(y1v)

    outs = [(D_MODEL, F32), (D_MODEL, BF16), (D_MODEL, BF16), (D_MODEL, BF16), (2 * D_MODEL, BF16), (HP, BF16),
            (SSM_W, F32), (SSM_W, BF16), (SSM_W, BF16)]
    accs = [D_MODEL, D_MODEL, 2 * D_MODEL, SSM_W]
    return pl.pallas_call(
        body, name="merge_bwd", grid=(l // tl,),
        in_specs=[_row(tl, D_MODEL), _row(tl, D_MODEL), _row(tl, D_MODEL), _row(tl, D_MODEL), _row(tl, D_MODEL),
                  _row(tl, 2 * D_MODEL), _row(tl, D_MODEL), _row(tl, D_MODEL), _row(tl, SSM_W),
                  _const((HP, D_MODEL)), _const((SSM_W, D_MODEL)), _const((SSM_W, SSM_W)), _const((1, SSM_W)),
                  _const((D_MODEL, D_MODEL)), _const((1, D_MODEL)), _const((1, D_MODEL))],
        out_specs=[_row(tl, n) for n, _ in outs] + [_const((1, n)) for n in accs],
        out_shape=[jax.ShapeDtypeStruct((l, n), dt) for n, dt in outs]
        + [jax.ShapeDtypeStruct((1, n), F32) for n in accs],
        compiler_params=_params(("arbitrary",)),
    )(dhn2a, dhn2b, x1, dx2, z, gl, a, sm, y1, wba, wbs, wglu, bglu, wout, gpost, gpre)


def _proj_bwd(x, dx1, cq, ckv, dq, dk, dv, du, dgl, g1, win, gq, wuq, gkv, wukv, rc, rs, tl):
    l = x.shape[0]

    def body(x_ref, dx1_ref, cq_ref, ckv_ref, dq_ref, dk_ref, dv_ref, du_ref, dgl_ref,
             g1_ref, win_ref, gq_ref, wuq_ref, gkv_ref, wukv_ref, rc_ref, rs_ref,
             gx_ref, dql_ref, qn_ref, ckvn_ref, dproj_ref, dg1_ref, dgq_ref, dgkv_ref):
        @pl.when(pl.program_id(0) == 0)
        def _():
            for ref in (dg1_ref, dgq_ref, dgkv_ref):
                ref[...] = jnp.zeros_like(ref)

        c1 = rc_ref[...]
        s1 = rs_ref[...]
        dql = _rope_bwd(dq_ref[...], jnp.tile(c1, (1, N_HEADS)), jnp.tile(s1, (1, N_HEADS))).astype(BF16)
        dql_ref[...] = dql
        dqn = _dot_nt(dql, wuq_ref[...])
        cq = cq_ref[...]
        qn, _ = _rms(cq, gq_ref[...])
        qn_ref[...] = qn.astype(BF16)
        dcq, dgq = _rms_bwd(dqn, cq, gq_ref[...])
        dgq_ref[...] += dgq
        dkb = dk_ref[...]
        dvb = dv_ref[...]
        dkf = dkb.astype(F32)
        dkr = dkf[:, 0:HEAD_SLOT]
        for h in range(1, N_HEADS):
            dkr = dkr + dkf[:, h * HEAD_SLOT:(h + 1) * HEAD_SLOT]
        dkr = _rope_bwd(dkr, c1, s1)
        dckvn = _dot_nt(dkb, wukv_ref[:, :HP]) + _dot_nt(dvb, wukv_ref[:, HP:])
        ckv = ckv_ref[...]
        ckvn, _ = _rms(ckv, gkv_ref[...])
        ckvn_ref[...] = ckvn.astype(BF16)
        dckv, dgkv = _rms_bwd(dckvn, ckv, gkv_ref[...])
        dgkv_ref[...] += dgkv
        dproj_ref[:, P_CQ:P_CKV] = dcq.astype(BF16)
        dproj_ref[:, P_CKV:P_KR] = dckv.astype(BF16)
        dproj_ref[:, P_KR:P_U] = dkr.astype(BF16)
        dproj_ref[:, P_U:P_GL] = du_ref[...].astype(BF16)
        dproj_ref[:, P_GL:P_END] = dgl_ref[...]
        dhn = _dot_nt(dproj_ref[...], win_ref[...])
        dxa, dg1 = _rms_bwd(dhn, x_ref[...], g1_ref[...])
        dg1_ref[...] += dg1
        gx_ref[...] = dx1_ref[...] + dxa

    outs = [(D_MODEL, F32), (HP, BF16), (Q_RANK, BF16), (KV_RANK, BF16), (P_END, BF16)]
    accs = [D_MODEL, Q_RANK, KV_RANK]
    return pl.pallas_call(
        body, name="proj_bwd", grid=(l // tl,),
        in_specs=[_row(tl, D_MODEL), _row(tl, D_MODEL), _row(tl, Q_RANK), _row(tl, KV_RANK), _row(tl, HP),
                  _row(tl, HP), _row(tl, HP), _row(tl, SSM_W), _row(tl, 2 * D_MODEL),
                  _const((1, D_MODEL)), _const((D_MODEL, P_END)), _const((1, Q_RANK)), _const((Q_RANK, HP)),
                  _const((1, KV_RANK)), _const((KV_RANK, 2 * HP)), _row(tl, HEAD_SLOT), _row(tl, HEAD_SLOT)],
        out_specs=[_row(tl, n) for n, _ in outs] + [_const((1, n)) for n in accs],
        out_shape=[jax.ShapeDtypeStruct((l, n), dt) for n, dt in outs]
        + [jax.ShapeDtypeStruct((1, n), F32) for n in accs],
        compiler_params=_params(("arbitrary",)),
    )(x, dx1, cq, ckv, dq, dk, dv, du, dgl, g1, win, gq, wuq, gkv, wukv, rc, rs)


CONV_CB = 256
CONV_NB = D_FF // CONV_CB
CONV_ROWS = 16


def _conv3(h, halo, w, b):
    return b + w[0:1, :] * _shift_down(h, 2, halo) + w[1:2, :] * _shift_down(h, 1, halo) + w[2:3, :] * h


def _conv_fwd(h, cw, cb, tl):
    l = h.shape[0]

    def body(hg_ref, hv_ref, wg_ref, wv_ref, bg_ref, bv_ref, act_ref, halo_ref):
        @pl.when(pl.program_id(1) == 0)
        def _():
            halo_ref[...] = jnp.zeros_like(halo_ref)

        wg, wv, bg, bv = wg_ref[...], wv_ref[...], bg_ref[...], bv_ref[...]

        def chunk(n, prev):
            pg, pv = prev
            rows = pl.ds(pl.multiple_of(n * CONV_ROWS, CONV_ROWS), CONV_ROWS)
            hg = hg_ref[rows, :]
            hv = hv_ref[rows, :]
            act_ref[rows, :] = (_gelu(_conv3(hg, pg, wg, bg)) * _conv3(hv, pv, wv, bv)).astype(BF16)
            return hg[CONV_ROWS - SUBLANES:, :], hv[CONV_ROWS - SUBLANES:, :]

        pg, pv = lax.fori_loop(0, tl // CONV_ROWS, chunk, (halo_ref[0:SUBLANES, :], halo_ref[SUBLANES:, :]))
        halo_ref[0:SUBLANES, :] = pg
        halo_ref[SUBLANES:, :] = pv

    gmap = lambda c, r: (r, c)
    vmap = lambda c, r: (r, CONV_NB + c)
    return pl.pallas_call(
        body, name="conv_fwd", grid=(CONV_NB, l // tl),
        in_specs=[pl.BlockSpec((tl, CONV_CB), gmap), pl.BlockSpec((tl, CONV_CB), vmap),
                  pl.BlockSpec((3, CONV_CB), lambda c, r: (0, c)), pl.BlockSpec((3, CONV_CB), lambda c, r: (0, CONV_NB + c)),
                  pl.BlockSpec((1, CONV_CB), lambda c, r: (0, c)), pl.BlockSpec((1, CONV_CB), lambda c, r: (0, CONV_NB + c))],
        out_specs=pl.BlockSpec((tl, CONV_CB), gmap),
        out_shape=jax.ShapeDtypeStruct((l, D_FF), BF16),
        scratch_shapes=[pltpu.VMEM((2 * SUBLANES, CONV_CB), F32)],
        compiler_params=_params(("parallel", "arbitrary")),
    )(h, h, cw, cw, cb, cb)


def _conv_bwd(h, dact, cw, cb, tl):
    l = h.shape[0]
    nr = l // tl
    tpb = tl // SUBLANES

    def body(hg_ref, hv_ref, hgh_ref, hvh_ref, da_ref, wg_ref, wv_ref, bg_ref, bv_ref,
             dh_ref, dwg_ref, dwv_ref, dbg_ref, dbv_ref, car_ref):
        r = pl.program_id(1)

        @pl.when(r == 0)
        def _():
            for ref in (car_ref, dwg_ref, dwv_ref, dbg_ref, dbv_ref):
                ref[...] = jnp.zeros_like(ref)

        keep = jnp.where(r == nr - 1, 0.0, 1.0)
        wg, wv, bg, bv = wg_ref[...], wv_ref[...], bg_ref[...], bv_ref[...]
        nch = tl // CONV_ROWS

        def fold(x):
            s = x[0:SUBLANES, :]
            for k in range(1, CONV_ROWS // SUBLANES):
                s = s + x[k * SUBLANES:(k + 1) * SUBLANES, :]
            return s

        def chunk(n, carry):
            ncg, ncv, acc = carry
            idx = nch - 1 - n
            r0 = pl.multiple_of(idx * CONV_ROWS, CONV_ROWS)
            rows = pl.ds(r0, CONV_ROWS)
            before = pl.ds(pl.multiple_of(jnp.maximum(r0 - SUBLANES, 0), SUBLANES), SUBLANES)
            in_tile = idx > 0
            da = da_ref[rows, :].astype(F32)

            def half(h_ref, halo_ref, w, b):
                hh = h_ref[rows, :]
                prev = jnp.where(in_tile, h_ref[before, :], halo_ref[...] * keep)
                h1 = _shift_down(hh, 1, prev)
                h2 = _shift_down(hh, 2, prev)
                return hh, h1, h2, b + w[0:1, :] * h2 + w[1:2, :] * h1 + w[2:3, :] * hh

            hg, hg1, hg2, cg = half(hg_ref, hgh_ref, wg, bg)
            hv, hv1, hv2, cv = half(hv_ref, hvh_ref, wv, bv)
            dcg = da * cv * _gelu_grad(cg)
            dcv = da * _gelu(cg)

            def back(dc, hh, h1, h2, w, nxt, part):
                dh = w[2:3, :] * dc + w[1:2, :] * _shift_up(dc, 1, nxt) + w[0:1, :] * _shift_up(dc, 2, nxt)
                dh_ref[part, rows, :] = dh.astype(BF16)
                return [fold(dc * h2), fold(dc * h1), fold(dc * hh), fold(dc)]

            sums = back(dcg, hg, hg1, hg2, wg, ncg, 0) + back(dcv, hv, hv1, hv2, wv, ncv, 1)
            return dcg[0:SUBLANES, :], dcv[0:SUBLANES, :], [a + s for a, s in zip(acc, sums)]

        zero = jnp.zeros((SUBLANES, CONV_CB), F32)
        ncg, ncv, acc = lax.fori_loop(0, nch, chunk, (car_ref[0:SUBLANES, :], car_ref[SUBLANES:, :], [zero] * 8))
        car_ref[0:SUBLANES, :] = ncg
        car_ref[SUBLANES:, :] = ncv
        for half_acc, dw_ref, db_ref in ((acc[0:4], dwg_ref, dbg_ref), (acc[4:8], dwv_ref, dbv_ref)):
            for k in range(3):
                dw_ref[k:k + 1, :] += jnp.sum(half_acc[k], axis=0, keepdims=True)
            db_ref[...] += jnp.sum(half_acc[3], axis=0, keepdims=True)

    grev = lambda c, r: (nr - 1 - r, c)
    vrev = lambda c, r: (nr - 1 - r, CONV_NB + c)
    ghalo = lambda c, r: (jnp.maximum((nr - 1 - r) * tpb - 1, 0), c)
    vhalo = lambda c, r: (jnp.maximum((nr - 1 - r) * tpb - 1, 0), CONV_NB + c)
    colg = lambda c, r: (0, c)
    colv = lambda c, r: (0, CONV_NB + c)
    return pl.pallas_call(
        body, name="conv_bwd", grid=(CONV_NB, nr),
        in_specs=[pl.BlockSpec((tl, CONV_CB), grev), pl.BlockSpec((tl, CONV_CB), vrev),
                  pl.BlockSpec((SUBLANES, CONV_CB), ghalo), pl.BlockSpec((SUBLANES, CONV_CB), vhalo),
                  pl.BlockSpec((tl, CONV_CB), grev),
                  pl.BlockSpec((3, CONV_CB), colg), pl.BlockSpec((3, CONV_CB), colv),
                  pl.BlockSpec((1, CONV_CB), colg), pl.BlockSpec((1, CONV_CB), colv)],
        out_specs=[pl.BlockSpec((2, tl, CONV_CB), lambda c, r: (0, nr - 1 - r, c)),
                   pl.BlockSpec((3, CONV_CB), colg), pl.BlockSpec((3, CONV_CB), colg),
                   pl.BlockSpec((1, CONV_CB), colg), pl.BlockSpec((1, CONV_CB), colg)],
        out_shape=[jax.ShapeDtypeStruct((2, l, D_FF), BF16),
                   jax.ShapeDtypeStruct((3, D_FF), F32), jax.ShapeDtypeStruct((3, D_FF), F32),
                   jax.ShapeDtypeStruct((1, D_FF), F32), jax.ShapeDtypeStruct((1, D_FF), F32)],
        scratch_shapes=[pltpu.VMEM((2 * SUBLANES, CONV_CB), F32)],
        compiler_params=_params(("parallel", "arbitrary")),
    )(h, h, h, h, dact, cw, cw, cb, cb)


def _loss_head(ff, x1, tgt, g, tl):
    l = ff.shape[0]

    def body(ff_ref, x1_ref, tg_ref, g_ref, loss_ref, dx2_ref, dff_ref, dg_ref):
        @pl.when(pl.program_id(0) == 0)
        def _():
            loss_ref[...] = jnp.zeros_like(loss_ref)
            dg_ref[...] = jnp.zeros_like(dg_ref)

        f = ff_ref[...]
        gv = g_ref[...]
        n, _ = _rms(f, gv)
        e = x1_ref[...] + n - tg_ref[...]
        loss_ref[...] += 0.5 * jnp.sum(jnp.mean(e * e, axis=-1, keepdims=True), axis=0, keepdims=True)
        dx2 = e * (1.0 / D_MODEL)
        dx2_ref[...] = dx2
        dff, dg = _rms_bwd(dx2, f, gv)
        dff_ref[...] = dff.astype(BF16)
        dg_ref[...] += dg

    return pl.pallas_call(
        body, name="loss_head", grid=(l // tl,),
        in_specs=[_row(tl, D_MODEL), _row(tl, D_MODEL), _row(tl, D_MODEL), _const((1, D_MODEL))],
        out_specs=[_const((1, LANES)), _row(tl, D_MODEL), _row(tl, D_MODEL), _const((1, D_MODEL))],
        out_shape=[jax.ShapeDtypeStruct((1, LANES), F32), jax.ShapeDtypeStruct((l, D_MODEL), F32),
                   jax.ShapeDtypeStruct((l, D_MODEL), BF16), jax.ShapeDtypeStruct((1, D_MODEL), F32)],
        compiler_params=_params(("arbitrary",)),
    )(ff, x1, tgt, g)


def _ssm_disc(lam_re, lam_im, log_dt, b_re, b_im):
    dt = jnp.exp(log_dt)[:, None]
    mag = jnp.exp(lam_re * dt)
    ang = lam_im * dt
    a_re, a_im = mag * jnp.cos(ang), mag * jnp.sin(ang)
    den = lam_re * lam_re + lam_im * lam_im
    n_re, n_im = a_re - 1.0, a_im
    z_re = (n_re * lam_re + n_im * lam_im) / den
    z_im = (n_im * lam_re - n_re * lam_im) / den
    bb_re = z_re[..., None] * b_re - z_im[..., None] * b_im
    bb_im = z_re[..., None] * b_im + z_im[..., None] * b_re
    return a_re, a_im, bb_re, bb_im


_GPB = SSM_CB // SSM_P


def _embed_b(bb):
    t = bb.transpose(0, 2, 1).reshape(SSM_NB, _GPB, SSM_H, SSM_P)
    return jnp.einsum('mjhp,jk->mjhkp', t, jnp.eye(_GPB, dtype=bb.dtype)).reshape(SSM_NB, SSM_UB, SSM_CB)


def _extract_b(d):
    t = d.reshape(SSM_NB, _GPB, SSM_H, _GPB, SSM_P)
    t = jnp.einsum('mjhkp,jk->mjhp', t, jnp.eye(_GPB, dtype=d.dtype))
    return t.reshape(SSM_G, SSM_H, SSM_P).transpose(0, 2, 1)


def _embed_c(c):
    t = c.transpose(0, 2, 1).reshape(SSM_NB, _GPB, SSM_P, SSM_H)
    return jnp.einsum('mjph,jk->mjpkh', t, jnp.eye(_GPB, dtype=c.dtype)).reshape(SSM_NB, SSM_CB, SSM_UB)


def _extract_c(d):
    t = d.reshape(SSM_NB, _GPB, SSM_P, _GPB, SSM_H)
    t = jnp.einsum('mjpkh,jk->mjph', t, jnp.eye(_GPB, dtype=d.dtype))
    return t.reshape(SSM_G, SSM_P, SSM_H).transpose(0, 2, 1)


def _scan_tables(a_re, a_im, reverse):
    ar = a_re.reshape(1, SSM_CH)
    ai = (-a_im if reverse else a_im).reshape(1, SSM_CH)
    pr, pi = [ar], [ai]
    for _ in range(SUBLANES - 1):
        pr, pi = pr + [pr[-1] * ar - pi[-1] * ai], pi + [pr[-1] * ai + pi[-1] * ar]
    rows = jnp.arange(SUBLANES)[:, None]
    out = []
    for k in (1, 2, 4):
        valid = (rows + k <= SUBLANES - 1) if reverse else (rows >= k)
        out += [jnp.where(valid, pr[k - 1], 0.0), jnp.where(valid, pi[k - 1], 0.0)]
    order = list(range(SUBLANES - 1, -1, -1)) if reverse else list(range(SUBLANES))
    out += [jnp.concatenate([pr[n] for n in order], axis=0), jnp.concatenate([pi[n] for n in order], axis=0)]
    return jnp.stack(out).astype(F32)


def _pad_heads(w, d):
    lead = w.shape[:-1]
    w = w.reshape(lead + (N_HEADS, d))
    w = jnp.pad(w, [(0, 0)] * len(lead) + [(0, 0), (0, HEAD_SLOT - d)])
    return w.reshape(lead + (HP,))


def _unpad_heads(w, d):
    lead = w.shape[:-1]
    return w.reshape(lead + (N_HEADS, HEAD_SLOT))[..., :d].reshape(lead + (N_HEADS * d,))


def _chip_major(w, axis):
    k, n = w.shape
    if axis == 0:
        return w.reshape(N_CHIPS, k // N_CHIPS, n)
    return w.reshape(k, N_CHIPS, n // N_CHIPS).transpose(1, 0, 2)


def _from_chip_major(w, axis):
    if axis == 0:
        return w.reshape(-1, w.shape[2])
    return w.transpose(1, 0, 2).reshape(w.shape[1], -1)


def _pad_w_in(w):
    z = lambda n: jnp.zeros((w.shape[0], n), w.dtype)
    return jnp.concatenate([w[:, :640], z(KR_LANE), w[:, 640:672], z(HEAD_SLOT - KR_LANE - QK_ROPE), w[:, 672:]], axis=1)


def _unpad_w_in(w):
    return jnp.concatenate([w[:, :640], w[:, P_KR + KR_LANE:P_KR + KR_LANE + QK_ROPE], w[:, P_U:]], axis=1)


def _local_step(x, positions, tgt, wts, sp):
    l = x.shape[0]
    tl = min(256, l)
    ta = min(512, l)

    inv_freq = ROPE_THETA ** (-jnp.arange(0, QK_ROPE, 2, dtype=F32) / QK_ROPE)
    ang = positions.astype(F32)[:, None] * inv_freq
    cos, sin = jnp.cos(ang), jnp.sin(ang)
    one = jnp.ones((l, KR_LANE), F32)
    rc = jnp.concatenate([one, cos, cos, jnp.ones((l, HEAD_SLOT - KR_LANE - QK_ROPE), F32)], axis=1)
    rs = jnp.concatenate([0 * one, -sin, sin, jnp.zeros((l, HEAD_SLOT - KR_LANE - QK_ROPE), F32)], axis=1)

    win = _pad_w_in(wts["w_in"])
    wuq = _pad_heads(wts["w_uq"], QK_HEAD)
    wukv = jnp.concatenate([_pad_heads(wts["w_uk"], QK_NOPE), _pad_heads(wts["w_uv"], V_HEAD)], axis=1)

    disc_in = (sp["ssm_lambda_re"], sp["ssm_lambda_im"], sp["ssm_log_dt"], sp["ssm_b_re"], sp["ssm_b_im"])
    (a_re, a_im, bb_re, bb_im), disc_vjp = jax.vjp(_ssm_disc, *disc_in)
    bre, bim = _embed_b(bb_re).astype(BF16), _embed_b(bb_im).astype(BF16)
    cre, cim = _embed_c(sp["ssm_c_re"]).astype(BF16), _embed_c(sp["ssm_c_im"]).astype(BF16)
    dvec = sp["ssm_d"].reshape(1, SSM_W)
    tab_f = _scan_tables(a_re, a_im, False)
    tab_r = _scan_tables(a_re, a_im, True)

    g1, gq, gkv = sp["mix_norm_pre"], sp["q_norm"], sp["kv_norm"]
    gpost, gpre, gfin = sp["mix_norm_post"], sp["ffn_norm_pre"], sp["ffn_norm_post"]
    bgate, bglu, convb = sp["b_gate"], sp["b_glu"], sp["conv_b"]

    hn, cq, ckv, q, k, v, u, gl = _proj_fwd(x, g1, win, gq, wuq, gkv, wukv, rc, rs, bgate, tl)
    attn, lse = _attn_fwd(q, k, v, ta)
    y1, sre, sim = _ssm_fwd(u, bre, bim, cre, cim, dvec, tab_f, ta)
    late = wts["late"](y1)
    wba = jnp.pad(late["w_branch_attn"].reshape(N_HEADS, V_HEAD, D_MODEL),
                  ((0, 0), (0, HEAD_SLOT - V_HEAD), (0, 0))).reshape(HP, D_MODEL)
    wbs, wglu, wout = late["w_branch_ssm"], late["w_glu"], late["w_out"]
    wup, wdown, convw = late["w_up"], late["w_down"], late["conv_w"]
    a, sm, merged, z, x1, hn2, y3 = _merge_fwd(x, gl, attn, y1, wba, wbs, wglu, bglu, wout, gpost, gpre, tl)
    h = _mm(hn2, wup, "ffn_up")
    act = _conv_fwd(h, convw, convb, ta)
    ff = _mm(act, wdown, "ffn_down")
    loss, dx2, dff, dgfin = _loss_head(ff, x1, tgt, gfin, tl)

    dact = _mm(dff, wdown, "ffn_down_dx", out_dtype=BF16, bt=True)
    d_wdown = _mm_tn(act, dff, "ffn_down_dw", tk_cap=D_FF // 2)
    dh, dwg, dwv, dbg, dbv = _conv_bwd(h, dact, convw, convb, ta)
    d_convw = jnp.concatenate([dwg, dwv], axis=1)
    d_convb = jnp.concatenate([dbg, dbv], axis=1)
    dhn2a = _mm(dh, wup, "ffn_up_dx_gate", bt=True, b_col0=0, a_lead=0)
    dhn2b = _mm(dh, wup, "ffn_up_dx_val", bt=True, b_col0=1, a_lead=1)
    d_wup = _mm_tn(hn2, dh, "ffn_up_dw", chips=True)
    behind = wts["send_grads"]("ffn", {"w_up": d_wup, "w_down": _chip_major(d_wdown, 0)})
    (dx1, dz, dbra, dbrs, dgl, dattn, dy1, dt, y2, dgpre, dgpost, dbgate, dbglu) = _merge_bwd(
        dhn2a, dhn2b, x1, dx2, z, gl, a, sm, y1, wba, wbs, wglu, bglu, wout, gpost, gpre + behind, tl)
    d_wout = _mm_tn(merged, dz, "w_out_dw")
    d_wba = _mm_tn(attn, dbra, "w_branch_attn_dw", chips=True)
    d_wbs = _mm_tn(y3, dbrs, "w_branch_ssm_dw", chips=True)
    d_wglu = _mm_tn(y2, dt, "w_glu_dw")
    ncol = D_MODEL // N_CHIPS
    behind = wts["send_grads"]("mix", {
        "w_glu": _chip_major(d_wglu, 0),
        "w_branch_attn": d_wba.reshape(N_CHIPS, N_HEADS, HEAD_SLOT, ncol)[:, :, :V_HEAD].reshape(
            N_CHIPS, N_HEADS * V_HEAD, ncol),
        "w_branch_ssm": d_wbs,
        "w_out": _chip_major(d_wout, 0)})
    dq, dk, dv = _attn_bwd(q, k, v, dattn, lse + behind, _attn_delta(attn, dattn, min(2048, l)), ta)
    du, dbre, dbim, dcre, dcim, dare, daim, dd = _ssm_bwd(dy1, u, sre, sim, bre, bim, cre, cim, dvec, tab_r, ta)
    gx, dql, qn, ckvn, dproj, dg1, dgq, dgkv = _proj_bwd(
        x, dx1, cq, ckv, dq, dk, dv, du, dgl, g1, win, gq, wuq, gkv, wukv, rc, rs, tl)
    d_win = _mm_tn(hn, dproj, "w_in_dw")
    d_wuq = _mm_tn(qn, dql, "w_uq_dw")
    d_wuk = _mm_tn(ckvn, dk, "w_uk_dw")
    d_wuv = _mm_tn(ckvn, dv, "w_uv_dw")

    d_lre, d_lim, d_ldt, d_bre, d_bim = disc_vjp((dare.reshape(SSM_G, SSM_P), daim.reshape(SSM_G, SSM_P),
                                                  _extract_b(dbre), _extract_b(dbim)))
    big = {
        "w_in": _chip_major(_unpad_w_in(d_win), 1),
        "w_uq": _chip_major(_unpad_heads(d_wuq, QK_HEAD), 1),
        "w_uk": _chip_major(_unpad_heads(d_wuk, QK_NOPE), 1),
        "w_uv": _chip_major(_unpad_heads(d_wuv, V_HEAD), 1),
    }
    small = {
        "conv_w": d_convw,
        "mix_norm_pre": dg1, "q_norm": dgq, "kv_norm": dgkv,
        "ssm_lambda_re": d_lre, "ssm_lambda_im": d_lim, "ssm_log_dt": d_ldt,
        "ssm_b_re": d_bre, "ssm_b_im": d_bim,
        "ssm_c_re": _extract_c(dcre), "ssm_c_im": _extract_c(dcim),
        "ssm_d": dd.reshape(SSM_G, SSM_H), "b_glu": dbglu, "b_gate": dbgate,
        "mix_norm_post": dgpost, "ffn_norm_pre": dgpre, "conv_b": d_convb, "ffn_norm_post": dgfin,
    }
    return loss[0, 0], gx, big, small


_ANY = pl.BlockSpec(memory_space=pl.ANY)


ROW_TILE = 16


def _place():
    x, y, c = lax.axis_index("x"), lax.axis_index("y"), lax.axis_index("c")
    return x, y, c, 2 * x + y, [(1 - x, y), (x, 1 - y), (1 - x, 1 - y)]


def _half(rows, which):
    hr = rows // 2
    return pl.ds(pl.multiple_of(which * hr, ROW_TILE), hr)


def _remote(src, dst, send_sems, recv_sems, n, dev):
    return pltpu.make_async_remote_copy(src_ref=src, dst_ref=dst, send_sem=send_sems.at[n], recv_sem=recv_sems.at[n],
                                        device_id=dev, device_id_type=MESH)


def _gather_big(shards):
    nw = len(shards)
    rows = [s.shape[0] for s in shards]

    def body(*refs):
        ins, outs = refs[:nw], refs[nw:2 * nw]
        ici_send, ici_recv, d2d_send, d2d_recv = refs[2 * nw:]
        x, y, c, me, peers = _place()
        sent = []
        for i in range(nw):
            for p, (px, py) in enumerate(peers):
                cp = _remote(ins[i].at[_half(rows[i], c)], outs[i].at[me, _half(rows[i], c)], ici_send, ici_recv,
                             3 * i + p, (px, py, c))
                cp.start()
                sent.append(cp)
        for p, (px, py) in enumerate(peers):
            for i in range(nw):
                blk = outs[i].at[2 * px + py, _half(rows[i], c)]
                _remote(blk, blk, ici_send, ici_recv, 3 * i + p, (px, py, c)).wait_recv()
                cp = _remote(blk, blk, d2d_send, d2d_recv, 3 * i + p, (x, y, 1 - c))
                cp.start()
                sent.append(cp)
        for p, (px, py) in enumerate(peers):
            for i in range(nw):
                blk = outs[i].at[2 * px + py, _half(rows[i], 1 - c)]
                _remote(blk, blk, d2d_send, d2d_recv, 3 * i + p, (x, y, 1 - c)).wait_recv()
        for cp in sent:
            cp.wait_send()

    dma = pltpu.SemaphoreType.DMA
    return pl.pallas_call(
        body, name="gather_weights", in_specs=[_ANY] * nw, out_specs=[_ANY] * nw,
        out_shape=[jax.ShapeDtypeStruct((N_CHIPS,) + s.shape, s.dtype) for s in shards],
        scratch_shapes=[dma((3 * nw,)), dma((3 * nw,)), dma((3 * nw,)), dma((3 * nw,))],
    )(*shards)


_HBM = pl.BlockSpec(memory_space=pltpu.HBM)
_SEM = pl.BlockSpec(memory_space=pltpu.SEMAPHORE)
_DATAFLOW = pltpu.SideEffectType.DATAFLOW_SIDE_EFFECTING


def _exchange_start(shards, name, scatter):
    nw = len(shards)
    lands = [lax.empty(s.shape if scatter else (N_CHIPS,) + s.shape, s.dtype) for s in shards]

    def body(*refs):
        ins, zones = refs[:nw], refs[nw:2 * nw]
        send_sems, recv_sems, token = refs[2 * nw], refs[2 * nw + 1], refs[-1]
        x, y, c, me, peers = _place()
        for i in range(nw):
            for p, (px, py) in enumerate(peers):
                src = ins[i].at[2 * px + py] if scatter else ins[i]
                _remote(src, zones[i].at[me], send_sems, recv_sems, 3 * i + p, (px, py, c)).start()
        token[...] = jnp.zeros_like(token)

    thru = [pltpu.HBM(a.shape, a.dtype) for a in list(shards) + lands]
    dma = pltpu.SemaphoreType.DMA
    outs = pl.pallas_call(
        body, name=name,
        out_shape=(dma((3 * nw,)), dma((3 * nw,)), *thru, jax.ShapeDtypeStruct((SUBLANES, LANES), F32)),
        in_specs=[_HBM] * (2 * nw),
        out_specs=(_SEM, _SEM, *([_HBM] * (2 * nw)), pl.BlockSpec(memory_space=pltpu.VMEM)),
        input_output_aliases={i: 2 + i for i in range(2 * nw)},
        compiler_params=pltpu.CompilerParams(has_side_effects=_DATAFLOW),
    )(*[pltpu.with_memory_space_constraint(a, pltpu.HBM) for a in list(shards) + lands])
    return outs[0], outs[1], list(outs[2:2 + nw]), list(outs[2 + nw:2 + 2 * nw]), outs[-1]


def _exchange_wait(send_sems, recv_sems, shards, lands, after, name, scatter):
    nw = len(shards)

    def body(*refs):
        ins, zones = refs[:nw], refs[nw:2 * nw]
        send_sems, recv_sems = refs[2 * nw], refs[2 * nw + 1]
        x, y, c, me, peers = _place()
        for i in range(nw):
            for p, (px, py) in enumerate(peers):
                src = ins[i].at[2 * px + py] if scatter else ins[i]
                cp = _remote(src, zones[i].at[2 * px + py], send_sems, recv_sems, 3 * i + p, (px, py, c))
                cp.wait_send()
                cp.wait_recv()

    both = list(shards) + list(lands)
    outs = pl.pallas_call(
        body, name=name,
        out_shape=tuple(pltpu.HBM(a.shape, a.dtype) for a in both),
        in_specs=(*([_HBM] * (2 * nw)), _SEM, _SEM, _ANY), out_specs=[_HBM] * (2 * nw),
        input_output_aliases={i: i for i in range(2 * nw)},
        compiler_params=pltpu.CompilerParams(has_side_effects=_DATAFLOW),
    )(*both, send_sems, recv_sems, after)
    return list(outs[:nw]), list(outs[nw:])


def _reduce_to_sibling(grads, name):
    nw = len(grads)

    def body(*refs):
        ins, outs = refs[:nw], refs[nw:2 * nw]
        send_sems, recv_sems = refs[2 * nw:]
        x, y, c, _, _ = _place()
        sent = []
        for i in range(nw):
            cp = _remote(ins[i].at[pl.ds(0, N_CHIPS), _half(grads[i].shape[1], 1 - c)], outs[i], send_sems, recv_sems,
                         i, (x, y, 1 - c))
            cp.start()
            sent.append(cp)
        for cp in sent:
            cp.wait()

    dma = pltpu.SemaphoreType.DMA
    return pl.pallas_call(
        body, name=name, in_specs=[_ANY] * nw, out_specs=[_ANY] * nw,
        out_shape=[jax.ShapeDtypeStruct((N_CHIPS, g.shape[1] // 2, g.shape[2]), g.dtype) for g in grads],
        scratch_shapes=[dma((nw,)), dma((nw,))],
    )(*grads)


def _reduce_between_chips(pairs):
    nw = len(pairs)

    def body(*refs):
        ins, outs = refs[:nw], refs[nw:2 * nw]
        send_sems, recv_sems = refs[2 * nw:]
        x, y, c, me, peers = _place()
        sent = []
        for i in range(nw):
            for p, (px, py) in enumerate(peers):
                cp = _remote(ins[i].at[2 * px + py], outs[i].at[me], send_sems, recv_sems, 3 * i + p, (px, py, c))
                cp.start()
                sent.append(cp)
        for i in range(nw):
            for p, (px, py) in enumerate(peers):
                blk = outs[i].at[2 * px + py]
                _remote(blk, blk, send_sems, recv_sems, 3 * i + p, (px, py, c)).wait_recv()
        for cp in sent:
            cp.wait_send()

    dma = pltpu.SemaphoreType.DMA
    return pl.pallas_call(
        body, name="reduce_grads_ici", in_specs=[_ANY] * nw, out_specs=[_ANY] * nw,
        out_shape=[jax.ShapeDtypeStruct(p.shape, p.dtype) for p in pairs],
        scratch_shapes=[dma((3 * nw,)), dma((3 * nw,))],
    )(*pairs)


def _reduce_back(totals):
    nw = len(totals)

    def body(*refs):
        outs = refs[nw:2 * nw]
        send_sems, recv_sems = refs[2 * nw:]
        x, y, c, _, _ = _place()
        sent = []
        for i in range(nw):
            blk = outs[i].at[_half(totals[i].shape[0], c)]
            cp = _remote(blk, blk, send_sems, recv_sems, i, (x, y, 1 - c))
            cp.start()
            sent.append(cp)
        for i in range(nw):
            blk = outs[i].at[_half(totals[i].shape[0], 1 - c)]
            _remote(blk, blk, send_sems, recv_sems, i, (x, y, 1 - c)).wait_recv()
        for cp in sent:
            cp.wait_send()

    dma = pltpu.SemaphoreType.DMA
    return pl.pallas_call(
        body, name="reduce_grads_back", in_specs=[_ANY] * nw, out_specs=[_ANY] * nw,
        out_shape=[jax.ShapeDtypeStruct(t.shape, t.dtype) for t in totals],
        input_output_aliases={i: i for i in range(nw)},
        scratch_shapes=[dma((nw,)), dma((nw,))],
    )(*totals)


def _all_reduce_small(v, name):
    rows, w = v.shape
    hr = rows // 2
    assert hr % SUBLANES == 0

    def body(v_ref, out_ref, sib_ref, half_ref, chips_ref, send_sems, recv_sems):
        x, y, c, me, peers = _place()
        sibling = (x, y, 1 - c)
        mine = pl.ds(pl.multiple_of(c * hr, SUBLANES), hr)
        other = pl.ds(pl.multiple_of((1 - c) * hr, SUBLANES), hr)
        cp = _remote(v_ref, sib_ref, send_sems, recv_sems, 0, sibling)
        cp.start()
        cp.wait()
        half_ref[...] = v_ref[mine, :] + sib_ref[mine, :]
        sent = []
        for p, (px, py) in enumerate(peers):
            cp = _remote(half_ref, chips_ref.at[me], send_sems, recv_sems, 1 + p, (px, py, c))
            cp.start()
            sent.append(cp)
        chips_ref[me] = half_ref[...]
        for p, (px, py) in enumerate(peers):
            _remote(half_ref, chips_ref.at[2 * px + py], send_sems, recv_sems, 1 + p, (px, py, c)).wait_recv()
        for cp in sent:
            cp.wait_send()
        out_ref[mine, :] = ((chips_ref[0] + chips_ref[1]) + chips_ref[2]) + chips_ref[3]
        cp = _remote(out_ref.at[mine], out_ref.at[mine], send_sems, recv_sems, 4, sibling)
        cp.start()
        _remote(out_ref.at[other], out_ref.at[other], send_sems, recv_sems, 4, sibling).wait_recv()
        cp.wait_send()

    vm = pl.BlockSpec(memory_space=pltpu.VMEM)
    return pl.pallas_call(
        body, name=name, in_specs=[vm], out_specs=vm,
        out_shape=jax.ShapeDtypeStruct((rows, w), F32),
        scratch_shapes=[pltpu.VMEM((rows, w), F32), pltpu.VMEM((hr, w), F32), pltpu.VMEM((N_CHIPS, hr, w), F32),
                        pltpu.SemaphoreType.DMA((5,)), pltpu.SemaphoreType.DMA((5,))],
        compiler_params=pltpu.CompilerParams(vmem_limit_bytes=VMEM_LIMIT),
    )(v)


ELEMENTWISE_BLOCK = 256 * 1024


def _rows_tile(rows, cols):
    best = None
    for t in range(SUBLANES, rows + 1, SUBLANES):
        if rows % t == 0 and t * cols <= ELEMENTWISE_BLOCK:
            best = t
    return rows if best is None else best


def _add_pair(g, t, core, name):
    nb, n, w = t.shape
    tr = _rows_tile(n, w)
    steps = n // tr

    def body(core_ref, g_ref, t_ref, o_ref):
        o_ref[...] = (g_ref[...] + t_ref[...]).astype(BF16)

    spec = pl.BlockSpec((1, tr, w), lambda j, i, core_ref: (j, i, 0))
    return pl.pallas_call(
        body, name=name,
        grid_spec=pltpu.PrefetchScalarGridSpec(
            num_scalar_prefetch=1, grid=(nb, steps),
            in_specs=[pl.BlockSpec((1, tr, w), lambda j, i, core_ref: (j, core_ref[0] * steps + i, 0)), spec],
            out_specs=spec),
        out_shape=jax.ShapeDtypeStruct(t.shape, BF16),
        compiler_params=_params(("parallel", "parallel")))(core, g, t)


def _add_chips(landed, pairs, place, name):
    nb, n, w = landed.shape
    tr = _rows_tile(n, w)
    steps = n // tr

    def body(place_ref, r_ref, own_ref, o_ref):
        me = place_ref[0]
        acc = None
        for k in range(nb):
            blk = jnp.where(me == k, own_ref[0], r_ref[k]).astype(F32)
            acc = blk if acc is None else acc + blk
        o_ref[...] = acc

    return pl.pallas_call(
        body, name=name,
        grid_spec=pltpu.PrefetchScalarGridSpec(
            num_scalar_prefetch=1, grid=(steps,),
            in_specs=[pl.BlockSpec((nb, tr, w), lambda i, place_ref: (0, i, 0)),
                      pl.BlockSpec((1, tr, w), lambda i, place_ref: (place_ref[0], i, 0))],
            out_specs=pl.BlockSpec((tr, w), lambda i, place_ref: (place_ref[1] * steps + i, 0))),
        out_shape=jax.ShapeDtypeStruct((2 * n, w), F32),
        compiler_params=_params(("parallel",)))(place, landed, pairs)


def _adamw(w, g, m, v, name):
    rows, wd = w.shape
    tr = _rows_tile(rows, wd)
    c1 = 1.0 - ADAM_B1 ** ADAM_STEP
    c2 = 1.0 - ADAM_B2 ** ADAM_STEP

    def body(w_ref, g_ref, m_ref, v_ref, d_ref, mo_ref, vo_ref):
        gv = g_ref[...]
        m2 = ADAM_B1 * m_ref[...] + (1.0 - ADAM_B1) * gv
        v2 = ADAM_B2 * v_ref[...] + (1.0 - ADAM_B2) * (gv * gv)
        mo_ref[...] = m2
        vo_ref[...] = v2
        d_ref[...] = -ADAM_LR * ((m2 / c1) / (jnp.sqrt(v2 / c2) + ADAM_EPS) + ADAM_WD * w_ref[...])

    spec = pl.BlockSpec((tr, wd), lambda i: (i, 0))
    shp = jax.ShapeDtypeStruct((rows, wd), F32)
    return pl.pallas_call(body, name=name, grid=(rows // tr,), in_specs=[spec] * 4, out_specs=[spec] * 3,
                          out_shape=[shp] * 3, compiler_params=_params(("parallel",)))(w, g, m, v)


BIG = [("w_in", (1024, 3232), 1), ("w_uq", (384, 768), 1), ("w_uk", (256, 512), 1), ("w_uv", (256, 512), 1),
       ("w_glu", (512, 512), 0), ("w_branch_attn", (512, 1024), 1), ("w_branch_ssm", (512, 1024), 1),
       ("w_out", (1024, 1024), 0), ("w_up", (1024, 5632), 1), ("conv_w", (3, 5632), 1), ("w_down", (2816, 1024), 0)]
SMALL = [("mix_norm_pre", (1024,)), ("q_norm", (384,)), ("kv_norm", (256,)), ("ssm_lambda_re", (32, 64)),
         ("ssm_lambda_im", (32, 64)), ("ssm_log_dt", (32,)), ("ssm_b_re", (32, 64, 16)), ("ssm_b_im", (32, 64, 16)),
         ("ssm_c_re", (32, 16, 64)), ("ssm_c_im", (32, 16, 64)), ("ssm_d", (32, 16)), ("b_glu", (512,)),
         ("b_gate", (2048,)), ("mix_norm_post", (1024,)), ("ffn_norm_pre", (1024,)), ("conv_b", (5632,)),
         ("ffn_norm_post", (1024,))]
MATMUL_W = [b for b in BIG if b[0] != "conv_w"]
FIRST_W = ("w_in", "w_uq", "w_uk", "w_uv")
CONV_W_SHAPE = (3, 2 * D_FF)
CONV_W_SHARD = (3, 2 * D_FF // N_CHIPS)
SMALL_SUM = [("loss", (1,))] + SMALL + [("conv_w", CONV_W_SHAPE)]
SMALL_ADAM = SMALL + [("conv_w", CONV_W_SHARD)]


def _pack_flat(layout, vals):
    flat = jnp.concatenate([vals[n].astype(F32).reshape(-1) for n, _ in layout])
    rows = -(-(-(-flat.shape[0] // FLAT_W)) // (2 * SUBLANES)) * 2 * SUBLANES
    return jnp.pad(flat, (0, rows * FLAT_W - flat.shape[0])).reshape(rows, FLAT_W)


def _unpack_flat(layout, flat):
    flat = flat.reshape(-1)
    out = {}
    o = 0
    for name, shape in layout:
        n = math.prod(shape)
        out[name] = flat[o:o + n].reshape(shape)
        o += n
    return out


_ARG_NAMES = ["x", "positions"] + [n for n in (
    "mix_norm_pre", "w_in", "q_norm", "w_uq", "kv_norm", "w_uk", "w_uv", "ssm_lambda_re", "ssm_lambda_im", "ssm_log_dt",
    "ssm_b_re", "ssm_b_im", "ssm_c_re", "ssm_c_im", "ssm_d", "w_glu", "b_glu", "w_branch_attn", "w_branch_ssm",
    "b_gate", "w_out", "mix_norm_post", "ffn_norm_pre", "w_up", "conv_w", "conv_b", "w_down", "ffn_norm_post")]
_WEIGHTS = _ARG_NAMES[2:]


def _gather_weights(w):
    early = [b for b in MATMUL_W if b[0] in FIRST_W]
    late = [b for b in BIG if b[0] not in FIRST_W]
    own = (jnp.arange(N_CHIPS) == 2 * lax.axis_index("x") + lax.axis_index("y"))[:, None, None]

    def whole(layout, mine, gathered):
        return {name: _from_chip_major(jnp.where(own, s[None], g), axis)
                for (name, _, axis), s, g in zip(layout, mine, gathered)}

    mine = [w[name].astype(BF16) for name, _, _ in early]
    full = whole(early, mine, _gather_big(mine))
    mine_late = [w[name].astype(F32 if name == "conv_w" else BF16) for name, _, _ in late]
    _, mine_late = lax.optimization_barrier((full["w_in"], mine_late))
    send_sems, recv_sems, shards_thru, lands_thru, token = _exchange_start(mine_late, "gather_late_start", scatter=False)

    def late_weights(after):
        shards, lands = _exchange_wait(send_sems, recv_sems, shards_thru, lands_thru, after, "gather_late_wait",
                                       scatter=False)
        return whole(late, shards, lands)

    full["late"] = late_weights
    full["token"] = token[0, 0]
    return full


def _pair_sums(names, grads, tag):
    core = lax.axis_index("c").astype(jnp.int32).reshape(1)
    theirs = _reduce_to_sibling(grads, "reduce_grads_d2d" + tag)
    return [_add_pair(g, t, core, "reduce_pair_" + n) for n, g, t in zip(names, grads, theirs)]


def _start_reduce(tag, grads):
    names = list(grads)
    pairs = _pair_sums(names, [grads[n] for n in names], "_" + tag)
    send_sems, recv_sems, pairs_thru, lands_thru, token = _exchange_start(pairs, "reduce_" + tag + "_start", scatter=True)
    return (tag, names, send_sems, recv_sems, pairs_thru, lands_thru), token[0, 0]


def _reduce_grads(gbig, pending, loss, gsmall):
    core = lax.axis_index("c").astype(jnp.int32).reshape(1)
    chip = (2 * lax.axis_index("x") + lax.axis_index("y")).astype(jnp.int32).reshape(1)
    place = jnp.concatenate([chip, core])
    names = list(gbig)
    pairs = _pair_sums(names, [gbig[n] for n in names], "")
    landed = list(_reduce_between_chips(pairs))
    for tag, sent_names, send_sems, recv_sems, pairs_thru, lands_thru in pending:
        got_pairs, got_landed = _exchange_wait(send_sems, recv_sems, pairs_thru, lands_thru, gbig["w_in"],
                                               "reduce_" + tag + "_wait", scatter=True)
        names, pairs, landed = names + sent_names, pairs + got_pairs, landed + got_landed
    totals = [_add_chips(r, p, place, "reduce_chips_" + n) for n, r, p in zip(names, landed, pairs)]
    g_red = dict(zip(names, _reduce_back(totals)))

    vals = dict(gsmall)
    vals["loss"] = loss
    small_red = _unpack_flat(SMALL_SUM, _all_reduce_small(_pack_flat(SMALL_SUM, vals), "reduce_small"))
    return g_red, small_red


def _step(args):
    x = args["x"][0]
    positions = args["positions"][0]
    tgt = args["loss_target"][0]
    w = {n: args[n][0] for n in _WEIGHTS}
    m = {n: args["m_" + n][0] for n in _WEIGHTS}
    v = {n: args["v_" + n][0] for n in _WEIGHTS}

    full = _gather_weights(w)
    sp = {n: w[n].reshape(s) for n, s in SMALL}
    for n in ("mix_norm_pre", "q_norm", "kv_norm", "b_glu", "b_gate", "mix_norm_post", "ffn_norm_pre", "conv_b",
              "ffn_norm_post"):
        sp[n] = sp[n].reshape(1, -1)
    sp["mix_norm_pre"] = sp["mix_norm_pre"] + full.pop("token")
    pending = []

    def send_grads(tag, grads):
        state, token = _start_reduce(tag, grads)
        pending.append(state)
        return token

    full["send_grads"] = send_grads
    loss, gx, gbig, gsmall = _local_step(x, positions, tgt, full, sp)
    g_red, small_red = _reduce_grads(gbig, pending, loss, gsmall)

    chip = 2 * lax.axis_index("x") + lax.axis_index("y")
    grads = dict(small_red)
    grads["conv_w"] = lax.dynamic_slice_in_dim(small_red["conv_w"], chip * CONV_W_SHARD[1], CONV_W_SHARD[1], axis=1)
    grads.update(g_red)

    outs = {"grad_" + n: grads[n] for n in _WEIGHTS}
    for name, _, _ in MATMUL_W:
        d, m2, v2 = _adamw(w[name], grads[name], m[name], v[name], "adamw_" + name)
        outs["delta_" + name], outs["new_m_" + name], outs["new_v_" + name] = d, m2, v2
    d_sm, m_sm, v_sm = _adamw(_pack_flat(SMALL_ADAM, w), _pack_flat(SMALL_ADAM, grads), _pack_flat(SMALL_ADAM, m),
                              _pack_flat(SMALL_ADAM, v), "adamw_small")
    for prefix, flat in (("delta_", d_sm), ("new_m_", m_sm), ("new_v_", v_sm)):
        for n, val in _unpack_flat(SMALL_ADAM, flat).items():
            outs[prefix + n] = val
    outs = {n: val.reshape(args[n.split("_", 1)[1] if not n.startswith("new_") else n[6:]].shape)
            for n, val in outs.items()}
    res = [small_red["loss"][0], gx[None]]
    for prefix in ("grad_", "delta_", "new_m_", "new_v_"):
        res += [outs[prefix + n] for n in _WEIGHTS]
    return tuple(res)


def kernel(x, positions, mix_norm_pre, w_in, q_norm, w_uq, kv_norm, w_uk, w_uv, ssm_lambda_re, ssm_lambda_im, ssm_log_dt, ssm_b_re, ssm_b_im, ssm_c_re, ssm_c_im, ssm_d, w_glu, b_glu, w_branch_attn, w_branch_ssm, b_gate, w_out, mix_norm_post, ffn_norm_pre, w_up, conv_w, conv_b, w_down, ffn_norm_post, loss_target, m_mix_norm_pre, m_w_in, m_q_norm, m_w_uq, m_kv_norm, m_w_uk, m_w_uv, m_ssm_lambda_re, m_ssm_lambda_im, m_ssm_log_dt, m_ssm_b_re, m_ssm_b_im, m_ssm_c_re, m_ssm_c_im, m_ssm_d, m_w_glu, m_b_glu, m_w_branch_attn, m_w_branch_ssm, m_b_gate, m_w_out, m_mix_norm_post, m_ffn_norm_pre, m_w_up, m_conv_w, m_conv_b, m_w_down, m_ffn_norm_post, v_mix_norm_pre, v_w_in, v_q_norm, v_w_uq, v_kv_norm, v_w_uk, v_w_uv, v_ssm_lambda_re, v_ssm_lambda_im, v_ssm_log_dt, v_ssm_b_re, v_ssm_b_im, v_ssm_c_re, v_ssm_c_im, v_ssm_d, v_w_glu, v_b_glu, v_w_branch_attn, v_w_branch_ssm, v_b_gate, v_w_out, v_mix_norm_post, v_ffn_norm_pre, v_w_up, v_conv_w, v_conv_b, v_w_down, v_ffn_norm_post):
    given = dict(locals())
    return _step(given)
```

```python
import math

import jax
import jax.numpy as jnp
from jax import lax
from jax.experimental import pallas as pl
from jax.experimental.pallas import tpu as pltpu

F32 = jnp.float32
BF16 = jnp.bfloat16
MESH = pl.DeviceIdType.MESH

D_MODEL = 1024
N_HEADS = 8
QK_NOPE = 64
QK_ROPE = 32
QK_HEAD = QK_NOPE + QK_ROPE
V_HEAD = 64
Q_RANK = 384
KV_RANK = 256
ROPE_THETA = 10000.0
SSM_W = 512
SSM_H = 16
SSM_G = 32
SSM_P = 64
SSM_CH = SSM_G * SSM_P
D_FF = 2816
EPS = 1e-6
ADAM_LR = 0.001
ADAM_B1 = 0.9
ADAM_B2 = 0.999
ADAM_EPS = 1e-08
ADAM_WD = 0.01
ADAM_STEP = 10

LANES = 128
SUBLANES = 8
VMEM_LIMIT = 56 * 1024 * 1024

HEAD_SLOT = LANES
HP = N_HEADS * HEAD_SLOT
P_CQ, P_CKV, P_KR, P_U, P_GL, P_END = 0, 384, 640, 768, 1280, 3328
KR_LANE = 64

FLAT_W = 1024
N_CHIPS = 4


def _tile(n, cap):
    if n <= cap:
        return n
    best = None
    for t in range(LANES, cap + 1, LANES):
        if n % t == 0:
            best = t
    assert best is not None, (n, cap)
    return best


def _params(sem):
    return pltpu.CompilerParams(dimension_semantics=sem, vmem_limit_bytes=VMEM_LIMIT)


def _dot(a, b):
    return jnp.dot(a, b, preferred_element_type=F32)


def _dot_nt(a, b):
    return lax.dot_general(a, b, (((1,), (1,)), ((), ())), preferred_element_type=F32)


def _dot_tn(a, b):
    return lax.dot_general(a, b, (((0,), (0,)), ((), ())), preferred_element_type=F32)


def _rms(x, g):
    r = lax.rsqrt(jnp.mean(x * x, axis=-1, keepdims=True) + EPS)
    return x * r * g, r


def _rms_bwd(dy, x, g):
    r = lax.rsqrt(jnp.mean(x * x, axis=-1, keepdims=True) + EPS)
    dyg = dy * g
    dx = r * dyg - x * (r * r * r) * jnp.mean(dyg * x, axis=-1, keepdims=True)
    dg = jnp.sum(dy * x * r, axis=0, keepdims=True)
    return dx, dg


_GELU_K0 = math.sqrt(2.0 / math.pi)
_GELU_K1 = 0.044715


def _gelu(x):
    th = jnp.tanh(_GELU_K0 * (x + _GELU_K1 * x * x * x))
    return 0.5 * x * (1.0 + th)


def _gelu_grad(x):
    th = jnp.tanh(_GELU_K0 * (x + _GELU_K1 * x * x * x))
    return 0.5 * (1.0 + th) + 0.5 * x * (1.0 - th * th) * _GELU_K0 * (1.0 + 3.0 * _GELU_K1 * x * x)


def _sigmoid(x):
    return 1.0 / (1.0 + jnp.exp(-x))


def _rope(q, c, s):
    n = q.shape[1]
    lane = lax.broadcasted_iota(jnp.int32, q.shape, 1) % HEAD_SLOT
    sw = jnp.where(lane < KR_LANE + QK_ROPE // 2, pltpu.roll(q, n - QK_ROPE // 2, 1), pltpu.roll(q, QK_ROPE // 2, 1))
    return q * c + sw * s


def _rope_bwd(dy, c, s):
    n = dy.shape[1]
    t = dy * s
    lane = lax.broadcasted_iota(jnp.int32, dy.shape, 1) % HEAD_SLOT
    sw = jnp.where(lane < KR_LANE + QK_ROPE // 2, pltpu.roll(t, n - QK_ROPE // 2, 1), pltpu.roll(t, QK_ROPE // 2, 1))
    rope_lane = jnp.logical_and(lane >= KR_LANE, lane < KR_LANE + QK_ROPE)
    return dy * c + jnp.where(rope_lane, sw, 0.0)


def _shift_down(x, k, halo):
    xs = pltpu.roll(x, k, 0)
    hs = pltpu.roll(halo, k, 0)
    rows = lax.broadcasted_iota(jnp.int32, halo.shape, 0)
    top = jnp.where(rows < k, hs, xs[0:SUBLANES])
    return jnp.concatenate([top, xs[SUBLANES:]], axis=0)


def _shift_up(x, k, halo):
    t = x.shape[0]
    xs = pltpu.roll(x, t - k, 0)
    hs = pltpu.roll(halo, SUBLANES - k, 0)
    rows = lax.broadcasted_iota(jnp.int32, halo.shape, 0)
    bot = jnp.where(rows >= SUBLANES - k, hs, xs[t - SUBLANES:])
    return jnp.concatenate([xs[:t - SUBLANES], bot], axis=0)


def _mm(a, b, name, out_dtype=F32, bt=False, b_col0=0, n=None, tm_cap=1024, tn_cap=1408, a_lead=None):
    m, k = a.shape[-2:]
    if bt:
        n_full = b.shape[0]
        n = n_full
    else:
        n = b.shape[1] if n is None else n
    tm = min(tm_cap, m)
    tn = _tile(n, tn_cap)

    def body(a_ref, b_ref, o_ref):
        if bt:
            o_ref[...] = _dot_nt(a_ref[...], b_ref[...]).astype(out_dtype)
        else:
            o_ref[...] = _dot(a_ref[...], b_ref[...]).astype(out_dtype)

    if bt:
        b_spec = pl.BlockSpec((tn, k), lambda j, i: (j, b_col0))
    else:
        off = b_col0 * (n // tn)
        b_spec = pl.BlockSpec((k, tn), lambda j, i: (0, off + j))
    if a_lead is None:
        a_spec = pl.BlockSpec((tm, k), lambda j, i: (i, 0))
    else:
        a_spec = pl.BlockSpec((None, tm, k), lambda j, i: (a_lead, i, 0))
    return pl.pallas_call(
        body, name=name, grid=(n // tn, m // tm),
        in_specs=[a_spec, b_spec],
        out_specs=pl.BlockSpec((tm, tn), lambda j, i: (i, j)),
        out_shape=jax.ShapeDtypeStruct((m, n), out_dtype),
        compiler_params=_params(("parallel", "parallel")),
    )(a, b)


def _mm_tn(a, b, name, tk_cap=1024, tn_cap=1664, tl_cap=1024, chips=False):
    l, k = a.shape
    tk = _tile(k, tk_cap)
    tl = min(tl_cap, l)

    def body(a_ref, b_ref, o_ref):
        @pl.when(pl.program_id(2) == 0)
        def _():
            o_ref[...] = jnp.zeros_like(o_ref)

        o_ref[...] += _dot_tn(a_ref[...], b_ref[...])

    if chips:
        n = b.shape[-1] * (b.shape[0] if b.ndim == 3 else 1)
        tn = n // N_CHIPS
        assert tn % LANES == 0
        if b.ndim == 3:
            per = N_CHIPS // b.shape[0]
            b_spec = pl.BlockSpec((None, tl, tn), lambda i, j, r: (j // per, r, j % per))
        else:
            b_spec = pl.BlockSpec((tl, tn), lambda i, j, r: (r, j))
        out_spec = pl.BlockSpec((None, tk, tn), lambda i, j, r: (j, i, 0))
        out_shape = jax.ShapeDtypeStruct((N_CHIPS, k, tn), F32)
    else:
        n = b.shape[1]
        tn = _tile(n, tn_cap)
        b_spec = pl.BlockSpec((tl, tn), lambda i, j, r: (r, j))
        out_spec = pl.BlockSpec((tk, tn), lambda i, j, r: (i, j))
        out_shape = jax.ShapeDtypeStruct((k, n), F32)
    return pl.pallas_call(
        body, name=name, grid=(k // tk, n // tn, l // tl),
        in_specs=[pl.BlockSpec((tl, tk), lambda i, j, r: (r, i)), b_spec],
        out_specs=out_spec, out_shape=out_shape,
        compiler_params=_params(("parallel", "parallel", "arbitrary")),
    )(a, b)


def _row(tl, n):
    return pl.BlockSpec((tl, n), lambda i: (i, 0))


def _const(shape):
    return pl.BlockSpec(shape, lambda i: tuple(0 for _ in shape))


def _proj_fwd(x, g1, win, gq, wuq, gkv, wukv, rc, rs, bg, tl):
    l = x.shape[0]

    def body(x_ref, g1_ref, win_ref, gq_ref, wuq_ref, gkv_ref, wukv_ref, rc_ref, rs_ref, bg_ref,
             hn_ref, cq_ref, ckv_ref, q_ref, k_ref, v_ref, u_ref, gl_ref):
        hn, _ = _rms(x_ref[...], g1_ref[...])
        hnb = hn.astype(BF16)
        hn_ref[...] = hnb
        proj = _dot(hnb, win_ref[...])
        cq = proj[:, P_CQ:P_CKV]
        ckv = proj[:, P_CKV:P_KR]
        kr = proj[:, P_KR:P_U]
        cq_ref[...] = cq
        ckv_ref[...] = ckv
        u_ref[...] = proj[:, P_U:P_GL]
        gl_ref[...] = proj[:, P_GL:P_END] + bg_ref[...]
        qn, _ = _rms(cq, gq_ref[...])
        q = _dot(qn.astype(BF16), wuq_ref[...])
        c1 = rc_ref[...]
        s1 = rs_ref[...]
        q_ref[...] = (_rope(q, jnp.tile(c1, (1, N_HEADS)), jnp.tile(s1, (1, N_HEADS))) * Q_PRESCALE).astype(BF16)
        ckvn, _ = _rms(ckv, gkv_ref[...])
        kv = _dot(ckvn.astype(BF16), wukv_ref[...])
        krr = _rope(kr, c1, s1)
        k_ref[...] = (kv[:, :HP] + jnp.tile(krr, (1, N_HEADS))).astype(BF16)
        v_ref[...] = kv[:, HP:].astype(BF16)

    outs = [(D_MODEL, BF16), (Q_RANK, F32), (KV_RANK, F32), (HP, BF16), (HP, BF16), (HP, BF16),
            (SSM_W, F32), (2 * D_MODEL, F32)]
    return pl.pallas_call(
        body, name="proj_fwd", grid=(l // tl,),
        in_specs=[_row(tl, D_MODEL), _const((1, D_MODEL)), _const((D_MODEL, P_END)), _const((1, Q_RANK)),
                  _const((Q_RANK, HP)), _const((1, KV_RANK)), _const((KV_RANK, 2 * HP)),
                  _row(tl, HEAD_SLOT), _row(tl, HEAD_SLOT), _const((1, 2 * D_MODEL))],
        out_specs=[_row(tl, n) for n, _ in outs],
        out_shape=[jax.ShapeDtypeStruct((l, n), dt) for n, dt in outs],
        compiler_params=_params(("parallel",)),
    )(x, g1, win, gq, wuq, gkv, wukv, rc, rs, bg)


_NEG = -1e30


LOG2E = 1.0 / math.log(2.0)
LN2 = math.log(2.0)
ATTN_SCALE = 1.0 / math.sqrt(QK_HEAD)
Q_PRESCALE = ATTN_SCALE * LOG2E
HEADS_PER_STEP = 2
PAIR_W = HEADS_PER_STEP * HEAD_SLOT


def _causal_pairs(nq, by_query):
    if by_query:
        pairs = [(i, j) for i in range(nq) for j in range(i + 1)]
    else:
        pairs = [(i, j) for j in range(nq) for i in range(j, nq)]
    return jnp.array([p[0] for p in pairs], jnp.int32), jnp.array([p[1] for p in pairs], jnp.int32)


def _diag_mask_t(s):
    rows = lax.broadcasted_iota(jnp.int32, s.shape, 0)
    cols = lax.broadcasted_iota(jnp.int32, s.shape, 1)
    return jnp.where(rows <= cols, s, _NEG)


def _attn_fwd(q, k, v, tq):
    l = q.shape[0]
    nq = l // tq
    it, jt = _causal_pairs(nq, True)

    def body(it_ref, jt_ref, q_ref, k_ref, v_ref, o_ref, lse_ref, m_ref, l_ref, acc_ref):
        t = pl.program_id(1)
        i = it_ref[t]
        j = jt_ref[t]

        @pl.when(j == 0)
        def _():
            m_ref[...] = jnp.full_like(m_ref, _NEG)
            l_ref[...] = jnp.zeros_like(l_ref)
            acc_ref[...] = jnp.zeros_like(acc_ref)

        def update(on_diagonal):
            for hh in range(HEADS_PER_STEP):
                sl = slice(hh * HEAD_SLOT, (hh + 1) * HEAD_SLOT)
                s = _dot_nt(k_ref[:, sl], q_ref[:, sl])
                if on_diagonal:
                    s = _diag_mask_t(s)
                m_old = m_ref[hh]
                m_new = jnp.maximum(m_old, jnp.max(s, axis=0, keepdims=True))
                p = jnp.exp2(s - m_new)
                alpha = jnp.exp2(m_old - m_new)
                l_ref[hh] = alpha * l_ref[hh] + jnp.sum(p, axis=0, keepdims=True)
                acc_ref[hh] = alpha * acc_ref[hh] + _dot_tn(v_ref[:, sl], p.astype(BF16))
                m_ref[hh] = m_new

        @pl.when(j < i)
        def _():
            update(False)

        @pl.when(j == i)
        def _():
            update(True)
            for hh in range(HEADS_PER_STEP):
                sl = slice(hh * HEAD_SLOT, (hh + 1) * HEAD_SLOT)
                o_ref[:, sl] = (acc_ref[hh] / l_ref[hh]).T.astype(BF16)
                lse_ref[hh] = m_ref[hh] + jnp.log(l_ref[hh]) * LOG2E

    blk = (tq, PAIR_W)
    qmap = lambda h, t, it_ref, jt_ref: (it_ref[t], h)
    kmap = lambda h, t, it_ref, jt_ref: (jt_ref[t], h)
    row = pl.BlockSpec((HEADS_PER_STEP, 1, tq), lambda h, t, it_ref, jt_ref: (h, 0, it_ref[t]))
    return pl.pallas_call(
        body, name="attn_fwd",
        grid_spec=pltpu.PrefetchScalarGridSpec(
            num_scalar_prefetch=2, grid=(N_HEADS // HEADS_PER_STEP, it.shape[0]),
            in_specs=[pl.BlockSpec(blk, qmap), pl.BlockSpec(blk, kmap), pl.BlockSpec(blk, kmap)],
            out_specs=[pl.BlockSpec(blk, qmap), row],
            scratch_shapes=[pltpu.VMEM((HEADS_PER_STEP, 1, tq), F32), pltpu.VMEM((HEADS_PER_STEP, 1, tq), F32),
                            pltpu.VMEM((HEADS_PER_STEP, HEAD_SLOT, tq), F32)]),
        out_shape=[jax.ShapeDtypeStruct((l, HP), BF16), jax.ShapeDtypeStruct((N_HEADS, 1, l), F32)],
        compiler_params=_params(("parallel", "arbitrary")),
    )(it, jt, q, k, v)


def _attn_delta(o, do, tq):
    l = o.shape[0]

    def body(o_ref, do_ref, d_ref):
        prod = o_ref[...].astype(F32) * do_ref[...].astype(F32)
        for hh in range(HEADS_PER_STEP):
            d_ref[hh] = jnp.sum(prod[:, hh * HEAD_SLOT:(hh + 1) * HEAD_SLOT].T, axis=0, keepdims=True)

    blk = pl.BlockSpec((tq, PAIR_W), lambda h, i: (i, h))
    return pl.pallas_call(
        body, name="attn_delta", grid=(N_HEADS // HEADS_PER_STEP, l // tq), in_specs=[blk, blk],
        out_specs=pl.BlockSpec((HEADS_PER_STEP, 1, tq), lambda h, i: (h, 0, i)),
        out_shape=jax.ShapeDtypeStruct((N_HEADS, 1, l), F32),
        compiler_params=_params(("parallel", "parallel")),
    )(o, do)


def _attn_bwd(q, k, v, do, lse, delta, tq):
    l = q.shape[0]
    nq = l // tq
    it, jt = _causal_pairs(nq, False)

    def body(it_ref, jt_ref, q_ref, k_ref, v_ref, do_ref, lse_ref, dl_ref, dq_ref, dk_ref, dv_ref, dka_ref, dva_ref):
        t = pl.program_id(1)
        i = it_ref[t]
        j = jt_ref[t]

        @pl.when(t == 0)
        def _():
            dq_ref[...] = jnp.zeros_like(dq_ref)

        @pl.when(i == j)
        def _():
            dka_ref[...] = jnp.zeros_like(dka_ref)
            dva_ref[...] = jnp.zeros_like(dva_ref)

        def update(on_diagonal):
            r0 = pl.multiple_of(i * tq, tq)
            for hh in range(HEADS_PER_STEP):
                sl = slice(hh * HEAD_SLOT, (hh + 1) * HEAD_SLOT)
                qb = q_ref[:, sl]
                kb = k_ref[:, sl]
                dob = do_ref[:, sl]
                s = _dot_nt(kb, qb)
                if on_diagonal:
                    s = _diag_mask_t(s)
                p = jnp.exp2(s - lse_ref[hh])
                dva_ref[:, sl] += _dot(p.astype(BF16), dob)
                dp = _dot_nt(v_ref[:, sl], dob)
                ds = (p * (dp - dl_ref[hh])).astype(BF16)
                dka_ref[:, sl] += _dot(ds, qb)
                dq_ref[pl.ds(r0, tq), sl] += ATTN_SCALE * _dot_tn(ds, kb)

        @pl.when(j < i)
        def _():
            update(False)

        @pl.when(j == i)
        def _():
            update(True)

        @pl.when(i == nq - 1)
        def _():
            dk_ref[...] = (dka_ref[...] * LN2).astype(BF16)
            dv_ref[...] = dva_ref[...].astype(BF16)

    blk = (tq, PAIR_W)
    qmap = lambda h, t, it_ref, jt_ref: (it_ref[t], h)
    kmap = lambda h, t, it_ref, jt_ref: (jt_ref[t], h)
    row = pl.BlockSpec((HEADS_PER_STEP, 1, tq), lambda h, t, it_ref, jt_ref: (h, 0, it_ref[t]))
    return pl.pallas_call(
        body, name="attn_bwd",
        grid_spec=pltpu.PrefetchScalarGridSpec(
            num_scalar_prefetch=2, grid=(N_HEADS // HEADS_PER_STEP, it.shape[0]),
            in_specs=[pl.BlockSpec(blk, qmap), pl.BlockSpec(blk, kmap), pl.BlockSpec(blk, kmap),
                      pl.BlockSpec(blk, qmap), row, row],
            out_specs=[pl.BlockSpec((l, PAIR_W), lambda h, t, it_ref, jt_ref: (0, h)), pl.BlockSpec(blk, kmap),
                       pl.BlockSpec(blk, kmap)],
            scratch_shapes=[pltpu.VMEM(blk, F32), pltpu.VMEM(blk, F32)]),
        out_shape=[jax.ShapeDtypeStruct((l, HP), F32), jax.ShapeDtypeStruct((l, HP), BF16),
                   jax.ShapeDtypeStruct((l, HP), BF16)],
        compiler_params=_params(("parallel", "arbitrary")),
    )(it, jt, q, k, v, do, lse, delta)


SSM_CB = 512
SSM_UB = 128
SSM_NB = SSM_CH // SSM_CB


def _scan_tiles(re_ref, im_ref, tab, carry, n_tiles, reverse):
    group = 2
    assert n_tiles % group == 0
    pr, pi = tab[6], tab[7]

    def inside(sr, si):
        for step, k in enumerate((1, 2, 4)):
            mr, mi = tab[2 * step], tab[2 * step + 1]
            sh = (SUBLANES - k) if reverse else k
            rr = pltpu.roll(sr, sh, 0)
            ri = pltpu.roll(si, sh, 0)
            sr, si = sr + mr * rr - mi * ri, si + mr * ri + mi * rr
        return sr, si

    def body(n, c):
        cr, ci = c
        first = (n_tiles - group * (n + 1)) if reverse else group * n
        r0 = pl.multiple_of(first * SUBLANES, group * SUBLANES)
        rows = [pl.ds(r0 + g * SUBLANES, SUBLANES) for g in range(group)]
        tiles = [inside(re_ref[r, :], im_ref[r, :]) for r in rows]
        for g in (range(group - 1, -1, -1) if reverse else range(group)):
            sr, si = tiles[g]
            sr, si = sr + pr * cr - pi * ci, si + pr * ci + pi * cr
            re_ref[rows[g], :] = sr
            im_ref[rows[g], :] = si
            edge = slice(0, 1) if reverse else slice(SUBLANES - 1, SUBLANES)
            cr, ci = sr[edge, :], si[edge, :]
        return cr, ci

    return lax.fori_loop(0, n_tiles // group, body, carry)


def _ssm_fwd(u, bre, bim, cre, cim, dvec, tab, tt):
    l = u.shape[0]
    nt = l // tt

    def body(u_ref, bre_ref, bim_ref, cre_ref, cim_ref, d_ref, tab_ref, y_ref, sre_ref, sim_ref, car_ref):
        @pl.when(pl.program_id(1) == 0)
        def _():
            car_ref[...] = jnp.zeros_like(car_ref)

        uf = u_ref[...]
        ub = uf.astype(BF16)
        sre_ref[...] = _dot(ub, bre_ref[0])
        sim_ref[...] = _dot(ub, bim_ref[0])
        tab_v = [tab_ref[n] for n in range(8)]
        cr, ci = _scan_tiles(sre_ref, sim_ref, tab_v, (car_ref[0:1, :], car_ref[8:9, :]), tt // SUBLANES, False)
        car_ref[0:1, :] = cr
        car_ref[8:9, :] = ci
        y_ref[...] = (_dot(sre_ref[...].astype(BF16), cre_ref[0]) - _dot(sim_ref[...].astype(BF16), cim_ref[0])
                      + d_ref[...] * uf)

    return pl.pallas_call(
        body, name="ssm_fwd", grid=(SSM_NB, nt),
        in_specs=[pl.BlockSpec((tt, SSM_UB), lambda m, t: (t, m)),
                  pl.BlockSpec((1, SSM_UB, SSM_CB), lambda m, t: (m, 0, 0)),
                  pl.BlockSpec((1, SSM_UB, SSM_CB), lambda m, t: (m, 0, 0)),
                  pl.BlockSpec((1, SSM_CB, SSM_UB), lambda m, t: (m, 0, 0)),
                  pl.BlockSpec((1, SSM_CB, SSM_UB), lambda m, t: (m, 0, 0)),
                  pl.BlockSpec((1, SSM_UB), lambda m, t: (0, m)),
                  pl.BlockSpec((8, SUBLANES, SSM_CB), lambda m, t: (0, 0, m))],
        out_specs=[pl.BlockSpec((tt, SSM_UB), lambda m, t: (t, m)),
                   pl.BlockSpec((tt, SSM_CB), lambda m, t: (t, m)),
                   pl.BlockSpec((tt, SSM_CB), lambda m, t: (t, m))],
        out_shape=[jax.ShapeDtypeStruct((l, SSM_W), F32), jax.ShapeDtypeStruct((l, SSM_CH), F32),
                   jax.ShapeDtypeStruct((l, SSM_CH), F32)],
        scratch_shapes=[pltpu.VMEM((2 * SUBLANES, SSM_CB), F32)],
        compiler_params=_params(("parallel", "arbitrary")),
    )(u, bre, bim, cre, cim, dvec, tab)


def _ssm_bwd(dy, u, sre, sim, bre, bim, cre, cim, dvec, tab, tt):
    l = u.shape[0]
    nt = l // tt
    tpb = tt // SUBLANES

    def body(dy_ref, u_ref, sre_ref, sim_ref, hre_ref, him_ref, bre_ref, bim_ref, cre_ref, cim_ref, d_ref, tab_ref,
             du_ref, dbre_ref, dbim_ref, dcre_ref, dcim_ref, dare_ref, daim_ref, dd_ref, lr_ref, li_ref, car_ref):
        t = pl.program_id(1)

        @pl.when(t == 0)
        def _():
            car_ref[...] = jnp.zeros_like(car_ref)
            for ref in (dbre_ref, dbim_ref, dcre_ref, dcim_ref, dare_ref, daim_ref, dd_ref):
                ref[...] = jnp.zeros_like(ref)

        dyf = dy_ref[...]
        dyb = dyf.astype(BF16)
        uf = u_ref[...]
        s_re = sre_ref[...]
        s_im = sim_ref[...]
        lr_ref[...] = _dot_nt(dyb, cre_ref[0])
        li_ref[...] = -_dot_nt(dyb, cim_ref[0])
        dcre_ref[0] += _dot_tn(s_re.astype(BF16), dyb)
        dcim_ref[0] -= _dot_tn(s_im.astype(BF16), dyb)
        tab_v = [tab_ref[n] for n in range(8)]
        cr, ci = _scan_tiles(lr_ref, li_ref, tab_v, (car_ref[0:1, :], car_ref[8:9, :]), tpb, True)
        car_ref[0:1, :] = cr
        car_ref[8:9, :] = ci
        lam_r = lr_ref[...]
        lam_i = li_ref[...]
        keep = jnp.where(t == nt - 1, 0.0, 1.0)
        sp_r = _shift_down(s_re, 1, hre_ref[...] * keep)
        sp_i = _shift_down(s_im, 1, him_ref[...] * keep)
        dare_ref[...] += jnp.sum(lam_r * sp_r + lam_i * sp_i, axis=0, keepdims=True)
        daim_ref[...] += jnp.sum(lam_i * sp_r - lam_r * sp_i, axis=0, keepdims=True)
        lrb = lam_r.astype(BF16)
        lib = lam_i.astype(BF16)
        du_ref[...] = _dot_nt(lrb, bre_ref[0]) + _dot_nt(lib, bim_ref[0]) + dyf * d_ref[...]
        ub = uf.astype(BF16)
        dbre_ref[0] += _dot_tn(ub, lrb)
        dbim_ref[0] += _dot_tn(ub, lib)
        dd_ref[...] += jnp.sum(dyf * uf, axis=0, keepdims=True)

    rev = lambda m, t: (nt - 1 - t, m)
    halo = lambda m, t: (jnp.maximum((nt - 1 - t) * tpb - 1, 0), m)
    wb = pl.BlockSpec((1, SSM_UB, SSM_CB), lambda m, t: (m, 0, 0))
    wc = pl.BlockSpec((1, SSM_CB, SSM_UB), lambda m, t: (m, 0, 0))
    vec_c = pl.BlockSpec((1, SSM_CB), lambda m, t: (0, m))
    vec_u = pl.BlockSpec((1, SSM_UB), lambda m, t: (0, m))
    return pl.pallas_call(
        body, name="ssm_bwd", grid=(SSM_NB, nt),
        in_specs=[pl.BlockSpec((tt, SSM_UB), rev), pl.BlockSpec((tt, SSM_UB), rev),
                  pl.BlockSpec((tt, SSM_CB), rev), pl.BlockSpec((tt, SSM_CB), rev),
                  pl.BlockSpec((SUBLANES, SSM_CB), halo), pl.BlockSpec((SUBLANES, SSM_CB), halo),
                  wb, wb, wc, wc, vec_u,
                  pl.BlockSpec((8, SUBLANES, SSM_CB), lambda m, t: (0, 0, m))],
        out_specs=[pl.BlockSpec((tt, SSM_UB), rev), wb, wb, wc, wc, vec_c, vec_c, vec_u],
        out_shape=[jax.ShapeDtypeStruct((l, SSM_W), F32),
                   jax.ShapeDtypeStruct((SSM_NB, SSM_UB, SSM_CB), F32), jax.ShapeDtypeStruct((SSM_NB, SSM_UB, SSM_CB), F32),
                   jax.ShapeDtypeStruct((SSM_NB, SSM_CB, SSM_UB), F32), jax.ShapeDtypeStruct((SSM_NB, SSM_CB, SSM_UB), F32),
                   jax.ShapeDtypeStruct((1, SSM_CH), F32), jax.ShapeDtypeStruct((1, SSM_CH), F32),
                   jax.ShapeDtypeStruct((1, SSM_W), F32)],
        scratch_shapes=[pltpu.VMEM((tt, SSM_CB), F32), pltpu.VMEM((tt, SSM_CB), F32),
                        pltpu.VMEM((2 * SUBLANES, SSM_CB), F32)],
        compiler_params=_params(("parallel", "arbitrary")),
    )(dy, u, sre, sim, sre, sim, bre, bim, cre, cim, dvec, tab)


def _merge_fwd(x, gl, attn, y1, wba, wbs, wglu, bglu, wout, gpost, gpre, tl):
    l = x.shape[0]

    def body(x_ref, gl_ref, at_ref, y1_ref, wba_ref, wbs_ref, wglu_ref, bglu_ref, wout_ref, gpost_ref, gpre_ref,
             a_ref, sm_ref, mg_ref, z_ref, x1_ref, hn2_ref, y3_ref):
        y2 = _gelu(y1_ref[...])
        sg = _sigmoid(_dot(y2.astype(BF16), wglu_ref[...]) + bglu_ref[...])
        y3 = (y2 * sg).astype(BF16)
        y3_ref[...] = y3
        a = _dot(at_ref[...], wba_ref[...])
        sm = _dot(y3, wbs_ref[...])
        a_ref[...] = a
        sm_ref[...] = sm
        g = _sigmoid(gl_ref[...])
        merged = (g[:, :D_MODEL] * a + g[:, D_MODEL:] * sm).astype(BF16)
        mg_ref[...] = merged
        z = _dot(merged, wout_ref[...])
        z_ref[...] = z
        n, _ = _rms(z, gpost_ref[...])
        x1 = x_ref[...] + n
        x1_ref[...] = x1
        hn2, _ = _rms(x1, gpre_ref[...])
        hn2_ref[...] = hn2.astype(BF16)

    outs = [(D_MODEL, F32), (D_MODEL, F32), (D_MODEL, BF16), (D_MODEL, F32), (D_MODEL, F32), (D_MODEL, BF16),
            (SSM_W, BF16)]
    return pl.pallas_call(
        body, name="merge_fwd", grid=(l // tl,),
        in_specs=[_row(tl, D_MODEL), _row(tl, 2 * D_MODEL), _row(tl, HP), _row(tl, SSM_W),
                  _const((HP, D_MODEL)), _const((SSM_W, D_MODEL)), _const((SSM_W, SSM_W)), _const((1, SSM_W)),
                  _const((D_MODEL, D_MODEL)), _const((1, D_MODEL)), _const((1, D_MODEL))],
        out_specs=[_row(tl, n) for n, _ in outs],
        out_shape=[jax.ShapeDtypeStruct((l, n), dt) for n, dt in outs],
        compiler_params=_params(("parallel",)),
    )(x, gl, attn, y1, wba, wbs, wglu, bglu, wout, gpost, gpre)


def _merge_bwd(dhn2a, dhn2b, x1, dx2, z, gl, a, sm, y1, wba, wbs, wglu, bglu, wout, gpost, gpre, tl):
    l = x1.shape[0]

    def body(da_ref, db_ref, x1_ref, dx2_ref, z_ref, gl_ref, a_ref, sm_ref, y1_ref,
             wba_ref, wbs_ref, wglu_ref, bglu_ref, wout_ref, gpost_ref, gpre_ref,
             dx1_ref, dz_ref, dbra_ref, dbrs_ref, dgl_ref, dat_ref, dy1_ref, dt_ref, y2_ref,
             dgpre_ref, dgpost_ref, dbg_ref, dbglu_ref):
        @pl.when(pl.program_id(0) == 0)
        def _():
            for ref in (dgpre_ref, dgpost_ref, dbg_ref, dbglu_ref):
                ref[...] = jnp.zeros_like(ref)

        dhn2 = da_ref[...] + db_ref[...]
        dx1a, dgpre = _rms_bwd(dhn2, x1_ref[...], gpre_ref[...])
        dgpre_ref[...] += dgpre
        dx1 = dx2_ref[...] + dx1a
        dx1_ref[...] = dx1
        dz, dgpost = _rms_bwd(dx1, z_ref[...], gpost_ref[...])
        dgpost_ref[...] += dgpost
        dzb = dz.astype(BF16)
        dz_ref[...] = dzb
        dm = _dot_nt(dzb, wout_ref[...])
        g = _sigmoid(gl_ref[...])
        g0 = g[:, :D_MODEL]
        g1 = g[:, D_MODEL:]
        dbra = (dm * g0).astype(BF16)
        dbrs = (dm * g1).astype(BF16)
        dbra_ref[...] = dbra
        dbrs_ref[...] = dbrs
        dgl0 = dm * a_ref[...] * g0 * (1.0 - g0)
        dgl1 = dm * sm_ref[...] * g1 * (1.0 - g1)
        dgl_ref[:, :D_MODEL] = dgl0.astype(BF16)
        dgl_ref[:, D_MODEL:] = dgl1.astype(BF16)
        dbg_ref[:, :D_MODEL] += jnp.sum(dgl0, axis=0, keepdims=True)
        dbg_ref[:, D_MODEL:] += jnp.sum(dgl1, axis=0, keepdims=True)
        dat_ref[...] = _dot_nt(dbra, wba_ref[...]).astype(BF16)
        dy3 = _dot_nt(dbrs, wbs_ref[...])
        y1v = y1_ref[...]
        y2 = _gelu(y1v)
        y2b = y2.astype(BF16)
        y2_ref[...] = y2b
        sg = _sigmoid(_dot(y2b, wglu_ref[...]) + bglu_ref[...])
        dt = dy3 * y2 * sg * (1.0 - sg)
        dtb = dt.astype(BF16)
        dt_ref[...] = dtb
        dbglu_ref[...] += jnp.sum(dt, axis=0, keepdims=True)
        dy2 = dy3 * sg + _dot_nt(dtb, wglu_ref[...])
        dy1_ref[...] = dy2 * _gelu_grad(y1v)

    outs = [(D_MODEL, F32), (D_MODEL, BF16), (D_MODEL, BF16), (D_MODEL, BF16), (2 * D_MODEL, BF16), (HP, BF16),
            (SSM_W, F32), (SSM_W, BF16), (SSM_W, BF16)]
    accs = [D_MODEL, D_MODEL, 2 * D_MODEL, SSM_W]
    return pl.pallas_call(
        body, name="merge_bwd", grid=(l // tl,),
        in_specs=[_row(tl, D_MODEL), _row(tl, D_MODEL), _row(tl, D_MODEL), _row(tl, D_MODEL), _row(tl, D_MODEL),
                  _row(tl, 2 * D_MODEL), _row(tl, D_MODEL), _row(tl, D_MODEL), _row(tl, SSM_W),
                  _const((HP, D_MODEL)), _const((SSM_W, D_MODEL)), _const((SSM_W, SSM_W)), _const((1, SSM_W)),
                  _const((D_MODEL, D_MODEL)), _const((1, D_MODEL)), _const((1, D_MODEL))],
        out_specs=[_row(tl, n) for n, _ in outs] + [_const((1, n)) for n in accs],
        out_shape=[jax.ShapeDtypeStruct((l, n), dt) for n, dt in outs]
        + [jax.ShapeDtypeStruct((1, n), F32) for n in accs],
        compiler_params=_params(("arbitrary",)),
    )(dhn2a, dhn2b, x1, dx2, z, gl, a, sm, y1, wba, wbs, wglu, bglu, wout, gpost, gpre)


def _proj_bwd(x, dx1, cq, ckv, dq, dk, dv, du, dgl, g1, win, gq, wuq, gkv, wukv, rc, rs, tl):
    l = x.shape[0]

    def body(x_ref, dx1_ref, cq_ref, ckv_ref, dq_ref, dk_ref, dv_ref, du_ref, dgl_ref,
             g1_ref, win_ref, gq_ref, wuq_ref, gkv_ref, wukv_ref, rc_ref, rs_ref,
             gx_ref, dql_ref, qn_ref, ckvn_ref, dproj_ref, dg1_ref, dgq_ref, dgkv_ref):
        @pl.when(pl.program_id(0) == 0)
        def _():
            for ref in (dg1_ref, dgq_ref, dgkv_ref):
                ref[...] = jnp.zeros_like(ref)

        c1 = rc_ref[...]
        s1 = rs_ref[...]
        dql = _rope_bwd(dq_ref[...], jnp.tile(c1, (1, N_HEADS)), jnp.tile(s1, (1, N_HEADS))).astype(BF16)
        dql_ref[...] = dql
        dqn = _dot_nt(dql, wuq_ref[...])
        cq = cq_ref[...]
        qn, _ = _rms(cq, gq_ref[...])
        qn_ref[...] = qn.astype(BF16)
        dcq, dgq = _rms_bwd(dqn, cq, gq_ref[...])
        dgq_ref[...] += dgq
        dkb = dk_ref[...]
        dvb = dv_ref[...]
        dkf = dkb.astype(F32)
        dkr = dkf[:, 0:HEAD_SLOT]
        for h in range(1, N_HEADS):
            dkr = dkr + dkf[:, h * HEAD_SLOT:(h + 1) * HEAD_SLOT]
        dkr = _rope_bwd(dkr, c1, s1)
        dckvn = _dot_nt(dkb, wukv_ref[:, :HP]) + _dot_nt(dvb, wukv_ref[:, HP:])
        ckv = ckv_ref[...]
        ckvn, _ = _rms(ckv, gkv_ref[...])
        ckvn_ref[...] = ckvn.astype(BF16)
        dckv, dgkv = _rms_bwd(dckvn, ckv, gkv_ref[...])
        dgkv_ref[...] += dgkv
        dproj_ref[:, P_CQ:P_CKV] = dcq.astype(BF16)
        dproj_ref[:, P_CKV:P_KR] = dckv.astype(BF16)
        dproj_ref[:, P_KR:P_U] = dkr.astype(BF16)
        dproj_ref[:, P_U:P_GL] = du_ref[...].astype(BF16)
        dproj_ref[:, P_GL:P_END] = dgl_ref[...]
        dhn = _dot_nt(dproj_ref[...], win_ref[...])
        dxa, dg1 = _rms_bwd(dhn, x_ref[...], g1_ref[...])
        dg1_ref[...] += dg1
        gx_ref[...] = dx1_ref[...] + dxa

    outs = [(D_MODEL, F32), (HP, BF16), (Q_RANK, BF16), (KV_RANK, BF16), (P_END, BF16)]
    accs = [D_MODEL, Q_RANK, KV_RANK]
    return pl.pallas_call(
        body, name="proj_bwd", grid=(l // tl,),
        in_specs=[_row(tl, D_MODEL), _row(tl, D_MODEL), _row(tl, Q_RANK), _row(tl, KV_RANK), _row(tl, HP),
                  _row(tl, HP), _row(tl, HP), _row(tl, SSM_W), _row(tl, 2 * D_MODEL),
                  _const((1, D_MODEL)), _const((D_MODEL, P_END)), _const((1, Q_RANK)), _const((Q_RANK, HP)),
                  _const((1, KV_RANK)), _const((KV_RANK, 2 * HP)), _row(tl, HEAD_SLOT), _row(tl, HEAD_SLOT)],
        out_specs=[_row(tl, n) for n, _ in outs] + [_const((1, n)) for n in accs],
        out_shape=[jax.ShapeDtypeStruct((l, n), dt) for n, dt in outs]
        + [jax.ShapeDtypeStruct((1, n), F32) for n in accs],
        compiler_params=_params(("arbitrary",)),
    )(x, dx1, cq, ckv, dq, dk, dv, du, dgl, g1, win, gq, wuq, gkv, wukv, rc, rs)


CONV_CB = 256
CONV_NB = D_FF // CONV_CB
CONV_ROWS = 16


def _conv3(h, halo, w, b):
    return b + w[0:1, :] * _shift_down(h, 2, halo) + w[1:2, :] * _shift_down(h, 1, halo) + w[2:3, :] * h


def _conv_fwd(h, cw, cb, tl):
    l = h.shape[0]

    def body(hg_ref, hv_ref, wg_ref, wv_ref, bg_ref, bv_ref, act_ref, halo_ref):
        @pl.when(pl.program_id(1) == 0)
        def _():
            halo_ref[...] = jnp.zeros_like(halo_ref)

        hg = hg_ref[...]
        hv = hv_ref[...]
        cg = _conv3(hg, halo_ref[0:SUBLANES, :], wg_ref[...], bg_ref[...])
        cv = _conv3(hv, halo_ref[SUBLANES:, :], wv_ref[...], bv_ref[...])
        act_ref[...] = (_gelu(cg) * cv).astype(BF16)
        halo_ref[0:SUBLANES, :] = hg[tl - SUBLANES:, :]
        halo_ref[SUBLANES:, :] = hv[tl - SUBLANES:, :]

    gmap = lambda c, r: (r, c)
    vmap = lambda c, r: (r, CONV_NB + c)
    return pl.pallas_call(
        body, name="conv_fwd", grid=(CONV_NB, l // tl),
        in_specs=[pl.BlockSpec((tl, CONV_CB), gmap), pl.BlockSpec((tl, CONV_CB), vmap),
                  pl.BlockSpec((3, CONV_CB), lambda c, r: (0, c)), pl.BlockSpec((3, CONV_CB), lambda c, r: (0, CONV_NB + c)),
                  pl.BlockSpec((1, CONV_CB), lambda c, r: (0, c)), pl.BlockSpec((1, CONV_CB), lambda c, r: (0, CONV_NB + c))],
        out_specs=pl.BlockSpec((tl, CONV_CB), gmap),
        out_shape=jax.ShapeDtypeStruct((l, D_FF), BF16),
        scratch_shapes=[pltpu.VMEM((2 * SUBLANES, CONV_CB), F32)],
        compiler_params=_params(("parallel", "arbitrary")),
    )(h, h, cw, cw, cb, cb)


def _conv_bwd(h, dact, cw, cb, tl):
    l = h.shape[0]
    nr = l // tl
    tpb = tl // SUBLANES

    def body(hg_ref, hv_ref, hgh_ref, hvh_ref, da_ref, wg_ref, wv_ref, bg_ref, bv_ref,
             dh_ref, dwg_ref, dwv_ref, dbg_ref, dbv_ref, car_ref):
        r = pl.program_id(1)

        @pl.when(r == 0)
        def _():
            for ref in (car_ref, dwg_ref, dwv_ref, dbg_ref, dbv_ref):
                ref[...] = jnp.zeros_like(ref)

        keep = jnp.where(r == nr - 1, 0.0, 1.0)
        wg, wv, bg, bv = wg_ref[...], wv_ref[...], bg_ref[...], bv_ref[...]
        nch = tl // CONV_ROWS

        def fold(x):
            s = x[0:SUBLANES, :]
            for k in range(1, CONV_ROWS // SUBLANES):
                s = s + x[k * SUBLANES:(k + 1) * SUBLANES, :]
            return s

        def chunk(n, carry):
            ncg, ncv, acc = carry
            idx = nch - 1 - n
            r0 = pl.multiple_of(idx * CONV_ROWS, CONV_ROWS)
            rows = pl.ds(r0, CONV_ROWS)
            before = pl.ds(pl.multiple_of(jnp.maximum(r0 - SUBLANES, 0), SUBLANES), SUBLANES)
            in_tile = idx > 0
            da = da_ref[rows, :].astype(F32)

            def half(h_ref, halo_ref, w, b):
                hh = h_ref[rows, :]
                prev = jnp.where(in_tile, h_ref[before, :], halo_ref[...] * keep)
                h1 = _shift_down(hh, 1, prev)
                h2 = _shift_down(hh, 2, prev)
                return hh, h1, h2, b + w[0:1, :] * h2 + w[1:2, :] * h1 + w[2:3, :] * hh

            hg, hg1, hg2, cg = half(hg_ref, hgh_ref, wg, bg)
            hv, hv1, hv2, cv = half(hv_ref, hvh_ref, wv, bv)
            dcg = da * cv * _gelu_grad(cg)
            dcv = da * _gelu(cg)

            def back(dc, hh, h1, h2, w, nxt, part):
                dh = w[2:3, :] * dc + w[1:2, :] * _shift_up(dc, 1, nxt) + w[0:1, :] * _shift_up(dc, 2, nxt)
                dh_ref[part, rows, :] = dh.astype(BF16)
                return [fold(dc * h2), fold(dc * h1), fold(dc * hh), fold(dc)]

            sums = back(dcg, hg, hg1, hg2, wg, ncg, 0) + back(dcv, hv, hv1, hv2, wv, ncv, 1)
            return dcg[0:SUBLANES, :], dcv[0:SUBLANES, :], [a + s for a, s in zip(acc, sums)]

        zero = jnp.zeros((SUBLANES, CONV_CB), F32)
        ncg, ncv, acc = lax.fori_loop(0, nch, chunk, (car_ref[0:SUBLANES, :], car_ref[SUBLANES:, :], [zero] * 8))
        car_ref[0:SUBLANES, :] = ncg
        car_ref[SUBLANES:, :] = ncv
        for half_acc, dw_ref, db_ref in ((acc[0:4], dwg_ref, dbg_ref), (acc[4:8], dwv_ref, dbv_ref)):
            for k in range(3):
                dw_ref[k:k + 1, :] += jnp.sum(half_acc[k], axis=0, keepdims=True)
            db_ref[...] += jnp.sum(half_acc[3], axis=0, keepdims=True)

    grev = lambda c, r: (nr - 1 - r, c)
    vrev = lambda c, r: (nr - 1 - r, CONV_NB + c)
    ghalo = lambda c, r: (jnp.maximum((nr - 1 - r) * tpb - 1, 0), c)
    vhalo = lambda c, r: (jnp.maximum((nr - 1 - r) * tpb - 1, 0), CONV_NB + c)
    colg = lambda c, r: (0, c)
    colv = lambda c, r: (0, CONV_NB + c)
    return pl.pallas_call(
        body, name="conv_bwd", grid=(CONV_NB, nr),
        in_specs=[pl.BlockSpec((tl, CONV_CB), grev), pl.BlockSpec((tl, CONV_CB), vrev),
                  pl.BlockSpec((SUBLANES, CONV_CB), ghalo), pl.BlockSpec((SUBLANES, CONV_CB), vhalo),
                  pl.BlockSpec((tl, CONV_CB), grev),
                  pl.BlockSpec((3, CONV_CB), colg), pl.BlockSpec((3, CONV_CB), colv),
                  pl.BlockSpec((1, CONV_CB), colg), pl.BlockSpec((1, CONV_CB), colv)],
        out_specs=[pl.BlockSpec((2, tl, CONV_CB), lambda c, r: (0, nr - 1 - r, c)),
                   pl.BlockSpec((3, CONV_CB), colg), pl.BlockSpec((3, CONV_CB), colg),
                   pl.BlockSpec((1, CONV_CB), colg), pl.BlockSpec((1, CONV_CB), colg)],
        out_shape=[jax.ShapeDtypeStruct((2, l, D_FF), BF16),
                   jax.ShapeDtypeStruct((3, D_FF), F32), jax.ShapeDtypeStruct((3, D_FF), F32),
                   jax.ShapeDtypeStruct((1, D_FF), F32), jax.ShapeDtypeStruct((1, D_FF), F32)],
        scratch_shapes=[pltpu.VMEM((2 * SUBLANES, CONV_CB), F32)],
        compiler_params=_params(("parallel", "arbitrary")),
    )(h, h, h, h, dact, cw, cw, cb, cb)


def _loss_head(ff, x1, tgt, g, tl):
    l = ff.shape[0]

    def body(ff_ref, x1_ref, tg_ref, g_ref, loss_ref, dx2_ref, dff_ref, dg_ref):
        @pl.when(pl.program_id(0) == 0)
        def _():
            loss_ref[...] = jnp.zeros_like(loss_ref)
            dg_ref[...] = jnp.zeros_like(dg_ref)

        f = ff_ref[...]
        gv = g_ref[...]
        n, _ = _rms(f, gv)
        e = x1_ref[...] + n - tg_ref[...]
        loss_ref[...] += 0.5 * jnp.sum(jnp.mean(e * e, axis=-1, keepdims=True), axis=0, keepdims=True)
        dx2 = e * (1.0 / D_MODEL)
        dx2_ref[...] = dx2
        dff, dg = _rms_bwd(dx2, f, gv)
        dff_ref[...] = dff.astype(BF16)
        dg_ref[...] += dg

    return pl.pallas_call(
        body, name="loss_head", grid=(l // tl,),
        in_specs=[_row(tl, D_MODEL), _row(tl, D_MODEL), _row(tl, D_MODEL), _const((1, D_MODEL))],
        out_specs=[_const((1, LANES)), _row(tl, D_MODEL), _row(tl, D_MODEL), _const((1, D_MODEL))],
        out_shape=[jax.ShapeDtypeStruct((1, LANES), F32), jax.ShapeDtypeStruct((l, D_MODEL), F32),
                   jax.ShapeDtypeStruct((l, D_MODEL), BF16), jax.ShapeDtypeStruct((1, D_MODEL), F32)],
        compiler_params=_params(("arbitrary",)),
    )(ff, x1, tgt, g)


def _ssm_disc(lam_re, lam_im, log_dt, b_re, b_im):
    dt = jnp.exp(log_dt)[:, None]
    mag = jnp.exp(lam_re * dt)
    ang = lam_im * dt
    a_re, a_im = mag * jnp.cos(ang), mag * jnp.sin(ang)
    den = lam_re * lam_re + lam_im * lam_im
    n_re, n_im = a_re - 1.0, a_im
    z_re = (n_re * lam_re + n_im * lam_im) / den
    z_im = (n_im * lam_re - n_re * lam_im) / den
    bb_re = z_re[..., None] * b_re - z_im[..., None] * b_im
    bb_im = z_re[..., None] * b_im + z_im[..., None] * b_re
    return a_re, a_im, bb_re, bb_im


_GPB = SSM_CB // SSM_P


def _embed_b(bb):
    t = bb.transpose(0, 2, 1).reshape(SSM_NB, _GPB, SSM_H, SSM_P)
    return jnp.einsum('mjhp,jk->mjhkp', t, jnp.eye(_GPB, dtype=bb.dtype)).reshape(SSM_NB, SSM_UB, SSM_CB)


def _extract_b(d):
    t = d.reshape(SSM_NB, _GPB, SSM_H, _GPB, SSM_P)
    t = jnp.einsum('mjhkp,jk->mjhp', t, jnp.eye(_GPB, dtype=d.dtype))
    return t.reshape(SSM_G, SSM_H, SSM_P).transpose(0, 2, 1)


def _embed_c(c):
    t = c.transpose(0, 2, 1).reshape(SSM_NB, _GPB, SSM_P, SSM_H)
    return jnp.einsum('mjph,jk->mjpkh', t, jnp.eye(_GPB, dtype=c.dtype)).reshape(SSM_NB, SSM_CB, SSM_UB)


def _extract_c(d):
    t = d.reshape(SSM_NB, _GPB, SSM_P, _GPB, SSM_H)
    t = jnp.einsum('mjpkh,jk->mjph', t, jnp.eye(_GPB, dtype=d.dtype))
    return t.reshape(SSM_G, SSM_P, SSM_H).transpose(0, 2, 1)


def _scan_tables(a_re, a_im, reverse):
    ar = a_re.reshape(1, SSM_CH)
    ai = (-a_im if reverse else a_im).reshape(1, SSM_CH)
    pr, pi = [ar], [ai]
    for _ in range(SUBLANES - 1):
        pr, pi = pr + [pr[-1] * ar - pi[-1] * ai], pi + [pr[-1] * ai + pi[-1] * ar]
    rows = jnp.arange(SUBLANES)[:, None]
    out = []
    for k in (1, 2, 4):
        valid = (rows + k <= SUBLANES - 1) if reverse else (rows >= k)
        out += [jnp.where(valid, pr[k - 1], 0.0), jnp.where(valid, pi[k - 1], 0.0)]
    order = list(range(SUBLANES - 1, -1, -1)) if reverse else list(range(SUBLANES))
    out += [jnp.concatenate([pr[n] for n in order], axis=0), jnp.concatenate([pi[n] for n in order], axis=0)]
    return jnp.stack(out).astype(F32)


def _pad_heads(w, d):
    lead = w.shape[:-1]
    w = w.reshape(lead + (N_HEADS, d))
    w = jnp.pad(w, [(0, 0)] * len(lead) + [(0, 0), (0, HEAD_SLOT - d)])
    return w.reshape(lead + (HP,))


def _unpad_heads(w, d):
    lead = w.shape[:-1]
    return w.reshape(lead + (N_HEADS, HEAD_SLOT))[..., :d].reshape(lead + (N_HEADS * d,))


def _chip_major(w, axis):
    k, n = w.shape
    if axis == 0:
        return w.reshape(N_CHIPS, k // N_CHIPS, n)
    return w.reshape(k, N_CHIPS, n // N_CHIPS).transpose(1, 0, 2)


def _from_chip_major(w, axis):
    if axis == 0:
        return w.reshape(-1, w.shape[2])
    return w.transpose(1, 0, 2).reshape(w.shape[1], -1)


def _pad_w_in(w):
    z = lambda n: jnp.zeros((w.shape[0], n), w.dtype)
    return jnp.concatenate([w[:, :640], z(KR_LANE), w[:, 640:672], z(HEAD_SLOT - KR_LANE - QK_ROPE), w[:, 672:]], axis=1)


def _unpad_w_in(w):
    return jnp.concatenate([w[:, :640], w[:, P_KR + KR_LANE:P_KR + KR_LANE + QK_ROPE], w[:, P_U:]], axis=1)


def _local_step(x, positions, tgt, wts, sp):
    l = x.shape[0]
    tl = min(256, l)
    ta = min(512, l)

    inv_freq = ROPE_THETA ** (-jnp.arange(0, QK_ROPE, 2, dtype=F32) / QK_ROPE)
    ang = positions.astype(F32)[:, None] * inv_freq
    cos, sin = jnp.cos(ang), jnp.sin(ang)
    one = jnp.ones((l, KR_LANE), F32)
    rc = jnp.concatenate([one, cos, cos, jnp.ones((l, HEAD_SLOT - KR_LANE - QK_ROPE), F32)], axis=1)
    rs = jnp.concatenate([0 * one, -sin, sin, jnp.zeros((l, HEAD_SLOT - KR_LANE - QK_ROPE), F32)], axis=1)

    win = _pad_w_in(wts["w_in"])
    wuq = _pad_heads(wts["w_uq"], QK_HEAD)
    wukv = jnp.concatenate([_pad_heads(wts["w_uk"], QK_NOPE), _pad_heads(wts["w_uv"], V_HEAD)], axis=1)

    disc_in = (sp["ssm_lambda_re"], sp["ssm_lambda_im"], sp["ssm_log_dt"], sp["ssm_b_re"], sp["ssm_b_im"])
    (a_re, a_im, bb_re, bb_im), disc_vjp = jax.vjp(_ssm_disc, *disc_in)
    bre, bim = _embed_b(bb_re).astype(BF16), _embed_b(bb_im).astype(BF16)
    cre, cim = _embed_c(sp["ssm_c_re"]).astype(BF16), _embed_c(sp["ssm_c_im"]).astype(BF16)
    dvec = sp["ssm_d"].reshape(1, SSM_W)
    tab_f = _scan_tables(a_re, a_im, False)
    tab_r = _scan_tables(a_re, a_im, True)

    g1, gq, gkv = sp["mix_norm_pre"], sp["q_norm"], sp["kv_norm"]
    gpost, gpre, gfin = sp["mix_norm_post"], sp["ffn_norm_pre"], sp["ffn_norm_post"]
    bgate, bglu, convb = sp["b_gate"], sp["b_glu"], sp["conv_b"]

    hn, cq, ckv, q, k, v, u, gl = _proj_fwd(x, g1, win, gq, wuq, gkv, wukv, rc, rs, bgate, tl)
    attn, lse = _attn_fwd(q, k, v, ta)
    y1, sre, sim = _ssm_fwd(u, bre, bim, cre, cim, dvec, tab_f, ta)
    wba = jnp.pad(wts["w_branch_attn"].reshape(N_HEADS, V_HEAD, D_MODEL),
                  ((0, 0), (0, HEAD_SLOT - V_HEAD), (0, 0))).reshape(HP, D_MODEL)
    wbs, wglu, wout = wts["w_branch_ssm"], wts["w_glu"], wts["w_out"]
    a, sm, merged, z, x1, hn2, y3 = _merge_fwd(x, gl, attn, y1, wba, wbs, wglu, bglu, wout, gpost, gpre, tl)
    late = wts["late"](x1)
    wup, wdown, convw = late["w_up"], late["w_down"], late["conv_w"]
    h = _mm(hn2, wup, "ffn_up")
    act = _conv_fwd(h, convw, convb, ta)
    ff = _mm(act, wdown, "ffn_down")
    loss, dx2, dff, dgfin = _loss_head(ff, x1, tgt, gfin, tl)

    dact = _mm(dff, wdown, "ffn_down_dx", out_dtype=BF16, bt=True)
    d_wdown = _mm_tn(act, dff, "ffn_down_dw", tk_cap=D_FF // 2)
    dh, dwg, dwv, dbg, dbv = _conv_bwd(h, dact, convw, convb, ta)
    d_convw = jnp.concatenate([dwg, dwv], axis=1)
    d_convb = jnp.concatenate([dbg, dbv], axis=1)
    dhn2a = _mm(dh, wup, "ffn_up_dx_gate", bt=True, b_col0=0, a_lead=0)
    dhn2b = _mm(dh, wup, "ffn_up_dx_val", bt=True, b_col0=1, a_lead=1)
    d_wup = _mm_tn(hn2, dh, "ffn_up_dw", chips=True)
    behind = wts["send_grads"]("ffn", {"w_up": d_wup, "w_down": _chip_major(d_wdown, 0)})
    (dx1, dz, dbra, dbrs, dgl, dattn, dy1, dt, y2, dgpre, dgpost, dbgate, dbglu) = _merge_bwd(
        dhn2a, dhn2b, x1, dx2, z, gl, a, sm, y1, wba, wbs, wglu, bglu, wout, gpost, gpre + behind, tl)
    d_wout = _mm_tn(merged, dz, "w_out_dw")
    d_wba = _mm_tn(attn, dbra, "w_branch_attn_dw", chips=True)
    d_wbs = _mm_tn(y3, dbrs, "w_branch_ssm_dw", chips=True)
    d_wglu = _mm_tn(y2, dt, "w_glu_dw")
    ncol = D_MODEL // N_CHIPS
    behind = wts["send_grads"]("mix", {
        "w_glu": _chip_major(d_wglu, 0),
        "w_branch_attn": d_wba.reshape(N_CHIPS, N_HEADS, HEAD_SLOT, ncol)[:, :, :V_HEAD].reshape(
            N_CHIPS, N_HEADS * V_HEAD, ncol),
        "w_branch_ssm": d_wbs,
        "w_out": _chip_major(d_wout, 0)})
    dq, dk, dv = _attn_bwd(q, k, v, dattn, lse + behind, _attn_delta(attn, dattn, min(2048, l)), ta)
    du, dbre, dbim, dcre, dcim, dare, daim, dd = _ssm_bwd(dy1, u, sre, sim, bre, bim, cre, cim, dvec, tab_r, ta)
    gx, dql, qn, ckvn, dproj, dg1, dgq, dgkv = _proj_bwd(
        x, dx1, cq, ckv, dq, dk, dv, du, dgl, g1, win, gq, wuq, gkv, wukv, rc, rs, tl)
    d_win = _mm_tn(hn, dproj, "w_in_dw")
    d_wuq = _mm_tn(qn, dql, "w_uq_dw")
    d_wuk = _mm_tn(ckvn, dk, "w_uk_dw")
    d_wuv = _mm_tn(ckvn, dv, "w_uv_dw")

    d_lre, d_lim, d_ldt, d_bre, d_bim = disc_vjp((dare.reshape(SSM_G, SSM_P), daim.reshape(SSM_G, SSM_P),
                                                  _extract_b(dbre), _extract_b(dbim)))
    big = {
        "w_in": _chip_major(_unpad_w_in(d_win), 1),
        "w_uq": _chip_major(_unpad_heads(d_wuq, QK_HEAD), 1),
        "w_uk": _chip_major(_unpad_heads(d_wuk, QK_NOPE), 1),
        "w_uv": _chip_major(_unpad_heads(d_wuv, V_HEAD), 1),
    }
    small = {
        "conv_w": d_convw,
        "mix_norm_pre": dg1, "q_norm": dgq, "kv_norm": dgkv,
        "ssm_lambda_re": d_lre, "ssm_lambda_im": d_lim, "ssm_log_dt": d_ldt,
        "ssm_b_re": d_bre, "ssm_b_im": d_bim,
        "ssm_c_re": _extract_c(dcre), "ssm_c_im": _extract_c(dcim),
        "ssm_d": dd.reshape(SSM_G, SSM_H), "b_glu": dbglu, "b_gate": dbgate,
        "mix_norm_post": dgpost, "ffn_norm_pre": dgpre, "conv_b": d_convb, "ffn_norm_post": dgfin,
    }
    return loss[0, 0], gx, big, small


_ANY = pl.BlockSpec(memory_space=pl.ANY)


ROW_TILE = 16


def _place():
    x, y, c = lax.axis_index("x"), lax.axis_index("y"), lax.axis_index("c")
    return x, y, c, 2 * x + y, [(1 - x, y), (x, 1 - y), (1 - x, 1 - y)]


def _half(rows, which):
    hr = rows // 2
    return pl.ds(pl.multiple_of(which * hr, ROW_TILE), hr)


def _remote(src, dst, send_sems, recv_sems, n, dev):
    return pltpu.make_async_remote_copy(src_ref=src, dst_ref=dst, send_sem=send_sems.at[n], recv_sem=recv_sems.at[n],
                                        device_id=dev, device_id_type=MESH)


def _gather_big(shards):
    nw = len(shards)
    rows = [s.shape[0] for s in shards]

    def body(*refs):
        ins, outs = refs[:nw], refs[nw:2 * nw]
        ici_send, ici_recv, d2d_send, d2d_recv = refs[2 * nw:]
        x, y, c, me, peers = _place()
        sent = []
        for i in range(nw):
            for p, (px, py) in enumerate(peers):
                cp = _remote(ins[i].at[_half(rows[i], c)], outs[i].at[me, _half(rows[i], c)], ici_send, ici_recv,
                             3 * i + p, (px, py, c))
                cp.start()
                sent.append(cp)
        for p, (px, py) in enumerate(peers):
            for i in range(nw):
                blk = outs[i].at[2 * px + py, _half(rows[i], c)]
                _remote(blk, blk, ici_send, ici_recv, 3 * i + p, (px, py, c)).wait_recv()
                cp = _remote(blk, blk, d2d_send, d2d_recv, 3 * i + p, (x, y, 1 - c))
                cp.start()
                sent.append(cp)
        for p, (px, py) in enumerate(peers):
            for i in range(nw):
                blk = outs[i].at[2 * px + py, _half(rows[i], 1 - c)]
                _remote(blk, blk, d2d_send, d2d_recv, 3 * i + p, (x, y, 1 - c)).wait_recv()
        for cp in sent:
            cp.wait_send()

    dma = pltpu.SemaphoreType.DMA
    return pl.pallas_call(
        body, name="gather_weights", in_specs=[_ANY] * nw, out_specs=[_ANY] * nw,
        out_shape=[jax.ShapeDtypeStruct((N_CHIPS,) + s.shape, s.dtype) for s in shards],
        scratch_shapes=[dma((3 * nw,)), dma((3 * nw,)), dma((3 * nw,)), dma((3 * nw,))],
    )(*shards)


_HBM = pl.BlockSpec(memory_space=pltpu.HBM)
_SEM = pl.BlockSpec(memory_space=pltpu.SEMAPHORE)
_DATAFLOW = pltpu.SideEffectType.DATAFLOW_SIDE_EFFECTING


def _exchange_start(shards, name, scatter):
    nw = len(shards)
    lands = [lax.empty(s.shape if scatter else (N_CHIPS,) + s.shape, s.dtype) for s in shards]

    def body(*refs):
        ins, zones = refs[:nw], refs[nw:2 * nw]
        send_sems, recv_sems, token = refs[2 * nw], refs[2 * nw + 1], refs[-1]
        x, y, c, me, peers = _place()
        for i in range(nw):
            for p, (px, py) in enumerate(peers):
                src = ins[i].at[2 * px + py] if scatter else ins[i]
                _remote(src, zones[i].at[me], send_sems, recv_sems, 3 * i + p, (px, py, c)).start()
        token[...] = jnp.zeros_like(token)

    thru = [pltpu.HBM(a.shape, a.dtype) for a in list(shards) + lands]
    dma = pltpu.SemaphoreType.DMA
    outs = pl.pallas_call(
        body, name=name,
        out_shape=(dma((3 * nw,)), dma((3 * nw,)), *thru, jax.ShapeDtypeStruct((SUBLANES, LANES), F32)),
        in_specs=[_HBM] * (2 * nw),
        out_specs=(_SEM, _SEM, *([_HBM] * (2 * nw)), pl.BlockSpec(memory_space=pltpu.VMEM)),
        input_output_aliases={i: 2 + i for i in range(2 * nw)},
        compiler_params=pltpu.CompilerParams(has_side_effects=_DATAFLOW),
    )(*[pltpu.with_memory_space_constraint(a, pltpu.HBM) for a in list(shards) + lands])
    return outs[0], outs[1], list(outs[2:2 + nw]), list(outs[2 + nw:2 + 2 * nw]), outs[-1]


def _exchange_wait(send_sems, recv_sems, shards, lands, after, name, scatter):
    nw = len(shards)

    def body(*refs):
        ins, zones = refs[:nw], refs[nw:2 * nw]
        send_sems, recv_sems = refs[2 * nw], refs[2 * nw + 1]
        x, y, c, me, peers = _place()
        for i in range(nw):
            for p, (px, py) in enumerate(peers):
                src = ins[i].at[2 * px + py] if scatter else ins[i]
                cp = _remote(src, zones[i].at[2 * px + py], send_sems, recv_sems, 3 * i + p, (px, py, c))
                cp.wait_send()
                cp.wait_recv()

    both = list(shards) + list(lands)
    outs = pl.pallas_call(
        body, name=name,
        out_shape=tuple(pltpu.HBM(a.shape, a.dtype) for a in both),
        in_specs=(*([_HBM] * (2 * nw)), _SEM, _SEM, _ANY), out_specs=[_HBM] * (2 * nw),
        input_output_aliases={i: i for i in range(2 * nw)},
        compiler_params=pltpu.CompilerParams(has_side_effects=_DATAFLOW),
    )(*both, send_sems, recv_sems, after)
    return list(outs[:nw]), list(outs[nw:])


def _reduce_to_sibling(grads, name):
    nw = len(grads)

    def body(*refs):
        ins, outs = refs[:nw], refs[nw:2 * nw]
        send_sems, recv_sems = refs[2 * nw:]
        x, y, c, _, _ = _place()
        sent = []
        for i in range(nw):
            cp = _remote(ins[i].at[pl.ds(0, N_CHIPS), _half(grads[i].shape[1], 1 - c)], outs[i], send_sems, recv_sems,
                         i, (x, y, 1 - c))
            cp.start()
            sent.append(cp)
        for cp in sent:
            cp.wait()

    dma = pltpu.SemaphoreType.DMA
    return pl.pallas_call(
        body, name=name, in_specs=[_ANY] * nw, out_specs=[_ANY] * nw,
        out_shape=[jax.ShapeDtypeStruct((N_CHIPS, g.shape[1] // 2, g.shape[2]), g.dtype) for g in grads],
        scratch_shapes=[dma((nw,)), dma((nw,))],
    )(*grads)


def _reduce_between_chips(pairs):
    nw = len(pairs)

    def body(*refs):
        ins, outs = refs[:nw], refs[nw:2 * nw]
        send_sems, recv_sems = refs[2 * nw:]
        x, y, c, me, peers = _place()
        sent = []
        for i in range(nw):
            for p, (px, py) in enumerate(peers):
                cp = _remote(ins[i].at[2 * px + py], outs[i].at[me], send_sems, recv_sems, 3 * i + p, (px, py, c))
                cp.start()
                sent.append(cp)
        for i in range(nw):
            for p, (px, py) in enumerate(peers):
                blk = outs[i].at[2 * px + py]
                _remote(blk, blk, send_sems, recv_sems, 3 * i + p, (px, py, c)).wait_recv()
        for cp in sent:
            cp.wait_send()

    dma = pltpu.SemaphoreType.DMA
    return pl.pallas_call(
        body, name="reduce_grads_ici", in_specs=[_ANY] * nw, out_specs=[_ANY] * nw,
        out_shape=[jax.ShapeDtypeStruct(p.shape, p.dtype) for p in pairs],
        scratch_shapes=[dma((3 * nw,)), dma((3 * nw,))],
    )(*pairs)


def _reduce_back(totals):
    nw = len(totals)

    def body(*refs):
        outs = refs[nw:2 * nw]
        send_sems, recv_sems = refs[2 * nw:]
        x, y, c, _, _ = _place()
        sent = []
        for i in range(nw):
            blk = outs[i].at[_half(totals[i].shape[0], c)]
            cp = _remote(blk, blk, send_sems, recv_sems, i, (x, y, 1 - c))
            cp.start()
            sent.append(cp)
        for i in range(nw):
            blk = outs[i].at[_half(totals[i].shape[0], 1 - c)]
            _remote(blk, blk, send_sems, recv_sems, i, (x, y, 1 - c)).wait_recv()
        for cp in sent:
            cp.wait_send()

    dma = pltpu.SemaphoreType.DMA
    return pl.pallas_call(
        body, name="reduce_grads_back", in_specs=[_ANY] * nw, out_specs=[_ANY] * nw,
        out_shape=[jax.ShapeDtypeStruct(t.shape, t.dtype) for t in totals],
        input_output_aliases={i: i for i in range(nw)},
        scratch_shapes=[dma((nw,)), dma((nw,))],
    )(*totals)


def _all_reduce_small(v, name):
    rows, w = v.shape
    hr = rows // 2
    assert hr % SUBLANES == 0

    def body(v_ref, out_ref, sib_ref, half_ref, chips_ref, send_sems, recv_sems):
        x, y, c, me, peers = _place()
        sibling = (x, y, 1 - c)
        mine = pl.ds(pl.multiple_of(c * hr, SUBLANES), hr)
        other = pl.ds(pl.multiple_of((1 - c) * hr, SUBLANES), hr)
        cp = _remote(v_ref, sib_ref, send_sems, recv_sems, 0, sibling)
        cp.start()
        cp.wait()
        half_ref[...] = v_ref[mine, :] + sib_ref[mine, :]
        sent = []
        for p, (px, py) in enumerate(peers):
            cp = _remote(half_ref, chips_ref.at[me], send_sems, recv_sems, 1 + p, (px, py, c))
            cp.start()
            sent.append(cp)
        chips_ref[me] = half_ref[...]
        for p, (px, py) in enumerate(peers):
            _remote(half_ref, chips_ref.at[2 * px + py], send_sems, recv_sems, 1 + p, (px, py, c)).wait_recv()
        for cp in sent:
            cp.wait_send()
        out_ref[mine, :] = ((chips_ref[0] + chips_ref[1]) + chips_ref[2]) + chips_ref[3]
        cp = _remote(out_ref.at[mine], out_ref.at[mine], send_sems, recv_sems, 4, sibling)
        cp.start()
        _remote(out_ref.at[other], out_ref.at[other], send_sems, recv_sems, 4, sibling).wait_recv()
        cp.wait_send()

    vm = pl.BlockSpec(memory_space=pltpu.VMEM)
    return pl.pallas_call(
        body, name=name, in_specs=[vm], out_specs=vm,
        out_shape=jax.ShapeDtypeStruct((rows, w), F32),
        scratch_shapes=[pltpu.VMEM((rows, w), F32), pltpu.VMEM((hr, w), F32), pltpu.VMEM((N_CHIPS, hr, w), F32),
                        pltpu.SemaphoreType.DMA((5,)), pltpu.SemaphoreType.DMA((5,))],
        compiler_params=pltpu.CompilerParams(vmem_limit_bytes=VMEM_LIMIT),
    )(v)


ELEMENTWISE_BLOCK = 256 * 1024


def _rows_tile(rows, cols):
    best = None
    for t in range(SUBLANES, rows + 1, SUBLANES):
        if rows % t == 0 and t * cols <= ELEMENTWISE_BLOCK:
            best = t
    return rows if best is None else best


def _add_pair(g, t, core, name):
    nb, n, w = t.shape
    tr = _rows_tile(n, w)
    steps = n // tr

    def body(core_ref, g_ref, t_ref, o_ref):
        o_ref[...] = (g_ref[...] + t_ref[...]).astype(BF16)

    spec = pl.BlockSpec((1, tr, w), lambda j, i, core_ref: (j, i, 0))
    return pl.pallas_call(
        body, name=name,
        grid_spec=pltpu.PrefetchScalarGridSpec(
            num_scalar_prefetch=1, grid=(nb, steps),
            in_specs=[pl.BlockSpec((1, tr, w), lambda j, i, core_ref: (j, core_ref[0] * steps + i, 0)), spec],
            out_specs=spec),
        out_shape=jax.ShapeDtypeStruct(t.shape, BF16),
        compiler_params=_params(("parallel", "parallel")))(core, g, t)


def _add_chips(landed, pairs, place, name):
    nb, n, w = landed.shape
    tr = _rows_tile(n, w)
    steps = n // tr

    def body(place_ref, r_ref, own_ref, o_ref):
        me = place_ref[0]
        acc = None
        for k in range(nb):
            blk = jnp.where(me == k, own_ref[0], r_ref[k]).astype(F32)
            acc = blk if acc is None else acc + blk
        o_ref[...] = acc

    return pl.pallas_call(
        body, name=name,
        grid_spec=pltpu.PrefetchScalarGridSpec(
            num_scalar_prefetch=1, grid=(steps,),
            in_specs=[pl.BlockSpec((nb, tr, w), lambda i, place_ref: (0, i, 0)),
                      pl.BlockSpec((1, tr, w), lambda i, place_ref: (place_ref[0], i, 0))],
            out_specs=pl.BlockSpec((tr, w), lambda i, place_ref: (place_ref[1] * steps + i, 0))),
        out_shape=jax.ShapeDtypeStruct((2 * n, w), F32),
        compiler_params=_params(("parallel",)))(place, landed, pairs)


def _adamw(w, g, m, v, name):
    rows, wd = w.shape
    tr = _rows_tile(rows, wd)
    c1 = 1.0 - ADAM_B1 ** ADAM_STEP
    c2 = 1.0 - ADAM_B2 ** ADAM_STEP

    def body(w_ref, g_ref, m_ref, v_ref, d_ref, mo_ref, vo_ref):
        gv = g_ref[...]
        m2 = ADAM_B1 * m_ref[...] + (1.0 - ADAM_B1) * gv
        v2 = ADAM_B2 * v_ref[...] + (1.0 - ADAM_B2) * (gv * gv)
        mo_ref[...] = m2
        vo_ref[...] = v2
        d_ref[...] = -ADAM_LR * ((m2 / c1) / (jnp.sqrt(v2 / c2) + ADAM_EPS) + ADAM_WD * w_ref[...])

    spec = pl.BlockSpec((tr, wd), lambda i: (i, 0))
    shp = jax.ShapeDtypeStruct((rows, wd), F32)
    return pl.pallas_call(body, name=name, grid=(rows // tr,), in_specs=[spec] * 4, out_specs=[spec] * 3,
                          out_shape=[shp] * 3, compiler_params=_params(("parallel",)))(w, g, m, v)


BIG = [("w_in", (1024, 3232), 1), ("w_uq", (384, 768), 1), ("w_uk", (256, 512), 1), ("w_uv", (256, 512), 1),
       ("w_glu", (512, 512), 0), ("w_branch_attn", (512, 1024), 1), ("w_branch_ssm", (512, 1024), 1),
       ("w_out", (1024, 1024), 0), ("w_up", (1024, 5632), 1), ("conv_w", (3, 5632), 1), ("w_down", (2816, 1024), 0)]
SMALL = [("mix_norm_pre", (1024,)), ("q_norm", (384,)), ("kv_norm", (256,)), ("ssm_lambda_re", (32, 64)),
         ("ssm_lambda_im", (32, 64)), ("ssm_log_dt", (32,)), ("ssm_b_re", (32, 64, 16)), ("ssm_b_im", (32, 64, 16)),
         ("ssm_c_re", (32, 16, 64)), ("ssm_c_im", (32, 16, 64)), ("ssm_d", (32, 16)), ("b_glu", (512,)),
         ("b_gate", (2048,)), ("mix_norm_post", (1024,)), ("ffn_norm_pre", (1024,)), ("conv_b", (5632,)),
         ("ffn_norm_post", (1024,))]
MATMUL_W = [b for b in BIG if b[0] != "conv_w"]
LATE_W = ("w_up", "w_down", "conv_w")
CONV_W_SHAPE = (3, 2 * D_FF)
CONV_W_SHARD = (3, 2 * D_FF // N_CHIPS)
SMALL_SUM = [("loss", (1,))] + SMALL + [("conv_w", CONV_W_SHAPE)]
SMALL_ADAM = SMALL + [("conv_w", CONV_W_SHARD)]


def _pack_flat(layout, vals):
    flat = jnp.concatenate([vals[n].astype(F32).reshape(-1) for n, _ in layout])
    rows = -(-(-(-flat.shape[0] // FLAT_W)) // (2 * SUBLANES)) * 2 * SUBLANES
    return jnp.pad(flat, (0, rows * FLAT_W - flat.shape[0])).reshape(rows, FLAT_W)


def _unpack_flat(layout, flat):
    flat = flat.reshape(-1)
    out = {}
    o = 0
    for name, shape in layout:
        n = math.prod(shape)
        out[name] = flat[o:o + n].reshape(shape)
        o += n
    return out


_ARG_NAMES = ["x", "positions"] + [n for n in (
    "mix_norm_pre", "w_in", "q_norm", "w_uq", "kv_norm", "w_uk", "w_uv", "ssm_lambda_re", "ssm_lambda_im", "ssm_log_dt",
    "ssm_b_re", "ssm_b_im", "ssm_c_re", "ssm_c_im", "ssm_d", "w_glu", "b_glu", "w_branch_attn", "w_branch_ssm",
    "b_gate", "w_out", "mix_norm_post", "ffn_norm_pre", "w_up", "conv_w", "conv_b", "w_down", "ffn_norm_post")]
_WEIGHTS = _ARG_NAMES[2:]


def _gather_weights(w):
    early = [b for b in MATMUL_W if b[0] not in LATE_W]
    late = [b for b in BIG if b[0] in LATE_W]
    own = (jnp.arange(N_CHIPS) == 2 * lax.axis_index("x") + lax.axis_index("y"))[:, None, None]

    def whole(layout, mine, gathered):
        return {name: _from_chip_major(jnp.where(own, s[None], g), axis)
                for (name, _, axis), s, g in zip(layout, mine, gathered)}

    mine = [w[name].astype(BF16) for name, _, _ in early]
    full = whole(early, mine, _gather_big(mine))
    mine_late = [w[name].astype(F32 if name == "conv_w" else BF16) for name, _, _ in late]
    _, mine_late = lax.optimization_barrier((full["w_in"], mine_late))
    send_sems, recv_sems, shards_thru, lands_thru, token = _exchange_start(mine_late, "gather_late_start", scatter=False)

    def late_weights(after):
        shards, lands = _exchange_wait(send_sems, recv_sems, shards_thru, lands_thru, after, "gather_late_wait",
                                       scatter=False)
        return whole(late, shards, lands)

    full["late"] = late_weights
    full["token"] = token[0, 0]
    return full


def _pair_sums(names, grads, tag):
    core = lax.axis_index("c").astype(jnp.int32).reshape(1)
    theirs = _reduce_to_sibling(grads, "reduce_grads_d2d" + tag)
    return [_add_pair(g, t, core, "reduce_pair_" + n) for n, g, t in zip(names, grads, theirs)]


def _start_reduce(tag, grads):
    names = list(grads)
    pairs = _pair_sums(names, [grads[n] for n in names], "_" + tag)
    send_sems, recv_sems, pairs_thru, lands_thru, token = _exchange_start(pairs, "reduce_" + tag + "_start", scatter=True)
    return (tag, names, send_sems, recv_sems, pairs_thru, lands_thru), token[0, 0]


def _reduce_grads(gbig, pending, loss, gsmall):
    core = lax.axis_index("c").astype(jnp.int32).reshape(1)
    chip = (2 * lax.axis_index("x") + lax.axis_index("y")).astype(jnp.int32).reshape(1)
    place = jnp.concatenate([chip, core])
    names = list(gbig)
    pairs = _pair_sums(names, [gbig[n] for n in names], "")
    landed = list(_reduce_between_chips(pairs))
    for tag, sent_names, send_sems, recv_sems, pairs_thru, lands_thru in pending:
        got_pairs, got_landed = _exchange_wait(send_sems, recv_sems, pairs_thru, lands_thru, gbig["w_in"],
                                               "reduce_" + tag + "_wait", scatter=True)
        names, pairs, landed = names + sent_names, pairs + got_pairs, landed + got_landed
    totals = [_add_chips(r, p, place, "reduce_chips_" + n) for n, r, p in zip(names, landed, pairs)]
    g_red = dict(zip(names, _reduce_back(totals)))

    vals = dict(gsmall)
    vals["loss"] = loss
    small_red = _unpack_flat(SMALL_SUM, _all_reduce_small(_pack_flat(SMALL_SUM, vals), "reduce_small"))
    return g_red, small_red


def _step(args):
    x = args["x"][0]
    positions = args["positions"][0]
    tgt = args["loss_target"][0]
    w = {n: args[n][0] for n in _WEIGHTS}
    m = {n: args["m_" + n][0] for n in _WEIGHTS}
    v = {n: args["v_" + n][0] for n in _WEIGHTS}

    full = _gather_weights(w)
    sp = {n: w[n].reshape(s) for n, s in SMALL}
    for n in ("mix_norm_pre", "q_norm", "kv_norm", "b_glu", "b_gate", "mix_norm_post", "ffn_norm_pre", "conv_b",
              "ffn_norm_post"):
        sp[n] = sp[n].reshape(1, -1)
    sp["mix_norm_pre"] = sp["mix_norm_pre"] + full.pop("token")
    pending = []

    def send_grads(tag, grads):
        state, token = _start_reduce(tag, grads)
        pending.append(state)
        return token

    full["send_grads"] = send_grads
    loss, gx, gbig, gsmall = _local_step(x, positions, tgt, full, sp)
    g_red, small_red = _reduce_grads(gbig, pending, loss, gsmall)

    chip = 2 * lax.axis_index("x") + lax.axis_index("y")
    grads = dict(small_red)
    grads["conv_w"] = lax.dynamic_slice_in_dim(small_red["conv_w"], chip * CONV_W_SHARD[1], CONV_W_SHARD[1], axis=1)
    grads.update(g_red)

    outs = {"grad_" + n: grads[n] for n in _WEIGHTS}
    for name, _, _ in MATMUL_W:
        d, m2, v2 = _adamw(w[name], grads[name], m[name], v[name], "adamw_" + name)
        outs["delta_" + name], outs["new_m_" + name], outs["new_v_" + name] = d, m2, v2
    d_sm, m_sm, v_sm = _adamw(_pack_flat(SMALL_ADAM, w), _pack_flat(SMALL_ADAM, grads), _pack_flat(SMALL_ADAM, m),
                              _pack_flat(SMALL_ADAM, v), "adamw_small")
    for prefix, flat in (("delta_", d_sm), ("new_m_", m_sm), ("new_v_", v_sm)):
        for n, val in _unpack_flat(SMALL_ADAM, flat).items():
            outs[prefix + n] = val
    outs = {n: val.reshape(args[n.split("_", 1)[1] if not n.startswith("new_") else n[6:]].shape)
            for n, val in outs.items()}
    res = [small_red["loss"][0], gx[None]]
    for prefix in ("grad_", "delta_", "new_m_", "new_v_"):
        res += [outs[prefix + n] for n in _WEIGHTS]
    return tuple(res)


def kernel(x, positions, mix_norm_pre, w_in, q_norm, w_uq, kv_norm, w_uk, w_uv, ssm_lambda_re, ssm_lambda_im, ssm_log_dt, ssm_b_re, ssm_b_im, ssm_c_re, ssm_c_im, ssm_d, w_glu, b_glu, w_branch_attn, w_branch_ssm, b_gate, w_out, mix_norm_post, ffn_norm_pre, w_up, conv_w, conv_b, w_down, ffn_norm_post, loss_target, m_mix_norm_pre, m_w_in, m_q_norm, m_w_uq, m_kv_norm, m_w_uk, m_w_uv, m_ssm_lambda_re, m_ssm_lambda_im, m_ssm_log_dt, m_ssm_b_re, m_ssm_b_im, m_ssm_c_re, m_ssm_c_im, m_ssm_d, m_w_glu, m_b_glu, m_w_branch_attn, m_w_branch_ssm, m_b_gate, m_w_out, m_mix_norm_post, m_ffn_norm_pre, m_w_up, m_conv_w, m_conv_b, m_w_down, m_ffn_norm_post, v_mix_norm_pre, v_w_in, v_q_norm, v_w_uq, v_kv_norm, v_w_uk, v_w_uv, v_ssm_lambda_re, v_ssm_lambda_im, v_ssm_log_dt, v_ssm_b_re, v_ssm_b_im, v_ssm_c_re, v_ssm_c_im, v_ssm_d, v_w_glu, v_b_glu, v_w_branch_attn, v_w_branch_ssm, v_b_gate, v_w_out, v_mix_norm_post, v_ffn_norm_pre, v_w_up, v_conv_w, v_conv_b, v_w_down, v_ffn_norm_post):
    given = dict(locals())
    return _step(given)
```

```python
import math

import jax
import jax.numpy as jnp
from jax import lax
from jax.experimental import pallas as pl
from jax.experimental.pallas import tpu as pltpu

F32 = jnp.float32
BF16 = jnp.bfloat16
MESH = pl.DeviceIdType.MESH

D_MODEL = 1024
N_HEADS = 8
QK_NOPE = 64
QK_ROPE = 32
QK_HEAD = QK_NOPE + QK_ROPE
V_HEAD = 64
Q_RANK = 384
KV_RANK = 256
ROPE_THETA = 10000.0
SSM_W = 512
SSM_H = 16
SSM_G = 32
SSM_P = 64
SSM_CH = SSM_G * SSM_P
D_FF = 2816
EPS = 1e-6
ADAM_LR = 0.001
ADAM_B1 = 0.9
ADAM_B2 = 0.999
ADAM_EPS = 1e-08
ADAM_WD = 0.01
ADAM_STEP = 10

LANES = 128
SUBLANES = 8
VMEM_LIMIT = 56 * 1024 * 1024

HEAD_SLOT = LANES
HP = N_HEADS * HEAD_SLOT
P_CQ, P_CKV, P_KR, P_U, P_GL, P_END = 0, 384, 640, 768, 1280, 3328
KR_LANE = 64

FLAT_W = 1024
N_CHIPS = 4


def _tile(n, cap):
    if n <= cap:
        return n
    best = None
    for t in range(LANES, cap + 1, LANES):
        if n % t == 0:
            best = t
    assert best is not None, (n, cap)
    return best


def _params(sem):
    return pltpu.CompilerParams(dimension_semantics=sem, vmem_limit_bytes=VMEM_LIMIT)


def _dot(a, b):
    return jnp.dot(a, b, preferred_element_type=F32)


def _dot_nt(a, b):
    return lax.dot_general(a, b, (((1,), (1,)), ((), ())), preferred_element_type=F32)


def _dot_tn(a, b):
    return lax.dot_general(a, b, (((0,), (0,)), ((), ())), preferred_element_type=F32)


def _rms(x, g):
    r = lax.rsqrt(jnp.mean(x * x, axis=-1, keepdims=True) + EPS)
    return x * r * g, r


def _rms_bwd(dy, x, g):
    r = lax.rsqrt(jnp.mean(x * x, axis=-1, keepdims=True) + EPS)
    dyg = dy * g
    dx = r * dyg - x * (r * r * r) * jnp.mean(dyg * x, axis=-1, keepdims=True)
    dg = jnp.sum(dy * x * r, axis=0, keepdims=True)
    return dx, dg


_GELU_K0 = math.sqrt(2.0 / math.pi)
_GELU_K1 = 0.044715


def _gelu(x):
    th = jnp.tanh(_GELU_K0 * (x + _GELU_K1 * x * x * x))
    return 0.5 * x * (1.0 + th)


def _gelu_grad(x):
    th = jnp.tanh(_GELU_K0 * (x + _GELU_K1 * x * x * x))
    return 0.5 * (1.0 + th) + 0.5 * x * (1.0 - th * th) * _GELU_K0 * (1.0 + 3.0 * _GELU_K1 * x * x)


def _sigmoid(x):
    return 1.0 / (1.0 + jnp.exp(-x))


def _rope(q, c, s):
    n = q.shape[1]
    lane = lax.broadcasted_iota(jnp.int32, q.shape, 1) % HEAD_SLOT
    sw = jnp.where(lane < KR_LANE + QK_ROPE // 2, pltpu.roll(q, n - QK_ROPE // 2, 1), pltpu.roll(q, QK_ROPE // 2, 1))
    return q * c + sw * s


def _rope_bwd(dy, c, s):
    n = dy.shape[1]
    t = dy * s
    lane = lax.broadcasted_iota(jnp.int32, dy.shape, 1) % HEAD_SLOT
    sw = jnp.where(lane < KR_LANE + QK_ROPE // 2, pltpu.roll(t, n - QK_ROPE // 2, 1), pltpu.roll(t, QK_ROPE // 2, 1))
    rope_lane = jnp.logical_and(lane >= KR_LANE, lane < KR_LANE + QK_ROPE)
    return dy * c + jnp.where(rope_lane, sw, 0.0)


def _shift_down(x, k, halo):
    xs = pltpu.roll(x, k, 0)
    hs = pltpu.roll(halo, k, 0)
    rows = lax.broadcasted_iota(jnp.int32, halo.shape, 0)
    top = jnp.where(rows < k, hs, xs[0:SUBLANES])
    return jnp.concatenate([top, xs[SUBLANES:]], axis=0)


def _shift_up(x, k, halo):
    t = x.shape[0]
    xs = pltpu.roll(x, t - k, 0)
    hs = pltpu.roll(halo, SUBLANES - k, 0)
    rows = lax.broadcasted_iota(jnp.int32, halo.shape, 0)
    bot = jnp.where(rows >= SUBLANES - k, hs, xs[t - SUBLANES:])
    return jnp.concatenate([xs[:t - SUBLANES], bot], axis=0)


def _mm(a, b, name, out_dtype=F32, bt=False, b_col0=0, n=None, tm_cap=1024, tn_cap=1408, a_lead=None):
    m, k = a.shape[-2:]
    if bt:
        n_full = b.shape[0]
        n = n_full
    else:
        n = b.shape[1] if n is None else n
    tm = min(tm_cap, m)
    tn = _tile(n, tn_cap)

    def body(a_ref, b_ref, o_ref):
        if bt:
            o_ref[...] = _dot_nt(a_ref[...], b_ref[...]).astype(out_dtype)
        else:
            o_ref[...] = _dot(a_ref[...], b_ref[...]).astype(out_dtype)

    if bt:
        b_spec = pl.BlockSpec((tn, k), lambda j, i: (j, b_col0))
    else:
        off = b_col0 * (n // tn)
        b_spec = pl.BlockSpec((k, tn), lambda j, i: (0, off + j))
    if a_lead is None:
        a_spec = pl.BlockSpec((tm, k), lambda j, i: (i, 0))
    else:
        a_spec = pl.BlockSpec((None, tm, k), lambda j, i: (a_lead, i, 0))
    return pl.pallas_call(
        body, name=name, grid=(n // tn, m // tm),
        in_specs=[a_spec, b_spec],
        out_specs=pl.BlockSpec((tm, tn), lambda j, i: (i, j)),
        out_shape=jax.ShapeDtypeStruct((m, n), out_dtype),
        compiler_params=_params(("parallel", "parallel")),
    )(a, b)


def _mm_tn(a, b, name, tk_cap=1024, tn_cap=1664, tl_cap=1024, chips=False):
    l, k = a.shape
    tk = _tile(k, tk_cap)
    tl = min(tl_cap, l)

    def body(a_ref, b_ref, o_ref):
        @pl.when(pl.program_id(2) == 0)
        def _():
            o_ref[...] = jnp.zeros_like(o_ref)

        o_ref[...] += _dot_tn(a_ref[...], b_ref[...])

    if chips:
        n = b.shape[-1] * (b.shape[0] if b.ndim == 3 else 1)
        tn = n // N_CHIPS
        assert tn % LANES == 0
        if b.ndim == 3:
            per = N_CHIPS // b.shape[0]
            b_spec = pl.BlockSpec((None, tl, tn), lambda i, j, r: (j // per, r, j % per))
        else:
            b_spec = pl.BlockSpec((tl, tn), lambda i, j, r: (r, j))
        out_spec = pl.BlockSpec((None, tk, tn), lambda i, j, r: (j, i, 0))
        out_shape = jax.ShapeDtypeStruct((N_CHIPS, k, tn), F32)
    else:
        n = b.shape[1]
        tn = _tile(n, tn_cap)
        b_spec = pl.BlockSpec((tl, tn), lambda i, j, r: (r, j))
        out_spec = pl.BlockSpec((tk, tn), lambda i, j, r: (i, j))
        out_shape = jax.ShapeDtypeStruct((k, n), F32)
    return pl.pallas_call(
        body, name=name, grid=(k // tk, n // tn, l // tl),
        in_specs=[pl.BlockSpec((tl, tk), lambda i, j, r: (r, i)), b_spec],
        out_specs=out_spec, out_shape=out_shape,
        compiler_params=_params(("parallel", "parallel", "arbitrary")),
    )(a, b)


def _row(tl, n):
    return pl.BlockSpec((tl, n), lambda i: (i, 0))


def _const(shape):
    return pl.BlockSpec(shape, lambda i: tuple(0 for _ in shape))


def _proj_fwd(x, g1, win, gq, wuq, gkv, wukv, rc, rs, bg, tl):
    l = x.shape[0]

    def body(x_ref, g1_ref, win_ref, gq_ref, wuq_ref, gkv_ref, wukv_ref, rc_ref, rs_ref, bg_ref,
             hn_ref, cq_ref, ckv_ref, q_ref, k_ref, v_ref, u_ref, gl_ref):
        hn, _ = _rms(x_ref[...], g1_ref[...])
        hnb = hn.astype(BF16)
        hn_ref[...] = hnb
        proj = _dot(hnb, win_ref[...])
        cq = proj[:, P_CQ:P_CKV]
        ckv = proj[:, P_CKV:P_KR]
        kr = proj[:, P_KR:P_U]
        cq_ref[...] = cq
        ckv_ref[...] = ckv
        u_ref[...] = proj[:, P_U:P_GL]
        gl_ref[...] = proj[:, P_GL:P_END] + bg_ref[...]
        qn, _ = _rms(cq, gq_ref[...])
        q = _dot(qn.astype(BF16), wuq_ref[...])
        c1 = rc_ref[...]
        s1 = rs_ref[...]
        q_ref[...] = (_rope(q, jnp.tile(c1, (1, N_HEADS)), jnp.tile(s1, (1, N_HEADS))) * Q_PRESCALE).astype(BF16)
        ckvn, _ = _rms(ckv, gkv_ref[...])
        kv = _dot(ckvn.astype(BF16), wukv_ref[...])
        krr = _rope(kr, c1, s1)
        k_ref[...] = (kv[:, :HP] + jnp.tile(krr, (1, N_HEADS))).astype(BF16)
        v_ref[...] = kv[:, HP:].astype(BF16)

    outs = [(D_MODEL, BF16), (Q_RANK, F32), (KV_RANK, F32), (HP, BF16), (HP, BF16), (HP, BF16),
            (SSM_W, F32), (2 * D_MODEL, F32)]
    return pl.pallas_call(
        body, name="proj_fwd", grid=(l // tl,),
        in_specs=[_row(tl, D_MODEL), _const((1, D_MODEL)), _const((D_MODEL, P_END)), _const((1, Q_RANK)),
                  _const((Q_RANK, HP)), _const((1, KV_RANK)), _const((KV_RANK, 2 * HP)),
                  _row(tl, HEAD_SLOT), _row(tl, HEAD_SLOT), _const((1, 2 * D_MODEL))],
        out_specs=[_row(tl, n) for n, _ in outs],
        out_shape=[jax.ShapeDtypeStruct((l, n), dt) for n, dt in outs],
        compiler_params=_params(("parallel",)),
    )(x, g1, win, gq, wuq, gkv, wukv, rc, rs, bg)


_NEG = -1e30


LOG2E = 1.0 / math.log(2.0)
LN2 = math.log(2.0)
ATTN_SCALE = 1.0 / math.sqrt(QK_HEAD)
Q_PRESCALE = ATTN_SCALE * LOG2E
HEADS_PER_STEP = 4
PAIR_W = HEADS_PER_STEP * HEAD_SLOT


def _causal_pairs(nq, by_query):
    if by_query:
        pairs = [(i, j) for i in range(nq) for j in range(i + 1)]
    else:
        pairs = [(i, j) for j in range(nq) for i in range(j, nq)]
    return jnp.array([p[0] for p in pairs], jnp.int32), jnp.array([p[1] for p in pairs], jnp.int32)


def _diag_mask_t(s):
    rows = lax.broadcasted_iota(jnp.int32, s.shape, 0)
    cols = lax.broadcasted_iota(jnp.int32, s.shape, 1)
    return jnp.where(rows <= cols, s, _NEG)


def _attn_fwd(q, k, v, tq):
    l = q.shape[0]
    nq = l // tq
    it, jt = _causal_pairs(nq, True)

    def body(it_ref, jt_ref, q_ref, k_ref, v_ref, o_ref, lse_ref, m_ref, l_ref, acc_ref):
        t = pl.program_id(1)
        i = it_ref[t]
        j = jt_ref[t]

        @pl.when(j == 0)
        def _():
            m_ref[...] = jnp.full_like(m_ref, _NEG)
            l_ref[...] = jnp.zeros_like(l_ref)
            acc_ref[...] = jnp.zeros_like(acc_ref)

        def update(on_diagonal):
            for hh in range(HEADS_PER_STEP):
                sl = slice(hh * HEAD_SLOT, (hh + 1) * HEAD_SLOT)
                s = _dot_nt(k_ref[:, sl], q_ref[:, sl])
                if on_diagonal:
                    s = _diag_mask_t(s)
                m_old = m_ref[hh]
                m_new = jnp.maximum(m_old, jnp.max(s, axis=0, keepdims=True))
                p = jnp.exp2(s - m_new)
                alpha = jnp.exp2(m_old - m_new)
                l_ref[hh] = alpha * l_ref[hh] + jnp.sum(p, axis=0, keepdims=True)
                acc_ref[hh] = alpha * acc_ref[hh] + _dot_tn(v_ref[:, sl], p.astype(BF16))
                m_ref[hh] = m_new

        @pl.when(j < i)
        def _():
            update(False)

        @pl.when(j == i)
        def _():
            update(True)
            for hh in range(HEADS_PER_STEP):
                sl = slice(hh * HEAD_SLOT, (hh + 1) * HEAD_SLOT)
                o_ref[:, sl] = (acc_ref[hh] / l_ref[hh]).T.astype(BF16)
                lse_ref[hh] = m_ref[hh] + jnp.log(l_ref[hh]) * LOG2E

    blk = (tq, PAIR_W)
    qmap = lambda h, t, it_ref, jt_ref: (it_ref[t], h)
    kmap = lambda h, t, it_ref, jt_ref: (jt_ref[t], h)
    row = pl.BlockSpec((HEADS_PER_STEP, 1, tq), lambda h, t, it_ref, jt_ref: (h, 0, it_ref[t]))
    return pl.pallas_call(
        body, name="attn_fwd",
        grid_spec=pltpu.PrefetchScalarGridSpec(
            num_scalar_prefetch=2, grid=(N_HEADS // HEADS_PER_STEP, it.shape[0]),
            in_specs=[pl.BlockSpec(blk, qmap), pl.BlockSpec(blk, kmap), pl.BlockSpec(blk, kmap)],
            out_specs=[pl.BlockSpec(blk, qmap), row],
            scratch_shapes=[pltpu.VMEM((HEADS_PER_STEP, 1, tq), F32), pltpu.VMEM((HEADS_PER_STEP, 1, tq), F32),
                            pltpu.VMEM((HEADS_PER_STEP, HEAD_SLOT, tq), F32)]),
        out_shape=[jax.ShapeDtypeStruct((l, HP), BF16), jax.ShapeDtypeStruct((N_HEADS, 1, l), F32)],
        compiler_params=_params(("parallel", "arbitrary")),
    )(it, jt, q, k, v)


def _attn_delta(o, do, tq):
    l = o.shape[0]

    def body(o_ref, do_ref, d_ref):
        prod = o_ref[...].astype(F32) * do_ref[...].astype(F32)
        for hh in range(HEADS_PER_STEP):
            d_ref[hh] = jnp.sum(prod[:, hh * HEAD_SLOT:(hh + 1) * HEAD_SLOT].T, axis=0, keepdims=True)

    blk = pl.BlockSpec((tq, PAIR_W), lambda h, i: (i, h))
    return pl.pallas_call(
        body, name="attn_delta", grid=(N_HEADS // HEADS_PER_STEP, l // tq), in_specs=[blk, blk],
        out_specs=pl.BlockSpec((HEADS_PER_STEP, 1, tq), lambda h, i: (h, 0, i)),
        out_shape=jax.ShapeDtypeStruct((N_HEADS, 1, l), F32),
        compiler_params=_params(("parallel", "parallel")),
    )(o, do)


def _attn_bwd(q, k, v, do, lse, delta, tq):
    l = q.shape[0]
    nq = l // tq
    it, jt = _causal_pairs(nq, False)

    def body(it_ref, jt_ref, q_ref, k_ref, v_ref, do_ref, lse_ref, dl_ref, dq_ref, dk_ref, dv_ref, dka_ref, dva_ref):
        t = pl.program_id(1)
        i = it_ref[t]
        j = jt_ref[t]

        @pl.when(t == 0)
        def _():
            dq_ref[...] = jnp.zeros_like(dq_ref)

        @pl.when(i == j)
        def _():
            dka_ref[...] = jnp.zeros_like(dka_ref)
            dva_ref[...] = jnp.zeros_like(dva_ref)

        def update(on_diagonal):
            r0 = pl.multiple_of(i * tq, tq)
            for hh in range(HEADS_PER_STEP):
                sl = slice(hh * HEAD_SLOT, (hh + 1) * HEAD_SLOT)
                qb = q_ref[:, sl]
                kb = k_ref[:, sl]
                dob = do_ref[:, sl]
                s = _dot_nt(kb, qb)
                if on_diagonal:
                    s = _diag_mask_t(s)
                p = jnp.exp2(s - lse_ref[hh])
                dva_ref[:, sl] += _dot(p.astype(BF16), dob)
                dp = _dot_nt(v_ref[:, sl], dob)
                ds = (p * (dp - dl_ref[hh])).astype(BF16)
                dka_ref[:, sl] += _dot(ds, qb)
                dq_ref[pl.ds(r0, tq), sl] += ATTN_SCALE * _dot_tn(ds, kb)

        @pl.when(j < i)
        def _():
            update(False)

        @pl.when(j == i)
        def _():
            update(True)

        @pl.when(i == nq - 1)
        def _():
            dk_ref[...] = (dka_ref[...] * LN2).astype(BF16)
            dv_ref[...] = dva_ref[...].astype(BF16)

    blk = (tq, PAIR_W)
    qmap = lambda h, t, it_ref, jt_ref: (it_ref[t], h)
    kmap = lambda h, t, it_ref, jt_ref: (jt_ref[t], h)
    row = pl.BlockSpec((HEADS_PER_STEP, 1, tq), lambda h, t, it_ref, jt_ref: (h, 0, it_ref[t]))
    return pl.pallas_call(
        body, name="attn_bwd",
        grid_spec=pltpu.PrefetchScalarGridSpec(
            num_scalar_prefetch=2, grid=(N_HEADS // HEADS_PER_STEP, it.shape[0]),
            in_specs=[pl.BlockSpec(blk, qmap), pl.BlockSpec(blk, kmap), pl.BlockSpec(blk, kmap),
                      pl.BlockSpec(blk, qmap), row, row],
            out_specs=[pl.BlockSpec((l, PAIR_W), lambda h, t, it_ref, jt_ref: (0, h)), pl.BlockSpec(blk, kmap),
                       pl.BlockSpec(blk, kmap)],
            scratch_shapes=[pltpu.VMEM(blk, F32), pltpu.VMEM(blk, F32)]),
        out_shape=[jax.ShapeDtypeStruct((l, HP), F32), jax.ShapeDtypeStruct((l, HP), BF16),
                   jax.ShapeDtypeStruct((l, HP), BF16)],
        compiler_params=_params(("parallel", "arbitrary")),
    )(it, jt, q, k, v, do, lse, delta)


SSM_CB = 512
SSM_UB = 128
SSM_NB = SSM_CH // SSM_CB


def _scan_tiles(re_ref, im_ref, tab, carry, n_tiles, reverse):
    group = 2
    assert n_tiles % group == 0
    pr, pi = tab[6], tab[7]

    def inside(sr, si):
        for step, k in enumerate((1, 2, 4)):
            mr, mi = tab[2 * step], tab[2 * step + 1]
            sh = (SUBLANES - k) if reverse else k
            rr = pltpu.roll(sr, sh, 0)
            ri = pltpu.roll(si, sh, 0)
            sr, si = sr + mr * rr - mi * ri, si + mr * ri + mi * rr
        return sr, si

    def body(n, c):
        cr, ci = c
        first = (n_tiles - group * (n + 1)) if reverse else group * n
        r0 = pl.multiple_of(first * SUBLANES, group * SUBLANES)
        rows = [pl.ds(r0 + g * SUBLANES, SUBLANES) for g in range(group)]
        tiles = [inside(re_ref[r, :], im_ref[r, :]) for r in rows]
        for g in (range(group - 1, -1, -1) if reverse else range(group)):
            sr, si = tiles[g]
            sr, si = sr + pr * cr - pi * ci, si + pr * ci + pi * cr
            re_ref[rows[g], :] = sr
            im_ref[rows[g], :] = si
            edge = slice(0, 1) if reverse else slice(SUBLANES - 1, SUBLANES)
            cr, ci = sr[edge, :], si[edge, :]
        return cr, ci

    return lax.fori_loop(0, n_tiles // group, body, carry)


def _ssm_fwd(u, bre, bim, cre, cim, dvec, tab, tt):
    l = u.shape[0]
    nt = l // tt

    def body(u_ref, bre_ref, bim_ref, cre_ref, cim_ref, d_ref, tab_ref, y_ref, sre_ref, sim_ref, car_ref):
        @pl.when(pl.program_id(1) == 0)
        def _():
            car_ref[...] = jnp.zeros_like(car_ref)

        uf = u_ref[...]
        ub = uf.astype(BF16)
        sre_ref[...] = _dot(ub, bre_ref[0])
        sim_ref[...] = _dot(ub, bim_ref[0])
        tab_v = [tab_ref[n] for n in range(8)]
        cr, ci = _scan_tiles(sre_ref, sim_ref, tab_v, (car_ref[0:1, :], car_ref[8:9, :]), tt // SUBLANES, False)
        car_ref[0:1, :] = cr
        car_ref[8:9, :] = ci
        y_ref[...] = (_dot(sre_ref[...].astype(BF16), cre_ref[0]) - _dot(sim_ref[...].astype(BF16), cim_ref[0])
                      + d_ref[...] * uf)

    return pl.pallas_call(
        body, name="ssm_fwd", grid=(SSM_NB, nt),
        in_specs=[pl.BlockSpec((tt, SSM_UB), lambda m, t: (t, m)),
                  pl.BlockSpec((1, SSM_UB, SSM_CB), lambda m, t: (m, 0, 0)),
                  pl.BlockSpec((1, SSM_UB, SSM_CB), lambda m, t: (m, 0, 0)),
                  pl.BlockSpec((1, SSM_CB, SSM_UB), lambda m, t: (m, 0, 0)),
                  pl.BlockSpec((1, SSM_CB, SSM_UB), lambda m, t: (m, 0, 0)),
                  pl.BlockSpec((1, SSM_UB), lambda m, t: (0, m)),
                  pl.BlockSpec((8, SUBLANES, SSM_CB), lambda m, t: (0, 0, m))],
        out_specs=[pl.BlockSpec((tt, SSM_UB), lambda m, t: (t, m)),
                   pl.BlockSpec((tt, SSM_CB), lambda m, t: (t, m)),
                   pl.BlockSpec((tt, SSM_CB), lambda m, t: (t, m))],
        out_shape=[jax.ShapeDtypeStruct((l, SSM_W), F32), jax.ShapeDtypeStruct((l, SSM_CH), F32),
                   jax.ShapeDtypeStruct((l, SSM_CH), F32)],
        scratch_shapes=[pltpu.VMEM((2 * SUBLANES, SSM_CB), F32)],
        compiler_params=_params(("parallel", "arbitrary")),
    )(u, bre, bim, cre, cim, dvec, tab)


def _ssm_bwd(dy, u, sre, sim, bre, bim, cre, cim, dvec, tab, tt):
    l = u.shape[0]
    nt = l // tt
    tpb = tt // SUBLANES

    def body(dy_ref, u_ref, sre_ref, sim_ref, hre_ref, him_ref, bre_ref, bim_ref, cre_ref, cim_ref, d_ref, tab_ref,
             du_ref, dbre_ref, dbim_ref, dcre_ref, dcim_ref, dare_ref, daim_ref, dd_ref, lr_ref, li_ref, car_ref):
        t = pl.program_id(1)

        @pl.when(t == 0)
        def _():
            car_ref[...] = jnp.zeros_like(car_ref)
            for ref in (dbre_ref, dbim_ref, dcre_ref, dcim_ref, dare_ref, daim_ref, dd_ref):
                ref[...] = jnp.zeros_like(ref)

        dyf = dy_ref[...]
        dyb = dyf.astype(BF16)
        uf = u_ref[...]
        s_re = sre_ref[...]
        s_im = sim_ref[...]
        lr_ref[...] = _dot_nt(dyb, cre_ref[0])
        li_ref[...] = -_dot_nt(dyb, cim_ref[0])
        dcre_ref[0] += _dot_tn(s_re.astype(BF16), dyb)
        dcim_ref[0] -= _dot_tn(s_im.astype(BF16), dyb)
        tab_v = [tab_ref[n] for n in range(8)]
        cr, ci = _scan_tiles(lr_ref, li_ref, tab_v, (car_ref[0:1, :], car_ref[8:9, :]), tpb, True)
        car_ref[0:1, :] = cr
        car_ref[8:9, :] = ci
        lam_r = lr_ref[...]
        lam_i = li_ref[...]
        keep = jnp.where(t == nt - 1, 0.0, 1.0)
        sp_r = _shift_down(s_re, 1, hre_ref[...] * keep)
        sp_i = _shift_down(s_im, 1, him_ref[...] * keep)
        dare_ref[...] += jnp.sum(lam_r * sp_r + lam_i * sp_i, axis=0, keepdims=True)
        daim_ref[...] += jnp.sum(lam_i * sp_r - lam_r * sp_i, axis=0, keepdims=True)
        lrb = lam_r.astype(BF16)
        lib = lam_i.astype(BF16)
        du_ref[...] = _dot_nt(lrb, bre_ref[0]) + _dot_nt(lib, bim_ref[0]) + dyf * d_ref[...]
        ub = uf.astype(BF16)
        dbre_ref[0] += _dot_tn(ub, lrb)
        dbim_ref[0] += _dot_tn(ub, lib)
        dd_ref[...] += jnp.sum(dyf * uf, axis=0, keepdims=True)

    rev = lambda m, t: (nt - 1 - t, m)
    halo = lambda m, t: (jnp.maximum((nt - 1 - t) * tpb - 1, 0), m)
    wb = pl.BlockSpec((1, SSM_UB, SSM_CB), lambda m, t: (m, 0, 0))
    wc = pl.BlockSpec((1, SSM_CB, SSM_UB), lambda m, t: (m, 0, 0))
    vec_c = pl.BlockSpec((1, SSM_CB), lambda m, t: (0, m))
    vec_u = pl.BlockSpec((1, SSM_UB), lambda m, t: (0, m))
    return pl.pallas_call(
        body, name="ssm_bwd", grid=(SSM_NB, nt),
        in_specs=[pl.BlockSpec((tt, SSM_UB), rev), pl.BlockSpec((tt, SSM_UB), rev),
                  pl.BlockSpec((tt, SSM_CB), rev), pl.BlockSpec((tt, SSM_CB), rev),
                  pl.BlockSpec((SUBLANES, SSM_CB), halo), pl.BlockSpec((SUBLANES, SSM_CB), halo),
                  wb, wb, wc, wc, vec_u,
                  pl.BlockSpec((8, SUBLANES, SSM_CB), lambda m, t: (0, 0, m))],
        out_specs=[pl.BlockSpec((tt, SSM_UB), rev), wb, wb, wc, wc, vec_c, vec_c, vec_u],
        out_shape=[jax.ShapeDtypeStruct((l, SSM_W), F32),
                   jax.ShapeDtypeStruct((SSM_NB, SSM_UB, SSM_CB), F32), jax.ShapeDtypeStruct((SSM_NB, SSM_UB, SSM_CB), F32),
                   jax.ShapeDtypeStruct((SSM_NB, SSM_CB, SSM_UB), F32), jax.ShapeDtypeStruct((SSM_NB, SSM_CB, SSM_UB), F32),
                   jax.ShapeDtypeStruct((1, SSM_CH), F32), jax.ShapeDtypeStruct((1, SSM_CH), F32),
                   jax.ShapeDtypeStruct((1, SSM_W), F32)],
        scratch_shapes=[pltpu.VMEM((tt, SSM_CB), F32), pltpu.VMEM((tt, SSM_CB), F32),
                        pltpu.VMEM((2 * SUBLANES, SSM_CB), F32)],
        compiler_params=_params(("parallel", "arbitrary")),
    )(dy, u, sre, sim, sre, sim, bre, bim, cre, cim, dvec, tab)


def _merge_fwd(x, gl, attn, y1, wba, wbs, wglu, bglu, wout, gpost, gpre, tl):
    l = x.shape[0]

    def body(x_ref, gl_ref, at_ref, y1_ref, wba_ref, wbs_ref, wglu_ref, bglu_ref, wout_ref, gpost_ref, gpre_ref,
             a_ref, sm_ref, mg_ref, z_ref, x1_ref, hn2_ref, y3_ref):
        y2 = _gelu(y1_ref[...])
        sg = _sigmoid(_dot(y2.astype(BF16), wglu_ref[...]) + bglu_ref[...])
        y3 = (y2 * sg).astype(BF16)
        y3_ref[...] = y3
        a = _dot(at_ref[...], wba_ref[...])
        sm = _dot(y3, wbs_ref[...])
        a_ref[...] = a
        sm_ref[...] = sm
        g = _sigmoid(gl_ref[...])
        merged = (g[:, :D_MODEL] * a + g[:, D_MODEL:] * sm).astype(BF16)
        mg_ref[...] = merged
        z = _dot(merged, wout_ref[...])
        z_ref[...] = z
        n, _ = _rms(z, gpost_ref[...])
        x1 = x_ref[...] + n
        x1_ref[...] = x1
        hn2, _ = _rms(x1, gpre_ref[...])
        hn2_ref[...] = hn2.astype(BF16)

    outs = [(D_MODEL, F32), (D_MODEL, F32), (D_MODEL, BF16), (D_MODEL, F32), (D_MODEL, F32), (D_MODEL, BF16),
            (SSM_W, BF16)]
    return pl.pallas_call(
        body, name="merge_fwd", grid=(l // tl,),
        in_specs=[_row(tl, D_MODEL), _row(tl, 2 * D_MODEL), _row(tl, HP), _row(tl, SSM_W),
                  _const((HP, D_MODEL)), _const((SSM_W, D_MODEL)), _const((SSM_W, SSM_W)), _const((1, SSM_W)),
                  _const((D_MODEL, D_MODEL)), _const((1, D_MODEL)), _const((1, D_MODEL))],
        out_specs=[_row(tl, n) for n, _ in outs],
        out_shape=[jax.ShapeDtypeStruct((l, n), dt) for n, dt in outs],
        compiler_params=_params(("parallel",)),
    )(x, gl, attn, y1, wba, wbs, wglu, bglu, wout, gpost, gpre)


def _merge_bwd(dhn2a, dhn2b, x1, dx2, z, gl, a, sm, y1, wba, wbs, wglu, bglu, wout, gpost, gpre, tl):
    l = x1.shape[0]

    def body(da_ref, db_ref, x1_ref, dx2_ref, z_ref, gl_ref, a_ref, sm_ref, y1_ref,
             wba_ref, wbs_ref, wglu_ref, bglu_ref, wout_ref, gpost_ref, gpre_ref,
             dx1_ref, dz_ref, dbra_ref, dbrs_ref, dgl_ref, dat_ref, dy1_ref, dt_ref, y2_ref,
             dgpre_ref, dgpost_ref, dbg_ref, dbglu_ref):
        @pl.when(pl.program_id(0) == 0)
        def _():
            for ref in (dgpre_ref, dgpost_ref, dbg_ref, dbglu_ref):
                ref[...] = jnp.zeros_like(ref)

        dhn2 = da_ref[...] + db_ref[...]
        dx1a, dgpre = _rms_bwd(dhn2, x1_ref[...], gpre_ref[...])
        dgpre_ref[...] += dgpre
        dx1 = dx2_ref[...] + dx1a
        dx1_ref[...] = dx1
        dz, dgpost = _rms_bwd(dx1, z_ref[...], gpost_ref[...])
        dgpost_ref[...] += dgpost
        dzb = dz.astype(BF16)
        dz_ref[...] = dzb
        dm = _dot_nt(dzb, wout_ref[...])
        g = _sigmoid(gl_ref[...])
        g0 = g[:, :D_MODEL]
        g1 = g[:, D_MODEL:]
        dbra = (dm * g0).astype(BF16)
        dbrs = (dm * g1).astype(BF16)
        dbra_ref[...] = dbra
        dbrs_ref[...] = dbrs
        dgl0 = dm * a_ref[...] * g0 * (1.0 - g0)
        dgl1 = dm * sm_ref[...] * g1 * (1.0 - g1)
        dgl_ref[:, :D_MODEL] = dgl0.astype(BF16)
        dgl_ref[:, D_MODEL:] = dgl1.astype(BF16)
        dbg_ref[:, :D_MODEL] += jnp.sum(dgl0, axis=0, keepdims=True)
        dbg_ref[:, D_MODEL:] += jnp.sum(dgl1, axis=0, keepdims=True)
        dat_ref[...] = _dot_nt(dbra, wba_ref[...]).astype(BF16)
        dy3 = _dot_nt(dbrs, wbs_ref[...])
        y1v = y1_ref[...]
        y2 = _gelu(y1v)
        y2b = y2.astype(BF16)
        y2_ref[...] = y2b
        sg = _sigmoid(_dot(y2b, wglu_ref[...]) + bglu_ref[...])
        dt = dy3 * y2 * sg * (1.0 - sg)
        dtb = dt.astype(BF16)
        dt_ref[...] = dtb
        dbglu_ref[...] += jnp.sum(dt, axis=0, keepdims=True)
        dy2 = dy3 * sg + _dot_nt(dtb, wglu_ref[...])
        dy1_ref[...] = dy2 * _gelu_grad(y1v)

    outs = [(D_MODEL, F32), (D_MODEL, BF16), (D_MODEL, BF16), (D_MODEL, BF16), (2 * D_MODEL, BF16), (HP, BF16),
            (SSM_W, F32), (SSM_W, BF16), (SSM_W, BF16)]
    accs = [D_MODEL, D_MODEL, 2 * D_MODEL, SSM_W]
    return pl.pallas_call(
        body, name="merge_bwd", grid=(l // tl,),
        in_specs=[_row(tl, D_MODEL), _row(tl, D_MODEL), _row(tl, D_MODEL), _row(tl, D_MODEL), _row(tl, D_MODEL),
                  _row(tl, 2 * D_MODEL), _row(tl, D_MODEL), _row(tl, D_MODEL), _row(tl, SSM_W),
                  _const((HP, D_MODEL)), _const((SSM_W, D_MODEL)), _const((SSM_W, SSM_W)), _const((1, SSM_W)),
                  _const((D_MODEL, D_MODEL)), _const((1, D_MODEL)), _const((1, D_MODEL))],
        out_specs=[_row(tl, n) for n, _ in outs] + [_const((1, n)) for n in accs],
        out_shape=[jax.ShapeDtypeStruct((l, n), dt) for n, dt in outs]
        + [jax.ShapeDtypeStruct((1, n), F32) for n in accs],
        compiler_params=_params(("arbitrary",)),
    )(dhn2a, dhn2b, x1, dx2, z, gl, a, sm, y1, wba, wbs, wglu, bglu, wout, gpost, gpre)


def _proj_bwd(x, dx1, cq, ckv, dq, dk, dv, du, dgl, g1, win, gq, wuq, gkv, wukv, rc, rs, tl):
    l = x.shape[0]

    def body(x_ref, dx1_ref, cq_ref, ckv_ref, dq_ref, dk_ref, dv_ref, du_ref, dgl_ref,
             g1_ref, win_ref, gq_ref, wuq_ref, gkv_ref, wukv_ref, rc_ref, rs_ref,
             gx_ref, dql_ref, qn_ref, ckvn_ref, dproj_ref, dg1_ref, dgq_ref, dgkv_ref):
        @pl.when(pl.program_id(0) == 0)
        def _():
            for ref in (dg1_ref, dgq_ref, dgkv_ref):
                ref[...] = jnp.zeros_like(ref)

        c1 = rc_ref[...]
        s1 = rs_ref[...]
        dql = _rope_bwd(dq_ref[...], jnp.tile(c1, (1, N_HEADS)), jnp.tile(s1, (1, N_HEADS))).astype(BF16)
        dql_ref[...] = dql
        dqn = _dot_nt(dql, wuq_ref[...])
        cq = cq_ref[...]
        qn, _ = _rms(cq, gq_ref[...])
        qn_ref[...] = qn.astype(BF16)
        dcq, dgq = _rms_bwd(dqn, cq, gq_ref[...])
        dgq_ref[...] += dgq
        dkb = dk_ref[...]
        dvb = dv_ref[...]
        dkf = dkb.astype(F32)
        dkr = dkf[:, 0:HEAD_SLOT]
        for h in range(1, N_HEADS):
            dkr = dkr + dkf[:, h * HEAD_SLOT:(h + 1) * HEAD_SLOT]
        dkr = _rope_bwd(dkr, c1, s1)
        dckvn = _dot_nt(dkb, wukv_ref[:, :HP]) + _dot_nt(dvb, wukv_ref[:, HP:])
        ckv = ckv_ref[...]
        ckvn, _ = _rms(ckv, gkv_ref[...])
        ckvn_ref[...] = ckvn.astype(BF16)
        dckv, dgkv = _rms_bwd(dckvn, ckv, gkv_ref[...])
        dgkv_ref[...] += dgkv
        dproj_ref[:, P_CQ:P_CKV] = dcq.astype(BF16)
        dproj_ref[:, P_CKV:P_KR] = dckv.astype(BF16)
        dproj_ref[:, P_KR:P_U] = dkr.astype(BF16)
        dproj_ref[:, P_U:P_GL] = du_ref[...].astype(BF16)
        dproj_ref[:, P_GL:P_END] = dgl_ref[...]
        dhn = _dot_nt(dproj_ref[...], win_ref[...])
        dxa, dg1 = _rms_bwd(dhn, x_ref[...], g1_ref[...])
        dg1_ref[...] += dg1
        gx_ref[...] = dx1_ref[...] + dxa

    outs = [(D_MODEL, F32), (HP, BF16), (Q_RANK, BF16), (KV_RANK, BF16), (P_END, BF16)]
    accs = [D_MODEL, Q_RANK, KV_RANK]
    return pl.pallas_call(
        body, name="proj_bwd", grid=(l // tl,),
        in_specs=[_row(tl, D_MODEL), _row(tl, D_MODEL), _row(tl, Q_RANK), _row(tl, KV_RANK), _row(tl, HP),
                  _row(tl, HP), _row(tl, HP), _row(tl, SSM_W), _row(tl, 2 * D_MODEL),
                  _const((1, D_MODEL)), _const((D_MODEL, P_END)), _const((1, Q_RANK)), _const((Q_RANK, HP)),
                  _const((1, KV_RANK)), _const((KV_RANK, 2 * HP)), _row(tl, HEAD_SLOT), _row(tl, HEAD_SLOT)],
        out_specs=[_row(tl, n) for n, _ in outs] + [_const((1, n)) for n in accs],
        out_shape=[jax.ShapeDtypeStruct((l, n), dt) for n, dt in outs]
        + [jax.ShapeDtypeStruct((1, n), F32) for n in accs],
        compiler_params=_params(("arbitrary",)),
    )(x, dx1, cq, ckv, dq, dk, dv, du, dgl, g1, win, gq, wuq, gkv, wukv, rc, rs)


CONV_CB = 256
CONV_NB = D_FF // CONV_CB
CONV_ROWS = 16


def _conv3(h, halo, w, b):
    return b + w[0:1, :] * _shift_down(h, 2, halo) + w[1:2, :] * _shift_down(h, 1, halo) + w[2:3, :] * h


def _conv_fwd(h, cw, cb, tl):
    l = h.shape[0]

    def body(hg_ref, hv_ref, wg_ref, wv_ref, bg_ref, bv_ref, act_ref, halo_ref):
        @pl.when(pl.program_id(1) == 0)
        def _():
            halo_ref[...] = jnp.zeros_like(halo_ref)

        hg = hg_ref[...]
        hv = hv_ref[...]
        cg = _conv3(hg, halo_ref[0:SUBLANES, :], wg_ref[...], bg_ref[...])
        cv = _conv3(hv, halo_ref[SUBLANES:, :], wv_ref[...], bv_ref[...])
        act_ref[...] = (_gelu(cg) * cv).astype(BF16)
        halo_ref[0:SUBLANES, :] = hg[tl - SUBLANES:, :]
        halo_ref[SUBLANES:, :] = hv[tl - SUBLANES:, :]

    gmap = lambda c, r: (r, c)
    vmap = lambda c, r: (r, CONV_NB + c)
    return pl.pallas_call(
        body, name="conv_fwd", grid=(CONV_NB, l // tl),
        in_specs=[pl.BlockSpec((tl, CONV_CB), gmap), pl.BlockSpec((tl, CONV_CB), vmap),
                  pl.BlockSpec((3, CONV_CB), lambda c, r: (0, c)), pl.BlockSpec((3, CONV_CB), lambda c, r: (0, CONV_NB + c)),
                  pl.BlockSpec((1, CONV_CB), lambda c, r: (0, c)), pl.BlockSpec((1, CONV_CB), lambda c, r: (0, CONV_NB + c))],
        out_specs=pl.BlockSpec((tl, CONV_CB), gmap),
        out_shape=jax.ShapeDtypeStruct((l, D_FF), BF16),
        scratch_shapes=[pltpu.VMEM((2 * SUBLANES, CONV_CB), F32)],
        compiler_params=_params(("parallel", "arbitrary")),
    )(h, h, cw, cw, cb, cb)


def _conv_bwd(h, dact, cw, cb, tl):
    l = h.shape[0]
    nr = l // tl
    tpb = tl // SUBLANES

    def body(hg_ref, hv_ref, hgh_ref, hvh_ref, da_ref, wg_ref, wv_ref, bg_ref, bv_ref,
             dh_ref, dwg_ref, dwv_ref, dbg_ref, dbv_ref, car_ref):
        r = pl.program_id(1)

        @pl.when(r == 0)
        def _():
            for ref in (car_ref, dwg_ref, dwv_ref, dbg_ref, dbv_ref):
                ref[...] = jnp.zeros_like(ref)

        keep = jnp.where(r == nr - 1, 0.0, 1.0)
        wg, wv, bg, bv = wg_ref[...], wv_ref[...], bg_ref[...], bv_ref[...]
        nch = tl // CONV_ROWS

        def fold(x):
            s = x[0:SUBLANES, :]
            for k in range(1, CONV_ROWS // SUBLANES):
                s = s + x[k * SUBLANES:(k + 1) * SUBLANES, :]
            return s

        def chunk(n, carry):
            ncg, ncv, acc = carry
            idx = nch - 1 - n
            r0 = pl.multiple_of(idx * CONV_ROWS, CONV_ROWS)
            rows = pl.ds(r0, CONV_ROWS)
            before = pl.ds(pl.multiple_of(jnp.maximum(r0 - SUBLANES, 0), SUBLANES), SUBLANES)
            in_tile = idx > 0
            da = da_ref[rows, :].astype(F32)

            def half(h_ref, halo_ref, w, b):
                hh = h_ref[rows, :]
                prev = jnp.where(in_tile, h_ref[before, :], halo_ref[...] * keep)
                h1 = _shift_down(hh, 1, prev)
                h2 = _shift_down(hh, 2, prev)
                return hh, h1, h2, b + w[0:1, :] * h2 + w[1:2, :] * h1 + w[2:3, :] * hh

            hg, hg1, hg2, cg = half(hg_ref, hgh_ref, wg, bg)
            hv, hv1, hv2, cv = half(hv_ref, hvh_ref, wv, bv)
            dcg = da * cv * _gelu_grad(cg)
            dcv = da * _gelu(cg)

            def back(dc, hh, h1, h2, w, nxt, part):
                dh = w[2:3, :] * dc + w[1:2, :] * _shift_up(dc, 1, nxt) + w[0:1, :] * _shift_up(dc, 2, nxt)
                dh_ref[part, rows, :] = dh.astype(BF16)
                return [fold(dc * h2), fold(dc * h1), fold(dc * hh), fold(dc)]

            sums = back(dcg, hg, hg1, hg2, wg, ncg, 0) + back(dcv, hv, hv1, hv2, wv, ncv, 1)
            return dcg[0:SUBLANES, :], dcv[0:SUBLANES, :], [a + s for a, s in zip(acc, sums)]

        zero = jnp.zeros((SUBLANES, CONV_CB), F32)
        ncg, ncv, acc = lax.fori_loop(0, nch, chunk, (car_ref[0:SUBLANES, :], car_ref[SUBLANES:, :], [zero] * 8))
        car_ref[0:SUBLANES, :] = ncg
        car_ref[SUBLANES:, :] = ncv
        for half_acc, dw_ref, db_ref in ((acc[0:4], dwg_ref, dbg_ref), (acc[4:8], dwv_ref, dbv_ref)):
            for k in range(3):
                dw_ref[k:k + 1, :] += jnp.sum(half_acc[k], axis=0, keepdims=True)
            db_ref[...] += jnp.sum(half_acc[3], axis=0, keepdims=True)

    grev = lambda c, r: (nr - 1 - r, c)
    vrev = lambda c, r: (nr - 1 - r, CONV_NB + c)
    ghalo = lambda c, r: (jnp.maximum((nr - 1 - r) * tpb - 1, 0), c)
    vhalo = lambda c, r: (jnp.maximum((nr - 1 - r) * tpb - 1, 0), CONV_NB + c)
    colg = lambda c, r: (0, c)
    colv = lambda c, r: (0, CONV_NB + c)
    return pl.pallas_call(
        body, name="conv_bwd", grid=(CONV_NB, nr),
        in_specs=[pl.BlockSpec((tl, CONV_CB), grev), pl.BlockSpec((tl, CONV_CB), vrev),
                  pl.BlockSpec((SUBLANES, CONV_CB), ghalo), pl.BlockSpec((SUBLANES, CONV_CB), vhalo),
                  pl.BlockSpec((tl, CONV_CB), grev),
                  pl.BlockSpec((3, CONV_CB), colg), pl.BlockSpec((3, CONV_CB), colv),
                  pl.BlockSpec((1, CONV_CB), colg), pl.BlockSpec((1, CONV_CB), colv)],
        out_specs=[pl.BlockSpec((2, tl, CONV_CB), lambda c, r: (0, nr - 1 - r, c)),
                   pl.BlockSpec((3, CONV_CB), colg), pl.BlockSpec((3, CONV_CB), colg),
                   pl.BlockSpec((1, CONV_CB), colg), pl.BlockSpec((1, CONV_CB), colg)],
        out_shape=[jax.ShapeDtypeStruct((2, l, D_FF), BF16),
                   jax.ShapeDtypeStruct((3, D_FF), F32), jax.ShapeDtypeStruct((3, D_FF), F32),
                   jax.ShapeDtypeStruct((1, D_FF), F32), jax.ShapeDtypeStruct((1, D_FF), F32)],
        scratch_shapes=[pltpu.VMEM((2 * SUBLANES, CONV_CB), F32)],
        compiler_params=_params(("parallel", "arbitrary")),
    )(h, h, h, h, dact, cw, cw, cb, cb)


def _loss_head(ff, x1, tgt, g, tl):
    l = ff.shape[0]

    def body(ff_ref, x1_ref, tg_ref, g_ref, loss_ref, dx2_ref, dff_ref, dg_ref):
        @pl.when(pl.program_id(0) == 0)
        def _():
            loss_ref[...] = jnp.zeros_like(loss_ref)
            dg_ref[...] = jnp.zeros_like(dg_ref)

        f = ff_ref[...]
        gv = g_ref[...]
        n, _ = _rms(f, gv)
        e = x1_ref[...] + n - tg_ref[...]
        loss_ref[...] += 0.5 * jnp.sum(jnp.mean(e * e, axis=-1, keepdims=True), axis=0, keepdims=True)
        dx2 = e * (1.0 / D_MODEL)
        dx2_ref[...] = dx2
        dff, dg = _rms_bwd(dx2, f, gv)
        dff_ref[...] = dff.astype(BF16)
        dg_ref[...] += dg

    return pl.pallas_call(
        body, name="loss_head", grid=(l // tl,),
        in_specs=[_row(tl, D_MODEL), _row(tl, D_MODEL), _row(tl, D_MODEL), _const((1, D_MODEL))],
        out_specs=[_const((1, LANES)), _row(tl, D_MODEL), _row(tl, D_MODEL), _const((1, D_MODEL))],
        out_shape=[jax.ShapeDtypeStruct((1, LANES), F32), jax.ShapeDtypeStruct((l, D_MODEL), F32),
                   jax.ShapeDtypeStruct((l, D_MODEL), BF16), jax.ShapeDtypeStruct((1, D_MODEL), F32)],
        compiler_params=_params(("arbitrary",)),
    )(ff, x1, tgt, g)


def _ssm_disc(lam_re, lam_im, log_dt, b_re, b_im):
    dt = jnp.exp(log_dt)[:, None]
    mag = jnp.exp(lam_re * dt)
    ang = lam_im * dt
    a_re, a_im = mag * jnp.cos(ang), mag * jnp.sin(ang)
    den = lam_re * lam_re + lam_im * lam_im
    n_re, n_im = a_re - 1.0, a_im
    z_re = (n_re * lam_re + n_im * lam_im) / den
    z_im = (n_im * lam_re - n_re * lam_im) / den
    bb_re = z_re[..., None] * b_re - z_im[..., None] * b_im
    bb_im = z_re[..., None] * b_im + z_im[..., None] * b_re
    return a_re, a_im, bb_re, bb_im


_GPB = SSM_CB // SSM_P


def _embed_b(bb):
    t = bb.transpose(0, 2, 1).reshape(SSM_NB, _GPB, SSM_H, SSM_P)
    return jnp.einsum('mjhp,jk->mjhkp', t, jnp.eye(_GPB, dtype=bb.dtype)).reshape(SSM_NB, SSM_UB, SSM_CB)


def _extract_b(d):
    t = d.reshape(SSM_NB, _GPB, SSM_H, _GPB, SSM_P)
    t = jnp.einsum('mjhkp,jk->mjhp', t, jnp.eye(_GPB, dtype=d.dtype))
    return t.reshape(SSM_G, SSM_H, SSM_P).transpose(0, 2, 1)


def _embed_c(c):
    t = c.transpose(0, 2, 1).reshape(SSM_NB, _GPB, SSM_P, SSM_H)
    return jnp.einsum('mjph,jk->mjpkh', t, jnp.eye(_GPB, dtype=c.dtype)).reshape(SSM_NB, SSM_CB, SSM_UB)


def _extract_c(d):
    t = d.reshape(SSM_NB, _GPB, SSM_P, _GPB, SSM_H)
    t = jnp.einsum('mjpkh,jk->mjph', t, jnp.eye(_GPB, dtype=d.dtype))
    return t.reshape(SSM_G, SSM_P, SSM_H).transpose(0, 2, 1)


def _scan_tables(a_re, a_im, reverse):
    ar = a_re.reshape(1, SSM_CH)
    ai = (-a_im if reverse else a_im).reshape(1, SSM_CH)
    pr, pi = [ar], [ai]
    for _ in range(SUBLANES - 1):
        pr, pi = pr + [pr[-1] * ar - pi[-1] * ai], pi + [pr[-1] * ai + pi[-1] * ar]
    rows = jnp.arange(SUBLANES)[:, None]
    out = []
    for k in (1, 2, 4):
        valid = (rows + k <= SUBLANES - 1) if reverse else (rows >= k)
        out += [jnp.where(valid, pr[k - 1], 0.0), jnp.where(valid, pi[k - 1], 0.0)]
    order = list(range(SUBLANES - 1, -1, -1)) if reverse else list(range(SUBLANES))
    out += [jnp.concatenate([pr[n] for n in order], axis=0), jnp.concatenate([pi[n] for n in order], axis=0)]
    return jnp.stack(out).astype(F32)


def _pad_heads(w, d):
    lead = w.shape[:-1]
    w = w.reshape(lead + (N_HEADS, d))
    w = jnp.pad(w, [(0, 0)] * len(lead) + [(0, 0), (0, HEAD_SLOT - d)])
    return w.reshape(lead + (HP,))


def _unpad_heads(w, d):
    lead = w.shape[:-1]
    return w.reshape(lead + (N_HEADS, HEAD_SLOT))[..., :d].reshape(lead + (N_HEADS * d,))


def _chip_major(w, axis):
    k, n = w.shape
    if axis == 0:
        return w.reshape(N_CHIPS, k // N_CHIPS, n)
    return w.reshape(k, N_CHIPS, n // N_CHIPS).transpose(1, 0, 2)


def _from_chip_major(w, axis):
    if axis == 0:
        return w.reshape(-1, w.shape[2])
    return w.transpose(1, 0, 2).reshape(w.shape[1], -1)


def _pad_w_in(w):
    z = lambda n: jnp.zeros((w.shape[0], n), w.dtype)
    return jnp.concatenate([w[:, :640], z(KR_LANE), w[:, 640:672], z(HEAD_SLOT - KR_LANE - QK_ROPE), w[:, 672:]], axis=1)


def _unpad_w_in(w):
    return jnp.concatenate([w[:, :640], w[:, P_KR + KR_LANE:P_KR + KR_LANE + QK_ROPE], w[:, P_U:]], axis=1)


def _local_step(x, positions, tgt, wts, sp):
    l = x.shape[0]
    tl = min(256, l)
    ta = min(512, l)

    inv_freq = ROPE_THETA ** (-jnp.arange(0, QK_ROPE, 2, dtype=F32) / QK_ROPE)
    ang = positions.astype(F32)[:, None] * inv_freq
    cos, sin = jnp.cos(ang), jnp.sin(ang)
    one = jnp.ones((l, KR_LANE), F32)
    rc = jnp.concatenate([one, cos, cos, jnp.ones((l, HEAD_SLOT - KR_LANE - QK_ROPE), F32)], axis=1)
    rs = jnp.concatenate([0 * one, -sin, sin, jnp.zeros((l, HEAD_SLOT - KR_LANE - QK_ROPE), F32)], axis=1)

    win = _pad_w_in(wts["w_in"])
    wuq = _pad_heads(wts["w_uq"], QK_HEAD)
    wukv = jnp.concatenate([_pad_heads(wts["w_uk"], QK_NOPE), _pad_heads(wts["w_uv"], V_HEAD)], axis=1)

    disc_in = (sp["ssm_lambda_re"], sp["ssm_lambda_im"], sp["ssm_log_dt"], sp["ssm_b_re"], sp["ssm_b_im"])
    (a_re, a_im, bb_re, bb_im), disc_vjp = jax.vjp(_ssm_disc, *disc_in)
    bre, bim = _embed_b(bb_re).astype(BF16), _embed_b(bb_im).astype(BF16)
    cre, cim = _embed_c(sp["ssm_c_re"]).astype(BF16), _embed_c(sp["ssm_c_im"]).astype(BF16)
    dvec = sp["ssm_d"].reshape(1, SSM_W)
    tab_f = _scan_tables(a_re, a_im, False)
    tab_r = _scan_tables(a_re, a_im, True)

    g1, gq, gkv = sp["mix_norm_pre"], sp["q_norm"], sp["kv_norm"]
    gpost, gpre, gfin = sp["mix_norm_post"], sp["ffn_norm_pre"], sp["ffn_norm_post"]
    bgate, bglu, convb = sp["b_gate"], sp["b_glu"], sp["conv_b"]

    hn, cq, ckv, q, k, v, u, gl = _proj_fwd(x, g1, win, gq, wuq, gkv, wukv, rc, rs, bgate, tl)
    attn, lse = _attn_fwd(q, k, v, ta)
    y1, sre, sim = _ssm_fwd(u, bre, bim, cre, cim, dvec, tab_f, ta)
    wba = jnp.pad(wts["w_branch_attn"].reshape(N_HEADS, V_HEAD, D_MODEL),
                  ((0, 0), (0, HEAD_SLOT - V_HEAD), (0, 0))).reshape(HP, D_MODEL)
    wbs, wglu, wout = wts["w_branch_ssm"], wts["w_glu"], wts["w_out"]
    a, sm, merged, z, x1, hn2, y3 = _merge_fwd(x, gl, attn, y1, wba, wbs, wglu, bglu, wout, gpost, gpre, tl)
    late = wts["late"](x1)
    wup, wdown, convw = late["w_up"], late["w_down"], late["conv_w"]
    h = _mm(hn2, wup, "ffn_up")
    act = _conv_fwd(h, convw, convb, ta)
    ff = _mm(act, wdown, "ffn_down")
    loss, dx2, dff, dgfin = _loss_head(ff, x1, tgt, gfin, tl)

    dact = _mm(dff, wdown, "ffn_down_dx", out_dtype=BF16, bt=True)
    d_wdown = _mm_tn(act, dff, "ffn_down_dw", tk_cap=D_FF // 2)
    dh, dwg, dwv, dbg, dbv = _conv_bwd(h, dact, convw, convb, ta)
    d_convw = jnp.concatenate([dwg, dwv], axis=1)
    d_convb = jnp.concatenate([dbg, dbv], axis=1)
    dhn2a = _mm(dh, wup, "ffn_up_dx_gate", bt=True, b_col0=0, a_lead=0)
    dhn2b = _mm(dh, wup, "ffn_up_dx_val", bt=True, b_col0=1, a_lead=1)
    d_wup = _mm_tn(hn2, dh, "ffn_up_dw", chips=True)
    behind = wts["send_grads"]("ffn", {"w_up": d_wup, "w_down": _chip_major(d_wdown, 0)})
    (dx1, dz, dbra, dbrs, dgl, dattn, dy1, dt, y2, dgpre, dgpost, dbgate, dbglu) = _merge_bwd(
        dhn2a, dhn2b, x1, dx2, z, gl, a, sm, y1, wba, wbs, wglu, bglu, wout, gpost, gpre + behind, tl)
    d_wout = _mm_tn(merged, dz, "w_out_dw")
    d_wba = _mm_tn(attn, dbra, "w_branch_attn_dw", chips=True)
    d_wbs = _mm_tn(y3, dbrs, "w_branch_ssm_dw", chips=True)
    d_wglu = _mm_tn(y2, dt, "w_glu_dw")
    ncol = D_MODEL // N_CHIPS
    behind = wts["send_grads"]("mix", {
        "w_glu": _chip_major(d_wglu, 0),
        "w_branch_attn": d_wba.reshape(N_CHIPS, N_HEADS, HEAD_SLOT, ncol)[:, :, :V_HEAD].reshape(
            N_CHIPS, N_HEADS * V_HEAD, ncol),
        "w_branch_ssm": d_wbs,
        "w_out": _chip_major(d_wout, 0)})
    dq, dk, dv = _attn_bwd(q, k, v, dattn, lse + behind, _attn_delta(attn, dattn, min(2048, l)), ta)
    du, dbre, dbim, dcre, dcim, dare, daim, dd = _ssm_bwd(dy1, u, sre, sim, bre, bim, cre, cim, dvec, tab_r, ta)
    gx, dql, qn, ckvn, dproj, dg1, dgq, dgkv = _proj_bwd(
        x, dx1, cq, ckv, dq, dk, dv, du, dgl, g1, win, gq, wuq, gkv, wukv, rc, rs, tl)
    d_win = _mm_tn(hn, dproj, "w_in_dw")
    d_wuq = _mm_tn(qn, dql, "w_uq_dw")
    d_wuk = _mm_tn(ckvn, dk, "w_uk_dw")
    d_wuv = _mm_tn(ckvn, dv, "w_uv_dw")

    d_lre, d_lim, d_ldt, d_bre, d_bim = disc_vjp((dare.reshape(SSM_G, SSM_P), daim.reshape(SSM_G, SSM_P),
                                                  _extract_b(dbre), _extract_b(dbim)))
    big = {
        "w_in": _chip_major(_unpad_w_in(d_win), 1),
        "w_uq": _chip_major(_unpad_heads(d_wuq, QK_HEAD), 1),
        "w_uk": _chip_major(_unpad_heads(d_wuk, QK_NOPE), 1),
        "w_uv": _chip_major(_unpad_heads(d_wuv, V_HEAD), 1),
    }
    small = {
        "conv_w": d_convw,
        "mix_norm_pre": dg1, "q_norm": dgq, "kv_norm": dgkv,
        "ssm_lambda_re": d_lre, "ssm_lambda_im": d_lim, "ssm_log_dt": d_ldt,
        "ssm_b_re": d_bre, "ssm_b_im": d_bim,
        "ssm_c_re": _extract_c(dcre), "ssm_c_im": _extract_c(dcim),
        "ssm_d": dd.reshape(SSM_G, SSM_H), "b_glu": dbglu, "b_gate": dbgate,
        "mix_norm_post": dgpost, "ffn_norm_pre": dgpre, "conv_b": d_convb, "ffn_norm_post": dgfin,
    }
    return loss[0, 0], gx, big, small


_ANY = pl.BlockSpec(memory_space=pl.ANY)


ROW_TILE = 16


def _place():
    x, y, c = lax.axis_index("x"), lax.axis_index("y"), lax.axis_index("c")
    return x, y, c, 2 * x + y, [(1 - x, y), (x, 1 - y), (1 - x, 1 - y)]


def _half(rows, which):
    hr = rows // 2
    return pl.ds(pl.multiple_of(which * hr, ROW_TILE), hr)


def _remote(src, dst, send_sems, recv_sems, n, dev):
    return pltpu.make_async_remote_copy(src_ref=src, dst_ref=dst, send_sem=send_sems.at[n], recv_sem=recv_sems.at[n],
                                        device_id=dev, device_id_type=MESH)


def _gather_big(shards):
    nw = len(shards)
    rows = [s.shape[0] for s in shards]

    def body(*refs):
        ins, outs = refs[:nw], refs[nw:2 * nw]
        ici_send, ici_recv, d2d_send, d2d_recv = refs[2 * nw:]
        x, y, c, me, peers = _place()
        sent = []
        for i in range(nw):
            for p, (px, py) in enumerate(peers):
                cp = _remote(ins[i].at[_half(rows[i], c)], outs[i].at[me, _half(rows[i], c)], ici_send, ici_recv,
                             3 * i + p, (px, py, c))
                cp.start()
                sent.append(cp)
        for p, (px, py) in enumerate(peers):
            for i in range(nw):
                blk = outs[i].at[2 * px + py, _half(rows[i], c)]
                _remote(blk, blk, ici_send, ici_recv, 3 * i + p, (px, py, c)).wait_recv()
                cp = _remote(blk, blk, d2d_send, d2d_recv, 3 * i + p, (x, y, 1 - c))
                cp.start()
                sent.append(cp)
        for p, (px, py) in enumerate(peers):
            for i in range(nw):
                blk = outs[i].at[2 * px + py, _half(rows[i], 1 - c)]
                _remote(blk, blk, d2d_send, d2d_recv, 3 * i + p, (x, y, 1 - c)).wait_recv()
        for cp in sent:
            cp.wait_send()

    dma = pltpu.SemaphoreType.DMA
    return pl.pallas_call(
        body, name="gather_weights", in_specs=[_ANY] * nw, out_specs=[_ANY] * nw,
        out_shape=[jax.ShapeDtypeStruct((N_CHIPS,) + s.shape, s.dtype) for s in shards],
        scratch_shapes=[dma((3 * nw,)), dma((3 * nw,)), dma((3 * nw,)), dma((3 * nw,))],
    )(*shards)


_HBM = pl.BlockSpec(memory_space=pltpu.HBM)
_SEM = pl.BlockSpec(memory_space=pltpu.SEMAPHORE)
_DATAFLOW = pltpu.SideEffectType.DATAFLOW_SIDE_EFFECTING


def _exchange_start(shards, name, scatter):
    nw = len(shards)
    lands = [lax.empty(s.shape if scatter else (N_CHIPS,) + s.shape, s.dtype) for s in shards]

    def body(*refs):
        ins, zones = refs[:nw], refs[nw:2 * nw]
        send_sems, recv_sems, token = refs[2 * nw], refs[2 * nw + 1], refs[-1]
        x, y, c, me, peers = _place()
        for i in range(nw):
            for p, (px, py) in enumerate(peers):
                src = ins[i].at[2 * px + py] if scatter else ins[i]
                _remote(src, zones[i].at[me], send_sems, recv_sems, 3 * i + p, (px, py, c)).start()
        token[...] = jnp.zeros_like(token)

    thru = [pltpu.HBM(a.shape, a.dtype) for a in list(shards) + lands]
    dma = pltpu.SemaphoreType.DMA
    outs = pl.pallas_call(
        body, name=name,
        out_shape=(dma((3 * nw,)), dma((3 * nw,)), *thru, jax.ShapeDtypeStruct((SUBLANES, LANES), F32)),
        in_specs=[_HBM] * (2 * nw),
        out_specs=(_SEM, _SEM, *([_HBM] * (2 * nw)), pl.BlockSpec(memory_space=pltpu.VMEM)),
        input_output_aliases={i: 2 + i for i in range(2 * nw)},
        compiler_params=pltpu.CompilerParams(has_side_effects=_DATAFLOW),
    )(*[pltpu.with_memory_space_constraint(a, pltpu.HBM) for a in list(shards) + lands])
    return outs[0], outs[1], list(outs[2:2 + nw]), list(outs[2 + nw:2 + 2 * nw]), outs[-1]


def _exchange_wait(send_sems, recv_sems, shards, lands, after, name, scatter):
    nw = len(shards)

    def body(*refs):
        ins, zones = refs[:nw], refs[nw:2 * nw]
        send_sems, recv_sems = refs[2 * nw], refs[2 * nw + 1]
        x, y, c, me, peers = _place()
        for i in range(nw):
            for p, (px, py) in enumerate(peers):
                src = ins[i].at[2 * px + py] if scatter else ins[i]
                cp = _remote(src, zones[i].at[2 * px + py], send_sems, recv_sems, 3 * i + p, (px, py, c))
                cp.wait_send()
                cp.wait_recv()

    both = list(shards) + list(lands)
    outs = pl.pallas_call(
        body, name=name,
        out_shape=tuple(pltpu.HBM(a.shape, a.dtype) for a in both),
        in_specs=(*([_HBM] * (2 * nw)), _SEM, _SEM, _ANY), out_specs=[_HBM] * (2 * nw),
        input_output_aliases={i: i for i in range(2 * nw)},
        compiler_params=pltpu.CompilerParams(has_side_effects=_DATAFLOW),
    )(*both, send_sems, recv_sems, after)
    return list(outs[:nw]), list(outs[nw:])


def _reduce_to_sibling(grads, name):
    nw = len(grads)

    def body(*refs):
        ins, outs = refs[:nw], refs[nw:2 * nw]
        send_sems, recv_sems = refs[2 * nw:]
        x, y, c, _, _ = _place()
        sent = []
        for i in range(nw):
            cp = _remote(ins[i].at[pl.ds(0, N_CHIPS), _half(grads[i].shape[1], 1 - c)], outs[i], send_sems, recv_sems,
                         i, (x, y, 1 - c))
            cp.start()
            sent.append(cp)
        for cp in sent:
            cp.wait()

    dma = pltpu.SemaphoreType.DMA
    return pl.pallas_call(
        body, name=name, in_specs=[_ANY] * nw, out_specs=[_ANY] * nw,
        out_shape=[jax.ShapeDtypeStruct((N_CHIPS, g.shape[1] // 2, g.shape[2]), g.dtype) for g in grads],
        scratch_shapes=[dma((nw,)), dma((nw,))],
    )(*grads)


def _reduce_between_chips(pairs):
    nw = len(pairs)

    def body(*refs):
        ins, outs = refs[:nw], refs[nw:2 * nw]
        send_sems, recv_sems = refs[2 * nw:]
        x, y, c, me, peers = _place()
        sent = []
        for i in range(nw):
            for p, (px, py) in enumerate(peers):
                cp = _remote(ins[i].at[2 * px + py], outs[i].at[me], send_sems, recv_sems, 3 * i + p, (px, py, c))
                cp.start()
                sent.append(cp)
        for i in range(nw):
            for p, (px, py) in enumerate(peers):
                blk = outs[i].at[2 * px + py]
                _remote(blk, blk, send_sems, recv_sems, 3 * i + p, (px, py, c)).wait_recv()
        for cp in sent:
            cp.wait_send()

    dma = pltpu.SemaphoreType.DMA
    return pl.pallas_call(
        body, name="reduce_grads_ici", in_specs=[_ANY] * nw, out_specs=[_ANY] * nw,
        out_shape=[jax.ShapeDtypeStruct(p.shape, p.dtype) for p in pairs],
        scratch_shapes=[dma((3 * nw,)), dma((3 * nw,))],
    )(*pairs)


def _reduce_back(totals):
    nw = len(totals)

    def body(*refs):
        outs = refs[nw:2 * nw]
        send_sems, recv_sems = refs[2 * nw:]
        x, y, c, _, _ = _place()
        sent = []
        for i in range(nw):
            blk = outs[i].at[_half(totals[i].shape[0], c)]
            cp = _remote(blk, blk, send_sems, recv_sems, i, (x, y, 1 - c))
            cp.start()
            sent.append(cp)
        for i in range(nw):
            blk = outs[i].at[_half(totals[i].shape[0], 1 - c)]
            _remote(blk, blk, send_sems, recv_sems, i, (x, y, 1 - c)).wait_recv()
        for cp in sent:
            cp.wait_send()

    dma = pltpu.SemaphoreType.DMA
    return pl.pallas_call(
        body, name="reduce_grads_back", in_specs=[_ANY] * nw, out_specs=[_ANY] * nw,
        out_shape=[jax.ShapeDtypeStruct(t.shape, t.dtype) for t in totals],
        input_output_aliases={i: i for i in range(nw)},
        scratch_shapes=[dma((nw,)), dma((nw,))],
    )(*totals)


def _all_reduce_small(v, name):
    rows, w = v.shape
    hr = rows // 2
    assert hr % SUBLANES == 0

    def body(v_ref, out_ref, sib_ref, half_ref, chips_ref, send_sems, recv_sems):
        x, y, c, me, peers = _place()
        sibling = (x, y, 1 - c)
        mine = pl.ds(pl.multiple_of(c * hr, SUBLANES), hr)
        other = pl.ds(pl.multiple_of((1 - c) * hr, SUBLANES), hr)
        cp = _remote(v_ref, sib_ref, send_sems, recv_sems, 0, sibling)
        cp.start()
        cp.wait()
        half_ref[...] = v_ref[mine, :] + sib_ref[mine, :]
        sent = []
        for p, (px, py) in enumerate(peers):
            cp = _remote(half_ref, chips_ref.at[me], send_sems, recv_sems, 1 + p, (px, py, c))
            cp.start()
            sent.append(cp)
        chips_ref[me] = half_ref[...]
        for p, (px, py) in enumerate(peers):
            _remote(half_ref, chips_ref.at[2 * px + py], send_sems, recv_sems, 1 + p, (px, py, c)).wait_recv()
        for cp in sent:
            cp.wait_send()
        out_ref[mine, :] = ((chips_ref[0] + chips_ref[1]) + chips_ref[2]) + chips_ref[3]
        cp = _remote(out_ref.at[mine], out_ref.at[mine], send_sems, recv_sems, 4, sibling)
        cp.start()
        _remote(out_ref.at[other], out_ref.at[other], send_sems, recv_sems, 4, sibling).wait_recv()
        cp.wait_send()

    vm = pl.BlockSpec(memory_space=pltpu.VMEM)
    return pl.pallas_call(
        body, name=name, in_specs=[vm], out_specs=vm,
        out_shape=jax.ShapeDtypeStruct((rows, w), F32),
        scratch_shapes=[pltpu.VMEM((rows, w), F32), pltpu.VMEM((hr, w), F32), pltpu.VMEM((N_CHIPS, hr, w), F32),
                        pltpu.SemaphoreType.DMA((5,)), pltpu.SemaphoreType.DMA((5,))],
        compiler_params=pltpu.CompilerParams(vmem_limit_bytes=VMEM_LIMIT),
    )(v)


ELEMENTWISE_BLOCK = 256 * 1024


def _rows_tile(rows, cols):
    best = None
    for t in range(SUBLANES, rows + 1, SUBLANES):
        if rows % t == 0 and t * cols <= ELEMENTWISE_BLOCK:
            best = t
    return rows if best is None else best


def _add_pair(g, t, core, name):
    nb, n, w = t.shape
    tr = _rows_tile(n, w)
    steps = n // tr

    def body(core_ref, g_ref, t_ref, o_ref):
        o_ref[...] = (g_ref[...] + t_ref[...]).astype(BF16)

    spec = pl.BlockSpec((1, tr, w), lambda j, i, core_ref: (j, i, 0))
    return pl.pallas_call(
        body, name=name,
        grid_spec=pltpu.PrefetchScalarGridSpec(
            num_scalar_prefetch=1, grid=(nb, steps),
            in_specs=[pl.BlockSpec((1, tr, w), lambda j, i, core_ref: (j, core_ref[0] * steps + i, 0)), spec],
            out_specs=spec),
        out_shape=jax.ShapeDtypeStruct(t.shape, BF16),
        compiler_params=_params(("parallel", "parallel")))(core, g, t)


def _add_chips(landed, pairs, place, name):
    nb, n, w = landed.shape
    tr = _rows_tile(n, w)
    steps = n // tr

    def body(place_ref, r_ref, own_ref, o_ref):
        me = place_ref[0]
        acc = None
        for k in range(nb):
            blk = jnp.where(me == k, own_ref[0], r_ref[k]).astype(F32)
            acc = blk if acc is None else acc + blk
        o_ref[...] = acc

    return pl.pallas_call(
        body, name=name,
        grid_spec=pltpu.PrefetchScalarGridSpec(
            num_scalar_prefetch=1, grid=(steps,),
            in_specs=[pl.BlockSpec((nb, tr, w), lambda i, place_ref: (0, i, 0)),
                      pl.BlockSpec((1, tr, w), lambda i, place_ref: (place_ref[0], i, 0))],
            out_specs=pl.BlockSpec((tr, w), lambda i, place_ref: (place_ref[1] * steps + i, 0))),
        out_shape=jax.ShapeDtypeStruct((2 * n, w), F32),
        compiler_params=_params(("parallel",)))(place, landed, pairs)


def _adamw(w, g, m, v, name):
    rows, wd = w.shape
    tr = _rows_tile(rows, wd)
    c1 = 1.0 - ADAM_B1 ** ADAM_STEP
    c2 = 1.0 - ADAM_B2 ** ADAM_STEP

    def body(w_ref, g_ref, m_ref, v_ref, go_ref, d_ref, mo_ref, vo_ref):
        gv = g_ref[...]
        go_ref[...] = gv
        m2 = ADAM_B1 * m_ref[...] + (1.0 - ADAM_B1) * gv
        v2 = ADAM_B2 * v_ref[...] + (1.0 - ADAM_B2) * (gv * gv)
        mo_ref[...] = m2
        vo_ref[...] = v2
        d_ref[...] = -ADAM_LR * ((m2 / c1) / (jnp.sqrt(v2 / c2) + ADAM_EPS) + ADAM_WD * w_ref[...])

    spec = pl.BlockSpec((tr, wd), lambda i: (i, 0))
    shp = jax.ShapeDtypeStruct((rows, wd), F32)
    return pl.pallas_call(body, name=name, grid=(rows // tr,), in_specs=[spec] * 4, out_specs=[spec] * 4,
                          out_shape=[shp] * 4, compiler_params=_params(("parallel",)))(w, g, m, v)


BIG = [("w_in", (1024, 3232), 1), ("w_uq", (384, 768), 1), ("w_uk", (256, 512), 1), ("w_uv", (256, 512), 1),
       ("w_glu", (512, 512), 0), ("w_branch_attn", (512, 1024), 1), ("w_branch_ssm", (512, 1024), 1),
       ("w_out", (1024, 1024), 0), ("w_up", (1024, 5632), 1), ("conv_w", (3, 5632), 1), ("w_down", (2816, 1024), 0)]
SMALL = [("mix_norm_pre", (1024,)), ("q_norm", (384,)), ("kv_norm", (256,)), ("ssm_lambda_re", (32, 64)),
         ("ssm_lambda_im", (32, 64)), ("ssm_log_dt", (32,)), ("ssm_b_re", (32, 64, 16)), ("ssm_b_im", (32, 64, 16)),
         ("ssm_c_re", (32, 16, 64)), ("ssm_c_im", (32, 16, 64)), ("ssm_d", (32, 16)), ("b_glu", (512,)),
         ("b_gate", (2048,)), ("mix_norm_post", (1024,)), ("ffn_norm_pre", (1024,)), ("conv_b", (5632,)),
         ("ffn_norm_post", (1024,))]
MATMUL_W = [b for b in BIG if b[0] != "conv_w"]
LATE_W = ("w_up", "w_down", "conv_w")
CONV_W_SHAPE = (3, 2 * D_FF)
CONV_W_SHARD = (3, 2 * D_FF // N_CHIPS)
SMALL_SUM = [("loss", (1,))] + SMALL + [("conv_w", CONV_W_SHAPE)]
SMALL_ADAM = SMALL + [("conv_w", CONV_W_SHARD)]


def _pack_flat(layout, vals):
    flat = jnp.concatenate([vals[n].astype(F32).reshape(-1) for n, _ in layout])
    rows = -(-(-(-flat.shape[0] // FLAT_W)) // (2 * SUBLANES)) * 2 * SUBLANES
    return jnp.pad(flat, (0, rows * FLAT_W - flat.shape[0])).reshape(rows, FLAT_W)


def _unpack_flat(layout, flat):
    flat = flat.reshape(-1)
    out = {}
    o = 0
    for name, shape in layout:
        n = math.prod(shape)
        out[name] = flat[o:o + n].reshape(shape)
        o += n
    return out


_ARG_NAMES = ["x", "positions"] + [n for n in (
    "mix_norm_pre", "w_in", "q_norm", "w_uq", "kv_norm", "w_uk", "w_uv", "ssm_lambda_re", "ssm_lambda_im", "ssm_log_dt",
    "ssm_b_re", "ssm_b_im", "ssm_c_re", "ssm_c_im", "ssm_d", "w_glu", "b_glu", "w_branch_attn", "w_branch_ssm",
    "b_gate", "w_out", "mix_norm_post", "ffn_norm_pre", "w_up", "conv_w", "conv_b", "w_down", "ffn_norm_post")]
_WEIGHTS = _ARG_NAMES[2:]


def _gather_weights(w):
    early = [b for b in MATMUL_W if b[0] not in LATE_W]
    late = [b for b in BIG if b[0] in LATE_W]
    own = (jnp.arange(N_CHIPS) == 2 * lax.axis_index("x") + lax.axis_index("y"))[:, None, None]

    def whole(layout, mine, gathered):
        return {name: _from_chip_major(jnp.where(own, s[None], g), axis)
                for (name, _, axis), s, g in zip(layout, mine, gathered)}

    mine = [w[name].astype(BF16) for name, _, _ in early]
    full = whole(early, mine, _gather_big(mine))
    mine_late = [w[name].astype(F32 if name == "conv_w" else BF16) for name, _, _ in late]
    _, mine_late = lax.optimization_barrier((full["w_in"], mine_late))
    send_sems, recv_sems, shards_thru, lands_thru, token = _exchange_start(mine_late, "gather_late_start", scatter=False)

    def late_weights(after):
        shards, lands = _exchange_wait(send_sems, recv_sems, shards_thru, lands_thru, after, "gather_late_wait",
                                       scatter=False)
        return whole(late, shards, lands)

    full["late"] = late_weights
    full["token"] = token[0, 0]
    return full


def _pair_sums(names, grads, tag):
    core = lax.axis_index("c").astype(jnp.int32).reshape(1)
    theirs = _reduce_to_sibling(grads, "reduce_grads_d2d" + tag)
    return [_add_pair(g, t, core, "reduce_pair_" + n) for n, g, t in zip(names, grads, theirs)]


def _start_reduce(tag, grads):
    names = list(grads)
    pairs = _pair_sums(names, [grads[n] for n in names], "_" + tag)
    send_sems, recv_sems, pairs_thru, lands_thru, token = _exchange_start(pairs, "reduce_" + tag + "_start", scatter=True)
    return (tag, names, send_sems, recv_sems, pairs_thru, lands_thru), token[0, 0]


def _reduce_grads(gbig, pending, loss, gsmall):
    core = lax.axis_index("c").astype(jnp.int32).reshape(1)
    chip = (2 * lax.axis_index("x") + lax.axis_index("y")).astype(jnp.int32).reshape(1)
    place = jnp.concatenate([chip, core])
    names = list(gbig)
    pairs = _pair_sums(names, [gbig[n] for n in names], "")
    landed = list(_reduce_between_chips(pairs))
    for tag, sent_names, send_sems, recv_sems, pairs_thru, lands_thru in pending:
        got_pairs, got_landed = _exchange_wait(send_sems, recv_sems, pairs_thru, lands_thru, gbig["w_in"],
                                               "reduce_" + tag + "_wait", scatter=True)
        names, pairs, landed = names + sent_names, pairs + got_pairs, landed + got_landed
    totals = [_add_chips(r, p, place, "reduce_chips_" + n) for n, r, p in zip(names, landed, pairs)]
    g_red = dict(zip(names, _reduce_back(totals)))

    vals = dict(gsmall)
    vals["loss"] = loss
    small_red = _unpack_flat(SMALL_SUM, _all_reduce_small(_pack_flat(SMALL_SUM, vals), "reduce_small"))
    return g_red, small_red


def _step(args):
    x = args["x"][0]
    positions = args["positions"][0]
    tgt = args["loss_target"][0]
    w = {n: args[n][0] for n in _WEIGHTS}
    m = {n: args["m_" + n][0] for n in _WEIGHTS}
    v = {n: args["v_" + n][0] for n in _WEIGHTS}

    full = _gather_weights(w)
    sp = {n: w[n].reshape(s) for n, s in SMALL}
    for n in ("mix_norm_pre", "q_norm", "kv_norm", "b_glu", "b_gate", "mix_norm_post", "ffn_norm_pre", "conv_b",
              "ffn_norm_post"):
        sp[n] = sp[n].reshape(1, -1)
    sp["mix_norm_pre"] = sp["mix_norm_pre"] + full.pop("token")
    pending = []

    def send_grads(tag, grads):
        state, token = _start_reduce(tag, grads)
        pending.append(state)
        return token

    full["send_grads"] = send_grads
    loss, gx, gbig, gsmall = _local_step(x, positions, tgt, full, sp)
    g_red, small_red = _reduce_grads(gbig, pending, loss, gsmall)

    chip = 2 * lax.axis_index("x") + lax.axis_index("y")
    grads = dict(small_red)
    grads["conv_w"] = lax.dynamic_slice_in_dim(small_red["conv_w"], chip * CONV_W_SHARD[1], CONV_W_SHARD[1], axis=1)
    grads.update(g_red)

    outs = {"grad_" + n: grads[n] for n in _WEIGHTS}
    for name, _, _ in MATMUL_W:
        g2, d, m2, v2 = _adamw(w[name], grads[name], m[name], v[name], "adamw_" + name)
        outs["grad_" + name], outs["delta_" + name], outs["new_m_" + name], outs["new_v_" + name] = g2, d, m2, v2
    _, d_sm, m_sm, v_sm = _adamw(_pack_flat(SMALL_ADAM, w), _pack_flat(SMALL_ADAM, grads), _pack_flat(SMALL_ADAM, m),
                                 _pack_flat(SMALL_ADAM, v), "adamw_small")
    for prefix, flat in (("delta_", d_sm), ("new_m_", m_sm), ("new_v_", v_sm)):
        for n, val in _unpack_flat(SMALL_ADAM, flat).items():
            outs[prefix + n] = val
    outs = {n: val.reshape(args[n.split("_", 1)[1] if not n.startswith("new_") else n[6:]].shape)
            for n, val in outs.items()}
    res = [small_red["loss"][0], gx[None]]
    for prefix in ("grad_", "delta_", "new_m_", "new_v_"):
        res += [outs[prefix + n] for n in _WEIGHTS]
    return tuple(res)


def kernel(x, positions, mix_norm_pre, w_in, q_norm, w_uq, kv_norm, w_uk, w_uv, ssm_lambda_re, ssm_lambda_im, ssm_log_dt, ssm_b_re, ssm_b_im, ssm_c_re, ssm_c_im, ssm_d, w_glu, b_glu, w_branch_attn, w_branch_ssm, b_gate, w_out, mix_norm_post, ffn_norm_pre, w_up, conv_w, conv_b, w_down, ffn_norm_post, loss_target, m_mix_norm_pre, m_w_in, m_q_norm, m_w_uq, m_kv_norm, m_w_uk, m_w_uv, m_ssm_lambda_re, m_ssm_lambda_im, m_ssm_log_dt, m_ssm_b_re, m_ssm_b_im, m_ssm_c_re, m_ssm_c_im, m_ssm_d, m_w_glu, m_b_glu, m_w_branch_attn, m_w_branch_ssm, m_b_gate, m_w_out, m_mix_norm_post, m_ffn_norm_pre, m_w_up, m_conv_w, m_conv_b, m_w_down, m_ffn_norm_post, v_mix_norm_pre, v_w_in, v_q_norm, v_w_uq, v_kv_norm, v_w_uk, v_w_uv, v_ssm_lambda_re, v_ssm_lambda_im, v_ssm_log_dt, v_ssm_b_re, v_ssm_b_im, v_ssm_c_re, v_ssm_c_im, v_ssm_d, v_w_glu, v_b_glu, v_w_branch_attn, v_w_branch_ssm, v_b_gate, v_w_out, v_mix_norm_post, v_ffn_norm_pre, v_w_up, v_conv_w, v_conv_b, v_w_down, v_ffn_norm_post):
    given = dict(locals())
    return _step(given)
```

```python
import math

import jax
import jax.numpy as jnp
from jax import lax
from jax.experimental import pallas as pl
from jax.experimental.pallas import tpu as pltpu

F32 = jnp.float32
BF16 = jnp.bfloat16
MESH = pl.DeviceIdType.MESH

D_MODEL = 1024
N_HEADS = 8
QK_NOPE = 64
QK_ROPE = 32
QK_HEAD = QK_NOPE + QK_ROPE
V_HEAD = 64
Q_RANK = 384
KV_RANK = 256
ROPE_THETA = 10000.0
SSM_W = 512
SSM_H = 16
SSM_G = 32
SSM_P = 64
SSM_CH = SSM_G * SSM_P
D_FF = 2816
EPS = 1e-6
ADAM_LR = 0.001
ADAM_B1 = 0.9
ADAM_B2 = 0.999
ADAM_EPS = 1e-08
ADAM_WD = 0.01
ADAM_STEP = 10

LANES = 128
SUBLANES = 8
VMEM_LIMIT = 56 * 1024 * 1024

HEAD_SLOT = LANES
HP = N_HEADS * HEAD_SLOT
P_CQ, P_CKV, P_KR, P_U, P_GL, P_END = 0, 384, 640, 768, 1280, 3328
KR_LANE = 64

FLAT_W = 1024
N_CHIPS = 4


def _tile(n, cap):
    if n <= cap:
        return n
    best = None
    for t in range(LANES, cap + 1, LANES):
        if n % t == 0:
            best = t
    assert best is not None, (n, cap)
    return best


def _params(sem):
    return pltpu.CompilerParams(dimension_semantics=sem, vmem_limit_bytes=VMEM_LIMIT)


def _dot(a, b):
    return jnp.dot(a, b, preferred_element_type=F32)


def _dot_nt(a, b):
    return lax.dot_general(a, b, (((1,), (1,)), ((), ())), preferred_element_type=F32)


def _dot_tn(a, b):
    return lax.dot_general(a, b, (((0,), (0,)), ((), ())), preferred_element_type=F32)


def _rms(x, g):
    r = lax.rsqrt(jnp.mean(x * x, axis=-1, keepdims=True) + EPS)
    return x * r * g, r


def _rms_bwd(dy, x, g):
    r = lax.rsqrt(jnp.mean(x * x, axis=-1, keepdims=True) + EPS)
    dyg = dy * g
    dx = r * dyg - x * (r * r * r) * jnp.mean(dyg * x, axis=-1, keepdims=True)
    dg = jnp.sum(dy * x * r, axis=0, keepdims=True)
    return dx, dg


_GELU_K0 = math.sqrt(2.0 / math.pi)
_GELU_K1 = 0.044715


def _gelu(x):
    th = jnp.tanh(_GELU_K0 * (x + _GELU_K1 * x * x * x))
    return 0.5 * x * (1.0 + th)


def _gelu_grad(x):
    th = jnp.tanh(_GELU_K0 * (x + _GELU_K1 * x * x * x))
    return 0.5 * (1.0 + th) + 0.5 * x * (1.0 - th * th) * _GELU_K0 * (1.0 + 3.0 * _GELU_K1 * x * x)


def _sigmoid(x):
    return 1.0 / (1.0 + jnp.exp(-x))


def _rope(q, c, s):
    n = q.shape[1]
    lane = lax.broadcasted_iota(jnp.int32, q.shape, 1) % HEAD_SLOT
    sw = jnp.where(lane < KR_LANE + QK_ROPE // 2, pltpu.roll(q, n - QK_ROPE // 2, 1), pltpu.roll(q, QK_ROPE // 2, 1))
    return q * c + sw * s


def _rope_bwd(dy, c, s):
    n = dy.shape[1]
    t = dy * s
    lane = lax.broadcasted_iota(jnp.int32, dy.shape, 1) % HEAD_SLOT
    sw = jnp.where(lane < KR_LANE + QK_ROPE // 2, pltpu.roll(t, n - QK_ROPE // 2, 1), pltpu.roll(t, QK_ROPE // 2, 1))
    rope_lane = jnp.logical_and(lane >= KR_LANE, lane < KR_LANE + QK_ROPE)
    return dy * c + jnp.where(rope_lane, sw, 0.0)


def _shift_down(x, k, halo):
    xs = pltpu.roll(x, k, 0)
    hs = pltpu.roll(halo, k, 0)
    rows = lax.broadcasted_iota(jnp.int32, halo.shape, 0)
    top = jnp.where(rows < k, hs, xs[0:SUBLANES])
    return jnp.concatenate([top, xs[SUBLANES:]], axis=0)


def _shift_up(x, k, halo):
    t = x.shape[0]
    xs = pltpu.roll(x, t - k, 0)
    hs = pltpu.roll(halo, SUBLANES - k, 0)
    rows = lax.broadcasted_iota(jnp.int32, halo.shape, 0)
    bot = jnp.where(rows >= SUBLANES - k, hs, xs[t - SUBLANES:])
    return jnp.concatenate([xs[:t - SUBLANES], bot], axis=0)


def _mm(a, b, name, out_dtype=F32, bt=False, b_col0=0, n=None, tm_cap=1024, tn_cap=1408, a_lead=None):
    m, k = a.shape[-2:]
    if bt:
        n_full = b.shape[0]
        n = n_full
    else:
        n = b.shape[1] if n is None else n
    tm = min(tm_cap, m)
    tn = _tile(n, tn_cap)

    def body(a_ref, b_ref, o_ref):
        if bt:
            o_ref[...] = _dot_nt(a_ref[...], b_ref[...]).astype(out_dtype)
        else:
            o_ref[...] = _dot(a_ref[...], b_ref[...]).astype(out_dtype)

    if bt:
        b_spec = pl.BlockSpec((tn, k), lambda j, i: (j, b_col0))
    else:
        off = b_col0 * (n // tn)
        b_spec = pl.BlockSpec((k, tn), lambda j, i: (0, off + j))
    if a_lead is None:
        a_spec = pl.BlockSpec((tm, k), lambda j, i: (i, 0))
    else:
        a_spec = pl.BlockSpec((None, tm, k), lambda j, i: (a_lead, i, 0))
    return pl.pallas_call(
        body, name=name, grid=(n // tn, m // tm),
        in_specs=[a_spec, b_spec],
        out_specs=pl.BlockSpec((tm, tn), lambda j, i: (i, j)),
        out_shape=jax.ShapeDtypeStruct((m, n), out_dtype),
        compiler_params=_params(("parallel", "parallel")),
    )(a, b)


def _mm_tn(a, b, name, tk_cap=1024, tn_cap=1664, tl_cap=1024, chips=False):
    l, k = a.shape
    tk = _tile(k, tk_cap)
    tl = min(tl_cap, l)

    def body(a_ref, b_ref, o_ref):
        @pl.when(pl.program_id(2) == 0)
        def _():
            o_ref[...] = jnp.zeros_like(o_ref)

        o_ref[...] += _dot_tn(a_ref[...], b_ref[...])

    if chips:
        n = b.shape[-1] * (b.shape[0] if b.ndim == 3 else 1)
        tn = n // N_CHIPS
        assert tn % LANES == 0
        if b.ndim == 3:
            per = N_CHIPS // b.shape[0]
            b_spec = pl.BlockSpec((None, tl, tn), lambda i, j, r: (j // per, r, j % per))
        else:
            b_spec = pl.BlockSpec((tl, tn), lambda i, j, r: (r, j))
        out_spec = pl.BlockSpec((None, tk, tn), lambda i, j, r: (j, i, 0))
        out_shape = jax.ShapeDtypeStruct((N_CHIPS, k, tn), F32)
    else:
        n = b.shape[1]
        tn = _tile(n, tn_cap)
        b_spec = pl.BlockSpec((tl, tn), lambda i, j, r: (r, j))
        out_spec = pl.BlockSpec((tk, tn), lambda i, j, r: (i, j))
        out_shape = jax.ShapeDtypeStruct((k, n), F32)
    return pl.pallas_call(
        body, name=name, grid=(k // tk, n // tn, l // tl),
        in_specs=[pl.BlockSpec((tl, tk), lambda i, j, r: (r, i)), b_spec],
        out_specs=out_spec, out_shape=out_shape,
        compiler_params=_params(("parallel", "parallel", "arbitrary")),
    )(a, b)


def _row(tl, n):
    return pl.BlockSpec((tl, n), lambda i: (i, 0))


def _const(shape):
    return pl.BlockSpec(shape, lambda i: tuple(0 for _ in shape))


def _proj_fwd(x, g1, win, gq, wuq, gkv, wukv, rc, rs, bg, tl):
    l = x.shape[0]

    def body(x_ref, g1_ref, win_ref, gq_ref, wuq_ref, gkv_ref, wukv_ref, rc_ref, rs_ref, bg_ref,
             hn_ref, cq_ref, ckv_ref, q_ref, k_ref, v_ref, u_ref, gl_ref):
        hn, _ = _rms(x_ref[...], g1_ref[...])
        hnb = hn.astype(BF16)
        hn_ref[...] = hnb
        proj = _dot(hnb, win_ref[...])
        cq = proj[:, P_CQ:P_CKV]
        ckv = proj[:, P_CKV:P_KR]
        kr = proj[:, P_KR:P_U]
        cq_ref[...] = cq
        ckv_ref[...] = ckv
        u_ref[...] = proj[:, P_U:P_GL]
        gl_ref[...] = proj[:, P_GL:P_END] + bg_ref[...]
        qn, _ = _rms(cq, gq_ref[...])
        q = _dot(qn.astype(BF16), wuq_ref[...])
        c1 = rc_ref[...]
        s1 = rs_ref[...]
        q_ref[...] = (_rope(q, jnp.tile(c1, (1, N_HEADS)), jnp.tile(s1, (1, N_HEADS))) * Q_PRESCALE).astype(BF16)
        ckvn, _ = _rms(ckv, gkv_ref[...])
        kv = _dot(ckvn.astype(BF16), wukv_ref[...])
        krr = _rope(kr, c1, s1)
        k_ref[...] = (kv[:, :HP] + jnp.tile(krr, (1, N_HEADS))).astype(BF16)
        v_ref[...] = kv[:, HP:].astype(BF16)

    outs = [(D_MODEL, BF16), (Q_RANK, F32), (KV_RANK, F32), (HP, BF16), (HP, BF16), (HP, BF16),
            (SSM_W, F32), (2 * D_MODEL, F32)]
    return pl.pallas_call(
        body, name="proj_fwd", grid=(l // tl,),
        in_specs=[_row(tl, D_MODEL), _const((1, D_MODEL)), _const((D_MODEL, P_END)), _const((1, Q_RANK)),
                  _const((Q_RANK, HP)), _const((1, KV_RANK)), _const((KV_RANK, 2 * HP)),
                  _row(tl, HEAD_SLOT), _row(tl, HEAD_SLOT), _const((1, 2 * D_MODEL))],
        out_specs=[_row(tl, n) for n, _ in outs],
        out_shape=[jax.ShapeDtypeStruct((l, n), dt) for n, dt in outs],
        compiler_params=_params(("parallel",)),
    )(x, g1, win, gq, wuq, gkv, wukv, rc, rs, bg)


_NEG = -1e30


LOG2E = 1.0 / math.log(2.0)
LN2 = math.log(2.0)
ATTN_SCALE = 1.0 / math.sqrt(QK_HEAD)
Q_PRESCALE = ATTN_SCALE * LOG2E
HEADS_PER_STEP = 4
PAIR_W = HEADS_PER_STEP * HEAD_SLOT


def _causal_pairs(nq, by_query):
    if by_query:
        pairs = [(i, j) for i in range(nq) for j in range(i + 1)]
    else:
        pairs = [(i, j) for j in range(nq) for i in range(j, nq)]
    return jnp.array([p[0] for p in pairs], jnp.int32), jnp.array([p[1] for p in pairs], jnp.int32)


def _diag_mask_t(s):
    rows = lax.broadcasted_iota(jnp.int32, s.shape, 0)
    cols = lax.broadcasted_iota(jnp.int32, s.shape, 1)
    return jnp.where(rows <= cols, s, _NEG)


def _attn_fwd(q, k, v, tq):
    l = q.shape[0]
    nq = l // tq
    it, jt = _causal_pairs(nq, True)

    def body(it_ref, jt_ref, q_ref, k_ref, v_ref, o_ref, lse_ref, m_ref, l_ref, acc_ref):
        t = pl.program_id(1)
        i = it_ref[t]
        j = jt_ref[t]

        @pl.when(j == 0)
        def _():
            m_ref[...] = jnp.full_like(m_ref, _NEG)
            l_ref[...] = jnp.zeros_like(l_ref)
            acc_ref[...] = jnp.zeros_like(acc_ref)

        def update(on_diagonal):
            for hh in range(HEADS_PER_STEP):
                sl = slice(hh * HEAD_SLOT, (hh + 1) * HEAD_SLOT)
                s = _dot_nt(k_ref[:, sl], q_ref[:, sl])
                if on_diagonal:
                    s = _diag_mask_t(s)
                m_old = m_ref[hh]
                m_new = jnp.maximum(m_old, jnp.max(s, axis=0, keepdims=True))
                p = jnp.exp2(s - m_new)
                alpha = jnp.exp2(m_old - m_new)
                l_ref[hh] = alpha * l_ref[hh] + jnp.sum(p, axis=0, keepdims=True)
                acc_ref[hh] = alpha * acc_ref[hh] + _dot_tn(v_ref[:, sl], p.astype(BF16))
                m_ref[hh] = m_new

        @pl.when(j < i)
        def _():
            update(False)

        @pl.when(j == i)
        def _():
            update(True)
            for hh in range(HEADS_PER_STEP):
                sl = slice(hh * HEAD_SLOT, (hh + 1) * HEAD_SLOT)
                o_ref[:, sl] = (acc_ref[hh] / l_ref[hh]).T.astype(BF16)
                lse_ref[hh] = m_ref[hh] + jnp.log(l_ref[hh]) * LOG2E

    blk = (tq, PAIR_W)
    qmap = lambda h, t, it_ref, jt_ref: (it_ref[t], h)
    kmap = lambda h, t, it_ref, jt_ref: (jt_ref[t], h)
    row = pl.BlockSpec((HEADS_PER_STEP, 1, tq), lambda h, t, it_ref, jt_ref: (h, 0, it_ref[t]))
    return pl.pallas_call(
        body, name="attn_fwd",
        grid_spec=pltpu.PrefetchScalarGridSpec(
            num_scalar_prefetch=2, grid=(N_HEADS // HEADS_PER_STEP, it.shape[0]),
            in_specs=[pl.BlockSpec(blk, qmap), pl.BlockSpec(blk, kmap), pl.BlockSpec(blk, kmap)],
            out_specs=[pl.BlockSpec(blk, qmap), row],
            scratch_shapes=[pltpu.VMEM((HEADS_PER_STEP, 1, tq), F32), pltpu.VMEM((HEADS_PER_STEP, 1, tq), F32),
                            pltpu.VMEM((HEADS_PER_STEP, HEAD_SLOT, tq), F32)]),
        out_shape=[jax.ShapeDtypeStruct((l, HP), BF16), jax.ShapeDtypeStruct((N_HEADS, 1, l), F32)],
        compiler_params=_params(("parallel", "arbitrary")),
    )(it, jt, q, k, v)


def _attn_delta(o, do, tq):
    l = o.shape[0]

    def body(o_ref, do_ref, d_ref):
        prod = o_ref[...].astype(F32) * do_ref[...].astype(F32)
        for hh in range(HEADS_PER_STEP):
            d_ref[hh] = jnp.sum(prod[:, hh * HEAD_SLOT:(hh + 1) * HEAD_SLOT].T, axis=0, keepdims=True)

    blk = pl.BlockSpec((tq, PAIR_W), lambda h, i: (i, h))
    return pl.pallas_call(
        body, name="attn_delta", grid=(N_HEADS // HEADS_PER_STEP, l // tq), in_specs=[blk, blk],
        out_specs=pl.BlockSpec((HEADS_PER_STEP, 1, tq), lambda h, i: (h, 0, i)),
        out_shape=jax.ShapeDtypeStruct((N_HEADS, 1, l), F32),
        compiler_params=_params(("parallel", "parallel")),
    )(o, do)


def _attn_bwd(q, k, v, do, lse, delta, tq):
    l = q.shape[0]
    nq = l // tq
    it, jt = _causal_pairs(nq, False)

    def body(it_ref, jt_ref, q_ref, k_ref, v_ref, do_ref, lse_ref, dl_ref, dq_ref, dk_ref, dv_ref, dka_ref, dva_ref):
        t = pl.program_id(1)
        i = it_ref[t]
        j = jt_ref[t]

        @pl.when(t == 0)
        def _():
            dq_ref[...] = jnp.zeros_like(dq_ref)

        @pl.when(i == j)
        def _():
            dka_ref[...] = jnp.zeros_like(dka_ref)
            dva_ref[...] = jnp.zeros_like(dva_ref)

        def update(on_diagonal):
            r0 = pl.multiple_of(i * tq, tq)
            for hh in range(HEADS_PER_STEP):
                sl = slice(hh * HEAD_SLOT, (hh + 1) * HEAD_SLOT)
                qb = q_ref[:, sl]
                kb = k_ref[:, sl]
                dob = do_ref[:, sl]
                s = _dot_nt(kb, qb)
                if on_diagonal:
                    s = _diag_mask_t(s)
                p = jnp.exp2(s - lse_ref[hh])
                dva_ref[:, sl] += _dot(p.astype(BF16), dob)
                dp = _dot_nt(v_ref[:, sl], dob)
                ds = (p * (dp - dl_ref[hh])).astype(BF16)
                dka_ref[:, sl] += _dot(ds, qb)
                dq_ref[pl.ds(r0, tq), sl] += ATTN_SCALE * _dot_tn(ds, kb)

        @pl.when(j < i)
        def _():
            update(False)

        @pl.when(j == i)
        def _():
            update(True)

        @pl.when(i == nq - 1)
        def _():
            dk_ref[...] = (dka_ref[...] * LN2).astype(BF16)
            dv_ref[...] = dva_ref[...].astype(BF16)

    blk = (tq, PAIR_W)
    qmap = lambda h, t, it_ref, jt_ref: (it_ref[t], h)
    kmap = lambda h, t, it_ref, jt_ref: (jt_ref[t], h)
    row = pl.BlockSpec((HEADS_PER_STEP, 1, tq), lambda h, t, it_ref, jt_ref: (h, 0, it_ref[t]))
    return pl.pallas_call(
        body, name="attn_bwd",
        grid_spec=pltpu.PrefetchScalarGridSpec(
            num_scalar_prefetch=2, grid=(N_HEADS // HEADS_PER_STEP, it.shape[0]),
            in_specs=[pl.BlockSpec(blk, qmap), pl.BlockSpec(blk, kmap), pl.BlockSpec(blk, kmap),
                      pl.BlockSpec(blk, qmap), row, row],
            out_specs=[pl.BlockSpec((l, PAIR_W), lambda h, t, it_ref, jt_ref: (0, h)), pl.BlockSpec(blk, kmap),
                       pl.BlockSpec(blk, kmap)],
            scratch_shapes=[pltpu.VMEM(blk, F32), pltpu.VMEM(blk, F32)]),
        out_shape=[jax.ShapeDtypeStruct((l, HP), F32), jax.ShapeDtypeStruct((l, HP), BF16),
                   jax.ShapeDtypeStruct((l, HP), BF16)],
        compiler_params=_params(("parallel", "arbitrary")),
    )(it, jt, q, k, v, do, lse, delta)


SSM_CB = 512
SSM_UB = 128
SSM_NB = SSM_CH // SSM_CB


def _scan_tiles(re_ref, im_ref, tab, carry, n_tiles, reverse):
    group = 2
    assert n_tiles % group == 0
    pr, pi = tab[6], tab[7]

    def inside(sr, si):
        for step, k in enumerate((1, 2, 4)):
            mr, mi = tab[2 * step], tab[2 * step + 1]
            sh = (SUBLANES - k) if reverse else k
            rr = pltpu.roll(sr, sh, 0)
            ri = pltpu.roll(si, sh, 0)
            sr, si = sr + mr * rr - mi * ri, si + mr * ri + mi * rr
        return sr, si

    def body(n, c):
        cr, ci = c
        first = (n_tiles - group * (n + 1)) if reverse else group * n
        r0 = pl.multiple_of(first * SUBLANES, group * SUBLANES)
        rows = [pl.ds(r0 + g * SUBLANES, SUBLANES) for g in range(group)]
        tiles = [inside(re_ref[r, :], im_ref[r, :]) for r in rows]
        for g in (range(group - 1, -1, -1) if reverse else range(group)):
            sr, si = tiles[g]
            sr, si = sr + pr * cr - pi * ci, si + pr * ci + pi * cr
            re_ref[rows[g], :] = sr
            im_ref[rows[g], :] = si
            edge = slice(0, 1) if reverse else slice(SUBLANES - 1, SUBLANES)
            cr, ci = sr[edge, :], si[edge, :]
        return cr, ci

    return lax.fori_loop(0, n_tiles // group, body, carry)


def _ssm_fwd(u, bre, bim, cre, cim, dvec, tab, tt):
    l = u.shape[0]
    nt = l // tt

    def body(u_ref, bre_ref, bim_ref, cre_ref, cim_ref, d_ref, tab_ref, y_ref, sre_ref, sim_ref, car_ref):
        @pl.when(pl.program_id(1) == 0)
        def _():
            car_ref[...] = jnp.zeros_like(car_ref)

        uf = u_ref[...]
        ub = uf.astype(BF16)
        sre_ref[...] = _dot(ub, bre_ref[0])
        sim_ref[...] = _dot(ub, bim_ref[0])
        tab_v = [tab_ref[n] for n in range(8)]
        cr, ci = _scan_tiles(sre_ref, sim_ref, tab_v, (car_ref[0:1, :], car_ref[8:9, :]), tt // SUBLANES, False)
        car_ref[0:1, :] = cr
        car_ref[8:9, :] = ci
        y_ref[...] = (_dot(sre_ref[...].astype(BF16), cre_ref[0]) - _dot(sim_ref[...].astype(BF16), cim_ref[0])
                      + d_ref[...] * uf)

    return pl.pallas_call(
        body, name="ssm_fwd", grid=(SSM_NB, nt),
        in_specs=[pl.BlockSpec((tt, SSM_UB), lambda m, t: (t, m)),
                  pl.BlockSpec((1, SSM_UB, SSM_CB), lambda m, t: (m, 0, 0)),
                  pl.BlockSpec((1, SSM_UB, SSM_CB), lambda m, t: (m, 0, 0)),
                  pl.BlockSpec((1, SSM_CB, SSM_UB), lambda m, t: (m, 0, 0)),
                  pl.BlockSpec((1, SSM_CB, SSM_UB), lambda m, t: (m, 0, 0)),
                  pl.BlockSpec((1, SSM_UB), lambda m, t: (0, m)),
                  pl.BlockSpec((8, SUBLANES, SSM_CB), lambda m, t: (0, 0, m))],
        out_specs=[pl.BlockSpec((tt, SSM_UB), lambda m, t: (t, m)),
                   pl.BlockSpec((tt, SSM_CB), lambda m, t: (t, m)),
                   pl.BlockSpec((tt, SSM_CB), lambda m, t: (t, m))],
        out_shape=[jax.ShapeDtypeStruct((l, SSM_W), F32), jax.ShapeDtypeStruct((l, SSM_CH), F32),
                   jax.ShapeDtypeStruct((l, SSM_CH), F32)],
        scratch_shapes=[pltpu.VMEM((2 * SUBLANES, SSM_CB), F32)],
        compiler_params=_params(("parallel", "arbitrary")),
    )(u, bre, bim, cre, cim, dvec, tab)


def _ssm_bwd(dy, u, sre, sim, bre, bim, cre, cim, dvec, tab, tt):
    l = u.shape[0]
    nt = l // tt
    tpb = tt // SUBLANES

    def body(dy_ref, u_ref, sre_ref, sim_ref, hre_ref, him_ref, bre_ref, bim_ref, cre_ref, cim_ref, d_ref, tab_ref,
             du_ref, dbre_ref, dbim_ref, dcre_ref, dcim_ref, dare_ref, daim_ref, dd_ref, lr_ref, li_ref, car_ref):
        t = pl.program_id(1)

        @pl.when(t == 0)
        def _():
            car_ref[...] = jnp.zeros_like(car_ref)
            for ref in (dbre_ref, dbim_ref, dcre_ref, dcim_ref, dare_ref, daim_ref, dd_ref):
                ref[...] = jnp.zeros_like(ref)

        dyf = dy_ref[...]
        dyb = dyf.astype(BF16)
        uf = u_ref[...]
        s_re = sre_ref[...]
        s_im = sim_ref[...]
        lr_ref[...] = _dot_nt(dyb, cre_ref[0])
        li_ref[...] = -_dot_nt(dyb, cim_ref[0])
        dcre_ref[0] += _dot_tn(s_re.astype(BF16), dyb)
        dcim_ref[0] -= _dot_tn(s_im.astype(BF16), dyb)
        tab_v = [tab_ref[n] for n in range(8)]
        cr, ci = _scan_tiles(lr_ref, li_ref, tab_v, (car_ref[0:1, :], car_ref[8:9, :]), tpb, True)
        car_ref[0:1, :] = cr
        car_ref[8:9, :] = ci
        lam_r = lr_ref[...]
        lam_i = li_ref[...]
        keep = jnp.where(t == nt - 1, 0.0, 1.0)
        sp_r = _shift_down(s_re, 1, hre_ref[...] * keep)
        sp_i = _shift_down(s_im, 1, him_ref[...] * keep)
        dare_ref[...] += jnp.sum(lam_r * sp_r + lam_i * sp_i, axis=0, keepdims=True)
        daim_ref[...] += jnp.sum(lam_i * sp_r - lam_r * sp_i, axis=0, keepdims=True)
        lrb = lam_r.astype(BF16)
        lib = lam_i.astype(BF16)
        du_ref[...] = _dot_nt(lrb, bre_ref[0]) + _dot_nt(lib, bim_ref[0]) + dyf * d_ref[...]
        ub = uf.astype(BF16)
        dbre_ref[0] += _dot_tn(ub, lrb)
        dbim_ref[0] += _dot_tn(ub, lib)
        dd_ref[...] += jnp.sum(dyf * uf, axis=0, keepdims=True)

    rev = lambda m, t: (nt - 1 - t, m)
    halo = lambda m, t: (jnp.maximum((nt - 1 - t) * tpb - 1, 0), m)
    wb = pl.BlockSpec((1, SSM_UB, SSM_CB), lambda m, t: (m, 0, 0))
    wc = pl.BlockSpec((1, SSM_CB, SSM_UB), lambda m, t: (m, 0, 0))
    vec_c = pl.BlockSpec((1, SSM_CB), lambda m, t: (0, m))
    vec_u = pl.BlockSpec((1, SSM_UB), lambda m, t: (0, m))
    return pl.pallas_call(
        body, name="ssm_bwd", grid=(SSM_NB, nt),
        in_specs=[pl.BlockSpec((tt, SSM_UB), rev), pl.BlockSpec((tt, SSM_UB), rev),
                  pl.BlockSpec((tt, SSM_CB), rev), pl.BlockSpec((tt, SSM_CB), rev),
                  pl.BlockSpec((SUBLANES, SSM_CB), halo), pl.BlockSpec((SUBLANES, SSM_CB), halo),
                  wb, wb, wc, wc, vec_u,
                  pl.BlockSpec((8, SUBLANES, SSM_CB), lambda m, t: (0, 0, m))],
        out_specs=[pl.BlockSpec((tt, SSM_UB), rev), wb, wb, wc, wc, vec_c, vec_c, vec_u],
        out_shape=[jax.ShapeDtypeStruct((l, SSM_W), F32),
                   jax.ShapeDtypeStruct((SSM_NB, SSM_UB, SSM_CB), F32), jax.ShapeDtypeStruct((SSM_NB, SSM_UB, SSM_CB), F32),
                   jax.ShapeDtypeStruct((SSM_NB, SSM_CB, SSM_UB), F32), jax.ShapeDtypeStruct((SSM_NB, SSM_CB, SSM_UB), F32),
                   jax.ShapeDtypeStruct((1, SSM_CH), F32), jax.ShapeDtypeStruct((1, SSM_CH), F32),
                   jax.ShapeDtypeStruct((1, SSM_W), F32)],
        scratch_shapes=[pltpu.VMEM((tt, SSM_CB), F32), pltpu.VMEM((tt, SSM_CB), F32),
                        pltpu.VMEM((2 * SUBLANES, SSM_CB), F32)],
        compiler_params=_params(("parallel", "arbitrary")),
    )(dy, u, sre, sim, sre, sim, bre, bim, cre, cim, dvec, tab)


def _merge_fwd(x, gl, attn, y1, wba, wbs, wglu, bglu, wout, gpost, gpre, tl):
    l = x.shape[0]

    def body(x_ref, gl_ref, at_ref, y1_ref, wba_ref, wbs_ref, wglu_ref, bglu_ref, wout_ref, gpost_ref, gpre_ref,
             a_ref, sm_ref, mg_ref, z_ref, x1_ref, hn2_ref, y3_ref):
        y2 = _gelu(y1_ref[...])
        sg = _sigmoid(_dot(y2.astype(BF16), wglu_ref[...]) + bglu_ref[...])
        y3 = (y2 * sg).astype(BF16)
        y3_ref[...] = y3
        a = _dot(at_ref[...], wba_ref[...])
        sm = _dot(y3, wbs_ref[...])
        a_ref[...] = a
        sm_ref[...] = sm
        g = _sigmoid(gl_ref[...])
        merged = (g[:, :D_MODEL] * a + g[:, D_MODEL:] * sm).astype(BF16)
        mg_ref[...] = merged
        z = _dot(merged, wout_ref[...])
        z_ref[...] = z
        n, _ = _rms(z, gpost_ref[...])
        x1 = x_ref[...] + n
        x1_ref[...] = x1
        hn2, _ = _rms(x1, gpre_ref[...])
        hn2_ref[...] = hn2.astype(BF16)

    outs = [(D_MODEL, F32), (D_MODEL, F32), (D_MODEL, BF16), (D_MODEL, F32), (D_MODEL, F32), (D_MODEL, BF16),
            (SSM_W, BF16)]
    return pl.pallas_call(
        body, name="merge_fwd", grid=(l // tl,),
        in_specs=[_row(tl, D_MODEL), _row(tl, 2 * D_MODEL), _row(tl, HP), _row(tl, SSM_W),
                  _const((HP, D_MODEL)), _const((SSM_W, D_MODEL)), _const((SSM_W, SSM_W)), _const((1, SSM_W)),
                  _const((D_MODEL, D_MODEL)), _const((1, D_MODEL)), _const((1, D_MODEL))],
        out_specs=[_row(tl, n) for n, _ in outs],
        out_shape=[jax.ShapeDtypeStruct((l, n), dt) for n, dt in outs],
        compiler_params=_params(("parallel",)),
    )(x, gl, attn, y1, wba, wbs, wglu, bglu, wout, gpost, gpre)


def _merge_bwd(dhn2a, dhn2b, x1, dx2, z, gl, a, sm, y1, wba, wbs, wglu, bglu, wout, gpost, gpre, tl):
    l = x1.shape[0]

    def body(da_ref, db_ref, x1_ref, dx2_ref, z_ref, gl_ref, a_ref, sm_ref, y1_ref,
             wba_ref, wbs_ref, wglu_ref, bglu_ref, wout_ref, gpost_ref, gpre_ref,
             dx1_ref, dz_ref, dbra_ref, dbrs_ref, dgl_ref, dat_ref, dy1_ref, dt_ref, y2_ref,
             dgpre_ref, dgpost_ref, dbg_ref, dbglu_ref):
        @pl.when(pl.program_id(0) == 0)
        def _():
            for ref in (dgpre_ref, dgpost_ref, dbg_ref, dbglu_ref):
                ref[...] = jnp.zeros_like(ref)

        dhn2 = da_ref[...] + db_ref[...]
        dx1a, dgpre = _rms_bwd(dhn2, x1_ref[...], gpre_ref[...])
        dgpre_ref[...] += dgpre
        dx1 = dx2_ref[...] + dx1a
        dx1_ref[...] = dx1
        dz, dgpost = _rms_bwd(dx1, z_ref[...], gpost_ref[...])
        dgpost_ref[...] += dgpost
        dzb = dz.astype(BF16)
        dz_ref[...] = dzb
        dm = _dot_nt(dzb, wout_ref[...])
        g = _sigmoid(gl_ref[...])
        g0 = g[:, :D_MODEL]
        g1 = g[:, D_MODEL:]
        dbra = (dm * g0).astype(BF16)
        dbrs = (dm * g1).astype(BF16)
        dbra_ref[...] = dbra
        dbrs_ref[...] = dbrs
        dgl0 = dm * a_ref[...] * g0 * (1.0 - g0)
        dgl1 = dm * sm_ref[...] * g1 * (1.0 - g1)
        dgl_ref[:, :D_MODEL] = dgl0.astype(BF16)
        dgl_ref[:, D_MODEL:] = dgl1.astype(BF16)
        dbg_ref[:, :D_MODEL] += jnp.sum(dgl0, axis=0, keepdims=True)
        dbg_ref[:, D_MODEL:] += jnp.sum(dgl1, axis=0, keepdims=True)
        dat_ref[...] = _dot_nt(dbra, wba_ref[...]).astype(BF16)
        dy3 = _dot_nt(dbrs, wbs_ref[...])
        y1v = y1_ref[...]
        y2 = _gelu(y1v)
        y2b = y2.astype(BF16)
        y2_ref[...] = y2b
        sg = _sigmoid(_dot(y2b, wglu_ref[...]) + bglu_ref[...])
        dt = dy3 * y2 * sg * (1.0 - sg)
        dtb = dt.astype(BF16)
        dt_ref[...] = dtb
        dbglu_ref[...] += jnp.sum(dt, axis=0, keepdims=True)
        dy2 = dy3 * sg + _dot_nt(dtb, wglu_ref[...])
        dy1_ref[...] = dy2 * _gelu_grad(y1v)

    outs = [(D_MODEL, F32), (D_MODEL, BF16), (D_MODEL, BF16), (D_MODEL, BF16), (2 * D_MODEL, BF16), (HP, BF16),
            (SSM_W, F32), (SSM_W, BF16), (SSM_W, BF16)]
    accs = [D_MODEL, D_MODEL, 2 * D_MODEL, SSM_W]
    return pl.pallas_call(
        body, name="merge_bwd", grid=(l // tl,),
        in_specs=[_row(tl, D_MODEL), _row(tl, D_MODEL), _row(tl, D_MODEL), _row(tl, D_MODEL), _row(tl, D_MODEL),
                  _row(tl, 2 * D_MODEL), _row(tl, D_MODEL), _row(tl, D_MODEL), _row(tl, SSM_W),
                  _const((HP, D_MODEL)), _const((SSM_W, D_MODEL)), _const((SSM_W, SSM_W)), _const((1, SSM_W)),
                  _const((D_MODEL, D_MODEL)), _const((1, D_MODEL)), _const((1, D_MODEL))],
        out_specs=[_row(tl, n) for n, _ in outs] + [_const((1, n)) for n in accs],
        out_shape=[jax.ShapeDtypeStruct((l, n), dt) for n, dt in outs]
        + [jax.ShapeDtypeStruct((1, n), F32) for n in accs],
        compiler_params=_params(("arbitrary",)),
    )(dhn2a, dhn2b, x1, dx2, z, gl, a, sm, y1, wba, wbs, wglu, bglu, wout, gpost, gpre)


def _proj_bwd(x, dx1, cq, ckv, dq, dk, dv, du, dgl, g1, win, gq, wuq, gkv, wukv, rc, rs, tl):
    l = x.shape[0]

    def body(x_ref, dx1_ref, cq_ref, ckv_ref, dq_ref, dk_ref, dv_ref, du_ref, dgl_ref,
             g1_ref, win_ref, gq_ref, wuq_ref, gkv_ref, wukv_ref, rc_ref, rs_ref,
             gx_ref, dql_ref, qn_ref, ckvn_ref, dproj_ref, dg1_ref, dgq_ref, dgkv_ref):
        @pl.when(pl.program_id(0) == 0)
        def _():
            for ref in (dg1_ref, dgq_ref, dgkv_ref):
                ref[...] = jnp.zeros_like(ref)

        c1 = rc_ref[...]
        s1 = rs_ref[...]
        dql = _rope_bwd(dq_ref[...], jnp.tile(c1, (1, N_HEADS)), jnp.tile(s1, (1, N_HEADS))).astype(BF16)
        dql_ref[...] = dql
        dqn = _dot_nt(dql, wuq_ref[...])
        cq = cq_ref[...]
        qn, _ = _rms(cq, gq_ref[...])
        qn_ref[...] = qn.astype(BF16)
        dcq, dgq = _rms_bwd(dqn, cq, gq_ref[...])
        dgq_ref[...] += dgq
        dkb = dk_ref[...]
        dvb = dv_ref[...]
        dkf = dkb.astype(F32)
        dkr = dkf[:, 0:HEAD_SLOT]
        for h in range(1, N_HEADS):
            dkr = dkr + dkf[:, h * HEAD_SLOT:(h + 1) * HEAD_SLOT]
        dkr = _rope_bwd(dkr, c1, s1)
        dckvn = _dot_nt(dkb, wukv_ref[:, :HP]) + _dot_nt(dvb, wukv_ref[:, HP:])
        ckv = ckv_ref[...]
        ckvn, _ = _rms(ckv, gkv_ref[...])
        ckvn_ref[...] = ckvn.astype(BF16)
        dckv, dgkv = _rms_bwd(dckvn, ckv, gkv_ref[...])
        dgkv_ref[...] += dgkv
        dproj_ref[:, P_CQ:P_CKV] = dcq.astype(BF16)
        dproj_ref[:, P_CKV:P_KR] = dckv.astype(BF16)
        dproj_ref[:, P_KR:P_U] = dkr.astype(BF16)
        dproj_ref[:, P_U:P_GL] = du_ref[...].astype(BF16)
        dproj_ref[:, P_GL:P_END] = dgl_ref[...]
        dhn = _dot_nt(dproj_ref[...], win_ref[...])
        dxa, dg1 = _rms_bwd(dhn, x_ref[...], g1_ref[...])
        dg1_ref[...] += dg1
        gx_ref[...] = dx1_ref[...] + dxa

    outs = [(D_MODEL, F32), (HP, BF16), (Q_RANK, BF16), (KV_RANK, BF16), (P_END, BF16)]
    accs = [D_MODEL, Q_RANK, KV_RANK]
    return pl.pallas_call(
        body, name="proj_bwd", grid=(l // tl,),
        in_specs=[_row(tl, D_MODEL), _row(tl, D_MODEL), _row(tl, Q_RANK), _row(tl, KV_RANK), _row(tl, HP),
                  _row(tl, HP), _row(tl, HP), _row(tl, SSM_W), _row(tl, 2 * D_MODEL),
                  _const((1, D_MODEL)), _const((D_MODEL, P_END)), _const((1, Q_RANK)), _const((Q_RANK, HP)),
                  _const((1, KV_RANK)), _const((KV_RANK, 2 * HP)), _row(tl, HEAD_SLOT), _row(tl, HEAD_SLOT)],
        out_specs=[_row(tl, n) for n, _ in outs] + [_const((1, n)) for n in accs],
        out_shape=[jax.ShapeDtypeStruct((l, n), dt) for n, dt in outs]
        + [jax.ShapeDtypeStruct((1, n), F32) for n in accs],
        compiler_params=_params(("arbitrary",)),
    )(x, dx1, cq, ckv, dq, dk, dv, du, dgl, g1, win, gq, wuq, gkv, wukv, rc, rs)


CONV_CB = 256
CONV_NB = D_FF // CONV_CB
CONV_ROWS = 16


def _conv3(h, halo, w, b):
    return b + w[0:1, :] * _shift_down(h, 2, halo) + w[1:2, :] * _shift_down(h, 1, halo) + w[2:3, :] * h


def _conv_fwd(h, cw, cb, tl):
    l = h.shape[0]

    def body(hg_ref, hv_ref, wg_ref, wv_ref, bg_ref, bv_ref, act_ref, halo_ref):
        @pl.when(pl.program_id(1) == 0)
        def _():
            halo_ref[...] = jnp.zeros_like(halo_ref)

        hg = hg_ref[...]
        hv = hv_ref[...]
        cg = _conv3(hg, halo_ref[0:SUBLANES, :], wg_ref[...], bg_ref[...])
        cv = _conv3(hv, halo_ref[SUBLANES:, :], wv_ref[...], bv_ref[...])
        act_ref[...] = (_gelu(cg) * cv).astype(BF16)
        halo_ref[0:SUBLANES, :] = hg[tl - SUBLANES:, :]
        halo_ref[SUBLANES:, :] = hv[tl - SUBLANES:, :]

    gmap = lambda c, r: (r, c)
    vmap = lambda c, r: (r, CONV_NB + c)
    return pl.pallas_call(
        body, name="conv_fwd", grid=(CONV_NB, l // tl),
        in_specs=[pl.BlockSpec((tl, CONV_CB), gmap), pl.BlockSpec((tl, CONV_CB), vmap),
                  pl.BlockSpec((3, CONV_CB), lambda c, r: (0, c)), pl.BlockSpec((3, CONV_CB), lambda c, r: (0, CONV_NB + c)),
                  pl.BlockSpec((1, CONV_CB), lambda c, r: (0, c)), pl.BlockSpec((1, CONV_CB), lambda c, r: (0, CONV_NB + c))],
        out_specs=pl.BlockSpec((tl, CONV_CB), gmap),
        out_shape=jax.ShapeDtypeStruct((l, D_FF), BF16),
        scratch_shapes=[pltpu.VMEM((2 * SUBLANES, CONV_CB), F32)],
        compiler_params=_params(("parallel", "arbitrary")),
    )(h, h, cw, cw, cb, cb)


def _conv_bwd(h, dact, cw, cb, tl):
    l = h.shape[0]
    nr = l // tl
    tpb = tl // SUBLANES

    def body(hg_ref, hv_ref, hgh_ref, hvh_ref, da_ref, wg_ref, wv_ref, bg_ref, bv_ref,
             dh_ref, dwg_ref, dwv_ref, dbg_ref, dbv_ref, car_ref):
        r = pl.program_id(1)

        @pl.when(r == 0)
        def _():
            for ref in (car_ref, dwg_ref, dwv_ref, dbg_ref, dbv_ref):
                ref[...] = jnp.zeros_like(ref)

        keep = jnp.where(r == nr - 1, 0.0, 1.0)
        wg, wv, bg, bv = wg_ref[...], wv_ref[...], bg_ref[...], bv_ref[...]
        nch = tl // CONV_ROWS

        def fold(x):
            s = x[0:SUBLANES, :]
            for k in range(1, CONV_ROWS // SUBLANES):
                s = s + x[k * SUBLANES:(k + 1) * SUBLANES, :]
            return s

        def chunk(n, carry):
            ncg, ncv, acc = carry
            idx = nch - 1 - n
            r0 = pl.multiple_of(idx * CONV_ROWS, CONV_ROWS)
            rows = pl.ds(r0, CONV_ROWS)
            before = pl.ds(pl.multiple_of(jnp.maximum(r0 - SUBLANES, 0), SUBLANES), SUBLANES)
            in_tile = idx > 0
            da = da_ref[rows, :].astype(F32)

            def half(h_ref, halo_ref, w, b):
                hh = h_ref[rows, :]
                prev = jnp.where(in_tile, h_ref[before, :], halo_ref[...] * keep)
                h1 = _shift_down(hh, 1, prev)
                h2 = _shift_down(hh, 2, prev)
                return hh, h1, h2, b + w[0:1, :] * h2 + w[1:2, :] * h1 + w[2:3, :] * hh

            hg, hg1, hg2, cg = half(hg_ref, hgh_ref, wg, bg)
            hv, hv1, hv2, cv = half(hv_ref, hvh_ref, wv, bv)
            dcg = da * cv * _gelu_grad(cg)
            dcv = da * _gelu(cg)

            def back(dc, hh, h1, h2, w, nxt, part):
                dh = w[2:3, :] * dc + w[1:2, :] * _shift_up(dc, 1, nxt) + w[0:1, :] * _shift_up(dc, 2, nxt)
                dh_ref[part, rows, :] = dh.astype(BF16)
                return [fold(dc * h2), fold(dc * h1), fold(dc * hh), fold(dc)]

            sums = back(dcg, hg, hg1, hg2, wg, ncg, 0) + back(dcv, hv, hv1, hv2, wv, ncv, 1)
            return dcg[0:SUBLANES, :], dcv[0:SUBLANES, :], [a + s for a, s in zip(acc, sums)]

        zero = jnp.zeros((SUBLANES, CONV_CB), F32)
        ncg, ncv, acc = lax.fori_loop(0, nch, chunk, (car_ref[0:SUBLANES, :], car_ref[SUBLANES:, :], [zero] * 8))
        car_ref[0:SUBLANES, :] = ncg
        car_ref[SUBLANES:, :] = ncv
        for half_acc, dw_ref, db_ref in ((acc[0:4], dwg_ref, dbg_ref), (acc[4:8], dwv_ref, dbv_ref)):
            for k in range(3):
                dw_ref[k:k + 1, :] += jnp.sum(half_acc[k], axis=0, keepdims=True)
            db_ref[...] += jnp.sum(half_acc[3], axis=0, keepdims=True)

    grev = lambda c, r: (nr - 1 - r, c)
    vrev = lambda c, r: (nr - 1 - r, CONV_NB + c)
    ghalo = lambda c, r: (jnp.maximum((nr - 1 - r) * tpb - 1, 0), c)
    vhalo = lambda c, r: (jnp.maximum((nr - 1 - r) * tpb - 1, 0), CONV_NB + c)
    colg = lambda c, r: (0, c)
    colv = lambda c, r: (0, CONV_NB + c)
    return pl.pallas_call(
        body, name="conv_bwd", grid=(CONV_NB, nr),
        in_specs=[pl.BlockSpec((tl, CONV_CB), grev), pl.BlockSpec((tl, CONV_CB), vrev),
                  pl.BlockSpec((SUBLANES, CONV_CB), ghalo), pl.BlockSpec((SUBLANES, CONV_CB), vhalo),
                  pl.BlockSpec((tl, CONV_CB), grev),
                  pl.BlockSpec((3, CONV_CB), colg), pl.BlockSpec((3, CONV_CB), colv),
                  pl.BlockSpec((1, CONV_CB), colg), pl.BlockSpec((1, CONV_CB), colv)],
        out_specs=[pl.BlockSpec((2, tl, CONV_CB), lambda c, r: (0, nr - 1 - r, c)),
                   pl.BlockSpec((3, CONV_CB), colg), pl.BlockSpec((3, CONV_CB), colg),
                   pl.BlockSpec((1, CONV_CB), colg), pl.BlockSpec((1, CONV_CB), colg)],
        out_shape=[jax.ShapeDtypeStruct((2, l, D_FF), BF16),
                   jax.ShapeDtypeStruct((3, D_FF), F32), jax.ShapeDtypeStruct((3, D_FF), F32),
                   jax.ShapeDtypeStruct((1, D_FF), F32), jax.ShapeDtypeStruct((1, D_FF), F32)],
        scratch_shapes=[pltpu.VMEM((2 * SUBLANES, CONV_CB), F32)],
        compiler_params=_params(("parallel", "arbitrary")),
    )(h, h, h, h, dact, cw, cw, cb, cb)


def _loss_head(ff, x1, tgt, g, tl):
    l = ff.shape[0]

    def body(ff_ref, x1_ref, tg_ref, g_ref, loss_ref, dx2_ref, dff_ref, dg_ref):
        @pl.when(pl.program_id(0) == 0)
        def _():
            loss_ref[...] = jnp.zeros_like(loss_ref)
            dg_ref[...] = jnp.zeros_like(dg_ref)

        f = ff_ref[...]
        gv = g_ref[...]
        n, _ = _rms(f, gv)
        e = x1_ref[...] + n - tg_ref[...]
        loss_ref[...] += 0.5 * jnp.sum(jnp.mean(e * e, axis=-1, keepdims=True), axis=0, keepdims=True)
        dx2 = e * (1.0 / D_MODEL)
        dx2_ref[...] = dx2
        dff, dg = _rms_bwd(dx2, f, gv)
        dff_ref[...] = dff.astype(BF16)
        dg_ref[...] += dg

    return pl.pallas_call(
        body, name="loss_head", grid=(l // tl,),
        in_specs=[_row(tl, D_MODEL), _row(tl, D_MODEL), _row(tl, D_MODEL), _const((1, D_MODEL))],
        out_specs=[_const((1, LANES)), _row(tl, D_MODEL), _row(tl, D_MODEL), _const((1, D_MODEL))],
        out_shape=[jax.ShapeDtypeStruct((1, LANES), F32), jax.ShapeDtypeStruct((l, D_MODEL), F32),
                   jax.ShapeDtypeStruct((l, D_MODEL), BF16), jax.ShapeDtypeStruct((1, D_MODEL), F32)],
        compiler_params=_params(("arbitrary",)),
    )(ff, x1, tgt, g)


def _ssm_disc(lam_re, lam_im, log_dt, b_re, b_im):
    dt = jnp.exp(log_dt)[:, None]
    mag = jnp.exp(lam_re * dt)
    ang = lam_im * dt
    a_re, a_im = mag * jnp.cos(ang), mag * jnp.sin(ang)
    den = lam_re * lam_re + lam_im * lam_im
    n_re, n_im = a_re - 1.0, a_im
    z_re = (n_re * lam_re + n_im * lam_im) / den
    z_im = (n_im * lam_re - n_re * lam_im) / den
    bb_re = z_re[..., None] * b_re - z_im[..., None] * b_im
    bb_im = z_re[..., None] * b_im + z_im[..., None] * b_re
    return a_re, a_im, bb_re, bb_im


_GPB = SSM_CB // SSM_P


def _embed_b(bb):
    t = bb.transpose(0, 2, 1).reshape(SSM_NB, _GPB, SSM_H, SSM_P)
    return jnp.einsum('mjhp,jk->mjhkp', t, jnp.eye(_GPB, dtype=bb.dtype)).reshape(SSM_NB, SSM_UB, SSM_CB)


def _extract_b(d):
    t = d.reshape(SSM_NB, _GPB, SSM_H, _GPB, SSM_P)
    t = jnp.einsum('mjhkp,jk->mjhp', t, jnp.eye(_GPB, dtype=d.dtype))
    return t.reshape(SSM_G, SSM_H, SSM_P).transpose(0, 2, 1)


def _embed_c(c):
    t = c.transpose(0, 2, 1).reshape(SSM_NB, _GPB, SSM_P, SSM_H)
    return jnp.einsum('mjph,jk->mjpkh', t, jnp.eye(_GPB, dtype=c.dtype)).reshape(SSM_NB, SSM_CB, SSM_UB)


def _extract_c(d):
    t = d.reshape(SSM_NB, _GPB, SSM_P, _GPB, SSM_H)
    t = jnp.einsum('mjpkh,jk->mjph', t, jnp.eye(_GPB, dtype=d.dtype))
    return t.reshape(SSM_G, SSM_P, SSM_H).transpose(0, 2, 1)


def _scan_tables(a_re, a_im, reverse):
    ar = a_re.reshape(1, SSM_CH)
    ai = (-a_im if reverse else a_im).reshape(1, SSM_CH)
    pr, pi = [ar], [ai]
    for _ in range(SUBLANES - 1):
        pr, pi = pr + [pr[-1] * ar - pi[-1] * ai], pi + [pr[-1] * ai + pi[-1] * ar]
    rows = jnp.arange(SUBLANES)[:, None]
    out = []
    for k in (1, 2, 4):
        valid = (rows + k <= SUBLANES - 1) if reverse else (rows >= k)
        out += [jnp.where(valid, pr[k - 1], 0.0), jnp.where(valid, pi[k - 1], 0.0)]
    order = list(range(SUBLANES - 1, -1, -1)) if reverse else list(range(SUBLANES))
    out += [jnp.concatenate([pr[n] for n in order], axis=0), jnp.concatenate([pi[n] for n in order], axis=0)]
    return jnp.stack(out).astype(F32)


def _pad_heads(w, d):
    lead = w.shape[:-1]
    w = w.reshape(lead + (N_HEADS, d))
    w = jnp.pad(w, [(0, 0)] * len(lead) + [(0, 0), (0, HEAD_SLOT - d)])
    return w.reshape(lead + (HP,))


def _unpad_heads(w, d):
    lead = w.shape[:-1]
    return w.reshape(lead + (N_HEADS, HEAD_SLOT))[..., :d].reshape(lead + (N_HEADS * d,))


def _chip_major(w, axis):
    k, n = w.shape
    if axis == 0:
        return w.reshape(N_CHIPS, k // N_CHIPS, n)
    return w.reshape(k, N_CHIPS, n // N_CHIPS).transpose(1, 0, 2)


def _from_chip_major(w, axis):
    if axis == 0:
        return w.reshape(-1, w.shape[2])
    return w.transpose(1, 0, 2).reshape(w.shape[1], -1)


def _pad_w_in(w):
    z = lambda n: jnp.zeros((w.shape[0], n), w.dtype)
    return jnp.concatenate([w[:, :640], z(KR_LANE), w[:, 640:672], z(HEAD_SLOT - KR_LANE - QK_ROPE), w[:, 672:]], axis=1)


def _unpad_w_in(w):
    return jnp.concatenate([w[:, :640], w[:, P_KR + KR_LANE:P_KR + KR_LANE + QK_ROPE], w[:, P_U:]], axis=1)


def _local_step(x, positions, tgt, wts, sp):
    l = x.shape[0]
    tl = min(256, l)
    ta = min(512, l)

    inv_freq = ROPE_THETA ** (-jnp.arange(0, QK_ROPE, 2, dtype=F32) / QK_ROPE)
    ang = positions.astype(F32)[:, None] * inv_freq
    cos, sin = jnp.cos(ang), jnp.sin(ang)
    one = jnp.ones((l, KR_LANE), F32)
    rc = jnp.concatenate([one, cos, cos, jnp.ones((l, HEAD_SLOT - KR_LANE - QK_ROPE), F32)], axis=1)
    rs = jnp.concatenate([0 * one, -sin, sin, jnp.zeros((l, HEAD_SLOT - KR_LANE - QK_ROPE), F32)], axis=1)

    win = _pad_w_in(wts["w_in"])
    wuq = _pad_heads(wts["w_uq"], QK_HEAD)
    wukv = jnp.concatenate([_pad_heads(wts["w_uk"], QK_NOPE), _pad_heads(wts["w_uv"], V_HEAD)], axis=1)

    disc_in = (sp["ssm_lambda_re"], sp["ssm_lambda_im"], sp["ssm_log_dt"], sp["ssm_b_re"], sp["ssm_b_im"])
    (a_re, a_im, bb_re, bb_im), disc_vjp = jax.vjp(_ssm_disc, *disc_in)
    bre, bim = _embed_b(bb_re).astype(BF16), _embed_b(bb_im).astype(BF16)
    cre, cim = _embed_c(sp["ssm_c_re"]).astype(BF16), _embed_c(sp["ssm_c_im"]).astype(BF16)
    dvec = sp["ssm_d"].reshape(1, SSM_W)
    tab_f = _scan_tables(a_re, a_im, False)
    tab_r = _scan_tables(a_re, a_im, True)

    g1, gq, gkv = sp["mix_norm_pre"], sp["q_norm"], sp["kv_norm"]
    gpost, gpre, gfin = sp["mix_norm_post"], sp["ffn_norm_pre"], sp["ffn_norm_post"]
    bgate, bglu, convb = sp["b_gate"], sp["b_glu"], sp["conv_b"]

    hn, cq, ckv, q, k, v, u, gl = _proj_fwd(x, g1, win, gq, wuq, gkv, wukv, rc, rs, bgate, tl)
    attn, lse = _attn_fwd(q, k, v, ta)
    y1, sre, sim = _ssm_fwd(u, bre, bim, cre, cim, dvec, tab_f, ta)
    wba = jnp.pad(wts["w_branch_attn"].reshape(N_HEADS, V_HEAD, D_MODEL),
                  ((0, 0), (0, HEAD_SLOT - V_HEAD), (0, 0))).reshape(HP, D_MODEL)
    wbs, wglu, wout = wts["w_branch_ssm"], wts["w_glu"], wts["w_out"]
    a, sm, merged, z, x1, hn2, y3 = _merge_fwd(x, gl, attn, y1, wba, wbs, wglu, bglu, wout, gpost, gpre, tl)
    late = wts["late"](x1)
    wup, wdown, convw = late["w_up"], late["w_down"], late["conv_w"]
    h = _mm(hn2, wup, "ffn_up")
    act = _conv_fwd(h, convw, convb, ta)
    ff = _mm(act, wdown, "ffn_down")
    loss, dx2, dff, dgfin = _loss_head(ff, x1, tgt, gfin, tl)

    dact = _mm(dff, wdown, "ffn_down_dx", out_dtype=BF16, bt=True)
    d_wdown = _mm_tn(act, dff, "ffn_down_dw", tk_cap=D_FF // 2)
    dh, dwg, dwv, dbg, dbv = _conv_bwd(h, dact, convw, convb, ta)
    d_convw = jnp.concatenate([dwg, dwv], axis=1)
    d_convb = jnp.concatenate([dbg, dbv], axis=1)
    dhn2a = _mm(dh, wup, "ffn_up_dx_gate", bt=True, b_col0=0, a_lead=0)
    dhn2b = _mm(dh, wup, "ffn_up_dx_val", bt=True, b_col0=1, a_lead=1)
    d_wup = _mm_tn(hn2, dh, "ffn_up_dw", chips=True)
    behind = wts["send_grads"]("ffn", {"w_up": d_wup, "w_down": _chip_major(d_wdown, 0)})
    (dx1, dz, dbra, dbrs, dgl, dattn, dy1, dt, y2, dgpre, dgpost, dbgate, dbglu) = _merge_bwd(
        dhn2a, dhn2b, x1, dx2, z, gl, a, sm, y1, wba, wbs, wglu, bglu, wout, gpost, gpre + behind, tl)
    d_wout = _mm_tn(merged, dz, "w_out_dw")
    d_wba = _mm_tn(attn, dbra, "w_branch_attn_dw", chips=True)
    d_wbs = _mm_tn(y3, dbrs, "w_branch_ssm_dw", chips=True)
    d_wglu = _mm_tn(y2, dt, "w_glu_dw")
    ncol = D_MODEL // N_CHIPS
    behind = wts["send_grads"]("mix", {
        "w_glu": _chip_major(d_wglu, 0),
        "w_branch_attn": d_wba.reshape(N_CHIPS, N_HEADS, HEAD_SLOT, ncol)[:, :, :V_HEAD].reshape(
            N_CHIPS, N_HEADS * V_HEAD, ncol),
        "w_branch_ssm": d_wbs,
        "w_out": _chip_major(d_wout, 0)})
    dq, dk, dv = _attn_bwd(q, k, v, dattn, lse + behind, _attn_delta(attn, dattn, min(2048, l)), ta)
    du, dbre, dbim, dcre, dcim, dare, daim, dd = _ssm_bwd(dy1, u, sre, sim, bre, bim, cre, cim, dvec, tab_r, ta)
    gx, dql, qn, ckvn, dproj, dg1, dgq, dgkv = _proj_bwd(
        x, dx1, cq, ckv, dq, dk, dv, du, dgl, g1, win, gq, wuq, gkv, wukv, rc, rs, tl)
    d_win = _mm_tn(hn, dproj, "w_in_dw")
    d_wuq = _mm_tn(qn, dql, "w_uq_dw")
    d_wuk = _mm_tn(ckvn, dk, "w_uk_dw")
    d_wuv = _mm_tn(ckvn, dv, "w_uv_dw")

    d_lre, d_lim, d_ldt, d_bre, d_bim = disc_vjp((dare.reshape(SSM_G, SSM_P), daim.reshape(SSM_G, SSM_P),
                                                  _extract_b(dbre), _extract_b(dbim)))
    big = {
        "w_in": _chip_major(_unpad_w_in(d_win), 1),
        "w_uq": _chip_major(_unpad_heads(d_wuq, QK_HEAD), 1),
        "w_uk": _chip_major(_unpad_heads(d_wuk, QK_NOPE), 1),
        "w_uv": _chip_major(_unpad_heads(d_wuv, V_HEAD), 1),
    }
    small = {
        "conv_w": d_convw,
        "mix_norm_pre": dg1, "q_norm": dgq, "kv_norm": dgkv,
        "ssm_lambda_re": d_lre, "ssm_lambda_im": d_lim, "ssm_log_dt": d_ldt,
        "ssm_b_re": d_bre, "ssm_b_im": d_bim,
        "ssm_c_re": _extract_c(dcre), "ssm_c_im": _extract_c(dcim),
        "ssm_d": dd.reshape(SSM_G, SSM_H), "b_glu": dbglu, "b_gate": dbgate,
        "mix_norm_post": dgpost, "ffn_norm_pre": dgpre, "conv_b": d_convb, "ffn_norm_post": dgfin,
    }
    return loss[0, 0], gx, big, small


_ANY = pl.BlockSpec(memory_space=pl.ANY)


ROW_TILE = 16


def _place():
    x, y, c = lax.axis_index("x"), lax.axis_index("y"), lax.axis_index("c")
    return x, y, c, 2 * x + y, [(1 - x, y), (x, 1 - y), (1 - x, 1 - y)]


def _half(rows, which):
    hr = rows // 2
    return pl.ds(pl.multiple_of(which * hr, ROW_TILE), hr)


def _remote(src, dst, send_sems, recv_sems, n, dev):
    return pltpu.make_async_remote_copy(src_ref=src, dst_ref=dst, send_sem=send_sems.at[n], recv_sem=recv_sems.at[n],
                                        device_id=dev, device_id_type=MESH)


def _gather_big(shards):
    nw = len(shards)
    rows = [s.shape[0] for s in shards]

    def body(*refs):
        ins, outs = refs[:nw], refs[nw:2 * nw]
        ici_send, ici_recv, d2d_send, d2d_recv = refs[2 * nw:]
        x, y, c, me, peers = _place()
        sent = []
        for i in range(nw):
            for p, (px, py) in enumerate(peers):
                cp = _remote(ins[i].at[_half(rows[i], c)], outs[i].at[me, _half(rows[i], c)], ici_send, ici_recv,
                             3 * i + p, (px, py, c))
                cp.start()
                sent.append(cp)
        for p, (px, py) in enumerate(peers):
            for i in range(nw):
                blk = outs[i].at[2 * px + py, _half(rows[i], c)]
                _remote(blk, blk, ici_send, ici_recv, 3 * i + p, (px, py, c)).wait_recv()
                cp = _remote(blk, blk, d2d_send, d2d_recv, 3 * i + p, (x, y, 1 - c))
                cp.start()
                sent.append(cp)
        for p, (px, py) in enumerate(peers):
            for i in range(nw):
                blk = outs[i].at[2 * px + py, _half(rows[i], 1 - c)]
                _remote(blk, blk, d2d_send, d2d_recv, 3 * i + p, (x, y, 1 - c)).wait_recv()
        for cp in sent:
            cp.wait_send()

    dma = pltpu.SemaphoreType.DMA
    return pl.pallas_call(
        body, name="gather_weights", in_specs=[_ANY] * nw, out_specs=[_ANY] * nw,
        out_shape=[jax.ShapeDtypeStruct((N_CHIPS,) + s.shape, s.dtype) for s in shards],
        scratch_shapes=[dma((3 * nw,)), dma((3 * nw,)), dma((3 * nw,)), dma((3 * nw,))],
    )(*shards)


_HBM = pl.BlockSpec(memory_space=pltpu.HBM)
_SEM = pl.BlockSpec(memory_space=pltpu.SEMAPHORE)
_DATAFLOW = pltpu.SideEffectType.DATAFLOW_SIDE_EFFECTING


def _exchange_start(shards, name, scatter):
    nw = len(shards)
    lands = [lax.empty(s.shape if scatter else (N_CHIPS,) + s.shape, s.dtype) for s in shards]

    def body(*refs):
        ins, zones = refs[:nw], refs[nw:2 * nw]
        send_sems, recv_sems, token = refs[2 * nw], refs[2 * nw + 1], refs[-1]
        x, y, c, me, peers = _place()
        for i in range(nw):
            for p, (px, py) in enumerate(peers):
                src = ins[i].at[2 * px + py] if scatter else ins[i]
                _remote(src, zones[i].at[me], send_sems, recv_sems, 3 * i + p, (px, py, c)).start()
        token[...] = jnp.zeros_like(token)

    thru = [pltpu.HBM(a.shape, a.dtype) for a in list(shards) + lands]
    dma = pltpu.SemaphoreType.DMA
    outs = pl.pallas_call(
        body, name=name,
        out_shape=(dma((3 * nw,)), dma((3 * nw,)), *thru, jax.ShapeDtypeStruct((SUBLANES, LANES), F32)),
        in_specs=[_HBM] * (2 * nw),
        out_specs=(_SEM, _SEM, *([_HBM] * (2 * nw)), pl.BlockSpec(memory_space=pltpu.VMEM)),
        input_output_aliases={i: 2 + i for i in range(2 * nw)},
        compiler_params=pltpu.CompilerParams(has_side_effects=_DATAFLOW),
    )(*[pltpu.with_memory_space_constraint(a, pltpu.HBM) for a in list(shards) + lands])
    return outs[0], outs[1], list(outs[2:2 + nw]), list(outs[2 + nw:2 + 2 * nw]), outs[-1]


def _exchange_wait(send_sems, recv_sems, shards, lands, after, name, scatter):
    nw = len(shards)

    def body(*refs):
        ins, zones = refs[:nw], refs[nw:2 * nw]
        send_sems, recv_sems = refs[2 * nw], refs[2 * nw + 1]
        x, y, c, me, peers = _place()
        for i in range(nw):
            for p, (px, py) in enumerate(peers):
                src = ins[i].at[2 * px + py] if scatter else ins[i]
                cp = _remote(src, zones[i].at[2 * px + py], send_sems, recv_sems, 3 * i + p, (px, py, c))
                cp.wait_send()
                cp.wait_recv()

    both = list(shards) + list(lands)
    outs = pl.pallas_call(
        body, name=name,
        out_shape=tuple(pltpu.HBM(a.shape, a.dtype) for a in both),
        in_specs=(*([_HBM] * (2 * nw)), _SEM, _SEM, _ANY), out_specs=[_HBM] * (2 * nw),
        input_output_aliases={i: i for i in range(2 * nw)},
        compiler_params=pltpu.CompilerParams(has_side_effects=_DATAFLOW),
    )(*both, send_sems, recv_sems, after)
    return list(outs[:nw]), list(outs[nw:])


def _reduce_to_sibling(grads, name):
    nw = len(grads)

    def body(*refs):
        ins, outs = refs[:nw], refs[nw:2 * nw]
        send_sems, recv_sems = refs[2 * nw:]
        x, y, c, _, _ = _place()
        sent = []
        for i in range(nw):
            cp = _remote(ins[i].at[pl.ds(0, N_CHIPS), _half(grads[i].shape[1], 1 - c)], outs[i], send_sems, recv_sems,
                         i, (x, y, 1 - c))
            cp.start()
            sent.append(cp)
        for cp in sent:
            cp.wait()

    dma = pltpu.SemaphoreType.DMA
    return pl.pallas_call(
        body, name=name, in_specs=[_ANY] * nw, out_specs=[_ANY] * nw,
        out_shape=[jax.ShapeDtypeStruct((N_CHIPS, g.shape[1] // 2, g.shape[2]), g.dtype) for g in grads],
        scratch_shapes=[dma((nw,)), dma((nw,))],
    )(*grads)


def _reduce_back(totals, name):
    nw = len(totals)

    def body(*refs):
        outs = refs[nw:2 * nw]
        send_sems, recv_sems = refs[2 * nw:]
        x, y, c, _, _ = _place()
        sent = []
        for i in range(nw):
            blk = outs[i].at[_half(totals[i].shape[0], c)]
            cp = _remote(blk, blk, send_sems, recv_sems, i, (x, y, 1 - c))
            cp.start()
            sent.append(cp)
        for i in range(nw):
            blk = outs[i].at[_half(totals[i].shape[0], 1 - c)]
            _remote(blk, blk, send_sems, recv_sems, i, (x, y, 1 - c)).wait_recv()
        for cp in sent:
            cp.wait_send()

    dma = pltpu.SemaphoreType.DMA
    return pl.pallas_call(
        body, name=name, in_specs=[_ANY] * nw, out_specs=[_ANY] * nw,
        out_shape=[jax.ShapeDtypeStruct(t.shape, t.dtype) for t in totals],
        input_output_aliases={i: i for i in range(nw)},
        scratch_shapes=[dma((nw,)), dma((nw,))],
    )(*totals)


def _all_reduce_small(v, name):
    rows, w = v.shape
    hr = rows // 2
    assert hr % SUBLANES == 0

    def body(v_ref, out_ref, sib_ref, half_ref, chips_ref, send_sems, recv_sems):
        x, y, c, me, peers = _place()
        sibling = (x, y, 1 - c)
        mine = pl.ds(pl.multiple_of(c * hr, SUBLANES), hr)
        other = pl.ds(pl.multiple_of((1 - c) * hr, SUBLANES), hr)
        cp = _remote(v_ref, sib_ref, send_sems, recv_sems, 0, sibling)
        cp.start()
        cp.wait()
        half_ref[...] = v_ref[mine, :] + sib_ref[mine, :]
        sent = []
        for p, (px, py) in enumerate(peers):
            cp = _remote(half_ref, chips_ref.at[me], send_sems, recv_sems, 1 + p, (px, py, c))
            cp.start()
            sent.append(cp)
        chips_ref[me] = half_ref[...]
        for p, (px, py) in enumerate(peers):
            _remote(half_ref, chips_ref.at[2 * px + py], send_sems, recv_sems, 1 + p, (px, py, c)).wait_recv()
        for cp in sent:
            cp.wait_send()
        out_ref[mine, :] = ((chips_ref[0] + chips_ref[1]) + chips_ref[2]) + chips_ref[3]
        cp = _remote(out_ref.at[mine], out_ref.at[mine], send_sems, recv_sems, 4, sibling)
        cp.start()
        _remote(out_ref.at[other], out_ref.at[other], send_sems, recv_sems, 4, sibling).wait_recv()
        cp.wait_send()

    vm = pl.BlockSpec(memory_space=pltpu.VMEM)
    return pl.pallas_call(
        body, name=name, in_specs=[vm], out_specs=vm,
        out_shape=jax.ShapeDtypeStruct((rows, w), F32),
        scratch_shapes=[pltpu.VMEM((rows, w), F32), pltpu.VMEM((hr, w), F32), pltpu.VMEM((N_CHIPS, hr, w), F32),
                        pltpu.SemaphoreType.DMA((5,)), pltpu.SemaphoreType.DMA((5,))],
        compiler_params=pltpu.CompilerParams(vmem_limit_bytes=VMEM_LIMIT),
    )(v)


ELEMENTWISE_BLOCK = 256 * 1024


def _rows_tile(rows, cols):
    best = None
    for t in range(SUBLANES, rows + 1, SUBLANES):
        if rows % t == 0 and t * cols <= ELEMENTWISE_BLOCK:
            best = t
    return rows if best is None else best


def _add_pair(g, t, core, name):
    nb, n, w = t.shape
    tr = _rows_tile(n, w)
    steps = n // tr

    def body(core_ref, g_ref, t_ref, o_ref):
        o_ref[...] = (g_ref[...] + t_ref[...]).astype(BF16)

    spec = pl.BlockSpec((1, tr, w), lambda j, i, core_ref: (j, i, 0))
    return pl.pallas_call(
        body, name=name,
        grid_spec=pltpu.PrefetchScalarGridSpec(
            num_scalar_prefetch=1, grid=(nb, steps),
            in_specs=[pl.BlockSpec((1, tr, w), lambda j, i, core_ref: (j, core_ref[0] * steps + i, 0)), spec],
            out_specs=spec),
        out_shape=jax.ShapeDtypeStruct(t.shape, BF16),
        compiler_params=_params(("parallel", "parallel")))(core, g, t)


def _add_chips(landed, pairs, place, name):
    nb, n, w = landed.shape
    tr = _rows_tile(n, w)
    steps = n // tr

    def body(place_ref, r_ref, own_ref, o_ref):
        me = place_ref[0]
        acc = None
        for k in range(nb):
            blk = jnp.where(me == k, own_ref[0], r_ref[k]).astype(F32)
            acc = blk if acc is None else acc + blk
        o_ref[...] = acc

    return pl.pallas_call(
        body, name=name,
        grid_spec=pltpu.PrefetchScalarGridSpec(
            num_scalar_prefetch=1, grid=(steps,),
            in_specs=[pl.BlockSpec((nb, tr, w), lambda i, place_ref: (0, i, 0)),
                      pl.BlockSpec((1, tr, w), lambda i, place_ref: (place_ref[0], i, 0))],
            out_specs=pl.BlockSpec((tr, w), lambda i, place_ref: (place_ref[1] * steps + i, 0))),
        out_shape=jax.ShapeDtypeStruct((2 * n, w), F32),
        compiler_params=_params(("parallel",)))(place, landed, pairs)


def _adamw(w, g, m, v, name):
    rows, wd = w.shape
    tr = _rows_tile(rows, wd)
    c1 = 1.0 - ADAM_B1 ** ADAM_STEP
    c2 = 1.0 - ADAM_B2 ** ADAM_STEP

    def body(w_ref, g_ref, m_ref, v_ref, go_ref, d_ref, mo_ref, vo_ref):
        gv = g_ref[...]
        go_ref[...] = gv
        m2 = ADAM_B1 * m_ref[...] + (1.0 - ADAM_B1) * gv
        v2 = ADAM_B2 * v_ref[...] + (1.0 - ADAM_B2) * (gv * gv)
        mo_ref[...] = m2
        vo_ref[...] = v2
        d_ref[...] = -ADAM_LR * ((m2 / c1) / (jnp.sqrt(v2 / c2) + ADAM_EPS) + ADAM_WD * w_ref[...])

    spec = pl.BlockSpec((tr, wd), lambda i: (i, 0))
    shp = jax.ShapeDtypeStruct((rows, wd), F32)
    return pl.pallas_call(body, name=name, grid=(rows // tr,), in_specs=[spec] * 4, out_specs=[spec] * 4,
                          out_shape=[shp] * 4, compiler_params=_params(("parallel",)))(w, g, m, v)


BIG = [("w_in", (1024, 3232), 1), ("w_uq", (384, 768), 1), ("w_uk", (256, 512), 1), ("w_uv", (256, 512), 1),
       ("w_glu", (512, 512), 0), ("w_branch_attn", (512, 1024), 1), ("w_branch_ssm", (512, 1024), 1),
       ("w_out", (1024, 1024), 0), ("w_up", (1024, 5632), 1), ("conv_w", (3, 5632), 1), ("w_down", (2816, 1024), 0)]
SMALL = [("mix_norm_pre", (1024,)), ("q_norm", (384,)), ("kv_norm", (256,)), ("ssm_lambda_re", (32, 64)),
         ("ssm_lambda_im", (32, 64)), ("ssm_log_dt", (32,)), ("ssm_b_re", (32, 64, 16)), ("ssm_b_im", (32, 64, 16)),
         ("ssm_c_re", (32, 16, 64)), ("ssm_c_im", (32, 16, 64)), ("ssm_d", (32, 16)), ("b_glu", (512,)),
         ("b_gate", (2048,)), ("mix_norm_post", (1024,)), ("ffn_norm_pre", (1024,)), ("conv_b", (5632,)),
         ("ffn_norm_post", (1024,))]
MATMUL_W = [b for b in BIG if b[0] != "conv_w"]
LATE_W = ("w_up", "w_down", "conv_w")
CONV_W_SHAPE = (3, 2 * D_FF)
CONV_W_SHARD = (3, 2 * D_FF // N_CHIPS)
SMALL_SUM = [("loss", (1,))] + SMALL + [("conv_w", CONV_W_SHAPE)]
SMALL_ADAM = SMALL + [("conv_w", CONV_W_SHARD)]


def _pack_flat(layout, vals):
    flat = jnp.concatenate([vals[n].astype(F32).reshape(-1) for n, _ in layout])
    rows = -(-(-(-flat.shape[0] // FLAT_W)) // (2 * SUBLANES)) * 2 * SUBLANES
    return jnp.pad(flat, (0, rows * FLAT_W - flat.shape[0])).reshape(rows, FLAT_W)


def _unpack_flat(layout, flat):
    flat = flat.reshape(-1)
    out = {}
    o = 0
    for name, shape in layout:
        n = math.prod(shape)
        out[name] = flat[o:o + n].reshape(shape)
        o += n
    return out


_ARG_NAMES = ["x", "positions"] + [n for n in (
    "mix_norm_pre", "w_in", "q_norm", "w_uq", "kv_norm", "w_uk", "w_uv", "ssm_lambda_re", "ssm_lambda_im", "ssm_log_dt",
    "ssm_b_re", "ssm_b_im", "ssm_c_re", "ssm_c_im", "ssm_d", "w_glu", "b_glu", "w_branch_attn", "w_branch_ssm",
    "b_gate", "w_out", "mix_norm_post", "ffn_norm_pre", "w_up", "conv_w", "conv_b", "w_down", "ffn_norm_post")]
_WEIGHTS = _ARG_NAMES[2:]


def _gather_weights(w):
    early = [b for b in MATMUL_W if b[0] not in LATE_W]
    late = [b for b in BIG if b[0] in LATE_W]
    own = (jnp.arange(N_CHIPS) == 2 * lax.axis_index("x") + lax.axis_index("y"))[:, None, None]

    def whole(layout, mine, gathered):
        return {name: _from_chip_major(jnp.where(own, s[None], g), axis)
                for (name, _, axis), s, g in zip(layout, mine, gathered)}

    mine = [w[name].astype(BF16) for name, _, _ in early]
    full = whole(early, mine, _gather_big(mine))
    mine_late = [w[name].astype(F32 if name == "conv_w" else BF16) for name, _, _ in late]
    _, mine_late = lax.optimization_barrier((full["w_in"], mine_late))
    send_sems, recv_sems, shards_thru, lands_thru, token = _exchange_start(mine_late, "gather_late_start", scatter=False)

    def late_weights(after):
        shards, lands = _exchange_wait(send_sems, recv_sems, shards_thru, lands_thru, after, "gather_late_wait",
                                       scatter=False)
        return whole(late, shards, lands)

    full["late"] = late_weights
    full["token"] = token[0, 0]
    return full


def _pair_sums(names, grads, tag):
    core = lax.axis_index("c").astype(jnp.int32).reshape(1)
    theirs = _reduce_to_sibling(grads, "reduce_grads_d2d" + tag)
    return [_add_pair(g, t, core, "reduce_pair_" + n) for n, g, t in zip(names, grads, theirs)]


def _start_reduce(tag, grads):
    names = list(grads)
    pairs = _pair_sums(names, [grads[n] for n in names], "_" + tag)
    send_sems, recv_sems, pairs_thru, lands_thru, token = _exchange_start(pairs, "reduce_" + tag + "_start", scatter=True)
    return (tag, names, send_sems, recv_sems, pairs_thru, lands_thru), token[0, 0]


def _reduce_grads(gbig, pending, loss, gsmall, use_sent):
    core = lax.axis_index("c").astype(jnp.int32).reshape(1)
    chip = (2 * lax.axis_index("x") + lax.axis_index("y")).astype(jnp.int32).reshape(1)
    place = jnp.concatenate([chip, core])

    def finish(names, pairs, landed, name):
        totals = [_add_chips(r, p, place, "reduce_chips_" + n) for n, r, p in zip(names, landed, pairs)]
        return dict(zip(names, _reduce_back(totals, name)))

    names = list(gbig)
    pairs = _pair_sums(names, [gbig[n] for n in names], "")
    send_sems, recv_sems, pairs_thru, lands_thru, token = _exchange_start(pairs, "reduce_last_start", scatter=True)
    sent_names, sent_pairs, sent_landed = [], [], []
    for tag, group, g_send, g_recv, g_pairs, g_lands in pending:
        got_pairs, got_landed = _exchange_wait(g_send, g_recv, g_pairs, g_lands, token, "reduce_" + tag + "_wait",
                                               scatter=True)
        sent_names, sent_pairs, sent_landed = sent_names + group, sent_pairs + got_pairs, sent_landed + got_landed
    g_sent = finish(sent_names, sent_pairs, sent_landed, "reduce_back_sent")
    vals = dict(gsmall)
    vals["loss"] = loss
    small_red = _unpack_flat(SMALL_SUM, _all_reduce_small(_pack_flat(SMALL_SUM, vals), "reduce_small"))
    after = use_sent(g_sent, small_red)
    pairs, landed = _exchange_wait(send_sems, recv_sems, pairs_thru, lands_thru, after, "reduce_last_wait", scatter=True)
    return finish(names, pairs, landed, "reduce_back_last"), small_red


def _step(args):
    x = args["x"][0]
    positions = args["positions"][0]
    tgt = args["loss_target"][0]
    w = {n: args[n][0] for n in _WEIGHTS}
    m = {n: args["m_" + n][0] for n in _WEIGHTS}
    v = {n: args["v_" + n][0] for n in _WEIGHTS}

    full = _gather_weights(w)
    sp = {n: w[n].reshape(s) for n, s in SMALL}
    for n in ("mix_norm_pre", "q_norm", "kv_norm", "b_glu", "b_gate", "mix_norm_post", "ffn_norm_pre", "conv_b",
              "ffn_norm_post"):
        sp[n] = sp[n].reshape(1, -1)
    sp["mix_norm_pre"] = sp["mix_norm_pre"] + full.pop("token")
    pending = []

    def send_grads(tag, grads):
        state, token = _start_reduce(tag, grads)
        pending.append(state)
        return token

    full["send_grads"] = send_grads
    loss, gx, gbig, gsmall = _local_step(x, positions, tgt, full, sp)
    outs = {}

    def adam_big(g_red):
        for name in g_red:
            g2, d, m2, v2 = _adamw(w[name], g_red[name], m[name], v[name], "adamw_" + name)
            outs["grad_" + name], outs["delta_" + name], outs["new_m_" + name], outs["new_v_" + name] = g2, d, m2, v2
        return v2

    def use_sent(g_sent, small_red):
        chip = 2 * lax.axis_index("x") + lax.axis_index("y")
        grads = dict(small_red)
        grads["conv_w"] = lax.dynamic_slice_in_dim(small_red["conv_w"], chip * CONV_W_SHARD[1], CONV_W_SHARD[1], axis=1)
        outs.update({"grad_" + n: grads[n] for n, _ in SMALL_ADAM})
        _, d_sm, m_sm, v_sm = _adamw(_pack_flat(SMALL_ADAM, w), _pack_flat(SMALL_ADAM, grads),
                                     _pack_flat(SMALL_ADAM, m), _pack_flat(SMALL_ADAM, v), "adamw_small")
        for prefix, flat in (("delta_", d_sm), ("new_m_", m_sm), ("new_v_", v_sm)):
            for n, val in _unpack_flat(SMALL_ADAM, flat).items():
                outs[prefix + n] = val
        return adam_big(g_sent)

    g_last, small_red = _reduce_grads(gbig, pending, loss, gsmall, use_sent)
    adam_big(g_last)
    outs = {n: val.reshape(args[n.split("_", 1)[1] if not n.startswith("new_") else n[6:]].shape)
            for n, val in outs.items()}
    res = [small_red["loss"][0], gx[None]]
    for prefix in ("grad_", "delta_", "new_m_", "new_v_"):
        res += [outs[prefix + n] for n in _WEIGHTS]
    return tuple(res)


def kernel(x, positions, mix_norm_pre, w_in, q_norm, w_uq, kv_norm, w_uk, w_uv, ssm_lambda_re, ssm_lambda_im, ssm_log_dt, ssm_b_re, ssm_b_im, ssm_c_re, ssm_c_im, ssm_d, w_glu, b_glu, w_branch_attn, w_branch_ssm, b_gate, w_out, mix_norm_post, ffn_norm_pre, w_up, conv_w, conv_b, w_down, ffn_norm_post, loss_target, m_mix_norm_pre, m_w_in, m_q_norm, m_w_uq, m_kv_norm, m_w_uk, m_w_uv, m_ssm_lambda_re, m_ssm_lambda_im, m_ssm_log_dt, m_ssm_b_re, m_ssm_b_im, m_ssm_c_re, m_ssm_c_im, m_ssm_d, m_w_glu, m_b_glu, m_w_branch_attn, m_w_branch_ssm, m_b_gate, m_w_out, m_mix_norm_post, m_ffn_norm_pre, m_w_up, m_conv_w, m_conv_b, m_w_down, m_ffn_norm_post, v_mix_norm_pre, v_w_in, v_q_norm, v_w_uq, v_kv_norm, v_w_uk, v_w_uv, v_ssm_lambda_re, v_ssm_lambda_im, v_ssm_log_dt, v_ssm_b_re, v_ssm_b_im, v_ssm_c_re, v_ssm_c_im, v_ssm_d, v_w_glu, v_b_glu, v_w_branch_attn, v_w_branch_ssm, v_b_gate, v_w_out, v_mix_norm_post, v_ffn_norm_pre, v_w_up, v_conv_w, v_conv_b, v_w_down, v_ffn_norm_post):
    given = dict(locals())
    return _step(given)
```

```python
import math

import jax
import jax.numpy as jnp
from jax import lax
from jax.experimental import pallas as pl
from jax.experimental.pallas import tpu as pltpu

F32 = jnp.float32
BF16 = jnp.bfloat16
MESH = pl.DeviceIdType.MESH

D_MODEL = 1024
N_HEADS = 8
QK_NOPE = 64
QK_ROPE = 32
QK_HEAD = QK_NOPE + QK_ROPE
V_HEAD = 64
Q_RANK = 384
KV_RANK = 256
ROPE_THETA = 10000.0
SSM_W = 512
SSM_H = 16
SSM_G = 32
SSM_P = 64
SSM_CH = SSM_G * SSM_P
D_FF = 2816
EPS = 1e-6
ADAM_LR = 0.001
ADAM_B1 = 0.9
ADAM_B2 = 0.999
ADAM_EPS = 1e-08
ADAM_WD = 0.01
ADAM_STEP = 10

LANES = 128
SUBLANES = 8
VMEM_LIMIT = 56 * 1024 * 1024

HEAD_SLOT = LANES
HP = N_HEADS * HEAD_SLOT
P_CQ, P_CKV, P_KR, P_U, P_GL, P_END = 0, 384, 640, 768, 1280, 3328
KR_LANE = 64

FLAT_W = 1024
N_CHIPS = 4


def _tile(n, cap):
    if n <= cap:
        return n
    best = None
    for t in range(LANES, cap + 1, LANES):
        if n % t == 0:
            best = t
    assert best is not None, (n, cap)
    return best


def _params(sem):
    return pltpu.CompilerParams(dimension_semantics=sem, vmem_limit_bytes=VMEM_LIMIT)


def _dot(a, b):
    return jnp.dot(a, b, preferred_element_type=F32)


def _dot_nt(a, b):
    return lax.dot_general(a, b, (((1,), (1,)), ((), ())), preferred_element_type=F32)


def _dot_tn(a, b):
    return lax.dot_general(a, b, (((0,), (0,)), ((), ())), preferred_element_type=F32)


def _rms(x, g):
    r = lax.rsqrt(jnp.mean(x * x, axis=-1, keepdims=True) + EPS)
    return x * r * g, r


def _rms_bwd(dy, x, g):
    r = lax.rsqrt(jnp.mean(x * x, axis=-1, keepdims=True) + EPS)
    dyg = dy * g
    dx = r * dyg - x * (r * r * r) * jnp.mean(dyg * x, axis=-1, keepdims=True)
    dg = jnp.sum(dy * x * r, axis=0, keepdims=True)
    return dx, dg


_GELU_K0 = math.sqrt(2.0 / math.pi)
_GELU_K1 = 0.044715


def _gelu(x):
    th = jnp.tanh(_GELU_K0 * (x + _GELU_K1 * x * x * x))
    return 0.5 * x * (1.0 + th)


def _gelu_grad(x):
    th = jnp.tanh(_GELU_K0 * (x + _GELU_K1 * x * x * x))
    return 0.5 * (1.0 + th) + 0.5 * x * (1.0 - th * th) * _GELU_K0 * (1.0 + 3.0 * _GELU_K1 * x * x)


def _sigmoid(x):
    return 1.0 / (1.0 + jnp.exp(-x))


def _rope(q, c, s):
    n = q.shape[1]
    lane = lax.broadcasted_iota(jnp.int32, q.shape, 1) % HEAD_SLOT
    sw = jnp.where(lane < KR_LANE + QK_ROPE // 2, pltpu.roll(q, n - QK_ROPE // 2, 1), pltpu.roll(q, QK_ROPE // 2, 1))
    return q * c + sw * s


def _rope_bwd(dy, c, s):
    n = dy.shape[1]
    t = dy * s
    lane = lax.broadcasted_iota(jnp.int32, dy.shape, 1) % HEAD_SLOT
    sw = jnp.where(lane < KR_LANE + QK_ROPE // 2, pltpu.roll(t, n - QK_ROPE // 2, 1), pltpu.roll(t, QK_ROPE // 2, 1))
    rope_lane = jnp.logical_and(lane >= KR_LANE, lane < KR_LANE + QK_ROPE)
    return dy * c + jnp.where(rope_lane, sw, 0.0)


def _shift_down(x, k, halo):
    xs = pltpu.roll(x, k, 0)
    hs = pltpu.roll(halo, k, 0)
    rows = lax.broadcasted_iota(jnp.int32, halo.shape, 0)
    top = jnp.where(rows < k, hs, xs[0:SUBLANES])
    return jnp.concatenate([top, xs[SUBLANES:]], axis=0)


def _shift_up(x, k, halo):
    t = x.shape[0]
    xs = pltpu.roll(x, t - k, 0)
    hs = pltpu.roll(halo, SUBLANES - k, 0)
    rows = lax.broadcasted_iota(jnp.int32, halo.shape, 0)
    bot = jnp.where(rows >= SUBLANES - k, hs, xs[t - SUBLANES:])
    return jnp.concatenate([xs[:t - SUBLANES], bot], axis=0)


def _mm(a, b, name, out_dtype=F32, bt=False, b_col0=0, n=None, tm_cap=1024, tn_cap=1408, a_lead=None):
    m, k = a.shape[-2:]
    if bt:
        n_full = b.shape[0]
        n = n_full
    else:
        n = b.shape[1] if n is None else n
    tm = min(tm_cap, m)
    tn = _tile(n, tn_cap)

    def body(a_ref, b_ref, o_ref):
        if bt:
            o_ref[...] = _dot_nt(a_ref[...], b_ref[...]).astype(out_dtype)
        else:
            o_ref[...] = _dot(a_ref[...], b_ref[...]).astype(out_dtype)

    if bt:
        b_spec = pl.BlockSpec((tn, k), lambda j, i: (j, b_col0))
    else:
        off = b_col0 * (n // tn)
        b_spec = pl.BlockSpec((k, tn), lambda j, i: (0, off + j))
    if a_lead is None:
        a_spec = pl.BlockSpec((tm, k), lambda j, i: (i, 0))
    else:
        a_spec = pl.BlockSpec((None, tm, k), lambda j, i: (a_lead, i, 0))
    return pl.pallas_call(
        body, name=name, grid=(n // tn, m // tm),
        in_specs=[a_spec, b_spec],
        out_specs=pl.BlockSpec((tm, tn), lambda j, i: (i, j)),
        out_shape=jax.ShapeDtypeStruct((m, n), out_dtype),
        compiler_params=_params(("parallel", "parallel")),
    )(a, b)


def _mm_tn(a, b, name, tk_cap=1024, tn_cap=1664, tl_cap=1024, chips=False):
    l, k = a.shape
    tk = _tile(k, tk_cap)
    tl = min(tl_cap, l)

    def body(a_ref, b_ref, o_ref):
        @pl.when(pl.program_id(2) == 0)
        def _():
            o_ref[...] = jnp.zeros_like(o_ref)

        o_ref[...] += _dot_tn(a_ref[...], b_ref[...])

    if chips:
        n = b.shape[-1] * (b.shape[0] if b.ndim == 3 else 1)
        tn = n // N_CHIPS
        assert tn % LANES == 0
        if b.ndim == 3:
            per = N_CHIPS // b.shape[0]
            b_spec = pl.BlockSpec((None, tl, tn), lambda i, j, r: (j // per, r, j % per))
        else:
            b_spec = pl.BlockSpec((tl, tn), lambda i, j, r: (r, j))
        out_spec = pl.BlockSpec((None, tk, tn), lambda i, j, r: (j, i, 0))
        out_shape = jax.ShapeDtypeStruct((N_CHIPS, k, tn), F32)
    else:
        n = b.shape[1]
        tn = _tile(n, tn_cap)
        b_spec = pl.BlockSpec((tl, tn), lambda i, j, r: (r, j))
        out_spec = pl.BlockSpec((tk, tn), lambda i, j, r: (i, j))
        out_shape = jax.ShapeDtypeStruct((k, n), F32)
    return pl.pallas_call(
        body, name=name, grid=(k // tk, n // tn, l // tl),
        in_specs=[pl.BlockSpec((tl, tk), lambda i, j, r: (r, i)), b_spec],
        out_specs=out_spec, out_shape=out_shape,
        compiler_params=_params(("parallel", "parallel", "arbitrary")),
    )(a, b)


def _row(tl, n):
    return pl.BlockSpec((tl, n), lambda i: (i, 0))


def _const(shape):
    return pl.BlockSpec(shape, lambda i: tuple(0 for _ in shape))


def _proj_fwd(x, g1, win, gq, wuq, gkv, wukv, rc, rs, bg, tl):
    l = x.shape[0]

    def body(x_ref, g1_ref, win_ref, gq_ref, wuq_ref, gkv_ref, wukv_ref, rc_ref, rs_ref, bg_ref,
             hn_ref, cq_ref, ckv_ref, q_ref, k_ref, v_ref, u_ref, gl_ref):
        hn, _ = _rms(x_ref[...], g1_ref[...])
        hnb = hn.astype(BF16)
        hn_ref[...] = hnb
        proj = _dot(hnb, win_ref[...])
        cq = proj[:, P_CQ:P_CKV]
        ckv = proj[:, P_CKV:P_KR]
        kr = proj[:, P_KR:P_U]
        cq_ref[...] = cq
        ckv_ref[...] = ckv
        u_ref[...] = proj[:, P_U:P_GL]
        gl_ref[...] = proj[:, P_GL:P_END] + bg_ref[...]
        qn, _ = _rms(cq, gq_ref[...])
        q = _dot(qn.astype(BF16), wuq_ref[...])
        c1 = rc_ref[...]
        s1 = rs_ref[...]
        q_ref[...] = (_rope(q, jnp.tile(c1, (1, N_HEADS)), jnp.tile(s1, (1, N_HEADS))) * Q_PRESCALE).astype(BF16)
        ckvn, _ = _rms(ckv, gkv_ref[...])
        kv = _dot(ckvn.astype(BF16), wukv_ref[...])
        krr = _rope(kr, c1, s1)
        k_ref[...] = (kv[:, :HP] + jnp.tile(krr, (1, N_HEADS))).astype(BF16)
        v_ref[...] = kv[:, HP:].astype(BF16)

    outs = [(D_MODEL, BF16), (Q_RANK, F32), (KV_RANK, F32), (HP, BF16), (HP, BF16), (HP, BF16),
            (SSM_W, F32), (2 * D_MODEL, F32)]
    return pl.pallas_call(
        body, name="proj_fwd", grid=(l // tl,),
        in_specs=[_row(tl, D_MODEL), _const((1, D_MODEL)), _const((D_MODEL, P_END)), _const((1, Q_RANK)),
                  _const((Q_RANK, HP)), _const((1, KV_RANK)), _const((KV_RANK, 2 * HP)),
                  _row(tl, HEAD_SLOT), _row(tl, HEAD_SLOT), _const((1, 2 * D_MODEL))],
        out_specs=[_row(tl, n) for n, _ in outs],
        out_shape=[jax.ShapeDtypeStruct((l, n), dt) for n, dt in outs],
        compiler_params=_params(("parallel",)),
    )(x, g1, win, gq, wuq, gkv, wukv, rc, rs, bg)


_NEG = -1e30


LOG2E = 1.0 / math.log(2.0)
LN2 = math.log(2.0)
ATTN_SCALE = 1.0 / math.sqrt(QK_HEAD)
Q_PRESCALE = ATTN_SCALE * LOG2E
HEADS_PER_STEP = 4
PAIR_W = HEADS_PER_STEP * HEAD_SLOT


def _causal_pairs(nq, by_query):
    if by_query:
        pairs = [(i, j) for i in range(nq) for j in range(i + 1)]
    else:
        pairs = [(i, j) for j in range(nq) for i in range(j, nq)]
    return jnp.array([p[0] for p in pairs], jnp.int32), jnp.array([p[1] for p in pairs], jnp.int32)


def _diag_mask_t(s):
    rows = lax.broadcasted_iota(jnp.int32, s.shape, 0)
    cols = lax.broadcasted_iota(jnp.int32, s.shape, 1)
    return jnp.where(rows <= cols, s, _NEG)


def _attn_fwd(q, k, v, tq):
    l = q.shape[0]
    nq = l // tq
    it, jt = _causal_pairs(nq, True)

    def body(it_ref, jt_ref, q_ref, k_ref, v_ref, o_ref, lse_ref, m_ref, l_ref, acc_ref):
        t = pl.program_id(1)
        i = it_ref[t]
        j = jt_ref[t]

        @pl.when(j == 0)
        def _():
            m_ref[...] = jnp.full_like(m_ref, _NEG)
            l_ref[...] = jnp.zeros_like(l_ref)
            acc_ref[...] = jnp.zeros_like(acc_ref)

        def update(on_diagonal):
            for hh in range(HEADS_PER_STEP):
                sl = slice(hh * HEAD_SLOT, (hh + 1) * HEAD_SLOT)
                s = _dot_nt(k_ref[:, sl], q_ref[:, sl])
                if on_diagonal:
                    s = _diag_mask_t(s)
                m_old = m_ref[hh]
                m_new = jnp.maximum(m_old, jnp.max(s, axis=0, keepdims=True))
                p = jnp.exp2(s - m_new)
                alpha = jnp.exp2(m_old - m_new)
                l_ref[hh] = alpha * l_ref[hh] + jnp.sum(p, axis=0, keepdims=True)
                acc_ref[hh] = alpha * acc_ref[hh] + _dot_tn(v_ref[:, sl], p.astype(BF16))
                m_ref[hh] = m_new

        @pl.when(j < i)
        def _():
            update(False)

        @pl.when(j == i)
        def _():
            update(True)
            for hh in range(HEADS_PER_STEP):
                sl = slice(hh * HEAD_SLOT, (hh + 1) * HEAD_SLOT)
                o_ref[:, sl] = (acc_ref[hh] / l_ref[hh]).T.astype(BF16)
                lse_ref[hh] = m_ref[hh] + jnp.log(l_ref[hh]) * LOG2E

    blk = (tq, PAIR_W)
    qmap = lambda h, t, it_ref, jt_ref: (it_ref[t], h)
    kmap = lambda h, t, it_ref, jt_ref: (jt_ref[t], h)
    row = pl.BlockSpec((HEADS_PER_STEP, 1, tq), lambda h, t, it_ref, jt_ref: (h, 0, it_ref[t]))
    return pl.pallas_call(
        body, name="attn_fwd",
        grid_spec=pltpu.PrefetchScalarGridSpec(
            num_scalar_prefetch=2, grid=(N_HEADS // HEADS_PER_STEP, it.shape[0]),
            in_specs=[pl.BlockSpec(blk, qmap), pl.BlockSpec(blk, kmap), pl.BlockSpec(blk, kmap)],
            out_specs=[pl.BlockSpec(blk, qmap), row],
            scratch_shapes=[pltpu.VMEM((HEADS_PER_STEP, 1, tq), F32), pltpu.VMEM((HEADS_PER_STEP, 1, tq), F32),
                            pltpu.VMEM((HEADS_PER_STEP, HEAD_SLOT, tq), F32)]),
        out_shape=[jax.ShapeDtypeStruct((l, HP), BF16), jax.ShapeDtypeStruct((N_HEADS, 1, l), F32)],
        compiler_params=_params(("parallel", "arbitrary")),
    )(it, jt, q, k, v)


def _attn_delta(o, do, tq):
    l = o.shape[0]

    def body(o_ref, do_ref, d_ref):
        prod = o_ref[...].astype(F32) * do_ref[...].astype(F32)
        for hh in range(HEADS_PER_STEP):
            d_ref[hh] = jnp.sum(prod[:, hh * HEAD_SLOT:(hh + 1) * HEAD_SLOT].T, axis=0, keepdims=True)

    blk = pl.BlockSpec((tq, PAIR_W), lambda h, i: (i, h))
    return pl.pallas_call(
        body, name="attn_delta", grid=(N_HEADS // HEADS_PER_STEP, l // tq), in_specs=[blk, blk],
        out_specs=pl.BlockSpec((HEADS_PER_STEP, 1, tq), lambda h, i: (h, 0, i)),
        out_shape=jax.ShapeDtypeStruct((N_HEADS, 1, l), F32),
        compiler_params=_params(("parallel", "parallel")),
    )(o, do)


def _attn_bwd(q, k, v, do, lse, delta, tq):
    l = q.shape[0]
    nq = l // tq
    it, jt = _causal_pairs(nq, False)

    def body(it_ref, jt_ref, q_ref, k_ref, v_ref, do_ref, lse_ref, dl_ref, dq_ref, dk_ref, dv_ref, dka_ref, dva_ref):
        t = pl.program_id(1)
        i = it_ref[t]
        j = jt_ref[t]

        @pl.when(t == 0)
        def _():
            dq_ref[...] = jnp.zeros_like(dq_ref)

        @pl.when(i == j)
        def _():
            dka_ref[...] = jnp.zeros_like(dka_ref)
            dva_ref[...] = jnp.zeros_like(dva_ref)

        def update(on_diagonal):
            r0 = pl.multiple_of(i * tq, tq)
            for hh in range(HEADS_PER_STEP):
                sl = slice(hh * HEAD_SLOT, (hh + 1) * HEAD_SLOT)
                qb = q_ref[:, sl]
                kb = k_ref[:, sl]
                dob = do_ref[:, sl]
                s = _dot_nt(kb, qb)
                if on_diagonal:
                    s = _diag_mask_t(s)
                p = jnp.exp2(s - lse_ref[hh])
                dva_ref[:, sl] += _dot(p.astype(BF16), dob)
                dp = _dot_nt(v_ref[:, sl], dob)
                ds = (p * (dp - dl_ref[hh])).astype(BF16)
                dka_ref[:, sl] += _dot(ds, qb)
                dq_ref[pl.ds(r0, tq), sl] += ATTN_SCALE * _dot_tn(ds, kb)

        @pl.when(j < i)
        def _():
            update(False)

        @pl.when(j == i)
        def _():
            update(True)

        @pl.when(i == nq - 1)
        def _():
            dk_ref[...] = (dka_ref[...] * LN2).astype(BF16)
            dv_ref[...] = dva_ref[...].astype(BF16)

    blk = (tq, PAIR_W)
    qmap = lambda h, t, it_ref, jt_ref: (it_ref[t], h)
    kmap = lambda h, t, it_ref, jt_ref: (jt_ref[t], h)
    row = pl.BlockSpec((HEADS_PER_STEP, 1, tq), lambda h, t, it_ref, jt_ref: (h, 0, it_ref[t]))
    return pl.pallas_call(
        body, name="attn_bwd",
        grid_spec=pltpu.PrefetchScalarGridSpec(
            num_scalar_prefetch=2, grid=(N_HEADS // HEADS_PER_STEP, it.shape[0]),
            in_specs=[pl.BlockSpec(blk, qmap), pl.BlockSpec(blk, kmap), pl.BlockSpec(blk, kmap),
                      pl.BlockSpec(blk, qmap), row, row],
            out_specs=[pl.BlockSpec((l, PAIR_W), lambda h, t, it_ref, jt_ref: (0, h)), pl.BlockSpec(blk, kmap),
                       pl.BlockSpec(blk, kmap)],
            scratch_shapes=[pltpu.VMEM(blk, F32), pltpu.VMEM(blk, F32)]),
        out_shape=[jax.ShapeDtypeStruct((l, HP), F32), jax.ShapeDtypeStruct((l, HP), BF16),
                   jax.ShapeDtypeStruct((l, HP), BF16)],
        compiler_params=_params(("parallel", "arbitrary")),
    )(it, jt, q, k, v, do, lse, delta)


SSM_CB = 512
SSM_UB = 128
SSM_NB = SSM_CH // SSM_CB


def _scan_tiles(re_ref, im_ref, tab, carry, n_tiles, reverse):
    group = 2
    assert n_tiles % group == 0
    pr, pi = tab[6], tab[7]

    def inside(sr, si):
        for step, k in enumerate((1, 2, 4)):
            mr, mi = tab[2 * step], tab[2 * step + 1]
            sh = (SUBLANES - k) if reverse else k
            rr = pltpu.roll(sr, sh, 0)
            ri = pltpu.roll(si, sh, 0)
            sr, si = sr + mr * rr - mi * ri, si + mr * ri + mi * rr
        return sr, si

    def body(n, c):
        cr, ci = c
        first = (n_tiles - group * (n + 1)) if reverse else group * n
        r0 = pl.multiple_of(first * SUBLANES, group * SUBLANES)
        rows = [pl.ds(r0 + g * SUBLANES, SUBLANES) for g in range(group)]
        tiles = [inside(re_ref[r, :], im_ref[r, :]) for r in rows]
        for g in (range(group - 1, -1, -1) if reverse else range(group)):
            sr, si = tiles[g]
            sr, si = sr + pr * cr - pi * ci, si + pr * ci + pi * cr
            re_ref[rows[g], :] = sr
            im_ref[rows[g], :] = si
            edge = slice(0, 1) if reverse else slice(SUBLANES - 1, SUBLANES)
            cr, ci = sr[edge, :], si[edge, :]
        return cr, ci

    return lax.fori_loop(0, n_tiles // group, body, carry)


def _ssm_fwd(u, bre, bim, cre, cim, dvec, tab, tt):
    l = u.shape[0]
    nt = l // tt

    def body(u_ref, bre_ref, bim_ref, cre_ref, cim_ref, d_ref, tab_ref, y_ref, sre_ref, sim_ref, car_ref):
        @pl.when(pl.program_id(1) == 0)
        def _():
            car_ref[...] = jnp.zeros_like(car_ref)

        uf = u_ref[...]
        ub = uf.astype(BF16)
        sre_ref[...] = _dot(ub, bre_ref[0])
        sim_ref[...] = _dot(ub, bim_ref[0])
        tab_v = [tab_ref[n] for n in range(8)]
        cr, ci = _scan_tiles(sre_ref, sim_ref, tab_v, (car_ref[0:1, :], car_ref[8:9, :]), tt // SUBLANES, False)
        car_ref[0:1, :] = cr
        car_ref[8:9, :] = ci
        y_ref[...] = (_dot(sre_ref[...].astype(BF16), cre_ref[0]) - _dot(sim_ref[...].astype(BF16), cim_ref[0])
                      + d_ref[...] * uf)

    return pl.pallas_call(
        body, name="ssm_fwd", grid=(SSM_NB, nt),
        in_specs=[pl.BlockSpec((tt, SSM_UB), lambda m, t: (t, m)),
                  pl.BlockSpec((1, SSM_UB, SSM_CB), lambda m, t: (m, 0, 0)),
                  pl.BlockSpec((1, SSM_UB, SSM_CB), lambda m, t: (m, 0, 0)),
                  pl.BlockSpec((1, SSM_CB, SSM_UB), lambda m, t: (m, 0, 0)),
                  pl.BlockSpec((1, SSM_CB, SSM_UB), lambda m, t: (m, 0, 0)),
                  pl.BlockSpec((1, SSM_UB), lambda m, t: (0, m)),
                  pl.BlockSpec((8, SUBLANES, SSM_CB), lambda m, t: (0, 0, m))],
        out_specs=[pl.BlockSpec((tt, SSM_UB), lambda m, t: (t, m)),
                   pl.BlockSpec((tt, SSM_CB), lambda m, t: (t, m)),
                   pl.BlockSpec((tt, SSM_CB), lambda m, t: (t, m))],
        out_shape=[jax.ShapeDtypeStruct((l, SSM_W), F32), jax.ShapeDtypeStruct((l, SSM_CH), F32),
                   jax.ShapeDtypeStruct((l, SSM_CH), F32)],
        scratch_shapes=[pltpu.VMEM((2 * SUBLANES, SSM_CB), F32)],
        compiler_params=_params(("parallel", "arbitrary")),
    )(u, bre, bim, cre, cim, dvec, tab)


def _ssm_bwd(dy, u, sre, sim, bre, bim, cre, cim, dvec, tab, tt):
    l = u.shape[0]
    nt = l // tt
    tpb = tt // SUBLANES

    def body(dy_ref, u_ref, sre_ref, sim_ref, hre_ref, him_ref, bre_ref, bim_ref, cre_ref, cim_ref, d_ref, tab_ref,
             du_ref, dbre_ref, dbim_ref, dcre_ref, dcim_ref, dare_ref, daim_ref, dd_ref, lr_ref, li_ref, car_ref):
        t = pl.program_id(1)

        @pl.when(t == 0)
        def _():
            car_ref[...] = jnp.zeros_like(car_ref)
            for ref in (dbre_ref, dbim_ref, dcre_ref, dcim_ref, dare_ref, daim_ref, dd_ref):
                ref[...] = jnp.zeros_like(ref)

        dyf = dy_ref[...]
        dyb = dyf.astype(BF16)
        uf = u_ref[...]
        s_re = sre_ref[...]
        s_im = sim_ref[...]
        lr_ref[...] = _dot_nt(dyb, cre_ref[0])
        li_ref[...] = -_dot_nt(dyb, cim_ref[0])
        dcre_ref[0] += _dot_tn(s_re.astype(BF16), dyb)
        dcim_ref[0] -= _dot_tn(s_im.astype(BF16), dyb)
        tab_v = [tab_ref[n] for n in range(8)]
        cr, ci = _scan_tiles(lr_ref, li_ref, tab_v, (car_ref[0:1, :], car_ref[8:9, :]), tpb, True)
        car_ref[0:1, :] = cr
        car_ref[8:9, :] = ci
        lam_r = lr_ref[...]
        lam_i = li_ref[...]
        keep = jnp.where(t == nt - 1, 0.0, 1.0)
        sp_r = _shift_down(s_re, 1, hre_ref[...] * keep)
        sp_i = _shift_down(s_im, 1, him_ref[...] * keep)
        dare_ref[...] += jnp.sum(lam_r * sp_r + lam_i * sp_i, axis=0, keepdims=True)
        daim_ref[...] += jnp.sum(lam_i * sp_r - lam_r * sp_i, axis=0, keepdims=True)
        lrb = lam_r.astype(BF16)
        lib = lam_i.astype(BF16)
        du_ref[...] = _dot_nt(lrb, bre_ref[0]) + _dot_nt(lib, bim_ref[0]) + dyf * d_ref[...]
        ub = uf.astype(BF16)
        dbre_ref[0] += _dot_tn(ub, lrb)
        dbim_ref[0] += _dot_tn(ub, lib)
        dd_ref[...] += jnp.sum(dyf * uf, axis=0, keepdims=True)

    rev = lambda m, t: (nt - 1 - t, m)
    halo = lambda m, t: (jnp.maximum((nt - 1 - t) * tpb - 1, 0), m)
    wb = pl.BlockSpec((1, SSM_UB, SSM_CB), lambda m, t: (m, 0, 0))
    wc = pl.BlockSpec((1, SSM_CB, SSM_UB), lambda m, t: (m, 0, 0))
    vec_c = pl.BlockSpec((1, SSM_CB), lambda m, t: (0, m))
    vec_u = pl.BlockSpec((1, SSM_UB), lambda m, t: (0, m))
    return pl.pallas_call(
        body, name="ssm_bwd", grid=(SSM_NB, nt),
        in_specs=[pl.BlockSpec((tt, SSM_UB), rev), pl.BlockSpec((tt, SSM_UB), rev),
                  pl.BlockSpec((tt, SSM_CB), rev), pl.BlockSpec((tt, SSM_CB), rev),
                  pl.BlockSpec((SUBLANES, SSM_CB), halo), pl.BlockSpec((SUBLANES, SSM_CB), halo),
                  wb, wb, wc, wc, vec_u,
                  pl.BlockSpec((8, SUBLANES, SSM_CB), lambda m, t: (0, 0, m))],
        out_specs=[pl.BlockSpec((tt, SSM_UB), rev), wb, wb, wc, wc, vec_c, vec_c, vec_u],
        out_shape=[jax.ShapeDtypeStruct((l, SSM_W), F32),
                   jax.ShapeDtypeStruct((SSM_NB, SSM_UB, SSM_CB), F32), jax.ShapeDtypeStruct((SSM_NB, SSM_UB, SSM_CB), F32),
                   jax.ShapeDtypeStruct((SSM_NB, SSM_CB, SSM_UB), F32), jax.ShapeDtypeStruct((SSM_NB, SSM_CB, SSM_UB), F32),
                   jax.ShapeDtypeStruct((1, SSM_CH), F32), jax.ShapeDtypeStruct((1, SSM_CH), F32),
                   jax.ShapeDtypeStruct((1, SSM_W), F32)],
        scratch_shapes=[pltpu.VMEM((tt, SSM_CB), F32), pltpu.VMEM((tt, SSM_CB), F32),
                        pltpu.VMEM((2 * SUBLANES, SSM_CB), F32)],
        compiler_params=_params(("parallel", "arbitrary")),
    )(dy, u, sre, sim, sre, sim, bre, bim, cre, cim, dvec, tab)


def _merge_fwd(x, gl, attn, y1, wba, wbs, wglu, bglu, wout, gpost, gpre, tl):
    l = x.shape[0]

    def body(x_ref, gl_ref, at_ref, y1_ref, wba_ref, wbs_ref, wglu_ref, bglu_ref, wout_ref, gpost_ref, gpre_ref,
             a_ref, sm_ref, mg_ref, z_ref, x1_ref, hn2_ref, y3_ref):
        y2 = _gelu(y1_ref[...])
        sg = _sigmoid(_dot(y2.astype(BF16), wglu_ref[...]) + bglu_ref[...])
        y3 = (y2 * sg).astype(BF16)
        y3_ref[...] = y3
        a = _dot(at_ref[...], wba_ref[...])
        sm = _dot(y3, wbs_ref[...])
        a_ref[...] = a
        sm_ref[...] = sm
        g = _sigmoid(gl_ref[...])
        merged = (g[:, :D_MODEL] * a + g[:, D_MODEL:] * sm).astype(BF16)
        mg_ref[...] = merged
        z = _dot(merged, wout_ref[...])
        z_ref[...] = z
        n, _ = _rms(z, gpost_ref[...])
        x1 = x_ref[...] + n
        x1_ref[...] = x1
        hn2, _ = _rms(x1, gpre_ref[...])
        hn2_ref[...] = hn2.astype(BF16)

    outs = [(D_MODEL, F32), (D_MODEL, F32), (D_MODEL, BF16), (D_MODEL, F32), (D_MODEL, F32), (D_MODEL, BF16),
            (SSM_W, BF16)]
    return pl.pallas_call(
        body, name="merge_fwd", grid=(l // tl,),
        in_specs=[_row(tl, D_MODEL), _row(tl, 2 * D_MODEL), _row(tl, HP), _row(tl, SSM_W),
                  _const((HP, D_MODEL)), _const((SSM_W, D_MODEL)), _const((SSM_W, SSM_W)), _const((1, SSM_W)),
                  _const((D_MODEL, D_MODEL)), _const((1, D_MODEL)), _const((1, D_MODEL))],
        out_specs=[_row(tl, n) for n, _ in outs],
        out_shape=[jax.ShapeDtypeStruct((l, n), dt) for n, dt in outs],
        compiler_params=_params(("parallel",)),
    )(x, gl, attn, y1, wba, wbs, wglu, bglu, wout, gpost, gpre)


def _merge_bwd(dhn2a, dhn2b, x1, dx2, z, gl, a, sm, y1, wba, wbs, wglu, bglu, wout, gpost, gpre, tl):
    l = x1.shape[0]

    def body(da_ref, db_ref, x1_ref, dx2_ref, z_ref, gl_ref, a_ref, sm_ref, y1_ref,
             wba_ref, wbs_ref, wglu_ref, bglu_ref, wout_ref, gpost_ref, gpre_ref,
             dx1_ref, dz_ref, dbra_ref, dbrs_ref, dgl_ref, dat_ref, dy1_ref, dt_ref, y2_ref,
             dgpre_ref, dgpost_ref, dbg_ref, dbglu_ref):
        @pl.when(pl.program_id(0) == 0)
        def _():
            for ref in (dgpre_ref, dgpost_ref, dbg_ref, dbglu_ref):
                ref[...] = jnp.zeros_like(ref)

        dhn2 = da_ref[...] + db_ref[...]
        dx1a, dgpre = _rms_bwd(dhn2, x1_ref[...], gpre_ref[...])
        dgpre_ref[...] += dgpre
        dx1 = dx2_ref[...] + dx1a
        dx1_ref[...] = dx1
        dz, dgpost = _rms_bwd(dx1, z_ref[...], gpost_ref[...])
        dgpost_ref[...] += dgpost
        dzb = dz.astype(BF16)
        dz_ref[...] = dzb
        dm = _dot_nt(dzb, wout_ref[...])
        g = _sigmoid(gl_ref[...])
        g0 = g[:, :D_MODEL]
        g1 = g[:, D_MODEL:]
        dbra = (dm * g0).astype(BF16)
        dbrs = (dm * g1).astype(BF16)
        dbra_ref[...] = dbra
        dbrs_ref[...] = dbrs
        dgl0 = dm * a_ref[...] * g0 * (1.0 - g0)
        dgl1 = dm * sm_ref[...] * g1 * (1.0 - g1)
        dgl_ref[:, :D_MODEL] = dgl0.astype(BF16)
        dgl_ref[:, D_MODEL:] = dgl1.astype(BF16)
        dbg_ref[:, :D_MODEL] += jnp.sum(dgl0, axis=0, keepdims=True)
        dbg_ref[:, D_MODEL:] += jnp.sum(dgl1, axis=0, keepdims=True)
        dat_ref[...] = _dot_nt(dbra, wba_ref[...]).astype(BF16)
        dy3 = _dot_nt(dbrs, wbs_ref[...])
        y1v = y1_ref[...]
        y2 = _gelu(y1v)
        y2b = y2.astype(BF16)
        y2_ref[...] = y2b
        sg = _sigmoid(_dot(y2b, wglu_ref[...]) + bglu_ref[...])
        dt = dy3 * y2 * sg * (1.0 - sg)
        dtb = dt.astype(BF16)
        dt_ref[...] = dtb
        dbglu_ref[...] += jnp.sum(dt, axis=0, keepdims=True)
        dy2 = dy3 * sg + _dot_nt(dtb, wglu_ref[...])
        dy1_ref[...] = dy2 * _gelu_grad(y1v)

    outs = [(D_MODEL, F32), (D_MODEL, BF16), (D_MODEL, BF16), (D_MODEL, BF16), (2 * D_MODEL, BF16), (HP, BF16),
            (SSM_W, F32), (SSM_W, BF16), (SSM_W, BF16)]
    accs = [D_MODEL, D_MODEL, 2 * D_MODEL, SSM_W]
    return pl.pallas_call(
        body, name="merge_bwd", grid=(l // tl,),
        in_specs=[_row(tl, D_MODEL), _row(tl, D_MODEL), _row(tl, D_MODEL), _row(tl, D_MODEL), _row(tl, D_MODEL),
                  _row(tl, 2 * D_MODEL), _row(tl, D_MODEL), _row(tl, D_MODEL), _row(tl, SSM_W),
                  _const((HP, D_MODEL)), _const((SSM_W, D_MODEL)), _const((SSM_W, SSM_W)), _const((1, SSM_W)),
                  _const((D_MODEL, D_MODEL)), _const((1, D_MODEL)), _const((1, D_MODEL))],
        out_specs=[_row(tl, n) for n, _ in outs] + [_const((1, n)) for n in accs],
        out_shape=[jax.ShapeDtypeStruct((l, n), dt) for n, dt in outs]
        + [jax.ShapeDtypeStruct((1, n), F32) for n in accs],
        compiler_params=_params(("arbitrary",)),
    )(dhn2a, dhn2b, x1, dx2, z, gl, a, sm, y1, wba, wbs, wglu, bglu, wout, gpost, gpre)


def _proj_bwd(x, dx1, cq, ckv, dq, dk, dv, du, dgl, g1, win, gq, wuq, gkv, wukv, rc, rs, tl):
    l = x.shape[0]

    def body(x_ref, dx1_ref, cq_ref, ckv_ref, dq_ref, dk_ref, dv_ref, du_ref, dgl_ref,
             g1_ref, win_ref, gq_ref, wuq_ref, gkv_ref, wukv_ref, rc_ref, rs_ref,
             gx_ref, dql_ref, qn_ref, ckvn_ref, dproj_ref, dg1_ref, dgq_ref, dgkv_ref):
        @pl.when(pl.program_id(0) == 0)
        def _():
            for ref in (dg1_ref, dgq_ref, dgkv_ref):
                ref[...] = jnp.zeros_like(ref)

        c1 = rc_ref[...]
        s1 = rs_ref[...]
        dql = _rope_bwd(dq_ref[...], jnp.tile(c1, (1, N_HEADS)), jnp.tile(s1, (1, N_HEADS))).astype(BF16)
        dql_ref[...] = dql
        dqn = _dot_nt(dql, wuq_ref[...])
        cq = cq_ref[...]
        qn, _ = _rms(cq, gq_ref[...])
        qn_ref[...] = qn.astype(BF16)
        dcq, dgq = _rms_bwd(dqn, cq, gq_ref[...])
        dgq_ref[...] += dgq
        dkb = dk_ref[...]
        dvb = dv_ref[...]
        dkf = dkb.astype(F32)
        dkr = dkf[:, 0:HEAD_SLOT]
        for h in range(1, N_HEADS):
            dkr = dkr + dkf[:, h * HEAD_SLOT:(h + 1) * HEAD_SLOT]
        dkr = _rope_bwd(dkr, c1, s1)
        dckvn = _dot_nt(dkb, wukv_ref[:, :HP]) + _dot_nt(dvb, wukv_ref[:, HP:])
        ckv = ckv_ref[...]
        ckvn, _ = _rms(ckv, gkv_ref[...])
        ckvn_ref[...] = ckvn.astype(BF16)
        dckv, dgkv = _rms_bwd(dckvn, ckv, gkv_ref[...])
        dgkv_ref[...] += dgkv
        dproj_ref[:, P_CQ:P_CKV] = dcq.astype(BF16)
        dproj_ref[:, P_CKV:P_KR] = dckv.astype(BF16)
        dproj_ref[:, P_KR:P_U] = dkr.astype(BF16)
        dproj_ref[:, P_U:P_GL] = du_ref[...].astype(BF16)
        dproj_ref[:, P_GL:P_END] = dgl_ref[...]
        dhn = _dot_nt(dproj_ref[...], win_ref[...])
        dxa, dg1 = _rms_bwd(dhn, x_ref[...], g1_ref[...])
        dg1_ref[...] += dg1
        gx_ref[...] = dx1_ref[...] + dxa

    outs = [(D_MODEL, F32), (HP, BF16), (Q_RANK, BF16), (KV_RANK, BF16), (P_END, BF16)]
    accs = [D_MODEL, Q_RANK, KV_RANK]
    return pl.pallas_call(
        body, name="proj_bwd", grid=(l // tl,),
        in_specs=[_row(tl, D_MODEL), _row(tl, D_MODEL), _row(tl, Q_RANK), _row(tl, KV_RANK), _row(tl, HP),
                  _row(tl, HP), _row(tl, HP), _row(tl, SSM_W), _row(tl, 2 * D_MODEL),
                  _const((1, D_MODEL)), _const((D_MODEL, P_END)), _const((1, Q_RANK)), _const((Q_RANK, HP)),
                  _const((1, KV_RANK)), _const((KV_RANK, 2 * HP)), _row(tl, HEAD_SLOT), _row(tl, HEAD_SLOT)],
        out_specs=[_row(tl, n) for n, _ in outs] + [_const((1, n)) for n in accs],
        out_shape=[jax.ShapeDtypeStruct((l, n), dt) for n, dt in outs]
        + [jax.ShapeDtypeStruct((1, n), F32) for n in accs],
        compiler_params=_params(("arbitrary",)),
    )(x, dx1, cq, ckv, dq, dk, dv, du, dgl, g1, win, gq, wuq, gkv, wukv, rc, rs)


CONV_CB = 256
CONV_NB = D_FF // CONV_CB
CONV_ROWS = 16


def _conv3(h, halo, w, b):
    return b + w[0:1, :] * _shift_down(h, 2, halo) + w[1:2, :] * _shift_down(h, 1, halo) + w[2:3, :] * h


def _conv_fwd(h, cw, cb, tl):
    l = h.shape[0]

    def body(hg_ref, hv_ref, wg_ref, wv_ref, bg_ref, bv_ref, act_ref, halo_ref):
        @pl.when(pl.program_id(1) == 0)
        def _():
            halo_ref[...] = jnp.zeros_like(halo_ref)

        hg = hg_ref[...]
        hv = hv_ref[...]
        cg = _conv3(hg, halo_ref[0:SUBLANES, :], wg_ref[...], bg_ref[...])
        cv = _conv3(hv, halo_ref[SUBLANES:, :], wv_ref[...], bv_ref[...])
        act_ref[...] = (_gelu(cg) * cv).astype(BF16)
        halo_ref[0:SUBLANES, :] = hg[tl - SUBLANES:, :]
        halo_ref[SUBLANES:, :] = hv[tl - SUBLANES:, :]

    gmap = lambda c, r: (r, c)
    vmap = lambda c, r: (r, CONV_NB + c)
    return pl.pallas_call(
        body, name="conv_fwd", grid=(CONV_NB, l // tl),
        in_specs=[pl.BlockSpec((tl, CONV_CB), gmap), pl.BlockSpec((tl, CONV_CB), vmap),
                  pl.BlockSpec((3, CONV_CB), lambda c, r: (0, c)), pl.BlockSpec((3, CONV_CB), lambda c, r: (0, CONV_NB + c)),
                  pl.BlockSpec((1, CONV_CB), lambda c, r: (0, c)), pl.BlockSpec((1, CONV_CB), lambda c, r: (0, CONV_NB + c))],
        out_specs=pl.BlockSpec((tl, CONV_CB), gmap),
        out_shape=jax.ShapeDtypeStruct((l, D_FF), BF16),
        scratch_shapes=[pltpu.VMEM((2 * SUBLANES, CONV_CB), F32)],
        compiler_params=_params(("parallel", "arbitrary")),
    )(h, h, cw, cw, cb, cb)


def _conv_bwd(h, dact, cw, cb, tl):
    l = h.shape[0]
    nr = l // tl
    tpb = tl // SUBLANES

    def body(hg_ref, hv_ref, hgh_ref, hvh_ref, da_ref, wg_ref, wv_ref, bg_ref, bv_ref,
             dh_ref, dwg_ref, dwv_ref, dbg_ref, dbv_ref, car_ref):
        r = pl.program_id(1)

        @pl.when(r == 0)
        def _():
            for ref in (car_ref, dwg_ref, dwv_ref, dbg_ref, dbv_ref):
                ref[...] = jnp.zeros_like(ref)

        keep = jnp.where(r == nr - 1, 0.0, 1.0)
        wg, wv, bg, bv = wg_ref[...], wv_ref[...], bg_ref[...], bv_ref[...]
        nch = tl // CONV_ROWS

        def fold(x):
            s = x[0:SUBLANES, :]
            for k in range(1, CONV_ROWS // SUBLANES):
                s = s + x[k * SUBLANES:(k + 1) * SUBLANES, :]
            return s

        def chunk(n, carry):
            ncg, ncv, acc = carry
            idx = nch - 1 - n
            r0 = pl.multiple_of(idx * CONV_ROWS, CONV_ROWS)
            rows = pl.ds(r0, CONV_ROWS)
            before = pl.ds(pl.multiple_of(jnp.maximum(r0 - SUBLANES, 0), SUBLANES), SUBLANES)
            in_tile = idx > 0
            da = da_ref[rows, :].astype(F32)

            def half(h_ref, halo_ref, w, b):
                hh = h_ref[rows, :]
                prev = jnp.where(in_tile, h_ref[before, :], halo_ref[...] * keep)
                h1 = _shift_down(hh, 1, prev)
                h2 = _shift_down(hh, 2, prev)
                return hh, h1, h2, b + w[0:1, :] * h2 + w[1:2, :] * h1 + w[2:3, :] * hh

            hg, hg1, hg2, cg = half(hg_ref, hgh_ref, wg, bg)
            hv, hv1, hv2, cv = half(hv_ref, hvh_ref, wv, bv)
            dcg = da * cv * _gelu_grad(cg)
            dcv = da * _gelu(cg)

            def back(dc, hh, h1, h2, w, nxt, part):
                dh = w[2:3, :] * dc + w[1:2, :] * _shift_up(dc, 1, nxt) + w[0:1, :] * _shift_up(dc, 2, nxt)
                dh_ref[part, rows, :] = dh.astype(BF16)
                return [fold(dc * h2), fold(dc * h1), fold(dc * hh), fold(dc)]

            sums = back(dcg, hg, hg1, hg2, wg, ncg, 0) + back(dcv, hv, hv1, hv2, wv, ncv, 1)
            return dcg[0:SUBLANES, :], dcv[0:SUBLANES, :], [a + s for a, s in zip(acc, sums)]

        zero = jnp.zeros((SUBLANES, CONV_CB), F32)
        ncg, ncv, acc = lax.fori_loop(0, nch, chunk, (car_ref[0:SUBLANES, :], car_ref[SUBLANES:, :], [zero] * 8))
        car_ref[0:SUBLANES, :] = ncg
        car_ref[SUBLANES:, :] = ncv
        for half_acc, dw_ref, db_ref in ((acc[0:4], dwg_ref, dbg_ref), (acc[4:8], dwv_ref, dbv_ref)):
            for k in range(3):
                dw_ref[k:k + 1, :] += jnp.sum(half_acc[k], axis=0, keepdims=True)
            db_ref[...] += jnp.sum(half_acc[3], axis=0, keepdims=True)

    grev = lambda c, r: (nr - 1 - r, c)
    vrev = lambda c, r: (nr - 1 - r, CONV_NB + c)
    ghalo = lambda c, r: (jnp.maximum((nr - 1 - r) * tpb - 1, 0), c)
    vhalo = lambda c, r: (jnp.maximum((nr - 1 - r) * tpb - 1, 0), CONV_NB + c)
    colg = lambda c, r: (0, c)
    colv = lambda c, r: (0, CONV_NB + c)
    return pl.pallas_call(
        body, name="conv_bwd", grid=(CONV_NB, nr),
        in_specs=[pl.BlockSpec((tl, CONV_CB), grev), pl.BlockSpec((tl, CONV_CB), vrev),
                  pl.BlockSpec((SUBLANES, CONV_CB), ghalo), pl.BlockSpec((SUBLANES, CONV_CB), vhalo),
                  pl.BlockSpec((tl, CONV_CB), grev),
                  pl.BlockSpec((3, CONV_CB), colg), pl.BlockSpec((3, CONV_CB), colv),
                  pl.BlockSpec((1, CONV_CB), colg), pl.BlockSpec((1, CONV_CB), colv)],
        out_specs=[pl.BlockSpec((2, tl, CONV_CB), lambda c, r: (0, nr - 1 - r, c)),
                   pl.BlockSpec((3, CONV_CB), colg), pl.BlockSpec((3, CONV_CB), colg),
                   pl.BlockSpec((1, CONV_CB), colg), pl.BlockSpec((1, CONV_CB), colg)],
        out_shape=[jax.ShapeDtypeStruct((2, l, D_FF), BF16),
                   jax.ShapeDtypeStruct((3, D_FF), F32), jax.ShapeDtypeStruct((3, D_FF), F32),
                   jax.ShapeDtypeStruct((1, D_FF), F32), jax.ShapeDtypeStruct((1, D_FF), F32)],
        scratch_shapes=[pltpu.VMEM((2 * SUBLANES, CONV_CB), F32)],
        compiler_params=_params(("parallel", "arbitrary")),
    )(h, h, h, h, dact, cw, cw, cb, cb)


def _loss_head(ff, x1, tgt, g, tl):
    l = ff.shape[0]

    def body(ff_ref, x1_ref, tg_ref, g_ref, loss_ref, dx2_ref, dff_ref, dg_ref):
        @pl.when(pl.program_id(0) == 0)
        def _():
            loss_ref[...] = jnp.zeros_like(loss_ref)
            dg_ref[...] = jnp.zeros_like(dg_ref)

        f = ff_ref[...]
        gv = g_ref[...]
        n, _ = _rms(f, gv)
        e = x1_ref[...] + n - tg_ref[...]
        loss_ref[...] += 0.5 * jnp.sum(jnp.mean(e * e, axis=-1, keepdims=True), axis=0, keepdims=True)
        dx2 = e * (1.0 / D_MODEL)
        dx2_ref[...] = dx2
        dff, dg = _rms_bwd(dx2, f, gv)
        dff_ref[...] = dff.astype(BF16)
        dg_ref[...] += dg

    return pl.pallas_call(
        body, name="loss_head", grid=(l // tl,),
        in_specs=[_row(tl, D_MODEL), _row(tl, D_MODEL), _row(tl, D_MODEL), _const((1, D_MODEL))],
        out_specs=[_const((1, LANES)), _row(tl, D_MODEL), _row(tl, D_MODEL), _const((1, D_MODEL))],
        out_shape=[jax.ShapeDtypeStruct((1, LANES), F32), jax.ShapeDtypeStruct((l, D_MODEL), F32),
                   jax.ShapeDtypeStruct((l, D_MODEL), BF16), jax.ShapeDtypeStruct((1, D_MODEL), F32)],
        compiler_params=_params(("arbitrary",)),
    )(ff, x1, tgt, g)


def _ssm_disc(lam_re, lam_im, log_dt, b_re, b_im):
    dt = jnp.exp(log_dt)[:, None]
    mag = jnp.exp(lam_re * dt)
    ang = lam_im * dt
    a_re, a_im = mag * jnp.cos(ang), mag * jnp.sin(ang)
    den = lam_re * lam_re + lam_im * lam_im
    n_re, n_im = a_re - 1.0, a_im
    z_re = (n_re * lam_re + n_im * lam_im) / den
    z_im = (n_im * lam_re - n_re * lam_im) / den
    bb_re = z_re[..., None] * b_re - z_im[..., None] * b_im
    bb_im = z_re[..., None] * b_im + z_im[..., None] * b_re
    return a_re, a_im, bb_re, bb_im


_GPB = SSM_CB // SSM_P


def _embed_b(bb):
    t = bb.transpose(0, 2, 1).reshape(SSM_NB, _GPB, SSM_H, SSM_P)
    return jnp.einsum('mjhp,jk->mjhkp', t, jnp.eye(_GPB, dtype=bb.dtype)).reshape(SSM_NB, SSM_UB, SSM_CB)


def _extract_b(d):
    t = d.reshape(SSM_NB, _GPB, SSM_H, _GPB, SSM_P)
    t = jnp.einsum('mjhkp,jk->mjhp', t, jnp.eye(_GPB, dtype=d.dtype))
    return t.reshape(SSM_G, SSM_H, SSM_P).transpose(0, 2, 1)


def _embed_c(c):
    t = c.transpose(0, 2, 1).reshape(SSM_NB, _GPB, SSM_P, SSM_H)
    return jnp.einsum('mjph,jk->mjpkh', t, jnp.eye(_GPB, dtype=c.dtype)).reshape(SSM_NB, SSM_CB, SSM_UB)


def _extract_c(d):
    t = d.reshape(SSM_NB, _GPB, SSM_P, _GPB, SSM_H)
    t = jnp.einsum('mjpkh,jk->mjph', t, jnp.eye(_GPB, dtype=d.dtype))
    return t.reshape(SSM_G, SSM_P, SSM_H).transpose(0, 2, 1)


def _scan_tables(a_re, a_im, reverse):
    ar = a_re.reshape(1, SSM_CH)
    ai = (-a_im if reverse else a_im).reshape(1, SSM_CH)
    pr, pi = [ar], [ai]
    for _ in range(SUBLANES - 1):
        pr, pi = pr + [pr[-1] * ar - pi[-1] * ai], pi + [pr[-1] * ai + pi[-1] * ar]
    rows = jnp.arange(SUBLANES)[:, None]
    out = []
    for k in (1, 2, 4):
        valid = (rows + k <= SUBLANES - 1) if reverse else (rows >= k)
        out += [jnp.where(valid, pr[k - 1], 0.0), jnp.where(valid, pi[k - 1], 0.0)]
    order = list(range(SUBLANES - 1, -1, -1)) if reverse else list(range(SUBLANES))
    out += [jnp.concatenate([pr[n] for n in order], axis=0), jnp.concatenate([pi[n] for n in order], axis=0)]
    return jnp.stack(out).astype(F32)


def _pad_heads(w, d):
    lead = w.shape[:-1]
    w = w.reshape(lead + (N_HEADS, d))
    w = jnp.pad(w, [(0, 0)] * len(lead) + [(0, 0), (0, HEAD_SLOT - d)])
    return w.reshape(lead + (HP,))


def _unpad_heads(w, d):
    lead = w.shape[:-1]
    return w.reshape(lead + (N_HEADS, HEAD_SLOT))[..., :d].reshape(lead + (N_HEADS * d,))


def _chip_major(w, axis):
    k, n = w.shape
    if axis == 0:
        return w.reshape(N_CHIPS, k // N_CHIPS, n)
    return w.reshape(k, N_CHIPS, n // N_CHIPS).transpose(1, 0, 2)


def _from_chip_major(w, axis):
    if axis == 0:
        return w.reshape(-1, w.shape[2])
    return w.transpose(1, 0, 2).reshape(w.shape[1], -1)


def _pad_w_in(w):
    z = lambda n: jnp.zeros((w.shape[0], n), w.dtype)
    return jnp.concatenate([w[:, :640], z(KR_LANE), w[:, 640:672], z(HEAD_SLOT - KR_LANE - QK_ROPE), w[:, 672:]], axis=1)


def _unpad_w_in(w):
    return jnp.concatenate([w[:, :640], w[:, P_KR + KR_LANE:P_KR + KR_LANE + QK_ROPE], w[:, P_U:]], axis=1)


def _local_step(x, positions, tgt, wts, sp):
    l = x.shape[0]
    tl = min(256, l)
    ta = min(512, l)

    inv_freq = ROPE_THETA ** (-jnp.arange(0, QK_ROPE, 2, dtype=F32) / QK_ROPE)
    ang = positions.astype(F32)[:, None] * inv_freq
    cos, sin = jnp.cos(ang), jnp.sin(ang)
    one = jnp.ones((l, KR_LANE), F32)
    rc = jnp.concatenate([one, cos, cos, jnp.ones((l, HEAD_SLOT - KR_LANE - QK_ROPE), F32)], axis=1)
    rs = jnp.concatenate([0 * one, -sin, sin, jnp.zeros((l, HEAD_SLOT - KR_LANE - QK_ROPE), F32)], axis=1)

    win = _pad_w_in(wts["w_in"])
    wuq = _pad_heads(wts["w_uq"], QK_HEAD)
    wukv = jnp.concatenate([_pad_heads(wts["w_uk"], QK_NOPE), _pad_heads(wts["w_uv"], V_HEAD)], axis=1)

    disc_in = (sp["ssm_lambda_re"], sp["ssm_lambda_im"], sp["ssm_log_dt"], sp["ssm_b_re"], sp["ssm_b_im"])
    (a_re, a_im, bb_re, bb_im), disc_vjp = jax.vjp(_ssm_disc, *disc_in)
    bre, bim = _embed_b(bb_re).astype(BF16), _embed_b(bb_im).astype(BF16)
    cre, cim = _embed_c(sp["ssm_c_re"]).astype(BF16), _embed_c(sp["ssm_c_im"]).astype(BF16)
    dvec = sp["ssm_d"].reshape(1, SSM_W)
    tab_f = _scan_tables(a_re, a_im, False)
    tab_r = _scan_tables(a_re, a_im, True)

    g1, gq, gkv = sp["mix_norm_pre"], sp["q_norm"], sp["kv_norm"]
    gpost, gpre, gfin = sp["mix_norm_post"], sp["ffn_norm_pre"], sp["ffn_norm_post"]
    bgate, bglu, convb = sp["b_gate"], sp["b_glu"], sp["conv_b"]

    hn, cq, ckv, q, k, v, u, gl = _proj_fwd(x, g1, win, gq, wuq, gkv, wukv, rc, rs, bgate, tl)
    attn, lse = _attn_fwd(q, k, v, ta)
    y1, sre, sim = _ssm_fwd(u, bre, bim, cre, cim, dvec, tab_f, ta)
    wba = jnp.pad(wts["w_branch_attn"].reshape(N_HEADS, V_HEAD, D_MODEL),
                  ((0, 0), (0, HEAD_SLOT - V_HEAD), (0, 0))).reshape(HP, D_MODEL)
    wbs, wglu, wout = wts["w_branch_ssm"], wts["w_glu"], wts["w_out"]
    a, sm, merged, z, x1, hn2, y3 = _merge_fwd(x, gl, attn, y1, wba, wbs, wglu, bglu, wout, gpost, gpre, tl)
    late = wts["late"](x1)
    wup, wdown, convw = late["w_up"], late["w_down"], late["conv_w"]
    h = _mm(hn2, wup, "ffn_up")
    act = _conv_fwd(h, convw, convb, ta)
    ff = _mm(act, wdown, "ffn_down")
    loss, dx2, dff, dgfin = _loss_head(ff, x1, tgt, gfin, tl)

    dact = _mm(dff, wdown, "ffn_down_dx", out_dtype=BF16, bt=True)
    d_wdown = _mm_tn(act, dff, "ffn_down_dw", tk_cap=D_FF // 2)
    dh, dwg, dwv, dbg, dbv = _conv_bwd(h, dact, convw, convb, ta)
    d_convw = jnp.concatenate([dwg, dwv], axis=1)
    d_convb = jnp.concatenate([dbg, dbv], axis=1)
    dhn2a = _mm(dh, wup, "ffn_up_dx_gate", bt=True, b_col0=0, a_lead=0)
    dhn2b = _mm(dh, wup, "ffn_up_dx_val", bt=True, b_col0=1, a_lead=1)
    d_wup = _mm_tn(hn2, dh, "ffn_up_dw", chips=True)
    behind = wts["send_grads"]("ffn", {"w_up": d_wup, "w_down": _chip_major(d_wdown, 0)})
    (dx1, dz, dbra, dbrs, dgl, dattn, dy1, dt, y2, dgpre, dgpost, dbgate, dbglu) = _merge_bwd(
        dhn2a, dhn2b, x1, dx2, z, gl, a, sm, y1, wba, wbs, wglu, bglu, wout, gpost, gpre + behind, tl)
    d_wout = _mm_tn(merged, dz, "w_out_dw")
    d_wba = _mm_tn(attn, dbra, "w_branch_attn_dw", chips=True)
    d_wbs = _mm_tn(y3, dbrs, "w_branch_ssm_dw", chips=True)
    d_wglu = _mm_tn(y2, dt, "w_glu_dw")
    ncol = D_MODEL // N_CHIPS
    behind = wts["send_grads"]("mix", {
        "w_glu": _chip_major(d_wglu, 0),
        "w_branch_attn": d_wba.reshape(N_CHIPS, N_HEADS, HEAD_SLOT, ncol)[:, :, :V_HEAD].reshape(
            N_CHIPS, N_HEADS * V_HEAD, ncol),
        "w_branch_ssm": d_wbs,
        "w_out": _chip_major(d_wout, 0)}, after=d_wglu)
    dq, dk, dv = _attn_bwd(q, k, v, dattn, lse + behind, _attn_delta(attn, dattn, min(2048, l)), ta)
    du, dbre, dbim, dcre, dcim, dare, daim, dd = _ssm_bwd(dy1, u, sre, sim, bre, bim, cre, cim, dvec, tab_r, ta)
    behind = wts["send_grads"]("none", {}, after=du)
    gx, dql, qn, ckvn, dproj, dg1, dgq, dgkv = _proj_bwd(
        x, dx1, cq, ckv, dq, dk, dv, du, dgl, g1 + behind, win, gq, wuq, gkv, wukv, rc, rs, tl)
    d_win = _mm_tn(hn, dproj, "w_in_dw")
    d_wuq = _mm_tn(qn, dql, "w_uq_dw")
    d_wuk = _mm_tn(ckvn, dk, "w_uk_dw")
    d_wuv = _mm_tn(ckvn, dv, "w_uv_dw")

    d_lre, d_lim, d_ldt, d_bre, d_bim = disc_vjp((dare.reshape(SSM_G, SSM_P), daim.reshape(SSM_G, SSM_P),
                                                  _extract_b(dbre), _extract_b(dbim)))
    big = {
        "w_in": _chip_major(_unpad_w_in(d_win), 1),
        "w_uq": _chip_major(_unpad_heads(d_wuq, QK_HEAD), 1),
        "w_uk": _chip_major(_unpad_heads(d_wuk, QK_NOPE), 1),
        "w_uv": _chip_major(_unpad_heads(d_wuv, V_HEAD), 1),
    }
    small = {
        "conv_w": d_convw,
        "mix_norm_pre": dg1, "q_norm": dgq, "kv_norm": dgkv,
        "ssm_lambda_re": d_lre, "ssm_lambda_im": d_lim, "ssm_log_dt": d_ldt,
        "ssm_b_re": d_bre, "ssm_b_im": d_bim,
        "ssm_c_re": _extract_c(dcre), "ssm_c_im": _extract_c(dcim),
        "ssm_d": dd.reshape(SSM_G, SSM_H), "b_glu": dbglu, "b_gate": dbgate,
        "mix_norm_post": dgpost, "ffn_norm_pre": dgpre, "conv_b": d_convb, "ffn_norm_post": dgfin,
    }
    return loss[0, 0], gx, big, small


_ANY = pl.BlockSpec(memory_space=pl.ANY)


ROW_TILE = 16


def _place():
    x, y, c = lax.axis_index("x"), lax.axis_index("y"), lax.axis_index("c")
    return x, y, c, 2 * x + y, [(1 - x, y), (x, 1 - y), (1 - x, 1 - y)]


def _half(rows, which):
    hr = rows // 2
    return pl.ds(pl.multiple_of(which * hr, ROW_TILE), hr)


def _remote(src, dst, send_sems, recv_sems, n, dev):
    return pltpu.make_async_remote_copy(src_ref=src, dst_ref=dst, send_sem=send_sems.at[n], recv_sem=recv_sems.at[n],
                                        device_id=dev, device_id_type=MESH)


def _gather_big(shards):
    nw = len(shards)
    rows = [s.shape[0] for s in shards]

    def body(*refs):
        ins, outs = refs[:nw], refs[nw:2 * nw]
        ici_send, ici_recv, d2d_send, d2d_recv = refs[2 * nw:]
        x, y, c, me, peers = _place()
        sent = []
        for i in range(nw):
            for p, (px, py) in enumerate(peers):
                cp = _remote(ins[i].at[_half(rows[i], c)], outs[i].at[me, _half(rows[i], c)], ici_send, ici_recv,
                             3 * i + p, (px, py, c))
                cp.start()
                sent.append(cp)
        for p, (px, py) in enumerate(peers):
            for i in range(nw):
                blk = outs[i].at[2 * px + py, _half(rows[i], c)]
                _remote(blk, blk, ici_send, ici_recv, 3 * i + p, (px, py, c)).wait_recv()
                cp = _remote(blk, blk, d2d_send, d2d_recv, 3 * i + p, (x, y, 1 - c))
                cp.start()
                sent.append(cp)
        for p, (px, py) in enumerate(peers):
            for i in range(nw):
                blk = outs[i].at[2 * px + py, _half(rows[i], 1 - c)]
                _remote(blk, blk, d2d_send, d2d_recv, 3 * i + p, (x, y, 1 - c)).wait_recv()
        for cp in sent:
            cp.wait_send()

    dma = pltpu.SemaphoreType.DMA
    return pl.pallas_call(
        body, name="gather_weights", in_specs=[_ANY] * nw, out_specs=[_ANY] * nw,
        out_shape=[jax.ShapeDtypeStruct((N_CHIPS,) + s.shape, s.dtype) for s in shards],
        scratch_shapes=[dma((3 * nw,)), dma((3 * nw,)), dma((3 * nw,)), dma((3 * nw,))],
    )(*shards)


_HBM = pl.BlockSpec(memory_space=pltpu.HBM)
_SEM = pl.BlockSpec(memory_space=pltpu.SEMAPHORE)
_DATAFLOW = pltpu.SideEffectType.DATAFLOW_SIDE_EFFECTING


def _exchange_start(shards, name, scatter):
    nw = len(shards)
    lands = [lax.empty(s.shape if scatter else (N_CHIPS,) + s.shape, s.dtype) for s in shards]

    def body(*refs):
        ins, zones = refs[:nw], refs[nw:2 * nw]
        send_sems, recv_sems, token = refs[2 * nw], refs[2 * nw + 1], refs[-1]
        x, y, c, me, peers = _place()
        for i in range(nw):
            for p, (px, py) in enumerate(peers):
                src = ins[i].at[2 * px + py] if scatter else ins[i]
                _remote(src, zones[i].at[me], send_sems, recv_sems, 3 * i + p, (px, py, c)).start()
        token[...] = jnp.zeros_like(token)

    thru = [pltpu.HBM(a.shape, a.dtype) for a in list(shards) + lands]
    dma = pltpu.SemaphoreType.DMA
    outs = pl.pallas_call(
        body, name=name,
        out_shape=(dma((3 * nw,)), dma((3 * nw,)), *thru, jax.ShapeDtypeStruct((SUBLANES, LANES), F32)),
        in_specs=[_HBM] * (2 * nw),
        out_specs=(_SEM, _SEM, *([_HBM] * (2 * nw)), pl.BlockSpec(memory_space=pltpu.VMEM)),
        input_output_aliases={i: 2 + i for i in range(2 * nw)},
        compiler_params=pltpu.CompilerParams(has_side_effects=_DATAFLOW),
    )(*[pltpu.with_memory_space_constraint(a, pltpu.HBM) for a in list(shards) + lands])
    return outs[0], outs[1], list(outs[2:2 + nw]), list(outs[2 + nw:2 + 2 * nw]), outs[-1]


def _exchange_wait(send_sems, recv_sems, shards, lands, after, name, scatter):
    nw = len(shards)

    def body(*refs):
        ins, zones = refs[:nw], refs[nw:2 * nw]
        send_sems, recv_sems = refs[2 * nw], refs[2 * nw + 1]
        x, y, c, me, peers = _place()
        for i in range(nw):
            for p, (px, py) in enumerate(peers):
                src = ins[i].at[2 * px + py] if scatter else ins[i]
                cp = _remote(src, zones[i].at[2 * px + py], send_sems, recv_sems, 3 * i + p, (px, py, c))
                cp.wait_send()
                cp.wait_recv()

    both = list(shards) + list(lands)
    outs = pl.pallas_call(
        body, name=name,
        out_shape=tuple(pltpu.HBM(a.shape, a.dtype) for a in both),
        in_specs=(*([_HBM] * (2 * nw)), _SEM, _SEM, _ANY), out_specs=[_HBM] * (2 * nw),
        input_output_aliases={i: i for i in range(2 * nw)},
        compiler_params=pltpu.CompilerParams(has_side_effects=_DATAFLOW),
    )(*both, send_sems, recv_sems, after)
    return list(outs[:nw]), list(outs[nw:])


def _sibling_start(grads, name):
    nw = len(grads)
    lands = [lax.empty((N_CHIPS, g.shape[1] // 2, g.shape[2]), g.dtype) for g in grads]

    def body(*refs):
        ins, zones = refs[:nw], refs[nw:2 * nw]
        send_sems, recv_sems, token = refs[2 * nw], refs[2 * nw + 1], refs[-1]
        x, y, c, _, _ = _place()
        for i in range(nw):
            _remote(ins[i].at[pl.ds(0, N_CHIPS), _half(grads[i].shape[1], 1 - c)], zones[i], send_sems, recv_sems,
                    i, (x, y, 1 - c)).start()
        token[...] = jnp.zeros_like(token)

    thru = [pltpu.HBM(a.shape, a.dtype) for a in list(grads) + lands]
    dma = pltpu.SemaphoreType.DMA
    outs = pl.pallas_call(
        body, name=name,
        out_shape=(dma((nw,)), dma((nw,)), *thru, jax.ShapeDtypeStruct((SUBLANES, LANES), F32)),
        in_specs=[_HBM] * (2 * nw),
        out_specs=(_SEM, _SEM, *([_HBM] * (2 * nw)), pl.BlockSpec(memory_space=pltpu.VMEM)),
        input_output_aliases={i: 2 + i for i in range(2 * nw)},
        compiler_params=pltpu.CompilerParams(has_side_effects=_DATAFLOW),
    )(*[pltpu.with_memory_space_constraint(a, pltpu.HBM) for a in list(grads) + lands])
    return outs[0], outs[1], list(outs[2:2 + nw]), list(outs[2 + nw:2 + 2 * nw]), outs[-1]


def _sibling_wait(send_sems, recv_sems, grads, lands, after, name):
    nw = len(grads)

    def body(*refs):
        ins, zones = refs[:nw], refs[nw:2 * nw]
        send_sems, recv_sems = refs[2 * nw], refs[2 * nw + 1]
        x, y, c, _, _ = _place()
        for i in range(nw):
            cp = _remote(ins[i].at[pl.ds(0, N_CHIPS), _half(grads[i].shape[1], 1 - c)], zones[i], send_sems, recv_sems,
                         i, (x, y, 1 - c))
            cp.wait_send()
            cp.wait_recv()

    both = list(grads) + list(lands)
    outs = pl.pallas_call(
        body, name=name,
        out_shape=tuple(pltpu.HBM(a.shape, a.dtype) for a in both),
        in_specs=(*([_HBM] * (2 * nw)), _SEM, _SEM, _ANY), out_specs=[_HBM] * (2 * nw),
        input_output_aliases={i: i for i in range(2 * nw)},
        compiler_params=pltpu.CompilerParams(has_side_effects=_DATAFLOW),
    )(*both, send_sems, recv_sems, after)
    return list(outs[:nw]), list(outs[nw:])


def _reduce_to_sibling(grads, name):
    nw = len(grads)

    def body(*refs):
        ins, outs = refs[:nw], refs[nw:2 * nw]
        send_sems, recv_sems = refs[2 * nw:]
        x, y, c, _, _ = _place()
        sent = []
        for i in range(nw):
            cp = _remote(ins[i].at[pl.ds(0, N_CHIPS), _half(grads[i].shape[1], 1 - c)], outs[i], send_sems, recv_sems,
                         i, (x, y, 1 - c))
            cp.start()
            sent.append(cp)
        for cp in sent:
            cp.wait()

    dma = pltpu.SemaphoreType.DMA
    return pl.pallas_call(
        body, name=name, in_specs=[_ANY] * nw, out_specs=[_ANY] * nw,
        out_shape=[jax.ShapeDtypeStruct((N_CHIPS, g.shape[1] // 2, g.shape[2]), g.dtype) for g in grads],
        scratch_shapes=[dma((nw,)), dma((nw,))],
    )(*grads)


def _reduce_back(totals, name):
    nw = len(totals)

    def body(*refs):
        outs = refs[nw:2 * nw]
        send_sems, recv_sems = refs[2 * nw:]
        x, y, c, _, _ = _place()
        sent = []
        for i in range(nw):
            blk = outs[i].at[_half(totals[i].shape[0], c)]
            cp = _remote(blk, blk, send_sems, recv_sems, i, (x, y, 1 - c))
            cp.start()
            sent.append(cp)
        for i in range(nw):
            blk = outs[i].at[_half(totals[i].shape[0], 1 - c)]
            _remote(blk, blk, send_sems, recv_sems, i, (x, y, 1 - c)).wait_recv()
        for cp in sent:
            cp.wait_send()

    dma = pltpu.SemaphoreType.DMA
    return pl.pallas_call(
        body, name=name, in_specs=[_ANY] * nw, out_specs=[_ANY] * nw,
        out_shape=[jax.ShapeDtypeStruct(t.shape, t.dtype) for t in totals],
        input_output_aliases={i: i for i in range(nw)},
        scratch_shapes=[dma((nw,)), dma((nw,))],
    )(*totals)


def _all_reduce_small(v, name):
    rows, w = v.shape
    hr = rows // 2
    assert hr % SUBLANES == 0

    def body(v_ref, out_ref, sib_ref, half_ref, chips_ref, send_sems, recv_sems):
        x, y, c, me, peers = _place()
        sibling = (x, y, 1 - c)
        mine = pl.ds(pl.multiple_of(c * hr, SUBLANES), hr)
        other = pl.ds(pl.multiple_of((1 - c) * hr, SUBLANES), hr)
        cp = _remote(v_ref, sib_ref, send_sems, recv_sems, 0, sibling)
        cp.start()
        cp.wait()
        half_ref[...] = v_ref[mine, :] + sib_ref[mine, :]
        sent = []
        for p, (px, py) in enumerate(peers):
            cp = _remote(half_ref, chips_ref.at[me], send_sems, recv_sems, 1 + p, (px, py, c))
            cp.start()
            sent.append(cp)
        chips_ref[me] = half_ref[...]
        for p, (px, py) in enumerate(peers):
            _remote(half_ref, chips_ref.at[2 * px + py], send_sems, recv_sems, 1 + p, (px, py, c)).wait_recv()
        for cp in sent:
            cp.wait_send()
        out_ref[mine, :] = ((chips_ref[0] + chips_ref[1]) + chips_ref[2]) + chips_ref[3]
        cp = _remote(out_ref.at[mine], out_ref.at[mine], send_sems, recv_sems, 4, sibling)
        cp.start()
        _remote(out_ref.at[other], out_ref.at[other], send_sems, recv_sems, 4, sibling).wait_recv()
        cp.wait_send()

    vm = pl.BlockSpec(memory_space=pltpu.VMEM)
    return pl.pallas_call(
        body, name=name, in_specs=[vm], out_specs=vm,
        out_shape=jax.ShapeDtypeStruct((rows, w), F32),
        scratch_shapes=[pltpu.VMEM((rows, w), F32), pltpu.VMEM((hr, w), F32), pltpu.VMEM((N_CHIPS, hr, w), F32),
                        pltpu.SemaphoreType.DMA((5,)), pltpu.SemaphoreType.DMA((5,))],
        compiler_params=pltpu.CompilerParams(vmem_limit_bytes=VMEM_LIMIT),
    )(v)


ELEMENTWISE_BLOCK = 512 * 1024


def _rows_tile(rows, cols):
    best = None
    for t in range(SUBLANES, rows + 1, SUBLANES):
        if rows % t == 0 and t * cols <= ELEMENTWISE_BLOCK:
            best = t
    return rows if best is None else best


def _add_pair(g, t, core, name):
    nb, n, w = t.shape
    tr = _rows_tile(n, w)
    steps = n // tr

    def body(core_ref, g_ref, t_ref, o_ref):
        o_ref[...] = (g_ref[...] + t_ref[...]).astype(BF16)

    spec = pl.BlockSpec((1, tr, w), lambda j, i, core_ref: (j, i, 0))
    return pl.pallas_call(
        body, name=name,
        grid_spec=pltpu.PrefetchScalarGridSpec(
            num_scalar_prefetch=1, grid=(nb, steps),
            in_specs=[pl.BlockSpec((1, tr, w), lambda j, i, core_ref: (j, core_ref[0] * steps + i, 0)), spec],
            out_specs=spec),
        out_shape=jax.ShapeDtypeStruct(t.shape, BF16),
        compiler_params=_params(("parallel", "parallel")))(core, g, t)


def _add_chips(landed, pairs, place, name):
    nb, n, w = landed.shape
    tr = _rows_tile(n, w)
    steps = n // tr

    def body(place_ref, r_ref, own_ref, o_ref):
        me = place_ref[0]
        acc = None
        for k in range(nb):
            blk = jnp.where(me == k, own_ref[0], r_ref[k]).astype(F32)
            acc = blk if acc is None else acc + blk
        o_ref[...] = acc

    return pl.pallas_call(
        body, name=name,
        grid_spec=pltpu.PrefetchScalarGridSpec(
            num_scalar_prefetch=1, grid=(steps,),
            in_specs=[pl.BlockSpec((nb, tr, w), lambda i, place_ref: (0, i, 0)),
                      pl.BlockSpec((1, tr, w), lambda i, place_ref: (place_ref[0], i, 0))],
            out_specs=pl.BlockSpec((tr, w), lambda i, place_ref: (place_ref[1] * steps + i, 0))),
        out_shape=jax.ShapeDtypeStruct((2 * n, w), F32),
        compiler_params=_params(("parallel",)))(place, landed, pairs)


def _adamw(w, g, m, v, name):
    rows, wd = w.shape
    tr = _rows_tile(rows, wd)
    c1 = 1.0 - ADAM_B1 ** ADAM_STEP
    c2 = 1.0 - ADAM_B2 ** ADAM_STEP

    def body(w_ref, g_ref, m_ref, v_ref, go_ref, d_ref, mo_ref, vo_ref):
        gv = g_ref[...]
        go_ref[...] = gv
        m2 = ADAM_B1 * m_ref[...] + (1.0 - ADAM_B1) * gv
        v2 = ADAM_B2 * v_ref[...] + (1.0 - ADAM_B2) * (gv * gv)
        mo_ref[...] = m2
        vo_ref[...] = v2
        d_ref[...] = -ADAM_LR * ((m2 / c1) / (jnp.sqrt(v2 / c2) + ADAM_EPS) + ADAM_WD * w_ref[...])

    spec = pl.BlockSpec((tr, wd), lambda i: (i, 0))
    shp = jax.ShapeDtypeStruct((rows, wd), F32)
    return pl.pallas_call(body, name=name, grid=(rows // tr,), in_specs=[spec] * 4, out_specs=[spec] * 4,
                          out_shape=[shp] * 4, compiler_params=_params(("parallel",)))(w, g, m, v)


BIG = [("w_in", (1024, 3232), 1), ("w_uq", (384, 768), 1), ("w_uk", (256, 512), 1), ("w_uv", (256, 512), 1),
       ("w_glu", (512, 512), 0), ("w_branch_attn", (512, 1024), 1), ("w_branch_ssm", (512, 1024), 1),
       ("w_out", (1024, 1024), 0), ("w_up", (1024, 5632), 1), ("conv_w", (3, 5632), 1), ("w_down", (2816, 1024), 0)]
SMALL = [("mix_norm_pre", (1024,)), ("q_norm", (384,)), ("kv_norm", (256,)), ("ssm_lambda_re", (32, 64)),
         ("ssm_lambda_im", (32, 64)), ("ssm_log_dt", (32,)), ("ssm_b_re", (32, 64, 16)), ("ssm_b_im", (32, 64, 16)),
         ("ssm_c_re", (32, 16, 64)), ("ssm_c_im", (32, 16, 64)), ("ssm_d", (32, 16)), ("b_glu", (512,)),
         ("b_gate", (2048,)), ("mix_norm_post", (1024,)), ("ffn_norm_pre", (1024,)), ("conv_b", (5632,)),
         ("ffn_norm_post", (1024,))]
MATMUL_W = [b for b in BIG if b[0] != "conv_w"]
LATE_W = ("w_up", "w_down", "conv_w")
CONV_W_SHAPE = (3, 2 * D_FF)
CONV_W_SHARD = (3, 2 * D_FF // N_CHIPS)
SMALL_SUM = [("loss", (1,))] + SMALL + [("conv_w", CONV_W_SHAPE)]
SMALL_ADAM = SMALL + [("conv_w", CONV_W_SHARD)]


def _pack_flat(layout, vals):
    flat = jnp.concatenate([vals[n].astype(F32).reshape(-1) for n, _ in layout])
    rows = -(-(-(-flat.shape[0] // FLAT_W)) // (2 * SUBLANES)) * 2 * SUBLANES
    return jnp.pad(flat, (0, rows * FLAT_W - flat.shape[0])).reshape(rows, FLAT_W)


def _unpack_flat(layout, flat):
    flat = flat.reshape(-1)
    out = {}
    o = 0
    for name, shape in layout:
        n = math.prod(shape)
        out[name] = flat[o:o + n].reshape(shape)
        o += n
    return out


_ARG_NAMES = ["x", "positions"] + [n for n in (
    "mix_norm_pre", "w_in", "q_norm", "w_uq", "kv_norm", "w_uk", "w_uv", "ssm_lambda_re", "ssm_lambda_im", "ssm_log_dt",
    "ssm_b_re", "ssm_b_im", "ssm_c_re", "ssm_c_im", "ssm_d", "w_glu", "b_glu", "w_branch_attn", "w_branch_ssm",
    "b_gate", "w_out", "mix_norm_post", "ffn_norm_pre", "w_up", "conv_w", "conv_b", "w_down", "ffn_norm_post")]
_WEIGHTS = _ARG_NAMES[2:]


def _gather_weights(w):
    early = [b for b in MATMUL_W if b[0] not in LATE_W]
    late = [b for b in BIG if b[0] in LATE_W]
    own = (jnp.arange(N_CHIPS) == 2 * lax.axis_index("x") + lax.axis_index("y"))[:, None, None]

    def whole(layout, mine, gathered):
        return {name: _from_chip_major(jnp.where(own, s[None], g), axis)
                for (name, _, axis), s, g in zip(layout, mine, gathered)}

    mine = [w[name].astype(BF16) for name, _, _ in early]
    full = whole(early, mine, _gather_big(mine))
    mine_late = [w[name].astype(F32 if name == "conv_w" else BF16) for name, _, _ in late]
    _, mine_late = lax.optimization_barrier((full["w_in"], mine_late))
    send_sems, recv_sems, shards_thru, lands_thru, token = _exchange_start(mine_late, "gather_late_start", scatter=False)

    def late_weights(after):
        shards, lands = _exchange_wait(send_sems, recv_sems, shards_thru, lands_thru, after, "gather_late_wait",
                                       scatter=False)
        return whole(late, shards, lands)

    full["late"] = late_weights
    full["token"] = token[0, 0]
    return full


def _pair_sums(names, grads, tag):
    core = lax.axis_index("c").astype(jnp.int32).reshape(1)
    theirs = _reduce_to_sibling(grads, "reduce_grads_d2d" + tag)
    return [_add_pair(g, t, core, "reduce_pair_" + n) for n, g, t in zip(names, grads, theirs)]


def _send_grads(tag, grads, after, flying, pending):
    token = jnp.zeros((), F32)
    if flying:
        tag0, names0, state0 = flying.pop()
        core = lax.axis_index("c").astype(jnp.int32).reshape(1)
        mine, theirs = _sibling_wait(*state0, after, "reduce_" + tag0 + "_d2d_wait")
        pairs = [_add_pair(g, t, core, "reduce_pair_" + n) for n, g, t in zip(names0, mine, theirs)]
        send_sems, recv_sems, pairs_thru, lands_thru, tok = _exchange_start(pairs, "reduce_" + tag0 + "_start",
                                                                          scatter=True)
        pending.append((tag0, names0, send_sems, recv_sems, pairs_thru, lands_thru))
        token = token + tok[0, 0]
    if grads:
        names = list(grads)
        send_sems, recv_sems, grads_thru, lands_thru, tok = _sibling_start([grads[n] for n in names],
                                                                          "reduce_" + tag + "_d2d_start")
        flying.append((tag, names, (send_sems, recv_sems, grads_thru, lands_thru)))
        token = token + tok[0, 0]
    return token


def _reduce_grads(gbig, pending, loss, gsmall, use_sent):
    core = lax.axis_index("c").astype(jnp.int32).reshape(1)
    chip = (2 * lax.axis_index("x") + lax.axis_index("y")).astype(jnp.int32).reshape(1)
    place = jnp.concatenate([chip, core])

    def finish(names, pairs, landed, name):
        totals = [_add_chips(r, p, place, "reduce_chips_" + n) for n, r, p in zip(names, landed, pairs)]
        return dict(zip(names, _reduce_back(totals, name)))

    names = list(gbig)
    pairs = _pair_sums(names, [gbig[n] for n in names], "")
    send_sems, recv_sems, pairs_thru, lands_thru, token = _exchange_start(pairs, "reduce_last_start", scatter=True)
    sent_names, sent_pairs, sent_landed = [], [], []
    for tag, group, g_send, g_recv, g_pairs, g_lands in pending:
        got_pairs, got_landed = _exchange_wait(g_send, g_recv, g_pairs, g_lands, token, "reduce_" + tag + "_wait",
                                               scatter=True)
        sent_names, sent_pairs, sent_landed = sent_names + group, sent_pairs + got_pairs, sent_landed + got_landed
    g_sent = finish(sent_names, sent_pairs, sent_landed, "reduce_back_sent")
    vals = dict(gsmall)
    vals["loss"] = loss
    small_red = _unpack_flat(SMALL_SUM, _all_reduce_small(_pack_flat(SMALL_SUM, vals), "reduce_small"))
    after = use_sent(g_sent, small_red)
    pairs, landed = _exchange_wait(send_sems, recv_sems, pairs_thru, lands_thru, after, "reduce_last_wait", scatter=True)
    return finish(names, pairs, landed, "reduce_back_last"), small_red


def _step(args):
    x = args["x"][0]
    positions = args["positions"][0]
    tgt = args["loss_target"][0]
    w = {n: args[n][0] for n in _WEIGHTS}
    m = {n: args["m_" + n][0] for n in _WEIGHTS}
    v = {n: args["v_" + n][0] for n in _WEIGHTS}

    full = _gather_weights(w)
    sp = {n: w[n].reshape(s) for n, s in SMALL}
    for n in ("mix_norm_pre", "q_norm", "kv_norm", "b_glu", "b_gate", "mix_norm_post", "ffn_norm_pre", "conv_b",
              "ffn_norm_post"):
        sp[n] = sp[n].reshape(1, -1)
    sp["mix_norm_pre"] = sp["mix_norm_pre"] + full.pop("token")
    pending, flying = [], []
    full["send_grads"] = lambda tag, grads, after=None: _send_grads(tag, grads, after, flying, pending)
    loss, gx, gbig, gsmall = _local_step(x, positions, tgt, full, sp)
    outs = {}

    def adam_big(g_red):
        for name in g_red:
            g2, d, m2, v2 = _adamw(w[name], g_red[name], m[name], v[name], "adamw_" + name)
            outs["grad_" + name], outs["delta_" + name], outs["new_m_" + name], outs["new_v_" + name] = g2, d, m2, v2
        return v2

    def use_sent(g_sent, small_red):
        chip = 2 * lax.axis_index("x") + lax.axis_index("y")
        grads = dict(small_red)
        grads["conv_w"] = lax.dynamic_slice_in_dim(small_red["conv_w"], chip * CONV_W_SHARD[1], CONV_W_SHARD[1], axis=1)
        outs.update({"grad_" + n: grads[n] for n, _ in SMALL_ADAM})
        _, d_sm, m_sm, v_sm = _adamw(_pack_flat(SMALL_ADAM, w), _pack_flat(SMALL_ADAM, grads),
                                     _pack_flat(SMALL_ADAM, m), _pack_flat(SMALL_ADAM, v), "adamw_small")
        for prefix, flat in (("delta_", d_sm), ("new_m_", m_sm), ("new_v_", v_sm)):
            for n, val in _unpack_flat(SMALL_ADAM, flat).items():
                outs[prefix + n] = val
        return adam_big(g_sent)

    g_last, small_red = _reduce_grads(gbig, pending, loss, gsmall, use_sent)
    adam_big(g_last)
    outs = {n: val.reshape(args[n.split("_", 1)[1] if not n.startswith("new_") else n[6:]].shape)
            for n, val in outs.items()}
    res = [small_red["loss"][0], gx[None]]
    for prefix in ("grad_", "delta_", "new_m_", "new_v_"):
        res += [outs[prefix + n] for n in _WEIGHTS]
    return tuple(res)


def kernel(x, positions, mix_norm_pre, w_in, q_norm, w_uq, kv_norm, w_uk, w_uv, ssm_lambda_re, ssm_lambda_im, ssm_log_dt, ssm_b_re, ssm_b_im, ssm_c_re, ssm_c_im, ssm_d, w_glu, b_glu, w_branch_attn, w_branch_ssm, b_gate, w_out, mix_norm_post, ffn_norm_pre, w_up, conv_w, conv_b, w_down, ffn_norm_post, loss_target, m_mix_norm_pre, m_w_in, m_q_norm, m_w_uq, m_kv_norm, m_w_uk, m_w_uv, m_ssm_lambda_re, m_ssm_lambda_im, m_ssm_log_dt, m_ssm_b_re, m_ssm_b_im, m_ssm_c_re, m_ssm_c_im, m_ssm_d, m_w_glu, m_b_glu, m_w_branch_attn, m_w_branch_ssm, m_b_gate, m_w_out, m_mix_norm_post, m_ffn_norm_pre, m_w_up, m_conv_w, m_conv_b, m_w_down, m_ffn_norm_post, v_mix_norm_pre, v_w_in, v_q_norm, v_w_uq, v_kv_norm, v_w_uk, v_w_uv, v_ssm_lambda_re, v_ssm_lambda_im, v_ssm_log_dt, v_ssm_b_re, v_ssm_b_im, v_ssm_c_re, v_ssm_c_im, v_ssm_d, v_w_glu, v_b_glu, v_w_branch_attn, v_w_branch_ssm, v_b_gate, v_w_out, v_mix_norm_post, v_ffn_norm_pre, v_w_up, v_conv_w, v_conv_b, v_w_down, v_ffn_norm_post):
    given = dict(locals())
    return _step(given)
```

```python
import math

import jax
import jax.numpy as jnp
from jax import lax
from jax.experimental import pallas as pl
from jax.experimental.pallas import tpu as pltpu

F32 = jnp.float32
BF16 = jnp.bfloat16
MESH = pl.DeviceIdType.MESH

D_MODEL = 1024
N_HEADS = 8
QK_NOPE = 64
QK_ROPE = 32
QK_HEAD = QK_NOPE + QK_ROPE
V_HEAD = 64
Q_RANK = 384
KV_RANK = 256
ROPE_THETA = 10000.0
SSM_W = 512
SSM_H = 16
SSM_G = 32
SSM_P = 64
SSM_CH = SSM_G * SSM_P
D_FF = 2816
EPS = 1e-6
ADAM_LR = 0.001
ADAM_B1 = 0.9
ADAM_B2 = 0.999
ADAM_EPS = 1e-08
ADAM_WD = 0.01
ADAM_STEP = 10

LANES = 128
SUBLANES = 8
VMEM_LIMIT = 56 * 1024 * 1024

HEAD_SLOT = LANES
HP = N_HEADS * HEAD_SLOT
P_CQ, P_CKV, P_KR, P_U, P_GL, P_END = 0, 384, 640, 768, 1280, 3328
KR_LANE = 64

FLAT_W = 1024
N_CHIPS = 4


def _tile(n, cap):
    if n <= cap:
        return n
    best = None
    for t in range(LANES, cap + 1, LANES):
        if n % t == 0:
            best = t
    assert best is not None, (n, cap)
    return best


def _params(sem):
    return pltpu.CompilerParams(dimension_semantics=sem, vmem_limit_bytes=VMEM_LIMIT)


def _dot(a, b):
    return jnp.dot(a, b, preferred_element_type=F32)


def _dot_nt(a, b):
    return lax.dot_general(a, b, (((1,), (1,)), ((), ())), preferred_element_type=F32)


def _dot_tn(a, b):
    return lax.dot_general(a, b, (((0,), (0,)), ((), ())), preferred_element_type=F32)


def _rms(x, g):
    r = lax.rsqrt(jnp.mean(x * x, axis=-1, keepdims=True) + EPS)
    return x * r * g, r


def _rms_bwd(dy, x, g):
    r = lax.rsqrt(jnp.mean(x * x, axis=-1, keepdims=True) + EPS)
    dyg = dy * g
    dx = r * dyg - x * (r * r * r) * jnp.mean(dyg * x, axis=-1, keepdims=True)
    dg = jnp.sum(dy * x * r, axis=0, keepdims=True)
    return dx, dg


_GELU_K0 = math.sqrt(2.0 / math.pi)
_GELU_K1 = 0.044715


def _gelu(x):
    th = jnp.tanh(_GELU_K0 * (x + _GELU_K1 * x * x * x))
    return 0.5 * x * (1.0 + th)


def _gelu_grad(x):
    th = jnp.tanh(_GELU_K0 * (x + _GELU_K1 * x * x * x))
    return 0.5 * (1.0 + th) + 0.5 * x * (1.0 - th * th) * _GELU_K0 * (1.0 + 3.0 * _GELU_K1 * x * x)


def _sigmoid(x):
    return 1.0 / (1.0 + jnp.exp(-x))


def _rope(q, c, s):
    n = q.shape[1]
    lane = lax.broadcasted_iota(jnp.int32, q.shape, 1) % HEAD_SLOT
    sw = jnp.where(lane < KR_LANE + QK_ROPE // 2, pltpu.roll(q, n - QK_ROPE // 2, 1), pltpu.roll(q, QK_ROPE // 2, 1))
    return q * c + sw * s


def _rope_bwd(dy, c, s):
    n = dy.shape[1]
    t = dy * s
    lane = lax.broadcasted_iota(jnp.int32, dy.shape, 1) % HEAD_SLOT
    sw = jnp.where(lane < KR_LANE + QK_ROPE // 2, pltpu.roll(t, n - QK_ROPE // 2, 1), pltpu.roll(t, QK_ROPE // 2, 1))
    rope_lane = jnp.logical_and(lane >= KR_LANE, lane < KR_LANE + QK_ROPE)
    return dy * c + jnp.where(rope_lane, sw, 0.0)


def _shift_down(x, k, halo):
    xs = pltpu.roll(x, k, 0)
    hs = pltpu.roll(halo, k, 0)
    rows = lax.broadcasted_iota(jnp.int32, halo.shape, 0)
    top = jnp.where(rows < k, hs, xs[0:SUBLANES])
    return jnp.concatenate([top, xs[SUBLANES:]], axis=0)


def _shift_up(x, k, halo):
    t = x.shape[0]
    xs = pltpu.roll(x, t - k, 0)
    hs = pltpu.roll(halo, SUBLANES - k, 0)
    rows = lax.broadcasted_iota(jnp.int32, halo.shape, 0)
    bot = jnp.where(rows >= SUBLANES - k, hs, xs[t - SUBLANES:])
    return jnp.concatenate([xs[:t - SUBLANES], bot], axis=0)


def _mm(a, b, name, out_dtype=F32, bt=False, b_col0=0, n=None, tm_cap=1024, tn_cap=1408, a_lead=None):
    m, k = a.shape[-2:]
    if bt:
        n_full = b.shape[0]
        n = n_full
    else:
        n = b.shape[1] if n is None else n
    tm = min(tm_cap, m)
    tn = _tile(n, tn_cap)

    def body(a_ref, b_ref, o_ref):
        if bt:
            o_ref[...] = _dot_nt(a_ref[...], b_ref[...]).astype(out_dtype)
        else:
            o_ref[...] = _dot(a_ref[...], b_ref[...]).astype(out_dtype)

    if bt:
        b_spec = pl.BlockSpec((tn, k), lambda j, i: (j, b_col0))
    else:
        off = b_col0 * (n // tn)
        b_spec = pl.BlockSpec((k, tn), lambda j, i: (0, off + j))
    if a_lead is None:
        a_spec = pl.BlockSpec((tm, k), lambda j, i: (i, 0))
    else:
        a_spec = pl.BlockSpec((None, tm, k), lambda j, i: (a_lead, i, 0))
    return pl.pallas_call(
        body, name=name, grid=(n // tn, m // tm),
        in_specs=[a_spec, b_spec],
        out_specs=pl.BlockSpec((tm, tn), lambda j, i: (i, j)),
        out_shape=jax.ShapeDtypeStruct((m, n), out_dtype),
        compiler_params=_params(("parallel", "parallel")),
    )(a, b)


def _mm_tn(a, b, name, tk_cap=1024, tn_cap=1664, tl_cap=1024, chips=False):
    l, k = a.shape
    tk = _tile(k, tk_cap)
    tl = min(tl_cap, l)

    def body(a_ref, b_ref, o_ref):
        @pl.when(pl.program_id(2) == 0)
        def _():
            o_ref[...] = jnp.zeros_like(o_ref)

        o_ref[...] += _dot_tn(a_ref[...], b_ref[...])

    if chips:
        n = b.shape[-1] * (b.shape[0] if b.ndim == 3 else 1)
        tn = n // N_CHIPS
        assert tn % LANES == 0
        if b.ndim == 3:
            per = N_CHIPS // b.shape[0]
            b_spec = pl.BlockSpec((None, tl, tn), lambda i, j, r: (j // per, r, j % per))
        else:
            b_spec = pl.BlockSpec((tl, tn), lambda i, j, r: (r, j))
        out_spec = pl.BlockSpec((None, tk, tn), lambda i, j, r: (j, i, 0))
        out_shape = jax.ShapeDtypeStruct((N_CHIPS, k, tn), F32)
    else:
        n = b.shape[1]
        tn = _tile(n, tn_cap)
        b_spec = pl.BlockSpec((tl, tn), lambda i, j, r: (r, j))
        out_spec = pl.BlockSpec((tk, tn), lambda i, j, r: (i, j))
        out_shape = jax.ShapeDtypeStruct((k, n), F32)
    return pl.pallas_call(
        body, name=name, grid=(k // tk, n // tn, l // tl),
        in_specs=[pl.BlockSpec((tl, tk), lambda i, j, r: (r, i)), b_spec],
        out_specs=out_spec, out_shape=out_shape,
        compiler_params=_params(("parallel", "parallel", "arbitrary")),
    )(a, b)


def _row(tl, n):
    return pl.BlockSpec((tl, n), lambda i: (i, 0))


def _const(shape):
    return pl.BlockSpec(shape, lambda i: tuple(0 for _ in shape))


def _proj_fwd(x, g1, win, gq, wuq, gkv, wukv, rc, rs, bg, tl):
    l = x.shape[0]

    def body(x_ref, g1_ref, win_ref, gq_ref, wuq_ref, gkv_ref, wukv_ref, rc_ref, rs_ref, bg_ref,
             hn_ref, cq_ref, ckv_ref, q_ref, k_ref, v_ref, u_ref, gl_ref):
        hn, _ = _rms(x_ref[...], g1_ref[...])
        hnb = hn.astype(BF16)
        hn_ref[...] = hnb
        proj = _dot(hnb, win_ref[...])
        cq = proj[:, P_CQ:P_CKV]
        ckv = proj[:, P_CKV:P_KR]
        kr = proj[:, P_KR:P_U]
        cq_ref[...] = cq
        ckv_ref[...] = ckv
        u_ref[...] = proj[:, P_U:P_GL]
        gl_ref[...] = proj[:, P_GL:P_END] + bg_ref[...]
        qn, _ = _rms(cq, gq_ref[...])
        q = _dot(qn.astype(BF16), wuq_ref[...])
        c1 = rc_ref[...]
        s1 = rs_ref[...]
        q_ref[...] = (_rope(q, jnp.tile(c1, (1, N_HEADS)), jnp.tile(s1, (1, N_HEADS))) * Q_PRESCALE).astype(BF16)
        ckvn, _ = _rms(ckv, gkv_ref[...])
        kv = _dot(ckvn.astype(BF16), wukv_ref[...])
        krr = _rope(kr, c1, s1)
        k_ref[...] = (kv[:, :HP] + jnp.tile(krr, (1, N_HEADS))).astype(BF16)
        v_ref[...] = kv[:, HP:].astype(BF16)

    outs = [(D_MODEL, BF16), (Q_RANK, F32), (KV_RANK, F32), (HP, BF16), (HP, BF16), (HP, BF16),
            (SSM_W, F32), (2 * D_MODEL, F32)]
    return pl.pallas_call(
        body, name="proj_fwd", grid=(l // tl,),
        in_specs=[_row(tl, D_MODEL), _const((1, D_MODEL)), _const((D_MODEL, P_END)), _const((1, Q_RANK)),
                  _const((Q_RANK, HP)), _const((1, KV_RANK)), _const((KV_RANK, 2 * HP)),
                  _row(tl, HEAD_SLOT), _row(tl, HEAD_SLOT), _const((1, 2 * D_MODEL))],
        out_specs=[_row(tl, n) for n, _ in outs],
        out_shape=[jax.ShapeDtypeStruct((l, n), dt) for n, dt in outs],
        compiler_params=_params(("parallel",)),
    )(x, g1, win, gq, wuq, gkv, wukv, rc, rs, bg)


_NEG = -1e30


LOG2E = 1.0 / math.log(2.0)
LN2 = math.log(2.0)
ATTN_SCALE = 1.0 / math.sqrt(QK_HEAD)
Q_PRESCALE = ATTN_SCALE * LOG2E
HEADS_PER_STEP = 4
PAIR_W = HEADS_PER_STEP * HEAD_SLOT


def _causal_pairs(nq, by_query):
    if by_query:
        pairs = [(i, j) for i in range(nq) for j in range(i + 1)]
    else:
        pairs = [(i, j) for j in range(nq) for i in range(j, nq)]
    return jnp.array([p[0] for p in pairs], jnp.int32), jnp.array([p[1] for p in pairs], jnp.int32)


def _diag_mask_t(s):
    rows = lax.broadcasted_iota(jnp.int32, s.shape, 0)
    cols = lax.broadcasted_iota(jnp.int32, s.shape, 1)
    return jnp.where(rows <= cols, s, _NEG)


def _attn_fwd(q, k, v, tq):
    l = q.shape[0]
    nq = l // tq
    it, jt = _causal_pairs(nq, True)

    def body(it_ref, jt_ref, q_ref, k_ref, v_ref, o_ref, lse_ref, m_ref, l_ref, acc_ref):
        t = pl.program_id(1)
        i = it_ref[t]
        j = jt_ref[t]

        @pl.when(j == 0)
        def _():
            m_ref[...] = jnp.full_like(m_ref, _NEG)
            l_ref[...] = jnp.zeros_like(l_ref)
            acc_ref[...] = jnp.zeros_like(acc_ref)

        def update(on_diagonal):
            for hh in range(HEADS_PER_STEP):
                sl = slice(hh * HEAD_SLOT, (hh + 1) * HEAD_SLOT)
                s = _dot_nt(k_ref[:, sl], q_ref[:, sl])
                if on_diagonal:
                    s = _diag_mask_t(s)
                m_old = m_ref[hh]
                m_new = jnp.maximum(m_old, jnp.max(s, axis=0, keepdims=True))
                p = jnp.exp2(s - m_new)
                alpha = jnp.exp2(m_old - m_new)
                l_ref[hh] = alpha * l_ref[hh] + jnp.sum(p, axis=0, keepdims=True)
                acc_ref[hh] = alpha * acc_ref[hh] + _dot_tn(v_ref[:, sl], p.astype(BF16))
                m_ref[hh] = m_new

        @pl.when(j < i)
        def _():
            update(False)

        @pl.when(j == i)
        def _():
            update(True)
            for hh in range(HEADS_PER_STEP):
                sl = slice(hh * HEAD_SLOT, (hh + 1) * HEAD_SLOT)
                o_ref[:, sl] = (acc_ref[hh] / l_ref[hh]).T.astype(BF16)
                lse_ref[hh] = m_ref[hh] + jnp.log(l_ref[hh]) * LOG2E

    blk = (tq, PAIR_W)
    qmap = lambda h, t, it_ref, jt_ref: (it_ref[t], h)
    kmap = lambda h, t, it_ref, jt_ref: (jt_ref[t], h)
    row = pl.BlockSpec((HEADS_PER_STEP, 1, tq), lambda h, t, it_ref, jt_ref: (h, 0, it_ref[t]))
    return pl.pallas_call(
        body, name="attn_fwd",
        grid_spec=pltpu.PrefetchScalarGridSpec(
            num_scalar_prefetch=2, grid=(N_HEADS // HEADS_PER_STEP, it.shape[0]),
            in_specs=[pl.BlockSpec(blk, qmap), pl.BlockSpec(blk, kmap), pl.BlockSpec(blk, kmap)],
            out_specs=[pl.BlockSpec(blk, qmap), row],
            scratch_shapes=[pltpu.VMEM((HEADS_PER_STEP, 1, tq), F32), pltpu.VMEM((HEADS_PER_STEP, 1, tq), F32),
                            pltpu.VMEM((HEADS_PER_STEP, HEAD_SLOT, tq), F32)]),
        out_shape=[jax.ShapeDtypeStruct((l, HP), BF16), jax.ShapeDtypeStruct((N_HEADS, 1, l), F32)],
        compiler_params=_params(("parallel", "arbitrary")),
    )(it, jt, q, k, v)


def _attn_delta(o, do, tq):
    l = o.shape[0]

    def body(o_ref, do_ref, d_ref):
        prod = o_ref[...].astype(F32) * do_ref[...].astype(F32)
        for hh in range(HEADS_PER_STEP):
            d_ref[hh] = jnp.sum(prod[:, hh * HEAD_SLOT:(hh + 1) * HEAD_SLOT].T, axis=0, keepdims=True)

    blk = pl.BlockSpec((tq, PAIR_W), lambda h, i: (i, h))
    return pl.pallas_call(
        body, name="attn_delta", grid=(N_HEADS // HEADS_PER_STEP, l // tq), in_specs=[blk, blk],
        out_specs=pl.BlockSpec((HEADS_PER_STEP, 1, tq), lambda h, i: (h, 0, i)),
        out_shape=jax.ShapeDtypeStruct((N_HEADS, 1, l), F32),
        compiler_params=_params(("parallel", "parallel")),
    )(o, do)


def _attn_bwd(q, k, v, do, lse, delta, tq):
    l = q.shape[0]
    nq = l // tq
    it, jt = _causal_pairs(nq, False)

    def body(it_ref, jt_ref, q_ref, k_ref, v_ref, do_ref, lse_ref, dl_ref, dq_ref, dk_ref, dv_ref, dka_ref, dva_ref):
        t = pl.program_id(1)
        i = it_ref[t]
        j = jt_ref[t]

        @pl.when(t == 0)
        def _():
            dq_ref[...] = jnp.zeros_like(dq_ref)

        @pl.when(i == j)
        def _():
            dka_ref[...] = jnp.zeros_like(dka_ref)
            dva_ref[...] = jnp.zeros_like(dva_ref)

        def update(on_diagonal):
            r0 = pl.multiple_of(i * tq, tq)
            for hh in range(HEADS_PER_STEP):
                sl = slice(hh * HEAD_SLOT, (hh + 1) * HEAD_SLOT)
                qb = q_ref[:, sl]
                kb = k_ref[:, sl]
                dob = do_ref[:, sl]
                s = _dot_nt(kb, qb)
                if on_diagonal:
                    s = _diag_mask_t(s)
                p = jnp.exp2(s - lse_ref[hh])
                dva_ref[:, sl] += _dot(p.astype(BF16), dob)
                dp = _dot_nt(v_ref[:, sl], dob)
                ds = (p * (dp - dl_ref[hh])).astype(BF16)
                dka_ref[:, sl] += _dot(ds, qb)
                dq_ref[pl.ds(r0, tq), sl] += ATTN_SCALE * _dot_tn(ds, kb)

        @pl.when(j < i)
        def _():
            update(False)

        @pl.when(j == i)
        def _():
            update(True)

        @pl.when(i == nq - 1)
        def _():
            dk_ref[...] = (dka_ref[...] * LN2).astype(BF16)
            dv_ref[...] = dva_ref[...].astype(BF16)

    blk = (tq, PAIR_W)
    qmap = lambda h, t, it_ref, jt_ref: (it_ref[t], h)
    kmap = lambda h, t, it_ref, jt_ref: (jt_ref[t], h)
    row = pl.BlockSpec((HEADS_PER_STEP, 1, tq), lambda h, t, it_ref, jt_ref: (h, 0, it_ref[t]))
    return pl.pallas_call(
        body, name="attn_bwd",
        grid_spec=pltpu.PrefetchScalarGridSpec(
            num_scalar_prefetch=2, grid=(N_HEADS // HEADS_PER_STEP, it.shape[0]),
            in_specs=[pl.BlockSpec(blk, qmap), pl.BlockSpec(blk, kmap), pl.BlockSpec(blk, kmap),
                      pl.BlockSpec(blk, qmap), row, row],
            out_specs=[pl.BlockSpec((l, PAIR_W), lambda h, t, it_ref, jt_ref: (0, h)), pl.BlockSpec(blk, kmap),
                       pl.BlockSpec(blk, kmap)],
            scratch_shapes=[pltpu.VMEM(blk, F32), pltpu.VMEM(blk, F32)]),
        out_shape=[jax.ShapeDtypeStruct((l, HP), F32), jax.ShapeDtypeStruct((l, HP), BF16),
                   jax.ShapeDtypeStruct((l, HP), BF16)],
        compiler_params=_params(("parallel", "arbitrary")),
    )(it, jt, q, k, v, do, lse, delta)


SSM_CB = 512
SSM_UB = 128
SSM_NB = SSM_CH // SSM_CB


def _to_segments(x, tt):
    l, c = x.shape
    return x.reshape(l // tt, SUBLANES, tt // SUBLANES, c).transpose(0, 2, 1, 3).reshape(l, c)


def _from_segments(x, tt):
    l, c = x.shape
    return x.reshape(l // tt, tt // SUBLANES, SUBLANES, c).transpose(0, 2, 1, 3).reshape(l, c)


def _seg_scan(re_ref, im_ref, ab_ref, pw_ref, car_ref, n_tiles, reverse):
    ar, ai = ab_ref[0], ab_ref[1]

    def local(n, prev):
        pr, pi = prev
        idx = (n_tiles - 1 - n) if reverse else n
        rows = pl.ds(pl.multiple_of(idx * SUBLANES, SUBLANES), SUBLANES)
        sr = re_ref[rows, :] + ar * pr - ai * pi
        si = im_ref[rows, :] + ar * pi + ai * pr
        re_ref[rows, :] = sr
        im_ref[rows, :] = si
        return sr, si

    zero = jnp.zeros((SUBLANES, re_ref.shape[1]), F32)
    er, ei = lax.fori_loop(0, n_tiles, local, (zero, zero), unroll=2)
    a_r, a_i = ab_ref[2][0:1, :], ab_ref[3][0:1, :]
    cr, ci = car_ref[0:1, :], car_ref[SUBLANES:SUBLANES + 1, :]
    seg = lax.broadcasted_iota(jnp.int32, zero.shape, 0)
    ir, ii = zero, zero
    for s in (range(SUBLANES - 1, -1, -1) if reverse else range(SUBLANES)):
        ir = jnp.where(seg == s, cr, ir)
        ii = jnp.where(seg == s, ci, ii)
        cr, ci = er[s:s + 1, :] + a_r * cr - a_i * ci, ei[s:s + 1, :] + a_r * ci + a_i * cr
    car_ref[0:1, :] = cr
    car_ref[SUBLANES:SUBLANES + 1, :] = ci

    def fix(n, carry):
        rows = pl.ds(pl.multiple_of(n * SUBLANES, SUBLANES), SUBLANES)
        pr, pi = pw_ref[0, rows, :], pw_ref[1, rows, :]
        re_ref[rows, :] += pr * ir - pi * ii
        im_ref[rows, :] += pr * ii + pi * ir
        return carry

    lax.fori_loop(0, n_tiles, fix, 0, unroll=2)


def _ssm_fwd(u, bre, bim, cre, cim, dvec, ab, pw, tt):
    l = u.shape[0]
    nt = l // tt

    def body(u_ref, bre_ref, bim_ref, cre_ref, cim_ref, d_ref, ab_ref, pw_ref, y_ref, sre_ref, sim_ref, car_ref):
        @pl.when(pl.program_id(1) == 0)
        def _():
            car_ref[...] = jnp.zeros_like(car_ref)

        uf = u_ref[...]
        ub = uf.astype(BF16)
        sre_ref[...] = _dot(ub, bre_ref[0])
        sim_ref[...] = _dot(ub, bim_ref[0])
        _seg_scan(sre_ref, sim_ref, ab_ref, pw_ref, car_ref, tt // SUBLANES, False)
        y_ref[...] = (_dot(sre_ref[...].astype(BF16), cre_ref[0]) - _dot(sim_ref[...].astype(BF16), cim_ref[0])
                      + d_ref[...] * uf)

    return pl.pallas_call(
        body, name="ssm_fwd", grid=(SSM_NB, nt),
        in_specs=[pl.BlockSpec((tt, SSM_UB), lambda m, t: (t, m)),
                  pl.BlockSpec((1, SSM_UB, SSM_CB), lambda m, t: (m, 0, 0)),
                  pl.BlockSpec((1, SSM_UB, SSM_CB), lambda m, t: (m, 0, 0)),
                  pl.BlockSpec((1, SSM_CB, SSM_UB), lambda m, t: (m, 0, 0)),
                  pl.BlockSpec((1, SSM_CB, SSM_UB), lambda m, t: (m, 0, 0)),
                  pl.BlockSpec((1, SSM_UB), lambda m, t: (0, m)),
                  pl.BlockSpec((4, SUBLANES, SSM_CB), lambda m, t: (0, 0, m)),
                  pl.BlockSpec((2, tt, SSM_CB), lambda m, t: (0, 0, m))],
        out_specs=[pl.BlockSpec((tt, SSM_UB), lambda m, t: (t, m)),
                   pl.BlockSpec((tt, SSM_CB), lambda m, t: (t, m)),
                   pl.BlockSpec((tt, SSM_CB), lambda m, t: (t, m))],
        out_shape=[jax.ShapeDtypeStruct((l, SSM_W), F32), jax.ShapeDtypeStruct((l, SSM_CH), F32),
                   jax.ShapeDtypeStruct((l, SSM_CH), F32)],
        scratch_shapes=[pltpu.VMEM((2 * SUBLANES, SSM_CB), F32)],
        compiler_params=_params(("parallel", "arbitrary")),
    )(u, bre, bim, cre, cim, dvec, ab, pw)


def _ssm_bwd(dy, u, sre, sim, bre, bim, cre, cim, dvec, ab, pw, tt):
    l = u.shape[0]
    nt = l // tt
    tpb = tt // SUBLANES

    def body(dy_ref, u_ref, sre_ref, sim_ref, hre_ref, him_ref, bre_ref, bim_ref, cre_ref, cim_ref, d_ref, ab_ref,
             pw_ref, du_ref, dbre_ref, dbim_ref, dcre_ref, dcim_ref, dare_ref, daim_ref, dd_ref, lr_ref, li_ref,
             car_ref):
        t = pl.program_id(1)

        @pl.when(t == 0)
        def _():
            car_ref[...] = jnp.zeros_like(car_ref)
            for ref in (dbre_ref, dbim_ref, dcre_ref, dcim_ref, dare_ref, daim_ref, dd_ref):
                ref[...] = jnp.zeros_like(ref)

        dyf = dy_ref[...]
        dyb = dyf.astype(BF16)
        uf = u_ref[...]
        s_re = sre_ref[...]
        s_im = sim_ref[...]
        lr_ref[...] = _dot_nt(dyb, cre_ref[0])
        li_ref[...] = -_dot_nt(dyb, cim_ref[0])
        dcre_ref[0] += _dot_tn(s_re.astype(BF16), dyb)
        dcim_ref[0] -= _dot_tn(s_im.astype(BF16), dyb)
        _seg_scan(lr_ref, li_ref, ab_ref, pw_ref, car_ref, tpb, True)
        keep = jnp.where(t == nt - 1, 0.0, 1.0)
        first = pl.ds(0, SUBLANES)
        last = pl.ds((tpb - 1) * SUBLANES, SUBLANES)
        top = lax.broadcasted_iota(jnp.int32, (SUBLANES, SSM_CB), 0) == 0
        sp_r = jnp.where(top, pltpu.roll(hre_ref[...] * keep, 1, 0), pltpu.roll(sre_ref[last, :], 1, 0))
        sp_i = jnp.where(top, pltpu.roll(him_ref[...] * keep, 1, 0), pltpu.roll(sim_ref[last, :], 1, 0))
        acc_r = lr_ref[first, :] * sp_r + li_ref[first, :] * sp_i
        acc_i = li_ref[first, :] * sp_r - lr_ref[first, :] * sp_i

        def overlap(n, acc):
            rows = pl.ds(pl.multiple_of(n * SUBLANES, SUBLANES), SUBLANES)
            before = pl.ds(pl.multiple_of((n - 1) * SUBLANES, SUBLANES), SUBLANES)
            l_r, l_i, p_r, p_i = lr_ref[rows, :], li_ref[rows, :], sre_ref[before, :], sim_ref[before, :]
            return acc[0] + l_r * p_r + l_i * p_i, acc[1] + l_i * p_r - l_r * p_i

        acc_r, acc_i = lax.fori_loop(1, tpb, overlap, (acc_r, acc_i), unroll=2)
        dare_ref[...] += jnp.sum(acc_r, axis=0, keepdims=True)
        daim_ref[...] += jnp.sum(acc_i, axis=0, keepdims=True)
        lrb = lr_ref[...].astype(BF16)
        lib = li_ref[...].astype(BF16)
        du_ref[...] = _dot_nt(lrb, bre_ref[0]) + _dot_nt(lib, bim_ref[0]) + dyf * d_ref[...]
        ub = uf.astype(BF16)
        dbre_ref[0] += _dot_tn(ub, lrb)
        dbim_ref[0] += _dot_tn(ub, lib)
        dd_ref[...] += jnp.sum(dyf * uf, axis=0, keepdims=True)

    rev = lambda m, t: (nt - 1 - t, m)
    halo = lambda m, t: (jnp.maximum((nt - 1 - t) * tpb - 1, 0), m)
    wb = pl.BlockSpec((1, SSM_UB, SSM_CB), lambda m, t: (m, 0, 0))
    wc = pl.BlockSpec((1, SSM_CB, SSM_UB), lambda m, t: (m, 0, 0))
    vec_c = pl.BlockSpec((1, SSM_CB), lambda m, t: (0, m))
    vec_u = pl.BlockSpec((1, SSM_UB), lambda m, t: (0, m))
    return pl.pallas_call(
        body, name="ssm_bwd", grid=(SSM_NB, nt),
        in_specs=[pl.BlockSpec((tt, SSM_UB), rev), pl.BlockSpec((tt, SSM_UB), rev),
                  pl.BlockSpec((tt, SSM_CB), rev), pl.BlockSpec((tt, SSM_CB), rev),
                  pl.BlockSpec((SUBLANES, SSM_CB), halo), pl.BlockSpec((SUBLANES, SSM_CB), halo),
                  wb, wb, wc, wc, vec_u,
                  pl.BlockSpec((4, SUBLANES, SSM_CB), lambda m, t: (0, 0, m)),
                  pl.BlockSpec((2, tt, SSM_CB), lambda m, t: (0, 0, m))],
        out_specs=[pl.BlockSpec((tt, SSM_UB), rev), wb, wb, wc, wc, vec_c, vec_c, vec_u],
        out_shape=[jax.ShapeDtypeStruct((l, SSM_W), F32),
                   jax.ShapeDtypeStruct((SSM_NB, SSM_UB, SSM_CB), F32), jax.ShapeDtypeStruct((SSM_NB, SSM_UB, SSM_CB), F32),
                   jax.ShapeDtypeStruct((SSM_NB, SSM_CB, SSM_UB), F32), jax.ShapeDtypeStruct((SSM_NB, SSM_CB, SSM_UB), F32),
                   jax.ShapeDtypeStruct((1, SSM_CH), F32), jax.ShapeDtypeStruct((1, SSM_CH), F32),
                   jax.ShapeDtypeStruct((1, SSM_W), F32)],
        scratch_shapes=[pltpu.VMEM((tt, SSM_CB), F32), pltpu.VMEM((tt, SSM_CB), F32),
                        pltpu.VMEM((2 * SUBLANES, SSM_CB), F32)],
        compiler_params=_params(("parallel", "arbitrary")),
    )(dy, u, sre, sim, sre, sim, bre, bim, cre, cim, dvec, ab, pw)


def _merge_fwd(x, gl, attn, y1, wba, wbs, wglu, bglu, wout, gpost, gpre, tl):
    l = x.shape[0]

    def body(x_ref, gl_ref, at_ref, y1_ref, wba_ref, wbs_ref, wglu_ref, bglu_ref, wout_ref, gpost_ref, gpre_ref,
             a_ref, sm_ref, mg_ref, z_ref, x1_ref, hn2_ref, y3_ref):
        y2 = _gelu(y1_ref[...])
        sg = _sigmoid(_dot(y2.astype(BF16), wglu_ref[...]) + bglu_ref[...])
        y3 = (y2 * sg).astype(BF16)
        y3_ref[...] = y3
        a = _dot(at_ref[...], wba_ref[...])
        sm = _dot(y3, wbs_ref[...])
        a_ref[...] = a
        sm_ref[...] = sm
        g = _sigmoid(gl_ref[...])
        merged = (g[:, :D_MODEL] * a + g[:, D_MODEL:] * sm).astype(BF16)
        mg_ref[...] = merged
        z = _dot(merged, wout_ref[...])
        z_ref[...] = z
        n, _ = _rms(z, gpost_ref[...])
        x1 = x_ref[...] + n
        x1_ref[...] = x1
        hn2, _ = _rms(x1, gpre_ref[...])
        hn2_ref[...] = hn2.astype(BF16)

    outs = [(D_MODEL, F32), (D_MODEL, F32), (D_MODEL, BF16), (D_MODEL, F32), (D_MODEL, F32), (D_MODEL, BF16),
            (SSM_W, BF16)]
    return pl.pallas_call(
        body, name="merge_fwd", grid=(l // tl,),
        in_specs=[_row(tl, D_MODEL), _row(tl, 2 * D_MODEL), _row(tl, HP), _row(tl, SSM_W),
                  _const((HP, D_MODEL)), _const((SSM_W, D_MODEL)), _const((SSM_W, SSM_W)), _const((1, SSM_W)),
                  _const((D_MODEL, D_MODEL)), _const((1, D_MODEL)), _const((1, D_MODEL))],
        out_specs=[_row(tl, n) for n, _ in outs],
        out_shape=[jax.ShapeDtypeStruct((l, n), dt) for n, dt in outs],
        compiler_params=_params(("parallel",)),
    )(x, gl, attn, y1, wba, wbs, wglu, bglu, wout, gpost, gpre)


def _merge_bwd(dhn2a, dhn2b, x1, dx2, z, gl, a, sm, y1, wba, wbs, wglu, bglu, wout, gpost, gpre, tl):
    l = x1.shape[0]

    def body(da_ref, db_ref, x1_ref, dx2_ref, z_ref, gl_ref, a_ref, sm_ref, y1_ref,
             wba_ref, wbs_ref, wglu_ref, bglu_ref, wout_ref, gpost_ref, gpre_ref,
             dx1_ref, dz_ref, dbra_ref, dbrs_ref, dgl_ref, dat_ref, dy1_ref, dt_ref, y2_ref,
             dgpre_ref, dgpost_ref, dbg_ref, dbglu_ref):
        @pl.when(pl.program_id(0) == 0)
        def _():
            for ref in (dgpre_ref, dgpost_ref, dbg_ref, dbglu_ref):
                ref[...] = jnp.zeros_like(ref)

        dhn2 = da_ref[...] + db_ref[...]
        dx1a, dgpre = _rms_bwd(dhn2, x1_ref[...], gpre_ref[...])
        dgpre_ref[...] += dgpre
        dx1 = dx2_ref[...] + dx1a
        dx1_ref[...] = dx1
        dz, dgpost = _rms_bwd(dx1, z_ref[...], gpost_ref[...])
        dgpost_ref[...] += dgpost
        dzb = dz.astype(BF16)
        dz_ref[...] = dzb
        dm = _dot_nt(dzb, wout_ref[...])
        g = _sigmoid(gl_ref[...])
        g0 = g[:, :D_MODEL]
        g1 = g[:, D_MODEL:]
        dbra = (dm * g0).astype(BF16)
        dbrs = (dm * g1).astype(BF16)
        dbra_ref[...] = dbra
        dbrs_ref[...] = dbrs
        dgl0 = dm * a_ref[...] * g0 * (1.0 - g0)
        dgl1 = dm * sm_ref[...] * g1 * (1.0 - g1)
        dgl_ref[:, :D_MODEL] = dgl0.astype(BF16)
        dgl_ref[:, D_MODEL:] = dgl1.astype(BF16)
        dbg_ref[:, :D_MODEL] += jnp.sum(dgl0, axis=0, keepdims=True)
        dbg_ref[:, D_MODEL:] += jnp.sum(dgl1, axis=0, keepdims=True)
        dat_ref[...] = _dot_nt(dbra, wba_ref[...]).astype(BF16)
        dy3 = _dot_nt(dbrs, wbs_ref[...])
        y1v = y1_ref[...]
        y2 = _gelu(y1v)
        y2b = y2.astype(BF16)
        y2_ref[...] = y2b
        sg = _sigmoid(_dot(y2b, wglu_ref[...]) + bglu_ref[...])
        dt = dy3 * y2 * sg * (1.0 - sg)
        dtb = dt.astype(BF16)
        dt_ref[...] = dtb
        dbglu_ref[...] += jnp.sum(dt, axis=0, keepdims=True)
        dy2 = dy3 * sg + _dot_nt(dtb, wglu_ref[...])
        dy1_ref[...] = dy2 * _gelu_grad(y1v)

    outs = [(D_MODEL, F32), (D_MODEL, BF16), (D_MODEL, BF16), (D_MODEL, BF16), (2 * D_MODEL, BF16), (HP, BF16),
            (SSM_W, F32), (SSM_W, BF16), (SSM_W, BF16)]
    accs = [D_MODEL, D_MODEL, 2 * D_MODEL, SSM_W]
    return pl.pallas_call(
        body, name="merge_bwd", grid=(l // tl,),
        in_specs=[_row(tl, D_MODEL), _row(tl, D_MODEL), _row(tl, D_MODEL), _row(tl, D_MODEL), _row(tl, D_MODEL),
                  _row(tl, 2 * D_MODEL), _row(tl, D_MODEL), _row(tl, D_MODEL), _row(tl, SSM_W),
                  _const((HP, D_MODEL)), _const((SSM_W, D_MODEL)), _const((SSM_W, SSM_W)), _const((1, SSM_W)),
                  _const((D_MODEL, D_MODEL)), _const((1, D_MODEL)), _const((1, D_MODEL))],
        out_specs=[_row(tl, n) for n, _ in outs] + [_const((1, n)) for n in accs],
        out_shape=[jax.ShapeDtypeStruct((l, n), dt) for n, dt in outs]
        + [jax.ShapeDtypeStruct((1, n), F32) for n in accs],
        compiler_params=_params(("arbitrary",)),
    )(dhn2a, dhn2b, x1, dx2, z, gl, a, sm, y1, wba, wbs, wglu, bglu, wout, gpost, gpre)


def _proj_bwd(x, dx1, cq, ckv, dq, dk, dv, du, dgl, g1, win, gq, wuq, gkv, wukv, rc, rs, tl):
    l = x.shape[0]

    def body(x_ref, dx1_ref, cq_ref, ckv_ref, dq_ref, dk_ref, dv_ref, du_ref, dgl_ref,
             g1_ref, win_ref, gq_ref, wuq_ref, gkv_ref, wukv_ref, rc_ref, rs_ref,
             gx_ref, dql_ref, qn_ref, ckvn_ref, dproj_ref, dg1_ref, dgq_ref, dgkv_ref):
        @pl.when(pl.program_id(0) == 0)
        def _():
            for ref in (dg1_ref, dgq_ref, dgkv_ref):
                ref[...] = jnp.zeros_like(ref)

        c1 = rc_ref[...]
        s1 = rs_ref[...]
        dql = _rope_bwd(dq_ref[...], jnp.tile(c1, (1, N_HEADS)), jnp.tile(s1, (1, N_HEADS))).astype(BF16)
        dql_ref[...] = dql
        dqn = _dot_nt(dql, wuq_ref[...])
        cq = cq_ref[...]
        qn, _ = _rms(cq, gq_ref[...])
        qn_ref[...] = qn.astype(BF16)
        dcq, dgq = _rms_bwd(dqn, cq, gq_ref[...])
        dgq_ref[...] += dgq
        dkb = dk_ref[...]
        dvb = dv_ref[...]
        dkf = dkb.astype(F32)
        dkr = dkf[:, 0:HEAD_SLOT]
        for h in range(1, N_HEADS):
            dkr = dkr + dkf[:, h * HEAD_SLOT:(h + 1) * HEAD_SLOT]
        dkr = _rope_bwd(dkr, c1, s1)
        dckvn = _dot_nt(dkb, wukv_ref[:, :HP]) + _dot_nt(dvb, wukv_ref[:, HP:])
        ckv = ckv_ref[...]
        ckvn, _ = _rms(ckv, gkv_ref[...])
        ckvn_ref[...] = ckvn.astype(BF16)
        dckv, dgkv = _rms_bwd(dckvn, ckv, gkv_ref[...])
        dgkv_ref[...] += dgkv
        dproj_ref[:, P_CQ:P_CKV] = dcq.astype(BF16)
        dproj_ref[:, P_CKV:P_KR] = dckv.astype(BF16)
        dproj_ref[:, P_KR:P_U] = dkr.astype(BF16)
        dproj_ref[:, P_U:P_GL] = du_ref[...].astype(BF16)
        dproj_ref[:, P_GL:P_END] = dgl_ref[...]
        dhn = _dot_nt(dproj_ref[...], win_ref[...])
        dxa, dg1 = _rms_bwd(dhn, x_ref[...], g1_ref[...])
        dg1_ref[...] += dg1
        gx_ref[...] = dx1_ref[...] + dxa

    outs = [(D_MODEL, F32), (HP, BF16), (Q_RANK, BF16), (KV_RANK, BF16), (P_END, BF16)]
    accs = [D_MODEL, Q_RANK, KV_RANK]
    return pl.pallas_call(
        body, name="proj_bwd", grid=(l // tl,),
        in_specs=[_row(tl, D_MODEL), _row(tl, D_MODEL), _row(tl, Q_RANK), _row(tl, KV_RANK), _row(tl, HP),
                  _row(tl, HP), _row(tl, HP), _row(tl, SSM_W), _row(tl, 2 * D_MODEL),
                  _const((1, D_MODEL)), _const((D_MODEL, P_END)), _const((1, Q_RANK)), _const((Q_RANK, HP)),
                  _const((1, KV_RANK)), _const((KV_RANK, 2 * HP)), _row(tl, HEAD_SLOT), _row(tl, HEAD_SLOT)],
        out_specs=[_row(tl, n) for n, _ in outs] + [_const((1, n)) for n in accs],
        out_shape=[jax.ShapeDtypeStruct((l, n), dt) for n, dt in outs]
        + [jax.ShapeDtypeStruct((1, n), F32) for n in accs],
        compiler_params=_params(("arbitrary",)),
    )(x, dx1, cq, ckv, dq, dk, dv, du, dgl, g1, win, gq, wuq, gkv, wukv, rc, rs)


CONV_CB = 256
CONV_NB = D_FF // CONV_CB
CONV_ROWS = 16


def _conv3(h, halo, w, b):
    return b + w[0:1, :] * _shift_down(h, 2, halo) + w[1:2, :] * _shift_down(h, 1, halo) + w[2:3, :] * h


def _conv_fwd(h, cw, cb, tl):
    l = h.shape[0]

    def body(hg_ref, hv_ref, wg_ref, wv_ref, bg_ref, bv_ref, act_ref, halo_ref):
        @pl.when(pl.program_id(1) == 0)
        def _():
            halo_ref[...] = jnp.zeros_like(halo_ref)

        hg = hg_ref[...]
        hv = hv_ref[...]
        cg = _conv3(hg, halo_ref[0:SUBLANES, :], wg_ref[...], bg_ref[...])
        cv = _conv3(hv, halo_ref[SUBLANES:, :], wv_ref[...], bv_ref[...])
        act_ref[...] = (_gelu(cg) * cv).astype(BF16)
        halo_ref[0:SUBLANES, :] = hg[tl - SUBLANES:, :]
        halo_ref[SUBLANES:, :] = hv[tl - SUBLANES:, :]

    gmap = lambda c, r: (r, c)
    vmap = lambda c, r: (r, CONV_NB + c)
    return pl.pallas_call(
        body, name="conv_fwd", grid=(CONV_NB, l // tl),
        in_specs=[pl.BlockSpec((tl, CONV_CB), gmap), pl.BlockSpec((tl, CONV_CB), vmap),
                  pl.BlockSpec((3, CONV_CB), lambda c, r: (0, c)), pl.BlockSpec((3, CONV_CB), lambda c, r: (0, CONV_NB + c)),
                  pl.BlockSpec((1, CONV_CB), lambda c, r: (0, c)), pl.BlockSpec((1, CONV_CB), lambda c, r: (0, CONV_NB + c))],
        out_specs=pl.BlockSpec((tl, CONV_CB), gmap),
        out_shape=jax.ShapeDtypeStruct((l, D_FF), BF16),
        scratch_shapes=[pltpu.VMEM((2 * SUBLANES, CONV_CB), F32)],
        compiler_params=_params(("parallel", "arbitrary")),
    )(h, h, cw, cw, cb, cb)


def _conv_bwd(h, dact, cw, cb, tl):
    l = h.shape[0]
    nr = l // tl
    tpb = tl // SUBLANES

    def body(hg_ref, hv_ref, hgh_ref, hvh_ref, da_ref, wg_ref, wv_ref, bg_ref, bv_ref,
             dh_ref, dwg_ref, dwv_ref, dbg_ref, dbv_ref, car_ref):
        r = pl.program_id(1)

        @pl.when(r == 0)
        def _():
            for ref in (car_ref, dwg_ref, dwv_ref, dbg_ref, dbv_ref):
                ref[...] = jnp.zeros_like(ref)

        keep = jnp.where(r == nr - 1, 0.0, 1.0)
        wg, wv, bg, bv = wg_ref[...], wv_ref[...], bg_ref[...], bv_ref[...]
        nch = tl // CONV_ROWS

        def fold(x):
            s = x[0:SUBLANES, :]
            for k in range(1, CONV_ROWS // SUBLANES):
                s = s + x[k * SUBLANES:(k + 1) * SUBLANES, :]
            return s

        def chunk(n, carry):
            ncg, ncv, acc = carry
            idx = nch - 1 - n
            r0 = pl.multiple_of(idx * CONV_ROWS, CONV_ROWS)
            rows = pl.ds(r0, CONV_ROWS)
            before = pl.ds(pl.multiple_of(jnp.maximum(r0 - SUBLANES, 0), SUBLANES), SUBLANES)
            in_tile = idx > 0
            da = da_ref[rows, :].astype(F32)

            def half(h_ref, halo_ref, w, b):
                hh = h_ref[rows, :]
                prev = jnp.where(in_tile, h_ref[before, :], halo_ref[...] * keep)
                h1 = _shift_down(hh, 1, prev)
                h2 = _shift_down(hh, 2, prev)
                return hh, h1, h2, b + w[0:1, :] * h2 + w[1:2, :] * h1 + w[2:3, :] * hh

            hg, hg1, hg2, cg = half(hg_ref, hgh_ref, wg, bg)
            hv, hv1, hv2, cv = half(hv_ref, hvh_ref, wv, bv)
            dcg = da * cv * _gelu_grad(cg)
            dcv = da * _gelu(cg)

            def back(dc, hh, h1, h2, w, nxt, part):
                dh = w[2:3, :] * dc + w[1:2, :] * _shift_up(dc, 1, nxt) + w[0:1, :] * _shift_up(dc, 2, nxt)
                dh_ref[part, rows, :] = dh.astype(BF16)
                return [fold(dc * h2), fold(dc * h1), fold(dc * hh), fold(dc)]

            sums = back(dcg, hg, hg1, hg2, wg, ncg, 0) + back(dcv, hv, hv1, hv2, wv, ncv, 1)
            return dcg[0:SUBLANES, :], dcv[0:SUBLANES, :], [a + s for a, s in zip(acc, sums)]

        zero = jnp.zeros((SUBLANES, CONV_CB), F32)
        ncg, ncv, acc = lax.fori_loop(0, nch, chunk, (car_ref[0:SUBLANES, :], car_ref[SUBLANES:, :], [zero] * 8))
        car_ref[0:SUBLANES, :] = ncg
        car_ref[SUBLANES:, :] = ncv
        for half_acc, dw_ref, db_ref in ((acc[0:4], dwg_ref, dbg_ref), (acc[4:8], dwv_ref, dbv_ref)):
            for k in range(3):
                dw_ref[k:k + 1, :] += jnp.sum(half_acc[k], axis=0, keepdims=True)
            db_ref[...] += jnp.sum(half_acc[3], axis=0, keepdims=True)

    grev = lambda c, r: (nr - 1 - r, c)
    vrev = lambda c, r: (nr - 1 - r, CONV_NB + c)
    ghalo = lambda c, r: (jnp.maximum((nr - 1 - r) * tpb - 1, 0), c)
    vhalo = lambda c, r: (jnp.maximum((nr - 1 - r) * tpb - 1, 0), CONV_NB + c)
    colg = lambda c, r: (0, c)
    colv = lambda c, r: (0, CONV_NB + c)
    return pl.pallas_call(
        body, name="conv_bwd", grid=(CONV_NB, nr),
        in_specs=[pl.BlockSpec((tl, CONV_CB), grev), pl.BlockSpec((tl, CONV_CB), vrev),
                  pl.BlockSpec((SUBLANES, CONV_CB), ghalo), pl.BlockSpec((SUBLANES, CONV_CB), vhalo),
                  pl.BlockSpec((tl, CONV_CB), grev),
                  pl.BlockSpec((3, CONV_CB), colg), pl.BlockSpec((3, CONV_CB), colv),
                  pl.BlockSpec((1, CONV_CB), colg), pl.BlockSpec((1, CONV_CB), colv)],
        out_specs=[pl.BlockSpec((2, tl, CONV_CB), lambda c, r: (0, nr - 1 - r, c)),
                   pl.BlockSpec((3, CONV_CB), colg), pl.BlockSpec((3, CONV_CB), colg),
                   pl.BlockSpec((1, CONV_CB), colg), pl.BlockSpec((1, CONV_CB), colg)],
        out_shape=[jax.ShapeDtypeStruct((2, l, D_FF), BF16),
                   jax.ShapeDtypeStruct((3, D_FF), F32), jax.ShapeDtypeStruct((3, D_FF), F32),
                   jax.ShapeDtypeStruct((1, D_FF), F32), jax.ShapeDtypeStruct((1, D_FF), F32)],
        scratch_shapes=[pltpu.VMEM((2 * SUBLANES, CONV_CB), F32)],
        compiler_params=_params(("parallel", "arbitrary")),
    )(h, h, h, h, dact, cw, cw, cb, cb)


def _loss_head(ff, x1, tgt, g, tl):
    l = ff.shape[0]

    def body(ff_ref, x1_ref, tg_ref, g_ref, loss_ref, dx2_ref, dff_ref, dg_ref):
        @pl.when(pl.program_id(0) == 0)
        def _():
            loss_ref[...] = jnp.zeros_like(loss_ref)
            dg_ref[...] = jnp.zeros_like(dg_ref)

        f = ff_ref[...]
        gv = g_ref[...]
        n, _ = _rms(f, gv)
        e = x1_ref[...] + n - tg_ref[...]
        loss_ref[...] += 0.5 * jnp.sum(jnp.mean(e * e, axis=-1, keepdims=True), axis=0, keepdims=True)
        dx2 = e * (1.0 / D_MODEL)
        dx2_ref[...] = dx2
        dff, dg = _rms_bwd(dx2, f, gv)
        dff_ref[...] = dff.astype(BF16)
        dg_ref[...] += dg

    return pl.pallas_call(
        body, name="loss_head", grid=(l // tl,),
        in_specs=[_row(tl, D_MODEL), _row(tl, D_MODEL), _row(tl, D_MODEL), _const((1, D_MODEL))],
        out_specs=[_const((1, LANES)), _row(tl, D_MODEL), _row(tl, D_MODEL), _const((1, D_MODEL))],
        out_shape=[jax.ShapeDtypeStruct((1, LANES), F32), jax.ShapeDtypeStruct((l, D_MODEL), F32),
                   jax.ShapeDtypeStruct((l, D_MODEL), BF16), jax.ShapeDtypeStruct((1, D_MODEL), F32)],
        compiler_params=_params(("arbitrary",)),
    )(ff, x1, tgt, g)


def _ssm_disc(lam_re, lam_im, log_dt, b_re, b_im):
    dt = jnp.exp(log_dt)[:, None]
    mag = jnp.exp(lam_re * dt)
    ang = lam_im * dt
    a_re, a_im = mag * jnp.cos(ang), mag * jnp.sin(ang)
    den = lam_re * lam_re + lam_im * lam_im
    n_re, n_im = a_re - 1.0, a_im
    z_re = (n_re * lam_re + n_im * lam_im) / den
    z_im = (n_im * lam_re - n_re * lam_im) / den
    bb_re = z_re[..., None] * b_re - z_im[..., None] * b_im
    bb_im = z_re[..., None] * b_im + z_im[..., None] * b_re
    return a_re, a_im, bb_re, bb_im


_GPB = SSM_CB // SSM_P


def _embed_b(bb):
    t = bb.transpose(0, 2, 1).reshape(SSM_NB, _GPB, SSM_H, SSM_P)
    return jnp.einsum('mjhp,jk->mjhkp', t, jnp.eye(_GPB, dtype=bb.dtype)).reshape(SSM_NB, SSM_UB, SSM_CB)


def _extract_b(d):
    t = d.reshape(SSM_NB, _GPB, SSM_H, _GPB, SSM_P)
    t = jnp.einsum('mjhkp,jk->mjhp', t, jnp.eye(_GPB, dtype=d.dtype))
    return t.reshape(SSM_G, SSM_H, SSM_P).transpose(0, 2, 1)


def _embed_c(c):
    t = c.transpose(0, 2, 1).reshape(SSM_NB, _GPB, SSM_P, SSM_H)
    return jnp.einsum('mjph,jk->mjpkh', t, jnp.eye(_GPB, dtype=c.dtype)).reshape(SSM_NB, SSM_CB, SSM_UB)


def _extract_c(d):
    t = d.reshape(SSM_NB, _GPB, SSM_P, _GPB, SSM_H)
    t = jnp.einsum('mjpkh,jk->mjph', t, jnp.eye(_GPB, dtype=d.dtype))
    return t.reshape(SSM_G, SSM_P, SSM_H).transpose(0, 2, 1)


def _scan_tables(a_re, a_im, n_tiles, reverse):
    ar = a_re.reshape(1, SSM_CH)
    ai = (-a_im if reverse else a_im).reshape(1, SSM_CH)
    k = jnp.arange(1, n_tiles + 1)[:, None]
    pr, pi = jnp.ones((n_tiles, SSM_CH), F32), jnp.zeros((n_tiles, SSM_CH), F32)
    sr, si = ar, ai
    for bit in range(n_tiles.bit_length()):
        use = ((k >> bit) & 1) == 1
        pr, pi = jnp.where(use, pr * sr - pi * si, pr), jnp.where(use, pr * si + pi * sr, pi)
        sr, si = sr * sr - si * si, 2.0 * sr * si
    on8 = lambda v: jnp.broadcast_to(v, (SUBLANES, SSM_CH))
    ab = jnp.stack([on8(ar), on8(ai), on8(pr[-1:]), on8(pi[-1:])])
    if reverse:
        pr, pi = pr[::-1], pi[::-1]
    pw = jnp.stack([jnp.repeat(pr, SUBLANES, axis=0), jnp.repeat(pi, SUBLANES, axis=0)])
    return ab.astype(F32), pw.astype(F32)


def _pad_heads(w, d):
    lead = w.shape[:-1]
    w = w.reshape(lead + (N_HEADS, d))
    w = jnp.pad(w, [(0, 0)] * len(lead) + [(0, 0), (0, HEAD_SLOT - d)])
    return w.reshape(lead + (HP,))


def _unpad_heads(w, d):
    lead = w.shape[:-1]
    return w.reshape(lead + (N_HEADS, HEAD_SLOT))[..., :d].reshape(lead + (N_HEADS * d,))


def _chip_major(w, axis):
    k, n = w.shape
    if axis == 0:
        return w.reshape(N_CHIPS, k // N_CHIPS, n)
    return w.reshape(k, N_CHIPS, n // N_CHIPS).transpose(1, 0, 2)


def _from_chip_major(w, axis):
    if axis == 0:
        return w.reshape(-1, w.shape[2])
    return w.transpose(1, 0, 2).reshape(w.shape[1], -1)


def _pad_w_in(w):
    z = lambda n: jnp.zeros((w.shape[0], n), w.dtype)
    return jnp.concatenate([w[:, :640], z(KR_LANE), w[:, 640:672], z(HEAD_SLOT - KR_LANE - QK_ROPE), w[:, 672:]], axis=1)


def _unpad_w_in(w):
    return jnp.concatenate([w[:, :640], w[:, P_KR + KR_LANE:P_KR + KR_LANE + QK_ROPE], w[:, P_U:]], axis=1)


def _local_step(x, positions, tgt, wts, sp):
    l = x.shape[0]
    tl = min(256, l)
    ta = min(512, l)

    inv_freq = ROPE_THETA ** (-jnp.arange(0, QK_ROPE, 2, dtype=F32) / QK_ROPE)
    ang = positions.astype(F32)[:, None] * inv_freq
    cos, sin = jnp.cos(ang), jnp.sin(ang)
    one = jnp.ones((l, KR_LANE), F32)
    rc = jnp.concatenate([one, cos, cos, jnp.ones((l, HEAD_SLOT - KR_LANE - QK_ROPE), F32)], axis=1)
    rs = jnp.concatenate([0 * one, -sin, sin, jnp.zeros((l, HEAD_SLOT - KR_LANE - QK_ROPE), F32)], axis=1)

    win = _pad_w_in(wts["w_in"])
    wuq = _pad_heads(wts["w_uq"], QK_HEAD)
    wukv = jnp.concatenate([_pad_heads(wts["w_uk"], QK_NOPE), _pad_heads(wts["w_uv"], V_HEAD)], axis=1)

    disc_in = (sp["ssm_lambda_re"], sp["ssm_lambda_im"], sp["ssm_log_dt"], sp["ssm_b_re"], sp["ssm_b_im"])
    (a_re, a_im, bb_re, bb_im), disc_vjp = jax.vjp(_ssm_disc, *disc_in)
    bre, bim = _embed_b(bb_re).astype(BF16), _embed_b(bb_im).astype(BF16)
    cre, cim = _embed_c(sp["ssm_c_re"]).astype(BF16), _embed_c(sp["ssm_c_im"]).astype(BF16)
    dvec = sp["ssm_d"].reshape(1, SSM_W)
    ab_f, pw_f = _scan_tables(a_re, a_im, ta // SUBLANES, False)
    ab_r, pw_r = _scan_tables(a_re, a_im, ta // SUBLANES, True)

    g1, gq, gkv = sp["mix_norm_pre"], sp["q_norm"], sp["kv_norm"]
    gpost, gpre, gfin = sp["mix_norm_post"], sp["ffn_norm_pre"], sp["ffn_norm_post"]
    bgate, bglu, convb = sp["b_gate"], sp["b_glu"], sp["conv_b"]

    hn, cq, ckv, q, k, v, u, gl = _proj_fwd(x, g1, win, gq, wuq, gkv, wukv, rc, rs, bgate, tl)
    attn, lse = _attn_fwd(q, k, v, ta)
    u_seg = _to_segments(u, ta)
    y1_seg, sre, sim = _ssm_fwd(u_seg, bre, bim, cre, cim, dvec, ab_f, pw_f, ta)
    y1 = _from_segments(y1_seg, ta)
    wba = jnp.pad(wts["w_branch_attn"].reshape(N_HEADS, V_HEAD, D_MODEL),
                  ((0, 0), (0, HEAD_SLOT - V_HEAD), (0, 0))).reshape(HP, D_MODEL)
    wbs, wglu, wout = wts["w_branch_ssm"], wts["w_glu"], wts["w_out"]
    a, sm, merged, z, x1, hn2, y3 = _merge_fwd(x, gl, attn, y1, wba, wbs, wglu, bglu, wout, gpost, gpre, tl)
    late = wts["late"](x1)
    wup, wdown, convw = late["w_up"], late["w_down"], late["conv_w"]
    h = _mm(hn2, wup, "ffn_up")
    act = _conv_fwd(h, convw, convb, ta)
    ff = _mm(act, wdown, "ffn_down")
    loss, dx2, dff, dgfin = _loss_head(ff, x1, tgt, gfin, tl)

    dact = _mm(dff, wdown, "ffn_down_dx", out_dtype=BF16, bt=True)
    d_wdown = _mm_tn(act, dff, "ffn_down_dw", tk_cap=D_FF // 2)
    dh, dwg, dwv, dbg, dbv = _conv_bwd(h, dact, convw, convb, ta)
    d_convw = jnp.concatenate([dwg, dwv], axis=1)
    d_convb = jnp.concatenate([dbg, dbv], axis=1)
    dhn2a = _mm(dh, wup, "ffn_up_dx_gate", bt=True, b_col0=0, a_lead=0)
    dhn2b = _mm(dh, wup, "ffn_up_dx_val", bt=True, b_col0=1, a_lead=1)
    d_wup = _mm_tn(hn2, dh, "ffn_up_dw", chips=True)
    behind = wts["send_grads"]("ffn", {"w_up": d_wup, "w_down": _chip_major(d_wdown, 0)})
    (dx1, dz, dbra, dbrs, dgl, dattn, dy1, dt, y2, dgpre, dgpost, dbgate, dbglu) = _merge_bwd(
        dhn2a, dhn2b, x1, dx2, z, gl, a, sm, y1, wba, wbs, wglu, bglu, wout, gpost, gpre + behind, tl)
    d_wout = _mm_tn(merged, dz, "w_out_dw")
    d_wba = _mm_tn(attn, dbra, "w_branch_attn_dw", chips=True)
    d_wbs = _mm_tn(y3, dbrs, "w_branch_ssm_dw", chips=True)
    d_wglu = _mm_tn(y2, dt, "w_glu_dw")
    ncol = D_MODEL // N_CHIPS
    behind = wts["send_grads"]("mix", {
        "w_glu": _chip_major(d_wglu, 0),
        "w_branch_attn": d_wba.reshape(N_CHIPS, N_HEADS, HEAD_SLOT, ncol)[:, :, :V_HEAD].reshape(
            N_CHIPS, N_HEADS * V_HEAD, ncol),
        "w_branch_ssm": d_wbs,
        "w_out": _chip_major(d_wout, 0)}, after=d_wglu)
    dq, dk, dv = _attn_bwd(q, k, v, dattn, lse + behind, _attn_delta(attn, dattn, min(2048, l)), ta)
    du_seg, dbre, dbim, dcre, dcim, dare, daim, dd = _ssm_bwd(
        _to_segments(dy1, ta), u_seg, sre, sim, bre, bim, cre, cim, dvec, ab_r, pw_r, ta)
    du = _from_segments(du_seg, ta)
    behind = wts["send_grads"]("none", {}, after=du)
    gx, dql, qn, ckvn, dproj, dg1, dgq, dgkv = _proj_bwd(
        x, dx1, cq, ckv, dq, dk, dv, du, dgl, g1 + behind, win, gq, wuq, gkv, wukv, rc, rs, tl)
    d_win = _mm_tn(hn, dproj, "w_in_dw")
    d_wuq = _mm_tn(qn, dql, "w_uq_dw")
    d_wuk = _mm_tn(ckvn, dk, "w_uk_dw")
    d_wuv = _mm_tn(ckvn, dv, "w_uv_dw")

    d_lre, d_lim, d_ldt, d_bre, d_bim = disc_vjp((dare.reshape(SSM_G, SSM_P), daim.reshape(SSM_G, SSM_P),
                                                  _extract_b(dbre), _extract_b(dbim)))
    big = {
        "w_in": _chip_major(_unpad_w_in(d_win), 1),
        "w_uq": _chip_major(_unpad_heads(d_wuq, QK_HEAD), 1),
        "w_uk": _chip_major(_unpad_heads(d_wuk, QK_NOPE), 1),
        "w_uv": _chip_major(_unpad_heads(d_wuv, V_HEAD), 1),
    }
    small = {
        "conv_w": d_convw,
        "mix_norm_pre": dg1, "q_norm": dgq, "kv_norm": dgkv,
        "ssm_lambda_re": d_lre, "ssm_lambda_im": d_lim, "ssm_log_dt": d_ldt,
        "ssm_b_re": d_bre, "ssm_b_im": d_bim,
        "ssm_c_re": _extract_c(dcre), "ssm_c_im": _extract_c(dcim),
        "ssm_d": dd.reshape(SSM_G, SSM_H), "b_glu": dbglu, "b_gate": dbgate,
        "mix_norm_post": dgpost, "ffn_norm_pre": dgpre, "conv_b": d_convb, "ffn_norm_post": dgfin,
    }
    return loss[0, 0], gx, big, small


_ANY = pl.BlockSpec(memory_space=pl.ANY)


ROW_TILE = 16


def _place():
    x, y, c = lax.axis_index("x"), lax.axis_index("y"), lax.axis_index("c")
    return x, y, c, 2 * x + y, [(1 - x, y), (x, 1 - y), (1 - x, 1 - y)]


def _half(rows, which):
    hr = rows // 2
    return pl.ds(pl.multiple_of(which * hr, ROW_TILE), hr)


def _remote(src, dst, send_sems, recv_sems, n, dev):
    return pltpu.make_async_remote_copy(src_ref=src, dst_ref=dst, send_sem=send_sems.at[n], recv_sem=recv_sems.at[n],
                                        device_id=dev, device_id_type=MESH)


def _gather_big(shards):
    nw = len(shards)
    rows = [s.shape[0] for s in shards]

    def body(*refs):
        ins, outs = refs[:nw], refs[nw:2 * nw]
        ici_send, ici_recv, d2d_send, d2d_recv = refs[2 * nw:]
        x, y, c, me, peers = _place()
        sent = []
        for i in range(nw):
            for p, (px, py) in enumerate(peers):
                cp = _remote(ins[i].at[_half(rows[i], c)], outs[i].at[me, _half(rows[i], c)], ici_send, ici_recv,
                             3 * i + p, (px, py, c))
                cp.start()
                sent.append(cp)
        for p, (px, py) in enumerate(peers):
            for i in range(nw):
                blk = outs[i].at[2 * px + py, _half(rows[i], c)]
                _remote(blk, blk, ici_send, ici_recv, 3 * i + p, (px, py, c)).wait_recv()
                cp = _remote(blk, blk, d2d_send, d2d_recv, 3 * i + p, (x, y, 1 - c))
                cp.start()
                sent.append(cp)
        for p, (px, py) in enumerate(peers):
            for i in range(nw):
                blk = outs[i].at[2 * px + py, _half(rows[i], 1 - c)]
                _remote(blk, blk, d2d_send, d2d_recv, 3 * i + p, (x, y, 1 - c)).wait_recv()
        for cp in sent:
            cp.wait_send()

    dma = pltpu.SemaphoreType.DMA
    return pl.pallas_call(
        body, name="gather_weights", in_specs=[_ANY] * nw, out_specs=[_ANY] * nw,
        out_shape=[jax.ShapeDtypeStruct((N_CHIPS,) + s.shape, s.dtype) for s in shards],
        scratch_shapes=[dma((3 * nw,)), dma((3 * nw,)), dma((3 * nw,)), dma((3 * nw,))],
    )(*shards)


_HBM = pl.BlockSpec(memory_space=pltpu.HBM)
_SEM = pl.BlockSpec(memory_space=pltpu.SEMAPHORE)
_DATAFLOW = pltpu.SideEffectType.DATAFLOW_SIDE_EFFECTING


def _exchange_start(shards, name, scatter):
    nw = len(shards)
    lands = [lax.empty(s.shape if scatter else (N_CHIPS,) + s.shape, s.dtype) for s in shards]

    def body(*refs):
        ins, zones = refs[:nw], refs[nw:2 * nw]
        send_sems, recv_sems, token = refs[2 * nw], refs[2 * nw + 1], refs[-1]
        x, y, c, me, peers = _place()
        for i in range(nw):
            for p, (px, py) in enumerate(peers):
                src = ins[i].at[2 * px + py] if scatter else ins[i]
                _remote(src, zones[i].at[me], send_sems, recv_sems, 3 * i + p, (px, py, c)).start()
        token[...] = jnp.zeros_like(token)

    thru = [pltpu.HBM(a.shape, a.dtype) for a in list(shards) + lands]
    dma = pltpu.SemaphoreType.DMA
    outs = pl.pallas_call(
        body, name=name,
        out_shape=(dma((3 * nw,)), dma((3 * nw,)), *thru, jax.ShapeDtypeStruct((SUBLANES, LANES), F32)),
        in_specs=[_HBM] * (2 * nw),
        out_specs=(_SEM, _SEM, *([_HBM] * (2 * nw)), pl.BlockSpec(memory_space=pltpu.VMEM)),
        input_output_aliases={i: 2 + i for i in range(2 * nw)},
        compiler_params=pltpu.CompilerParams(has_side_effects=_DATAFLOW),
    )(*[pltpu.with_memory_space_constraint(a, pltpu.HBM) for a in list(shards) + lands])
    return outs[0], outs[1], list(outs[2:2 + nw]), list(outs[2 + nw:2 + 2 * nw]), outs[-1]


def _exchange_wait(send_sems, recv_sems, shards, lands, after, name, scatter):
    nw = len(shards)

    def body(*refs):
        ins, zones = refs[:nw], refs[nw:2 * nw]
        send_sems, recv_sems = refs[2 * nw], refs[2 * nw + 1]
        x, y, c, me, peers = _place()
        for i in range(nw):
            for p, (px, py) in enumerate(peers):
                src = ins[i].at[2 * px + py] if scatter else ins[i]
                cp = _remote(src, zones[i].at[2 * px + py], send_sems, recv_sems, 3 * i + p, (px, py, c))
                cp.wait_send()
                cp.wait_recv()

    both = list(shards) + list(lands)
    outs = pl.pallas_call(
        body, name=name,
        out_shape=tuple(pltpu.HBM(a.shape, a.dtype) for a in both),
        in_specs=(*([_HBM] * (2 * nw)), _SEM, _SEM, _ANY), out_specs=[_HBM] * (2 * nw),
        input_output_aliases={i: i for i in range(2 * nw)},
        compiler_params=pltpu.CompilerParams(has_side_effects=_DATAFLOW),
    )(*both, send_sems, recv_sems, after)
    return list(outs[:nw]), list(outs[nw:])


def _sibling_start(grads, name):
    nw = len(grads)
    lands = [lax.empty((N_CHIPS, g.shape[1] // 2, g.shape[2]), g.dtype) for g in grads]

    def body(*refs):
        ins, zones = refs[:nw], refs[nw:2 * nw]
        send_sems, recv_sems, token = refs[2 * nw], refs[2 * nw + 1], refs[-1]
        x, y, c, _, _ = _place()
        for i in range(nw):
            _remote(ins[i].at[pl.ds(0, N_CHIPS), _half(grads[i].shape[1], 1 - c)], zones[i], send_sems, recv_sems,
                    i, (x, y, 1 - c)).start()
        token[...] = jnp.zeros_like(token)

    thru = [pltpu.HBM(a.shape, a.dtype) for a in list(grads) + lands]
    dma = pltpu.SemaphoreType.DMA
    outs = pl.pallas_call(
        body, name=name,
        out_shape=(dma((nw,)), dma((nw,)), *thru, jax.ShapeDtypeStruct((SUBLANES, LANES), F32)),
        in_specs=[_HBM] * (2 * nw),
        out_specs=(_SEM, _SEM, *([_HBM] * (2 * nw)), pl.BlockSpec(memory_space=pltpu.VMEM)),
        input_output_aliases={i: 2 + i for i in range(2 * nw)},
        compiler_params=pltpu.CompilerParams(has_side_effects=_DATAFLOW),
    )(*[pltpu.with_memory_space_constraint(a, pltpu.HBM) for a in list(grads) + lands])
    return outs[0], outs[1], list(outs[2:2 + nw]), list(outs[2 + nw:2 + 2 * nw]), outs[-1]


def _sibling_wait(send_sems, recv_sems, grads, lands, after, name):
    nw = len(grads)

    def body(*refs):
        ins, zones = refs[:nw], refs[nw:2 * nw]
        send_sems, recv_sems = refs[2 * nw], refs[2 * nw + 1]
        x, y, c, _, _ = _place()
        for i in range(nw):
            cp = _remote(ins[i].at[pl.ds(0, N_CHIPS), _half(grads[i].shape[1], 1 - c)], zones[i], send_sems, recv_sems,
                         i, (x, y, 1 - c))
            cp.wait_send()
            cp.wait_recv()

    both = list(grads) + list(lands)
    outs = pl.pallas_call(
        body, name=name,
        out_shape=tuple(pltpu.HBM(a.shape, a.dtype) for a in both),
        in_specs=(*([_HBM] * (2 * nw)), _SEM, _SEM, _ANY), out_specs=[_HBM] * (2 * nw),
        input_output_aliases={i: i for i in range(2 * nw)},
        compiler_params=pltpu.CompilerParams(has_side_effects=_DATAFLOW),
    )(*both, send_sems, recv_sems, after)
    return list(outs[:nw]), list(outs[nw:])


def _reduce_to_sibling(grads, name):
    nw = len(grads)

    def body(*refs):
        ins, outs = refs[:nw], refs[nw:2 * nw]
        send_sems, recv_sems = refs[2 * nw:]
        x, y, c, _, _ = _place()
        sent = []
        for i in range(nw):
            cp = _remote(ins[i].at[pl.ds(0, N_CHIPS), _half(grads[i].shape[1], 1 - c)], outs[i], send_sems, recv_sems,
                         i, (x, y, 1 - c))
            cp.start()
            sent.append(cp)
        for cp in sent:
            cp.wait()

    dma = pltpu.SemaphoreType.DMA
    return pl.pallas_call(
        body, name=name, in_specs=[_ANY] * nw, out_specs=[_ANY] * nw,
        out_shape=[jax.ShapeDtypeStruct((N_CHIPS, g.shape[1] // 2, g.shape[2]), g.dtype) for g in grads],
        scratch_shapes=[dma((nw,)), dma((nw,))],
    )(*grads)


def _reduce_back(totals, name):
    nw = len(totals)

    def body(*refs):
        outs = refs[nw:2 * nw]
        send_sems, recv_sems = refs[2 * nw:]
        x, y, c, _, _ = _place()
        sent = []
        for i in range(nw):
            blk = outs[i].at[_half(totals[i].shape[0], c)]
            cp = _remote(blk, blk, send_sems, recv_sems, i, (x, y, 1 - c))
            cp.start()
            sent.append(cp)
        for i in range(nw):
            blk = outs[i].at[_half(totals[i].shape[0], 1 - c)]
            _remote(blk, blk, send_sems, recv_sems, i, (x, y, 1 - c)).wait_recv()
        for cp in sent:
            cp.wait_send()

    dma = pltpu.SemaphoreType.DMA
    return pl.pallas_call(
        body, name=name, in_specs=[_ANY] * nw, out_specs=[_ANY] * nw,
        out_shape=[jax.ShapeDtypeStruct(t.shape, t.dtype) for t in totals],
        input_output_aliases={i: i for i in range(nw)},
        scratch_shapes=[dma((nw,)), dma((nw,))],
    )(*totals)


def _all_reduce_small(v, name):
    rows, w = v.shape
    hr = rows // 2
    assert hr % SUBLANES == 0

    def body(v_ref, out_ref, sib_ref, half_ref, chips_ref, send_sems, recv_sems):
        x, y, c, me, peers = _place()
        sibling = (x, y, 1 - c)
        mine = pl.ds(pl.multiple_of(c * hr, SUBLANES), hr)
        other = pl.ds(pl.multiple_of((1 - c) * hr, SUBLANES), hr)
        cp = _remote(v_ref, sib_ref, send_sems, recv_sems, 0, sibling)
        cp.start()
        cp.wait()
        half_ref[...] = v_ref[mine, :] + sib_ref[mine, :]
        sent = []
        for p, (px, py) in enumerate(peers):
            cp = _remote(half_ref, chips_ref.at[me], send_sems, recv_sems, 1 + p, (px, py, c))
            cp.start()
            sent.append(cp)
        chips_ref[me] = half_ref[...]
        for p, (px, py) in enumerate(peers):
            _remote(half_ref, chips_ref.at[2 * px + py], send_sems, recv_sems, 1 + p, (px, py, c)).wait_recv()
        for cp in sent:
            cp.wait_send()
        out_ref[mine, :] = ((chips_ref[0] + chips_ref[1]) + chips_ref[2]) + chips_ref[3]
        cp = _remote(out_ref.at[mine], out_ref.at[mine], send_sems, recv_sems, 4, sibling)
        cp.start()
        _remote(out_ref.at[other], out_ref.at[other], send_sems, recv_sems, 4, sibling).wait_recv()
        cp.wait_send()

    vm = pl.BlockSpec(memory_space=pltpu.VMEM)
    return pl.pallas_call(
        body, name=name, in_specs=[vm], out_specs=vm,
        out_shape=jax.ShapeDtypeStruct((rows, w), F32),
        scratch_shapes=[pltpu.VMEM((rows, w), F32), pltpu.VMEM((hr, w), F32), pltpu.VMEM((N_CHIPS, hr, w), F32),
                        pltpu.SemaphoreType.DMA((5,)), pltpu.SemaphoreType.DMA((5,))],
        compiler_params=pltpu.CompilerParams(vmem_limit_bytes=VMEM_LIMIT),
    )(v)


ELEMENTWISE_BLOCK = 512 * 1024


def _rows_tile(rows, cols):
    best = None
    for t in range(SUBLANES, rows + 1, SUBLANES):
        if rows % t == 0 and t * cols <= ELEMENTWISE_BLOCK:
            best = t
    return rows if best is None else best


def _add_pair(g, t, core, name):
    nb, n, w = t.shape
    tr = _rows_tile(n, w)
    steps = n // tr

    def body(core_ref, g_ref, t_ref, o_ref):
        o_ref[...] = (g_ref[...] + t_ref[...]).astype(BF16)

    spec = pl.BlockSpec((1, tr, w), lambda j, i, core_ref: (j, i, 0))
    return pl.pallas_call(
        body, name=name,
        grid_spec=pltpu.PrefetchScalarGridSpec(
            num_scalar_prefetch=1, grid=(nb, steps),
            in_specs=[pl.BlockSpec((1, tr, w), lambda j, i, core_ref: (j, core_ref[0] * steps + i, 0)), spec],
            out_specs=spec),
        out_shape=jax.ShapeDtypeStruct(t.shape, BF16),
        compiler_params=_params(("parallel", "parallel")))(core, g, t)


def _add_chips(landed, pairs, place, name):
    nb, n, w = landed.shape
    tr = _rows_tile(n, w)
    steps = n // tr

    def body(place_ref, r_ref, own_ref, o_ref):
        me = place_ref[0]
        acc = None
        for k in range(nb):
            blk = jnp.where(me == k, own_ref[0], r_ref[k]).astype(F32)
            acc = blk if acc is None else acc + blk
        o_ref[...] = acc

    return pl.pallas_call(
        body, name=name,
        grid_spec=pltpu.PrefetchScalarGridSpec(
            num_scalar_prefetch=1, grid=(steps,),
            in_specs=[pl.BlockSpec((nb, tr, w), lambda i, place_ref: (0, i, 0)),
                      pl.BlockSpec((1, tr, w), lambda i, place_ref: (place_ref[0], i, 0))],
            out_specs=pl.BlockSpec((tr, w), lambda i, place_ref: (place_ref[1] * steps + i, 0))),
        out_shape=jax.ShapeDtypeStruct((2 * n, w), F32),
        compiler_params=_params(("parallel",)))(place, landed, pairs)


def _adamw(w, g, m, v, name):
    rows, wd = w.shape
    tr = _rows_tile(rows, wd)
    c1 = 1.0 - ADAM_B1 ** ADAM_STEP
    c2 = 1.0 - ADAM_B2 ** ADAM_STEP

    def body(w_ref, g_ref, m_ref, v_ref, go_ref, d_ref, mo_ref, vo_ref):
        gv = g_ref[...]
        go_ref[...] = gv
        m2 = ADAM_B1 * m_ref[...] + (1.0 - ADAM_B1) * gv
        v2 = ADAM_B2 * v_ref[...] + (1.0 - ADAM_B2) * (gv * gv)
        mo_ref[...] = m2
        vo_ref[...] = v2
        d_ref[...] = -ADAM_LR * ((m2 / c1) / (jnp.sqrt(v2 / c2) + ADAM_EPS) + ADAM_WD * w_ref[...])

    spec = pl.BlockSpec((tr, wd), lambda i: (i, 0))
    shp = jax.ShapeDtypeStruct((rows, wd), F32)
    return pl.pallas_call(body, name=name, grid=(rows // tr,), in_specs=[spec] * 4, out_specs=[spec] * 4,
                          out_shape=[shp] * 4, compiler_params=_params(("parallel",)))(w, g, m, v)


BIG = [("w_in", (1024, 3232), 1), ("w_uq", (384, 768), 1), ("w_uk", (256, 512), 1), ("w_uv", (256, 512), 1),
       ("w_glu", (512, 512), 0), ("w_branch_attn", (512, 1024), 1), ("w_branch_ssm", (512, 1024), 1),
       ("w_out", (1024, 1024), 0), ("w_up", (1024, 5632), 1), ("conv_w", (3, 5632), 1), ("w_down", (2816, 1024), 0)]
SMALL = [("mix_norm_pre", (1024,)), ("q_norm", (384,)), ("kv_norm", (256,)), ("ssm_lambda_re", (32, 64)),
         ("ssm_lambda_im", (32, 64)), ("ssm_log_dt", (32,)), ("ssm_b_re", (32, 64, 16)), ("ssm_b_im", (32, 64, 16)),
         ("ssm_c_re", (32, 16, 64)), ("ssm_c_im", (32, 16, 64)), ("ssm_d", (32, 16)), ("b_glu", (512,)),
         ("b_gate", (2048,)), ("mix_norm_post", (1024,)), ("ffn_norm_pre", (1024,)), ("conv_b", (5632,)),
         ("ffn_norm_post", (1024,))]
MATMUL_W = [b for b in BIG if b[0] != "conv_w"]
LATE_W = ("w_up", "w_down", "conv_w")
CONV_W_SHAPE = (3, 2 * D_FF)
CONV_W_SHARD = (3, 2 * D_FF // N_CHIPS)
SMALL_SUM = [("loss", (1,))] + SMALL + [("conv_w", CONV_W_SHAPE)]
SMALL_ADAM = SMALL + [("conv_w", CONV_W_SHARD)]


def _pack_flat(layout, vals):
    flat = jnp.concatenate([vals[n].astype(F32).reshape(-1) for n, _ in layout])
    rows = -(-(-(-flat.shape[0] // FLAT_W)) // (2 * SUBLANES)) * 2 * SUBLANES
    return jnp.pad(flat, (0, rows * FLAT_W - flat.shape[0])).reshape(rows, FLAT_W)


def _unpack_flat(layout, flat):
    flat = flat.reshape(-1)
    out = {}
    o = 0
    for name, shape in layout:
        n = math.prod(shape)
        out[name] = flat[o:o + n].reshape(shape)
        o += n
    return out


_ARG_NAMES = ["x", "positions"] + [n for n in (
    "mix_norm_pre", "w_in", "q_norm", "w_uq", "kv_norm", "w_uk", "w_uv", "ssm_lambda_re", "ssm_lambda_im", "ssm_log_dt",
    "ssm_b_re", "ssm_b_im", "ssm_c_re", "ssm_c_im", "ssm_d", "w_glu", "b_glu", "w_branch_attn", "w_branch_ssm",
    "b_gate", "w_out", "mix_norm_post", "ffn_norm_pre", "w_up", "conv_w", "conv_b", "w_down", "ffn_norm_post")]
_WEIGHTS = _ARG_NAMES[2:]


def _gather_weights(w):
    early = [b for b in MATMUL_W if b[0] not in LATE_W]
    late = [b for b in BIG if b[0] in LATE_W]
    own = (jnp.arange(N_CHIPS) == 2 * lax.axis_index("x") + lax.axis_index("y"))[:, None, None]

    def whole(layout, mine, gathered):
        return {name: _from_chip_major(jnp.where(own, s[None], g), axis)
                for (name, _, axis), s, g in zip(layout, mine, gathered)}

    mine = [w[name].astype(BF16) for name, _, _ in early]
    full = whole(early, mine, _gather_big(mine))
    mine_late = [w[name].astype(F32 if name == "conv_w" else BF16) for name, _, _ in late]
    _, mine_late = lax.optimization_barrier((full["w_in"], mine_late))
    send_sems, recv_sems, shards_thru, lands_thru, token = _exchange_start(mine_late, "gather_late_start", scatter=False)

    def late_weights(after):
        shards, lands = _exchange_wait(send_sems, recv_sems, shards_thru, lands_thru, after, "gather_late_wait",
                                       scatter=False)
        return whole(late, shards, lands)

    full["late"] = late_weights
    full["token"] = token[0, 0]
    return full


def _pair_sums(names, grads, tag):
    core = lax.axis_index("c").astype(jnp.int32).reshape(1)
    theirs = _reduce_to_sibling(grads, "reduce_grads_d2d" + tag)
    return [_add_pair(g, t, core, "reduce_pair_" + n) for n, g, t in zip(names, grads, theirs)]


def _send_grads(tag, grads, after, flying, pending):
    token = jnp.zeros((), F32)
    if flying:
        tag0, names0, state0 = flying.pop()
        core = lax.axis_index("c").astype(jnp.int32).reshape(1)
        mine, theirs = _sibling_wait(*state0, after, "reduce_" + tag0 + "_d2d_wait")
        pairs = [_add_pair(g, t, core, "reduce_pair_" + n) for n, g, t in zip(names0, mine, theirs)]
        send_sems, recv_sems, pairs_thru, lands_thru, tok = _exchange_start(pairs, "reduce_" + tag0 + "_start",
                                                                          scatter=True)
        pending.append((tag0, names0, send_sems, recv_sems, pairs_thru, lands_thru))
        token = token + tok[0, 0]
    if grads:
        names = list(grads)
        send_sems, recv_sems, grads_thru, lands_thru, tok = _sibling_start([grads[n] for n in names],
                                                                          "reduce_" + tag + "_d2d_start")
        flying.append((tag, names, (send_sems, recv_sems, grads_thru, lands_thru)))
        token = token + tok[0, 0]
    return token


def _reduce_grads(gbig, pending, loss, gsmall, use_sent):
    core = lax.axis_index("c").astype(jnp.int32).reshape(1)
    chip = (2 * lax.axis_index("x") + lax.axis_index("y")).astype(jnp.int32).reshape(1)
    place = jnp.concatenate([chip, core])

    def finish(names, pairs, landed, name):
        totals = [_add_chips(r, p, place, "reduce_chips_" + n) for n, r, p in zip(names, landed, pairs)]
        return dict(zip(names, _reduce_back(totals, name)))

    names = list(gbig)
    pairs = _pair_sums(names, [gbig[n] for n in names], "")
    send_sems, recv_sems, pairs_thru, lands_thru, token = _exchange_start(pairs, "reduce_last_start", scatter=True)
    sent_names, sent_pairs, sent_landed = [], [], []
    for tag, group, g_send, g_recv, g_pairs, g_lands in pending:
        got_pairs, got_landed = _exchange_wait(g_send, g_recv, g_pairs, g_lands, token, "reduce_" + tag + "_wait",
                                               scatter=True)
        sent_names, sent_pairs, sent_landed = sent_names + group, sent_pairs + got_pairs, sent_landed + got_landed
    g_sent = finish(sent_names, sent_pairs, sent_landed, "reduce_back_sent")
    vals = dict(gsmall)
    vals["loss"] = loss
    small_red = _unpack_flat(SMALL_SUM, _all_reduce_small(_pack_flat(SMALL_SUM, vals), "reduce_small"))
    after = use_sent(g_sent, small_red)
    pairs, landed = _exchange_wait(send_sems, recv_sems, pairs_thru, lands_thru, after, "reduce_last_wait", scatter=True)
    return finish(names, pairs, landed, "reduce_back_last"), small_red


def _step(args):
    x = args["x"][0]
    positions = args["positions"][0]
    tgt = args["loss_target"][0]
    w = {n: args[n][0] for n in _WEIGHTS}
    m = {n: args["m_" + n][0] for n in _WEIGHTS}
    v = {n: args["v_" + n][0] for n in _WEIGHTS}

    full = _gather_weights(w)
    sp = {n: w[n].reshape(s) for n, s in SMALL}
    for n in ("mix_norm_pre", "q_norm", "kv_norm", "b_glu", "b_gate", "mix_norm_post", "ffn_norm_pre", "conv_b",
              "ffn_norm_post"):
        sp[n] = sp[n].reshape(1, -1)
    sp["mix_norm_pre"] = sp["mix_norm_pre"] + full.pop("token")
    pending, flying = [], []
    full["send_grads"] = lambda tag, grads, after=None: _send_grads(tag, grads, after, flying, pending)
    loss, gx, gbig, gsmall = _local_step(x, positions, tgt, full, sp)
    outs = {}

    def adam_big(g_red):
        for name in g_red:
            g2, d, m2, v2 = _adamw(w[name], g_red[name], m[name], v[name], "adamw_" + name)
            outs["grad_" + name], outs["delta_" + name], outs["new_m_" + name], outs["new_v_" + name] = g2, d, m2, v2
        return v2

    def use_sent(g_sent, small_red):
        chip = 2 * lax.axis_index("x") + lax.axis_index("y")
        grads = dict(small_red)
        grads["conv_w"] = lax.dynamic_slice_in_dim(small_red["conv_w"], chip * CONV_W_SHARD[1], CONV_W_SHARD[1], axis=1)
        outs.update({"grad_" + n: grads[n] for n, _ in SMALL_ADAM})
        _, d_sm, m_sm, v_sm = _adamw(_pack_flat(SMALL_ADAM, w), _pack_flat(SMALL_ADAM, grads),
                                     _pack_flat(SMALL_ADAM, m), _pack_flat(SMALL_ADAM, v), "adamw_small")
        for prefix, flat in (("delta_", d_sm), ("new_m_", m_sm), ("new_v_", v_sm)):
            for n, val in _unpack_flat(SMALL_ADAM, flat).items():
                outs[prefix + n] = val
        return adam_big(g_sent)

    g_last, small_red = _reduce_grads(gbig, pending, loss, gsmall, use_sent)
    adam_big(g_last)
    outs = {n: val.reshape(args[n.split("_", 1)[1] if not n.startswith("new_") else n[6:]].shape)
            for n, val in outs.items()}
    res = [small_red["loss"][0], gx[None]]
    for prefix in ("grad_", "delta_", "new_m_", "new_v_"):
        res += [outs[prefix + n] for n in _WEIGHTS]
    return tuple(res)


def kernel(x, positions, mix_norm_pre, w_in, q_norm, w_uq, kv_norm, w_uk, w_uv, ssm_lambda_re, ssm_lambda_im, ssm_log_dt, ssm_b_re, ssm_b_im, ssm_c_re, ssm_c_im, ssm_d, w_glu, b_glu, w_branch_attn, w_branch_ssm, b_gate, w_out, mix_norm_post, ffn_norm_pre, w_up, conv_w, conv_b, w_down, ffn_norm_post, loss_target, m_mix_norm_pre, m_w_in, m_q_norm, m_w_uq, m_kv_norm, m_w_uk, m_w_uv, m_ssm_lambda_re, m_ssm_lambda_im, m_ssm_log_dt, m_ssm_b_re, m_ssm_b_im, m_ssm_c_re, m_ssm_c_im, m_ssm_d, m_w_glu, m_b_glu, m_w_branch_attn, m_w_branch_ssm, m_b_gate, m_w_out, m_mix_norm_post, m_ffn_norm_pre, m_w_up, m_conv_w, m_conv_b, m_w_down, m_ffn_norm_post, v_mix_norm_pre, v_w_in, v_q_norm, v_w_uq, v_kv_norm, v_w_uk, v_w_uv, v_ssm_lambda_re, v_ssm_lambda_im, v_ssm_log_dt, v_ssm_b_re, v_ssm_b_im, v_ssm_c_re, v_ssm_c_im, v_ssm_d, v_w_glu, v_b_glu, v_w_branch_attn, v_w_branch_ssm, v_b_gate, v_w_out, v_mix_norm_post, v_ffn_norm_pre, v_w_up, v_conv_w, v_conv_b, v_w_down, v_ffn_norm_post):
    given = dict(locals())
    return _step(given)
```

```python
import math

import jax
import jax.numpy as jnp
from jax import lax
from jax.experimental import pallas as pl
from jax.experimental.pallas import tpu as pltpu

F32 = jnp.float32
BF16 = jnp.bfloat16
MESH = pl.DeviceIdType.MESH

D_MODEL = 1024
N_HEADS = 8
QK_NOPE = 64
QK_ROPE = 32
QK_HEAD = QK_NOPE + QK_ROPE
V_HEAD = 64
Q_RANK = 384
KV_RANK = 256
ROPE_THETA = 10000.0
SSM_W = 512
SSM_H = 16
SSM_G = 32
SSM_P = 64
SSM_CH = SSM_G * SSM_P
D_FF = 2816
EPS = 1e-6
ADAM_LR = 0.001
ADAM_B1 = 0.9
ADAM_B2 = 0.999
ADAM_EPS = 1e-08
ADAM_WD = 0.01
ADAM_STEP = 10

LANES = 128
SUBLANES = 8
VMEM_LIMIT = 56 * 1024 * 1024

HEAD_SLOT = LANES
HP = N_HEADS * HEAD_SLOT
P_CQ, P_CKV, P_KR, P_U, P_GL, P_END = 0, 384, 640, 768, 1280, 3328
KR_LANE = 64

FLAT_W = 1024
N_CHIPS = 4


def _tile(n, cap):
    if n <= cap:
        return n
    best = None
    for t in range(LANES, cap + 1, LANES):
        if n % t == 0:
            best = t
    assert best is not None, (n, cap)
    return best


def _params(sem):
    return pltpu.CompilerParams(dimension_semantics=sem, vmem_limit_bytes=VMEM_LIMIT)


def _dot(a, b):
    return jnp.dot(a, b, preferred_element_type=F32)


def _dot_nt(a, b):
    return lax.dot_general(a, b, (((1,), (1,)), ((), ())), preferred_element_type=F32)


def _dot_tn(a, b):
    return lax.dot_general(a, b, (((0,), (0,)), ((), ())), preferred_element_type=F32)


def _rms(x, g):
    r = lax.rsqrt(jnp.mean(x * x, axis=-1, keepdims=True) + EPS)
    return x * r * g, r


def _rms_bwd(dy, x, g):
    r = lax.rsqrt(jnp.mean(x * x, axis=-1, keepdims=True) + EPS)
    dyg = dy * g
    dx = r * dyg - x * (r * r * r) * jnp.mean(dyg * x, axis=-1, keepdims=True)
    dg = jnp.sum(dy * x * r, axis=0, keepdims=True)
    return dx, dg


_GELU_K0 = math.sqrt(2.0 / math.pi)
_GELU_K1 = 0.044715


def _gelu(x):
    th = jnp.tanh(_GELU_K0 * (x + _GELU_K1 * x * x * x))
    return 0.5 * x * (1.0 + th)


def _gelu_grad(x):
    th = jnp.tanh(_GELU_K0 * (x + _GELU_K1 * x * x * x))
    return 0.5 * (1.0 + th) + 0.5 * x * (1.0 - th * th) * _GELU_K0 * (1.0 + 3.0 * _GELU_K1 * x * x)


def _sigmoid(x):
    return 1.0 / (1.0 + jnp.exp(-x))


def _rope(q, c, s):
    n = q.shape[1]
    lane = lax.broadcasted_iota(jnp.int32, q.shape, 1) % HEAD_SLOT
    sw = jnp.where(lane < KR_LANE + QK_ROPE // 2, pltpu.roll(q, n - QK_ROPE // 2, 1), pltpu.roll(q, QK_ROPE // 2, 1))
    return q * c + sw * s


def _rope_bwd(dy, c, s):
    n = dy.shape[1]
    t = dy * s
    lane = lax.broadcasted_iota(jnp.int32, dy.shape, 1) % HEAD_SLOT
    sw = jnp.where(lane < KR_LANE + QK_ROPE // 2, pltpu.roll(t, n - QK_ROPE // 2, 1), pltpu.roll(t, QK_ROPE // 2, 1))
    rope_lane = jnp.logical_and(lane >= KR_LANE, lane < KR_LANE + QK_ROPE)
    return dy * c + jnp.where(rope_lane, sw, 0.0)


def _shift_down(x, k, halo):
    xs = pltpu.roll(x, k, 0)
    hs = pltpu.roll(halo, k, 0)
    rows = lax.broadcasted_iota(jnp.int32, halo.shape, 0)
    top = jnp.where(rows < k, hs, xs[0:SUBLANES])
    return jnp.concatenate([top, xs[SUBLANES:]], axis=0)


def _shift_up(x, k, halo):
    t = x.shape[0]
    xs = pltpu.roll(x, t - k, 0)
    hs = pltpu.roll(halo, SUBLANES - k, 0)
    rows = lax.broadcasted_iota(jnp.int32, halo.shape, 0)
    bot = jnp.where(rows >= SUBLANES - k, hs, xs[t - SUBLANES:])
    return jnp.concatenate([xs[:t - SUBLANES], bot], axis=0)


def _mm(a, b, name, out_dtype=F32, bt=False, tm_cap=1024, tn_cap=1408):
    m, k = a.shape[-2:]
    parts = a.shape[0] if a.ndim == 3 else 1
    n = b.shape[0] if bt else b.shape[1]
    tm = min(tm_cap, m)
    tn = _tile(n, tn_cap)

    def body(a_ref, b_ref, o_ref):
        if not bt:
            o_ref[...] = _dot(a_ref[...], b_ref[...]).astype(out_dtype)
        elif parts == 1:
            o_ref[...] = _dot_nt(a_ref[...], b_ref[...]).astype(out_dtype)
        else:
            @pl.when(pl.program_id(2) == 0)
            def _():
                o_ref[...] = jnp.zeros_like(o_ref)

            o_ref[...] += _dot_nt(a_ref[...], b_ref[...])

    if bt:
        b_spec = pl.BlockSpec((tn, k), lambda j, i, s: (j, s))
    else:
        b_spec = pl.BlockSpec((k, tn), lambda j, i, s: (0, j))
    if a.ndim == 3:
        assert bt and out_dtype == F32
        a_spec = pl.BlockSpec((None, tm, k), lambda j, i, s: (s, i, 0))
    else:
        a_spec = pl.BlockSpec((tm, k), lambda j, i, s: (i, 0))
    return pl.pallas_call(
        body, name=name, grid=(n // tn, m // tm, parts),
        in_specs=[a_spec, b_spec],
        out_specs=pl.BlockSpec((tm, tn), lambda j, i, s: (i, j)),
        out_shape=jax.ShapeDtypeStruct((m, n), out_dtype),
        compiler_params=_params(("parallel", "parallel", "arbitrary")),
    )(a, b)


def _mm_tn(a, b, name, tk_cap=1024, tn_cap=1664, tl_cap=1024, chips=False):
    l, k = a.shape
    tk = _tile(k, tk_cap)
    tl = min(tl_cap, l)

    def body(a_ref, b_ref, o_ref):
        @pl.when(pl.program_id(2) == 0)
        def _():
            o_ref[...] = jnp.zeros_like(o_ref)

        o_ref[...] += _dot_tn(a_ref[...], b_ref[...])

    if chips:
        n = b.shape[-1] * (b.shape[0] if b.ndim == 3 else 1)
        tn = n // N_CHIPS
        assert tn % LANES == 0
        if b.ndim == 3:
            per = N_CHIPS // b.shape[0]
            b_spec = pl.BlockSpec((None, tl, tn), lambda i, j, r: (j // per, r, j % per))
        else:
            b_spec = pl.BlockSpec((tl, tn), lambda i, j, r: (r, j))
        out_spec = pl.BlockSpec((None, tk, tn), lambda i, j, r: (j, i, 0))
        out_shape = jax.ShapeDtypeStruct((N_CHIPS, k, tn), F32)
    else:
        n = b.shape[1]
        tn = _tile(n, tn_cap)
        b_spec = pl.BlockSpec((tl, tn), lambda i, j, r: (r, j))
        out_spec = pl.BlockSpec((tk, tn), lambda i, j, r: (i, j))
        out_shape = jax.ShapeDtypeStruct((k, n), F32)
    return pl.pallas_call(
        body, name=name, grid=(k // tk, n // tn, l // tl),
        in_specs=[pl.BlockSpec((tl, tk), lambda i, j, r: (r, i)), b_spec],
        out_specs=out_spec, out_shape=out_shape,
        compiler_params=_params(("parallel", "parallel", "arbitrary")),
    )(a, b)


def _row(tl, n):
    return pl.BlockSpec((tl, n), lambda i: (i, 0))


def _const(shape):
    return pl.BlockSpec(shape, lambda i: tuple(0 for _ in shape))


def _proj_fwd(x, g1, win, gq, wuq, gkv, wukv, rc, rs, bg, tl):
    l = x.shape[0]

    def body(x_ref, g1_ref, win_ref, gq_ref, wuq_ref, gkv_ref, wukv_ref, rc_ref, rs_ref, bg_ref,
             hn_ref, cq_ref, ckv_ref, q_ref, k_ref, v_ref, u_ref, gl_ref):
        hn, _ = _rms(x_ref[...], g1_ref[...])
        hnb = hn.astype(BF16)
        hn_ref[...] = hnb
        proj = _dot(hnb, win_ref[...])
        cq = proj[:, P_CQ:P_CKV]
        ckv = proj[:, P_CKV:P_KR]
        kr = proj[:, P_KR:P_U]
        cq_ref[...] = cq
        ckv_ref[...] = ckv
        u_ref[...] = proj[:, P_U:P_GL]
        gl_ref[...] = proj[:, P_GL:P_END] + bg_ref[...]
        qn, _ = _rms(cq, gq_ref[...])
        q = _dot(qn.astype(BF16), wuq_ref[...])
        c1 = rc_ref[...]
        s1 = rs_ref[...]
        q_ref[...] = (_rope(q, jnp.tile(c1, (1, N_HEADS)), jnp.tile(s1, (1, N_HEADS))) * Q_PRESCALE).astype(BF16)
        ckvn, _ = _rms(ckv, gkv_ref[...])
        kv = _dot(ckvn.astype(BF16), wukv_ref[...])
        krr = _rope(kr, c1, s1)
        k_ref[...] = (kv[:, :HP] + jnp.tile(krr, (1, N_HEADS))).astype(BF16)
        v_ref[...] = kv[:, HP:].astype(BF16)

    outs = [(D_MODEL, BF16), (Q_RANK, F32), (KV_RANK, F32), (HP, BF16), (HP, BF16), (HP, BF16),
            (SSM_W, F32), (2 * D_MODEL, F32)]
    return pl.pallas_call(
        body, name="proj_fwd", grid=(l // tl,),
        in_specs=[_row(tl, D_MODEL), _const((1, D_MODEL)), _const((D_MODEL, P_END)), _const((1, Q_RANK)),
                  _const((Q_RANK, HP)), _const((1, KV_RANK)), _const((KV_RANK, 2 * HP)),
                  _row(tl, HEAD_SLOT), _row(tl, HEAD_SLOT), _const((1, 2 * D_MODEL))],
        out_specs=[_row(tl, n) for n, _ in outs],
        out_shape=[jax.ShapeDtypeStruct((l, n), dt) for n, dt in outs],
        compiler_params=_params(("parallel",)),
    )(x, g1, win, gq, wuq, gkv, wukv, rc, rs, bg)


_NEG = -1e30


LOG2E = 1.0 / math.log(2.0)
LN2 = math.log(2.0)
ATTN_SCALE = 1.0 / math.sqrt(QK_HEAD)
Q_PRESCALE = ATTN_SCALE * LOG2E
HEADS_PER_STEP = 4
PAIR_W = HEADS_PER_STEP * HEAD_SLOT


def _causal_pairs(nq, by_query):
    if by_query:
        pairs = [(i, j) for i in range(nq) for j in range(i + 1)]
    else:
        pairs = [(i, j) for j in range(nq) for i in range(j, nq)]
    return jnp.array([p[0] for p in pairs], jnp.int32), jnp.array([p[1] for p in pairs], jnp.int32)


def _diag_mask_t(s):
    rows = lax.broadcasted_iota(jnp.int32, s.shape, 0)
    cols = lax.broadcasted_iota(jnp.int32, s.shape, 1)
    return jnp.where(rows <= cols, s, _NEG)


def _attn_fwd(q, k, v, tq):
    l = q.shape[0]
    nq = l // tq
    it, jt = _causal_pairs(nq, True)

    def body(it_ref, jt_ref, q_ref, k_ref, v_ref, o_ref, lse_ref, m_ref, l_ref, acc_ref):
        t = pl.program_id(1)
        i = it_ref[t]
        j = jt_ref[t]

        @pl.when(j == 0)
        def _():
            m_ref[...] = jnp.full_like(m_ref, _NEG)
            l_ref[...] = jnp.zeros_like(l_ref)
            acc_ref[...] = jnp.zeros_like(acc_ref)

        def update(on_diagonal):
            for hh in range(HEADS_PER_STEP):
                sl = slice(hh * HEAD_SLOT, (hh + 1) * HEAD_SLOT)
                s = _dot_nt(k_ref[:, sl], q_ref[:, sl])
                if on_diagonal:
                    s = _diag_mask_t(s)
                m_old = m_ref[hh]
                m_new = jnp.maximum(m_old, jnp.max(s, axis=0, keepdims=True))
                p = jnp.exp2(s - m_new)
                alpha = jnp.exp2(m_old - m_new)
                l_ref[hh] = alpha * l_ref[hh] + jnp.sum(p, axis=0, keepdims=True)
                acc_ref[hh] = alpha * acc_ref[hh] + _dot_tn(v_ref[:, sl], p.astype(BF16))
                m_ref[hh] = m_new

        @pl.when(j < i)
        def _():
            update(False)

        @pl.when(j == i)
        def _():
            update(True)
            for hh in range(HEADS_PER_STEP):
                sl = slice(hh * HEAD_SLOT, (hh + 1) * HEAD_SLOT)
                o_ref[:, sl] = (acc_ref[hh] / l_ref[hh]).T.astype(BF16)
                lse_ref[hh] = m_ref[hh] + jnp.log(l_ref[hh]) * LOG2E

    blk = (tq, PAIR_W)
    qmap = lambda h, t, it_ref, jt_ref: (it_ref[t], h)
    kmap = lambda h, t, it_ref, jt_ref: (jt_ref[t], h)
    row = pl.BlockSpec((HEADS_PER_STEP, 1, tq), lambda h, t, it_ref, jt_ref: (h, 0, it_ref[t]))
    return pl.pallas_call(
        body, name="attn_fwd",
        grid_spec=pltpu.PrefetchScalarGridSpec(
            num_scalar_prefetch=2, grid=(N_HEADS // HEADS_PER_STEP, it.shape[0]),
            in_specs=[pl.BlockSpec(blk, qmap), pl.BlockSpec(blk, kmap), pl.BlockSpec(blk, kmap)],
            out_specs=[pl.BlockSpec(blk, qmap), row],
            scratch_shapes=[pltpu.VMEM((HEADS_PER_STEP, 1, tq), F32), pltpu.VMEM((HEADS_PER_STEP, 1, tq), F32),
                            pltpu.VMEM((HEADS_PER_STEP, HEAD_SLOT, tq), F32)]),
        out_shape=[jax.ShapeDtypeStruct((l, HP), BF16), jax.ShapeDtypeStruct((N_HEADS, 1, l), F32)],
        compiler_params=_params(("parallel", "arbitrary")),
    )(it, jt, q, k, v)


def _attn_delta(o, do, tq):
    l = o.shape[0]

    def body(o_ref, do_ref, d_ref):
        prod = o_ref[...].astype(F32) * do_ref[...].astype(F32)
        for hh in range(HEADS_PER_STEP):
            d_ref[hh] = jnp.sum(prod[:, hh * HEAD_SLOT:(hh + 1) * HEAD_SLOT].T, axis=0, keepdims=True)

    blk = pl.BlockSpec((tq, PAIR_W), lambda h, i: (i, h))
    return pl.pallas_call(
        body, name="attn_delta", grid=(N_HEADS // HEADS_PER_STEP, l // tq), in_specs=[blk, blk],
        out_specs=pl.BlockSpec((HEADS_PER_STEP, 1, tq), lambda h, i: (h, 0, i)),
        out_shape=jax.ShapeDtypeStruct((N_HEADS, 1, l), F32),
        compiler_params=_params(("parallel", "parallel")),
    )(o, do)


def _attn_bwd(q, k, v, do, lse, delta, tq):
    l = q.shape[0]
    nq = l // tq
    it, jt = _causal_pairs(nq, False)

    def body(it_ref, jt_ref, q_ref, k_ref, v_ref, do_ref, lse_ref, dl_ref, dq_ref, dk_ref, dv_ref, dka_ref, dva_ref):
        t = pl.program_id(1)
        i = it_ref[t]
        j = jt_ref[t]

        @pl.when(t == 0)
        def _():
            dq_ref[...] = jnp.zeros_like(dq_ref)

        @pl.when(i == j)
        def _():
            dka_ref[...] = jnp.zeros_like(dka_ref)
            dva_ref[...] = jnp.zeros_like(dva_ref)

        def update(on_diagonal):
            r0 = pl.multiple_of(i * tq, tq)
            for hh in range(HEADS_PER_STEP):
                sl = slice(hh * HEAD_SLOT, (hh + 1) * HEAD_SLOT)
                qb = q_ref[:, sl]
                kb = k_ref[:, sl]
                dob = do_ref[:, sl]
                s = _dot_nt(kb, qb)
                if on_diagonal:
                    s = _diag_mask_t(s)
                p = jnp.exp2(s - lse_ref[hh])
                dva_ref[:, sl] += _dot(p.astype(BF16), dob)
                dp = _dot_nt(v_ref[:, sl], dob)
                ds = (p * (dp - dl_ref[hh])).astype(BF16)
                dka_ref[:, sl] += _dot(ds, qb)
                dq_ref[pl.ds(r0, tq), sl] += ATTN_SCALE * _dot_tn(ds, kb)

        @pl.when(j < i)
        def _():
            update(False)

        @pl.when(j == i)
        def _():
            update(True)

        @pl.when(i == nq - 1)
        def _():
            dk_ref[...] = (dka_ref[...] * LN2).astype(BF16)
            dv_ref[...] = dva_ref[...].astype(BF16)

    blk = (tq, PAIR_W)
    qmap = lambda h, t, it_ref, jt_ref: (it_ref[t], h)
    kmap = lambda h, t, it_ref, jt_ref: (jt_ref[t], h)
    row = pl.BlockSpec((HEADS_PER_STEP, 1, tq), lambda h, t, it_ref, jt_ref: (h, 0, it_ref[t]))
    return pl.pallas_call(
        body, name="attn_bwd",
        grid_spec=pltpu.PrefetchScalarGridSpec(
            num_scalar_prefetch=2, grid=(N_HEADS // HEADS_PER_STEP, it.shape[0]),
            in_specs=[pl.BlockSpec(blk, qmap), pl.BlockSpec(blk, kmap), pl.BlockSpec(blk, kmap),
                      pl.BlockSpec(blk, qmap), row, row],
            out_specs=[pl.BlockSpec((l, PAIR_W), lambda h, t, it_ref, jt_ref: (0, h)), pl.BlockSpec(blk, kmap),
                       pl.BlockSpec(blk, kmap)],
            scratch_shapes=[pltpu.VMEM(blk, F32), pltpu.VMEM(blk, F32)]),
        out_shape=[jax.ShapeDtypeStruct((l, HP), F32), jax.ShapeDtypeStruct((l, HP), BF16),
                   jax.ShapeDtypeStruct((l, HP), BF16)],
        compiler_params=_params(("parallel", "arbitrary")),
    )(it, jt, q, k, v, do, lse, delta)


SSM_CB = 512
SSM_UB = 128
SSM_NB = SSM_CH // SSM_CB


def _scan_tiles(re_ref, im_ref, tab, carry, n_tiles, reverse):
    group = 2
    assert n_tiles % group == 0
    pr, pi = tab[6], tab[7]

    def inside(sr, si):
        for step, k in enumerate((1, 2, 4)):
            mr, mi = tab[2 * step], tab[2 * step + 1]
            sh = (SUBLANES - k) if reverse else k
            rr = pltpu.roll(sr, sh, 0)
            ri = pltpu.roll(si, sh, 0)
            sr, si = sr + mr * rr - mi * ri, si + mr * ri + mi * rr
        return sr, si

    def body(n, c):
        cr, ci = c
        first = (n_tiles - group * (n + 1)) if reverse else group * n
        r0 = pl.multiple_of(first * SUBLANES, group * SUBLANES)
        rows = [pl.ds(r0 + g * SUBLANES, SUBLANES) for g in range(group)]
        tiles = [inside(re_ref[r, :], im_ref[r, :]) for r in rows]
        for g in (range(group - 1, -1, -1) if reverse else range(group)):
            sr, si = tiles[g]
            sr, si = sr + pr * cr - pi * ci, si + pr * ci + pi * cr
            re_ref[rows[g], :] = sr
            im_ref[rows[g], :] = si
            edge = slice(0, 1) if reverse else slice(SUBLANES - 1, SUBLANES)
            cr, ci = sr[edge, :], si[edge, :]
        return cr, ci

    return lax.fori_loop(0, n_tiles // group, body, carry)


def _ssm_fwd(u, bre, bim, cre, cim, dvec, tab, tt):
    l = u.shape[0]
    nt = l // tt

    def body(u_ref, bre_ref, bim_ref, cre_ref, cim_ref, d_ref, tab_ref, y_ref, sre_ref, sim_ref, car_ref):
        @pl.when(pl.program_id(1) == 0)
        def _():
            car_ref[...] = jnp.zeros_like(car_ref)

        uf = u_ref[...]
        ub = uf.astype(BF16)
        sre_ref[...] = _dot(ub, bre_ref[0])
        sim_ref[...] = _dot(ub, bim_ref[0])
        tab_v = [tab_ref[n] for n in range(8)]
        cr, ci = _scan_tiles(sre_ref, sim_ref, tab_v, (car_ref[0:1, :], car_ref[8:9, :]), tt // SUBLANES, False)
        car_ref[0:1, :] = cr
        car_ref[8:9, :] = ci
        y_ref[...] = (_dot(sre_ref[...].astype(BF16), cre_ref[0]) - _dot(sim_ref[...].astype(BF16), cim_ref[0])
                      + d_ref[...] * uf)

    return pl.pallas_call(
        body, name="ssm_fwd", grid=(SSM_NB, nt),
        in_specs=[pl.BlockSpec((tt, SSM_UB), lambda m, t: (t, m)),
                  pl.BlockSpec((1, SSM_UB, SSM_CB), lambda m, t: (m, 0, 0)),
                  pl.BlockSpec((1, SSM_UB, SSM_CB), lambda m, t: (m, 0, 0)),
                  pl.BlockSpec((1, SSM_CB, SSM_UB), lambda m, t: (m, 0, 0)),
                  pl.BlockSpec((1, SSM_CB, SSM_UB), lambda m, t: (m, 0, 0)),
                  pl.BlockSpec((1, SSM_UB), lambda m, t: (0, m)),
                  pl.BlockSpec((8, SUBLANES, SSM_CB), lambda m, t: (0, 0, m))],
        out_specs=[pl.BlockSpec((tt, SSM_UB), lambda m, t: (t, m)),
                   pl.BlockSpec((tt, SSM_CB), lambda m, t: (t, m)),
                   pl.BlockSpec((tt, SSM_CB), lambda m, t: (t, m))],
        out_shape=[jax.ShapeDtypeStruct((l, SSM_W), F32), jax.ShapeDtypeStruct((l, SSM_CH), F32),
                   jax.ShapeDtypeStruct((l, SSM_CH), F32)],
        scratch_shapes=[pltpu.VMEM((2 * SUBLANES, SSM_CB), F32)],
        compiler_params=_params(("parallel", "arbitrary")),
    )(u, bre, bim, cre, cim, dvec, tab)


def _ssm_bwd(dy, u, sre, sim, bre, bim, cre, cim, dvec, tab, tt):
    l = u.shape[0]
    nt = l // tt
    tpb = tt // SUBLANES

    def body(dy_ref, u_ref, sre_ref, sim_ref, hre_ref, him_ref, bre_ref, bim_ref, cre_ref, cim_ref, d_ref, tab_ref,
             du_ref, dbre_ref, dbim_ref, dcre_ref, dcim_ref, dare_ref, daim_ref, dd_ref, lr_ref, li_ref, car_ref):
        t = pl.program_id(1)

        @pl.when(t == 0)
        def _():
            car_ref[...] = jnp.zeros_like(car_ref)
            for ref in (dbre_ref, dbim_ref, dcre_ref, dcim_ref, dare_ref, daim_ref, dd_ref):
                ref[...] = jnp.zeros_like(ref)

        dyf = dy_ref[...]
        dyb = dyf.astype(BF16)
        uf = u_ref[...]
        s_re = sre_ref[...]
        s_im = sim_ref[...]
        lr_ref[...] = _dot_nt(dyb, cre_ref[0])
        li_ref[...] = -_dot_nt(dyb, cim_ref[0])
        dcre_ref[0] += _dot_tn(s_re.astype(BF16), dyb)
        dcim_ref[0] -= _dot_tn(s_im.astype(BF16), dyb)
        tab_v = [tab_ref[n] for n in range(8)]
        cr, ci = _scan_tiles(lr_ref, li_ref, tab_v, (car_ref[0:1, :], car_ref[8:9, :]), tpb, True)
        car_ref[0:1, :] = cr
        car_ref[8:9, :] = ci
        lam_r = lr_ref[...]
        lam_i = li_ref[...]
        keep = jnp.where(t == nt - 1, 0.0, 1.0)
        sp_r = _shift_down(s_re, 1, hre_ref[...] * keep)
        sp_i = _shift_down(s_im, 1, him_ref[...] * keep)
        dare_ref[...] += jnp.sum(lam_r * sp_r + lam_i * sp_i, axis=0, keepdims=True)
        daim_ref[...] += jnp.sum(lam_i * sp_r - lam_r * sp_i, axis=0, keepdims=True)
        lrb = lam_r.astype(BF16)
        lib = lam_i.astype(BF16)
        du_ref[...] = _dot_nt(lrb, bre_ref[0]) + _dot_nt(lib, bim_ref[0]) + dyf * d_ref[...]
        ub = uf.astype(BF16)
        dbre_ref[0] += _dot_tn(ub, lrb)
        dbim_ref[0] += _dot_tn(ub, lib)
        dd_ref[...] += jnp.sum(dyf * uf, axis=0, keepdims=True)

    rev = lambda m, t: (nt - 1 - t, m)
    halo = lambda m, t: (jnp.maximum((nt - 1 - t) * tpb - 1, 0), m)
    wb = pl.BlockSpec((1, SSM_UB, SSM_CB), lambda m, t: (m, 0, 0))
    wc = pl.BlockSpec((1, SSM_CB, SSM_UB), lambda m, t: (m, 0, 0))
    vec_c = pl.BlockSpec((1, SSM_CB), lambda m, t: (0, m))
    vec_u = pl.BlockSpec((1, SSM_UB), lambda m, t: (0, m))
    return pl.pallas_call(
        body, name="ssm_bwd", grid=(SSM_NB, nt),
        in_specs=[pl.BlockSpec((tt, SSM_UB), rev), pl.BlockSpec((tt, SSM_UB), rev),
                  pl.BlockSpec((tt, SSM_CB), rev), pl.BlockSpec((tt, SSM_CB), rev),
                  pl.BlockSpec((SUBLANES, SSM_CB), halo), pl.BlockSpec((SUBLANES, SSM_CB), halo),
                  wb, wb, wc, wc, vec_u,
                  pl.BlockSpec((8, SUBLANES, SSM_CB), lambda m, t: (0, 0, m))],
        out_specs=[pl.BlockSpec((tt, SSM_UB), rev), wb, wb, wc, wc, vec_c, vec_c, vec_u],
        out_shape=[jax.ShapeDtypeStruct((l, SSM_W), F32),
                   jax.ShapeDtypeStruct((SSM_NB, SSM_UB, SSM_CB), F32), jax.ShapeDtypeStruct((SSM_NB, SSM_UB, SSM_CB), F32),
                   jax.ShapeDtypeStruct((SSM_NB, SSM_CB, SSM_UB), F32), jax.ShapeDtypeStruct((SSM_NB, SSM_CB, SSM_UB), F32),
                   jax.ShapeDtypeStruct((1, SSM_CH), F32), jax.ShapeDtypeStruct((1, SSM_CH), F32),
                   jax.ShapeDtypeStruct((1, SSM_W), F32)],
        scratch_shapes=[pltpu.VMEM((tt, SSM_CB), F32), pltpu.VMEM((tt, SSM_CB), F32),
                        pltpu.VMEM((2 * SUBLANES, SSM_CB), F32)],
        compiler_params=_params(("parallel", "arbitrary")),
    )(dy, u, sre, sim, sre, sim, bre, bim, cre, cim, dvec, tab)


def _merge_fwd(x, gl, attn, y1, wba, wbs, wglu, bglu, wout, gpost, gpre, tl):
    l = x.shape[0]

    def body(x_ref, gl_ref, at_ref, y1_ref, wba_ref, wbs_ref, wglu_ref, bglu_ref, wout_ref, gpost_ref, gpre_ref,
             a_ref, sm_ref, mg_ref, z_ref, x1_ref, hn2_ref, y3_ref):
        y2 = _gelu(y1_ref[...])
        sg = _sigmoid(_dot(y2.astype(BF16), wglu_ref[...]) + bglu_ref[...])
        y3 = (y2 * sg).astype(BF16)
        y3_ref[...] = y3
        a = _dot(at_ref[...], wba_ref[...])
        sm = _dot(y3, wbs_ref[...])
        a_ref[...] = a.astype(BF16)
        sm_ref[...] = sm.astype(BF16)
        g = _sigmoid(gl_ref[...])
        merged = (g[:, :D_MODEL] * a + g[:, D_MODEL:] * sm).astype(BF16)
        mg_ref[...] = merged
        z = _dot(merged, wout_ref[...])
        z_ref[...] = z
        n, _ = _rms(z, gpost_ref[...])
        x1 = x_ref[...] + n
        x1_ref[...] = x1
        hn2, _ = _rms(x1, gpre_ref[...])
        hn2_ref[...] = hn2.astype(BF16)

    outs = [(D_MODEL, BF16), (D_MODEL, BF16), (D_MODEL, BF16), (D_MODEL, F32), (D_MODEL, F32), (D_MODEL, BF16),
            (SSM_W, BF16)]
    return pl.pallas_call(
        body, name="merge_fwd", grid=(l // tl,),
        in_specs=[_row(tl, D_MODEL), _row(tl, 2 * D_MODEL), _row(tl, HP), _row(tl, SSM_W),
                  _const((HP, D_MODEL)), _const((SSM_W, D_MODEL)), _const((SSM_W, SSM_W)), _const((1, SSM_W)),
                  _const((D_MODEL, D_MODEL)), _const((1, D_MODEL)), _const((1, D_MODEL))],
        out_specs=[_row(tl, n) for n, _ in outs],
        out_shape=[jax.ShapeDtypeStruct((l, n), dt) for n, dt in outs],
        compiler_params=_params(("parallel",)),
    )(x, gl, attn, y1, wba, wbs, wglu, bglu, wout, gpost, gpre)


def _merge_bwd(dhn2, x1, dx2, z, gl, a, sm, y1, wba, wbs, wglu, bglu, wout, gpost, gpre, tl):
    l = x1.shape[0]

    def body(dhn2_ref, x1_ref, dx2_ref, z_ref, gl_ref, a_ref, sm_ref, y1_ref,
             wba_ref, wbs_ref, wglu_ref, bglu_ref, wout_ref, gpost_ref, gpre_ref,
             dx1_ref, dz_ref, dbra_ref, dbrs_ref, dgl_ref, dat_ref, dy1_ref, dt_ref, y2_ref,
             dgpre_ref, dgpost_ref, dbg_ref, dbglu_ref):
        @pl.when(pl.program_id(0) == 0)
        def _():
            for ref in (dgpre_ref, dgpost_ref, dbg_ref, dbglu_ref):
                ref[...] = jnp.zeros_like(ref)

        dx1a, dgpre = _rms_bwd(dhn2_ref[...], x1_ref[...], gpre_ref[...])
        dgpre_ref[...] += dgpre
        dx1 = dx2_ref[...] + dx1a
        dx1_ref[...] = dx1
        dz, dgpost = _rms_bwd(dx1, z_ref[...], gpost_ref[...])
        dgpost_ref[...] += dgpost
        dzb = dz.astype(BF16)
        dz_ref[...] = dzb
        dm = _dot_nt(dzb, wout_ref[...])
        g = _sigmoid(gl_ref[...])
        g0 = g[:, :D_MODEL]
        g1 = g[:, D_MODEL:]
        dbra = (dm * g0).astype(BF16)
        dbrs = (dm * g1).astype(BF16)
        dbra_ref[...] = dbra
        dbrs_ref[...] = dbrs
        dgl0 = dm * a_ref[...].astype(F32) * g0 * (1.0 - g0)
        dgl1 = dm * sm_ref[...].astype(F32) * g1 * (1.0 - g1)
        dgl_ref[:, :D_MODEL] = dgl0.astype(BF16)
        dgl_ref[:, D_MODEL:] = dgl1.astype(BF16)
        dbg_ref[:, :D_MODEL] += jnp.sum(dgl0, axis=0, keepdims=True)
        dbg_ref[:, D_MODEL:] += jnp.sum(dgl1, axis=0, keepdims=True)
        dat_ref[...] = _dot_nt(dbra, wba_ref[...]).astype(BF16)
        dy3 = _dot_nt(dbrs, wbs_ref[...])
        y1v = y1_ref[...]
        y2 = _gelu(y1v)
        y2b = y2.astype(BF16)
        y2_ref[...] = y2b
        sg = _sigmoid(_dot(y2b, wglu_ref[...]) + bglu_ref[...])
        dt = dy3 * y2 * sg * (1.0 - sg)
        dtb = dt.astype(BF16)
        dt_ref[...] = dtb
        dbglu_ref[...] += jnp.sum(dt, axis=0, keepdims=True)
        dy2 = dy3 * sg + _dot_nt(dtb, wglu_ref[...])
        dy1_ref[...] = dy2 * _gelu_grad(y1v)

    outs = [(D_MODEL, F32), (D_MODEL, BF16), (D_MODEL, BF16), (D_MODEL, BF16), (2 * D_MODEL, BF16), (HP, BF16),
            (SSM_W, F32), (SSM_W, BF16), (SSM_W, BF16)]
    accs = [D_MODEL, D_MODEL, 2 * D_MODEL, SSM_W]
    return pl.pallas_call(
        body, name="merge_bwd", grid=(l // tl,),
        in_specs=[_row(tl, D_MODEL), _row(tl, D_MODEL), _row(tl, D_MODEL), _row(tl, D_MODEL),
                  _row(tl, 2 * D_MODEL), _row(tl, D_MODEL), _row(tl, D_MODEL), _row(tl, SSM_W),
                  _const((HP, D_MODEL)), _const((SSM_W, D_MODEL)), _const((SSM_W, SSM_W)), _const((1, SSM_W)),
                  _const((D_MODEL, D_MODEL)), _const((1, D_MODEL)), _const((1, D_MODEL))],
        out_specs=[_row(tl, n) for n, _ in outs] + [_const((1, n)) for n in accs],
        out_shape=[jax.ShapeDtypeStruct((l, n), dt) for n, dt in outs]
        + [jax.ShapeDtypeStruct((1, n), F32) for n in accs],
        compiler_params=_params(("arbitrary",)),
    )(dhn2, x1, dx2, z, gl, a, sm, y1, wba, wbs, wglu, bglu, wout, gpost, gpre)


def _proj_bwd(x, dx1, cq, ckv, dq, dk, dv, du, dgl, g1, win, gq, wuq, gkv, wukv, rc, rs, tl):
    l = x.shape[0]

    def body(x_ref, dx1_ref, cq_ref, ckv_ref, dq_ref, dk_ref, dv_ref, du_ref, dgl_ref,
             g1_ref, win_ref, gq_ref, wuq_ref, gkv_ref, wukv_ref, rc_ref, rs_ref,
             gx_ref, dql_ref, qn_ref, ckvn_ref, dproj_ref, dg1_ref, dgq_ref, dgkv_ref):
        @pl.when(pl.program_id(0) == 0)
        def _():
            for ref in (dg1_ref, dgq_ref, dgkv_ref):
                ref[...] = jnp.zeros_like(ref)

        c1 = rc_ref[...]
        s1 = rs_ref[...]
        dql = _rope_bwd(dq_ref[...], jnp.tile(c1, (1, N_HEADS)), jnp.tile(s1, (1, N_HEADS))).astype(BF16)
        dql_ref[...] = dql
        dqn = _dot_nt(dql, wuq_ref[...])
        cq = cq_ref[...]
        qn, _ = _rms(cq, gq_ref[...])
        qn_ref[...] = qn.astype(BF16)
        dcq, dgq = _rms_bwd(dqn, cq, gq_ref[...])
        dgq_ref[...] += dgq
        dkb = dk_ref[...]
        dvb = dv_ref[...]
        dkf = dkb.astype(F32)
        dkr = dkf[:, 0:HEAD_SLOT]
        for h in range(1, N_HEADS):
            dkr = dkr + dkf[:, h * HEAD_SLOT:(h + 1) * HEAD_SLOT]
        dkr = _rope_bwd(dkr, c1, s1)
        dckvn = _dot_nt(dkb, wukv_ref[:, :HP]) + _dot_nt(dvb, wukv_ref[:, HP:])
        ckv = ckv_ref[...]
        ckvn, _ = _rms(ckv, gkv_ref[...])
        ckvn_ref[...] = ckvn.astype(BF16)
        dckv, dgkv = _rms_bwd(dckvn, ckv, gkv_ref[...])
        dgkv_ref[...] += dgkv
        dproj_ref[:, P_CQ:P_CKV] = dcq.astype(BF16)
        dproj_ref[:, P_CKV:P_KR] = dckv.astype(BF16)
        dproj_ref[:, P_KR:P_U] = dkr.astype(BF16)
        dproj_ref[:, P_U:P_GL] = du_ref[...].astype(BF16)
        dproj_ref[:, P_GL:P_END] = dgl_ref[...]
        dhn = _dot_nt(dproj_ref[...], win_ref[...])
        dxa, dg1 = _rms_bwd(dhn, x_ref[...], g1_ref[...])
        dg1_ref[...] += dg1
        gx_ref[...] = dx1_ref[...] + dxa

    outs = [(D_MODEL, F32), (HP, BF16), (Q_RANK, BF16), (KV_RANK, BF16), (P_END, BF16)]
    accs = [D_MODEL, Q_RANK, KV_RANK]
    return pl.pallas_call(
        body, name="proj_bwd", grid=(l // tl,),
        in_specs=[_row(tl, D_MODEL), _row(tl, D_MODEL), _row(tl, Q_RANK), _row(tl, KV_RANK), _row(tl, HP),
                  _row(tl, HP), _row(tl, HP), _row(tl, SSM_W), _row(tl, 2 * D_MODEL),
                  _const((1, D_MODEL)), _const((D_MODEL, P_END)), _const((1, Q_RANK)), _const((Q_RANK, HP)),
                  _const((1, KV_RANK)), _const((KV_RANK, 2 * HP)), _row(tl, HEAD_SLOT), _row(tl, HEAD_SLOT)],
        out_specs=[_row(tl, n) for n, _ in outs] + [_const((1, n)) for n in accs],
        out_shape=[jax.ShapeDtypeStruct((l, n), dt) for n, dt in outs]
        + [jax.ShapeDtypeStruct((1, n), F32) for n in accs],
        compiler_params=_params(("arbitrary",)),
    )(x, dx1, cq, ckv, dq, dk, dv, du, dgl, g1, win, gq, wuq, gkv, wukv, rc, rs)


CONV_CB = 256
CONV_NB = D_FF // CONV_CB
CONV_ROWS = 16


def _conv3(h, halo, w, b):
    return b + w[0:1, :] * _shift_down(h, 2, halo) + w[1:2, :] * _shift_down(h, 1, halo) + w[2:3, :] * h


def _conv_fwd(h, cw, cb, tl):
    l = h.shape[0]

    def body(hg_ref, hv_ref, wg_ref, wv_ref, bg_ref, bv_ref, act_ref, halo_ref):
        @pl.when(pl.program_id(1) == 0)
        def _():
            halo_ref[...] = jnp.zeros_like(halo_ref)

        hg = hg_ref[...]
        hv = hv_ref[...]
        cg = _conv3(hg, halo_ref[0:SUBLANES, :], wg_ref[...], bg_ref[...])
        cv = _conv3(hv, halo_ref[SUBLANES:, :], wv_ref[...], bv_ref[...])
        act_ref[...] = (_gelu(cg) * cv).astype(BF16)
        halo_ref[0:SUBLANES, :] = hg[tl - SUBLANES:, :]
        halo_ref[SUBLANES:, :] = hv[tl - SUBLANES:, :]

    gmap = lambda c, r: (r, c)
    vmap = lambda c, r: (r, CONV_NB + c)
    return pl.pallas_call(
        body, name="conv_fwd", grid=(CONV_NB, l // tl),
        in_specs=[pl.BlockSpec((tl, CONV_CB), gmap), pl.BlockSpec((tl, CONV_CB), vmap),
                  pl.BlockSpec((3, CONV_CB), lambda c, r: (0, c)), pl.BlockSpec((3, CONV_CB), lambda c, r: (0, CONV_NB + c)),
                  pl.BlockSpec((1, CONV_CB), lambda c, r: (0, c)), pl.BlockSpec((1, CONV_CB), lambda c, r: (0, CONV_NB + c))],
        out_specs=pl.BlockSpec((tl, CONV_CB), gmap),
        out_shape=jax.ShapeDtypeStruct((l, D_FF), BF16),
        scratch_shapes=[pltpu.VMEM((2 * SUBLANES, CONV_CB), F32)],
        compiler_params=_params(("parallel", "arbitrary")),
    )(h, h, cw, cw, cb, cb)


def _conv_bwd(h, dact, cw, cb, tl):
    l = h.shape[0]
    nr = l // tl
    tpb = tl // SUBLANES

    def body(hg_ref, hv_ref, hgh_ref, hvh_ref, da_ref, wg_ref, wv_ref, bg_ref, bv_ref,
             dh_ref, dwg_ref, dwv_ref, dbg_ref, dbv_ref, car_ref):
        r = pl.program_id(1)

        @pl.when(r == 0)
        def _():
            for ref in (car_ref, dwg_ref, dwv_ref, dbg_ref, dbv_ref):
                ref[...] = jnp.zeros_like(ref)

        keep = jnp.where(r == nr - 1, 0.0, 1.0)
        wg, wv, bg, bv = wg_ref[...], wv_ref[...], bg_ref[...], bv_ref[...]
        nch = tl // CONV_ROWS

        def fold(x):
            s = x[0:SUBLANES, :]
            for k in range(1, CONV_ROWS // SUBLANES):
                s = s + x[k * SUBLANES:(k + 1) * SUBLANES, :]
            return s

        def chunk(n, carry):
            ncg, ncv, acc = carry
            idx = nch - 1 - n
            r0 = pl.multiple_of(idx * CONV_ROWS, CONV_ROWS)
            rows = pl.ds(r0, CONV_ROWS)
            before = pl.ds(pl.multiple_of(jnp.maximum(r0 - SUBLANES, 0), SUBLANES), SUBLANES)
            in_tile = idx > 0
            da = da_ref[rows, :].astype(F32)

            def half(h_ref, halo_ref, w, b):
                hh = h_ref[rows, :]
                prev = jnp.where(in_tile, h_ref[before, :], halo_ref[...] * keep)
                h1 = _shift_down(hh, 1, prev)
                h2 = _shift_down(hh, 2, prev)
                return hh, h1, h2, b + w[0:1, :] * h2 + w[1:2, :] * h1 + w[2:3, :] * hh

            hg, hg1, hg2, cg = half(hg_ref, hgh_ref, wg, bg)
            hv, hv1, hv2, cv = half(hv_ref, hvh_ref, wv, bv)
            dcg = da * cv * _gelu_grad(cg)
            dcv = da * _gelu(cg)

            def back(dc, hh, h1, h2, w, nxt, part):
                dh = w[2:3, :] * dc + w[1:2, :] * _shift_up(dc, 1, nxt) + w[0:1, :] * _shift_up(dc, 2, nxt)
                dh_ref[part, rows, :] = dh.astype(BF16)
                return [fold(dc * h2), fold(dc * h1), fold(dc * hh), fold(dc)]

            sums = back(dcg, hg, hg1, hg2, wg, ncg, 0) + back(dcv, hv, hv1, hv2, wv, ncv, 1)
            return dcg[0:SUBLANES, :], dcv[0:SUBLANES, :], [a + s for a, s in zip(acc, sums)]

        zero = jnp.zeros((SUBLANES, CONV_CB), F32)
        ncg, ncv, acc = lax.fori_loop(0, nch, chunk, (car_ref[0:SUBLANES, :], car_ref[SUBLANES:, :], [zero] * 8))
        car_ref[0:SUBLANES, :] = ncg
        car_ref[SUBLANES:, :] = ncv
        for half_acc, dw_ref, db_ref in ((acc[0:4], dwg_ref, dbg_ref), (acc[4:8], dwv_ref, dbv_ref)):
            for k in range(3):
                dw_ref[k:k + 1, :] += jnp.sum(half_acc[k], axis=0, keepdims=True)
            db_ref[...] += jnp.sum(half_acc[3], axis=0, keepdims=True)

    grev = lambda c, r: (nr - 1 - r, c)
    vrev = lambda c, r: (nr - 1 - r, CONV_NB + c)
    ghalo = lambda c, r: (jnp.maximum((nr - 1 - r) * tpb - 1, 0), c)
    vhalo = lambda c, r: (jnp.maximum((nr - 1 - r) * tpb - 1, 0), CONV_NB + c)
    colg = lambda c, r: (0, c)
    colv = lambda c, r: (0, CONV_NB + c)
    return pl.pallas_call(
        body, name="conv_bwd", grid=(CONV_NB, nr),
        in_specs=[pl.BlockSpec((tl, CONV_CB), grev), pl.BlockSpec((tl, CONV_CB), vrev),
                  pl.BlockSpec((SUBLANES, CONV_CB), ghalo), pl.BlockSpec((SUBLANES, CONV_CB), vhalo),
                  pl.BlockSpec((tl, CONV_CB), grev),
                  pl.BlockSpec((3, CONV_CB), colg), pl.BlockSpec((3, CONV_CB), colv),
                  pl.BlockSpec((1, CONV_CB), colg), pl.BlockSpec((1, CONV_CB), colv)],
        out_specs=[pl.BlockSpec((2, tl, CONV_CB), lambda c, r: (0, nr - 1 - r, c)),
                   pl.BlockSpec((3, CONV_CB), colg), pl.BlockSpec((3, CONV_CB), colg),
                   pl.BlockSpec((1, CONV_CB), colg), pl.BlockSpec((1, CONV_CB), colg)],
        out_shape=[jax.ShapeDtypeStruct((2, l, D_FF), BF16),
                   jax.ShapeDtypeStruct((3, D_FF), F32), jax.ShapeDtypeStruct((3, D_FF), F32),
                   jax.ShapeDtypeStruct((1, D_FF), F32), jax.ShapeDtypeStruct((1, D_FF), F32)],
        scratch_shapes=[pltpu.VMEM((2 * SUBLANES, CONV_CB), F32)],
        compiler_params=_params(("parallel", "arbitrary")),
    )(h, h, h, h, dact, cw, cw, cb, cb)


def _loss_head(ff, x1, tgt, g, tl):
    l = ff.shape[0]

    def body(ff_ref, x1_ref, tg_ref, g_ref, loss_ref, dx2_ref, dff_ref, dg_ref):
        @pl.when(pl.program_id(0) == 0)
        def _():
            loss_ref[...] = jnp.zeros_like(loss_ref)
            dg_ref[...] = jnp.zeros_like(dg_ref)

        f = ff_ref[...]
        gv = g_ref[...]
        n, _ = _rms(f, gv)
        e = x1_ref[...] + n - tg_ref[...]
        loss_ref[...] += 0.5 * jnp.sum(jnp.mean(e * e, axis=-1, keepdims=True), axis=0, keepdims=True)
        dx2 = e * (1.0 / D_MODEL)
        dx2_ref[...] = dx2
        dff, dg = _rms_bwd(dx2, f, gv)
        dff_ref[...] = dff.astype(BF16)
        dg_ref[...] += dg

    return pl.pallas_call(
        body, name="loss_head", grid=(l // tl,),
        in_specs=[_row(tl, D_MODEL), _row(tl, D_MODEL), _row(tl, D_MODEL), _const((1, D_MODEL))],
        out_specs=[_const((1, LANES)), _row(tl, D_MODEL), _row(tl, D_MODEL), _const((1, D_MODEL))],
        out_shape=[jax.ShapeDtypeStruct((1, LANES), F32), jax.ShapeDtypeStruct((l, D_MODEL), F32),
                   jax.ShapeDtypeStruct((l, D_MODEL), BF16), jax.ShapeDtypeStruct((1, D_MODEL), F32)],
        compiler_params=_params(("arbitrary",)),
    )(ff, x1, tgt, g)


def _ssm_disc(lam_re, lam_im, log_dt, b_re, b_im):
    dt = jnp.exp(log_dt)[:, None]
    mag = jnp.exp(lam_re * dt)
    ang = lam_im * dt
    a_re, a_im = mag * jnp.cos(ang), mag * jnp.sin(ang)
    den = lam_re * lam_re + lam_im * lam_im
    n_re, n_im = a_re - 1.0, a_im
    z_re = (n_re * lam_re + n_im * lam_im) / den
    z_im = (n_im * lam_re - n_re * lam_im) / den
    bb_re = z_re[..., None] * b_re - z_im[..., None] * b_im
    bb_im = z_re[..., None] * b_im + z_im[..., None] * b_re
    return a_re, a_im, bb_re, bb_im


_GPB = SSM_CB // SSM_P


def _embed_b(bb):
    t = bb.transpose(0, 2, 1).reshape(SSM_NB, _GPB, SSM_H, SSM_P)
    return jnp.einsum('mjhp,jk->mjhkp', t, jnp.eye(_GPB, dtype=bb.dtype)).reshape(SSM_NB, SSM_UB, SSM_CB)


def _extract_b(d):
    t = d.reshape(SSM_NB, _GPB, SSM_H, _GPB, SSM_P)
    t = jnp.einsum('mjhkp,jk->mjhp', t, jnp.eye(_GPB, dtype=d.dtype))
    return t.reshape(SSM_G, SSM_H, SSM_P).transpose(0, 2, 1)


def _embed_c(c):
    t = c.transpose(0, 2, 1).reshape(SSM_NB, _GPB, SSM_P, SSM_H)
    return jnp.einsum('mjph,jk->mjpkh', t, jnp.eye(_GPB, dtype=c.dtype)).reshape(SSM_NB, SSM_CB, SSM_UB)


def _extract_c(d):
    t = d.reshape(SSM_NB, _GPB, SSM_P, _GPB, SSM_H)
    t = jnp.einsum('mjpkh,jk->mjph', t, jnp.eye(_GPB, dtype=d.dtype))
    return t.reshape(SSM_G, SSM_P, SSM_H).transpose(0, 2, 1)


def _scan_tables(a_re, a_im, reverse):
    ar = a_re.reshape(1, SSM_CH)
    ai = (-a_im if reverse else a_im).reshape(1, SSM_CH)
    pr, pi = [ar], [ai]
    for _ in range(SUBLANES - 1):
        pr, pi = pr + [pr[-1] * ar - pi[-1] * ai], pi + [pr[-1] * ai + pi[-1] * ar]
    rows = jnp.arange(SUBLANES)[:, None]
    out = []
    for k in (1, 2, 4):
        valid = (rows + k <= SUBLANES - 1) if reverse else (rows >= k)
        out += [jnp.where(valid, pr[k - 1], 0.0), jnp.where(valid, pi[k - 1], 0.0)]
    order = list(range(SUBLANES - 1, -1, -1)) if reverse else list(range(SUBLANES))
    out += [jnp.concatenate([pr[n] for n in order], axis=0), jnp.concatenate([pi[n] for n in order], axis=0)]
    return jnp.stack(out).astype(F32)


def _pad_heads(w, d):
    lead = w.shape[:-1]
    w = w.reshape(lead + (N_HEADS, d))
    w = jnp.pad(w, [(0, 0)] * len(lead) + [(0, 0), (0, HEAD_SLOT - d)])
    return w.reshape(lead + (HP,))


def _unpad_heads(w, d):
    lead = w.shape[:-1]
    return w.reshape(lead + (N_HEADS, HEAD_SLOT))[..., :d].reshape(lead + (N_HEADS * d,))


def _chip_major(w, axis):
    k, n = w.shape
    if axis == 0:
        return w.reshape(N_CHIPS, k // N_CHIPS, n)
    return w.reshape(k, N_CHIPS, n // N_CHIPS).transpose(1, 0, 2)


def _from_chip_major(w, axis):
    if axis == 0:
        return w.reshape(-1, w.shape[2])
    return w.transpose(1, 0, 2).reshape(w.shape[1], -1)


def _pad_w_in(w):
    z = lambda n: jnp.zeros((w.shape[0], n), w.dtype)
    return jnp.concatenate([w[:, :640], z(KR_LANE), w[:, 640:672], z(HEAD_SLOT - KR_LANE - QK_ROPE), w[:, 672:]], axis=1)


def _unpad_w_in(w):
    return jnp.concatenate([w[:, :640], w[:, P_KR + KR_LANE:P_KR + KR_LANE + QK_ROPE], w[:, P_U:]], axis=1)


def _local_step(x, positions, tgt, wts, sp):
    l = x.shape[0]
    tl = min(256, l)
    ta = min(512, l)

    inv_freq = ROPE_THETA ** (-jnp.arange(0, QK_ROPE, 2, dtype=F32) / QK_ROPE)
    ang = positions.astype(F32)[:, None] * inv_freq
    cos, sin = jnp.cos(ang), jnp.sin(ang)
    one = jnp.ones((l, KR_LANE), F32)
    rc = jnp.concatenate([one, cos, cos, jnp.ones((l, HEAD_SLOT - KR_LANE - QK_ROPE), F32)], axis=1)
    rs = jnp.concatenate([0 * one, -sin, sin, jnp.zeros((l, HEAD_SLOT - KR_LANE - QK_ROPE), F32)], axis=1)

    win = _pad_w_in(wts["w_in"])
    wuq = _pad_heads(wts["w_uq"], QK_HEAD)
    wukv = jnp.concatenate([_pad_heads(wts["w_uk"], QK_NOPE), _pad_heads(wts["w_uv"], V_HEAD)], axis=1)

    disc_in = (sp["ssm_lambda_re"], sp["ssm_lambda_im"], sp["ssm_log_dt"], sp["ssm_b_re"], sp["ssm_b_im"])
    (a_re, a_im, bb_re, bb_im), disc_vjp = jax.vjp(_ssm_disc, *disc_in)
    bre, bim = _embed_b(bb_re).astype(BF16), _embed_b(bb_im).astype(BF16)
    cre, cim = _embed_c(sp["ssm_c_re"]).astype(BF16), _embed_c(sp["ssm_c_im"]).astype(BF16)
    dvec = sp["ssm_d"].reshape(1, SSM_W)
    tab_f = _scan_tables(a_re, a_im, False)
    tab_r = _scan_tables(a_re, a_im, True)

    g1, gq, gkv = sp["mix_norm_pre"], sp["q_norm"], sp["kv_norm"]
    gpost, gpre, gfin = sp["mix_norm_post"], sp["ffn_norm_pre"], sp["ffn_norm_post"]
    bgate, bglu, convb = sp["b_gate"], sp["b_glu"], sp["conv_b"]

    hn, cq, ckv, q, k, v, u, gl = _proj_fwd(x, g1, win, gq, wuq, gkv, wukv, rc, rs, bgate, tl)
    attn, lse = _attn_fwd(q, k, v, ta)
    y1, sre, sim = _ssm_fwd(u, bre, bim, cre, cim, dvec, tab_f, ta)
    wba = jnp.pad(wts["w_branch_attn"].reshape(N_HEADS, V_HEAD, D_MODEL),
                  ((0, 0), (0, HEAD_SLOT - V_HEAD), (0, 0))).reshape(HP, D_MODEL)
    wbs, wglu, wout = wts["w_branch_ssm"], wts["w_glu"], wts["w_out"]
    a, sm, merged, z, x1, hn2, y3 = _merge_fwd(x, gl, attn, y1, wba, wbs, wglu, bglu, wout, gpost, gpre, tl)
    late = wts["late"](x1)
    wup, wdown, convw = late["w_up"], late["w_down"], late["conv_w"]
    h = _mm(hn2, wup, "ffn_up")
    act = _conv_fwd(h, convw, convb, ta)
    ff = _mm(act, wdown, "ffn_down")
    loss, dx2, dff, dgfin = _loss_head(ff, x1, tgt, gfin, tl)

    dact = _mm(dff, wdown, "ffn_down_dx", out_dtype=BF16, bt=True)
    d_wdown = _mm_tn(act, dff, "ffn_down_dw", tk_cap=D_FF // 2)
    dh, dwg, dwv, dbg, dbv = _conv_bwd(h, dact, convw, convb, ta)
    d_convw = jnp.concatenate([dwg, dwv], axis=1)
    d_convb = jnp.concatenate([dbg, dbv], axis=1)
    dhn2 = _mm(dh, wup, "ffn_up_dx", bt=True)
    d_wup = _mm_tn(hn2, dh, "ffn_up_dw", chips=True)
    behind = wts["send_grads"]("ffn", {"w_up": d_wup, "w_down": _chip_major(d_wdown, 0)})
    (dx1, dz, dbra, dbrs, dgl, dattn, dy1, dt, y2, dgpre, dgpost, dbgate, dbglu) = _merge_bwd(
        dhn2, x1, dx2, z, gl, a, sm, y1, wba, wbs, wglu, bglu, wout, gpost, gpre + behind, tl)
    d_wout = _mm_tn(merged, dz, "w_out_dw")
    d_wba = _mm_tn(attn, dbra, "w_branch_attn_dw", chips=True)
    d_wbs = _mm_tn(y3, dbrs, "w_branch_ssm_dw", chips=True)
    d_wglu = _mm_tn(y2, dt, "w_glu_dw")
    ncol = D_MODEL // N_CHIPS
    behind = wts["send_grads"]("mix", {
        "w_glu": _chip_major(d_wglu, 0),
        "w_branch_attn": d_wba.reshape(N_CHIPS, N_HEADS, HEAD_SLOT, ncol)[:, :, :V_HEAD].reshape(
            N_CHIPS, N_HEADS * V_HEAD, ncol),
        "w_branch_ssm": d_wbs,
        "w_out": _chip_major(d_wout, 0)}, after=d_wglu)
    dq, dk, dv = _attn_bwd(q, k, v, dattn, lse + behind, _attn_delta(attn, dattn, min(2048, l)), ta)
    du, dbre, dbim, dcre, dcim, dare, daim, dd = _ssm_bwd(dy1, u, sre, sim, bre, bim, cre, cim, dvec, tab_r, ta)
    behind = wts["send_grads"]("none", {}, after=du)
    gx, dql, qn, ckvn, dproj, dg1, dgq, dgkv = _proj_bwd(
        x, dx1, cq, ckv, dq, dk, dv, du, dgl, g1 + behind, win, gq, wuq, gkv, wukv, rc, rs, tl)
    d_win = _mm_tn(hn, dproj, "w_in_dw")
    d_wuq = _mm_tn(qn, dql, "w_uq_dw")
    d_wuk = _mm_tn(ckvn, dk, "w_uk_dw")
    d_wuv = _mm_tn(ckvn, dv, "w_uv_dw")

    d_lre, d_lim, d_ldt, d_bre, d_bim = disc_vjp((dare.reshape(SSM_G, SSM_P), daim.reshape(SSM_G, SSM_P),
                                                  _extract_b(dbre), _extract_b(dbim)))
    big = {
        "w_in": _chip_major(_unpad_w_in(d_win), 1),
        "w_uq": _chip_major(_unpad_heads(d_wuq, QK_HEAD), 1),
        "w_uk": _chip_major(_unpad_heads(d_wuk, QK_NOPE), 1),
        "w_uv": _chip_major(_unpad_heads(d_wuv, V_HEAD), 1),
    }
    small = {
        "conv_w": d_convw,
        "mix_norm_pre": dg1, "q_norm": dgq, "kv_norm": dgkv,
        "ssm_lambda_re": d_lre, "ssm_lambda_im": d_lim, "ssm_log_dt": d_ldt,
        "ssm_b_re": d_bre, "ssm_b_im": d_bim,
        "ssm_c_re": _extract_c(dcre), "ssm_c_im": _extract_c(dcim),
        "ssm_d": dd.reshape(SSM_G, SSM_H), "b_glu": dbglu, "b_gate": dbgate,
        "mix_norm_post": dgpost, "ffn_norm_pre": dgpre, "conv_b": d_convb, "ffn_norm_post": dgfin,
    }
    return loss[0, 0], gx, big, small


_ANY = pl.BlockSpec(memory_space=pl.ANY)


ROW_TILE = 16


def _place():
    x, y, c = lax.axis_index("x"), lax.axis_index("y"), lax.axis_index("c")
    return x, y, c, 2 * x + y, [(1 - x, y), (x, 1 - y), (1 - x, 1 - y)]


def _half(rows, which):
    hr = rows // 2
    return pl.ds(pl.multiple_of(which * hr, ROW_TILE), hr)


def _remote(src, dst, send_sems, recv_sems, n, dev):
    return pltpu.make_async_remote_copy(src_ref=src, dst_ref=dst, send_sem=send_sems.at[n], recv_sem=recv_sems.at[n],
                                        device_id=dev, device_id_type=MESH)


def _gather_big(shards):
    nw = len(shards)
    rows = [s.shape[0] for s in shards]

    def body(*refs):
        ins, outs = refs[:nw], refs[nw:2 * nw]
        ici_send, ici_recv, d2d_send, d2d_recv = refs[2 * nw:]
        x, y, c, me, peers = _place()
        sent = []
        for i in range(nw):
            for p, (px, py) in enumerate(peers):
                cp = _remote(ins[i].at[_half(rows[i], c)], outs[i].at[me, _half(rows[i], c)], ici_send, ici_recv,
                             3 * i + p, (px, py, c))
                cp.start()
                sent.append(cp)
        for p, (px, py) in enumerate(peers):
            for i in range(nw):
                blk = outs[i].at[2 * px + py, _half(rows[i], c)]
                _remote(blk, blk, ici_send, ici_recv, 3 * i + p, (px, py, c)).wait_recv()
                cp = _remote(blk, blk, d2d_send, d2d_recv, 3 * i + p, (x, y, 1 - c))
                cp.start()
                sent.append(cp)
        for p, (px, py) in enumerate(peers):
            for i in range(nw):
                blk = outs[i].at[2 * px + py, _half(rows[i], 1 - c)]
                _remote(blk, blk, d2d_send, d2d_recv, 3 * i + p, (x, y, 1 - c)).wait_recv()
        for cp in sent:
            cp.wait_send()

    dma = pltpu.SemaphoreType.DMA
    return pl.pallas_call(
        body, name="gather_weights", in_specs=[_ANY] * nw, out_specs=[_ANY] * nw,
        out_shape=[jax.ShapeDtypeStruct((N_CHIPS,) + s.shape, s.dtype) for s in shards],
        scratch_shapes=[dma((3 * nw,)), dma((3 * nw,)), dma((3 * nw,)), dma((3 * nw,))],
    )(*shards)


_HBM = pl.BlockSpec(memory_space=pltpu.HBM)
_SEM = pl.BlockSpec(memory_space=pltpu.SEMAPHORE)
_DATAFLOW = pltpu.SideEffectType.DATAFLOW_SIDE_EFFECTING


def _exchange_start(shards, name, scatter):
    nw = len(shards)
    lands = [lax.empty(s.shape if scatter else (N_CHIPS,) + s.shape, s.dtype) for s in shards]

    def body(*refs):
        ins, zones = refs[:nw], refs[nw:2 * nw]
        send_sems, recv_sems, token = refs[2 * nw], refs[2 * nw + 1], refs[-1]
        x, y, c, me, peers = _place()
        for i in range(nw):
            for p, (px, py) in enumerate(peers):
                src = ins[i].at[2 * px + py] if scatter else ins[i]
                _remote(src, zones[i].at[me], send_sems, recv_sems, 3 * i + p, (px, py, c)).start()
        token[...] = jnp.zeros_like(token)

    thru = [pltpu.HBM(a.shape, a.dtype) for a in list(shards) + lands]
    dma = pltpu.SemaphoreType.DMA
    outs = pl.pallas_call(
        body, name=name,
        out_shape=(dma((3 * nw,)), dma((3 * nw,)), *thru, jax.ShapeDtypeStruct((SUBLANES, LANES), F32)),
        in_specs=[_HBM] * (2 * nw),
        out_specs=(_SEM, _SEM, *([_HBM] * (2 * nw)), pl.BlockSpec(memory_space=pltpu.VMEM)),
        input_output_aliases={i: 2 + i for i in range(2 * nw)},
        compiler_params=pltpu.CompilerParams(has_side_effects=_DATAFLOW),
    )(*[pltpu.with_memory_space_constraint(a, pltpu.HBM) for a in list(shards) + lands])
    return outs[0], outs[1], list(outs[2:2 + nw]), list(outs[2 + nw:2 + 2 * nw]), outs[-1]


def _exchange_wait(send_sems, recv_sems, shards, lands, after, name, scatter):
    nw = len(shards)

    def body(*refs):
        ins, zones = refs[:nw], refs[nw:2 * nw]
        send_sems, recv_sems = refs[2 * nw], refs[2 * nw + 1]
        x, y, c, me, peers = _place()
        for i in range(nw):
            for p, (px, py) in enumerate(peers):
                src = ins[i].at[2 * px + py] if scatter else ins[i]
                cp = _remote(src, zones[i].at[2 * px + py], send_sems, recv_sems, 3 * i + p, (px, py, c))
                cp.wait_send()
                cp.wait_recv()

    both = list(shards) + list(lands)
    outs = pl.pallas_call(
        body, name=name,
        out_shape=tuple(pltpu.HBM(a.shape, a.dtype) for a in both),
        in_specs=(*([_HBM] * (2 * nw)), _SEM, _SEM, _ANY), out_specs=[_HBM] * (2 * nw),
        input_output_aliases={i: i for i in range(2 * nw)},
        compiler_params=pltpu.CompilerParams(has_side_effects=_DATAFLOW),
    )(*both, send_sems, recv_sems, after)
    return list(outs[:nw]), list(outs[nw:])


def _sibling_start(grads, name):
    nw = len(grads)
    lands = [lax.empty((N_CHIPS, g.shape[1] // 2, g.shape[2]), g.dtype) for g in grads]

    def body(*refs):
        ins, zones = refs[:nw], refs[nw:2 * nw]
        send_sems, recv_sems, token = refs[2 * nw], refs[2 * nw + 1], refs[-1]
        x, y, c, _, _ = _place()
        for i in range(nw):
            _remote(ins[i].at[pl.ds(0, N_CHIPS), _half(grads[i].shape[1], 1 - c)], zones[i], send_sems, recv_sems,
                    i, (x, y, 1 - c)).start()
        token[...] = jnp.zeros_like(token)

    thru = [pltpu.HBM(a.shape, a.dtype) for a in list(grads) + lands]
    dma = pltpu.SemaphoreType.DMA
    outs = pl.pallas_call(
        body, name=name,
        out_shape=(dma((nw,)), dma((nw,)), *thru, jax.ShapeDtypeStruct((SUBLANES, LANES), F32)),
        in_specs=[_HBM] * (2 * nw),
        out_specs=(_SEM, _SEM, *([_HBM] * (2 * nw)), pl.BlockSpec(memory_space=pltpu.VMEM)),
        input_output_aliases={i: 2 + i for i in range(2 * nw)},
        compiler_params=pltpu.CompilerParams(has_side_effects=_DATAFLOW),
    )(*[pltpu.with_memory_space_constraint(a, pltpu.HBM) for a in list(grads) + lands])
    return outs[0], outs[1], list(outs[2:2 + nw]), list(outs[2 + nw:2 + 2 * nw]), outs[-1]


def _sibling_wait(send_sems, recv_sems, grads, lands, after, name):
    nw = len(grads)

    def body(*refs):
        ins, zones = refs[:nw], refs[nw:2 * nw]
        send_sems, recv_sems = refs[2 * nw], refs[2 * nw + 1]
        x, y, c, _, _ = _place()
        for i in range(nw):
            cp = _remote(ins[i].at[pl.ds(0, N_CHIPS), _half(grads[i].shape[1], 1 - c)], zones[i], send_sems, recv_sems,
                         i, (x, y, 1 - c))
            cp.wait_send()
            cp.wait_recv()

    both = list(grads) + list(lands)
    outs = pl.pallas_call(
        body, name=name,
        out_shape=tuple(pltpu.HBM(a.shape, a.dtype) for a in both),
        in_specs=(*([_HBM] * (2 * nw)), _SEM, _SEM, _ANY), out_specs=[_HBM] * (2 * nw),
        input_output_aliases={i: i for i in range(2 * nw)},
        compiler_params=pltpu.CompilerParams(has_side_effects=_DATAFLOW),
    )(*both, send_sems, recv_sems, after)
    return list(outs[:nw]), list(outs[nw:])


def _reduce_to_sibling(grads, name):
    nw = len(grads)

    def body(*refs):
        ins, outs = refs[:nw], refs[nw:2 * nw]
        send_sems, recv_sems = refs[2 * nw:]
        x, y, c, _, _ = _place()
        sent = []
        for i in range(nw):
            cp = _remote(ins[i].at[pl.ds(0, N_CHIPS), _half(grads[i].shape[1], 1 - c)], outs[i], send_sems, recv_sems,
                         i, (x, y, 1 - c))
            cp.start()
            sent.append(cp)
        for cp in sent:
            cp.wait()

    dma = pltpu.SemaphoreType.DMA
    return pl.pallas_call(
        body, name=name, in_specs=[_ANY] * nw, out_specs=[_ANY] * nw,
        out_shape=[jax.ShapeDtypeStruct((N_CHIPS, g.shape[1] // 2, g.shape[2]), g.dtype) for g in grads],
        scratch_shapes=[dma((nw,)), dma((nw,))],
    )(*grads)


def _reduce_back(totals, name):
    nw = len(totals)

    def body(*refs):
        outs = refs[nw:2 * nw]
        send_sems, recv_sems = refs[2 * nw:]
        x, y, c, _, _ = _place()
        sent = []
        for i in range(nw):
            blk = outs[i].at[_half(totals[i].shape[0], c)]
            cp = _remote(blk, blk, send_sems, recv_sems, i, (x, y, 1 - c))
            cp.start()
            sent.append(cp)
        for i in range(nw):
            blk = outs[i].at[_half(totals[i].shape[0], 1 - c)]
            _remote(blk, blk, send_sems, recv_sems, i, (x, y, 1 - c)).wait_recv()
        for cp in sent:
            cp.wait_send()

    dma = pltpu.SemaphoreType.DMA
    return pl.pallas_call(
        body, name=name, in_specs=[_ANY] * nw, out_specs=[_ANY] * nw,
        out_shape=[jax.ShapeDtypeStruct(t.shape, t.dtype) for t in totals],
        input_output_aliases={i: i for i in range(nw)},
        scratch_shapes=[dma((nw,)), dma((nw,))],
    )(*totals)


def _all_reduce_small(v, name):
    rows, w = v.shape
    hr = rows // 2
    assert hr % SUBLANES == 0

    def body(v_ref, out_ref, sib_ref, half_ref, chips_ref, send_sems, recv_sems):
        x, y, c, me, peers = _place()
        sibling = (x, y, 1 - c)
        mine = pl.ds(pl.multiple_of(c * hr, SUBLANES), hr)
        other = pl.ds(pl.multiple_of((1 - c) * hr, SUBLANES), hr)
        cp = _remote(v_ref, sib_ref, send_sems, recv_sems, 0, sibling)
        cp.start()
        cp.wait()
        half_ref[...] = v_ref[mine, :] + sib_ref[mine, :]
        sent = []
        for p, (px, py) in enumerate(peers):
            cp = _remote(half_ref, chips_ref.at[me], send_sems, recv_sems, 1 + p, (px, py, c))
            cp.start()
            sent.append(cp)
        chips_ref[me] = half_ref[...]
        for p, (px, py) in enumerate(peers):
            _remote(half_ref, chips_ref.at[2 * px + py], send_sems, recv_sems, 1 + p, (px, py, c)).wait_recv()
        for cp in sent:
            cp.wait_send()
        out_ref[mine, :] = ((chips_ref[0] + chips_ref[1]) + chips_ref[2]) + chips_ref[3]
        cp = _remote(out_ref.at[mine], out_ref.at[mine], send_sems, recv_sems, 4, sibling)
        cp.start()
        _remote(out_ref.at[other], out_ref.at[other], send_sems, recv_sems, 4, sibling).wait_recv()
        cp.wait_send()

    vm = pl.BlockSpec(memory_space=pltpu.VMEM)
    return pl.pallas_call(
        body, name=name, in_specs=[vm], out_specs=vm,
        out_shape=jax.ShapeDtypeStruct((rows, w), F32),
        scratch_shapes=[pltpu.VMEM((rows, w), F32), pltpu.VMEM((hr, w), F32), pltpu.VMEM((N_CHIPS, hr, w), F32),
                        pltpu.SemaphoreType.DMA((5,)), pltpu.SemaphoreType.DMA((5,))],
        compiler_params=pltpu.CompilerParams(vmem_limit_bytes=VMEM_LIMIT),
    )(v)


ELEMENTWISE_BLOCK = 512 * 1024


def _rows_tile(rows, cols):
    best = None
    for t in range(SUBLANES, rows + 1, SUBLANES):
        if rows % t == 0 and t * cols <= ELEMENTWISE_BLOCK:
            best = t
    return rows if best is None else best


def _add_pair(g, t, core, name):
    nb, n, w = t.shape
    tr = _rows_tile(n, w)
    steps = n // tr

    def body(core_ref, g_ref, t_ref, o_ref):
        o_ref[...] = (g_ref[...] + t_ref[...]).astype(BF16)

    spec = pl.BlockSpec((1, tr, w), lambda j, i, core_ref: (j, i, 0))
    return pl.pallas_call(
        body, name=name,
        grid_spec=pltpu.PrefetchScalarGridSpec(
            num_scalar_prefetch=1, grid=(nb, steps),
            in_specs=[pl.BlockSpec((1, tr, w), lambda j, i, core_ref: (j, core_ref[0] * steps + i, 0)), spec],
            out_specs=spec),
        out_shape=jax.ShapeDtypeStruct(t.shape, BF16),
        compiler_params=_params(("parallel", "parallel")))(core, g, t)


def _add_chips(landed, pairs, place, name):
    nb, n, w = landed.shape
    tr = _rows_tile(n, w)
    steps = n // tr

    def body(place_ref, r_ref, own_ref, o_ref):
        me = place_ref[0]
        acc = None
        for k in range(nb):
            blk = jnp.where(me == k, own_ref[0], r_ref[k]).astype(F32)
            acc = blk if acc is None else acc + blk
        o_ref[...] = acc

    return pl.pallas_call(
        body, name=name,
        grid_spec=pltpu.PrefetchScalarGridSpec(
            num_scalar_prefetch=1, grid=(steps,),
            in_specs=[pl.BlockSpec((nb, tr, w), lambda i, place_ref: (0, i, 0)),
                      pl.BlockSpec((1, tr, w), lambda i, place_ref: (place_ref[0], i, 0))],
            out_specs=pl.BlockSpec((tr, w), lambda i, place_ref: (place_ref[1] * steps + i, 0))),
        out_shape=jax.ShapeDtypeStruct((2 * n, w), F32),
        compiler_params=_params(("parallel",)))(place, landed, pairs)


def _adamw(w, g, m, v, name):
    rows, wd = w.shape
    tr = _rows_tile(rows, wd)
    c1 = 1.0 - ADAM_B1 ** ADAM_STEP
    c2 = 1.0 - ADAM_B2 ** ADAM_STEP

    def body(w_ref, g_ref, m_ref, v_ref, go_ref, d_ref, mo_ref, vo_ref):
        gv = g_ref[...]
        go_ref[...] = gv
        m2 = ADAM_B1 * m_ref[...] + (1.0 - ADAM_B1) * gv
        v2 = ADAM_B2 * v_ref[...] + (1.0 - ADAM_B2) * (gv * gv)
        mo_ref[...] = m2
        vo_ref[...] = v2
        d_ref[...] = -ADAM_LR * ((m2 / c1) / (jnp.sqrt(v2 / c2) + ADAM_EPS) + ADAM_WD * w_ref[...])

    spec = pl.BlockSpec((tr, wd), lambda i: (i, 0))
    shp = jax.ShapeDtypeStruct((rows, wd), F32)
    return pl.pallas_call(body, name=name, grid=(rows // tr,), in_specs=[spec] * 4, out_specs=[spec] * 4,
                          out_shape=[shp] * 4, compiler_params=_params(("parallel",)))(w, g, m, v)


BIG = [("w_in", (1024, 3232), 1), ("w_uq", (384, 768), 1), ("w_uk", (256, 512), 1), ("w_uv", (256, 512), 1),
       ("w_glu", (512, 512), 0), ("w_branch_attn", (512, 1024), 1), ("w_branch_ssm", (512, 1024), 1),
       ("w_out", (1024, 1024), 0), ("w_up", (1024, 5632), 1), ("conv_w", (3, 5632), 1), ("w_down", (2816, 1024), 0)]
SMALL = [("mix_norm_pre", (1024,)), ("q_norm", (384,)), ("kv_norm", (256,)), ("ssm_lambda_re", (32, 64)),
         ("ssm_lambda_im", (32, 64)), ("ssm_log_dt", (32,)), ("ssm_b_re", (32, 64, 16)), ("ssm_b_im", (32, 64, 16)),
         ("ssm_c_re", (32, 16, 64)), ("ssm_c_im", (32, 16, 64)), ("ssm_d", (32, 16)), ("b_glu", (512,)),
         ("b_gate", (2048,)), ("mix_norm_post", (1024,)), ("ffn_norm_pre", (1024,)), ("conv_b", (5632,)),
         ("ffn_norm_post", (1024,))]
MATMUL_W = [b for b in BIG if b[0] != "conv_w"]
LATE_W = ("w_up", "w_down", "conv_w")
CONV_W_SHAPE = (3, 2 * D_FF)
CONV_W_SHARD = (3, 2 * D_FF // N_CHIPS)
SMALL_SUM = [("loss", (1,))] + SMALL + [("conv_w", CONV_W_SHAPE)]
SMALL_ADAM = SMALL + [("conv_w", CONV_W_SHARD)]


def _pack_flat(layout, vals):
    flat = jnp.concatenate([vals[n].astype(F32).reshape(-1) for n, _ in layout])
    rows = -(-(-(-flat.shape[0] // FLAT_W)) // (2 * SUBLANES)) * 2 * SUBLANES
    return jnp.pad(flat, (0, rows * FLAT_W - flat.shape[0])).reshape(rows, FLAT_W)


def _unpack_flat(layout, flat):
    flat = flat.reshape(-1)
    out = {}
    o = 0
    for name, shape in layout:
        n = math.prod(shape)
        out[name] = flat[o:o + n].reshape(shape)
        o += n
    return out


_ARG_NAMES = ["x", "positions"] + [n for n in (
    "mix_norm_pre", "w_in", "q_norm", "w_uq", "kv_norm", "w_uk", "w_uv", "ssm_lambda_re", "ssm_lambda_im", "ssm_log_dt",
    "ssm_b_re", "ssm_b_im", "ssm_c_re", "ssm_c_im", "ssm_d", "w_glu", "b_glu", "w_branch_attn", "w_branch_ssm",
    "b_gate", "w_out", "mix_norm_post", "ffn_norm_pre", "w_up", "conv_w", "conv_b", "w_down", "ffn_norm_post")]
_WEIGHTS = _ARG_NAMES[2:]


def _gather_weights(w):
    early = [b for b in MATMUL_W if b[0] not in LATE_W]
    late = [b for b in BIG if b[0] in LATE_W]
    own = (jnp.arange(N_CHIPS) == 2 * lax.axis_index("x") + lax.axis_index("y"))[:, None, None]

    def whole(layout, mine, gathered):
        return {name: _from_chip_major(jnp.where(own, s[None], g), axis)
                for (name, _, axis), s, g in zip(layout, mine, gathered)}

    mine = [w[name].astype(BF16) for name, _, _ in early]
    gathered = _gather_big(mine)
    full = whole(early, mine, gathered)
    mine_late = [w[name].astype(F32 if name == "conv_w" else BF16) for name, _, _ in late]
    _, mine_late = lax.optimization_barrier((gathered[0], mine_late))
    send_sems, recv_sems, shards_thru, lands_thru, token = _exchange_start(mine_late, "gather_late_start", scatter=False)

    def late_weights(after):
        shards, lands = _exchange_wait(send_sems, recv_sems, shards_thru, lands_thru, after, "gather_late_wait",
                                       scatter=False)
        return whole(late, shards, lands)

    full["late"] = late_weights
    full["token"] = token[0, 0]
    return full


def _pair_sums(names, grads, tag):
    core = lax.axis_index("c").astype(jnp.int32).reshape(1)
    theirs = _reduce_to_sibling(grads, "reduce_grads_d2d" + tag)
    return [_add_pair(g, t, core, "reduce_pair_" + n) for n, g, t in zip(names, grads, theirs)]


def _send_grads(tag, grads, after, flying, pending):
    token = jnp.zeros((), F32)
    if flying:
        tag0, names0, state0 = flying.pop()
        core = lax.axis_index("c").astype(jnp.int32).reshape(1)
        mine, theirs = _sibling_wait(*state0, after, "reduce_" + tag0 + "_d2d_wait")
        pairs = [_add_pair(g, t, core, "reduce_pair_" + n) for n, g, t in zip(names0, mine, theirs)]
        send_sems, recv_sems, pairs_thru, lands_thru, tok = _exchange_start(pairs, "reduce_" + tag0 + "_start",
                                                                          scatter=True)
        pending.append((tag0, names0, send_sems, recv_sems, pairs_thru, lands_thru))
        token = token + tok[0, 0]
    if grads:
        names = list(grads)
        send_sems, recv_sems, grads_thru, lands_thru, tok = _sibling_start([grads[n] for n in names],
                                                                          "reduce_" + tag + "_d2d_start")
        flying.append((tag, names, (send_sems, recv_sems, grads_thru, lands_thru)))
        token = token + tok[0, 0]
    return token


def _reduce_grads(gbig, pending, loss, gsmall, use_sent):
    core = lax.axis_index("c").astype(jnp.int32).reshape(1)
    chip = (2 * lax.axis_index("x") + lax.axis_index("y")).astype(jnp.int32).reshape(1)
    place = jnp.concatenate([chip, core])

    def finish(names, pairs, landed, name):
        totals = [_add_chips(r, p, place, "reduce_chips_" + n) for n, r, p in zip(names, landed, pairs)]
        return dict(zip(names, _reduce_back(totals, name)))

    names = list(gbig)
    pairs = _pair_sums(names, [gbig[n] for n in names], "")
    send_sems, recv_sems, pairs_thru, lands_thru, token = _exchange_start(pairs, "reduce_last_start", scatter=True)
    sent_names, sent_pairs, sent_landed = [], [], []
    for tag, group, g_send, g_recv, g_pairs, g_lands in pending:
        got_pairs, got_landed = _exchange_wait(g_send, g_recv, g_pairs, g_lands, token, "reduce_" + tag + "_wait",
                                               scatter=True)
        sent_names, sent_pairs, sent_landed = sent_names + group, sent_pairs + got_pairs, sent_landed + got_landed
    g_sent = finish(sent_names, sent_pairs, sent_landed, "reduce_back_sent")
    vals = dict(gsmall)
    vals["loss"] = loss
    small_red = _unpack_flat(SMALL_SUM, _all_reduce_small(_pack_flat(SMALL_SUM, vals), "reduce_small"))
    after = use_sent(g_sent, small_red)
    pairs, landed = _exchange_wait(send_sems, recv_sems, pairs_thru, lands_thru, after, "reduce_last_wait", scatter=True)
    return finish(names, pairs, landed, "reduce_back_last"), small_red


def _step(args):
    x = args["x"][0]
    positions = args["positions"][0]
    tgt = args["loss_target"][0]
    w = {n: args[n][0] for n in _WEIGHTS}
    m = {n: args["m_" + n][0] for n in _WEIGHTS}
    v = {n: args["v_" + n][0] for n in _WEIGHTS}

    full = _gather_weights(w)
    sp = {n: w[n].reshape(s) for n, s in SMALL}
    for n in ("mix_norm_pre", "q_norm", "kv_norm", "b_glu", "b_gate", "mix_norm_post", "ffn_norm_pre", "conv_b",
              "ffn_norm_post"):
        sp[n] = sp[n].reshape(1, -1)
    sp["mix_norm_pre"] = sp["mix_norm_pre"] + full.pop("token")
    pending, flying = [], []
    full["send_grads"] = lambda tag, grads, after=None: _send_grads(tag, grads, after, flying, pending)
    loss, gx, gbig, gsmall = _local_step(x, positions, tgt, full, sp)
    outs = {}

    def adam_big(g_red):
        for name in g_red:
            g2, d, m2, v2 = _adamw(w[name], g_red[name], m[name], v[name], "adamw_" + name)
            outs["grad_" + name], outs["delta_" + name], outs["new_m_" + name], outs["new_v_" + name] = g2, d, m2, v2
        return v2

    def use_sent(g_sent, small_red):
        chip = 2 * lax.axis_index("x") + lax.axis_index("y")
        grads = dict(small_red)
        grads["conv_w"] = lax.dynamic_slice_in_dim(small_red["conv_w"], chip * CONV_W_SHARD[1], CONV_W_SHARD[1], axis=1)
        outs.update({"grad_" + n: grads[n] for n, _ in SMALL_ADAM})
        _, d_sm, m_sm, v_sm = _adamw(_pack_flat(SMALL_ADAM, w), _pack_flat(SMALL_ADAM, grads),
                                     _pack_flat(SMALL_ADAM, m), _pack_flat(SMALL_ADAM, v), "adamw_small")
        for prefix, flat in (("delta_", d_sm), ("new_m_", m_sm), ("new_v_", v_sm)):
            for n, val in _unpack_flat(SMALL_ADAM, flat).items():
                outs[prefix + n] = val
        return adam_big(g_sent)

    g_last, small_red = _reduce_grads(gbig, pending, loss, gsmall, use_sent)
    adam_big(g_last)
    outs = {n: val.reshape(args[n.split("_", 1)[1] if not n.startswith("new_") else n[6:]].shape)
            for n, val in outs.items()}
    res = [small_red["loss"][0], gx[None]]
    for prefix in ("grad_", "delta_", "new_m_", "new_v_"):
        res += [outs[prefix + n] for n in _WEIGHTS]
    return tuple(res)


def kernel(x, positions, mix_norm_pre, w_in, q_norm, w_uq, kv_norm, w_uk, w_uv, ssm_lambda_re, ssm_lambda_im, ssm_log_dt, ssm_b_re, ssm_b_im, ssm_c_re, ssm_c_im, ssm_d, w_glu, b_glu, w_branch_attn, w_branch_ssm, b_gate, w_out, mix_norm_post, ffn_norm_pre, w_up, conv_w, conv_b, w_down, ffn_norm_post, loss_target, m_mix_norm_pre, m_w_in, m_q_norm, m_w_uq, m_kv_norm, m_w_uk, m_w_uv, m_ssm_lambda_re, m_ssm_lambda_im, m_ssm_log_dt, m_ssm_b_re, m_ssm_b_im, m_ssm_c_re, m_ssm_c_im, m_ssm_d, m_w_glu, m_b_glu, m_w_branch_attn, m_w_branch_ssm, m_b_gate, m_w_out, m_mix_norm_post, m_ffn_norm_pre, m_w_up, m_conv_w, m_conv_b, m_w_down, m_ffn_norm_post, v_mix_norm_pre, v_w_in, v_q_norm, v_w_uq, v_kv_norm, v_w_uk, v_w_uv, v_ssm_lambda_re, v_ssm_lambda_im, v_ssm_log_dt, v_ssm_b_re, v_ssm_b_im, v_ssm_c_re, v_ssm_c_im, v_ssm_d, v_w_glu, v_b_glu, v_w_branch_attn, v_w_branch_ssm, v_b_gate, v_w_out, v_mix_norm_post, v_ffn_norm_pre, v_w_up, v_conv_w, v_conv_b, v_w_down, v_ffn_norm_post):
    given = dict(locals())
    return _step(given)
```

```python
import math

import jax
import jax.numpy as jnp
from jax import lax
from jax.experimental import pallas as pl
from jax.experimental.pallas import tpu as pltpu

F32 = jnp.float32
BF16 = jnp.bfloat16
MESH = pl.DeviceIdType.MESH

D_MODEL = 1024
N_HEADS = 8
QK_NOPE = 64
QK_ROPE = 32
QK_HEAD = QK_NOPE + QK_ROPE
V_HEAD = 64
Q_RANK = 384
KV_RANK = 256
ROPE_THETA = 10000.0
SSM_W = 512
SSM_H = 16
SSM_G = 32
SSM_P = 64
SSM_CH = SSM_G * SSM_P
D_FF = 2816
EPS = 1e-6
ADAM_LR = 0.001
ADAM_B1 = 0.9
ADAM_B2 = 0.999
ADAM_EPS = 1e-08
ADAM_WD = 0.01
ADAM_STEP = 10

LANES = 128
SUBLANES = 8
VMEM_LIMIT = 56 * 1024 * 1024

HEAD_SLOT = LANES
HP = N_HEADS * HEAD_SLOT
P_CQ, P_CKV, P_KR, P_U, P_GL, P_END = 0, 384, 640, 768, 1280, 3328
KR_LANE = 64

FLAT_W = 1024
N_CHIPS = 4


def _tile(n, cap):
    if n <= cap:
        return n
    best = None
    for t in range(LANES, cap + 1, LANES):
        if n % t == 0:
            best = t
    assert best is not None, (n, cap)
    return best


def _params(sem):
    return pltpu.CompilerParams(dimension_semantics=sem, vmem_limit_bytes=VMEM_LIMIT)


def _dot(a, b):
    return jnp.dot(a, b, preferred_element_type=F32)


def _dot_nt(a, b):
    return lax.dot_general(a, b, (((1,), (1,)), ((), ())), preferred_element_type=F32)


def _dot_tn(a, b):
    return lax.dot_general(a, b, (((0,), (0,)), ((), ())), preferred_element_type=F32)


def _rms(x, g):
    r = lax.rsqrt(jnp.mean(x * x, axis=-1, keepdims=True) + EPS)
    return x * r * g, r


def _rms_bwd(dy, x, g):
    r = lax.rsqrt(jnp.mean(x * x, axis=-1, keepdims=True) + EPS)
    dyg = dy * g
    dx = r * dyg - x * (r * r * r) * jnp.mean(dyg * x, axis=-1, keepdims=True)
    dg = jnp.sum(dy * x * r, axis=0, keepdims=True)
    return dx, dg


_GELU_K0 = math.sqrt(2.0 / math.pi)
_GELU_K1 = 0.044715


def _gelu(x):
    th = jnp.tanh(_GELU_K0 * (x + _GELU_K1 * x * x * x))
    return 0.5 * x * (1.0 + th)


def _gelu_grad(x):
    th = jnp.tanh(_GELU_K0 * (x + _GELU_K1 * x * x * x))
    return 0.5 * (1.0 + th) + 0.5 * x * (1.0 - th * th) * _GELU_K0 * (1.0 + 3.0 * _GELU_K1 * x * x)


def _sigmoid(x):
    return 1.0 / (1.0 + jnp.exp(-x))


def _rope(q, c, s):
    n = q.shape[1]
    lane = lax.broadcasted_iota(jnp.int32, q.shape, 1) % HEAD_SLOT
    sw = jnp.where(lane < KR_LANE + QK_ROPE // 2, pltpu.roll(q, n - QK_ROPE // 2, 1), pltpu.roll(q, QK_ROPE // 2, 1))
    return q * c + sw * s


def _rope_bwd(dy, c, s):
    n = dy.shape[1]
    t = dy * s
    lane = lax.broadcasted_iota(jnp.int32, dy.shape, 1) % HEAD_SLOT
    sw = jnp.where(lane < KR_LANE + QK_ROPE // 2, pltpu.roll(t, n - QK_ROPE // 2, 1), pltpu.roll(t, QK_ROPE // 2, 1))
    rope_lane = jnp.logical_and(lane >= KR_LANE, lane < KR_LANE + QK_ROPE)
    return dy * c + jnp.where(rope_lane, sw, 0.0)


def _shift_down(x, k, halo):
    xs = pltpu.roll(x, k, 0)
    hs = pltpu.roll(halo, k, 0)
    rows = lax.broadcasted_iota(jnp.int32, halo.shape, 0)
    top = jnp.where(rows < k, hs, xs[0:SUBLANES])
    return jnp.concatenate([top, xs[SUBLANES:]], axis=0)


def _shift_up(x, k, halo):
    t = x.shape[0]
    xs = pltpu.roll(x, t - k, 0)
    hs = pltpu.roll(halo, SUBLANES - k, 0)
    rows = lax.broadcasted_iota(jnp.int32, halo.shape, 0)
    bot = jnp.where(rows >= SUBLANES - k, hs, xs[t - SUBLANES:])
    return jnp.concatenate([xs[:t - SUBLANES], bot], axis=0)


def _mm(a, b, name, out_dtype=F32, bt=False, tm_cap=1024, tn_cap=1408):
    m, k = a.shape[-2:]
    parts = a.shape[0] if a.ndim == 3 else 1
    n = b.shape[0] if bt else b.shape[1]
    tm = min(tm_cap, m)
    tn = _tile(n, tn_cap)

    def body(a_ref, b_ref, o_ref):
        if not bt:
            o_ref[...] = _dot(a_ref[...], b_ref[...]).astype(out_dtype)
        elif parts == 1:
            o_ref[...] = _dot_nt(a_ref[...], b_ref[...]).astype(out_dtype)
        else:
            @pl.when(pl.program_id(2) == 0)
            def _():
                o_ref[...] = jnp.zeros_like(o_ref)

            o_ref[...] += _dot_nt(a_ref[...], b_ref[...])

    if bt:
        b_spec = pl.BlockSpec((tn, k), lambda j, i, s: (j, s))
    else:
        b_spec = pl.BlockSpec((k, tn), lambda j, i, s: (0, j))
    if a.ndim == 3:
        assert bt and out_dtype == F32
        a_spec = pl.BlockSpec((None, tm, k), lambda j, i, s: (s, i, 0))
    else:
        a_spec = pl.BlockSpec((tm, k), lambda j, i, s: (i, 0))
    return pl.pallas_call(
        body, name=name, grid=(n // tn, m // tm, parts),
        in_specs=[a_spec, b_spec],
        out_specs=pl.BlockSpec((tm, tn), lambda j, i, s: (i, j)),
        out_shape=jax.ShapeDtypeStruct((m, n), out_dtype),
        compiler_params=_params(("parallel", "parallel", "arbitrary")),
    )(a, b)


def _mm_tn(a, b, name, tk_cap=1024, tn_cap=1664, tl_cap=1024, chips=False):
    l, k = a.shape
    tk = _tile(k, tk_cap)
    tl = min(tl_cap, l)

    def body(a_ref, b_ref, o_ref):
        @pl.when(pl.program_id(2) == 0)
        def _():
            o_ref[...] = jnp.zeros_like(o_ref)

        o_ref[...] += _dot_tn(a_ref[...], b_ref[...])

    if chips:
        n = b.shape[-1] * (b.shape[0] if b.ndim == 3 else 1)
        tn = n // N_CHIPS
        assert tn % LANES == 0
        if b.ndim == 3:
            per = N_CHIPS // b.shape[0]
            b_spec = pl.BlockSpec((None, tl, tn), lambda i, j, r: (j // per, r, j % per))
        else:
            b_spec = pl.BlockSpec((tl, tn), lambda i, j, r: (r, j))
        out_spec = pl.BlockSpec((None, tk, tn), lambda i, j, r: (j, i, 0))
        out_shape = jax.ShapeDtypeStruct((N_CHIPS, k, tn), F32)
    else:
        n = b.shape[1]
        tn = _tile(n, tn_cap)
        b_spec = pl.BlockSpec((tl, tn), lambda i, j, r: (r, j))
        out_spec = pl.BlockSpec((tk, tn), lambda i, j, r: (i, j))
        out_shape = jax.ShapeDtypeStruct((k, n), F32)
    return pl.pallas_call(
        body, name=name, grid=(k // tk, n // tn, l // tl),
        in_specs=[pl.BlockSpec((tl, tk), lambda i, j, r: (r, i)), b_spec],
        out_specs=out_spec, out_shape=out_shape,
        compiler_params=_params(("parallel", "parallel", "arbitrary")),
    )(a, b)


def _row(tl, n):
    return pl.BlockSpec((tl, n), lambda i: (i, 0))


def _const(shape):
    return pl.BlockSpec(shape, lambda i: tuple(0 for _ in shape))


def _proj_fwd(x, g1, win, gq, wuq, gkv, wukv, rc, rs, bg, tl):
    l = x.shape[0]

    def body(x_ref, g1_ref, win_ref, gq_ref, wuq_ref, gkv_ref, wukv_ref, rc_ref, rs_ref, bg_ref,
             hn_ref, cq_ref, ckv_ref, q_ref, k_ref, v_ref, u_ref, gl_ref):
        hn, _ = _rms(x_ref[...], g1_ref[...])
        hnb = hn.astype(BF16)
        hn_ref[...] = hnb
        proj = _dot(hnb, win_ref[...])
        cq = proj[:, P_CQ:P_CKV]
        ckv = proj[:, P_CKV:P_KR]
        kr = proj[:, P_KR:P_U]
        cq_ref[...] = cq
        ckv_ref[...] = ckv
        u_ref[...] = proj[:, P_U:P_GL]
        gl_ref[...] = proj[:, P_GL:P_END] + bg_ref[...]
        qn, _ = _rms(cq, gq_ref[...])
        q = _dot(qn.astype(BF16), wuq_ref[...])
        c1 = rc_ref[...]
        s1 = rs_ref[...]
        q_ref[...] = (_rope(q, jnp.tile(c1, (1, N_HEADS)), jnp.tile(s1, (1, N_HEADS))) * Q_PRESCALE).astype(BF16)
        ckvn, _ = _rms(ckv, gkv_ref[...])
        kv = _dot(ckvn.astype(BF16), wukv_ref[...])
        krr = _rope(kr, c1, s1)
        k_ref[...] = (kv[:, :HP] + jnp.tile(krr, (1, N_HEADS))).astype(BF16)
        v_ref[...] = kv[:, HP:].astype(BF16)

    outs = [(D_MODEL, BF16), (Q_RANK, F32), (KV_RANK, F32), (HP, BF16), (HP, BF16), (HP, BF16),
            (SSM_W, F32), (2 * D_MODEL, F32)]
    return pl.pallas_call(
        body, name="proj_fwd", grid=(l // tl,),
        in_specs=[_row(tl, D_MODEL), _const((1, D_MODEL)), _const((D_MODEL, P_END)), _const((1, Q_RANK)),
                  _const((Q_RANK, HP)), _const((1, KV_RANK)), _const((KV_RANK, 2 * HP)),
                  _row(tl, HEAD_SLOT), _row(tl, HEAD_SLOT), _const((1, 2 * D_MODEL))],
        out_specs=[_row(tl, n) for n, _ in outs],
        out_shape=[jax.ShapeDtypeStruct((l, n), dt) for n, dt in outs],
        compiler_params=_params(("parallel",)),
    )(x, g1, win, gq, wuq, gkv, wukv, rc, rs, bg)


_NEG = -1e30


LOG2E = 1.0 / math.log(2.0)
LN2 = math.log(2.0)
ATTN_SCALE = 1.0 / math.sqrt(QK_HEAD)
Q_PRESCALE = ATTN_SCALE * LOG2E
HEADS_PER_STEP = 4
PAIR_W = HEADS_PER_STEP * HEAD_SLOT


def _causal_pairs(nq, by_query):
    if by_query:
        pairs = [(i, j) for i in range(nq) for j in range(i + 1)]
    else:
        pairs = [(i, j) for j in range(nq) for i in range(j, nq)]
    return jnp.array([p[0] for p in pairs], jnp.int32), jnp.array([p[1] for p in pairs], jnp.int32)


def _diag_mask_t(s):
    rows = lax.broadcasted_iota(jnp.int32, s.shape, 0)
    cols = lax.broadcasted_iota(jnp.int32, s.shape, 1)
    return jnp.where(rows <= cols, s, _NEG)


def _attn_fwd(q, k, v, tq):
    l = q.shape[0]
    nq = l // tq
    it, jt = _causal_pairs(nq, True)

    def body(it_ref, jt_ref, q_ref, k_ref, v_ref, o_ref, lse_ref, m_ref, l_ref, acc_ref):
        t = pl.program_id(1)
        i = it_ref[t]
        j = jt_ref[t]

        @pl.when(j == 0)
        def _():
            m_ref[...] = jnp.full_like(m_ref, _NEG)
            l_ref[...] = jnp.zeros_like(l_ref)
            acc_ref[...] = jnp.zeros_like(acc_ref)

        def update(on_diagonal):
            for hh in range(HEADS_PER_STEP):
                sl = slice(hh * HEAD_SLOT, (hh + 1) * HEAD_SLOT)
                s = _dot_nt(k_ref[:, sl], q_ref[:, sl])
                if on_diagonal:
                    s = _diag_mask_t(s)
                m_old = m_ref[hh]
                m_new = jnp.maximum(m_old, jnp.max(s, axis=0, keepdims=True))
                p = jnp.exp2(s - m_new)
                alpha = jnp.exp2(m_old - m_new)
                l_ref[hh] = alpha * l_ref[hh] + jnp.sum(p, axis=0, keepdims=True)
                acc_ref[hh] = alpha * acc_ref[hh] + _dot_tn(v_ref[:, sl], p.astype(BF16))
                m_ref[hh] = m_new

        @pl.when(j < i)
        def _():
            update(False)

        @pl.when(j == i)
        def _():
            update(True)
            for hh in range(HEADS_PER_STEP):
                sl = slice(hh * HEAD_SLOT, (hh + 1) * HEAD_SLOT)
                o_ref[:, sl] = (acc_ref[hh] / l_ref[hh]).T.astype(BF16)
                lse_ref[hh] = m_ref[hh] + jnp.log(l_ref[hh]) * LOG2E

    blk = (tq, PAIR_W)
    qmap = lambda h, t, it_ref, jt_ref: (it_ref[t], h)
    kmap = lambda h, t, it_ref, jt_ref: (jt_ref[t], h)
    row = pl.BlockSpec((HEADS_PER_STEP, 1, tq), lambda h, t, it_ref, jt_ref: (h, 0, it_ref[t]))
    return pl.pallas_call(
        body, name="attn_fwd",
        grid_spec=pltpu.PrefetchScalarGridSpec(
            num_scalar_prefetch=2, grid=(N_HEADS // HEADS_PER_STEP, it.shape[0]),
            in_specs=[pl.BlockSpec(blk, qmap), pl.BlockSpec(blk, kmap), pl.BlockSpec(blk, kmap)],
            out_specs=[pl.BlockSpec(blk, qmap), row],
            scratch_shapes=[pltpu.VMEM((HEADS_PER_STEP, 1, tq), F32), pltpu.VMEM((HEADS_PER_STEP, 1, tq), F32),
                            pltpu.VMEM((HEADS_PER_STEP, HEAD_SLOT, tq), F32)]),
        out_shape=[jax.ShapeDtypeStruct((l, HP), BF16), jax.ShapeDtypeStruct((N_HEADS, 1, l), F32)],
        compiler_params=_params(("parallel", "arbitrary")),
    )(it, jt, q, k, v)


def _attn_delta(o, do, tq):
    l = o.shape[0]

    def body(o_ref, do_ref, d_ref):
        prod = o_ref[...].astype(F32) * do_ref[...].astype(F32)
        for hh in range(HEADS_PER_STEP):
            d_ref[hh] = jnp.sum(prod[:, hh * HEAD_SLOT:(hh + 1) * HEAD_SLOT].T, axis=0, keepdims=True)

    blk = pl.BlockSpec((tq, PAIR_W), lambda h, i: (i, h))
    return pl.pallas_call(
        body, name="attn_delta", grid=(N_HEADS // HEADS_PER_STEP, l // tq), in_specs=[blk, blk],
        out_specs=pl.BlockSpec((HEADS_PER_STEP, 1, tq), lambda h, i: (h, 0, i)),
        out_shape=jax.ShapeDtypeStruct((N_HEADS, 1, l), F32),
        compiler_params=_params(("parallel", "parallel")),
    )(o, do)


def _attn_bwd(q, k, v, do, lse, delta, tq):
    l = q.shape[0]
    nq = l // tq
    it, jt = _causal_pairs(nq, False)

    def body(it_ref, jt_ref, q_ref, k_ref, v_ref, do_ref, lse_ref, dl_ref, dq_ref, dk_ref, dv_ref, dka_ref, dva_ref):
        t = pl.program_id(1)
        i = it_ref[t]
        j = jt_ref[t]

        @pl.when(t == 0)
        def _():
            dq_ref[...] = jnp.zeros_like(dq_ref)

        @pl.when(i == j)
        def _():
            dka_ref[...] = jnp.zeros_like(dka_ref)
            dva_ref[...] = jnp.zeros_like(dva_ref)

        def update(on_diagonal):
            r0 = pl.multiple_of(i * tq, tq)
            for hh in range(HEADS_PER_STEP):
                sl = slice(hh * HEAD_SLOT, (hh + 1) * HEAD_SLOT)
                qb = q_ref[:, sl]
                kb = k_ref[:, sl]
                dob = do_ref[:, sl]
                s = _dot_nt(kb, qb)
                if on_diagonal:
                    s = _diag_mask_t(s)
                p = jnp.exp2(s - lse_ref[hh])
                dva_ref[:, sl] += _dot(p.astype(BF16), dob)
                dp = _dot_nt(v_ref[:, sl], dob)
                ds = (p * (dp - dl_ref[hh])).astype(BF16)
                dka_ref[:, sl] += _dot(ds, qb)
                dq_ref[pl.ds(r0, tq), sl] += ATTN_SCALE * _dot_tn(ds, kb)

        @pl.when(j < i)
        def _():
            update(False)

        @pl.when(j == i)
        def _():
            update(True)

        @pl.when(i == nq - 1)
        def _():
            dk_ref[...] = (dka_ref[...] * LN2).astype(BF16)
            dv_ref[...] = dva_ref[...].astype(BF16)

    blk = (tq, PAIR_W)
    qmap = lambda h, t, it_ref, jt_ref: (it_ref[t], h)
    kmap = lambda h, t, it_ref, jt_ref: (jt_ref[t], h)
    row = pl.BlockSpec((HEADS_PER_STEP, 1, tq), lambda h, t, it_ref, jt_ref: (h, 0, it_ref[t]))
    return pl.pallas_call(
        body, name="attn_bwd",
        grid_spec=pltpu.PrefetchScalarGridSpec(
            num_scalar_prefetch=2, grid=(N_HEADS // HEADS_PER_STEP, it.shape[0]),
            in_specs=[pl.BlockSpec(blk, qmap), pl.BlockSpec(blk, kmap), pl.BlockSpec(blk, kmap),
                      pl.BlockSpec(blk, qmap), row, row],
            out_specs=[pl.BlockSpec((l, PAIR_W), lambda h, t, it_ref, jt_ref: (0, h)), pl.BlockSpec(blk, kmap),
                       pl.BlockSpec(blk, kmap)],
            scratch_shapes=[pltpu.VMEM(blk, F32), pltpu.VMEM(blk, F32)]),
        out_shape=[jax.ShapeDtypeStruct((l, HP), F32), jax.ShapeDtypeStruct((l, HP), BF16),
                   jax.ShapeDtypeStruct((l, HP), BF16)],
        compiler_params=_params(("parallel", "arbitrary")),
    )(it, jt, q, k, v, do, lse, delta)


SSM_CB = 512
SSM_UB = 128
SSM_NB = SSM_CH // SSM_CB


def _scan_tiles(re_ref, im_ref, tab, carry, n_tiles, reverse):
    group = 2
    assert n_tiles % group == 0
    pr, pi = tab[6], tab[7]

    def inside(sr, si):
        for step, k in enumerate((1, 2, 4)):
            mr, mi = tab[2 * step], tab[2 * step + 1]
            sh = (SUBLANES - k) if reverse else k
            rr = pltpu.roll(sr, sh, 0)
            ri = pltpu.roll(si, sh, 0)
            sr, si = sr + mr * rr - mi * ri, si + mr * ri + mi * rr
        return sr, si

    def body(n, c):
        cr, ci = c
        first = (n_tiles - group * (n + 1)) if reverse else group * n
        r0 = pl.multiple_of(first * SUBLANES, group * SUBLANES)
        rows = [pl.ds(r0 + g * SUBLANES, SUBLANES) for g in range(group)]
        tiles = [inside(re_ref[r, :], im_ref[r, :]) for r in rows]
        for g in (range(group - 1, -1, -1) if reverse else range(group)):
            sr, si = tiles[g]
            sr, si = sr + pr * cr - pi * ci, si + pr * ci + pi * cr
            re_ref[rows[g], :] = sr
            im_ref[rows[g], :] = si
            edge = slice(0, 1) if reverse else slice(SUBLANES - 1, SUBLANES)
            cr, ci = sr[edge, :], si[edge, :]
        return cr, ci

    return lax.fori_loop(0, n_tiles // group, body, carry)


def _ssm_fwd(u, bre, bim, cre, cim, dvec, tab, tt):
    l = u.shape[0]
    nt = l // tt

    def body(u_ref, bre_ref, bim_ref, cre_ref, cim_ref, d_ref, tab_ref, y_ref, sre_ref, sim_ref, car_ref):
        @pl.when(pl.program_id(1) == 0)
        def _():
            car_ref[...] = jnp.zeros_like(car_ref)

        uf = u_ref[...]
        ub = uf.astype(BF16)
        sre_ref[...] = _dot(ub, bre_ref[0])
        sim_ref[...] = _dot(ub, bim_ref[0])
        tab_v = [tab_ref[n] for n in range(8)]
        cr, ci = _scan_tiles(sre_ref, sim_ref, tab_v, (car_ref[0:1, :], car_ref[8:9, :]), tt // SUBLANES, False)
        car_ref[0:1, :] = cr
        car_ref[8:9, :] = ci
        y_ref[...] = (_dot(sre_ref[...].astype(BF16), cre_ref[0]) - _dot(sim_ref[...].astype(BF16), cim_ref[0])
                      + d_ref[...] * uf)

    return pl.pallas_call(
        body, name="ssm_fwd", grid=(SSM_NB, nt),
        in_specs=[pl.BlockSpec((tt, SSM_UB), lambda m, t: (t, m)),
                  pl.BlockSpec((1, SSM_UB, SSM_CB), lambda m, t: (m, 0, 0)),
                  pl.BlockSpec((1, SSM_UB, SSM_CB), lambda m, t: (m, 0, 0)),
                  pl.BlockSpec((1, SSM_CB, SSM_UB), lambda m, t: (m, 0, 0)),
                  pl.BlockSpec((1, SSM_CB, SSM_UB), lambda m, t: (m, 0, 0)),
                  pl.BlockSpec((1, SSM_UB), lambda m, t: (0, m)),
                  pl.BlockSpec((8, SUBLANES, SSM_CB), lambda m, t: (0, 0, m))],
        out_specs=[pl.BlockSpec((tt, SSM_UB), lambda m, t: (t, m)),
                   pl.BlockSpec((tt, SSM_CB), lambda m, t: (t, m)),
                   pl.BlockSpec((tt, SSM_CB), lambda m, t: (t, m))],
        out_shape=[jax.ShapeDtypeStruct((l, SSM_W), F32), jax.ShapeDtypeStruct((l, SSM_CH), F32),
                   jax.ShapeDtypeStruct((l, SSM_CH), F32)],
        scratch_shapes=[pltpu.VMEM((2 * SUBLANES, SSM_CB), F32)],
        compiler_params=_params(("parallel", "arbitrary")),
    )(u, bre, bim, cre, cim, dvec, tab)


def _ssm_bwd(dy, u, sre, sim, bre, bim, cre, cim, dvec, tab, tt):
    l = u.shape[0]
    nt = l // tt
    tpb = tt // SUBLANES

    def body(dy_ref, u_ref, sre_ref, sim_ref, hre_ref, him_ref, bre_ref, bim_ref, cre_ref, cim_ref, d_ref, tab_ref,
             du_ref, dbre_ref, dbim_ref, dcre_ref, dcim_ref, dare_ref, daim_ref, dd_ref, lr_ref, li_ref, car_ref):
        t = pl.program_id(1)

        @pl.when(t == 0)
        def _():
            car_ref[...] = jnp.zeros_like(car_ref)
            for ref in (dbre_ref, dbim_ref, dcre_ref, dcim_ref, dare_ref, daim_ref, dd_ref):
                ref[...] = jnp.zeros_like(ref)

        dyf = dy_ref[...]
        dyb = dyf.astype(BF16)
        uf = u_ref[...]
        s_re = sre_ref[...]
        s_im = sim_ref[...]
        lr_ref[...] = _dot_nt(dyb, cre_ref[0])
        li_ref[...] = -_dot_nt(dyb, cim_ref[0])
        dcre_ref[0] += _dot_tn(s_re.astype(BF16), dyb)
        dcim_ref[0] -= _dot_tn(s_im.astype(BF16), dyb)
        tab_v = [tab_ref[n] for n in range(8)]
        cr, ci = _scan_tiles(lr_ref, li_ref, tab_v, (car_ref[0:1, :], car_ref[8:9, :]), tpb, True)
        car_ref[0:1, :] = cr
        car_ref[8:9, :] = ci
        lam_r = lr_ref[...]
        lam_i = li_ref[...]
        keep = jnp.where(t == nt - 1, 0.0, 1.0)
        sp_r = _shift_down(s_re, 1, hre_ref[...] * keep)
        sp_i = _shift_down(s_im, 1, him_ref[...] * keep)
        dare_ref[...] += jnp.sum(lam_r * sp_r + lam_i * sp_i, axis=0, keepdims=True)
        daim_ref[...] += jnp.sum(lam_i * sp_r - lam_r * sp_i, axis=0, keepdims=True)
        lrb = lam_r.astype(BF16)
        lib = lam_i.astype(BF16)
        du_ref[...] = _dot_nt(lrb, bre_ref[0]) + _dot_nt(lib, bim_ref[0]) + dyf * d_ref[...]
        ub = uf.astype(BF16)
        dbre_ref[0] += _dot_tn(ub, lrb)
        dbim_ref[0] += _dot_tn(ub, lib)
        dd_ref[...] += jnp.sum(dyf * uf, axis=0, keepdims=True)

    rev = lambda m, t: (nt - 1 - t, m)
    halo = lambda m, t: (jnp.maximum((nt - 1 - t) * tpb - 1, 0), m)
    wb = pl.BlockSpec((1, SSM_UB, SSM_CB), lambda m, t: (m, 0, 0))
    wc = pl.BlockSpec((1, SSM_CB, SSM_UB), lambda m, t: (m, 0, 0))
    vec_c = pl.BlockSpec((1, SSM_CB), lambda m, t: (0, m))
    vec_u = pl.BlockSpec((1, SSM_UB), lambda m, t: (0, m))
    return pl.pallas_call(
        body, name="ssm_bwd", grid=(SSM_NB, nt),
        in_specs=[pl.BlockSpec((tt, SSM_UB), rev), pl.BlockSpec((tt, SSM_UB), rev),
                  pl.BlockSpec((tt, SSM_CB), rev), pl.BlockSpec((tt, SSM_CB), rev),
                  pl.BlockSpec((SUBLANES, SSM_CB), halo), pl.BlockSpec((SUBLANES, SSM_CB), halo),
                  wb, wb, wc, wc, vec_u,
                  pl.BlockSpec((8, SUBLANES, SSM_CB), lambda m, t: (0, 0, m))],
        out_specs=[pl.BlockSpec((tt, SSM_UB), rev), wb, wb, wc, wc, vec_c, vec_c, vec_u],
        out_shape=[jax.ShapeDtypeStruct((l, SSM_W), F32),
                   jax.ShapeDtypeStruct((SSM_NB, SSM_UB, SSM_CB), F32), jax.ShapeDtypeStruct((SSM_NB, SSM_UB, SSM_CB), F32),
                   jax.ShapeDtypeStruct((SSM_NB, SSM_CB, SSM_UB), F32), jax.ShapeDtypeStruct((SSM_NB, SSM_CB, SSM_UB), F32),
                   jax.ShapeDtypeStruct((1, SSM_CH), F32), jax.ShapeDtypeStruct((1, SSM_CH), F32),
                   jax.ShapeDtypeStruct((1, SSM_W), F32)],
        scratch_shapes=[pltpu.VMEM((tt, SSM_CB), F32), pltpu.VMEM((tt, SSM_CB), F32),
                        pltpu.VMEM((2 * SUBLANES, SSM_CB), F32)],
        compiler_params=_params(("parallel", "arbitrary")),
    )(dy, u, sre, sim, sre, sim, bre, bim, cre, cim, dvec, tab)


def _merge_fwd(x, gl, attn, y1, wba, wbs, wglu, bglu, wout, gpost, gpre, tl):
    l = x.shape[0]

    def body(x_ref, gl_ref, at_ref, y1_ref, wba_ref, wbs_ref, wglu_ref, bglu_ref, wout_ref, gpost_ref, gpre_ref,
             a_ref, sm_ref, mg_ref, z_ref, x1_ref, hn2_ref, y3_ref):
        y2 = _gelu(y1_ref[...])
        sg = _sigmoid(_dot(y2.astype(BF16), wglu_ref[...]) + bglu_ref[...])
        y3 = (y2 * sg).astype(BF16)
        y3_ref[...] = y3
        a = _dot(at_ref[...], wba_ref[...])
        sm = _dot(y3, wbs_ref[...])
        a_ref[...] = a.astype(BF16)
        sm_ref[...] = sm.astype(BF16)
        g = _sigmoid(gl_ref[...])
        merged = (g[:, :D_MODEL] * a + g[:, D_MODEL:] * sm).astype(BF16)
        mg_ref[...] = merged
        z = _dot(merged, wout_ref[...])
        z_ref[...] = z
        n, _ = _rms(z, gpost_ref[...])
        x1 = x_ref[...] + n
        x1_ref[...] = x1
        hn2, _ = _rms(x1, gpre_ref[...])
        hn2_ref[...] = hn2.astype(BF16)

    outs = [(D_MODEL, BF16), (D_MODEL, BF16), (D_MODEL, BF16), (D_MODEL, F32), (D_MODEL, F32), (D_MODEL, BF16),
            (SSM_W, BF16)]
    return pl.pallas_call(
        body, name="merge_fwd", grid=(l // tl,),
        in_specs=[_row(tl, D_MODEL), _row(tl, 2 * D_MODEL), _row(tl, HP), _row(tl, SSM_W),
                  _const((HP, D_MODEL)), _const((SSM_W, D_MODEL)), _const((SSM_W, SSM_W)), _const((1, SSM_W)),
                  _const((D_MODEL, D_MODEL)), _const((1, D_MODEL)), _const((1, D_MODEL))],
        out_specs=[_row(tl, n) for n, _ in outs],
        out_shape=[jax.ShapeDtypeStruct((l, n), dt) for n, dt in outs],
        compiler_params=_params(("parallel",)),
    )(x, gl, attn, y1, wba, wbs, wglu, bglu, wout, gpost, gpre)


def _merge_bwd(dhn2, x1, dx2, z, gl, a, sm, y1, wba, wbs, wglu, bglu, wout, gpost, gpre, tl):
    l = x1.shape[0]

    def body(dhn2_ref, x1_ref, dx2_ref, z_ref, gl_ref, a_ref, sm_ref, y1_ref,
             wba_ref, wbs_ref, wglu_ref, bglu_ref, wout_ref, gpost_ref, gpre_ref,
             dx1_ref, dz_ref, dbra_ref, dbrs_ref, dgl_ref, dat_ref, dy1_ref, dt_ref, y2_ref,
             dgpre_ref, dgpost_ref, dbg_ref, dbglu_ref):
        @pl.when(pl.program_id(0) == 0)
        def _():
            for ref in (dgpre_ref, dgpost_ref, dbg_ref, dbglu_ref):
                ref[...] = jnp.zeros_like(ref)

        dx1a, dgpre = _rms_bwd(dhn2_ref[...], x1_ref[...], gpre_ref[...])
        dgpre_ref[...] += dgpre
        dx1 = dx2_ref[...] + dx1a
        dx1_ref[...] = dx1
        dz, dgpost = _rms_bwd(dx1, z_ref[...], gpost_ref[...])
        dgpost_ref[...] += dgpost
        dzb = dz.astype(BF16)
        dz_ref[...] = dzb
        dm = _dot_nt(dzb, wout_ref[...])
        g = _sigmoid(gl_ref[...])
        g0 = g[:, :D_MODEL]
        g1 = g[:, D_MODEL:]
        dbra = (dm * g0).astype(BF16)
        dbrs = (dm * g1).astype(BF16)
        dbra_ref[...] = dbra
        dbrs_ref[...] = dbrs
        dgl0 = dm * a_ref[...].astype(F32) * g0 * (1.0 - g0)
        dgl1 = dm * sm_ref[...].astype(F32) * g1 * (1.0 - g1)
        dgl_ref[:, :D_MODEL] = dgl0.astype(BF16)
        dgl_ref[:, D_MODEL:] = dgl1.astype(BF16)
        dbg_ref[:, :D_MODEL] += jnp.sum(dgl0, axis=0, keepdims=True)
        dbg_ref[:, D_MODEL:] += jnp.sum(dgl1, axis=0, keepdims=True)
        dat_ref[...] = _dot_nt(dbra, wba_ref[...]).astype(BF16)
        dy3 = _dot_nt(dbrs, wbs_ref[...])
        y1v = y1_ref[...]
        y2 = _gelu(y1v)
        y2b = y2.astype(BF16)
        y2_ref[...] = y2b
        sg = _sigmoid(_dot(y2b, wglu_ref[...]) + bglu_ref[...])
        dt = dy3 * y2 * sg * (1.0 - sg)
        dtb = dt.astype(BF16)
        dt_ref[...] = dtb
        dbglu_ref[...] += jnp.sum(dt, axis=0, keepdims=True)
        dy2 = dy3 * sg + _dot_nt(dtb, wglu_ref[...])
        dy1_ref[...] = dy2 * _gelu_grad(y1v)

    outs = [(D_MODEL, F32), (D_MODEL, BF16), (D_MODEL, BF16), (D_MODEL, BF16), (2 * D_MODEL, BF16), (HP, BF16),
            (SSM_W, F32), (SSM_W, BF16), (SSM_W, BF16)]
    accs = [D_MODEL, D_MODEL, 2 * D_MODEL, SSM_W]
    return pl.pallas_call(
        body, name="merge_bwd", grid=(l // tl,),
        in_specs=[_row(tl, D_MODEL), _row(tl, D_MODEL), _row(tl, D_MODEL), _row(tl, D_MODEL),
                  _row(tl, 2 * D_MODEL), _row(tl, D_MODEL), _row(tl, D_MODEL), _row(tl, SSM_W),
                  _const((HP, D_MODEL)), _const((SSM_W, D_MODEL)), _const((SSM_W, SSM_W)), _const((1, SSM_W)),
                  _const((D_MODEL, D_MODEL)), _const((1, D_MODEL)), _const((1, D_MODEL))],
        out_specs=[_row(tl, n) for n, _ in outs] + [_const((1, n)) for n in accs],
        out_shape=[jax.ShapeDtypeStruct((l, n), dt) for n, dt in outs]
        + [jax.ShapeDtypeStruct((1, n), F32) for n in accs],
        compiler_params=_params(("arbitrary",)),
    )(dhn2, x1, dx2, z, gl, a, sm, y1, wba, wbs, wglu, bglu, wout, gpost, gpre)


def _proj_bwd(x, dx1, cq, ckv, dq, dk, dv, du, dgl, g1, win, gq, wuq, gkv, wukv, rc, rs, tl):
    l = x.shape[0]

    def body(x_ref, dx1_ref, cq_ref, ckv_ref, dq_ref, dk_ref, dv_ref, du_ref, dgl_ref,
             g1_ref, win_ref, gq_ref, wuq_ref, gkv_ref, wukv_ref, rc_ref, rs_ref,
             gx_ref, dql_ref, qn_ref, ckvn_ref, dproj_ref, dg1_ref, dgq_ref, dgkv_ref):
        @pl.when(pl.program_id(0) == 0)
        def _():
            for ref in (dg1_ref, dgq_ref, dgkv_ref):
                ref[...] = jnp.zeros_like(ref)

        c1 = rc_ref[...]
        s1 = rs_ref[...]
        dql = _rope_bwd(dq_ref[...], jnp.tile(c1, (1, N_HEADS)), jnp.tile(s1, (1, N_HEADS))).astype(BF16)
        dql_ref[...] = dql
        dqn = _dot_nt(dql, wuq_ref[...])
        cq = cq_ref[...]
        qn, _ = _rms(cq, gq_ref[...])
        qn_ref[...] = qn.astype(BF16)
        dcq, dgq = _rms_bwd(dqn, cq, gq_ref[...])
        dgq_ref[...] += dgq
        dkb = dk_ref[...]
        dvb = dv_ref[...]
        dkf = dkb.astype(F32)
        dkr = dkf[:, 0:HEAD_SLOT]
        for h in range(1, N_HEADS):
            dkr = dkr + dkf[:, h * HEAD_SLOT:(h + 1) * HEAD_SLOT]
        dkr = _rope_bwd(dkr, c1, s1)
        dckvn = _dot_nt(dkb, wukv_ref[:, :HP]) + _dot_nt(dvb, wukv_ref[:, HP:])
        ckv = ckv_ref[...]
        ckvn, _ = _rms(ckv, gkv_ref[...])
        ckvn_ref[...] = ckvn.astype(BF16)
        dckv, dgkv = _rms_bwd(dckvn, ckv, gkv_ref[...])
        dgkv_ref[...] += dgkv
        dproj_ref[:, P_CQ:P_CKV] = dcq.astype(BF16)
        dproj_ref[:, P_CKV:P_KR] = dckv.astype(BF16)
        dproj_ref[:, P_KR:P_U] = dkr.astype(BF16)
        dproj_ref[:, P_U:P_GL] = du_ref[...].astype(BF16)
        dproj_ref[:, P_GL:P_END] = dgl_ref[...]
        dhn = _dot_nt(dproj_ref[...], win_ref[...])
        dxa, dg1 = _rms_bwd(dhn, x_ref[...], g1_ref[...])
        dg1_ref[...] += dg1
        gx_ref[...] = dx1_ref[...] + dxa

    outs = [(D_MODEL, F32), (HP, BF16), (Q_RANK, BF16), (KV_RANK, BF16), (P_END, BF16)]
    accs = [D_MODEL, Q_RANK, KV_RANK]
    return pl.pallas_call(
        body, name="proj_bwd", grid=(l // tl,),
        in_specs=[_row(tl, D_MODEL), _row(tl, D_MODEL), _row(tl, Q_RANK), _row(tl, KV_RANK), _row(tl, HP),
                  _row(tl, HP), _row(tl, HP), _row(tl, SSM_W), _row(tl, 2 * D_MODEL),
                  _const((1, D_MODEL)), _const((D_MODEL, P_END)), _const((1, Q_RANK)), _const((Q_RANK, HP)),
                  _const((1, KV_RANK)), _const((KV_RANK, 2 * HP)), _row(tl, HEAD_SLOT), _row(tl, HEAD_SLOT)],
        out_specs=[_row(tl, n) for n, _ in outs] + [_const((1, n)) for n in accs],
        out_shape=[jax.ShapeDtypeStruct((l, n), dt) for n, dt in outs]
        + [jax.ShapeDtypeStruct((1, n), F32) for n in accs],
        compiler_params=_params(("arbitrary",)),
    )(x, dx1, cq, ckv, dq, dk, dv, du, dgl, g1, win, gq, wuq, gkv, wukv, rc, rs)


CONV_CB = 256
CONV_NB = D_FF // CONV_CB
CONV_ROWS = 16


def _conv3(h, halo, w, b):
    return b + w[0:1, :] * _shift_down(h, 2, halo) + w[1:2, :] * _shift_down(h, 1, halo) + w[2:3, :] * h


def _conv_fwd(h, cw, cb, tl):
    l = h.shape[0]

    def body(hg_ref, hv_ref, wg_ref, wv_ref, bg_ref, bv_ref, act_ref, halo_ref):
        @pl.when(pl.program_id(1) == 0)
        def _():
            halo_ref[...] = jnp.zeros_like(halo_ref)

        hg = hg_ref[...]
        hv = hv_ref[...]
        cg = _conv3(hg, halo_ref[0:SUBLANES, :], wg_ref[...], bg_ref[...])
        cv = _conv3(hv, halo_ref[SUBLANES:, :], wv_ref[...], bv_ref[...])
        act_ref[...] = (_gelu(cg) * cv).astype(BF16)
        halo_ref[0:SUBLANES, :] = hg[tl - SUBLANES:, :]
        halo_ref[SUBLANES:, :] = hv[tl - SUBLANES:, :]

    gmap = lambda c, r: (r, c)
    vmap = lambda c, r: (r, CONV_NB + c)
    return pl.pallas_call(
        body, name="conv_fwd", grid=(CONV_NB, l // tl),
        in_specs=[pl.BlockSpec((tl, CONV_CB), gmap), pl.BlockSpec((tl, CONV_CB), vmap),
                  pl.BlockSpec((3, CONV_CB), lambda c, r: (0, c)), pl.BlockSpec((3, CONV_CB), lambda c, r: (0, CONV_NB + c)),
                  pl.BlockSpec((1, CONV_CB), lambda c, r: (0, c)), pl.BlockSpec((1, CONV_CB), lambda c, r: (0, CONV_NB + c))],
        out_specs=pl.BlockSpec((tl, CONV_CB), gmap),
        out_shape=jax.ShapeDtypeStruct((l, D_FF), BF16),
        scratch_shapes=[pltpu.VMEM((2 * SUBLANES, CONV_CB), F32)],
        compiler_params=_params(("parallel", "arbitrary")),
    )(h, h, cw, cw, cb, cb)


def _conv_bwd(h, dact, cw, cb, tl):
    l = h.shape[0]
    nr = l // tl
    tpb = tl // SUBLANES

    def body(hg_ref, hv_ref, hgh_ref, hvh_ref, da_ref, wg_ref, wv_ref, bg_ref, bv_ref,
             dh_ref, dwg_ref, dwv_ref, dbg_ref, dbv_ref, car_ref):
        r = pl.program_id(1)

        @pl.when(r == 0)
        def _():
            for ref in (car_ref, dwg_ref, dwv_ref, dbg_ref, dbv_ref):
                ref[...] = jnp.zeros_like(ref)

        keep = jnp.where(r == nr - 1, 0.0, 1.0)
        wg, wv, bg, bv = wg_ref[...], wv_ref[...], bg_ref[...], bv_ref[...]
        nch = tl // CONV_ROWS

        def fold(x):
            s = x[0:SUBLANES, :]
            for k in range(1, CONV_ROWS // SUBLANES):
                s = s + x[k * SUBLANES:(k + 1) * SUBLANES, :]
            return s

        def chunk(n, carry):
            ncg, ncv, acc = carry
            idx = nch - 1 - n
            r0 = pl.multiple_of(idx * CONV_ROWS, CONV_ROWS)
            rows = pl.ds(r0, CONV_ROWS)
            before = pl.ds(pl.multiple_of(jnp.maximum(r0 - SUBLANES, 0), SUBLANES), SUBLANES)
            in_tile = idx > 0
            da = da_ref[rows, :].astype(F32)

            def half(h_ref, halo_ref, w, b):
                hh = h_ref[rows, :]
                prev = jnp.where(in_tile, h_ref[before, :], halo_ref[...] * keep)
                h1 = _shift_down(hh, 1, prev)
                h2 = _shift_down(hh, 2, prev)
                return hh, h1, h2, b + w[0:1, :] * h2 + w[1:2, :] * h1 + w[2:3, :] * hh

            hg, hg1, hg2, cg = half(hg_ref, hgh_ref, wg, bg)
            hv, hv1, hv2, cv = half(hv_ref, hvh_ref, wv, bv)
            dcg = da * cv * _gelu_grad(cg)
            dcv = da * _gelu(cg)

            def back(dc, hh, h1, h2, w, nxt, part):
                dh = w[2:3, :] * dc + w[1:2, :] * _shift_up(dc, 1, nxt) + w[0:1, :] * _shift_up(dc, 2, nxt)
                dh_ref[part, rows, :] = dh.astype(BF16)
                return [fold(dc * h2), fold(dc * h1), fold(dc * hh), fold(dc)]

            sums = back(dcg, hg, hg1, hg2, wg, ncg, 0) + back(dcv, hv, hv1, hv2, wv, ncv, 1)
            return dcg[0:SUBLANES, :], dcv[0:SUBLANES, :], [a + s for a, s in zip(acc, sums)]

        zero = jnp.zeros((SUBLANES, CONV_CB), F32)
        ncg, ncv, acc = lax.fori_loop(0, nch, chunk, (car_ref[0:SUBLANES, :], car_ref[SUBLANES:, :], [zero] * 8))
        car_ref[0:SUBLANES, :] = ncg
        car_ref[SUBLANES:, :] = ncv
        for half_acc, dw_ref, db_ref in ((acc[0:4], dwg_ref, dbg_ref), (acc[4:8], dwv_ref, dbv_ref)):
            for k in range(3):
                dw_ref[k:k + 1, :] += jnp.sum(half_acc[k], axis=0, keepdims=True)
            db_ref[...] += jnp.sum(half_acc[3], axis=0, keepdims=True)

    grev = lambda c, r: (nr - 1 - r, c)
    vrev = lambda c, r: (nr - 1 - r, CONV_NB + c)
    ghalo = lambda c, r: (jnp.maximum((nr - 1 - r) * tpb - 1, 0), c)
    vhalo = lambda c, r: (jnp.maximum((nr - 1 - r) * tpb - 1, 0), CONV_NB + c)
    colg = lambda c, r: (0, c)
    colv = lambda c, r: (0, CONV_NB + c)
    return pl.pallas_call(
        body, name="conv_bwd", grid=(CONV_NB, nr),
        in_specs=[pl.BlockSpec((tl, CONV_CB), grev), pl.BlockSpec((tl, CONV_CB), vrev),
                  pl.BlockSpec((SUBLANES, CONV_CB), ghalo), pl.BlockSpec((SUBLANES, CONV_CB), vhalo),
                  pl.BlockSpec((tl, CONV_CB), grev),
                  pl.BlockSpec((3, CONV_CB), colg), pl.BlockSpec((3, CONV_CB), colv),
                  pl.BlockSpec((1, CONV_CB), colg), pl.BlockSpec((1, CONV_CB), colv)],
        out_specs=[pl.BlockSpec((2, tl, CONV_CB), lambda c, r: (0, nr - 1 - r, c)),
                   pl.BlockSpec((3, CONV_CB), colg), pl.BlockSpec((3, CONV_CB), colg),
                   pl.BlockSpec((1, CONV_CB), colg), pl.BlockSpec((1, CONV_CB), colg)],
        out_shape=[jax.ShapeDtypeStruct((2, l, D_FF), BF16),
                   jax.ShapeDtypeStruct((3, D_FF), F32), jax.ShapeDtypeStruct((3, D_FF), F32),
                   jax.ShapeDtypeStruct((1, D_FF), F32), jax.ShapeDtypeStruct((1, D_FF), F32)],
        scratch_shapes=[pltpu.VMEM((2 * SUBLANES, CONV_CB), F32)],
        compiler_params=_params(("parallel", "arbitrary")),
    )(h, h, h, h, dact, cw, cw, cb, cb)


def _loss_head(ff, x1, tgt, g, tl):
    l = ff.shape[0]

    def body(ff_ref, x1_ref, tg_ref, g_ref, loss_ref, dx2_ref, dff_ref, dg_ref):
        @pl.when(pl.program_id(0) == 0)
        def _():
            loss_ref[...] = jnp.zeros_like(loss_ref)
            dg_ref[...] = jnp.zeros_like(dg_ref)

        f = ff_ref[...]
        gv = g_ref[...]
        n, _ = _rms(f, gv)
        e = x1_ref[...] + n - tg_ref[...]
        loss_ref[...] += 0.5 * jnp.sum(jnp.mean(e * e, axis=-1, keepdims=True), axis=0, keepdims=True)
        dx2 = e * (1.0 / D_MODEL)
        dx2_ref[...] = dx2
        dff, dg = _rms_bwd(dx2, f, gv)
        dff_ref[...] = dff.astype(BF16)
        dg_ref[...] += dg

    return pl.pallas_call(
        body, name="loss_head", grid=(l // tl,),
        in_specs=[_row(tl, D_MODEL), _row(tl, D_MODEL), _row(tl, D_MODEL), _const((1, D_MODEL))],
        out_specs=[_const((1, LANES)), _row(tl, D_MODEL), _row(tl, D_MODEL), _const((1, D_MODEL))],
        out_shape=[jax.ShapeDtypeStruct((1, LANES), F32), jax.ShapeDtypeStruct((l, D_MODEL), F32),
                   jax.ShapeDtypeStruct((l, D_MODEL), BF16), jax.ShapeDtypeStruct((1, D_MODEL), F32)],
        compiler_params=_params(("arbitrary",)),
    )(ff, x1, tgt, g)


def _ssm_disc(lam_re, lam_im, log_dt, b_re, b_im):
    dt = jnp.exp(log_dt)[:, None]
    mag = jnp.exp(lam_re * dt)
    ang = lam_im * dt
    a_re, a_im = mag * jnp.cos(ang), mag * jnp.sin(ang)
    den = lam_re * lam_re + lam_im * lam_im
    n_re, n_im = a_re - 1.0, a_im
    z_re = (n_re * lam_re + n_im * lam_im) / den
    z_im = (n_im * lam_re - n_re * lam_im) / den
    bb_re = z_re[..., None] * b_re - z_im[..., None] * b_im
    bb_im = z_re[..., None] * b_im + z_im[..., None] * b_re
    return a_re, a_im, bb_re, bb_im


_GPB = SSM_CB // SSM_P


def _embed_b(bb):
    t = bb.transpose(0, 2, 1).reshape(SSM_NB, _GPB, SSM_H, SSM_P)
    return jnp.einsum('mjhp,jk->mjhkp', t, jnp.eye(_GPB, dtype=bb.dtype)).reshape(SSM_NB, SSM_UB, SSM_CB)


def _extract_b(d):
    t = d.reshape(SSM_NB, _GPB, SSM_H, _GPB, SSM_P)
    t = jnp.einsum('mjhkp,jk->mjhp', t, jnp.eye(_GPB, dtype=d.dtype))
    return t.reshape(SSM_G, SSM_H, SSM_P).transpose(0, 2, 1)


def _embed_c(c):
    t = c.transpose(0, 2, 1).reshape(SSM_NB, _GPB, SSM_P, SSM_H)
    return jnp.einsum('mjph,jk->mjpkh', t, jnp.eye(_GPB, dtype=c.dtype)).reshape(SSM_NB, SSM_CB, SSM_UB)


def _extract_c(d):
    t = d.reshape(SSM_NB, _GPB, SSM_P, _GPB, SSM_H)
    t = jnp.einsum('mjpkh,jk->mjph', t, jnp.eye(_GPB, dtype=d.dtype))
    return t.reshape(SSM_G, SSM_P, SSM_H).transpose(0, 2, 1)


def _scan_tables(a_re, a_im, reverse):
    ar = a_re.reshape(1, SSM_CH)
    ai = (-a_im if reverse else a_im).reshape(1, SSM_CH)
    pr, pi = [ar], [ai]
    for _ in range(SUBLANES - 1):
        pr, pi = pr + [pr[-1] * ar - pi[-1] * ai], pi + [pr[-1] * ai + pi[-1] * ar]
    rows = jnp.arange(SUBLANES)[:, None]
    out = []
    for k in (1, 2, 4):
        valid = (rows + k <= SUBLANES - 1) if reverse else (rows >= k)
        out += [jnp.where(valid, pr[k - 1], 0.0), jnp.where(valid, pi[k - 1], 0.0)]
    order = list(range(SUBLANES - 1, -1, -1)) if reverse else list(range(SUBLANES))
    out += [jnp.concatenate([pr[n] for n in order], axis=0), jnp.concatenate([pi[n] for n in order], axis=0)]
    return jnp.stack(out).astype(F32)


def _pad_heads(w, d):
    lead = w.shape[:-1]
    w = w.reshape(lead + (N_HEADS, d))
    w = jnp.pad(w, [(0, 0)] * len(lead) + [(0, 0), (0, HEAD_SLOT - d)])
    return w.reshape(lead + (HP,))


def _unpad_heads(w, d):
    lead = w.shape[:-1]
    return w.reshape(lead + (N_HEADS, HEAD_SLOT))[..., :d].reshape(lead + (N_HEADS * d,))


def _chip_major(w, axis):
    k, n = w.shape
    if axis == 0:
        return w.reshape(N_CHIPS, k // N_CHIPS, n)
    return w.reshape(k, N_CHIPS, n // N_CHIPS).transpose(1, 0, 2)


def _from_chip_major(w, axis):
    if axis == 0:
        return w.reshape(-1, w.shape[2])
    return w.transpose(1, 0, 2).reshape(w.shape[1], -1)


def _pad_w_in(w):
    z = lambda n: jnp.zeros((w.shape[0], n), w.dtype)
    return jnp.concatenate([w[:, :640], z(KR_LANE), w[:, 640:672], z(HEAD_SLOT - KR_LANE - QK_ROPE), w[:, 672:]], axis=1)


def _unpad_w_in(w):
    return jnp.concatenate([w[:, :640], w[:, P_KR + KR_LANE:P_KR + KR_LANE + QK_ROPE], w[:, P_U:]], axis=1)


def _local_step(x, positions, tgt, wts, sp):
    l = x.shape[0]
    tl = min(256, l)
    ta = min(512, l)
    ts = min(1024, l)

    inv_freq = ROPE_THETA ** (-jnp.arange(0, QK_ROPE, 2, dtype=F32) / QK_ROPE)
    ang = positions.astype(F32)[:, None] * inv_freq
    cos, sin = jnp.cos(ang), jnp.sin(ang)
    one = jnp.ones((l, KR_LANE), F32)
    rc = jnp.concatenate([one, cos, cos, jnp.ones((l, HEAD_SLOT - KR_LANE - QK_ROPE), F32)], axis=1)
    rs = jnp.concatenate([0 * one, -sin, sin, jnp.zeros((l, HEAD_SLOT - KR_LANE - QK_ROPE), F32)], axis=1)

    win = _pad_w_in(wts["w_in"])
    wuq = _pad_heads(wts["w_uq"], QK_HEAD)
    wukv = jnp.concatenate([_pad_heads(wts["w_uk"], QK_NOPE), _pad_heads(wts["w_uv"], V_HEAD)], axis=1)

    disc_in = (sp["ssm_lambda_re"], sp["ssm_lambda_im"], sp["ssm_log_dt"], sp["ssm_b_re"], sp["ssm_b_im"])
    (a_re, a_im, bb_re, bb_im), disc_vjp = jax.vjp(_ssm_disc, *disc_in)
    bre, bim = _embed_b(bb_re).astype(BF16), _embed_b(bb_im).astype(BF16)
    cre, cim = _embed_c(sp["ssm_c_re"]).astype(BF16), _embed_c(sp["ssm_c_im"]).astype(BF16)
    dvec = sp["ssm_d"].reshape(1, SSM_W)
    tab_f = _scan_tables(a_re, a_im, False)
    tab_r = _scan_tables(a_re, a_im, True)

    g1, gq, gkv = sp["mix_norm_pre"], sp["q_norm"], sp["kv_norm"]
    gpost, gpre, gfin = sp["mix_norm_post"], sp["ffn_norm_pre"], sp["ffn_norm_post"]
    bgate, bglu, convb = sp["b_gate"], sp["b_glu"], sp["conv_b"]

    hn, cq, ckv, q, k, v, u, gl = _proj_fwd(x, g1, win, gq, wuq, gkv, wukv, rc, rs, bgate, tl)
    attn, lse = _attn_fwd(q, k, v, ta)
    y1, sre, sim = _ssm_fwd(u, bre, bim, cre, cim, dvec, tab_f, ts)
    wba = jnp.pad(wts["w_branch_attn"].reshape(N_HEADS, V_HEAD, D_MODEL),
                  ((0, 0), (0, HEAD_SLOT - V_HEAD), (0, 0))).reshape(HP, D_MODEL)
    wbs, wglu, wout = wts["w_branch_ssm"], wts["w_glu"], wts["w_out"]
    a, sm, merged, z, x1, hn2, y3 = _merge_fwd(x, gl, attn, y1, wba, wbs, wglu, bglu, wout, gpost, gpre, tl)
    late = wts["late"](x1)
    wup, wdown, convw = late["w_up"], late["w_down"], late["conv_w"]
    h = _mm(hn2, wup, "ffn_up")
    act = _conv_fwd(h, convw, convb, ts)
    ff = _mm(act, wdown, "ffn_down")
    loss, dx2, dff, dgfin = _loss_head(ff, x1, tgt, gfin, tl)

    dact = _mm(dff, wdown, "ffn_down_dx", out_dtype=BF16, bt=True)
    d_wdown = _mm_tn(act, dff, "ffn_down_dw", tk_cap=D_FF // 2)
    dh, dwg, dwv, dbg, dbv = _conv_bwd(h, dact, convw, convb, ts)
    d_convw = jnp.concatenate([dwg, dwv], axis=1)
    d_convb = jnp.concatenate([dbg, dbv], axis=1)
    dhn2 = _mm(dh, wup, "ffn_up_dx", bt=True)
    d_wup = _mm_tn(hn2, dh, "ffn_up_dw", chips=True)
    behind = wts["send_grads"]("ffn", {"w_up": d_wup, "w_down": _chip_major(d_wdown, 0)})
    (dx1, dz, dbra, dbrs, dgl, dattn, dy1, dt, y2, dgpre, dgpost, dbgate, dbglu) = _merge_bwd(
        dhn2, x1, dx2, z, gl, a, sm, y1, wba, wbs, wglu, bglu, wout, gpost, gpre + behind, tl)
    d_wout = _mm_tn(merged, dz, "w_out_dw")
    d_wba = _mm_tn(attn, dbra, "w_branch_attn_dw", chips=True)
    d_wbs = _mm_tn(y3, dbrs, "w_branch_ssm_dw", chips=True)
    d_wglu = _mm_tn(y2, dt, "w_glu_dw")
    ncol = D_MODEL // N_CHIPS
    behind = wts["send_grads"]("mix", {
        "w_glu": _chip_major(d_wglu, 0),
        "w_branch_attn": d_wba.reshape(N_CHIPS, N_HEADS, HEAD_SLOT, ncol)[:, :, :V_HEAD].reshape(
            N_CHIPS, N_HEADS * V_HEAD, ncol),
        "w_branch_ssm": d_wbs,
        "w_out": _chip_major(d_wout, 0)}, after=d_wglu)
    dq, dk, dv = _attn_bwd(q, k, v, dattn, lse + behind, _attn_delta(attn, dattn, min(2048, l)), ta)
    du, dbre, dbim, dcre, dcim, dare, daim, dd = _ssm_bwd(dy1, u, sre, sim, bre, bim, cre, cim, dvec, tab_r, ts)
    behind = wts["send_grads"]("none", {}, after=du)
    gx, dql, qn, ckvn, dproj, dg1, dgq, dgkv = _proj_bwd(
        x, dx1, cq, ckv, dq, dk, dv, du, dgl, g1 + behind, win, gq, wuq, gkv, wukv, rc, rs, tl)
    d_win = _mm_tn(hn, dproj, "w_in_dw")
    d_wuq = _mm_tn(qn, dql, "w_uq_dw")
    d_wuk = _mm_tn(ckvn, dk, "w_uk_dw")
    d_wuv = _mm_tn(ckvn, dv, "w_uv_dw")

    d_lre, d_lim, d_ldt, d_bre, d_bim = disc_vjp((dare.reshape(SSM_G, SSM_P), daim.reshape(SSM_G, SSM_P),
                                                  _extract_b(dbre), _extract_b(dbim)))
    big = {
        "w_in": _chip_major(_unpad_w_in(d_win), 1),
        "w_uq": _chip_major(_unpad_heads(d_wuq, QK_HEAD), 1),
        "w_uk": _chip_major(_unpad_heads(d_wuk, QK_NOPE), 1),
        "w_uv": _chip_major(_unpad_heads(d_wuv, V_HEAD), 1),
    }
    small = {
        "conv_w": d_convw,
        "mix_norm_pre": dg1, "q_norm": dgq, "kv_norm": dgkv,
        "ssm_lambda_re": d_lre, "ssm_lambda_im": d_lim, "ssm_log_dt": d_ldt,
        "ssm_b_re": d_bre, "ssm_b_im": d_bim,
        "ssm_c_re": _extract_c(dcre), "ssm_c_im": _extract_c(dcim),
        "ssm_d": dd.reshape(SSM_G, SSM_H), "b_glu": dbglu, "b_gate": dbgate,
        "mix_norm_post": dgpost, "ffn_norm_pre": dgpre, "conv_b": d_convb, "ffn_norm_post": dgfin,
    }
    return loss[0, 0], gx, big, small


_ANY = pl.BlockSpec(memory_space=pl.ANY)


ROW_TILE = 16


def _place():
    x, y, c = lax.axis_index("x"), lax.axis_index("y"), lax.axis_index("c")
    return x, y, c, 2 * x + y, [(1 - x, y), (x, 1 - y), (1 - x, 1 - y)]


def _half(rows, which):
    hr = rows // 2
    return pl.ds(pl.multiple_of(which * hr, ROW_TILE), hr)


def _remote(src, dst, send_sems, recv_sems, n, dev):
    return pltpu.make_async_remote_copy(src_ref=src, dst_ref=dst, send_sem=send_sems.at[n], recv_sem=recv_sems.at[n],
                                        device_id=dev, device_id_type=MESH)


def _gather_big(shards):
    nw = len(shards)
    rows = [s.shape[0] for s in shards]

    def body(*refs):
        ins, outs = refs[:nw], refs[nw:2 * nw]
        ici_send, ici_recv, d2d_send, d2d_recv = refs[2 * nw:]
        x, y, c, me, peers = _place()
        sent = []
        for i in range(nw):
            for p, (px, py) in enumerate(peers):
                cp = _remote(ins[i].at[_half(rows[i], c)], outs[i].at[me, _half(rows[i], c)], ici_send, ici_recv,
                             3 * i + p, (px, py, c))
                cp.start()
                sent.append(cp)
        for p, (px, py) in enumerate(peers):
            for i in range(nw):
                blk = outs[i].at[2 * px + py, _half(rows[i], c)]
                _remote(blk, blk, ici_send, ici_recv, 3 * i + p, (px, py, c)).wait_recv()
                cp = _remote(blk, blk, d2d_send, d2d_recv, 3 * i + p, (x, y, 1 - c))
                cp.start()
                sent.append(cp)
        for p, (px, py) in enumerate(peers):
            for i in range(nw):
                blk = outs[i].at[2 * px + py, _half(rows[i], 1 - c)]
                _remote(blk, blk, d2d_send, d2d_recv, 3 * i + p, (x, y, 1 - c)).wait_recv()
        for cp in sent:
            cp.wait_send()

    dma = pltpu.SemaphoreType.DMA
    return pl.pallas_call(
        body, name="gather_weights", in_specs=[_ANY] * nw, out_specs=[_ANY] * nw,
        out_shape=[jax.ShapeDtypeStruct((N_CHIPS,) + s.shape, s.dtype) for s in shards],
        scratch_shapes=[dma((3 * nw,)), dma((3 * nw,)), dma((3 * nw,)), dma((3 * nw,))],
    )(*shards)


_HBM = pl.BlockSpec(memory_space=pltpu.HBM)
_SEM = pl.BlockSpec(memory_space=pltpu.SEMAPHORE)
_DATAFLOW = pltpu.SideEffectType.DATAFLOW_SIDE_EFFECTING


def _exchange_start(shards, name, scatter):
    nw = len(shards)
    lands = [lax.empty(s.shape if scatter else (N_CHIPS,) + s.shape, s.dtype) for s in shards]

    def body(*refs):
        ins, zones = refs[:nw], refs[nw:2 * nw]
        send_sems, recv_sems, token = refs[2 * nw], refs[2 * nw + 1], refs[-1]
        x, y, c, me, peers = _place()
        for i in range(nw):
            for p, (px, py) in enumerate(peers):
                src = ins[i].at[2 * px + py] if scatter else ins[i]
                _remote(src, zones[i].at[me], send_sems, recv_sems, 3 * i + p, (px, py, c)).start()
        token[...] = jnp.zeros_like(token)

    thru = [pltpu.HBM(a.shape, a.dtype) for a in list(shards) + lands]
    dma = pltpu.SemaphoreType.DMA
    outs = pl.pallas_call(
        body, name=name,
        out_shape=(dma((3 * nw,)), dma((3 * nw,)), *thru, jax.ShapeDtypeStruct((SUBLANES, LANES), F32)),
        in_specs=[_HBM] * (2 * nw),
        out_specs=(_SEM, _SEM, *([_HBM] * (2 * nw)), pl.BlockSpec(memory_space=pltpu.VMEM)),
        input_output_aliases={i: 2 + i for i in range(2 * nw)},
        compiler_params=pltpu.CompilerParams(has_side_effects=_DATAFLOW),
    )(*[pltpu.with_memory_space_constraint(a, pltpu.HBM) for a in list(shards) + lands])
    return outs[0], outs[1], list(outs[2:2 + nw]), list(outs[2 + nw:2 + 2 * nw]), outs[-1]


def _exchange_wait(send_sems, recv_sems, shards, lands, after, name, scatter):
    nw = len(shards)

    def body(*refs):
        ins, zones = refs[:nw], refs[nw:2 * nw]
        send_sems, recv_sems = refs[2 * nw], refs[2 * nw + 1]
        x, y, c, me, peers = _place()
        for i in range(nw):
            for p, (px, py) in enumerate(peers):
                src = ins[i].at[2 * px + py] if scatter else ins[i]
                cp = _remote(src, zones[i].at[2 * px + py], send_sems, recv_sems, 3 * i + p, (px, py, c))
                cp.wait_send()
                cp.wait_recv()

    both = list(shards) + list(lands)
    outs = pl.pallas_call(
        body, name=name,
        out_shape=tuple(pltpu.HBM(a.shape, a.dtype) for a in both),
        in_specs=(*([_HBM] * (2 * nw)), _SEM, _SEM, _ANY), out_specs=[_HBM] * (2 * nw),
        input_output_aliases={i: i for i in range(2 * nw)},
        compiler_params=pltpu.CompilerParams(has_side_effects=_DATAFLOW),
    )(*both, send_sems, recv_sems, after)
    return list(outs[:nw]), list(outs[nw:])


def _sibling_start(grads, name):
    nw = len(grads)
    lands = [lax.empty((N_CHIPS, g.shape[1] // 2, g.shape[2]), g.dtype) for g in grads]

    def body(*refs):
        ins, zones = refs[:nw], refs[nw:2 * nw]
        send_sems, recv_sems, token = refs[2 * nw], refs[2 * nw + 1], refs[-1]
        x, y, c, _, _ = _place()
        for i in range(nw):
            _remote(ins[i].at[pl.ds(0, N_CHIPS), _half(grads[i].shape[1], 1 - c)], zones[i], send_sems, recv_sems,
                    i, (x, y, 1 - c)).start()
        token[...] = jnp.zeros_like(token)

    thru = [pltpu.HBM(a.shape, a.dtype) for a in list(grads) + lands]
    dma = pltpu.SemaphoreType.DMA
    outs = pl.pallas_call(
        body, name=name,
        out_shape=(dma((nw,)), dma((nw,)), *thru, jax.ShapeDtypeStruct((SUBLANES, LANES), F32)),
        in_specs=[_HBM] * (2 * nw),
        out_specs=(_SEM, _SEM, *([_HBM] * (2 * nw)), pl.BlockSpec(memory_space=pltpu.VMEM)),
        input_output_aliases={i: 2 + i for i in range(2 * nw)},
        compiler_params=pltpu.CompilerParams(has_side_effects=_DATAFLOW),
    )(*[pltpu.with_memory_space_constraint(a, pltpu.HBM) for a in list(grads) + lands])
    return outs[0], outs[1], list(outs[2:2 + nw]), list(outs[2 + nw:2 + 2 * nw]), outs[-1]


def _sibling_wait(send_sems, recv_sems, grads, lands, after, name):
    nw = len(grads)

    def body(*refs):
        ins, zones = refs[:nw], refs[nw:2 * nw]
        send_sems, recv_sems = refs[2 * nw], refs[2 * nw + 1]
        x, y, c, _, _ = _place()
        for i in range(nw):
            cp = _remote(ins[i].at[pl.ds(0, N_CHIPS), _half(grads[i].shape[1], 1 - c)], zones[i], send_sems, recv_sems,
                         i, (x, y, 1 - c))
            cp.wait_send()
            cp.wait_recv()

    both = list(grads) + list(lands)
    outs = pl.pallas_call(
        body, name=name,
        out_shape=tuple(pltpu.HBM(a.shape, a.dtype) for a in both),
        in_specs=(*([_HBM] * (2 * nw)), _SEM, _SEM, _ANY), out_specs=[_HBM] * (2 * nw),
        input_output_aliases={i: i for i in range(2 * nw)},
        compiler_params=pltpu.CompilerParams(has_side_effects=_DATAFLOW),
    )(*both, send_sems, recv_sems, after)
    return list(outs[:nw]), list(outs[nw:])


def _reduce_to_sibling(grads, name):
    nw = len(grads)

    def body(*refs):
        ins, outs = refs[:nw], refs[nw:2 * nw]
        send_sems, recv_sems = refs[2 * nw:]
        x, y, c, _, _ = _place()
        sent = []
        for i in range(nw):
            cp = _remote(ins[i].at[pl.ds(0, N_CHIPS), _half(grads[i].shape[1], 1 - c)], outs[i], send_sems, recv_sems,
                         i, (x, y, 1 - c))
            cp.start()
            sent.append(cp)
        for cp in sent:
            cp.wait()

    dma = pltpu.SemaphoreType.DMA
    return pl.pallas_call(
        body, name=name, in_specs=[_ANY] * nw, out_specs=[_ANY] * nw,
        out_shape=[jax.ShapeDtypeStruct((N_CHIPS, g.shape[1] // 2, g.shape[2]), g.dtype) for g in grads],
        scratch_shapes=[dma((nw,)), dma((nw,))],
    )(*grads)


def _reduce_back(totals, name):
    nw = len(totals)

    def body(*refs):
        outs = refs[nw:2 * nw]
        send_sems, recv_sems = refs[2 * nw:]
        x, y, c, _, _ = _place()
        sent = []
        for i in range(nw):
            blk = outs[i].at[_half(totals[i].shape[0], c)]
            cp = _remote(blk, blk, send_sems, recv_sems, i, (x, y, 1 - c))
            cp.start()
            sent.append(cp)
        for i in range(nw):
            blk = outs[i].at[_half(totals[i].shape[0], 1 - c)]
            _remote(blk, blk, send_sems, recv_sems, i, (x, y, 1 - c)).wait_recv()
        for cp in sent:
            cp.wait_send()

    dma = pltpu.SemaphoreType.DMA
    return pl.pallas_call(
        body, name=name, in_specs=[_ANY] * nw, out_specs=[_ANY] * nw,
        out_shape=[jax.ShapeDtypeStruct(t.shape, t.dtype) for t in totals],
        input_output_aliases={i: i for i in range(nw)},
        scratch_shapes=[dma((nw,)), dma((nw,))],
    )(*totals)


def _all_reduce_small(v, name):
    rows, w = v.shape
    hr = rows // 2
    assert hr % SUBLANES == 0

    def body(v_ref, out_ref, sib_ref, half_ref, chips_ref, send_sems, recv_sems):
        x, y, c, me, peers = _place()
        sibling = (x, y, 1 - c)
        mine = pl.ds(pl.multiple_of(c * hr, SUBLANES), hr)
        other = pl.ds(pl.multiple_of((1 - c) * hr, SUBLANES), hr)
        cp = _remote(v_ref, sib_ref, send_sems, recv_sems, 0, sibling)
        cp.start()
        cp.wait()
        half_ref[...] = v_ref[mine, :] + sib_ref[mine, :]
        sent = []
        for p, (px, py) in enumerate(peers):
            cp = _remote(half_ref, chips_ref.at[me], send_sems, recv_sems, 1 + p, (px, py, c))
            cp.start()
            sent.append(cp)
        chips_ref[me] = half_ref[...]
        for p, (px, py) in enumerate(peers):
            _remote(half_ref, chips_ref.at[2 * px + py], send_sems, recv_sems, 1 + p, (px, py, c)).wait_recv()
        for cp in sent:
            cp.wait_send()
        out_ref[mine, :] = ((chips_ref[0] + chips_ref[1]) + chips_ref[2]) + chips_ref[3]
        cp = _remote(out_ref.at[mine], out_ref.at[mine], send_sems, recv_sems, 4, sibling)
        cp.start()
        _remote(out_ref.at[other], out_ref.at[other], send_sems, recv_sems, 4, sibling).wait_recv()
        cp.wait_send()

    vm = pl.BlockSpec(memory_space=pltpu.VMEM)
    return pl.pallas_call(
        body, name=name, in_specs=[vm], out_specs=vm,
        out_shape=jax.ShapeDtypeStruct((rows, w), F32),
        scratch_shapes=[pltpu.VMEM((rows, w), F32), pltpu.VMEM((hr, w), F32), pltpu.VMEM((N_CHIPS, hr, w), F32),
                        pltpu.SemaphoreType.DMA((5,)), pltpu.SemaphoreType.DMA((5,))],
        compiler_params=pltpu.CompilerParams(vmem_limit_bytes=VMEM_LIMIT),
    )(v)


ELEMENTWISE_BLOCK = 512 * 1024


def _rows_tile(rows, cols):
    best = None
    for t in range(SUBLANES, rows + 1, SUBLANES):
        if rows % t == 0 and t * cols <= ELEMENTWISE_BLOCK:
            best = t
    return rows if best is None else best


def _add_pair(g, t, core, name):
    nb, n, w = t.shape
    tr = _rows_tile(n, w)
    steps = n // tr

    def body(core_ref, g_ref, t_ref, o_ref):
        o_ref[...] = (g_ref[...] + t_ref[...]).astype(BF16)

    spec = pl.BlockSpec((1, tr, w), lambda j, i, core_ref: (j, i, 0))
    return pl.pallas_call(
        body, name=name,
        grid_spec=pltpu.PrefetchScalarGridSpec(
            num_scalar_prefetch=1, grid=(nb, steps),
            in_specs=[pl.BlockSpec((1, tr, w), lambda j, i, core_ref: (j, core_ref[0] * steps + i, 0)), spec],
            out_specs=spec),
        out_shape=jax.ShapeDtypeStruct(t.shape, BF16),
        compiler_params=_params(("parallel", "parallel")))(core, g, t)


def _add_chips(landed, pairs, place, name):
    nb, n, w = landed.shape
    tr = _rows_tile(n, w)
    steps = n // tr

    def body(place_ref, r_ref, own_ref, o_ref):
        me = place_ref[0]
        acc = None
        for k in range(nb):
            blk = jnp.where(me == k, own_ref[0], r_ref[k]).astype(F32)
            acc = blk if acc is None else acc + blk
        o_ref[...] = acc

    return pl.pallas_call(
        body, name=name,
        grid_spec=pltpu.PrefetchScalarGridSpec(
            num_scalar_prefetch=1, grid=(steps,),
            in_specs=[pl.BlockSpec((nb, tr, w), lambda i, place_ref: (0, i, 0)),
                      pl.BlockSpec((1, tr, w), lambda i, place_ref: (place_ref[0], i, 0))],
            out_specs=pl.BlockSpec((tr, w), lambda i, place_ref: (place_ref[1] * steps + i, 0))),
        out_shape=jax.ShapeDtypeStruct((2 * n, w), F32),
        compiler_params=_params(("parallel",)))(place, landed, pairs)


def _adamw(w, g, m, v, name):
    rows, wd = w.shape
    tr = _rows_tile(rows, wd)
    c1 = 1.0 - ADAM_B1 ** ADAM_STEP
    c2 = 1.0 - ADAM_B2 ** ADAM_STEP

    def body(w_ref, g_ref, m_ref, v_ref, go_ref, d_ref, mo_ref, vo_ref):
        gv = g_ref[...]
        go_ref[...] = gv
        m2 = ADAM_B1 * m_ref[...] + (1.0 - ADAM_B1) * gv
        v2 = ADAM_B2 * v_ref[...] + (1.0 - ADAM_B2) * (gv * gv)
        mo_ref[...] = m2
        vo_ref[...] = v2
        d_ref[...] = -ADAM_LR * ((m2 / c1) / (jnp.sqrt(v2 / c2) + ADAM_EPS) + ADAM_WD * w_ref[...])

    spec = pl.BlockSpec((tr, wd), lambda i: (i, 0))
    shp = jax.ShapeDtypeStruct((rows, wd), F32)
    return pl.pallas_call(body, name=name, grid=(rows // tr,), in_specs=[spec] * 4, out_specs=[spec] * 4,
                          out_shape=[shp] * 4, compiler_params=_params(("parallel",)))(w, g, m, v)


BIG = [("w_in", (1024, 3232), 1), ("w_uq", (384, 768), 1), ("w_uk", (256, 512), 1), ("w_uv", (256, 512), 1),
       ("w_glu", (512, 512), 0), ("w_branch_attn", (512, 1024), 1), ("w_branch_ssm", (512, 1024), 1),
       ("w_out", (1024, 1024), 0), ("w_up", (1024, 5632), 1), ("conv_w", (3, 5632), 1), ("w_down", (2816, 1024), 0)]
SMALL = [("mix_norm_pre", (1024,)), ("q_norm", (384,)), ("kv_norm", (256,)), ("ssm_lambda_re", (32, 64)),
         ("ssm_lambda_im", (32, 64)), ("ssm_log_dt", (32,)), ("ssm_b_re", (32, 64, 16)), ("ssm_b_im", (32, 64, 16)),
         ("ssm_c_re", (32, 16, 64)), ("ssm_c_im", (32, 16, 64)), ("ssm_d", (32, 16)), ("b_glu", (512,)),
         ("b_gate", (2048,)), ("mix_norm_post", (1024,)), ("ffn_norm_pre", (1024,)), ("conv_b", (5632,)),
         ("ffn_norm_post", (1024,))]
MATMUL_W = [b for b in BIG if b[0] != "conv_w"]
LATE_W = ("w_up", "w_down", "conv_w")
CONV_W_SHAPE = (3, 2 * D_FF)
CONV_W_SHARD = (3, 2 * D_FF // N_CHIPS)
SMALL_SUM = [("loss", (1,))] + SMALL + [("conv_w", CONV_W_SHAPE)]
SMALL_ADAM = SMALL + [("conv_w", CONV_W_SHARD)]


def _pack_flat(layout, vals):
    flat = jnp.concatenate([vals[n].astype(F32).reshape(-1) for n, _ in layout])
    rows = -(-(-(-flat.shape[0] // FLAT_W)) // (2 * SUBLANES)) * 2 * SUBLANES
    return jnp.pad(flat, (0, rows * FLAT_W - flat.shape[0])).reshape(rows, FLAT_W)


def _unpack_flat(layout, flat):
    flat = flat.reshape(-1)
    out = {}
    o = 0
    for name, shape in layout:
        n = math.prod(shape)
        out[name] = flat[o:o + n].reshape(shape)
        o += n
    return out


_ARG_NAMES = ["x", "positions"] + [n for n in (
    "mix_norm_pre", "w_in", "q_norm", "w_uq", "kv_norm", "w_uk", "w_uv", "ssm_lambda_re", "ssm_lambda_im", "ssm_log_dt",
    "ssm_b_re", "ssm_b_im", "ssm_c_re", "ssm_c_im", "ssm_d", "w_glu", "b_glu", "w_branch_attn", "w_branch_ssm",
    "b_gate", "w_out", "mix_norm_post", "ffn_norm_pre", "w_up", "conv_w", "conv_b", "w_down", "ffn_norm_post")]
_WEIGHTS = _ARG_NAMES[2:]


def _gather_weights(w):
    early = [b for b in MATMUL_W if b[0] not in LATE_W]
    late = [b for b in BIG if b[0] in LATE_W]
    own = (jnp.arange(N_CHIPS) == 2 * lax.axis_index("x") + lax.axis_index("y"))[:, None, None]

    def whole(layout, mine, gathered):
        return {name: _from_chip_major(jnp.where(own, s[None], g), axis)
                for (name, _, axis), s, g in zip(layout, mine, gathered)}

    mine = [w[name].astype(BF16) for name, _, _ in early]
    gathered = _gather_big(mine)
    full = whole(early, mine, gathered)
    mine_late = [w[name].astype(F32 if name == "conv_w" else BF16) for name, _, _ in late]
    _, mine_late = lax.optimization_barrier((gathered[0], mine_late))
    send_sems, recv_sems, shards_thru, lands_thru, token = _exchange_start(mine_late, "gather_late_start", scatter=False)

    def late_weights(after):
        shards, lands = _exchange_wait(send_sems, recv_sems, shards_thru, lands_thru, after, "gather_late_wait",
                                       scatter=False)
        return whole(late, shards, lands)

    full["late"] = late_weights
    full["token"] = token[0, 0]
    return full


def _pair_sums(names, grads, tag):
    core = lax.axis_index("c").astype(jnp.int32).reshape(1)
    theirs = _reduce_to_sibling(grads, "reduce_grads_d2d" + tag)
    return [_add_pair(g, t, core, "reduce_pair_" + n) for n, g, t in zip(names, grads, theirs)]


def _send_grads(tag, grads, after, flying, pending):
    token = jnp.zeros((), F32)
    if flying:
        tag0, names0, state0 = flying.pop()
        core = lax.axis_index("c").astype(jnp.int32).reshape(1)
        mine, theirs = _sibling_wait(*state0, after, "reduce_" + tag0 + "_d2d_wait")
        pairs = [_add_pair(g, t, core, "reduce_pair_" + n) for n, g, t in zip(names0, mine, theirs)]
        send_sems, recv_sems, pairs_thru, lands_thru, tok = _exchange_start(pairs, "reduce_" + tag0 + "_start",
                                                                          scatter=True)
        pending.append((tag0, names0, send_sems, recv_sems, pairs_thru, lands_thru))
        token = token + tok[0, 0]
    if grads:
        names = list(grads)
        send_sems, recv_sems, grads_thru, lands_thru, tok = _sibling_start([grads[n] for n in names],
                                                                          "reduce_" + tag + "_d2d_start")
        flying.append((tag, names, (send_sems, recv_sems, grads_thru, lands_thru)))
        token = token + tok[0, 0]
    return token


def _reduce_grads(gbig, pending, loss, gsmall, use_sent):
    core = lax.axis_index("c").astype(jnp.int32).reshape(1)
    chip = (2 * lax.axis_index("x") + lax.axis_index("y")).astype(jnp.int32).reshape(1)
    place = jnp.concatenate([chip, core])

    def finish(names, pairs, landed, name):
        totals = [_add_chips(r, p, place, "reduce_chips_" + n) for n, r, p in zip(names, landed, pairs)]
        return dict(zip(names, _reduce_back(totals, name)))

    names = list(gbig)
    pairs = _pair_sums(names, [gbig[n] for n in names], "")
    send_sems, recv_sems, pairs_thru, lands_thru, token = _exchange_start(pairs, "reduce_last_start", scatter=True)
    sent_names, sent_pairs, sent_landed = [], [], []
    for tag, group, g_send, g_recv, g_pairs, g_lands in pending:
        got_pairs, got_landed = _exchange_wait(g_send, g_recv, g_pairs, g_lands, token, "reduce_" + tag + "_wait",
                                               scatter=True)
        sent_names, sent_pairs, sent_landed = sent_names + group, sent_pairs + got_pairs, sent_landed + got_landed
    g_sent = finish(sent_names, sent_pairs, sent_landed, "reduce_back_sent")
    vals = dict(gsmall)
    vals["loss"] = loss
    small_red = _unpack_flat(SMALL_SUM, _all_reduce_small(_pack_flat(SMALL_SUM, vals), "reduce_small"))
    after = use_sent(g_sent, small_red)
    pairs, landed = _exchange_wait(send_sems, recv_sems, pairs_thru, lands_thru, after, "reduce_last_wait", scatter=True)
    return finish(names, pairs, landed, "reduce_back_last"), small_red


def _step(args):
    x = args["x"][0]
    positions = args["positions"][0]
    tgt = args["loss_target"][0]
    w = {n: args[n][0] for n in _WEIGHTS}
    m = {n: args["m_" + n][0] for n in _WEIGHTS}
    v = {n: args["v_" + n][0] for n in _WEIGHTS}

    full = _gather_weights(w)
    sp = {n: w[n].reshape(s) for n, s in SMALL}
    for n in ("mix_norm_pre", "q_norm", "kv_norm", "b_glu", "b_gate", "mix_norm_post", "ffn_norm_pre", "conv_b",
              "ffn_norm_post"):
        sp[n] = sp[n].reshape(1, -1)
    sp["mix_norm_pre"] = sp["mix_norm_pre"] + full.pop("token")
    pending, flying = [], []
    full["send_grads"] = lambda tag, grads, after=None: _send_grads(tag, grads, after, flying, pending)
    loss, gx, gbig, gsmall = _local_step(x, positions, tgt, full, sp)
    outs = {}

    def adam_big(g_red):
        for name in g_red:
            g2, d, m2, v2 = _adamw(w[name], g_red[name], m[name], v[name], "adamw_" + name)
            outs["grad_" + name], outs["delta_" + name], outs["new_m_" + name], outs["new_v_" + name] = g2, d, m2, v2
        return v2

    def use_sent(g_sent, small_red):
        chip = 2 * lax.axis_index("x") + lax.axis_index("y")
        grads = dict(small_red)
        grads["conv_w"] = lax.dynamic_slice_in_dim(small_red["conv_w"], chip * CONV_W_SHARD[1], CONV_W_SHARD[1], axis=1)
        outs.update({"grad_" + n: grads[n] for n, _ in SMALL_ADAM})
        _, d_sm, m_sm, v_sm = _adamw(_pack_flat(SMALL_ADAM, w), _pack_flat(SMALL_ADAM, grads),
                                     _pack_flat(SMALL_ADAM, m), _pack_flat(SMALL_ADAM, v), "adamw_small")
        for prefix, flat in (("delta_", d_sm), ("new_m_", m_sm), ("new_v_", v_sm)):
            for n, val in _unpack_flat(SMALL_ADAM, flat).items():
                outs[prefix + n] = val
        return adam_big(g_sent)

    g_last, small_red = _reduce_grads(gbig, pending, loss, gsmall, use_sent)
    adam_big(g_last)
    outs = {n: val.reshape(args[n.split("_", 1)[1] if not n.startswith("new_") else n[6:]].shape)
            for n, val in outs.items()}
    res = [small_red["loss"][0], gx[None]]
    for prefix in ("grad_", "delta_", "new_m_", "new_v_"):
        res += [outs[prefix + n] for n in _WEIGHTS]
    return tuple(res)


def kernel(x, positions, mix_norm_pre, w_in, q_norm, w_uq, kv_norm, w_uk, w_uv, ssm_lambda_re, ssm_lambda_im, ssm_log_dt, ssm_b_re, ssm_b_im, ssm_c_re, ssm_c_im, ssm_d, w_glu, b_glu, w_branch_attn, w_branch_ssm, b_gate, w_out, mix_norm_post, ffn_norm_pre, w_up, conv_w, conv_b, w_down, ffn_norm_post, loss_target, m_mix_norm_pre, m_w_in, m_q_norm, m_w_uq, m_kv_norm, m_w_uk, m_w_uv, m_ssm_lambda_re, m_ssm_lambda_im, m_ssm_log_dt, m_ssm_b_re, m_ssm_b_im, m_ssm_c_re, m_ssm_c_im, m_ssm_d, m_w_glu, m_b_glu, m_w_branch_attn, m_w_branch_ssm, m_b_gate, m_w_out, m_mix_norm_post, m_ffn_norm_pre, m_w_up, m_conv_w, m_conv_b, m_w_down, m_ffn_norm_post, v_mix_norm_pre, v_w_in, v_q_norm, v_w_uq, v_kv_norm, v_w_uk, v_w_uv, v_ssm_lambda_re, v_ssm_lambda_im, v_ssm_log_dt, v_ssm_b_re, v_ssm_b_im, v_ssm_c_re, v_ssm_c_im, v_ssm_d, v_w_glu, v_b_glu, v_w_branch_attn, v_w_branch_ssm, v_b_gate, v_w_out, v_mix_norm_post, v_ffn_norm_pre, v_w_up, v_conv_w, v_conv_b, v_w_down, v_ffn_norm_post):
    given = dict(locals())
    return _step(given)
```

```python
import math

import jax
import jax.numpy as jnp
from jax import lax
from jax.experimental import pallas as pl
from jax.experimental.pallas import tpu as pltpu

F32 = jnp.float32
BF16 = jnp.bfloat16
MESH = pl.DeviceIdType.MESH

D_MODEL = 1024
N_HEADS = 8
QK_NOPE = 64
QK_ROPE = 32
QK_HEAD = QK_NOPE + QK_ROPE
V_HEAD = 64
Q_RANK = 384
KV_RANK = 256
ROPE_THETA = 10000.0
SSM_W = 512
SSM_H = 16
SSM_G = 32
SSM_P = 64
SSM_CH = SSM_G * SSM_P
D_FF = 2816
EPS = 1e-6
ADAM_LR = 0.001
ADAM_B1 = 0.9
ADAM_B2 = 0.999
ADAM_EPS = 1e-08
ADAM_WD = 0.01
ADAM_STEP = 10

LANES = 128
SUBLANES = 8
VMEM_LIMIT = 56 * 1024 * 1024

HEAD_SLOT = LANES
HP = N_HEADS * HEAD_SLOT
P_CQ, P_CKV, P_KR, P_U, P_GL, P_END = 0, 384, 640, 768, 1280, 3328
KR_LANE = 64

FLAT_W = 1024
N_CHIPS = 4


def _tile(n, cap):
    if n <= cap:
        return n
    best = None
    for t in range(LANES, cap + 1, LANES):
        if n % t == 0:
            best = t
    assert best is not None, (n, cap)
    return best


def _params(sem):
    return pltpu.CompilerParams(dimension_semantics=sem, vmem_limit_bytes=VMEM_LIMIT)


def _dot(a, b):
    return jnp.dot(a, b, preferred_element_type=F32)


def _dot_nt(a, b):
    return lax.dot_general(a, b, (((1,), (1,)), ((), ())), preferred_element_type=F32)


def _dot_tn(a, b):
    return lax.dot_general(a, b, (((0,), (0,)), ((), ())), preferred_element_type=F32)


def _rms(x, g):
    r = lax.rsqrt(jnp.mean(x * x, axis=-1, keepdims=True) + EPS)
    return x * r * g, r


def _rms_bwd(dy, x, g):
    r = lax.rsqrt(jnp.mean(x * x, axis=-1, keepdims=True) + EPS)
    dyg = dy * g
    dx = r * dyg - x * (r * r * r) * jnp.mean(dyg * x, axis=-1, keepdims=True)
    dg = jnp.sum(dy * x * r, axis=0, keepdims=True)
    return dx, dg


_GELU_K0 = math.sqrt(2.0 / math.pi)
_GELU_K1 = 0.044715


def _gelu(x):
    th = jnp.tanh(_GELU_K0 * (x + _GELU_K1 * x * x * x))
    return 0.5 * x * (1.0 + th)


def _gelu_grad(x):
    th = jnp.tanh(_GELU_K0 * (x + _GELU_K1 * x * x * x))
    return 0.5 * (1.0 + th) + 0.5 * x * (1.0 - th * th) * _GELU_K0 * (1.0 + 3.0 * _GELU_K1 * x * x)


def _sigmoid(x):
    return 1.0 / (1.0 + jnp.exp(-x))


def _rope(q, c, s):
    n = q.shape[1]
    lane = lax.broadcasted_iota(jnp.int32, q.shape, 1) % HEAD_SLOT
    sw = jnp.where(lane < KR_LANE + QK_ROPE // 2, pltpu.roll(q, n - QK_ROPE // 2, 1), pltpu.roll(q, QK_ROPE // 2, 1))
    return q * c + sw * s


def _rope_bwd(dy, c, s):
    n = dy.shape[1]
    t = dy * s
    lane = lax.broadcasted_iota(jnp.int32, dy.shape, 1) % HEAD_SLOT
    sw = jnp.where(lane < KR_LANE + QK_ROPE // 2, pltpu.roll(t, n - QK_ROPE // 2, 1), pltpu.roll(t, QK_ROPE // 2, 1))
    rope_lane = jnp.logical_and(lane >= KR_LANE, lane < KR_LANE + QK_ROPE)
    return dy * c + jnp.where(rope_lane, sw, 0.0)


def _shift_down(x, k, halo):
    xs = pltpu.roll(x, k, 0)
    hs = pltpu.roll(halo, k, 0)
    rows = lax.broadcasted_iota(jnp.int32, halo.shape, 0)
    top = jnp.where(rows < k, hs, xs[0:SUBLANES])
    return jnp.concatenate([top, xs[SUBLANES:]], axis=0)


def _shift_up(x, k, halo):
    t = x.shape[0]
    xs = pltpu.roll(x, t - k, 0)
    hs = pltpu.roll(halo, SUBLANES - k, 0)
    rows = lax.broadcasted_iota(jnp.int32, halo.shape, 0)
    bot = jnp.where(rows >= SUBLANES - k, hs, xs[t - SUBLANES:])
    return jnp.concatenate([xs[:t - SUBLANES], bot], axis=0)


def _mm(a, b, name, out_dtype=F32, bt=False, tm_cap=1024, tn_cap=1408):
    m, k = a.shape[-2:]
    parts = a.shape[0] if a.ndim == 3 else 1
    n = b.shape[0] if bt else b.shape[1]
    tm = min(tm_cap, m)
    tn = _tile(n, tn_cap)

    def body(a_ref, b_ref, o_ref):
        if not bt:
            o_ref[...] = _dot(a_ref[...], b_ref[...]).astype(out_dtype)
        elif parts == 1:
            o_ref[...] = _dot_nt(a_ref[...], b_ref[...]).astype(out_dtype)
        else:
            @pl.when(pl.program_id(2) == 0)
            def _():
                o_ref[...] = jnp.zeros_like(o_ref)

            o_ref[...] += _dot_nt(a_ref[...], b_ref[...])

    if bt:
        b_spec = pl.BlockSpec((tn, k), lambda j, i, s: (j, s))
    else:
        b_spec = pl.BlockSpec((k, tn), lambda j, i, s: (0, j))
    if a.ndim == 3:
        assert bt and out_dtype == F32
        a_spec = pl.BlockSpec((None, tm, k), lambda j, i, s: (s, i, 0))
    else:
        a_spec = pl.BlockSpec((tm, k), lambda j, i, s: (i, 0))
    return pl.pallas_call(
        body, name=name, grid=(n // tn, m // tm, parts),
        in_specs=[a_spec, b_spec],
        out_specs=pl.BlockSpec((tm, tn), lambda j, i, s: (i, j)),
        out_shape=jax.ShapeDtypeStruct((m, n), out_dtype),
        compiler_params=_params(("parallel", "parallel", "arbitrary")),
    )(a, b)


def _mm_tn(a, b, name, tk_cap=1024, tn_cap=1664, tl_cap=1024, chips=False):
    l, k = a.shape
    tk = _tile(k, tk_cap)
    tl = min(tl_cap, l)

    def body(a_ref, b_ref, o_ref):
        @pl.when(pl.program_id(2) == 0)
        def _():
            o_ref[...] = jnp.zeros_like(o_ref)

        o_ref[...] += _dot_tn(a_ref[...], b_ref[...])

    if chips:
        n = b.shape[-1] * (b.shape[0] if b.ndim == 3 else 1)
        tn = n // N_CHIPS
        assert tn % LANES == 0
        if b.ndim == 3:
            per = N_CHIPS // b.shape[0]
            b_spec = pl.BlockSpec((None, tl, tn), lambda i, j, r: (j // per, r, j % per))
        else:
            b_spec = pl.BlockSpec((tl, tn), lambda i, j, r: (r, j))
        out_spec = pl.BlockSpec((None, tk, tn), lambda i, j, r: (j, i, 0))
        out_shape = jax.ShapeDtypeStruct((N_CHIPS, k, tn), F32)
    else:
        n = b.shape[1]
        tn = _tile(n, tn_cap)
        b_spec = pl.BlockSpec((tl, tn), lambda i, j, r: (r, j))
        out_spec = pl.BlockSpec((tk, tn), lambda i, j, r: (i, j))
        out_shape = jax.ShapeDtypeStruct((k, n), F32)
    return pl.pallas_call(
        body, name=name, grid=(k // tk, n // tn, l // tl),
        in_specs=[pl.BlockSpec((tl, tk), lambda i, j, r: (r, i)), b_spec],
        out_specs=out_spec, out_shape=out_shape,
        compiler_params=_params(("parallel", "parallel", "arbitrary")),
    )(a, b)


def _row(tl, n):
    return pl.BlockSpec((tl, n), lambda i: (i, 0))


def _const(shape):
    return pl.BlockSpec(shape, lambda i: tuple(0 for _ in shape))


def _proj_fwd(x, g1, win, gq, wuq, gkv, wukv, rc, rs, bg, tl):
    l = x.shape[0]

    def body(x_ref, g1_ref, win_ref, gq_ref, wuq_ref, gkv_ref, wukv_ref, rc_ref, rs_ref, bg_ref,
             hn_ref, cq_ref, ckv_ref, q_ref, k_ref, v_ref, u_ref, gl_ref):
        hn, _ = _rms(x_ref[...], g1_ref[...])
        hnb = hn.astype(BF16)
        hn_ref[...] = hnb
        proj = _dot(hnb, win_ref[...])
        cq = proj[:, P_CQ:P_CKV]
        ckv = proj[:, P_CKV:P_KR]
        kr = proj[:, P_KR:P_U]
        cq_ref[...] = cq
        ckv_ref[...] = ckv
        u_ref[...] = proj[:, P_U:P_GL]
        gl_ref[...] = proj[:, P_GL:P_END] + bg_ref[...]
        qn, _ = _rms(cq, gq_ref[...])
        q = _dot(qn.astype(BF16), wuq_ref[...])
        c1 = rc_ref[...]
        s1 = rs_ref[...]
        q_ref[...] = (_rope(q, jnp.tile(c1, (1, N_HEADS)), jnp.tile(s1, (1, N_HEADS))) * Q_PRESCALE).astype(BF16)
        ckvn, _ = _rms(ckv, gkv_ref[...])
        kv = _dot(ckvn.astype(BF16), wukv_ref[...])
        krr = _rope(kr, c1, s1)
        k_ref[...] = (kv[:, :HP] + jnp.tile(krr, (1, N_HEADS))).astype(BF16)
        v_ref[...] = kv[:, HP:].astype(BF16)

    outs = [(D_MODEL, BF16), (Q_RANK, F32), (KV_RANK, F32), (HP, BF16), (HP, BF16), (HP, BF16),
            (SSM_W, F32), (2 * D_MODEL, F32)]
    return pl.pallas_call(
        body, name="proj_fwd", grid=(l // tl,),
        in_specs=[_row(tl, D_MODEL), _const((1, D_MODEL)), _const((D_MODEL, P_END)), _const((1, Q_RANK)),
                  _const((Q_RANK, HP)), _const((1, KV_RANK)), _const((KV_RANK, 2 * HP)),
                  _row(tl, HEAD_SLOT), _row(tl, HEAD_SLOT), _const((1, 2 * D_MODEL))],
        out_specs=[_row(tl, n) for n, _ in outs],
        out_shape=[jax.ShapeDtypeStruct((l, n), dt) for n, dt in outs],
        compiler_params=_params(("parallel",)),
    )(x, g1, win, gq, wuq, gkv, wukv, rc, rs, bg)


_NEG = -1e30


LOG2E = 1.0 / math.log(2.0)
LN2 = math.log(2.0)
ATTN_SCALE = 1.0 / math.sqrt(QK_HEAD)
Q_PRESCALE = ATTN_SCALE * LOG2E
HEADS_PER_STEP = 4
PAIR_W = HEADS_PER_STEP * HEAD_SLOT


def _causal_pairs(nq, by_query):
    if by_query:
        pairs = [(i, j) for i in range(nq) for j in range(i + 1)]
    else:
        pairs = [(i, j) for j in range(nq) for i in range(j, nq)]
    return jnp.array([p[0] for p in pairs], jnp.int32), jnp.array([p[1] for p in pairs], jnp.int32)


def _diag_mask_t(s):
    rows = lax.broadcasted_iota(jnp.int32, s.shape, 0)
    cols = lax.broadcasted_iota(jnp.int32, s.shape, 1)
    return jnp.where(rows <= cols, s, _NEG)


def _attn_fwd(q, k, v, tq):
    l = q.shape[0]
    nq = l // tq
    it, jt = _causal_pairs(nq, True)

    def body(it_ref, jt_ref, q_ref, k_ref, v_ref, o_ref, lse_ref, m_ref, l_ref, acc_ref):
        t = pl.program_id(1)
        i = it_ref[t]
        j = jt_ref[t]

        @pl.when(j == 0)
        def _():
            m_ref[...] = jnp.full_like(m_ref, _NEG)
            l_ref[...] = jnp.zeros_like(l_ref)
            acc_ref[...] = jnp.zeros_like(acc_ref)

        def update(on_diagonal):
            for hh in range(HEADS_PER_STEP):
                sl = slice(hh * HEAD_SLOT, (hh + 1) * HEAD_SLOT)
                s = _dot_nt(k_ref[:, sl], q_ref[:, sl])
                if on_diagonal:
                    s = _diag_mask_t(s)
                m_old = m_ref[hh]
                m_new = jnp.maximum(m_old, jnp.max(s, axis=0, keepdims=True))
                p = jnp.exp2(s - m_new)
                alpha = jnp.exp2(m_old - m_new)
                l_ref[hh] = alpha * l_ref[hh] + jnp.sum(p, axis=0, keepdims=True)
                acc_ref[hh] = alpha * acc_ref[hh] + _dot_tn(v_ref[:, sl], p.astype(BF16))
                m_ref[hh] = m_new

        @pl.when(j < i)
        def _():
            update(False)

        @pl.when(j == i)
        def _():
            update(True)
            for hh in range(HEADS_PER_STEP):
                sl = slice(hh * HEAD_SLOT, (hh + 1) * HEAD_SLOT)
                o_ref[:, sl] = (acc_ref[hh] / l_ref[hh]).T.astype(BF16)
                lse_ref[hh] = m_ref[hh] + jnp.log(l_ref[hh]) * LOG2E

    blk = (tq, PAIR_W)
    qmap = lambda h, t, it_ref, jt_ref: (it_ref[t], h)
    kmap = lambda h, t, it_ref, jt_ref: (jt_ref[t], h)
    row = pl.BlockSpec((HEADS_PER_STEP, 1, tq), lambda h, t, it_ref, jt_ref: (h, 0, it_ref[t]))
    return pl.pallas_call(
        body, name="attn_fwd",
        grid_spec=pltpu.PrefetchScalarGridSpec(
            num_scalar_prefetch=2, grid=(N_HEADS // HEADS_PER_STEP, it.shape[0]),
            in_specs=[pl.BlockSpec(blk, qmap), pl.BlockSpec(blk, kmap), pl.BlockSpec(blk, kmap)],
            out_specs=[pl.BlockSpec(blk, qmap), row],
            scratch_shapes=[pltpu.VMEM((HEADS_PER_STEP, 1, tq), F32), pltpu.VMEM((HEADS_PER_STEP, 1, tq), F32),
                            pltpu.VMEM((HEADS_PER_STEP, HEAD_SLOT, tq), F32)]),
        out_shape=[jax.ShapeDtypeStruct((l, HP), BF16), jax.ShapeDtypeStruct((N_HEADS, 1, l), F32)],
        compiler_params=_params(("parallel", "arbitrary")),
    )(it, jt, q, k, v)


def _attn_delta(o, do, tq):
    l = o.shape[0]

    def body(o_ref, do_ref, d_ref):
        prod = o_ref[...].astype(F32) * do_ref[...].astype(F32)
        for hh in range(HEADS_PER_STEP):
            d_ref[hh] = jnp.sum(prod[:, hh * HEAD_SLOT:(hh + 1) * HEAD_SLOT].T, axis=0, keepdims=True)

    blk = pl.BlockSpec((tq, PAIR_W), lambda h, i: (i, h))
    return pl.pallas_call(
        body, name="attn_delta", grid=(N_HEADS // HEADS_PER_STEP, l // tq), in_specs=[blk, blk],
        out_specs=pl.BlockSpec((HEADS_PER_STEP, 1, tq), lambda h, i: (h, 0, i)),
        out_shape=jax.ShapeDtypeStruct((N_HEADS, 1, l), F32),
        compiler_params=_params(("parallel", "parallel")),
    )(o, do)


def _attn_bwd(q, k, v, do, lse, delta, tq):
    l = q.shape[0]
    nq = l // tq
    it, jt = _causal_pairs(nq, False)

    def body(it_ref, jt_ref, q_ref, k_ref, v_ref, do_ref, lse_ref, dl_ref, dq_ref, dk_ref, dv_ref, dka_ref, dva_ref):
        t = pl.program_id(1)
        i = it_ref[t]
        j = jt_ref[t]

        @pl.when(t == 0)
        def _():
            dq_ref[...] = jnp.zeros_like(dq_ref)

        @pl.when(i == j)
        def _():
            dka_ref[...] = jnp.zeros_like(dka_ref)
            dva_ref[...] = jnp.zeros_like(dva_ref)

        def update(on_diagonal):
            r0 = pl.multiple_of(i * tq, tq)
            for hh in range(HEADS_PER_STEP):
                sl = slice(hh * HEAD_SLOT, (hh + 1) * HEAD_SLOT)
                qb = q_ref[:, sl]
                kb = k_ref[:, sl]
                dob = do_ref[:, sl]
                s = _dot_nt(kb, qb)
                if on_diagonal:
                    s = _diag_mask_t(s)
                p = jnp.exp2(s - lse_ref[hh])
                dva_ref[:, sl] += _dot(p.astype(BF16), dob)
                dp = _dot_nt(v_ref[:, sl], dob)
                ds = (p * (dp - dl_ref[hh])).astype(BF16)
                dka_ref[:, sl] += _dot(ds, qb)
                dq_ref[pl.ds(r0, tq), sl] += ATTN_SCALE * _dot_tn(ds, kb)

        @pl.when(j < i)
        def _():
            update(False)

        @pl.when(j == i)
        def _():
            update(True)

        @pl.when(i == nq - 1)
        def _():
            dk_ref[...] = (dka_ref[...] * LN2).astype(BF16)
            dv_ref[...] = dva_ref[...].astype(BF16)

    blk = (tq, PAIR_W)
    qmap = lambda h, t, it_ref, jt_ref: (it_ref[t], h)
    kmap = lambda h, t, it_ref, jt_ref: (jt_ref[t], h)
    row = pl.BlockSpec((HEADS_PER_STEP, 1, tq), lambda h, t, it_ref, jt_ref: (h, 0, it_ref[t]))
    return pl.pallas_call(
        body, name="attn_bwd",
        grid_spec=pltpu.PrefetchScalarGridSpec(
            num_scalar_prefetch=2, grid=(N_HEADS // HEADS_PER_STEP, it.shape[0]),
            in_specs=[pl.BlockSpec(blk, qmap), pl.BlockSpec(blk, kmap), pl.BlockSpec(blk, kmap),
                      pl.BlockSpec(blk, qmap), row, row],
            out_specs=[pl.BlockSpec((l, PAIR_W), lambda h, t, it_ref, jt_ref: (0, h)), pl.BlockSpec(blk, kmap),
                       pl.BlockSpec(blk, kmap)],
            scratch_shapes=[pltpu.VMEM(blk, F32), pltpu.VMEM(blk, F32)]),
        out_shape=[jax.ShapeDtypeStruct((l, HP), F32), jax.ShapeDtypeStruct((l, HP), BF16),
                   jax.ShapeDtypeStruct((l, HP), BF16)],
        compiler_params=_params(("parallel", "arbitrary")),
    )(it, jt, q, k, v, do, lse, delta)


SSM_CB = 512
SSM_UB = 128
SSM_NB = SSM_CH // SSM_CB


def _scan_tiles(re_ref, im_ref, tab, carry, n_tiles, reverse):
    group = 2
    assert n_tiles % group == 0
    pr, pi = tab[6], tab[7]

    def inside(sr, si):
        for step, k in enumerate((1, 2, 4)):
            mr, mi = tab[2 * step], tab[2 * step + 1]
            sh = (SUBLANES - k) if reverse else k
            rr = pltpu.roll(sr, sh, 0)
            ri = pltpu.roll(si, sh, 0)
            sr, si = sr + mr * rr - mi * ri, si + mr * ri + mi * rr
        return sr, si

    def body(n, c):
        cr, ci = c
        first = (n_tiles - group * (n + 1)) if reverse else group * n
        r0 = pl.multiple_of(first * SUBLANES, group * SUBLANES)
        rows = [pl.ds(r0 + g * SUBLANES, SUBLANES) for g in range(group)]
        tiles = [inside(re_ref[r, :], im_ref[r, :]) for r in rows]
        for g in (range(group - 1, -1, -1) if reverse else range(group)):
            sr, si = tiles[g]
            sr, si = sr + pr * cr - pi * ci, si + pr * ci + pi * cr
            re_ref[rows[g], :] = sr
            im_ref[rows[g], :] = si
            edge = slice(0, 1) if reverse else slice(SUBLANES - 1, SUBLANES)
            cr, ci = sr[edge, :], si[edge, :]
        return cr, ci

    return lax.fori_loop(0, n_tiles // group, body, carry)


def _ssm_fwd(u, bre, bim, cre, cim, dvec, tab, tt):
    l = u.shape[0]
    nt = l // tt

    def body(u_ref, bre_ref, bim_ref, cre_ref, cim_ref, d_ref, tab_ref, y_ref, sre_ref, sim_ref, car_ref):
        @pl.when(pl.program_id(1) == 0)
        def _():
            car_ref[...] = jnp.zeros_like(car_ref)

        uf = u_ref[...]
        ub = uf.astype(BF16)
        sre_ref[...] = _dot(ub, bre_ref[0])
        sim_ref[...] = _dot(ub, bim_ref[0])
        tab_v = [tab_ref[n] for n in range(8)]
        cr, ci = _scan_tiles(sre_ref, sim_ref, tab_v, (car_ref[0:1, :], car_ref[8:9, :]), tt // SUBLANES, False)
        car_ref[0:1, :] = cr
        car_ref[8:9, :] = ci
        y_ref[...] = (_dot(sre_ref[...].astype(BF16), cre_ref[0]) - _dot(sim_ref[...].astype(BF16), cim_ref[0])
                      + d_ref[...] * uf)

    return pl.pallas_call(
        body, name="ssm_fwd", grid=(SSM_NB, nt),
        in_specs=[pl.BlockSpec((tt, SSM_UB), lambda m, t: (t, m)),
                  pl.BlockSpec((1, SSM_UB, SSM_CB), lambda m, t: (m, 0, 0)),
                  pl.BlockSpec((1, SSM_UB, SSM_CB), lambda m, t: (m, 0, 0)),
                  pl.BlockSpec((1, SSM_CB, SSM_UB), lambda m, t: (m, 0, 0)),
                  pl.BlockSpec((1, SSM_CB, SSM_UB), lambda m, t: (m, 0, 0)),
                  pl.BlockSpec((1, SSM_UB), lambda m, t: (0, m)),
                  pl.BlockSpec((8, SUBLANES, SSM_CB), lambda m, t: (0, 0, m))],
        out_specs=[pl.BlockSpec((tt, SSM_UB), lambda m, t: (t, m)),
                   pl.BlockSpec((tt, SSM_CB), lambda m, t: (t, m)),
                   pl.BlockSpec((tt, SSM_CB), lambda m, t: (t, m))],
        out_shape=[jax.ShapeDtypeStruct((l, SSM_W), F32), jax.ShapeDtypeStruct((l, SSM_CH), F32),
                   jax.ShapeDtypeStruct((l, SSM_CH), F32)],
        scratch_shapes=[pltpu.VMEM((2 * SUBLANES, SSM_CB), F32)],
        compiler_params=_params(("parallel", "arbitrary")),
    )(u, bre, bim, cre, cim, dvec, tab)


def _ssm_bwd(dy, u, sre, sim, bre, bim, cre, cim, dvec, tab, tt):
    l = u.shape[0]
    nt = l // tt
    tpb = tt // SUBLANES

    def body(dy_ref, u_ref, sre_ref, sim_ref, hre_ref, him_ref, bre_ref, bim_ref, cre_ref, cim_ref, d_ref, tab_ref,
             du_ref, dbre_ref, dbim_ref, dcre_ref, dcim_ref, dare_ref, daim_ref, dd_ref, lr_ref, li_ref, car_ref):
        t = pl.program_id(1)

        @pl.when(t == 0)
        def _():
            car_ref[...] = jnp.zeros_like(car_ref)
            for ref in (dbre_ref, dbim_ref, dcre_ref, dcim_ref, dare_ref, daim_ref, dd_ref):
                ref[...] = jnp.zeros_like(ref)

        dyf = dy_ref[...]
        dyb = dyf.astype(BF16)
        uf = u_ref[...]
        s_re = sre_ref[...]
        s_im = sim_ref[...]
        lr_ref[...] = _dot_nt(dyb, cre_ref[0])
        li_ref[...] = -_dot_nt(dyb, cim_ref[0])
        dcre_ref[0] += _dot_tn(s_re.astype(BF16), dyb)
        dcim_ref[0] -= _dot_tn(s_im.astype(BF16), dyb)
        tab_v = [tab_ref[n] for n in range(8)]
        cr, ci = _scan_tiles(lr_ref, li_ref, tab_v, (car_ref[0:1, :], car_ref[8:9, :]), tpb, True)
        car_ref[0:1, :] = cr
        car_ref[8:9, :] = ci
        lam_r = lr_ref[...]
        lam_i = li_ref[...]
        keep = jnp.where(t == nt - 1, 0.0, 1.0)
        sp_r = _shift_down(s_re, 1, hre_ref[...] * keep)
        sp_i = _shift_down(s_im, 1, him_ref[...] * keep)
        dare_ref[...] += jnp.sum(lam_r * sp_r + lam_i * sp_i, axis=0, keepdims=True)
        daim_ref[...] += jnp.sum(lam_i * sp_r - lam_r * sp_i, axis=0, keepdims=True)
        lrb = lam_r.astype(BF16)
        lib = lam_i.astype(BF16)
        du_ref[...] = _dot_nt(lrb, bre_ref[0]) + _dot_nt(lib, bim_ref[0]) + dyf * d_ref[...]
        ub = uf.astype(BF16)
        dbre_ref[0] += _dot_tn(ub, lrb)
        dbim_ref[0] += _dot_tn(ub, lib)
        dd_ref[...] += jnp.sum(dyf * uf, axis=0, keepdims=True)

    rev = lambda m, t: (nt - 1 - t, m)
    halo = lambda m, t: (jnp.maximum((nt - 1 - t) * tpb - 1, 0), m)
    wb = pl.BlockSpec((1, SSM_UB, SSM_CB), lambda m, t: (m, 0, 0))
    wc = pl.BlockSpec((1, SSM_CB, SSM_UB), lambda m, t: (m, 0, 0))
    vec_c = pl.BlockSpec((1, SSM_CB), lambda m, t: (0, m))
    vec_u = pl.BlockSpec((1, SSM_UB), lambda m, t: (0, m))
    return pl.pallas_call(
        body, name="ssm_bwd", grid=(SSM_NB, nt),
        in_specs=[pl.BlockSpec((tt, SSM_UB), rev), pl.BlockSpec((tt, SSM_UB), rev),
                  pl.BlockSpec((tt, SSM_CB), rev), pl.BlockSpec((tt, SSM_CB), rev),
                  pl.BlockSpec((SUBLANES, SSM_CB), halo), pl.BlockSpec((SUBLANES, SSM_CB), halo),
                  wb, wb, wc, wc, vec_u,
                  pl.BlockSpec((8, SUBLANES, SSM_CB), lambda m, t: (0, 0, m))],
        out_specs=[pl.BlockSpec((tt, SSM_UB), rev), wb, wb, wc, wc, vec_c, vec_c, vec_u],
        out_shape=[jax.ShapeDtypeStruct((l, SSM_W), F32),
                   jax.ShapeDtypeStruct((SSM_NB, SSM_UB, SSM_CB), F32), jax.ShapeDtypeStruct((SSM_NB, SSM_UB, SSM_CB), F32),
                   jax.ShapeDtypeStruct((SSM_NB, SSM_CB, SSM_UB), F32), jax.ShapeDtypeStruct((SSM_NB, SSM_CB, SSM_UB), F32),
                   jax.ShapeDtypeStruct((1, SSM_CH), F32), jax.ShapeDtypeStruct((1, SSM_CH), F32),
                   jax.ShapeDtypeStruct((1, SSM_W), F32)],
        scratch_shapes=[pltpu.VMEM((tt, SSM_CB), F32), pltpu.VMEM((tt, SSM_CB), F32),
                        pltpu.VMEM((2 * SUBLANES, SSM_CB), F32)],
        compiler_params=_params(("parallel", "arbitrary")),
    )(dy, u, sre, sim, sre, sim, bre, bim, cre, cim, dvec, tab)


def _merge_fwd(x, gl, attn, y1, wba, wbs, wglu, bglu, wout, gpost, gpre, tl):
    l = x.shape[0]

    def body(x_ref, gl_ref, at_ref, y1_ref, wba_ref, wbs_ref, wglu_ref, bglu_ref, wout_ref, gpost_ref, gpre_ref,
             a_ref, sm_ref, mg_ref, z_ref, x1_ref, hn2_ref, y3_ref):
        y2 = _gelu(y1_ref[...])
        sg = _sigmoid(_dot(y2.astype(BF16), wglu_ref[...]) + bglu_ref[...])
        y3 = (y2 * sg).astype(BF16)
        y3_ref[...] = y3
        a = _dot(at_ref[...], wba_ref[...])
        sm = _dot(y3, wbs_ref[...])
        a_ref[...] = a.astype(BF16)
        sm_ref[...] = sm.astype(BF16)
        g = _sigmoid(gl_ref[...])
        merged = (g[:, :D_MODEL] * a + g[:, D_MODEL:] * sm).astype(BF16)
        mg_ref[...] = merged
        z = _dot(merged, wout_ref[...])
        z_ref[...] = z
        n, _ = _rms(z, gpost_ref[...])
        x1 = x_ref[...] + n
        x1_ref[...] = x1
        hn2, _ = _rms(x1, gpre_ref[...])
        hn2_ref[...] = hn2.astype(BF16)

    outs = [(D_MODEL, BF16), (D_MODEL, BF16), (D_MODEL, BF16), (D_MODEL, F32), (D_MODEL, F32), (D_MODEL, BF16),
            (SSM_W, BF16)]
    return pl.pallas_call(
        body, name="merge_fwd", grid=(l // tl,),
        in_specs=[_row(tl, D_MODEL), _row(tl, 2 * D_MODEL), _row(tl, HP), _row(tl, SSM_W),
                  _const((HP, D_MODEL)), _const((SSM_W, D_MODEL)), _const((SSM_W, SSM_W)), _const((1, SSM_W)),
                  _const((D_MODEL, D_MODEL)), _const((1, D_MODEL)), _const((1, D_MODEL))],
        out_specs=[_row(tl, n) for n, _ in outs],
        out_shape=[jax.ShapeDtypeStruct((l, n), dt) for n, dt in outs],
        compiler_params=_params(("parallel",)),
    )(x, gl, attn, y1, wba, wbs, wglu, bglu, wout, gpost, gpre)


def _merge_bwd(dhn2, x1, dx2, z, gl, a, sm, y1, wba, wbs, wglu, bglu, wout, gpost, gpre, tl):
    l = x1.shape[0]

    def body(dhn2_ref, x1_ref, dx2_ref, z_ref, gl_ref, a_ref, sm_ref, y1_ref,
             wba_ref, wbs_ref, wglu_ref, bglu_ref, wout_ref, gpost_ref, gpre_ref,
             dx1_ref, dz_ref, dbra_ref, dbrs_ref, dgl_ref, dat_ref, dy1_ref, dt_ref, y2_ref,
             dgpre_ref, dgpost_ref, dbg_ref, dbglu_ref):
        @pl.when(pl.program_id(0) == 0)
        def _():
            for ref in (dgpre_ref, dgpost_ref, dbg_ref, dbglu_ref):
                ref[...] = jnp.zeros_like(ref)

        dx1a, dgpre = _rms_bwd(dhn2_ref[...], x1_ref[...], gpre_ref[...])
        dgpre_ref[...] += dgpre
        dx1 = dx2_ref[...] + dx1a
        dx1_ref[...] = dx1
        dz, dgpost = _rms_bwd(dx1, z_ref[...], gpost_ref[...])
        dgpost_ref[...] += dgpost
        dzb = dz.astype(BF16)
        dz_ref[...] = dzb
        dm = _dot_nt(dzb, wout_ref[...])
        g = _sigmoid(gl_ref[...])
        g0 = g[:, :D_MODEL]
        g1 = g[:, D_MODEL:]
        dbra = (dm * g0).astype(BF16)
        dbrs = (dm * g1).astype(BF16)
        dbra_ref[...] = dbra
        dbrs_ref[...] = dbrs
        dgl0 = dm * a_ref[...].astype(F32) * g0 * (1.0 - g0)
        dgl1 = dm * sm_ref[...].astype(F32) * g1 * (1.0 - g1)
        dgl_ref[:, :D_MODEL] = dgl0.astype(BF16)
        dgl_ref[:, D_MODEL:] = dgl1.astype(BF16)
        dbg_ref[:, :D_MODEL] += jnp.sum(dgl0, axis=0, keepdims=True)
        dbg_ref[:, D_MODEL:] += jnp.sum(dgl1, axis=0, keepdims=True)
        dat_ref[...] = _dot_nt(dbra, wba_ref[...]).astype(BF16)
        dy3 = _dot_nt(dbrs, wbs_ref[...])
        y1v = y1_ref[...]
        y2 = _gelu(y1v)
        y2b = y2.astype(BF16)
        y2_ref[...] = y2b
        sg = _sigmoid(_dot(y2b, wglu_ref[...]) + bglu_ref[...])
        dt = dy3 * y2 * sg * (1.0 - sg)
        dtb = dt.astype(BF16)
        dt_ref[...] = dtb
        dbglu_ref[...] += jnp.sum(dt, axis=0, keepdims=True)
        dy2 = dy3 * sg + _dot_nt(dtb, wglu_ref[...])
        dy1_ref[...] = dy2 * _gelu_grad(y1v)

    outs = [(D_MODEL, F32), (D_MODEL, BF16), (D_MODEL, BF16), (D_MODEL, BF16), (2 * D_MODEL, BF16), (HP, BF16),
            (SSM_W, F32), (SSM_W, BF16), (SSM_W, BF16)]
    accs = [D_MODEL, D_MODEL, 2 * D_MODEL, SSM_W]
    return pl.pallas_call(
        body, name="merge_bwd", grid=(l // tl,),
        in_specs=[_row(tl, D_MODEL), _row(tl, D_MODEL), _row(tl, D_MODEL), _row(tl, D_MODEL),
                  _row(tl, 2 * D_MODEL), _row(tl, D_MODEL), _row(tl, D_MODEL), _row(tl, SSM_W),
                  _const((HP, D_MODEL)), _const((SSM_W, D_MODEL)), _const((SSM_W, SSM_W)), _const((1, SSM_W)),
                  _const((D_MODEL, D_MODEL)), _const((1, D_MODEL)), _const((1, D_MODEL))],
        out_specs=[_row(tl, n) for n, _ in outs] + [_const((1, n)) for n in accs],
        out_shape=[jax.ShapeDtypeStruct((l, n), dt) for n, dt in outs]
        + [jax.ShapeDtypeStruct((1, n), F32) for n in accs],
        compiler_params=_params(("arbitrary",)),
    )(dhn2, x1, dx2, z, gl, a, sm, y1, wba, wbs, wglu, bglu, wout, gpost, gpre)


def _proj_bwd(x, dx1, cq, ckv, dq, dk, dv, du, dgl, g1, win, gq, wuq, gkv, wukv, rc, rs, tl):
    l = x.shape[0]

    def body(x_ref, dx1_ref, cq_ref, ckv_ref, dq_ref, dk_ref, dv_ref, du_ref, dgl_ref,
             g1_ref, win_ref, gq_ref, wuq_ref, gkv_ref, wukv_ref, rc_ref, rs_ref,
             gx_ref, dql_ref, qn_ref, ckvn_ref, dproj_ref, dg1_ref, dgq_ref, dgkv_ref):
        @pl.when(pl.program_id(0) == 0)
        def _():
            for ref in (dg1_ref, dgq_ref, dgkv_ref):
                ref[...] = jnp.zeros_like(ref)

        c1 = rc_ref[...]
        s1 = rs_ref[...]
        dql = _rope_bwd(dq_ref[...], jnp.tile(c1, (1, N_HEADS)), jnp.tile(s1, (1, N_HEADS))).astype(BF16)
        dql_ref[...] = dql
        dqn = _dot_nt(dql, wuq_ref[...])
        cq = cq_ref[...]
        qn, _ = _rms(cq, gq_ref[...])
        qn_ref[...] = qn.astype(BF16)
        dcq, dgq = _rms_bwd(dqn, cq, gq_ref[...])
        dgq_ref[...] += dgq
        dkb = dk_ref[...]
        dvb = dv_ref[...]
        dkf = dkb.astype(F32)
        dkr = dkf[:, 0:HEAD_SLOT]
        for h in range(1, N_HEADS):
            dkr = dkr + dkf[:, h * HEAD_SLOT:(h + 1) * HEAD_SLOT]
        dkr = _rope_bwd(dkr, c1, s1)
        dckvn = _dot_nt(dkb, wukv_ref[:, :HP]) + _dot_nt(dvb, wukv_ref[:, HP:])
        ckv = ckv_ref[...]
        ckvn, _ = _rms(ckv, gkv_ref[...])
        ckvn_ref[...] = ckvn.astype(BF16)
        dckv, dgkv = _rms_bwd(dckvn, ckv, gkv_ref[...])
        dgkv_ref[...] += dgkv
        dproj_ref[:, P_CQ:P_CKV] = dcq.astype(BF16)
        dproj_ref[:, P_CKV:P_KR] = dckv.astype(BF16)
        dproj_ref[:, P_KR:P_U] = dkr.astype(BF16)
        dproj_ref[:, P_U:P_GL] = du_ref[...].astype(BF16)
        dproj_ref[:, P_GL:P_END] = dgl_ref[...]
        dhn = _dot_nt(dproj_ref[...], win_ref[...])
        dxa, dg1 = _rms_bwd(dhn, x_ref[...], g1_ref[...])
        dg1_ref[...] += dg1
        gx_ref[...] = dx1_ref[...] + dxa

    outs = [(D_MODEL, F32), (HP, BF16), (Q_RANK, BF16), (KV_RANK, BF16), (P_END, BF16)]
    accs = [D_MODEL, Q_RANK, KV_RANK]
    return pl.pallas_call(
        body, name="proj_bwd", grid=(l // tl,),
        in_specs=[_row(tl, D_MODEL), _row(tl, D_MODEL), _row(tl, Q_RANK), _row(tl, KV_RANK), _row(tl, HP),
                  _row(tl, HP), _row(tl, HP), _row(tl, SSM_W), _row(tl, 2 * D_MODEL),
                  _const((1, D_MODEL)), _const((D_MODEL, P_END)), _const((1, Q_RANK)), _const((Q_RANK, HP)),
                  _const((1, KV_RANK)), _const((KV_RANK, 2 * HP)), _row(tl, HEAD_SLOT), _row(tl, HEAD_SLOT)],
        out_specs=[_row(tl, n) for n, _ in outs] + [_const((1, n)) for n in accs],
        out_shape=[jax.ShapeDtypeStruct((l, n), dt) for n, dt in outs]
        + [jax.ShapeDtypeStruct((1, n), F32) for n in accs],
        compiler_params=_params(("arbitrary",)),
    )(x, dx1, cq, ckv, dq, dk, dv, du, dgl, g1, win, gq, wuq, gkv, wukv, rc, rs)


CONV_CB = 256
CONV_NB = D_FF // CONV_CB
CONV_ROWS = 16


def _conv3(h, halo, w, b):
    return b + w[0:1, :] * _shift_down(h, 2, halo) + w[1:2, :] * _shift_down(h, 1, halo) + w[2:3, :] * h


def _conv_fwd(h, cw, cb, tl):
    l = h.shape[0]

    def body(hg_ref, hv_ref, wg_ref, wv_ref, bg_ref, bv_ref, act_ref, halo_ref):
        @pl.when(pl.program_id(1) == 0)
        def _():
            halo_ref[...] = jnp.zeros_like(halo_ref)

        hg = hg_ref[...]
        hv = hv_ref[...]
        cg = _conv3(hg, halo_ref[0:SUBLANES, :], wg_ref[...], bg_ref[...])
        cv = _conv3(hv, halo_ref[SUBLANES:, :], wv_ref[...], bv_ref[...])
        act_ref[...] = (_gelu(cg) * cv).astype(BF16)
        halo_ref[0:SUBLANES, :] = hg[tl - SUBLANES:, :]
        halo_ref[SUBLANES:, :] = hv[tl - SUBLANES:, :]

    gmap = lambda c, r: (r, c)
    vmap = lambda c, r: (r, CONV_NB + c)
    return pl.pallas_call(
        body, name="conv_fwd", grid=(CONV_NB, l // tl),
        in_specs=[pl.BlockSpec((tl, CONV_CB), gmap), pl.BlockSpec((tl, CONV_CB), vmap),
                  pl.BlockSpec((3, CONV_CB), lambda c, r: (0, c)), pl.BlockSpec((3, CONV_CB), lambda c, r: (0, CONV_NB + c)),
                  pl.BlockSpec((1, CONV_CB), lambda c, r: (0, c)), pl.BlockSpec((1, CONV_CB), lambda c, r: (0, CONV_NB + c))],
        out_specs=pl.BlockSpec((tl, CONV_CB), gmap),
        out_shape=jax.ShapeDtypeStruct((l, D_FF), BF16),
        scratch_shapes=[pltpu.VMEM((2 * SUBLANES, CONV_CB), F32)],
        compiler_params=_params(("parallel", "arbitrary")),
    )(h, h, cw, cw, cb, cb)


def _conv_bwd(h, dact, cw, cb, tl):
    l = h.shape[0]
    nr = l // tl
    tpb = tl // SUBLANES

    def body(hg_ref, hv_ref, hgh_ref, hvh_ref, da_ref, wg_ref, wv_ref, bg_ref, bv_ref,
             dh_ref, dwg_ref, dwv_ref, dbg_ref, dbv_ref, car_ref):
        r = pl.program_id(1)

        @pl.when(r == 0)
        def _():
            for ref in (car_ref, dwg_ref, dwv_ref, dbg_ref, dbv_ref):
                ref[...] = jnp.zeros_like(ref)

        keep = jnp.where(r == nr - 1, 0.0, 1.0)
        wg, wv, bg, bv = wg_ref[...], wv_ref[...], bg_ref[...], bv_ref[...]
        nch = tl // CONV_ROWS

        def fold(x):
            s = x[0:SUBLANES, :]
            for k in range(1, CONV_ROWS // SUBLANES):
                s = s + x[k * SUBLANES:(k + 1) * SUBLANES, :]
            return s

        def chunk(n, carry):
            ncg, ncv, acc = carry
            idx = nch - 1 - n
            r0 = pl.multiple_of(idx * CONV_ROWS, CONV_ROWS)
            rows = pl.ds(r0, CONV_ROWS)
            before = pl.ds(pl.multiple_of(jnp.maximum(r0 - SUBLANES, 0), SUBLANES), SUBLANES)
            in_tile = idx > 0
            da = da_ref[rows, :].astype(F32)

            def half(h_ref, halo_ref, w, b):
                hh = h_ref[rows, :]
                prev = jnp.where(in_tile, h_ref[before, :], halo_ref[...] * keep)
                h1 = _shift_down(hh, 1, prev)
                h2 = _shift_down(hh, 2, prev)
                return hh, h1, h2, b + w[0:1, :] * h2 + w[1:2, :] * h1 + w[2:3, :] * hh

            hg, hg1, hg2, cg = half(hg_ref, hgh_ref, wg, bg)
            hv, hv1, hv2, cv = half(hv_ref, hvh_ref, wv, bv)
            dcg = da * cv * _gelu_grad(cg)
            dcv = da * _gelu(cg)

            def back(dc, hh, h1, h2, w, nxt, part):
                dh = w[2:3, :] * dc + w[1:2, :] * _shift_up(dc, 1, nxt) + w[0:1, :] * _shift_up(dc, 2, nxt)
                dh_ref[part, rows, :] = dh.astype(BF16)
                return [fold(dc * h2), fold(dc * h1), fold(dc * hh), fold(dc)]

            sums = back(dcg, hg, hg1, hg2, wg, ncg, 0) + back(dcv, hv, hv1, hv2, wv, ncv, 1)
            return dcg[0:SUBLANES, :], dcv[0:SUBLANES, :], [a + s for a, s in zip(acc, sums)]

        zero = jnp.zeros((SUBLANES, CONV_CB), F32)
        ncg, ncv, acc = lax.fori_loop(0, nch, chunk, (car_ref[0:SUBLANES, :], car_ref[SUBLANES:, :], [zero] * 8))
        car_ref[0:SUBLANES, :] = ncg
        car_ref[SUBLANES:, :] = ncv
        for half_acc, dw_ref, db_ref in ((acc[0:4], dwg_ref, dbg_ref), (acc[4:8], dwv_ref, dbv_ref)):
            for k in range(3):
                dw_ref[k:k + 1, :] += jnp.sum(half_acc[k], axis=0, keepdims=True)
            db_ref[...] += jnp.sum(half_acc[3], axis=0, keepdims=True)

    grev = lambda c, r: (nr - 1 - r, c)
    vrev = lambda c, r: (nr - 1 - r, CONV_NB + c)
    ghalo = lambda c, r: (jnp.maximum((nr - 1 - r) * tpb - 1, 0), c)
    vhalo = lambda c, r: (jnp.maximum((nr - 1 - r) * tpb - 1, 0), CONV_NB + c)
    colg = lambda c, r: (0, c)
    colv = lambda c, r: (0, CONV_NB + c)
    return pl.pallas_call(
        body, name="conv_bwd", grid=(CONV_NB, nr),
        in_specs=[pl.BlockSpec((tl, CONV_CB), grev), pl.BlockSpec((tl, CONV_CB), vrev),
                  pl.BlockSpec((SUBLANES, CONV_CB), ghalo), pl.BlockSpec((SUBLANES, CONV_CB), vhalo),
                  pl.BlockSpec((tl, CONV_CB), grev),
                  pl.BlockSpec((3, CONV_CB), colg), pl.BlockSpec((3, CONV_CB), colv),
                  pl.BlockSpec((1, CONV_CB), colg), pl.BlockSpec((1, CONV_CB), colv)],
        out_specs=[pl.BlockSpec((2, tl, CONV_CB), lambda c, r: (0, nr - 1 - r, c)),
                   pl.BlockSpec((3, CONV_CB), colg), pl.BlockSpec((3, CONV_CB), colg),
                   pl.BlockSpec((1, CONV_CB), colg), pl.BlockSpec((1, CONV_CB), colg)],
        out_shape=[jax.ShapeDtypeStruct((2, l, D_FF), BF16),
                   jax.ShapeDtypeStruct((3, D_FF), F32), jax.ShapeDtypeStruct((3, D_FF), F32),
                   jax.ShapeDtypeStruct((1, D_FF), F32), jax.ShapeDtypeStruct((1, D_FF), F32)],
        scratch_shapes=[pltpu.VMEM((2 * SUBLANES, CONV_CB), F32)],
        compiler_params=_params(("parallel", "arbitrary")),
    )(h, h, h, h, dact, cw, cw, cb, cb)


def _loss_head(ff, x1, tgt, g, tl):
    l = ff.shape[0]

    def body(ff_ref, x1_ref, tg_ref, g_ref, loss_ref, dx2_ref, dff_ref, dg_ref):
        @pl.when(pl.program_id(0) == 0)
        def _():
            loss_ref[...] = jnp.zeros_like(loss_ref)
            dg_ref[...] = jnp.zeros_like(dg_ref)

        f = ff_ref[...]
        gv = g_ref[...]
        n, _ = _rms(f, gv)
        e = x1_ref[...] + n - tg_ref[...]
        loss_ref[...] += 0.5 * jnp.sum(jnp.mean(e * e, axis=-1, keepdims=True), axis=0, keepdims=True)
        dx2 = e * (1.0 / D_MODEL)
        dx2_ref[...] = dx2
        dff, dg = _rms_bwd(dx2, f, gv)
        dff_ref[...] = dff.astype(BF16)
        dg_ref[...] += dg

    return pl.pallas_call(
        body, name="loss_head", grid=(l // tl,),
        in_specs=[_row(tl, D_MODEL), _row(tl, D_MODEL), _row(tl, D_MODEL), _const((1, D_MODEL))],
        out_specs=[_const((1, LANES)), _row(tl, D_MODEL), _row(tl, D_MODEL), _const((1, D_MODEL))],
        out_shape=[jax.ShapeDtypeStruct((1, LANES), F32), jax.ShapeDtypeStruct((l, D_MODEL), F32),
                   jax.ShapeDtypeStruct((l, D_MODEL), BF16), jax.ShapeDtypeStruct((1, D_MODEL), F32)],
        compiler_params=_params(("arbitrary",)),
    )(ff, x1, tgt, g)


def _ssm_disc(lam_re, lam_im, log_dt, b_re, b_im):
    dt = jnp.exp(log_dt)[:, None]
    mag = jnp.exp(lam_re * dt)
    ang = lam_im * dt
    a_re, a_im = mag * jnp.cos(ang), mag * jnp.sin(ang)
    den = lam_re * lam_re + lam_im * lam_im
    n_re, n_im = a_re - 1.0, a_im
    z_re = (n_re * lam_re + n_im * lam_im) / den
    z_im = (n_im * lam_re - n_re * lam_im) / den
    bb_re = z_re[..., None] * b_re - z_im[..., None] * b_im
    bb_im = z_re[..., None] * b_im + z_im[..., None] * b_re
    return a_re, a_im, bb_re, bb_im


_GPB = SSM_CB // SSM_P


def _embed_b(bb):
    t = bb.transpose(0, 2, 1).reshape(SSM_NB, _GPB, SSM_H, SSM_P)
    return jnp.einsum('mjhp,jk->mjhkp', t, jnp.eye(_GPB, dtype=bb.dtype)).reshape(SSM_NB, SSM_UB, SSM_CB)


def _extract_b(d):
    t = d.reshape(SSM_NB, _GPB, SSM_H, _GPB, SSM_P)
    t = jnp.einsum('mjhkp,jk->mjhp', t, jnp.eye(_GPB, dtype=d.dtype))
    return t.reshape(SSM_G, SSM_H, SSM_P).transpose(0, 2, 1)


def _embed_c(c):
    t = c.transpose(0, 2, 1).reshape(SSM_NB, _GPB, SSM_P, SSM_H)
    return jnp.einsum('mjph,jk->mjpkh', t, jnp.eye(_GPB, dtype=c.dtype)).reshape(SSM_NB, SSM_CB, SSM_UB)


def _extract_c(d):
    t = d.reshape(SSM_NB, _GPB, SSM_P, _GPB, SSM_H)
    t = jnp.einsum('mjpkh,jk->mjph', t, jnp.eye(_GPB, dtype=d.dtype))
    return t.reshape(SSM_G, SSM_P, SSM_H).transpose(0, 2, 1)


def _scan_tables(a_re, a_im, reverse):
    ar = a_re.reshape(1, SSM_CH)
    ai = (-a_im if reverse else a_im).reshape(1, SSM_CH)
    pr, pi = [ar], [ai]
    for _ in range(SUBLANES - 1):
        pr, pi = pr + [pr[-1] * ar - pi[-1] * ai], pi + [pr[-1] * ai + pi[-1] * ar]
    rows = jnp.arange(SUBLANES)[:, None]
    out = []
    for k in (1, 2, 4):
        valid = (rows + k <= SUBLANES - 1) if reverse else (rows >= k)
        out += [jnp.where(valid, pr[k - 1], 0.0), jnp.where(valid, pi[k - 1], 0.0)]
    order = list(range(SUBLANES - 1, -1, -1)) if reverse else list(range(SUBLANES))
    out += [jnp.concatenate([pr[n] for n in order], axis=0), jnp.concatenate([pi[n] for n in order], axis=0)]
    return jnp.stack(out).astype(F32)


def _pad_heads(w, d):
    lead = w.shape[:-1]
    w = w.reshape(lead + (N_HEADS, d))
    w = jnp.pad(w, [(0, 0)] * len(lead) + [(0, 0), (0, HEAD_SLOT - d)])
    return w.reshape(lead + (HP,))


def _unpad_heads(w, d):
    lead = w.shape[:-1]
    return w.reshape(lead + (N_HEADS, HEAD_SLOT))[..., :d].reshape(lead + (N_HEADS * d,))


def _chip_major(w, axis):
    k, n = w.shape
    if axis == 0:
        return w.reshape(N_CHIPS, k // N_CHIPS, n)
    return w.reshape(k, N_CHIPS, n // N_CHIPS).transpose(1, 0, 2)


def _from_chip_major(w, axis):
    if axis == 0:
        return w.reshape(-1, w.shape[2])
    return w.transpose(1, 0, 2).reshape(w.shape[1], -1)


def _pad_w_in(w):
    z = lambda n: jnp.zeros((w.shape[0], n), w.dtype)
    return jnp.concatenate([w[:, :640], z(KR_LANE), w[:, 640:672], z(HEAD_SLOT - KR_LANE - QK_ROPE), w[:, 672:]], axis=1)


def _unpad_w_in(w):
    return jnp.concatenate([w[:, :640], w[:, P_KR + KR_LANE:P_KR + KR_LANE + QK_ROPE], w[:, P_U:]], axis=1)


def _local_step(x, positions, tgt, wts, sp):
    l = x.shape[0]
    tl = min(256, l)
    ta = min(512, l)
    ts = min(1024, l)
    tc = min(2048, l)

    inv_freq = ROPE_THETA ** (-jnp.arange(0, QK_ROPE, 2, dtype=F32) / QK_ROPE)
    ang = positions.astype(F32)[:, None] * inv_freq
    cos, sin = jnp.cos(ang), jnp.sin(ang)
    one = jnp.ones((l, KR_LANE), F32)
    rc = jnp.concatenate([one, cos, cos, jnp.ones((l, HEAD_SLOT - KR_LANE - QK_ROPE), F32)], axis=1)
    rs = jnp.concatenate([0 * one, -sin, sin, jnp.zeros((l, HEAD_SLOT - KR_LANE - QK_ROPE), F32)], axis=1)

    win = _pad_w_in(wts["w_in"])
    wuq = _pad_heads(wts["w_uq"], QK_HEAD)
    wukv = jnp.concatenate([_pad_heads(wts["w_uk"], QK_NOPE), _pad_heads(wts["w_uv"], V_HEAD)], axis=1)

    disc_in = (sp["ssm_lambda_re"], sp["ssm_lambda_im"], sp["ssm_log_dt"], sp["ssm_b_re"], sp["ssm_b_im"])
    (a_re, a_im, bb_re, bb_im), disc_vjp = jax.vjp(_ssm_disc, *disc_in)
    bre, bim = _embed_b(bb_re).astype(BF16), _embed_b(bb_im).astype(BF16)
    cre, cim = _embed_c(sp["ssm_c_re"]).astype(BF16), _embed_c(sp["ssm_c_im"]).astype(BF16)
    dvec = sp["ssm_d"].reshape(1, SSM_W)
    tab_f = _scan_tables(a_re, a_im, False)
    tab_r = _scan_tables(a_re, a_im, True)

    g1, gq, gkv = sp["mix_norm_pre"], sp["q_norm"], sp["kv_norm"]
    gpost, gpre, gfin = sp["mix_norm_post"], sp["ffn_norm_pre"], sp["ffn_norm_post"]
    bgate, bglu, convb = sp["b_gate"], sp["b_glu"], sp["conv_b"]

    hn, cq, ckv, q, k, v, u, gl = _proj_fwd(x, g1, win, gq, wuq, gkv, wukv, rc, rs, bgate, tl)
    attn, lse = _attn_fwd(q, k, v, ta)
    y1, sre, sim = _ssm_fwd(u, bre, bim, cre, cim, dvec, tab_f, ts)
    wba = jnp.pad(wts["w_branch_attn"].reshape(N_HEADS, V_HEAD, D_MODEL),
                  ((0, 0), (0, HEAD_SLOT - V_HEAD), (0, 0))).reshape(HP, D_MODEL)
    wbs, wglu, wout = wts["w_branch_ssm"], wts["w_glu"], wts["w_out"]
    a, sm, merged, z, x1, hn2, y3 = _merge_fwd(x, gl, attn, y1, wba, wbs, wglu, bglu, wout, gpost, gpre, tl)
    late = wts["late"](x1)
    wup, wdown, convw = late["w_up"], late["w_down"], late["conv_w"]
    h = _mm(hn2, wup, "ffn_up")
    act = _conv_fwd(h, convw, convb, tc)
    ff = _mm(act, wdown, "ffn_down")
    loss, dx2, dff, dgfin = _loss_head(ff, x1, tgt, gfin, tl)

    dact = _mm(dff, wdown, "ffn_down_dx", out_dtype=BF16, bt=True)
    d_wdown = _mm_tn(act, dff, "ffn_down_dw", tk_cap=D_FF // 2)
    dh, dwg, dwv, dbg, dbv = _conv_bwd(h, dact, convw, convb, tc)
    d_convw = jnp.concatenate([dwg, dwv], axis=1)
    d_convb = jnp.concatenate([dbg, dbv], axis=1)
    dhn2 = _mm(dh, wup, "ffn_up_dx", bt=True)
    d_wup = _mm_tn(hn2, dh, "ffn_up_dw", chips=True)
    behind = wts["send_grads"]("ffn", {"w_up": d_wup, "w_down": _chip_major(d_wdown, 0)})
    (dx1, dz, dbra, dbrs, dgl, dattn, dy1, dt, y2, dgpre, dgpost, dbgate, dbglu) = _merge_bwd(
        dhn2, x1, dx2, z, gl, a, sm, y1, wba, wbs, wglu, bglu, wout, gpost, gpre + behind, tl)
    d_wout = _mm_tn(merged, dz, "w_out_dw")
    d_wba = _mm_tn(attn, dbra, "w_branch_attn_dw", chips=True)
    d_wbs = _mm_tn(y3, dbrs, "w_branch_ssm_dw", chips=True)
    d_wglu = _mm_tn(y2, dt, "w_glu_dw")
    ncol = D_MODEL // N_CHIPS
    behind = wts["send_grads"]("mix", {
        "w_glu": _chip_major(d_wglu, 0),
        "w_branch_attn": d_wba.reshape(N_CHIPS, N_HEADS, HEAD_SLOT, ncol)[:, :, :V_HEAD].reshape(
            N_CHIPS, N_HEADS * V_HEAD, ncol),
        "w_branch_ssm": d_wbs,
        "w_out": _chip_major(d_wout, 0)}, after=d_wglu)
    dq, dk, dv = _attn_bwd(q, k, v, dattn, lse + behind, _attn_delta(attn, dattn, min(2048, l)), ta)
    du, dbre, dbim, dcre, dcim, dare, daim, dd = _ssm_bwd(dy1, u, sre, sim, bre, bim, cre, cim, dvec, tab_r, ts)
    behind = wts["send_grads"]("none", {}, after=du)
    gx, dql, qn, ckvn, dproj, dg1, dgq, dgkv = _proj_bwd(
        x, dx1, cq, ckv, dq, dk, dv, du, dgl, g1 + behind, win, gq, wuq, gkv, wukv, rc, rs, tl)
    d_win = _mm_tn(hn, dproj, "w_in_dw")
    d_wuq = _mm_tn(qn, dql, "w_uq_dw")
    d_wuk = _mm_tn(ckvn, dk, "w_uk_dw")
    d_wuv = _mm_tn(ckvn, dv, "w_uv_dw")

    d_lre, d_lim, d_ldt, d_bre, d_bim = disc_vjp((dare.reshape(SSM_G, SSM_P), daim.reshape(SSM_G, SSM_P),
                                                  _extract_b(dbre), _extract_b(dbim)))
    big = {
        "w_in": _chip_major(_unpad_w_in(d_win), 1),
        "w_uq": _chip_major(_unpad_heads(d_wuq, QK_HEAD), 1),
        "w_uk": _chip_major(_unpad_heads(d_wuk, QK_NOPE), 1),
        "w_uv": _chip_major(_unpad_heads(d_wuv, V_HEAD), 1),
    }
    small = {
        "conv_w": d_convw,
        "mix_norm_pre": dg1, "q_norm": dgq, "kv_norm": dgkv,
        "ssm_lambda_re": d_lre, "ssm_lambda_im": d_lim, "ssm_log_dt": d_ldt,
        "ssm_b_re": d_bre, "ssm_b_im": d_bim,
        "ssm_c_re": _extract_c(dcre), "ssm_c_im": _extract_c(dcim),
        "ssm_d": dd.reshape(SSM_G, SSM_H), "b_glu": dbglu, "b_gate": dbgate,
        "mix_norm_post": dgpost, "ffn_norm_pre": dgpre, "conv_b": d_convb, "ffn_norm_post": dgfin,
    }
    return loss[0, 0], gx, big, small


_ANY = pl.BlockSpec(memory_space=pl.ANY)


ROW_TILE = 16


def _place():
    x, y, c = lax.axis_index("x"), lax.axis_index("y"), lax.axis_index("c")
    return x, y, c, 2 * x + y, [(1 - x, y), (x, 1 - y), (1 - x, 1 - y)]


def _half(rows, which):
    hr = rows // 2
    return pl.ds(pl.multiple_of(which * hr, ROW_TILE), hr)


def _remote(src, dst, send_sems, recv_sems, n, dev):
    return pltpu.make_async_remote_copy(src_ref=src, dst_ref=dst, send_sem=send_sems.at[n], recv_sem=recv_sems.at[n],
                                        device_id=dev, device_id_type=MESH)


def _gather_big(shards):
    nw = len(shards)
    rows = [s.shape[0] for s in shards]

    def body(*refs):
        ins, outs = refs[:nw], refs[nw:2 * nw]
        ici_send, ici_recv, d2d_send, d2d_recv = refs[2 * nw:]
        x, y, c, me, peers = _place()
        sent = []
        for i in range(nw):
            for p, (px, py) in enumerate(peers):
                cp = _remote(ins[i].at[_half(rows[i], c)], outs[i].at[me, _half(rows[i], c)], ici_send, ici_recv,
                             3 * i + p, (px, py, c))
                cp.start()
                sent.append(cp)
        for p, (px, py) in enumerate(peers):
            for i in range(nw):
                blk = outs[i].at[2 * px + py, _half(rows[i], c)]
                _remote(blk, blk, ici_send, ici_recv, 3 * i + p, (px, py, c)).wait_recv()
                cp = _remote(blk, blk, d2d_send, d2d_recv, 3 * i + p, (x, y, 1 - c))
                cp.start()
                sent.append(cp)
        for p, (px, py) in enumerate(peers):
            for i in range(nw):
                blk = outs[i].at[2 * px + py, _half(rows[i], 1 - c)]
                _remote(blk, blk, d2d_send, d2d_recv, 3 * i + p, (x, y, 1 - c)).wait_recv()
        for cp in sent:
            cp.wait_send()

    dma = pltpu.SemaphoreType.DMA
    return pl.pallas_call(
        body, name="gather_weights", in_specs=[_ANY] * nw, out_specs=[_ANY] * nw,
        out_shape=[jax.ShapeDtypeStruct((N_CHIPS,) + s.shape, s.dtype) for s in shards],
        scratch_shapes=[dma((3 * nw,)), dma((3 * nw,)), dma((3 * nw,)), dma((3 * nw,))],
    )(*shards)


_HBM = pl.BlockSpec(memory_space=pltpu.HBM)
_SEM = pl.BlockSpec(memory_space=pltpu.SEMAPHORE)
_DATAFLOW = pltpu.SideEffectType.DATAFLOW_SIDE_EFFECTING


def _exchange_start(shards, name, scatter):
    nw = len(shards)
    lands = [lax.empty(s.shape if scatter else (N_CHIPS,) + s.shape, s.dtype) for s in shards]

    def body(*refs):
        ins, zones = refs[:nw], refs[nw:2 * nw]
        send_sems, recv_sems, token = refs[2 * nw], refs[2 * nw + 1], refs[-1]
        x, y, c, me, peers = _place()
        for i in range(nw):
            for p, (px, py) in enumerate(peers):
                src = ins[i].at[2 * px + py] if scatter else ins[i]
                _remote(src, zones[i].at[me], send_sems, recv_sems, 3 * i + p, (px, py, c)).start()
        token[...] = jnp.zeros_like(token)

    thru = [pltpu.HBM(a.shape, a.dtype) for a in list(shards) + lands]
    dma = pltpu.SemaphoreType.DMA
    outs = pl.pallas_call(
        body, name=name,
        out_shape=(dma((3 * nw,)), dma((3 * nw,)), *thru, jax.ShapeDtypeStruct((SUBLANES, LANES), F32)),
        in_specs=[_HBM] * (2 * nw),
        out_specs=(_SEM, _SEM, *([_HBM] * (2 * nw)), pl.BlockSpec(memory_space=pltpu.VMEM)),
        input_output_aliases={i: 2 + i for i in range(2 * nw)},
        compiler_params=pltpu.CompilerParams(has_side_effects=_DATAFLOW),
    )(*[pltpu.with_memory_space_constraint(a, pltpu.HBM) for a in list(shards) + lands])
    return outs[0], outs[1], list(outs[2:2 + nw]), list(outs[2 + nw:2 + 2 * nw]), outs[-1]


def _exchange_wait(send_sems, recv_sems, shards, lands, after, name, scatter):
    nw = len(shards)

    def body(*refs):
        ins, zones = refs[:nw], refs[nw:2 * nw]
        send_sems, recv_sems = refs[2 * nw], refs[2 * nw + 1]
        x, y, c, me, peers = _place()
        for i in range(nw):
            for p, (px, py) in enumerate(peers):
                src = ins[i].at[2 * px + py] if scatter else ins[i]
                cp = _remote(src, zones[i].at[2 * px + py], send_sems, recv_sems, 3 * i + p, (px, py, c))
                cp.wait_send()
                cp.wait_recv()

    both = list(shards) + list(lands)
    outs = pl.pallas_call(
        body, name=name,
        out_shape=tuple(pltpu.HBM(a.shape, a.dtype) for a in both),
        in_specs=(*([_HBM] * (2 * nw)), _SEM, _SEM, _ANY), out_specs=[_HBM] * (2 * nw),
        input_output_aliases={i: i for i in range(2 * nw)},
        compiler_params=pltpu.CompilerParams(has_side_effects=_DATAFLOW),
    )(*both, send_sems, recv_sems, after)
    return list(outs[:nw]), list(outs[nw:])


def _sibling_start(grads, name):
    nw = len(grads)
    lands = [lax.empty((N_CHIPS, g.shape[1] // 2, g.shape[2]), g.dtype) for g in grads]

    def body(*refs):
        ins, zones = refs[:nw], refs[nw:2 * nw]
        send_sems, recv_sems, token = refs[2 * nw], refs[2 * nw + 1], refs[-1]
        x, y, c, _, _ = _place()
        for i in range(nw):
            _remote(ins[i].at[pl.ds(0, N_CHIPS), _half(grads[i].shape[1], 1 - c)], zones[i], send_sems, recv_sems,
                    i, (x, y, 1 - c)).start()
        token[...] = jnp.zeros_like(token)

    thru = [pltpu.HBM(a.shape, a.dtype) for a in list(grads) + lands]
    dma = pltpu.SemaphoreType.DMA
    outs = pl.pallas_call(
        body, name=name,
        out_shape=(dma((nw,)), dma((nw,)), *thru, jax.ShapeDtypeStruct((SUBLANES, LANES), F32)),
        in_specs=[_HBM] * (2 * nw),
        out_specs=(_SEM, _SEM, *([_HBM] * (2 * nw)), pl.BlockSpec(memory_space=pltpu.VMEM)),
        input_output_aliases={i: 2 + i for i in range(2 * nw)},
        compiler_params=pltpu.CompilerParams(has_side_effects=_DATAFLOW),
    )(*[pltpu.with_memory_space_constraint(a, pltpu.HBM) for a in list(grads) + lands])
    return outs[0], outs[1], list(outs[2:2 + nw]), list(outs[2 + nw:2 + 2 * nw]), outs[-1]


def _sibling_wait(send_sems, recv_sems, grads, lands, after, name):
    nw = len(grads)

    def body(*refs):
        ins, zones = refs[:nw], refs[nw:2 * nw]
        send_sems, recv_sems = refs[2 * nw], refs[2 * nw + 1]
        x, y, c, _, _ = _place()
        for i in range(nw):
            cp = _remote(ins[i].at[pl.ds(0, N_CHIPS), _half(grads[i].shape[1], 1 - c)], zones[i], send_sems, recv_sems,
                         i, (x, y, 1 - c))
            cp.wait_send()
            cp.wait_recv()

    both = list(grads) + list(lands)
    outs = pl.pallas_call(
        body, name=name,
        out_shape=tuple(pltpu.HBM(a.shape, a.dtype) for a in both),
        in_specs=(*([_HBM] * (2 * nw)), _SEM, _SEM, _ANY), out_specs=[_HBM] * (2 * nw),
        input_output_aliases={i: i for i in range(2 * nw)},
        compiler_params=pltpu.CompilerParams(has_side_effects=_DATAFLOW),
    )(*both, send_sems, recv_sems, after)
    return list(outs[:nw]), list(outs[nw:])


def _reduce_to_sibling(grads, name):
    nw = len(grads)

    def body(*refs):
        ins, outs = refs[:nw], refs[nw:2 * nw]
        send_sems, recv_sems = refs[2 * nw:]
        x, y, c, _, _ = _place()
        sent = []
        for i in range(nw):
            cp = _remote(ins[i].at[pl.ds(0, N_CHIPS), _half(grads[i].shape[1], 1 - c)], outs[i], send_sems, recv_sems,
                         i, (x, y, 1 - c))
            cp.start()
            sent.append(cp)
        for cp in sent:
            cp.wait()

    dma = pltpu.SemaphoreType.DMA
    return pl.pallas_call(
        body, name=name, in_specs=[_ANY] * nw, out_specs=[_ANY] * nw,
        out_shape=[jax.ShapeDtypeStruct((N_CHIPS, g.shape[1] // 2, g.shape[2]), g.dtype) for g in grads],
        scratch_shapes=[dma((nw,)), dma((nw,))],
    )(*grads)


def _reduce_back(totals, name):
    nw = len(totals)

    def body(*refs):
        outs = refs[nw:2 * nw]
        send_sems, recv_sems = refs[2 * nw:]
        x, y, c, _, _ = _place()
        sent = []
        for i in range(nw):
            blk = outs[i].at[_half(totals[i].shape[0], c)]
            cp = _remote(blk, blk, send_sems, recv_sems, i, (x, y, 1 - c))
            cp.start()
            sent.append(cp)
        for i in range(nw):
            blk = outs[i].at[_half(totals[i].shape[0], 1 - c)]
            _remote(blk, blk, send_sems, recv_sems, i, (x, y, 1 - c)).wait_recv()
        for cp in sent:
            cp.wait_send()

    dma = pltpu.SemaphoreType.DMA
    return pl.pallas_call(
        body, name=name, in_specs=[_ANY] * nw, out_specs=[_ANY] * nw,
        out_shape=[jax.ShapeDtypeStruct(t.shape, t.dtype) for t in totals],
        input_output_aliases={i: i for i in range(nw)},
        scratch_shapes=[dma((nw,)), dma((nw,))],
    )(*totals)


def _all_reduce_small(v, name):
    rows, w = v.shape
    hr = rows // 2
    assert hr % SUBLANES == 0

    def body(v_ref, out_ref, sib_ref, half_ref, chips_ref, send_sems, recv_sems):
        x, y, c, me, peers = _place()
        sibling = (x, y, 1 - c)
        mine = pl.ds(pl.multiple_of(c * hr, SUBLANES), hr)
        other = pl.ds(pl.multiple_of((1 - c) * hr, SUBLANES), hr)
        cp = _remote(v_ref, sib_ref, send_sems, recv_sems, 0, sibling)
        cp.start()
        cp.wait()
        half_ref[...] = v_ref[mine, :] + sib_ref[mine, :]
        sent = []
        for p, (px, py) in enumerate(peers):
            cp = _remote(half_ref, chips_ref.at[me], send_sems, recv_sems, 1 + p, (px, py, c))
            cp.start()
            sent.append(cp)
        chips_ref[me] = half_ref[...]
        for p, (px, py) in enumerate(peers):
            _remote(half_ref, chips_ref.at[2 * px + py], send_sems, recv_sems, 1 + p, (px, py, c)).wait_recv()
        for cp in sent:
            cp.wait_send()
        out_ref[mine, :] = ((chips_ref[0] + chips_ref[1]) + chips_ref[2]) + chips_ref[3]
        cp = _remote(out_ref.at[mine], out_ref.at[mine], send_sems, recv_sems, 4, sibling)
        cp.start()
        _remote(out_ref.at[other], out_ref.at[other], send_sems, recv_sems, 4, sibling).wait_recv()
        cp.wait_send()

    vm = pl.BlockSpec(memory_space=pltpu.VMEM)
    return pl.pallas_call(
        body, name=name, in_specs=[vm], out_specs=vm,
        out_shape=jax.ShapeDtypeStruct((rows, w), F32),
        scratch_shapes=[pltpu.VMEM((rows, w), F32), pltpu.VMEM((hr, w), F32), pltpu.VMEM((N_CHIPS, hr, w), F32),
                        pltpu.SemaphoreType.DMA((5,)), pltpu.SemaphoreType.DMA((5,))],
        compiler_params=pltpu.CompilerParams(vmem_limit_bytes=VMEM_LIMIT),
    )(v)


ELEMENTWISE_BLOCK = 512 * 1024


def _rows_tile(rows, cols):
    best = None
    for t in range(SUBLANES, rows + 1, SUBLANES):
        if rows % t == 0 and t * cols <= ELEMENTWISE_BLOCK:
            best = t
    return rows if best is None else best


def _add_pair(g, t, core, name):
    nb, n, w = t.shape
    tr = _rows_tile(n, w)
    steps = n // tr

    def body(core_ref, g_ref, t_ref, o_ref):
        o_ref[...] = (g_ref[...] + t_ref[...]).astype(BF16)

    spec = pl.BlockSpec((1, tr, w), lambda j, i, core_ref: (j, i, 0))
    return pl.pallas_call(
        body, name=name,
        grid_spec=pltpu.PrefetchScalarGridSpec(
            num_scalar_prefetch=1, grid=(nb, steps),
            in_specs=[pl.BlockSpec((1, tr, w), lambda j, i, core_ref: (j, core_ref[0] * steps + i, 0)), spec],
            out_specs=spec),
        out_shape=jax.ShapeDtypeStruct(t.shape, BF16),
        compiler_params=_params(("parallel", "parallel")))(core, g, t)


def _add_chips(landed, pairs, place, name):
    nb, n, w = landed.shape
    tr = _rows_tile(n, w)
    steps = n // tr

    def body(place_ref, r_ref, own_ref, o_ref):
        me = place_ref[0]
        acc = None
        for k in range(nb):
            blk = jnp.where(me == k, own_ref[0], r_ref[k]).astype(F32)
            acc = blk if acc is None else acc + blk
        o_ref[...] = acc

    return pl.pallas_call(
        body, name=name,
        grid_spec=pltpu.PrefetchScalarGridSpec(
            num_scalar_prefetch=1, grid=(steps,),
            in_specs=[pl.BlockSpec((nb, tr, w), lambda i, place_ref: (0, i, 0)),
                      pl.BlockSpec((1, tr, w), lambda i, place_ref: (place_ref[0], i, 0))],
            out_specs=pl.BlockSpec((tr, w), lambda i, place_ref: (place_ref[1] * steps + i, 0))),
        out_shape=jax.ShapeDtypeStruct((2 * n, w), F32),
        compiler_params=_params(("parallel",)))(place, landed, pairs)


def _adamw(w, g, m, v, name):
    rows, wd = w.shape
    tr = _rows_tile(rows, wd)
    c1 = 1.0 - ADAM_B1 ** ADAM_STEP
    c2 = 1.0 - ADAM_B2 ** ADAM_STEP

    def body(w_ref, g_ref, m_ref, v_ref, go_ref, d_ref, mo_ref, vo_ref):
        gv = g_ref[...]
        go_ref[...] = gv
        m2 = ADAM_B1 * m_ref[...] + (1.0 - ADAM_B1) * gv
        v2 = ADAM_B2 * v_ref[...] + (1.0 - ADAM_B2) * (gv * gv)
        mo_ref[...] = m2
        vo_ref[...] = v2
        d_ref[...] = -ADAM_LR * ((m2 / c1) / (jnp.sqrt(v2 / c2) + ADAM_EPS) + ADAM_WD * w_ref[...])

    spec = pl.BlockSpec((tr, wd), lambda i: (i, 0))
    shp = jax.ShapeDtypeStruct((rows, wd), F32)
    return pl.pallas_call(body, name=name, grid=(rows // tr,), in_specs=[spec] * 4, out_specs=[spec] * 4,
                          out_shape=[shp] * 4, compiler_params=_params(("parallel",)))(w, g, m, v)


BIG = [("w_in", (1024, 3232), 1), ("w_uq", (384, 768), 1), ("w_uk", (256, 512), 1), ("w_uv", (256, 512), 1),
       ("w_glu", (512, 512), 0), ("w_branch_attn", (512, 1024), 1), ("w_branch_ssm", (512, 1024), 1),
       ("w_out", (1024, 1024), 0), ("w_up", (1024, 5632), 1), ("conv_w", (3, 5632), 1), ("w_down", (2816, 1024), 0)]
SMALL = [("mix_norm_pre", (1024,)), ("q_norm", (384,)), ("kv_norm", (256,)), ("ssm_lambda_re", (32, 64)),
         ("ssm_lambda_im", (32, 64)), ("ssm_log_dt", (32,)), ("ssm_b_re", (32, 64, 16)), ("ssm_b_im", (32, 64, 16)),
         ("ssm_c_re", (32, 16, 64)), ("ssm_c_im", (32, 16, 64)), ("ssm_d", (32, 16)), ("b_glu", (512,)),
         ("b_gate", (2048,)), ("mix_norm_post", (1024,)), ("ffn_norm_pre", (1024,)), ("conv_b", (5632,)),
         ("ffn_norm_post", (1024,))]
MATMUL_W = [b for b in BIG if b[0] != "conv_w"]
LATE_W = ("w_up", "w_down", "conv_w")
CONV_W_SHAPE = (3, 2 * D_FF)
CONV_W_SHARD = (3, 2 * D_FF // N_CHIPS)
SMALL_SUM = [("loss", (1,))] + SMALL + [("conv_w", CONV_W_SHAPE)]
SMALL_ADAM = SMALL + [("conv_w", CONV_W_SHARD)]


def _pack_flat(layout, vals):
    flat = jnp.concatenate([vals[n].astype(F32).reshape(-1) for n, _ in layout])
    rows = -(-(-(-flat.shape[0] // FLAT_W)) // (2 * SUBLANES)) * 2 * SUBLANES
    return jnp.pad(flat, (0, rows * FLAT_W - flat.shape[0])).reshape(rows, FLAT_W)


def _unpack_flat(layout, flat):
    flat = flat.reshape(-1)
    out = {}
    o = 0
    for name, shape in layout:
        n = math.prod(shape)
        out[name] = flat[o:o + n].reshape(shape)
        o += n
    return out


_ARG_NAMES = ["x", "positions"] + [n for n in (
    "mix_norm_pre", "w_in", "q_norm", "w_uq", "kv_norm", "w_uk", "w_uv", "ssm_lambda_re", "ssm_lambda_im", "ssm_log_dt",
    "ssm_b_re", "ssm_b_im", "ssm_c_re", "ssm_c_im", "ssm_d", "w_glu", "b_glu", "w_branch_attn", "w_branch_ssm",
    "b_gate", "w_out", "mix_norm_post", "ffn_norm_pre", "w_up", "conv_w", "conv_b", "w_down", "ffn_norm_post")]
_WEIGHTS = _ARG_NAMES[2:]


def _gather_weights(w):
    early = [b for b in MATMUL_W if b[0] not in LATE_W]
    late = [b for b in BIG if b[0] in LATE_W]
    own = (jnp.arange(N_CHIPS) == 2 * lax.axis_index("x") + lax.axis_index("y"))[:, None, None]

    def whole(layout, mine, gathered):
        return {name: _from_chip_major(jnp.where(own, s[None], g), axis)
                for (name, _, axis), s, g in zip(layout, mine, gathered)}

    mine = [w[name].astype(BF16) for name, _, _ in early]
    gathered = _gather_big(mine)
    full = whole(early, mine, gathered)
    mine_late = [w[name].astype(F32 if name == "conv_w" else BF16) for name, _, _ in late]
    _, mine_late = lax.optimization_barrier((gathered[0], mine_late))
    send_sems, recv_sems, shards_thru, lands_thru, token = _exchange_start(mine_late, "gather_late_start", scatter=False)

    def late_weights(after):
        shards, lands = _exchange_wait(send_sems, recv_sems, shards_thru, lands_thru, after, "gather_late_wait",
                                       scatter=False)
        return whole(late, shards, lands)

    full["late"] = late_weights
    full["token"] = token[0, 0]
    return full


def _pair_sums(names, grads, tag):
    core = lax.axis_index("c").astype(jnp.int32).reshape(1)
    theirs = _reduce_to_sibling(grads, "reduce_grads_d2d" + tag)
    return [_add_pair(g, t, core, "reduce_pair_" + n) for n, g, t in zip(names, grads, theirs)]


def _send_grads(tag, grads, after, flying, pending):
    token = jnp.zeros((), F32)
    if flying:
        tag0, names0, state0 = flying.pop()
        core = lax.axis_index("c").astype(jnp.int32).reshape(1)
        mine, theirs = _sibling_wait(*state0, after, "reduce_" + tag0 + "_d2d_wait")
        pairs = [_add_pair(g, t, core, "reduce_pair_" + n) for n, g, t in zip(names0, mine, theirs)]
        send_sems, recv_sems, pairs_thru, lands_thru, tok = _exchange_start(pairs, "reduce_" + tag0 + "_start",
                                                                          scatter=True)
        pending.append((tag0, names0, send_sems, recv_sems, pairs_thru, lands_thru))
        token = token + tok[0, 0]
    if grads:
        names = list(grads)
        send_sems, recv_sems, grads_thru, lands_thru, tok = _sibling_start([grads[n] for n in names],
                                                                          "reduce_" + tag + "_d2d_start")
        flying.append((tag, names, (send_sems, recv_sems, grads_thru, lands_thru)))
        token = token + tok[0, 0]
    return token


def _reduce_grads(gbig, pending, loss, gsmall, use_sent):
    core = lax.axis_index("c").astype(jnp.int32).reshape(1)
    chip = (2 * lax.axis_index("x") + lax.axis_index("y")).astype(jnp.int32).reshape(1)
    place = jnp.concatenate([chip, core])

    def finish(names, pairs, landed, name):
        totals = [_add_chips(r, p, place, "reduce_chips_" + n) for n, r, p in zip(names, landed, pairs)]
        return dict(zip(names, _reduce_back(totals, name)))

    names = list(gbig)
    pairs = _pair_sums(names, [gbig[n] for n in names], "")
    send_sems, recv_sems, pairs_thru, lands_thru, token = _exchange_start(pairs, "reduce_last_start", scatter=True)
    sent_names, sent_pairs, sent_landed = [], [], []
    for tag, group, g_send, g_recv, g_pairs, g_lands in pending:
        got_pairs, got_landed = _exchange_wait(g_send, g_recv, g_pairs, g_lands, token, "reduce_" + tag + "_wait",
                                               scatter=True)
        sent_names, sent_pairs, sent_landed = sent_names + group, sent_pairs + got_pairs, sent_landed + got_landed
    g_sent = finish(sent_names, sent_pairs, sent_landed, "reduce_back_sent")
    vals = dict(gsmall)
    vals["loss"] = loss
    small_red = _unpack_flat(SMALL_SUM, _all_reduce_small(_pack_flat(SMALL_SUM, vals), "reduce_small"))
    after = use_sent(g_sent, small_red)
    pairs, landed = _exchange_wait(send_sems, recv_sems, pairs_thru, lands_thru, after, "reduce_last_wait", scatter=True)
    return finish(names, pairs, landed, "reduce_back_last"), small_red


def _step(args):
    x = args["x"][0]
    positions = args["positions"][0]
    tgt = args["loss_target"][0]
    w = {n: args[n][0] for n in _WEIGHTS}
    m = {n: args["m_" + n][0] for n in _WEIGHTS}
    v = {n: args["v_" + n][0] for n in _WEIGHTS}

    full = _gather_weights(w)
    sp = {n: w[n].reshape(s) for n, s in SMALL}
    for n in ("mix_norm_pre", "q_norm", "kv_norm", "b_glu", "b_gate", "mix_norm_post", "ffn_norm_pre", "conv_b",
              "ffn_norm_post"):
        sp[n] = sp[n].reshape(1, -1)
    sp["mix_norm_pre"] = sp["mix_norm_pre"] + full.pop("token")
    pending, flying = [], []
    full["send_grads"] = lambda tag, grads, after=None: _send_grads(tag, grads, after, flying, pending)
    loss, gx, gbig, gsmall = _local_step(x, positions, tgt, full, sp)
    outs = {}

    def adam_big(g_red):
        for name in g_red:
            g2, d, m2, v2 = _adamw(w[name], g_red[name], m[name], v[name], "adamw_" + name)
            outs["grad_" + name], outs["delta_" + name], outs["new_m_" + name], outs["new_v_" + name] = g2, d, m2, v2
        return v2

    def use_sent(g_sent, small_red):
        chip = 2 * lax.axis_index("x") + lax.axis_index("y")
        grads = dict(small_red)
        grads["conv_w"] = lax.dynamic_slice_in_dim(small_red["conv_w"], chip * CONV_W_SHARD[1], CONV_W_SHARD[1], axis=1)
        outs.update({"grad_" + n: grads[n] for n, _ in SMALL_ADAM})
        _, d_sm, m_sm, v_sm = _adamw(_pack_flat(SMALL_ADAM, w), _pack_flat(SMALL_ADAM, grads),
                                     _pack_flat(SMALL_ADAM, m), _pack_flat(SMALL_ADAM, v), "adamw_small")
        for prefix, flat in (("delta_", d_sm), ("new_m_", m_sm), ("new_v_", v_sm)):
            for n, val in _unpack_flat(SMALL_ADAM, flat).items():
                outs[prefix + n] = val
        return adam_big(g_sent)

    g_last, small_red = _reduce_grads(gbig, pending, loss, gsmall, use_sent)
    adam_big(g_last)
    outs = {n: val.reshape(args[n.split("_", 1)[1] if not n.startswith("new_") else n[6:]].shape)
            for n, val in outs.items()}
    res = [small_red["loss"][0], gx[None]]
    for prefix in ("grad_", "delta_", "new_m_", "new_v_"):
        res += [outs[prefix + n] for n in _WEIGHTS]
    return tuple(res)


def kernel(x, positions, mix_norm_pre, w_in, q_norm, w_uq, kv_norm, w_uk, w_uv, ssm_lambda_re, ssm_lambda_im, ssm_log_dt, ssm_b_re, ssm_b_im, ssm_c_re, ssm_c_im, ssm_d, w_glu, b_glu, w_branch_attn, w_branch_ssm, b_gate, w_out, mix_norm_post, ffn_norm_pre, w_up, conv_w, conv_b, w_down, ffn_norm_post, loss_target, m_mix_norm_pre, m_w_in, m_q_norm, m_w_uq, m_kv_norm, m_w_uk, m_w_uv, m_ssm_lambda_re, m_ssm_lambda_im, m_ssm_log_dt, m_ssm_b_re, m_ssm_b_im, m_ssm_c_re, m_ssm_c_im, m_ssm_d, m_w_glu, m_b_glu, m_w_branch_attn, m_w_branch_ssm, m_b_gate, m_w_out, m_mix_norm_post, m_ffn_norm_pre, m_w_up, m_conv_w, m_conv_b, m_w_down, m_ffn_norm_post, v_mix_norm_pre, v_w_in, v_q_norm, v_w_uq, v_kv_norm, v_w_uk, v_w_uv, v_ssm_lambda_re, v_ssm_lambda_im, v_ssm_log_dt, v_ssm_b_re, v_ssm_b_im, v_ssm_c_re, v_ssm_c_im, v_ssm_d, v_w_glu, v_b_glu, v_w_branch_attn, v_w_branch_ssm, v_b_gate, v_w_out, v_mix_norm_post, v_ffn_norm_pre, v_w_up, v_conv_w, v_conv_b, v_w_down, v_ffn_norm_post):
    given = dict(locals())
    return _step(given)
```

```python
import math

import jax
import jax.numpy as jnp
from jax import lax
from jax.experimental import pallas as pl
from jax.experimental.pallas import tpu as pltpu

F32 = jnp.float32
BF16 = jnp.bfloat16
MESH = pl.DeviceIdType.MESH

D_MODEL = 1024
N_HEADS = 8
QK_NOPE = 64
QK_ROPE = 32
QK_HEAD = QK_NOPE + QK_ROPE
V_HEAD = 64
Q_RANK = 384
KV_RANK = 256
ROPE_THETA = 10000.0
SSM_W = 512
SSM_H = 16
SSM_G = 32
SSM_P = 64
SSM_CH = SSM_G * SSM_P
D_FF = 2816
EPS = 1e-6
ADAM_LR = 0.001
ADAM_B1 = 0.9
ADAM_B2 = 0.999
ADAM_EPS = 1e-08
ADAM_WD = 0.01
ADAM_STEP = 10

LANES = 128
SUBLANES = 8
VMEM_LIMIT = 56 * 1024 * 1024

HEAD_SLOT = LANES
HP = N_HEADS * HEAD_SLOT
P_CQ, P_CKV, P_KR, P_U, P_GL, P_END = 0, 384, 640, 768, 1280, 3328
KR_LANE = 64

FLAT_W = 1024
N_CHIPS = 4


def _tile(n, cap):
    if n <= cap:
        return n
    best = None
    for t in range(LANES, cap + 1, LANES):
        if n % t == 0:
            best = t
    assert best is not None, (n, cap)
    return best


def _params(sem):
    return pltpu.CompilerParams(dimension_semantics=sem, vmem_limit_bytes=VMEM_LIMIT)


def _dot(a, b):
    return jnp.dot(a, b, preferred_element_type=F32)


def _dot_nt(a, b):
    return lax.dot_general(a, b, (((1,), (1,)), ((), ())), preferred_element_type=F32)


def _dot_tn(a, b):
    return lax.dot_general(a, b, (((0,), (0,)), ((), ())), preferred_element_type=F32)


def _rms(x, g):
    r = lax.rsqrt(jnp.mean(x * x, axis=-1, keepdims=True) + EPS)
    return x * r * g, r


def _rms_bwd(dy, x, g):
    r = lax.rsqrt(jnp.mean(x * x, axis=-1, keepdims=True) + EPS)
    dyg = dy * g
    dx = r * dyg - x * (r * r * r) * jnp.mean(dyg * x, axis=-1, keepdims=True)
    dg = jnp.sum(dy * x * r, axis=0, keepdims=True)
    return dx, dg


_GELU_K0 = math.sqrt(2.0 / math.pi)
_GELU_K1 = 0.044715


def _gelu(x):
    th = jnp.tanh(_GELU_K0 * (x + _GELU_K1 * x * x * x))
    return 0.5 * x * (1.0 + th)


def _gelu_grad(x):
    th = jnp.tanh(_GELU_K0 * (x + _GELU_K1 * x * x * x))
    return 0.5 * (1.0 + th) + 0.5 * x * (1.0 - th * th) * _GELU_K0 * (1.0 + 3.0 * _GELU_K1 * x * x)


def _sigmoid(x):
    return 1.0 / (1.0 + jnp.exp(-x))


def _rope(q, c, s):
    n = q.shape[1]
    lane = lax.broadcasted_iota(jnp.int32, q.shape, 1) % HEAD_SLOT
    sw = jnp.where(lane < KR_LANE + QK_ROPE // 2, pltpu.roll(q, n - QK_ROPE // 2, 1), pltpu.roll(q, QK_ROPE // 2, 1))
    return q * c + sw * s


def _rope_bwd(dy, c, s):
    n = dy.shape[1]
    t = dy * s
    lane = lax.broadcasted_iota(jnp.int32, dy.shape, 1) % HEAD_SLOT
    sw = jnp.where(lane < KR_LANE + QK_ROPE // 2, pltpu.roll(t, n - QK_ROPE // 2, 1), pltpu.roll(t, QK_ROPE // 2, 1))
    rope_lane = jnp.logical_and(lane >= KR_LANE, lane < KR_LANE + QK_ROPE)
    return dy * c + jnp.where(rope_lane, sw, 0.0)


def _shift_down(x, k, halo):
    xs = pltpu.roll(x, k, 0)
    hs = pltpu.roll(halo, k, 0)
    rows = lax.broadcasted_iota(jnp.int32, halo.shape, 0)
    top = jnp.where(rows < k, hs, xs[0:SUBLANES])
    return jnp.concatenate([top, xs[SUBLANES:]], axis=0)


def _shift_up(x, k, halo):
    t = x.shape[0]
    xs = pltpu.roll(x, t - k, 0)
    hs = pltpu.roll(halo, SUBLANES - k, 0)
    rows = lax.broadcasted_iota(jnp.int32, halo.shape, 0)
    bot = jnp.where(rows >= SUBLANES - k, hs, xs[t - SUBLANES:])
    return jnp.concatenate([xs[:t - SUBLANES], bot], axis=0)


def _mm(a, b, name, out_dtype=F32, bt=False, tm_cap=1024, tn_cap=1408):
    m, k = a.shape[-2:]
    parts = a.shape[0] if a.ndim == 3 else 1
    n = b.shape[0] if bt else b.shape[1]
    tm = min(tm_cap, m)
    tn = _tile(n, tn_cap)

    def body(a_ref, b_ref, o_ref):
        if not bt:
            o_ref[...] = _dot(a_ref[...], b_ref[...]).astype(out_dtype)
        elif parts == 1:
            o_ref[...] = _dot_nt(a_ref[...], b_ref[...]).astype(out_dtype)
        else:
            @pl.when(pl.program_id(2) == 0)
            def _():
                o_ref[...] = jnp.zeros_like(o_ref)

            o_ref[...] += _dot_nt(a_ref[...], b_ref[...])

    if bt:
        b_spec = pl.BlockSpec((tn, k), lambda j, i, s: (j, s))
    else:
        b_spec = pl.BlockSpec((k, tn), lambda j, i, s: (0, j))
    if a.ndim == 3:
        assert bt and out_dtype == F32
        a_spec = pl.BlockSpec((None, tm, k), lambda j, i, s: (s, i, 0))
    else:
        a_spec = pl.BlockSpec((tm, k), lambda j, i, s: (i, 0))
    return pl.pallas_call(
        body, name=name, grid=(n // tn, m // tm, parts),
        in_specs=[a_spec, b_spec],
        out_specs=pl.BlockSpec((tm, tn), lambda j, i, s: (i, j)),
        out_shape=jax.ShapeDtypeStruct((m, n), out_dtype),
        compiler_params=_params(("parallel", "parallel", "arbitrary")),
    )(a, b)


def _mm_tn(a, b, name, tk_cap=1024, tn_cap=1664, tl_cap=2048, chips=False):
    l, k = a.shape
    tk = _tile(k, tk_cap)
    tl = min(tl_cap, l)

    def body(a_ref, b_ref, o_ref):
        @pl.when(pl.program_id(2) == 0)
        def _():
            o_ref[...] = jnp.zeros_like(o_ref)

        o_ref[...] += _dot_tn(a_ref[...], b_ref[...])

    if chips:
        n = b.shape[-1] * (b.shape[0] if b.ndim == 3 else 1)
        tn = n // N_CHIPS
        assert tn % LANES == 0
        if b.ndim == 3:
            per = N_CHIPS // b.shape[0]
            b_spec = pl.BlockSpec((None, tl, tn), lambda i, j, r: (j // per, r, j % per))
        else:
            b_spec = pl.BlockSpec((tl, tn), lambda i, j, r: (r, j))
        out_spec = pl.BlockSpec((None, tk, tn), lambda i, j, r: (j, i, 0))
        out_shape = jax.ShapeDtypeStruct((N_CHIPS, k, tn), F32)
    else:
        n = b.shape[1]
        tn = _tile(n, tn_cap)
        b_spec = pl.BlockSpec((tl, tn), lambda i, j, r: (r, j))
        out_spec = pl.BlockSpec((tk, tn), lambda i, j, r: (i, j))
        out_shape = jax.ShapeDtypeStruct((k, n), F32)
    return pl.pallas_call(
        body, name=name, grid=(k // tk, n // tn, l // tl),
        in_specs=[pl.BlockSpec((tl, tk), lambda i, j, r: (r, i)), b_spec],
        out_specs=out_spec, out_shape=out_shape,
        compiler_params=_params(("parallel", "parallel", "arbitrary")),
    )(a, b)


def _row(tl, n):
    return pl.BlockSpec((tl, n), lambda i: (i, 0))


def _const(shape):
    return pl.BlockSpec(shape, lambda i: tuple(0 for _ in shape))


def _proj_fwd(x, g1, win, gq, wuq, gkv, wukv, rc, rs, bg, tl):
    l = x.shape[0]

    def body(x_ref, g1_ref, win_ref, gq_ref, wuq_ref, gkv_ref, wukv_ref, rc_ref, rs_ref, bg_ref,
             hn_ref, cq_ref, ckv_ref, q_ref, k_ref, v_ref, u_ref, gl_ref):
        hn, _ = _rms(x_ref[...], g1_ref[...])
        hnb = hn.astype(BF16)
        hn_ref[...] = hnb
        proj = _dot(hnb, win_ref[...])
        cq = proj[:, P_CQ:P_CKV]
        ckv = proj[:, P_CKV:P_KR]
        kr = proj[:, P_KR:P_U]
        cq_ref[...] = cq
        ckv_ref[...] = ckv
        u_ref[...] = proj[:, P_U:P_GL]
        gl_ref[...] = proj[:, P_GL:P_END] + bg_ref[...]
        qn, _ = _rms(cq, gq_ref[...])
        q = _dot(qn.astype(BF16), wuq_ref[...])
        c1 = rc_ref[...]
        s1 = rs_ref[...]
        q_ref[...] = (_rope(q, jnp.tile(c1, (1, N_HEADS)), jnp.tile(s1, (1, N_HEADS))) * Q_PRESCALE).astype(BF16)
        ckvn, _ = _rms(ckv, gkv_ref[...])
        kv = _dot(ckvn.astype(BF16), wukv_ref[...])
        krr = _rope(kr, c1, s1)
        k_ref[...] = (kv[:, :HP] + jnp.tile(krr, (1, N_HEADS))).astype(BF16)
        v_ref[...] = kv[:, HP:].astype(BF16)

    outs = [(D_MODEL, BF16), (Q_RANK, F32), (KV_RANK, F32), (HP, BF16), (HP, BF16), (HP, BF16),
            (SSM_W, F32), (2 * D_MODEL, F32)]
    return pl.pallas_call(
        body, name="proj_fwd", grid=(l // tl,),
        in_specs=[_row(tl, D_MODEL), _const((1, D_MODEL)), _const((D_MODEL, P_END)), _const((1, Q_RANK)),
                  _const((Q_RANK, HP)), _const((1, KV_RANK)), _const((KV_RANK, 2 * HP)),
                  _row(tl, HEAD_SLOT), _row(tl, HEAD_SLOT), _const((1, 2 * D_MODEL))],
        out_specs=[_row(tl, n) for n, _ in outs],
        out_shape=[jax.ShapeDtypeStruct((l, n), dt) for n, dt in outs],
        compiler_params=_params(("parallel",)),
    )(x, g1, win, gq, wuq, gkv, wukv, rc, rs, bg)


_NEG = -1e30


LOG2E = 1.0 / math.log(2.0)
LN2 = math.log(2.0)
ATTN_SCALE = 1.0 / math.sqrt(QK_HEAD)
Q_PRESCALE = ATTN_SCALE * LOG2E
HEADS_PER_STEP = 4
PAIR_W = HEADS_PER_STEP * HEAD_SLOT


def _causal_pairs(nq, by_query):
    if by_query:
        pairs = [(i, j) for i in range(nq) for j in range(i + 1)]
    else:
        pairs = [(i, j) for j in range(nq) for i in range(j, nq)]
    return jnp.array([p[0] for p in pairs], jnp.int32), jnp.array([p[1] for p in pairs], jnp.int32)


def _diag_mask_t(s):
    rows = lax.broadcasted_iota(jnp.int32, s.shape, 0)
    cols = lax.broadcasted_iota(jnp.int32, s.shape, 1)
    return jnp.where(rows <= cols, s, _NEG)


def _attn_fwd(q, k, v, tq):
    l = q.shape[0]
    nq = l // tq
    it, jt = _causal_pairs(nq, True)

    def body(it_ref, jt_ref, q_ref, k_ref, v_ref, o_ref, lse_ref, m_ref, l_ref, acc_ref):
        t = pl.program_id(1)
        i = it_ref[t]
        j = jt_ref[t]

        @pl.when(j == 0)
        def _():
            m_ref[...] = jnp.full_like(m_ref, _NEG)
            l_ref[...] = jnp.zeros_like(l_ref)
            acc_ref[...] = jnp.zeros_like(acc_ref)

        def update(on_diagonal):
            for hh in range(HEADS_PER_STEP):
                sl = slice(hh * HEAD_SLOT, (hh + 1) * HEAD_SLOT)
                s = _dot_nt(k_ref[:, sl], q_ref[:, sl])
                if on_diagonal:
                    s = _diag_mask_t(s)
                m_old = m_ref[hh]
                m_new = jnp.maximum(m_old, jnp.max(s, axis=0, keepdims=True))
                p = jnp.exp2(s - m_new)
                alpha = jnp.exp2(m_old - m_new)
                l_ref[hh] = alpha * l_ref[hh] + jnp.sum(p, axis=0, keepdims=True)
                acc_ref[hh] = alpha * acc_ref[hh] + _dot_tn(v_ref[:, sl], p.astype(BF16))
                m_ref[hh] = m_new

        @pl.when(j < i)
        def _():
            update(False)

        @pl.when(j == i)
        def _():
            update(True)
            for hh in range(HEADS_PER_STEP):
                sl = slice(hh * HEAD_SLOT, (hh + 1) * HEAD_SLOT)
                o_ref[:, sl] = (acc_ref[hh] / l_ref[hh]).T.astype(BF16)
                lse_ref[hh] = m_ref[hh] + jnp.log(l_ref[hh]) * LOG2E

    blk = (tq, PAIR_W)
    qmap = lambda h, t, it_ref, jt_ref: (it_ref[t], h)
    kmap = lambda h, t, it_ref, jt_ref: (jt_ref[t], h)
    row = pl.BlockSpec((HEADS_PER_STEP, 1, tq), lambda h, t, it_ref, jt_ref: (h, 0, it_ref[t]))
    return pl.pallas_call(
        body, name="attn_fwd",
        grid_spec=pltpu.PrefetchScalarGridSpec(
            num_scalar_prefetch=2, grid=(N_HEADS // HEADS_PER_STEP, it.shape[0]),
            in_specs=[pl.BlockSpec(blk, qmap), pl.BlockSpec(blk, kmap), pl.BlockSpec(blk, kmap)],
            out_specs=[pl.BlockSpec(blk, qmap), row],
            scratch_shapes=[pltpu.VMEM((HEADS_PER_STEP, 1, tq), F32), pltpu.VMEM((HEADS_PER_STEP, 1, tq), F32),
                            pltpu.VMEM((HEADS_PER_STEP, HEAD_SLOT, tq), F32)]),
        out_shape=[jax.ShapeDtypeStruct((l, HP), BF16), jax.ShapeDtypeStruct((N_HEADS, 1, l), F32)],
        compiler_params=_params(("parallel", "arbitrary")),
    )(it, jt, q, k, v)


def _attn_delta(o, do, tq):
    l = o.shape[0]

    def body(o_ref, do_ref, d_ref):
        prod = o_ref[...].astype(F32) * do_ref[...].astype(F32)
        for hh in range(HEADS_PER_STEP):
            d_ref[hh] = jnp.sum(prod[:, hh * HEAD_SLOT:(hh + 1) * HEAD_SLOT].T, axis=0, keepdims=True)

    blk = pl.BlockSpec((tq, PAIR_W), lambda h, i: (i, h))
    return pl.pallas_call(
        body, name="attn_delta", grid=(N_HEADS // HEADS_PER_STEP, l // tq), in_specs=[blk, blk],
        out_specs=pl.BlockSpec((HEADS_PER_STEP, 1, tq), lambda h, i: (h, 0, i)),
        out_shape=jax.ShapeDtypeStruct((N_HEADS, 1, l), F32),
        compiler_params=_params(("parallel", "parallel")),
    )(o, do)


def _attn_bwd(q, k, v, do, lse, delta, tq):
    l = q.shape[0]
    nq = l // tq
    it, jt = _causal_pairs(nq, False)

    def body(it_ref, jt_ref, q_ref, k_ref, v_ref, do_ref, lse_ref, dl_ref, dq_ref, dk_ref, dv_ref, dka_ref, dva_ref):
        t = pl.program_id(1)
        i = it_ref[t]
        j = jt_ref[t]

        @pl.when(t == 0)
        def _():
            dq_ref[...] = jnp.zeros_like(dq_ref)

        @pl.when(i == j)
        def _():
            dka_ref[...] = jnp.zeros_like(dka_ref)
            dva_ref[...] = jnp.zeros_like(dva_ref)

        def update(on_diagonal):
            r0 = pl.multiple_of(i * tq, tq)
            for hh in range(HEADS_PER_STEP):
                sl = slice(hh * HEAD_SLOT, (hh + 1) * HEAD_SLOT)
                qb = q_ref[:, sl]
                kb = k_ref[:, sl]
                dob = do_ref[:, sl]
                s = _dot_nt(kb, qb)
                if on_diagonal:
                    s = _diag_mask_t(s)
                p = jnp.exp2(s - lse_ref[hh])
                dva_ref[:, sl] += _dot(p.astype(BF16), dob)
                dp = _dot_nt(v_ref[:, sl], dob)
                ds = (p * (dp - dl_ref[hh])).astype(BF16)
                dka_ref[:, sl] += _dot(ds, qb)
                dq_ref[pl.ds(r0, tq), sl] += ATTN_SCALE * _dot_tn(ds, kb)

        @pl.when(j < i)
        def _():
            update(False)

        @pl.when(j == i)
        def _():
            update(True)

        @pl.when(i == nq - 1)
        def _():
            dk_ref[...] = (dka_ref[...] * LN2).astype(BF16)
            dv_ref[...] = dva_ref[...].astype(BF16)

    blk = (tq, PAIR_W)
    qmap = lambda h, t, it_ref, jt_ref: (it_ref[t], h)
    kmap = lambda h, t, it_ref, jt_ref: (jt_ref[t], h)
    row = pl.BlockSpec((HEADS_PER_STEP, 1, tq), lambda h, t, it_ref, jt_ref: (h, 0, it_ref[t]))
    return pl.pallas_call(
        body, name="attn_bwd",
        grid_spec=pltpu.PrefetchScalarGridSpec(
            num_scalar_prefetch=2, grid=(N_HEADS // HEADS_PER_STEP, it.shape[0]),
            in_specs=[pl.BlockSpec(blk, qmap), pl.BlockSpec(blk, kmap), pl.BlockSpec(blk, kmap),
                      pl.BlockSpec(blk, qmap), row, row],
            out_specs=[pl.BlockSpec((l, PAIR_W), lambda h, t, it_ref, jt_ref: (0, h)), pl.BlockSpec(blk, kmap),
                       pl.BlockSpec(blk, kmap)],
            scratch_shapes=[pltpu.VMEM(blk, F32), pltpu.VMEM(blk, F32)]),
        out_shape=[jax.ShapeDtypeStruct((l, HP), F32), jax.ShapeDtypeStruct((l, HP), BF16),
                   jax.ShapeDtypeStruct((l, HP), BF16)],
        compiler_params=_params(("parallel", "arbitrary")),
    )(it, jt, q, k, v, do, lse, delta)


SSM_CB = 512
SSM_UB = 128
SSM_NB = SSM_CH // SSM_CB


def _scan_tiles(re_ref, im_ref, tab, carry, n_tiles, reverse):
    group = 2
    assert n_tiles % group == 0
    pr, pi = tab[6], tab[7]

    def inside(sr, si):
        for step, k in enumerate((1, 2, 4)):
            mr, mi = tab[2 * step], tab[2 * step + 1]
            sh = (SUBLANES - k) if reverse else k
            rr = pltpu.roll(sr, sh, 0)
            ri = pltpu.roll(si, sh, 0)
            sr, si = sr + mr * rr - mi * ri, si + mr * ri + mi * rr
        return sr, si

    def body(n, c):
        cr, ci = c
        first = (n_tiles - group * (n + 1)) if reverse else group * n
        r0 = pl.multiple_of(first * SUBLANES, group * SUBLANES)
        rows = [pl.ds(r0 + g * SUBLANES, SUBLANES) for g in range(group)]
        tiles = [inside(re_ref[r, :], im_ref[r, :]) for r in rows]
        for g in (range(group - 1, -1, -1) if reverse else range(group)):
            sr, si = tiles[g]
            sr, si = sr + pr * cr - pi * ci, si + pr * ci + pi * cr
            re_ref[rows[g], :] = sr
            im_ref[rows[g], :] = si
            edge = slice(0, 1) if reverse else slice(SUBLANES - 1, SUBLANES)
            cr, ci = sr[edge, :], si[edge, :]
        return cr, ci

    return lax.fori_loop(0, n_tiles // group, body, carry)


def _ssm_fwd(u, bre, bim, cre, cim, dvec, tab, tt):
    l = u.shape[0]
    nt = l // tt

    def body(u_ref, bre_ref, bim_ref, cre_ref, cim_ref, d_ref, tab_ref, y_ref, sre_ref, sim_ref, car_ref):
        @pl.when(pl.program_id(1) == 0)
        def _():
            car_ref[...] = jnp.zeros_like(car_ref)

        uf = u_ref[...]
        ub = uf.astype(BF16)
        sre_ref[...] = _dot(ub, bre_ref[0])
        sim_ref[...] = _dot(ub, bim_ref[0])
        tab_v = [tab_ref[n] for n in range(8)]
        cr, ci = _scan_tiles(sre_ref, sim_ref, tab_v, (car_ref[0:1, :], car_ref[8:9, :]), tt // SUBLANES, False)
        car_ref[0:1, :] = cr
        car_ref[8:9, :] = ci
        y_ref[...] = (_dot(sre_ref[...].astype(BF16), cre_ref[0]) - _dot(sim_ref[...].astype(BF16), cim_ref[0])
                      + d_ref[...] * uf)

    return pl.pallas_call(
        body, name="ssm_fwd", grid=(SSM_NB, nt),
        in_specs=[pl.BlockSpec((tt, SSM_UB), lambda m, t: (t, m)),
                  pl.BlockSpec((1, SSM_UB, SSM_CB), lambda m, t: (m, 0, 0)),
                  pl.BlockSpec((1, SSM_UB, SSM_CB), lambda m, t: (m, 0, 0)),
                  pl.BlockSpec((1, SSM_CB, SSM_UB), lambda m, t: (m, 0, 0)),
                  pl.BlockSpec((1, SSM_CB, SSM_UB), lambda m, t: (m, 0, 0)),
                  pl.BlockSpec((1, SSM_UB), lambda m, t: (0, m)),
                  pl.BlockSpec((8, SUBLANES, SSM_CB), lambda m, t: (0, 0, m))],
        out_specs=[pl.BlockSpec((tt, SSM_UB), lambda m, t: (t, m)),
                   pl.BlockSpec((tt, SSM_CB), lambda m, t: (t, m)),
                   pl.BlockSpec((tt, SSM_CB), lambda m, t: (t, m))],
        out_shape=[jax.ShapeDtypeStruct((l, SSM_W), F32), jax.ShapeDtypeStruct((l, SSM_CH), F32),
                   jax.ShapeDtypeStruct((l, SSM_CH), F32)],
        scratch_shapes=[pltpu.VMEM((2 * SUBLANES, SSM_CB), F32)],
        compiler_params=_params(("parallel", "arbitrary")),
    )(u, bre, bim, cre, cim, dvec, tab)


def _ssm_bwd(dy, u, sre, sim, bre, bim, cre, cim, dvec, tab, tt):
    l = u.shape[0]
    nt = l // tt
    tpb = tt // SUBLANES

    def body(dy_ref, u_ref, sre_ref, sim_ref, hre_ref, him_ref, bre_ref, bim_ref, cre_ref, cim_ref, d_ref, tab_ref,
             du_ref, dbre_ref, dbim_ref, dcre_ref, dcim_ref, dare_ref, daim_ref, dd_ref, lr_ref, li_ref, car_ref):
        t = pl.program_id(1)

        @pl.when(t == 0)
        def _():
            car_ref[...] = jnp.zeros_like(car_ref)
            for ref in (dbre_ref, dbim_ref, dcre_ref, dcim_ref, dare_ref, daim_ref, dd_ref):
                ref[...] = jnp.zeros_like(ref)

        dyf = dy_ref[...]
        dyb = dyf.astype(BF16)
        uf = u_ref[...]
        s_re = sre_ref[...]
        s_im = sim_ref[...]
        lr_ref[...] = _dot_nt(dyb, cre_ref[0])
        li_ref[...] = -_dot_nt(dyb, cim_ref[0])
        dcre_ref[0] += _dot_tn(s_re.astype(BF16), dyb)
        dcim_ref[0] -= _dot_tn(s_im.astype(BF16), dyb)
        tab_v = [tab_ref[n] for n in range(8)]
        cr, ci = _scan_tiles(lr_ref, li_ref, tab_v, (car_ref[0:1, :], car_ref[8:9, :]), tpb, True)
        car_ref[0:1, :] = cr
        car_ref[8:9, :] = ci
        lam_r = lr_ref[...]
        lam_i = li_ref[...]
        keep = jnp.where(t == nt - 1, 0.0, 1.0)
        sp_r = _shift_down(s_re, 1, hre_ref[...] * keep)
        sp_i = _shift_down(s_im, 1, him_ref[...] * keep)
        dare_ref[...] += jnp.sum(lam_r * sp_r + lam_i * sp_i, axis=0, keepdims=True)
        daim_ref[...] += jnp.sum(lam_i * sp_r - lam_r * sp_i, axis=0, keepdims=True)
        lrb = lam_r.astype(BF16)
        lib = lam_i.astype(BF16)
        du_ref[...] = _dot_nt(lrb, bre_ref[0]) + _dot_nt(lib, bim_ref[0]) + dyf * d_ref[...]
        ub = uf.astype(BF16)
        dbre_ref[0] += _dot_tn(ub, lrb)
        dbim_ref[0] += _dot_tn(ub, lib)
        dd_ref[...] += jnp.sum(dyf * uf, axis=0, keepdims=True)

    rev = lambda m, t: (nt - 1 - t, m)
    halo = lambda m, t: (jnp.maximum((nt - 1 - t) * tpb - 1, 0), m)
    wb = pl.BlockSpec((1, SSM_UB, SSM_CB), lambda m, t: (m, 0, 0))
    wc = pl.BlockSpec((1, SSM_CB, SSM_UB), lambda m, t: (m, 0, 0))
    vec_c = pl.BlockSpec((1, SSM_CB), lambda m, t: (0, m))
    vec_u = pl.BlockSpec((1, SSM_UB), lambda m, t: (0, m))
    return pl.pallas_call(
        body, name="ssm_bwd", grid=(SSM_NB, nt),
        in_specs=[pl.BlockSpec((tt, SSM_UB), rev), pl.BlockSpec((tt, SSM_UB), rev),
                  pl.BlockSpec((tt, SSM_CB), rev), pl.BlockSpec((tt, SSM_CB), rev),
                  pl.BlockSpec((SUBLANES, SSM_CB), halo), pl.BlockSpec((SUBLANES, SSM_CB), halo),
                  wb, wb, wc, wc, vec_u,
                  pl.BlockSpec((8, SUBLANES, SSM_CB), lambda m, t: (0, 0, m))],
        out_specs=[pl.BlockSpec((tt, SSM_UB), rev), wb, wb, wc, wc, vec_c, vec_c, vec_u],
        out_shape=[jax.ShapeDtypeStruct((l, SSM_W), F32),
                   jax.ShapeDtypeStruct((SSM_NB, SSM_UB, SSM_CB), F32), jax.ShapeDtypeStruct((SSM_NB, SSM_UB, SSM_CB), F32),
                   jax.ShapeDtypeStruct((SSM_NB, SSM_CB, SSM_UB), F32), jax.ShapeDtypeStruct((SSM_NB, SSM_CB, SSM_UB), F32),
                   jax.ShapeDtypeStruct((1, SSM_CH), F32), jax.ShapeDtypeStruct((1, SSM_CH), F32),
                   jax.ShapeDtypeStruct((1, SSM_W), F32)],
        scratch_shapes=[pltpu.VMEM((tt, SSM_CB), F32), pltpu.VMEM((tt, SSM_CB), F32),
                        pltpu.VMEM((2 * SUBLANES, SSM_CB), F32)],
        compiler_params=_params(("parallel", "arbitrary")),
    )(dy, u, sre, sim, sre, sim, bre, bim, cre, cim, dvec, tab)


def _merge_fwd(x, gl, attn, y1, wba, wbs, wglu, bglu, wout, gpost, gpre, tl):
    l = x.shape[0]

    def body(x_ref, gl_ref, at_ref, y1_ref, wba_ref, wbs_ref, wglu_ref, bglu_ref, wout_ref, gpost_ref, gpre_ref,
             a_ref, sm_ref, mg_ref, z_ref, x1_ref, hn2_ref, y3_ref):
        y2 = _gelu(y1_ref[...])
        sg = _sigmoid(_dot(y2.astype(BF16), wglu_ref[...]) + bglu_ref[...])
        y3 = (y2 * sg).astype(BF16)
        y3_ref[...] = y3
        a = _dot(at_ref[...], wba_ref[...])
        sm = _dot(y3, wbs_ref[...])
        a_ref[...] = a.astype(BF16)
        sm_ref[...] = sm.astype(BF16)
        g = _sigmoid(gl_ref[...])
        merged = (g[:, :D_MODEL] * a + g[:, D_MODEL:] * sm).astype(BF16)
        mg_ref[...] = merged
        z = _dot(merged, wout_ref[...])
        z_ref[...] = z
        n, _ = _rms(z, gpost_ref[...])
        x1 = x_ref[...] + n
        x1_ref[...] = x1
        hn2, _ = _rms(x1, gpre_ref[...])
        hn2_ref[...] = hn2.astype(BF16)

    outs = [(D_MODEL, BF16), (D_MODEL, BF16), (D_MODEL, BF16), (D_MODEL, F32), (D_MODEL, F32), (D_MODEL, BF16),
            (SSM_W, BF16)]
    return pl.pallas_call(
        body, name="merge_fwd", grid=(l // tl,),
        in_specs=[_row(tl, D_MODEL), _row(tl, 2 * D_MODEL), _row(tl, HP), _row(tl, SSM_W),
                  _const((HP, D_MODEL)), _const((SSM_W, D_MODEL)), _const((SSM_W, SSM_W)), _const((1, SSM_W)),
                  _const((D_MODEL, D_MODEL)), _const((1, D_MODEL)), _const((1, D_MODEL))],
        out_specs=[_row(tl, n) for n, _ in outs],
        out_shape=[jax.ShapeDtypeStruct((l, n), dt) for n, dt in outs],
        compiler_params=_params(("parallel",)),
    )(x, gl, attn, y1, wba, wbs, wglu, bglu, wout, gpost, gpre)


def _merge_bwd(dhn2, x1, dx2, z, gl, a, sm, y1, wba, wbs, wglu, bglu, wout, gpost, gpre, tl):
    l = x1.shape[0]

    def body(dhn2_ref, x1_ref, dx2_ref, z_ref, gl_ref, a_ref, sm_ref, y1_ref,
             wba_ref, wbs_ref, wglu_ref, bglu_ref, wout_ref, gpost_ref, gpre_ref,
             dx1_ref, dz_ref, dbra_ref, dbrs_ref, dgl_ref, dat_ref, dy1_ref, dt_ref, y2_ref,
             dgpre_ref, dgpost_ref, dbg_ref, dbglu_ref):
        @pl.when(pl.program_id(0) == 0)
        def _():
            for ref in (dgpre_ref, dgpost_ref, dbg_ref, dbglu_ref):
                ref[...] = jnp.zeros_like(ref)

        dx1a, dgpre = _rms_bwd(dhn2_ref[...], x1_ref[...], gpre_ref[...])
        dgpre_ref[...] += dgpre
        dx1 = dx2_ref[...] + dx1a
        dx1_ref[...] = dx1
        dz, dgpost = _rms_bwd(dx1, z_ref[...], gpost_ref[...])
        dgpost_ref[...] += dgpost
        dzb = dz.astype(BF16)
        dz_ref[...] = dzb
        dm = _dot_nt(dzb, wout_ref[...])
        g = _sigmoid(gl_ref[...])
        g0 = g[:, :D_MODEL]
        g1 = g[:, D_MODEL:]
        dbra = (dm * g0).astype(BF16)
        dbrs = (dm * g1).astype(BF16)
        dbra_ref[...] = dbra
        dbrs_ref[...] = dbrs
        dgl0 = dm * a_ref[...].astype(F32) * g0 * (1.0 - g0)
        dgl1 = dm * sm_ref[...].astype(F32) * g1 * (1.0 - g1)
        dgl_ref[:, :D_MODEL] = dgl0.astype(BF16)
        dgl_ref[:, D_MODEL:] = dgl1.astype(BF16)
        dbg_ref[:, :D_MODEL] += jnp.sum(dgl0, axis=0, keepdims=True)
        dbg_ref[:, D_MODEL:] += jnp.sum(dgl1, axis=0, keepdims=True)
        dat_ref[...] = _dot_nt(dbra, wba_ref[...]).astype(BF16)
        dy3 = _dot_nt(dbrs, wbs_ref[...])
        y1v = y1_ref[...]
        y2 = _gelu(y1v)
        y2b = y2.astype(BF16)
        y2_ref[...] = y2b
        sg = _sigmoid(_dot(y2b, wglu_ref[...]) + bglu_ref[...])
        dt = dy3 * y2 * sg * (1.0 - sg)
        dtb = dt.astype(BF16)
        dt_ref[...] = dtb
        dbglu_ref[...] += jnp.sum(dt, axis=0, keepdims=True)
        dy2 = dy3 * sg + _dot_nt(dtb, wglu_ref[...])
        dy1_ref[...] = dy2 * _gelu_grad(y1v)

    outs = [(D_MODEL, F32), (D_MODEL, BF16), (D_MODEL, BF16), (D_MODEL, BF16), (2 * D_MODEL, BF16), (HP, BF16),
            (SSM_W, F32), (SSM_W, BF16), (SSM_W, BF16)]
    accs = [D_MODEL, D_MODEL, 2 * D_MODEL, SSM_W]
    return pl.pallas_call(
        body, name="merge_bwd", grid=(l // tl,),
        in_specs=[_row(tl, D_MODEL), _row(tl, D_MODEL), _row(tl, D_MODEL), _row(tl, D_MODEL),
                  _row(tl, 2 * D_MODEL), _row(tl, D_MODEL), _row(tl, D_MODEL), _row(tl, SSM_W),
                  _const((HP, D_MODEL)), _const((SSM_W, D_MODEL)), _const((SSM_W, SSM_W)), _const((1, SSM_W)),
                  _const((D_MODEL, D_MODEL)), _const((1, D_MODEL)), _const((1, D_MODEL))],
        out_specs=[_row(tl, n) for n, _ in outs] + [_const((1, n)) for n in accs],
        out_shape=[jax.ShapeDtypeStruct((l, n), dt) for n, dt in outs]
        + [jax.ShapeDtypeStruct((1, n), F32) for n in accs],
        compiler_params=_params(("arbitrary",)),
    )(dhn2, x1, dx2, z, gl, a, sm, y1, wba, wbs, wglu, bglu, wout, gpost, gpre)


def _proj_bwd(x, dx1, cq, ckv, dq, dk, dv, du, dgl, g1, win, gq, wuq, gkv, wukv, rc, rs, tl):
    l = x.shape[0]

    def body(x_ref, dx1_ref, cq_ref, ckv_ref, dq_ref, dk_ref, dv_ref, du_ref, dgl_ref,
             g1_ref, win_ref, gq_ref, wuq_ref, gkv_ref, wukv_ref, rc_ref, rs_ref,
             gx_ref, dql_ref, qn_ref, ckvn_ref, dproj_ref, dg1_ref, dgq_ref, dgkv_ref):
        @pl.when(pl.program_id(0) == 0)
        def _():
            for ref in (dg1_ref, dgq_ref, dgkv_ref):
                ref[...] = jnp.zeros_like(ref)

        c1 = rc_ref[...]
        s1 = rs_ref[...]
        dql = _rope_bwd(dq_ref[...], jnp.tile(c1, (1, N_HEADS)), jnp.tile(s1, (1, N_HEADS))).astype(BF16)
        dql_ref[...] = dql
        dqn = _dot_nt(dql, wuq_ref[...])
        cq = cq_ref[...]
        qn, _ = _rms(cq, gq_ref[...])
        qn_ref[...] = qn.astype(BF16)
        dcq, dgq = _rms_bwd(dqn, cq, gq_ref[...])
        dgq_ref[...] += dgq
        dkb = dk_ref[...]
        dvb = dv_ref[...]
        dkf = dkb.astype(F32)
        dkr = dkf[:, 0:HEAD_SLOT]
        for h in range(1, N_HEADS):
            dkr = dkr + dkf[:, h * HEAD_SLOT:(h + 1) * HEAD_SLOT]
        dkr = _rope_bwd(dkr, c1, s1)
        dckvn = _dot_nt(dkb, wukv_ref[:, :HP]) + _dot_nt(dvb, wukv_ref[:, HP:])
        ckv = ckv_ref[...]
        ckvn, _ = _rms(ckv, gkv_ref[...])
        ckvn_ref[...] = ckvn.astype(BF16)
        dckv, dgkv = _rms_bwd(dckvn, ckv, gkv_ref[...])
        dgkv_ref[...] += dgkv
        dproj_ref[:, P_CQ:P_CKV] = dcq.astype(BF16)
        dproj_ref[:, P_CKV:P_KR] = dckv.astype(BF16)
        dproj_ref[:, P_KR:P_U] = dkr.astype(BF16)
        dproj_ref[:, P_U:P_GL] = du_ref[...].astype(BF16)
        dproj_ref[:, P_GL:P_END] = dgl_ref[...]
        dhn = _dot_nt(dproj_ref[...], win_ref[...])
        dxa, dg1 = _rms_bwd(dhn, x_ref[...], g1_ref[...])
        dg1_ref[...] += dg1
        gx_ref[...] = dx1_ref[...] + dxa

    outs = [(D_MODEL, F32), (HP, BF16), (Q_RANK, BF16), (KV_RANK, BF16), (P_END, BF16)]
    accs = [D_MODEL, Q_RANK, KV_RANK]
    return pl.pallas_call(
        body, name="proj_bwd", grid=(l // tl,),
        in_specs=[_row(tl, D_MODEL), _row(tl, D_MODEL), _row(tl, Q_RANK), _row(tl, KV_RANK), _row(tl, HP),
                  _row(tl, HP), _row(tl, HP), _row(tl, SSM_W), _row(tl, 2 * D_MODEL),
                  _const((1, D_MODEL)), _const((D_MODEL, P_END)), _const((1, Q_RANK)), _const((Q_RANK, HP)),
                  _const((1, KV_RANK)), _const((KV_RANK, 2 * HP)), _row(tl, HEAD_SLOT), _row(tl, HEAD_SLOT)],
        out_specs=[_row(tl, n) for n, _ in outs] + [_const((1, n)) for n in accs],
        out_shape=[jax.ShapeDtypeStruct((l, n), dt) for n, dt in outs]
        + [jax.ShapeDtypeStruct((1, n), F32) for n in accs],
        compiler_params=_params(("arbitrary",)),
    )(x, dx1, cq, ckv, dq, dk, dv, du, dgl, g1, win, gq, wuq, gkv, wukv, rc, rs)


CONV_CB = 256
CONV_NB = D_FF // CONV_CB
CONV_ROWS = 16


def _conv3(h, halo, w, b):
    return b + w[0:1, :] * _shift_down(h, 2, halo) + w[1:2, :] * _shift_down(h, 1, halo) + w[2:3, :] * h


def _conv_fwd(h, cw, cb, tl):
    l = h.shape[0]

    def body(hg_ref, hv_ref, wg_ref, wv_ref, bg_ref, bv_ref, act_ref, halo_ref):
        @pl.when(pl.program_id(1) == 0)
        def _():
            halo_ref[...] = jnp.zeros_like(halo_ref)

        hg = hg_ref[...]
        hv = hv_ref[...]
        cg = _conv3(hg, halo_ref[0:SUBLANES, :], wg_ref[...], bg_ref[...])
        cv = _conv3(hv, halo_ref[SUBLANES:, :], wv_ref[...], bv_ref[...])
        act_ref[...] = (_gelu(cg) * cv).astype(BF16)
        halo_ref[0:SUBLANES, :] = hg[tl - SUBLANES:, :]
        halo_ref[SUBLANES:, :] = hv[tl - SUBLANES:, :]

    gmap = lambda c, r: (r, c)
    vmap = lambda c, r: (r, CONV_NB + c)
    return pl.pallas_call(
        body, name="conv_fwd", grid=(CONV_NB, l // tl),
        in_specs=[pl.BlockSpec((tl, CONV_CB), gmap), pl.BlockSpec((tl, CONV_CB), vmap),
                  pl.BlockSpec((3, CONV_CB), lambda c, r: (0, c)), pl.BlockSpec((3, CONV_CB), lambda c, r: (0, CONV_NB + c)),
                  pl.BlockSpec((1, CONV_CB), lambda c, r: (0, c)), pl.BlockSpec((1, CONV_CB), lambda c, r: (0, CONV_NB + c))],
        out_specs=pl.BlockSpec((tl, CONV_CB), gmap),
        out_shape=jax.ShapeDtypeStruct((l, D_FF), BF16),
        scratch_shapes=[pltpu.VMEM((2 * SUBLANES, CONV_CB), F32)],
        compiler_params=_params(("parallel", "arbitrary")),
    )(h, h, cw, cw, cb, cb)


def _conv_bwd(h, dact, cw, cb, tl):
    l = h.shape[0]
    nr = l // tl
    tpb = tl // SUBLANES

    def body(hg_ref, hv_ref, hgh_ref, hvh_ref, da_ref, wg_ref, wv_ref, bg_ref, bv_ref,
             dh_ref, dwg_ref, dwv_ref, dbg_ref, dbv_ref, car_ref):
        r = pl.program_id(1)

        @pl.when(r == 0)
        def _():
            for ref in (car_ref, dwg_ref, dwv_ref, dbg_ref, dbv_ref):
                ref[...] = jnp.zeros_like(ref)

        keep = jnp.where(r == nr - 1, 0.0, 1.0)
        wg, wv, bg, bv = wg_ref[...], wv_ref[...], bg_ref[...], bv_ref[...]
        nch = tl // CONV_ROWS

        def fold(x):
            s = x[0:SUBLANES, :]
            for k in range(1, CONV_ROWS // SUBLANES):
                s = s + x[k * SUBLANES:(k + 1) * SUBLANES, :]
            return s

        def chunk(n, carry):
            ncg, ncv, acc = carry
            idx = nch - 1 - n
            r0 = pl.multiple_of(idx * CONV_ROWS, CONV_ROWS)
            rows = pl.ds(r0, CONV_ROWS)
            before = pl.ds(pl.multiple_of(jnp.maximum(r0 - SUBLANES, 0), SUBLANES), SUBLANES)
            in_tile = idx > 0
            da = da_ref[rows, :].astype(F32)

            def half(h_ref, halo_ref, w, b):
                hh = h_ref[rows, :]
                prev = jnp.where(in_tile, h_ref[before, :], halo_ref[...] * keep)
                h1 = _shift_down(hh, 1, prev)
                h2 = _shift_down(hh, 2, prev)
                return hh, h1, h2, b + w[0:1, :] * h2 + w[1:2, :] * h1 + w[2:3, :] * hh

            hg, hg1, hg2, cg = half(hg_ref, hgh_ref, wg, bg)
            hv, hv1, hv2, cv = half(hv_ref, hvh_ref, wv, bv)
            dcg = da * cv * _gelu_grad(cg)
            dcv = da * _gelu(cg)

            def back(dc, hh, h1, h2, w, nxt, part):
                dh = w[2:3, :] * dc + w[1:2, :] * _shift_up(dc, 1, nxt) + w[0:1, :] * _shift_up(dc, 2, nxt)
                dh_ref[part, rows, :] = dh.astype(BF16)
                return [fold(dc * h2), fold(dc * h1), fold(dc * hh), fold(dc)]

            sums = back(dcg, hg, hg1, hg2, wg, ncg, 0) + back(dcv, hv, hv1, hv2, wv, ncv, 1)
            return dcg[0:SUBLANES, :], dcv[0:SUBLANES, :], [a + s for a, s in zip(acc, sums)]

        zero = jnp.zeros((SUBLANES, CONV_CB), F32)
        ncg, ncv, acc = lax.fori_loop(0, nch, chunk, (car_ref[0:SUBLANES, :], car_ref[SUBLANES:, :], [zero] * 8))
        car_ref[0:SUBLANES, :] = ncg
        car_ref[SUBLANES:, :] = ncv
        for half_acc, dw_ref, db_ref in ((acc[0:4], dwg_ref, dbg_ref), (acc[4:8], dwv_ref, dbv_ref)):
            for k in range(3):
                dw_ref[k:k + 1, :] += jnp.sum(half_acc[k], axis=0, keepdims=True)
            db_ref[...] += jnp.sum(half_acc[3], axis=0, keepdims=True)

    grev = lambda c, r: (nr - 1 - r, c)
    vrev = lambda c, r: (nr - 1 - r, CONV_NB + c)
    ghalo = lambda c, r: (jnp.maximum((nr - 1 - r) * tpb - 1, 0), c)
    vhalo = lambda c, r: (jnp.maximum((nr - 1 - r) * tpb - 1, 0), CONV_NB + c)
    colg = lambda c, r: (0, c)
    colv = lambda c, r: (0, CONV_NB + c)
    return pl.pallas_call(
        body, name="conv_bwd", grid=(CONV_NB, nr),
        in_specs=[pl.BlockSpec((tl, CONV_CB), grev), pl.BlockSpec((tl, CONV_CB), vrev),
                  pl.BlockSpec((SUBLANES, CONV_CB), ghalo), pl.BlockSpec((SUBLANES, CONV_CB), vhalo),
                  pl.BlockSpec((tl, CONV_CB), grev),
                  pl.BlockSpec((3, CONV_CB), colg), pl.BlockSpec((3, CONV_CB), colv),
                  pl.BlockSpec((1, CONV_CB), colg), pl.BlockSpec((1, CONV_CB), colv)],
        out_specs=[pl.BlockSpec((2, tl, CONV_CB), lambda c, r: (0, nr - 1 - r, c)),
                   pl.BlockSpec((3, CONV_CB), colg), pl.BlockSpec((3, CONV_CB), colg),
                   pl.BlockSpec((1, CONV_CB), colg), pl.BlockSpec((1, CONV_CB), colg)],
        out_shape=[jax.ShapeDtypeStruct((2, l, D_FF), BF16),
                   jax.ShapeDtypeStruct((3, D_FF), F32), jax.ShapeDtypeStruct((3, D_FF), F32),
                   jax.ShapeDtypeStruct((1, D_FF), F32), jax.ShapeDtypeStruct((1, D_FF), F32)],
        scratch_shapes=[pltpu.VMEM((2 * SUBLANES, CONV_CB), F32)],
        compiler_params=_params(("parallel", "arbitrary")),
    )(h, h, h, h, dact, cw, cw, cb, cb)


def _loss_head(ff, x1, tgt, g, tl):
    l = ff.shape[0]

    def body(ff_ref, x1_ref, tg_ref, g_ref, loss_ref, dx2_ref, dff_ref, dg_ref):
        @pl.when(pl.program_id(0) == 0)
        def _():
            loss_ref[...] = jnp.zeros_like(loss_ref)
            dg_ref[...] = jnp.zeros_like(dg_ref)

        f = ff_ref[...]
        gv = g_ref[...]
        n, _ = _rms(f, gv)
        e = x1_ref[...] + n - tg_ref[...]
        loss_ref[...] += 0.5 * jnp.sum(jnp.mean(e * e, axis=-1, keepdims=True), axis=0, keepdims=True)
        dx2 = e * (1.0 / D_MODEL)
        dx2_ref[...] = dx2
        dff, dg = _rms_bwd(dx2, f, gv)
        dff_ref[...] = dff.astype(BF16)
        dg_ref[...] += dg

    return pl.pallas_call(
        body, name="loss_head", grid=(l // tl,),
        in_specs=[_row(tl, D_MODEL), _row(tl, D_MODEL), _row(tl, D_MODEL), _const((1, D_MODEL))],
        out_specs=[_const((1, LANES)), _row(tl, D_MODEL), _row(tl, D_MODEL), _const((1, D_MODEL))],
        out_shape=[jax.ShapeDtypeStruct((1, LANES), F32), jax.ShapeDtypeStruct((l, D_MODEL), F32),
                   jax.ShapeDtypeStruct((l, D_MODEL), BF16), jax.ShapeDtypeStruct((1, D_MODEL), F32)],
        compiler_params=_params(("arbitrary",)),
    )(ff, x1, tgt, g)


def _ssm_disc(lam_re, lam_im, log_dt, b_re, b_im):
    dt = jnp.exp(log_dt)[:, None]
    mag = jnp.exp(lam_re * dt)
    ang = lam_im * dt
    a_re, a_im = mag * jnp.cos(ang), mag * jnp.sin(ang)
    den = lam_re * lam_re + lam_im * lam_im
    n_re, n_im = a_re - 1.0, a_im
    z_re = (n_re * lam_re + n_im * lam_im) / den
    z_im = (n_im * lam_re - n_re * lam_im) / den
    bb_re = z_re[..., None] * b_re - z_im[..., None] * b_im
    bb_im = z_re[..., None] * b_im + z_im[..., None] * b_re
    return a_re, a_im, bb_re, bb_im


_GPB = SSM_CB // SSM_P


def _embed_b(bb):
    t = bb.transpose(0, 2, 1).reshape(SSM_NB, _GPB, SSM_H, SSM_P)
    return jnp.einsum('mjhp,jk->mjhkp', t, jnp.eye(_GPB, dtype=bb.dtype)).reshape(SSM_NB, SSM_UB, SSM_CB)


def _extract_b(d):
    t = d.reshape(SSM_NB, _GPB, SSM_H, _GPB, SSM_P)
    t = jnp.einsum('mjhkp,jk->mjhp', t, jnp.eye(_GPB, dtype=d.dtype))
    return t.reshape(SSM_G, SSM_H, SSM_P).transpose(0, 2, 1)


def _embed_c(c):
    t = c.transpose(0, 2, 1).reshape(SSM_NB, _GPB, SSM_P, SSM_H)
    return jnp.einsum('mjph,jk->mjpkh', t, jnp.eye(_GPB, dtype=c.dtype)).reshape(SSM_NB, SSM_CB, SSM_UB)


def _extract_c(d):
    t = d.reshape(SSM_NB, _GPB, SSM_P, _GPB, SSM_H)
    t = jnp.einsum('mjpkh,jk->mjph', t, jnp.eye(_GPB, dtype=d.dtype))
    return t.reshape(SSM_G, SSM_P, SSM_H).transpose(0, 2, 1)


def _scan_tables(a_re, a_im, reverse):
    ar = a_re.reshape(1, SSM_CH)
    ai = (-a_im if reverse else a_im).reshape(1, SSM_CH)
    pr, pi = [ar], [ai]
    for _ in range(SUBLANES - 1):
        pr, pi = pr + [pr[-1] * ar - pi[-1] * ai], pi + [pr[-1] * ai + pi[-1] * ar]
    rows = jnp.arange(SUBLANES)[:, None]
    out = []
    for k in (1, 2, 4):
        valid = (rows + k <= SUBLANES - 1) if reverse else (rows >= k)
        out += [jnp.where(valid, pr[k - 1], 0.0), jnp.where(valid, pi[k - 1], 0.0)]
    order = list(range(SUBLANES - 1, -1, -1)) if reverse else list(range(SUBLANES))
    out += [jnp.concatenate([pr[n] for n in order], axis=0), jnp.concatenate([pi[n] for n in order], axis=0)]
    return jnp.stack(out).astype(F32)


def _pad_heads(w, d):
    lead = w.shape[:-1]
    w = w.reshape(lead + (N_HEADS, d))
    w = jnp.pad(w, [(0, 0)] * len(lead) + [(0, 0), (0, HEAD_SLOT - d)])
    return w.reshape(lead + (HP,))


def _unpad_heads(w, d):
    lead = w.shape[:-1]
    return w.reshape(lead + (N_HEADS, HEAD_SLOT))[..., :d].reshape(lead + (N_HEADS * d,))


def _chip_major(w, axis):
    k, n = w.shape
    if axis == 0:
        return w.reshape(N_CHIPS, k // N_CHIPS, n)
    return w.reshape(k, N_CHIPS, n // N_CHIPS).transpose(1, 0, 2)


def _from_chip_major(w, axis):
    if axis == 0:
        return w.reshape(-1, w.shape[2])
    return w.transpose(1, 0, 2).reshape(w.shape[1], -1)


def _pad_w_in(w):
    z = lambda n: jnp.zeros((w.shape[0], n), w.dtype)
    return jnp.concatenate([w[:, :640], z(KR_LANE), w[:, 640:672], z(HEAD_SLOT - KR_LANE - QK_ROPE), w[:, 672:]], axis=1)


def _unpad_w_in(w):
    return jnp.concatenate([w[:, :640], w[:, P_KR + KR_LANE:P_KR + KR_LANE + QK_ROPE], w[:, P_U:]], axis=1)


def _local_step(x, positions, tgt, wts, sp):
    l = x.shape[0]
    tl = min(256, l)
    ta = min(512, l)
    ts = min(2048, l)

    inv_freq = ROPE_THETA ** (-jnp.arange(0, QK_ROPE, 2, dtype=F32) / QK_ROPE)
    ang = positions.astype(F32)[:, None] * inv_freq
    cos, sin = jnp.cos(ang), jnp.sin(ang)
    one = jnp.ones((l, KR_LANE), F32)
    rc = jnp.concatenate([one, cos, cos, jnp.ones((l, HEAD_SLOT - KR_LANE - QK_ROPE), F32)], axis=1)
    rs = jnp.concatenate([0 * one, -sin, sin, jnp.zeros((l, HEAD_SLOT - KR_LANE - QK_ROPE), F32)], axis=1)

    win = _pad_w_in(wts["w_in"])
    wuq = _pad_heads(wts["w_uq"], QK_HEAD)
    wukv = jnp.concatenate([_pad_heads(wts["w_uk"], QK_NOPE), _pad_heads(wts["w_uv"], V_HEAD)], axis=1)

    disc_in = (sp["ssm_lambda_re"], sp["ssm_lambda_im"], sp["ssm_log_dt"], sp["ssm_b_re"], sp["ssm_b_im"])
    (a_re, a_im, bb_re, bb_im), disc_vjp = jax.vjp(_ssm_disc, *disc_in)
    bre, bim = _embed_b(bb_re).astype(BF16), _embed_b(bb_im).astype(BF16)
    cre, cim = _embed_c(sp["ssm_c_re"]).astype(BF16), _embed_c(sp["ssm_c_im"]).astype(BF16)
    dvec = sp["ssm_d"].reshape(1, SSM_W)
    tab_f = _scan_tables(a_re, a_im, False)
    tab_r = _scan_tables(a_re, a_im, True)

    g1, gq, gkv = sp["mix_norm_pre"], sp["q_norm"], sp["kv_norm"]
    gpost, gpre, gfin = sp["mix_norm_post"], sp["ffn_norm_pre"], sp["ffn_norm_post"]
    bgate, bglu, convb = sp["b_gate"], sp["b_glu"], sp["conv_b"]

    hn, cq, ckv, q, k, v, u, gl = _proj_fwd(x, g1, win, gq, wuq, gkv, wukv, rc, rs, bgate, tl)
    attn, lse = _attn_fwd(q, k, v, ta)
    y1, sre, sim = _ssm_fwd(u, bre, bim, cre, cim, dvec, tab_f, ts)
    wba = jnp.pad(wts["w_branch_attn"].reshape(N_HEADS, V_HEAD, D_MODEL),
                  ((0, 0), (0, HEAD_SLOT - V_HEAD), (0, 0))).reshape(HP, D_MODEL)
    wbs, wglu, wout = wts["w_branch_ssm"], wts["w_glu"], wts["w_out"]
    a, sm, merged, z, x1, hn2, y3 = _merge_fwd(x, gl, attn, y1, wba, wbs, wglu, bglu, wout, gpost, gpre, tl)
    late = wts["late"](x1)
    wup, wdown, convw = late["w_up"], late["w_down"], late["conv_w"]
    h = _mm(hn2, wup, "ffn_up")
    act = _conv_fwd(h, convw, convb, ts)
    ff = _mm(act, wdown, "ffn_down")
    loss, dx2, dff, dgfin = _loss_head(ff, x1, tgt, gfin, tl)

    dact = _mm(dff, wdown, "ffn_down_dx", out_dtype=BF16, bt=True)
    d_wdown = _mm_tn(act, dff, "ffn_down_dw", tk_cap=D_FF // 2)
    dh, dwg, dwv, dbg, dbv = _conv_bwd(h, dact, convw, convb, ts)
    d_convw = jnp.concatenate([dwg, dwv], axis=1)
    d_convb = jnp.concatenate([dbg, dbv], axis=1)
    dhn2 = _mm(dh, wup, "ffn_up_dx", bt=True)
    d_wup = _mm_tn(hn2, dh, "ffn_up_dw", chips=True)
    behind = wts["send_grads"]("ffn", {"w_up": d_wup, "w_down": _chip_major(d_wdown, 0)})
    (dx1, dz, dbra, dbrs, dgl, dattn, dy1, dt, y2, dgpre, dgpost, dbgate, dbglu) = _merge_bwd(
        dhn2, x1, dx2, z, gl, a, sm, y1, wba, wbs, wglu, bglu, wout, gpost, gpre + behind, tl)
    d_wout = _mm_tn(merged, dz, "w_out_dw")
    d_wba = _mm_tn(attn, dbra, "w_branch_attn_dw", chips=True)
    d_wbs = _mm_tn(y3, dbrs, "w_branch_ssm_dw", chips=True)
    d_wglu = _mm_tn(y2, dt, "w_glu_dw")
    ncol = D_MODEL // N_CHIPS
    behind = wts["send_grads"]("mix", {
        "w_glu": _chip_major(d_wglu, 0),
        "w_branch_attn": d_wba.reshape(N_CHIPS, N_HEADS, HEAD_SLOT, ncol)[:, :, :V_HEAD].reshape(
            N_CHIPS, N_HEADS * V_HEAD, ncol),
        "w_branch_ssm": d_wbs,
        "w_out": _chip_major(d_wout, 0)}, after=d_wglu)
    dq, dk, dv = _attn_bwd(q, k, v, dattn, lse + behind, _attn_delta(attn, dattn, min(2048, l)), ta)
    du, dbre, dbim, dcre, dcim, dare, daim, dd = _ssm_bwd(dy1, u, sre, sim, bre, bim, cre, cim, dvec, tab_r, ts)
    behind = wts["send_grads"]("none", {}, after=du)
    gx, dql, qn, ckvn, dproj, dg1, dgq, dgkv = _proj_bwd(
        x, dx1, cq, ckv, dq, dk, dv, du, dgl, g1 + behind, win, gq, wuq, gkv, wukv, rc, rs, tl)
    d_win = _mm_tn(hn, dproj, "w_in_dw")
    d_wuq = _mm_tn(qn, dql, "w_uq_dw")
    d_wuk = _mm_tn(ckvn, dk, "w_uk_dw")
    d_wuv = _mm_tn(ckvn, dv, "w_uv_dw")

    d_lre, d_lim, d_ldt, d_bre, d_bim = disc_vjp((dare.reshape(SSM_G, SSM_P), daim.reshape(SSM_G, SSM_P),
                                                  _extract_b(dbre), _extract_b(dbim)))
    big = {
        "w_in": _chip_major(_unpad_w_in(d_win), 1),
        "w_uq": _chip_major(_unpad_heads(d_wuq, QK_HEAD), 1),
        "w_uk": _chip_major(_unpad_heads(d_wuk, QK_NOPE), 1),
        "w_uv": _chip_major(_unpad_heads(d_wuv, V_HEAD), 1),
    }
    small = {
        "conv_w": d_convw,
        "mix_norm_pre": dg1, "q_norm": dgq, "kv_norm": dgkv,
        "ssm_lambda_re": d_lre, "ssm_lambda_im": d_lim, "ssm_log_dt": d_ldt,
        "ssm_b_re": d_bre, "ssm_b_im": d_bim,
        "ssm_c_re": _extract_c(dcre), "ssm_c_im": _extract_c(dcim),
        "ssm_d": dd.reshape(SSM_G, SSM_H), "b_glu": dbglu, "b_gate": dbgate,
        "mix_norm_post": dgpost, "ffn_norm_pre": dgpre, "conv_b": d_convb, "ffn_norm_post": dgfin,
    }
    return loss[0, 0], gx, big, small


_ANY = pl.BlockSpec(memory_space=pl.ANY)


ROW_TILE = 16


def _place():
    x, y, c = lax.axis_index("x"), lax.axis_index("y"), lax.axis_index("c")
    return x, y, c, 2 * x + y, [(1 - x, y), (x, 1 - y), (1 - x, 1 - y)]


def _half(rows, which):
    hr = rows // 2
    return pl.ds(pl.multiple_of(which * hr, ROW_TILE), hr)


def _remote(src, dst, send_sems, recv_sems, n, dev):
    return pltpu.make_async_remote_copy(src_ref=src, dst_ref=dst, send_sem=send_sems.at[n], recv_sem=recv_sems.at[n],
                                        device_id=dev, device_id_type=MESH)


def _gather_big(shards):
    nw = len(shards)
    rows = [s.shape[0] for s in shards]

    def body(*refs):
        ins, outs = refs[:nw], refs[nw:2 * nw]
        ici_send, ici_recv, d2d_send, d2d_recv = refs[2 * nw:]
        x, y, c, me, peers = _place()
        sent = []
        for i in range(nw):
            for p, (px, py) in enumerate(peers):
                cp = _remote(ins[i].at[_half(rows[i], c)], outs[i].at[me, _half(rows[i], c)], ici_send, ici_recv,
                             3 * i + p, (px, py, c))
                cp.start()
                sent.append(cp)
        for p, (px, py) in enumerate(peers):
            for i in range(nw):
                blk = outs[i].at[2 * px + py, _half(rows[i], c)]
                _remote(blk, blk, ici_send, ici_recv, 3 * i + p, (px, py, c)).wait_recv()
                cp = _remote(blk, blk, d2d_send, d2d_recv, 3 * i + p, (x, y, 1 - c))
                cp.start()
                sent.append(cp)
        for p, (px, py) in enumerate(peers):
            for i in range(nw):
                blk = outs[i].at[2 * px + py, _half(rows[i], 1 - c)]
                _remote(blk, blk, d2d_send, d2d_recv, 3 * i + p, (x, y, 1 - c)).wait_recv()
        for cp in sent:
            cp.wait_send()

    dma = pltpu.SemaphoreType.DMA
    return pl.pallas_call(
        body, name="gather_weights", in_specs=[_ANY] * nw, out_specs=[_ANY] * nw,
        out_shape=[jax.ShapeDtypeStruct((N_CHIPS,) + s.shape, s.dtype) for s in shards],
        scratch_shapes=[dma((3 * nw,)), dma((3 * nw,)), dma((3 * nw,)), dma((3 * nw,))],
    )(*shards)


_HBM = pl.BlockSpec(memory_space=pltpu.HBM)
_SEM = pl.BlockSpec(memory_space=pltpu.SEMAPHORE)
_DATAFLOW = pltpu.SideEffectType.DATAFLOW_SIDE_EFFECTING


def _exchange_start(shards, name, scatter):
    nw = len(shards)
    lands = [lax.empty(s.shape if scatter else (N_CHIPS,) + s.shape, s.dtype) for s in shards]

    def body(*refs):
        ins, zones = refs[:nw], refs[nw:2 * nw]
        send_sems, recv_sems, token = refs[2 * nw], refs[2 * nw + 1], refs[-1]
        x, y, c, me, peers = _place()
        for i in range(nw):
            for p, (px, py) in enumerate(peers):
                src = ins[i].at[2 * px + py] if scatter else ins[i]
                _remote(src, zones[i].at[me], send_sems, recv_sems, 3 * i + p, (px, py, c)).start()
        token[...] = jnp.zeros_like(token)

    thru = [pltpu.HBM(a.shape, a.dtype) for a in list(shards) + lands]
    dma = pltpu.SemaphoreType.DMA
    outs = pl.pallas_call(
        body, name=name,
        out_shape=(dma((3 * nw,)), dma((3 * nw,)), *thru, jax.ShapeDtypeStruct((SUBLANES, LANES), F32)),
        in_specs=[_HBM] * (2 * nw),
        out_specs=(_SEM, _SEM, *([_HBM] * (2 * nw)), pl.BlockSpec(memory_space=pltpu.VMEM)),
        input_output_aliases={i: 2 + i for i in range(2 * nw)},
        compiler_params=pltpu.CompilerParams(has_side_effects=_DATAFLOW),
    )(*[pltpu.with_memory_space_constraint(a, pltpu.HBM) for a in list(shards) + lands])
    return outs[0], outs[1], list(outs[2:2 + nw]), list(outs[2 + nw:2 + 2 * nw]), outs[-1]


def _exchange_wait(send_sems, recv_sems, shards, lands, after, name, scatter):
    nw = len(shards)

    def body(*refs):
        ins, zones = refs[:nw], refs[nw:2 * nw]
        send_sems, recv_sems = refs[2 * nw], refs[2 * nw + 1]
        x, y, c, me, peers = _place()
        for i in range(nw):
            for p, (px, py) in enumerate(peers):
                src = ins[i].at[2 * px + py] if scatter else ins[i]
                cp = _remote(src, zones[i].at[2 * px + py], send_sems, recv_sems, 3 * i + p, (px, py, c))
                cp.wait_send()
                cp.wait_recv()

    both = list(shards) + list(lands)
    outs = pl.pallas_call(
        body, name=name,
        out_shape=tuple(pltpu.HBM(a.shape, a.dtype) for a in both),
        in_specs=(*([_HBM] * (2 * nw)), _SEM, _SEM, _ANY), out_specs=[_HBM] * (2 * nw),
        input_output_aliases={i: i for i in range(2 * nw)},
        compiler_params=pltpu.CompilerParams(has_side_effects=_DATAFLOW),
    )(*both, send_sems, recv_sems, after)
    return list(outs[:nw]), list(outs[nw:])


def _sibling_start(grads, name):
    nw = len(grads)
    lands = [lax.empty((N_CHIPS, g.shape[1] // 2, g.shape[2]), g.dtype) for g in grads]

    def body(*refs):
        ins, zones = refs[:nw], refs[nw:2 * nw]
        send_sems, recv_sems, token = refs[2 * nw], refs[2 * nw + 1], refs[-1]
        x, y, c, _, _ = _place()
        for i in range(nw):
            _remote(ins[i].at[pl.ds(0, N_CHIPS), _half(grads[i].shape[1], 1 - c)], zones[i], send_sems, recv_sems,
                    i, (x, y, 1 - c)).start()
        token[...] = jnp.zeros_like(token)

    thru = [pltpu.HBM(a.shape, a.dtype) for a in list(grads) + lands]
    dma = pltpu.SemaphoreType.DMA
    outs = pl.pallas_call(
        body, name=name,
        out_shape=(dma((nw,)), dma((nw,)), *thru, jax.ShapeDtypeStruct((SUBLANES, LANES), F32)),
        in_specs=[_HBM] * (2 * nw),
        out_specs=(_SEM, _SEM, *([_HBM] * (2 * nw)), pl.BlockSpec(memory_space=pltpu.VMEM)),
        input_output_aliases={i: 2 + i for i in range(2 * nw)},
        compiler_params=pltpu.CompilerParams(has_side_effects=_DATAFLOW),
    )(*[pltpu.with_memory_space_constraint(a, pltpu.HBM) for a in list(grads) + lands])
    return outs[0], outs[1], list(outs[2:2 + nw]), list(outs[2 + nw:2 + 2 * nw]), outs[-1]


def _sibling_wait(send_sems, recv_sems, grads, lands, after, name):
    nw = len(grads)

    def body(*refs):
        ins, zones = refs[:nw], refs[nw:2 * nw]
        send_sems, recv_sems = refs[2 * nw], refs[2 * nw + 1]
        x, y, c, _, _ = _place()
        for i in range(nw):
            cp = _remote(ins[i].at[pl.ds(0, N_CHIPS), _half(grads[i].shape[1], 1 - c)], zones[i], send_sems, recv_sems,
                         i, (x, y, 1 - c))
            cp.wait_send()
            cp.wait_recv()

    both = list(grads) + list(lands)
    outs = pl.pallas_call(
        body, name=name,
        out_shape=tuple(pltpu.HBM(a.shape, a.dtype) for a in both),
        in_specs=(*([_HBM] * (2 * nw)), _SEM, _SEM, _ANY), out_specs=[_HBM] * (2 * nw),
        input_output_aliases={i: i for i in range(2 * nw)},
        compiler_params=pltpu.CompilerParams(has_side_effects=_DATAFLOW),
    )(*both, send_sems, recv_sems, after)
    return list(outs[:nw]), list(outs[nw:])


def _reduce_to_sibling(grads, name):
    nw = len(grads)

    def body(*refs):
        ins, outs = refs[:nw], refs[nw:2 * nw]
        send_sems, recv_sems = refs[2 * nw:]
        x, y, c, _, _ = _place()
        sent = []
        for i in range(nw):
            cp = _remote(ins[i].at[pl.ds(0, N_CHIPS), _half(grads[i].shape[1], 1 - c)], outs[i], send_sems, recv_sems,
                         i, (x, y, 1 - c))
            cp.start()
            sent.append(cp)
        for cp in sent:
            cp.wait()

    dma = pltpu.SemaphoreType.DMA
    return pl.pallas_call(
        body, name=name, in_specs=[_ANY] * nw, out_specs=[_ANY] * nw,
        out_shape=[jax.ShapeDtypeStruct((N_CHIPS, g.shape[1] // 2, g.shape[2]), g.dtype) for g in grads],
        scratch_shapes=[dma((nw,)), dma((nw,))],
    )(*grads)


def _reduce_back(totals, name):
    nw = len(totals)

    def body(*refs):
        outs = refs[nw:2 * nw]
        send_sems, recv_sems = refs[2 * nw:]
        x, y, c, _, _ = _place()
        sent = []
        for i in range(nw):
            blk = outs[i].at[_half(totals[i].shape[0], c)]
            cp = _remote(blk, blk, send_sems, recv_sems, i, (x, y, 1 - c))
            cp.start()
            sent.append(cp)
        for i in range(nw):
            blk = outs[i].at[_half(totals[i].shape[0], 1 - c)]
            _remote(blk, blk, send_sems, recv_sems, i, (x, y, 1 - c)).wait_recv()
        for cp in sent:
            cp.wait_send()

    dma = pltpu.SemaphoreType.DMA
    return pl.pallas_call(
        body, name=name, in_specs=[_ANY] * nw, out_specs=[_ANY] * nw,
        out_shape=[jax.ShapeDtypeStruct(t.shape, t.dtype) for t in totals],
        input_output_aliases={i: i for i in range(nw)},
        scratch_shapes=[dma((nw,)), dma((nw,))],
    )(*totals)


def _all_reduce_small(v, name):
    rows, w = v.shape
    hr = rows // 2
    assert hr % SUBLANES == 0

    def body(v_ref, out_ref, sib_ref, half_ref, chips_ref, send_sems, recv_sems):
        x, y, c, me, peers = _place()
        sibling = (x, y, 1 - c)
        mine = pl.ds(pl.multiple_of(c * hr, SUBLANES), hr)
        other = pl.ds(pl.multiple_of((1 - c) * hr, SUBLANES), hr)
        cp = _remote(v_ref, sib_ref, send_sems, recv_sems, 0, sibling)
        cp.start()
        cp.wait()
        half_ref[...] = v_ref[mine, :] + sib_ref[mine, :]
        sent = []
        for p, (px, py) in enumerate(peers):
            cp = _remote(half_ref, chips_ref.at[me], send_sems, recv_sems, 1 + p, (px, py, c))
            cp.start()
            sent.append(cp)
        chips_ref[me] = half_ref[...]
        for p, (px, py) in enumerate(peers):
            _remote(half_ref, chips_ref.at[2 * px + py], send_sems, recv_sems, 1 + p, (px, py, c)).wait_recv()
        for cp in sent:
            cp.wait_send()
        out_ref[mine, :] = ((chips_ref[0] + chips_ref[1]) + chips_ref[2]) + chips_ref[3]
        cp = _remote(out_ref.at[mine], out_ref.at[mine], send_sems, recv_sems, 4, sibling)
        cp.start()
        _remote(out_ref.at[other], out_ref.at[other], send_sems, recv_sems, 4, sibling).wait_recv()
        cp.wait_send()

    vm = pl.BlockSpec(memory_space=pltpu.VMEM)
    return pl.pallas_call(
        body, name=name, in_specs=[vm], out_specs=vm,
        out_shape=jax.ShapeDtypeStruct((rows, w), F32),
        scratch_shapes=[pltpu.VMEM((rows, w), F32), pltpu.VMEM((hr, w), F32), pltpu.VMEM((N_CHIPS, hr, w), F32),
                        pltpu.SemaphoreType.DMA((5,)), pltpu.SemaphoreType.DMA((5,))],
        compiler_params=pltpu.CompilerParams(vmem_limit_bytes=VMEM_LIMIT),
    )(v)


ELEMENTWISE_BLOCK = 512 * 1024


def _rows_tile(rows, cols):
    best = None
    for t in range(SUBLANES, rows + 1, SUBLANES):
        if rows % t == 0 and t * cols <= ELEMENTWISE_BLOCK:
            best = t
    return rows if best is None else best


def _add_pair(g, t, core, name):
    nb, n, w = t.shape
    tr = _rows_tile(n, w)
    steps = n // tr

    def body(core_ref, g_ref, t_ref, o_ref):
        o_ref[...] = (g_ref[...] + t_ref[...]).astype(BF16)

    spec = pl.BlockSpec((1, tr, w), lambda j, i, core_ref: (j, i, 0))
    return pl.pallas_call(
        body, name=name,
        grid_spec=pltpu.PrefetchScalarGridSpec(
            num_scalar_prefetch=1, grid=(nb, steps),
            in_specs=[pl.BlockSpec((1, tr, w), lambda j, i, core_ref: (j, core_ref[0] * steps + i, 0)), spec],
            out_specs=spec),
        out_shape=jax.ShapeDtypeStruct(t.shape, BF16),
        compiler_params=_params(("parallel", "parallel")))(core, g, t)


def _add_chips(landed, pairs, place, name):
    nb, n, w = landed.shape
    tr = _rows_tile(n, w)
    steps = n // tr

    def body(place_ref, r_ref, own_ref, o_ref):
        me = place_ref[0]
        acc = None
        for k in range(nb):
            blk = jnp.where(me == k, own_ref[0], r_ref[k]).astype(F32)
            acc = blk if acc is None else acc + blk
        o_ref[...] = acc

    return pl.pallas_call(
        body, name=name,
        grid_spec=pltpu.PrefetchScalarGridSpec(
            num_scalar_prefetch=1, grid=(steps,),
            in_specs=[pl.BlockSpec((nb, tr, w), lambda i, place_ref: (0, i, 0)),
                      pl.BlockSpec((1, tr, w), lambda i, place_ref: (place_ref[0], i, 0))],
            out_specs=pl.BlockSpec((tr, w), lambda i, place_ref: (place_ref[1] * steps + i, 0))),
        out_shape=jax.ShapeDtypeStruct((2 * n, w), F32),
        compiler_params=_params(("parallel",)))(place, landed, pairs)


def _adamw(w, g, m, v, name):
    rows, wd = w.shape
    tr = _rows_tile(rows, wd)
    c1 = 1.0 - ADAM_B1 ** ADAM_STEP
    c2 = 1.0 - ADAM_B2 ** ADAM_STEP

    def body(w_ref, g_ref, m_ref, v_ref, go_ref, d_ref, mo_ref, vo_ref):
        gv = g_ref[...]
        go_ref[...] = gv
        m2 = ADAM_B1 * m_ref[...] + (1.0 - ADAM_B1) * gv
        v2 = ADAM_B2 * v_ref[...] + (1.0 - ADAM_B2) * (gv * gv)
        mo_ref[...] = m2
        vo_ref[...] = v2
        d_ref[...] = -ADAM_LR * ((m2 / c1) / (jnp.sqrt(v2 / c2) + ADAM_EPS) + ADAM_WD * w_ref[...])

    spec = pl.BlockSpec((tr, wd), lambda i: (i, 0))
    shp = jax.ShapeDtypeStruct((rows, wd), F32)
    return pl.pallas_call(body, name=name, grid=(rows // tr,), in_specs=[spec] * 4, out_specs=[spec] * 4,
                          out_shape=[shp] * 4, compiler_params=_params(("parallel",)))(w, g, m, v)


BIG = [("w_in", (1024, 3232), 1), ("w_uq", (384, 768), 1), ("w_uk", (256, 512), 1), ("w_uv", (256, 512), 1),
       ("w_glu", (512, 512), 0), ("w_branch_attn", (512, 1024), 1), ("w_branch_ssm", (512, 1024), 1),
       ("w_out", (1024, 1024), 0), ("w_up", (1024, 5632), 1), ("conv_w", (3, 5632), 1), ("w_down", (2816, 1024), 0)]
SMALL = [("mix_norm_pre", (1024,)), ("q_norm", (384,)), ("kv_norm", (256,)), ("ssm_lambda_re", (32, 64)),
         ("ssm_lambda_im", (32, 64)), ("ssm_log_dt", (32,)), ("ssm_b_re", (32, 64, 16)), ("ssm_b_im", (32, 64, 16)),
         ("ssm_c_re", (32, 16, 64)), ("ssm_c_im", (32, 16, 64)), ("ssm_d", (32, 16)), ("b_glu", (512,)),
         ("b_gate", (2048,)), ("mix_norm_post", (1024,)), ("ffn_norm_pre", (1024,)), ("conv_b", (5632,)),
         ("ffn_norm_post", (1024,))]
MATMUL_W = [b for b in BIG if b[0] != "conv_w"]
LATE_W = ("w_up", "w_down", "conv_w")
CONV_W_SHAPE = (3, 2 * D_FF)
CONV_W_SHARD = (3, 2 * D_FF // N_CHIPS)
SMALL_SUM = [("loss", (1,))] + SMALL + [("conv_w", CONV_W_SHAPE)]
SMALL_ADAM = SMALL + [("conv_w", CONV_W_SHARD)]


def _pack_flat(layout, vals):
    flat = jnp.concatenate([vals[n].astype(F32).reshape(-1) for n, _ in layout])
    rows = -(-(-(-flat.shape[0] // FLAT_W)) // (2 * SUBLANES)) * 2 * SUBLANES
    return jnp.pad(flat, (0, rows * FLAT_W - flat.shape[0])).reshape(rows, FLAT_W)


def _unpack_flat(layout, flat):
    flat = flat.reshape(-1)
    out = {}
    o = 0
    for name, shape in layout:
        n = math.prod(shape)
        out[name] = flat[o:o + n].reshape(shape)
        o += n
    return out


_ARG_NAMES = ["x", "positions"] + [n for n in (
    "mix_norm_pre", "w_in", "q_norm", "w_uq", "kv_norm", "w_uk", "w_uv", "ssm_lambda_re", "ssm_lambda_im", "ssm_log_dt",
    "ssm_b_re", "ssm_b_im", "ssm_c_re", "ssm_c_im", "ssm_d", "w_glu", "b_glu", "w_branch_attn", "w_branch_ssm",
    "b_gate", "w_out", "mix_norm_post", "ffn_norm_pre", "w_up", "conv_w", "conv_b", "w_down", "ffn_norm_post")]
_WEIGHTS = _ARG_NAMES[2:]


def _gather_weights(w):
    early = [b for b in MATMUL_W if b[0] not in LATE_W]
    late = [b for b in BIG if b[0] in LATE_W]
    own = (jnp.arange(N_CHIPS) == 2 * lax.axis_index("x") + lax.axis_index("y"))[:, None, None]

    def whole(layout, mine, gathered):
        return {name: _from_chip_major(jnp.where(own, s[None], g), axis)
                for (name, _, axis), s, g in zip(layout, mine, gathered)}

    mine = [w[name].astype(BF16) for name, _, _ in early]
    gathered = _gather_big(mine)
    full = whole(early, mine, gathered)
    mine_late = [w[name].astype(F32 if name == "conv_w" else BF16) for name, _, _ in late]
    _, mine_late = lax.optimization_barrier((gathered[0], mine_late))
    send_sems, recv_sems, shards_thru, lands_thru, token = _exchange_start(mine_late, "gather_late_start", scatter=False)

    def late_weights(after):
        shards, lands = _exchange_wait(send_sems, recv_sems, shards_thru, lands_thru, after, "gather_late_wait",
                                       scatter=False)
        return whole(late, shards, lands)

    full["late"] = late_weights
    full["token"] = token[0, 0]
    return full


def _pair_sums(names, grads, tag):
    core = lax.axis_index("c").astype(jnp.int32).reshape(1)
    theirs = _reduce_to_sibling(grads, "reduce_grads_d2d" + tag)
    return [_add_pair(g, t, core, "reduce_pair_" + n) for n, g, t in zip(names, grads, theirs)]


def _send_grads(tag, grads, after, flying, pending):
    token = jnp.zeros((), F32)
    if flying:
        tag0, names0, state0 = flying.pop()
        core = lax.axis_index("c").astype(jnp.int32).reshape(1)
        mine, theirs = _sibling_wait(*state0, after, "reduce_" + tag0 + "_d2d_wait")
        pairs = [_add_pair(g, t, core, "reduce_pair_" + n) for n, g, t in zip(names0, mine, theirs)]
        send_sems, recv_sems, pairs_thru, lands_thru, tok = _exchange_start(pairs, "reduce_" + tag0 + "_start",
                                                                          scatter=True)
        pending.append((tag0, names0, send_sems, recv_sems, pairs_thru, lands_thru))
        token = token + tok[0, 0]
    if grads:
        names = list(grads)
        send_sems, recv_sems, grads_thru, lands_thru, tok = _sibling_start([grads[n] for n in names],
                                                                          "reduce_" + tag + "_d2d_start")
        flying.append((tag, names, (send_sems, recv_sems, grads_thru, lands_thru)))
        token = token + tok[0, 0]
    return token


def _reduce_grads(gbig, pending, loss, gsmall, use_sent):
    core = lax.axis_index("c").astype(jnp.int32).reshape(1)
    chip = (2 * lax.axis_index("x") + lax.axis_index("y")).astype(jnp.int32).reshape(1)
    place = jnp.concatenate([chip, core])

    def finish(names, pairs, landed, name):
        totals = [_add_chips(r, p, place, "reduce_chips_" + n) for n, r, p in zip(names, landed, pairs)]
        return dict(zip(names, _reduce_back(totals, name)))

    names = list(gbig)
    pairs = _pair_sums(names, [gbig[n] for n in names], "")
    send_sems, recv_sems, pairs_thru, lands_thru, token = _exchange_start(pairs, "reduce_last_start", scatter=True)
    sent_names, sent_pairs, sent_landed = [], [], []
    for tag, group, g_send, g_recv, g_pairs, g_lands in pending:
        got_pairs, got_landed = _exchange_wait(g_send, g_recv, g_pairs, g_lands, token, "reduce_" + tag + "_wait",
                                               scatter=True)
        sent_names, sent_pairs, sent_landed = sent_names + group, sent_pairs + got_pairs, sent_landed + got_landed
    g_sent = finish(sent_names, sent_pairs, sent_landed, "reduce_back_sent")
    vals = dict(gsmall)
    vals["loss"] = loss
    small_red = _unpack_flat(SMALL_SUM, _all_reduce_small(_pack_flat(SMALL_SUM, vals), "reduce_small"))
    after = use_sent(g_sent, small_red)
    pairs, landed = _exchange_wait(send_sems, recv_sems, pairs_thru, lands_thru, after, "reduce_last_wait", scatter=True)
    return finish(names, pairs, landed, "reduce_back_last"), small_red


def _step(args):
    x = args["x"][0]
    positions = args["positions"][0]
    tgt = args["loss_target"][0]
    w = {n: args[n][0] for n in _WEIGHTS}
    m = {n: args["m_" + n][0] for n in _WEIGHTS}
    v = {n: args["v_" + n][0] for n in _WEIGHTS}

    full = _gather_weights(w)
    sp = {n: w[n].reshape(s) for n, s in SMALL}
    for n in ("mix_norm_pre", "q_norm", "kv_norm", "b_glu", "b_gate", "mix_norm_post", "ffn_norm_pre", "conv_b",
              "ffn_norm_post"):
        sp[n] = sp[n].reshape(1, -1)
    sp["mix_norm_pre"] = sp["mix_norm_pre"] + full.pop("token")
    pending, flying = [], []
    full["send_grads"] = lambda tag, grads, after=None: _send_grads(tag, grads, after, flying, pending)
    loss, gx, gbig, gsmall = _local_step(x, positions, tgt, full, sp)
    outs = {}

    def adam_big(g_red):
        for name in g_red:
            g2, d, m2, v2 = _adamw(w[name], g_red[name], m[name], v[name], "adamw_" + name)
            outs["grad_" + name], outs["delta_" + name], outs["new_m_" + name], outs["new_v_" + name] = g2, d, m2, v2
        return v2

    def use_sent(g_sent, small_red):
        chip = 2 * lax.axis_index("x") + lax.axis_index("y")
        grads = dict(small_red)
        grads["conv_w"] = lax.dynamic_slice_in_dim(small_red["conv_w"], chip * CONV_W_SHARD[1], CONV_W_SHARD[1], axis=1)
        outs.update({"grad_" + n: grads[n] for n, _ in SMALL_ADAM})
        _, d_sm, m_sm, v_sm = _adamw(_pack_flat(SMALL_ADAM, w), _pack_flat(SMALL_ADAM, grads),
                                     _pack_flat(SMALL_ADAM, m), _pack_flat(SMALL_ADAM, v), "adamw_small")
        for prefix, flat in (("delta_", d_sm), ("new_m_", m_sm), ("new_v_", v_sm)):
            for n, val in _unpack_flat(SMALL_ADAM, flat).items():
                outs[prefix + n] = val
        return adam_big(g_sent)

    g_last, small_red = _reduce_grads(gbig, pending, loss, gsmall, use_sent)
    adam_big(g_last)
    outs = {n: val.reshape(args[n.split("_", 1)[1] if not n.startswith("new_") else n[6:]].shape)
            for n, val in outs.items()}
    res = [small_red["loss"][0], gx[None]]
    for prefix in ("grad_", "delta_", "new_m_", "new_v_"):
        res += [outs[prefix + n] for n in _WEIGHTS]
    return tuple(res)


def kernel(x, positions, mix_norm_pre, w_in, q_norm, w_uq, kv_norm, w_uk, w_uv, ssm_lambda_re, ssm_lambda_im, ssm_log_dt, ssm_b_re, ssm_b_im, ssm_c_re, ssm_c_im, ssm_d, w_glu, b_glu, w_branch_attn, w_branch_ssm, b_gate, w_out, mix_norm_post, ffn_norm_pre, w_up, conv_w, conv_b, w_down, ffn_norm_post, loss_target, m_mix_norm_pre, m_w_in, m_q_norm, m_w_uq, m_kv_norm, m_w_uk, m_w_uv, m_ssm_lambda_re, m_ssm_lambda_im, m_ssm_log_dt, m_ssm_b_re, m_ssm_b_im, m_ssm_c_re, m_ssm_c_im, m_ssm_d, m_w_glu, m_b_glu, m_w_branch_attn, m_w_branch_ssm, m_b_gate, m_w_out, m_mix_norm_post, m_ffn_norm_pre, m_w_up, m_conv_w, m_conv_b, m_w_down, m_ffn_norm_post, v_mix_norm_pre, v_w_in, v_q_norm, v_w_uq, v_kv_norm, v_w_uk, v_w_uv, v_ssm_lambda_re, v_ssm_lambda_im, v_ssm_log_dt, v_ssm_b_re, v_ssm_b_im, v_ssm_c_re, v_ssm_c_im, v_ssm_d, v_w_glu, v_b_glu, v_w_branch_attn, v_w_branch_ssm, v_b_gate, v_w_out, v_mix_norm_post, v_ffn_norm_pre, v_w_up, v_conv_w, v_conv_b, v_w_down, v_ffn_norm_post):
    given = dict(locals())
    return _step(given)
```

```python
import math

import jax
import jax.numpy as jnp
from jax import lax
from jax.experimental import pallas as pl
from jax.experimental.pallas import tpu as pltpu

F32 = jnp.float32
BF16 = jnp.bfloat16
MESH = pl.DeviceIdType.MESH

D_MODEL = 1024
N_HEADS = 8
QK_NOPE = 64
QK_ROPE = 32
QK_HEAD = QK_NOPE + QK_ROPE
V_HEAD = 64
Q_RANK = 384
KV_RANK = 256
ROPE_THETA = 10000.0
SSM_W = 512
SSM_H = 16
SSM_G = 32
SSM_P = 64
SSM_CH = SSM_G * SSM_P
D_FF = 2816
EPS = 1e-6
ADAM_LR = 0.001
ADAM_B1 = 0.9
ADAM_B2 = 0.999
ADAM_EPS = 1e-08
ADAM_WD = 0.01
ADAM_STEP = 10

LANES = 128
SUBLANES = 8
VMEM_LIMIT = 56 * 1024 * 1024

HEAD_SLOT = LANES
HP = N_HEADS * HEAD_SLOT
P_CQ, P_CKV, P_KR, P_U, P_GL, P_END = 0, 384, 640, 768, 1280, 3328
KR_LANE = 64

FLAT_W = 1024
N_CHIPS = 4


def _tile(n, cap):
    if n <= cap:
        return n
    best = None
    for t in range(LANES, cap + 1, LANES):
        if n % t == 0:
            best = t
    assert best is not None, (n, cap)
    return best


def _params(sem):
    return pltpu.CompilerParams(dimension_semantics=sem, vmem_limit_bytes=VMEM_LIMIT)


def _dot(a, b):
    return jnp.dot(a, b, preferred_element_type=F32)


def _dot_nt(a, b):
    return lax.dot_general(a, b, (((1,), (1,)), ((), ())), preferred_element_type=F32)


def _dot_tn(a, b):
    return lax.dot_general(a, b, (((0,), (0,)), ((), ())), preferred_element_type=F32)


def _rms(x, g):
    r = lax.rsqrt(jnp.mean(x * x, axis=-1, keepdims=True) + EPS)
    return x * r * g, r


def _rms_bwd(dy, x, g):
    r = lax.rsqrt(jnp.mean(x * x, axis=-1, keepdims=True) + EPS)
    dyg = dy * g
    dx = r * dyg - x * (r * r * r) * jnp.mean(dyg * x, axis=-1, keepdims=True)
    dg = jnp.sum(dy * x * r, axis=0, keepdims=True)
    return dx, dg


_GELU_K0 = math.sqrt(2.0 / math.pi)
_GELU_K1 = 0.044715


def _gelu(x):
    th = jnp.tanh(_GELU_K0 * (x + _GELU_K1 * x * x * x))
    return 0.5 * x * (1.0 + th)


def _gelu_grad(x):
    th = jnp.tanh(_GELU_K0 * (x + _GELU_K1 * x * x * x))
    return 0.5 * (1.0 + th) + 0.5 * x * (1.0 - th * th) * _GELU_K0 * (1.0 + 3.0 * _GELU_K1 * x * x)


def _sigmoid(x):
    return 1.0 / (1.0 + jnp.exp(-x))


def _rope(q, c, s):
    n = q.shape[1]
    lane = lax.broadcasted_iota(jnp.int32, q.shape, 1) % HEAD_SLOT
    sw = jnp.where(lane < KR_LANE + QK_ROPE // 2, pltpu.roll(q, n - QK_ROPE // 2, 1), pltpu.roll(q, QK_ROPE // 2, 1))
    return q * c + sw * s


def _rope_bwd(dy, c, s):
    n = dy.shape[1]
    t = dy * s
    lane = lax.broadcasted_iota(jnp.int32, dy.shape, 1) % HEAD_SLOT
    sw = jnp.where(lane < KR_LANE + QK_ROPE // 2, pltpu.roll(t, n - QK_ROPE // 2, 1), pltpu.roll(t, QK_ROPE // 2, 1))
    rope_lane = jnp.logical_and(lane >= KR_LANE, lane < KR_LANE + QK_ROPE)
    return dy * c + jnp.where(rope_lane, sw, 0.0)


def _shift_down(x, k, halo):
    xs = pltpu.roll(x, k, 0)
    hs = pltpu.roll(halo, k, 0)
    rows = lax.broadcasted_iota(jnp.int32, halo.shape, 0)
    top = jnp.where(rows < k, hs, xs[0:SUBLANES])
    return jnp.concatenate([top, xs[SUBLANES:]], axis=0)


def _shift_up(x, k, halo):
    t = x.shape[0]
    xs = pltpu.roll(x, t - k, 0)
    hs = pltpu.roll(halo, SUBLANES - k, 0)
    rows = lax.broadcasted_iota(jnp.int32, halo.shape, 0)
    bot = jnp.where(rows >= SUBLANES - k, hs, xs[t - SUBLANES:])
    return jnp.concatenate([xs[:t - SUBLANES], bot], axis=0)


def _mm(a, b, name, out_dtype=F32, bt=False, tm_cap=1024, tn_cap=1408):
    m, k = a.shape[-2:]
    parts = a.shape[0] if a.ndim == 3 else 1
    n = b.shape[0] if bt else b.shape[1]
    tm = min(tm_cap, m)
    tn = _tile(n, tn_cap)

    def body(a_ref, b_ref, o_ref):
        if not bt:
            o_ref[...] = _dot(a_ref[...], b_ref[...]).astype(out_dtype)
        elif parts == 1:
            o_ref[...] = _dot_nt(a_ref[...], b_ref[...]).astype(out_dtype)
        else:
            @pl.when(pl.program_id(2) == 0)
            def _():
                o_ref[...] = jnp.zeros_like(o_ref)

            o_ref[...] += _dot_nt(a_ref[...], b_ref[...])

    if bt:
        b_spec = pl.BlockSpec((tn, k), lambda j, i, s: (j, s))
    else:
        b_spec = pl.BlockSpec((k, tn), lambda j, i, s: (0, j))
    if a.ndim == 3:
        assert bt and out_dtype == F32
        a_spec = pl.BlockSpec((None, tm, k), lambda j, i, s: (s, i, 0))
    else:
        a_spec = pl.BlockSpec((tm, k), lambda j, i, s: (i, 0))
    return pl.pallas_call(
        body, name=name, grid=(n // tn, m // tm, parts),
        in_specs=[a_spec, b_spec],
        out_specs=pl.BlockSpec((tm, tn), lambda j, i, s: (i, j)),
        out_shape=jax.ShapeDtypeStruct((m, n), out_dtype),
        compiler_params=_params(("parallel", "parallel", "arbitrary")),
    )(a, b)


def _mm_tn(a, b, name, tk_cap=1024, tn_cap=1664, tl_cap=2048, chips=False):
    l, k = a.shape
    tk = _tile(k, tk_cap)
    tl = min(tl_cap, l)

    def body(a_ref, b_ref, o_ref):
        @pl.when(pl.program_id(2) == 0)
        def _():
            o_ref[...] = jnp.zeros_like(o_ref)

        o_ref[...] += _dot_tn(a_ref[...], b_ref[...])

    if chips:
        n = b.shape[-1] * (b.shape[0] if b.ndim == 3 else 1)
        tn = n // N_CHIPS
        assert tn % LANES == 0
        if b.ndim == 3:
            per = N_CHIPS // b.shape[0]
            b_spec = pl.BlockSpec((None, tl, tn), lambda i, j, r: (j // per, r, j % per))
        else:
            b_spec = pl.BlockSpec((tl, tn), lambda i, j, r: (r, j))
        out_spec = pl.BlockSpec((None, tk, tn), lambda i, j, r: (j, i, 0))
        out_shape = jax.ShapeDtypeStruct((N_CHIPS, k, tn), F32)
    else:
        n = b.shape[1]
        tn = _tile(n, tn_cap)
        b_spec = pl.BlockSpec((tl, tn), lambda i, j, r: (r, j))
        out_spec = pl.BlockSpec((tk, tn), lambda i, j, r: (i, j))
        out_shape = jax.ShapeDtypeStruct((k, n), F32)
    return pl.pallas_call(
        body, name=name, grid=(k // tk, n // tn, l // tl),
        in_specs=[pl.BlockSpec((tl, tk), lambda i, j, r: (r, i)), b_spec],
        out_specs=out_spec, out_shape=out_shape,
        compiler_params=_params(("parallel", "parallel", "arbitrary")),
    )(a, b)


def _row(tl, n):
    return pl.BlockSpec((tl, n), lambda i: (i, 0))


def _const(shape):
    return pl.BlockSpec(shape, lambda i: tuple(0 for _ in shape))


def _proj_fwd(x, g1, win, gq, wuq, gkv, wukv, rc, rs, bg, tl):
    l = x.shape[0]

    def body(x_ref, g1_ref, win_ref, gq_ref, wuq_ref, gkv_ref, wukv_ref, rc_ref, rs_ref, bg_ref,
             hn_ref, cq_ref, ckv_ref, q_ref, k_ref, v_ref, u_ref, gl_ref):
        hn, _ = _rms(x_ref[...], g1_ref[...])
        hnb = hn.astype(BF16)
        hn_ref[...] = hnb
        proj = _dot(hnb, win_ref[...])
        cq = proj[:, P_CQ:P_CKV]
        ckv = proj[:, P_CKV:P_KR]
        kr = proj[:, P_KR:P_U]
        cq_ref[...] = cq
        ckv_ref[...] = ckv
        u_ref[...] = proj[:, P_U:P_GL]
        gl_ref[...] = proj[:, P_GL:P_END] + bg_ref[...]
        qn, _ = _rms(cq, gq_ref[...])
        q = _dot(qn.astype(BF16), wuq_ref[...])
        c1 = rc_ref[...]
        s1 = rs_ref[...]
        q_ref[...] = (_rope(q, jnp.tile(c1, (1, N_HEADS)), jnp.tile(s1, (1, N_HEADS))) * Q_PRESCALE).astype(BF16)
        ckvn, _ = _rms(ckv, gkv_ref[...])
        kv = _dot(ckvn.astype(BF16), wukv_ref[...])
        krr = _rope(kr, c1, s1)
        k_ref[...] = (kv[:, :HP] + jnp.tile(krr, (1, N_HEADS))).astype(BF16)
        v_ref[...] = kv[:, HP:].astype(BF16)

    outs = [(D_MODEL, BF16), (Q_RANK, F32), (KV_RANK, F32), (HP, BF16), (HP, BF16), (HP, BF16),
            (SSM_W, F32), (2 * D_MODEL, F32)]
    return pl.pallas_call(
        body, name="proj_fwd", grid=(l // tl,),
        in_specs=[_row(tl, D_MODEL), _const((1, D_MODEL)), _const((D_MODEL, P_END)), _const((1, Q_RANK)),
                  _const((Q_RANK, HP)), _const((1, KV_RANK)), _const((KV_RANK, 2 * HP)),
                  _row(tl, HEAD_SLOT), _row(tl, HEAD_SLOT), _const((1, 2 * D_MODEL))],
        out_specs=[_row(tl, n) for n, _ in outs],
        out_shape=[jax.ShapeDtypeStruct((l, n), dt) for n, dt in outs],
        compiler_params=_params(("parallel",)),
    )(x, g1, win, gq, wuq, gkv, wukv, rc, rs, bg)


_NEG = -1e30


LOG2E = 1.0 / math.log(2.0)
LN2 = math.log(2.0)
ATTN_SCALE = 1.0 / math.sqrt(QK_HEAD)
Q_PRESCALE = ATTN_SCALE * LOG2E
FWD_HEADS = 8
BWD_HEADS = 4


def _causal_pairs(nq, by_query):
    if by_query:
        pairs = [(i, j) for i in range(nq) for j in range(i + 1)]
    else:
        pairs = [(i, j) for j in range(nq) for i in range(j, nq)]
    return jnp.array([p[0] for p in pairs], jnp.int32), jnp.array([p[1] for p in pairs], jnp.int32)


def _diag_mask_t(s):
    rows = lax.broadcasted_iota(jnp.int32, s.shape, 0)
    cols = lax.broadcasted_iota(jnp.int32, s.shape, 1)
    return jnp.where(rows <= cols, s, _NEG)


def _attn_fwd(q, k, v, tq, heads):
    l = q.shape[0]
    nq = l // tq
    it, jt = _causal_pairs(nq, True)

    def body(it_ref, jt_ref, q_ref, k_ref, v_ref, o_ref, lse_ref, m_ref, l_ref, acc_ref):
        t = pl.program_id(1)
        i = it_ref[t]
        j = jt_ref[t]

        @pl.when(j == 0)
        def _():
            m_ref[...] = jnp.full_like(m_ref, _NEG)
            l_ref[...] = jnp.zeros_like(l_ref)
            acc_ref[...] = jnp.zeros_like(acc_ref)

        def update(on_diagonal):
            for hh in range(heads):
                sl = slice(hh * HEAD_SLOT, (hh + 1) * HEAD_SLOT)
                s = _dot_nt(k_ref[:, sl], q_ref[:, sl])
                if on_diagonal:
                    s = _diag_mask_t(s)
                m_old = m_ref[hh]
                m_new = jnp.maximum(m_old, jnp.max(s, axis=0, keepdims=True))
                p = jnp.exp2(s - m_new)
                alpha = jnp.exp2(m_old - m_new)
                l_ref[hh] = alpha * l_ref[hh] + jnp.sum(p, axis=0, keepdims=True)
                acc_ref[hh] = alpha * acc_ref[hh] + _dot_tn(v_ref[:, sl], p.astype(BF16))
                m_ref[hh] = m_new

        @pl.when(j < i)
        def _():
            update(False)

        @pl.when(j == i)
        def _():
            update(True)
            for hh in range(heads):
                sl = slice(hh * HEAD_SLOT, (hh + 1) * HEAD_SLOT)
                o_ref[:, sl] = (acc_ref[hh] / l_ref[hh]).T.astype(BF16)
                lse_ref[hh] = m_ref[hh] + jnp.log(l_ref[hh]) * LOG2E

    blk = (tq, (heads * HEAD_SLOT))
    qmap = lambda h, t, it_ref, jt_ref: (it_ref[t], h)
    kmap = lambda h, t, it_ref, jt_ref: (jt_ref[t], h)
    row = pl.BlockSpec((heads, 1, tq), lambda h, t, it_ref, jt_ref: (h, 0, it_ref[t]))
    return pl.pallas_call(
        body, name="attn_fwd",
        grid_spec=pltpu.PrefetchScalarGridSpec(
            num_scalar_prefetch=2, grid=(N_HEADS // heads, it.shape[0]),
            in_specs=[pl.BlockSpec(blk, qmap), pl.BlockSpec(blk, kmap), pl.BlockSpec(blk, kmap)],
            out_specs=[pl.BlockSpec(blk, qmap), row],
            scratch_shapes=[pltpu.VMEM((heads, 1, tq), F32), pltpu.VMEM((heads, 1, tq), F32),
                            pltpu.VMEM((heads, HEAD_SLOT, tq), F32)]),
        out_shape=[jax.ShapeDtypeStruct((l, HP), BF16), jax.ShapeDtypeStruct((N_HEADS, 1, l), F32)],
        compiler_params=_params(("parallel", "arbitrary")),
    )(it, jt, q, k, v)


def _attn_delta(o, do, tq, heads):
    l = o.shape[0]

    def body(o_ref, do_ref, d_ref):
        prod = o_ref[...].astype(F32) * do_ref[...].astype(F32)
        for hh in range(heads):
            d_ref[hh] = jnp.sum(prod[:, hh * HEAD_SLOT:(hh + 1) * HEAD_SLOT].T, axis=0, keepdims=True)

    blk = pl.BlockSpec((tq, (heads * HEAD_SLOT)), lambda h, i: (i, h))
    return pl.pallas_call(
        body, name="attn_delta", grid=(N_HEADS // heads, l // tq), in_specs=[blk, blk],
        out_specs=pl.BlockSpec((heads, 1, tq), lambda h, i: (h, 0, i)),
        out_shape=jax.ShapeDtypeStruct((N_HEADS, 1, l), F32),
        compiler_params=_params(("parallel", "parallel")),
    )(o, do)


def _attn_bwd(q, k, v, do, lse, delta, tq, heads):
    l = q.shape[0]
    nq = l // tq
    it, jt = _causal_pairs(nq, False)

    def body(it_ref, jt_ref, q_ref, k_ref, v_ref, do_ref, lse_ref, dl_ref, dq_ref, dk_ref, dv_ref, dka_ref, dva_ref):
        t = pl.program_id(1)
        i = it_ref[t]
        j = jt_ref[t]

        @pl.when(t == 0)
        def _():
            dq_ref[...] = jnp.zeros_like(dq_ref)

        @pl.when(i == j)
        def _():
            dka_ref[...] = jnp.zeros_like(dka_ref)
            dva_ref[...] = jnp.zeros_like(dva_ref)

        def update(on_diagonal):
            r0 = pl.multiple_of(i * tq, tq)
            for hh in range(heads):
                sl = slice(hh * HEAD_SLOT, (hh + 1) * HEAD_SLOT)
                qb = q_ref[:, sl]
                kb = k_ref[:, sl]
                dob = do_ref[:, sl]
                s = _dot_nt(kb, qb)
                if on_diagonal:
                    s = _diag_mask_t(s)
                p = jnp.exp2(s - lse_ref[hh])
                dva_ref[:, sl] += _dot(p.astype(BF16), dob)
                dp = _dot_nt(v_ref[:, sl], dob)
                ds = (p * (dp - dl_ref[hh])).astype(BF16)
                dka_ref[:, sl] += _dot(ds, qb)
                dq_ref[pl.ds(r0, tq), sl] += ATTN_SCALE * _dot_tn(ds, kb)

        @pl.when(j < i)
        def _():
            update(False)

        @pl.when(j == i)
        def _():
            update(True)

        @pl.when(i == nq - 1)
        def _():
            dk_ref[...] = (dka_ref[...] * LN2).astype(BF16)
            dv_ref[...] = dva_ref[...].astype(BF16)

    blk = (tq, (heads * HEAD_SLOT))
    qmap = lambda h, t, it_ref, jt_ref: (it_ref[t], h)
    kmap = lambda h, t, it_ref, jt_ref: (jt_ref[t], h)
    row = pl.BlockSpec((heads, 1, tq), lambda h, t, it_ref, jt_ref: (h, 0, it_ref[t]))
    return pl.pallas_call(
        body, name="attn_bwd",
        grid_spec=pltpu.PrefetchScalarGridSpec(
            num_scalar_prefetch=2, grid=(N_HEADS // heads, it.shape[0]),
            in_specs=[pl.BlockSpec(blk, qmap), pl.BlockSpec(blk, kmap), pl.BlockSpec(blk, kmap),
                      pl.BlockSpec(blk, qmap), row, row],
            out_specs=[pl.BlockSpec((l, (heads * HEAD_SLOT)), lambda h, t, it_ref, jt_ref: (0, h)), pl.BlockSpec(blk, kmap),
                       pl.BlockSpec(blk, kmap)],
            scratch_shapes=[pltpu.VMEM(blk, F32), pltpu.VMEM(blk, F32)]),
        out_shape=[jax.ShapeDtypeStruct((l, HP), F32), jax.ShapeDtypeStruct((l, HP), BF16),
                   jax.ShapeDtypeStruct((l, HP), BF16)],
        compiler_params=_params(("parallel", "arbitrary")),
    )(it, jt, q, k, v, do, lse, delta)


SSM_CB = 512
SSM_UB = 128
SSM_NB = SSM_CH // SSM_CB


def _scan_tiles(re_ref, im_ref, tab, carry, n_tiles, reverse):
    group = 2
    assert n_tiles % group == 0
    pr, pi = tab[6], tab[7]

    def inside(sr, si):
        for step, k in enumerate((1, 2, 4)):
            mr, mi = tab[2 * step], tab[2 * step + 1]
            sh = (SUBLANES - k) if reverse else k
            rr = pltpu.roll(sr, sh, 0)
            ri = pltpu.roll(si, sh, 0)
            sr, si = sr + mr * rr - mi * ri, si + mr * ri + mi * rr
        return sr, si

    def body(n, c):
        cr, ci = c
        first = (n_tiles - group * (n + 1)) if reverse else group * n
        r0 = pl.multiple_of(first * SUBLANES, group * SUBLANES)
        rows = [pl.ds(r0 + g * SUBLANES, SUBLANES) for g in range(group)]
        tiles = [inside(re_ref[r, :], im_ref[r, :]) for r in rows]
        for g in (range(group - 1, -1, -1) if reverse else range(group)):
            sr, si = tiles[g]
            sr, si = sr + pr * cr - pi * ci, si + pr * ci + pi * cr
            re_ref[rows[g], :] = sr
            im_ref[rows[g], :] = si
            edge = slice(0, 1) if reverse else slice(SUBLANES - 1, SUBLANES)
            cr, ci = sr[edge, :], si[edge, :]
        return cr, ci

    return lax.fori_loop(0, n_tiles // group, body, carry)


def _ssm_fwd(u, bre, bim, cre, cim, dvec, tab, tt):
    l = u.shape[0]
    nt = l // tt

    def body(u_ref, bre_ref, bim_ref, cre_ref, cim_ref, d_ref, tab_ref, y_ref, sre_ref, sim_ref, car_ref):
        @pl.when(pl.program_id(1) == 0)
        def _():
            car_ref[...] = jnp.zeros_like(car_ref)

        uf = u_ref[...]
        ub = uf.astype(BF16)
        sre_ref[...] = _dot(ub, bre_ref[0])
        sim_ref[...] = _dot(ub, bim_ref[0])
        tab_v = [tab_ref[n] for n in range(8)]
        cr, ci = _scan_tiles(sre_ref, sim_ref, tab_v, (car_ref[0:1, :], car_ref[8:9, :]), tt // SUBLANES, False)
        car_ref[0:1, :] = cr
        car_ref[8:9, :] = ci
        y_ref[...] = (_dot(sre_ref[...].astype(BF16), cre_ref[0]) - _dot(sim_ref[...].astype(BF16), cim_ref[0])
                      + d_ref[...] * uf)

    return pl.pallas_call(
        body, name="ssm_fwd", grid=(SSM_NB, nt),
        in_specs=[pl.BlockSpec((tt, SSM_UB), lambda m, t: (t, m)),
                  pl.BlockSpec((1, SSM_UB, SSM_CB), lambda m, t: (m, 0, 0)),
                  pl.BlockSpec((1, SSM_UB, SSM_CB), lambda m, t: (m, 0, 0)),
                  pl.BlockSpec((1, SSM_CB, SSM_UB), lambda m, t: (m, 0, 0)),
                  pl.BlockSpec((1, SSM_CB, SSM_UB), lambda m, t: (m, 0, 0)),
                  pl.BlockSpec((1, SSM_UB), lambda m, t: (0, m)),
                  pl.BlockSpec((8, SUBLANES, SSM_CB), lambda m, t: (0, 0, m))],
        out_specs=[pl.BlockSpec((tt, SSM_UB), lambda m, t: (t, m)),
                   pl.BlockSpec((tt, SSM_CB), lambda m, t: (t, m)),
                   pl.BlockSpec((tt, SSM_CB), lambda m, t: (t, m))],
        out_shape=[jax.ShapeDtypeStruct((l, SSM_W), F32), jax.ShapeDtypeStruct((l, SSM_CH), F32),
                   jax.ShapeDtypeStruct((l, SSM_CH), F32)],
        scratch_shapes=[pltpu.VMEM((2 * SUBLANES, SSM_CB), F32)],
        compiler_params=_params(("parallel", "arbitrary")),
    )(u, bre, bim, cre, cim, dvec, tab)


def _ssm_bwd(dy, u, sre, sim, bre, bim, cre, cim, dvec, tab, tt):
    l = u.shape[0]
    nt = l // tt
    tpb = tt // SUBLANES

    def body(dy_ref, u_ref, sre_ref, sim_ref, hre_ref, him_ref, bre_ref, bim_ref, cre_ref, cim_ref, d_ref, tab_ref,
             du_ref, dbre_ref, dbim_ref, dcre_ref, dcim_ref, dare_ref, daim_ref, dd_ref, lr_ref, li_ref, car_ref):
        t = pl.program_id(1)

        @pl.when(t == 0)
        def _():
            car_ref[...] = jnp.zeros_like(car_ref)
            for ref in (dbre_ref, dbim_ref, dcre_ref, dcim_ref, dare_ref, daim_ref, dd_ref):
                ref[...] = jnp.zeros_like(ref)

        dyf = dy_ref[...]
        dyb = dyf.astype(BF16)
        uf = u_ref[...]
        s_re = sre_ref[...]
        s_im = sim_ref[...]
        lr_ref[...] = _dot_nt(dyb, cre_ref[0])
        li_ref[...] = -_dot_nt(dyb, cim_ref[0])
        dcre_ref[0] += _dot_tn(s_re.astype(BF16), dyb)
        dcim_ref[0] -= _dot_tn(s_im.astype(BF16), dyb)
        tab_v = [tab_ref[n] for n in range(8)]
        cr, ci = _scan_tiles(lr_ref, li_ref, tab_v, (car_ref[0:1, :], car_ref[8:9, :]), tpb, True)
        car_ref[0:1, :] = cr
        car_ref[8:9, :] = ci
        lam_r = lr_ref[...]
        lam_i = li_ref[...]
        keep = jnp.where(t == nt - 1, 0.0, 1.0)
        sp_r = _shift_down(s_re, 1, hre_ref[...] * keep)
        sp_i = _shift_down(s_im, 1, him_ref[...] * keep)
        dare_ref[...] += jnp.sum(lam_r * sp_r + lam_i * sp_i, axis=0, keepdims=True)
        daim_ref[...] += jnp.sum(lam_i * sp_r - lam_r * sp_i, axis=0, keepdims=True)
        lrb = lam_r.astype(BF16)
        lib = lam_i.astype(BF16)
        du_ref[...] = _dot_nt(lrb, bre_ref[0]) + _dot_nt(lib, bim_ref[0]) + dyf * d_ref[...]
        ub = uf.astype(BF16)
        dbre_ref[0] += _dot_tn(ub, lrb)
        dbim_ref[0] += _dot_tn(ub, lib)
        dd_ref[...] += jnp.sum(dyf * uf, axis=0, keepdims=True)

    rev = lambda m, t: (nt - 1 - t, m)
    halo = lambda m, t: (jnp.maximum((nt - 1 - t) * tpb - 1, 0), m)
    wb = pl.BlockSpec((1, SSM_UB, SSM_CB), lambda m, t: (m, 0, 0))
    wc = pl.BlockSpec((1, SSM_CB, SSM_UB), lambda m, t: (m, 0, 0))
    vec_c = pl.BlockSpec((1, SSM_CB), lambda m, t: (0, m))
    vec_u = pl.BlockSpec((1, SSM_UB), lambda m, t: (0, m))
    return pl.pallas_call(
        body, name="ssm_bwd", grid=(SSM_NB, nt),
        in_specs=[pl.BlockSpec((tt, SSM_UB), rev), pl.BlockSpec((tt, SSM_UB), rev),
                  pl.BlockSpec((tt, SSM_CB), rev), pl.BlockSpec((tt, SSM_CB), rev),
                  pl.BlockSpec((SUBLANES, SSM_CB), halo), pl.BlockSpec((SUBLANES, SSM_CB), halo),
                  wb, wb, wc, wc, vec_u,
                  pl.BlockSpec((8, SUBLANES, SSM_CB), lambda m, t: (0, 0, m))],
        out_specs=[pl.BlockSpec((tt, SSM_UB), rev), wb, wb, wc, wc, vec_c, vec_c, vec_u],
        out_shape=[jax.ShapeDtypeStruct((l, SSM_W), F32),
                   jax.ShapeDtypeStruct((SSM_NB, SSM_UB, SSM_CB), F32), jax.ShapeDtypeStruct((SSM_NB, SSM_UB, SSM_CB), F32),
                   jax.ShapeDtypeStruct((SSM_NB, SSM_CB, SSM_UB), F32), jax.ShapeDtypeStruct((SSM_NB, SSM_CB, SSM_UB), F32),
                   jax.ShapeDtypeStruct((1, SSM_CH), F32), jax.ShapeDtypeStruct((1, SSM_CH), F32),
                   jax.ShapeDtypeStruct((1, SSM_W), F32)],
        scratch_shapes=[pltpu.VMEM((tt, SSM_CB), F32), pltpu.VMEM((tt, SSM_CB), F32),
                        pltpu.VMEM((2 * SUBLANES, SSM_CB), F32)],
        compiler_params=_params(("parallel", "arbitrary")),
    )(dy, u, sre, sim, sre, sim, bre, bim, cre, cim, dvec, tab)


def _merge_fwd(x, gl, attn, y1, wba, wbs, wglu, bglu, wout, gpost, gpre, tl):
    l = x.shape[0]

    def body(x_ref, gl_ref, at_ref, y1_ref, wba_ref, wbs_ref, wglu_ref, bglu_ref, wout_ref, gpost_ref, gpre_ref,
             a_ref, sm_ref, mg_ref, z_ref, x1_ref, hn2_ref, y3_ref):
        y2 = _gelu(y1_ref[...])
        sg = _sigmoid(_dot(y2.astype(BF16), wglu_ref[...]) + bglu_ref[...])
        y3 = (y2 * sg).astype(BF16)
        y3_ref[...] = y3
        a = _dot(at_ref[...], wba_ref[...])
        sm = _dot(y3, wbs_ref[...])
        a_ref[...] = a.astype(BF16)
        sm_ref[...] = sm.astype(BF16)
        g = _sigmoid(gl_ref[...])
        merged = (g[:, :D_MODEL] * a + g[:, D_MODEL:] * sm).astype(BF16)
        mg_ref[...] = merged
        z = _dot(merged, wout_ref[...])
        z_ref[...] = z
        n, _ = _rms(z, gpost_ref[...])
        x1 = x_ref[...] + n
        x1_ref[...] = x1
        hn2, _ = _rms(x1, gpre_ref[...])
        hn2_ref[...] = hn2.astype(BF16)

    outs = [(D_MODEL, BF16), (D_MODEL, BF16), (D_MODEL, BF16), (D_MODEL, F32), (D_MODEL, F32), (D_MODEL, BF16),
            (SSM_W, BF16)]
    return pl.pallas_call(
        body, name="merge_fwd", grid=(l // tl,),
        in_specs=[_row(tl, D_MODEL), _row(tl, 2 * D_MODEL), _row(tl, HP), _row(tl, SSM_W),
                  _const((HP, D_MODEL)), _const((SSM_W, D_MODEL)), _const((SSM_W, SSM_W)), _const((1, SSM_W)),
                  _const((D_MODEL, D_MODEL)), _const((1, D_MODEL)), _const((1, D_MODEL))],
        out_specs=[_row(tl, n) for n, _ in outs],
        out_shape=[jax.ShapeDtypeStruct((l, n), dt) for n, dt in outs],
        compiler_params=_params(("parallel",)),
    )(x, gl, attn, y1, wba, wbs, wglu, bglu, wout, gpost, gpre)


def _merge_bwd(dhn2, x1, dx2, z, gl, a, sm, y1, wba, wbs, wglu, bglu, wout, gpost, gpre, tl):
    l = x1.shape[0]

    def body(dhn2_ref, x1_ref, dx2_ref, z_ref, gl_ref, a_ref, sm_ref, y1_ref,
             wba_ref, wbs_ref, wglu_ref, bglu_ref, wout_ref, gpost_ref, gpre_ref,
             dx1_ref, dz_ref, dbra_ref, dbrs_ref, dgl_ref, dat_ref, dy1_ref, dt_ref, y2_ref,
             dgpre_ref, dgpost_ref, dbg_ref, dbglu_ref):
        @pl.when(pl.program_id(0) == 0)
        def _():
            for ref in (dgpre_ref, dgpost_ref, dbg_ref, dbglu_ref):
                ref[...] = jnp.zeros_like(ref)

        dx1a, dgpre = _rms_bwd(dhn2_ref[...], x1_ref[...], gpre_ref[...])
        dgpre_ref[...] += dgpre
        dx1 = dx2_ref[...] + dx1a
        dx1_ref[...] = dx1
        dz, dgpost = _rms_bwd(dx1, z_ref[...], gpost_ref[...])
        dgpost_ref[...] += dgpost
        dzb = dz.astype(BF16)
        dz_ref[...] = dzb
        dm = _dot_nt(dzb, wout_ref[...])
        g = _sigmoid(gl_ref[...])
        g0 = g[:, :D_MODEL]
        g1 = g[:, D_MODEL:]
        dbra = (dm * g0).astype(BF16)
        dbrs = (dm * g1).astype(BF16)
        dbra_ref[...] = dbra
        dbrs_ref[...] = dbrs
        dgl0 = dm * a_ref[...].astype(F32) * g0 * (1.0 - g0)
        dgl1 = dm * sm_ref[...].astype(F32) * g1 * (1.0 - g1)
        dgl_ref[:, :D_MODEL] = dgl0.astype(BF16)
        dgl_ref[:, D_MODEL:] = dgl1.astype(BF16)
        dbg_ref[:, :D_MODEL] += jnp.sum(dgl0, axis=0, keepdims=True)
        dbg_ref[:, D_MODEL:] += jnp.sum(dgl1, axis=0, keepdims=True)
        dat_ref[...] = _dot_nt(dbra, wba_ref[...]).astype(BF16)
        dy3 = _dot_nt(dbrs, wbs_ref[...])
        y1v = y1_ref[...]
        y2 = _gelu(y1v)
        y2b = y2.astype(BF16)
        y2_ref[...] = y2b
        sg = _sigmoid(_dot(y2b, wglu_ref[...]) + bglu_ref[...])
        dt = dy3 * y2 * sg * (1.0 - sg)
        dtb = dt.astype(BF16)
        dt_ref[...] = dtb
        dbglu_ref[...] += jnp.sum(dt, axis=0, keepdims=True)
        dy2 = dy3 * sg + _dot_nt(dtb, wglu_ref[...])
        dy1_ref[...] = dy2 * _gelu_grad(y1v)

    outs = [(D_MODEL, F32), (D_MODEL, BF16), (D_MODEL, BF16), (D_MODEL, BF16), (2 * D_MODEL, BF16), (HP, BF16),
            (SSM_W, F32), (SSM_W, BF16), (SSM_W, BF16)]
    accs = [D_MODEL, D_MODEL, 2 * D_MODEL, SSM_W]
    return pl.pallas_call(
        body, name="merge_bwd", grid=(l // tl,),
        in_specs=[_row(tl, D_MODEL), _row(tl, D_MODEL), _row(tl, D_MODEL), _row(tl, D_MODEL),
                  _row(tl, 2 * D_MODEL), _row(tl, D_MODEL), _row(tl, D_MODEL), _row(tl, SSM_W),
                  _const((HP, D_MODEL)), _const((SSM_W, D_MODEL)), _const((SSM_W, SSM_W)), _const((1, SSM_W)),
                  _const((D_MODEL, D_MODEL)), _const((1, D_MODEL)), _const((1, D_MODEL))],
        out_specs=[_row(tl, n) for n, _ in outs] + [_const((1, n)) for n in accs],
        out_shape=[jax.ShapeDtypeStruct((l, n), dt) for n, dt in outs]
        + [jax.ShapeDtypeStruct((1, n), F32) for n in accs],
        compiler_params=_params(("arbitrary",)),
    )(dhn2, x1, dx2, z, gl, a, sm, y1, wba, wbs, wglu, bglu, wout, gpost, gpre)


def _proj_bwd(x, dx1, cq, ckv, dq, dk, dv, du, dgl, g1, win, gq, wuq, gkv, wukv, rc, rs, tl):
    l = x.shape[0]

    def body(x_ref, dx1_ref, cq_ref, ckv_ref, dq_ref, dk_ref, dv_ref, du_ref, dgl_ref,
             g1_ref, win_ref, gq_ref, wuq_ref, gkv_ref, wukv_ref, rc_ref, rs_ref,
             gx_ref, dql_ref, qn_ref, ckvn_ref, dproj_ref, dg1_ref, dgq_ref, dgkv_ref):
        @pl.when(pl.program_id(0) == 0)
        def _():
            for ref in (dg1_ref, dgq_ref, dgkv_ref):
                ref[...] = jnp.zeros_like(ref)

        c1 = rc_ref[...]
        s1 = rs_ref[...]
        dql = _rope_bwd(dq_ref[...], jnp.tile(c1, (1, N_HEADS)), jnp.tile(s1, (1, N_HEADS))).astype(BF16)
        dql_ref[...] = dql
        dqn = _dot_nt(dql, wuq_ref[...])
        cq = cq_ref[...]
        qn, _ = _rms(cq, gq_ref[...])
        qn_ref[...] = qn.astype(BF16)
        dcq, dgq = _rms_bwd(dqn, cq, gq_ref[...])
        dgq_ref[...] += dgq
        dkb = dk_ref[...]
        dvb = dv_ref[...]
        dkf = dkb.astype(F32)
        dkr = dkf[:, 0:HEAD_SLOT]
        for h in range(1, N_HEADS):
            dkr = dkr + dkf[:, h * HEAD_SLOT:(h + 1) * HEAD_SLOT]
        dkr = _rope_bwd(dkr, c1, s1)
        dckvn = _dot_nt(dkb, wukv_ref[:, :HP]) + _dot_nt(dvb, wukv_ref[:, HP:])
        ckv = ckv_ref[...]
        ckvn, _ = _rms(ckv, gkv_ref[...])
        ckvn_ref[...] = ckvn.astype(BF16)
        dckv, dgkv = _rms_bwd(dckvn, ckv, gkv_ref[...])
        dgkv_ref[...] += dgkv
        dproj_ref[:, P_CQ:P_CKV] = dcq.astype(BF16)
        dproj_ref[:, P_CKV:P_KR] = dckv.astype(BF16)
        dproj_ref[:, P_KR:P_U] = dkr.astype(BF16)
        dproj_ref[:, P_U:P_GL] = du_ref[...].astype(BF16)
        dproj_ref[:, P_GL:P_END] = dgl_ref[...]
        dhn = _dot_nt(dproj_ref[...], win_ref[...])
        dxa, dg1 = _rms_bwd(dhn, x_ref[...], g1_ref[...])
        dg1_ref[...] += dg1
        gx_ref[...] = dx1_ref[...] + dxa

    outs = [(D_MODEL, F32), (HP, BF16), (Q_RANK, BF16), (KV_RANK, BF16), (P_END, BF16)]
    accs = [D_MODEL, Q_RANK, KV_RANK]
    return pl.pallas_call(
        body, name="proj_bwd", grid=(l // tl,),
        in_specs=[_row(tl, D_MODEL), _row(tl, D_MODEL), _row(tl, Q_RANK), _row(tl, KV_RANK), _row(tl, HP),
                  _row(tl, HP), _row(tl, HP), _row(tl, SSM_W), _row(tl, 2 * D_MODEL),
                  _const((1, D_MODEL)), _const((D_MODEL, P_END)), _const((1, Q_RANK)), _const((Q_RANK, HP)),
                  _const((1, KV_RANK)), _const((KV_RANK, 2 * HP)), _row(tl, HEAD_SLOT), _row(tl, HEAD_SLOT)],
        out_specs=[_row(tl, n) for n, _ in outs] + [_const((1, n)) for n in accs],
        out_shape=[jax.ShapeDtypeStruct((l, n), dt) for n, dt in outs]
        + [jax.ShapeDtypeStruct((1, n), F32) for n in accs],
        compiler_params=_params(("arbitrary",)),
    )(x, dx1, cq, ckv, dq, dk, dv, du, dgl, g1, win, gq, wuq, gkv, wukv, rc, rs)


CONV_CB = 256
CONV_NB = D_FF // CONV_CB
CONV_ROWS = 16


def _conv3(h, halo, w, b):
    return b + w[0:1, :] * _shift_down(h, 2, halo) + w[1:2, :] * _shift_down(h, 1, halo) + w[2:3, :] * h


def _conv_fwd(h, cw, cb, tl):
    l = h.shape[0]

    def body(hg_ref, hv_ref, wg_ref, wv_ref, bg_ref, bv_ref, act_ref, halo_ref):
        @pl.when(pl.program_id(1) == 0)
        def _():
            halo_ref[...] = jnp.zeros_like(halo_ref)

        hg = hg_ref[...]
        hv = hv_ref[...]
        cg = _conv3(hg, halo_ref[0:SUBLANES, :], wg_ref[...], bg_ref[...])
        cv = _conv3(hv, halo_ref[SUBLANES:, :], wv_ref[...], bv_ref[...])
        act_ref[...] = (_gelu(cg) * cv).astype(BF16)
        halo_ref[0:SUBLANES, :] = hg[tl - SUBLANES:, :]
        halo_ref[SUBLANES:, :] = hv[tl - SUBLANES:, :]

    gmap = lambda c, r: (r, c)
    vmap = lambda c, r: (r, CONV_NB + c)
    return pl.pallas_call(
        body, name="conv_fwd", grid=(CONV_NB, l // tl),
        in_specs=[pl.BlockSpec((tl, CONV_CB), gmap), pl.BlockSpec((tl, CONV_CB), vmap),
                  pl.BlockSpec((3, CONV_CB), lambda c, r: (0, c)), pl.BlockSpec((3, CONV_CB), lambda c, r: (0, CONV_NB + c)),
                  pl.BlockSpec((1, CONV_CB), lambda c, r: (0, c)), pl.BlockSpec((1, CONV_CB), lambda c, r: (0, CONV_NB + c))],
        out_specs=pl.BlockSpec((tl, CONV_CB), gmap),
        out_shape=jax.ShapeDtypeStruct((l, D_FF), BF16),
        scratch_shapes=[pltpu.VMEM((2 * SUBLANES, CONV_CB), F32)],
        compiler_params=_params(("parallel", "arbitrary")),
    )(h, h, cw, cw, cb, cb)


def _conv_bwd(h, dact, cw, cb, tl):
    l = h.shape[0]
    nr = l // tl
    tpb = tl // SUBLANES

    def body(hg_ref, hv_ref, hgh_ref, hvh_ref, da_ref, wg_ref, wv_ref, bg_ref, bv_ref,
             dh_ref, dwg_ref, dwv_ref, dbg_ref, dbv_ref, car_ref):
        r = pl.program_id(1)

        @pl.when(r == 0)
        def _():
            for ref in (car_ref, dwg_ref, dwv_ref, dbg_ref, dbv_ref):
                ref[...] = jnp.zeros_like(ref)

        keep = jnp.where(r == nr - 1, 0.0, 1.0)
        wg, wv, bg, bv = wg_ref[...], wv_ref[...], bg_ref[...], bv_ref[...]
        nch = tl // CONV_ROWS

        def fold(x):
            s = x[0:SUBLANES, :]
            for k in range(1, CONV_ROWS // SUBLANES):
                s = s + x[k * SUBLANES:(k + 1) * SUBLANES, :]
            return s

        def chunk(n, carry):
            ncg, ncv, acc = carry
            idx = nch - 1 - n
            r0 = pl.multiple_of(idx * CONV_ROWS, CONV_ROWS)
            rows = pl.ds(r0, CONV_ROWS)
            before = pl.ds(pl.multiple_of(jnp.maximum(r0 - SUBLANES, 0), SUBLANES), SUBLANES)
            in_tile = idx > 0
            da = da_ref[rows, :].astype(F32)

            def half(h_ref, halo_ref, w, b):
                hh = h_ref[rows, :]
                prev = jnp.where(in_tile, h_ref[before, :], halo_ref[...] * keep)
                h1 = _shift_down(hh, 1, prev)
                h2 = _shift_down(hh, 2, prev)
                return hh, h1, h2, b + w[0:1, :] * h2 + w[1:2, :] * h1 + w[2:3, :] * hh

            hg, hg1, hg2, cg = half(hg_ref, hgh_ref, wg, bg)
            hv, hv1, hv2, cv = half(hv_ref, hvh_ref, wv, bv)
            dcg = da * cv * _gelu_grad(cg)
            dcv = da * _gelu(cg)

            def back(dc, hh, h1, h2, w, nxt, part):
                dh = w[2:3, :] * dc + w[1:2, :] * _shift_up(dc, 1, nxt) + w[0:1, :] * _shift_up(dc, 2, nxt)
                dh_ref[part, rows, :] = dh.astype(BF16)
                return [fold(dc * h2), fold(dc * h1), fold(dc * hh), fold(dc)]

            sums = back(dcg, hg, hg1, hg2, wg, ncg, 0) + back(dcv, hv, hv1, hv2, wv, ncv, 1)
            return dcg[0:SUBLANES, :], dcv[0:SUBLANES, :], [a + s for a, s in zip(acc, sums)]

        zero = jnp.zeros((SUBLANES, CONV_CB), F32)
        ncg, ncv, acc = lax.fori_loop(0, nch, chunk, (car_ref[0:SUBLANES, :], car_ref[SUBLANES:, :], [zero] * 8))
        car_ref[0:SUBLANES, :] = ncg
        car_ref[SUBLANES:, :] = ncv
        for half_acc, dw_ref, db_ref in ((acc[0:4], dwg_ref, dbg_ref), (acc[4:8], dwv_ref, dbv_ref)):
            for k in range(3):
                dw_ref[k:k + 1, :] += jnp.sum(half_acc[k], axis=0, keepdims=True)
            db_ref[...] += jnp.sum(half_acc[3], axis=0, keepdims=True)

    grev = lambda c, r: (nr - 1 - r, c)
    vrev = lambda c, r: (nr - 1 - r, CONV_NB + c)
    ghalo = lambda c, r: (jnp.maximum((nr - 1 - r) * tpb - 1, 0), c)
    vhalo = lambda c, r: (jnp.maximum((nr - 1 - r) * tpb - 1, 0), CONV_NB + c)
    colg = lambda c, r: (0, c)
    colv = lambda c, r: (0, CONV_NB + c)
    return pl.pallas_call(
        body, name="conv_bwd", grid=(CONV_NB, nr),
        in_specs=[pl.BlockSpec((tl, CONV_CB), grev), pl.BlockSpec((tl, CONV_CB), vrev),
                  pl.BlockSpec((SUBLANES, CONV_CB), ghalo), pl.BlockSpec((SUBLANES, CONV_CB), vhalo),
                  pl.BlockSpec((tl, CONV_CB), grev),
                  pl.BlockSpec((3, CONV_CB), colg), pl.BlockSpec((3, CONV_CB), colv),
                  pl.BlockSpec((1, CONV_CB), colg), pl.BlockSpec((1, CONV_CB), colv)],
        out_specs=[pl.BlockSpec((2, tl, CONV_CB), lambda c, r: (0, nr - 1 - r, c)),
                   pl.BlockSpec((3, CONV_CB), colg), pl.BlockSpec((3, CONV_CB), colg),
                   pl.BlockSpec((1, CONV_CB), colg), pl.BlockSpec((1, CONV_CB), colg)],
        out_shape=[jax.ShapeDtypeStruct((2, l, D_FF), BF16),
                   jax.ShapeDtypeStruct((3, D_FF), F32), jax.ShapeDtypeStruct((3, D_FF), F32),
                   jax.ShapeDtypeStruct((1, D_FF), F32), jax.ShapeDtypeStruct((1, D_FF), F32)],
        scratch_shapes=[pltpu.VMEM((2 * SUBLANES, CONV_CB), F32)],
        compiler_params=_params(("parallel", "arbitrary")),
    )(h, h, h, h, dact, cw, cw, cb, cb)


def _loss_head(ff, x1, tgt, g, tl):
    l = ff.shape[0]

    def body(ff_ref, x1_ref, tg_ref, g_ref, loss_ref, dx2_ref, dff_ref, dg_ref):
        @pl.when(pl.program_id(0) == 0)
        def _():
            loss_ref[...] = jnp.zeros_like(loss_ref)
            dg_ref[...] = jnp.zeros_like(dg_ref)

        f = ff_ref[...]
        gv = g_ref[...]
        n, _ = _rms(f, gv)
        e = x1_ref[...] + n - tg_ref[...]
        loss_ref[...] += 0.5 * jnp.sum(jnp.mean(e * e, axis=-1, keepdims=True), axis=0, keepdims=True)
        dx2 = e * (1.0 / D_MODEL)
        dx2_ref[...] = dx2
        dff, dg = _rms_bwd(dx2, f, gv)
        dff_ref[...] = dff.astype(BF16)
        dg_ref[...] += dg

    return pl.pallas_call(
        body, name="loss_head", grid=(l // tl,),
        in_specs=[_row(tl, D_MODEL), _row(tl, D_MODEL), _row(tl, D_MODEL), _const((1, D_MODEL))],
        out_specs=[_const((1, LANES)), _row(tl, D_MODEL), _row(tl, D_MODEL), _const((1, D_MODEL))],
        out_shape=[jax.ShapeDtypeStruct((1, LANES), F32), jax.ShapeDtypeStruct((l, D_MODEL), F32),
                   jax.ShapeDtypeStruct((l, D_MODEL), BF16), jax.ShapeDtypeStruct((1, D_MODEL), F32)],
        compiler_params=_params(("arbitrary",)),
    )(ff, x1, tgt, g)


def _ssm_disc(lam_re, lam_im, log_dt, b_re, b_im):
    dt = jnp.exp(log_dt)[:, None]
    mag = jnp.exp(lam_re * dt)
    ang = lam_im * dt
    a_re, a_im = mag * jnp.cos(ang), mag * jnp.sin(ang)
    den = lam_re * lam_re + lam_im * lam_im
    n_re, n_im = a_re - 1.0, a_im
    z_re = (n_re * lam_re + n_im * lam_im) / den
    z_im = (n_im * lam_re - n_re * lam_im) / den
    bb_re = z_re[..., None] * b_re - z_im[..., None] * b_im
    bb_im = z_re[..., None] * b_im + z_im[..., None] * b_re
    return a_re, a_im, bb_re, bb_im


_GPB = SSM_CB // SSM_P


def _embed_b(bb):
    t = bb.transpose(0, 2, 1).reshape(SSM_NB, _GPB, SSM_H, SSM_P)
    return jnp.einsum('mjhp,jk->mjhkp', t, jnp.eye(_GPB, dtype=bb.dtype)).reshape(SSM_NB, SSM_UB, SSM_CB)


def _extract_b(d):
    t = d.reshape(SSM_NB, _GPB, SSM_H, _GPB, SSM_P)
    t = jnp.einsum('mjhkp,jk->mjhp', t, jnp.eye(_GPB, dtype=d.dtype))
    return t.reshape(SSM_G, SSM_H, SSM_P).transpose(0, 2, 1)


def _embed_c(c):
    t = c.transpose(0, 2, 1).reshape(SSM_NB, _GPB, SSM_P, SSM_H)
    return jnp.einsum('mjph,jk->mjpkh', t, jnp.eye(_GPB, dtype=c.dtype)).reshape(SSM_NB, SSM_CB, SSM_UB)


def _extract_c(d):
    t = d.reshape(SSM_NB, _GPB, SSM_P, _GPB, SSM_H)
    t = jnp.einsum('mjpkh,jk->mjph', t, jnp.eye(_GPB, dtype=d.dtype))
    return t.reshape(SSM_G, SSM_P, SSM_H).transpose(0, 2, 1)


def _scan_tables(a_re, a_im, reverse):
    ar = a_re.reshape(1, SSM_CH)
    ai = (-a_im if reverse else a_im).reshape(1, SSM_CH)
    pr, pi = [ar], [ai]
    for _ in range(SUBLANES - 1):
        pr, pi = pr + [pr[-1] * ar - pi[-1] * ai], pi + [pr[-1] * ai + pi[-1] * ar]
    rows = jnp.arange(SUBLANES)[:, None]
    out = []
    for k in (1, 2, 4):
        valid = (rows + k <= SUBLANES - 1) if reverse else (rows >= k)
        out += [jnp.where(valid, pr[k - 1], 0.0), jnp.where(valid, pi[k - 1], 0.0)]
    order = list(range(SUBLANES - 1, -1, -1)) if reverse else list(range(SUBLANES))
    out += [jnp.concatenate([pr[n] for n in order], axis=0), jnp.concatenate([pi[n] for n in order], axis=0)]
    return jnp.stack(out).astype(F32)


def _pad_heads(w, d):
    lead = w.shape[:-1]
    w = w.reshape(lead + (N_HEADS, d))
    w = jnp.pad(w, [(0, 0)] * len(lead) + [(0, 0), (0, HEAD_SLOT - d)])
    return w.reshape(lead + (HP,))


def _unpad_heads(w, d):
    lead = w.shape[:-1]
    return w.reshape(lead + (N_HEADS, HEAD_SLOT))[..., :d].reshape(lead + (N_HEADS * d,))


def _chip_major(w, axis):
    k, n = w.shape
    if axis == 0:
        return w.reshape(N_CHIPS, k // N_CHIPS, n)
    return w.reshape(k, N_CHIPS, n // N_CHIPS).transpose(1, 0, 2)


def _from_chip_major(w, axis):
    if axis == 0:
        return w.reshape(-1, w.shape[2])
    return w.transpose(1, 0, 2).reshape(w.shape[1], -1)


def _pad_w_in(w):
    z = lambda n: jnp.zeros((w.shape[0], n), w.dtype)
    return jnp.concatenate([w[:, :640], z(KR_LANE), w[:, 640:672], z(HEAD_SLOT - KR_LANE - QK_ROPE), w[:, 672:]], axis=1)


def _unpad_w_in(w):
    return jnp.concatenate([w[:, :640], w[:, P_KR + KR_LANE:P_KR + KR_LANE + QK_ROPE], w[:, P_U:]], axis=1)


def _local_step(x, positions, tgt, wts, sp):
    l = x.shape[0]
    tl = min(256, l)
    ta = min(512, l)
    ts = min(2048, l)

    inv_freq = ROPE_THETA ** (-jnp.arange(0, QK_ROPE, 2, dtype=F32) / QK_ROPE)
    ang = positions.astype(F32)[:, None] * inv_freq
    cos, sin = jnp.cos(ang), jnp.sin(ang)
    one = jnp.ones((l, KR_LANE), F32)
    rc = jnp.concatenate([one, cos, cos, jnp.ones((l, HEAD_SLOT - KR_LANE - QK_ROPE), F32)], axis=1)
    rs = jnp.concatenate([0 * one, -sin, sin, jnp.zeros((l, HEAD_SLOT - KR_LANE - QK_ROPE), F32)], axis=1)

    win = _pad_w_in(wts["w_in"])
    wuq = _pad_heads(wts["w_uq"], QK_HEAD)
    wukv = jnp.concatenate([_pad_heads(wts["w_uk"], QK_NOPE), _pad_heads(wts["w_uv"], V_HEAD)], axis=1)

    disc_in = (sp["ssm_lambda_re"], sp["ssm_lambda_im"], sp["ssm_log_dt"], sp["ssm_b_re"], sp["ssm_b_im"])
    (a_re, a_im, bb_re, bb_im), disc_vjp = jax.vjp(_ssm_disc, *disc_in)
    bre, bim = _embed_b(bb_re).astype(BF16), _embed_b(bb_im).astype(BF16)
    cre, cim = _embed_c(sp["ssm_c_re"]).astype(BF16), _embed_c(sp["ssm_c_im"]).astype(BF16)
    dvec = sp["ssm_d"].reshape(1, SSM_W)
    tab_f = _scan_tables(a_re, a_im, False)
    tab_r = _scan_tables(a_re, a_im, True)

    g1, gq, gkv = sp["mix_norm_pre"], sp["q_norm"], sp["kv_norm"]
    gpost, gpre, gfin = sp["mix_norm_post"], sp["ffn_norm_pre"], sp["ffn_norm_post"]
    bgate, bglu, convb = sp["b_gate"], sp["b_glu"], sp["conv_b"]

    hn, cq, ckv, q, k, v, u, gl = _proj_fwd(x, g1, win, gq, wuq, gkv, wukv, rc, rs, bgate, tl)
    attn, lse = _attn_fwd(q, k, v, ta, FWD_HEADS)
    y1, sre, sim = _ssm_fwd(u, bre, bim, cre, cim, dvec, tab_f, ts)
    wba = jnp.pad(wts["w_branch_attn"].reshape(N_HEADS, V_HEAD, D_MODEL),
                  ((0, 0), (0, HEAD_SLOT - V_HEAD), (0, 0))).reshape(HP, D_MODEL)
    wbs, wglu, wout = wts["w_branch_ssm"], wts["w_glu"], wts["w_out"]
    a, sm, merged, z, x1, hn2, y3 = _merge_fwd(x, gl, attn, y1, wba, wbs, wglu, bglu, wout, gpost, gpre, tl)
    late = wts["late"](x1)
    wup, wdown, convw = late["w_up"], late["w_down"], late["conv_w"]
    h = _mm(hn2, wup, "ffn_up")
    act = _conv_fwd(h, convw, convb, ts)
    ff = _mm(act, wdown, "ffn_down")
    loss, dx2, dff, dgfin = _loss_head(ff, x1, tgt, gfin, tl)

    dact = _mm(dff, wdown, "ffn_down_dx", out_dtype=BF16, bt=True)
    d_wdown = _mm_tn(act, dff, "ffn_down_dw", tk_cap=D_FF // 2)
    dh, dwg, dwv, dbg, dbv = _conv_bwd(h, dact, convw, convb, ts)
    d_convw = jnp.concatenate([dwg, dwv], axis=1)
    d_convb = jnp.concatenate([dbg, dbv], axis=1)
    dhn2 = _mm(dh, wup, "ffn_up_dx", bt=True)
    d_wup = _mm_tn(hn2, dh, "ffn_up_dw", chips=True)
    behind = wts["send_grads"]("ffn", {"w_up": d_wup, "w_down": _chip_major(d_wdown, 0)})
    (dx1, dz, dbra, dbrs, dgl, dattn, dy1, dt, y2, dgpre, dgpost, dbgate, dbglu) = _merge_bwd(
        dhn2, x1, dx2, z, gl, a, sm, y1, wba, wbs, wglu, bglu, wout, gpost, gpre + behind, tl)
    d_wout = _mm_tn(merged, dz, "w_out_dw")
    d_wba = _mm_tn(attn, dbra, "w_branch_attn_dw", chips=True)
    d_wbs = _mm_tn(y3, dbrs, "w_branch_ssm_dw", chips=True)
    d_wglu = _mm_tn(y2, dt, "w_glu_dw")
    ncol = D_MODEL // N_CHIPS
    behind = wts["send_grads"]("mix", {
        "w_glu": _chip_major(d_wglu, 0),
        "w_branch_attn": d_wba.reshape(N_CHIPS, N_HEADS, HEAD_SLOT, ncol)[:, :, :V_HEAD].reshape(
            N_CHIPS, N_HEADS * V_HEAD, ncol),
        "w_branch_ssm": d_wbs,
        "w_out": _chip_major(d_wout, 0)}, after=d_wglu)
    dq, dk, dv = _attn_bwd(q, k, v, dattn, lse + behind, _attn_delta(attn, dattn, min(2048, l), BWD_HEADS), ta,
                            BWD_HEADS)
    du, dbre, dbim, dcre, dcim, dare, daim, dd = _ssm_bwd(dy1, u, sre, sim, bre, bim, cre, cim, dvec, tab_r, ts)
    behind = wts["send_grads"]("none", {}, after=du)
    gx, dql, qn, ckvn, dproj, dg1, dgq, dgkv = _proj_bwd(
        x, dx1, cq, ckv, dq, dk, dv, du, dgl, g1 + behind, win, gq, wuq, gkv, wukv, rc, rs, tl)
    d_win = _mm_tn(hn, dproj, "w_in_dw")
    d_wuq = _mm_tn(qn, dql, "w_uq_dw")
    d_wuk = _mm_tn(ckvn, dk, "w_uk_dw")
    d_wuv = _mm_tn(ckvn, dv, "w_uv_dw")

    d_lre, d_lim, d_ldt, d_bre, d_bim = disc_vjp((dare.reshape(SSM_G, SSM_P), daim.reshape(SSM_G, SSM_P),
                                                  _extract_b(dbre), _extract_b(dbim)))
    big = {
        "w_in": _chip_major(_unpad_w_in(d_win), 1),
        "w_uq": _chip_major(_unpad_heads(d_wuq, QK_HEAD), 1),
        "w_uk": _chip_major(_unpad_heads(d_wuk, QK_NOPE), 1),
        "w_uv": _chip_major(_unpad_heads(d_wuv, V_HEAD), 1),
    }
    small = {
        "conv_w": d_convw,
        "mix_norm_pre": dg1, "q_norm": dgq, "kv_norm": dgkv,
        "ssm_lambda_re": d_lre, "ssm_lambda_im": d_lim, "ssm_log_dt": d_ldt,
        "ssm_b_re": d_bre, "ssm_b_im": d_bim,
        "ssm_c_re": _extract_c(dcre), "ssm_c_im": _extract_c(dcim),
        "ssm_d": dd.reshape(SSM_G, SSM_H), "b_glu": dbglu, "b_gate": dbgate,
        "mix_norm_post": dgpost, "ffn_norm_pre": dgpre, "conv_b": d_convb, "ffn_norm_post": dgfin,
    }
    return loss[0, 0], gx, big, small


_ANY = pl.BlockSpec(memory_space=pl.ANY)


ROW_TILE = 16


def _place():
    x, y, c = lax.axis_index("x"), lax.axis_index("y"), lax.axis_index("c")
    return x, y, c, 2 * x + y, [(1 - x, y), (x, 1 - y), (1 - x, 1 - y)]


def _half(rows, which):
    hr = rows // 2
    return pl.ds(pl.multiple_of(which * hr, ROW_TILE), hr)


def _remote(src, dst, send_sems, recv_sems, n, dev):
    return pltpu.make_async_remote_copy(src_ref=src, dst_ref=dst, send_sem=send_sems.at[n], recv_sem=recv_sems.at[n],
                                        device_id=dev, device_id_type=MESH)


def _gather_big(shards):
    nw = len(shards)
    rows = [s.shape[0] for s in shards]

    def body(*refs):
        ins, outs = refs[:nw], refs[nw:2 * nw]
        ici_send, ici_recv, d2d_send, d2d_recv = refs[2 * nw:]
        x, y, c, me, peers = _place()
        sent = []
        for i in range(nw):
            for p, (px, py) in enumerate(peers):
                cp = _remote(ins[i].at[_half(rows[i], c)], outs[i].at[me, _half(rows[i], c)], ici_send, ici_recv,
                             3 * i + p, (px, py, c))
                cp.start()
                sent.append(cp)
        for p, (px, py) in enumerate(peers):
            for i in range(nw):
                blk = outs[i].at[2 * px + py, _half(rows[i], c)]
                _remote(blk, blk, ici_send, ici_recv, 3 * i + p, (px, py, c)).wait_recv()
                cp = _remote(blk, blk, d2d_send, d2d_recv, 3 * i + p, (x, y, 1 - c))
                cp.start()
                sent.append(cp)
        for p, (px, py) in enumerate(peers):
            for i in range(nw):
                blk = outs[i].at[2 * px + py, _half(rows[i], 1 - c)]
                _remote(blk, blk, d2d_send, d2d_recv, 3 * i + p, (x, y, 1 - c)).wait_recv()
        for cp in sent:
            cp.wait_send()

    dma = pltpu.SemaphoreType.DMA
    return pl.pallas_call(
        body, name="gather_weights", in_specs=[_ANY] * nw, out_specs=[_ANY] * nw,
        out_shape=[jax.ShapeDtypeStruct((N_CHIPS,) + s.shape, s.dtype) for s in shards],
        scratch_shapes=[dma((3 * nw,)), dma((3 * nw,)), dma((3 * nw,)), dma((3 * nw,))],
    )(*shards)


_HBM = pl.BlockSpec(memory_space=pltpu.HBM)
_SEM = pl.BlockSpec(memory_space=pltpu.SEMAPHORE)
_DATAFLOW = pltpu.SideEffectType.DATAFLOW_SIDE_EFFECTING


def _exchange_start(shards, name, scatter):
    nw = len(shards)
    lands = [lax.empty(s.shape if scatter else (N_CHIPS,) + s.shape, s.dtype) for s in shards]

    def body(*refs):
        ins, zones = refs[:nw], refs[nw:2 * nw]
        send_sems, recv_sems, token = refs[2 * nw], refs[2 * nw + 1], refs[-1]
        x, y, c, me, peers = _place()
        for i in range(nw):
            for p, (px, py) in enumerate(peers):
                src = ins[i].at[2 * px + py] if scatter else ins[i]
                _remote(src, zones[i].at[me], send_sems, recv_sems, 3 * i + p, (px, py, c)).start()
        token[...] = jnp.zeros_like(token)

    thru = [pltpu.HBM(a.shape, a.dtype) for a in list(shards) + lands]
    dma = pltpu.SemaphoreType.DMA
    outs = pl.pallas_call(
        body, name=name,
        out_shape=(dma((3 * nw,)), dma((3 * nw,)), *thru, jax.ShapeDtypeStruct((SUBLANES, LANES), F32)),
        in_specs=[_HBM] * (2 * nw),
        out_specs=(_SEM, _SEM, *([_HBM] * (2 * nw)), pl.BlockSpec(memory_space=pltpu.VMEM)),
        input_output_aliases={i: 2 + i for i in range(2 * nw)},
        compiler_params=pltpu.CompilerParams(has_side_effects=_DATAFLOW),
    )(*[pltpu.with_memory_space_constraint(a, pltpu.HBM) for a in list(shards) + lands])
    return outs[0], outs[1], list(outs[2:2 + nw]), list(outs[2 + nw:2 + 2 * nw]), outs[-1]


def _exchange_wait(send_sems, recv_sems, shards, lands, after, name, scatter):
    nw = len(shards)

    def body(*refs):
        ins, zones = refs[:nw], refs[nw:2 * nw]
        send_sems, recv_sems = refs[2 * nw], refs[2 * nw + 1]
        x, y, c, me, peers = _place()
        for i in range(nw):
            for p, (px, py) in enumerate(peers):
                src = ins[i].at[2 * px + py] if scatter else ins[i]
                cp = _remote(src, zones[i].at[2 * px + py], send_sems, recv_sems, 3 * i + p, (px, py, c))
                cp.wait_send()
                cp.wait_recv()

    both = list(shards) + list(lands)
    outs = pl.pallas_call(
        body, name=name,
        out_shape=tuple(pltpu.HBM(a.shape, a.dtype) for a in both),
        in_specs=(*([_HBM] * (2 * nw)), _SEM, _SEM, _ANY), out_specs=[_HBM] * (2 * nw),
        input_output_aliases={i: i for i in range(2 * nw)},
        compiler_params=pltpu.CompilerParams(has_side_effects=_DATAFLOW),
    )(*both, send_sems, recv_sems, after)
    return list(outs[:nw]), list(outs[nw:])


def _sibling_start(grads, name):
    nw = len(grads)
    lands = [lax.empty((N_CHIPS, g.shape[1] // 2, g.shape[2]), g.dtype) for g in grads]

    def body(*refs):
        ins, zones = refs[:nw], refs[nw:2 * nw]
        send_sems, recv_sems, token = refs[2 * nw], refs[2 * nw + 1], refs[-1]
        x, y, c, _, _ = _place()
        for i in range(nw):
            _remote(ins[i].at[pl.ds(0, N_CHIPS), _half(grads[i].shape[1], 1 - c)], zones[i], send_sems, recv_sems,
                    i, (x, y, 1 - c)).start()
        token[...] = jnp.zeros_like(token)

    thru = [pltpu.HBM(a.shape, a.dtype) for a in list(grads) + lands]
    dma = pltpu.SemaphoreType.DMA
    outs = pl.pallas_call(
        body, name=name,
        out_shape=(dma((nw,)), dma((nw,)), *thru, jax.ShapeDtypeStruct((SUBLANES, LANES), F32)),
        in_specs=[_HBM] * (2 * nw),
        out_specs=(_SEM, _SEM, *([_HBM] * (2 * nw)), pl.BlockSpec(memory_space=pltpu.VMEM)),
        input_output_aliases={i: 2 + i for i in range(2 * nw)},
        compiler_params=pltpu.CompilerParams(has_side_effects=_DATAFLOW),
    )(*[pltpu.with_memory_space_constraint(a, pltpu.HBM) for a in list(grads) + lands])
    return outs[0], outs[1], list(outs[2:2 + nw]), list(outs[2 + nw:2 + 2 * nw]), outs[-1]


def _sibling_wait(send_sems, recv_sems, grads, lands, after, name):
    nw = len(grads)

    def body(*refs):
        ins, zones = refs[:nw], refs[nw:2 * nw]
        send_sems, recv_sems = refs[2 * nw], refs[2 * nw + 1]
        x, y, c, _, _ = _place()
        for i in range(nw):
            cp = _remote(ins[i].at[pl.ds(0, N_CHIPS), _half(grads[i].shape[1], 1 - c)], zones[i], send_sems, recv_sems,
                         i, (x, y, 1 - c))
            cp.wait_send()
            cp.wait_recv()

    both = list(grads) + list(lands)
    outs = pl.pallas_call(
        body, name=name,
        out_shape=tuple(pltpu.HBM(a.shape, a.dtype) for a in both),
        in_specs=(*([_HBM] * (2 * nw)), _SEM, _SEM, _ANY), out_specs=[_HBM] * (2 * nw),
        input_output_aliases={i: i for i in range(2 * nw)},
        compiler_params=pltpu.CompilerParams(has_side_effects=_DATAFLOW),
    )(*both, send_sems, recv_sems, after)
    return list(outs[:nw]), list(outs[nw:])


def _reduce_to_sibling(grads, name):
    nw = len(grads)

    def body(*refs):
        ins, outs = refs[:nw], refs[nw:2 * nw]
        send_sems, recv_sems = refs[2 * nw:]
        x, y, c, _, _ = _place()
        sent = []
        for i in range(nw):
            cp = _remote(ins[i].at[pl.ds(0, N_CHIPS), _half(grads[i].shape[1], 1 - c)], outs[i], send_sems, recv_sems,
                         i, (x, y, 1 - c))
            cp.start()
            sent.append(cp)
        for cp in sent:
            cp.wait()

    dma = pltpu.SemaphoreType.DMA
    return pl.pallas_call(
        body, name=name, in_specs=[_ANY] * nw, out_specs=[_ANY] * nw,
        out_shape=[jax.ShapeDtypeStruct((N_CHIPS, g.shape[1] // 2, g.shape[2]), g.dtype) for g in grads],
        scratch_shapes=[dma((nw,)), dma((nw,))],
    )(*grads)


def _reduce_back(totals, name):
    nw = len(totals)

    def body(*refs):
        outs = refs[nw:2 * nw]
        send_sems, recv_sems = refs[2 * nw:]
        x, y, c, _, _ = _place()
        sent = []
        for i in range(nw):
            blk = outs[i].at[_half(totals[i].shape[0], c)]
            cp = _remote(blk, blk, send_sems, recv_sems, i, (x, y, 1 - c))
            cp.start()
            sent.append(cp)
        for i in range(nw):
            blk = outs[i].at[_half(totals[i].shape[0], 1 - c)]
            _remote(blk, blk, send_sems, recv_sems, i, (x, y, 1 - c)).wait_recv()
        for cp in sent:
            cp.wait_send()

    dma = pltpu.SemaphoreType.DMA
    return pl.pallas_call(
        body, name=name, in_specs=[_ANY] * nw, out_specs=[_ANY] * nw,
        out_shape=[jax.ShapeDtypeStruct(t.shape, t.dtype) for t in totals],
        input_output_aliases={i: i for i in range(nw)},
        scratch_shapes=[dma((nw,)), dma((nw,))],
    )(*totals)


def _all_reduce_small(v, name):
    rows, w = v.shape
    hr = rows // 2
    assert hr % SUBLANES == 0

    def body(v_ref, out_ref, sib_ref, half_ref, chips_ref, send_sems, recv_sems):
        x, y, c, me, peers = _place()
        sibling = (x, y, 1 - c)
        mine = pl.ds(pl.multiple_of(c * hr, SUBLANES), hr)
        other = pl.ds(pl.multiple_of((1 - c) * hr, SUBLANES), hr)
        cp = _remote(v_ref, sib_ref, send_sems, recv_sems, 0, sibling)
        cp.start()
        cp.wait()
        half_ref[...] = v_ref[mine, :] + sib_ref[mine, :]
        sent = []
        for p, (px, py) in enumerate(peers):
            cp = _remote(half_ref, chips_ref.at[me], send_sems, recv_sems, 1 + p, (px, py, c))
            cp.start()
            sent.append(cp)
        chips_ref[me] = half_ref[...]
        for p, (px, py) in enumerate(peers):
            _remote(half_ref, chips_ref.at[2 * px + py], send_sems, recv_sems, 1 + p, (px, py, c)).wait_recv()
        for cp in sent:
            cp.wait_send()
        out_ref[mine, :] = ((chips_ref[0] + chips_ref[1]) + chips_ref[2]) + chips_ref[3]
        cp = _remote(out_ref.at[mine], out_ref.at[mine], send_sems, recv_sems, 4, sibling)
        cp.start()
        _remote(out_ref.at[other], out_ref.at[other], send_sems, recv_sems, 4, sibling).wait_recv()
        cp.wait_send()

    vm = pl.BlockSpec(memory_space=pltpu.VMEM)
    return pl.pallas_call(
        body, name=name, in_specs=[vm], out_specs=vm,
        out_shape=jax.ShapeDtypeStruct((rows, w), F32),
        scratch_shapes=[pltpu.VMEM((rows, w), F32), pltpu.VMEM((hr, w), F32), pltpu.VMEM((N_CHIPS, hr, w), F32),
                        pltpu.SemaphoreType.DMA((5,)), pltpu.SemaphoreType.DMA((5,))],
        compiler_params=pltpu.CompilerParams(vmem_limit_bytes=VMEM_LIMIT),
    )(v)


ELEMENTWISE_BLOCK = 512 * 1024


def _rows_tile(rows, cols):
    best = None
    for t in range(SUBLANES, rows + 1, SUBLANES):
        if rows % t == 0 and t * cols <= ELEMENTWISE_BLOCK:
            best = t
    return rows if best is None else best


def _add_pair(g, t, core, name):
    nb, n, w = t.shape
    tr = _rows_tile(n, w)
    steps = n // tr

    def body(core_ref, g_ref, t_ref, o_ref):
        o_ref[...] = (g_ref[...] + t_ref[...]).astype(BF16)

    spec = pl.BlockSpec((1, tr, w), lambda j, i, core_ref: (j, i, 0))
    return pl.pallas_call(
        body, name=name,
        grid_spec=pltpu.PrefetchScalarGridSpec(
            num_scalar_prefetch=1, grid=(nb, steps),
            in_specs=[pl.BlockSpec((1, tr, w), lambda j, i, core_ref: (j, core_ref[0] * steps + i, 0)), spec],
            out_specs=spec),
        out_shape=jax.ShapeDtypeStruct(t.shape, BF16),
        compiler_params=_params(("parallel", "parallel")))(core, g, t)


def _add_chips(landed, pairs, place, name):
    nb, n, w = landed.shape
    tr = _rows_tile(n, w)
    steps = n // tr

    def body(place_ref, r_ref, own_ref, o_ref):
        me = place_ref[0]
        acc = None
        for k in range(nb):
            blk = jnp.where(me == k, own_ref[0], r_ref[k]).astype(F32)
            acc = blk if acc is None else acc + blk
        o_ref[...] = acc

    return pl.pallas_call(
        body, name=name,
        grid_spec=pltpu.PrefetchScalarGridSpec(
            num_scalar_prefetch=1, grid=(steps,),
            in_specs=[pl.BlockSpec((nb, tr, w), lambda i, place_ref: (0, i, 0)),
                      pl.BlockSpec((1, tr, w), lambda i, place_ref: (place_ref[0], i, 0))],
            out_specs=pl.BlockSpec((tr, w), lambda i, place_ref: (place_ref[1] * steps + i, 0))),
        out_shape=jax.ShapeDtypeStruct((2 * n, w), F32),
        compiler_params=_params(("parallel",)))(place, landed, pairs)


def _adamw(w, g, m, v, name):
    rows, wd = w.shape
    tr = _rows_tile(rows, wd)
    c1 = 1.0 - ADAM_B1 ** ADAM_STEP
    c2 = 1.0 - ADAM_B2 ** ADAM_STEP

    def body(w_ref, g_ref, m_ref, v_ref, go_ref, d_ref, mo_ref, vo_ref):
        gv = g_ref[...]
        go_ref[...] = gv
        m2 = ADAM_B1 * m_ref[...] + (1.0 - ADAM_B1) * gv
        v2 = ADAM_B2 * v_ref[...] + (1.0 - ADAM_B2) * (gv * gv)
        mo_ref[...] = m2
        vo_ref[...] = v2
        d_ref[...] = -ADAM_LR * ((m2 / c1) / (jnp.sqrt(v2 / c2) + ADAM_EPS) + ADAM_WD * w_ref[...])

    spec = pl.BlockSpec((tr, wd), lambda i: (i, 0))
    shp = jax.ShapeDtypeStruct((rows, wd), F32)
    return pl.pallas_call(body, name=name, grid=(rows // tr,), in_specs=[spec] * 4, out_specs=[spec] * 4,
                          out_shape=[shp] * 4, compiler_params=_params(("parallel",)))(w, g, m, v)


BIG = [("w_in", (1024, 3232), 1), ("w_uq", (384, 768), 1), ("w_uk", (256, 512), 1), ("w_uv", (256, 512), 1),
       ("w_glu", (512, 512), 0), ("w_branch_attn", (512, 1024), 1), ("w_branch_ssm", (512, 1024), 1),
       ("w_out", (1024, 1024), 0), ("w_up", (1024, 5632), 1), ("conv_w", (3, 5632), 1), ("w_down", (2816, 1024), 0)]
SMALL = [("mix_norm_pre", (1024,)), ("q_norm", (384,)), ("kv_norm", (256,)), ("ssm_lambda_re", (32, 64)),
         ("ssm_lambda_im", (32, 64)), ("ssm_log_dt", (32,)), ("ssm_b_re", (32, 64, 16)), ("ssm_b_im", (32, 64, 16)),
         ("ssm_c_re", (32, 16, 64)), ("ssm_c_im", (32, 16, 64)), ("ssm_d", (32, 16)), ("b_glu", (512,)),
         ("b_gate", (2048,)), ("mix_norm_post", (1024,)), ("ffn_norm_pre", (1024,)), ("conv_b", (5632,)),
         ("ffn_norm_post", (1024,))]
MATMUL_W = [b for b in BIG if b[0] != "conv_w"]
LATE_W = ("w_up", "w_down", "conv_w")
CONV_W_SHAPE = (3, 2 * D_FF)
CONV_W_SHARD = (3, 2 * D_FF // N_CHIPS)
SMALL_SUM = [("loss", (1,))] + SMALL + [("conv_w", CONV_W_SHAPE)]
SMALL_ADAM = SMALL + [("conv_w", CONV_W_SHARD)]


def _pack_flat(layout, vals):
    flat = jnp.concatenate([vals[n].astype(F32).reshape(-1) for n, _ in layout])
    rows = -(-(-(-flat.shape[0] // FLAT_W)) // (2 * SUBLANES)) * 2 * SUBLANES
    return jnp.pad(flat, (0, rows * FLAT_W - flat.shape[0])).reshape(rows, FLAT_W)


def _unpack_flat(layout, flat):
    flat = flat.reshape(-1)
    out = {}
    o = 0
    for name, shape in layout:
        n = math.prod(shape)
        out[name] = flat[o:o + n].reshape(shape)
        o += n
    return out


_ARG_NAMES = ["x", "positions"] + [n for n in (
    "mix_norm_pre", "w_in", "q_norm", "w_uq", "kv_norm", "w_uk", "w_uv", "ssm_lambda_re", "ssm_lambda_im", "ssm_log_dt",
    "ssm_b_re", "ssm_b_im", "ssm_c_re", "ssm_c_im", "ssm_d", "w_glu", "b_glu", "w_branch_attn", "w_branch_ssm",
    "b_gate", "w_out", "mix_norm_post", "ffn_norm_pre", "w_up", "conv_w", "conv_b", "w_down", "ffn_norm_post")]
_WEIGHTS = _ARG_NAMES[2:]


def _gather_weights(w):
    early = [b for b in MATMUL_W if b[0] not in LATE_W]
    late = [b for b in BIG if b[0] in LATE_W]
    own = (jnp.arange(N_CHIPS) == 2 * lax.axis_index("x") + lax.axis_index("y"))[:, None, None]

    def whole(layout, mine, gathered):
        return {name: _from_chip_major(jnp.where(own, s[None], g), axis)
                for (name, _, axis), s, g in zip(layout, mine, gathered)}

    mine = [w[name].astype(BF16) for name, _, _ in early]
    gathered = _gather_big(mine)
    full = whole(early, mine, gathered)
    mine_late = [w[name].astype(F32 if name == "conv_w" else BF16) for name, _, _ in late]
    _, mine_late = lax.optimization_barrier((gathered[0], mine_late))
    send_sems, recv_sems, shards_thru, lands_thru, token = _exchange_start(mine_late, "gather_late_start", scatter=False)

    def late_weights(after):
        shards, lands = _exchange_wait(send_sems, recv_sems, shards_thru, lands_thru, after, "gather_late_wait",
                                       scatter=False)
        return whole(late, shards, lands)

    full["late"] = late_weights
    full["token"] = token[0, 0]
    return full


def _pair_sums(names, grads, tag):
    core = lax.axis_index("c").astype(jnp.int32).reshape(1)
    theirs = _reduce_to_sibling(grads, "reduce_grads_d2d" + tag)
    return [_add_pair(g, t, core, "reduce_pair_" + n) for n, g, t in zip(names, grads, theirs)]


def _send_grads(tag, grads, after, flying, pending):
    token = jnp.zeros((), F32)
    if flying:
        tag0, names0, state0 = flying.pop()
        core = lax.axis_index("c").astype(jnp.int32).reshape(1)
        mine, theirs = _sibling_wait(*state0, after, "reduce_" + tag0 + "_d2d_wait")
        pairs = [_add_pair(g, t, core, "reduce_pair_" + n) for n, g, t in zip(names0, mine, theirs)]
        send_sems, recv_sems, pairs_thru, lands_thru, tok = _exchange_start(pairs, "reduce_" + tag0 + "_start",
                                                                          scatter=True)
        pending.append((tag0, names0, send_sems, recv_sems, pairs_thru, lands_thru))
        token = token + tok[0, 0]
    if grads:
        names = list(grads)
        send_sems, recv_sems, grads_thru, lands_thru, tok = _sibling_start([grads[n] for n in names],
                                                                          "reduce_" + tag + "_d2d_start")
        flying.append((tag, names, (send_sems, recv_sems, grads_thru, lands_thru)))
        token = token + tok[0, 0]
    return token


def _reduce_grads(gbig, pending, loss, gsmall, use_sent):
    core = lax.axis_index("c").astype(jnp.int32).reshape(1)
    chip = (2 * lax.axis_index("x") + lax.axis_index("y")).astype(jnp.int32).reshape(1)
    place = jnp.concatenate([chip, core])

    def finish(names, pairs, landed, name):
        totals = [_add_chips(r, p, place, "reduce_chips_" + n) for n, r, p in zip(names, landed, pairs)]
        return dict(zip(names, _reduce_back(totals, name)))

    names = list(gbig)
    pairs = _pair_sums(names, [gbig[n] for n in names], "")
    send_sems, recv_sems, pairs_thru, lands_thru, token = _exchange_start(pairs, "reduce_last_start", scatter=True)
    sent_names, sent_pairs, sent_landed = [], [], []
    for tag, group, g_send, g_recv, g_pairs, g_lands in pending:
        got_pairs, got_landed = _exchange_wait(g_send, g_recv, g_pairs, g_lands, token, "reduce_" + tag + "_wait",
                                               scatter=True)
        sent_names, sent_pairs, sent_landed = sent_names + group, sent_pairs + got_pairs, sent_landed + got_landed
    g_sent = finish(sent_names, sent_pairs, sent_landed, "reduce_back_sent")
    vals = dict(gsmall)
    vals["loss"] = loss
    small_red = _unpack_flat(SMALL_SUM, _all_reduce_small(_pack_flat(SMALL_SUM, vals), "reduce_small"))
    after = use_sent(g_sent, small_red)
    pairs, landed = _exchange_wait(send_sems, recv_sems, pairs_thru, lands_thru, after, "reduce_last_wait", scatter=True)
    return finish(names, pairs, landed, "reduce_back_last"), small_red


def _step(args):
    x = args["x"][0]
    positions = args["positions"][0]
    tgt = args["loss_target"][0]
    w = {n: args[n][0] for n in _WEIGHTS}
    m = {n: args["m_" + n][0] for n in _WEIGHTS}
    v = {n: args["v_" + n][0] for n in _WEIGHTS}

    full = _gather_weights(w)
    sp = {n: w[n].reshape(s) for n, s in SMALL}
    for n in ("mix_norm_pre", "q_norm", "kv_norm", "b_glu", "b_gate", "mix_norm_post", "ffn_norm_pre", "conv_b",
              "ffn_norm_post"):
        sp[n] = sp[n].reshape(1, -1)
    sp["mix_norm_pre"] = sp["mix_norm_pre"] + full.pop("token")
    pending, flying = [], []
    full["send_grads"] = lambda tag, grads, after=None: _send_grads(tag, grads, after, flying, pending)
    loss, gx, gbig, gsmall = _local_step(x, positions, tgt, full, sp)
    outs = {}

    def adam_big(g_red):
        for name in g_red:
            g2, d, m2, v2 = _adamw(w[name], g_red[name], m[name], v[name], "adamw_" + name)
            outs["grad_" + name], outs["delta_" + name], outs["new_m_" + name], outs["new_v_" + name] = g2, d, m2, v2
        return v2

    def use_sent(g_sent, small_red):
        chip = 2 * lax.axis_index("x") + lax.axis_index("y")
        grads = dict(small_red)
        grads["conv_w"] = lax.dynamic_slice_in_dim(small_red["conv_w"], chip * CONV_W_SHARD[1], CONV_W_SHARD[1], axis=1)
        outs.update({"grad_" + n: grads[n] for n, _ in SMALL_ADAM})
        _, d_sm, m_sm, v_sm = _adamw(_pack_flat(SMALL_ADAM, w), _pack_flat(SMALL_ADAM, grads),
                                     _pack_flat(SMALL_ADAM, m), _pack_flat(SMALL_ADAM, v), "adamw_small")
        for prefix, flat in (("delta_", d_sm), ("new_m_", m_sm), ("new_v_", v_sm)):
            for n, val in _unpack_flat(SMALL_ADAM, flat).items():
                outs[prefix + n] = val
        return adam_big(g_sent)

    g_last, small_red = _reduce_grads(gbig, pending, loss, gsmall, use_sent)
    adam_big(g_last)
    outs = {n: val.reshape(args[n.split("_", 1)[1] if not n.startswith("new_") else n[6:]].shape)
            for n, val in outs.items()}
    res = [small_red["loss"][0], gx[None]]
    for prefix in ("grad_", "delta_", "new_m_", "new_v_"):
        res += [outs[prefix + n] for n in _WEIGHTS]
    return tuple(res)


def kernel(x, positions, mix_norm_pre, w_in, q_norm, w_uq, kv_norm, w_uk, w_uv, ssm_lambda_re, ssm_lambda_im, ssm_log_dt, ssm_b_re, ssm_b_im, ssm_c_re, ssm_c_im, ssm_d, w_glu, b_glu, w_branch_attn, w_branch_ssm, b_gate, w_out, mix_norm_post, ffn_norm_pre, w_up, conv_w, conv_b, w_down, ffn_norm_post, loss_target, m_mix_norm_pre, m_w_in, m_q_norm, m_w_uq, m_kv_norm, m_w_uk, m_w_uv, m_ssm_lambda_re, m_ssm_lambda_im, m_ssm_log_dt, m_ssm_b_re, m_ssm_b_im, m_ssm_c_re, m_ssm_c_im, m_ssm_d, m_w_glu, m_b_glu, m_w_branch_attn, m_w_branch_ssm, m_b_gate, m_w_out, m_mix_norm_post, m_ffn_norm_pre, m_w_up, m_conv_w, m_conv_b, m_w_down, m_ffn_norm_post, v_mix_norm_pre, v_w_in, v_q_norm, v_w_uq, v_kv_norm, v_w_uk, v_w_uv, v_ssm_lambda_re, v_ssm_lambda_im, v_ssm_log_dt, v_ssm_b_re, v_ssm_b_im, v_ssm_c_re, v_ssm_c_im, v_ssm_d, v_w_glu, v_b_glu, v_w_branch_attn, v_w_branch_ssm, v_b_gate, v_w_out, v_mix_norm_post, v_ffn_norm_pre, v_w_up, v_conv_w, v_conv_b, v_w_down, v_ffn_norm_post):
    given = dict(locals())
    return _step(given)
```

```python
import math

import jax
import jax.numpy as jnp
from jax import lax
from jax.experimental import pallas as pl
from jax.experimental.pallas import tpu as pltpu

F32 = jnp.float32
BF16 = jnp.bfloat16
MESH = pl.DeviceIdType.MESH

D_MODEL = 1024
N_HEADS = 8
QK_NOPE = 64
QK_ROPE = 32
QK_HEAD = QK_NOPE + QK_ROPE
V_HEAD = 64
Q_RANK = 384
KV_RANK = 256
ROPE_THETA = 10000.0
SSM_W = 512
SSM_H = 16
SSM_G = 32
SSM_P = 64
SSM_CH = SSM_G * SSM_P
D_FF = 2816
EPS = 1e-6
ADAM_LR = 0.001
ADAM_B1 = 0.9
ADAM_B2 = 0.999
ADAM_EPS = 1e-08
ADAM_WD = 0.01
ADAM_STEP = 10

LANES = 128
SUBLANES = 8
VMEM_LIMIT = 56 * 1024 * 1024

HEAD_SLOT = LANES
HP = N_HEADS * HEAD_SLOT
P_CQ, P_CKV, P_KR, P_U, P_GL, P_END = 0, 384, 640, 768, 1280, 3328
KR_LANE = 64

FLAT_W = 1024
N_CHIPS = 4


def _tile(n, cap):
    if n <= cap:
        return n
    best = None
    for t in range(LANES, cap + 1, LANES):
        if n % t == 0:
            best = t
    assert best is not None, (n, cap)
    return best


def _params(sem):
    return pltpu.CompilerParams(dimension_semantics=sem, vmem_limit_bytes=VMEM_LIMIT)


def _dot(a, b):
    return jnp.dot(a, b, preferred_element_type=F32)


def _dot_nt(a, b):
    return lax.dot_general(a, b, (((1,), (1,)), ((), ())), preferred_element_type=F32)


def _dot_tn(a, b):
    return lax.dot_general(a, b, (((0,), (0,)), ((), ())), preferred_element_type=F32)


def _rms(x, g):
    r = lax.rsqrt(jnp.mean(x * x, axis=-1, keepdims=True) + EPS)
    return x * r * g, r


def _rms_bwd(dy, x, g):
    r = lax.rsqrt(jnp.mean(x * x, axis=-1, keepdims=True) + EPS)
    dyg = dy * g
    dx = r * dyg - x * (r * r * r) * jnp.mean(dyg * x, axis=-1, keepdims=True)
    dg = jnp.sum(dy * x * r, axis=0, keepdims=True)
    return dx, dg


_GELU_K0 = math.sqrt(2.0 / math.pi)
_GELU_K1 = 0.044715


def _gelu(x):
    th = jnp.tanh(_GELU_K0 * (x + _GELU_K1 * x * x * x))
    return 0.5 * x * (1.0 + th)


def _gelu_grad(x):
    th = jnp.tanh(_GELU_K0 * (x + _GELU_K1 * x * x * x))
    return 0.5 * (1.0 + th) + 0.5 * x * (1.0 - th * th) * _GELU_K0 * (1.0 + 3.0 * _GELU_K1 * x * x)


def _sigmoid(x):
    return 1.0 / (1.0 + jnp.exp(-x))


def _rope(q, c, s):
    n = q.shape[1]
    lane = lax.broadcasted_iota(jnp.int32, q.shape, 1) % HEAD_SLOT
    sw = jnp.where(lane < KR_LANE + QK_ROPE // 2, pltpu.roll(q, n - QK_ROPE // 2, 1), pltpu.roll(q, QK_ROPE // 2, 1))
    return q * c + sw * s


def _rope_bwd(dy, c, s):
    n = dy.shape[1]
    t = dy * s
    lane = lax.broadcasted_iota(jnp.int32, dy.shape, 1) % HEAD_SLOT
    sw = jnp.where(lane < KR_LANE + QK_ROPE // 2, pltpu.roll(t, n - QK_ROPE // 2, 1), pltpu.roll(t, QK_ROPE // 2, 1))
    rope_lane = jnp.logical_and(lane >= KR_LANE, lane < KR_LANE + QK_ROPE)
    return dy * c + jnp.where(rope_lane, sw, 0.0)


def _shift_down(x, k, halo):
    xs = pltpu.roll(x, k, 0)
    hs = pltpu.roll(halo, k, 0)
    rows = lax.broadcasted_iota(jnp.int32, halo.shape, 0)
    top = jnp.where(rows < k, hs, xs[0:SUBLANES])
    return jnp.concatenate([top, xs[SUBLANES:]], axis=0)


def _shift_up(x, k, halo):
    t = x.shape[0]
    xs = pltpu.roll(x, t - k, 0)
    hs = pltpu.roll(halo, SUBLANES - k, 0)
    rows = lax.broadcasted_iota(jnp.int32, halo.shape, 0)
    bot = jnp.where(rows >= SUBLANES - k, hs, xs[t - SUBLANES:])
    return jnp.concatenate([xs[:t - SUBLANES], bot], axis=0)


def _mm(a, b, name, out_dtype=F32, bt=False, tm_cap=1024, tn_cap=1408):
    m, k = a.shape[-2:]
    parts = a.shape[0] if a.ndim == 3 else 1
    n = b.shape[0] if bt else b.shape[1]
    tm = min(tm_cap, m)
    tn = _tile(n, tn_cap)

    def body(a_ref, b_ref, o_ref):
        if not bt:
            o_ref[...] = _dot(a_ref[...], b_ref[...]).astype(out_dtype)
        elif parts == 1:
            o_ref[...] = _dot_nt(a_ref[...], b_ref[...]).astype(out_dtype)
        else:
            @pl.when(pl.program_id(2) == 0)
            def _():
                o_ref[...] = jnp.zeros_like(o_ref)

            o_ref[...] += _dot_nt(a_ref[...], b_ref[...])

    if bt:
        b_spec = pl.BlockSpec((tn, k), lambda j, i, s: (j, s))
    else:
        b_spec = pl.BlockSpec((k, tn), lambda j, i, s: (0, j))
    if a.ndim == 3:
        assert bt and out_dtype == F32
        a_spec = pl.BlockSpec((None, tm, k), lambda j, i, s: (s, i, 0))
    else:
        a_spec = pl.BlockSpec((tm, k), lambda j, i, s: (i, 0))
    return pl.pallas_call(
        body, name=name, grid=(n // tn, m // tm, parts),
        in_specs=[a_spec, b_spec],
        out_specs=pl.BlockSpec((tm, tn), lambda j, i, s: (i, j)),
        out_shape=jax.ShapeDtypeStruct((m, n), out_dtype),
        compiler_params=_params(("parallel", "parallel", "arbitrary")),
    )(a, b)


def _mm_tn(a, b, name, tk_cap=1024, tn_cap=1664, tl_cap=2048, chips=False):
    l, k = a.shape
    tk = _tile(k, tk_cap)
    tl = min(tl_cap, l)

    def body(a_ref, b_ref, o_ref):
        @pl.when(pl.program_id(2) == 0)
        def _():
            o_ref[...] = jnp.zeros_like(o_ref)

        o_ref[...] += _dot_tn(a_ref[...], b_ref[...])

    if chips:
        n = b.shape[-1] * (b.shape[0] if b.ndim == 3 else 1)
        tn = n // N_CHIPS
        assert tn % LANES == 0
        if b.ndim == 3:
            per = N_CHIPS // b.shape[0]
            b_spec = pl.BlockSpec((None, tl, tn), lambda i, j, r: (j // per, r, j % per))
        else:
            b_spec = pl.BlockSpec((tl, tn), lambda i, j, r: (r, j))
        out_spec = pl.BlockSpec((None, tk, tn), lambda i, j, r: (j, i, 0))
        out_shape = jax.ShapeDtypeStruct((N_CHIPS, k, tn), F32)
    else:
        n = b.shape[1]
        tn = _tile(n, tn_cap)
        b_spec = pl.BlockSpec((tl, tn), lambda i, j, r: (r, j))
        out_spec = pl.BlockSpec((tk, tn), lambda i, j, r: (i, j))
        out_shape = jax.ShapeDtypeStruct((k, n), F32)
    return pl.pallas_call(
        body, name=name, grid=(k // tk, n // tn, l // tl),
        in_specs=[pl.BlockSpec((tl, tk), lambda i, j, r: (r, i)), b_spec],
        out_specs=out_spec, out_shape=out_shape,
        compiler_params=_params(("parallel", "parallel", "arbitrary")),
    )(a, b)


def _row(tl, n):
    return pl.BlockSpec((tl, n), lambda i: (i, 0))


def _const(shape):
    return pl.BlockSpec(shape, lambda i: tuple(0 for _ in shape))


def _proj_fwd(x, g1, win, gq, wuq, gkv, wukv, rc, rs, bg, tl):
    l = x.shape[0]

    def body(x_ref, g1_ref, win_ref, gq_ref, wuq_ref, gkv_ref, wukv_ref, rc_ref, rs_ref, bg_ref,
             hn_ref, cq_ref, ckv_ref, q_ref, k_ref, v_ref, u_ref, gl_ref):
        hn, _ = _rms(x_ref[...], g1_ref[...])
        hnb = hn.astype(BF16)
        hn_ref[...] = hnb
        proj = _dot(hnb, win_ref[...])
        cq = proj[:, P_CQ:P_CKV]
        ckv = proj[:, P_CKV:P_KR]
        kr = proj[:, P_KR:P_U]
        cq_ref[...] = cq
        ckv_ref[...] = ckv
        u_ref[...] = proj[:, P_U:P_GL]
        gl_ref[...] = proj[:, P_GL:P_END] + bg_ref[...]
        qn, _ = _rms(cq, gq_ref[...])
        q = _dot(qn.astype(BF16), wuq_ref[...])
        c1 = rc_ref[...]
        s1 = rs_ref[...]
        q_ref[...] = (_rope(q, jnp.tile(c1, (1, N_HEADS)), jnp.tile(s1, (1, N_HEADS))) * Q_PRESCALE).astype(BF16)
        ckvn, _ = _rms(ckv, gkv_ref[...])
        kv = _dot(ckvn.astype(BF16), wukv_ref[...])
        krr = _rope(kr, c1, s1)
        k_ref[...] = (kv[:, :HP] + jnp.tile(krr, (1, N_HEADS))).astype(BF16)
        v_ref[...] = kv[:, HP:].astype(BF16)

    outs = [(D_MODEL, BF16), (Q_RANK, F32), (KV_RANK, F32), (HP, BF16), (HP, BF16), (HP, BF16),
            (SSM_W, F32), (2 * D_MODEL, F32)]
    return pl.pallas_call(
        body, name="proj_fwd", grid=(l // tl,),
        in_specs=[_row(tl, D_MODEL), _const((1, D_MODEL)), _const((D_MODEL, P_END)), _const((1, Q_RANK)),
                  _const((Q_RANK, HP)), _const((1, KV_RANK)), _const((KV_RANK, 2 * HP)),
                  _row(tl, HEAD_SLOT), _row(tl, HEAD_SLOT), _const((1, 2 * D_MODEL))],
        out_specs=[_row(tl, n) for n, _ in outs],
        out_shape=[jax.ShapeDtypeStruct((l, n), dt) for n, dt in outs],
        compiler_params=_params(("parallel",)),
    )(x, g1, win, gq, wuq, gkv, wukv, rc, rs, bg)


_NEG = -1e30


LOG2E = 1.0 / math.log(2.0)
LN2 = math.log(2.0)
ATTN_SCALE = 1.0 / math.sqrt(QK_HEAD)
Q_PRESCALE = ATTN_SCALE * LOG2E
FWD_HEADS = 8
BWD_HEADS = 8


def _causal_pairs(nq, by_query):
    if by_query:
        pairs = [(i, j) for i in range(nq) for j in range(i + 1)]
    else:
        pairs = [(i, j) for j in range(nq) for i in range(j, nq)]
    return jnp.array([p[0] for p in pairs], jnp.int32), jnp.array([p[1] for p in pairs], jnp.int32)


def _diag_mask_t(s):
    rows = lax.broadcasted_iota(jnp.int32, s.shape, 0)
    cols = lax.broadcasted_iota(jnp.int32, s.shape, 1)
    return jnp.where(rows <= cols, s, _NEG)


def _attn_fwd(q, k, v, tq, heads):
    l = q.shape[0]
    nq = l // tq
    it, jt = _causal_pairs(nq, True)

    def body(it_ref, jt_ref, q_ref, k_ref, v_ref, o_ref, lse_ref, m_ref, l_ref, acc_ref):
        t = pl.program_id(1)
        i = it_ref[t]
        j = jt_ref[t]

        @pl.when(j == 0)
        def _():
            m_ref[...] = jnp.full_like(m_ref, _NEG)
            l_ref[...] = jnp.zeros_like(l_ref)
            acc_ref[...] = jnp.zeros_like(acc_ref)

        def update(on_diagonal):
            for hh in range(heads):
                sl = slice(hh * HEAD_SLOT, (hh + 1) * HEAD_SLOT)
                s = _dot_nt(k_ref[:, sl], q_ref[:, sl])
                if on_diagonal:
                    s = _diag_mask_t(s)
                m_old = m_ref[hh]
                m_new = jnp.maximum(m_old, jnp.max(s, axis=0, keepdims=True))
                p = jnp.exp2(s - m_new)
                alpha = jnp.exp2(m_old - m_new)
                l_ref[hh] = alpha * l_ref[hh] + jnp.sum(p, axis=0, keepdims=True)
                acc_ref[hh] = alpha * acc_ref[hh] + _dot_tn(v_ref[:, sl], p.astype(BF16))
                m_ref[hh] = m_new

        @pl.when(j < i)
        def _():
            update(False)

        @pl.when(j == i)
        def _():
            update(True)
            for hh in range(heads):
                sl = slice(hh * HEAD_SLOT, (hh + 1) * HEAD_SLOT)
                o_ref[:, sl] = (acc_ref[hh] / l_ref[hh]).T.astype(BF16)
                lse_ref[hh] = m_ref[hh] + jnp.log(l_ref[hh]) * LOG2E

    blk = (tq, (heads * HEAD_SLOT))
    qmap = lambda h, t, it_ref, jt_ref: (it_ref[t], h)
    kmap = lambda h, t, it_ref, jt_ref: (jt_ref[t], h)
    row = pl.BlockSpec((heads, 1, tq), lambda h, t, it_ref, jt_ref: (h, 0, it_ref[t]))
    return pl.pallas_call(
        body, name="attn_fwd",
        grid_spec=pltpu.PrefetchScalarGridSpec(
            num_scalar_prefetch=2, grid=(N_HEADS // heads, it.shape[0]),
            in_specs=[pl.BlockSpec(blk, qmap), pl.BlockSpec(blk, kmap), pl.BlockSpec(blk, kmap)],
            out_specs=[pl.BlockSpec(blk, qmap), row],
            scratch_shapes=[pltpu.VMEM((heads, 1, tq), F32), pltpu.VMEM((heads, 1, tq), F32),
                            pltpu.VMEM((heads, HEAD_SLOT, tq), F32)]),
        out_shape=[jax.ShapeDtypeStruct((l, HP), BF16), jax.ShapeDtypeStruct((N_HEADS, 1, l), F32)],
        compiler_params=_params(("parallel", "arbitrary")),
    )(it, jt, q, k, v)


def _attn_delta(o, do, tq, heads):
    l = o.shape[0]

    def body(o_ref, do_ref, d_ref):
        prod = o_ref[...].astype(F32) * do_ref[...].astype(F32)
        for hh in range(heads):
            d_ref[hh] = jnp.sum(prod[:, hh * HEAD_SLOT:(hh + 1) * HEAD_SLOT].T, axis=0, keepdims=True)

    blk = pl.BlockSpec((tq, (heads * HEAD_SLOT)), lambda h, i: (i, h))
    return pl.pallas_call(
        body, name="attn_delta", grid=(N_HEADS // heads, l // tq), in_specs=[blk, blk],
        out_specs=pl.BlockSpec((heads, 1, tq), lambda h, i: (h, 0, i)),
        out_shape=jax.ShapeDtypeStruct((N_HEADS, 1, l), F32),
        compiler_params=_params(("parallel", "parallel")),
    )(o, do)


def _attn_bwd(q, k, v, do, lse, delta, tq, heads):
    l = q.shape[0]
    nq = l // tq
    it, jt = _causal_pairs(nq, False)

    def body(it_ref, jt_ref, q_ref, k_ref, v_ref, do_ref, lse_ref, dl_ref, dq_ref, dk_ref, dv_ref, dka_ref, dva_ref):
        t = pl.program_id(1)
        i = it_ref[t]
        j = jt_ref[t]

        @pl.when(t == 0)
        def _():
            dq_ref[...] = jnp.zeros_like(dq_ref)

        @pl.when(i == j)
        def _():
            dka_ref[...] = jnp.zeros_like(dka_ref)
            dva_ref[...] = jnp.zeros_like(dva_ref)

        def update(on_diagonal):
            r0 = pl.multiple_of(i * tq, tq)
            for hh in range(heads):
                sl = slice(hh * HEAD_SLOT, (hh + 1) * HEAD_SLOT)
                qb = q_ref[:, sl]
                kb = k_ref[:, sl]
                dob = do_ref[:, sl]
                s = _dot_nt(kb, qb)
                if on_diagonal:
                    s = _diag_mask_t(s)
                p = jnp.exp2(s - lse_ref[hh])
                dva_ref[:, sl] += _dot(p.astype(BF16), dob)
                dp = _dot_nt(v_ref[:, sl], dob)
                ds = (p * (dp - dl_ref[hh])).astype(BF16)
                dka_ref[:, sl] += _dot(ds, qb)
                dq_ref[pl.ds(r0, tq), sl] += ATTN_SCALE * _dot_tn(ds, kb)

        @pl.when(j < i)
        def _():
            update(False)

        @pl.when(j == i)
        def _():
            update(True)

        @pl.when(i == nq - 1)
        def _():
            dk_ref[...] = (dka_ref[...] * LN2).astype(BF16)
            dv_ref[...] = dva_ref[...].astype(BF16)

    blk = (tq, (heads * HEAD_SLOT))
    qmap = lambda h, t, it_ref, jt_ref: (it_ref[t], h)
    kmap = lambda h, t, it_ref, jt_ref: (jt_ref[t], h)
    row = pl.BlockSpec((heads, 1, tq), lambda h, t, it_ref, jt_ref: (h, 0, it_ref[t]))
    return pl.pallas_call(
        body, name="attn_bwd",
        grid_spec=pltpu.PrefetchScalarGridSpec(
            num_scalar_prefetch=2, grid=(N_HEADS // heads, it.shape[0]),
            in_specs=[pl.BlockSpec(blk, qmap), pl.BlockSpec(blk, kmap), pl.BlockSpec(blk, kmap),
                      pl.BlockSpec(blk, qmap), row, row],
            out_specs=[pl.BlockSpec((l, (heads * HEAD_SLOT)), lambda h, t, it_ref, jt_ref: (0, h)), pl.BlockSpec(blk, kmap),
                       pl.BlockSpec(blk, kmap)],
            scratch_shapes=[pltpu.VMEM(blk, F32), pltpu.VMEM(blk, F32)]),
        out_shape=[jax.ShapeDtypeStruct((l, HP), F32), jax.ShapeDtypeStruct((l, HP), BF16),
                   jax.ShapeDtypeStruct((l, HP), BF16)],
        compiler_params=_params(("parallel", "arbitrary")),
    )(it, jt, q, k, v, do, lse, delta)


SSM_CB = 512
SSM_UB = 128
SSM_NB = SSM_CH // SSM_CB


def _scan_tiles(re_ref, im_ref, tab, carry, n_tiles, reverse):
    group = 2
    assert n_tiles % group == 0
    pr, pi = tab[6], tab[7]

    def inside(sr, si):
        for step, k in enumerate((1, 2, 4)):
            mr, mi = tab[2 * step], tab[2 * step + 1]
            sh = (SUBLANES - k) if reverse else k
            rr = pltpu.roll(sr, sh, 0)
            ri = pltpu.roll(si, sh, 0)
            sr, si = sr + mr * rr - mi * ri, si + mr * ri + mi * rr
        return sr, si

    def body(n, c):
        cr, ci = c
        first = (n_tiles - group * (n + 1)) if reverse else group * n
        r0 = pl.multiple_of(first * SUBLANES, group * SUBLANES)
        rows = [pl.ds(r0 + g * SUBLANES, SUBLANES) for g in range(group)]
        tiles = [inside(re_ref[r, :], im_ref[r, :]) for r in rows]
        for g in (range(group - 1, -1, -1) if reverse else range(group)):
            sr, si = tiles[g]
            sr, si = sr + pr * cr - pi * ci, si + pr * ci + pi * cr
            re_ref[rows[g], :] = sr
            im_ref[rows[g], :] = si
            edge = slice(0, 1) if reverse else slice(SUBLANES - 1, SUBLANES)
            cr, ci = sr[edge, :], si[edge, :]
        return cr, ci

    return lax.fori_loop(0, n_tiles // group, body, carry)


def _ssm_fwd(u, bre, bim, cre, cim, dvec, tab, tt):
    l = u.shape[0]
    nt = l // tt

    def body(u_ref, bre_ref, bim_ref, cre_ref, cim_ref, d_ref, tab_ref, y_ref, sre_ref, sim_ref, car_ref):
        @pl.when(pl.program_id(1) == 0)
        def _():
            car_ref[...] = jnp.zeros_like(car_ref)

        uf = u_ref[...]
        ub = uf.astype(BF16)
        sre_ref[...] = _dot(ub, bre_ref[0])
        sim_ref[...] = _dot(ub, bim_ref[0])
        tab_v = [tab_ref[n] for n in range(8)]
        cr, ci = _scan_tiles(sre_ref, sim_ref, tab_v, (car_ref[0:1, :], car_ref[8:9, :]), tt // SUBLANES, False)
        car_ref[0:1, :] = cr
        car_ref[8:9, :] = ci
        y_ref[...] = (_dot(sre_ref[...].astype(BF16), cre_ref[0]) - _dot(sim_ref[...].astype(BF16), cim_ref[0])
                      + d_ref[...] * uf)

    return pl.pallas_call(
        body, name="ssm_fwd", grid=(SSM_NB, nt),
        in_specs=[pl.BlockSpec((tt, SSM_UB), lambda m, t: (t, m)),
                  pl.BlockSpec((1, SSM_UB, SSM_CB), lambda m, t: (m, 0, 0)),
                  pl.BlockSpec((1, SSM_UB, SSM_CB), lambda m, t: (m, 0, 0)),
                  pl.BlockSpec((1, SSM_CB, SSM_UB), lambda m, t: (m, 0, 0)),
                  pl.BlockSpec((1, SSM_CB, SSM_UB), lambda m, t: (m, 0, 0)),
                  pl.BlockSpec((1, SSM_UB), lambda m, t: (0, m)),
                  pl.BlockSpec((8, SUBLANES, SSM_CB), lambda m, t: (0, 0, m))],
        out_specs=[pl.BlockSpec((tt, SSM_UB), lambda m, t: (t, m)),
                   pl.BlockSpec((tt, SSM_CB), lambda m, t: (t, m)),
                   pl.BlockSpec((tt, SSM_CB), lambda m, t: (t, m))],
        out_shape=[jax.ShapeDtypeStruct((l, SSM_W), F32), jax.ShapeDtypeStruct((l, SSM_CH), F32),
                   jax.ShapeDtypeStruct((l, SSM_CH), F32)],
        scratch_shapes=[pltpu.VMEM((2 * SUBLANES, SSM_CB), F32)],
        compiler_params=_params(("parallel", "arbitrary")),
    )(u, bre, bim, cre, cim, dvec, tab)


def _ssm_bwd(dy, u, sre, sim, bre, bim, cre, cim, dvec, tab, tt):
    l = u.shape[0]
    nt = l // tt
    tpb = tt // SUBLANES

    def body(dy_ref, u_ref, sre_ref, sim_ref, hre_ref, him_ref, bre_ref, bim_ref, cre_ref, cim_ref, d_ref, tab_ref,
             du_ref, dbre_ref, dbim_ref, dcre_ref, dcim_ref, dare_ref, daim_ref, dd_ref, lr_ref, li_ref, car_ref):
        t = pl.program_id(1)

        @pl.when(t == 0)
        def _():
            car_ref[...] = jnp.zeros_like(car_ref)
            for ref in (dbre_ref, dbim_ref, dcre_ref, dcim_ref, dare_ref, daim_ref, dd_ref):
                ref[...] = jnp.zeros_like(ref)

        dyf = dy_ref[...]
        dyb = dyf.astype(BF16)
        uf = u_ref[...]
        s_re = sre_ref[...]
        s_im = sim_ref[...]
        lr_ref[...] = _dot_nt(dyb, cre_ref[0])
        li_ref[...] = -_dot_nt(dyb, cim_ref[0])
        dcre_ref[0] += _dot_tn(s_re.astype(BF16), dyb)
        dcim_ref[0] -= _dot_tn(s_im.astype(BF16), dyb)
        tab_v = [tab_ref[n] for n in range(8)]
        cr, ci = _scan_tiles(lr_ref, li_ref, tab_v, (car_ref[0:1, :], car_ref[8:9, :]), tpb, True)
        car_ref[0:1, :] = cr
        car_ref[8:9, :] = ci
        lam_r = lr_ref[...]
        lam_i = li_ref[...]
        keep = jnp.where(t == nt - 1, 0.0, 1.0)
        sp_r = _shift_down(s_re, 1, hre_ref[...] * keep)
        sp_i = _shift_down(s_im, 1, him_ref[...] * keep)
        dare_ref[...] += jnp.sum(lam_r * sp_r + lam_i * sp_i, axis=0, keepdims=True)
        daim_ref[...] += jnp.sum(lam_i * sp_r - lam_r * sp_i, axis=0, keepdims=True)
        lrb = lam_r.astype(BF16)
        lib = lam_i.astype(BF16)
        du_ref[...] = _dot_nt(lrb, bre_ref[0]) + _dot_nt(lib, bim_ref[0]) + dyf * d_ref[...]
        ub = uf.astype(BF16)
        dbre_ref[0] += _dot_tn(ub, lrb)
        dbim_ref[0] += _dot_tn(ub, lib)
        dd_ref[...] += jnp.sum(dyf * uf, axis=0, keepdims=True)

    rev = lambda m, t: (nt - 1 - t, m)
    halo = lambda m, t: (jnp.maximum((nt - 1 - t) * tpb - 1, 0), m)
    wb = pl.BlockSpec((1, SSM_UB, SSM_CB), lambda m, t: (m, 0, 0))
    wc = pl.BlockSpec((1, SSM_CB, SSM_UB), lambda m, t: (m, 0, 0))
    vec_c = pl.BlockSpec((1, SSM_CB), lambda m, t: (0, m))
    vec_u = pl.BlockSpec((1, SSM_UB), lambda m, t: (0, m))
    return pl.pallas_call(
        body, name="ssm_bwd", grid=(SSM_NB, nt),
        in_specs=[pl.BlockSpec((tt, SSM_UB), rev), pl.BlockSpec((tt, SSM_UB), rev),
                  pl.BlockSpec((tt, SSM_CB), rev), pl.BlockSpec((tt, SSM_CB), rev),
                  pl.BlockSpec((SUBLANES, SSM_CB), halo), pl.BlockSpec((SUBLANES, SSM_CB), halo),
                  wb, wb, wc, wc, vec_u,
                  pl.BlockSpec((8, SUBLANES, SSM_CB), lambda m, t: (0, 0, m))],
        out_specs=[pl.BlockSpec((tt, SSM_UB), rev), wb, wb, wc, wc, vec_c, vec_c, vec_u],
        out_shape=[jax.ShapeDtypeStruct((l, SSM_W), F32),
                   jax.ShapeDtypeStruct((SSM_NB, SSM_UB, SSM_CB), F32), jax.ShapeDtypeStruct((SSM_NB, SSM_UB, SSM_CB), F32),
                   jax.ShapeDtypeStruct((SSM_NB, SSM_CB, SSM_UB), F32), jax.ShapeDtypeStruct((SSM_NB, SSM_CB, SSM_UB), F32),
                   jax.ShapeDtypeStruct((1, SSM_CH), F32), jax.ShapeDtypeStruct((1, SSM_CH), F32),
                   jax.ShapeDtypeStruct((1, SSM_W), F32)],
        scratch_shapes=[pltpu.VMEM((tt, SSM_CB), F32), pltpu.VMEM((tt, SSM_CB), F32),
                        pltpu.VMEM((2 * SUBLANES, SSM_CB), F32)],
        compiler_params=_params(("parallel", "arbitrary")),
    )(dy, u, sre, sim, sre, sim, bre, bim, cre, cim, dvec, tab)


def _merge_fwd(x, gl, attn, y1, wba, wbs, wglu, bglu, wout, gpost, gpre, tl):
    l = x.shape[0]

    def body(x_ref, gl_ref, at_ref, y1_ref, wba_ref, wbs_ref, wglu_ref, bglu_ref, wout_ref, gpost_ref, gpre_ref,
             a_ref, sm_ref, mg_ref, z_ref, x1_ref, hn2_ref, y3_ref):
        y2 = _gelu(y1_ref[...])
        sg = _sigmoid(_dot(y2.astype(BF16), wglu_ref[...]) + bglu_ref[...])
        y3 = (y2 * sg).astype(BF16)
        y3_ref[...] = y3
        a = _dot(at_ref[...], wba_ref[...])
        sm = _dot(y3, wbs_ref[...])
        a_ref[...] = a.astype(BF16)
        sm_ref[...] = sm.astype(BF16)
        g = _sigmoid(gl_ref[...])
        merged = (g[:, :D_MODEL] * a + g[:, D_MODEL:] * sm).astype(BF16)
        mg_ref[...] = merged
        z = _dot(merged, wout_ref[...])
        z_ref[...] = z
        n, _ = _rms(z, gpost_ref[...])
        x1 = x_ref[...] + n
        x1_ref[...] = x1
        hn2, _ = _rms(x1, gpre_ref[...])
        hn2_ref[...] = hn2.astype(BF16)

    outs = [(D_MODEL, BF16), (D_MODEL, BF16), (D_MODEL, BF16), (D_MODEL, F32), (D_MODEL, F32), (D_MODEL, BF16),
            (SSM_W, BF16)]
    return pl.pallas_call(
        body, name="merge_fwd", grid=(l // tl,),
        in_specs=[_row(tl, D_MODEL), _row(tl, 2 * D_MODEL), _row(tl, HP), _row(tl, SSM_W),
                  _const((HP, D_MODEL)), _const((SSM_W, D_MODEL)), _const((SSM_W, SSM_W)), _const((1, SSM_W)),
                  _const((D_MODEL, D_MODEL)), _const((1, D_MODEL)), _const((1, D_MODEL))],
        out_specs=[_row(tl, n) for n, _ in outs],
        out_shape=[jax.ShapeDtypeStruct((l, n), dt) for n, dt in outs],
        compiler_params=_params(("parallel",)),
    )(x, gl, attn, y1, wba, wbs, wglu, bglu, wout, gpost, gpre)


def _merge_bwd(dhn2, x1, dx2, z, gl, a, sm, y1, wba, wbs, wglu, bglu, wout, gpost, gpre, tl):
    l = x1.shape[0]

    def body(dhn2_ref, x1_ref, dx2_ref, z_ref, gl_ref, a_ref, sm_ref, y1_ref,
             wba_ref, wbs_ref, wglu_ref, bglu_ref, wout_ref, gpost_ref, gpre_ref,
             dx1_ref, dz_ref, dbra_ref, dbrs_ref, dgl_ref, dat_ref, dy1_ref, dt_ref, y2_ref,
             dgpre_ref, dgpost_ref, dbg_ref, dbglu_ref):
        @pl.when(pl.program_id(0) == 0)
        def _():
            for ref in (dgpre_ref, dgpost_ref, dbg_ref, dbglu_ref):
                ref[...] = jnp.zeros_like(ref)

        dx1a, dgpre = _rms_bwd(dhn2_ref[...], x1_ref[...], gpre_ref[...])
        dgpre_ref[...] += dgpre
        dx1 = dx2_ref[...] + dx1a
        dx1_ref[...] = dx1
        dz, dgpost = _rms_bwd(dx1, z_ref[...], gpost_ref[...])
        dgpost_ref[...] += dgpost
        dzb = dz.astype(BF16)
        dz_ref[...] = dzb
        dm = _dot_nt(dzb, wout_ref[...])
        g = _sigmoid(gl_ref[...])
        g0 = g[:, :D_MODEL]
        g1 = g[:, D_MODEL:]
        dbra = (dm * g0).astype(BF16)
        dbrs = (dm * g1).astype(BF16)
        dbra_ref[...] = dbra
        dbrs_ref[...] = dbrs
        dgl0 = dm * a_ref[...].astype(F32) * g0 * (1.0 - g0)
        dgl1 = dm * sm_ref[...].astype(F32) * g1 * (1.0 - g1)
        dgl_ref[:, :D_MODEL] = dgl0.astype(BF16)
        dgl_ref[:, D_MODEL:] = dgl1.astype(BF16)
        dbg_ref[:, :D_MODEL] += jnp.sum(dgl0, axis=0, keepdims=True)
        dbg_ref[:, D_MODEL:] += jnp.sum(dgl1, axis=0, keepdims=True)
        dat_ref[...] = _dot_nt(dbra, wba_ref[...]).astype(BF16)
        dy3 = _dot_nt(dbrs, wbs_ref[...])
        y1v = y1_ref[...]
        y2 = _gelu(y1v)
        y2b = y2.astype(BF16)
        y2_ref[...] = y2b
        sg = _sigmoid(_dot(y2b, wglu_ref[...]) + bglu_ref[...])
        dt = dy3 * y2 * sg * (1.0 - sg)
        dtb = dt.astype(BF16)
        dt_ref[...] = dtb
        dbglu_ref[...] += jnp.sum(dt, axis=0, keepdims=True)
        dy2 = dy3 * sg + _dot_nt(dtb, wglu_ref[...])
        dy1_ref[...] = dy2 * _gelu_grad(y1v)

    outs = [(D_MODEL, F32), (D_MODEL, BF16), (D_MODEL, BF16), (D_MODEL, BF16), (2 * D_MODEL, BF16), (HP, BF16),
            (SSM_W, F32), (SSM_W, BF16), (SSM_W, BF16)]
    accs = [D_MODEL, D_MODEL, 2 * D_MODEL, SSM_W]
    return pl.pallas_call(
        body, name="merge_bwd", grid=(l // tl,),
        in_specs=[_row(tl, D_MODEL), _row(tl, D_MODEL), _row(tl, D_MODEL), _row(tl, D_MODEL),
                  _row(tl, 2 * D_MODEL), _row(tl, D_MODEL), _row(tl, D_MODEL), _row(tl, SSM_W),
                  _const((HP, D_MODEL)), _const((SSM_W, D_MODEL)), _const((SSM_W, SSM_W)), _const((1, SSM_W)),
                  _const((D_MODEL, D_MODEL)), _const((1, D_MODEL)), _const((1, D_MODEL))],
        out_specs=[_row(tl, n) for n, _ in outs] + [_const((1, n)) for n in accs],
        out_shape=[jax.ShapeDtypeStruct((l, n), dt) for n, dt in outs]
        + [jax.ShapeDtypeStruct((1, n), F32) for n in accs],
        compiler_params=_params(("arbitrary",)),
    )(dhn2, x1, dx2, z, gl, a, sm, y1, wba, wbs, wglu, bglu, wout, gpost, gpre)


def _proj_bwd(x, dx1, cq, ckv, dq, dk, dv, du, dgl, g1, win, gq, wuq, gkv, wukv, rc, rs, tl):
    l = x.shape[0]

    def body(x_ref, dx1_ref, cq_ref, ckv_ref, dq_ref, dk_ref, dv_ref, du_ref, dgl_ref,
             g1_ref, win_ref, gq_ref, wuq_ref, gkv_ref, wukv_ref, rc_ref, rs_ref,
             gx_ref, dql_ref, qn_ref, ckvn_ref, dproj_ref, dg1_ref, dgq_ref, dgkv_ref):
        @pl.when(pl.program_id(0) == 0)
        def _():
            for ref in (dg1_ref, dgq_ref, dgkv_ref):
                ref[...] = jnp.zeros_like(ref)

        c1 = rc_ref[...]
        s1 = rs_ref[...]
        dql = _rope_bwd(dq_ref[...], jnp.tile(c1, (1, N_HEADS)), jnp.tile(s1, (1, N_HEADS))).astype(BF16)
        dql_ref[...] = dql
        dqn = _dot_nt(dql, wuq_ref[...])
        cq = cq_ref[...]
        qn, _ = _rms(cq, gq_ref[...])
        qn_ref[...] = qn.astype(BF16)
        dcq, dgq = _rms_bwd(dqn, cq, gq_ref[...])
        dgq_ref[...] += dgq
        dkb = dk_ref[...]
        dvb = dv_ref[...]
        dkf = dkb.astype(F32)
        dkr = dkf[:, 0:HEAD_SLOT]
        for h in range(1, N_HEADS):
            dkr = dkr + dkf[:, h * HEAD_SLOT:(h + 1) * HEAD_SLOT]
        dkr = _rope_bwd(dkr, c1, s1)
        dckvn = _dot_nt(dkb, wukv_ref[:, :HP]) + _dot_nt(dvb, wukv_ref[:, HP:])
        ckv = ckv_ref[...]
        ckvn, _ = _rms(ckv, gkv_ref[...])
        ckvn_ref[...] = ckvn.astype(BF16)
        dckv, dgkv = _rms_bwd(dckvn, ckv, gkv_ref[...])
        dgkv_ref[...] += dgkv
        dproj_ref[:, P_CQ:P_CKV] = dcq.astype(BF16)
        dproj_ref[:, P_CKV:P_KR] = dckv.astype(BF16)
        dproj_ref[:, P_KR:P_U] = dkr.astype(BF16)
        dproj_ref[:, P_U:P_GL] = du_ref[...].astype(BF16)
        dproj_ref[:, P_GL:P_END] = dgl_ref[...]
        dhn = _dot_nt(dproj_ref[...], win_ref[...])
        dxa, dg1 = _rms_bwd(dhn, x_ref[...], g1_ref[...])
        dg1_ref[...] += dg1
        gx_ref[...] = dx1_ref[...] + dxa

    outs = [(D_MODEL, F32), (HP, BF16), (Q_RANK, BF16), (KV_RANK, BF16), (P_END, BF16)]
    accs = [D_MODEL, Q_RANK, KV_RANK]
    return pl.pallas_call(
        body, name="proj_bwd", grid=(l // tl,),
        in_specs=[_row(tl, D_MODEL), _row(tl, D_MODEL), _row(tl, Q_RANK), _row(tl, KV_RANK), _row(tl, HP),
                  _row(tl, HP), _row(tl, HP), _row(tl, SSM_W), _row(tl, 2 * D_MODEL),
                  _const((1, D_MODEL)), _const((D_MODEL, P_END)), _const((1, Q_RANK)), _const((Q_RANK, HP)),
                  _const((1, KV_RANK)), _const((KV_RANK, 2 * HP)), _row(tl, HEAD_SLOT), _row(tl, HEAD_SLOT)],
        out_specs=[_row(tl, n) for n, _ in outs] + [_const((1, n)) for n in accs],
        out_shape=[jax.ShapeDtypeStruct((l, n), dt) for n, dt in outs]
        + [jax.ShapeDtypeStruct((1, n), F32) for n in accs],
        compiler_params=_params(("arbitrary",)),
    )(x, dx1, cq, ckv, dq, dk, dv, du, dgl, g1, win, gq, wuq, gkv, wukv, rc, rs)


CONV_CB = 256
CONV_NB = D_FF // CONV_CB
CONV_ROWS = 16


def _conv3(h, halo, w, b):
    return b + w[0:1, :] * _shift_down(h, 2, halo) + w[1:2, :] * _shift_down(h, 1, halo) + w[2:3, :] * h


def _conv_fwd(h, cw, cb, tl):
    l = h.shape[0]

    def body(hg_ref, hv_ref, wg_ref, wv_ref, bg_ref, bv_ref, act_ref, halo_ref):
        @pl.when(pl.program_id(1) == 0)
        def _():
            halo_ref[...] = jnp.zeros_like(halo_ref)

        hg = hg_ref[...]
        hv = hv_ref[...]
        cg = _conv3(hg, halo_ref[0:SUBLANES, :], wg_ref[...], bg_ref[...])
        cv = _conv3(hv, halo_ref[SUBLANES:, :], wv_ref[...], bv_ref[...])
        act_ref[...] = (_gelu(cg) * cv).astype(BF16)
        halo_ref[0:SUBLANES, :] = hg[tl - SUBLANES:, :]
        halo_ref[SUBLANES:, :] = hv[tl - SUBLANES:, :]

    gmap = lambda c, r: (r, c)
    vmap = lambda c, r: (r, CONV_NB + c)
    return pl.pallas_call(
        body, name="conv_fwd", grid=(CONV_NB, l // tl),
        in_specs=[pl.BlockSpec((tl, CONV_CB), gmap), pl.BlockSpec((tl, CONV_CB), vmap),
                  pl.BlockSpec((3, CONV_CB), lambda c, r: (0, c)), pl.BlockSpec((3, CONV_CB), lambda c, r: (0, CONV_NB + c)),
                  pl.BlockSpec((1, CONV_CB), lambda c, r: (0, c)), pl.BlockSpec((1, CONV_CB), lambda c, r: (0, CONV_NB + c))],
        out_specs=pl.BlockSpec((tl, CONV_CB), gmap),
        out_shape=jax.ShapeDtypeStruct((l, D_FF), BF16),
        scratch_shapes=[pltpu.VMEM((2 * SUBLANES, CONV_CB), F32)],
        compiler_params=_params(("parallel", "arbitrary")),
    )(h, h, cw, cw, cb, cb)


def _conv_bwd(h, dact, cw, cb, tl):
    l = h.shape[0]
    nr = l // tl
    tpb = tl // SUBLANES

    def body(hg_ref, hv_ref, hgh_ref, hvh_ref, da_ref, wg_ref, wv_ref, bg_ref, bv_ref,
             dh_ref, dwg_ref, dwv_ref, dbg_ref, dbv_ref, car_ref):
        r = pl.program_id(1)

        @pl.when(r == 0)
        def _():
            for ref in (car_ref, dwg_ref, dwv_ref, dbg_ref, dbv_ref):
                ref[...] = jnp.zeros_like(ref)

        keep = jnp.where(r == nr - 1, 0.0, 1.0)
        wg, wv, bg, bv = wg_ref[...], wv_ref[...], bg_ref[...], bv_ref[...]
        nch = tl // CONV_ROWS

        def fold(x):
            s = x[0:SUBLANES, :]
            for k in range(1, CONV_ROWS // SUBLANES):
                s = s + x[k * SUBLANES:(k + 1) * SUBLANES, :]
            return s

        def chunk(n, carry):
            ncg, ncv, acc = carry
            idx = nch - 1 - n
            r0 = pl.multiple_of(idx * CONV_ROWS, CONV_ROWS)
            rows = pl.ds(r0, CONV_ROWS)
            before = pl.ds(pl.multiple_of(jnp.maximum(r0 - SUBLANES, 0), SUBLANES), SUBLANES)
            in_tile = idx > 0
            da = da_ref[rows, :].astype(F32)

            def half(h_ref, halo_ref, w, b):
                hh = h_ref[rows, :]
                prev = jnp.where(in_tile, h_ref[before, :], halo_ref[...] * keep)
                h1 = _shift_down(hh, 1, prev)
                h2 = _shift_down(hh, 2, prev)
                return hh, h1, h2, b + w[0:1, :] * h2 + w[1:2, :] * h1 + w[2:3, :] * hh

            hg, hg1, hg2, cg = half(hg_ref, hgh_ref, wg, bg)
            hv, hv1, hv2, cv = half(hv_ref, hvh_ref, wv, bv)
            dcg = da * cv * _gelu_grad(cg)
            dcv = da * _gelu(cg)

            def back(dc, hh, h1, h2, w, nxt, part):
                dh = w[2:3, :] * dc + w[1:2, :] * _shift_up(dc, 1, nxt) + w[0:1, :] * _shift_up(dc, 2, nxt)
                dh_ref[part, rows, :] = dh.astype(BF16)
                return [fold(dc * h2), fold(dc * h1), fold(dc * hh), fold(dc)]

            sums = back(dcg, hg, hg1, hg2, wg, ncg, 0) + back(dcv, hv, hv1, hv2, wv, ncv, 1)
            return dcg[0:SUBLANES, :], dcv[0:SUBLANES, :], [a + s for a, s in zip(acc, sums)]

        zero = jnp.zeros((SUBLANES, CONV_CB), F32)
        ncg, ncv, acc = lax.fori_loop(0, nch, chunk, (car_ref[0:SUBLANES, :], car_ref[SUBLANES:, :], [zero] * 8))
        car_ref[0:SUBLANES, :] = ncg
        car_ref[SUBLANES:, :] = ncv
        for half_acc, dw_ref, db_ref in ((acc[0:4], dwg_ref, dbg_ref), (acc[4:8], dwv_ref, dbv_ref)):
            for k in range(3):
                dw_ref[k:k + 1, :] += jnp.sum(half_acc[k], axis=0, keepdims=True)
            db_ref[...] += jnp.sum(half_acc[3], axis=0, keepdims=True)

    grev = lambda c, r: (nr - 1 - r, c)
    vrev = lambda c, r: (nr - 1 - r, CONV_NB + c)
    ghalo = lambda c, r: (jnp.maximum((nr - 1 - r) * tpb - 1, 0), c)
    vhalo = lambda c, r: (jnp.maximum((nr - 1 - r) * tpb - 1, 0), CONV_NB + c)
    colg = lambda c, r: (0, c)
    colv = lambda c, r: (0, CONV_NB + c)
    return pl.pallas_call(
        body, name="conv_bwd", grid=(CONV_NB, nr),
        in_specs=[pl.BlockSpec((tl, CONV_CB), grev), pl.BlockSpec((tl, CONV_CB), vrev),
                  pl.BlockSpec((SUBLANES, CONV_CB), ghalo), pl.BlockSpec((SUBLANES, CONV_CB), vhalo),
                  pl.BlockSpec((tl, CONV_CB), grev),
                  pl.BlockSpec((3, CONV_CB), colg), pl.BlockSpec((3, CONV_CB), colv),
                  pl.BlockSpec((1, CONV_CB), colg), pl.BlockSpec((1, CONV_CB), colv)],
        out_specs=[pl.BlockSpec((2, tl, CONV_CB), lambda c, r: (0, nr - 1 - r, c)),
                   pl.BlockSpec((3, CONV_CB), colg), pl.BlockSpec((3, CONV_CB), colg),
                   pl.BlockSpec((1, CONV_CB), colg), pl.BlockSpec((1, CONV_CB), colg)],
        out_shape=[jax.ShapeDtypeStruct((2, l, D_FF), BF16),
                   jax.ShapeDtypeStruct((3, D_FF), F32), jax.ShapeDtypeStruct((3, D_FF), F32),
                   jax.ShapeDtypeStruct((1, D_FF), F32), jax.ShapeDtypeStruct((1, D_FF), F32)],
        scratch_shapes=[pltpu.VMEM((2 * SUBLANES, CONV_CB), F32)],
        compiler_params=_params(("parallel", "arbitrary")),
    )(h, h, h, h, dact, cw, cw, cb, cb)


def _loss_head(ff, x1, tgt, g, tl):
    l = ff.shape[0]

    def body(ff_ref, x1_ref, tg_ref, g_ref, loss_ref, dx2_ref, dff_ref, dg_ref):
        @pl.when(pl.program_id(0) == 0)
        def _():
            loss_ref[...] = jnp.zeros_like(loss_ref)
            dg_ref[...] = jnp.zeros_like(dg_ref)

        f = ff_ref[...]
        gv = g_ref[...]
        n, _ = _rms(f, gv)
        e = x1_ref[...] + n - tg_ref[...]
        loss_ref[...] += 0.5 * jnp.sum(jnp.mean(e * e, axis=-1, keepdims=True), axis=0, keepdims=True)
        dx2 = e * (1.0 / D_MODEL)
        dx2_ref[...] = dx2
        dff, dg = _rms_bwd(dx2, f, gv)
        dff_ref[...] = dff.astype(BF16)
        dg_ref[...] += dg

    return pl.pallas_call(
        body, name="loss_head", grid=(l // tl,),
        in_specs=[_row(tl, D_MODEL), _row(tl, D_MODEL), _row(tl, D_MODEL), _const((1, D_MODEL))],
        out_specs=[_const((1, LANES)), _row(tl, D_MODEL), _row(tl, D_MODEL), _const((1, D_MODEL))],
        out_shape=[jax.ShapeDtypeStruct((1, LANES), F32), jax.ShapeDtypeStruct((l, D_MODEL), F32),
                   jax.ShapeDtypeStruct((l, D_MODEL), BF16), jax.ShapeDtypeStruct((1, D_MODEL), F32)],
        compiler_params=_params(("arbitrary",)),
    )(ff, x1, tgt, g)


def _ssm_disc(lam_re, lam_im, log_dt, b_re, b_im):
    dt = jnp.exp(log_dt)[:, None]
    mag = jnp.exp(lam_re * dt)
    ang = lam_im * dt
    a_re, a_im = mag * jnp.cos(ang), mag * jnp.sin(ang)
    den = lam_re * lam_re + lam_im * lam_im
    n_re, n_im = a_re - 1.0, a_im
    z_re = (n_re * lam_re + n_im * lam_im) / den
    z_im = (n_im * lam_re - n_re * lam_im) / den
    bb_re = z_re[..., None] * b_re - z_im[..., None] * b_im
    bb_im = z_re[..., None] * b_im + z_im[..., None] * b_re
    return a_re, a_im, bb_re, bb_im


_GPB = SSM_CB // SSM_P


def _embed_b(bb):
    t = bb.transpose(0, 2, 1).reshape(SSM_NB, _GPB, SSM_H, SSM_P)
    return jnp.einsum('mjhp,jk->mjhkp', t, jnp.eye(_GPB, dtype=bb.dtype)).reshape(SSM_NB, SSM_UB, SSM_CB)


def _extract_b(d):
    t = d.reshape(SSM_NB, _GPB, SSM_H, _GPB, SSM_P)
    t = jnp.einsum('mjhkp,jk->mjhp', t, jnp.eye(_GPB, dtype=d.dtype))
    return t.reshape(SSM_G, SSM_H, SSM_P).transpose(0, 2, 1)


def _embed_c(c):
    t = c.transpose(0, 2, 1).reshape(SSM_NB, _GPB, SSM_P, SSM_H)
    return jnp.einsum('mjph,jk->mjpkh', t, jnp.eye(_GPB, dtype=c.dtype)).reshape(SSM_NB, SSM_CB, SSM_UB)


def _extract_c(d):
    t = d.reshape(SSM_NB, _GPB, SSM_P, _GPB, SSM_H)
    t = jnp.einsum('mjpkh,jk->mjph', t, jnp.eye(_GPB, dtype=d.dtype))
    return t.reshape(SSM_G, SSM_P, SSM_H).transpose(0, 2, 1)


def _scan_tables(a_re, a_im, reverse):
    ar = a_re.reshape(1, SSM_CH)
    ai = (-a_im if reverse else a_im).reshape(1, SSM_CH)
    pr, pi = [ar], [ai]
    for _ in range(SUBLANES - 1):
        pr, pi = pr + [pr[-1] * ar - pi[-1] * ai], pi + [pr[-1] * ai + pi[-1] * ar]
    rows = jnp.arange(SUBLANES)[:, None]
    out = []
    for k in (1, 2, 4):
        valid = (rows + k <= SUBLANES - 1) if reverse else (rows >= k)
        out += [jnp.where(valid, pr[k - 1], 0.0), jnp.where(valid, pi[k - 1], 0.0)]
    order = list(range(SUBLANES - 1, -1, -1)) if reverse else list(range(SUBLANES))
    out += [jnp.concatenate([pr[n] for n in order], axis=0), jnp.concatenate([pi[n] for n in order], axis=0)]
    return jnp.stack(out).astype(F32)


def _pad_heads(w, d):
    lead = w.shape[:-1]
    w = w.reshape(lead + (N_HEADS, d))
    w = jnp.pad(w, [(0, 0)] * len(lead) + [(0, 0), (0, HEAD_SLOT - d)])
    return w.reshape(lead + (HP,))


def _unpad_heads(w, d):
    lead = w.shape[:-1]
    return w.reshape(lead + (N_HEADS, HEAD_SLOT))[..., :d].reshape(lead + (N_HEADS * d,))


def _chip_major(w, axis):
    k, n = w.shape
    if axis == 0:
        return w.reshape(N_CHIPS, k // N_CHIPS, n)
    return w.reshape(k, N_CHIPS, n // N_CHIPS).transpose(1, 0, 2)


def _from_chip_major(w, axis):
    if axis == 0:
        return w.reshape(-1, w.shape[2])
    return w.transpose(1, 0, 2).reshape(w.shape[1], -1)


def _pad_w_in(w):
    z = lambda n: jnp.zeros((w.shape[0], n), w.dtype)
    return jnp.concatenate([w[:, :640], z(KR_LANE), w[:, 640:672], z(HEAD_SLOT - KR_LANE - QK_ROPE), w[:, 672:]], axis=1)


def _unpad_w_in(w):
    return jnp.concatenate([w[:, :640], w[:, P_KR + KR_LANE:P_KR + KR_LANE + QK_ROPE], w[:, P_U:]], axis=1)


def _local_step(x, positions, tgt, wts, sp):
    l = x.shape[0]
    tl = min(256, l)
    ta = min(512, l)
    ts = min(2048, l)

    inv_freq = ROPE_THETA ** (-jnp.arange(0, QK_ROPE, 2, dtype=F32) / QK_ROPE)
    ang = positions.astype(F32)[:, None] * inv_freq
    cos, sin = jnp.cos(ang), jnp.sin(ang)
    one = jnp.ones((l, KR_LANE), F32)
    rc = jnp.concatenate([one, cos, cos, jnp.ones((l, HEAD_SLOT - KR_LANE - QK_ROPE), F32)], axis=1)
    rs = jnp.concatenate([0 * one, -sin, sin, jnp.zeros((l, HEAD_SLOT - KR_LANE - QK_ROPE), F32)], axis=1)

    win = _pad_w_in(wts["w_in"])
    wuq = _pad_heads(wts["w_uq"], QK_HEAD)
    wukv = jnp.concatenate([_pad_heads(wts["w_uk"], QK_NOPE), _pad_heads(wts["w_uv"], V_HEAD)], axis=1)

    disc_in = (sp["ssm_lambda_re"], sp["ssm_lambda_im"], sp["ssm_log_dt"], sp["ssm_b_re"], sp["ssm_b_im"])
    (a_re, a_im, bb_re, bb_im), disc_vjp = jax.vjp(_ssm_disc, *disc_in)
    bre, bim = _embed_b(bb_re).astype(BF16), _embed_b(bb_im).astype(BF16)
    cre, cim = _embed_c(sp["ssm_c_re"]).astype(BF16), _embed_c(sp["ssm_c_im"]).astype(BF16)
    dvec = sp["ssm_d"].reshape(1, SSM_W)
    tab_f = _scan_tables(a_re, a_im, False)
    tab_r = _scan_tables(a_re, a_im, True)

    g1, gq, gkv = sp["mix_norm_pre"], sp["q_norm"], sp["kv_norm"]
    gpost, gpre, gfin = sp["mix_norm_post"], sp["ffn_norm_pre"], sp["ffn_norm_post"]
    bgate, bglu, convb = sp["b_gate"], sp["b_glu"], sp["conv_b"]

    hn, cq, ckv, q, k, v, u, gl = _proj_fwd(x, g1, win, gq, wuq, gkv, wukv, rc, rs, bgate, tl)
    attn, lse = _attn_fwd(q, k, v, ta, FWD_HEADS)
    y1, sre, sim = _ssm_fwd(u, bre, bim, cre, cim, dvec, tab_f, ts)
    wba = jnp.pad(wts["w_branch_attn"].reshape(N_HEADS, V_HEAD, D_MODEL),
                  ((0, 0), (0, HEAD_SLOT - V_HEAD), (0, 0))).reshape(HP, D_MODEL)
    wbs, wglu, wout = wts["w_branch_ssm"], wts["w_glu"], wts["w_out"]
    a, sm, merged, z, x1, hn2, y3 = _merge_fwd(x, gl, attn, y1, wba, wbs, wglu, bglu, wout, gpost, gpre, tl)
    late = wts["late"](x1)
    wup, wdown, convw = late["w_up"], late["w_down"], late["conv_w"]
    h = _mm(hn2, wup, "ffn_up")
    act = _conv_fwd(h, convw, convb, ts)
    ff = _mm(act, wdown, "ffn_down")
    loss, dx2, dff, dgfin = _loss_head(ff, x1, tgt, gfin, ta)

    dact = _mm(dff, wdown, "ffn_down_dx", out_dtype=BF16, bt=True)
    d_wdown = _mm_tn(act, dff, "ffn_down_dw", tk_cap=D_FF // 2)
    dh, dwg, dwv, dbg, dbv = _conv_bwd(h, dact, convw, convb, ts)
    d_convw = jnp.concatenate([dwg, dwv], axis=1)
    d_convb = jnp.concatenate([dbg, dbv], axis=1)
    dhn2 = _mm(dh, wup, "ffn_up_dx", bt=True)
    d_wup = _mm_tn(hn2, dh, "ffn_up_dw", chips=True)
    behind = wts["send_grads"]("ffn", {"w_up": d_wup, "w_down": _chip_major(d_wdown, 0)})
    (dx1, dz, dbra, dbrs, dgl, dattn, dy1, dt, y2, dgpre, dgpost, dbgate, dbglu) = _merge_bwd(
        dhn2, x1, dx2, z, gl, a, sm, y1, wba, wbs, wglu, bglu, wout, gpost, gpre + behind, tl)
    d_wout = _mm_tn(merged, dz, "w_out_dw")
    d_wba = _mm_tn(attn, dbra, "w_branch_attn_dw", chips=True)
    d_wbs = _mm_tn(y3, dbrs, "w_branch_ssm_dw", chips=True)
    d_wglu = _mm_tn(y2, dt, "w_glu_dw")
    ncol = D_MODEL // N_CHIPS
    behind = wts["send_grads"]("mix", {
        "w_glu": _chip_major(d_wglu, 0),
        "w_branch_attn": d_wba.reshape(N_CHIPS, N_HEADS, HEAD_SLOT, ncol)[:, :, :V_HEAD].reshape(
            N_CHIPS, N_HEADS * V_HEAD, ncol),
        "w_branch_ssm": d_wbs,
        "w_out": _chip_major(d_wout, 0)}, after=d_wglu)
    dq, dk, dv = _attn_bwd(q, k, v, dattn, lse + behind, _attn_delta(attn, dattn, min(2048, l), BWD_HEADS), ta,
                            BWD_HEADS)
    du, dbre, dbim, dcre, dcim, dare, daim, dd = _ssm_bwd(dy1, u, sre, sim, bre, bim, cre, cim, dvec, tab_r, ts)
    behind = wts["send_grads"]("none", {}, after=du)
    gx, dql, qn, ckvn, dproj, dg1, dgq, dgkv = _proj_bwd(
        x, dx1, cq, ckv, dq, dk, dv, du, dgl, g1 + behind, win, gq, wuq, gkv, wukv, rc, rs, tl)
    d_win = _mm_tn(hn, dproj, "w_in_dw")
    d_wuq = _mm_tn(qn, dql, "w_uq_dw")
    d_wuk = _mm_tn(ckvn, dk, "w_uk_dw")
    d_wuv = _mm_tn(ckvn, dv, "w_uv_dw")

    d_lre, d_lim, d_ldt, d_bre, d_bim = disc_vjp((dare.reshape(SSM_G, SSM_P), daim.reshape(SSM_G, SSM_P),
                                                  _extract_b(dbre), _extract_b(dbim)))
    big = {
        "w_in": _chip_major(_unpad_w_in(d_win), 1),
        "w_uq": _chip_major(_unpad_heads(d_wuq, QK_HEAD), 1),
        "w_uk": _chip_major(_unpad_heads(d_wuk, QK_NOPE), 1),
        "w_uv": _chip_major(_unpad_heads(d_wuv, V_HEAD), 1),
    }
    small = {
        "conv_w": d_convw,
        "mix_norm_pre": dg1, "q_norm": dgq, "kv_norm": dgkv,
        "ssm_lambda_re": d_lre, "ssm_lambda_im": d_lim, "ssm_log_dt": d_ldt,
        "ssm_b_re": d_bre, "ssm_b_im": d_bim,
        "ssm_c_re": _extract_c(dcre), "ssm_c_im": _extract_c(dcim),
        "ssm_d": dd.reshape(SSM_G, SSM_H), "b_glu": dbglu, "b_gate": dbgate,
        "mix_norm_post": dgpost, "ffn_norm_pre": dgpre, "conv_b": d_convb, "ffn_norm_post": dgfin,
    }
    return loss[0, 0], gx, big, small


_ANY = pl.BlockSpec(memory_space=pl.ANY)


ROW_TILE = 16


def _place():
    x, y, c = lax.axis_index("x"), lax.axis_index("y"), lax.axis_index("c")
    return x, y, c, 2 * x + y, [(1 - x, y), (x, 1 - y), (1 - x, 1 - y)]


def _half(rows, which):
    hr = rows // 2
    return pl.ds(pl.multiple_of(which * hr, ROW_TILE), hr)


def _remote(src, dst, send_sems, recv_sems, n, dev):
    return pltpu.make_async_remote_copy(src_ref=src, dst_ref=dst, send_sem=send_sems.at[n], recv_sem=recv_sems.at[n],
                                        device_id=dev, device_id_type=MESH)


def _gather_big(shards):
    nw = len(shards)
    rows = [s.shape[0] for s in shards]

    def body(*refs):
        ins, outs = refs[:nw], refs[nw:2 * nw]
        ici_send, ici_recv, d2d_send, d2d_recv = refs[2 * nw:]
        x, y, c, me, peers = _place()
        sent = []
        for i in range(nw):
            for p, (px, py) in enumerate(peers):
                cp = _remote(ins[i].at[_half(rows[i], c)], outs[i].at[me, _half(rows[i], c)], ici_send, ici_recv,
                             3 * i + p, (px, py, c))
                cp.start()
                sent.append(cp)
        for p, (px, py) in enumerate(peers):
            for i in range(nw):
                blk = outs[i].at[2 * px + py, _half(rows[i], c)]
                _remote(blk, blk, ici_send, ici_recv, 3 * i + p, (px, py, c)).wait_recv()
                cp = _remote(blk, blk, d2d_send, d2d_recv, 3 * i + p, (x, y, 1 - c))
                cp.start()
                sent.append(cp)
        for p, (px, py) in enumerate(peers):
            for i in range(nw):
                blk = outs[i].at[2 * px + py, _half(rows[i], 1 - c)]
                _remote(blk, blk, d2d_send, d2d_recv, 3 * i + p, (x, y, 1 - c)).wait_recv()
        for cp in sent:
            cp.wait_send()

    dma = pltpu.SemaphoreType.DMA
    return pl.pallas_call(
        body, name="gather_weights", in_specs=[_ANY] * nw, out_specs=[_ANY] * nw,
        out_shape=[jax.ShapeDtypeStruct((N_CHIPS,) + s.shape, s.dtype) for s in shards],
        scratch_shapes=[dma((3 * nw,)), dma((3 * nw,)), dma((3 * nw,)), dma((3 * nw,))],
    )(*shards)


_HBM = pl.BlockSpec(memory_space=pltpu.HBM)
_SEM = pl.BlockSpec(memory_space=pltpu.SEMAPHORE)
_DATAFLOW = pltpu.SideEffectType.DATAFLOW_SIDE_EFFECTING


def _exchange_start(shards, name, scatter):
    nw = len(shards)
    lands = [lax.empty(s.shape if scatter else (N_CHIPS,) + s.shape, s.dtype) for s in shards]

    def body(*refs):
        ins, zones = refs[:nw], refs[nw:2 * nw]
        send_sems, recv_sems, token = refs[2 * nw], refs[2 * nw + 1], refs[-1]
        x, y, c, me, peers = _place()
        for i in range(nw):
            for p, (px, py) in enumerate(peers):
                src = ins[i].at[2 * px + py] if scatter else ins[i]
                _remote(src, zones[i].at[me], send_sems, recv_sems, 3 * i + p, (px, py, c)).start()
        token[...] = jnp.zeros_like(token)

    thru = [pltpu.HBM(a.shape, a.dtype) for a in list(shards) + lands]
    dma = pltpu.SemaphoreType.DMA
    outs = pl.pallas_call(
        body, name=name,
        out_shape=(dma((3 * nw,)), dma((3 * nw,)), *thru, jax.ShapeDtypeStruct((SUBLANES, LANES), F32)),
        in_specs=[_HBM] * (2 * nw),
        out_specs=(_SEM, _SEM, *([_HBM] * (2 * nw)), pl.BlockSpec(memory_space=pltpu.VMEM)),
        input_output_aliases={i: 2 + i for i in range(2 * nw)},
        compiler_params=pltpu.CompilerParams(has_side_effects=_DATAFLOW),
    )(*[pltpu.with_memory_space_constraint(a, pltpu.HBM) for a in list(shards) + lands])
    return outs[0], outs[1], list(outs[2:2 + nw]), list(outs[2 + nw:2 + 2 * nw]), outs[-1]


def _exchange_wait(send_sems, recv_sems, shards, lands, after, name, scatter):
    nw = len(shards)

    def body(*refs):
        ins, zones = refs[:nw], refs[nw:2 * nw]
        send_sems, recv_sems = refs[2 * nw], refs[2 * nw + 1]
        x, y, c, me, peers = _place()
        for i in range(nw):
            for p, (px, py) in enumerate(peers):
                src = ins[i].at[2 * px + py] if scatter else ins[i]
                cp = _remote(src, zones[i].at[2 * px + py], send_sems, recv_sems, 3 * i + p, (px, py, c))
                cp.wait_send()
                cp.wait_recv()

    both = list(shards) + list(lands)
    outs = pl.pallas_call(
        body, name=name,
        out_shape=tuple(pltpu.HBM(a.shape, a.dtype) for a in both),
        in_specs=(*([_HBM] * (2 * nw)), _SEM, _SEM, _ANY), out_specs=[_HBM] * (2 * nw),
        input_output_aliases={i: i for i in range(2 * nw)},
        compiler_params=pltpu.CompilerParams(has_side_effects=_DATAFLOW),
    )(*both, send_sems, recv_sems, after)
    return list(outs[:nw]), list(outs[nw:])


def _sibling_start(grads, name):
    nw = len(grads)
    lands = [lax.empty((N_CHIPS, g.shape[1] // 2, g.shape[2]), g.dtype) for g in grads]

    def body(*refs):
        ins, zones = refs[:nw], refs[nw:2 * nw]
        send_sems, recv_sems, token = refs[2 * nw], refs[2 * nw + 1], refs[-1]
        x, y, c, _, _ = _place()
        for i in range(nw):
            _remote(ins[i].at[pl.ds(0, N_CHIPS), _half(grads[i].shape[1], 1 - c)], zones[i], send_sems, recv_sems,
                    i, (x, y, 1 - c)).start()
        token[...] = jnp.zeros_like(token)

    thru = [pltpu.HBM(a.shape, a.dtype) for a in list(grads) + lands]
    dma = pltpu.SemaphoreType.DMA
    outs = pl.pallas_call(
        body, name=name,
        out_shape=(dma((nw,)), dma((nw,)), *thru, jax.ShapeDtypeStruct((SUBLANES, LANES), F32)),
        in_specs=[_HBM] * (2 * nw),
        out_specs=(_SEM, _SEM, *([_HBM] * (2 * nw)), pl.BlockSpec(memory_space=pltpu.VMEM)),
        input_output_aliases={i: 2 + i for i in range(2 * nw)},
        compiler_params=pltpu.CompilerParams(has_side_effects=_DATAFLOW),
    )(*[pltpu.with_memory_space_constraint(a, pltpu.HBM) for a in list(grads) + lands])
    return outs[0], outs[1], list(outs[2:2 + nw]), list(outs[2 + nw:2 + 2 * nw]), outs[-1]


def _sibling_wait(send_sems, recv_sems, grads, lands, after, name):
    nw = len(grads)

    def body(*refs):
        ins, zones = refs[:nw], refs[nw:2 * nw]
        send_sems, recv_sems = refs[2 * nw], refs[2 * nw + 1]
        x, y, c, _, _ = _place()
        for i in range(nw):
            cp = _remote(ins[i].at[pl.ds(0, N_CHIPS), _half(grads[i].shape[1], 1 - c)], zones[i], send_sems, recv_sems,
                         i, (x, y, 1 - c))
            cp.wait_send()
            cp.wait_recv()

    both = list(grads) + list(lands)
    outs = pl.pallas_call(
        body, name=name,
        out_shape=tuple(pltpu.HBM(a.shape, a.dtype) for a in both),
        in_specs=(*([_HBM] * (2 * nw)), _SEM, _SEM, _ANY), out_specs=[_HBM] * (2 * nw),
        input_output_aliases={i: i for i in range(2 * nw)},
        compiler_params=pltpu.CompilerParams(has_side_effects=_DATAFLOW),
    )(*both, send_sems, recv_sems, after)
    return list(outs[:nw]), list(outs[nw:])


def _reduce_to_sibling(grads, name):
    nw = len(grads)

    def body(*refs):
        ins, outs = refs[:nw], refs[nw:2 * nw]
        send_sems, recv_sems = refs[2 * nw:]
        x, y, c, _, _ = _place()
        sent = []
        for i in range(nw):
            cp = _remote(ins[i].at[pl.ds(0, N_CHIPS), _half(grads[i].shape[1], 1 - c)], outs[i], send_sems, recv_sems,
                         i, (x, y, 1 - c))
            cp.start()
            sent.append(cp)
        for cp in sent:
            cp.wait()

    dma = pltpu.SemaphoreType.DMA
    return pl.pallas_call(
        body, name=name, in_specs=[_ANY] * nw, out_specs=[_ANY] * nw,
        out_shape=[jax.ShapeDtypeStruct((N_CHIPS, g.shape[1] // 2, g.shape[2]), g.dtype) for g in grads],
        scratch_shapes=[dma((nw,)), dma((nw,))],
    )(*grads)


def _reduce_back(totals, name):
    nw = len(totals)

    def body(*refs):
        outs = refs[nw:2 * nw]
        send_sems, recv_sems = refs[2 * nw:]
        x, y, c, _, _ = _place()
        sent = []
        for i in range(nw):
            blk = outs[i].at[_half(totals[i].shape[0], c)]
            cp = _remote(blk, blk, send_sems, recv_sems, i, (x, y, 1 - c))
            cp.start()
            sent.append(cp)
        for i in range(nw):
            blk = outs[i].at[_half(totals[i].shape[0], 1 - c)]
            _remote(blk, blk, send_sems, recv_sems, i, (x, y, 1 - c)).wait_recv()
        for cp in sent:
            cp.wait_send()

    dma = pltpu.SemaphoreType.DMA
    return pl.pallas_call(
        body, name=name, in_specs=[_ANY] * nw, out_specs=[_ANY] * nw,
        out_shape=[jax.ShapeDtypeStruct(t.shape, t.dtype) for t in totals],
        input_output_aliases={i: i for i in range(nw)},
        scratch_shapes=[dma((nw,)), dma((nw,))],
    )(*totals)


def _all_reduce_small(v, name):
    rows, w = v.shape
    hr = rows // 2
    assert hr % SUBLANES == 0

    def body(v_ref, out_ref, sib_ref, half_ref, chips_ref, send_sems, recv_sems):
        x, y, c, me, peers = _place()
        sibling = (x, y, 1 - c)
        mine = pl.ds(pl.multiple_of(c * hr, SUBLANES), hr)
        other = pl.ds(pl.multiple_of((1 - c) * hr, SUBLANES), hr)
        cp = _remote(v_ref, sib_ref, send_sems, recv_sems, 0, sibling)
        cp.start()
        cp.wait()
        half_ref[...] = v_ref[mine, :] + sib_ref[mine, :]
        sent = []
        for p, (px, py) in enumerate(peers):
            cp = _remote(half_ref, chips_ref.at[me], send_sems, recv_sems, 1 + p, (px, py, c))
            cp.start()
            sent.append(cp)
        chips_ref[me] = half_ref[...]
        for p, (px, py) in enumerate(peers):
            _remote(half_ref, chips_ref.at[2 * px + py], send_sems, recv_sems, 1 + p, (px, py, c)).wait_recv()
        for cp in sent:
            cp.wait_send()
        out_ref[mine, :] = ((chips_ref[0] + chips_ref[1]) + chips_ref[2]) + chips_ref[3]
        cp = _remote(out_ref.at[mine], out_ref.at[mine], send_sems, recv_sems, 4, sibling)
        cp.start()
        _remote(out_ref.at[other], out_ref.at[other], send_sems, recv_sems, 4, sibling).wait_recv()
        cp.wait_send()

    vm = pl.BlockSpec(memory_space=pltpu.VMEM)
    return pl.pallas_call(
        body, name=name, in_specs=[vm], out_specs=vm,
        out_shape=jax.ShapeDtypeStruct((rows, w), F32),
        scratch_shapes=[pltpu.VMEM((rows, w), F32), pltpu.VMEM((hr, w), F32), pltpu.VMEM((N_CHIPS, hr, w), F32),
                        pltpu.SemaphoreType.DMA((5,)), pltpu.SemaphoreType.DMA((5,))],
        compiler_params=pltpu.CompilerParams(vmem_limit_bytes=VMEM_LIMIT),
    )(v)


ELEMENTWISE_BLOCK = 512 * 1024


def _rows_tile(rows, cols, cap=ELEMENTWISE_BLOCK):
    best = None
    for t in range(SUBLANES, rows + 1, SUBLANES):
        if rows % t == 0 and t * cols <= cap:
            best = t
    return rows if best is None else best


def _add_pair(g, t, core, name):
    nb, n, w = t.shape
    tr = _rows_tile(n, w, 2 * ELEMENTWISE_BLOCK)
    steps = n // tr

    def body(core_ref, g_ref, t_ref, o_ref):
        o_ref[...] = (g_ref[...] + t_ref[...]).astype(BF16)

    spec = pl.BlockSpec((1, tr, w), lambda j, i, core_ref: (j, i, 0))
    return pl.pallas_call(
        body, name=name,
        grid_spec=pltpu.PrefetchScalarGridSpec(
            num_scalar_prefetch=1, grid=(nb, steps),
            in_specs=[pl.BlockSpec((1, tr, w), lambda j, i, core_ref: (j, core_ref[0] * steps + i, 0)), spec],
            out_specs=spec),
        out_shape=jax.ShapeDtypeStruct(t.shape, BF16),
        compiler_params=_params(("parallel", "parallel")))(core, g, t)


def _add_chips(landed, pairs, place, name):
    nb, n, w = landed.shape
    tr = _rows_tile(n, w)
    steps = n // tr

    def body(place_ref, r_ref, own_ref, o_ref):
        me = place_ref[0]
        acc = None
        for k in range(nb):
            blk = jnp.where(me == k, own_ref[0], r_ref[k]).astype(F32)
            acc = blk if acc is None else acc + blk
        o_ref[...] = acc

    return pl.pallas_call(
        body, name=name,
        grid_spec=pltpu.PrefetchScalarGridSpec(
            num_scalar_prefetch=1, grid=(steps,),
            in_specs=[pl.BlockSpec((nb, tr, w), lambda i, place_ref: (0, i, 0)),
                      pl.BlockSpec((1, tr, w), lambda i, place_ref: (place_ref[0], i, 0))],
            out_specs=pl.BlockSpec((tr, w), lambda i, place_ref: (place_ref[1] * steps + i, 0))),
        out_shape=jax.ShapeDtypeStruct((2 * n, w), F32),
        compiler_params=_params(("parallel",)))(place, landed, pairs)


def _adamw(w, g, m, v, name):
    rows, wd = w.shape
    tr = _rows_tile(rows, wd)
    c1 = 1.0 - ADAM_B1 ** ADAM_STEP
    c2 = 1.0 - ADAM_B2 ** ADAM_STEP

    def body(w_ref, g_ref, m_ref, v_ref, go_ref, d_ref, mo_ref, vo_ref):
        gv = g_ref[...]
        go_ref[...] = gv
        m2 = ADAM_B1 * m_ref[...] + (1.0 - ADAM_B1) * gv
        v2 = ADAM_B2 * v_ref[...] + (1.0 - ADAM_B2) * (gv * gv)
        mo_ref[...] = m2
        vo_ref[...] = v2
        d_ref[...] = -ADAM_LR * ((m2 / c1) / (jnp.sqrt(v2 / c2) + ADAM_EPS) + ADAM_WD * w_ref[...])

    spec = pl.BlockSpec((tr, wd), lambda i: (i, 0))
    shp = jax.ShapeDtypeStruct((rows, wd), F32)
    return pl.pallas_call(body, name=name, grid=(rows // tr,), in_specs=[spec] * 4, out_specs=[spec] * 4,
                          out_shape=[shp] * 4, compiler_params=_params(("parallel",)))(w, g, m, v)


BIG = [("w_in", (1024, 3232), 1), ("w_uq", (384, 768), 1), ("w_uk", (256, 512), 1), ("w_uv", (256, 512), 1),
       ("w_glu", (512, 512), 0), ("w_branch_attn", (512, 1024), 1), ("w_branch_ssm", (512, 1024), 1),
       ("w_out", (1024, 1024), 0), ("w_up", (1024, 5632), 1), ("conv_w", (3, 5632), 1), ("w_down", (2816, 1024), 0)]
SMALL = [("mix_norm_pre", (1024,)), ("q_norm", (384,)), ("kv_norm", (256,)), ("ssm_lambda_re", (32, 64)),
         ("ssm_lambda_im", (32, 64)), ("ssm_log_dt", (32,)), ("ssm_b_re", (32, 64, 16)), ("ssm_b_im", (32, 64, 16)),
         ("ssm_c_re", (32, 16, 64)), ("ssm_c_im", (32, 16, 64)), ("ssm_d", (32, 16)), ("b_glu", (512,)),
         ("b_gate", (2048,)), ("mix_norm_post", (1024,)), ("ffn_norm_pre", (1024,)), ("conv_b", (5632,)),
         ("ffn_norm_post", (1024,))]
MATMUL_W = [b for b in BIG if b[0] != "conv_w"]
LATE_W = ("w_up", "w_down", "conv_w")
CONV_W_SHAPE = (3, 2 * D_FF)
CONV_W_SHARD = (3, 2 * D_FF // N_CHIPS)
SMALL_SUM = [("loss", (1,))] + SMALL + [("conv_w", CONV_W_SHAPE)]
SMALL_ADAM = SMALL + [("conv_w", CONV_W_SHARD)]


def _pack_flat(layout, vals):
    flat = jnp.concatenate([vals[n].astype(F32).reshape(-1) for n, _ in layout])
    rows = -(-(-(-flat.shape[0] // FLAT_W)) // (2 * SUBLANES)) * 2 * SUBLANES
    return jnp.pad(flat, (0, rows * FLAT_W - flat.shape[0])).reshape(rows, FLAT_W)


def _unpack_flat(layout, flat):
    flat = flat.reshape(-1)
    out = {}
    o = 0
    for name, shape in layout:
        n = math.prod(shape)
        out[name] = flat[o:o + n].reshape(shape)
        o += n
    return out


_ARG_NAMES = ["x", "positions"] + [n for n in (
    "mix_norm_pre", "w_in", "q_norm", "w_uq", "kv_norm", "w_uk", "w_uv", "ssm_lambda_re", "ssm_lambda_im", "ssm_log_dt",
    "ssm_b_re", "ssm_b_im", "ssm_c_re", "ssm_c_im", "ssm_d", "w_glu", "b_glu", "w_branch_attn", "w_branch_ssm",
    "b_gate", "w_out", "mix_norm_post", "ffn_norm_pre", "w_up", "conv_w", "conv_b", "w_down", "ffn_norm_post")]
_WEIGHTS = _ARG_NAMES[2:]


def _gather_weights(w):
    early = [b for b in MATMUL_W if b[0] not in LATE_W]
    late = [b for b in BIG if b[0] in LATE_W]
    own = (jnp.arange(N_CHIPS) == 2 * lax.axis_index("x") + lax.axis_index("y"))[:, None, None]

    def whole(layout, mine, gathered):
        return {name: _from_chip_major(jnp.where(own, s[None], g), axis)
                for (name, _, axis), s, g in zip(layout, mine, gathered)}

    mine = [w[name].astype(BF16) for name, _, _ in early]
    gathered = _gather_big(mine)
    full = whole(early, mine, gathered)
    mine_late = [w[name].astype(F32 if name == "conv_w" else BF16) for name, _, _ in late]
    _, mine_late = lax.optimization_barrier((gathered[0], mine_late))
    send_sems, recv_sems, shards_thru, lands_thru, token = _exchange_start(mine_late, "gather_late_start", scatter=False)

    def late_weights(after):
        shards, lands = _exchange_wait(send_sems, recv_sems, shards_thru, lands_thru, after, "gather_late_wait",
                                       scatter=False)
        return whole(late, shards, lands)

    full["late"] = late_weights
    full["token"] = token[0, 0]
    return full


def _pair_sums(names, grads, tag):
    core = lax.axis_index("c").astype(jnp.int32).reshape(1)
    theirs = _reduce_to_sibling(grads, "reduce_grads_d2d" + tag)
    return [_add_pair(g, t, core, "reduce_pair_" + n) for n, g, t in zip(names, grads, theirs)]


def _send_grads(tag, grads, after, flying, pending):
    token = jnp.zeros((), F32)
    if flying:
        tag0, names0, state0 = flying.pop()
        core = lax.axis_index("c").astype(jnp.int32).reshape(1)
        mine, theirs = _sibling_wait(*state0, after, "reduce_" + tag0 + "_d2d_wait")
        pairs = [_add_pair(g, t, core, "reduce_pair_" + n) for n, g, t in zip(names0, mine, theirs)]
        send_sems, recv_sems, pairs_thru, lands_thru, tok = _exchange_start(pairs, "reduce_" + tag0 + "_start",
                                                                          scatter=True)
        pending.append((tag0, names0, send_sems, recv_sems, pairs_thru, lands_thru))
        token = token + tok[0, 0]
    if grads:
        names = list(grads)
        send_sems, recv_sems, grads_thru, lands_thru, tok = _sibling_start([grads[n] for n in names],
                                                                          "reduce_" + tag + "_d2d_start")
        flying.append((tag, names, (send_sems, recv_sems, grads_thru, lands_thru)))
        token = token + tok[0, 0]
    return token


def _reduce_grads(gbig, pending, loss, gsmall, use_sent):
    core = lax.axis_index("c").astype(jnp.int32).reshape(1)
    chip = (2 * lax.axis_index("x") + lax.axis_index("y")).astype(jnp.int32).reshape(1)
    place = jnp.concatenate([chip, core])

    def finish(names, pairs, landed, name):
        totals = [_add_chips(r, p, place, "reduce_chips_" + n) for n, r, p in zip(names, landed, pairs)]
        return dict(zip(names, _reduce_back(totals, name)))

    names = list(gbig)
    pairs = _pair_sums(names, [gbig[n] for n in names], "")
    send_sems, recv_sems, pairs_thru, lands_thru, token = _exchange_start(pairs, "reduce_last_start", scatter=True)
    sent_names, sent_pairs, sent_landed = [], [], []
    for tag, group, g_send, g_recv, g_pairs, g_lands in pending:
        got_pairs, got_landed = _exchange_wait(g_send, g_recv, g_pairs, g_lands, token, "reduce_" + tag + "_wait",
                                               scatter=True)
        sent_names, sent_pairs, sent_landed = sent_names + group, sent_pairs + got_pairs, sent_landed + got_landed
    g_sent = finish(sent_names, sent_pairs, sent_landed, "reduce_back_sent")
    vals = dict(gsmall)
    vals["loss"] = loss
    small_red = _unpack_flat(SMALL_SUM, _all_reduce_small(_pack_flat(SMALL_SUM, vals), "reduce_small"))
    after = use_sent(g_sent, small_red)
    pairs, landed = _exchange_wait(send_sems, recv_sems, pairs_thru, lands_thru, after, "reduce_last_wait", scatter=True)
    return finish(names, pairs, landed, "reduce_back_last"), small_red


def _step(args):
    x = args["x"][0]
    positions = args["positions"][0]
    tgt = args["loss_target"][0]
    w = {n: args[n][0] for n in _WEIGHTS}
    m = {n: args["m_" + n][0] for n in _WEIGHTS}
    v = {n: args["v_" + n][0] for n in _WEIGHTS}

    full = _gather_weights(w)
    sp = {n: w[n].reshape(s) for n, s in SMALL}
    for n in ("mix_norm_pre", "q_norm", "kv_norm", "b_glu", "b_gate", "mix_norm_post", "ffn_norm_pre", "conv_b",
              "ffn_norm_post"):
        sp[n] = sp[n].reshape(1, -1)
    sp["mix_norm_pre"] = sp["mix_norm_pre"] + full.pop("token")
    pending, flying = [], []
    full["send_grads"] = lambda tag, grads, after=None: _send_grads(tag, grads, after, flying, pending)
    loss, gx, gbig, gsmall = _local_step(x, positions, tgt, full, sp)
    outs = {}

    def adam_big(g_red):
        for name in g_red:
            g2, d, m2, v2 = _adamw(w[name], g_red[name], m[name], v[name], "adamw_" + name)
            outs["grad_" + name], outs["delta_" + name], outs["new_m_" + name], outs["new_v_" + name] = g2, d, m2, v2
        return v2

    def use_sent(g_sent, small_red):
        chip = 2 * lax.axis_index("x") + lax.axis_index("y")
        grads = dict(small_red)
        grads["conv_w"] = lax.dynamic_slice_in_dim(small_red["conv_w"], chip * CONV_W_SHARD[1], CONV_W_SHARD[1], axis=1)
        outs.update({"grad_" + n: grads[n] for n, _ in SMALL_ADAM})
        _, d_sm, m_sm, v_sm = _adamw(_pack_flat(SMALL_ADAM, w), _pack_flat(SMALL_ADAM, grads),
                                     _pack_flat(SMALL_ADAM, m), _pack_flat(SMALL_ADAM, v), "adamw_small")
        for prefix, flat in (("delta_", d_sm), ("new_m_", m_sm), ("new_v_", v_sm)):
            for n, val in _unpack_flat(SMALL_ADAM, flat).items():
                outs[prefix + n] = val
        return adam_big(g_sent)

    g_last, small_red = _reduce_grads(gbig, pending, loss, gsmall, use_sent)
    adam_big(g_last)
    outs = {n: val.reshape(args[n.split("_", 1)[1] if not n.startswith("new_") else n[6:]].shape)
            for n, val in outs.items()}
    res = [small_red["loss"][0], gx[None]]
    for prefix in ("grad_", "delta_", "new_m_", "new_v_"):
        res += [outs[prefix + n] for n in _WEIGHTS]
    return tuple(res)


def kernel(x, positions, mix_norm_pre, w_in, q_norm, w_uq, kv_norm, w_uk, w_uv, ssm_lambda_re, ssm_lambda_im, ssm_log_dt, ssm_b_re, ssm_b_im, ssm_c_re, ssm_c_im, ssm_d, w_glu, b_glu, w_branch_attn, w_branch_ssm, b_gate, w_out, mix_norm_post, ffn_norm_pre, w_up, conv_w, conv_b, w_down, ffn_norm_post, loss_target, m_mix_norm_pre, m_w_in, m_q_norm, m_w_uq, m_kv_norm, m_w_uk, m_w_uv, m_ssm_lambda_re, m_ssm_lambda_im, m_ssm_log_dt, m_ssm_b_re, m_ssm_b_im, m_ssm_c_re, m_ssm_c_im, m_ssm_d, m_w_glu, m_b_glu, m_w_branch_attn, m_w_branch_ssm, m_b_gate, m_w_out, m_mix_norm_post, m_ffn_norm_pre, m_w_up, m_conv_w, m_conv_b, m_w_down, m_ffn_norm_post, v_mix_norm_pre, v_w_in, v_q_norm, v_w_uq, v_kv_norm, v_w_uk, v_w_uv, v_ssm_lambda_re, v_ssm_lambda_im, v_ssm_log_dt, v_ssm_b_re, v_ssm_b_im, v_ssm_c_re, v_ssm_c_im, v_ssm_d, v_w_glu, v_b_glu, v_w_branch_attn, v_w_branch_ssm, v_b_gate, v_w_out, v_mix_norm_post, v_ffn_norm_pre, v_w_up, v_conv_w, v_conv_b, v_w_down, v_ffn_norm_post):
    given = dict(locals())
    return _step(given)
```

```python
import math

import jax
import jax.numpy as jnp
from jax import lax
from jax.experimental import pallas as pl
from jax.experimental.pallas import tpu as pltpu

F32 = jnp.float32
BF16 = jnp.bfloat16
MESH = pl.DeviceIdType.MESH

D_MODEL = 1024
N_HEADS = 8
QK_NOPE = 64
QK_ROPE = 32
QK_HEAD = QK_NOPE + QK_ROPE
V_HEAD = 64
Q_RANK = 384
KV_RANK = 256
ROPE_THETA = 10000.0
SSM_W = 512
SSM_H = 16
SSM_G = 32
SSM_P = 64
SSM_CH = SSM_G * SSM_P
D_FF = 2816
EPS = 1e-6
ADAM_LR = 0.001
ADAM_B1 = 0.9
ADAM_B2 = 0.999
ADAM_EPS = 1e-08
ADAM_WD = 0.01
ADAM_STEP = 10

LANES = 128
SUBLANES = 8
VMEM_LIMIT = 56 * 1024 * 1024

HEAD_SLOT = LANES
HP = N_HEADS * HEAD_SLOT
P_CQ, P_CKV, P_KR, P_U, P_GL, P_END = 0, 384, 640, 768, 1280, 3328
KR_LANE = 64

FLAT_W = 1024
N_CHIPS = 4


def _tile(n, cap):
    if n <= cap:
        return n
    best = None
    for t in range(LANES, cap + 1, LANES):
        if n % t == 0:
            best = t
    assert best is not None, (n, cap)
    return best


def _params(sem):
    return pltpu.CompilerParams(dimension_semantics=sem, vmem_limit_bytes=VMEM_LIMIT)


def _dot(a, b):
    return jnp.dot(a, b, preferred_element_type=F32)


def _dot_nt(a, b):
    return lax.dot_general(a, b, (((1,), (1,)), ((), ())), preferred_element_type=F32)


def _dot_tn(a, b):
    return lax.dot_general(a, b, (((0,), (0,)), ((), ())), preferred_element_type=F32)


def _rms(x, g):
    r = lax.rsqrt(jnp.mean(x * x, axis=-1, keepdims=True) + EPS)
    return x * r * g, r


def _rms_bwd(dy, x, g):
    r = lax.rsqrt(jnp.mean(x * x, axis=-1, keepdims=True) + EPS)
    dyg = dy * g
    dx = r * dyg - x * (r * r * r) * jnp.mean(dyg * x, axis=-1, keepdims=True)
    dg = jnp.sum(dy * x * r, axis=0, keepdims=True)
    return dx, dg


_GELU_K0 = math.sqrt(2.0 / math.pi)
_GELU_K1 = 0.044715


def _gelu(x):
    th = jnp.tanh(_GELU_K0 * (x + _GELU_K1 * x * x * x))
    return 0.5 * x * (1.0 + th)


def _gelu_grad(x):
    th = jnp.tanh(_GELU_K0 * (x + _GELU_K1 * x * x * x))
    return 0.5 * (1.0 + th) + 0.5 * x * (1.0 - th * th) * _GELU_K0 * (1.0 + 3.0 * _GELU_K1 * x * x)


def _sigmoid(x):
    return 1.0 / (1.0 + jnp.exp(-x))


def _rope(q, c, s):
    n = q.shape[1]
    lane = lax.broadcasted_iota(jnp.int32, q.shape, 1) % HEAD_SLOT
    sw = jnp.where(lane < KR_LANE + QK_ROPE // 2, pltpu.roll(q, n - QK_ROPE // 2, 1), pltpu.roll(q, QK_ROPE // 2, 1))
    return q * c + sw * s


def _rope_bwd(dy, c, s):
    n = dy.shape[1]
    t = dy * s
    lane = lax.broadcasted_iota(jnp.int32, dy.shape, 1) % HEAD_SLOT
    sw = jnp.where(lane < KR_LANE + QK_ROPE // 2, pltpu.roll(t, n - QK_ROPE // 2, 1), pltpu.roll(t, QK_ROPE // 2, 1))
    rope_lane = jnp.logical_and(lane >= KR_LANE, lane < KR_LANE + QK_ROPE)
    return dy * c + jnp.where(rope_lane, sw, 0.0)


def _shift_down(x, k, halo):
    xs = pltpu.roll(x, k, 0)
    hs = pltpu.roll(halo, k, 0)
    rows = lax.broadcasted_iota(jnp.int32, halo.shape, 0)
    top = jnp.where(rows < k, hs, xs[0:SUBLANES])
    return jnp.concatenate([top, xs[SUBLANES:]], axis=0)


def _shift_up(x, k, halo):
    t = x.shape[0]
    xs = pltpu.roll(x, t - k, 0)
    hs = pltpu.roll(halo, SUBLANES - k, 0)
    rows = lax.broadcasted_iota(jnp.int32, halo.shape, 0)
    bot = jnp.where(rows >= SUBLANES - k, hs, xs[t - SUBLANES:])
    return jnp.concatenate([xs[:t - SUBLANES], bot], axis=0)


def _mm(a, b, name, out_dtype=F32, bt=False, tm_cap=1024, tn_cap=1408):
    m, k = a.shape[-2:]
    parts = a.shape[0] if a.ndim == 3 else 1
    n = b.shape[0] if bt else b.shape[1]
    tm = min(tm_cap, m)
    tn = _tile(n, tn_cap)

    def body(a_ref, b_ref, o_ref):
        if not bt:
            o_ref[...] = _dot(a_ref[...], b_ref[...]).astype(out_dtype)
        elif parts == 1:
            o_ref[...] = _dot_nt(a_ref[...], b_ref[...]).astype(out_dtype)
        else:
            @pl.when(pl.program_id(2) == 0)
            def _():
                o_ref[...] = jnp.zeros_like(o_ref)

            o_ref[...] += _dot_nt(a_ref[...], b_ref[...])

    if bt:
        b_spec = pl.BlockSpec((tn, k), lambda j, i, s: (j, s))
    else:
        b_spec = pl.BlockSpec((k, tn), lambda j, i, s: (0, j))
    if a.ndim == 3:
        assert bt and out_dtype == F32
        a_spec = pl.BlockSpec((None, tm, k), lambda j, i, s: (s, i, 0))
    else:
        a_spec = pl.BlockSpec((tm, k), lambda j, i, s: (i, 0))
    return pl.pallas_call(
        body, name=name, grid=(n // tn, m // tm, parts),
        in_specs=[a_spec, b_spec],
        out_specs=pl.BlockSpec((tm, tn), lambda j, i, s: (i, j)),
        out_shape=jax.ShapeDtypeStruct((m, n), out_dtype),
        compiler_params=_params(("parallel", "parallel", "arbitrary")),
    )(a, b)


def _mm_tn(a, b, name, tk_cap=1024, tn_cap=1664, tl_cap=2048, chips=False):
    l, k = a.shape
    tk = _tile(k, tk_cap)
    tl = min(tl_cap, l)

    def body(a_ref, b_ref, o_ref):
        @pl.when(pl.program_id(2) == 0)
        def _():
            o_ref[...] = jnp.zeros_like(o_ref)

        o_ref[...] += _dot_tn(a_ref[...], b_ref[...])

    if chips:
        n = b.shape[-1] * (b.shape[0] if b.ndim == 3 else 1)
        tn = n // N_CHIPS
        assert tn % LANES == 0
        if b.ndim == 3:
            per = N_CHIPS // b.shape[0]
            b_spec = pl.BlockSpec((None, tl, tn), lambda i, j, r: (j // per, r, j % per))
        else:
            b_spec = pl.BlockSpec((tl, tn), lambda i, j, r: (r, j))
        out_spec = pl.BlockSpec((None, tk, tn), lambda i, j, r: (j, i, 0))
        out_shape = jax.ShapeDtypeStruct((N_CHIPS, k, tn), F32)
    else:
        n = b.shape[1]
        tn = _tile(n, tn_cap)
        b_spec = pl.BlockSpec((tl, tn), lambda i, j, r: (r, j))
        out_spec = pl.BlockSpec((tk, tn), lambda i, j, r: (i, j))
        out_shape = jax.ShapeDtypeStruct((k, n), F32)
    return pl.pallas_call(
        body, name=name, grid=(k // tk, n // tn, l // tl),
        in_specs=[pl.BlockSpec((tl, tk), lambda i, j, r: (r, i)), b_spec],
        out_specs=out_spec, out_shape=out_shape,
        compiler_params=_params(("parallel", "parallel", "arbitrary")),
    )(a, b)


def _row(tl, n):
    return pl.BlockSpec((tl, n), lambda i: (i, 0))


def _const(shape):
    return pl.BlockSpec(shape, lambda i: tuple(0 for _ in shape))


def _proj_fwd(x, g1, win, gq, wuq, gkv, wukv, rc, rs, bg, tl):
    l = x.shape[0]

    def body(x_ref, g1_ref, win_ref, gq_ref, wuq_ref, gkv_ref, wukv_ref, rc_ref, rs_ref, bg_ref,
             hn_ref, cq_ref, ckv_ref, q_ref, k_ref, v_ref, u_ref, gl_ref):
        hn, _ = _rms(x_ref[...], g1_ref[...])
        hnb = hn.astype(BF16)
        hn_ref[...] = hnb
        proj = _dot(hnb, win_ref[...])
        cq = proj[:, P_CQ:P_CKV]
        ckv = proj[:, P_CKV:P_KR]
        kr = proj[:, P_KR:P_U]
        cq_ref[...] = cq
        ckv_ref[...] = ckv
        u_ref[...] = proj[:, P_U:P_GL]
        gl_ref[...] = proj[:, P_GL:P_END] + bg_ref[...]
        qn, _ = _rms(cq, gq_ref[...])
        q = _dot(qn.astype(BF16), wuq_ref[...])
        c1 = rc_ref[...]
        s1 = rs_ref[...]
        q_ref[...] = (_rope(q, jnp.tile(c1, (1, N_HEADS)), jnp.tile(s1, (1, N_HEADS))) * Q_PRESCALE).astype(BF16)
        ckvn, _ = _rms(ckv, gkv_ref[...])
        kv = _dot(ckvn.astype(BF16), wukv_ref[...])
        krr = _rope(kr, c1, s1)
        k_ref[...] = (kv[:, :HP] + jnp.tile(krr, (1, N_HEADS))).astype(BF16)
        v_ref[...] = kv[:, HP:].astype(BF16)

    outs = [(D_MODEL, BF16), (Q_RANK, F32), (KV_RANK, F32), (HP, BF16), (HP, BF16), (HP, BF16),
            (SSM_W, F32), (2 * D_MODEL, F32)]
    return pl.pallas_call(
        body, name="proj_fwd", grid=(l // tl,),
        in_specs=[_row(tl, D_MODEL), _const((1, D_MODEL)), _const((D_MODEL, P_END)), _const((1, Q_RANK)),
                  _const((Q_RANK, HP)), _const((1, KV_RANK)), _const((KV_RANK, 2 * HP)),
                  _row(tl, HEAD_SLOT), _row(tl, HEAD_SLOT), _const((1, 2 * D_MODEL))],
        out_specs=[_row(tl, n) for n, _ in outs],
        out_shape=[jax.ShapeDtypeStruct((l, n), dt) for n, dt in outs],
        compiler_params=_params(("parallel",)),
    )(x, g1, win, gq, wuq, gkv, wukv, rc, rs, bg)


_NEG = -1e30


LOG2E = 1.0 / math.log(2.0)
LN2 = math.log(2.0)
ATTN_SCALE = 1.0 / math.sqrt(QK_HEAD)
Q_PRESCALE = ATTN_SCALE * LOG2E
FWD_HEADS = 8
BWD_HEADS = 8


def _causal_pairs(nq, by_query):
    if by_query:
        pairs = [(i, j) for i in range(nq) for j in range(i + 1)]
    else:
        pairs = [(i, j) for j in range(nq) for i in range(j, nq)]
    return jnp.array([p[0] for p in pairs], jnp.int32), jnp.array([p[1] for p in pairs], jnp.int32)


def _diag_mask_t(s):
    rows = lax.broadcasted_iota(jnp.int32, s.shape, 0)
    cols = lax.broadcasted_iota(jnp.int32, s.shape, 1)
    return jnp.where(rows <= cols, s, _NEG)


def _attn_fwd(q, k, v, tq, heads):
    l = q.shape[0]
    nq = l // tq
    it, jt = _causal_pairs(nq, True)

    def body(it_ref, jt_ref, q_ref, k_ref, v_ref, o_ref, lse_ref, m_ref, l_ref, acc_ref):
        t = pl.program_id(1)
        i = it_ref[t]
        j = jt_ref[t]

        @pl.when(j == 0)
        def _():
            m_ref[...] = jnp.full_like(m_ref, _NEG)
            l_ref[...] = jnp.zeros_like(l_ref)
            acc_ref[...] = jnp.zeros_like(acc_ref)

        def update(on_diagonal):
            for hh in range(heads):
                sl = slice(hh * HEAD_SLOT, (hh + 1) * HEAD_SLOT)
                s = _dot_nt(k_ref[:, sl], q_ref[:, sl])
                if on_diagonal:
                    s = _diag_mask_t(s)
                m_old = m_ref[hh]
                m_new = jnp.maximum(m_old, jnp.max(s, axis=0, keepdims=True))
                p = jnp.exp2(s - m_new)
                alpha = jnp.exp2(m_old - m_new)
                l_ref[hh] = alpha * l_ref[hh] + jnp.sum(p, axis=0, keepdims=True)
                acc_ref[hh] = alpha * acc_ref[hh] + _dot_tn(v_ref[:, sl], p.astype(BF16))
                m_ref[hh] = m_new

        @pl.when(j < i)
        def _():
            update(False)

        @pl.when(j == i)
        def _():
            update(True)
            for hh in range(heads):
                sl = slice(hh * HEAD_SLOT, (hh + 1) * HEAD_SLOT)
                o_ref[:, sl] = (acc_ref[hh] / l_ref[hh]).T.astype(BF16)
                lse_ref[hh] = m_ref[hh] + jnp.log(l_ref[hh]) * LOG2E

    blk = (tq, (heads * HEAD_SLOT))
    qmap = lambda h, t, it_ref, jt_ref: (it_ref[t], h)
    kmap = lambda h, t, it_ref, jt_ref: (jt_ref[t], h)
    row = pl.BlockSpec((heads, 1, tq), lambda h, t, it_ref, jt_ref: (h, 0, it_ref[t]))
    return pl.pallas_call(
        body, name="attn_fwd",
        grid_spec=pltpu.PrefetchScalarGridSpec(
            num_scalar_prefetch=2, grid=(N_HEADS // heads, it.shape[0]),
            in_specs=[pl.BlockSpec(blk, qmap), pl.BlockSpec(blk, kmap), pl.BlockSpec(blk, kmap)],
            out_specs=[pl.BlockSpec(blk, qmap), row],
            scratch_shapes=[pltpu.VMEM((heads, 1, tq), F32), pltpu.VMEM((heads, 1, tq), F32),
                            pltpu.VMEM((heads, HEAD_SLOT, tq), F32)]),
        out_shape=[jax.ShapeDtypeStruct((l, HP), BF16), jax.ShapeDtypeStruct((N_HEADS, 1, l), F32)],
        compiler_params=_params(("parallel", "arbitrary")),
    )(it, jt, q, k, v)


def _attn_delta(o, do, tq, heads):
    l = o.shape[0]

    def body(o_ref, do_ref, d_ref):
        prod = o_ref[...].astype(F32) * do_ref[...].astype(F32)
        for hh in range(heads):
            d_ref[hh] = jnp.sum(prod[:, hh * HEAD_SLOT:(hh + 1) * HEAD_SLOT].T, axis=0, keepdims=True)

    blk = pl.BlockSpec((tq, (heads * HEAD_SLOT)), lambda h, i: (i, h))
    return pl.pallas_call(
        body, name="attn_delta", grid=(N_HEADS // heads, l // tq), in_specs=[blk, blk],
        out_specs=pl.BlockSpec((heads, 1, tq), lambda h, i: (h, 0, i)),
        out_shape=jax.ShapeDtypeStruct((N_HEADS, 1, l), F32),
        compiler_params=_params(("parallel", "parallel")),
    )(o, do)


def _attn_bwd(q, k, v, do, lse, delta, tq, heads):
    l = q.shape[0]
    nq = l // tq
    it, jt = _causal_pairs(nq, False)

    def body(it_ref, jt_ref, q_ref, k_ref, v_ref, do_ref, lse_ref, dl_ref, dq_ref, dk_ref, dv_ref, dka_ref, dva_ref):
        t = pl.program_id(1)
        i = it_ref[t]
        j = jt_ref[t]

        @pl.when(t == 0)
        def _():
            dq_ref[...] = jnp.zeros_like(dq_ref)

        @pl.when(i == j)
        def _():
            dka_ref[...] = jnp.zeros_like(dka_ref)
            dva_ref[...] = jnp.zeros_like(dva_ref)

        def update(on_diagonal):
            r0 = pl.multiple_of(i * tq, tq)
            for hh in range(heads):
                sl = slice(hh * HEAD_SLOT, (hh + 1) * HEAD_SLOT)
                qb = q_ref[:, sl]
                kb = k_ref[:, sl]
                dob = do_ref[:, sl]
                s = _dot_nt(kb, qb)
                if on_diagonal:
                    s = _diag_mask_t(s)
                p = jnp.exp2(s - lse_ref[hh])
                dva_ref[:, sl] += _dot(p.astype(BF16), dob)
                dp = _dot_nt(v_ref[:, sl], dob)
                ds = (p * (dp - dl_ref[hh])).astype(BF16)
                dka_ref[:, sl] += _dot(ds, qb)
                dq_ref[pl.ds(r0, tq), sl] += ATTN_SCALE * _dot_tn(ds, kb)

        @pl.when(j < i)
        def _():
            update(False)

        @pl.when(j == i)
        def _():
            update(True)

        @pl.when(i == nq - 1)
        def _():
            dk_ref[...] = (dka_ref[...] * LN2).astype(BF16)
            dv_ref[...] = dva_ref[...].astype(BF16)

    blk = (tq, (heads * HEAD_SLOT))
    qmap = lambda h, t, it_ref, jt_ref: (it_ref[t], h)
    kmap = lambda h, t, it_ref, jt_ref: (jt_ref[t], h)
    row = pl.BlockSpec((heads, 1, tq), lambda h, t, it_ref, jt_ref: (h, 0, it_ref[t]))
    return pl.pallas_call(
        body, name="attn_bwd",
        grid_spec=pltpu.PrefetchScalarGridSpec(
            num_scalar_prefetch=2, grid=(N_HEADS // heads, it.shape[0]),
            in_specs=[pl.BlockSpec(blk, qmap), pl.BlockSpec(blk, kmap), pl.BlockSpec(blk, kmap),
                      pl.BlockSpec(blk, qmap), row, row],
            out_specs=[pl.BlockSpec((l, (heads * HEAD_SLOT)), lambda h, t, it_ref, jt_ref: (0, h)), pl.BlockSpec(blk, kmap),
                       pl.BlockSpec(blk, kmap)],
            scratch_shapes=[pltpu.VMEM(blk, F32), pltpu.VMEM(blk, F32)]),
        out_shape=[jax.ShapeDtypeStruct((l, HP), F32), jax.ShapeDtypeStruct((l, HP), BF16),
                   jax.ShapeDtypeStruct((l, HP), BF16)],
        compiler_params=_params(("parallel", "arbitrary")),
    )(it, jt, q, k, v, do, lse, delta)


SSM_CB = 512
SSM_UB = 128
SSM_NB = SSM_CH // SSM_CB


def _scan_tiles(re_ref, im_ref, tab, carry, n_tiles, reverse):
    group = 2
    assert n_tiles % group == 0
    pr, pi = tab[6], tab[7]

    def inside(sr, si):
        for step, k in enumerate((1, 2, 4)):
            mr, mi = tab[2 * step], tab[2 * step + 1]
            sh = (SUBLANES - k) if reverse else k
            rr = pltpu.roll(sr, sh, 0)
            ri = pltpu.roll(si, sh, 0)
            sr, si = sr + mr * rr - mi * ri, si + mr * ri + mi * rr
        return sr, si

    def body(n, c):
        cr, ci = c
        first = (n_tiles - group * (n + 1)) if reverse else group * n
        r0 = pl.multiple_of(first * SUBLANES, group * SUBLANES)
        rows = [pl.ds(r0 + g * SUBLANES, SUBLANES) for g in range(group)]
        tiles = [inside(re_ref[r, :], im_ref[r, :]) for r in rows]
        for g in (range(group - 1, -1, -1) if reverse else range(group)):
            sr, si = tiles[g]
            sr, si = sr + pr * cr - pi * ci, si + pr * ci + pi * cr
            re_ref[rows[g], :] = sr
            im_ref[rows[g], :] = si
            edge = slice(0, 1) if reverse else slice(SUBLANES - 1, SUBLANES)
            cr, ci = sr[edge, :], si[edge, :]
        return cr, ci

    return lax.fori_loop(0, n_tiles // group, body, carry)


def _ssm_fwd(u, bre, bim, cre, cim, dvec, tab, tt):
    l = u.shape[0]
    nt = l // tt

    def body(u_ref, bre_ref, bim_ref, cre_ref, cim_ref, d_ref, tab_ref, y_ref, sre_ref, sim_ref, car_ref):
        @pl.when(pl.program_id(1) == 0)
        def _():
            car_ref[...] = jnp.zeros_like(car_ref)

        uf = u_ref[...]
        ub = uf.astype(BF16)
        sre_ref[...] = _dot(ub, bre_ref[0])
        sim_ref[...] = _dot(ub, bim_ref[0])
        tab_v = [tab_ref[n] for n in range(8)]
        cr, ci = _scan_tiles(sre_ref, sim_ref, tab_v, (car_ref[0:1, :], car_ref[8:9, :]), tt // SUBLANES, False)
        car_ref[0:1, :] = cr
        car_ref[8:9, :] = ci
        y_ref[...] = (_dot(sre_ref[...].astype(BF16), cre_ref[0]) - _dot(sim_ref[...].astype(BF16), cim_ref[0])
                      + d_ref[...] * uf)

    return pl.pallas_call(
        body, name="ssm_fwd", grid=(SSM_NB, nt),
        in_specs=[pl.BlockSpec((tt, SSM_UB), lambda m, t: (t, m)),
                  pl.BlockSpec((1, SSM_UB, SSM_CB), lambda m, t: (m, 0, 0)),
                  pl.BlockSpec((1, SSM_UB, SSM_CB), lambda m, t: (m, 0, 0)),
                  pl.BlockSpec((1, SSM_CB, SSM_UB), lambda m, t: (m, 0, 0)),
                  pl.BlockSpec((1, SSM_CB, SSM_UB), lambda m, t: (m, 0, 0)),
                  pl.BlockSpec((1, SSM_UB), lambda m, t: (0, m)),
                  pl.BlockSpec((8, SUBLANES, SSM_CB), lambda m, t: (0, 0, m))],
        out_specs=[pl.BlockSpec((tt, SSM_UB), lambda m, t: (t, m)),
                   pl.BlockSpec((tt, SSM_CB), lambda m, t: (t, m)),
                   pl.BlockSpec((tt, SSM_CB), lambda m, t: (t, m))],
        out_shape=[jax.ShapeDtypeStruct((l, SSM_W), F32), jax.ShapeDtypeStruct((l, SSM_CH), F32),
                   jax.ShapeDtypeStruct((l, SSM_CH), F32)],
        scratch_shapes=[pltpu.VMEM((2 * SUBLANES, SSM_CB), F32)],
        compiler_params=_params(("parallel", "arbitrary")),
    )(u, bre, bim, cre, cim, dvec, tab)


def _ssm_bwd(dy, u, sre, sim, bre, bim, cre, cim, dvec, tab, tt):
    l = u.shape[0]
    nt = l // tt
    tpb = tt // SUBLANES

    def body(dy_ref, u_ref, sre_ref, sim_ref, hre_ref, him_ref, bre_ref, bim_ref, cre_ref, cim_ref, d_ref, tab_ref,
             du_ref, dbre_ref, dbim_ref, dcre_ref, dcim_ref, dare_ref, daim_ref, dd_ref, lr_ref, li_ref, car_ref):
        t = pl.program_id(1)

        @pl.when(t == 0)
        def _():
            car_ref[...] = jnp.zeros_like(car_ref)
            for ref in (dbre_ref, dbim_ref, dcre_ref, dcim_ref, dare_ref, daim_ref, dd_ref):
                ref[...] = jnp.zeros_like(ref)

        dyf = dy_ref[...]
        dyb = dyf.astype(BF16)
        uf = u_ref[...]
        s_re = sre_ref[...]
        s_im = sim_ref[...]
        lr_ref[...] = _dot_nt(dyb, cre_ref[0])
        li_ref[...] = -_dot_nt(dyb, cim_ref[0])
        dcre_ref[0] += _dot_tn(s_re.astype(BF16), dyb)
        dcim_ref[0] -= _dot_tn(s_im.astype(BF16), dyb)
        tab_v = [tab_ref[n] for n in range(8)]
        cr, ci = _scan_tiles(lr_ref, li_ref, tab_v, (car_ref[0:1, :], car_ref[8:9, :]), tpb, True)
        car_ref[0:1, :] = cr
        car_ref[8:9, :] = ci
        lam_r = lr_ref[...]
        lam_i = li_ref[...]
        keep = jnp.where(t == nt - 1, 0.0, 1.0)
        sp_r = _shift_down(s_re, 1, hre_ref[...] * keep)
        sp_i = _shift_down(s_im, 1, him_ref[...] * keep)
        dare_ref[...] += jnp.sum(lam_r * sp_r + lam_i * sp_i, axis=0, keepdims=True)
        daim_ref[...] += jnp.sum(lam_i * sp_r - lam_r * sp_i, axis=0, keepdims=True)
        lrb = lam_r.astype(BF16)
        lib = lam_i.astype(BF16)
        du_ref[...] = _dot_nt(lrb, bre_ref[0]) + _dot_nt(lib, bim_ref[0]) + dyf * d_ref[...]
        ub = uf.astype(BF16)
        dbre_ref[0] += _dot_tn(ub, lrb)
        dbim_ref[0] += _dot_tn(ub, lib)
        dd_ref[...] += jnp.sum(dyf * uf, axis=0, keepdims=True)

    rev = lambda m, t: (nt - 1 - t, m)
    halo = lambda m, t: (jnp.maximum((nt - 1 - t) * tpb - 1, 0), m)
    wb = pl.BlockSpec((1, SSM_UB, SSM_CB), lambda m, t: (m, 0, 0))
    wc = pl.BlockSpec((1, SSM_CB, SSM_UB), lambda m, t: (m, 0, 0))
    vec_c = pl.BlockSpec((1, SSM_CB), lambda m, t: (0, m))
    vec_u = pl.BlockSpec((1, SSM_UB), lambda m, t: (0, m))
    return pl.pallas_call(
        body, name="ssm_bwd", grid=(SSM_NB, nt),
        in_specs=[pl.BlockSpec((tt, SSM_UB), rev), pl.BlockSpec((tt, SSM_UB), rev),
                  pl.BlockSpec((tt, SSM_CB), rev), pl.BlockSpec((tt, SSM_CB), rev),
                  pl.BlockSpec((SUBLANES, SSM_CB), halo), pl.BlockSpec((SUBLANES, SSM_CB), halo),
                  wb, wb, wc, wc, vec_u,
                  pl.BlockSpec((8, SUBLANES, SSM_CB), lambda m, t: (0, 0, m))],
        out_specs=[pl.BlockSpec((tt, SSM_UB), rev), wb, wb, wc, wc, vec_c, vec_c, vec_u],
        out_shape=[jax.ShapeDtypeStruct((l, SSM_W), F32),
                   jax.ShapeDtypeStruct((SSM_NB, SSM_UB, SSM_CB), F32), jax.ShapeDtypeStruct((SSM_NB, SSM_UB, SSM_CB), F32),
                   jax.ShapeDtypeStruct((SSM_NB, SSM_CB, SSM_UB), F32), jax.ShapeDtypeStruct((SSM_NB, SSM_CB, SSM_UB), F32),
                   jax.ShapeDtypeStruct((1, SSM_CH), F32), jax.ShapeDtypeStruct((1, SSM_CH), F32),
                   jax.ShapeDtypeStruct((1, SSM_W), F32)],
        scratch_shapes=[pltpu.VMEM((tt, SSM_CB), F32), pltpu.VMEM((tt, SSM_CB), F32),
                        pltpu.VMEM((2 * SUBLANES, SSM_CB), F32)],
        compiler_params=_params(("parallel", "arbitrary")),
    )(dy, u, sre, sim, sre, sim, bre, bim, cre, cim, dvec, tab)


def _merge_fwd(x, gl, attn, y1, wba, wbs, wglu, bglu, wout, gpost, gpre, tl):
    l = x.shape[0]

    def body(x_ref, gl_ref, at_ref, y1_ref, wba_ref, wbs_ref, wglu_ref, bglu_ref, wout_ref, gpost_ref, gpre_ref,
             a_ref, sm_ref, mg_ref, z_ref, x1_ref, hn2_ref, y3_ref):
        y2 = _gelu(y1_ref[...])
        sg = _sigmoid(_dot(y2.astype(BF16), wglu_ref[...]) + bglu_ref[...])
        y3 = (y2 * sg).astype(BF16)
        y3_ref[...] = y3
        a = _dot(at_ref[...], wba_ref[...])
        sm = _dot(y3, wbs_ref[...])
        a_ref[...] = a.astype(BF16)
        sm_ref[...] = sm.astype(BF16)
        g = _sigmoid(gl_ref[...])
        merged = (g[:, :D_MODEL] * a + g[:, D_MODEL:] * sm).astype(BF16)
        mg_ref[...] = merged
        z = _dot(merged, wout_ref[...])
        z_ref[...] = z
        n, _ = _rms(z, gpost_ref[...])
        x1 = x_ref[...] + n
        x1_ref[...] = x1
        hn2, _ = _rms(x1, gpre_ref[...])
        hn2_ref[...] = hn2.astype(BF16)

    outs = [(D_MODEL, BF16), (D_MODEL, BF16), (D_MODEL, BF16), (D_MODEL, F32), (D_MODEL, F32), (D_MODEL, BF16),
            (SSM_W, BF16)]
    return pl.pallas_call(
        body, name="merge_fwd", grid=(l // tl,),
        in_specs=[_row(tl, D_MODEL), _row(tl, 2 * D_MODEL), _row(tl, HP), _row(tl, SSM_W),
                  _const((HP, D_MODEL)), _const((SSM_W, D_MODEL)), _const((SSM_W, SSM_W)), _const((1, SSM_W)),
                  _const((D_MODEL, D_MODEL)), _const((1, D_MODEL)), _const((1, D_MODEL))],
        out_specs=[_row(tl, n) for n, _ in outs],
        out_shape=[jax.ShapeDtypeStruct((l, n), dt) for n, dt in outs],
        compiler_params=_params(("parallel",)),
    )(x, gl, attn, y1, wba, wbs, wglu, bglu, wout, gpost, gpre)


def _merge_bwd(dhn2, x1, dx2, z, gl, a, sm, y1, wba, wbs, wglu, bglu, wout, gpost, gpre, tl):
    l = x1.shape[0]

    def body(dhn2_ref, x1_ref, dx2_ref, z_ref, gl_ref, a_ref, sm_ref, y1_ref,
             wba_ref, wbs_ref, wglu_ref, bglu_ref, wout_ref, gpost_ref, gpre_ref,
             dx1_ref, dz_ref, dbra_ref, dbrs_ref, dgl_ref, dat_ref, dy1_ref, dt_ref, y2_ref,
             dgpre_ref, dgpost_ref, dbg_ref, dbglu_ref):
        @pl.when(pl.program_id(0) == 0)
        def _():
            for ref in (dgpre_ref, dgpost_ref, dbg_ref, dbglu_ref):
                ref[...] = jnp.zeros_like(ref)

        dx1a, dgpre = _rms_bwd(dhn2_ref[...], x1_ref[...], gpre_ref[...])
        dgpre_ref[...] += dgpre
        dx1 = dx2_ref[...] + dx1a
        dx1_ref[...] = dx1
        dz, dgpost = _rms_bwd(dx1, z_ref[...], gpost_ref[...])
        dgpost_ref[...] += dgpost
        dzb = dz.astype(BF16)
        dz_ref[...] = dzb
        dm = _dot_nt(dzb, wout_ref[...])
        g = _sigmoid(gl_ref[...])
        g0 = g[:, :D_MODEL]
        g1 = g[:, D_MODEL:]
        dbra = (dm * g0).astype(BF16)
        dbrs = (dm * g1).astype(BF16)
        dbra_ref[...] = dbra
        dbrs_ref[...] = dbrs
        dgl0 = dm * a_ref[...].astype(F32) * g0 * (1.0 - g0)
        dgl1 = dm * sm_ref[...].astype(F32) * g1 * (1.0 - g1)
        dgl_ref[:, :D_MODEL] = dgl0.astype(BF16)
        dgl_ref[:, D_MODEL:] = dgl1.astype(BF16)
        dbg_ref[:, :D_MODEL] += jnp.sum(dgl0, axis=0, keepdims=True)
        dbg_ref[:, D_MODEL:] += jnp.sum(dgl1, axis=0, keepdims=True)
        dat_ref[...] = _dot_nt(dbra, wba_ref[...]).astype(BF16)
        dy3 = _dot_nt(dbrs, wbs_ref[...])
        y1v = y1_ref[...]
        y2 = _gelu(y1v)
        y2b = y2.astype(BF16)
        y2_ref[...] = y2b
        sg = _sigmoid(_dot(y2b, wglu_ref[...]) + bglu_ref[...])
        dt = dy3 * y2 * sg * (1.0 - sg)
        dtb = dt.astype(BF16)
        dt_ref[...] = dtb
        dbglu_ref[...] += jnp.sum(dt, axis=0, keepdims=True)
        dy2 = dy3 * sg + _dot_nt(dtb, wglu_ref[...])
        dy1_ref[...] = dy2 * _gelu_grad(y1v)

    outs = [(D_MODEL, F32), (D_MODEL, BF16), (D_MODEL, BF16), (D_MODEL, BF16), (2 * D_MODEL, BF16), (HP, BF16),
            (SSM_W, F32), (SSM_W, BF16), (SSM_W, BF16)]
    accs = [D_MODEL, D_MODEL, 2 * D_MODEL, SSM_W]
    return pl.pallas_call(
        body, name="merge_bwd", grid=(l // tl,),
        in_specs=[_row(tl, D_MODEL), _row(tl, D_MODEL), _row(tl, D_MODEL), _row(tl, D_MODEL),
                  _row(tl, 2 * D_MODEL), _row(tl, D_MODEL), _row(tl, D_MODEL), _row(tl, SSM_W),
                  _const((HP, D_MODEL)), _const((SSM_W, D_MODEL)), _const((SSM_W, SSM_W)), _const((1, SSM_W)),
                  _const((D_MODEL, D_MODEL)), _const((1, D_MODEL)), _const((1, D_MODEL))],
        out_specs=[_row(tl, n) for n, _ in outs] + [_const((1, n)) for n in accs],
        out_shape=[jax.ShapeDtypeStruct((l, n), dt) for n, dt in outs]
        + [jax.ShapeDtypeStruct((1, n), F32) for n in accs],
        compiler_params=_params(("arbitrary",)),
    )(dhn2, x1, dx2, z, gl, a, sm, y1, wba, wbs, wglu, bglu, wout, gpost, gpre)


def _proj_bwd(x, dx1, cq, ckv, dq, dk, dv, du, dgl, g1, win, gq, wuq, gkv, wukv, rc, rs, tl):
    l = x.shape[0]

    def body(x_ref, dx1_ref, cq_ref, ckv_ref, dq_ref, dk_ref, dv_ref, du_ref, dgl_ref,
             g1_ref, win_ref, gq_ref, wuq_ref, gkv_ref, wukv_ref, rc_ref, rs_ref,
             gx_ref, dql_ref, qn_ref, ckvn_ref, dproj_ref, dg1_ref, dgq_ref, dgkv_ref):
        @pl.when(pl.program_id(0) == 0)
        def _():
            for ref in (dg1_ref, dgq_ref, dgkv_ref):
                ref[...] = jnp.zeros_like(ref)

        c1 = rc_ref[...]
        s1 = rs_ref[...]
        dql = _rope_bwd(dq_ref[...], jnp.tile(c1, (1, N_HEADS)), jnp.tile(s1, (1, N_HEADS))).astype(BF16)
        dql_ref[...] = dql
        dqn = _dot_nt(dql, wuq_ref[...])
        cq = cq_ref[...]
        qn, _ = _rms(cq, gq_ref[...])
        qn_ref[...] = qn.astype(BF16)
        dcq, dgq = _rms_bwd(dqn, cq, gq_ref[...])
        dgq_ref[...] += dgq
        dkb = dk_ref[...]
        dvb = dv_ref[...]
        dkf = dkb.astype(F32)
        dkr = dkf[:, 0:HEAD_SLOT]
        for h in range(1, N_HEADS):
            dkr = dkr + dkf[:, h * HEAD_SLOT:(h + 1) * HEAD_SLOT]
        dkr = _rope_bwd(dkr, c1, s1)
        dckvn = _dot_nt(dkb, wukv_ref[:, :HP]) + _dot_nt(dvb, wukv_ref[:, HP:])
        ckv = ckv_ref[...]
        ckvn, _ = _rms(ckv, gkv_ref[...])
        ckvn_ref[...] = ckvn.astype(BF16)
        dckv, dgkv = _rms_bwd(dckvn, ckv, gkv_ref[...])
        dgkv_ref[...] += dgkv
        dproj_ref[:, P_CQ:P_CKV] = dcq.astype(BF16)
        dproj_ref[:, P_CKV:P_KR] = dckv.astype(BF16)
        dproj_ref[:, P_KR:P_U] = dkr.astype(BF16)
        dproj_ref[:, P_U:P_GL] = du_ref[...].astype(BF16)
        dproj_ref[:, P_GL:P_END] = dgl_ref[...]
        dhn = _dot_nt(dproj_ref[...], win_ref[...])
        dxa, dg1 = _rms_bwd(dhn, x_ref[...], g1_ref[...])
        dg1_ref[...] += dg1
        gx_ref[...] = dx1_ref[...] + dxa

    outs = [(D_MODEL, F32), (HP, BF16), (Q_RANK, BF16), (KV_RANK, BF16), (P_END, BF16)]
    accs = [D_MODEL, Q_RANK, KV_RANK]
    return pl.pallas_call(
        body, name="proj_bwd", grid=(l // tl,),
        in_specs=[_row(tl, D_MODEL), _row(tl, D_MODEL), _row(tl, Q_RANK), _row(tl, KV_RANK), _row(tl, HP),
                  _row(tl, HP), _row(tl, HP), _row(tl, SSM_W), _row(tl, 2 * D_MODEL),
                  _const((1, D_MODEL)), _const((D_MODEL, P_END)), _const((1, Q_RANK)), _const((Q_RANK, HP)),
                  _const((1, KV_RANK)), _const((KV_RANK, 2 * HP)), _row(tl, HEAD_SLOT), _row(tl, HEAD_SLOT)],
        out_specs=[_row(tl, n) for n, _ in outs] + [_const((1, n)) for n in accs],
        out_shape=[jax.ShapeDtypeStruct((l, n), dt) for n, dt in outs]
        + [jax.ShapeDtypeStruct((1, n), F32) for n in accs],
        compiler_params=_params(("arbitrary",)),
    )(x, dx1, cq, ckv, dq, dk, dv, du, dgl, g1, win, gq, wuq, gkv, wukv, rc, rs)


CONV_CB = 256
CONV_NB = D_FF // CONV_CB
CONV_ROWS = 16


def _conv3(h, halo, w, b):
    return b + w[0:1, :] * _shift_down(h, 2, halo) + w[1:2, :] * _shift_down(h, 1, halo) + w[2:3, :] * h


def _conv_fwd(h, cw, cb, tl):
    l = h.shape[0]

    def body(hg_ref, hv_ref, wg_ref, wv_ref, bg_ref, bv_ref, act_ref, halo_ref):
        @pl.when(pl.program_id(1) == 0)
        def _():
            halo_ref[...] = jnp.zeros_like(halo_ref)

        hg = hg_ref[...]
        hv = hv_ref[...]
        cg = _conv3(hg, halo_ref[0:SUBLANES, :], wg_ref[...], bg_ref[...])
        cv = _conv3(hv, halo_ref[SUBLANES:, :], wv_ref[...], bv_ref[...])
        act_ref[...] = (_gelu(cg) * cv).astype(BF16)
        halo_ref[0:SUBLANES, :] = hg[tl - SUBLANES:, :]
        halo_ref[SUBLANES:, :] = hv[tl - SUBLANES:, :]

    gmap = lambda c, r: (r, c)
    vmap = lambda c, r: (r, CONV_NB + c)
    return pl.pallas_call(
        body, name="conv_fwd", grid=(CONV_NB, l // tl),
        in_specs=[pl.BlockSpec((tl, CONV_CB), gmap), pl.BlockSpec((tl, CONV_CB), vmap),
                  pl.BlockSpec((3, CONV_CB), lambda c, r: (0, c)), pl.BlockSpec((3, CONV_CB), lambda c, r: (0, CONV_NB + c)),
                  pl.BlockSpec((1, CONV_CB), lambda c, r: (0, c)), pl.BlockSpec((1, CONV_CB), lambda c, r: (0, CONV_NB + c))],
        out_specs=pl.BlockSpec((tl, CONV_CB), gmap),
        out_shape=jax.ShapeDtypeStruct((l, D_FF), BF16),
        scratch_shapes=[pltpu.VMEM((2 * SUBLANES, CONV_CB), F32)],
        compiler_params=_params(("parallel", "arbitrary")),
    )(h, h, cw, cw, cb, cb)


def _conv_bwd(h, dact, cw, cb, tl):
    l = h.shape[0]
    nr = l // tl
    tpb = tl // SUBLANES

    def body(hg_ref, hv_ref, hgh_ref, hvh_ref, da_ref, wg_ref, wv_ref, bg_ref, bv_ref,
             dh_ref, dwg_ref, dwv_ref, dbg_ref, dbv_ref, car_ref):
        r = pl.program_id(1)

        @pl.when(r == 0)
        def _():
            for ref in (car_ref, dwg_ref, dwv_ref, dbg_ref, dbv_ref):
                ref[...] = jnp.zeros_like(ref)

        keep = jnp.where(r == nr - 1, 0.0, 1.0)
        wg, wv, bg, bv = wg_ref[...], wv_ref[...], bg_ref[...], bv_ref[...]
        nch = tl // CONV_ROWS

        def fold(x):
            s = x[0:SUBLANES, :]
            for k in range(1, CONV_ROWS // SUBLANES):
                s = s + x[k * SUBLANES:(k + 1) * SUBLANES, :]
            return s

        def chunk(n, carry):
            ncg, ncv, acc = carry
            idx = nch - 1 - n
            r0 = pl.multiple_of(idx * CONV_ROWS, CONV_ROWS)
            rows = pl.ds(r0, CONV_ROWS)
            before = pl.ds(pl.multiple_of(jnp.maximum(r0 - SUBLANES, 0), SUBLANES), SUBLANES)
            in_tile = idx > 0
            da = da_ref[rows, :].astype(F32)

            def half(h_ref, halo_ref, w, b):
                hh = h_ref[rows, :]
                prev = jnp.where(in_tile, h_ref[before, :], halo_ref[...] * keep)
                h1 = _shift_down(hh, 1, prev)
                h2 = _shift_down(hh, 2, prev)
                return hh, h1, h2, b + w[0:1, :] * h2 + w[1:2, :] * h1 + w[2:3, :] * hh

            hg, hg1, hg2, cg = half(hg_ref, hgh_ref, wg, bg)
            hv, hv1, hv2, cv = half(hv_ref, hvh_ref, wv, bv)
            dcg = da * cv * _gelu_grad(cg)
            dcv = da * _gelu(cg)

            def back(dc, hh, h1, h2, w, nxt, part):
                dh = w[2:3, :] * dc + w[1:2, :] * _shift_up(dc, 1, nxt) + w[0:1, :] * _shift_up(dc, 2, nxt)
                dh_ref[part, rows, :] = dh.astype(BF16)
                return [fold(dc * h2), fold(dc * h1), fold(dc * hh), fold(dc)]

            sums = back(dcg, hg, hg1, hg2, wg, ncg, 0) + back(dcv, hv, hv1, hv2, wv, ncv, 1)
            return dcg[0:SUBLANES, :], dcv[0:SUBLANES, :], [a + s for a, s in zip(acc, sums)]

        zero = jnp.zeros((SUBLANES, CONV_CB), F32)
        ncg, ncv, acc = lax.fori_loop(0, nch, chunk, (car_ref[0:SUBLANES, :], car_ref[SUBLANES:, :], [zero] * 8))
        car_ref[0:SUBLANES, :] = ncg
        car_ref[SUBLANES:, :] = ncv
        for half_acc, dw_ref, db_ref in ((acc[0:4], dwg_ref, dbg_ref), (acc[4:8], dwv_ref, dbv_ref)):
            for k in range(3):
                dw_ref[k:k + 1, :] += jnp.sum(half_acc[k], axis=0, keepdims=True)
            db_ref[...] += jnp.sum(half_acc[3], axis=0, keepdims=True)

    grev = lambda c, r: (nr - 1 - r, c)
    vrev = lambda c, r: (nr - 1 - r, CONV_NB + c)
    ghalo = lambda c, r: (jnp.maximum((nr - 1 - r) * tpb - 1, 0), c)
    vhalo = lambda c, r: (jnp.maximum((nr - 1 - r) * tpb - 1, 0), CONV_NB + c)
    colg = lambda c, r: (0, c)
    colv = lambda c, r: (0, CONV_NB + c)
    return pl.pallas_call(
        body, name="conv_bwd", grid=(CONV_NB, nr),
        in_specs=[pl.BlockSpec((tl, CONV_CB), grev), pl.BlockSpec((tl, CONV_CB), vrev),
                  pl.BlockSpec((SUBLANES, CONV_CB), ghalo), pl.BlockSpec((SUBLANES, CONV_CB), vhalo),
                  pl.BlockSpec((tl, CONV_CB), grev),
                  pl.BlockSpec((3, CONV_CB), colg), pl.BlockSpec((3, CONV_CB), colv),
                  pl.BlockSpec((1, CONV_CB), colg), pl.BlockSpec((1, CONV_CB), colv)],
        out_specs=[pl.BlockSpec((2, tl, CONV_CB), lambda c, r: (0, nr - 1 - r, c)),
                   pl.BlockSpec((3, CONV_CB), colg), pl.BlockSpec((3, CONV_CB), colg),
                   pl.BlockSpec((1, CONV_CB), colg), pl.BlockSpec((1, CONV_CB), colg)],
        out_shape=[jax.ShapeDtypeStruct((2, l, D_FF), BF16),
                   jax.ShapeDtypeStruct((3, D_FF), F32), jax.ShapeDtypeStruct((3, D_FF), F32),
                   jax.ShapeDtypeStruct((1, D_FF), F32), jax.ShapeDtypeStruct((1, D_FF), F32)],
        scratch_shapes=[pltpu.VMEM((2 * SUBLANES, CONV_CB), F32)],
        compiler_params=_params(("parallel", "arbitrary")),
    )(h, h, h, h, dact, cw, cw, cb, cb)


def _loss_head(ff, x1, tgt, g, tl):
    l = ff.shape[0]

    def body(ff_ref, x1_ref, tg_ref, g_ref, loss_ref, dx2_ref, dff_ref, dg_ref):
        @pl.when(pl.program_id(0) == 0)
        def _():
            loss_ref[...] = jnp.zeros_like(loss_ref)
            dg_ref[...] = jnp.zeros_like(dg_ref)

        f = ff_ref[...]
        gv = g_ref[...]
        n, _ = _rms(f, gv)
        e = x1_ref[...] + n - tg_ref[...]
        loss_ref[...] += 0.5 * jnp.sum(jnp.mean(e * e, axis=-1, keepdims=True), axis=0, keepdims=True)
        dx2 = e * (1.0 / D_MODEL)
        dx2_ref[...] = dx2
        dff, dg = _rms_bwd(dx2, f, gv)
        dff_ref[...] = dff.astype(BF16)
        dg_ref[...] += dg

    return pl.pallas_call(
        body, name="loss_head", grid=(l // tl,),
        in_specs=[_row(tl, D_MODEL), _row(tl, D_MODEL), _row(tl, D_MODEL), _const((1, D_MODEL))],
        out_specs=[_const((1, LANES)), _row(tl, D_MODEL), _row(tl, D_MODEL), _const((1, D_MODEL))],
        out_shape=[jax.ShapeDtypeStruct((1, LANES), F32), jax.ShapeDtypeStruct((l, D_MODEL), F32),
                   jax.ShapeDtypeStruct((l, D_MODEL), BF16), jax.ShapeDtypeStruct((1, D_MODEL), F32)],
        compiler_params=_params(("arbitrary",)),
    )(ff, x1, tgt, g)


def _ssm_disc(lam_re, lam_im, log_dt, b_re, b_im):
    dt = jnp.exp(log_dt)[:, None]
    mag = jnp.exp(lam_re * dt)
    ang = lam_im * dt
    a_re, a_im = mag * jnp.cos(ang), mag * jnp.sin(ang)
    den = lam_re * lam_re + lam_im * lam_im
    n_re, n_im = a_re - 1.0, a_im
    z_re = (n_re * lam_re + n_im * lam_im) / den
    z_im = (n_im * lam_re - n_re * lam_im) / den
    bb_re = z_re[..., None] * b_re - z_im[..., None] * b_im
    bb_im = z_re[..., None] * b_im + z_im[..., None] * b_re
    return a_re, a_im, bb_re, bb_im


_GPB = SSM_CB // SSM_P


def _embed_b(bb):
    t = bb.transpose(0, 2, 1).reshape(SSM_NB, _GPB, SSM_H, SSM_P)
    return jnp.einsum('mjhp,jk->mjhkp', t, jnp.eye(_GPB, dtype=bb.dtype)).reshape(SSM_NB, SSM_UB, SSM_CB)


def _extract_b(d):
    t = d.reshape(SSM_NB, _GPB, SSM_H, _GPB, SSM_P)
    t = jnp.einsum('mjhkp,jk->mjhp', t, jnp.eye(_GPB, dtype=d.dtype))
    return t.reshape(SSM_G, SSM_H, SSM_P).transpose(0, 2, 1)


def _embed_c(c):
    t = c.transpose(0, 2, 1).reshape(SSM_NB, _GPB, SSM_P, SSM_H)
    return jnp.einsum('mjph,jk->mjpkh', t, jnp.eye(_GPB, dtype=c.dtype)).reshape(SSM_NB, SSM_CB, SSM_UB)


def _extract_c(d):
    t = d.reshape(SSM_NB, _GPB, SSM_P, _GPB, SSM_H)
    t = jnp.einsum('mjpkh,jk->mjph', t, jnp.eye(_GPB, dtype=d.dtype))
    return t.reshape(SSM_G, SSM_P, SSM_H).transpose(0, 2, 1)


def _scan_tables(a_re, a_im, reverse):
    ar = a_re.reshape(1, SSM_CH)
    ai = (-a_im if reverse else a_im).reshape(1, SSM_CH)
    pr, pi = [ar], [ai]
    for _ in range(SUBLANES - 1):
        pr, pi = pr + [pr[-1] * ar - pi[-1] * ai], pi + [pr[-1] * ai + pi[-1] * ar]
    rows = jnp.arange(SUBLANES)[:, None]
    out = []
    for k in (1, 2, 4):
        valid = (rows + k <= SUBLANES - 1) if reverse else (rows >= k)
        out += [jnp.where(valid, pr[k - 1], 0.0), jnp.where(valid, pi[k - 1], 0.0)]
    order = list(range(SUBLANES - 1, -1, -1)) if reverse else list(range(SUBLANES))
    out += [jnp.concatenate([pr[n] for n in order], axis=0), jnp.concatenate([pi[n] for n in order], axis=0)]
    return jnp.stack(out).astype(F32)


def _pad_heads(w, d):
    lead = w.shape[:-1]
    w = w.reshape(lead + (N_HEADS, d))
    w = jnp.pad(w, [(0, 0)] * len(lead) + [(0, 0), (0, HEAD_SLOT - d)])
    return w.reshape(lead + (HP,))


def _unpad_heads(w, d):
    lead = w.shape[:-1]
    return w.reshape(lead + (N_HEADS, HEAD_SLOT))[..., :d].reshape(lead + (N_HEADS * d,))


def _chip_major(w, axis):
    k, n = w.shape
    if axis == 0:
        return w.reshape(N_CHIPS, k // N_CHIPS, n)
    return w.reshape(k, N_CHIPS, n // N_CHIPS).transpose(1, 0, 2)


def _from_chip_major(w, axis):
    if axis == 0:
        return w.reshape(-1, w.shape[2])
    return w.transpose(1, 0, 2).reshape(w.shape[1], -1)


def _pad_w_in(w):
    z = lambda n: jnp.zeros((w.shape[0], n), w.dtype)
    return jnp.concatenate([w[:, :640], z(KR_LANE), w[:, 640:672], z(HEAD_SLOT - KR_LANE - QK_ROPE), w[:, 672:]], axis=1)


def _unpad_w_in(w):
    return jnp.concatenate([w[:, :640], w[:, P_KR + KR_LANE:P_KR + KR_LANE + QK_ROPE], w[:, P_U:]], axis=1)


def _local_step(x, positions, tgt, wts, sp):
    l = x.shape[0]
    tl = min(512, l)
    tb = min(256, l)
    ta = min(512, l)
    ts = min(2048, l)

    inv_freq = ROPE_THETA ** (-jnp.arange(0, QK_ROPE, 2, dtype=F32) / QK_ROPE)
    ang = positions.astype(F32)[:, None] * inv_freq
    cos, sin = jnp.cos(ang), jnp.sin(ang)
    one = jnp.ones((l, KR_LANE), F32)
    rc = jnp.concatenate([one, cos, cos, jnp.ones((l, HEAD_SLOT - KR_LANE - QK_ROPE), F32)], axis=1)
    rs = jnp.concatenate([0 * one, -sin, sin, jnp.zeros((l, HEAD_SLOT - KR_LANE - QK_ROPE), F32)], axis=1)

    win = _pad_w_in(wts["w_in"])
    wuq = _pad_heads(wts["w_uq"], QK_HEAD)
    wukv = jnp.concatenate([_pad_heads(wts["w_uk"], QK_NOPE), _pad_heads(wts["w_uv"], V_HEAD)], axis=1)

    disc_in = (sp["ssm_lambda_re"], sp["ssm_lambda_im"], sp["ssm_log_dt"], sp["ssm_b_re"], sp["ssm_b_im"])
    (a_re, a_im, bb_re, bb_im), disc_vjp = jax.vjp(_ssm_disc, *disc_in)
    bre, bim = _embed_b(bb_re).astype(BF16), _embed_b(bb_im).astype(BF16)
    cre, cim = _embed_c(sp["ssm_c_re"]).astype(BF16), _embed_c(sp["ssm_c_im"]).astype(BF16)
    dvec = sp["ssm_d"].reshape(1, SSM_W)
    tab_f = _scan_tables(a_re, a_im, False)
    tab_r = _scan_tables(a_re, a_im, True)

    g1, gq, gkv = sp["mix_norm_pre"], sp["q_norm"], sp["kv_norm"]
    gpost, gpre, gfin = sp["mix_norm_post"], sp["ffn_norm_pre"], sp["ffn_norm_post"]
    bgate, bglu, convb = sp["b_gate"], sp["b_glu"], sp["conv_b"]

    hn, cq, ckv, q, k, v, u, gl = _proj_fwd(x, g1, win, gq, wuq, gkv, wukv, rc, rs, bgate, tl)
    attn, lse = _attn_fwd(q, k, v, ta, FWD_HEADS)
    y1, sre, sim = _ssm_fwd(u, bre, bim, cre, cim, dvec, tab_f, ts)
    wba = jnp.pad(wts["w_branch_attn"].reshape(N_HEADS, V_HEAD, D_MODEL),
                  ((0, 0), (0, HEAD_SLOT - V_HEAD), (0, 0))).reshape(HP, D_MODEL)
    wbs, wglu, wout = wts["w_branch_ssm"], wts["w_glu"], wts["w_out"]
    a, sm, merged, z, x1, hn2, y3 = _merge_fwd(x, gl, attn, y1, wba, wbs, wglu, bglu, wout, gpost, gpre, tl)
    late = wts["late"](x1)
    wup, wdown, convw = late["w_up"], late["w_down"], late["conv_w"]
    h = _mm(hn2, wup, "ffn_up")
    act = _conv_fwd(h, convw, convb, ts)
    ff = _mm(act, wdown, "ffn_down")
    loss, dx2, dff, dgfin = _loss_head(ff, x1, tgt, gfin, ta)

    dact = _mm(dff, wdown, "ffn_down_dx", out_dtype=BF16, bt=True)
    d_wdown = _mm_tn(act, dff, "ffn_down_dw", tk_cap=D_FF // 2)
    dh, dwg, dwv, dbg, dbv = _conv_bwd(h, dact, convw, convb, ts)
    d_convw = jnp.concatenate([dwg, dwv], axis=1)
    d_convb = jnp.concatenate([dbg, dbv], axis=1)
    dhn2 = _mm(dh, wup, "ffn_up_dx", bt=True)
    d_wup = _mm_tn(hn2, dh, "ffn_up_dw", chips=True)
    behind = wts["send_grads"]("ffn", {"w_up": d_wup, "w_down": _chip_major(d_wdown, 0)})
    (dx1, dz, dbra, dbrs, dgl, dattn, dy1, dt, y2, dgpre, dgpost, dbgate, dbglu) = _merge_bwd(
        dhn2, x1, dx2, z, gl, a, sm, y1, wba, wbs, wglu, bglu, wout, gpost, gpre + behind, tb)
    d_wout = _mm_tn(merged, dz, "w_out_dw")
    d_wba = _mm_tn(attn, dbra, "w_branch_attn_dw", chips=True)
    d_wbs = _mm_tn(y3, dbrs, "w_branch_ssm_dw", chips=True)
    d_wglu = _mm_tn(y2, dt, "w_glu_dw")
    ncol = D_MODEL // N_CHIPS
    behind = wts["send_grads"]("mix", {
        "w_glu": _chip_major(d_wglu, 0),
        "w_branch_attn": d_wba.reshape(N_CHIPS, N_HEADS, HEAD_SLOT, ncol)[:, :, :V_HEAD].reshape(
            N_CHIPS, N_HEADS * V_HEAD, ncol),
        "w_branch_ssm": d_wbs,
        "w_out": _chip_major(d_wout, 0)}, after=d_wglu)
    dq, dk, dv = _attn_bwd(q, k, v, dattn, lse + behind, _attn_delta(attn, dattn, min(2048, l), BWD_HEADS), ta,
                            BWD_HEADS)
    du, dbre, dbim, dcre, dcim, dare, daim, dd = _ssm_bwd(dy1, u, sre, sim, bre, bim, cre, cim, dvec, tab_r, ts)
    behind = wts["send_grads"]("none", {}, after=du)
    gx, dql, qn, ckvn, dproj, dg1, dgq, dgkv = _proj_bwd(
        x, dx1, cq, ckv, dq, dk, dv, du, dgl, g1 + behind, win, gq, wuq, gkv, wukv, rc, rs, tl)
    d_win = _mm_tn(hn, dproj, "w_in_dw")
    d_wuq = _mm_tn(qn, dql, "w_uq_dw")
    d_wuk = _mm_tn(ckvn, dk, "w_uk_dw")
    d_wuv = _mm_tn(ckvn, dv, "w_uv_dw")

    d_lre, d_lim, d_ldt, d_bre, d_bim = disc_vjp((dare.reshape(SSM_G, SSM_P), daim.reshape(SSM_G, SSM_P),
                                                  _extract_b(dbre), _extract_b(dbim)))
    big = {
        "w_in": _chip_major(_unpad_w_in(d_win), 1),
        "w_uq": _chip_major(_unpad_heads(d_wuq, QK_HEAD), 1),
        "w_uk": _chip_major(_unpad_heads(d_wuk, QK_NOPE), 1),
        "w_uv": _chip_major(_unpad_heads(d_wuv, V_HEAD), 1),
    }
    small = {
        "conv_w": d_convw,
        "mix_norm_pre": dg1, "q_norm": dgq, "kv_norm": dgkv,
        "ssm_lambda_re": d_lre, "ssm_lambda_im": d_lim, "ssm_log_dt": d_ldt,
        "ssm_b_re": d_bre, "ssm_b_im": d_bim,
        "ssm_c_re": _extract_c(dcre), "ssm_c_im": _extract_c(dcim),
        "ssm_d": dd.reshape(SSM_G, SSM_H), "b_glu": dbglu, "b_gate": dbgate,
        "mix_norm_post": dgpost, "ffn_norm_pre": dgpre, "conv_b": d_convb, "ffn_norm_post": dgfin,
    }
    return loss[0, 0], gx, big, small


_ANY = pl.BlockSpec(memory_space=pl.ANY)


ROW_TILE = 16


def _place():
    x, y, c = lax.axis_index("x"), lax.axis_index("y"), lax.axis_index("c")
    return x, y, c, 2 * x + y, [(1 - x, y), (x, 1 - y), (1 - x, 1 - y)]


def _half(rows, which):
    hr = rows // 2
    return pl.ds(pl.multiple_of(which * hr, ROW_TILE), hr)


def _remote(src, dst, send_sems, recv_sems, n, dev):
    return pltpu.make_async_remote_copy(src_ref=src, dst_ref=dst, send_sem=send_sems.at[n], recv_sem=recv_sems.at[n],
                                        device_id=dev, device_id_type=MESH)


def _gather_big(shards):
    nw = len(shards)
    rows = [s.shape[0] for s in shards]

    def body(*refs):
        ins, outs = refs[:nw], refs[nw:2 * nw]
        ici_send, ici_recv, d2d_send, d2d_recv = refs[2 * nw:]
        x, y, c, me, peers = _place()
        sent = []
        for i in range(nw):
            for p, (px, py) in enumerate(peers):
                cp = _remote(ins[i].at[_half(rows[i], c)], outs[i].at[me, _half(rows[i], c)], ici_send, ici_recv,
                             3 * i + p, (px, py, c))
                cp.start()
                sent.append(cp)
        for p, (px, py) in enumerate(peers):
            for i in range(nw):
                blk = outs[i].at[2 * px + py, _half(rows[i], c)]
                _remote(blk, blk, ici_send, ici_recv, 3 * i + p, (px, py, c)).wait_recv()
                cp = _remote(blk, blk, d2d_send, d2d_recv, 3 * i + p, (x, y, 1 - c))
                cp.start()
                sent.append(cp)
        for p, (px, py) in enumerate(peers):
            for i in range(nw):
                blk = outs[i].at[2 * px + py, _half(rows[i], 1 - c)]
                _remote(blk, blk, d2d_send, d2d_recv, 3 * i + p, (x, y, 1 - c)).wait_recv()
        for cp in sent:
            cp.wait_send()

    dma = pltpu.SemaphoreType.DMA
    return pl.pallas_call(
        body, name="gather_weights", in_specs=[_ANY] * nw, out_specs=[_ANY] * nw,
        out_shape=[jax.ShapeDtypeStruct((N_CHIPS,) + s.shape, s.dtype) for s in shards],
        scratch_shapes=[dma((3 * nw,)), dma((3 * nw,)), dma((3 * nw,)), dma((3 * nw,))],
    )(*shards)


_HBM = pl.BlockSpec(memory_space=pltpu.HBM)
_SEM = pl.BlockSpec(memory_space=pltpu.SEMAPHORE)
_DATAFLOW = pltpu.SideEffectType.DATAFLOW_SIDE_EFFECTING


def _exchange_start(shards, name, scatter):
    nw = len(shards)
    lands = [lax.empty(s.shape if scatter else (N_CHIPS,) + s.shape, s.dtype) for s in shards]

    def body(*refs):
        ins, zones = refs[:nw], refs[nw:2 * nw]
        send_sems, recv_sems, token = refs[2 * nw], refs[2 * nw + 1], refs[-1]
        x, y, c, me, peers = _place()
        for i in range(nw):
            for p, (px, py) in enumerate(peers):
                src = ins[i].at[2 * px + py] if scatter else ins[i]
                _remote(src, zones[i].at[me], send_sems, recv_sems, 3 * i + p, (px, py, c)).start()
        token[...] = jnp.zeros_like(token)

    thru = [pltpu.HBM(a.shape, a.dtype) for a in list(shards) + lands]
    dma = pltpu.SemaphoreType.DMA
    outs = pl.pallas_call(
        body, name=name,
        out_shape=(dma((3 * nw,)), dma((3 * nw,)), *thru, jax.ShapeDtypeStruct((SUBLANES, LANES), F32)),
        in_specs=[_HBM] * (2 * nw),
        out_specs=(_SEM, _SEM, *([_HBM] * (2 * nw)), pl.BlockSpec(memory_space=pltpu.VMEM)),
        input_output_aliases={i: 2 + i for i in range(2 * nw)},
        compiler_params=pltpu.CompilerParams(has_side_effects=_DATAFLOW),
    )(*[pltpu.with_memory_space_constraint(a, pltpu.HBM) for a in list(shards) + lands])
    return outs[0], outs[1], list(outs[2:2 + nw]), list(outs[2 + nw:2 + 2 * nw]), outs[-1]


def _exchange_wait(send_sems, recv_sems, shards, lands, after, name, scatter):
    nw = len(shards)

    def body(*refs):
        ins, zones = refs[:nw], refs[nw:2 * nw]
        send_sems, recv_sems = refs[2 * nw], refs[2 * nw + 1]
        x, y, c, me, peers = _place()
        for i in range(nw):
            for p, (px, py) in enumerate(peers):
                src = ins[i].at[2 * px + py] if scatter else ins[i]
                cp = _remote(src, zones[i].at[2 * px + py], send_sems, recv_sems, 3 * i + p, (px, py, c))
                cp.wait_send()
                cp.wait_recv()

    both = list(shards) + list(lands)
    outs = pl.pallas_call(
        body, name=name,
        out_shape=tuple(pltpu.HBM(a.shape, a.dtype) for a in both),
        in_specs=(*([_HBM] * (2 * nw)), _SEM, _SEM, _ANY), out_specs=[_HBM] * (2 * nw),
        input_output_aliases={i: i for i in range(2 * nw)},
        compiler_params=pltpu.CompilerParams(has_side_effects=_DATAFLOW),
    )(*both, send_sems, recv_sems, after)
    return list(outs[:nw]), list(outs[nw:])


def _sibling_start(grads, name):
    nw = len(grads)
    lands = [lax.empty((N_CHIPS, g.shape[1] // 2, g.shape[2]), g.dtype) for g in grads]

    def body(*refs):
        ins, zones = refs[:nw], refs[nw:2 * nw]
        send_sems, recv_sems, token = refs[2 * nw], refs[2 * nw + 1], refs[-1]
        x, y, c, _, _ = _place()
        for i in range(nw):
            _remote(ins[i].at[pl.ds(0, N_CHIPS), _half(grads[i].shape[1], 1 - c)], zones[i], send_sems, recv_sems,
                    i, (x, y, 1 - c)).start()
        token[...] = jnp.zeros_like(token)

    thru = [pltpu.HBM(a.shape, a.dtype) for a in list(grads) + lands]
    dma = pltpu.SemaphoreType.DMA
    outs = pl.pallas_call(
        body, name=name,
        out_shape=(dma((nw,)), dma((nw,)), *thru, jax.ShapeDtypeStruct((SUBLANES, LANES), F32)),
        in_specs=[_HBM] * (2 * nw),
        out_specs=(_SEM, _SEM, *([_HBM] * (2 * nw)), pl.BlockSpec(memory_space=pltpu.VMEM)),
        input_output_aliases={i: 2 + i for i in range(2 * nw)},
        compiler_params=pltpu.CompilerParams(has_side_effects=_DATAFLOW),
    )(*[pltpu.with_memory_space_constraint(a, pltpu.HBM) for a in list(grads) + lands])
    return outs[0], outs[1], list(outs[2:2 + nw]), list(outs[2 + nw:2 + 2 * nw]), outs[-1]


def _sibling_wait(send_sems, recv_sems, grads, lands, after, name):
    nw = len(grads)

    def body(*refs):
        ins, zones = refs[:nw], refs[nw:2 * nw]
        send_sems, recv_sems = refs[2 * nw], refs[2 * nw + 1]
        x, y, c, _, _ = _place()
        for i in range(nw):
            cp = _remote(ins[i].at[pl.ds(0, N_CHIPS), _half(grads[i].shape[1], 1 - c)], zones[i], send_sems, recv_sems,
                         i, (x, y, 1 - c))
            cp.wait_send()
            cp.wait_recv()

    both = list(grads) + list(lands)
    outs = pl.pallas_call(
        body, name=name,
        out_shape=tuple(pltpu.HBM(a.shape, a.dtype) for a in both),
        in_specs=(*([_HBM] * (2 * nw)), _SEM, _SEM, _ANY), out_specs=[_HBM] * (2 * nw),
        input_output_aliases={i: i for i in range(2 * nw)},
        compiler_params=pltpu.CompilerParams(has_side_effects=_DATAFLOW),
    )(*both, send_sems, recv_sems, after)
    return list(outs[:nw]), list(outs[nw:])


def _reduce_to_sibling(grads, name):
    nw = len(grads)

    def body(*refs):
        ins, outs = refs[:nw], refs[nw:2 * nw]
        send_sems, recv_sems = refs[2 * nw:]
        x, y, c, _, _ = _place()
        sent = []
        for i in range(nw):
            cp = _remote(ins[i].at[pl.ds(0, N_CHIPS), _half(grads[i].shape[1], 1 - c)], outs[i], send_sems, recv_sems,
                         i, (x, y, 1 - c))
            cp.start()
            sent.append(cp)
        for cp in sent:
            cp.wait()

    dma = pltpu.SemaphoreType.DMA
    return pl.pallas_call(
        body, name=name, in_specs=[_ANY] * nw, out_specs=[_ANY] * nw,
        out_shape=[jax.ShapeDtypeStruct((N_CHIPS, g.shape[1] // 2, g.shape[2]), g.dtype) for g in grads],
        scratch_shapes=[dma((nw,)), dma((nw,))],
    )(*grads)


def _reduce_back(totals, name):
    nw = len(totals)

    def body(*refs):
        outs = refs[nw:2 * nw]
        send_sems, recv_sems = refs[2 * nw:]
        x, y, c, _, _ = _place()
        sent = []
        for i in range(nw):
            blk = outs[i].at[_half(totals[i].shape[0], c)]
            cp = _remote(blk, blk, send_sems, recv_sems, i, (x, y, 1 - c))
            cp.start()
            sent.append(cp)
        for i in range(nw):
            blk = outs[i].at[_half(totals[i].shape[0], 1 - c)]
            _remote(blk, blk, send_sems, recv_sems, i, (x, y, 1 - c)).wait_recv()
        for cp in sent:
            cp.wait_send()

    dma = pltpu.SemaphoreType.DMA
    return pl.pallas_call(
        body, name=name, in_specs=[_ANY] * nw, out_specs=[_ANY] * nw,
        out_shape=[jax.ShapeDtypeStruct(t.shape, t.dtype) for t in totals],
        input_output_aliases={i: i for i in range(nw)},
        scratch_shapes=[dma((nw,)), dma((nw,))],
    )(*totals)


def _all_reduce_small(v, name):
    rows, w = v.shape
    hr = rows // 2
    assert hr % SUBLANES == 0

    def body(v_ref, out_ref, sib_ref, half_ref, chips_ref, send_sems, recv_sems):
        x, y, c, me, peers = _place()
        sibling = (x, y, 1 - c)
        mine = pl.ds(pl.multiple_of(c * hr, SUBLANES), hr)
        other = pl.ds(pl.multiple_of((1 - c) * hr, SUBLANES), hr)
        cp = _remote(v_ref, sib_ref, send_sems, recv_sems, 0, sibling)
        cp.start()
        cp.wait()
        half_ref[...] = v_ref[mine, :] + sib_ref[mine, :]
        sent = []
        for p, (px, py) in enumerate(peers):
            cp = _remote(half_ref, chips_ref.at[me], send_sems, recv_sems, 1 + p, (px, py, c))
            cp.start()
            sent.append(cp)
        chips_ref[me] = half_ref[...]
        for p, (px, py) in enumerate(peers):
            _remote(half_ref, chips_ref.at[2 * px + py], send_sems, recv_sems, 1 + p, (px, py, c)).wait_recv()
        for cp in sent:
            cp.wait_send()
        out_ref[mine, :] = ((chips_ref[0] + chips_ref[1]) + chips_ref[2]) + chips_ref[3]
        cp = _remote(out_ref.at[mine], out_ref.at[mine], send_sems, recv_sems, 4, sibling)
        cp.start()
        _remote(out_ref.at[other], out_ref.at[other], send_sems, recv_sems, 4, sibling).wait_recv()
        cp.wait_send()

    vm = pl.BlockSpec(memory_space=pltpu.VMEM)
    return pl.pallas_call(
        body, name=name, in_specs=[vm], out_specs=vm,
        out_shape=jax.ShapeDtypeStruct((rows, w), F32),
        scratch_shapes=[pltpu.VMEM((rows, w), F32), pltpu.VMEM((hr, w), F32), pltpu.VMEM((N_CHIPS, hr, w), F32),
                        pltpu.SemaphoreType.DMA((5,)), pltpu.SemaphoreType.DMA((5,))],
        compiler_params=pltpu.CompilerParams(vmem_limit_bytes=VMEM_LIMIT),
    )(v)


ELEMENTWISE_BLOCK = 512 * 1024


def _rows_tile(rows, cols, cap=ELEMENTWISE_BLOCK):
    best = None
    for t in range(SUBLANES, rows + 1, SUBLANES):
        if rows % t == 0 and t * cols <= cap:
            best = t
    return rows if best is None else best


def _add_pair(g, t, core, name):
    nb, n, w = t.shape
    tr = _rows_tile(n, w, 2 * ELEMENTWISE_BLOCK)
    steps = n // tr

    def body(core_ref, g_ref, t_ref, o_ref):
        o_ref[...] = (g_ref[...] + t_ref[...]).astype(BF16)

    spec = pl.BlockSpec((1, tr, w), lambda j, i, core_ref: (j, i, 0))
    return pl.pallas_call(
        body, name=name,
        grid_spec=pltpu.PrefetchScalarGridSpec(
            num_scalar_prefetch=1, grid=(nb, steps),
            in_specs=[pl.BlockSpec((1, tr, w), lambda j, i, core_ref: (j, core_ref[0] * steps + i, 0)), spec],
            out_specs=spec),
        out_shape=jax.ShapeDtypeStruct(t.shape, BF16),
        compiler_params=_params(("parallel", "parallel")))(core, g, t)


def _add_chips(landed, pairs, place, name):
    nb, n, w = landed.shape
    tr = _rows_tile(n, w)
    steps = n // tr

    def body(place_ref, r_ref, own_ref, o_ref):
        me = place_ref[0]
        acc = None
        for k in range(nb):
            blk = jnp.where(me == k, own_ref[0], r_ref[k]).astype(F32)
            acc = blk if acc is None else acc + blk
        o_ref[...] = acc

    return pl.pallas_call(
        body, name=name,
        grid_spec=pltpu.PrefetchScalarGridSpec(
            num_scalar_prefetch=1, grid=(steps,),
            in_specs=[pl.BlockSpec((nb, tr, w), lambda i, place_ref: (0, i, 0)),
                      pl.BlockSpec((1, tr, w), lambda i, place_ref: (place_ref[0], i, 0))],
            out_specs=pl.BlockSpec((tr, w), lambda i, place_ref: (place_ref[1] * steps + i, 0))),
        out_shape=jax.ShapeDtypeStruct((2 * n, w), F32),
        compiler_params=_params(("parallel",)))(place, landed, pairs)


def _adamw(w, g, m, v, name):
    rows, wd = w.shape
    tr = _rows_tile(rows, wd)
    c1 = 1.0 - ADAM_B1 ** ADAM_STEP
    c2 = 1.0 - ADAM_B2 ** ADAM_STEP

    def body(w_ref, g_ref, m_ref, v_ref, go_ref, d_ref, mo_ref, vo_ref):
        gv = g_ref[...]
        go_ref[...] = gv
        m2 = ADAM_B1 * m_ref[...] + (1.0 - ADAM_B1) * gv
        v2 = ADAM_B2 * v_ref[...] + (1.0 - ADAM_B2) * (gv * gv)
        mo_ref[...] = m2
        vo_ref[...] = v2
        d_ref[...] = -ADAM_LR * ((m2 / c1) / (jnp.sqrt(v2 / c2) + ADAM_EPS) + ADAM_WD * w_ref[...])

    spec = pl.BlockSpec((tr, wd), lambda i: (i, 0))
    shp = jax.ShapeDtypeStruct((rows, wd), F32)
    return pl.pallas_call(body, name=name, grid=(rows // tr,), in_specs=[spec] * 4, out_specs=[spec] * 4,
                          out_shape=[shp] * 4, compiler_params=_params(("parallel",)))(w, g, m, v)


BIG = [("w_in", (1024, 3232), 1), ("w_uq", (384, 768), 1), ("w_uk", (256, 512), 1), ("w_uv", (256, 512), 1),
       ("w_glu", (512, 512), 0), ("w_branch_attn", (512, 1024), 1), ("w_branch_ssm", (512, 1024), 1),
       ("w_out", (1024, 1024), 0), ("w_up", (1024, 5632), 1), ("conv_w", (3, 5632), 1), ("w_down", (2816, 1024), 0)]
SMALL = [("mix_norm_pre", (1024,)), ("q_norm", (384,)), ("kv_norm", (256,)), ("ssm_lambda_re", (32, 64)),
         ("ssm_lambda_im", (32, 64)), ("ssm_log_dt", (32,)), ("ssm_b_re", (32, 64, 16)), ("ssm_b_im", (32, 64, 16)),
         ("ssm_c_re", (32, 16, 64)), ("ssm_c_im", (32, 16, 64)), ("ssm_d", (32, 16)), ("b_glu", (512,)),
         ("b_gate", (2048,)), ("mix_norm_post", (1024,)), ("ffn_norm_pre", (1024,)), ("conv_b", (5632,)),
         ("ffn_norm_post", (1024,))]
MATMUL_W = [b for b in BIG if b[0] != "conv_w"]
LATE_W = ("w_up", "w_down", "conv_w")
CONV_W_SHAPE = (3, 2 * D_FF)
CONV_W_SHARD = (3, 2 * D_FF // N_CHIPS)
SMALL_SUM = [("loss", (1,))] + SMALL + [("conv_w", CONV_W_SHAPE)]
SMALL_ADAM = SMALL + [("conv_w", CONV_W_SHARD)]


def _pack_flat(layout, vals):
    flat = jnp.concatenate([vals[n].astype(F32).reshape(-1) for n, _ in layout])
    rows = -(-(-(-flat.shape[0] // FLAT_W)) // (2 * SUBLANES)) * 2 * SUBLANES
    return jnp.pad(flat, (0, rows * FLAT_W - flat.shape[0])).reshape(rows, FLAT_W)


def _unpack_flat(layout, flat):
    flat = flat.reshape(-1)
    out = {}
    o = 0
    for name, shape in layout:
        n = math.prod(shape)
        out[name] = flat[o:o + n].reshape(shape)
        o += n
    return out


_ARG_NAMES = ["x", "positions"] + [n for n in (
    "mix_norm_pre", "w_in", "q_norm", "w_uq", "kv_norm", "w_uk", "w_uv", "ssm_lambda_re", "ssm_lambda_im", "ssm_log_dt",
    "ssm_b_re", "ssm_b_im", "ssm_c_re", "ssm_c_im", "ssm_d", "w_glu", "b_glu", "w_branch_attn", "w_branch_ssm",
    "b_gate", "w_out", "mix_norm_post", "ffn_norm_pre", "w_up", "conv_w", "conv_b", "w_down", "ffn_norm_post")]
_WEIGHTS = _ARG_NAMES[2:]


def _gather_weights(w):
    early = [b for b in MATMUL_W if b[0] not in LATE_W]
    late = [b for b in BIG if b[0] in LATE_W]
    own = (jnp.arange(N_CHIPS) == 2 * lax.axis_index("x") + lax.axis_index("y"))[:, None, None]

    def whole(layout, mine, gathered):
        return {name: _from_chip_major(jnp.where(own, s[None], g), axis)
                for (name, _, axis), s, g in zip(layout, mine, gathered)}

    mine = [w[name].astype(BF16) for name, _, _ in early]
    gathered = _gather_big(mine)
    full = whole(early, mine, gathered)
    mine_late = [w[name].astype(F32 if name == "conv_w" else BF16) for name, _, _ in late]
    _, mine_late = lax.optimization_barrier((gathered[0], mine_late))
    send_sems, recv_sems, shards_thru, lands_thru, token = _exchange_start(mine_late, "gather_late_start", scatter=False)

    def late_weights(after):
        shards, lands = _exchange_wait(send_sems, recv_sems, shards_thru, lands_thru, after, "gather_late_wait",
                                       scatter=False)
        return whole(late, shards, lands)

    full["late"] = late_weights
    full["token"] = token[0, 0]
    return full


def _pair_sums(names, grads, tag):
    core = lax.axis_index("c").astype(jnp.int32).reshape(1)
    theirs = _reduce_to_sibling(grads, "reduce_grads_d2d" + tag)
    return [_add_pair(g, t, core, "reduce_pair_" + n) for n, g, t in zip(names, grads, theirs)]


def _send_grads(tag, grads, after, flying, pending):
    token = jnp.zeros((), F32)
    if flying:
        tag0, names0, state0 = flying.pop()
        core = lax.axis_index("c").astype(jnp.int32).reshape(1)
        mine, theirs = _sibling_wait(*state0, after, "reduce_" + tag0 + "_d2d_wait")
        pairs = [_add_pair(g, t, core, "reduce_pair_" + n) for n, g, t in zip(names0, mine, theirs)]
        send_sems, recv_sems, pairs_thru, lands_thru, tok = _exchange_start(pairs, "reduce_" + tag0 + "_start",
                                                                          scatter=True)
        pending.append((tag0, names0, send_sems, recv_sems, pairs_thru, lands_thru))
        token = token + tok[0, 0]
    if grads:
        names = list(grads)
        send_sems, recv_sems, grads_thru, lands_thru, tok = _sibling_start([grads[n] for n in names],
                                                                          "reduce_" + tag + "_d2d_start")
        flying.append((tag, names, (send_sems, recv_sems, grads_thru, lands_thru)))
        token = token + tok[0, 0]
    return token


def _reduce_grads(gbig, pending, loss, gsmall, use_sent):
    core = lax.axis_index("c").astype(jnp.int32).reshape(1)
    chip = (2 * lax.axis_index("x") + lax.axis_index("y")).astype(jnp.int32).reshape(1)
    place = jnp.concatenate([chip, core])

    def finish(names, pairs, landed, name):
        totals = [_add_chips(r, p, place, "reduce_chips_" + n) for n, r, p in zip(names, landed, pairs)]
        return dict(zip(names, _reduce_back(totals, name)))

    names = list(gbig)
    pairs = _pair_sums(names, [gbig[n] for n in names], "")
    send_sems, recv_sems, pairs_thru, lands_thru, token = _exchange_start(pairs, "reduce_last_start", scatter=True)
    sent_names, sent_pairs, sent_landed = [], [], []
    for tag, group, g_send, g_recv, g_pairs, g_lands in pending:
        got_pairs, got_landed = _exchange_wait(g_send, g_recv, g_pairs, g_lands, token, "reduce_" + tag + "_wait",
                                               scatter=True)
        sent_names, sent_pairs, sent_landed = sent_names + group, sent_pairs + got_pairs, sent_landed + got_landed
    g_sent = finish(sent_names, sent_pairs, sent_landed, "reduce_back_sent")
    vals = dict(gsmall)
    vals["loss"] = loss
    small_red = _unpack_flat(SMALL_SUM, _all_reduce_small(_pack_flat(SMALL_SUM, vals), "reduce_small"))
    after = use_sent(g_sent, small_red)
    pairs, landed = _exchange_wait(send_sems, recv_sems, pairs_thru, lands_thru, after, "reduce_last_wait", scatter=True)
    return finish(names, pairs, landed, "reduce_back_last"), small_red


def _step(args):
    x = args["x"][0]
    positions = args["positions"][0]
    tgt = args["loss_target"][0]
    w = {n: args[n][0] for n in _WEIGHTS}
    m = {n: args["m_" + n][0] for n in _WEIGHTS}
    v = {n: args["v_" + n][0] for n in _WEIGHTS}

    full = _gather_weights(w)
    sp = {n: w[n].reshape(s) for n, s in SMALL}
    for n in ("mix_norm_pre", "q_norm", "kv_norm", "b_glu", "b_gate", "mix_norm_post", "ffn_norm_pre", "conv_b",
              "ffn_norm_post"):
        sp[n] = sp[n].reshape(1, -1)
    sp["mix_norm_pre"] = sp["mix_norm_pre"] + full.pop("token")
    pending, flying = [], []
    full["send_grads"] = lambda tag, grads, after=None: _send_grads(tag, grads, after, flying, pending)
    loss, gx, gbig, gsmall = _local_step(x, positions, tgt, full, sp)
    outs = {}

    def adam_big(g_red):
        for name in g_red:
            g2, d, m2, v2 = _adamw(w[name], g_red[name], m[name], v[name], "adamw_" + name)
            outs["grad_" + name], outs["delta_" + name], outs["new_m_" + name], outs["new_v_" + name] = g2, d, m2, v2
        return v2

    def use_sent(g_sent, small_red):
        chip = 2 * lax.axis_index("x") + lax.axis_index("y")
        grads = dict(small_red)
        grads["conv_w"] = lax.dynamic_slice_in_dim(small_red["conv_w"], chip * CONV_W_SHARD[1], CONV_W_SHARD[1], axis=1)
        outs.update({"grad_" + n: grads[n] for n, _ in SMALL_ADAM})
        _, d_sm, m_sm, v_sm = _adamw(_pack_flat(SMALL_ADAM, w), _pack_flat(SMALL_ADAM, grads),
                                     _pack_flat(SMALL_ADAM, m), _pack_flat(SMALL_ADAM, v), "adamw_small")
        for prefix, flat in (("delta_", d_sm), ("new_m_", m_sm), ("new_v_", v_sm)):
            for n, val in _unpack_flat(SMALL_ADAM, flat).items():
                outs[prefix + n] = val
        return adam_big(g_sent)

    g_last, small_red = _reduce_grads(gbig, pending, loss, gsmall, use_sent)
    adam_big(g_last)
    outs = {n: val.reshape(args[n.split("_", 1)[1] if not n.startswith("new_") else n[6:]].shape)
            for n, val in outs.items()}
    res = [small_red["loss"][0], gx[None]]
    for prefix in ("grad_", "delta_", "new_m_", "new_v_"):
        res += [outs[prefix + n] for n in _WEIGHTS]
    return tuple(res)


def kernel(x, positions, mix_norm_pre, w_in, q_norm, w_uq, kv_norm, w_uk, w_uv, ssm_lambda_re, ssm_lambda_im, ssm_log_dt, ssm_b_re, ssm_b_im, ssm_c_re, ssm_c_im, ssm_d, w_glu, b_glu, w_branch_attn, w_branch_ssm, b_gate, w_out, mix_norm_post, ffn_norm_pre, w_up, conv_w, conv_b, w_down, ffn_norm_post, loss_target, m_mix_norm_pre, m_w_in, m_q_norm, m_w_uq, m_kv_norm, m_w_uk, m_w_uv, m_ssm_lambda_re, m_ssm_lambda_im, m_ssm_log_dt, m_ssm_b_re, m_ssm_b_im, m_ssm_c_re, m_ssm_c_im, m_ssm_d, m_w_glu, m_b_glu, m_w_branch_attn, m_w_branch_ssm, m_b_gate, m_w_out, m_mix_norm_post, m_ffn_norm_pre, m_w_up, m_conv_w, m_conv_b, m_w_down, m_ffn_norm_post, v_mix_norm_pre, v_w_in, v_q_norm, v_w_uq, v_kv_norm, v_w_uk, v_w_uv, v_ssm_lambda_re, v_ssm_lambda_im, v_ssm_log_dt, v_ssm_b_re, v_ssm_b_im, v_ssm_c_re, v_ssm_c_im, v_ssm_d, v_w_glu, v_b_glu, v_w_branch_attn, v_w_branch_ssm, v_b_gate, v_w_out, v_mix_norm_post, v_ffn_norm_pre, v_w_up, v_conv_w, v_conv_b, v_w_down, v_ffn_norm_post):
    given = dict(locals())
    return _step(given)
```

```python
import math

import jax
import jax.numpy as jnp
from jax import lax
from jax.experimental import pallas as pl
from jax.experimental.pallas import tpu as pltpu

F32 = jnp.float32
BF16 = jnp.bfloat16
MESH = pl.DeviceIdType.MESH

D_MODEL = 1024
N_HEADS = 8
QK_NOPE = 64
QK_ROPE = 32
QK_HEAD = QK_NOPE + QK_ROPE
V_HEAD = 64
Q_RANK = 384
KV_RANK = 256
ROPE_THETA = 10000.0
SSM_W = 512
SSM_H = 16
SSM_G = 32
SSM_P = 64
SSM_CH = SSM_G * SSM_P
D_FF = 2816
EPS = 1e-6
ADAM_LR = 0.001
ADAM_B1 = 0.9
ADAM_B2 = 0.999
ADAM_EPS = 1e-08
ADAM_WD = 0.01
ADAM_STEP = 10

LANES = 128
SUBLANES = 8
VMEM_LIMIT = 56 * 1024 * 1024

HEAD_SLOT = LANES
HP = N_HEADS * HEAD_SLOT
P_CQ, P_CKV, P_KR, P_U, P_GL, P_END = 0, 384, 640, 768, 1280, 3328
KR_LANE = 64

FLAT_W = 1024
N_CHIPS = 4


def _tile(n, cap):
    if n <= cap:
        return n
    best = None
    for t in range(LANES, cap + 1, LANES):
        if n % t == 0:
            best = t
    assert best is not None, (n, cap)
    return best


def _params(sem):
    return pltpu.CompilerParams(dimension_semantics=sem, vmem_limit_bytes=VMEM_LIMIT)


def _dot(a, b):
    return jnp.dot(a, b, preferred_element_type=F32)


def _dot_nt(a, b):
    return lax.dot_general(a, b, (((1,), (1,)), ((), ())), preferred_element_type=F32)


def _dot_tn(a, b):
    return lax.dot_general(a, b, (((0,), (0,)), ((), ())), preferred_element_type=F32)


def _rms(x, g):
    r = lax.rsqrt(jnp.mean(x * x, axis=-1, keepdims=True) + EPS)
    return x * r * g, r


def _rms_bwd(dy, x, g):
    r = lax.rsqrt(jnp.mean(x * x, axis=-1, keepdims=True) + EPS)
    dyg = dy * g
    dx = r * dyg - x * (r * r * r) * jnp.mean(dyg * x, axis=-1, keepdims=True)
    dg = jnp.sum(dy * x * r, axis=0, keepdims=True)
    return dx, dg


_GELU_K0 = math.sqrt(2.0 / math.pi)
_GELU_K1 = 0.044715


def _gelu(x):
    th = jnp.tanh(_GELU_K0 * (x + _GELU_K1 * x * x * x))
    return 0.5 * x * (1.0 + th)


def _gelu_grad(x):
    th = jnp.tanh(_GELU_K0 * (x + _GELU_K1 * x * x * x))
    return 0.5 * (1.0 + th) + 0.5 * x * (1.0 - th * th) * _GELU_K0 * (1.0 + 3.0 * _GELU_K1 * x * x)


def _sigmoid(x):
    return 1.0 / (1.0 + jnp.exp(-x))


def _rope(q, c, s):
    n = q.shape[1]
    lane = lax.broadcasted_iota(jnp.int32, q.shape, 1) % HEAD_SLOT
    sw = jnp.where(lane < KR_LANE + QK_ROPE // 2, pltpu.roll(q, n - QK_ROPE // 2, 1), pltpu.roll(q, QK_ROPE // 2, 1))
    return q * c + sw * s


def _rope_bwd(dy, c, s):
    n = dy.shape[1]
    t = dy * s
    lane = lax.broadcasted_iota(jnp.int32, dy.shape, 1) % HEAD_SLOT
    sw = jnp.where(lane < KR_LANE + QK_ROPE // 2, pltpu.roll(t, n - QK_ROPE // 2, 1), pltpu.roll(t, QK_ROPE // 2, 1))
    rope_lane = jnp.logical_and(lane >= KR_LANE, lane < KR_LANE + QK_ROPE)
    return dy * c + jnp.where(rope_lane, sw, 0.0)


def _shift_down(x, k, halo):
    xs = pltpu.roll(x, k, 0)
    hs = pltpu.roll(halo, k, 0)
    rows = lax.broadcasted_iota(jnp.int32, halo.shape, 0)
    top = jnp.where(rows < k, hs, xs[0:SUBLANES])
    return jnp.concatenate([top, xs[SUBLANES:]], axis=0)


def _shift_up(x, k, halo):
    t = x.shape[0]
    xs = pltpu.roll(x, t - k, 0)
    hs = pltpu.roll(halo, SUBLANES - k, 0)
    rows = lax.broadcasted_iota(jnp.int32, halo.shape, 0)
    bot = jnp.where(rows >= SUBLANES - k, hs, xs[t - SUBLANES:])
    return jnp.concatenate([xs[:t - SUBLANES], bot], axis=0)


def _mm(a, b, name, out_dtype=F32, bt=False, tm_cap=1024, tn_cap=1408):
    m, k = a.shape[-2:]
    parts = a.shape[0] if a.ndim == 3 else 1
    n = b.shape[0] if bt else b.shape[1]
    tm = min(tm_cap, m)
    tn = _tile(n, tn_cap)

    def body(a_ref, b_ref, o_ref):
        if not bt:
            o_ref[...] = _dot(a_ref[...], b_ref[...]).astype(out_dtype)
        elif parts == 1:
            o_ref[...] = _dot_nt(a_ref[...], b_ref[...]).astype(out_dtype)
        else:
            @pl.when(pl.program_id(2) == 0)
            def _():
                o_ref[...] = jnp.zeros_like(o_ref)

            o_ref[...] += _dot_nt(a_ref[...], b_ref[...])

    if bt:
        b_spec = pl.BlockSpec((tn, k), lambda j, i, s: (j, s))
    else:
        b_spec = pl.BlockSpec((k, tn), lambda j, i, s: (0, j))
    if a.ndim == 3:
        assert bt and out_dtype == F32
        a_spec = pl.BlockSpec((None, tm, k), lambda j, i, s: (s, i, 0))
    else:
        a_spec = pl.BlockSpec((tm, k), lambda j, i, s: (i, 0))
    return pl.pallas_call(
        body, name=name, grid=(n // tn, m // tm, parts),
        in_specs=[a_spec, b_spec],
        out_specs=pl.BlockSpec((tm, tn), lambda j, i, s: (i, j)),
        out_shape=jax.ShapeDtypeStruct((m, n), out_dtype),
        compiler_params=_params(("parallel", "parallel", "arbitrary")),
    )(a, b)


def _mm_tn(a, b, name, tk_cap=1024, tn_cap=1664, tl_cap=2048, chips=False):
    l, k = a.shape
    tk = _tile(k, tk_cap)
    tl = min(tl_cap, l)

    def body(a_ref, b_ref, o_ref):
        @pl.when(pl.program_id(2) == 0)
        def _():
            o_ref[...] = jnp.zeros_like(o_ref)

        o_ref[...] += _dot_tn(a_ref[...], b_ref[...])

    if chips:
        n = b.shape[-1] * (b.shape[0] if b.ndim == 3 else 1)
        tn = n // N_CHIPS
        assert tn % LANES == 0
        if b.ndim == 3:
            per = N_CHIPS // b.shape[0]
            b_spec = pl.BlockSpec((None, tl, tn), lambda i, j, r: (j // per, r, j % per))
        else:
            b_spec = pl.BlockSpec((tl, tn), lambda i, j, r: (r, j))
        out_spec = pl.BlockSpec((None, tk, tn), lambda i, j, r: (j, i, 0))
        out_shape = jax.ShapeDtypeStruct((N_CHIPS, k, tn), F32)
    else:
        n = b.shape[1]
        tn = _tile(n, tn_cap)
        b_spec = pl.BlockSpec((tl, tn), lambda i, j, r: (r, j))
        out_spec = pl.BlockSpec((tk, tn), lambda i, j, r: (i, j))
        out_shape = jax.ShapeDtypeStruct((k, n), F32)
    return pl.pallas_call(
        body, name=name, grid=(k // tk, n // tn, l // tl),
        in_specs=[pl.BlockSpec((tl, tk), lambda i, j, r: (r, i)), b_spec],
        out_specs=out_spec, out_shape=out_shape,
        compiler_params=_params(("parallel", "parallel", "arbitrary")),
    )(a, b)


def _row(tl, n):
    return pl.BlockSpec((tl, n), lambda i: (i, 0))


def _const(shape):
    return pl.BlockSpec(shape, lambda i: tuple(0 for _ in shape))


def _proj_fwd(x, g1, win, gq, wuq, gkv, wukv, rc, rs, bg, tl):
    l = x.shape[0]

    def body(x_ref, g1_ref, win_ref, gq_ref, wuq_ref, gkv_ref, wukv_ref, rc_ref, rs_ref, bg_ref,
             hn_ref, cq_ref, ckv_ref, q_ref, k_ref, v_ref, u_ref, gl_ref):
        hn, _ = _rms(x_ref[...], g1_ref[...])
        hnb = hn.astype(BF16)
        hn_ref[...] = hnb
        proj = _dot(hnb, win_ref[...])
        cq = proj[:, P_CQ:P_CKV]
        ckv = proj[:, P_CKV:P_KR]
        kr = proj[:, P_KR:P_U]
        cq_ref[...] = cq
        ckv_ref[...] = ckv
        u_ref[...] = proj[:, P_U:P_GL]
        gl_ref[...] = (proj[:, P_GL:P_END] + bg_ref[...]).astype(BF16)
        qn, _ = _rms(cq, gq_ref[...])
        q = _dot(qn.astype(BF16), wuq_ref[...])
        c1 = rc_ref[...]
        s1 = rs_ref[...]
        q_ref[...] = (_rope(q, jnp.tile(c1, (1, N_HEADS)), jnp.tile(s1, (1, N_HEADS))) * Q_PRESCALE).astype(BF16)
        ckvn, _ = _rms(ckv, gkv_ref[...])
        kv = _dot(ckvn.astype(BF16), wukv_ref[...])
        krr = _rope(kr, c1, s1)
        k_ref[...] = (kv[:, :HP] + jnp.tile(krr, (1, N_HEADS))).astype(BF16)
        v_ref[...] = kv[:, HP:].astype(BF16)

    outs = [(D_MODEL, BF16), (Q_RANK, F32), (KV_RANK, F32), (HP, BF16), (HP, BF16), (HP, BF16),
            (SSM_W, F32), (2 * D_MODEL, BF16)]
    return pl.pallas_call(
        body, name="proj_fwd", grid=(l // tl,),
        in_specs=[_row(tl, D_MODEL), _const((1, D_MODEL)), _const((D_MODEL, P_END)), _const((1, Q_RANK)),
                  _const((Q_RANK, HP)), _const((1, KV_RANK)), _const((KV_RANK, 2 * HP)),
                  _row(tl, HEAD_SLOT), _row(tl, HEAD_SLOT), _const((1, 2 * D_MODEL))],
        out_specs=[_row(tl, n) for n, _ in outs],
        out_shape=[jax.ShapeDtypeStruct((l, n), dt) for n, dt in outs],
        compiler_params=_params(("parallel",)),
    )(x, g1, win, gq, wuq, gkv, wukv, rc, rs, bg)


_NEG = -1e30


LOG2E = 1.0 / math.log(2.0)
LN2 = math.log(2.0)
ATTN_SCALE = 1.0 / math.sqrt(QK_HEAD)
Q_PRESCALE = ATTN_SCALE * LOG2E
FWD_HEADS = 8
BWD_HEADS = 8


def _causal_pairs(nq, by_query):
    if by_query:
        pairs = [(i, j) for i in range(nq) for j in range(i + 1)]
    else:
        pairs = [(i, j) for j in range(nq) for i in range(j, nq)]
    return jnp.array([p[0] for p in pairs], jnp.int32), jnp.array([p[1] for p in pairs], jnp.int32)


def _diag_mask_t(s):
    rows = lax.broadcasted_iota(jnp.int32, s.shape, 0)
    cols = lax.broadcasted_iota(jnp.int32, s.shape, 1)
    return jnp.where(rows <= cols, s, _NEG)


def _attn_fwd(q, k, v, tq, heads):
    l = q.shape[0]
    nq = l // tq
    it, jt = _causal_pairs(nq, True)

    def body(it_ref, jt_ref, q_ref, k_ref, v_ref, o_ref, lse_ref, m_ref, l_ref, acc_ref):
        t = pl.program_id(1)
        i = it_ref[t]
        j = jt_ref[t]

        @pl.when(j == 0)
        def _():
            m_ref[...] = jnp.full_like(m_ref, _NEG)
            l_ref[...] = jnp.zeros_like(l_ref)
            acc_ref[...] = jnp.zeros_like(acc_ref)

        def update(on_diagonal):
            for hh in range(heads):
                sl = slice(hh * HEAD_SLOT, (hh + 1) * HEAD_SLOT)
                s = _dot_nt(k_ref[:, sl], q_ref[:, sl])
                if on_diagonal:
                    s = _diag_mask_t(s)
                m_old = m_ref[hh]
                m_new = jnp.maximum(m_old, jnp.max(s, axis=0, keepdims=True))
                p = jnp.exp2(s - m_new)
                alpha = jnp.exp2(m_old - m_new)
                l_ref[hh] = alpha * l_ref[hh] + jnp.sum(p, axis=0, keepdims=True)
                acc_ref[hh] = alpha * acc_ref[hh] + _dot_tn(v_ref[:, sl], p.astype(BF16))
                m_ref[hh] = m_new

        @pl.when(j < i)
        def _():
            update(False)

        @pl.when(j == i)
        def _():
            update(True)
            for hh in range(heads):
                sl = slice(hh * HEAD_SLOT, (hh + 1) * HEAD_SLOT)
                o_ref[:, sl] = (acc_ref[hh] / l_ref[hh]).T.astype(BF16)
                lse_ref[hh] = m_ref[hh] + jnp.log(l_ref[hh]) * LOG2E

    blk = (tq, (heads * HEAD_SLOT))
    qmap = lambda h, t, it_ref, jt_ref: (it_ref[t], h)
    kmap = lambda h, t, it_ref, jt_ref: (jt_ref[t], h)
    row = pl.BlockSpec((heads, 1, tq), lambda h, t, it_ref, jt_ref: (h, 0, it_ref[t]))
    return pl.pallas_call(
        body, name="attn_fwd",
        grid_spec=pltpu.PrefetchScalarGridSpec(
            num_scalar_prefetch=2, grid=(N_HEADS // heads, it.shape[0]),
            in_specs=[pl.BlockSpec(blk, qmap), pl.BlockSpec(blk, kmap), pl.BlockSpec(blk, kmap)],
            out_specs=[pl.BlockSpec(blk, qmap), row],
            scratch_shapes=[pltpu.VMEM((heads, 1, tq), F32), pltpu.VMEM((heads, 1, tq), F32),
                            pltpu.VMEM((heads, HEAD_SLOT, tq), F32)]),
        out_shape=[jax.ShapeDtypeStruct((l, HP), BF16), jax.ShapeDtypeStruct((N_HEADS, 1, l), F32)],
        compiler_params=_params(("parallel", "arbitrary")),
    )(it, jt, q, k, v)


def _attn_delta(o, do, tq, heads):
    l = o.shape[0]

    def body(o_ref, do_ref, d_ref):
        prod = o_ref[...].astype(F32) * do_ref[...].astype(F32)
        for hh in range(heads):
            d_ref[hh] = jnp.sum(prod[:, hh * HEAD_SLOT:(hh + 1) * HEAD_SLOT].T, axis=0, keepdims=True)

    blk = pl.BlockSpec((tq, (heads * HEAD_SLOT)), lambda h, i: (i, h))
    return pl.pallas_call(
        body, name="attn_delta", grid=(N_HEADS // heads, l // tq), in_specs=[blk, blk],
        out_specs=pl.BlockSpec((heads, 1, tq), lambda h, i: (h, 0, i)),
        out_shape=jax.ShapeDtypeStruct((N_HEADS, 1, l), F32),
        compiler_params=_params(("parallel", "parallel")),
    )(o, do)


def _attn_bwd(q, k, v, do, lse, delta, tq, heads):
    l = q.shape[0]
    nq = l // tq
    it, jt = _causal_pairs(nq, False)

    def body(it_ref, jt_ref, q_ref, k_ref, v_ref, do_ref, lse_ref, dl_ref, dq_ref, dk_ref, dv_ref, dka_ref, dva_ref):
        t = pl.program_id(1)
        i = it_ref[t]
        j = jt_ref[t]

        @pl.when(t == 0)
        def _():
            dq_ref[...] = jnp.zeros_like(dq_ref)

        @pl.when(i == j)
        def _():
            dka_ref[...] = jnp.zeros_like(dka_ref)
            dva_ref[...] = jnp.zeros_like(dva_ref)

        def update(on_diagonal):
            r0 = pl.multiple_of(i * tq, tq)
            for hh in range(heads):
                sl = slice(hh * HEAD_SLOT, (hh + 1) * HEAD_SLOT)
                qb = q_ref[:, sl]
                kb = k_ref[:, sl]
                dob = do_ref[:, sl]
                s = _dot_nt(kb, qb)
                if on_diagonal:
                    s = _diag_mask_t(s)
                p = jnp.exp2(s - lse_ref[hh])
                dva_ref[:, sl] += _dot(p.astype(BF16), dob)
                dp = _dot_nt(v_ref[:, sl], dob)
                ds = (p * (dp - dl_ref[hh])).astype(BF16)
                dka_ref[:, sl] += _dot(ds, qb)
                dq_ref[pl.ds(r0, tq), sl] += ATTN_SCALE * _dot_tn(ds, kb)

        @pl.when(j < i)
        def _():
            update(False)

        @pl.when(j == i)
        def _():
            update(True)

        @pl.when(i == nq - 1)
        def _():
            dk_ref[...] = (dka_ref[...] * LN2).astype(BF16)
            dv_ref[...] = dva_ref[...].astype(BF16)

    blk = (tq, (heads * HEAD_SLOT))
    qmap = lambda h, t, it_ref, jt_ref: (it_ref[t], h)
    kmap = lambda h, t, it_ref, jt_ref: (jt_ref[t], h)
    row = pl.BlockSpec((heads, 1, tq), lambda h, t, it_ref, jt_ref: (h, 0, it_ref[t]))
    return pl.pallas_call(
        body, name="attn_bwd",
        grid_spec=pltpu.PrefetchScalarGridSpec(
            num_scalar_prefetch=2, grid=(N_HEADS // heads, it.shape[0]),
            in_specs=[pl.BlockSpec(blk, qmap), pl.BlockSpec(blk, kmap), pl.BlockSpec(blk, kmap),
                      pl.BlockSpec(blk, qmap), row, row],
            out_specs=[pl.BlockSpec((l, (heads * HEAD_SLOT)), lambda h, t, it_ref, jt_ref: (0, h)), pl.BlockSpec(blk, kmap),
                       pl.BlockSpec(blk, kmap)],
            scratch_shapes=[pltpu.VMEM(blk, F32), pltpu.VMEM(blk, F32)]),
        out_shape=[jax.ShapeDtypeStruct((l, HP), F32), jax.ShapeDtypeStruct((l, HP), BF16),
                   jax.ShapeDtypeStruct((l, HP), BF16)],
        compiler_params=_params(("parallel", "arbitrary")),
    )(it, jt, q, k, v, do, lse, delta)


SSM_CB = 512
SSM_UB = 128
SSM_NB = SSM_CH // SSM_CB


def _scan_tiles(re_ref, im_ref, tab, carry, n_tiles, reverse):
    group = 2
    assert n_tiles % group == 0
    pr, pi = tab[6], tab[7]

    def inside(sr, si):
        for step, k in enumerate((1, 2, 4)):
            mr, mi = tab[2 * step], tab[2 * step + 1]
            sh = (SUBLANES - k) if reverse else k
            rr = pltpu.roll(sr, sh, 0)
            ri = pltpu.roll(si, sh, 0)
            sr, si = sr + mr * rr - mi * ri, si + mr * ri + mi * rr
        return sr, si

    def body(n, c):
        cr, ci = c
        first = (n_tiles - group * (n + 1)) if reverse else group * n
        r0 = pl.multiple_of(first * SUBLANES, group * SUBLANES)
        rows = [pl.ds(r0 + g * SUBLANES, SUBLANES) for g in range(group)]
        tiles = [inside(re_ref[r, :], im_ref[r, :]) for r in rows]
        for g in (range(group - 1, -1, -1) if reverse else range(group)):
            sr, si = tiles[g]
            sr, si = sr + pr * cr - pi * ci, si + pr * ci + pi * cr
            re_ref[rows[g], :] = sr
            im_ref[rows[g], :] = si
            edge = slice(0, 1) if reverse else slice(SUBLANES - 1, SUBLANES)
            cr, ci = sr[edge, :], si[edge, :]
        return cr, ci

    return lax.fori_loop(0, n_tiles // group, body, carry)


def _ssm_fwd(u, bre, bim, cre, cim, dvec, tab, tt):
    l = u.shape[0]
    nt = l // tt

    def body(u_ref, bre_ref, bim_ref, cre_ref, cim_ref, d_ref, tab_ref, y_ref, sre_ref, sim_ref, car_ref):
        @pl.when(pl.program_id(1) == 0)
        def _():
            car_ref[...] = jnp.zeros_like(car_ref)

        uf = u_ref[...]
        ub = uf.astype(BF16)
        sre_ref[...] = _dot(ub, bre_ref[0])
        sim_ref[...] = _dot(ub, bim_ref[0])
        tab_v = [tab_ref[n] for n in range(8)]
        cr, ci = _scan_tiles(sre_ref, sim_ref, tab_v, (car_ref[0:1, :], car_ref[8:9, :]), tt // SUBLANES, False)
        car_ref[0:1, :] = cr
        car_ref[8:9, :] = ci
        y_ref[...] = (_dot(sre_ref[...].astype(BF16), cre_ref[0]) - _dot(sim_ref[...].astype(BF16), cim_ref[0])
                      + d_ref[...] * uf)

    return pl.pallas_call(
        body, name="ssm_fwd", grid=(SSM_NB, nt),
        in_specs=[pl.BlockSpec((tt, SSM_UB), lambda m, t: (t, m)),
                  pl.BlockSpec((1, SSM_UB, SSM_CB), lambda m, t: (m, 0, 0)),
                  pl.BlockSpec((1, SSM_UB, SSM_CB), lambda m, t: (m, 0, 0)),
                  pl.BlockSpec((1, SSM_CB, SSM_UB), lambda m, t: (m, 0, 0)),
                  pl.BlockSpec((1, SSM_CB, SSM_UB), lambda m, t: (m, 0, 0)),
                  pl.BlockSpec((1, SSM_UB), lambda m, t: (0, m)),
                  pl.BlockSpec((8, SUBLANES, SSM_CB), lambda m, t: (0, 0, m))],
        out_specs=[pl.BlockSpec((tt, SSM_UB), lambda m, t: (t, m)),
                   pl.BlockSpec((tt, SSM_CB), lambda m, t: (t, m)),
                   pl.BlockSpec((tt, SSM_CB), lambda m, t: (t, m))],
        out_shape=[jax.ShapeDtypeStruct((l, SSM_W), F32), jax.ShapeDtypeStruct((l, SSM_CH), F32),
                   jax.ShapeDtypeStruct((l, SSM_CH), F32)],
        scratch_shapes=[pltpu.VMEM((2 * SUBLANES, SSM_CB), F32)],
        compiler_params=_params(("parallel", "arbitrary")),
    )(u, bre, bim, cre, cim, dvec, tab)


def _ssm_bwd(dy, u, sre, sim, bre, bim, cre, cim, dvec, tab, tt):
    l = u.shape[0]
    nt = l // tt
    tpb = tt // SUBLANES

    def body(dy_ref, u_ref, sre_ref, sim_ref, hre_ref, him_ref, bre_ref, bim_ref, cre_ref, cim_ref, d_ref, tab_ref,
             du_ref, dbre_ref, dbim_ref, dcre_ref, dcim_ref, dare_ref, daim_ref, dd_ref, lr_ref, li_ref, car_ref):
        t = pl.program_id(1)

        @pl.when(t == 0)
        def _():
            car_ref[...] = jnp.zeros_like(car_ref)
            for ref in (dbre_ref, dbim_ref, dcre_ref, dcim_ref, dare_ref, daim_ref, dd_ref):
                ref[...] = jnp.zeros_like(ref)

        dyf = dy_ref[...]
        dyb = dyf.astype(BF16)
        uf = u_ref[...]
        s_re = sre_ref[...]
        s_im = sim_ref[...]
        lr_ref[...] = _dot_nt(dyb, cre_ref[0])
        li_ref[...] = -_dot_nt(dyb, cim_ref[0])
        dcre_ref[0] += _dot_tn(s_re.astype(BF16), dyb)
        dcim_ref[0] -= _dot_tn(s_im.astype(BF16), dyb)
        tab_v = [tab_ref[n] for n in range(8)]
        cr, ci = _scan_tiles(lr_ref, li_ref, tab_v, (car_ref[0:1, :], car_ref[8:9, :]), tpb, True)
        car_ref[0:1, :] = cr
        car_ref[8:9, :] = ci
        lam_r = lr_ref[...]
        lam_i = li_ref[...]
        keep = jnp.where(t == nt - 1, 0.0, 1.0)
        sp_r = _shift_down(s_re, 1, hre_ref[...] * keep)
        sp_i = _shift_down(s_im, 1, him_ref[...] * keep)
        dare_ref[...] += jnp.sum(lam_r * sp_r + lam_i * sp_i, axis=0, keepdims=True)
        daim_ref[...] += jnp.sum(lam_i * sp_r - lam_r * sp_i, axis=0, keepdims=True)
        lrb = lam_r.astype(BF16)
        lib = lam_i.astype(BF16)
        du_ref[...] = _dot_nt(lrb, bre_ref[0]) + _dot_nt(lib, bim_ref[0]) + dyf * d_ref[...]
        ub = uf.astype(BF16)
        dbre_ref[0] += _dot_tn(ub, lrb)
        dbim_ref[0] += _dot_tn(ub, lib)
        dd_ref[...] += jnp.sum(dyf * uf, axis=0, keepdims=True)

    rev = lambda m, t: (nt - 1 - t, m)
    halo = lambda m, t: (jnp.maximum((nt - 1 - t) * tpb - 1, 0), m)
    wb = pl.BlockSpec((1, SSM_UB, SSM_CB), lambda m, t: (m, 0, 0))
    wc = pl.BlockSpec((1, SSM_CB, SSM_UB), lambda m, t: (m, 0, 0))
    vec_c = pl.BlockSpec((1, SSM_CB), lambda m, t: (0, m))
    vec_u = pl.BlockSpec((1, SSM_UB), lambda m, t: (0, m))
    return pl.pallas_call(
        body, name="ssm_bwd", grid=(SSM_NB, nt),
        in_specs=[pl.BlockSpec((tt, SSM_UB), rev), pl.BlockSpec((tt, SSM_UB), rev),
                  pl.BlockSpec((tt, SSM_CB), rev), pl.BlockSpec((tt, SSM_CB), rev),
                  pl.BlockSpec((SUBLANES, SSM_CB), halo), pl.BlockSpec((SUBLANES, SSM_CB), halo),
                  wb, wb, wc, wc, vec_u,
                  pl.BlockSpec((8, SUBLANES, SSM_CB), lambda m, t: (0, 0, m))],
        out_specs=[pl.BlockSpec((tt, SSM_UB), rev), wb, wb, wc, wc, vec_c, vec_c, vec_u],
        out_shape=[jax.ShapeDtypeStruct((l, SSM_W), F32),
                   jax.ShapeDtypeStruct((SSM_NB, SSM_UB, SSM_CB), F32), jax.ShapeDtypeStruct((SSM_NB, SSM_UB, SSM_CB), F32),
                   jax.ShapeDtypeStruct((SSM_NB, SSM_CB, SSM_UB), F32), jax.ShapeDtypeStruct((SSM_NB, SSM_CB, SSM_UB), F32),
                   jax.ShapeDtypeStruct((1, SSM_CH), F32), jax.ShapeDtypeStruct((1, SSM_CH), F32),
                   jax.ShapeDtypeStruct((1, SSM_W), F32)],
        scratch_shapes=[pltpu.VMEM((tt, SSM_CB), F32), pltpu.VMEM((tt, SSM_CB), F32),
                        pltpu.VMEM((2 * SUBLANES, SSM_CB), F32)],
        compiler_params=_params(("parallel", "arbitrary")),
    )(dy, u, sre, sim, sre, sim, bre, bim, cre, cim, dvec, tab)


def _merge_fwd(x, gl, attn, y1, wba, wbs, wglu, bglu, wout, gpost, gpre, tl):
    l = x.shape[0]

    def body(x_ref, gl_ref, at_ref, y1_ref, wba_ref, wbs_ref, wglu_ref, bglu_ref, wout_ref, gpost_ref, gpre_ref,
             a_ref, sm_ref, mg_ref, z_ref, x1_ref, hn2_ref, y3_ref):
        y2 = _gelu(y1_ref[...])
        sg = _sigmoid(_dot(y2.astype(BF16), wglu_ref[...]) + bglu_ref[...])
        y3 = (y2 * sg).astype(BF16)
        y3_ref[...] = y3
        a = _dot(at_ref[...], wba_ref[...])
        sm = _dot(y3, wbs_ref[...])
        a_ref[...] = a.astype(BF16)
        sm_ref[...] = sm.astype(BF16)
        g = _sigmoid(gl_ref[...].astype(F32))
        merged = (g[:, :D_MODEL] * a + g[:, D_MODEL:] * sm).astype(BF16)
        mg_ref[...] = merged
        z = _dot(merged, wout_ref[...])
        z_ref[...] = z
        n, _ = _rms(z, gpost_ref[...])
        x1 = x_ref[...] + n
        x1_ref[...] = x1
        hn2, _ = _rms(x1, gpre_ref[...])
        hn2_ref[...] = hn2.astype(BF16)

    outs = [(D_MODEL, BF16), (D_MODEL, BF16), (D_MODEL, BF16), (D_MODEL, F32), (D_MODEL, F32), (D_MODEL, BF16),
            (SSM_W, BF16)]
    return pl.pallas_call(
        body, name="merge_fwd", grid=(l // tl,),
        in_specs=[_row(tl, D_MODEL), _row(tl, 2 * D_MODEL), _row(tl, HP), _row(tl, SSM_W),
                  _const((HP, D_MODEL)), _const((SSM_W, D_MODEL)), _const((SSM_W, SSM_W)), _const((1, SSM_W)),
                  _const((D_MODEL, D_MODEL)), _const((1, D_MODEL)), _const((1, D_MODEL))],
        out_specs=[_row(tl, n) for n, _ in outs],
        out_shape=[jax.ShapeDtypeStruct((l, n), dt) for n, dt in outs],
        compiler_params=_params(("parallel",)),
    )(x, gl, attn, y1, wba, wbs, wglu, bglu, wout, gpost, gpre)


def _merge_bwd(dhn2, x1, dx2, z, gl, a, sm, y1, wba, wbs, wglu, bglu, wout, gpost, gpre, tl):
    l = x1.shape[0]

    def body(dhn2_ref, x1_ref, dx2_ref, z_ref, gl_ref, a_ref, sm_ref, y1_ref,
             wba_ref, wbs_ref, wglu_ref, bglu_ref, wout_ref, gpost_ref, gpre_ref,
             dx1_ref, dz_ref, dbra_ref, dbrs_ref, dgl_ref, dat_ref, dy1_ref, dt_ref, y2_ref,
             dgpre_ref, dgpost_ref, dbg_ref, dbglu_ref):
        @pl.when(pl.program_id(0) == 0)
        def _():
            for ref in (dgpre_ref, dgpost_ref, dbg_ref, dbglu_ref):
                ref[...] = jnp.zeros_like(ref)

        dx1a, dgpre = _rms_bwd(dhn2_ref[...], x1_ref[...], gpre_ref[...])
        dgpre_ref[...] += dgpre
        dx1 = dx2_ref[...] + dx1a
        dx1_ref[...] = dx1
        dz, dgpost = _rms_bwd(dx1, z_ref[...], gpost_ref[...])
        dgpost_ref[...] += dgpost
        dzb = dz.astype(BF16)
        dz_ref[...] = dzb
        dm = _dot_nt(dzb, wout_ref[...])
        g = _sigmoid(gl_ref[...].astype(F32))
        g0 = g[:, :D_MODEL]
        g1 = g[:, D_MODEL:]
        dbra = (dm * g0).astype(BF16)
        dbrs = (dm * g1).astype(BF16)
        dbra_ref[...] = dbra
        dbrs_ref[...] = dbrs
        dgl0 = dm * a_ref[...].astype(F32) * g0 * (1.0 - g0)
        dgl1 = dm * sm_ref[...].astype(F32) * g1 * (1.0 - g1)
        dgl_ref[:, :D_MODEL] = dgl0.astype(BF16)
        dgl_ref[:, D_MODEL:] = dgl1.astype(BF16)
        dbg_ref[:, :D_MODEL] += jnp.sum(dgl0, axis=0, keepdims=True)
        dbg_ref[:, D_MODEL:] += jnp.sum(dgl1, axis=0, keepdims=True)
        dat_ref[...] = _dot_nt(dbra, wba_ref[...]).astype(BF16)
        dy3 = _dot_nt(dbrs, wbs_ref[...])
        y1v = y1_ref[...]
        y2 = _gelu(y1v)
        y2b = y2.astype(BF16)
        y2_ref[...] = y2b
        sg = _sigmoid(_dot(y2b, wglu_ref[...]) + bglu_ref[...])
        dt = dy3 * y2 * sg * (1.0 - sg)
        dtb = dt.astype(BF16)
        dt_ref[...] = dtb
        dbglu_ref[...] += jnp.sum(dt, axis=0, keepdims=True)
        dy2 = dy3 * sg + _dot_nt(dtb, wglu_ref[...])
        dy1_ref[...] = dy2 * _gelu_grad(y1v)

    outs = [(D_MODEL, F32), (D_MODEL, BF16), (D_MODEL, BF16), (D_MODEL, BF16), (2 * D_MODEL, BF16), (HP, BF16),
            (SSM_W, F32), (SSM_W, BF16), (SSM_W, BF16)]
    accs = [D_MODEL, D_MODEL, 2 * D_MODEL, SSM_W]
    return pl.pallas_call(
        body, name="merge_bwd", grid=(l // tl,),
        in_specs=[_row(tl, D_MODEL), _row(tl, D_MODEL), _row(tl, D_MODEL), _row(tl, D_MODEL),
                  _row(tl, 2 * D_MODEL), _row(tl, D_MODEL), _row(tl, D_MODEL), _row(tl, SSM_W),
                  _const((HP, D_MODEL)), _const((SSM_W, D_MODEL)), _const((SSM_W, SSM_W)), _const((1, SSM_W)),
                  _const((D_MODEL, D_MODEL)), _const((1, D_MODEL)), _const((1, D_MODEL))],
        out_specs=[_row(tl, n) for n, _ in outs] + [_const((1, n)) for n in accs],
        out_shape=[jax.ShapeDtypeStruct((l, n), dt) for n, dt in outs]
        + [jax.ShapeDtypeStruct((1, n), F32) for n in accs],
        compiler_params=_params(("arbitrary",)),
    )(dhn2, x1, dx2, z, gl, a, sm, y1, wba, wbs, wglu, bglu, wout, gpost, gpre)


def _proj_bwd(x, dx1, cq, ckv, dq, dk, dv, du, dgl, g1, win, gq, wuq, gkv, wukv, rc, rs, tl):
    l = x.shape[0]

    def body(x_ref, dx1_ref, cq_ref, ckv_ref, dq_ref, dk_ref, dv_ref, du_ref, dgl_ref,
             g1_ref, win_ref, gq_ref, wuq_ref, gkv_ref, wukv_ref, rc_ref, rs_ref,
             gx_ref, dql_ref, qn_ref, ckvn_ref, dproj_ref, dg1_ref, dgq_ref, dgkv_ref):
        @pl.when(pl.program_id(0) == 0)
        def _():
            for ref in (dg1_ref, dgq_ref, dgkv_ref):
                ref[...] = jnp.zeros_like(ref)

        c1 = rc_ref[...]
        s1 = rs_ref[...]
        dql = _rope_bwd(dq_ref[...], jnp.tile(c1, (1, N_HEADS)), jnp.tile(s1, (1, N_HEADS))).astype(BF16)
        dql_ref[...] = dql
        dqn = _dot_nt(dql, wuq_ref[...])
        cq = cq_ref[...]
        qn, _ = _rms(cq, gq_ref[...])
        qn_ref[...] = qn.astype(BF16)
        dcq, dgq = _rms_bwd(dqn, cq, gq_ref[...])
        dgq_ref[...] += dgq
        dkb = dk_ref[...]
        dvb = dv_ref[...]
        dkf = dkb.astype(F32)
        dkr = dkf[:, 0:HEAD_SLOT]
        for h in range(1, N_HEADS):
            dkr = dkr + dkf[:, h * HEAD_SLOT:(h + 1) * HEAD_SLOT]
        dkr = _rope_bwd(dkr, c1, s1)
        dckvn = _dot_nt(dkb, wukv_ref[:, :HP]) + _dot_nt(dvb, wukv_ref[:, HP:])
        ckv = ckv_ref[...]
        ckvn, _ = _rms(ckv, gkv_ref[...])
        ckvn_ref[...] = ckvn.astype(BF16)
        dckv, dgkv = _rms_bwd(dckvn, ckv, gkv_ref[...])
        dgkv_ref[...] += dgkv
        dproj_ref[:, P_CQ:P_CKV] = dcq.astype(BF16)
        dproj_ref[:, P_CKV:P_KR] = dckv.astype(BF16)
        dproj_ref[:, P_KR:P_U] = dkr.astype(BF16)
        dproj_ref[:, P_U:P_GL] = du_ref[...].astype(BF16)
        dproj_ref[:, P_GL:P_END] = dgl_ref[...]
        dhn = _dot_nt(dproj_ref[...], win_ref[...])
        dxa, dg1 = _rms_bwd(dhn, x_ref[...], g1_ref[...])
        dg1_ref[...] += dg1
        gx_ref[...] = dx1_ref[...] + dxa

    outs = [(D_MODEL, F32), (HP, BF16), (Q_RANK, BF16), (KV_RANK, BF16), (P_END, BF16)]
    accs = [D_MODEL, Q_RANK, KV_RANK]
    return pl.pallas_call(
        body, name="proj_bwd", grid=(l // tl,),
        in_specs=[_row(tl, D_MODEL), _row(tl, D_MODEL), _row(tl, Q_RANK), _row(tl, KV_RANK), _row(tl, HP),
                  _row(tl, HP), _row(tl, HP), _row(tl, SSM_W), _row(tl, 2 * D_MODEL),
                  _const((1, D_MODEL)), _const((D_MODEL, P_END)), _const((1, Q_RANK)), _const((Q_RANK, HP)),
                  _const((1, KV_RANK)), _const((KV_RANK, 2 * HP)), _row(tl, HEAD_SLOT), _row(tl, HEAD_SLOT)],
        out_specs=[_row(tl, n) for n, _ in outs] + [_const((1, n)) for n in accs],
        out_shape=[jax.ShapeDtypeStruct((l, n), dt) for n, dt in outs]
        + [jax.ShapeDtypeStruct((1, n), F32) for n in accs],
        compiler_params=_params(("arbitrary",)),
    )(x, dx1, cq, ckv, dq, dk, dv, du, dgl, g1, win, gq, wuq, gkv, wukv, rc, rs)


CONV_CB = 256
CONV_NB = D_FF // CONV_CB
CONV_ROWS = 16


def _conv3(h, halo, w, b):
    return b + w[0:1, :] * _shift_down(h, 2, halo) + w[1:2, :] * _shift_down(h, 1, halo) + w[2:3, :] * h


def _conv_fwd(h, cw, cb, tl):
    l = h.shape[0]

    def body(hg_ref, hv_ref, wg_ref, wv_ref, bg_ref, bv_ref, act_ref, halo_ref):
        @pl.when(pl.program_id(1) == 0)
        def _():
            halo_ref[...] = jnp.zeros_like(halo_ref)

        hg = hg_ref[...]
        hv = hv_ref[...]
        cg = _conv3(hg, halo_ref[0:SUBLANES, :], wg_ref[...], bg_ref[...])
        cv = _conv3(hv, halo_ref[SUBLANES:, :], wv_ref[...], bv_ref[...])
        act_ref[...] = (_gelu(cg) * cv).astype(BF16)
        halo_ref[0:SUBLANES, :] = hg[tl - SUBLANES:, :]
        halo_ref[SUBLANES:, :] = hv[tl - SUBLANES:, :]

    gmap = lambda c, r: (r, c)
    vmap = lambda c, r: (r, CONV_NB + c)
    return pl.pallas_call(
        body, name="conv_fwd", grid=(CONV_NB, l // tl),
        in_specs=[pl.BlockSpec((tl, CONV_CB), gmap), pl.BlockSpec((tl, CONV_CB), vmap),
                  pl.BlockSpec((3, CONV_CB), lambda c, r: (0, c)), pl.BlockSpec((3, CONV_CB), lambda c, r: (0, CONV_NB + c)),
                  pl.BlockSpec((1, CONV_CB), lambda c, r: (0, c)), pl.BlockSpec((1, CONV_CB), lambda c, r: (0, CONV_NB + c))],
        out_specs=pl.BlockSpec((tl, CONV_CB), gmap),
        out_shape=jax.ShapeDtypeStruct((l, D_FF), BF16),
        scratch_shapes=[pltpu.VMEM((2 * SUBLANES, CONV_CB), F32)],
        compiler_params=_params(("parallel", "arbitrary")),
    )(h, h, cw, cw, cb, cb)


def _conv_bwd(h, dact, cw, cb, tl):
    l = h.shape[0]
    nr = l // tl
    tpb = tl // SUBLANES

    def body(hg_ref, hv_ref, hgh_ref, hvh_ref, da_ref, wg_ref, wv_ref, bg_ref, bv_ref,
             dh_ref, dwg_ref, dwv_ref, dbg_ref, dbv_ref, car_ref):
        r = pl.program_id(1)

        @pl.when(r == 0)
        def _():
            for ref in (car_ref, dwg_ref, dwv_ref, dbg_ref, dbv_ref):
                ref[...] = jnp.zeros_like(ref)

        keep = jnp.where(r == nr - 1, 0.0, 1.0)
        wg, wv, bg, bv = wg_ref[...], wv_ref[...], bg_ref[...], bv_ref[...]
        nch = tl // CONV_ROWS

        def fold(x):
            s = x[0:SUBLANES, :]
            for k in range(1, CONV_ROWS // SUBLANES):
                s = s + x[k * SUBLANES:(k + 1) * SUBLANES, :]
            return s

        def chunk(n, carry):
            ncg, ncv, acc = carry
            idx = nch - 1 - n
            r0 = pl.multiple_of(idx * CONV_ROWS, CONV_ROWS)
            rows = pl.ds(r0, CONV_ROWS)
            before = pl.ds(pl.multiple_of(jnp.maximum(r0 - SUBLANES, 0), SUBLANES), SUBLANES)
            in_tile = idx > 0
            da = da_ref[rows, :].astype(F32)

            def half(h_ref, halo_ref, w, b):
                hh = h_ref[rows, :]
                prev = jnp.where(in_tile, h_ref[before, :], halo_ref[...] * keep)
                h1 = _shift_down(hh, 1, prev)
                h2 = _shift_down(hh, 2, prev)
                return hh, h1, h2, b + w[0:1, :] * h2 + w[1:2, :] * h1 + w[2:3, :] * hh

            hg, hg1, hg2, cg = half(hg_ref, hgh_ref, wg, bg)
            hv, hv1, hv2, cv = half(hv_ref, hvh_ref, wv, bv)
            dcg = da * cv * _gelu_grad(cg)
            dcv = da * _gelu(cg)

            def back(dc, hh, h1, h2, w, nxt, part):
                dh = w[2:3, :] * dc + w[1:2, :] * _shift_up(dc, 1, nxt) + w[0:1, :] * _shift_up(dc, 2, nxt)
                dh_ref[part, rows, :] = dh.astype(BF16)
                return [fold(dc * h2), fold(dc * h1), fold(dc * hh), fold(dc)]

            sums = back(dcg, hg, hg1, hg2, wg, ncg, 0) + back(dcv, hv, hv1, hv2, wv, ncv, 1)
            return dcg[0:SUBLANES, :], dcv[0:SUBLANES, :], [a + s for a, s in zip(acc, sums)]

        zero = jnp.zeros((SUBLANES, CONV_CB), F32)
        ncg, ncv, acc = lax.fori_loop(0, nch, chunk, (car_ref[0:SUBLANES, :], car_ref[SUBLANES:, :], [zero] * 8))
        car_ref[0:SUBLANES, :] = ncg
        car_ref[SUBLANES:, :] = ncv
        for half_acc, dw_ref, db_ref in ((acc[0:4], dwg_ref, dbg_ref), (acc[4:8], dwv_ref, dbv_ref)):
            for k in range(3):
                dw_ref[k:k + 1, :] += jnp.sum(half_acc[k], axis=0, keepdims=True)
            db_ref[...] += jnp.sum(half_acc[3], axis=0, keepdims=True)

    grev = lambda c, r: (nr - 1 - r, c)
    vrev = lambda c, r: (nr - 1 - r, CONV_NB + c)
    ghalo = lambda c, r: (jnp.maximum((nr - 1 - r) * tpb - 1, 0), c)
    vhalo = lambda c, r: (jnp.maximum((nr - 1 - r) * tpb - 1, 0), CONV_NB + c)
    colg = lambda c, r: (0, c)
    colv = lambda c, r: (0, CONV_NB + c)
    return pl.pallas_call(
        body, name="conv_bwd", grid=(CONV_NB, nr),
        in_specs=[pl.BlockSpec((tl, CONV_CB), grev), pl.BlockSpec((tl, CONV_CB), vrev),
                  pl.BlockSpec((SUBLANES, CONV_CB), ghalo), pl.BlockSpec((SUBLANES, CONV_CB), vhalo),
                  pl.BlockSpec((tl, CONV_CB), grev),
                  pl.BlockSpec((3, CONV_CB), colg), pl.BlockSpec((3, CONV_CB), colv),
                  pl.BlockSpec((1, CONV_CB), colg), pl.BlockSpec((1, CONV_CB), colv)],
        out_specs=[pl.BlockSpec((2, tl, CONV_CB), lambda c, r: (0, nr - 1 - r, c)),
                   pl.BlockSpec((3, CONV_CB), colg), pl.BlockSpec((3, CONV_CB), colg),
                   pl.BlockSpec((1, CONV_CB), colg), pl.BlockSpec((1, CONV_CB), colg)],
        out_shape=[jax.ShapeDtypeStruct((2, l, D_FF), BF16),
                   jax.ShapeDtypeStruct((3, D_FF), F32), jax.ShapeDtypeStruct((3, D_FF), F32),
                   jax.ShapeDtypeStruct((1, D_FF), F32), jax.ShapeDtypeStruct((1, D_FF), F32)],
        scratch_shapes=[pltpu.VMEM((2 * SUBLANES, CONV_CB), F32)],
        compiler_params=_params(("parallel", "arbitrary")),
    )(h, h, h, h, dact, cw, cw, cb, cb)


def _loss_head(ff, x1, tgt, g, tl):
    l = ff.shape[0]

    def body(ff_ref, x1_ref, tg_ref, g_ref, loss_ref, dx2_ref, dff_ref, dg_ref):
        @pl.when(pl.program_id(0) == 0)
        def _():
            loss_ref[...] = jnp.zeros_like(loss_ref)
            dg_ref[...] = jnp.zeros_like(dg_ref)

        f = ff_ref[...]
        gv = g_ref[...]
        n, _ = _rms(f, gv)
        e = x1_ref[...] + n - tg_ref[...]
        loss_ref[...] += 0.5 * jnp.sum(jnp.mean(e * e, axis=-1, keepdims=True), axis=0, keepdims=True)
        dx2 = e * (1.0 / D_MODEL)
        dx2_ref[...] = dx2
        dff, dg = _rms_bwd(dx2, f, gv)
        dff_ref[...] = dff.astype(BF16)
        dg_ref[...] += dg

    return pl.pallas_call(
        body, name="loss_head", grid=(l // tl,),
        in_specs=[_row(tl, D_MODEL), _row(tl, D_MODEL), _row(tl, D_MODEL), _const((1, D_MODEL))],
        out_specs=[_const((1, LANES)), _row(tl, D_MODEL), _row(tl, D_MODEL), _const((1, D_MODEL))],
        out_shape=[jax.ShapeDtypeStruct((1, LANES), F32), jax.ShapeDtypeStruct((l, D_MODEL), F32),
                   jax.ShapeDtypeStruct((l, D_MODEL), BF16), jax.ShapeDtypeStruct((1, D_MODEL), F32)],
        compiler_params=_params(("arbitrary",)),
    )(ff, x1, tgt, g)


def _ssm_disc(lam_re, lam_im, log_dt, b_re, b_im):
    dt = jnp.exp(log_dt)[:, None]
    mag = jnp.exp(lam_re * dt)
    ang = lam_im * dt
    a_re, a_im = mag * jnp.cos(ang), mag * jnp.sin(ang)
    den = lam_re * lam_re + lam_im * lam_im
    n_re, n_im = a_re - 1.0, a_im
    z_re = (n_re * lam_re + n_im * lam_im) / den
    z_im = (n_im * lam_re - n_re * lam_im) / den
    bb_re = z_re[..., None] * b_re - z_im[..., None] * b_im
    bb_im = z_re[..., None] * b_im + z_im[..., None] * b_re
    return a_re, a_im, bb_re, bb_im


_GPB = SSM_CB // SSM_P


def _embed_b(bb):
    t = bb.transpose(0, 2, 1).reshape(SSM_NB, _GPB, SSM_H, SSM_P)
    return jnp.einsum('mjhp,jk->mjhkp', t, jnp.eye(_GPB, dtype=bb.dtype)).reshape(SSM_NB, SSM_UB, SSM_CB)


def _extract_b(d):
    t = d.reshape(SSM_NB, _GPB, SSM_H, _GPB, SSM_P)
    t = jnp.einsum('mjhkp,jk->mjhp', t, jnp.eye(_GPB, dtype=d.dtype))
    return t.reshape(SSM_G, SSM_H, SSM_P).transpose(0, 2, 1)


def _embed_c(c):
    t = c.transpose(0, 2, 1).reshape(SSM_NB, _GPB, SSM_P, SSM_H)
    return jnp.einsum('mjph,jk->mjpkh', t, jnp.eye(_GPB, dtype=c.dtype)).reshape(SSM_NB, SSM_CB, SSM_UB)


def _extract_c(d):
    t = d.reshape(SSM_NB, _GPB, SSM_P, _GPB, SSM_H)
    t = jnp.einsum('mjpkh,jk->mjph', t, jnp.eye(_GPB, dtype=d.dtype))
    return t.reshape(SSM_G, SSM_P, SSM_H).transpose(0, 2, 1)


def _scan_tables(a_re, a_im, reverse):
    ar = a_re.reshape(1, SSM_CH)
    ai = (-a_im if reverse else a_im).reshape(1, SSM_CH)
    pr, pi = [ar], [ai]
    for _ in range(SUBLANES - 1):
        pr, pi = pr + [pr[-1] * ar - pi[-1] * ai], pi + [pr[-1] * ai + pi[-1] * ar]
    rows = jnp.arange(SUBLANES)[:, None]
    out = []
    for k in (1, 2, 4):
        valid = (rows + k <= SUBLANES - 1) if reverse else (rows >= k)
        out += [jnp.where(valid, pr[k - 1], 0.0), jnp.where(valid, pi[k - 1], 0.0)]
    order = list(range(SUBLANES - 1, -1, -1)) if reverse else list(range(SUBLANES))
    out += [jnp.concatenate([pr[n] for n in order], axis=0), jnp.concatenate([pi[n] for n in order], axis=0)]
    return jnp.stack(out).astype(F32)


def _pad_heads(w, d):
    lead = w.shape[:-1]
    w = w.reshape(lead + (N_HEADS, d))
    w = jnp.pad(w, [(0, 0)] * len(lead) + [(0, 0), (0, HEAD_SLOT - d)])
    return w.reshape(lead + (HP,))


def _unpad_heads(w, d):
    lead = w.shape[:-1]
    return w.reshape(lead + (N_HEADS, HEAD_SLOT))[..., :d].reshape(lead + (N_HEADS * d,))


def _chip_major(w, axis):
    k, n = w.shape
    if axis == 0:
        return w.reshape(N_CHIPS, k // N_CHIPS, n)
    return w.reshape(k, N_CHIPS, n // N_CHIPS).transpose(1, 0, 2)


def _from_chip_major(w, axis):
    if axis == 0:
        return w.reshape(-1, w.shape[2])
    return w.transpose(1, 0, 2).reshape(w.shape[1], -1)


def _pad_w_in(w):
    z = lambda n: jnp.zeros((w.shape[0], n), w.dtype)
    return jnp.concatenate([w[:, :640], z(KR_LANE), w[:, 640:672], z(HEAD_SLOT - KR_LANE - QK_ROPE), w[:, 672:]], axis=1)


def _unpad_w_in(w):
    return jnp.concatenate([w[:, :640], w[:, P_KR + KR_LANE:P_KR + KR_LANE + QK_ROPE], w[:, P_U:]], axis=1)


def _local_step(x, positions, tgt, wts, sp):
    l = x.shape[0]
    tl = min(512, l)
    tb = min(256, l)
    ta = min(512, l)
    ts = min(2048, l)

    inv_freq = ROPE_THETA ** (-jnp.arange(0, QK_ROPE, 2, dtype=F32) / QK_ROPE)
    ang = positions.astype(F32)[:, None] * inv_freq
    cos, sin = jnp.cos(ang), jnp.sin(ang)
    one = jnp.ones((l, KR_LANE), F32)
    rc = jnp.concatenate([one, cos, cos, jnp.ones((l, HEAD_SLOT - KR_LANE - QK_ROPE), F32)], axis=1)
    rs = jnp.concatenate([0 * one, -sin, sin, jnp.zeros((l, HEAD_SLOT - KR_LANE - QK_ROPE), F32)], axis=1)

    win = _pad_w_in(wts["w_in"])
    wuq = _pad_heads(wts["w_uq"], QK_HEAD)
    wukv = jnp.concatenate([_pad_heads(wts["w_uk"], QK_NOPE), _pad_heads(wts["w_uv"], V_HEAD)], axis=1)

    disc_in = (sp["ssm_lambda_re"], sp["ssm_lambda_im"], sp["ssm_log_dt"], sp["ssm_b_re"], sp["ssm_b_im"])
    (a_re, a_im, bb_re, bb_im), disc_vjp = jax.vjp(_ssm_disc, *disc_in)
    bre, bim = _embed_b(bb_re).astype(BF16), _embed_b(bb_im).astype(BF16)
    cre, cim = _embed_c(sp["ssm_c_re"]).astype(BF16), _embed_c(sp["ssm_c_im"]).astype(BF16)
    dvec = sp["ssm_d"].reshape(1, SSM_W)
    tab_f = _scan_tables(a_re, a_im, False)
    tab_r = _scan_tables(a_re, a_im, True)

    g1, gq, gkv = sp["mix_norm_pre"], sp["q_norm"], sp["kv_norm"]
    gpost, gpre, gfin = sp["mix_norm_post"], sp["ffn_norm_pre"], sp["ffn_norm_post"]
    bgate, bglu, convb = sp["b_gate"], sp["b_glu"], sp["conv_b"]

    hn, cq, ckv, q, k, v, u, gl = _proj_fwd(x, g1, win, gq, wuq, gkv, wukv, rc, rs, bgate, tl)
    attn, lse = _attn_fwd(q, k, v, ta, FWD_HEADS)
    y1, sre, sim = _ssm_fwd(u, bre, bim, cre, cim, dvec, tab_f, ts)
    wba = jnp.pad(wts["w_branch_attn"].reshape(N_HEADS, V_HEAD, D_MODEL),
                  ((0, 0), (0, HEAD_SLOT - V_HEAD), (0, 0))).reshape(HP, D_MODEL)
    wbs, wglu, wout = wts["w_branch_ssm"], wts["w_glu"], wts["w_out"]
    a, sm, merged, z, x1, hn2, y3 = _merge_fwd(x, gl, attn, y1, wba, wbs, wglu, bglu, wout, gpost, gpre, tl)
    late = wts["late"](x1)
    wup, wdown, convw = late["w_up"], late["w_down"], late["conv_w"]
    h = _mm(hn2, wup, "ffn_up")
    act = _conv_fwd(h, convw, convb, ts)
    ff = _mm(act, wdown, "ffn_down")
    loss, dx2, dff, dgfin = _loss_head(ff, x1, tgt, gfin, ta)

    dact = _mm(dff, wdown, "ffn_down_dx", out_dtype=BF16, bt=True)
    d_wdown = _mm_tn(act, dff, "ffn_down_dw", tk_cap=D_FF // 2)
    dh, dwg, dwv, dbg, dbv = _conv_bwd(h, dact, convw, convb, ts)
    d_convw = jnp.concatenate([dwg, dwv], axis=1)
    d_convb = jnp.concatenate([dbg, dbv], axis=1)
    dhn2 = _mm(dh, wup, "ffn_up_dx", bt=True)
    d_wup = _mm_tn(hn2, dh, "ffn_up_dw", chips=True)
    behind = wts["send_grads"]("ffn", {"w_up": d_wup, "w_down": _chip_major(d_wdown, 0)})
    (dx1, dz, dbra, dbrs, dgl, dattn, dy1, dt, y2, dgpre, dgpost, dbgate, dbglu) = _merge_bwd(
        dhn2, x1, dx2, z, gl, a, sm, y1, wba, wbs, wglu, bglu, wout, gpost, gpre + behind, tb)
    d_wout = _mm_tn(merged, dz, "w_out_dw")
    d_wba = _mm_tn(attn, dbra, "w_branch_attn_dw", chips=True)
    d_wbs = _mm_tn(y3, dbrs, "w_branch_ssm_dw", chips=True)
    d_wglu = _mm_tn(y2, dt, "w_glu_dw")
    ncol = D_MODEL // N_CHIPS
    behind = wts["send_grads"]("mix", {
        "w_glu": _chip_major(d_wglu, 0),
        "w_branch_attn": d_wba.reshape(N_CHIPS, N_HEADS, HEAD_SLOT, ncol)[:, :, :V_HEAD].reshape(
            N_CHIPS, N_HEADS * V_HEAD, ncol),
        "w_branch_ssm": d_wbs,
        "w_out": _chip_major(d_wout, 0)}, after=d_wglu)
    dq, dk, dv = _attn_bwd(q, k, v, dattn, lse + behind, _attn_delta(attn, dattn, min(2048, l), BWD_HEADS), ta,
                            BWD_HEADS)
    du, dbre, dbim, dcre, dcim, dare, daim, dd = _ssm_bwd(dy1, u, sre, sim, bre, bim, cre, cim, dvec, tab_r, ts)
    behind = wts["send_grads"]("none", {}, after=du)
    gx, dql, qn, ckvn, dproj, dg1, dgq, dgkv = _proj_bwd(
        x, dx1, cq, ckv, dq, dk, dv, du, dgl, g1 + behind, win, gq, wuq, gkv, wukv, rc, rs, tl)
    d_win = _mm_tn(hn, dproj, "w_in_dw")
    d_wuq = _mm_tn(qn, dql, "w_uq_dw")
    d_wuk = _mm_tn(ckvn, dk, "w_uk_dw")
    d_wuv = _mm_tn(ckvn, dv, "w_uv_dw")

    d_lre, d_lim, d_ldt, d_bre, d_bim = disc_vjp((dare.reshape(SSM_G, SSM_P), daim.reshape(SSM_G, SSM_P),
                                                  _extract_b(dbre), _extract_b(dbim)))
    big = {
        "w_in": _chip_major(_unpad_w_in(d_win), 1),
        "w_uq": _chip_major(_unpad_heads(d_wuq, QK_HEAD), 1),
        "w_uk": _chip_major(_unpad_heads(d_wuk, QK_NOPE), 1),
        "w_uv": _chip_major(_unpad_heads(d_wuv, V_HEAD), 1),
    }
    small = {
        "conv_w": d_convw,
        "mix_norm_pre": dg1, "q_norm": dgq, "kv_norm": dgkv,
        "ssm_lambda_re": d_lre, "ssm_lambda_im": d_lim, "ssm_log_dt": d_ldt,
        "ssm_b_re": d_bre, "ssm_b_im": d_bim,
        "ssm_c_re": _extract_c(dcre), "ssm_c_im": _extract_c(dcim),
        "ssm_d": dd.reshape(SSM_G, SSM_H), "b_glu": dbglu, "b_gate": dbgate,
        "mix_norm_post": dgpost, "ffn_norm_pre": dgpre, "conv_b": d_convb, "ffn_norm_post": dgfin,
    }
    return loss[0, 0], gx, big, small


_ANY = pl.BlockSpec(memory_space=pl.ANY)


ROW_TILE = 16


def _place():
    x, y, c = lax.axis_index("x"), lax.axis_index("y"), lax.axis_index("c")
    return x, y, c, 2 * x + y, [(1 - x, y), (x, 1 - y), (1 - x, 1 - y)]


def _half(rows, which):
    hr = rows // 2
    return pl.ds(pl.multiple_of(which * hr, ROW_TILE), hr)


def _remote(src, dst, send_sems, recv_sems, n, dev):
    return pltpu.make_async_remote_copy(src_ref=src, dst_ref=dst, send_sem=send_sems.at[n], recv_sem=recv_sems.at[n],
                                        device_id=dev, device_id_type=MESH)


def _gather_big(shards):
    nw = len(shards)
    rows = [s.shape[0] for s in shards]

    def body(*refs):
        ins, outs = refs[:nw], refs[nw:2 * nw]
        ici_send, ici_recv, d2d_send, d2d_recv = refs[2 * nw:]
        x, y, c, me, peers = _place()
        sent = []
        for i in range(nw):
            for p, (px, py) in enumerate(peers):
                cp = _remote(ins[i].at[_half(rows[i], c)], outs[i].at[me, _half(rows[i], c)], ici_send, ici_recv,
                             3 * i + p, (px, py, c))
                cp.start()
                sent.append(cp)
        for p, (px, py) in enumerate(peers):
            for i in range(nw):
                blk = outs[i].at[2 * px + py, _half(rows[i], c)]
                _remote(blk, blk, ici_send, ici_recv, 3 * i + p, (px, py, c)).wait_recv()
                cp = _remote(blk, blk, d2d_send, d2d_recv, 3 * i + p, (x, y, 1 - c))
                cp.start()
                sent.append(cp)
        for p, (px, py) in enumerate(peers):
            for i in range(nw):
                blk = outs[i].at[2 * px + py, _half(rows[i], 1 - c)]
                _remote(blk, blk, d2d_send, d2d_recv, 3 * i + p, (x, y, 1 - c)).wait_recv()
        for cp in sent:
            cp.wait_send()

    dma = pltpu.SemaphoreType.DMA
    return pl.pallas_call(
        body, name="gather_weights", in_specs=[_ANY] * nw, out_specs=[_ANY] * nw,
        out_shape=[jax.ShapeDtypeStruct((N_CHIPS,) + s.shape, s.dtype) for s in shards],
        scratch_shapes=[dma((3 * nw,)), dma((3 * nw,)), dma((3 * nw,)), dma((3 * nw,))],
    )(*shards)


_HBM = pl.BlockSpec(memory_space=pltpu.HBM)
_SEM = pl.BlockSpec(memory_space=pltpu.SEMAPHORE)
_DATAFLOW = pltpu.SideEffectType.DATAFLOW_SIDE_EFFECTING


def _exchange_start(shards, name, scatter):
    nw = len(shards)
    lands = [lax.empty(s.shape if scatter else (N_CHIPS,) + s.shape, s.dtype) for s in shards]

    def body(*refs):
        ins, zones = refs[:nw], refs[nw:2 * nw]
        send_sems, recv_sems, token = refs[2 * nw], refs[2 * nw + 1], refs[-1]
        x, y, c, me, peers = _place()
        for i in range(nw):
            for p, (px, py) in enumerate(peers):
                src = ins[i].at[2 * px + py] if scatter else ins[i]
                _remote(src, zones[i].at[me], send_sems, recv_sems, 3 * i + p, (px, py, c)).start()
        token[...] = jnp.zeros_like(token)

    thru = [pltpu.HBM(a.shape, a.dtype) for a in list(shards) + lands]
    dma = pltpu.SemaphoreType.DMA
    outs = pl.pallas_call(
        body, name=name,
        out_shape=(dma((3 * nw,)), dma((3 * nw,)), *thru, jax.ShapeDtypeStruct((SUBLANES, LANES), F32)),
        in_specs=[_HBM] * (2 * nw),
        out_specs=(_SEM, _SEM, *([_HBM] * (2 * nw)), pl.BlockSpec(memory_space=pltpu.VMEM)),
        input_output_aliases={i: 2 + i for i in range(2 * nw)},
        compiler_params=pltpu.CompilerParams(has_side_effects=_DATAFLOW),
    )(*[pltpu.with_memory_space_constraint(a, pltpu.HBM) for a in list(shards) + lands])
    return outs[0], outs[1], list(outs[2:2 + nw]), list(outs[2 + nw:2 + 2 * nw]), outs[-1]


def _exchange_wait(send_sems, recv_sems, shards, lands, after, name, scatter):
    nw = len(shards)

    def body(*refs):
        ins, zones = refs[:nw], refs[nw:2 * nw]
        send_sems, recv_sems = refs[2 * nw], refs[2 * nw + 1]
        x, y, c, me, peers = _place()
        for i in range(nw):
            for p, (px, py) in enumerate(peers):
                src = ins[i].at[2 * px + py] if scatter else ins[i]
                cp = _remote(src, zones[i].at[2 * px + py], send_sems, recv_sems, 3 * i + p, (px, py, c))
                cp.wait_send()
                cp.wait_recv()

    both = list(shards) + list(lands)
    outs = pl.pallas_call(
        body, name=name,
        out_shape=tuple(pltpu.HBM(a.shape, a.dtype) for a in both),
        in_specs=(*([_HBM] * (2 * nw)), _SEM, _SEM, _ANY), out_specs=[_HBM] * (2 * nw),
        input_output_aliases={i: i for i in range(2 * nw)},
        compiler_params=pltpu.CompilerParams(has_side_effects=_DATAFLOW),
    )(*both, send_sems, recv_sems, after)
    return list(outs[:nw]), list(outs[nw:])


def _sibling_start(grads, name):
    nw = len(grads)
    lands = [lax.empty((N_CHIPS, g.shape[1] // 2, g.shape[2]), g.dtype) for g in grads]

    def body(*refs):
        ins, zones = refs[:nw], refs[nw:2 * nw]
        send_sems, recv_sems, token = refs[2 * nw], refs[2 * nw + 1], refs[-1]
        x, y, c, _, _ = _place()
        for i in range(nw):
            _remote(ins[i].at[pl.ds(0, N_CHIPS), _half(grads[i].shape[1], 1 - c)], zones[i], send_sems, recv_sems,
                    i, (x, y, 1 - c)).start()
        token[...] = jnp.zeros_like(token)

    thru = [pltpu.HBM(a.shape, a.dtype) for a in list(grads) + lands]
    dma = pltpu.SemaphoreType.DMA
    outs = pl.pallas_call(
        body, name=name,
        out_shape=(dma((nw,)), dma((nw,)), *thru, jax.ShapeDtypeStruct((SUBLANES, LANES), F32)),
        in_specs=[_HBM] * (2 * nw),
        out_specs=(_SEM, _SEM, *([_HBM] * (2 * nw)), pl.BlockSpec(memory_space=pltpu.VMEM)),
        input_output_aliases={i: 2 + i for i in range(2 * nw)},
        compiler_params=pltpu.CompilerParams(has_side_effects=_DATAFLOW),
    )(*[pltpu.with_memory_space_constraint(a, pltpu.HBM) for a in list(grads) + lands])
    return outs[0], outs[1], list(outs[2:2 + nw]), list(outs[2 + nw:2 + 2 * nw]), outs[-1]


def _sibling_wait(send_sems, recv_sems, grads, lands, after, name):
    nw = len(grads)

    def body(*refs):
        ins, zones = refs[:nw], refs[nw:2 * nw]
        send_sems, recv_sems = refs[2 * nw], refs[2 * nw + 1]
        x, y, c, _, _ = _place()
        for i in range(nw):
            cp = _remote(ins[i].at[pl.ds(0, N_CHIPS), _half(grads[i].shape[1], 1 - c)], zones[i], send_sems, recv_sems,
                         i, (x, y, 1 - c))
            cp.wait_send()
            cp.wait_recv()

    both = list(grads) + list(lands)
    outs = pl.pallas_call(
        body, name=name,
        out_shape=tuple(pltpu.HBM(a.shape, a.dtype) for a in both),
        in_specs=(*([_HBM] * (2 * nw)), _SEM, _SEM, _ANY), out_specs=[_HBM] * (2 * nw),
        input_output_aliases={i: i for i in range(2 * nw)},
        compiler_params=pltpu.CompilerParams(has_side_effects=_DATAFLOW),
    )(*both, send_sems, recv_sems, after)
    return list(outs[:nw]), list(outs[nw:])


def _reduce_to_sibling(grads, name):
    nw = len(grads)

    def body(*refs):
        ins, outs = refs[:nw], refs[nw:2 * nw]
        send_sems, recv_sems = refs[2 * nw:]
        x, y, c, _, _ = _place()
        sent = []
        for i in range(nw):
            cp = _remote(ins[i].at[pl.ds(0, N_CHIPS), _half(grads[i].shape[1], 1 - c)], outs[i], send_sems, recv_sems,
                         i, (x, y, 1 - c))
            cp.start()
            sent.append(cp)
        for cp in sent:
            cp.wait()

    dma = pltpu.SemaphoreType.DMA
    return pl.pallas_call(
        body, name=name, in_specs=[_ANY] * nw, out_specs=[_ANY] * nw,
        out_shape=[jax.ShapeDtypeStruct((N_CHIPS, g.shape[1] // 2, g.shape[2]), g.dtype) for g in grads],
        scratch_shapes=[dma((nw,)), dma((nw,))],
    )(*grads)


def _reduce_back(totals, name):
    nw = len(totals)

    def body(*refs):
        outs = refs[nw:2 * nw]
        send_sems, recv_sems = refs[2 * nw:]
        x, y, c, _, _ = _place()
        sent = []
        for i in range(nw):
            blk = outs[i].at[_half(totals[i].shape[0], c)]
            cp = _remote(blk, blk, send_sems, recv_sems, i, (x, y, 1 - c))
            cp.start()
            sent.append(cp)
        for i in range(nw):
            blk = outs[i].at[_half(totals[i].shape[0], 1 - c)]
            _remote(blk, blk, send_sems, recv_sems, i, (x, y, 1 - c)).wait_recv()
        for cp in sent:
            cp.wait_send()

    dma = pltpu.SemaphoreType.DMA
    return pl.pallas_call(
        body, name=name, in_specs=[_ANY] * nw, out_specs=[_ANY] * nw,
        out_shape=[jax.ShapeDtypeStruct(t.shape, t.dtype) for t in totals],
        input_output_aliases={i: i for i in range(nw)},
        scratch_shapes=[dma((nw,)), dma((nw,))],
    )(*totals)


def _all_reduce_small(v, name):
    rows, w = v.shape
    hr = rows // 2
    assert hr % SUBLANES == 0

    def body(v_ref, out_ref, sib_ref, half_ref, chips_ref, send_sems, recv_sems):
        x, y, c, me, peers = _place()
        sibling = (x, y, 1 - c)
        mine = pl.ds(pl.multiple_of(c * hr, SUBLANES), hr)
        other = pl.ds(pl.multiple_of((1 - c) * hr, SUBLANES), hr)
        cp = _remote(v_ref, sib_ref, send_sems, recv_sems, 0, sibling)
        cp.start()
        cp.wait()
        half_ref[...] = v_ref[mine, :] + sib_ref[mine, :]
        sent = []
        for p, (px, py) in enumerate(peers):
            cp = _remote(half_ref, chips_ref.at[me], send_sems, recv_sems, 1 + p, (px, py, c))
            cp.start()
            sent.append(cp)
        chips_ref[me] = half_ref[...]
        for p, (px, py) in enumerate(peers):
            _remote(half_ref, chips_ref.at[2 * px + py], send_sems, recv_sems, 1 + p, (px, py, c)).wait_recv()
        for cp in sent:
            cp.wait_send()
        out_ref[mine, :] = ((chips_ref[0] + chips_ref[1]) + chips_ref[2]) + chips_ref[3]
        cp = _remote(out_ref.at[mine], out_ref.at[mine], send_sems, recv_sems, 4, sibling)
        cp.start()
        _remote(out_ref.at[other], out_ref.at[other], send_sems, recv_sems, 4, sibling).wait_recv()
        cp.wait_send()

    vm = pl.BlockSpec(memory_space=pltpu.VMEM)
    return pl.pallas_call(
        body, name=name, in_specs=[vm], out_specs=vm,
        out_shape=jax.ShapeDtypeStruct((rows, w), F32),
        scratch_shapes=[pltpu.VMEM((rows, w), F32), pltpu.VMEM((hr, w), F32), pltpu.VMEM((N_CHIPS, hr, w), F32),
                        pltpu.SemaphoreType.DMA((5,)), pltpu.SemaphoreType.DMA((5,))],
        compiler_params=pltpu.CompilerParams(vmem_limit_bytes=VMEM_LIMIT),
    )(v)


ELEMENTWISE_BLOCK = 512 * 1024


def _rows_tile(rows, cols, cap=ELEMENTWISE_BLOCK):
    best = None
    for t in range(SUBLANES, rows + 1, SUBLANES):
        if rows % t == 0 and t * cols <= cap:
            best = t
    return rows if best is None else best


def _add_pair(g, t, core, name):
    nb, n, w = t.shape
    tr = _rows_tile(n, w, 2 * ELEMENTWISE_BLOCK)
    steps = n // tr

    def body(core_ref, g_ref, t_ref, o_ref):
        o_ref[...] = (g_ref[...] + t_ref[...]).astype(BF16)

    spec = pl.BlockSpec((1, tr, w), lambda j, i, core_ref: (j, i, 0))
    return pl.pallas_call(
        body, name=name,
        grid_spec=pltpu.PrefetchScalarGridSpec(
            num_scalar_prefetch=1, grid=(nb, steps),
            in_specs=[pl.BlockSpec((1, tr, w), lambda j, i, core_ref: (j, core_ref[0] * steps + i, 0)), spec],
            out_specs=spec),
        out_shape=jax.ShapeDtypeStruct(t.shape, BF16),
        compiler_params=_params(("parallel", "parallel")))(core, g, t)


def _add_chips(landed, pairs, place, name):
    nb, n, w = landed.shape
    tr = _rows_tile(n, w)
    steps = n // tr

    def body(place_ref, r_ref, own_ref, o_ref):
        me = place_ref[0]
        acc = None
        for k in range(nb):
            blk = jnp.where(me == k, own_ref[0], r_ref[k]).astype(F32)
            acc = blk if acc is None else acc + blk
        o_ref[...] = acc

    return pl.pallas_call(
        body, name=name,
        grid_spec=pltpu.PrefetchScalarGridSpec(
            num_scalar_prefetch=1, grid=(steps,),
            in_specs=[pl.BlockSpec((nb, tr, w), lambda i, place_ref: (0, i, 0)),
                      pl.BlockSpec((1, tr, w), lambda i, place_ref: (place_ref[0], i, 0))],
            out_specs=pl.BlockSpec((tr, w), lambda i, place_ref: (place_ref[1] * steps + i, 0))),
        out_shape=jax.ShapeDtypeStruct((2 * n, w), F32),
        compiler_params=_params(("parallel",)))(place, landed, pairs)


def _adamw(w, g, m, v, name):
    rows, wd = w.shape
    tr = _rows_tile(rows, wd)
    c1 = 1.0 - ADAM_B1 ** ADAM_STEP
    c2 = 1.0 - ADAM_B2 ** ADAM_STEP

    def body(w_ref, g_ref, m_ref, v_ref, go_ref, d_ref, mo_ref, vo_ref):
        gv = g_ref[...]
        go_ref[...] = gv
        m2 = ADAM_B1 * m_ref[...] + (1.0 - ADAM_B1) * gv
        v2 = ADAM_B2 * v_ref[...] + (1.0 - ADAM_B2) * (gv * gv)
        mo_ref[...] = m2
        vo_ref[...] = v2
        d_ref[...] = -ADAM_LR * ((m2 / c1) / (jnp.sqrt(v2 / c2) + ADAM_EPS) + ADAM_WD * w_ref[...])

    spec = pl.BlockSpec((tr, wd), lambda i: (i, 0))
    shp = jax.ShapeDtypeStruct((rows, wd), F32)
    return pl.pallas_call(body, name=name, grid=(rows // tr,), in_specs=[spec] * 4, out_specs=[spec] * 4,
                          out_shape=[shp] * 4, compiler_params=_params(("parallel",)))(w, g, m, v)


BIG = [("w_in", (1024, 3232), 1), ("w_uq", (384, 768), 1), ("w_uk", (256, 512), 1), ("w_uv", (256, 512), 1),
       ("w_glu", (512, 512), 0), ("w_branch_attn", (512, 1024), 1), ("w_branch_ssm", (512, 1024), 1),
       ("w_out", (1024, 1024), 0), ("w_up", (1024, 5632), 1), ("conv_w", (3, 5632), 1), ("w_down", (2816, 1024), 0)]
SMALL = [("mix_norm_pre", (1024,)), ("q_norm", (384,)), ("kv_norm", (256,)), ("ssm_lambda_re", (32, 64)),
         ("ssm_lambda_im", (32, 64)), ("ssm_log_dt", (32,)), ("ssm_b_re", (32, 64, 16)), ("ssm_b_im", (32, 64, 16)),
         ("ssm_c_re", (32, 16, 64)), ("ssm_c_im", (32, 16, 64)), ("ssm_d", (32, 16)), ("b_glu", (512,)),
         ("b_gate", (2048,)), ("mix_norm_post", (1024,)), ("ffn_norm_pre", (1024,)), ("conv_b", (5632,)),
         ("ffn_norm_post", (1024,))]
MATMUL_W = [b for b in BIG if b[0] != "conv_w"]
LATE_W = ("w_up", "w_down", "conv_w")
CONV_W_SHAPE = (3, 2 * D_FF)
CONV_W_SHARD = (3, 2 * D_FF // N_CHIPS)
SMALL_SUM = [("loss", (1,))] + SMALL + [("conv_w", CONV_W_SHAPE)]
SMALL_ADAM = SMALL + [("conv_w", CONV_W_SHARD)]


def _pack_flat(layout, vals):
    flat = jnp.concatenate([vals[n].astype(F32).reshape(-1) for n, _ in layout])
    rows = -(-(-(-flat.shape[0] // FLAT_W)) // (2 * SUBLANES)) * 2 * SUBLANES
    return jnp.pad(flat, (0, rows * FLAT_W - flat.shape[0])).reshape(rows, FLAT_W)


def _unpack_flat(layout, flat):
    flat = flat.reshape(-1)
    out = {}
    o = 0
    for name, shape in layout:
        n = math.prod(shape)
        out[name] = flat[o:o + n].reshape(shape)
        o += n
    return out


_ARG_NAMES = ["x", "positions"] + [n for n in (
    "mix_norm_pre", "w_in", "q_norm", "w_uq", "kv_norm", "w_uk", "w_uv", "ssm_lambda_re", "ssm_lambda_im", "ssm_log_dt",
    "ssm_b_re", "ssm_b_im", "ssm_c_re", "ssm_c_im", "ssm_d", "w_glu", "b_glu", "w_branch_attn", "w_branch_ssm",
    "b_gate", "w_out", "mix_norm_post", "ffn_norm_pre", "w_up", "conv_w", "conv_b", "w_down", "ffn_norm_post")]
_WEIGHTS = _ARG_NAMES[2:]


def _gather_weights(w):
    early = [b for b in MATMUL_W if b[0] not in LATE_W]
    late = [b for b in BIG if b[0] in LATE_W]
    own = (jnp.arange(N_CHIPS) == 2 * lax.axis_index("x") + lax.axis_index("y"))[:, None, None]

    def whole(layout, mine, gathered):
        return {name: _from_chip_major(jnp.where(own, s[None], g), axis)
                for (name, _, axis), s, g in zip(layout, mine, gathered)}

    mine = [w[name].astype(BF16) for name, _, _ in early]
    gathered = _gather_big(mine)
    full = whole(early, mine, gathered)
    mine_late = [w[name].astype(F32 if name == "conv_w" else BF16) for name, _, _ in late]
    _, mine_late = lax.optimization_barrier((gathered[0], mine_late))
    send_sems, recv_sems, shards_thru, lands_thru, token = _exchange_start(mine_late, "gather_late_start", scatter=False)

    def late_weights(after):
        shards, lands = _exchange_wait(send_sems, recv_sems, shards_thru, lands_thru, after, "gather_late_wait",
                                       scatter=False)
        return whole(late, shards, lands)

    full["late"] = late_weights
    full["token"] = token[0, 0]
    return full


def _pair_sums(names, grads, tag):
    core = lax.axis_index("c").astype(jnp.int32).reshape(1)
    theirs = _reduce_to_sibling(grads, "reduce_grads_d2d" + tag)
    return [_add_pair(g, t, core, "reduce_pair_" + n) for n, g, t in zip(names, grads, theirs)]


def _send_grads(tag, grads, after, flying, pending):
    token = jnp.zeros((), F32)
    if flying:
        tag0, names0, state0 = flying.pop()
        core = lax.axis_index("c").astype(jnp.int32).reshape(1)
        mine, theirs = _sibling_wait(*state0, after, "reduce_" + tag0 + "_d2d_wait")
        pairs = [_add_pair(g, t, core, "reduce_pair_" + n) for n, g, t in zip(names0, mine, theirs)]
        send_sems, recv_sems, pairs_thru, lands_thru, tok = _exchange_start(pairs, "reduce_" + tag0 + "_start",
                                                                          scatter=True)
        pending.append((tag0, names0, send_sems, recv_sems, pairs_thru, lands_thru))
        token = token + tok[0, 0]
    if grads:
        names = list(grads)
        send_sems, recv_sems, grads_thru, lands_thru, tok = _sibling_start([grads[n] for n in names],
                                                                          "reduce_" + tag + "_d2d_start")
        flying.append((tag, names, (send_sems, recv_sems, grads_thru, lands_thru)))
        token = token + tok[0, 0]
    return token


def _reduce_grads(gbig, pending, loss, gsmall, use_sent):
    core = lax.axis_index("c").astype(jnp.int32).reshape(1)
    chip = (2 * lax.axis_index("x") + lax.axis_index("y")).astype(jnp.int32).reshape(1)
    place = jnp.concatenate([chip, core])

    def finish(names, pairs, landed, name):
        totals = [_add_chips(r, p, place, "reduce_chips_" + n) for n, r, p in zip(names, landed, pairs)]
        return dict(zip(names, _reduce_back(totals, name)))

    names = list(gbig)
    pairs = _pair_sums(names, [gbig[n] for n in names], "")
    send_sems, recv_sems, pairs_thru, lands_thru, token = _exchange_start(pairs, "reduce_last_start", scatter=True)
    sent_names, sent_pairs, sent_landed = [], [], []
    for tag, group, g_send, g_recv, g_pairs, g_lands in pending:
        got_pairs, got_landed = _exchange_wait(g_send, g_recv, g_pairs, g_lands, token, "reduce_" + tag + "_wait",
                                               scatter=True)
        sent_names, sent_pairs, sent_landed = sent_names + group, sent_pairs + got_pairs, sent_landed + got_landed
    g_sent = finish(sent_names, sent_pairs, sent_landed, "reduce_back_sent")
    vals = dict(gsmall)
    vals["loss"] = loss
    small_red = _unpack_flat(SMALL_SUM, _all_reduce_small(_pack_flat(SMALL_SUM, vals), "reduce_small"))
    after = use_sent(g_sent, small_red)
    pairs, landed = _exchange_wait(send_sems, recv_sems, pairs_thru, lands_thru, after, "reduce_last_wait", scatter=True)
    return finish(names, pairs, landed, "reduce_back_last"), small_red


def _step(args):
    x = args["x"][0]
    positions = args["positions"][0]
    tgt = args["loss_target"][0]
    w = {n: args[n][0] for n in _WEIGHTS}
    m = {n: args["m_" + n][0] for n in _WEIGHTS}
    v = {n: args["v_" + n][0] for n in _WEIGHTS}

    full = _gather_weights(w)
    sp = {n: w[n].reshape(s) for n, s in SMALL}
    for n in ("mix_norm_pre", "q_norm", "kv_norm", "b_glu", "b_gate", "mix_norm_post", "ffn_norm_pre", "conv_b",
              "ffn_norm_post"):
        sp[n] = sp[n].reshape(1, -1)
    sp["mix_norm_pre"] = sp["mix_norm_pre"] + full.pop("token")
    pending, flying = [], []
    full["send_grads"] = lambda tag, grads, after=None: _send_grads(tag, grads, after, flying, pending)
    loss, gx, gbig, gsmall = _local_step(x, positions, tgt, full, sp)
    outs = {}

    def adam_big(g_red):
        for name in g_red:
            g2, d, m2, v2 = _adamw(w[name], g_red[name], m[name], v[name], "adamw_" + name)
            outs["grad_" + name], outs["delta_" + name], outs["new_m_" + name], outs["new_v_" + name] = g2, d, m2, v2
        return v2

    def use_sent(g_sent, small_red):
        chip = 2 * lax.axis_index("x") + lax.axis_index("y")
        grads = dict(small_red)
        grads["conv_w"] = lax.dynamic_slice_in_dim(small_red["conv_w"], chip * CONV_W_SHARD[1], CONV_W_SHARD[1], axis=1)
        outs.update({"grad_" + n: grads[n] for n, _ in SMALL_ADAM})
        _, d_sm, m_sm, v_sm = _adamw(_pack_flat(SMALL_ADAM, w), _pack_flat(SMALL_ADAM, grads),
                                     _pack_flat(SMALL_ADAM, m), _pack_flat(SMALL_ADAM, v), "adamw_small")
        for prefix, flat in (("delta_", d_sm), ("new_m_", m_sm), ("new_v_", v_sm)):
            for n, val in _unpack_flat(SMALL_ADAM, flat).items():
                outs[prefix + n] = val
        return adam_big(g_sent)

    g_last, small_red = _reduce_grads(gbig, pending, loss, gsmall, use_sent)
    adam_big(g_last)
    outs = {n: val.reshape(args[n.split("_", 1)[1] if not n.startswith("new_") else n[6:]].shape)
            for n, val in outs.items()}
    res = [small_red["loss"][0], gx[None]]
    for prefix in ("grad_", "delta_", "new_m_", "new_v_"):
        res += [outs[prefix + n] for n in _WEIGHTS]
    return tuple(res)


def kernel(x, positions, mix_norm_pre, w_in, q_norm, w_uq, kv_norm, w_uk, w_uv, ssm_lambda_re, ssm_lambda_im, ssm_log_dt, ssm_b_re, ssm_b_im, ssm_c_re, ssm_c_im, ssm_d, w_glu, b_glu, w_branch_attn, w_branch_ssm, b_gate, w_out, mix_norm_post, ffn_norm_pre, w_up, conv_w, conv_b, w_down, ffn_norm_post, loss_target, m_mix_norm_pre, m_w_in, m_q_norm, m_w_uq, m_kv_norm, m_w_uk, m_w_uv, m_ssm_lambda_re, m_ssm_lambda_im, m_ssm_log_dt, m_ssm_b_re, m_ssm_b_im, m_ssm_c_re, m_ssm_c_im, m_ssm_d, m_w_glu, m_b_glu, m_w_branch_attn, m_w_branch_ssm, m_b_gate, m_w_out, m_mix_norm_post, m_ffn_norm_pre, m_w_up, m_conv_w, m_conv_b, m_w_down, m_ffn_norm_post, v_mix_norm_pre, v_w_in, v_q_norm, v_w_uq, v_kv_norm, v_w_uk, v_w_uv, v_ssm_lambda_re, v_ssm_lambda_im, v_ssm_log_dt, v_ssm_b_re, v_ssm_b_im, v_ssm_c_re, v_ssm_c_im, v_ssm_d, v_w_glu, v_b_glu, v_w_branch_attn, v_w_branch_ssm, v_b_gate, v_w_out, v_mix_norm_post, v_ffn_norm_pre, v_w_up, v_conv_w, v_conv_b, v_w_down, v_ffn_norm_post):
    given = dict(locals())
    return _step(given)
```

```python
import math

import jax
import jax.numpy as jnp
from jax import lax
from jax.experimental import pallas as pl
from jax.experimental.pallas import tpu as pltpu

F32 = jnp.float32
BF16 = jnp.bfloat16
MESH = pl.DeviceIdType.MESH

D_MODEL = 1024
N_HEADS = 8
QK_NOPE = 64
QK_ROPE = 32
QK_HEAD = QK_NOPE + QK_ROPE
V_HEAD = 64
Q_RANK = 384
KV_RANK = 256
ROPE_THETA = 10000.0
SSM_W = 512
SSM_H = 16
SSM_G = 32
SSM_P = 64
SSM_CH = SSM_G * SSM_P
D_FF = 2816
EPS = 1e-6
ADAM_LR = 0.001
ADAM_B1 = 0.9
ADAM_B2 = 0.999
ADAM_EPS = 1e-08
ADAM_WD = 0.01
ADAM_STEP = 10

LANES = 128
SUBLANES = 8
VMEM_LIMIT = 56 * 1024 * 1024

HEAD_SLOT = LANES
HP = N_HEADS * HEAD_SLOT
P_CQ, P_CKV, P_KR, P_U, P_GL, P_END = 0, 384, 640, 768, 1280, 3328
KR_LANE = 64

FLAT_W = 1024
N_CHIPS = 4


def _tile(n, cap):
    if n <= cap:
        return n
    best = None
    for t in range(LANES, cap + 1, LANES):
        if n % t == 0:
            best = t
    assert best is not None, (n, cap)
    return best


def _params(sem):
    return pltpu.CompilerParams(dimension_semantics=sem, vmem_limit_bytes=VMEM_LIMIT)


def _dot(a, b):
    return jnp.dot(a, b, preferred_element_type=F32)


def _dot_nt(a, b):
    return lax.dot_general(a, b, (((1,), (1,)), ((), ())), preferred_element_type=F32)


def _dot_tn(a, b):
    return lax.dot_general(a, b, (((0,), (0,)), ((), ())), preferred_element_type=F32)


def _rms(x, g):
    r = lax.rsqrt(jnp.mean(x * x, axis=-1, keepdims=True) + EPS)
    return x * r * g, r


def _rms_bwd(dy, x, g):
    r = lax.rsqrt(jnp.mean(x * x, axis=-1, keepdims=True) + EPS)
    dyg = dy * g
    dx = r * dyg - x * (r * r * r) * jnp.mean(dyg * x, axis=-1, keepdims=True)
    dg = jnp.sum(dy * x * r, axis=0, keepdims=True)
    return dx, dg


_GELU_K0 = math.sqrt(2.0 / math.pi)
_GELU_K1 = 0.044715


def _gelu(x):
    th = jnp.tanh(_GELU_K0 * (x + _GELU_K1 * x * x * x))
    return 0.5 * x * (1.0 + th)


def _gelu_grad(x):
    th = jnp.tanh(_GELU_K0 * (x + _GELU_K1 * x * x * x))
    return 0.5 * (1.0 + th) + 0.5 * x * (1.0 - th * th) * _GELU_K0 * (1.0 + 3.0 * _GELU_K1 * x * x)


def _sigmoid(x):
    return 1.0 / (1.0 + jnp.exp(-x))


def _rope(q, c, s):
    n = q.shape[1]
    lane = lax.broadcasted_iota(jnp.int32, q.shape, 1) % HEAD_SLOT
    sw = jnp.where(lane < KR_LANE + QK_ROPE // 2, pltpu.roll(q, n - QK_ROPE // 2, 1), pltpu.roll(q, QK_ROPE // 2, 1))
    return q * c + sw * s


def _rope_bwd(dy, c, s):
    n = dy.shape[1]
    t = dy * s
    lane = lax.broadcasted_iota(jnp.int32, dy.shape, 1) % HEAD_SLOT
    sw = jnp.where(lane < KR_LANE + QK_ROPE // 2, pltpu.roll(t, n - QK_ROPE // 2, 1), pltpu.roll(t, QK_ROPE // 2, 1))
    rope_lane = jnp.logical_and(lane >= KR_LANE, lane < KR_LANE + QK_ROPE)
    return dy * c + jnp.where(rope_lane, sw, 0.0)


def _shift_down(x, k, halo):
    xs = pltpu.roll(x, k, 0)
    hs = pltpu.roll(halo, k, 0)
    rows = lax.broadcasted_iota(jnp.int32, halo.shape, 0)
    top = jnp.where(rows < k, hs, xs[0:SUBLANES])
    return jnp.concatenate([top, xs[SUBLANES:]], axis=0)


def _shift_up(x, k, halo):
    t = x.shape[0]
    xs = pltpu.roll(x, t - k, 0)
    hs = pltpu.roll(halo, SUBLANES - k, 0)
    rows = lax.broadcasted_iota(jnp.int32, halo.shape, 0)
    bot = jnp.where(rows >= SUBLANES - k, hs, xs[t - SUBLANES:])
    return jnp.concatenate([xs[:t - SUBLANES], bot], axis=0)


def _mm(a, b, name, out_dtype=F32, bt=False, tm_cap=1024, tn_cap=1408):
    m, k = a.shape[-2:]
    parts = a.shape[0] if a.ndim == 3 else 1
    n = b.shape[0] if bt else b.shape[1]
    tm = min(tm_cap, m)
    tn = _tile(n, tn_cap)

    def body(a_ref, b_ref, o_ref):
        if not bt:
            o_ref[...] = _dot(a_ref[...], b_ref[...]).astype(out_dtype)
        elif parts == 1:
            o_ref[...] = _dot_nt(a_ref[...], b_ref[...]).astype(out_dtype)
        else:
            @pl.when(pl.program_id(2) == 0)
            def _():
                o_ref[...] = jnp.zeros_like(o_ref)

            o_ref[...] += _dot_nt(a_ref[...], b_ref[...])

    if bt:
        b_spec = pl.BlockSpec((tn, k), lambda j, i, s: (j, s))
    else:
        b_spec = pl.BlockSpec((k, tn), lambda j, i, s: (0, j))
    if a.ndim == 3:
        assert bt and out_dtype == F32
        a_spec = pl.BlockSpec((None, tm, k), lambda j, i, s: (s, i, 0))
    else:
        a_spec = pl.BlockSpec((tm, k), lambda j, i, s: (i, 0))
    return pl.pallas_call(
        body, name=name, grid=(n // tn, m // tm, parts),
        in_specs=[a_spec, b_spec],
        out_specs=pl.BlockSpec((tm, tn), lambda j, i, s: (i, j)),
        out_shape=jax.ShapeDtypeStruct((m, n), out_dtype),
        compiler_params=_params(("parallel", "parallel", "arbitrary")),
    )(a, b)


def _mm_tn(a, b, name, tk_cap=1024, tn_cap=1664, tl_cap=2048, chips=False):
    l, k = a.shape
    tk = _tile(k, tk_cap)
    tl = min(tl_cap, l)

    def body(a_ref, b_ref, o_ref):
        @pl.when(pl.program_id(2) == 0)
        def _():
            o_ref[...] = jnp.zeros_like(o_ref)

        o_ref[...] += _dot_tn(a_ref[...], b_ref[...])

    if chips:
        n = b.shape[-1] * (b.shape[0] if b.ndim == 3 else 1)
        tn = n // N_CHIPS
        assert tn % LANES == 0
        if b.ndim == 3:
            per = N_CHIPS // b.shape[0]
            b_spec = pl.BlockSpec((None, tl, tn), lambda i, j, r: (j // per, r, j % per))
        else:
            b_spec = pl.BlockSpec((tl, tn), lambda i, j, r: (r, j))
        out_spec = pl.BlockSpec((None, tk, tn), lambda i, j, r: (j, i, 0))
        out_shape = jax.ShapeDtypeStruct((N_CHIPS, k, tn), F32)
    else:
        n = b.shape[1]
        tn = _tile(n, tn_cap)
        b_spec = pl.BlockSpec((tl, tn), lambda i, j, r: (r, j))
        out_spec = pl.BlockSpec((tk, tn), lambda i, j, r: (i, j))
        out_shape = jax.ShapeDtypeStruct((k, n), F32)
    return pl.pallas_call(
        body, name=name, grid=(k // tk, n // tn, l // tl),
        in_specs=[pl.BlockSpec((tl, tk), lambda i, j, r: (r, i)), b_spec],
        out_specs=out_spec, out_shape=out_shape,
        compiler_params=_params(("parallel", "parallel", "arbitrary")),
    )(a, b)


def _row(tl, n):
    return pl.BlockSpec((tl, n), lambda i: (i, 0))


def _const(shape):
    return pl.BlockSpec(shape, lambda i: tuple(0 for _ in shape))


def _proj_fwd(x, g1, win, gq, wuq, gkv, wukv, rc, rs, bg, tl):
    l = x.shape[0]

    def body(x_ref, g1_ref, win_ref, gq_ref, wuq_ref, gkv_ref, wukv_ref, rc_ref, rs_ref, bg_ref,
             hn_ref, cq_ref, ckv_ref, q_ref, k_ref, v_ref, u_ref, gl_ref):
        hn, _ = _rms(x_ref[...], g1_ref[...])
        hnb = hn.astype(BF16)
        hn_ref[...] = hnb
        proj = _dot(hnb, win_ref[...])
        cq = proj[:, P_CQ:P_CKV]
        ckv = proj[:, P_CKV:P_KR]
        kr = proj[:, P_KR:P_U]
        cq_ref[...] = cq
        ckv_ref[...] = ckv
        u_ref[...] = proj[:, P_U:P_GL]
        gl_ref[...] = (proj[:, P_GL:P_END] + bg_ref[...]).astype(BF16)
        qn, _ = _rms(cq, gq_ref[...])
        q = _dot(qn.astype(BF16), wuq_ref[...])
        c1 = rc_ref[...]
        s1 = rs_ref[...]
        q_ref[...] = (_rope(q, jnp.tile(c1, (1, N_HEADS)), jnp.tile(s1, (1, N_HEADS))) * Q_PRESCALE).astype(BF16)
        ckvn, _ = _rms(ckv, gkv_ref[...])
        kv = _dot(ckvn.astype(BF16), wukv_ref[...])
        krr = _rope(kr, c1, s1)
        k_ref[...] = (kv[:, :HP] + jnp.tile(krr, (1, N_HEADS))).astype(BF16)
        v_ref[...] = kv[:, HP:].astype(BF16)

    outs = [(D_MODEL, BF16), (Q_RANK, F32), (KV_RANK, F32), (HP, BF16), (HP, BF16), (HP, BF16),
            (SSM_W, F32), (2 * D_MODEL, BF16)]
    return pl.pallas_call(
        body, name="proj_fwd", grid=(l // tl,),
        in_specs=[_row(tl, D_MODEL), _const((1, D_MODEL)), _const((D_MODEL, P_END)), _const((1, Q_RANK)),
                  _const((Q_RANK, HP)), _const((1, KV_RANK)), _const((KV_RANK, 2 * HP)),
                  _row(tl, HEAD_SLOT), _row(tl, HEAD_SLOT), _const((1, 2 * D_MODEL))],
        out_specs=[_row(tl, n) for n, _ in outs],
        out_shape=[jax.ShapeDtypeStruct((l, n), dt) for n, dt in outs],
        compiler_params=_params(("parallel",)),
    )(x, g1, win, gq, wuq, gkv, wukv, rc, rs, bg)


_NEG = -1e30


LOG2E = 1.0 / math.log(2.0)
LN2 = math.log(2.0)
ATTN_SCALE = 1.0 / math.sqrt(QK_HEAD)
Q_PRESCALE = ATTN_SCALE * LOG2E
FWD_HEADS = 8
BWD_HEADS = 8


def _causal_pairs(nq, by_query):
    if by_query:
        pairs = [(i, j) for i in range(nq) for j in range(i + 1)]
    else:
        pairs = [(i, j) for j in range(nq) for i in range(j, nq)]
    return jnp.array([p[0] for p in pairs], jnp.int32), jnp.array([p[1] for p in pairs], jnp.int32)


def _diag_mask_t(s):
    rows = lax.broadcasted_iota(jnp.int32, s.shape, 0)
    cols = lax.broadcasted_iota(jnp.int32, s.shape, 1)
    return jnp.where(rows <= cols, s, _NEG)


def _attn_fwd(q, k, v, tq, heads):
    l = q.shape[0]
    nq = l // tq
    it, jt = _causal_pairs(nq, True)

    def body(it_ref, jt_ref, q_ref, k_ref, v_ref, o_ref, lse_ref, m_ref, l_ref, acc_ref):
        t = pl.program_id(1)
        i = it_ref[t]
        j = jt_ref[t]

        @pl.when(j == 0)
        def _():
            m_ref[...] = jnp.full_like(m_ref, _NEG)
            l_ref[...] = jnp.zeros_like(l_ref)
            acc_ref[...] = jnp.zeros_like(acc_ref)

        def update(on_diagonal):
            for hh in range(heads):
                sl = slice(hh * HEAD_SLOT, (hh + 1) * HEAD_SLOT)
                s = _dot_nt(k_ref[:, sl], q_ref[:, sl])
                if on_diagonal:
                    s = _diag_mask_t(s)
                m_old = m_ref[hh]
                m_new = jnp.maximum(m_old, jnp.max(s, axis=0, keepdims=True))
                p = jnp.exp2(s - m_new)
                alpha = jnp.exp2(m_old - m_new)
                l_ref[hh] = alpha * l_ref[hh] + jnp.sum(p, axis=0, keepdims=True)
                acc_ref[hh] = alpha * acc_ref[hh] + _dot_tn(v_ref[:, sl], p.astype(BF16))
                m_ref[hh] = m_new

        @pl.when(j < i)
        def _():
            update(False)

        @pl.when(j == i)
        def _():
            update(True)
            for hh in range(heads):
                sl = slice(hh * HEAD_SLOT, (hh + 1) * HEAD_SLOT)
                o_ref[:, sl] = (acc_ref[hh] / l_ref[hh]).T.astype(BF16)
                lse_ref[hh] = m_ref[hh] + jnp.log(l_ref[hh]) * LOG2E

    blk = (tq, (heads * HEAD_SLOT))
    qmap = lambda h, t, it_ref, jt_ref: (it_ref[t], h)
    kmap = lambda h, t, it_ref, jt_ref: (jt_ref[t], h)
    row = pl.BlockSpec((heads, 1, tq), lambda h, t, it_ref, jt_ref: (h, 0, it_ref[t]))
    return pl.pallas_call(
        body, name="attn_fwd",
        grid_spec=pltpu.PrefetchScalarGridSpec(
            num_scalar_prefetch=2, grid=(N_HEADS // heads, it.shape[0]),
            in_specs=[pl.BlockSpec(blk, qmap), pl.BlockSpec(blk, kmap), pl.BlockSpec(blk, kmap)],
            out_specs=[pl.BlockSpec(blk, qmap), row],
            scratch_shapes=[pltpu.VMEM((heads, 1, tq), F32), pltpu.VMEM((heads, 1, tq), F32),
                            pltpu.VMEM((heads, HEAD_SLOT, tq), F32)]),
        out_shape=[jax.ShapeDtypeStruct((l, HP), BF16), jax.ShapeDtypeStruct((N_HEADS, 1, l), F32)],
        compiler_params=_params(("parallel", "arbitrary")),
    )(it, jt, q, k, v)


def _attn_delta(o, do, tq, heads):
    l = o.shape[0]

    def body(o_ref, do_ref, d_ref):
        prod = o_ref[...].astype(F32) * do_ref[...].astype(F32)
        for hh in range(heads):
            d_ref[hh] = jnp.sum(prod[:, hh * HEAD_SLOT:(hh + 1) * HEAD_SLOT].T, axis=0, keepdims=True)

    blk = pl.BlockSpec((tq, (heads * HEAD_SLOT)), lambda h, i: (i, h))
    return pl.pallas_call(
        body, name="attn_delta", grid=(N_HEADS // heads, l // tq), in_specs=[blk, blk],
        out_specs=pl.BlockSpec((heads, 1, tq), lambda h, i: (h, 0, i)),
        out_shape=jax.ShapeDtypeStruct((N_HEADS, 1, l), F32),
        compiler_params=_params(("parallel", "parallel")),
    )(o, do)


def _attn_bwd(q, k, v, do, lse, delta, tq, heads):
    l = q.shape[0]
    nq = l // tq
    it, jt = _causal_pairs(nq, False)

    def body(it_ref, jt_ref, q_ref, k_ref, v_ref, do_ref, lse_ref, dl_ref, dq_ref, dk_ref, dv_ref, dka_ref, dva_ref):
        t = pl.program_id(1)
        i = it_ref[t]
        j = jt_ref[t]

        @pl.when(t == 0)
        def _():
            dq_ref[...] = jnp.zeros_like(dq_ref)

        @pl.when(i == j)
        def _():
            dka_ref[...] = jnp.zeros_like(dka_ref)
            dva_ref[...] = jnp.zeros_like(dva_ref)

        def update(on_diagonal):
            r0 = pl.multiple_of(i * tq, tq)
            for hh in range(heads):
                sl = slice(hh * HEAD_SLOT, (hh + 1) * HEAD_SLOT)
                qb = q_ref[:, sl]
                kb = k_ref[:, sl]
                dob = do_ref[:, sl]
                s = _dot_nt(kb, qb)
                if on_diagonal:
                    s = _diag_mask_t(s)
                p = jnp.exp2(s - lse_ref[hh])
                dva_ref[:, sl] += _dot(p.astype(BF16), dob)
                dp = _dot_nt(v_ref[:, sl], dob)
                ds = (p * (dp - dl_ref[hh])).astype(BF16)
                dka_ref[:, sl] += _dot(ds, qb)
                dq_ref[pl.ds(r0, tq), sl] += ATTN_SCALE * _dot_tn(ds, kb)

        @pl.when(j < i)
        def _():
            update(False)

        @pl.when(j == i)
        def _():
            update(True)

        @pl.when(i == nq - 1)
        def _():
            dk_ref[...] = (dka_ref[...] * LN2).astype(BF16)
            dv_ref[...] = dva_ref[...].astype(BF16)

    blk = (tq, (heads * HEAD_SLOT))
    qmap = lambda h, t, it_ref, jt_ref: (it_ref[t], h)
    kmap = lambda h, t, it_ref, jt_ref: (jt_ref[t], h)
    row = pl.BlockSpec((heads, 1, tq), lambda h, t, it_ref, jt_ref: (h, 0, it_ref[t]))
    return pl.pallas_call(
        body, name="attn_bwd",
        grid_spec=pltpu.PrefetchScalarGridSpec(
            num_scalar_prefetch=2, grid=(N_HEADS // heads, it.shape[0]),
            in_specs=[pl.BlockSpec(blk, qmap), pl.BlockSpec(blk, kmap), pl.BlockSpec(blk, kmap),
                      pl.BlockSpec(blk, qmap), row, row],
            out_specs=[pl.BlockSpec((l, (heads * HEAD_SLOT)), lambda h, t, it_ref, jt_ref: (0, h)), pl.BlockSpec(blk, kmap),
                       pl.BlockSpec(blk, kmap)],
            scratch_shapes=[pltpu.VMEM(blk, F32), pltpu.VMEM(blk, F32)]),
        out_shape=[jax.ShapeDtypeStruct((l, HP), F32), jax.ShapeDtypeStruct((l, HP), BF16),
                   jax.ShapeDtypeStruct((l, HP), BF16)],
        compiler_params=_params(("parallel", "arbitrary")),
    )(it, jt, q, k, v, do, lse, delta)


SSM_CB = 512
SSM_UB = 128
SSM_NB = SSM_CH // SSM_CB


def _scan_tiles(re_ref, im_ref, tab, carry, n_tiles, reverse):
    group = 2
    assert n_tiles % group == 0
    pr, pi = tab[6], tab[7]

    def inside(sr, si):
        for step, k in enumerate((1, 2, 4)):
            mr, mi = tab[2 * step], tab[2 * step + 1]
            sh = (SUBLANES - k) if reverse else k
            rr = pltpu.roll(sr, sh, 0)
            ri = pltpu.roll(si, sh, 0)
            sr, si = sr + mr * rr - mi * ri, si + mr * ri + mi * rr
        return sr, si

    def body(n, c):
        cr, ci = c
        first = (n_tiles - group * (n + 1)) if reverse else group * n
        r0 = pl.multiple_of(first * SUBLANES, group * SUBLANES)
        rows = [pl.ds(r0 + g * SUBLANES, SUBLANES) for g in range(group)]
        tiles = [inside(re_ref[r, :], im_ref[r, :]) for r in rows]
        for g in (range(group - 1, -1, -1) if reverse else range(group)):
            sr, si = tiles[g]
            sr, si = sr + pr * cr - pi * ci, si + pr * ci + pi * cr
            re_ref[rows[g], :] = sr
            im_ref[rows[g], :] = si
            edge = slice(0, 1) if reverse else slice(SUBLANES - 1, SUBLANES)
            cr, ci = sr[edge, :], si[edge, :]
        return cr, ci

    return lax.fori_loop(0, n_tiles // group, body, carry)


def _ssm_fwd(u, bre, bim, cre, cim, dvec, tab, tt):
    l = u.shape[0]
    nt = l // tt

    def body(u_ref, bre_ref, bim_ref, cre_ref, cim_ref, d_ref, tab_ref, y_ref, sre_ref, sim_ref, car_ref):
        @pl.when(pl.program_id(1) == 0)
        def _():
            car_ref[...] = jnp.zeros_like(car_ref)

        uf = u_ref[...]
        ub = uf.astype(BF16)
        sre_ref[...] = _dot(ub, bre_ref[0])
        sim_ref[...] = _dot(ub, bim_ref[0])
        tab_v = [tab_ref[n] for n in range(8)]
        cr, ci = _scan_tiles(sre_ref, sim_ref, tab_v, (car_ref[0:1, :], car_ref[8:9, :]), tt // SUBLANES, False)
        car_ref[0:1, :] = cr
        car_ref[8:9, :] = ci
        y_ref[...] = (_dot(sre_ref[...].astype(BF16), cre_ref[0]) - _dot(sim_ref[...].astype(BF16), cim_ref[0])
                      + d_ref[...] * uf)

    return pl.pallas_call(
        body, name="ssm_fwd", grid=(SSM_NB, nt),
        in_specs=[pl.BlockSpec((tt, SSM_UB), lambda m, t: (t, m)),
                  pl.BlockSpec((1, SSM_UB, SSM_CB), lambda m, t: (m, 0, 0)),
                  pl.BlockSpec((1, SSM_UB, SSM_CB), lambda m, t: (m, 0, 0)),
                  pl.BlockSpec((1, SSM_CB, SSM_UB), lambda m, t: (m, 0, 0)),
                  pl.BlockSpec((1, SSM_CB, SSM_UB), lambda m, t: (m, 0, 0)),
                  pl.BlockSpec((1, SSM_UB), lambda m, t: (0, m)),
                  pl.BlockSpec((8, SUBLANES, SSM_CB), lambda m, t: (0, 0, m))],
        out_specs=[pl.BlockSpec((tt, SSM_UB), lambda m, t: (t, m)),
                   pl.BlockSpec((tt, SSM_CB), lambda m, t: (t, m)),
                   pl.BlockSpec((tt, SSM_CB), lambda m, t: (t, m))],
        out_shape=[jax.ShapeDtypeStruct((l, SSM_W), F32), jax.ShapeDtypeStruct((l, SSM_CH), F32),
                   jax.ShapeDtypeStruct((l, SSM_CH), F32)],
        scratch_shapes=[pltpu.VMEM((2 * SUBLANES, SSM_CB), F32)],
        compiler_params=_params(("parallel", "arbitrary")),
    )(u, bre, bim, cre, cim, dvec, tab)


def _ssm_bwd(dy, u, sre, sim, bre, bim, cre, cim, dvec, tab, tt):
    l = u.shape[0]
    nt = l // tt
    tpb = tt // SUBLANES

    def body(dy_ref, u_ref, sre_ref, sim_ref, hre_ref, him_ref, bre_ref, bim_ref, cre_ref, cim_ref, d_ref, tab_ref,
             du_ref, dbre_ref, dbim_ref, dcre_ref, dcim_ref, dare_ref, daim_ref, dd_ref, lr_ref, li_ref, car_ref):
        t = pl.program_id(1)

        @pl.when(t == 0)
        def _():
            car_ref[...] = jnp.zeros_like(car_ref)
            for ref in (dbre_ref, dbim_ref, dcre_ref, dcim_ref, dare_ref, daim_ref, dd_ref):
                ref[...] = jnp.zeros_like(ref)

        dyf = dy_ref[...]
        dyb = dyf.astype(BF16)
        uf = u_ref[...]
        s_re = sre_ref[...]
        s_im = sim_ref[...]
        lr_ref[...] = _dot_nt(dyb, cre_ref[0])
        li_ref[...] = -_dot_nt(dyb, cim_ref[0])
        dcre_ref[0] += _dot_tn(s_re.astype(BF16), dyb)
        dcim_ref[0] -= _dot_tn(s_im.astype(BF16), dyb)
        tab_v = [tab_ref[n] for n in range(8)]
        cr, ci = _scan_tiles(lr_ref, li_ref, tab_v, (car_ref[0:1, :], car_ref[8:9, :]), tpb, True)
        car_ref[0:1, :] = cr
        car_ref[8:9, :] = ci
        lam_r = lr_ref[...]
        lam_i = li_ref[...]
        keep = jnp.where(t == nt - 1, 0.0, 1.0)
        sp_r = _shift_down(s_re, 1, hre_ref[...] * keep)
        sp_i = _shift_down(s_im, 1, him_ref[...] * keep)
        dare_ref[...] += jnp.sum(lam_r * sp_r + lam_i * sp_i, axis=0, keepdims=True)
        daim_ref[...] += jnp.sum(lam_i * sp_r - lam_r * sp_i, axis=0, keepdims=True)
        lrb = lam_r.astype(BF16)
        lib = lam_i.astype(BF16)
        du_ref[...] = _dot_nt(lrb, bre_ref[0]) + _dot_nt(lib, bim_ref[0]) + dyf * d_ref[...]
        ub = uf.astype(BF16)
        dbre_ref[0] += _dot_tn(ub, lrb)
        dbim_ref[0] += _dot_tn(ub, lib)
        dd_ref[...] += jnp.sum(dyf * uf, axis=0, keepdims=True)

    rev = lambda m, t: (nt - 1 - t, m)
    halo = lambda m, t: (jnp.maximum((nt - 1 - t) * tpb - 1, 0), m)
    wb = pl.BlockSpec((1, SSM_UB, SSM_CB), lambda m, t: (m, 0, 0))
    wc = pl.BlockSpec((1, SSM_CB, SSM_UB), lambda m, t: (m, 0, 0))
    vec_c = pl.BlockSpec((1, SSM_CB), lambda m, t: (0, m))
    vec_u = pl.BlockSpec((1, SSM_UB), lambda m, t: (0, m))
    return pl.pallas_call(
        body, name="ssm_bwd", grid=(SSM_NB, nt),
        in_specs=[pl.BlockSpec((tt, SSM_UB), rev), pl.BlockSpec((tt, SSM_UB), rev),
                  pl.BlockSpec((tt, SSM_CB), rev), pl.BlockSpec((tt, SSM_CB), rev),
                  pl.BlockSpec((SUBLANES, SSM_CB), halo), pl.BlockSpec((SUBLANES, SSM_CB), halo),
                  wb, wb, wc, wc, vec_u,
                  pl.BlockSpec((8, SUBLANES, SSM_CB), lambda m, t: (0, 0, m))],
        out_specs=[pl.BlockSpec((tt, SSM_UB), rev), wb, wb, wc, wc, vec_c, vec_c, vec_u],
        out_shape=[jax.ShapeDtypeStruct((l, SSM_W), F32),
                   jax.ShapeDtypeStruct((SSM_NB, SSM_UB, SSM_CB), F32), jax.ShapeDtypeStruct((SSM_NB, SSM_UB, SSM_CB), F32),
                   jax.ShapeDtypeStruct((SSM_NB, SSM_CB, SSM_UB), F32), jax.ShapeDtypeStruct((SSM_NB, SSM_CB, SSM_UB), F32),
                   jax.ShapeDtypeStruct((1, SSM_CH), F32), jax.ShapeDtypeStruct((1, SSM_CH), F32),
                   jax.ShapeDtypeStruct((1, SSM_W), F32)],
        scratch_shapes=[pltpu.VMEM((tt, SSM_CB), F32), pltpu.VMEM((tt, SSM_CB), F32),
                        pltpu.VMEM((2 * SUBLANES, SSM_CB), F32)],
        compiler_params=_params(("parallel", "arbitrary")),
    )(dy, u, sre, sim, sre, sim, bre, bim, cre, cim, dvec, tab)


def _merge_fwd(x, gl, attn, y1, wba, wbs, wglu, bglu, wout, gpost, gpre, tl):
    l = x.shape[0]

    def body(x_ref, gl_ref, at_ref, y1_ref, wba_ref, wbs_ref, wglu_ref, bglu_ref, wout_ref, gpost_ref, gpre_ref,
             a_ref, sm_ref, mg_ref, z_ref, x1_ref, hn2_ref, y3_ref):
        y2 = _gelu(y1_ref[...])
        sg = _sigmoid(_dot(y2.astype(BF16), wglu_ref[...]) + bglu_ref[...])
        y3 = (y2 * sg).astype(BF16)
        y3_ref[...] = y3
        a = _dot(at_ref[...], wba_ref[...])
        sm = _dot(y3, wbs_ref[...])
        a_ref[...] = a.astype(BF16)
        sm_ref[...] = sm.astype(BF16)
        g = _sigmoid(gl_ref[...].astype(F32))
        merged = (g[:, :D_MODEL] * a + g[:, D_MODEL:] * sm).astype(BF16)
        mg_ref[...] = merged
        z = _dot(merged, wout_ref[...])
        z_ref[...] = z
        n, _ = _rms(z, gpost_ref[...])
        x1 = x_ref[...] + n
        x1_ref[...] = x1
        hn2, _ = _rms(x1, gpre_ref[...])
        hn2_ref[...] = hn2.astype(BF16)

    outs = [(D_MODEL, BF16), (D_MODEL, BF16), (D_MODEL, BF16), (D_MODEL, F32), (D_MODEL, F32), (D_MODEL, BF16),
            (SSM_W, BF16)]
    return pl.pallas_call(
        body, name="merge_fwd", grid=(l // tl,),
        in_specs=[_row(tl, D_MODEL), _row(tl, 2 * D_MODEL), _row(tl, HP), _row(tl, SSM_W),
                  _const((HP, D_MODEL)), _const((SSM_W, D_MODEL)), _const((SSM_W, SSM_W)), _const((1, SSM_W)),
                  _const((D_MODEL, D_MODEL)), _const((1, D_MODEL)), _const((1, D_MODEL))],
        out_specs=[_row(tl, n) for n, _ in outs],
        out_shape=[jax.ShapeDtypeStruct((l, n), dt) for n, dt in outs],
        compiler_params=_params(("parallel",)),
    )(x, gl, attn, y1, wba, wbs, wglu, bglu, wout, gpost, gpre)


def _merge_bwd(dhn2, x1, dx2, z, gl, a, sm, y1, wba, wbs, wglu, bglu, wout, gpost, gpre, tl):
    l = x1.shape[0]

    def body(dhn2_ref, x1_ref, dx2_ref, z_ref, gl_ref, a_ref, sm_ref, y1_ref,
             wba_ref, wbs_ref, wglu_ref, bglu_ref, wout_ref, gpost_ref, gpre_ref,
             dx1_ref, dz_ref, dbra_ref, dbrs_ref, dgl_ref, dat_ref, dy1_ref, dt_ref, y2_ref,
             dgpre_ref, dgpost_ref, dbg_ref, dbglu_ref):
        @pl.when(pl.program_id(0) == 0)
        def _():
            for ref in (dgpre_ref, dgpost_ref, dbg_ref, dbglu_ref):
                ref[...] = jnp.zeros_like(ref)

        dx1a, dgpre = _rms_bwd(dhn2_ref[...], x1_ref[...], gpre_ref[...])
        dgpre_ref[...] += dgpre
        dx1 = dx2_ref[...] + dx1a
        dx1_ref[...] = dx1
        dz, dgpost = _rms_bwd(dx1, z_ref[...], gpost_ref[...])
        dgpost_ref[...] += dgpost
        dzb = dz.astype(BF16)
        dz_ref[...] = dzb
        dm = _dot_nt(dzb, wout_ref[...])
        g = _sigmoid(gl_ref[...].astype(F32))
        g0 = g[:, :D_MODEL]
        g1 = g[:, D_MODEL:]
        dbra = (dm * g0).astype(BF16)
        dbrs = (dm * g1).astype(BF16)
        dbra_ref[...] = dbra
        dbrs_ref[...] = dbrs
        dgl0 = dm * a_ref[...].astype(F32) * g0 * (1.0 - g0)
        dgl1 = dm * sm_ref[...].astype(F32) * g1 * (1.0 - g1)
        dgl_ref[:, :D_MODEL] = dgl0.astype(BF16)
        dgl_ref[:, D_MODEL:] = dgl1.astype(BF16)
        dbg_ref[:, :D_MODEL] += jnp.sum(dgl0, axis=0, keepdims=True)
        dbg_ref[:, D_MODEL:] += jnp.sum(dgl1, axis=0, keepdims=True)
        dat_ref[...] = _dot_nt(dbra, wba_ref[...]).astype(BF16)
        dy3 = _dot_nt(dbrs, wbs_ref[...])
        y1v = y1_ref[...]
        y2 = _gelu(y1v)
        y2b = y2.astype(BF16)
        y2_ref[...] = y2b
        sg = _sigmoid(_dot(y2b, wglu_ref[...]) + bglu_ref[...])
        dt = dy3 * y2 * sg * (1.0 - sg)
        dtb = dt.astype(BF16)
        dt_ref[...] = dtb
        dbglu_ref[...] += jnp.sum(dt, axis=0, keepdims=True)
        dy2 = dy3 * sg + _dot_nt(dtb, wglu_ref[...])
        dy1_ref[...] = dy2 * _gelu_grad(y1v)

    outs = [(D_MODEL, F32), (D_MODEL, BF16), (D_MODEL, BF16), (D_MODEL, BF16), (2 * D_MODEL, BF16), (HP, BF16),
            (SSM_W, F32), (SSM_W, BF16), (SSM_W, BF16)]
    accs = [D_MODEL, D_MODEL, 2 * D_MODEL, SSM_W]
    return pl.pallas_call(
        body, name="merge_bwd", grid=(l // tl,),
        in_specs=[_row(tl, D_MODEL), _row(tl, D_MODEL), _row(tl, D_MODEL), _row(tl, D_MODEL),
                  _row(tl, 2 * D_MODEL), _row(tl, D_MODEL), _row(tl, D_MODEL), _row(tl, SSM_W),
                  _const((HP, D_MODEL)), _const((SSM_W, D_MODEL)), _const((SSM_W, SSM_W)), _const((1, SSM_W)),
                  _const((D_MODEL, D_MODEL)), _const((1, D_MODEL)), _const((1, D_MODEL))],
        out_specs=[_row(tl, n) for n, _ in outs] + [_const((1, n)) for n in accs],
        out_shape=[jax.ShapeDtypeStruct((l, n), dt) for n, dt in outs]
        + [jax.ShapeDtypeStruct((1, n), F32) for n in accs],
        compiler_params=_params(("arbitrary",)),
    )(dhn2, x1, dx2, z, gl, a, sm, y1, wba, wbs, wglu, bglu, wout, gpost, gpre)


def _proj_bwd(x, dx1, cq, ckv, dq, dk, dv, du, dgl, g1, win, gq, wuq, gkv, wukv, rc, rs, tl):
    l = x.shape[0]

    def body(x_ref, dx1_ref, cq_ref, ckv_ref, dq_ref, dk_ref, dv_ref, du_ref, dgl_ref,
             g1_ref, win_ref, gq_ref, wuq_ref, gkv_ref, wukv_ref, rc_ref, rs_ref,
             gx_ref, dql_ref, qn_ref, ckvn_ref, dproj_ref, dg1_ref, dgq_ref, dgkv_ref):
        @pl.when(pl.program_id(0) == 0)
        def _():
            for ref in (dg1_ref, dgq_ref, dgkv_ref):
                ref[...] = jnp.zeros_like(ref)

        c1 = rc_ref[...]
        s1 = rs_ref[...]
        dql = _rope_bwd(dq_ref[...], jnp.tile(c1, (1, N_HEADS)), jnp.tile(s1, (1, N_HEADS))).astype(BF16)
        dql_ref[...] = dql
        dqn = _dot_nt(dql, wuq_ref[...])
        cq = cq_ref[...]
        qn, _ = _rms(cq, gq_ref[...])
        qn_ref[...] = qn.astype(BF16)
        dcq, dgq = _rms_bwd(dqn, cq, gq_ref[...])
        dgq_ref[...] += dgq
        dkb = dk_ref[...]
        dvb = dv_ref[...]
        dkf = dkb.astype(F32)
        dkr = dkf[:, 0:HEAD_SLOT]
        for h in range(1, N_HEADS):
            dkr = dkr + dkf[:, h * HEAD_SLOT:(h + 1) * HEAD_SLOT]
        dkr = _rope_bwd(dkr, c1, s1)
        dckvn = _dot_nt(dkb, wukv_ref[:, :HP]) + _dot_nt(dvb, wukv_ref[:, HP:])
        ckv = ckv_ref[...]
        ckvn, _ = _rms(ckv, gkv_ref[...])
        ckvn_ref[...] = ckvn.astype(BF16)
        dckv, dgkv = _rms_bwd(dckvn, ckv, gkv_ref[...])
        dgkv_ref[...] += dgkv
        dproj_ref[:, P_CQ:P_CKV] = dcq.astype(BF16)
        dproj_ref[:, P_CKV:P_KR] = dckv.astype(BF16)
        dproj_ref[:, P_KR:P_U] = dkr.astype(BF16)
        dproj_ref[:, P_U:P_GL] = du_ref[...].astype(BF16)
        dproj_ref[:, P_GL:P_END] = dgl_ref[...]
        dhn = _dot_nt(dproj_ref[...], win_ref[...])
        dxa, dg1 = _rms_bwd(dhn, x_ref[...], g1_ref[...])
        dg1_ref[...] += dg1
        gx_ref[...] = dx1_ref[...] + dxa

    outs = [(D_MODEL, F32), (HP, BF16), (Q_RANK, BF16), (KV_RANK, BF16), (P_END, BF16)]
    accs = [D_MODEL, Q_RANK, KV_RANK]
    return pl.pallas_call(
        body, name="proj_bwd", grid=(l // tl,),
        in_specs=[_row(tl, D_MODEL), _row(tl, D_MODEL), _row(tl, Q_RANK), _row(tl, KV_RANK), _row(tl, HP),
                  _row(tl, HP), _row(tl, HP), _row(tl, SSM_W), _row(tl, 2 * D_MODEL),
                  _const((1, D_MODEL)), _const((D_MODEL, P_END)), _const((1, Q_RANK)), _const((Q_RANK, HP)),
                  _const((1, KV_RANK)), _const((KV_RANK, 2 * HP)), _row(tl, HEAD_SLOT), _row(tl, HEAD_SLOT)],
        out_specs=[_row(tl, n) for n, _ in outs] + [_const((1, n)) for n in accs],
        out_shape=[jax.ShapeDtypeStruct((l, n), dt) for n, dt in outs]
        + [jax.ShapeDtypeStruct((1, n), F32) for n in accs],
        compiler_params=_params(("arbitrary",)),
    )(x, dx1, cq, ckv, dq, dk, dv, du, dgl, g1, win, gq, wuq, gkv, wukv, rc, rs)


CONV_CB = 256
CONV_NB = D_FF // CONV_CB
CONV_ROWS = 16


def _conv3(h, halo, w, b):
    return b + w[0:1, :] * _shift_down(h, 2, halo) + w[1:2, :] * _shift_down(h, 1, halo) + w[2:3, :] * h


def _conv_fwd(h, cw, cb, tl):
    l = h.shape[0]

    def body(hg_ref, hv_ref, wg_ref, wv_ref, bg_ref, bv_ref, act_ref, halo_ref):
        @pl.when(pl.program_id(1) == 0)
        def _():
            halo_ref[...] = jnp.zeros_like(halo_ref)

        hg = hg_ref[...]
        hv = hv_ref[...]
        cg = _conv3(hg, halo_ref[0:SUBLANES, :], wg_ref[...], bg_ref[...])
        cv = _conv3(hv, halo_ref[SUBLANES:, :], wv_ref[...], bv_ref[...])
        act_ref[...] = (_gelu(cg) * cv).astype(BF16)
        halo_ref[0:SUBLANES, :] = hg[tl - SUBLANES:, :]
        halo_ref[SUBLANES:, :] = hv[tl - SUBLANES:, :]

    gmap = lambda c, r: (r, c)
    vmap = lambda c, r: (r, CONV_NB + c)
    return pl.pallas_call(
        body, name="conv_fwd", grid=(CONV_NB, l // tl),
        in_specs=[pl.BlockSpec((tl, CONV_CB), gmap), pl.BlockSpec((tl, CONV_CB), vmap),
                  pl.BlockSpec((3, CONV_CB), lambda c, r: (0, c)), pl.BlockSpec((3, CONV_CB), lambda c, r: (0, CONV_NB + c)),
                  pl.BlockSpec((1, CONV_CB), lambda c, r: (0, c)), pl.BlockSpec((1, CONV_CB), lambda c, r: (0, CONV_NB + c))],
        out_specs=pl.BlockSpec((tl, CONV_CB), gmap),
        out_shape=jax.ShapeDtypeStruct((l, D_FF), BF16),
        scratch_shapes=[pltpu.VMEM((2 * SUBLANES, CONV_CB), F32)],
        compiler_params=_params(("parallel", "arbitrary")),
    )(h, h, cw, cw, cb, cb)


def _conv_bwd(h, dact, cw, cb, tl):
    l = h.shape[0]
    nr = l // tl
    tpb = tl // SUBLANES

    def body(hg_ref, hv_ref, hgh_ref, hvh_ref, da_ref, wg_ref, wv_ref, bg_ref, bv_ref,
             dh_ref, dwg_ref, dwv_ref, dbg_ref, dbv_ref, car_ref):
        r = pl.program_id(1)

        @pl.when(r == 0)
        def _():
            for ref in (car_ref, dwg_ref, dwv_ref, dbg_ref, dbv_ref):
                ref[...] = jnp.zeros_like(ref)

        keep = jnp.where(r == nr - 1, 0.0, 1.0)
        wg, wv, bg, bv = wg_ref[...], wv_ref[...], bg_ref[...], bv_ref[...]
        nch = tl // CONV_ROWS

        def fold(x):
            s = x[0:SUBLANES, :]
            for k in range(1, CONV_ROWS // SUBLANES):
                s = s + x[k * SUBLANES:(k + 1) * SUBLANES, :]
            return s

        def chunk(n, carry):
            ncg, ncv, acc = carry
            idx = nch - 1 - n
            r0 = pl.multiple_of(idx * CONV_ROWS, CONV_ROWS)
            rows = pl.ds(r0, CONV_ROWS)
            before = pl.ds(pl.multiple_of(jnp.maximum(r0 - SUBLANES, 0), SUBLANES), SUBLANES)
            in_tile = idx > 0
            da = da_ref[rows, :].astype(F32)

            def half(h_ref, halo_ref, w, b):
                hh = h_ref[rows, :]
                prev = jnp.where(in_tile, h_ref[before, :], halo_ref[...] * keep)
                h1 = _shift_down(hh, 1, prev)
                h2 = _shift_down(hh, 2, prev)
                return hh, h1, h2, b + w[0:1, :] * h2 + w[1:2, :] * h1 + w[2:3, :] * hh

            hg, hg1, hg2, cg = half(hg_ref, hgh_ref, wg, bg)
            hv, hv1, hv2, cv = half(hv_ref, hvh_ref, wv, bv)
            dcg = da * cv * _gelu_grad(cg)
            dcv = da * _gelu(cg)

            def back(dc, hh, h1, h2, w, nxt, part):
                dh = w[2:3, :] * dc + w[1:2, :] * _shift_up(dc, 1, nxt) + w[0:1, :] * _shift_up(dc, 2, nxt)
                dh_ref[part, rows, :] = dh.astype(BF16)
                return [fold(dc * h2), fold(dc * h1), fold(dc * hh), fold(dc)]

            sums = back(dcg, hg, hg1, hg2, wg, ncg, 0) + back(dcv, hv, hv1, hv2, wv, ncv, 1)
            return dcg[0:SUBLANES, :], dcv[0:SUBLANES, :], [a + s for a, s in zip(acc, sums)]

        zero = jnp.zeros((SUBLANES, CONV_CB), F32)
        ncg, ncv, acc = lax.fori_loop(0, nch, chunk, (car_ref[0:SUBLANES, :], car_ref[SUBLANES:, :], [zero] * 8))
        car_ref[0:SUBLANES, :] = ncg
        car_ref[SUBLANES:, :] = ncv
        for half_acc, dw_ref, db_ref in ((acc[0:4], dwg_ref, dbg_ref), (acc[4:8], dwv_ref, dbv_ref)):
            for k in range(3):
                dw_ref[k:k + 1, :] += jnp.sum(half_acc[k], axis=0, keepdims=True)
            db_ref[...] += jnp.sum(half_acc[3], axis=0, keepdims=True)

    grev = lambda c, r: (nr - 1 - r, c)
    vrev = lambda c, r: (nr - 1 - r, CONV_NB + c)
    ghalo = lambda c, r: (jnp.maximum((nr - 1 - r) * tpb - 1, 0), c)
    vhalo = lambda c, r: (jnp.maximum((nr - 1 - r) * tpb - 1, 0), CONV_NB + c)
    colg = lambda c, r: (0, c)
    colv = lambda c, r: (0, CONV_NB + c)
    return pl.pallas_call(
        body, name="conv_bwd", grid=(CONV_NB, nr),
        in_specs=[pl.BlockSpec((tl, CONV_CB), grev), pl.BlockSpec((tl, CONV_CB), vrev),
                  pl.BlockSpec((SUBLANES, CONV_CB), ghalo), pl.BlockSpec((SUBLANES, CONV_CB), vhalo),
                  pl.BlockSpec((tl, CONV_CB), grev),
                  pl.BlockSpec((3, CONV_CB), colg), pl.BlockSpec((3, CONV_CB), colv),
                  pl.BlockSpec((1, CONV_CB), colg), pl.BlockSpec((1, CONV_CB), colv)],
        out_specs=[pl.BlockSpec((2, tl, CONV_CB), lambda c, r: (0, nr - 1 - r, c)),
                   pl.BlockSpec((3, CONV_CB), colg), pl.BlockSpec((3, CONV_CB), colg),
                   pl.BlockSpec((1, CONV_CB), colg), pl.BlockSpec((1, CONV_CB), colg)],
        out_shape=[jax.ShapeDtypeStruct((2, l, D_FF), BF16),
                   jax.ShapeDtypeStruct((3, D_FF), F32), jax.ShapeDtypeStruct((3, D_FF), F32),
                   jax.ShapeDtypeStruct((1, D_FF), F32), jax.ShapeDtypeStruct((1, D_FF), F32)],
        scratch_shapes=[pltpu.VMEM((2 * SUBLANES, CONV_CB), F32)],
        compiler_params=_params(("parallel", "arbitrary")),
    )(h, h, h, h, dact, cw, cw, cb, cb)


def _loss_head(ff, x1, tgt, g, tl):
    l = ff.shape[0]

    def body(ff_ref, x1_ref, tg_ref, g_ref, loss_ref, dx2_ref, dff_ref, dg_ref):
        @pl.when(pl.program_id(0) == 0)
        def _():
            loss_ref[...] = jnp.zeros_like(loss_ref)
            dg_ref[...] = jnp.zeros_like(dg_ref)

        f = ff_ref[...]
        gv = g_ref[...]
        n, _ = _rms(f, gv)
        e = x1_ref[...] + n - tg_ref[...]
        loss_ref[...] += 0.5 * jnp.sum(jnp.mean(e * e, axis=-1, keepdims=True), axis=0, keepdims=True)
        dx2 = e * (1.0 / D_MODEL)
        dx2_ref[...] = dx2
        dff, dg = _rms_bwd(dx2, f, gv)
        dff_ref[...] = dff.astype(BF16)
        dg_ref[...] += dg

    return pl.pallas_call(
        body, name="loss_head", grid=(l // tl,),
        in_specs=[_row(tl, D_MODEL), _row(tl, D_MODEL), _row(tl, D_MODEL), _const((1, D_MODEL))],
        out_specs=[_const((1, LANES)), _row(tl, D_MODEL), _row(tl, D_MODEL), _const((1, D_MODEL))],
        out_shape=[jax.ShapeDtypeStruct((1, LANES), F32), jax.ShapeDtypeStruct((l, D_MODEL), F32),
                   jax.ShapeDtypeStruct((l, D_MODEL), BF16), jax.ShapeDtypeStruct((1, D_MODEL), F32)],
        compiler_params=_params(("arbitrary",)),
    )(ff, x1, tgt, g)


def _ssm_disc(lam_re, lam_im, log_dt, b_re, b_im):
    dt = jnp.exp(log_dt)[:, None]
    mag = jnp.exp(lam_re * dt)
    ang = lam_im * dt
    a_re, a_im = mag * jnp.cos(ang), mag * jnp.sin(ang)
    den = lam_re * lam_re + lam_im * lam_im
    n_re, n_im = a_re - 1.0, a_im
    z_re = (n_re * lam_re + n_im * lam_im) / den
    z_im = (n_im * lam_re - n_re * lam_im) / den
    bb_re = z_re[..., None] * b_re - z_im[..., None] * b_im
    bb_im = z_re[..., None] * b_im + z_im[..., None] * b_re
    return a_re, a_im, bb_re, bb_im


_GPB = SSM_CB // SSM_P


def _embed_b(bb):
    t = bb.transpose(0, 2, 1).reshape(SSM_NB, _GPB, SSM_H, SSM_P)
    return jnp.einsum('mjhp,jk->mjhkp', t, jnp.eye(_GPB, dtype=bb.dtype)).reshape(SSM_NB, SSM_UB, SSM_CB)


def _extract_b(d):
    t = d.reshape(SSM_NB, _GPB, SSM_H, _GPB, SSM_P)
    t = jnp.einsum('mjhkp,jk->mjhp', t, jnp.eye(_GPB, dtype=d.dtype))
    return t.reshape(SSM_G, SSM_H, SSM_P).transpose(0, 2, 1)


def _embed_c(c):
    t = c.transpose(0, 2, 1).reshape(SSM_NB, _GPB, SSM_P, SSM_H)
    return jnp.einsum('mjph,jk->mjpkh', t, jnp.eye(_GPB, dtype=c.dtype)).reshape(SSM_NB, SSM_CB, SSM_UB)


def _extract_c(d):
    t = d.reshape(SSM_NB, _GPB, SSM_P, _GPB, SSM_H)
    t = jnp.einsum('mjpkh,jk->mjph', t, jnp.eye(_GPB, dtype=d.dtype))
    return t.reshape(SSM_G, SSM_P, SSM_H).transpose(0, 2, 1)


def _scan_tables(a_re, a_im, reverse):
    ar = a_re.reshape(1, SSM_CH)
    ai = (-a_im if reverse else a_im).reshape(1, SSM_CH)
    pr, pi = [ar], [ai]
    for _ in range(SUBLANES - 1):
        pr, pi = pr + [pr[-1] * ar - pi[-1] * ai], pi + [pr[-1] * ai + pi[-1] * ar]
    rows = jnp.arange(SUBLANES)[:, None]
    out = []
    for k in (1, 2, 4):
        valid = (rows + k <= SUBLANES - 1) if reverse else (rows >= k)
        out += [jnp.where(valid, pr[k - 1], 0.0), jnp.where(valid, pi[k - 1], 0.0)]
    order = list(range(SUBLANES - 1, -1, -1)) if reverse else list(range(SUBLANES))
    out += [jnp.concatenate([pr[n] for n in order], axis=0), jnp.concatenate([pi[n] for n in order], axis=0)]
    return jnp.stack(out).astype(F32)


def _pad_heads(w, d):
    lead = w.shape[:-1]
    w = w.reshape(lead + (N_HEADS, d))
    w = jnp.pad(w, [(0, 0)] * len(lead) + [(0, 0), (0, HEAD_SLOT - d)])
    return w.reshape(lead + (HP,))


def _unpad_heads(w, d):
    lead = w.shape[:-1]
    return w.reshape(lead + (N_HEADS, HEAD_SLOT))[..., :d].reshape(lead + (N_HEADS * d,))


def _chip_major(w, axis):
    k, n = w.shape
    if axis == 0:
        return w.reshape(N_CHIPS, k // N_CHIPS, n)
    return w.reshape(k, N_CHIPS, n // N_CHIPS).transpose(1, 0, 2)


def _from_chip_major(w, axis):
    if axis == 0:
        return w.reshape(-1, w.shape[2])
    return w.transpose(1, 0, 2).reshape(w.shape[1], -1)


def _pad_w_in(w):
    z = lambda n: jnp.zeros((w.shape[0], n), w.dtype)
    return jnp.concatenate([w[:, :640], z(KR_LANE), w[:, 640:672], z(HEAD_SLOT - KR_LANE - QK_ROPE), w[:, 672:]], axis=1)


def _unpad_w_in(w):
    return jnp.concatenate([w[:, :640], w[:, P_KR + KR_LANE:P_KR + KR_LANE + QK_ROPE], w[:, P_U:]], axis=1)


def _local_step(x, positions, tgt, wts, sp):
    l = x.shape[0]
    tl = min(512, l)
    tb = min(256, l)
    ta = min(512, l)
    ts = min(2048, l)

    inv_freq = ROPE_THETA ** (-jnp.arange(0, QK_ROPE, 2, dtype=F32) / QK_ROPE)
    ang = positions.astype(F32)[:, None] * inv_freq
    cos, sin = jnp.cos(ang), jnp.sin(ang)
    one = jnp.ones((l, KR_LANE), F32)
    rc = jnp.concatenate([one, cos, cos, jnp.ones((l, HEAD_SLOT - KR_LANE - QK_ROPE), F32)], axis=1)
    rs = jnp.concatenate([0 * one, -sin, sin, jnp.zeros((l, HEAD_SLOT - KR_LANE - QK_ROPE), F32)], axis=1)

    win = _pad_w_in(wts["w_in"])
    wuq = _pad_heads(wts["w_uq"], QK_HEAD)
    wukv = jnp.concatenate([_pad_heads(wts["w_uk"], QK_NOPE), _pad_heads(wts["w_uv"], V_HEAD)], axis=1)

    disc_in = (sp["ssm_lambda_re"], sp["ssm_lambda_im"], sp["ssm_log_dt"], sp["ssm_b_re"], sp["ssm_b_im"])
    (a_re, a_im, bb_re, bb_im), disc_vjp = jax.vjp(_ssm_disc, *disc_in)
    bre, bim = _embed_b(bb_re).astype(BF16), _embed_b(bb_im).astype(BF16)
    cre, cim = _embed_c(sp["ssm_c_re"]).astype(BF16), _embed_c(sp["ssm_c_im"]).astype(BF16)
    dvec = sp["ssm_d"].reshape(1, SSM_W)
    tab_f = _scan_tables(a_re, a_im, False)
    tab_r = _scan_tables(a_re, a_im, True)

    g1, gq, gkv = sp["mix_norm_pre"], sp["q_norm"], sp["kv_norm"]
    gpost, gpre, gfin = sp["mix_norm_post"], sp["ffn_norm_pre"], sp["ffn_norm_post"]
    bgate, bglu, convb = sp["b_gate"], sp["b_glu"], sp["conv_b"]

    hn, cq, ckv, q, k, v, u, gl = _proj_fwd(x, g1, win, gq, wuq, gkv, wukv, rc, rs, bgate, tl)
    attn, lse = _attn_fwd(q, k, v, ta, FWD_HEADS)
    y1, sre, sim = _ssm_fwd(u, bre, bim, cre, cim, dvec, tab_f, ts)
    wba = jnp.pad(wts["w_branch_attn"].reshape(N_HEADS, V_HEAD, D_MODEL),
                  ((0, 0), (0, HEAD_SLOT - V_HEAD), (0, 0))).reshape(HP, D_MODEL)
    wbs, wglu, wout = wts["w_branch_ssm"], wts["w_glu"], wts["w_out"]
    a, sm, merged, z, x1, hn2, y3 = _merge_fwd(x, gl, attn, y1, wba, wbs, wglu, bglu, wout, gpost, gpre, tl)
    late = wts["late"](x1)
    wup, wdown, convw = late["w_up"], late["w_down"], late["conv_w"]
    h = _mm(hn2, wup, "ffn_up", tm_cap=2048)
    act = _conv_fwd(h, convw, convb, ts)
    ff = _mm(act, wdown, "ffn_down")
    loss, dx2, dff, dgfin = _loss_head(ff, x1, tgt, gfin, ta)

    dact = _mm(dff, wdown, "ffn_down_dx", out_dtype=BF16, bt=True, tm_cap=2048)
    d_wdown = _mm_tn(act, dff, "ffn_down_dw", tk_cap=D_FF // 2)
    dh, dwg, dwv, dbg, dbv = _conv_bwd(h, dact, convw, convb, ts)
    d_convw = jnp.concatenate([dwg, dwv], axis=1)
    d_convb = jnp.concatenate([dbg, dbv], axis=1)
    dhn2 = _mm(dh, wup, "ffn_up_dx", bt=True)
    d_wup = _mm_tn(hn2, dh, "ffn_up_dw", chips=True)
    behind = wts["send_grads"]("ffn", {"w_up": d_wup, "w_down": _chip_major(d_wdown, 0)})
    (dx1, dz, dbra, dbrs, dgl, dattn, dy1, dt, y2, dgpre, dgpost, dbgate, dbglu) = _merge_bwd(
        dhn2, x1, dx2, z, gl, a, sm, y1, wba, wbs, wglu, bglu, wout, gpost, gpre + behind, tb)
    d_wout = _mm_tn(merged, dz, "w_out_dw")
    d_wba = _mm_tn(attn, dbra, "w_branch_attn_dw", chips=True)
    d_wbs = _mm_tn(y3, dbrs, "w_branch_ssm_dw", chips=True)
    d_wglu = _mm_tn(y2, dt, "w_glu_dw")
    ncol = D_MODEL // N_CHIPS
    behind = wts["send_grads"]("mix", {
        "w_glu": _chip_major(d_wglu, 0),
        "w_branch_attn": d_wba.reshape(N_CHIPS, N_HEADS, HEAD_SLOT, ncol)[:, :, :V_HEAD].reshape(
            N_CHIPS, N_HEADS * V_HEAD, ncol),
        "w_branch_ssm": d_wbs,
        "w_out": _chip_major(d_wout, 0)}, after=d_wglu)
    dq, dk, dv = _attn_bwd(q, k, v, dattn, lse + behind, _attn_delta(attn, dattn, min(2048, l), BWD_HEADS), ta,
                            BWD_HEADS)
    du, dbre, dbim, dcre, dcim, dare, daim, dd = _ssm_bwd(dy1, u, sre, sim, bre, bim, cre, cim, dvec, tab_r, ts)
    behind = wts["send_grads"]("none", {}, after=du)
    gx, dql, qn, ckvn, dproj, dg1, dgq, dgkv = _proj_bwd(
        x, dx1, cq, ckv, dq, dk, dv, du, dgl, g1 + behind, win, gq, wuq, gkv, wukv, rc, rs, tl)
    d_win = _mm_tn(hn, dproj, "w_in_dw")
    d_wuq = _mm_tn(qn, dql, "w_uq_dw")
    d_wuk = _mm_tn(ckvn, dk, "w_uk_dw")
    d_wuv = _mm_tn(ckvn, dv, "w_uv_dw")

    d_lre, d_lim, d_ldt, d_bre, d_bim = disc_vjp((dare.reshape(SSM_G, SSM_P), daim.reshape(SSM_G, SSM_P),
                                                  _extract_b(dbre), _extract_b(dbim)))
    big = {
        "w_in": _chip_major(_unpad_w_in(d_win), 1),
        "w_uq": _chip_major(_unpad_heads(d_wuq, QK_HEAD), 1),
        "w_uk": _chip_major(_unpad_heads(d_wuk, QK_NOPE), 1),
        "w_uv": _chip_major(_unpad_heads(d_wuv, V_HEAD), 1),
    }
    small = {
        "conv_w": d_convw,
        "mix_norm_pre": dg1, "q_norm": dgq, "kv_norm": dgkv,
        "ssm_lambda_re": d_lre, "ssm_lambda_im": d_lim, "ssm_log_dt": d_ldt,
        "ssm_b_re": d_bre, "ssm_b_im": d_bim,
        "ssm_c_re": _extract_c(dcre), "ssm_c_im": _extract_c(dcim),
        "ssm_d": dd.reshape(SSM_G, SSM_H), "b_glu": dbglu, "b_gate": dbgate,
        "mix_norm_post": dgpost, "ffn_norm_pre": dgpre, "conv_b": d_convb, "ffn_norm_post": dgfin,
    }
    return loss[0, 0], gx, big, small


_ANY = pl.BlockSpec(memory_space=pl.ANY)


ROW_TILE = 16


def _place():
    x, y, c = lax.axis_index("x"), lax.axis_index("y"), lax.axis_index("c")
    return x, y, c, 2 * x + y, [(1 - x, y), (x, 1 - y), (1 - x, 1 - y)]


def _half(rows, which):
    hr = rows // 2
    return pl.ds(pl.multiple_of(which * hr, ROW_TILE), hr)


def _remote(src, dst, send_sems, recv_sems, n, dev):
    return pltpu.make_async_remote_copy(src_ref=src, dst_ref=dst, send_sem=send_sems.at[n], recv_sem=recv_sems.at[n],
                                        device_id=dev, device_id_type=MESH)


def _gather_big(shards):
    nw = len(shards)
    rows = [s.shape[0] for s in shards]

    def body(*refs):
        ins, outs = refs[:nw], refs[nw:2 * nw]
        ici_send, ici_recv, d2d_send, d2d_recv = refs[2 * nw:]
        x, y, c, me, peers = _place()
        sent = []
        for i in range(nw):
            for p, (px, py) in enumerate(peers):
                cp = _remote(ins[i].at[_half(rows[i], c)], outs[i].at[me, _half(rows[i], c)], ici_send, ici_recv,
                             3 * i + p, (px, py, c))
                cp.start()
                sent.append(cp)
        for p, (px, py) in enumerate(peers):
            for i in range(nw):
                blk = outs[i].at[2 * px + py, _half(rows[i], c)]
                _remote(blk, blk, ici_send, ici_recv, 3 * i + p, (px, py, c)).wait_recv()
                cp = _remote(blk, blk, d2d_send, d2d_recv, 3 * i + p, (x, y, 1 - c))
                cp.start()
                sent.append(cp)
        for p, (px, py) in enumerate(peers):
            for i in range(nw):
                blk = outs[i].at[2 * px + py, _half(rows[i], 1 - c)]
                _remote(blk, blk, d2d_send, d2d_recv, 3 * i + p, (x, y, 1 - c)).wait_recv()
        for cp in sent:
            cp.wait_send()

    dma = pltpu.SemaphoreType.DMA
    return pl.pallas_call(
        body, name="gather_weights", in_specs=[_ANY] * nw, out_specs=[_ANY] * nw,
        out_shape=[jax.ShapeDtypeStruct((N_CHIPS,) + s.shape, s.dtype) for s in shards],
        scratch_shapes=[dma((3 * nw,)), dma((3 * nw,)), dma((3 * nw,)), dma((3 * nw,))],
    )(*shards)


_HBM = pl.BlockSpec(memory_space=pltpu.HBM)
_SEM = pl.BlockSpec(memory_space=pltpu.SEMAPHORE)
_DATAFLOW = pltpu.SideEffectType.DATAFLOW_SIDE_EFFECTING


def _exchange_start(shards, name, scatter):
    nw = len(shards)
    lands = [lax.empty(s.shape if scatter else (N_CHIPS,) + s.shape, s.dtype) for s in shards]

    def body(*refs):
        ins, zones = refs[:nw], refs[nw:2 * nw]
        send_sems, recv_sems, token = refs[2 * nw], refs[2 * nw + 1], refs[-1]
        x, y, c, me, peers = _place()
        for i in range(nw):
            for p, (px, py) in enumerate(peers):
                src = ins[i].at[2 * px + py] if scatter else ins[i]
                _remote(src, zones[i].at[me], send_sems, recv_sems, 3 * i + p, (px, py, c)).start()
        token[...] = jnp.zeros_like(token)

    thru = [pltpu.HBM(a.shape, a.dtype) for a in list(shards) + lands]
    dma = pltpu.SemaphoreType.DMA
    outs = pl.pallas_call(
        body, name=name,
        out_shape=(dma((3 * nw,)), dma((3 * nw,)), *thru, jax.ShapeDtypeStruct((SUBLANES, LANES), F32)),
        in_specs=[_HBM] * (2 * nw),
        out_specs=(_SEM, _SEM, *([_HBM] * (2 * nw)), pl.BlockSpec(memory_space=pltpu.VMEM)),
        input_output_aliases={i: 2 + i for i in range(2 * nw)},
        compiler_params=pltpu.CompilerParams(has_side_effects=_DATAFLOW),
    )(*[pltpu.with_memory_space_constraint(a, pltpu.HBM) for a in list(shards) + lands])
    return outs[0], outs[1], list(outs[2:2 + nw]), list(outs[2 + nw:2 + 2 * nw]), outs[-1]


def _exchange_wait(send_sems, recv_sems, shards, lands, after, name, scatter):
    nw = len(shards)

    def body(*refs):
        ins, zones = refs[:nw], refs[nw:2 * nw]
        send_sems, recv_sems = refs[2 * nw], refs[2 * nw + 1]
        x, y, c, me, peers = _place()
        for i in range(nw):
            for p, (px, py) in enumerate(peers):
                src = ins[i].at[2 * px + py] if scatter else ins[i]
                cp = _remote(src, zones[i].at[2 * px + py], send_sems, recv_sems, 3 * i + p, (px, py, c))
                cp.wait_send()
                cp.wait_recv()

    both = list(shards) + list(lands)
    outs = pl.pallas_call(
        body, name=name,
        out_shape=tuple(pltpu.HBM(a.shape, a.dtype) for a in both),
        in_specs=(*([_HBM] * (2 * nw)), _SEM, _SEM, _ANY), out_specs=[_HBM] * (2 * nw),
        input_output_aliases={i: i for i in range(2 * nw)},
        compiler_params=pltpu.CompilerParams(has_side_effects=_DATAFLOW),
    )(*both, send_sems, recv_sems, after)
    return list(outs[:nw]), list(outs[nw:])


def _sibling_start(grads, name):
    nw = len(grads)
    lands = [lax.empty((N_CHIPS, g.shape[1] // 2, g.shape[2]), g.dtype) for g in grads]

    def body(*refs):
        ins, zones = refs[:nw], refs[nw:2 * nw]
        send_sems, recv_sems, token = refs[2 * nw], refs[2 * nw + 1], refs[-1]
        x, y, c, _, _ = _place()
        for i in range(nw):
            _remote(ins[i].at[pl.ds(0, N_CHIPS), _half(grads[i].shape[1], 1 - c)], zones[i], send_sems, recv_sems,
                    i, (x, y, 1 - c)).start()
        token[...] = jnp.zeros_like(token)

    thru = [pltpu.HBM(a.shape, a.dtype) for a in list(grads) + lands]
    dma = pltpu.SemaphoreType.DMA
    outs = pl.pallas_call(
        body, name=name,
        out_shape=(dma((nw,)), dma((nw,)), *thru, jax.ShapeDtypeStruct((SUBLANES, LANES), F32)),
        in_specs=[_HBM] * (2 * nw),
        out_specs=(_SEM, _SEM, *([_HBM] * (2 * nw)), pl.BlockSpec(memory_space=pltpu.VMEM)),
        input_output_aliases={i: 2 + i for i in range(2 * nw)},
        compiler_params=pltpu.CompilerParams(has_side_effects=_DATAFLOW),
    )(*[pltpu.with_memory_space_constraint(a, pltpu.HBM) for a in list(grads) + lands])
    return outs[0], outs[1], list(outs[2:2 + nw]), list(outs[2 + nw:2 + 2 * nw]), outs[-1]


def _sibling_wait(send_sems, recv_sems, grads, lands, after, name):
    nw = len(grads)

    def body(*refs):
        ins, zones = refs[:nw], refs[nw:2 * nw]
        send_sems, recv_sems = refs[2 * nw], refs[2 * nw + 1]
        x, y, c, _, _ = _place()
        for i in range(nw):
            cp = _remote(ins[i].at[pl.ds(0, N_CHIPS), _half(grads[i].shape[1], 1 - c)], zones[i], send_sems, recv_sems,
                         i, (x, y, 1 - c))
            cp.wait_send()
            cp.wait_recv()

    both = list(grads) + list(lands)
    outs = pl.pallas_call(
        body, name=name,
        out_shape=tuple(pltpu.HBM(a.shape, a.dtype) for a in both),
        in_specs=(*([_HBM] * (2 * nw)), _SEM, _SEM, _ANY), out_specs=[_HBM] * (2 * nw),
        input_output_aliases={i: i for i in range(2 * nw)},
        compiler_params=pltpu.CompilerParams(has_side_effects=_DATAFLOW),
    )(*both, send_sems, recv_sems, after)
    return list(outs[:nw]), list(outs[nw:])


def _reduce_to_sibling(grads, name):
    nw = len(grads)

    def body(*refs):
        ins, outs = refs[:nw], refs[nw:2 * nw]
        send_sems, recv_sems = refs[2 * nw:]
        x, y, c, _, _ = _place()
        sent = []
        for i in range(nw):
            cp = _remote(ins[i].at[pl.ds(0, N_CHIPS), _half(grads[i].shape[1], 1 - c)], outs[i], send_sems, recv_sems,
                         i, (x, y, 1 - c))
            cp.start()
            sent.append(cp)
        for cp in sent:
            cp.wait()

    dma = pltpu.SemaphoreType.DMA
    return pl.pallas_call(
        body, name=name, in_specs=[_ANY] * nw, out_specs=[_ANY] * nw,
        out_shape=[jax.ShapeDtypeStruct((N_CHIPS, g.shape[1] // 2, g.shape[2]), g.dtype) for g in grads],
        scratch_shapes=[dma((nw,)), dma((nw,))],
    )(*grads)


def _reduce_back(totals, name):
    nw = len(totals)

    def body(*refs):
        outs = refs[nw:2 * nw]
        send_sems, recv_sems = refs[2 * nw:]
        x, y, c, _, _ = _place()
        sent = []
        for i in range(nw):
            blk = outs[i].at[_half(totals[i].shape[0], c)]
            cp = _remote(blk, blk, send_sems, recv_sems, i, (x, y, 1 - c))
            cp.start()
            sent.append(cp)
        for i in range(nw):
            blk = outs[i].at[_half(totals[i].shape[0], 1 - c)]
            _remote(blk, blk, send_sems, recv_sems, i, (x, y, 1 - c)).wait_recv()
        for cp in sent:
            cp.wait_send()

    dma = pltpu.SemaphoreType.DMA
    return pl.pallas_call(
        body, name=name, in_specs=[_ANY] * nw, out_specs=[_ANY] * nw,
        out_shape=[jax.ShapeDtypeStruct(t.shape, t.dtype) for t in totals],
        input_output_aliases={i: i for i in range(nw)},
        scratch_shapes=[dma((nw,)), dma((nw,))],
    )(*totals)


def _all_reduce_small(v, name):
    rows, w = v.shape
    hr = rows // 2
    assert hr % SUBLANES == 0

    def body(v_ref, out_ref, sib_ref, half_ref, chips_ref, send_sems, recv_sems):
        x, y, c, me, peers = _place()
        sibling = (x, y, 1 - c)
        mine = pl.ds(pl.multiple_of(c * hr, SUBLANES), hr)
        other = pl.ds(pl.multiple_of((1 - c) * hr, SUBLANES), hr)
        cp = _remote(v_ref, sib_ref, send_sems, recv_sems, 0, sibling)
        cp.start()
        cp.wait()
        half_ref[...] = v_ref[mine, :] + sib_ref[mine, :]
        sent = []
        for p, (px, py) in enumerate(peers):
            cp = _remote(half_ref, chips_ref.at[me], send_sems, recv_sems, 1 + p, (px, py, c))
            cp.start()
            sent.append(cp)
        chips_ref[me] = half_ref[...]
        for p, (px, py) in enumerate(peers):
            _remote(half_ref, chips_ref.at[2 * px + py], send_sems, recv_sems, 1 + p, (px, py, c)).wait_recv()
        for cp in sent:
            cp.wait_send()
        out_ref[mine, :] = ((chips_ref[0] + chips_ref[1]) + chips_ref[2]) + chips_ref[3]
        cp = _remote(out_ref.at[mine], out_ref.at[mine], send_sems, recv_sems, 4, sibling)
        cp.start()
        _remote(out_ref.at[other], out_ref.at[other], send_sems, recv_sems, 4, sibling).wait_recv()
        cp.wait_send()

    vm = pl.BlockSpec(memory_space=pltpu.VMEM)
    return pl.pallas_call(
        body, name=name, in_specs=[vm], out_specs=vm,
        out_shape=jax.ShapeDtypeStruct((rows, w), F32),
        scratch_shapes=[pltpu.VMEM((rows, w), F32), pltpu.VMEM((hr, w), F32), pltpu.VMEM((N_CHIPS, hr, w), F32),
                        pltpu.SemaphoreType.DMA((5,)), pltpu.SemaphoreType.DMA((5,))],
        compiler_params=pltpu.CompilerParams(vmem_limit_bytes=VMEM_LIMIT),
    )(v)


ELEMENTWISE_BLOCK = 512 * 1024


def _rows_tile(rows, cols, cap=ELEMENTWISE_BLOCK):
    best = None
    for t in range(SUBLANES, rows + 1, SUBLANES):
        if rows % t == 0 and t * cols <= cap:
            best = t
    return rows if best is None else best


def _add_pair(g, t, core, name):
    nb, n, w = t.shape
    tr = _rows_tile(n, w, 2 * ELEMENTWISE_BLOCK)
    steps = n // tr

    def body(core_ref, g_ref, t_ref, o_ref):
        o_ref[...] = (g_ref[...] + t_ref[...]).astype(BF16)

    spec = pl.BlockSpec((1, tr, w), lambda j, i, core_ref: (j, i, 0))
    return pl.pallas_call(
        body, name=name,
        grid_spec=pltpu.PrefetchScalarGridSpec(
            num_scalar_prefetch=1, grid=(nb, steps),
            in_specs=[pl.BlockSpec((1, tr, w), lambda j, i, core_ref: (j, core_ref[0] * steps + i, 0)), spec],
            out_specs=spec),
        out_shape=jax.ShapeDtypeStruct(t.shape, BF16),
        compiler_params=_params(("parallel", "parallel")))(core, g, t)


def _add_chips(landed, pairs, place, name):
    nb, n, w = landed.shape
    tr = _rows_tile(n, w, 2 * ELEMENTWISE_BLOCK)
    steps = n // tr

    def body(place_ref, r_ref, own_ref, o_ref):
        me = place_ref[0]
        acc = None
        for k in range(nb):
            blk = jnp.where(me == k, own_ref[0], r_ref[k]).astype(F32)
            acc = blk if acc is None else acc + blk
        o_ref[...] = acc

    return pl.pallas_call(
        body, name=name,
        grid_spec=pltpu.PrefetchScalarGridSpec(
            num_scalar_prefetch=1, grid=(steps,),
            in_specs=[pl.BlockSpec((nb, tr, w), lambda i, place_ref: (0, i, 0)),
                      pl.BlockSpec((1, tr, w), lambda i, place_ref: (place_ref[0], i, 0))],
            out_specs=pl.BlockSpec((tr, w), lambda i, place_ref: (place_ref[1] * steps + i, 0))),
        out_shape=jax.ShapeDtypeStruct((2 * n, w), F32),
        compiler_params=_params(("parallel",)))(place, landed, pairs)


def _adamw(w, g, m, v, name):
    rows, wd = w.shape
    tr = _rows_tile(rows, wd)
    c1 = 1.0 - ADAM_B1 ** ADAM_STEP
    c2 = 1.0 - ADAM_B2 ** ADAM_STEP

    def body(w_ref, g_ref, m_ref, v_ref, go_ref, d_ref, mo_ref, vo_ref):
        gv = g_ref[...]
        go_ref[...] = gv
        m2 = ADAM_B1 * m_ref[...] + (1.0 - ADAM_B1) * gv
        v2 = ADAM_B2 * v_ref[...] + (1.0 - ADAM_B2) * (gv * gv)
        mo_ref[...] = m2
        vo_ref[...] = v2
        d_ref[...] = -ADAM_LR * ((m2 / c1) / (jnp.sqrt(v2 / c2) + ADAM_EPS) + ADAM_WD * w_ref[...])

    spec = pl.BlockSpec((tr, wd), lambda i: (i, 0))
    shp = jax.ShapeDtypeStruct((rows, wd), F32)
    return pl.pallas_call(body, name=name, grid=(rows // tr,), in_specs=[spec] * 4, out_specs=[spec] * 4,
                          out_shape=[shp] * 4, compiler_params=_params(("parallel",)))(w, g, m, v)


BIG = [("w_in", (1024, 3232), 1), ("w_uq", (384, 768), 1), ("w_uk", (256, 512), 1), ("w_uv", (256, 512), 1),
       ("w_glu", (512, 512), 0), ("w_branch_attn", (512, 1024), 1), ("w_branch_ssm", (512, 1024), 1),
       ("w_out", (1024, 1024), 0), ("w_up", (1024, 5632), 1), ("conv_w", (3, 5632), 1), ("w_down", (2816, 1024), 0)]
SMALL = [("mix_norm_pre", (1024,)), ("q_norm", (384,)), ("kv_norm", (256,)), ("ssm_lambda_re", (32, 64)),
         ("ssm_lambda_im", (32, 64)), ("ssm_log_dt", (32,)), ("ssm_b_re", (32, 64, 16)), ("ssm_b_im", (32, 64, 16)),
         ("ssm_c_re", (32, 16, 64)), ("ssm_c_im", (32, 16, 64)), ("ssm_d", (32, 16)), ("b_glu", (512,)),
         ("b_gate", (2048,)), ("mix_norm_post", (1024,)), ("ffn_norm_pre", (1024,)), ("conv_b", (5632,)),
         ("ffn_norm_post", (1024,))]
MATMUL_W = [b for b in BIG if b[0] != "conv_w"]
LATE_W = ("w_up", "w_down", "conv_w")
CONV_W_SHAPE = (3, 2 * D_FF)
CONV_W_SHARD = (3, 2 * D_FF // N_CHIPS)
SMALL_SUM = [("loss", (1,))] + SMALL + [("conv_w", CONV_W_SHAPE)]
SMALL_ADAM = SMALL + [("conv_w", CONV_W_SHARD)]


def _pack_flat(layout, vals):
    flat = jnp.concatenate([vals[n].astype(F32).reshape(-1) for n, _ in layout])
    rows = -(-(-(-flat.shape[0] // FLAT_W)) // (2 * SUBLANES)) * 2 * SUBLANES
    return jnp.pad(flat, (0, rows * FLAT_W - flat.shape[0])).reshape(rows, FLAT_W)


def _unpack_flat(layout, flat):
    flat = flat.reshape(-1)
    out = {}
    o = 0
    for name, shape in layout:
        n = math.prod(shape)
        out[name] = flat[o:o + n].reshape(shape)
        o += n
    return out


_ARG_NAMES = ["x", "positions"] + [n for n in (
    "mix_norm_pre", "w_in", "q_norm", "w_uq", "kv_norm", "w_uk", "w_uv", "ssm_lambda_re", "ssm_lambda_im", "ssm_log_dt",
    "ssm_b_re", "ssm_b_im", "ssm_c_re", "ssm_c_im", "ssm_d", "w_glu", "b_glu", "w_branch_attn", "w_branch_ssm",
    "b_gate", "w_out", "mix_norm_post", "ffn_norm_pre", "w_up", "conv_w", "conv_b", "w_down", "ffn_norm_post")]
_WEIGHTS = _ARG_NAMES[2:]


def _gather_weights(w):
    early = [b for b in MATMUL_W if b[0] not in LATE_W]
    late = [b for b in BIG if b[0] in LATE_W]
    own = (jnp.arange(N_CHIPS) == 2 * lax.axis_index("x") + lax.axis_index("y"))[:, None, None]

    def whole(layout, mine, gathered):
        return {name: _from_chip_major(jnp.where(own, s[None], g), axis)
                for (name, _, axis), s, g in zip(layout, mine, gathered)}

    mine = [w[name].astype(BF16) for name, _, _ in early]
    gathered = _gather_big(mine)
    full = whole(early, mine, gathered)
    mine_late = [w[name].astype(F32 if name == "conv_w" else BF16) for name, _, _ in late]
    _, mine_late = lax.optimization_barrier((gathered[0], mine_late))
    send_sems, recv_sems, shards_thru, lands_thru, token = _exchange_start(mine_late, "gather_late_start", scatter=False)

    def late_weights(after):
        shards, lands = _exchange_wait(send_sems, recv_sems, shards_thru, lands_thru, after, "gather_late_wait",
                                       scatter=False)
        return whole(late, shards, lands)

    full["late"] = late_weights
    full["token"] = token[0, 0]
    return full


def _pair_sums(names, grads, tag):
    core = lax.axis_index("c").astype(jnp.int32).reshape(1)
    theirs = _reduce_to_sibling(grads, "reduce_grads_d2d" + tag)
    return [_add_pair(g, t, core, "reduce_pair_" + n) for n, g, t in zip(names, grads, theirs)]


def _send_grads(tag, grads, after, flying, pending):
    token = jnp.zeros((), F32)
    if flying:
        tag0, names0, state0 = flying.pop()
        core = lax.axis_index("c").astype(jnp.int32).reshape(1)
        mine, theirs = _sibling_wait(*state0, after, "reduce_" + tag0 + "_d2d_wait")
        pairs = [_add_pair(g, t, core, "reduce_pair_" + n) for n, g, t in zip(names0, mine, theirs)]
        send_sems, recv_sems, pairs_thru, lands_thru, tok = _exchange_start(pairs, "reduce_" + tag0 + "_start",
                                                                          scatter=True)
        pending.append((tag0, names0, send_sems, recv_sems, pairs_thru, lands_thru))
        token = token + tok[0, 0]
    if grads:
        names = list(grads)
        send_sems, recv_sems, grads_thru, lands_thru, tok = _sibling_start([grads[n] for n in names],
                                                                          "reduce_" + tag + "_d2d_start")
        flying.append((tag, names, (send_sems, recv_sems, grads_thru, lands_thru)))
        token = token + tok[0, 0]
    return token


def _reduce_grads(gbig, pending, loss, gsmall, use_sent):
    core = lax.axis_index("c").astype(jnp.int32).reshape(1)
    chip = (2 * lax.axis_index("x") + lax.axis_index("y")).astype(jnp.int32).reshape(1)
    place = jnp.concatenate([chip, core])

    def finish(names, pairs, landed, name):
        totals = [_add_chips(r, p, place, "reduce_chips_" + n) for n, r, p in zip(names, landed, pairs)]
        return dict(zip(names, _reduce_back(totals, name)))

    names = list(gbig)
    pairs = _pair_sums(names, [gbig[n] for n in names], "")
    send_sems, recv_sems, pairs_thru, lands_thru, token = _exchange_start(pairs, "reduce_last_start", scatter=True)
    sent_names, sent_pairs, sent_landed = [], [], []
    for tag, group, g_send, g_recv, g_pairs, g_lands in pending:
        got_pairs, got_landed = _exchange_wait(g_send, g_recv, g_pairs, g_lands, token, "reduce_" + tag + "_wait",
                                               scatter=True)
        sent_names, sent_pairs, sent_landed = sent_names + group, sent_pairs + got_pairs, sent_landed + got_landed
    g_sent = finish(sent_names, sent_pairs, sent_landed, "reduce_back_sent")
    vals = dict(gsmall)
    vals["loss"] = loss
    small_red = _unpack_flat(SMALL_SUM, _all_reduce_small(_pack_flat(SMALL_SUM, vals), "reduce_small"))
    after = use_sent(g_sent, small_red)
    pairs, landed = _exchange_wait(send_sems, recv_sems, pairs_thru, lands_thru, after, "reduce_last_wait", scatter=True)
    return finish(names, pairs, landed, "reduce_back_last"), small_red


def _step(args):
    x = args["x"][0]
    positions = args["positions"][0]
    tgt = args["loss_target"][0]
    w = {n: args[n][0] for n in _WEIGHTS}
    m = {n: args["m_" + n][0] for n in _WEIGHTS}
    v = {n: args["v_" + n][0] for n in _WEIGHTS}

    full = _gather_weights(w)
    sp = {n: w[n].reshape(s) for n, s in SMALL}
    for n in ("mix_norm_pre", "q_norm", "kv_norm", "b_glu", "b_gate", "mix_norm_post", "ffn_norm_pre", "conv_b",
              "ffn_norm_post"):
        sp[n] = sp[n].reshape(1, -1)
    sp["mix_norm_pre"] = sp["mix_norm_pre"] + full.pop("token")
    pending, flying = [], []
    full["send_grads"] = lambda tag, grads, after=None: _send_grads(tag, grads, after, flying, pending)
    loss, gx, gbig, gsmall = _local_step(x, positions, tgt, full, sp)
    outs = {}

    def adam_big(g_red):
        for name in g_red:
            g2, d, m2, v2 = _adamw(w[name], g_red[name], m[name], v[name], "adamw_" + name)
            outs["grad_" + name], outs["delta_" + name], outs["new_m_" + name], outs["new_v_" + name] = g2, d, m2, v2
        return v2

    def use_sent(g_sent, small_red):
        chip = 2 * lax.axis_index("x") + lax.axis_index("y")
        grads = dict(small_red)
        grads["conv_w"] = lax.dynamic_slice_in_dim(small_red["conv_w"], chip * CONV_W_SHARD[1], CONV_W_SHARD[1], axis=1)
        outs.update({"grad_" + n: grads[n] for n, _ in SMALL_ADAM})
        _, d_sm, m_sm, v_sm = _adamw(_pack_flat(SMALL_ADAM, w), _pack_flat(SMALL_ADAM, grads),
                                     _pack_flat(SMALL_ADAM, m), _pack_flat(SMALL_ADAM, v), "adamw_small")
        for prefix, flat in (("delta_", d_sm), ("new_m_", m_sm), ("new_v_", v_sm)):
            for n, val in _unpack_flat(SMALL_ADAM, flat).items():
                outs[prefix + n] = val
        return adam_big(g_sent)

    g_last, small_red = _reduce_grads(gbig, pending, loss, gsmall, use_sent)
    adam_big(g_last)
    outs = {n: val.reshape(args[n.split("_", 1)[1] if not n.startswith("new_") else n[6:]].shape)
            for n, val in outs.items()}
    res = [small_red["loss"][0], gx[None]]
    for prefix in ("grad_", "delta_", "new_m_", "new_v_"):
        res += [outs[prefix + n] for n in _WEIGHTS]
    return tuple(res)


def kernel(x, positions, mix_norm_pre, w_in, q_norm, w_uq, kv_norm, w_uk, w_uv, ssm_lambda_re, ssm_lambda_im, ssm_log_dt, ssm_b_re, ssm_b_im, ssm_c_re, ssm_c_im, ssm_d, w_glu, b_glu, w_branch_attn, w_branch_ssm, b_gate, w_out, mix_norm_post, ffn_norm_pre, w_up, conv_w, conv_b, w_down, ffn_norm_post, loss_target, m_mix_norm_pre, m_w_in, m_q_norm, m_w_uq, m_kv_norm, m_w_uk, m_w_uv, m_ssm_lambda_re, m_ssm_lambda_im, m_ssm_log_dt, m_ssm_b_re, m_ssm_b_im, m_ssm_c_re, m_ssm_c_im, m_ssm_d, m_w_glu, m_b_glu, m_w_branch_attn, m_w_branch_ssm, m_b_gate, m_w_out, m_mix_norm_post, m_ffn_norm_pre, m_w_up, m_conv_w, m_conv_b, m_w_down, m_ffn_norm_post, v_mix_norm_pre, v_w_in, v_q_norm, v_w_uq, v_kv_norm, v_w_uk, v_w_uv, v_ssm_lambda_re, v_ssm_lambda_im, v_ssm_log_dt, v_ssm_b_re, v_ssm_b_im, v_ssm_c_re, v_ssm_c_im, v_ssm_d, v_w_glu, v_b_glu, v_w_branch_attn, v_w_branch_ssm, v_b_gate, v_w_out, v_mix_norm_post, v_ffn_norm_pre, v_w_up, v_conv_w, v_conv_b, v_w_down, v_ffn_norm_post):
    given = dict(locals())
    return _step(given)
```

```python
import math

import jax
import jax.numpy as jnp
from jax import lax
from jax.experimental import pallas as pl
from jax.experimental.pallas import tpu as pltpu

F32 = jnp.float32
BF16 = jnp.bfloat16
MESH = pl.DeviceIdType.MESH

D_MODEL = 1024
N_HEADS = 8
QK_NOPE = 64
QK_ROPE = 32
QK_HEAD = QK_NOPE + QK_ROPE
V_HEAD = 64
Q_RANK = 384
KV_RANK = 256
ROPE_THETA = 10000.0
SSM_W = 512
SSM_H = 16
SSM_G = 32
SSM_P = 64
SSM_CH = SSM_G * SSM_P
D_FF = 2816
EPS = 1e-6
ADAM_LR = 0.001
ADAM_B1 = 0.9
ADAM_B2 = 0.999
ADAM_EPS = 1e-08
ADAM_WD = 0.01
ADAM_STEP = 10

LANES = 128
SUBLANES = 8
VMEM_LIMIT = 56 * 1024 * 1024

HEAD_SLOT = LANES
HP = N_HEADS * HEAD_SLOT
P_CQ, P_CKV, P_KR, P_U, P_GL, P_END = 0, 384, 640, 768, 1280, 3328
KR_LANE = 64

FLAT_W = 1024
N_CHIPS = 4


def _tile(n, cap):
    if n <= cap:
        return n
    best = None
    for t in range(LANES, cap + 1, LANES):
        if n % t == 0:
            best = t
    assert best is not None, (n, cap)
    return best


def _params(sem):
    return pltpu.CompilerParams(dimension_semantics=sem, vmem_limit_bytes=VMEM_LIMIT)


def _dot(a, b):
    return jnp.dot(a, b, preferred_element_type=F32)


def _dot_nt(a, b):
    return lax.dot_general(a, b, (((1,), (1,)), ((), ())), preferred_element_type=F32)


def _dot_tn(a, b):
    return lax.dot_general(a, b, (((0,), (0,)), ((), ())), preferred_element_type=F32)


def _rms(x, g):
    r = lax.rsqrt(jnp.mean(x * x, axis=-1, keepdims=True) + EPS)
    return x * r * g, r


def _rms_bwd(dy, x, g):
    r = lax.rsqrt(jnp.mean(x * x, axis=-1, keepdims=True) + EPS)
    dyg = dy * g
    dx = r * dyg - x * (r * r * r) * jnp.mean(dyg * x, axis=-1, keepdims=True)
    dg = jnp.sum(dy * x * r, axis=0, keepdims=True)
    return dx, dg


_GELU_K0 = math.sqrt(2.0 / math.pi)
_GELU_K1 = 0.044715


def _gelu(x):
    th = jnp.tanh(_GELU_K0 * (x + _GELU_K1 * x * x * x))
    return 0.5 * x * (1.0 + th)


def _gelu_grad(x):
    th = jnp.tanh(_GELU_K0 * (x + _GELU_K1 * x * x * x))
    return 0.5 * (1.0 + th) + 0.5 * x * (1.0 - th * th) * _GELU_K0 * (1.0 + 3.0 * _GELU_K1 * x * x)


def _sigmoid(x):
    return 1.0 / (1.0 + jnp.exp(-x))


def _rope(q, c, s):
    n = q.shape[1]
    lane = lax.broadcasted_iota(jnp.int32, q.shape, 1) % HEAD_SLOT
    sw = jnp.where(lane < KR_LANE + QK_ROPE // 2, pltpu.roll(q, n - QK_ROPE // 2, 1), pltpu.roll(q, QK_ROPE // 2, 1))
    return q * c + sw * s


def _rope_bwd(dy, c, s):
    n = dy.shape[1]
    t = dy * s
    lane = lax.broadcasted_iota(jnp.int32, dy.shape, 1) % HEAD_SLOT
    sw = jnp.where(lane < KR_LANE + QK_ROPE // 2, pltpu.roll(t, n - QK_ROPE // 2, 1), pltpu.roll(t, QK_ROPE // 2, 1))
    rope_lane = jnp.logical_and(lane >= KR_LANE, lane < KR_LANE + QK_ROPE)
    return dy * c + jnp.where(rope_lane, sw, 0.0)


def _shift_down(x, k, halo):
    xs = pltpu.roll(x, k, 0)
    hs = pltpu.roll(halo, k, 0)
    rows = lax.broadcasted_iota(jnp.int32, halo.shape, 0)
    top = jnp.where(rows < k, hs, xs[0:SUBLANES])
    return jnp.concatenate([top, xs[SUBLANES:]], axis=0)


def _shift_up(x, k, halo):
    t = x.shape[0]
    xs = pltpu.roll(x, t - k, 0)
    hs = pltpu.roll(halo, SUBLANES - k, 0)
    rows = lax.broadcasted_iota(jnp.int32, halo.shape, 0)
    bot = jnp.where(rows >= SUBLANES - k, hs, xs[t - SUBLANES:])
    return jnp.concatenate([xs[:t - SUBLANES], bot], axis=0)


def _mm(a, b, name, out_dtype=F32, bt=False, tm_cap=1024, tn_cap=1408):
    m, k = a.shape[-2:]
    parts = a.shape[0] if a.ndim == 3 else 1
    n = b.shape[0] if bt else b.shape[1]
    tm = min(tm_cap, m)
    tn = _tile(n, tn_cap)

    def body(a_ref, b_ref, o_ref):
        if not bt:
            o_ref[...] = _dot(a_ref[...], b_ref[...]).astype(out_dtype)
        elif parts == 1:
            o_ref[...] = _dot_nt(a_ref[...], b_ref[...]).astype(out_dtype)
        else:
            @pl.when(pl.program_id(2) == 0)
            def _():
                o_ref[...] = jnp.zeros_like(o_ref)

            o_ref[...] += _dot_nt(a_ref[...], b_ref[...])

    if bt:
        b_spec = pl.BlockSpec((tn, k), lambda j, i, s: (j, s))
    else:
        b_spec = pl.BlockSpec((k, tn), lambda j, i, s: (0, j))
    if a.ndim == 3:
        assert bt and out_dtype == F32
        a_spec = pl.BlockSpec((None, tm, k), lambda j, i, s: (s, i, 0))
    else:
        a_spec = pl.BlockSpec((tm, k), lambda j, i, s: (i, 0))
    return pl.pallas_call(
        body, name=name, grid=(n // tn, m // tm, parts),
        in_specs=[a_spec, b_spec],
        out_specs=pl.BlockSpec((tm, tn), lambda j, i, s: (i, j)),
        out_shape=jax.ShapeDtypeStruct((m, n), out_dtype),
        compiler_params=_params(("parallel", "parallel", "arbitrary")),
    )(a, b)


def _mm_tn(a, b, name, tk_cap=1024, tn_cap=1664, tl_cap=2048, chips=False):
    l, k = a.shape
    tk = _tile(k, tk_cap)
    tl = min(tl_cap, l)

    def body(a_ref, b_ref, o_ref):
        @pl.when(pl.program_id(2) == 0)
        def _():
            o_ref[...] = jnp.zeros_like(o_ref)

        o_ref[...] += _dot_tn(a_ref[...], b_ref[...])

    if chips:
        n = b.shape[-1] * (b.shape[0] if b.ndim == 3 else 1)
        tn = n // N_CHIPS
        assert tn % LANES == 0
        if b.ndim == 3:
            per = N_CHIPS // b.shape[0]
            b_spec = pl.BlockSpec((None, tl, tn), lambda i, j, r: (j // per, r, j % per))
        else:
            b_spec = pl.BlockSpec((tl, tn), lambda i, j, r: (r, j))
        out_spec = pl.BlockSpec((None, tk, tn), lambda i, j, r: (j, i, 0))
        out_shape = jax.ShapeDtypeStruct((N_CHIPS, k, tn), F32)
    else:
        n = b.shape[1]
        tn = _tile(n, tn_cap)
        b_spec = pl.BlockSpec((tl, tn), lambda i, j, r: (r, j))
        out_spec = pl.BlockSpec((tk, tn), lambda i, j, r: (i, j))
        out_shape = jax.ShapeDtypeStruct((k, n), F32)
    return pl.pallas_call(
        body, name=name, grid=(k // tk, n // tn, l // tl),
        in_specs=[pl.BlockSpec((tl, tk), lambda i, j, r: (r, i)), b_spec],
        out_specs=out_spec, out_shape=out_shape,
        compiler_params=_params(("parallel", "parallel", "arbitrary")),
    )(a, b)


def _row(tl, n):
    return pl.BlockSpec((tl, n), lambda i: (i, 0))


def _const(shape):
    return pl.BlockSpec(shape, lambda i: tuple(0 for _ in shape))


def _proj_fwd(x, g1, win, gq, wuq, gkv, wukv, rc, rs, bg, tl):
    l = x.shape[0]

    def body(x_ref, g1_ref, win_ref, gq_ref, wuq_ref, gkv_ref, wukv_ref, rc_ref, rs_ref, bg_ref,
             hn_ref, cq_ref, ckv_ref, q_ref, k_ref, v_ref, u_ref, gl_ref):
        hn, _ = _rms(x_ref[...], g1_ref[...])
        hnb = hn.astype(BF16)
        hn_ref[...] = hnb
        proj = _dot(hnb, win_ref[...])
        cq = proj[:, P_CQ:P_CKV]
        ckv = proj[:, P_CKV:P_KR]
        kr = proj[:, P_KR:P_U]
        cq_ref[...] = cq
        ckv_ref[...] = ckv
        u_ref[...] = proj[:, P_U:P_GL]
        gl_ref[...] = (proj[:, P_GL:P_END] + bg_ref[...]).astype(BF16)
        qn, _ = _rms(cq, gq_ref[...])
        q = _dot(qn.astype(BF16), wuq_ref[...])
        c1 = rc_ref[...]
        s1 = rs_ref[...]
        q_ref[...] = (_rope(q, jnp.tile(c1, (1, N_HEADS)), jnp.tile(s1, (1, N_HEADS))) * Q_PRESCALE).astype(BF16)
        ckvn, _ = _rms(ckv, gkv_ref[...])
        kv = _dot(ckvn.astype(BF16), wukv_ref[...])
        krr = _rope(kr, c1, s1)
        k_ref[...] = (kv[:, :HP] + jnp.tile(krr, (1, N_HEADS))).astype(BF16)
        v_ref[...] = kv[:, HP:].astype(BF16)

    outs = [(D_MODEL, BF16), (Q_RANK, F32), (KV_RANK, F32), (HP, BF16), (HP, BF16), (HP, BF16),
            (SSM_W, F32), (2 * D_MODEL, BF16)]
    return pl.pallas_call(
        body, name="proj_fwd", grid=(l // tl,),
        in_specs=[_row(tl, D_MODEL), _const((1, D_MODEL)), _const((D_MODEL, P_END)), _const((1, Q_RANK)),
                  _const((Q_RANK, HP)), _const((1, KV_RANK)), _const((KV_RANK, 2 * HP)),
                  _row(tl, HEAD_SLOT), _row(tl, HEAD_SLOT), _const((1, 2 * D_MODEL))],
        out_specs=[_row(tl, n) for n, _ in outs],
        out_shape=[jax.ShapeDtypeStruct((l, n), dt) for n, dt in outs],
        compiler_params=_params(("parallel",)),
    )(x, g1, win, gq, wuq, gkv, wukv, rc, rs, bg)


_NEG = -1e30


LOG2E = 1.0 / math.log(2.0)
LN2 = math.log(2.0)
ATTN_SCALE = 1.0 / math.sqrt(QK_HEAD)
Q_PRESCALE = ATTN_SCALE * LOG2E
FWD_HEADS = 8
BWD_HEADS = 8


def _causal_pairs(nq, by_query):
    if by_query:
        pairs = [(i, j) for i in range(nq) for j in range(i + 1)]
    else:
        pairs = [(i, j) for j in range(nq) for i in range(j, nq)]
    return jnp.array([p[0] for p in pairs], jnp.int32), jnp.array([p[1] for p in pairs], jnp.int32)


def _diag_mask_t(s):
    rows = lax.broadcasted_iota(jnp.int32, s.shape, 0)
    cols = lax.broadcasted_iota(jnp.int32, s.shape, 1)
    return jnp.where(rows <= cols, s, _NEG)


def _attn_fwd(q, k, v, tq, heads):
    l = q.shape[0]
    nq = l // tq
    it, jt = _causal_pairs(nq, True)

    def body(it_ref, jt_ref, q_ref, k_ref, v_ref, o_ref, lse_ref, m_ref, l_ref, acc_ref):
        t = pl.program_id(1)
        i = it_ref[t]
        j = jt_ref[t]

        @pl.when(j == 0)
        def _():
            m_ref[...] = jnp.full_like(m_ref, _NEG)
            l_ref[...] = jnp.zeros_like(l_ref)
            acc_ref[...] = jnp.zeros_like(acc_ref)

        def update(on_diagonal):
            for hh in range(heads):
                sl = slice(hh * HEAD_SLOT, (hh + 1) * HEAD_SLOT)
                s = _dot_nt(k_ref[:, sl], q_ref[:, sl])
                if on_diagonal:
                    s = _diag_mask_t(s)
                m_old = m_ref[hh]
                m_new = jnp.maximum(m_old, jnp.max(s, axis=0, keepdims=True))
                p = jnp.exp2(s - m_new)
                alpha = jnp.exp2(m_old - m_new)
                l_ref[hh] = alpha * l_ref[hh] + jnp.sum(p, axis=0, keepdims=True)
                acc_ref[hh] = alpha * acc_ref[hh] + _dot_tn(v_ref[:, sl], p.astype(BF16))
                m_ref[hh] = m_new

        @pl.when(j < i)
        def _():
            update(False)

        @pl.when(j == i)
        def _():
            update(True)
            for hh in range(heads):
                sl = slice(hh * HEAD_SLOT, (hh + 1) * HEAD_SLOT)
                o_ref[:, sl] = (acc_ref[hh] / l_ref[hh]).T.astype(BF16)
                lse_ref[hh] = m_ref[hh] + jnp.log(l_ref[hh]) * LOG2E

    blk = (tq, (heads * HEAD_SLOT))
    qmap = lambda h, t, it_ref, jt_ref: (it_ref[t], h)
    kmap = lambda h, t, it_ref, jt_ref: (jt_ref[t], h)
    row = pl.BlockSpec((heads, 1, tq), lambda h, t, it_ref, jt_ref: (h, 0, it_ref[t]))
    return pl.pallas_call(
        body, name="attn_fwd",
        grid_spec=pltpu.PrefetchScalarGridSpec(
            num_scalar_prefetch=2, grid=(N_HEADS // heads, it.shape[0]),
            in_specs=[pl.BlockSpec(blk, qmap), pl.BlockSpec(blk, kmap), pl.BlockSpec(blk, kmap)],
            out_specs=[pl.BlockSpec(blk, qmap), row],
            scratch_shapes=[pltpu.VMEM((heads, 1, tq), F32), pltpu.VMEM((heads, 1, tq), F32),
                            pltpu.VMEM((heads, HEAD_SLOT, tq), F32)]),
        out_shape=[jax.ShapeDtypeStruct((l, HP), BF16), jax.ShapeDtypeStruct((N_HEADS, 1, l), F32)],
        compiler_params=_params(("parallel", "arbitrary")),
    )(it, jt, q, k, v)


def _attn_delta(o, do, tq, heads):
    l = o.shape[0]

    def body(o_ref, do_ref, d_ref):
        prod = o_ref[...].astype(F32) * do_ref[...].astype(F32)
        for hh in range(heads):
            d_ref[hh] = jnp.sum(prod[:, hh * HEAD_SLOT:(hh + 1) * HEAD_SLOT].T, axis=0, keepdims=True)

    blk = pl.BlockSpec((tq, (heads * HEAD_SLOT)), lambda h, i: (i, h))
    return pl.pallas_call(
        body, name="attn_delta", grid=(N_HEADS // heads, l // tq), in_specs=[blk, blk],
        out_specs=pl.BlockSpec((heads, 1, tq), lambda h, i: (h, 0, i)),
        out_shape=jax.ShapeDtypeStruct((N_HEADS, 1, l), F32),
        compiler_params=_params(("parallel", "parallel")),
    )(o, do)


def _attn_bwd(q, k, v, do, lse, delta, tq, heads):
    l = q.shape[0]
    nq = l // tq
    it, jt = _causal_pairs(nq, False)

    def body(it_ref, jt_ref, q_ref, k_ref, v_ref, do_ref, lse_ref, dl_ref, dq_ref, dk_ref, dv_ref, dka_ref, dva_ref):
        t = pl.program_id(1)
        i = it_ref[t]
        j = jt_ref[t]

        @pl.when(t == 0)
        def _():
            dq_ref[...] = jnp.zeros_like(dq_ref)

        @pl.when(i == j)
        def _():
            dka_ref[...] = jnp.zeros_like(dka_ref)
            dva_ref[...] = jnp.zeros_like(dva_ref)

        def update(on_diagonal):
            r0 = pl.multiple_of(i * tq, tq)
            for hh in range(heads):
                sl = slice(hh * HEAD_SLOT, (hh + 1) * HEAD_SLOT)
                qb = q_ref[:, sl]
                kb = k_ref[:, sl]
                dob = do_ref[:, sl]
                s = _dot_nt(kb, qb)
                if on_diagonal:
                    s = _diag_mask_t(s)
                p = jnp.exp2(s - lse_ref[hh])
                dva_ref[:, sl] += _dot(p.astype(BF16), dob)
                dp = _dot_nt(v_ref[:, sl], dob)
                ds = (p * (dp - dl_ref[hh])).astype(BF16)
                dka_ref[:, sl] += _dot(ds, qb)
                dq_ref[pl.ds(r0, tq), sl] += ATTN_SCALE * _dot_tn(ds, kb)

        @pl.when(j < i)
        def _():
            update(False)

        @pl.when(j == i)
        def _():
            update(True)

        @pl.when(i == nq - 1)
        def _():
            dk_ref[...] = (dka_ref[...] * LN2).astype(BF16)
            dv_ref[...] = dva_ref[...].astype(BF16)

    blk = (tq, (heads * HEAD_SLOT))
    qmap = lambda h, t, it_ref, jt_ref: (it_ref[t], h)
    kmap = lambda h, t, it_ref, jt_ref: (jt_ref[t], h)
    row = pl.BlockSpec((heads, 1, tq), lambda h, t, it_ref, jt_ref: (h, 0, it_ref[t]))
    return pl.pallas_call(
        body, name="attn_bwd",
        grid_spec=pltpu.PrefetchScalarGridSpec(
            num_scalar_prefetch=2, grid=(N_HEADS // heads, it.shape[0]),
            in_specs=[pl.BlockSpec(blk, qmap), pl.BlockSpec(blk, kmap), pl.BlockSpec(blk, kmap),
                      pl.BlockSpec(blk, qmap), row, row],
            out_specs=[pl.BlockSpec((l, (heads * HEAD_SLOT)), lambda h, t, it_ref, jt_ref: (0, h)), pl.BlockSpec(blk, kmap),
                       pl.BlockSpec(blk, kmap)],
            scratch_shapes=[pltpu.VMEM(blk, F32), pltpu.VMEM(blk, F32)]),
        out_shape=[jax.ShapeDtypeStruct((l, HP), F32), jax.ShapeDtypeStruct((l, HP), BF16),
                   jax.ShapeDtypeStruct((l, HP), BF16)],
        compiler_params=_params(("parallel", "arbitrary")),
    )(it, jt, q, k, v, do, lse, delta)


SSM_CB = 512
SSM_UB = 128
SSM_NB = SSM_CH // SSM_CB


def _scan_tiles(re_ref, im_ref, tab, carry, n_tiles, reverse):
    group = 2
    assert n_tiles % group == 0
    pr, pi = tab[6], tab[7]

    def inside(sr, si):
        for step, k in enumerate((1, 2, 4)):
            mr, mi = tab[2 * step], tab[2 * step + 1]
            sh = (SUBLANES - k) if reverse else k
            rr = pltpu.roll(sr, sh, 0)
            ri = pltpu.roll(si, sh, 0)
            sr, si = sr + mr * rr - mi * ri, si + mr * ri + mi * rr
        return sr, si

    def body(n, c):
        cr, ci = c
        first = (n_tiles - group * (n + 1)) if reverse else group * n
        r0 = pl.multiple_of(first * SUBLANES, group * SUBLANES)
        rows = [pl.ds(r0 + g * SUBLANES, SUBLANES) for g in range(group)]
        tiles = [inside(re_ref[r, :], im_ref[r, :]) for r in rows]
        for g in (range(group - 1, -1, -1) if reverse else range(group)):
            sr, si = tiles[g]
            sr, si = sr + pr * cr - pi * ci, si + pr * ci + pi * cr
            re_ref[rows[g], :] = sr
            im_ref[rows[g], :] = si
            edge = slice(0, 1) if reverse else slice(SUBLANES - 1, SUBLANES)
            cr, ci = sr[edge, :], si[edge, :]
        return cr, ci

    return lax.fori_loop(0, n_tiles // group, body, carry)


def _ssm_fwd(u, bre, bim, cre, cim, dvec, tab, tt):
    l = u.shape[0]
    nt = l // tt

    def body(u_ref, bre_ref, bim_ref, cre_ref, cim_ref, d_ref, tab_ref, y_ref, sre_ref, sim_ref, car_ref):
        @pl.when(pl.program_id(1) == 0)
        def _():
            car_ref[...] = jnp.zeros_like(car_ref)

        uf = u_ref[...]
        ub = uf.astype(BF16)
        sre_ref[...] = _dot(ub, bre_ref[0])
        sim_ref[...] = _dot(ub, bim_ref[0])
        tab_v = [tab_ref[n] for n in range(8)]
        cr, ci = _scan_tiles(sre_ref, sim_ref, tab_v, (car_ref[0:1, :], car_ref[8:9, :]), tt // SUBLANES, False)
        car_ref[0:1, :] = cr
        car_ref[8:9, :] = ci
        y_ref[...] = (_dot(sre_ref[...].astype(BF16), cre_ref[0]) - _dot(sim_ref[...].astype(BF16), cim_ref[0])
                      + d_ref[...] * uf)

    return pl.pallas_call(
        body, name="ssm_fwd", grid=(SSM_NB, nt),
        in_specs=[pl.BlockSpec((tt, SSM_UB), lambda m, t: (t, m)),
                  pl.BlockSpec((1, SSM_UB, SSM_CB), lambda m, t: (m, 0, 0)),
                  pl.BlockSpec((1, SSM_UB, SSM_CB), lambda m, t: (m, 0, 0)),
                  pl.BlockSpec((1, SSM_CB, SSM_UB), lambda m, t: (m, 0, 0)),
                  pl.BlockSpec((1, SSM_CB, SSM_UB), lambda m, t: (m, 0, 0)),
                  pl.BlockSpec((1, SSM_UB), lambda m, t: (0, m)),
                  pl.BlockSpec((8, SUBLANES, SSM_CB), lambda m, t: (0, 0, m))],
        out_specs=[pl.BlockSpec((tt, SSM_UB), lambda m, t: (t, m)),
                   pl.BlockSpec((tt, SSM_CB), lambda m, t: (t, m)),
                   pl.BlockSpec((tt, SSM_CB), lambda m, t: (t, m))],
        out_shape=[jax.ShapeDtypeStruct((l, SSM_W), F32), jax.ShapeDtypeStruct((l, SSM_CH), F32),
                   jax.ShapeDtypeStruct((l, SSM_CH), F32)],
        scratch_shapes=[pltpu.VMEM((2 * SUBLANES, SSM_CB), F32)],
        compiler_params=_params(("parallel", "arbitrary")),
    )(u, bre, bim, cre, cim, dvec, tab)


def _ssm_bwd(dy, u, sre, sim, bre, bim, cre, cim, dvec, tab, tt):
    l = u.shape[0]
    nt = l // tt
    tpb = tt // SUBLANES

    def body(dy_ref, u_ref, sre_ref, sim_ref, hre_ref, him_ref, bre_ref, bim_ref, cre_ref, cim_ref, d_ref, tab_ref,
             du_ref, dbre_ref, dbim_ref, dcre_ref, dcim_ref, dare_ref, daim_ref, dd_ref, lr_ref, li_ref, car_ref):
        t = pl.program_id(1)

        @pl.when(t == 0)
        def _():
            car_ref[...] = jnp.zeros_like(car_ref)
            for ref in (dbre_ref, dbim_ref, dcre_ref, dcim_ref, dare_ref, daim_ref, dd_ref):
                ref[...] = jnp.zeros_like(ref)

        dyf = dy_ref[...]
        dyb = dyf.astype(BF16)
        uf = u_ref[...]
        s_re = sre_ref[...]
        s_im = sim_ref[...]
        lr_ref[...] = _dot_nt(dyb, cre_ref[0])
        li_ref[...] = -_dot_nt(dyb, cim_ref[0])
        dcre_ref[0] += _dot_tn(s_re.astype(BF16), dyb)
        dcim_ref[0] -= _dot_tn(s_im.astype(BF16), dyb)
        tab_v = [tab_ref[n] for n in range(8)]
        cr, ci = _scan_tiles(lr_ref, li_ref, tab_v, (car_ref[0:1, :], car_ref[8:9, :]), tpb, True)
        car_ref[0:1, :] = cr
        car_ref[8:9, :] = ci
        lam_r = lr_ref[...]
        lam_i = li_ref[...]
        keep = jnp.where(t == nt - 1, 0.0, 1.0)
        sp_r = _shift_down(s_re, 1, hre_ref[...] * keep)
        sp_i = _shift_down(s_im, 1, him_ref[...] * keep)
        dare_ref[...] += jnp.sum(lam_r * sp_r + lam_i * sp_i, axis=0, keepdims=True)
        daim_ref[...] += jnp.sum(lam_i * sp_r - lam_r * sp_i, axis=0, keepdims=True)
        lrb = lam_r.astype(BF16)
        lib = lam_i.astype(BF16)
        du_ref[...] = _dot_nt(lrb, bre_ref[0]) + _dot_nt(lib, bim_ref[0]) + dyf * d_ref[...]
        ub = uf.astype(BF16)
        dbre_ref[0] += _dot_tn(ub, lrb)
        dbim_ref[0] += _dot_tn(ub, lib)
        dd_ref[...] += jnp.sum(dyf * uf, axis=0, keepdims=True)

    rev = lambda m, t: (nt - 1 - t, m)
    halo = lambda m, t: (jnp.maximum((nt - 1 - t) * tpb - 1, 0), m)
    wb = pl.BlockSpec((1, SSM_UB, SSM_CB), lambda m, t: (m, 0, 0))
    wc = pl.BlockSpec((1, SSM_CB, SSM_UB), lambda m, t: (m, 0, 0))
    vec_c = pl.BlockSpec((1, SSM_CB), lambda m, t: (0, m))
    vec_u = pl.BlockSpec((1, SSM_UB), lambda m, t: (0, m))
    return pl.pallas_call(
        body, name="ssm_bwd", grid=(SSM_NB, nt),
        in_specs=[pl.BlockSpec((tt, SSM_UB), rev), pl.BlockSpec((tt, SSM_UB), rev),
                  pl.BlockSpec((tt, SSM_CB), rev), pl.BlockSpec((tt, SSM_CB), rev),
                  pl.BlockSpec((SUBLANES, SSM_CB), halo), pl.BlockSpec((SUBLANES, SSM_CB), halo),
                  wb, wb, wc, wc, vec_u,
                  pl.BlockSpec((8, SUBLANES, SSM_CB), lambda m, t: (0, 0, m))],
        out_specs=[pl.BlockSpec((tt, SSM_UB), rev), wb, wb, wc, wc, vec_c, vec_c, vec_u],
        out_shape=[jax.ShapeDtypeStruct((l, SSM_W), F32),
                   jax.ShapeDtypeStruct((SSM_NB, SSM_UB, SSM_CB), F32), jax.ShapeDtypeStruct((SSM_NB, SSM_UB, SSM_CB), F32),
                   jax.ShapeDtypeStruct((SSM_NB, SSM_CB, SSM_UB), F32), jax.ShapeDtypeStruct((SSM_NB, SSM_CB, SSM_UB), F32),
                   jax.ShapeDtypeStruct((1, SSM_CH), F32), jax.ShapeDtypeStruct((1, SSM_CH), F32),
                   jax.ShapeDtypeStruct((1, SSM_W), F32)],
        scratch_shapes=[pltpu.VMEM((tt, SSM_CB), F32), pltpu.VMEM((tt, SSM_CB), F32),
                        pltpu.VMEM((2 * SUBLANES, SSM_CB), F32)],
        compiler_params=_params(("parallel", "arbitrary")),
    )(dy, u, sre, sim, sre, sim, bre, bim, cre, cim, dvec, tab)


def _merge_fwd(x, gl, attn, y1, wba, wbs, wglu, bglu, wout, gpost, gpre, tl):
    l = x.shape[0]

    def body(x_ref, gl_ref, at_ref, y1_ref, wba_ref, wbs_ref, wglu_ref, bglu_ref, wout_ref, gpost_ref, gpre_ref,
             a_ref, sm_ref, mg_ref, z_ref, x1_ref, hn2_ref, y3_ref):
        y2 = _gelu(y1_ref[...])
        sg = _sigmoid(_dot(y2.astype(BF16), wglu_ref[...]) + bglu_ref[...])
        y3 = (y2 * sg).astype(BF16)
        y3_ref[...] = y3
        a = _dot(at_ref[...], wba_ref[...])
        sm = _dot(y3, wbs_ref[...])
        a_ref[...] = a.astype(BF16)
        sm_ref[...] = sm.astype(BF16)
        g = _sigmoid(gl_ref[...].astype(F32))
        merged = (g[:, :D_MODEL] * a + g[:, D_MODEL:] * sm).astype(BF16)
        mg_ref[...] = merged
        z = _dot(merged, wout_ref[...])
        z_ref[...] = z
        n, _ = _rms(z, gpost_ref[...])
        x1 = x_ref[...] + n
        x1_ref[...] = x1
        hn2, _ = _rms(x1, gpre_ref[...])
        hn2_ref[...] = hn2.astype(BF16)

    outs = [(D_MODEL, BF16), (D_MODEL, BF16), (D_MODEL, BF16), (D_MODEL, F32), (D_MODEL, F32), (D_MODEL, BF16),
            (SSM_W, BF16)]
    return pl.pallas_call(
        body, name="merge_fwd", grid=(l // tl,),
        in_specs=[_row(tl, D_MODEL), _row(tl, 2 * D_MODEL), _row(tl, HP), _row(tl, SSM_W),
                  _const((HP, D_MODEL)), _const((SSM_W, D_MODEL)), _const((SSM_W, SSM_W)), _const((1, SSM_W)),
                  _const((D_MODEL, D_MODEL)), _const((1, D_MODEL)), _const((1, D_MODEL))],
        out_specs=[_row(tl, n) for n, _ in outs],
        out_shape=[jax.ShapeDtypeStruct((l, n), dt) for n, dt in outs],
        compiler_params=_params(("parallel",)),
    )(x, gl, attn, y1, wba, wbs, wglu, bglu, wout, gpost, gpre)


def _merge_bwd(dhn2, x1, dx2, z, gl, a, sm, y1, wba, wbs, wglu, bglu, wout, gpost, gpre, tl):
    l = x1.shape[0]

    def body(dhn2_ref, x1_ref, dx2_ref, z_ref, gl_ref, a_ref, sm_ref, y1_ref,
             wba_ref, wbs_ref, wglu_ref, bglu_ref, wout_ref, gpost_ref, gpre_ref,
             dx1_ref, dz_ref, dbra_ref, dbrs_ref, dgl_ref, dat_ref, dy1_ref, dt_ref, y2_ref,
             dgpre_ref, dgpost_ref, dbg_ref, dbglu_ref):
        @pl.when(pl.program_id(0) == 0)
        def _():
            for ref in (dgpre_ref, dgpost_ref, dbg_ref, dbglu_ref):
                ref[...] = jnp.zeros_like(ref)

        dx1a, dgpre = _rms_bwd(dhn2_ref[...], x1_ref[...], gpre_ref[...])
        dgpre_ref[...] += dgpre
        dx1 = dx2_ref[...] + dx1a
        dx1_ref[...] = dx1
        dz, dgpost = _rms_bwd(dx1, z_ref[...], gpost_ref[...])
        dgpost_ref[...] += dgpost
        dzb = dz.astype(BF16)
        dz_ref[...] = dzb
        dm = _dot_nt(dzb, wout_ref[...])
        g = _sigmoid(gl_ref[...].astype(F32))
        g0 = g[:, :D_MODEL]
        g1 = g[:, D_MODEL:]
        dbra = (dm * g0).astype(BF16)
        dbrs = (dm * g1).astype(BF16)
        dbra_ref[...] = dbra
        dbrs_ref[...] = dbrs
        dgl0 = dm * a_ref[...].astype(F32) * g0 * (1.0 - g0)
        dgl1 = dm * sm_ref[...].astype(F32) * g1 * (1.0 - g1)
        dgl_ref[:, :D_MODEL] = dgl0.astype(BF16)
        dgl_ref[:, D_MODEL:] = dgl1.astype(BF16)
        dbg_ref[:, :D_MODEL] += jnp.sum(dgl0, axis=0, keepdims=True)
        dbg_ref[:, D_MODEL:] += jnp.sum(dgl1, axis=0, keepdims=True)
        dat_ref[...] = _dot_nt(dbra, wba_ref[...]).astype(BF16)
        dy3 = _dot_nt(dbrs, wbs_ref[...])
        y1v = y1_ref[...]
        y2 = _gelu(y1v)
        y2b = y2.astype(BF16)
        y2_ref[...] = y2b
        sg = _sigmoid(_dot(y2b, wglu_ref[...]) + bglu_ref[...])
        dt = dy3 * y2 * sg * (1.0 - sg)
        dtb = dt.astype(BF16)
        dt_ref[...] = dtb
        dbglu_ref[...] += jnp.sum(dt, axis=0, keepdims=True)
        dy2 = dy3 * sg + _dot_nt(dtb, wglu_ref[...])
        dy1_ref[...] = dy2 * _gelu_grad(y1v)

    outs = [(D_MODEL, F32), (D_MODEL, BF16), (D_MODEL, BF16), (D_MODEL, BF16), (2 * D_MODEL, BF16), (HP, BF16),
            (SSM_W, F32), (SSM_W, BF16), (SSM_W, BF16)]
    accs = [D_MODEL, D_MODEL, 2 * D_MODEL, SSM_W]
    return pl.pallas_call(
        body, name="merge_bwd", grid=(l // tl,),
        in_specs=[_row(tl, D_MODEL), _row(tl, D_MODEL), _row(tl, D_MODEL), _row(tl, D_MODEL),
                  _row(tl, 2 * D_MODEL), _row(tl, D_MODEL), _row(tl, D_MODEL), _row(tl, SSM_W),
                  _const((HP, D_MODEL)), _const((SSM_W, D_MODEL)), _const((SSM_W, SSM_W)), _const((1, SSM_W)),
                  _const((D_MODEL, D_MODEL)), _const((1, D_MODEL)), _const((1, D_MODEL))],
        out_specs=[_row(tl, n) for n, _ in outs] + [_const((1, n)) for n in accs],
        out_shape=[jax.ShapeDtypeStruct((l, n), dt) for n, dt in outs]
        + [jax.ShapeDtypeStruct((1, n), F32) for n in accs],
        compiler_params=_params(("arbitrary",)),
    )(dhn2, x1, dx2, z, gl, a, sm, y1, wba, wbs, wglu, bglu, wout, gpost, gpre)


def _proj_bwd(x, dx1, cq, ckv, dq, dk, dv, du, dgl, g1, win, gq, wuq, gkv, wukv, rc, rs, tl):
    l = x.shape[0]

    def body(x_ref, dx1_ref, cq_ref, ckv_ref, dq_ref, dk_ref, dv_ref, du_ref, dgl_ref,
             g1_ref, win_ref, gq_ref, wuq_ref, gkv_ref, wukv_ref, rc_ref, rs_ref,
             gx_ref, dql_ref, qn_ref, ckvn_ref, dproj_ref, dg1_ref, dgq_ref, dgkv_ref):
        @pl.when(pl.program_id(0) == 0)
        def _():
            for ref in (dg1_ref, dgq_ref, dgkv_ref):
                ref[...] = jnp.zeros_like(ref)

        c1 = rc_ref[...]
        s1 = rs_ref[...]
        dql = _rope_bwd(dq_ref[...], jnp.tile(c1, (1, N_HEADS)), jnp.tile(s1, (1, N_HEADS))).astype(BF16)
        dql_ref[...] = dql
        dqn = _dot_nt(dql, wuq_ref[...])
        cq = cq_ref[...]
        qn, _ = _rms(cq, gq_ref[...])
        qn_ref[...] = qn.astype(BF16)
        dcq, dgq = _rms_bwd(dqn, cq, gq_ref[...])
        dgq_ref[...] += dgq
        dkb = dk_ref[...]
        dvb = dv_ref[...]
        dkf = dkb.astype(F32)
        dkr = dkf[:, 0:HEAD_SLOT]
        for h in range(1, N_HEADS):
            dkr = dkr + dkf[:, h * HEAD_SLOT:(h + 1) * HEAD_SLOT]
        dkr = _rope_bwd(dkr, c1, s1)
        dckvn = _dot_nt(dkb, wukv_ref[:, :HP]) + _dot_nt(dvb, wukv_ref[:, HP:])
        ckv = ckv_ref[...]
        ckvn, _ = _rms(ckv, gkv_ref[...])
        ckvn_ref[...] = ckvn.astype(BF16)
        dckv, dgkv = _rms_bwd(dckvn, ckv, gkv_ref[...])
        dgkv_ref[...] += dgkv
        dproj_ref[:, P_CQ:P_CKV] = dcq.astype(BF16)
        dproj_ref[:, P_CKV:P_KR] = dckv.astype(BF16)
        dproj_ref[:, P_KR:P_U] = dkr.astype(BF16)
        dproj_ref[:, P_U:P_GL] = du_ref[...].astype(BF16)
        dproj_ref[:, P_GL:P_END] = dgl_ref[...]
        dhn = _dot_nt(dproj_ref[...], win_ref[...])
        dxa, dg1 = _rms_bwd(dhn, x_ref[...], g1_ref[...])
        dg1_ref[...] += dg1
        gx_ref[...] = dx1_ref[...] + dxa

    outs = [(D_MODEL, F32), (HP, BF16), (Q_RANK, BF16), (KV_RANK, BF16), (P_END, BF16)]
    accs = [D_MODEL, Q_RANK, KV_RANK]
    return pl.pallas_call(
        body, name="proj_bwd", grid=(l // tl,),
        in_specs=[_row(tl, D_MODEL), _row(tl, D_MODEL), _row(tl, Q_RANK), _row(tl, KV_RANK), _row(tl, HP),
                  _row(tl, HP), _row(tl, HP), _row(tl, SSM_W), _row(tl, 2 * D_MODEL),
                  _const((1, D_MODEL)), _const((D_MODEL, P_END)), _const((1, Q_RANK)), _const((Q_RANK, HP)),
                  _const((1, KV_RANK)), _const((KV_RANK, 2 * HP)), _row(tl, HEAD_SLOT), _row(tl, HEAD_SLOT)],
        out_specs=[_row(tl, n) for n, _ in outs] + [_const((1, n)) for n in accs],
        out_shape=[jax.ShapeDtypeStruct((l, n), dt) for n, dt in outs]
        + [jax.ShapeDtypeStruct((1, n), F32) for n in accs],
        compiler_params=_params(("arbitrary",)),
    )(x, dx1, cq, ckv, dq, dk, dv, du, dgl, g1, win, gq, wuq, gkv, wukv, rc, rs)


CONV_CB = 256
CONV_NB = D_FF // CONV_CB
CONV_ROWS = 16


def _conv3(h, halo, w, b):
    return b + w[0:1, :] * _shift_down(h, 2, halo) + w[1:2, :] * _shift_down(h, 1, halo) + w[2:3, :] * h


def _conv_fwd(h, cw, cb, tl):
    l = h.shape[0]

    def body(hg_ref, hv_ref, wg_ref, wv_ref, bg_ref, bv_ref, act_ref, halo_ref):
        @pl.when(pl.program_id(1) == 0)
        def _():
            halo_ref[...] = jnp.zeros_like(halo_ref)

        hg = hg_ref[...]
        hv = hv_ref[...]
        cg = _conv3(hg, halo_ref[0:SUBLANES, :], wg_ref[...], bg_ref[...])
        cv = _conv3(hv, halo_ref[SUBLANES:, :], wv_ref[...], bv_ref[...])
        act_ref[...] = (_gelu(cg) * cv).astype(BF16)
        halo_ref[0:SUBLANES, :] = hg[tl - SUBLANES:, :]
        halo_ref[SUBLANES:, :] = hv[tl - SUBLANES:, :]

    gmap = lambda c, r: (r, c)
    vmap = lambda c, r: (r, CONV_NB + c)
    return pl.pallas_call(
        body, name="conv_fwd", grid=(CONV_NB, l // tl),
        in_specs=[pl.BlockSpec((tl, CONV_CB), gmap), pl.BlockSpec((tl, CONV_CB), vmap),
                  pl.BlockSpec((3, CONV_CB), lambda c, r: (0, c)), pl.BlockSpec((3, CONV_CB), lambda c, r: (0, CONV_NB + c)),
                  pl.BlockSpec((1, CONV_CB), lambda c, r: (0, c)), pl.BlockSpec((1, CONV_CB), lambda c, r: (0, CONV_NB + c))],
        out_specs=pl.BlockSpec((tl, CONV_CB), gmap),
        out_shape=jax.ShapeDtypeStruct((l, D_FF), BF16),
        scratch_shapes=[pltpu.VMEM((2 * SUBLANES, CONV_CB), F32)],
        compiler_params=_params(("parallel", "arbitrary")),
    )(h, h, cw, cw, cb, cb)


def _conv_bwd(h, dact, cw, cb, tl):
    l = h.shape[0]
    nr = l // tl
    tpb = tl // SUBLANES

    def body(hg_ref, hv_ref, hgh_ref, hvh_ref, da_ref, wg_ref, wv_ref, bg_ref, bv_ref,
             dh_ref, dwg_ref, dwv_ref, dbg_ref, dbv_ref, car_ref):
        r = pl.program_id(1)

        @pl.when(r == 0)
        def _():
            for ref in (car_ref, dwg_ref, dwv_ref, dbg_ref, dbv_ref):
                ref[...] = jnp.zeros_like(ref)

        keep = jnp.where(r == nr - 1, 0.0, 1.0)
        wg, wv, bg, bv = wg_ref[...], wv_ref[...], bg_ref[...], bv_ref[...]
        nch = tl // CONV_ROWS

        def fold(x):
            s = x[0:SUBLANES, :]
            for k in range(1, CONV_ROWS // SUBLANES):
                s = s + x[k * SUBLANES:(k + 1) * SUBLANES, :]
            return s

        def chunk(n, carry):
            ncg, ncv, acc = carry
            idx = nch - 1 - n
            r0 = pl.multiple_of(idx * CONV_ROWS, CONV_ROWS)
            rows = pl.ds(r0, CONV_ROWS)
            before = pl.ds(pl.multiple_of(jnp.maximum(r0 - SUBLANES, 0), SUBLANES), SUBLANES)
            in_tile = idx > 0
            da = da_ref[rows, :].astype(F32)

            def half(h_ref, halo_ref, w, b):
                hh = h_ref[rows, :]
                prev = jnp.where(in_tile, h_ref[before, :], halo_ref[...] * keep)
                h1 = _shift_down(hh, 1, prev)
                h2 = _shift_down(hh, 2, prev)
                return hh, h1, h2, b + w[0:1, :] * h2 + w[1:2, :] * h1 + w[2:3, :] * hh

            hg, hg1, hg2, cg = half(hg_ref, hgh_ref, wg, bg)
            hv, hv1, hv2, cv = half(hv_ref, hvh_ref, wv, bv)
            dcg = da * cv * _gelu_grad(cg)
            dcv = da * _gelu(cg)

            def back(dc, hh, h1, h2, w, nxt, part):
                dh = w[2:3, :] * dc + w[1:2, :] * _shift_up(dc, 1, nxt) + w[0:1, :] * _shift_up(dc, 2, nxt)
                dh_ref[part, rows, :] = dh.astype(BF16)
                return [fold(dc * h2), fold(dc * h1), fold(dc * hh), fold(dc)]

            sums = back(dcg, hg, hg1, hg2, wg, ncg, 0) + back(dcv, hv, hv1, hv2, wv, ncv, 1)
            return dcg[0:SUBLANES, :], dcv[0:SUBLANES, :], [a + s for a, s in zip(acc, sums)]

        zero = jnp.zeros((SUBLANES, CONV_CB), F32)
        ncg, ncv, acc = lax.fori_loop(0, nch, chunk, (car_ref[0:SUBLANES, :], car_ref[SUBLANES:, :], [zero] * 8))
        car_ref[0:SUBLANES, :] = ncg
        car_ref[SUBLANES:, :] = ncv
        for half_acc, dw_ref, db_ref in ((acc[0:4], dwg_ref, dbg_ref), (acc[4:8], dwv_ref, dbv_ref)):
            for k in range(3):
                dw_ref[k:k + 1, :] += jnp.sum(half_acc[k], axis=0, keepdims=True)
            db_ref[...] += jnp.sum(half_acc[3], axis=0, keepdims=True)

    grev = lambda c, r: (nr - 1 - r, c)
    vrev = lambda c, r: (nr - 1 - r, CONV_NB + c)
    ghalo = lambda c, r: (jnp.maximum((nr - 1 - r) * tpb - 1, 0), c)
    vhalo = lambda c, r: (jnp.maximum((nr - 1 - r) * tpb - 1, 0), CONV_NB + c)
    colg = lambda c, r: (0, c)
    colv = lambda c, r: (0, CONV_NB + c)
    return pl.pallas_call(
        body, name="conv_bwd", grid=(CONV_NB, nr),
        in_specs=[pl.BlockSpec((tl, CONV_CB), grev), pl.BlockSpec((tl, CONV_CB), vrev),
                  pl.BlockSpec((SUBLANES, CONV_CB), ghalo), pl.BlockSpec((SUBLANES, CONV_CB), vhalo),
                  pl.BlockSpec((tl, CONV_CB), grev),
                  pl.BlockSpec((3, CONV_CB), colg), pl.BlockSpec((3, CONV_CB), colv),
                  pl.BlockSpec((1, CONV_CB), colg), pl.BlockSpec((1, CONV_CB), colv)],
        out_specs=[pl.BlockSpec((2, tl, CONV_CB), lambda c, r: (0, nr - 1 - r, c)),
                   pl.BlockSpec((3, CONV_CB), colg), pl.BlockSpec((3, CONV_CB), colg),
                   pl.BlockSpec((1, CONV_CB), colg), pl.BlockSpec((1, CONV_CB), colg)],
        out_shape=[jax.ShapeDtypeStruct((2, l, D_FF), BF16),
                   jax.ShapeDtypeStruct((3, D_FF), F32), jax.ShapeDtypeStruct((3, D_FF), F32),
                   jax.ShapeDtypeStruct((1, D_FF), F32), jax.ShapeDtypeStruct((1, D_FF), F32)],
        scratch_shapes=[pltpu.VMEM((2 * SUBLANES, CONV_CB), F32)],
        compiler_params=_params(("parallel", "arbitrary")),
    )(h, h, h, h, dact, cw, cw, cb, cb)


def _loss_head(ff, x1, tgt, g, tl):
    l = ff.shape[0]

    def body(ff_ref, x1_ref, tg_ref, g_ref, loss_ref, dx2_ref, dff_ref, dg_ref):
        @pl.when(pl.program_id(0) == 0)
        def _():
            loss_ref[...] = jnp.zeros_like(loss_ref)
            dg_ref[...] = jnp.zeros_like(dg_ref)

        f = ff_ref[...]
        gv = g_ref[...]
        n, _ = _rms(f, gv)
        e = x1_ref[...] + n - tg_ref[...]
        loss_ref[...] += 0.5 * jnp.sum(jnp.mean(e * e, axis=-1, keepdims=True), axis=0, keepdims=True)
        dx2 = e * (1.0 / D_MODEL)
        dx2_ref[...] = dx2
        dff, dg = _rms_bwd(dx2, f, gv)
        dff_ref[...] = dff.astype(BF16)
        dg_ref[...] += dg

    return pl.pallas_call(
        body, name="loss_head", grid=(l // tl,),
        in_specs=[_row(tl, D_MODEL), _row(tl, D_MODEL), _row(tl, D_MODEL), _const((1, D_MODEL))],
        out_specs=[_const((1, LANES)), _row(tl, D_MODEL), _row(tl, D_MODEL), _const((1, D_MODEL))],
        out_shape=[jax.ShapeDtypeStruct((1, LANES), F32), jax.ShapeDtypeStruct((l, D_MODEL), F32),
                   jax.ShapeDtypeStruct((l, D_MODEL), BF16), jax.ShapeDtypeStruct((1, D_MODEL), F32)],
        compiler_params=_params(("arbitrary",)),
    )(ff, x1, tgt, g)


def _ssm_disc(lam_re, lam_im, log_dt, b_re, b_im):
    dt = jnp.exp(log_dt)[:, None]
    mag = jnp.exp(lam_re * dt)
    ang = lam_im * dt
    a_re, a_im = mag * jnp.cos(ang), mag * jnp.sin(ang)
    den = lam_re * lam_re + lam_im * lam_im
    n_re, n_im = a_re - 1.0, a_im
    z_re = (n_re * lam_re + n_im * lam_im) / den
    z_im = (n_im * lam_re - n_re * lam_im) / den
    bb_re = z_re[..., None] * b_re - z_im[..., None] * b_im
    bb_im = z_re[..., None] * b_im + z_im[..., None] * b_re
    return a_re, a_im, bb_re, bb_im


_GPB = SSM_CB // SSM_P


def _embed_b(bb):
    t = bb.transpose(0, 2, 1).reshape(SSM_NB, _GPB, SSM_H, SSM_P)
    return jnp.einsum('mjhp,jk->mjhkp', t, jnp.eye(_GPB, dtype=bb.dtype)).reshape(SSM_NB, SSM_UB, SSM_CB)


def _extract_b(d):
    t = d.reshape(SSM_NB, _GPB, SSM_H, _GPB, SSM_P)
    t = jnp.einsum('mjhkp,jk->mjhp', t, jnp.eye(_GPB, dtype=d.dtype))
    return t.reshape(SSM_G, SSM_H, SSM_P).transpose(0, 2, 1)


def _embed_c(c):
    t = c.transpose(0, 2, 1).reshape(SSM_NB, _GPB, SSM_P, SSM_H)
    return jnp.einsum('mjph,jk->mjpkh', t, jnp.eye(_GPB, dtype=c.dtype)).reshape(SSM_NB, SSM_CB, SSM_UB)


def _extract_c(d):
    t = d.reshape(SSM_NB, _GPB, SSM_P, _GPB, SSM_H)
    t = jnp.einsum('mjpkh,jk->mjph', t, jnp.eye(_GPB, dtype=d.dtype))
    return t.reshape(SSM_G, SSM_P, SSM_H).transpose(0, 2, 1)


def _scan_tables(a_re, a_im, reverse):
    ar = a_re.reshape(1, SSM_CH)
    ai = (-a_im if reverse else a_im).reshape(1, SSM_CH)
    pr, pi = [ar], [ai]
    for _ in range(SUBLANES - 1):
        pr, pi = pr + [pr[-1] * ar - pi[-1] * ai], pi + [pr[-1] * ai + pi[-1] * ar]
    rows = jnp.arange(SUBLANES)[:, None]
    out = []
    for k in (1, 2, 4):
        valid = (rows + k <= SUBLANES - 1) if reverse else (rows >= k)
        out += [jnp.where(valid, pr[k - 1], 0.0), jnp.where(valid, pi[k - 1], 0.0)]
    order = list(range(SUBLANES - 1, -1, -1)) if reverse else list(range(SUBLANES))
    out += [jnp.concatenate([pr[n] for n in order], axis=0), jnp.concatenate([pi[n] for n in order], axis=0)]
    return jnp.stack(out).astype(F32)


def _pad_heads(w, d):
    lead = w.shape[:-1]
    w = w.reshape(lead + (N_HEADS, d))
    w = jnp.pad(w, [(0, 0)] * len(lead) + [(0, 0), (0, HEAD_SLOT - d)])
    return w.reshape(lead + (HP,))


def _unpad_heads(w, d):
    lead = w.shape[:-1]
    return w.reshape(lead + (N_HEADS, HEAD_SLOT))[..., :d].reshape(lead + (N_HEADS * d,))


def _chip_major(w, axis):
    k, n = w.shape
    if axis == 0:
        return w.reshape(N_CHIPS, k // N_CHIPS, n)
    return w.reshape(k, N_CHIPS, n // N_CHIPS).transpose(1, 0, 2)


def _from_chip_major(w, axis):
    if axis == 0:
        return w.reshape(-1, w.shape[2])
    return w.transpose(1, 0, 2).reshape(w.shape[1], -1)


def _pad_w_in(w):
    z = lambda n: jnp.zeros((w.shape[0], n), w.dtype)
    return jnp.concatenate([w[:, :640], z(KR_LANE), w[:, 640:672], z(HEAD_SLOT - KR_LANE - QK_ROPE), w[:, 672:]], axis=1)


def _unpad_w_in(w):
    return jnp.concatenate([w[:, :640], w[:, P_KR + KR_LANE:P_KR + KR_LANE + QK_ROPE], w[:, P_U:]], axis=1)


def _local_step(x, positions, tgt, wts, sp):
    l = x.shape[0]
    tl = min(512, l)
    tb = min(256, l)
    ta = min(512, l)
    ts = min(2048, l)

    inv_freq = ROPE_THETA ** (-jnp.arange(0, QK_ROPE, 2, dtype=F32) / QK_ROPE)
    ang = positions.astype(F32)[:, None] * inv_freq
    cos, sin = jnp.cos(ang), jnp.sin(ang)
    one = jnp.ones((l, KR_LANE), F32)
    rc = jnp.concatenate([one, cos, cos, jnp.ones((l, HEAD_SLOT - KR_LANE - QK_ROPE), F32)], axis=1)
    rs = jnp.concatenate([0 * one, -sin, sin, jnp.zeros((l, HEAD_SLOT - KR_LANE - QK_ROPE), F32)], axis=1)

    win = _pad_w_in(wts["w_in"])
    wuq = _pad_heads(wts["w_uq"], QK_HEAD)
    wukv = jnp.concatenate([_pad_heads(wts["w_uk"], QK_NOPE), _pad_heads(wts["w_uv"], V_HEAD)], axis=1)

    disc_in = (sp["ssm_lambda_re"], sp["ssm_lambda_im"], sp["ssm_log_dt"], sp["ssm_b_re"], sp["ssm_b_im"])
    (a_re, a_im, bb_re, bb_im), disc_vjp = jax.vjp(_ssm_disc, *disc_in)
    bre, bim = _embed_b(bb_re).astype(BF16), _embed_b(bb_im).astype(BF16)
    cre, cim = _embed_c(sp["ssm_c_re"]).astype(BF16), _embed_c(sp["ssm_c_im"]).astype(BF16)
    dvec = sp["ssm_d"].reshape(1, SSM_W)
    tab_f = _scan_tables(a_re, a_im, False)
    tab_r = _scan_tables(a_re, a_im, True)

    g1, gq, gkv = sp["mix_norm_pre"], sp["q_norm"], sp["kv_norm"]
    gpost, gpre, gfin = sp["mix_norm_post"], sp["ffn_norm_pre"], sp["ffn_norm_post"]
    bgate, bglu, convb = sp["b_gate"], sp["b_glu"], sp["conv_b"]

    hn, cq, ckv, q, k, v, u, gl = _proj_fwd(x, g1, win, gq, wuq, gkv, wukv, rc, rs, bgate, tl)
    attn, lse = _attn_fwd(q, k, v, ta, FWD_HEADS)
    y1, sre, sim = _ssm_fwd(u, bre, bim, cre, cim, dvec, tab_f, ts)
    wba = jnp.pad(wts["w_branch_attn"].reshape(N_HEADS, V_HEAD, D_MODEL),
                  ((0, 0), (0, HEAD_SLOT - V_HEAD), (0, 0))).reshape(HP, D_MODEL)
    wbs, wglu, wout = wts["w_branch_ssm"], wts["w_glu"], wts["w_out"]
    a, sm, merged, z, x1, hn2, y3 = _merge_fwd(x, gl, attn, y1, wba, wbs, wglu, bglu, wout, gpost, gpre, tl)
    late = wts["late"](x1)
    wup, wdown, convw = late["w_up"], late["w_down"], late["conv_w"]
    h = _mm(hn2, wup, "ffn_up", tm_cap=2048)
    act = _conv_fwd(h, convw, convb, l)
    ff = _mm(act, wdown, "ffn_down")
    loss, dx2, dff, dgfin = _loss_head(ff, x1, tgt, gfin, ta)

    dact = _mm(dff, wdown, "ffn_down_dx", out_dtype=BF16, bt=True, tm_cap=2048)
    d_wdown = _mm_tn(act, dff, "ffn_down_dw", tk_cap=D_FF // 2)
    dh, dwg, dwv, dbg, dbv = _conv_bwd(h, dact, convw, convb, l)
    d_convw = jnp.concatenate([dwg, dwv], axis=1)
    d_convb = jnp.concatenate([dbg, dbv], axis=1)
    dhn2 = _mm(dh, wup, "ffn_up_dx", bt=True)
    d_wup = _mm_tn(hn2, dh, "ffn_up_dw", chips=True)
    behind = wts["send_grads"]("ffn", {"w_up": d_wup, "w_down": _chip_major(d_wdown, 0)})
    (dx1, dz, dbra, dbrs, dgl, dattn, dy1, dt, y2, dgpre, dgpost, dbgate, dbglu) = _merge_bwd(
        dhn2, x1, dx2, z, gl, a, sm, y1, wba, wbs, wglu, bglu, wout, gpost, gpre + behind, tb)
    d_wout = _mm_tn(merged, dz, "w_out_dw")
    d_wba = _mm_tn(attn, dbra, "w_branch_attn_dw", chips=True)
    d_wbs = _mm_tn(y3, dbrs, "w_branch_ssm_dw", chips=True)
    d_wglu = _mm_tn(y2, dt, "w_glu_dw")
    ncol = D_MODEL // N_CHIPS
    behind = wts["send_grads"]("mix", {
        "w_glu": _chip_major(d_wglu, 0),
        "w_branch_attn": d_wba.reshape(N_CHIPS, N_HEADS, HEAD_SLOT, ncol)[:, :, :V_HEAD].reshape(
            N_CHIPS, N_HEADS * V_HEAD, ncol),
        "w_branch_ssm": d_wbs,
        "w_out": _chip_major(d_wout, 0)}, after=d_wglu)
    dq, dk, dv = _attn_bwd(q, k, v, dattn, lse + behind, _attn_delta(attn, dattn, min(2048, l), BWD_HEADS), ta,
                            BWD_HEADS)
    du, dbre, dbim, dcre, dcim, dare, daim, dd = _ssm_bwd(dy1, u, sre, sim, bre, bim, cre, cim, dvec, tab_r, ts)
    behind = wts["send_grads"]("none", {}, after=du)
    gx, dql, qn, ckvn, dproj, dg1, dgq, dgkv = _proj_bwd(
        x, dx1, cq, ckv, dq, dk, dv, du, dgl, g1 + behind, win, gq, wuq, gkv, wukv, rc, rs, tl)
    d_win = _mm_tn(hn, dproj, "w_in_dw")
    d_wuq = _mm_tn(qn, dql, "w_uq_dw")
    d_wuk = _mm_tn(ckvn, dk, "w_uk_dw")
    d_wuv = _mm_tn(ckvn, dv, "w_uv_dw")

    d_lre, d_lim, d_ldt, d_bre, d_bim = disc_vjp((dare.reshape(SSM_G, SSM_P), daim.reshape(SSM_G, SSM_P),
                                                  _extract_b(dbre), _extract_b(dbim)))
    big = {
        "w_in": _chip_major(_unpad_w_in(d_win), 1),
        "w_uq": _chip_major(_unpad_heads(d_wuq, QK_HEAD), 1),
        "w_uk": _chip_major(_unpad_heads(d_wuk, QK_NOPE), 1),
        "w_uv": _chip_major(_unpad_heads(d_wuv, V_HEAD), 1),
    }
    small = {
        "conv_w": d_convw,
        "mix_norm_pre": dg1, "q_norm": dgq, "kv_norm": dgkv,
        "ssm_lambda_re": d_lre, "ssm_lambda_im": d_lim, "ssm_log_dt": d_ldt,
        "ssm_b_re": d_bre, "ssm_b_im": d_bim,
        "ssm_c_re": _extract_c(dcre), "ssm_c_im": _extract_c(dcim),
        "ssm_d": dd.reshape(SSM_G, SSM_H), "b_glu": dbglu, "b_gate": dbgate,
        "mix_norm_post": dgpost, "ffn_norm_pre": dgpre, "conv_b": d_convb, "ffn_norm_post": dgfin,
    }
    return loss[0, 0], gx, big, small


_ANY = pl.BlockSpec(memory_space=pl.ANY)


ROW_TILE = 16


def _place():
    x, y, c = lax.axis_index("x"), lax.axis_index("y"), lax.axis_index("c")
    return x, y, c, 2 * x + y, [(1 - x, y), (x, 1 - y), (1 - x, 1 - y)]


def _half(rows, which):
    hr = rows // 2
    return pl.ds(pl.multiple_of(which * hr, ROW_TILE), hr)


def _remote(src, dst, send_sems, recv_sems, n, dev):
    return pltpu.make_async_remote_copy(src_ref=src, dst_ref=dst, send_sem=send_sems.at[n], recv_sem=recv_sems.at[n],
                                        device_id=dev, device_id_type=MESH)


def _gather_big(shards):
    nw = len(shards)
    rows = [s.shape[0] for s in shards]

    def body(*refs):
        ins, outs = refs[:nw], refs[nw:2 * nw]
        ici_send, ici_recv, d2d_send, d2d_recv = refs[2 * nw:]
        x, y, c, me, peers = _place()
        sent = []
        for i in range(nw):
            for p, (px, py) in enumerate(peers):
                cp = _remote(ins[i].at[_half(rows[i], c)], outs[i].at[me, _half(rows[i], c)], ici_send, ici_recv,
                             3 * i + p, (px, py, c))
                cp.start()
                sent.append(cp)
        for p, (px, py) in enumerate(peers):
            for i in range(nw):
                blk = outs[i].at[2 * px + py, _half(rows[i], c)]
                _remote(blk, blk, ici_send, ici_recv, 3 * i + p, (px, py, c)).wait_recv()
                cp = _remote(blk, blk, d2d_send, d2d_recv, 3 * i + p, (x, y, 1 - c))
                cp.start()
                sent.append(cp)
        for p, (px, py) in enumerate(peers):
            for i in range(nw):
                blk = outs[i].at[2 * px + py, _half(rows[i], 1 - c)]
                _remote(blk, blk, d2d_send, d2d_recv, 3 * i + p, (x, y, 1 - c)).wait_recv()
        for cp in sent:
            cp.wait_send()

    dma = pltpu.SemaphoreType.DMA
    return pl.pallas_call(
        body, name="gather_weights", in_specs=[_ANY] * nw, out_specs=[_ANY] * nw,
        out_shape=[jax.ShapeDtypeStruct((N_CHIPS,) + s.shape, s.dtype) for s in shards],
        scratch_shapes=[dma((3 * nw,)), dma((3 * nw,)), dma((3 * nw,)), dma((3 * nw,))],
    )(*shards)


_HBM = pl.BlockSpec(memory_space=pltpu.HBM)
_SEM = pl.BlockSpec(memory_space=pltpu.SEMAPHORE)
_DATAFLOW = pltpu.SideEffectType.DATAFLOW_SIDE_EFFECTING


def _exchange_start(shards, name, scatter):
    nw = len(shards)
    lands = [lax.empty(s.shape if scatter else (N_CHIPS,) + s.shape, s.dtype) for s in shards]

    def body(*refs):
        ins, zones = refs[:nw], refs[nw:2 * nw]
        send_sems, recv_sems, token = refs[2 * nw], refs[2 * nw + 1], refs[-1]
        x, y, c, me, peers = _place()
        for i in range(nw):
            for p, (px, py) in enumerate(peers):
                src = ins[i].at[2 * px + py] if scatter else ins[i]
                _remote(src, zones[i].at[me], send_sems, recv_sems, 3 * i + p, (px, py, c)).start()
        token[...] = jnp.zeros_like(token)

    thru = [pltpu.HBM(a.shape, a.dtype) for a in list(shards) + lands]
    dma = pltpu.SemaphoreType.DMA
    outs = pl.pallas_call(
        body, name=name,
        out_shape=(dma((3 * nw,)), dma((3 * nw,)), *thru, jax.ShapeDtypeStruct((SUBLANES, LANES), F32)),
        in_specs=[_HBM] * (2 * nw),
        out_specs=(_SEM, _SEM, *([_HBM] * (2 * nw)), pl.BlockSpec(memory_space=pltpu.VMEM)),
        input_output_aliases={i: 2 + i for i in range(2 * nw)},
        compiler_params=pltpu.CompilerParams(has_side_effects=_DATAFLOW),
    )(*[pltpu.with_memory_space_constraint(a, pltpu.HBM) for a in list(shards) + lands])
    return outs[0], outs[1], list(outs[2:2 + nw]), list(outs[2 + nw:2 + 2 * nw]), outs[-1]


def _exchange_wait(send_sems, recv_sems, shards, lands, after, name, scatter):
    nw = len(shards)

    def body(*refs):
        ins, zones = refs[:nw], refs[nw:2 * nw]
        send_sems, recv_sems = refs[2 * nw], refs[2 * nw + 1]
        x, y, c, me, peers = _place()
        for i in range(nw):
            for p, (px, py) in enumerate(peers):
                src = ins[i].at[2 * px + py] if scatter else ins[i]
                cp = _remote(src, zones[i].at[2 * px + py], send_sems, recv_sems, 3 * i + p, (px, py, c))
                cp.wait_send()
                cp.wait_recv()

    both = list(shards) + list(lands)
    outs = pl.pallas_call(
        body, name=name,
        out_shape=tuple(pltpu.HBM(a.shape, a.dtype) for a in both),
        in_specs=(*([_HBM] * (2 * nw)), _SEM, _SEM, _ANY), out_specs=[_HBM] * (2 * nw),
        input_output_aliases={i: i for i in range(2 * nw)},
        compiler_params=pltpu.CompilerParams(has_side_effects=_DATAFLOW),
    )(*both, send_sems, recv_sems, after)
    return list(outs[:nw]), list(outs[nw:])


def _sibling_start(grads, name):
    nw = len(grads)
    lands = [lax.empty((N_CHIPS, g.shape[1] // 2, g.shape[2]), g.dtype) for g in grads]

    def body(*refs):
        ins, zones = refs[:nw], refs[nw:2 * nw]
        send_sems, recv_sems, token = refs[2 * nw], refs[2 * nw + 1], refs[-1]
        x, y, c, _, _ = _place()
        for i in range(nw):
            _remote(ins[i].at[pl.ds(0, N_CHIPS), _half(grads[i].shape[1], 1 - c)], zones[i], send_sems, recv_sems,
                    i, (x, y, 1 - c)).start()
        token[...] = jnp.zeros_like(token)

    thru = [pltpu.HBM(a.shape, a.dtype) for a in list(grads) + lands]
    dma = pltpu.SemaphoreType.DMA
    outs = pl.pallas_call(
        body, name=name,
        out_shape=(dma((nw,)), dma((nw,)), *thru, jax.ShapeDtypeStruct((SUBLANES, LANES), F32)),
        in_specs=[_HBM] * (2 * nw),
        out_specs=(_SEM, _SEM, *([_HBM] * (2 * nw)), pl.BlockSpec(memory_space=pltpu.VMEM)),
        input_output_aliases={i: 2 + i for i in range(2 * nw)},
        compiler_params=pltpu.CompilerParams(has_side_effects=_DATAFLOW),
    )(*[pltpu.with_memory_space_constraint(a, pltpu.HBM) for a in list(grads) + lands])
    return outs[0], outs[1], list(outs[2:2 + nw]), list(outs[2 + nw:2 + 2 * nw]), outs[-1]


def _sibling_wait(send_sems, recv_sems, grads, lands, after, name):
    nw = len(grads)

    def body(*refs):
        ins, zones = refs[:nw], refs[nw:2 * nw]
        send_sems, recv_sems = refs[2 * nw], refs[2 * nw + 1]
        x, y, c, _, _ = _place()
        for i in range(nw):
            cp = _remote(ins[i].at[pl.ds(0, N_CHIPS), _half(grads[i].shape[1], 1 - c)], zones[i], send_sems, recv_sems,
                         i, (x, y, 1 - c))
            cp.wait_send()
            cp.wait_recv()

    both = list(grads) + list(lands)
    outs = pl.pallas_call(
        body, name=name,
        out_shape=tuple(pltpu.HBM(a.shape, a.dtype) for a in both),
        in_specs=(*([_HBM] * (2 * nw)), _SEM, _SEM, _ANY), out_specs=[_HBM] * (2 * nw),
        input_output_aliases={i: i for i in range(2 * nw)},
        compiler_params=pltpu.CompilerParams(has_side_effects=_DATAFLOW),
    )(*both, send_sems, recv_sems, after)
    return list(outs[:nw]), list(outs[nw:])


def _reduce_to_sibling(grads, name):
    nw = len(grads)

    def body(*refs):
        ins, outs = refs[:nw], refs[nw:2 * nw]
        send_sems, recv_sems = refs[2 * nw:]
        x, y, c, _, _ = _place()
        sent = []
        for i in range(nw):
            cp = _remote(ins[i].at[pl.ds(0, N_CHIPS), _half(grads[i].shape[1], 1 - c)], outs[i], send_sems, recv_sems,
                         i, (x, y, 1 - c))
            cp.start()
            sent.append(cp)
        for cp in sent:
            cp.wait()

    dma = pltpu.SemaphoreType.DMA
    return pl.pallas_call(
        body, name=name, in_specs=[_ANY] * nw, out_specs=[_ANY] * nw,
        out_shape=[jax.ShapeDtypeStruct((N_CHIPS, g.shape[1] // 2, g.shape[2]), g.dtype) for g in grads],
        scratch_shapes=[dma((nw,)), dma((nw,))],
    )(*grads)


def _reduce_back(totals, name):
    nw = len(totals)

    def body(*refs):
        outs = refs[nw:2 * nw]
        send_sems, recv_sems = refs[2 * nw:]
        x, y, c, _, _ = _place()
        sent = []
        for i in range(nw):
            blk = outs[i].at[_half(totals[i].shape[0], c)]
            cp = _remote(blk, blk, send_sems, recv_sems, i, (x, y, 1 - c))
            cp.start()
            sent.append(cp)
        for i in range(nw):
            blk = outs[i].at[_half(totals[i].shape[0], 1 - c)]
            _remote(blk, blk, send_sems, recv_sems, i, (x, y, 1 - c)).wait_recv()
        for cp in sent:
            cp.wait_send()

    dma = pltpu.SemaphoreType.DMA
    return pl.pallas_call(
        body, name=name, in_specs=[_ANY] * nw, out_specs=[_ANY] * nw,
        out_shape=[jax.ShapeDtypeStruct(t.shape, t.dtype) for t in totals],
        input_output_aliases={i: i for i in range(nw)},
        scratch_shapes=[dma((nw,)), dma((nw,))],
    )(*totals)


def _all_reduce_small(v, name):
    rows, w = v.shape
    hr = rows // 2
    assert hr % SUBLANES == 0

    def body(v_ref, out_ref, sib_ref, half_ref, chips_ref, send_sems, recv_sems):
        x, y, c, me, peers = _place()
        sibling = (x, y, 1 - c)
        mine = pl.ds(pl.multiple_of(c * hr, SUBLANES), hr)
        other = pl.ds(pl.multiple_of((1 - c) * hr, SUBLANES), hr)
        cp = _remote(v_ref, sib_ref, send_sems, recv_sems, 0, sibling)
        cp.start()
        cp.wait()
        half_ref[...] = v_ref[mine, :] + sib_ref[mine, :]
        sent = []
        for p, (px, py) in enumerate(peers):
            cp = _remote(half_ref, chips_ref.at[me], send_sems, recv_sems, 1 + p, (px, py, c))
            cp.start()
            sent.append(cp)
        chips_ref[me] = half_ref[...]
        for p, (px, py) in enumerate(peers):
            _remote(half_ref, chips_ref.at[2 * px + py], send_sems, recv_sems, 1 + p, (px, py, c)).wait_recv()
        for cp in sent:
            cp.wait_send()
        out_ref[mine, :] = ((chips_ref[0] + chips_ref[1]) + chips_ref[2]) + chips_ref[3]
        cp = _remote(out_ref.at[mine], out_ref.at[mine], send_sems, recv_sems, 4, sibling)
        cp.start()
        _remote(out_ref.at[other], out_ref.at[other], send_sems, recv_sems, 4, sibling).wait_recv()
        cp.wait_send()

    vm = pl.BlockSpec(memory_space=pltpu.VMEM)
    return pl.pallas_call(
        body, name=name, in_specs=[vm], out_specs=vm,
        out_shape=jax.ShapeDtypeStruct((rows, w), F32),
        scratch_shapes=[pltpu.VMEM((rows, w), F32), pltpu.VMEM((hr, w), F32), pltpu.VMEM((N_CHIPS, hr, w), F32),
                        pltpu.SemaphoreType.DMA((5,)), pltpu.SemaphoreType.DMA((5,))],
        compiler_params=pltpu.CompilerParams(vmem_limit_bytes=VMEM_LIMIT),
    )(v)


ELEMENTWISE_BLOCK = 512 * 1024


def _rows_tile(rows, cols, cap=ELEMENTWISE_BLOCK):
    best = None
    for t in range(SUBLANES, rows + 1, SUBLANES):
        if rows % t == 0 and t * cols <= cap:
            best = t
    return rows if best is None else best


def _add_pair(g, t, core, name):
    nb, n, w = t.shape
    tr = _rows_tile(n, w, 2 * ELEMENTWISE_BLOCK)
    steps = n // tr

    def body(core_ref, g_ref, t_ref, o_ref):
        o_ref[...] = (g_ref[...] + t_ref[...]).astype(BF16)

    spec = pl.BlockSpec((1, tr, w), lambda j, i, core_ref: (j, i, 0))
    return pl.pallas_call(
        body, name=name,
        grid_spec=pltpu.PrefetchScalarGridSpec(
            num_scalar_prefetch=1, grid=(nb, steps),
            in_specs=[pl.BlockSpec((1, tr, w), lambda j, i, core_ref: (j, core_ref[0] * steps + i, 0)), spec],
            out_specs=spec),
        out_shape=jax.ShapeDtypeStruct(t.shape, BF16),
        compiler_params=_params(("parallel", "parallel")))(core, g, t)


def _add_chips(landed, pairs, place, name):
    nb, n, w = landed.shape
    tr = _rows_tile(n, w, 2 * ELEMENTWISE_BLOCK)
    steps = n // tr

    def body(place_ref, r_ref, own_ref, o_ref):
        me = place_ref[0]
        acc = None
        for k in range(nb):
            blk = jnp.where(me == k, own_ref[0], r_ref[k]).astype(F32)
            acc = blk if acc is None else acc + blk
        o_ref[...] = acc

    return pl.pallas_call(
        body, name=name,
        grid_spec=pltpu.PrefetchScalarGridSpec(
            num_scalar_prefetch=1, grid=(steps,),
            in_specs=[pl.BlockSpec((nb, tr, w), lambda i, place_ref: (0, i, 0)),
                      pl.BlockSpec((1, tr, w), lambda i, place_ref: (place_ref[0], i, 0))],
            out_specs=pl.BlockSpec((tr, w), lambda i, place_ref: (place_ref[1] * steps + i, 0))),
        out_shape=jax.ShapeDtypeStruct((2 * n, w), F32),
        compiler_params=_params(("parallel",)))(place, landed, pairs)


def _adamw(w, g, m, v, name):
    rows, wd = w.shape
    tr = _rows_tile(rows, wd)
    c1 = 1.0 - ADAM_B1 ** ADAM_STEP
    c2 = 1.0 - ADAM_B2 ** ADAM_STEP

    def body(w_ref, g_ref, m_ref, v_ref, go_ref, d_ref, mo_ref, vo_ref):
        gv = g_ref[...]
        go_ref[...] = gv
        m2 = ADAM_B1 * m_ref[...] + (1.0 - ADAM_B1) * gv
        v2 = ADAM_B2 * v_ref[...] + (1.0 - ADAM_B2) * (gv * gv)
        mo_ref[...] = m2
        vo_ref[...] = v2
        d_ref[...] = -ADAM_LR * ((m2 / c1) / (jnp.sqrt(v2 / c2) + ADAM_EPS) + ADAM_WD * w_ref[...])

    spec = pl.BlockSpec((tr, wd), lambda i: (i, 0))
    shp = jax.ShapeDtypeStruct((rows, wd), F32)
    return pl.pallas_call(body, name=name, grid=(rows // tr,), in_specs=[spec] * 4, out_specs=[spec] * 4,
                          out_shape=[shp] * 4, compiler_params=_params(("parallel",)))(w, g, m, v)


BIG = [("w_in", (1024, 3232), 1), ("w_uq", (384, 768), 1), ("w_uk", (256, 512), 1), ("w_uv", (256, 512), 1),
       ("w_glu", (512, 512), 0), ("w_branch_attn", (512, 1024), 1), ("w_branch_ssm", (512, 1024), 1),
       ("w_out", (1024, 1024), 0), ("w_up", (1024, 5632), 1), ("conv_w", (3, 5632), 1), ("w_down", (2816, 1024), 0)]
SMALL = [("mix_norm_pre", (1024,)), ("q_norm", (384,)), ("kv_norm", (256,)), ("ssm_lambda_re", (32, 64)),
         ("ssm_lambda_im", (32, 64)), ("ssm_log_dt", (32,)), ("ssm_b_re", (32, 64, 16)), ("ssm_b_im", (32, 64, 16)),
         ("ssm_c_re", (32, 16, 64)), ("ssm_c_im", (32, 16, 64)), ("ssm_d", (32, 16)), ("b_glu", (512,)),
         ("b_gate", (2048,)), ("mix_norm_post", (1024,)), ("ffn_norm_pre", (1024,)), ("conv_b", (5632,)),
         ("ffn_norm_post", (1024,))]
MATMUL_W = [b for b in BIG if b[0] != "conv_w"]
LATE_W = ("w_up", "w_down", "conv_w")
CONV_W_SHAPE = (3, 2 * D_FF)
CONV_W_SHARD = (3, 2 * D_FF // N_CHIPS)
SMALL_SUM = [("loss", (1,))] + SMALL + [("conv_w", CONV_W_SHAPE)]
SMALL_ADAM = SMALL + [("conv_w", CONV_W_SHARD)]


def _pack_flat(layout, vals):
    flat = jnp.concatenate([vals[n].astype(F32).reshape(-1) for n, _ in layout])
    rows = -(-(-(-flat.shape[0] // FLAT_W)) // (2 * SUBLANES)) * 2 * SUBLANES
    return jnp.pad(flat, (0, rows * FLAT_W - flat.shape[0])).reshape(rows, FLAT_W)


def _unpack_flat(layout, flat):
    flat = flat.reshape(-1)
    out = {}
    o = 0
    for name, shape in layout:
        n = math.prod(shape)
        out[name] = flat[o:o + n].reshape(shape)
        o += n
    return out


_ARG_NAMES = ["x", "positions"] + [n for n in (
    "mix_norm_pre", "w_in", "q_norm", "w_uq", "kv_norm", "w_uk", "w_uv", "ssm_lambda_re", "ssm_lambda_im", "ssm_log_dt",
    "ssm_b_re", "ssm_b_im", "ssm_c_re", "ssm_c_im", "ssm_d", "w_glu", "b_glu", "w_branch_attn", "w_branch_ssm",
    "b_gate", "w_out", "mix_norm_post", "ffn_norm_pre", "w_up", "conv_w", "conv_b", "w_down", "ffn_norm_post")]
_WEIGHTS = _ARG_NAMES[2:]


def _gather_weights(w):
    early = [b for b in MATMUL_W if b[0] not in LATE_W]
    late = [b for b in BIG if b[0] in LATE_W]
    own = (jnp.arange(N_CHIPS) == 2 * lax.axis_index("x") + lax.axis_index("y"))[:, None, None]

    def whole(layout, mine, gathered):
        return {name: _from_chip_major(jnp.where(own, s[None], g), axis)
                for (name, _, axis), s, g in zip(layout, mine, gathered)}

    mine = [w[name].astype(BF16) for name, _, _ in early]
    gathered = _gather_big(mine)
    full = whole(early, mine, gathered)
    mine_late = [w[name].astype(F32 if name == "conv_w" else BF16) for name, _, _ in late]
    _, mine_late = lax.optimization_barrier((gathered[0], mine_late))
    send_sems, recv_sems, shards_thru, lands_thru, token = _exchange_start(mine_late, "gather_late_start", scatter=False)

    def late_weights(after):
        shards, lands = _exchange_wait(send_sems, recv_sems, shards_thru, lands_thru, after, "gather_late_wait",
                                       scatter=False)
        return whole(late, shards, lands)

    full["late"] = late_weights
    full["token"] = token[0, 0]
    return full


def _pair_sums(names, grads, tag):
    core = lax.axis_index("c").astype(jnp.int32).reshape(1)
    theirs = _reduce_to_sibling(grads, "reduce_grads_d2d" + tag)
    return [_add_pair(g, t, core, "reduce_pair_" + n) for n, g, t in zip(names, grads, theirs)]


def _send_grads(tag, grads, after, flying, pending):
    token = jnp.zeros((), F32)
    if flying:
        tag0, names0, state0 = flying.pop()
        core = lax.axis_index("c").astype(jnp.int32).reshape(1)
        mine, theirs = _sibling_wait(*state0, after, "reduce_" + tag0 + "_d2d_wait")
        pairs = [_add_pair(g, t, core, "reduce_pair_" + n) for n, g, t in zip(names0, mine, theirs)]
        send_sems, recv_sems, pairs_thru, lands_thru, tok = _exchange_start(pairs, "reduce_" + tag0 + "_start",
                                                                          scatter=True)
        pending.append((tag0, names0, send_sems, recv_sems, pairs_thru, lands_thru))
        token = token + tok[0, 0]
    if grads:
        names = list(grads)
        send_sems, recv_sems, grads_thru, lands_thru, tok = _sibling_start([grads[n] for n in names],
                                                                          "reduce_" + tag + "_d2d_start")
        flying.append((tag, names, (send_sems, recv_sems, grads_thru, lands_thru)))
        token = token + tok[0, 0]
    return token


def _reduce_grads(gbig, pending, loss, gsmall, use_sent):
    core = lax.axis_index("c").astype(jnp.int32).reshape(1)
    chip = (2 * lax.axis_index("x") + lax.axis_index("y")).astype(jnp.int32).reshape(1)
    place = jnp.concatenate([chip, core])

    def finish(names, pairs, landed, name):
        totals = [_add_chips(r, p, place, "reduce_chips_" + n) for n, r, p in zip(names, landed, pairs)]
        return dict(zip(names, _reduce_back(totals, name)))

    names = list(gbig)
    pairs = _pair_sums(names, [gbig[n] for n in names], "")
    send_sems, recv_sems, pairs_thru, lands_thru, token = _exchange_start(pairs, "reduce_last_start", scatter=True)
    sent_names, sent_pairs, sent_landed = [], [], []
    for tag, group, g_send, g_recv, g_pairs, g_lands in pending:
        got_pairs, got_landed = _exchange_wait(g_send, g_recv, g_pairs, g_lands, token, "reduce_" + tag + "_wait",
                                               scatter=True)
        sent_names, sent_pairs, sent_landed = sent_names + group, sent_pairs + got_pairs, sent_landed + got_landed
    g_sent = finish(sent_names, sent_pairs, sent_landed, "reduce_back_sent")
    vals = dict(gsmall)
    vals["loss"] = loss
    small_red = _unpack_flat(SMALL_SUM, _all_reduce_small(_pack_flat(SMALL_SUM, vals), "reduce_small"))
    after = use_sent(g_sent, small_red)
    pairs, landed = _exchange_wait(send_sems, recv_sems, pairs_thru, lands_thru, after, "reduce_last_wait", scatter=True)
    return finish(names, pairs, landed, "reduce_back_last"), small_red


def _step(args):
    x = args["x"][0]
    positions = args["positions"][0]
    tgt = args["loss_target"][0]
    w = {n: args[n][0] for n in _WEIGHTS}
    m = {n: args["m_" + n][0] for n in _WEIGHTS}
    v = {n: args["v_" + n][0] for n in _WEIGHTS}

    full = _gather_weights(w)
    sp = {n: w[n].reshape(s) for n, s in SMALL}
    for n in ("mix_norm_pre", "q_norm", "kv_norm", "b_glu", "b_gate", "mix_norm_post", "ffn_norm_pre", "conv_b",
              "ffn_norm_post"):
        sp[n] = sp[n].reshape(1, -1)
    sp["mix_norm_pre"] = sp["mix_norm_pre"] + full.pop("token")
    pending, flying = [], []
    full["send_grads"] = lambda tag, grads, after=None: _send_grads(tag, grads, after, flying, pending)
    loss, gx, gbig, gsmall = _local_step(x, positions, tgt, full, sp)
    outs = {}

    def adam_big(g_red):
        for name in g_red:
            g2, d, m2, v2 = _adamw(w[name], g_red[name], m[name], v[name], "adamw_" + name)
            outs["grad_" + name], outs["delta_" + name], outs["new_m_" + name], outs["new_v_" + name] = g2, d, m2, v2
        return v2

    def use_sent(g_sent, small_red):
        chip = 2 * lax.axis_index("x") + lax.axis_index("y")
        grads = dict(small_red)
        grads["conv_w"] = lax.dynamic_slice_in_dim(small_red["conv_w"], chip * CONV_W_SHARD[1], CONV_W_SHARD[1], axis=1)
        outs.update({"grad_" + n: grads[n] for n, _ in SMALL_ADAM})
        _, d_sm, m_sm, v_sm = _adamw(_pack_flat(SMALL_ADAM, w), _pack_flat(SMALL_ADAM, grads),
                                     _pack_flat(SMALL_ADAM, m), _pack_flat(SMALL_ADAM, v), "adamw_small")
        for prefix, flat in (("delta_", d_sm), ("new_m_", m_sm), ("new_v_", v_sm)):
            for n, val in _unpack_flat(SMALL_ADAM, flat).items():
                outs[prefix + n] = val
        return adam_big(g_sent)

    g_last, small_red = _reduce_grads(gbig, pending, loss, gsmall, use_sent)
    adam_big(g_last)
    outs = {n: val.reshape(args[n.split("_", 1)[1] if not n.startswith("new_") else n[6:]].shape)
            for n, val in outs.items()}
    res = [small_red["loss"][0], gx[None]]
    for prefix in ("grad_", "delta_", "new_m_", "new_v_"):
        res += [outs[prefix + n] for n in _WEIGHTS]
    return tuple(res)


def kernel(x, positions, mix_norm_pre, w_in, q_norm, w_uq, kv_norm, w_uk, w_uv, ssm_lambda_re, ssm_lambda_im, ssm_log_dt, ssm_b_re, ssm_b_im, ssm_c_re, ssm_c_im, ssm_d, w_glu, b_glu, w_branch_attn, w_branch_ssm, b_gate, w_out, mix_norm_post, ffn_norm_pre, w_up, conv_w, conv_b, w_down, ffn_norm_post, loss_target, m_mix_norm_pre, m_w_in, m_q_norm, m_w_uq, m_kv_norm, m_w_uk, m_w_uv, m_ssm_lambda_re, m_ssm_lambda_im, m_ssm_log_dt, m_ssm_b_re, m_ssm_b_im, m_ssm_c_re, m_ssm_c_im, m_ssm_d, m_w_glu, m_b_glu, m_w_branch_attn, m_w_branch_ssm, m_b_gate, m_w_out, m_mix_norm_post, m_ffn_norm_pre, m_w_up, m_conv_w, m_conv_b, m_w_down, m_ffn_norm_post, v_mix_norm_pre, v_w_in, v_q_norm, v_w_uq, v_kv_norm, v_w_uk, v_w_uv, v_ssm_lambda_re, v_ssm_lambda_im, v_ssm_log_dt, v_ssm_b_re, v_ssm_b_im, v_ssm_c_re, v_ssm_c_im, v_ssm_d, v_w_glu, v_b_glu, v_w_branch_attn, v_w_branch_ssm, v_b_gate, v_w_out, v_mix_norm_post, v_ffn_norm_pre, v_w_up, v_conv_w, v_conv_b, v_w_down, v_ffn_norm_post):
    given = dict(locals())
    return _step(given)
```

```python
import math

import jax
import jax.numpy as jnp
from jax import lax
from jax.experimental import pallas as pl
from jax.experimental.pallas import tpu as pltpu

F32 = jnp.float32
BF16 = jnp.bfloat16
MESH = pl.DeviceIdType.MESH

D_MODEL = 1024
N_HEADS = 8
QK_NOPE = 64
QK_ROPE = 32
QK_HEAD = QK_NOPE + QK_ROPE
V_HEAD = 64
Q_RANK = 384
KV_RANK = 256
ROPE_THETA = 10000.0
SSM_W = 512
SSM_H = 16
SSM_G = 32
SSM_P = 64
SSM_CH = SSM_G * SSM_P
D_FF = 2816
EPS = 1e-6
ADAM_LR = 0.001
ADAM_B1 = 0.9
ADAM_B2 = 0.999
ADAM_EPS = 1e-08
ADAM_WD = 0.01
ADAM_STEP = 10

LANES = 128
SUBLANES = 8
VMEM_LIMIT = 56 * 1024 * 1024

HEAD_SLOT = LANES
HP = N_HEADS * HEAD_SLOT
P_CQ, P_CKV, P_KR, P_U, P_GL, P_END = 0, 384, 640, 768, 1280, 3328
KR_LANE = 64

FLAT_W = 1024
N_CHIPS = 4


def _tile(n, cap):
    if n <= cap:
        return n
    best = None
    for t in range(LANES, cap + 1, LANES):
        if n % t == 0:
            best = t
    assert best is not None, (n, cap)
    return best


def _params(sem):
    return pltpu.CompilerParams(dimension_semantics=sem, vmem_limit_bytes=VMEM_LIMIT)


def _dot(a, b):
    return jnp.dot(a, b, preferred_element_type=F32)


def _dot_nt(a, b):
    return lax.dot_general(a, b, (((1,), (1,)), ((), ())), preferred_element_type=F32)


def _dot_tn(a, b):
    return lax.dot_general(a, b, (((0,), (0,)), ((), ())), preferred_element_type=F32)


def _rms(x, g):
    r = lax.rsqrt(jnp.mean(x * x, axis=-1, keepdims=True) + EPS)
    return x * r * g, r


def _rms_bwd(dy, x, g):
    r = lax.rsqrt(jnp.mean(x * x, axis=-1, keepdims=True) + EPS)
    dyg = dy * g
    dx = r * dyg - x * (r * r * r) * jnp.mean(dyg * x, axis=-1, keepdims=True)
    dg = jnp.sum(dy * x * r, axis=0, keepdims=True)
    return dx, dg


_GELU_K0 = math.sqrt(2.0 / math.pi)
_GELU_K1 = 0.044715


def _gelu(x):
    th = jnp.tanh(_GELU_K0 * (x + _GELU_K1 * x * x * x))
    return 0.5 * x * (1.0 + th)


def _gelu_grad(x):
    th = jnp.tanh(_GELU_K0 * (x + _GELU_K1 * x * x * x))
    return 0.5 * (1.0 + th) + 0.5 * x * (1.0 - th * th) * _GELU_K0 * (1.0 + 3.0 * _GELU_K1 * x * x)


def _sigmoid(x):
    return 1.0 / (1.0 + jnp.exp(-x))


def _rope(q, c, s):
    n = q.shape[1]
    lane = lax.broadcasted_iota(jnp.int32, q.shape, 1) % HEAD_SLOT
    sw = jnp.where(lane < KR_LANE + QK_ROPE // 2, pltpu.roll(q, n - QK_ROPE // 2, 1), pltpu.roll(q, QK_ROPE // 2, 1))
    return q * c + sw * s


def _rope_bwd(dy, c, s):
    n = dy.shape[1]
    t = dy * s
    lane = lax.broadcasted_iota(jnp.int32, dy.shape, 1) % HEAD_SLOT
    sw = jnp.where(lane < KR_LANE + QK_ROPE // 2, pltpu.roll(t, n - QK_ROPE // 2, 1), pltpu.roll(t, QK_ROPE // 2, 1))
    rope_lane = jnp.logical_and(lane >= KR_LANE, lane < KR_LANE + QK_ROPE)
    return dy * c + jnp.where(rope_lane, sw, 0.0)


def _shift_down(x, k, halo):
    xs = pltpu.roll(x, k, 0)
    hs = pltpu.roll(halo, k, 0)
    rows = lax.broadcasted_iota(jnp.int32, halo.shape, 0)
    top = jnp.where(rows < k, hs, xs[0:SUBLANES])
    return jnp.concatenate([top, xs[SUBLANES:]], axis=0)


def _shift_up(x, k, halo):
    t = x.shape[0]
    xs = pltpu.roll(x, t - k, 0)
    hs = pltpu.roll(halo, SUBLANES - k, 0)
    rows = lax.broadcasted_iota(jnp.int32, halo.shape, 0)
    bot = jnp.where(rows >= SUBLANES - k, hs, xs[t - SUBLANES:])
    return jnp.concatenate([xs[:t - SUBLANES], bot], axis=0)


def _mm(a, b, name, out_dtype=F32, bt=False, tm_cap=1024, tn_cap=1408):
    m, k = a.shape[-2:]
    parts = a.shape[0] if a.ndim == 3 else 1
    n = b.shape[0] if bt else b.shape[1]
    tm = min(tm_cap, m)
    tn = _tile(n, tn_cap)

    def body(a_ref, b_ref, o_ref):
        if not bt:
            o_ref[...] = _dot(a_ref[...], b_ref[...]).astype(out_dtype)
        elif parts == 1:
            o_ref[...] = _dot_nt(a_ref[...], b_ref[...]).astype(out_dtype)
        else:
            @pl.when(pl.program_id(2) == 0)
            def _():
                o_ref[...] = jnp.zeros_like(o_ref)

            o_ref[...] += _dot_nt(a_ref[...], b_ref[...])

    if bt:
        b_spec = pl.BlockSpec((tn, k), lambda j, i, s: (j, s))
    else:
        b_spec = pl.BlockSpec((k, tn), lambda j, i, s: (0, j))
    if a.ndim == 3:
        assert bt and out_dtype == F32
        a_spec = pl.BlockSpec((None, tm, k), lambda j, i, s: (s, i, 0))
    else:
        a_spec = pl.BlockSpec((tm, k), lambda j, i, s: (i, 0))
    return pl.pallas_call(
        body, name=name, grid=(n // tn, m // tm, parts),
        in_specs=[a_spec, b_spec],
        out_specs=pl.BlockSpec((tm, tn), lambda j, i, s: (i, j)),
        out_shape=jax.ShapeDtypeStruct((m, n), out_dtype),
        compiler_params=_params(("parallel", "parallel", "arbitrary")),
    )(a, b)


def _mm_tn(a, b, name, tk_cap=1024, tn_cap=1664, tl_cap=2048, chips=False):
    l, k = a.shape
    tk = _tile(k, tk_cap)
    tl = min(tl_cap, l)

    def body(a_ref, b_ref, o_ref):
        @pl.when(pl.program_id(2) == 0)
        def _():
            o_ref[...] = jnp.zeros_like(o_ref)

        o_ref[...] += _dot_tn(a_ref[...], b_ref[...])

    if chips:
        n = b.shape[-1] * (b.shape[0] if b.ndim == 3 else 1)
        tn = n // N_CHIPS
        assert tn % LANES == 0
        if b.ndim == 3:
            per = N_CHIPS // b.shape[0]
            b_spec = pl.BlockSpec((None, tl, tn), lambda i, j, r: (j // per, r, j % per))
        else:
            b_spec = pl.BlockSpec((tl, tn), lambda i, j, r: (r, j))
        out_spec = pl.BlockSpec((None, tk, tn), lambda i, j, r: (j, i, 0))
        out_shape = jax.ShapeDtypeStruct((N_CHIPS, k, tn), F32)
    else:
        n = b.shape[1]
        tn = _tile(n, tn_cap)
        b_spec = pl.BlockSpec((tl, tn), lambda i, j, r: (r, j))
        out_spec = pl.BlockSpec((tk, tn), lambda i, j, r: (i, j))
        out_shape = jax.ShapeDtypeStruct((k, n), F32)
    return pl.pallas_call(
        body, name=name, grid=(k // tk, n // tn, l // tl),
        in_specs=[pl.BlockSpec((tl, tk), lambda i, j, r: (r, i)), b_spec],
        out_specs=out_spec, out_shape=out_shape,
        compiler_params=_params(("parallel", "parallel", "arbitrary")),
    )(a, b)


def _row(tl, n):
    return pl.BlockSpec((tl, n), lambda i: (i, 0))


def _const(shape):
    return pl.BlockSpec(shape, lambda i: tuple(0 for _ in shape))


def _proj_fwd(x, g1, win, gq, wuq, gkv, wukv, rc, rs, bg, tl):
    l = x.shape[0]

    def body(x_ref, g1_ref, win_ref, gq_ref, wuq_ref, gkv_ref, wukv_ref, rc_ref, rs_ref, bg_ref,
             hn_ref, cq_ref, ckv_ref, q_ref, k_ref, v_ref, u_ref, gl_ref):
        hn, _ = _rms(x_ref[...], g1_ref[...])
        hnb = hn.astype(BF16)
        hn_ref[...] = hnb
        proj = _dot(hnb, win_ref[...])
        cq = proj[:, P_CQ:P_CKV]
        ckv = proj[:, P_CKV:P_KR]
        kr = proj[:, P_KR:P_U]
        cq_ref[...] = cq
        ckv_ref[...] = ckv
        u_ref[...] = proj[:, P_U:P_GL]
        gl_ref[...] = (proj[:, P_GL:P_END] + bg_ref[...]).astype(BF16)
        qn, _ = _rms(cq, gq_ref[...])
        q = _dot(qn.astype(BF16), wuq_ref[...])
        c1 = rc_ref[...]
        s1 = rs_ref[...]
        q_ref[...] = (_rope(q, jnp.tile(c1, (1, N_HEADS)), jnp.tile(s1, (1, N_HEADS))) * Q_PRESCALE).astype(BF16)
        ckvn, _ = _rms(ckv, gkv_ref[...])
        kv = _dot(ckvn.astype(BF16), wukv_ref[...])
        krr = _rope(kr, c1, s1)
        k_ref[...] = (kv[:, :HP] + jnp.tile(krr, (1, N_HEADS))).astype(BF16)
        v_ref[...] = kv[:, HP:].astype(BF16)

    outs = [(D_MODEL, BF16), (Q_RANK, F32), (KV_RANK, F32), (HP, BF16), (HP, BF16), (HP, BF16),
            (SSM_W, F32), (2 * D_MODEL, BF16)]
    return pl.pallas_call(
        body, name="proj_fwd", grid=(l // tl,),
        in_specs=[_row(tl, D_MODEL), _const((1, D_MODEL)), _const((D_MODEL, P_END)), _const((1, Q_RANK)),
                  _const((Q_RANK, HP)), _const((1, KV_RANK)), _const((KV_RANK, 2 * HP)),
                  _row(tl, HEAD_SLOT), _row(tl, HEAD_SLOT), _const((1, 2 * D_MODEL))],
        out_specs=[_row(tl, n) for n, _ in outs],
        out_shape=[jax.ShapeDtypeStruct((l, n), dt) for n, dt in outs],
        compiler_params=_params(("parallel",)),
    )(x, g1, win, gq, wuq, gkv, wukv, rc, rs, bg)


_NEG = -1e30


LOG2E = 1.0 / math.log(2.0)
LN2 = math.log(2.0)
ATTN_SCALE = 1.0 / math.sqrt(QK_HEAD)
Q_PRESCALE = ATTN_SCALE * LOG2E
FWD_HEADS = 8
BWD_HEADS = 4


def _causal_pairs(nq, by_query):
    if by_query:
        pairs = [(i, j) for i in range(nq) for j in range(i + 1)]
    else:
        pairs = [(i, j) for j in range(nq) for i in range(j, nq)]
    return jnp.array([p[0] for p in pairs], jnp.int32), jnp.array([p[1] for p in pairs], jnp.int32)


def _diag_mask_t(s):
    rows = lax.broadcasted_iota(jnp.int32, s.shape, 0)
    cols = lax.broadcasted_iota(jnp.int32, s.shape, 1)
    return jnp.where(rows <= cols, s, _NEG)


def _attn_fwd(q, k, v, tq, heads):
    l = q.shape[0]
    nq = l // tq
    it, jt = _causal_pairs(nq, True)

    def body(it_ref, jt_ref, q_ref, k_ref, v_ref, o_ref, lse_ref, m_ref, l_ref, acc_ref):
        t = pl.program_id(1)
        i = it_ref[t]
        j = jt_ref[t]

        @pl.when(j == 0)
        def _():
            m_ref[...] = jnp.full_like(m_ref, _NEG)
            l_ref[...] = jnp.zeros_like(l_ref)
            acc_ref[...] = jnp.zeros_like(acc_ref)

        def update(on_diagonal):
            for hh in range(heads):
                sl = slice(hh * HEAD_SLOT, (hh + 1) * HEAD_SLOT)
                s = _dot_nt(k_ref[:, sl], q_ref[:, sl])
                if on_diagonal:
                    s = _diag_mask_t(s)
                m_old = m_ref[hh]
                m_new = jnp.maximum(m_old, jnp.max(s, axis=0, keepdims=True))
                p = jnp.exp2(s - m_new)
                alpha = jnp.exp2(m_old - m_new)
                l_ref[hh] = alpha * l_ref[hh] + jnp.sum(p, axis=0, keepdims=True)
                acc_ref[hh] = alpha * acc_ref[hh] + _dot_tn(v_ref[:, sl], p.astype(BF16))
                m_ref[hh] = m_new

        @pl.when(j < i)
        def _():
            update(False)

        @pl.when(j == i)
        def _():
            update(True)
            for hh in range(heads):
                sl = slice(hh * HEAD_SLOT, (hh + 1) * HEAD_SLOT)
                o_ref[:, sl] = (acc_ref[hh] / l_ref[hh]).T.astype(BF16)
                lse_ref[hh] = m_ref[hh] + jnp.log(l_ref[hh]) * LOG2E

    blk = (tq, (heads * HEAD_SLOT))
    qmap = lambda h, t, it_ref, jt_ref: (it_ref[t], h)
    kmap = lambda h, t, it_ref, jt_ref: (jt_ref[t], h)
    row = pl.BlockSpec((heads, 1, tq), lambda h, t, it_ref, jt_ref: (h, 0, it_ref[t]))
    return pl.pallas_call(
        body, name="attn_fwd",
        grid_spec=pltpu.PrefetchScalarGridSpec(
            num_scalar_prefetch=2, grid=(N_HEADS // heads, it.shape[0]),
            in_specs=[pl.BlockSpec(blk, qmap), pl.BlockSpec(blk, kmap), pl.BlockSpec(blk, kmap)],
            out_specs=[pl.BlockSpec(blk, qmap), row],
            scratch_shapes=[pltpu.VMEM((heads, 1, tq), F32), pltpu.VMEM((heads, 1, tq), F32),
                            pltpu.VMEM((heads, HEAD_SLOT, tq), F32)]),
        out_shape=[jax.ShapeDtypeStruct((l, HP), BF16), jax.ShapeDtypeStruct((N_HEADS, 1, l), F32)],
        compiler_params=_params(("parallel", "arbitrary")),
    )(it, jt, q, k, v)


def _attn_delta(o, do, tq, heads):
    l = o.shape[0]

    def body(o_ref, do_ref, d_ref):
        prod = o_ref[...].astype(F32) * do_ref[...].astype(F32)
        for hh in range(heads):
            d_ref[hh] = jnp.sum(prod[:, hh * HEAD_SLOT:(hh + 1) * HEAD_SLOT].T, axis=0, keepdims=True)

    blk = pl.BlockSpec((tq, (heads * HEAD_SLOT)), lambda h, i: (i, h))
    return pl.pallas_call(
        body, name="attn_delta", grid=(N_HEADS // heads, l // tq), in_specs=[blk, blk],
        out_specs=pl.BlockSpec((heads, 1, tq), lambda h, i: (h, 0, i)),
        out_shape=jax.ShapeDtypeStruct((N_HEADS, 1, l), F32),
        compiler_params=_params(("parallel", "parallel")),
    )(o, do)


def _attn_bwd(q, k, v, do, lse, delta, tq, heads):
    l = q.shape[0]
    nq = l // tq
    it, jt = _causal_pairs(nq, False)

    def body(it_ref, jt_ref, q_ref, k_ref, v_ref, do_ref, lse_ref, dl_ref, dq_ref, dk_ref, dv_ref, dka_ref, dva_ref):
        t = pl.program_id(1)
        i = it_ref[t]
        j = jt_ref[t]

        @pl.when(t == 0)
        def _():
            dq_ref[...] = jnp.zeros_like(dq_ref)

        @pl.when(i == j)
        def _():
            dka_ref[...] = jnp.zeros_like(dka_ref)
            dva_ref[...] = jnp.zeros_like(dva_ref)

        def update(on_diagonal):
            r0 = pl.multiple_of(i * tq, tq)
            for hh in range(heads):
                sl = slice(hh * HEAD_SLOT, (hh + 1) * HEAD_SLOT)
                qb = q_ref[:, sl]
                kb = k_ref[:, sl]
                dob = do_ref[:, sl]
                s = _dot_nt(kb, qb)
                if on_diagonal:
                    s = _diag_mask_t(s)
                p = jnp.exp2(s - lse_ref[hh])
                dva_ref[:, sl] += _dot(p.astype(BF16), dob)
                dp = _dot_nt(v_ref[:, sl], dob)
                ds = (p * (dp - dl_ref[hh])).astype(BF16)
                dka_ref[:, sl] += _dot(ds, qb)
                dq_ref[pl.ds(r0, tq), sl] += ATTN_SCALE * _dot_tn(ds, kb)

        @pl.when(j < i)
        def _():
            update(False)

        @pl.when(j == i)
        def _():
            update(True)

        @pl.when(i == nq - 1)
        def _():
            dk_ref[...] = (dka_ref[...] * LN2).astype(BF16)
            dv_ref[...] = dva_ref[...].astype(BF16)

    blk = (tq, (heads * HEAD_SLOT))
    qmap = lambda h, t, it_ref, jt_ref: (it_ref[t], h)
    kmap = lambda h, t, it_ref, jt_ref: (jt_ref[t], h)
    row = pl.BlockSpec((heads, 1, tq), lambda h, t, it_ref, jt_ref: (h, 0, it_ref[t]))
    return pl.pallas_call(
        body, name="attn_bwd",
        grid_spec=pltpu.PrefetchScalarGridSpec(
            num_scalar_prefetch=2, grid=(N_HEADS // heads, it.shape[0]),
            in_specs=[pl.BlockSpec(blk, qmap), pl.BlockSpec(blk, kmap), pl.BlockSpec(blk, kmap),
                      pl.BlockSpec(blk, qmap), row, row],
            out_specs=[pl.BlockSpec((l, (heads * HEAD_SLOT)), lambda h, t, it_ref, jt_ref: (0, h)), pl.BlockSpec(blk, kmap),
                       pl.BlockSpec(blk, kmap)],
            scratch_shapes=[pltpu.VMEM(blk, F32), pltpu.VMEM(blk, F32)]),
        out_shape=[jax.ShapeDtypeStruct((l, HP), F32), jax.ShapeDtypeStruct((l, HP), BF16),
                   jax.ShapeDtypeStruct((l, HP), BF16)],
        compiler_params=_params(("parallel", "arbitrary")),
    )(it, jt, q, k, v, do, lse, delta)


SSM_CB = 512
SSM_UB = 128
SSM_NB = SSM_CH // SSM_CB


def _scan_tiles(re_ref, im_ref, tab, carry, n_tiles, reverse):
    group = 2
    assert n_tiles % group == 0
    pr, pi = tab[6], tab[7]

    def inside(sr, si):
        for step, k in enumerate((1, 2, 4)):
            mr, mi = tab[2 * step], tab[2 * step + 1]
            sh = (SUBLANES - k) if reverse else k
            rr = pltpu.roll(sr, sh, 0)
            ri = pltpu.roll(si, sh, 0)
            sr, si = sr + mr * rr - mi * ri, si + mr * ri + mi * rr
        return sr, si

    def body(n, c):
        cr, ci = c
        first = (n_tiles - group * (n + 1)) if reverse else group * n
        r0 = pl.multiple_of(first * SUBLANES, group * SUBLANES)
        rows = [pl.ds(r0 + g * SUBLANES, SUBLANES) for g in range(group)]
        tiles = [inside(re_ref[r, :], im_ref[r, :]) for r in rows]
        for g in (range(group - 1, -1, -1) if reverse else range(group)):
            sr, si = tiles[g]
            sr, si = sr + pr * cr - pi * ci, si + pr * ci + pi * cr
            re_ref[rows[g], :] = sr
            im_ref[rows[g], :] = si
            edge = slice(0, 1) if reverse else slice(SUBLANES - 1, SUBLANES)
            cr, ci = sr[edge, :], si[edge, :]
        return cr, ci

    return lax.fori_loop(0, n_tiles // group, body, carry)


def _ssm_fwd(u, bre, bim, cre, cim, dvec, tab, tt):
    l = u.shape[0]
    nt = l // tt

    def body(u_ref, bre_ref, bim_ref, cre_ref, cim_ref, d_ref, tab_ref, y_ref, sre_ref, sim_ref, car_ref):
        @pl.when(pl.program_id(1) == 0)
        def _():
            car_ref[...] = jnp.zeros_like(car_ref)

        uf = u_ref[...]
        ub = uf.astype(BF16)
        sre_ref[...] = _dot(ub, bre_ref[0])
        sim_ref[...] = _dot(ub, bim_ref[0])
        tab_v = [tab_ref[n] for n in range(8)]
        cr, ci = _scan_tiles(sre_ref, sim_ref, tab_v, (car_ref[0:1, :], car_ref[8:9, :]), tt // SUBLANES, False)
        car_ref[0:1, :] = cr
        car_ref[8:9, :] = ci
        y_ref[...] = (_dot(sre_ref[...].astype(BF16), cre_ref[0]) - _dot(sim_ref[...].astype(BF16), cim_ref[0])
                      + d_ref[...] * uf)

    return pl.pallas_call(
        body, name="ssm_fwd", grid=(SSM_NB, nt),
        in_specs=[pl.BlockSpec((tt, SSM_UB), lambda m, t: (t, m)),
                  pl.BlockSpec((1, SSM_UB, SSM_CB), lambda m, t: (m, 0, 0)),
                  pl.BlockSpec((1, SSM_UB, SSM_CB), lambda m, t: (m, 0, 0)),
                  pl.BlockSpec((1, SSM_CB, SSM_UB), lambda m, t: (m, 0, 0)),
                  pl.BlockSpec((1, SSM_CB, SSM_UB), lambda m, t: (m, 0, 0)),
                  pl.BlockSpec((1, SSM_UB), lambda m, t: (0, m)),
                  pl.BlockSpec((8, SUBLANES, SSM_CB), lambda m, t: (0, 0, m))],
        out_specs=[pl.BlockSpec((tt, SSM_UB), lambda m, t: (t, m)),
                   pl.BlockSpec((tt, SSM_CB), lambda m, t: (t, m)),
                   pl.BlockSpec((tt, SSM_CB), lambda m, t: (t, m))],
        out_shape=[jax.ShapeDtypeStruct((l, SSM_W), F32), jax.ShapeDtypeStruct((l, SSM_CH), F32),
                   jax.ShapeDtypeStruct((l, SSM_CH), F32)],
        scratch_shapes=[pltpu.VMEM((2 * SUBLANES, SSM_CB), F32)],
        compiler_params=_params(("parallel", "arbitrary")),
    )(u, bre, bim, cre, cim, dvec, tab)


def _ssm_bwd(dy, u, sre, sim, bre, bim, cre, cim, dvec, tab, tt):
    l = u.shape[0]
    nt = l // tt
    tpb = tt // SUBLANES

    def body(dy_ref, u_ref, sre_ref, sim_ref, hre_ref, him_ref, bre_ref, bim_ref, cre_ref, cim_ref, d_ref, tab_ref,
             du_ref, dbre_ref, dbim_ref, dcre_ref, dcim_ref, dare_ref, daim_ref, dd_ref, lr_ref, li_ref, car_ref):
        t = pl.program_id(1)

        @pl.when(t == 0)
        def _():
            car_ref[...] = jnp.zeros_like(car_ref)
            for ref in (dbre_ref, dbim_ref, dcre_ref, dcim_ref, dare_ref, daim_ref, dd_ref):
                ref[...] = jnp.zeros_like(ref)

        dyf = dy_ref[...]
        dyb = dyf.astype(BF16)
        uf = u_ref[...]
        s_re = sre_ref[...]
        s_im = sim_ref[...]
        lr_ref[...] = _dot_nt(dyb, cre_ref[0])
        li_ref[...] = -_dot_nt(dyb, cim_ref[0])
        dcre_ref[0] += _dot_tn(s_re.astype(BF16), dyb)
        dcim_ref[0] -= _dot_tn(s_im.astype(BF16), dyb)
        tab_v = [tab_ref[n] for n in range(8)]
        cr, ci = _scan_tiles(lr_ref, li_ref, tab_v, (car_ref[0:1, :], car_ref[8:9, :]), tpb, True)
        car_ref[0:1, :] = cr
        car_ref[8:9, :] = ci
        lam_r = lr_ref[...]
        lam_i = li_ref[...]
        keep = jnp.where(t == nt - 1, 0.0, 1.0)
        sp_r = _shift_down(s_re, 1, hre_ref[...] * keep)
        sp_i = _shift_down(s_im, 1, him_ref[...] * keep)
        dare_ref[...] += jnp.sum(lam_r * sp_r + lam_i * sp_i, axis=0, keepdims=True)
        daim_ref[...] += jnp.sum(lam_i * sp_r - lam_r * sp_i, axis=0, keepdims=True)
        lrb = lam_r.astype(BF16)
        lib = lam_i.astype(BF16)
        du_ref[...] = _dot_nt(lrb, bre_ref[0]) + _dot_nt(lib, bim_ref[0]) + dyf * d_ref[...]
        ub = uf.astype(BF16)
        dbre_ref[0] += _dot_tn(ub, lrb)
        dbim_ref[0] += _dot_tn(ub, lib)
        dd_ref[...] += jnp.sum(dyf * uf, axis=0, keepdims=True)

    rev = lambda m, t: (nt - 1 - t, m)
    halo = lambda m, t: (jnp.maximum((nt - 1 - t) * tpb - 1, 0), m)
    wb = pl.BlockSpec((1, SSM_UB, SSM_CB), lambda m, t: (m, 0, 0))
    wc = pl.BlockSpec((1, SSM_CB, SSM_UB), lambda m, t: (m, 0, 0))
    vec_c = pl.BlockSpec((1, SSM_CB), lambda m, t: (0, m))
    vec_u = pl.BlockSpec((1, SSM_UB), lambda m, t: (0, m))
    return pl.pallas_call(
        body, name="ssm_bwd", grid=(SSM_NB, nt),
        in_specs=[pl.BlockSpec((tt, SSM_UB), rev), pl.BlockSpec((tt, SSM_UB), rev),
                  pl.BlockSpec((tt, SSM_CB), rev), pl.BlockSpec((tt, SSM_CB), rev),
                  pl.BlockSpec((SUBLANES, SSM_CB), halo), pl.BlockSpec((SUBLANES, SSM_CB), halo),
                  wb, wb, wc, wc, vec_u,
                  pl.BlockSpec((8, SUBLANES, SSM_CB), lambda m, t: (0, 0, m))],
        out_specs=[pl.BlockSpec((tt, SSM_UB), rev), wb, wb, wc, wc, vec_c, vec_c, vec_u],
        out_shape=[jax.ShapeDtypeStruct((l, SSM_W), F32),
                   jax.ShapeDtypeStruct((SSM_NB, SSM_UB, SSM_CB), F32), jax.ShapeDtypeStruct((SSM_NB, SSM_UB, SSM_CB), F32),
                   jax.ShapeDtypeStruct((SSM_NB, SSM_CB, SSM_UB), F32), jax.ShapeDtypeStruct((SSM_NB, SSM_CB, SSM_UB), F32),
                   jax.ShapeDtypeStruct((1, SSM_CH), F32), jax.ShapeDtypeStruct((1, SSM_CH), F32),
                   jax.ShapeDtypeStruct((1, SSM_W), F32)],
        scratch_shapes=[pltpu.VMEM((tt, SSM_CB), F32), pltpu.VMEM((tt, SSM_CB), F32),
                        pltpu.VMEM((2 * SUBLANES, SSM_CB), F32)],
        compiler_params=_params(("parallel", "arbitrary")),
    )(dy, u, sre, sim, sre, sim, bre, bim, cre, cim, dvec, tab)


def _merge_fwd(x, gl, attn, y1, wba, wbs, wglu, bglu, wout, gpost, gpre, tl):
    l = x.shape[0]

    def body(x_ref, gl_ref, at_ref, y1_ref, wba_ref, wbs_ref, wglu_ref, bglu_ref, wout_ref, gpost_ref, gpre_ref,
             a_ref, sm_ref, mg_ref, z_ref, x1_ref, hn2_ref, y3_ref):
        y2 = _gelu(y1_ref[...])
        sg = _sigmoid(_dot(y2.astype(BF16), wglu_ref[...]) + bglu_ref[...])
        y3 = (y2 * sg).astype(BF16)
        y3_ref[...] = y3
        a = _dot(at_ref[...], wba_ref[...])
        sm = _dot(y3, wbs_ref[...])
        a_ref[...] = a.astype(BF16)
        sm_ref[...] = sm.astype(BF16)
        g = _sigmoid(gl_ref[...].astype(F32))
        merged = (g[:, :D_MODEL] * a + g[:, D_MODEL:] * sm).astype(BF16)
        mg_ref[...] = merged
        z = _dot(merged, wout_ref[...])
        z_ref[...] = z
        n, _ = _rms(z, gpost_ref[...])
        x1 = x_ref[...] + n
        x1_ref[...] = x1
        hn2, _ = _rms(x1, gpre_ref[...])
        hn2_ref[...] = hn2.astype(BF16)

    outs = [(D_MODEL, BF16), (D_MODEL, BF16), (D_MODEL, BF16), (D_MODEL, F32), (D_MODEL, F32), (D_MODEL, BF16),
            (SSM_W, BF16)]
    return pl.pallas_call(
        body, name="merge_fwd", grid=(l // tl,),
        in_specs=[_row(tl, D_MODEL), _row(tl, 2 * D_MODEL), _row(tl, HP), _row(tl, SSM_W),
                  _const((HP, D_MODEL)), _const((SSM_W, D_MODEL)), _const((SSM_W, SSM_W)), _const((1, SSM_W)),
                  _const((D_MODEL, D_MODEL)), _const((1, D_MODEL)), _const((1, D_MODEL))],
        out_specs=[_row(tl, n) for n, _ in outs],
        out_shape=[jax.ShapeDtypeStruct((l, n), dt) for n, dt in outs],
        compiler_params=_params(("parallel",)),
    )(x, gl, attn, y1, wba, wbs, wglu, bglu, wout, gpost, gpre)


def _merge_bwd(dhn2, x1, dx2, z, gl, a, sm, y1, wba, wbs, wglu, bglu, wout, gpost, gpre, tl):
    l = x1.shape[0]

    def body(dhn2_ref, x1_ref, dx2_ref, z_ref, gl_ref, a_ref, sm_ref, y1_ref,
             wba_ref, wbs_ref, wglu_ref, bglu_ref, wout_ref, gpost_ref, gpre_ref,
             dx1_ref, dz_ref, dbra_ref, dbrs_ref, dgl_ref, dat_ref, dy1_ref, dt_ref, y2_ref,
             dgpre_ref, dgpost_ref, dbg_ref, dbglu_ref):
        @pl.when(pl.program_id(0) == 0)
        def _():
            for ref in (dgpre_ref, dgpost_ref, dbg_ref, dbglu_ref):
                ref[...] = jnp.zeros_like(ref)

        dx1a, dgpre = _rms_bwd(dhn2_ref[...], x1_ref[...], gpre_ref[...])
        dgpre_ref[...] += dgpre
        dx1 = dx2_ref[...] + dx1a
        dx1_ref[...] = dx1
        dz, dgpost = _rms_bwd(dx1, z_ref[...], gpost_ref[...])
        dgpost_ref[...] += dgpost
        dzb = dz.astype(BF16)
        dz_ref[...] = dzb
        dm = _dot_nt(dzb, wout_ref[...])
        g = _sigmoid(gl_ref[...].astype(F32))
        g0 = g[:, :D_MODEL]
        g1 = g[:, D_MODEL:]
        dbra = (dm * g0).astype(BF16)
        dbrs = (dm * g1).astype(BF16)
        dbra_ref[...] = dbra
        dbrs_ref[...] = dbrs
        dgl0 = dm * a_ref[...].astype(F32) * g0 * (1.0 - g0)
        dgl1 = dm * sm_ref[...].astype(F32) * g1 * (1.0 - g1)
        dgl_ref[:, :D_MODEL] = dgl0.astype(BF16)
        dgl_ref[:, D_MODEL:] = dgl1.astype(BF16)
        dbg_ref[:, :D_MODEL] += jnp.sum(dgl0, axis=0, keepdims=True)
        dbg_ref[:, D_MODEL:] += jnp.sum(dgl1, axis=0, keepdims=True)
        dat_ref[...] = _dot_nt(dbra, wba_ref[...]).astype(BF16)
        dy3 = _dot_nt(dbrs, wbs_ref[...])
        y1v = y1_ref[...]
        y2 = _gelu(y1v)
        y2b = y2.astype(BF16)
        y2_ref[...] = y2b
        sg = _sigmoid(_dot(y2b, wglu_ref[...]) + bglu_ref[...])
        dt = dy3 * y2 * sg * (1.0 - sg)
        dtb = dt.astype(BF16)
        dt_ref[...] = dtb
        dbglu_ref[...] += jnp.sum(dt, axis=0, keepdims=True)
        dy2 = dy3 * sg + _dot_nt(dtb, wglu_ref[...])
        dy1_ref[...] = dy2 * _gelu_grad(y1v)

    outs = [(D_MODEL, F32), (D_MODEL, BF16), (D_MODEL, BF16), (D_MODEL, BF16), (2 * D_MODEL, BF16), (HP, BF16),
            (SSM_W, F32), (SSM_W, BF16), (SSM_W, BF16)]
    accs = [D_MODEL, D_MODEL, 2 * D_MODEL, SSM_W]
    return pl.pallas_call(
        body, name="merge_bwd", grid=(l // tl,),
        in_specs=[_row(tl, D_MODEL), _row(tl, D_MODEL), _row(tl, D_MODEL), _row(tl, D_MODEL),
                  _row(tl, 2 * D_MODEL), _row(tl, D_MODEL), _row(tl, D_MODEL), _row(tl, SSM_W),
                  _const((HP, D_MODEL)), _const((SSM_W, D_MODEL)), _const((SSM_W, SSM_W)), _const((1, SSM_W)),
                  _const((D_MODEL, D_MODEL)), _const((1, D_MODEL)), _const((1, D_MODEL))],
        out_specs=[_row(tl, n) for n, _ in outs] + [_const((1, n)) for n in accs],
        out_shape=[jax.ShapeDtypeStruct((l, n), dt) for n, dt in outs]
        + [jax.ShapeDtypeStruct((1, n), F32) for n in accs],
        compiler_params=_params(("arbitrary",)),
    )(dhn2, x1, dx2, z, gl, a, sm, y1, wba, wbs, wglu, bglu, wout, gpost, gpre)


def _proj_bwd(x, dx1, cq, ckv, dq, dk, dv, du, dgl, g1, win, gq, wuq, gkv, wukv, rc, rs, tl):
    l = x.shape[0]

    def body(x_ref, dx1_ref, cq_ref, ckv_ref, dq_ref, dk_ref, dv_ref, du_ref, dgl_ref,
             g1_ref, win_ref, gq_ref, wuq_ref, gkv_ref, wukv_ref, rc_ref, rs_ref,
             gx_ref, dql_ref, qn_ref, ckvn_ref, dproj_ref, dg1_ref, dgq_ref, dgkv_ref):
        @pl.when(pl.program_id(0) == 0)
        def _():
            for ref in (dg1_ref, dgq_ref, dgkv_ref):
                ref[...] = jnp.zeros_like(ref)

        c1 = rc_ref[...]
        s1 = rs_ref[...]
        dql = _rope_bwd(dq_ref[...], jnp.tile(c1, (1, N_HEADS)), jnp.tile(s1, (1, N_HEADS))).astype(BF16)
        dql_ref[...] = dql
        dqn = _dot_nt(dql, wuq_ref[...])
        cq = cq_ref[...]
        qn, _ = _rms(cq, gq_ref[...])
        qn_ref[...] = qn.astype(BF16)
        dcq, dgq = _rms_bwd(dqn, cq, gq_ref[...])
        dgq_ref[...] += dgq
        dkb = dk_ref[...]
        dvb = dv_ref[...]
        dkf = dkb.astype(F32)
        dkr = dkf[:, 0:HEAD_SLOT]
        for h in range(1, N_HEADS):
            dkr = dkr + dkf[:, h * HEAD_SLOT:(h + 1) * HEAD_SLOT]
        dkr = _rope_bwd(dkr, c1, s1)
        dckvn = _dot_nt(dkb, wukv_ref[:, :HP]) + _dot_nt(dvb, wukv_ref[:, HP:])
        ckv = ckv_ref[...]
        ckvn, _ = _rms(ckv, gkv_ref[...])
        ckvn_ref[...] = ckvn.astype(BF16)
        dckv, dgkv = _rms_bwd(dckvn, ckv, gkv_ref[...])
        dgkv_ref[...] += dgkv
        dproj_ref[:, P_CQ:P_CKV] = dcq.astype(BF16)
        dproj_ref[:, P_CKV:P_KR] = dckv.astype(BF16)
        dproj_ref[:, P_KR:P_U] = dkr.astype(BF16)
        dproj_ref[:, P_U:P_GL] = du_ref[...].astype(BF16)
        dproj_ref[:, P_GL:P_END] = dgl_ref[...]
        dhn = _dot_nt(dproj_ref[...], win_ref[...])
        dxa, dg1 = _rms_bwd(dhn, x_ref[...], g1_ref[...])
        dg1_ref[...] += dg1
        gx_ref[...] = dx1_ref[...] + dxa

    outs = [(D_MODEL, F32), (HP, BF16), (Q_RANK, BF16), (KV_RANK, BF16), (P_END, BF16)]
    accs = [D_MODEL, Q_RANK, KV_RANK]
    return pl.pallas_call(
        body, name="proj_bwd", grid=(l // tl,),
        in_specs=[_row(tl, D_MODEL), _row(tl, D_MODEL), _row(tl, Q_RANK), _row(tl, KV_RANK), _row(tl, HP),
                  _row(tl, HP), _row(tl, HP), _row(tl, SSM_W), _row(tl, 2 * D_MODEL),
                  _const((1, D_MODEL)), _const((D_MODEL, P_END)), _const((1, Q_RANK)), _const((Q_RANK, HP)),
                  _const((1, KV_RANK)), _const((KV_RANK, 2 * HP)), _row(tl, HEAD_SLOT), _row(tl, HEAD_SLOT)],
        out_specs=[_row(tl, n) for n, _ in outs] + [_const((1, n)) for n in accs],
        out_shape=[jax.ShapeDtypeStruct((l, n), dt) for n, dt in outs]
        + [jax.ShapeDtypeStruct((1, n), F32) for n in accs],
        compiler_params=_params(("arbitrary",)),
    )(x, dx1, cq, ckv, dq, dk, dv, du, dgl, g1, win, gq, wuq, gkv, wukv, rc, rs)


CONV_CB = 256
CONV_NB = D_FF // CONV_CB
CONV_ROWS = 16


def _conv3(h, halo, w, b):
    return b + w[0:1, :] * _shift_down(h, 2, halo) + w[1:2, :] * _shift_down(h, 1, halo) + w[2:3, :] * h


def _conv_fwd(h, cw, cb, tl):
    l = h.shape[0]

    def body(hg_ref, hv_ref, wg_ref, wv_ref, bg_ref, bv_ref, act_ref, halo_ref):
        @pl.when(pl.program_id(1) == 0)
        def _():
            halo_ref[...] = jnp.zeros_like(halo_ref)

        hg = hg_ref[...]
        hv = hv_ref[...]
        cg = _conv3(hg, halo_ref[0:SUBLANES, :], wg_ref[...], bg_ref[...])
        cv = _conv3(hv, halo_ref[SUBLANES:, :], wv_ref[...], bv_ref[...])
        act_ref[...] = (_gelu(cg) * cv).astype(BF16)
        halo_ref[0:SUBLANES, :] = hg[tl - SUBLANES:, :]
        halo_ref[SUBLANES:, :] = hv[tl - SUBLANES:, :]

    gmap = lambda c, r: (r, c)
    vmap = lambda c, r: (r, CONV_NB + c)
    return pl.pallas_call(
        body, name="conv_fwd", grid=(CONV_NB, l // tl),
        in_specs=[pl.BlockSpec((tl, CONV_CB), gmap), pl.BlockSpec((tl, CONV_CB), vmap),
                  pl.BlockSpec((3, CONV_CB), lambda c, r: (0, c)), pl.BlockSpec((3, CONV_CB), lambda c, r: (0, CONV_NB + c)),
                  pl.BlockSpec((1, CONV_CB), lambda c, r: (0, c)), pl.BlockSpec((1, CONV_CB), lambda c, r: (0, CONV_NB + c))],
        out_specs=pl.BlockSpec((tl, CONV_CB), gmap),
        out_shape=jax.ShapeDtypeStruct((l, D_FF), BF16),
        scratch_shapes=[pltpu.VMEM((2 * SUBLANES, CONV_CB), F32)],
        compiler_params=_params(("parallel", "arbitrary")),
    )(h, h, cw, cw, cb, cb)


def _conv_bwd(h, dact, cw, cb, tl):
    l = h.shape[0]
    nr = l // tl
    tpb = tl // SUBLANES

    def body(hg_ref, hv_ref, hgh_ref, hvh_ref, da_ref, wg_ref, wv_ref, bg_ref, bv_ref,
             dh_ref, dwg_ref, dwv_ref, dbg_ref, dbv_ref, car_ref):
        r = pl.program_id(1)

        @pl.when(r == 0)
        def _():
            for ref in (car_ref, dwg_ref, dwv_ref, dbg_ref, dbv_ref):
                ref[...] = jnp.zeros_like(ref)

        keep = jnp.where(r == nr - 1, 0.0, 1.0)
        wg, wv, bg, bv = wg_ref[...], wv_ref[...], bg_ref[...], bv_ref[...]
        nch = tl // CONV_ROWS

        def fold(x):
            s = x[0:SUBLANES, :]
            for k in range(1, CONV_ROWS // SUBLANES):
                s = s + x[k * SUBLANES:(k + 1) * SUBLANES, :]
            return s

        def chunk(n, carry):
            ncg, ncv, acc = carry
            idx = nch - 1 - n
            r0 = pl.multiple_of(idx * CONV_ROWS, CONV_ROWS)
            rows = pl.ds(r0, CONV_ROWS)
            before = pl.ds(pl.multiple_of(jnp.maximum(r0 - SUBLANES, 0), SUBLANES), SUBLANES)
            in_tile = idx > 0
            da = da_ref[rows, :].astype(F32)

            def half(h_ref, halo_ref, w, b):
                hh = h_ref[rows, :]
                prev = jnp.where(in_tile, h_ref[before, :], halo_ref[...] * keep)
                h1 = _shift_down(hh, 1, prev)
                h2 = _shift_down(hh, 2, prev)
                return hh, h1, h2, b + w[0:1, :] * h2 + w[1:2, :] * h1 + w[2:3, :] * hh

            hg, hg1, hg2, cg = half(hg_ref, hgh_ref, wg, bg)
            hv, hv1, hv2, cv = half(hv_ref, hvh_ref, wv, bv)
            dcg = da * cv * _gelu_grad(cg)
            dcv = da * _gelu(cg)

            def back(dc, hh, h1, h2, w, nxt, part):
                dh = w[2:3, :] * dc + w[1:2, :] * _shift_up(dc, 1, nxt) + w[0:1, :] * _shift_up(dc, 2, nxt)
                dh_ref[part, rows, :] = dh.astype(BF16)
                return [fold(dc * h2), fold(dc * h1), fold(dc * hh), fold(dc)]

            sums = back(dcg, hg, hg1, hg2, wg, ncg, 0) + back(dcv, hv, hv1, hv2, wv, ncv, 1)
            return dcg[0:SUBLANES, :], dcv[0:SUBLANES, :], [a + s for a, s in zip(acc, sums)]

        zero = jnp.zeros((SUBLANES, CONV_CB), F32)
        ncg, ncv, acc = lax.fori_loop(0, nch, chunk, (car_ref[0:SUBLANES, :], car_ref[SUBLANES:, :], [zero] * 8))
        car_ref[0:SUBLANES, :] = ncg
        car_ref[SUBLANES:, :] = ncv
        for half_acc, dw_ref, db_ref in ((acc[0:4], dwg_ref, dbg_ref), (acc[4:8], dwv_ref, dbv_ref)):
            for k in range(3):
                dw_ref[k:k + 1, :] += jnp.sum(half_acc[k], axis=0, keepdims=True)
            db_ref[...] += jnp.sum(half_acc[3], axis=0, keepdims=True)

    grev = lambda c, r: (nr - 1 - r, c)
    vrev = lambda c, r: (nr - 1 - r, CONV_NB + c)
    ghalo = lambda c, r: (jnp.maximum((nr - 1 - r) * tpb - 1, 0), c)
    vhalo = lambda c, r: (jnp.maximum((nr - 1 - r) * tpb - 1, 0), CONV_NB + c)
    colg = lambda c, r: (0, c)
    colv = lambda c, r: (0, CONV_NB + c)
    return pl.pallas_call(
        body, name="conv_bwd", grid=(CONV_NB, nr),
        in_specs=[pl.BlockSpec((tl, CONV_CB), grev), pl.BlockSpec((tl, CONV_CB), vrev),
                  pl.BlockSpec((SUBLANES, CONV_CB), ghalo), pl.BlockSpec((SUBLANES, CONV_CB), vhalo),
                  pl.BlockSpec((tl, CONV_CB), grev),
                  pl.BlockSpec((3, CONV_CB), colg), pl.BlockSpec((3, CONV_CB), colv),
                  pl.BlockSpec((1, CONV_CB), colg), pl.BlockSpec((1, CONV_CB), colv)],
        out_specs=[pl.BlockSpec((2, tl, CONV_CB), lambda c, r: (0, nr - 1 - r, c)),
                   pl.BlockSpec((3, CONV_CB), colg), pl.BlockSpec((3, CONV_CB), colg),
                   pl.BlockSpec((1, CONV_CB), colg), pl.BlockSpec((1, CONV_CB), colg)],
        out_shape=[jax.ShapeDtypeStruct((2, l, D_FF), BF16),
                   jax.ShapeDtypeStruct((3, D_FF), F32), jax.ShapeDtypeStruct((3, D_FF), F32),
                   jax.ShapeDtypeStruct((1, D_FF), F32), jax.ShapeDtypeStruct((1, D_FF), F32)],
        scratch_shapes=[pltpu.VMEM((2 * SUBLANES, CONV_CB), F32)],
        compiler_params=_params(("parallel", "arbitrary")),
    )(h, h, h, h, dact, cw, cw, cb, cb)


def _loss_head(ff, x1, tgt, g, tl):
    l = ff.shape[0]

    def body(ff_ref, x1_ref, tg_ref, g_ref, loss_ref, dx2_ref, dff_ref, dg_ref):
        @pl.when(pl.program_id(0) == 0)
        def _():
            loss_ref[...] = jnp.zeros_like(loss_ref)
            dg_ref[...] = jnp.zeros_like(dg_ref)

        f = ff_ref[...]
        gv = g_ref[...]
        n, _ = _rms(f, gv)
        e = x1_ref[...] + n - tg_ref[...]
        loss_ref[...] += 0.5 * jnp.sum(jnp.mean(e * e, axis=-1, keepdims=True), axis=0, keepdims=True)
        dx2 = e * (1.0 / D_MODEL)
        dx2_ref[...] = dx2
        dff, dg = _rms_bwd(dx2, f, gv)
        dff_ref[...] = dff.astype(BF16)
        dg_ref[...] += dg

    return pl.pallas_call(
        body, name="loss_head", grid=(l // tl,),
        in_specs=[_row(tl, D_MODEL), _row(tl, D_MODEL), _row(tl, D_MODEL), _const((1, D_MODEL))],
        out_specs=[_const((1, LANES)), _row(tl, D_MODEL), _row(tl, D_MODEL), _const((1, D_MODEL))],
        out_shape=[jax.ShapeDtypeStruct((1, LANES), F32), jax.ShapeDtypeStruct((l, D_MODEL), F32),
                   jax.ShapeDtypeStruct((l, D_MODEL), BF16), jax.ShapeDtypeStruct((1, D_MODEL), F32)],
        compiler_params=_params(("arbitrary",)),
    )(ff, x1, tgt, g)


def _ssm_disc(lam_re, lam_im, log_dt, b_re, b_im):
    dt = jnp.exp(log_dt)[:, None]
    mag = jnp.exp(lam_re * dt)
    ang = lam_im * dt
    a_re, a_im = mag * jnp.cos(ang), mag * jnp.sin(ang)
    den = lam_re * lam_re + lam_im * lam_im
    n_re, n_im = a_re - 1.0, a_im
    z_re = (n_re * lam_re + n_im * lam_im) / den
    z_im = (n_im * lam_re - n_re * lam_im) / den
    bb_re = z_re[..., None] * b_re - z_im[..., None] * b_im
    bb_im = z_re[..., None] * b_im + z_im[..., None] * b_re
    return a_re, a_im, bb_re, bb_im


_GPB = SSM_CB // SSM_P


def _embed_b(bb):
    t = bb.transpose(0, 2, 1).reshape(SSM_NB, _GPB, SSM_H, SSM_P)
    return jnp.einsum('mjhp,jk->mjhkp', t, jnp.eye(_GPB, dtype=bb.dtype)).reshape(SSM_NB, SSM_UB, SSM_CB)


def _extract_b(d):
    t = d.reshape(SSM_NB, _GPB, SSM_H, _GPB, SSM_P)
    t = jnp.einsum('mjhkp,jk->mjhp', t, jnp.eye(_GPB, dtype=d.dtype))
    return t.reshape(SSM_G, SSM_H, SSM_P).transpose(0, 2, 1)


def _embed_c(c):
    t = c.transpose(0, 2, 1).reshape(SSM_NB, _GPB, SSM_P, SSM_H)
    return jnp.einsum('mjph,jk->mjpkh', t, jnp.eye(_GPB, dtype=c.dtype)).reshape(SSM_NB, SSM_CB, SSM_UB)


def _extract_c(d):
    t = d.reshape(SSM_NB, _GPB, SSM_P, _GPB, SSM_H)
    t = jnp.einsum('mjpkh,jk->mjph', t, jnp.eye(_GPB, dtype=d.dtype))
    return t.reshape(SSM_G, SSM_P, SSM_H).transpose(0, 2, 1)


def _scan_tables(a_re, a_im, reverse):
    ar = a_re.reshape(1, SSM_CH)
    ai = (-a_im if reverse else a_im).reshape(1, SSM_CH)
    pr, pi = [ar], [ai]
    for _ in range(SUBLANES - 1):
        pr, pi = pr + [pr[-1] * ar - pi[-1] * ai], pi + [pr[-1] * ai + pi[-1] * ar]
    rows = jnp.arange(SUBLANES)[:, None]
    out = []
    for k in (1, 2, 4):
        valid = (rows + k <= SUBLANES - 1) if reverse else (rows >= k)
        out += [jnp.where(valid, pr[k - 1], 0.0), jnp.where(valid, pi[k - 1], 0.0)]
    order = list(range(SUBLANES - 1, -1, -1)) if reverse else list(range(SUBLANES))
    out += [jnp.concatenate([pr[n] for n in order], axis=0), jnp.concatenate([pi[n] for n in order], axis=0)]
    return jnp.stack(out).astype(F32)


def _pad_heads(w, d):
    lead = w.shape[:-1]
    w = w.reshape(lead + (N_HEADS, d))
    w = jnp.pad(w, [(0, 0)] * len(lead) + [(0, 0), (0, HEAD_SLOT - d)])
    return w.reshape(lead + (HP,))


def _unpad_heads(w, d):
    lead = w.shape[:-1]
    return w.reshape(lead + (N_HEADS, HEAD_SLOT))[..., :d].reshape(lead + (N_HEADS * d,))


def _chip_major(w, axis):
    k, n = w.shape
    if axis == 0:
        return w.reshape(N_CHIPS, k // N_CHIPS, n)
    return w.reshape(k, N_CHIPS, n // N_CHIPS).transpose(1, 0, 2)


def _from_chip_major(w, axis):
    if axis == 0:
        return w.reshape(-1, w.shape[2])
    return w.transpose(1, 0, 2).reshape(w.shape[1], -1)


def _pad_w_in(w):
    z = lambda n: jnp.zeros((w.shape[0], n), w.dtype)
    return jnp.concatenate([w[:, :640], z(KR_LANE), w[:, 640:672], z(HEAD_SLOT - KR_LANE - QK_ROPE), w[:, 672:]], axis=1)


def _unpad_w_in(w):
    return jnp.concatenate([w[:, :640], w[:, P_KR + KR_LANE:P_KR + KR_LANE + QK_ROPE], w[:, P_U:]], axis=1)


def _local_step(x, positions, tgt, wts, sp):
    l = x.shape[0]
    tl = min(512, l)
    tb = min(256, l)
    ta = min(512, l)
    ts = min(2048, l)

    inv_freq = ROPE_THETA ** (-jnp.arange(0, QK_ROPE, 2, dtype=F32) / QK_ROPE)
    ang = positions.astype(F32)[:, None] * inv_freq
    cos, sin = jnp.cos(ang), jnp.sin(ang)
    one = jnp.ones((l, KR_LANE), F32)
    rc = jnp.concatenate([one, cos, cos, jnp.ones((l, HEAD_SLOT - KR_LANE - QK_ROPE), F32)], axis=1)
    rs = jnp.concatenate([0 * one, -sin, sin, jnp.zeros((l, HEAD_SLOT - KR_LANE - QK_ROPE), F32)], axis=1)

    win = _pad_w_in(wts["w_in"])
    wuq = _pad_heads(wts["w_uq"], QK_HEAD)
    wukv = jnp.concatenate([_pad_heads(wts["w_uk"], QK_NOPE), _pad_heads(wts["w_uv"], V_HEAD)], axis=1)

    disc_in = (sp["ssm_lambda_re"], sp["ssm_lambda_im"], sp["ssm_log_dt"], sp["ssm_b_re"], sp["ssm_b_im"])
    (a_re, a_im, bb_re, bb_im), disc_vjp = jax.vjp(_ssm_disc, *disc_in)
    bre, bim = _embed_b(bb_re).astype(BF16), _embed_b(bb_im).astype(BF16)
    cre, cim = _embed_c(sp["ssm_c_re"]).astype(BF16), _embed_c(sp["ssm_c_im"]).astype(BF16)
    dvec = sp["ssm_d"].reshape(1, SSM_W)
    tab_f = _scan_tables(a_re, a_im, False)
    tab_r = _scan_tables(a_re, a_im, True)

    g1, gq, gkv = sp["mix_norm_pre"], sp["q_norm"], sp["kv_norm"]
    gpost, gpre, gfin = sp["mix_norm_post"], sp["ffn_norm_pre"], sp["ffn_norm_post"]
    bgate, bglu, convb = sp["b_gate"], sp["b_glu"], sp["conv_b"]

    hn, cq, ckv, q, k, v, u, gl = _proj_fwd(x, g1, win, gq, wuq, gkv, wukv, rc, rs, bgate, tl)
    attn, lse = _attn_fwd(q, k, v, ta, FWD_HEADS)
    y1, sre, sim = _ssm_fwd(u, bre, bim, cre, cim, dvec, tab_f, ts)
    wba = jnp.pad(wts["w_branch_attn"].reshape(N_HEADS, V_HEAD, D_MODEL),
                  ((0, 0), (0, HEAD_SLOT - V_HEAD), (0, 0))).reshape(HP, D_MODEL)
    wbs, wglu, wout = wts["w_branch_ssm"], wts["w_glu"], wts["w_out"]
    a, sm, merged, z, x1, hn2, y3 = _merge_fwd(x, gl, attn, y1, wba, wbs, wglu, bglu, wout, gpost, gpre, tl)
    late = wts["late"](x1)
    wup, wdown, convw = late["w_up"], late["w_down"], late["conv_w"]
    h = _mm(hn2, wup, "ffn_up", tm_cap=2048)
    act = _conv_fwd(h, convw, convb, l)
    ff = _mm(act, wdown, "ffn_down")
    loss, dx2, dff, dgfin = _loss_head(ff, x1, tgt, gfin, ta)

    dact = _mm(dff, wdown, "ffn_down_dx", out_dtype=BF16, bt=True, tm_cap=2048)
    d_wdown = _mm_tn(act, dff, "ffn_down_dw", tk_cap=D_FF // 2)
    dh, dwg, dwv, dbg, dbv = _conv_bwd(h, dact, convw, convb, l)
    d_convw = jnp.concatenate([dwg, dwv], axis=1)
    d_convb = jnp.concatenate([dbg, dbv], axis=1)
    dhn2 = _mm(dh, wup, "ffn_up_dx", bt=True)
    d_wup = _mm_tn(hn2, dh, "ffn_up_dw", chips=True)
    behind = wts["send_grads"]("ffn", {"w_up": d_wup, "w_down": _chip_major(d_wdown, 0)})
    (dx1, dz, dbra, dbrs, dgl, dattn, dy1, dt, y2, dgpre, dgpost, dbgate, dbglu) = _merge_bwd(
        dhn2, x1, dx2, z, gl, a, sm, y1, wba, wbs, wglu, bglu, wout, gpost, gpre + behind, tb)
    d_wout = _mm_tn(merged, dz, "w_out_dw")
    d_wba = _mm_tn(attn, dbra, "w_branch_attn_dw", chips=True)
    d_wbs = _mm_tn(y3, dbrs, "w_branch_ssm_dw", chips=True)
    d_wglu = _mm_tn(y2, dt, "w_glu_dw")
    ncol = D_MODEL // N_CHIPS
    behind = wts["send_grads"]("mix", {
        "w_glu": _chip_major(d_wglu, 0),
        "w_branch_attn": d_wba.reshape(N_CHIPS, N_HEADS, HEAD_SLOT, ncol)[:, :, :V_HEAD].reshape(
            N_CHIPS, N_HEADS * V_HEAD, ncol),
        "w_branch_ssm": d_wbs,
        "w_out": _chip_major(d_wout, 0)}, after=d_wglu)
    dq, dk, dv = _attn_bwd(q, k, v, dattn, lse + behind, _attn_delta(attn, dattn, min(2048, l), BWD_HEADS),
                            min(1024, l), BWD_HEADS)
    du, dbre, dbim, dcre, dcim, dare, daim, dd = _ssm_bwd(dy1, u, sre, sim, bre, bim, cre, cim, dvec, tab_r, ts)
    behind = wts["send_grads"]("none", {}, after=du)
    gx, dql, qn, ckvn, dproj, dg1, dgq, dgkv = _proj_bwd(
        x, dx1, cq, ckv, dq, dk, dv, du, dgl, g1 + behind, win, gq, wuq, gkv, wukv, rc, rs, tl)
    d_win = _mm_tn(hn, dproj, "w_in_dw")
    d_wuq = _mm_tn(qn, dql, "w_uq_dw")
    d_wuk = _mm_tn(ckvn, dk, "w_uk_dw")
    d_wuv = _mm_tn(ckvn, dv, "w_uv_dw")

    d_lre, d_lim, d_ldt, d_bre, d_bim = disc_vjp((dare.reshape(SSM_G, SSM_P), daim.reshape(SSM_G, SSM_P),
                                                  _extract_b(dbre), _extract_b(dbim)))
    big = {
        "w_in": _chip_major(_unpad_w_in(d_win), 1),
        "w_uq": _chip_major(_unpad_heads(d_wuq, QK_HEAD), 1),
        "w_uk": _chip_major(_unpad_heads(d_wuk, QK_NOPE), 1),
        "w_uv": _chip_major(_unpad_heads(d_wuv, V_HEAD), 1),
    }
    small = {
        "conv_w": d_convw,
        "mix_norm_pre": dg1, "q_norm": dgq, "kv_norm": dgkv,
        "ssm_lambda_re": d_lre, "ssm_lambda_im": d_lim, "ssm_log_dt": d_ldt,
        "ssm_b_re": d_bre, "ssm_b_im": d_bim,
        "ssm_c_re": _extract_c(dcre), "ssm_c_im": _extract_c(dcim),
        "ssm_d": dd.reshape(SSM_G, SSM_H), "b_glu": dbglu, "b_gate": dbgate,
        "mix_norm_post": dgpost, "ffn_norm_pre": dgpre, "conv_b": d_convb, "ffn_norm_post": dgfin,
    }
    return loss[0, 0], gx, big, small


_ANY = pl.BlockSpec(memory_space=pl.ANY)


ROW_TILE = 16


def _place():
    x, y, c = lax.axis_index("x"), lax.axis_index("y"), lax.axis_index("c")
    return x, y, c, 2 * x + y, [(1 - x, y), (x, 1 - y), (1 - x, 1 - y)]


def _half(rows, which):
    hr = rows // 2
    return pl.ds(pl.multiple_of(which * hr, ROW_TILE), hr)


def _remote(src, dst, send_sems, recv_sems, n, dev):
    return pltpu.make_async_remote_copy(src_ref=src, dst_ref=dst, send_sem=send_sems.at[n], recv_sem=recv_sems.at[n],
                                        device_id=dev, device_id_type=MESH)


def _gather_big(shards):
    nw = len(shards)
    rows = [s.shape[0] for s in shards]

    def body(*refs):
        ins, outs = refs[:nw], refs[nw:2 * nw]
        ici_send, ici_recv, d2d_send, d2d_recv = refs[2 * nw:]
        x, y, c, me, peers = _place()
        sent = []
        for i in range(nw):
            for p, (px, py) in enumerate(peers):
                cp = _remote(ins[i].at[_half(rows[i], c)], outs[i].at[me, _half(rows[i], c)], ici_send, ici_recv,
                             3 * i + p, (px, py, c))
                cp.start()
                sent.append(cp)
        for p, (px, py) in enumerate(peers):
            for i in range(nw):
                blk = outs[i].at[2 * px + py, _half(rows[i], c)]
                _remote(blk, blk, ici_send, ici_recv, 3 * i + p, (px, py, c)).wait_recv()
                cp = _remote(blk, blk, d2d_send, d2d_recv, 3 * i + p, (x, y, 1 - c))
                cp.start()
                sent.append(cp)
        for p, (px, py) in enumerate(peers):
            for i in range(nw):
                blk = outs[i].at[2 * px + py, _half(rows[i], 1 - c)]
                _remote(blk, blk, d2d_send, d2d_recv, 3 * i + p, (x, y, 1 - c)).wait_recv()
        for cp in sent:
            cp.wait_send()

    dma = pltpu.SemaphoreType.DMA
    return pl.pallas_call(
        body, name="gather_weights", in_specs=[_ANY] * nw, out_specs=[_ANY] * nw,
        out_shape=[jax.ShapeDtypeStruct((N_CHIPS,) + s.shape, s.dtype) for s in shards],
        scratch_shapes=[dma((3 * nw,)), dma((3 * nw,)), dma((3 * nw,)), dma((3 * nw,))],
    )(*shards)


_HBM = pl.BlockSpec(memory_space=pltpu.HBM)
_SEM = pl.BlockSpec(memory_space=pltpu.SEMAPHORE)
_DATAFLOW = pltpu.SideEffectType.DATAFLOW_SIDE_EFFECTING


def _exchange_start(shards, name, scatter):
    nw = len(shards)
    lands = [lax.empty(s.shape if scatter else (N_CHIPS,) + s.shape, s.dtype) for s in shards]

    def body(*refs):
        ins, zones = refs[:nw], refs[nw:2 * nw]
        send_sems, recv_sems, token = refs[2 * nw], refs[2 * nw + 1], refs[-1]
        x, y, c, me, peers = _place()
        for i in range(nw):
            for p, (px, py) in enumerate(peers):
                src = ins[i].at[2 * px + py] if scatter else ins[i]
                _remote(src, zones[i].at[me], send_sems, recv_sems, 3 * i + p, (px, py, c)).start()
        token[...] = jnp.zeros_like(token)

    thru = [pltpu.HBM(a.shape, a.dtype) for a in list(shards) + lands]
    dma = pltpu.SemaphoreType.DMA
    outs = pl.pallas_call(
        body, name=name,
        out_shape=(dma((3 * nw,)), dma((3 * nw,)), *thru, jax.ShapeDtypeStruct((SUBLANES, LANES), F32)),
        in_specs=[_HBM] * (2 * nw),
        out_specs=(_SEM, _SEM, *([_HBM] * (2 * nw)), pl.BlockSpec(memory_space=pltpu.VMEM)),
        input_output_aliases={i: 2 + i for i in range(2 * nw)},
        compiler_params=pltpu.CompilerParams(has_side_effects=_DATAFLOW),
    )(*[pltpu.with_memory_space_constraint(a, pltpu.HBM) for a in list(shards) + lands])
    return outs[0], outs[1], list(outs[2:2 + nw]), list(outs[2 + nw:2 + 2 * nw]), outs[-1]


def _exchange_wait(send_sems, recv_sems, shards, lands, after, name, scatter):
    nw = len(shards)

    def body(*refs):
        ins, zones = refs[:nw], refs[nw:2 * nw]
        send_sems, recv_sems = refs[2 * nw], refs[2 * nw + 1]
        x, y, c, me, peers = _place()
        for i in range(nw):
            for p, (px, py) in enumerate(peers):
                src = ins[i].at[2 * px + py] if scatter else ins[i]
                cp = _remote(src, zones[i].at[2 * px + py], send_sems, recv_sems, 3 * i + p, (px, py, c))
                cp.wait_send()
                cp.wait_recv()

    both = list(shards) + list(lands)
    outs = pl.pallas_call(
        body, name=name,
        out_shape=tuple(pltpu.HBM(a.shape, a.dtype) for a in both),
        in_specs=(*([_HBM] * (2 * nw)), _SEM, _SEM, _ANY), out_specs=[_HBM] * (2 * nw),
        input_output_aliases={i: i for i in range(2 * nw)},
        compiler_params=pltpu.CompilerParams(has_side_effects=_DATAFLOW),
    )(*both, send_sems, recv_sems, after)
    return list(outs[:nw]), list(outs[nw:])


def _sibling_start(grads, name):
    nw = len(grads)
    lands = [lax.empty((N_CHIPS, g.shape[1] // 2, g.shape[2]), g.dtype) for g in grads]

    def body(*refs):
        ins, zones = refs[:nw], refs[nw:2 * nw]
        send_sems, recv_sems, token = refs[2 * nw], refs[2 * nw + 1], refs[-1]
        x, y, c, _, _ = _place()
        for i in range(nw):
            _remote(ins[i].at[pl.ds(0, N_CHIPS), _half(grads[i].shape[1], 1 - c)], zones[i], send_sems, recv_sems,
                    i, (x, y, 1 - c)).start()
        token[...] = jnp.zeros_like(token)

    thru = [pltpu.HBM(a.shape, a.dtype) for a in list(grads) + lands]
    dma = pltpu.SemaphoreType.DMA
    outs = pl.pallas_call(
        body, name=name,
        out_shape=(dma((nw,)), dma((nw,)), *thru, jax.ShapeDtypeStruct((SUBLANES, LANES), F32)),
        in_specs=[_HBM] * (2 * nw),
        out_specs=(_SEM, _SEM, *([_HBM] * (2 * nw)), pl.BlockSpec(memory_space=pltpu.VMEM)),
        input_output_aliases={i: 2 + i for i in range(2 * nw)},
        compiler_params=pltpu.CompilerParams(has_side_effects=_DATAFLOW),
    )(*[pltpu.with_memory_space_constraint(a, pltpu.HBM) for a in list(grads) + lands])
    return outs[0], outs[1], list(outs[2:2 + nw]), list(outs[2 + nw:2 + 2 * nw]), outs[-1]


def _sibling_wait(send_sems, recv_sems, grads, lands, after, name):
    nw = len(grads)

    def body(*refs):
        ins, zones = refs[:nw], refs[nw:2 * nw]
        send_sems, recv_sems = refs[2 * nw], refs[2 * nw + 1]
        x, y, c, _, _ = _place()
        for i in range(nw):
            cp = _remote(ins[i].at[pl.ds(0, N_CHIPS), _half(grads[i].shape[1], 1 - c)], zones[i], send_sems, recv_sems,
                         i, (x, y, 1 - c))
            cp.wait_send()
            cp.wait_recv()

    both = list(grads) + list(lands)
    outs = pl.pallas_call(
        body, name=name,
        out_shape=tuple(pltpu.HBM(a.shape, a.dtype) for a in both),
        in_specs=(*([_HBM] * (2 * nw)), _SEM, _SEM, _ANY), out_specs=[_HBM] * (2 * nw),
        input_output_aliases={i: i for i in range(2 * nw)},
        compiler_params=pltpu.CompilerParams(has_side_effects=_DATAFLOW),
    )(*both, send_sems, recv_sems, after)
    return list(outs[:nw]), list(outs[nw:])


def _reduce_to_sibling(grads, name):
    nw = len(grads)

    def body(*refs):
        ins, outs = refs[:nw], refs[nw:2 * nw]
        send_sems, recv_sems = refs[2 * nw:]
        x, y, c, _, _ = _place()
        sent = []
        for i in range(nw):
            cp = _remote(ins[i].at[pl.ds(0, N_CHIPS), _half(grads[i].shape[1], 1 - c)], outs[i], send_sems, recv_sems,
                         i, (x, y, 1 - c))
            cp.start()
            sent.append(cp)
        for cp in sent:
            cp.wait()

    dma = pltpu.SemaphoreType.DMA
    return pl.pallas_call(
        body, name=name, in_specs=[_ANY] * nw, out_specs=[_ANY] * nw,
        out_shape=[jax.ShapeDtypeStruct((N_CHIPS, g.shape[1] // 2, g.shape[2]), g.dtype) for g in grads],
        scratch_shapes=[dma((nw,)), dma((nw,))],
    )(*grads)


def _reduce_back(totals, name):
    nw = len(totals)

    def body(*refs):
        outs = refs[nw:2 * nw]
        send_sems, recv_sems = refs[2 * nw:]
        x, y, c, _, _ = _place()
        sent = []
        for i in range(nw):
            blk = outs[i].at[_half(totals[i].shape[0], c)]
            cp = _remote(blk, blk, send_sems, recv_sems, i, (x, y, 1 - c))
            cp.start()
            sent.append(cp)
        for i in range(nw):
            blk = outs[i].at[_half(totals[i].shape[0], 1 - c)]
            _remote(blk, blk, send_sems, recv_sems, i, (x, y, 1 - c)).wait_recv()
        for cp in sent:
            cp.wait_send()

    dma = pltpu.SemaphoreType.DMA
    return pl.pallas_call(
        body, name=name, in_specs=[_ANY] * nw, out_specs=[_ANY] * nw,
        out_shape=[jax.ShapeDtypeStruct(t.shape, t.dtype) for t in totals],
        input_output_aliases={i: i for i in range(nw)},
        scratch_shapes=[dma((nw,)), dma((nw,))],
    )(*totals)


def _all_reduce_small(v, name):
    rows, w = v.shape
    hr = rows // 2
    assert hr % SUBLANES == 0

    def body(v_ref, out_ref, sib_ref, half_ref, chips_ref, send_sems, recv_sems):
        x, y, c, me, peers = _place()
        sibling = (x, y, 1 - c)
        mine = pl.ds(pl.multiple_of(c * hr, SUBLANES), hr)
        other = pl.ds(pl.multiple_of((1 - c) * hr, SUBLANES), hr)
        cp = _remote(v_ref, sib_ref, send_sems, recv_sems, 0, sibling)
        cp.start()
        cp.wait()
        half_ref[...] = v_ref[mine, :] + sib_ref[mine, :]
        sent = []
        for p, (px, py) in enumerate(peers):
            cp = _remote(half_ref, chips_ref.at[me], send_sems, recv_sems, 1 + p, (px, py, c))
            cp.start()
            sent.append(cp)
        chips_ref[me] = half_ref[...]
        for p, (px, py) in enumerate(peers):
            _remote(half_ref, chips_ref.at[2 * px + py], send_sems, recv_sems, 1 + p, (px, py, c)).wait_recv()
        for cp in sent:
            cp.wait_send()
        out_ref[mine, :] = ((chips_ref[0] + chips_ref[1]) + chips_ref[2]) + chips_ref[3]
        cp = _remote(out_ref.at[mine], out_ref.at[mine], send_sems, recv_sems, 4, sibling)
        cp.start()
        _remote(out_ref.at[other], out_ref.at[other], send_sems, recv_sems, 4, sibling).wait_recv()
        cp.wait_send()

    vm = pl.BlockSpec(memory_space=pltpu.VMEM)
    return pl.pallas_call(
        body, name=name, in_specs=[vm], out_specs=vm,
        out_shape=jax.ShapeDtypeStruct((rows, w), F32),
        scratch_shapes=[pltpu.VMEM((rows, w), F32), pltpu.VMEM((hr, w), F32), pltpu.VMEM((N_CHIPS, hr, w), F32),
                        pltpu.SemaphoreType.DMA((5,)), pltpu.SemaphoreType.DMA((5,))],
        compiler_params=pltpu.CompilerParams(vmem_limit_bytes=VMEM_LIMIT),
    )(v)


ELEMENTWISE_BLOCK = 512 * 1024


def _rows_tile(rows, cols, cap=ELEMENTWISE_BLOCK):
    best = None
    for t in range(SUBLANES, rows + 1, SUBLANES):
        if rows % t == 0 and t * cols <= cap:
            best = t
    return rows if best is None else best


def _add_pair(g, t, core, name):
    nb, n, w = t.shape
    tr = _rows_tile(n, w, 2 * ELEMENTWISE_BLOCK)
    steps = n // tr

    def body(core_ref, g_ref, t_ref, o_ref):
        o_ref[...] = (g_ref[...] + t_ref[...]).astype(BF16)

    spec = pl.BlockSpec((1, tr, w), lambda j, i, core_ref: (j, i, 0))
    return pl.pallas_call(
        body, name=name,
        grid_spec=pltpu.PrefetchScalarGridSpec(
            num_scalar_prefetch=1, grid=(nb, steps),
            in_specs=[pl.BlockSpec((1, tr, w), lambda j, i, core_ref: (j, core_ref[0] * steps + i, 0)), spec],
            out_specs=spec),
        out_shape=jax.ShapeDtypeStruct(t.shape, BF16),
        compiler_params=_params(("parallel", "parallel")))(core, g, t)


def _add_chips(landed, pairs, place, name):
    nb, n, w = landed.shape
    tr = _rows_tile(n, w, 2 * ELEMENTWISE_BLOCK)
    steps = n // tr

    def body(place_ref, r_ref, own_ref, o_ref):
        me = place_ref[0]
        acc = None
        for k in range(nb):
            blk = jnp.where(me == k, own_ref[0], r_ref[k]).astype(F32)
            acc = blk if acc is None else acc + blk
        o_ref[...] = acc

    return pl.pallas_call(
        body, name=name,
        grid_spec=pltpu.PrefetchScalarGridSpec(
            num_scalar_prefetch=1, grid=(steps,),
            in_specs=[pl.BlockSpec((nb, tr, w), lambda i, place_ref: (0, i, 0)),
                      pl.BlockSpec((1, tr, w), lambda i, place_ref: (place_ref[0], i, 0))],
            out_specs=pl.BlockSpec((tr, w), lambda i, place_ref: (place_ref[1] * steps + i, 0))),
        out_shape=jax.ShapeDtypeStruct((2 * n, w), F32),
        compiler_params=_params(("parallel",)))(place, landed, pairs)


def _adamw(w, g, m, v, name):
    rows, wd = w.shape
    tr = _rows_tile(rows, wd)
    c1 = 1.0 - ADAM_B1 ** ADAM_STEP
    c2 = 1.0 - ADAM_B2 ** ADAM_STEP

    def body(w_ref, g_ref, m_ref, v_ref, go_ref, d_ref, mo_ref, vo_ref):
        gv = g_ref[...]
        go_ref[...] = gv
        m2 = ADAM_B1 * m_ref[...] + (1.0 - ADAM_B1) * gv
        v2 = ADAM_B2 * v_ref[...] + (1.0 - ADAM_B2) * (gv * gv)
        mo_ref[...] = m2
        vo_ref[...] = v2
        d_ref[...] = -ADAM_LR * ((m2 / c1) / (jnp.sqrt(v2 / c2) + ADAM_EPS) + ADAM_WD * w_ref[...])

    spec = pl.BlockSpec((tr, wd), lambda i: (i, 0))
    shp = jax.ShapeDtypeStruct((rows, wd), F32)
    return pl.pallas_call(body, name=name, grid=(rows // tr,), in_specs=[spec] * 4, out_specs=[spec] * 4,
                          out_shape=[shp] * 4, compiler_params=_params(("parallel",)))(w, g, m, v)


BIG = [("w_in", (1024, 3232), 1), ("w_uq", (384, 768), 1), ("w_uk", (256, 512), 1), ("w_uv", (256, 512), 1),
       ("w_glu", (512, 512), 0), ("w_branch_attn", (512, 1024), 1), ("w_branch_ssm", (512, 1024), 1),
       ("w_out", (1024, 1024), 0), ("w_up", (1024, 5632), 1), ("conv_w", (3, 5632), 1), ("w_down", (2816, 1024), 0)]
SMALL = [("mix_norm_pre", (1024,)), ("q_norm", (384,)), ("kv_norm", (256,)), ("ssm_lambda_re", (32, 64)),
         ("ssm_lambda_im", (32, 64)), ("ssm_log_dt", (32,)), ("ssm_b_re", (32, 64, 16)), ("ssm_b_im", (32, 64, 16)),
         ("ssm_c_re", (32, 16, 64)), ("ssm_c_im", (32, 16, 64)), ("ssm_d", (32, 16)), ("b_glu", (512,)),
         ("b_gate", (2048,)), ("mix_norm_post", (1024,)), ("ffn_norm_pre", (1024,)), ("conv_b", (5632,)),
         ("ffn_norm_post", (1024,))]
MATMUL_W = [b for b in BIG if b[0] != "conv_w"]
LATE_W = ("w_up", "w_down", "conv_w")
CONV_W_SHAPE = (3, 2 * D_FF)
CONV_W_SHARD = (3, 2 * D_FF // N_CHIPS)
SMALL_SUM = [("loss", (1,))] + SMALL + [("conv_w", CONV_W_SHAPE)]
SMALL_ADAM = SMALL + [("conv_w", CONV_W_SHARD)]


def _pack_flat(layout, vals):
    flat = jnp.concatenate([vals[n].astype(F32).reshape(-1) for n, _ in layout])
    rows = -(-(-(-flat.shape[0] // FLAT_W)) // (2 * SUBLANES)) * 2 * SUBLANES
    return jnp.pad(flat, (0, rows * FLAT_W - flat.shape[0])).reshape(rows, FLAT_W)


def _unpack_flat(layout, flat):
    flat = flat.reshape(-1)
    out = {}
    o = 0
    for name, shape in layout:
        n = math.prod(shape)
        out[name] = flat[o:o + n].reshape(shape)
        o += n
    return out


_ARG_NAMES = ["x", "positions"] + [n for n in (
    "mix_norm_pre", "w_in", "q_norm", "w_uq", "kv_norm", "w_uk", "w_uv", "ssm_lambda_re", "ssm_lambda_im", "ssm_log_dt",
    "ssm_b_re", "ssm_b_im", "ssm_c_re", "ssm_c_im", "ssm_d", "w_glu", "b_glu", "w_branch_attn", "w_branch_ssm",
    "b_gate", "w_out", "mix_norm_post", "ffn_norm_pre", "w_up", "conv_w", "conv_b", "w_down", "ffn_norm_post")]
_WEIGHTS = _ARG_NAMES[2:]


def _gather_weights(w):
    early = [b for b in MATMUL_W if b[0] not in LATE_W]
    late = [b for b in BIG if b[0] in LATE_W]
    own = (jnp.arange(N_CHIPS) == 2 * lax.axis_index("x") + lax.axis_index("y"))[:, None, None]

    def whole(layout, mine, gathered):
        return {name: _from_chip_major(jnp.where(own, s[None], g), axis)
                for (name, _, axis), s, g in zip(layout, mine, gathered)}

    mine = [w[name].astype(BF16) for name, _, _ in early]
    gathered = _gather_big(mine)
    full = whole(early, mine, gathered)
    mine_late = [w[name].astype(F32 if name == "conv_w" else BF16) for name, _, _ in late]
    _, mine_late = lax.optimization_barrier((gathered[0], mine_late))
    send_sems, recv_sems, shards_thru, lands_thru, token = _exchange_start(mine_late, "gather_late_start", scatter=False)

    def late_weights(after):
        shards, lands = _exchange_wait(send_sems, recv_sems, shards_thru, lands_thru, after, "gather_late_wait",
                                       scatter=False)
        return whole(late, shards, lands)

    full["late"] = late_weights
    full["token"] = token[0, 0]
    return full


def _pair_sums(names, grads, tag):
    core = lax.axis_index("c").astype(jnp.int32).reshape(1)
    theirs = _reduce_to_sibling(grads, "reduce_grads_d2d" + tag)
    return [_add_pair(g, t, core, "reduce_pair_" + n) for n, g, t in zip(names, grads, theirs)]


def _send_grads(tag, grads, after, flying, pending):
    token = jnp.zeros((), F32)
    if flying:
        tag0, names0, state0 = flying.pop()
        core = lax.axis_index("c").astype(jnp.int32).reshape(1)
        mine, theirs = _sibling_wait(*state0, after, "reduce_" + tag0 + "_d2d_wait")
        pairs = [_add_pair(g, t, core, "reduce_pair_" + n) for n, g, t in zip(names0, mine, theirs)]
        send_sems, recv_sems, pairs_thru, lands_thru, tok = _exchange_start(pairs, "reduce_" + tag0 + "_start",
                                                                          scatter=True)
        pending.append((tag0, names0, send_sems, recv_sems, pairs_thru, lands_thru))
        token = token + tok[0, 0]
    if grads:
        names = list(grads)
        send_sems, recv_sems, grads_thru, lands_thru, tok = _sibling_start([grads[n] for n in names],
                                                                          "reduce_" + tag + "_d2d_start")
        flying.append((tag, names, (send_sems, recv_sems, grads_thru, lands_thru)))
        token = token + tok[0, 0]
    return token


def _reduce_grads(gbig, pending, loss, gsmall, use_sent):
    core = lax.axis_index("c").astype(jnp.int32).reshape(1)
    chip = (2 * lax.axis_index("x") + lax.axis_index("y")).astype(jnp.int32).reshape(1)
    place = jnp.concatenate([chip, core])

    def finish(names, pairs, landed, name):
        totals = [_add_chips(r, p, place, "reduce_chips_" + n) for n, r, p in zip(names, landed, pairs)]
        return dict(zip(names, _reduce_back(totals, name)))

    names = list(gbig)
    pairs = _pair_sums(names, [gbig[n] for n in names], "")
    send_sems, recv_sems, pairs_thru, lands_thru, token = _exchange_start(pairs, "reduce_last_start", scatter=True)
    sent_names, sent_pairs, sent_landed = [], [], []
    for tag, group, g_send, g_recv, g_pairs, g_lands in pending:
        got_pairs, got_landed = _exchange_wait(g_send, g_recv, g_pairs, g_lands, token, "reduce_" + tag + "_wait",
                                               scatter=True)
        sent_names, sent_pairs, sent_landed = sent_names + group, sent_pairs + got_pairs, sent_landed + got_landed
    g_sent = finish(sent_names, sent_pairs, sent_landed, "reduce_back_sent")
    vals = dict(gsmall)
    vals["loss"] = loss
    small_red = _unpack_flat(SMALL_SUM, _all_reduce_small(_pack_flat(SMALL_SUM, vals), "reduce_small"))
    after = use_sent(g_sent, small_red)
    pairs, landed = _exchange_wait(send_sems, recv_sems, pairs_thru, lands_thru, after, "reduce_last_wait", scatter=True)
    return finish(names, pairs, landed, "reduce_back_last"), small_red


def _step(args):
    x = args["x"][0]
    positions = args["positions"][0]
    tgt = args["loss_target"][0]
    w = {n: args[n][0] for n in _WEIGHTS}
    m = {n: args["m_" + n][0] for n in _WEIGHTS}
    v = {n: args["v_" + n][0] for n in _WEIGHTS}

    full = _gather_weights(w)
    sp = {n: w[n].reshape(s) for n, s in SMALL}
    for n in ("mix_norm_pre", "q_norm", "kv_norm", "b_glu", "b_gate", "mix_norm_post", "ffn_norm_pre", "conv_b",
              "ffn_norm_post"):
        sp[n] = sp[n].reshape(1, -1)
    sp["mix_norm_pre"] = sp["mix_norm_pre"] + full.pop("token")
    pending, flying = [], []
    full["send_grads"] = lambda tag, grads, after=None: _send_grads(tag, grads, after, flying, pending)
    loss, gx, gbig, gsmall = _local_step(x, positions, tgt, full, sp)
    outs = {}

    def adam_big(g_red):
        for name in g_red:
            g2, d, m2, v2 = _adamw(w[name], g_red[name], m[name], v[name], "adamw_" + name)
            outs["grad_" + name], outs["delta_" + name], outs["new_m_" + name], outs["new_v_" + name] = g2, d, m2, v2
        return v2

    def use_sent(g_sent, small_red):
        chip = 2 * lax.axis_index("x") + lax.axis_index("y")
        grads = dict(small_red)
        grads["conv_w"] = lax.dynamic_slice_in_dim(small_red["conv_w"], chip * CONV_W_SHARD[1], CONV_W_SHARD[1], axis=1)
        outs.update({"grad_" + n: grads[n] for n, _ in SMALL_ADAM})
        _, d_sm, m_sm, v_sm = _adamw(_pack_flat(SMALL_ADAM, w), _pack_flat(SMALL_ADAM, grads),
                                     _pack_flat(SMALL_ADAM, m), _pack_flat(SMALL_ADAM, v), "adamw_small")
        for prefix, flat in (("delta_", d_sm), ("new_m_", m_sm), ("new_v_", v_sm)):
            for n, val in _unpack_flat(SMALL_ADAM, flat).items():
                outs[prefix + n] = val
        return adam_big(g_sent)

    g_last, small_red = _reduce_grads(gbig, pending, loss, gsmall, use_sent)
    adam_big(g_last)
    outs = {n: val.reshape(args[n.split("_", 1)[1] if not n.startswith("new_") else n[6:]].shape)
            for n, val in outs.items()}
    res = [small_red["loss"][0], gx[None]]
    for prefix in ("grad_", "delta_", "new_m_", "new_v_"):
        res += [outs[prefix + n] for n in _WEIGHTS]
    return tuple(res)


def kernel(x, positions, mix_norm_pre, w_in, q_norm, w_uq, kv_norm, w_uk, w_uv, ssm_lambda_re, ssm_lambda_im, ssm_log_dt, ssm_b_re, ssm_b_im, ssm_c_re, ssm_c_im, ssm_d, w_glu, b_glu, w_branch_attn, w_branch_ssm, b_gate, w_out, mix_norm_post, ffn_norm_pre, w_up, conv_w, conv_b, w_down, ffn_norm_post, loss_target, m_mix_norm_pre, m_w_in, m_q_norm, m_w_uq, m_kv_norm, m_w_uk, m_w_uv, m_ssm_lambda_re, m_ssm_lambda_im, m_ssm_log_dt, m_ssm_b_re, m_ssm_b_im, m_ssm_c_re, m_ssm_c_im, m_ssm_d, m_w_glu, m_b_glu, m_w_branch_attn, m_w_branch_ssm, m_b_gate, m_w_out, m_mix_norm_post, m_ffn_norm_pre, m_w_up, m_conv_w, m_conv_b, m_w_down, m_ffn_norm_post, v_mix_norm_pre, v_w_in, v_q_norm, v_w_uq, v_kv_norm, v_w_uk, v_w_uv, v_ssm_lambda_re, v_ssm_lambda_im, v_ssm_log_dt, v_ssm_b_re, v_ssm_b_im, v_ssm_c_re, v_ssm_c_im, v_ssm_d, v_w_glu, v_b_glu, v_w_branch_attn, v_w_branch_ssm, v_b_gate, v_w_out, v_mix_norm_post, v_ffn_norm_pre, v_w_up, v_conv_w, v_conv_b, v_w_down, v_ffn_norm_post):
    given = dict(locals())
    return _step(given)
```

```python
import math

import jax
import jax.numpy as jnp
from jax import lax
from jax.experimental import pallas as pl
from jax.experimental.pallas import tpu as pltpu

F32 = jnp.float32
BF16 = jnp.bfloat16
MESH = pl.DeviceIdType.MESH

D_MODEL = 1024
N_HEADS = 8
QK_NOPE = 64
QK_ROPE = 32
QK_HEAD = QK_NOPE + QK_ROPE
V_HEAD = 64
Q_RANK = 384
KV_RANK = 256
ROPE_THETA = 10000.0
SSM_W = 512
SSM_H = 16
SSM_G = 32
SSM_P = 64
SSM_CH = SSM_G * SSM_P
D_FF = 2816
EPS = 1e-6
ADAM_LR = 0.001
ADAM_B1 = 0.9
ADAM_B2 = 0.999
ADAM_EPS = 1e-08
ADAM_WD = 0.01
ADAM_STEP = 10

LANES = 128
SUBLANES = 8
VMEM_LIMIT = 56 * 1024 * 1024

HEAD_SLOT = LANES
HP = N_HEADS * HEAD_SLOT
P_CQ, P_CKV, P_KR, P_U, P_GL, P_END = 0, 384, 640, 768, 1280, 3328
KR_LANE = 64

FLAT_W = 1024
N_CHIPS = 4


def _tile(n, cap):
    if n <= cap:
        return n
    best = None
    for t in range(LANES, cap + 1, LANES):
        if n % t == 0:
            best = t
    assert best is not None, (n, cap)
    return best


def _params(sem):
    return pltpu.CompilerParams(dimension_semantics=sem, vmem_limit_bytes=VMEM_LIMIT)


def _dot(a, b):
    return jnp.dot(a, b, preferred_element_type=F32)


def _dot_nt(a, b):
    return lax.dot_general(a, b, (((1,), (1,)), ((), ())), preferred_element_type=F32)


def _dot_tn(a, b):
    return lax.dot_general(a, b, (((0,), (0,)), ((), ())), preferred_element_type=F32)


def _rms(x, g):
    r = lax.rsqrt(jnp.mean(x * x, axis=-1, keepdims=True) + EPS)
    return x * r * g, r


def _rms_bwd(dy, x, g):
    r = lax.rsqrt(jnp.mean(x * x, axis=-1, keepdims=True) + EPS)
    dyg = dy * g
    dx = r * dyg - x * (r * r * r) * jnp.mean(dyg * x, axis=-1, keepdims=True)
    dg = jnp.sum(dy * x * r, axis=0, keepdims=True)
    return dx, dg


_GELU_K0 = math.sqrt(2.0 / math.pi)
_GELU_K1 = 0.044715


def _gelu(x):
    th = jnp.tanh(_GELU_K0 * (x + _GELU_K1 * x * x * x))
    return 0.5 * x * (1.0 + th)


def _gelu_grad(x):
    th = jnp.tanh(_GELU_K0 * (x + _GELU_K1 * x * x * x))
    return 0.5 * (1.0 + th) + 0.5 * x * (1.0 - th * th) * _GELU_K0 * (1.0 + 3.0 * _GELU_K1 * x * x)


def _sigmoid(x):
    return 1.0 / (1.0 + jnp.exp(-x))


def _rope(q, c, s):
    n = q.shape[1]
    lane = lax.broadcasted_iota(jnp.int32, q.shape, 1) % HEAD_SLOT
    sw = jnp.where(lane < KR_LANE + QK_ROPE // 2, pltpu.roll(q, n - QK_ROPE // 2, 1), pltpu.roll(q, QK_ROPE // 2, 1))
    return q * c + sw * s


def _rope_bwd(dy, c, s):
    n = dy.shape[1]
    t = dy * s
    lane = lax.broadcasted_iota(jnp.int32, dy.shape, 1) % HEAD_SLOT
    sw = jnp.where(lane < KR_LANE + QK_ROPE // 2, pltpu.roll(t, n - QK_ROPE // 2, 1), pltpu.roll(t, QK_ROPE // 2, 1))
    rope_lane = jnp.logical_and(lane >= KR_LANE, lane < KR_LANE + QK_ROPE)
    return dy * c + jnp.where(rope_lane, sw, 0.0)


def _shift_down(x, k, halo):
    xs = pltpu.roll(x, k, 0)
    hs = pltpu.roll(halo, k, 0)
    rows = lax.broadcasted_iota(jnp.int32, halo.shape, 0)
    top = jnp.where(rows < k, hs, xs[0:SUBLANES])
    return jnp.concatenate([top, xs[SUBLANES:]], axis=0)


def _shift_up(x, k, halo):
    t = x.shape[0]
    xs = pltpu.roll(x, t - k, 0)
    hs = pltpu.roll(halo, SUBLANES - k, 0)
    rows = lax.broadcasted_iota(jnp.int32, halo.shape, 0)
    bot = jnp.where(rows >= SUBLANES - k, hs, xs[t - SUBLANES:])
    return jnp.concatenate([xs[:t - SUBLANES], bot], axis=0)


def _mm(a, b, name, out_dtype=F32, bt=False, tm_cap=1024, tn_cap=1408):
    m, k = a.shape[-2:]
    parts = a.shape[0] if a.ndim == 3 else 1
    n = b.shape[0] if bt else b.shape[1]
    tm = min(tm_cap, m)
    tn = _tile(n, tn_cap)

    def body(a_ref, b_ref, o_ref):
        if not bt:
            o_ref[...] = _dot(a_ref[...], b_ref[...]).astype(out_dtype)
        elif parts == 1:
            o_ref[...] = _dot_nt(a_ref[...], b_ref[...]).astype(out_dtype)
        else:
            @pl.when(pl.program_id(2) == 0)
            def _():
                o_ref[...] = jnp.zeros_like(o_ref)

            o_ref[...] += _dot_nt(a_ref[...], b_ref[...])

    if bt:
        b_spec = pl.BlockSpec((tn, k), lambda j, i, s: (j, s))
    else:
        b_spec = pl.BlockSpec((k, tn), lambda j, i, s: (0, j))
    if a.ndim == 3:
        assert bt and out_dtype == F32
        a_spec = pl.BlockSpec((None, tm, k), lambda j, i, s: (s, i, 0))
    else:
        a_spec = pl.BlockSpec((tm, k), lambda j, i, s: (i, 0))
    return pl.pallas_call(
        body, name=name, grid=(n // tn, m // tm, parts),
        in_specs=[a_spec, b_spec],
        out_specs=pl.BlockSpec((tm, tn), lambda j, i, s: (i, j)),
        out_shape=jax.ShapeDtypeStruct((m, n), out_dtype),
        compiler_params=_params(("parallel", "parallel", "arbitrary")),
    )(a, b)


def _mm_tn(a, b, name, tk_cap=1024, tn_cap=1664, tl_cap=2048, chips=False):
    l, k = a.shape
    tk = _tile(k, tk_cap)
    tl = min(tl_cap, l)

    def body(a_ref, b_ref, o_ref):
        @pl.when(pl.program_id(2) == 0)
        def _():
            o_ref[...] = jnp.zeros_like(o_ref)

        o_ref[...] += _dot_tn(a_ref[...], b_ref[...])

    if chips:
        n = b.shape[-1] * (b.shape[0] if b.ndim == 3 else 1)
        tn = n // N_CHIPS
        assert tn % LANES == 0
        if b.ndim == 3:
            per = N_CHIPS // b.shape[0]
            b_spec = pl.BlockSpec((None, tl, tn), lambda i, j, r: (j // per, r, j % per))
        else:
            b_spec = pl.BlockSpec((tl, tn), lambda i, j, r: (r, j))
        out_spec = pl.BlockSpec((None, tk, tn), lambda i, j, r: (j, i, 0))
        out_shape = jax.ShapeDtypeStruct((N_CHIPS, k, tn), F32)
    else:
        n = b.shape[1]
        tn = _tile(n, tn_cap)
        b_spec = pl.BlockSpec((tl, tn), lambda i, j, r: (r, j))
        out_spec = pl.BlockSpec((tk, tn), lambda i, j, r: (i, j))
        out_shape = jax.ShapeDtypeStruct((k, n), F32)
    return pl.pallas_call(
        body, name=name, grid=(k // tk, n // tn, l // tl),
        in_specs=[pl.BlockSpec((tl, tk), lambda i, j, r: (r, i)), b_spec],
        out_specs=out_spec, out_shape=out_shape,
        compiler_params=_params(("parallel", "parallel", "arbitrary")),
    )(a, b)


def _row(tl, n):
    return pl.BlockSpec((tl, n), lambda i: (i, 0))


def _const(shape):
    return pl.BlockSpec(shape, lambda i: tuple(0 for _ in shape))


def _proj_fwd(x, g1, win, gq, wuq, gkv, wukv, rc, rs, bg, tl):
    l = x.shape[0]

    def body(x_ref, g1_ref, win_ref, gq_ref, wuq_ref, gkv_ref, wukv_ref, rc_ref, rs_ref, bg_ref,
             hn_ref, cq_ref, ckv_ref, q_ref, k_ref, v_ref, u_ref, gl_ref):
        hn, _ = _rms(x_ref[...], g1_ref[...])
        hnb = hn.astype(BF16)
        hn_ref[...] = hnb
        proj = _dot(hnb, win_ref[...])
        cq = proj[:, P_CQ:P_CKV]
        ckv = proj[:, P_CKV:P_KR]
        kr = proj[:, P_KR:P_U]
        cq_ref[...] = cq
        ckv_ref[...] = ckv
        u_ref[...] = proj[:, P_U:P_GL]
        gl_ref[...] = (proj[:, P_GL:P_END] + bg_ref[...]).astype(BF16)
        qn, _ = _rms(cq, gq_ref[...])
        q = _dot(qn.astype(BF16), wuq_ref[...])
        c1 = rc_ref[...]
        s1 = rs_ref[...]
        q_ref[...] = (_rope(q, jnp.tile(c1, (1, N_HEADS)), jnp.tile(s1, (1, N_HEADS))) * Q_PRESCALE).astype(BF16)
        ckvn, _ = _rms(ckv, gkv_ref[...])
        kv = _dot(ckvn.astype(BF16), wukv_ref[...])
        krr = _rope(kr, c1, s1)
        k_ref[...] = (kv[:, :HP] + jnp.tile(krr, (1, N_HEADS))).astype(BF16)
        v_ref[...] = kv[:, HP:].astype(BF16)

    outs = [(D_MODEL, BF16), (Q_RANK, F32), (KV_RANK, F32), (HP, BF16), (HP, BF16), (HP, BF16),
            (SSM_W, F32), (2 * D_MODEL, BF16)]
    return pl.pallas_call(
        body, name="proj_fwd", grid=(l // tl,),
        in_specs=[_row(tl, D_MODEL), _const((1, D_MODEL)), _const((D_MODEL, P_END)), _const((1, Q_RANK)),
                  _const((Q_RANK, HP)), _const((1, KV_RANK)), _const((KV_RANK, 2 * HP)),
                  _row(tl, HEAD_SLOT), _row(tl, HEAD_SLOT), _const((1, 2 * D_MODEL))],
        out_specs=[_row(tl, n) for n, _ in outs],
        out_shape=[jax.ShapeDtypeStruct((l, n), dt) for n, dt in outs],
        compiler_params=_params(("parallel",)),
    )(x, g1, win, gq, wuq, gkv, wukv, rc, rs, bg)


_NEG = -1e30


LOG2E = 1.0 / math.log(2.0)
LN2 = math.log(2.0)
ATTN_SCALE = 1.0 / math.sqrt(QK_HEAD)
Q_PRESCALE = ATTN_SCALE * LOG2E
FWD_HEADS = 8
BWD_HEADS = 8


def _causal_pairs(nq, by_query):
    if by_query:
        pairs = [(i, j) for i in range(nq) for j in range(i + 1)]
    else:
        pairs = [(i, j) for j in range(nq) for i in range(j, nq)]
    return jnp.array([p[0] for p in pairs], jnp.int32), jnp.array([p[1] for p in pairs], jnp.int32)


def _diag_mask_t(s):
    rows = lax.broadcasted_iota(jnp.int32, s.shape, 0)
    cols = lax.broadcasted_iota(jnp.int32, s.shape, 1)
    return jnp.where(rows <= cols, s, _NEG)


def _attn_fwd(q, k, v, tq, heads):
    l = q.shape[0]
    nq = l // tq
    it, jt = _causal_pairs(nq, True)

    def body(it_ref, jt_ref, q_ref, k_ref, v_ref, o_ref, lse_ref, m_ref, l_ref, acc_ref):
        t = pl.program_id(1)
        i = it_ref[t]
        j = jt_ref[t]

        @pl.when(j == 0)
        def _():
            m_ref[...] = jnp.full_like(m_ref, _NEG)
            l_ref[...] = jnp.zeros_like(l_ref)
            acc_ref[...] = jnp.zeros_like(acc_ref)

        def update(on_diagonal):
            for hh in range(heads):
                sl = slice(hh * HEAD_SLOT, (hh + 1) * HEAD_SLOT)
                s = _dot_nt(k_ref[:, sl], q_ref[:, sl])
                if on_diagonal:
                    s = _diag_mask_t(s)
                m_old = m_ref[hh]
                m_new = jnp.maximum(m_old, jnp.max(s, axis=0, keepdims=True))
                p = jnp.exp2(s - m_new)
                alpha = jnp.exp2(m_old - m_new)
                l_ref[hh] = alpha * l_ref[hh] + jnp.sum(p, axis=0, keepdims=True)
                acc_ref[hh] = alpha * acc_ref[hh] + _dot_tn(v_ref[:, sl], p.astype(BF16))
                m_ref[hh] = m_new

        @pl.when(j < i)
        def _():
            update(False)

        @pl.when(j == i)
        def _():
            update(True)
            for hh in range(heads):
                sl = slice(hh * HEAD_SLOT, (hh + 1) * HEAD_SLOT)
                o_ref[:, sl] = (acc_ref[hh] / l_ref[hh]).T.astype(BF16)
                lse_ref[hh] = m_ref[hh] + jnp.log(l_ref[hh]) * LOG2E

    blk = (tq, (heads * HEAD_SLOT))
    qmap = lambda h, t, it_ref, jt_ref: (it_ref[t], h)
    kmap = lambda h, t, it_ref, jt_ref: (jt_ref[t], h)
    row = pl.BlockSpec((heads, 1, tq), lambda h, t, it_ref, jt_ref: (h, 0, it_ref[t]))
    return pl.pallas_call(
        body, name="attn_fwd",
        grid_spec=pltpu.PrefetchScalarGridSpec(
            num_scalar_prefetch=2, grid=(N_HEADS // heads, it.shape[0]),
            in_specs=[pl.BlockSpec(blk, qmap), pl.BlockSpec(blk, kmap), pl.BlockSpec(blk, kmap)],
            out_specs=[pl.BlockSpec(blk, qmap), row],
            scratch_shapes=[pltpu.VMEM((heads, 1, tq), F32), pltpu.VMEM((heads, 1, tq), F32),
                            pltpu.VMEM((heads, HEAD_SLOT, tq), F32)]),
        out_shape=[jax.ShapeDtypeStruct((l, HP), BF16), jax.ShapeDtypeStruct((N_HEADS, 1, l), F32)],
        compiler_params=_params(("parallel", "arbitrary")),
    )(it, jt, q, k, v)


def _attn_delta(o, do, tq, heads):
    l = o.shape[0]

    def body(o_ref, do_ref, d_ref):
        prod = o_ref[...].astype(F32) * do_ref[...].astype(F32)
        for hh in range(heads):
            d_ref[hh] = jnp.sum(prod[:, hh * HEAD_SLOT:(hh + 1) * HEAD_SLOT].T, axis=0, keepdims=True)

    blk = pl.BlockSpec((tq, (heads * HEAD_SLOT)), lambda h, i: (i, h))
    return pl.pallas_call(
        body, name="attn_delta", grid=(N_HEADS // heads, l // tq), in_specs=[blk, blk],
        out_specs=pl.BlockSpec((heads, 1, tq), lambda h, i: (h, 0, i)),
        out_shape=jax.ShapeDtypeStruct((N_HEADS, 1, l), F32),
        compiler_params=_params(("parallel", "parallel")),
    )(o, do)


def _attn_bwd(q, k, v, do, lse, delta, tq, heads):
    l = q.shape[0]
    nq = l // tq
    it, jt = _causal_pairs(nq, False)

    def body(it_ref, jt_ref, q_ref, k_ref, v_ref, do_ref, lse_ref, dl_ref, dq_ref, dk_ref, dv_ref, dka_ref, dva_ref):
        t = pl.program_id(1)
        i = it_ref[t]
        j = jt_ref[t]

        @pl.when(t == 0)
        def _():
            dq_ref[...] = jnp.zeros_like(dq_ref)

        @pl.when(i == j)
        def _():
            dka_ref[...] = jnp.zeros_like(dka_ref)
            dva_ref[...] = jnp.zeros_like(dva_ref)

        def update(on_diagonal):
            r0 = pl.multiple_of(i * tq, tq)
            for hh in range(heads):
                sl = slice(hh * HEAD_SLOT, (hh + 1) * HEAD_SLOT)
                qb = q_ref[:, sl]
                kb = k_ref[:, sl]
                dob = do_ref[:, sl]
                s = _dot_nt(kb, qb)
                if on_diagonal:
                    s = _diag_mask_t(s)
                p = jnp.exp2(s - lse_ref[hh])
                dva_ref[:, sl] += _dot(p.astype(BF16), dob)
                dp = _dot_nt(v_ref[:, sl], dob)
                ds = (p * (dp - dl_ref[hh])).astype(BF16)
                dka_ref[:, sl] += _dot(ds, qb)
                dq_ref[pl.ds(r0, tq), sl] += ATTN_SCALE * _dot_tn(ds, kb)

        @pl.when(j < i)
        def _():
            update(False)

        @pl.when(j == i)
        def _():
            update(True)

        @pl.when(i == nq - 1)
        def _():
            dk_ref[...] = (dka_ref[...] * LN2).astype(BF16)
            dv_ref[...] = dva_ref[...].astype(BF16)

    blk = (tq, (heads * HEAD_SLOT))
    qmap = lambda h, t, it_ref, jt_ref: (it_ref[t], h)
    kmap = lambda h, t, it_ref, jt_ref: (jt_ref[t], h)
    row = pl.BlockSpec((heads, 1, tq), lambda h, t, it_ref, jt_ref: (h, 0, it_ref[t]))
    return pl.pallas_call(
        body, name="attn_bwd",
        grid_spec=pltpu.PrefetchScalarGridSpec(
            num_scalar_prefetch=2, grid=(N_HEADS // heads, it.shape[0]),
            in_specs=[pl.BlockSpec(blk, qmap), pl.BlockSpec(blk, kmap), pl.BlockSpec(blk, kmap),
                      pl.BlockSpec(blk, qmap), row, row],
            out_specs=[pl.BlockSpec((l, (heads * HEAD_SLOT)), lambda h, t, it_ref, jt_ref: (0, h)), pl.BlockSpec(blk, kmap),
                       pl.BlockSpec(blk, kmap)],
            scratch_shapes=[pltpu.VMEM(blk, F32), pltpu.VMEM(blk, F32)]),
        out_shape=[jax.ShapeDtypeStruct((l, HP), F32), jax.ShapeDtypeStruct((l, HP), BF16),
                   jax.ShapeDtypeStruct((l, HP), BF16)],
        compiler_params=_params(("parallel", "arbitrary")),
    )(it, jt, q, k, v, do, lse, delta)


SSM_CB = 512
SSM_UB = 128
SSM_NB = SSM_CH // SSM_CB


def _scan_tiles(re_ref, im_ref, tab, carry, n_tiles, reverse):
    group = 2
    assert n_tiles % group == 0
    pr, pi = tab[6], tab[7]

    def inside(sr, si):
        for step, k in enumerate((1, 2, 4)):
            mr, mi = tab[2 * step], tab[2 * step + 1]
            sh = (SUBLANES - k) if reverse else k
            rr = pltpu.roll(sr, sh, 0)
            ri = pltpu.roll(si, sh, 0)
            sr, si = sr + mr * rr - mi * ri, si + mr * ri + mi * rr
        return sr, si

    def body(n, c):
        cr, ci = c
        first = (n_tiles - group * (n + 1)) if reverse else group * n
        r0 = pl.multiple_of(first * SUBLANES, group * SUBLANES)
        rows = [pl.ds(r0 + g * SUBLANES, SUBLANES) for g in range(group)]
        tiles = [inside(re_ref[r, :], im_ref[r, :]) for r in rows]
        for g in (range(group - 1, -1, -1) if reverse else range(group)):
            sr, si = tiles[g]
            sr, si = sr + pr * cr - pi * ci, si + pr * ci + pi * cr
            re_ref[rows[g], :] = sr
            im_ref[rows[g], :] = si
            edge = slice(0, 1) if reverse else slice(SUBLANES - 1, SUBLANES)
            cr, ci = sr[edge, :], si[edge, :]
        return cr, ci

    return lax.fori_loop(0, n_tiles // group, body, carry)


def _ssm_fwd(u, bre, bim, cre, cim, dvec, tab, tt):
    l = u.shape[0]
    nt = l // tt

    def body(u_ref, bre_ref, bim_ref, cre_ref, cim_ref, d_ref, tab_ref, y_ref, sre_ref, sim_ref, car_ref):
        @pl.when(pl.program_id(1) == 0)
        def _():
            car_ref[...] = jnp.zeros_like(car_ref)

        uf = u_ref[...]
        ub = uf.astype(BF16)
        sre_ref[...] = _dot(ub, bre_ref[0])
        sim_ref[...] = _dot(ub, bim_ref[0])
        tab_v = [tab_ref[n] for n in range(8)]
        cr, ci = _scan_tiles(sre_ref, sim_ref, tab_v, (car_ref[0:1, :], car_ref[8:9, :]), tt // SUBLANES, False)
        car_ref[0:1, :] = cr
        car_ref[8:9, :] = ci
        y_ref[...] = (_dot(sre_ref[...].astype(BF16), cre_ref[0]) - _dot(sim_ref[...].astype(BF16), cim_ref[0])
                      + d_ref[...] * uf)

    return pl.pallas_call(
        body, name="ssm_fwd", grid=(SSM_NB, nt),
        in_specs=[pl.BlockSpec((tt, SSM_UB), lambda m, t: (t, m)),
                  pl.BlockSpec((1, SSM_UB, SSM_CB), lambda m, t: (m, 0, 0)),
                  pl.BlockSpec((1, SSM_UB, SSM_CB), lambda m, t: (m, 0, 0)),
                  pl.BlockSpec((1, SSM_CB, SSM_UB), lambda m, t: (m, 0, 0)),
                  pl.BlockSpec((1, SSM_CB, SSM_UB), lambda m, t: (m, 0, 0)),
                  pl.BlockSpec((1, SSM_UB), lambda m, t: (0, m)),
                  pl.BlockSpec((8, SUBLANES, SSM_CB), lambda m, t: (0, 0, m))],
        out_specs=[pl.BlockSpec((tt, SSM_UB), lambda m, t: (t, m)),
                   pl.BlockSpec((tt, SSM_CB), lambda m, t: (t, m)),
                   pl.BlockSpec((tt, SSM_CB), lambda m, t: (t, m))],
        out_shape=[jax.ShapeDtypeStruct((l, SSM_W), F32), jax.ShapeDtypeStruct((l, SSM_CH), F32),
                   jax.ShapeDtypeStruct((l, SSM_CH), F32)],
        scratch_shapes=[pltpu.VMEM((2 * SUBLANES, SSM_CB), F32)],
        compiler_params=_params(("parallel", "arbitrary")),
    )(u, bre, bim, cre, cim, dvec, tab)


def _ssm_bwd(dy, u, sre, sim, bre, bim, cre, cim, dvec, tab, tt):
    l = u.shape[0]
    nt = l // tt
    tpb = tt // SUBLANES

    def body(dy_ref, u_ref, sre_ref, sim_ref, hre_ref, him_ref, bre_ref, bim_ref, cre_ref, cim_ref, d_ref, tab_ref,
             du_ref, dbre_ref, dbim_ref, dcre_ref, dcim_ref, dare_ref, daim_ref, dd_ref, lr_ref, li_ref, car_ref):
        t = pl.program_id(1)

        @pl.when(t == 0)
        def _():
            car_ref[...] = jnp.zeros_like(car_ref)
            for ref in (dbre_ref, dbim_ref, dcre_ref, dcim_ref, dare_ref, daim_ref, dd_ref):
                ref[...] = jnp.zeros_like(ref)

        dyf = dy_ref[...]
        dyb = dyf.astype(BF16)
        uf = u_ref[...]
        s_re = sre_ref[...]
        s_im = sim_ref[...]
        lr_ref[...] = _dot_nt(dyb, cre_ref[0])
        li_ref[...] = -_dot_nt(dyb, cim_ref[0])
        dcre_ref[0] += _dot_tn(s_re.astype(BF16), dyb)
        dcim_ref[0] -= _dot_tn(s_im.astype(BF16), dyb)
        tab_v = [tab_ref[n] for n in range(8)]
        cr, ci = _scan_tiles(lr_ref, li_ref, tab_v, (car_ref[0:1, :], car_ref[8:9, :]), tpb, True)
        car_ref[0:1, :] = cr
        car_ref[8:9, :] = ci
        lam_r = lr_ref[...]
        lam_i = li_ref[...]
        keep = jnp.where(t == nt - 1, 0.0, 1.0)
        sp_r = _shift_down(s_re, 1, hre_ref[...] * keep)
        sp_i = _shift_down(s_im, 1, him_ref[...] * keep)
        dare_ref[...] += jnp.sum(lam_r * sp_r + lam_i * sp_i, axis=0, keepdims=True)
        daim_ref[...] += jnp.sum(lam_i * sp_r - lam_r * sp_i, axis=0, keepdims=True)
        lrb = lam_r.astype(BF16)
        lib = lam_i.astype(BF16)
        du_ref[...] = _dot_nt(lrb, bre_ref[0]) + _dot_nt(lib, bim_ref[0]) + dyf * d_ref[...]
        ub = uf.astype(BF16)
        dbre_ref[0] += _dot_tn(ub, lrb)
        dbim_ref[0] += _dot_tn(ub, lib)
        dd_ref[...] += jnp.sum(dyf * uf, axis=0, keepdims=True)

    rev = lambda m, t: (nt - 1 - t, m)
    halo = lambda m, t: (jnp.maximum((nt - 1 - t) * tpb - 1, 0), m)
    wb = pl.BlockSpec((1, SSM_UB, SSM_CB), lambda m, t: (m, 0, 0))
    wc = pl.BlockSpec((1, SSM_CB, SSM_UB), lambda m, t: (m, 0, 0))
    vec_c = pl.BlockSpec((1, SSM_CB), lambda m, t: (0, m))
    vec_u = pl.BlockSpec((1, SSM_UB), lambda m, t: (0, m))
    return pl.pallas_call(
        body, name="ssm_bwd", grid=(SSM_NB, nt),
        in_specs=[pl.BlockSpec((tt, SSM_UB), rev), pl.BlockSpec((tt, SSM_UB), rev),
                  pl.BlockSpec((tt, SSM_CB), rev), pl.BlockSpec((tt, SSM_CB), rev),
                  pl.BlockSpec((SUBLANES, SSM_CB), halo), pl.BlockSpec((SUBLANES, SSM_CB), halo),
                  wb, wb, wc, wc, vec_u,
                  pl.BlockSpec((8, SUBLANES, SSM_CB), lambda m, t: (0, 0, m))],
        out_specs=[pl.BlockSpec((tt, SSM_UB), rev), wb, wb, wc, wc, vec_c, vec_c, vec_u],
        out_shape=[jax.ShapeDtypeStruct((l, SSM_W), F32),
                   jax.ShapeDtypeStruct((SSM_NB, SSM_UB, SSM_CB), F32), jax.ShapeDtypeStruct((SSM_NB, SSM_UB, SSM_CB), F32),
                   jax.ShapeDtypeStruct((SSM_NB, SSM_CB, SSM_UB), F32), jax.ShapeDtypeStruct((SSM_NB, SSM_CB, SSM_UB), F32),
                   jax.ShapeDtypeStruct((1, SSM_CH), F32), jax.ShapeDtypeStruct((1, SSM_CH), F32),
                   jax.ShapeDtypeStruct((1, SSM_W), F32)],
        scratch_shapes=[pltpu.VMEM((tt, SSM_CB), F32), pltpu.VMEM((tt, SSM_CB), F32),
                        pltpu.VMEM((2 * SUBLANES, SSM_CB), F32)],
        compiler_params=_params(("parallel", "arbitrary")),
    )(dy, u, sre, sim, sre, sim, bre, bim, cre, cim, dvec, tab)


def _merge_fwd(x, gl, attn, y1, wba, wbs, wglu, bglu, wout, gpost, gpre, tl):
    l = x.shape[0]

    def body(x_ref, gl_ref, at_ref, y1_ref, wba_ref, wbs_ref, wglu_ref, bglu_ref, wout_ref, gpost_ref, gpre_ref,
             a_ref, sm_ref, mg_ref, z_ref, x1_ref, hn2_ref, y3_ref):
        y2 = _gelu(y1_ref[...])
        sg = _sigmoid(_dot(y2.astype(BF16), wglu_ref[...]) + bglu_ref[...])
        y3 = (y2 * sg).astype(BF16)
        y3_ref[...] = y3
        a = _dot(at_ref[...], wba_ref[...])
        sm = _dot(y3, wbs_ref[...])
        a_ref[...] = a.astype(BF16)
        sm_ref[...] = sm.astype(BF16)
        g = _sigmoid(gl_ref[...].astype(F32))
        merged = (g[:, :D_MODEL] * a + g[:, D_MODEL:] * sm).astype(BF16)
        mg_ref[...] = merged
        z = _dot(merged, wout_ref[...])
        z_ref[...] = z
        n, _ = _rms(z, gpost_ref[...])
        x1 = x_ref[...] + n
        x1_ref[...] = x1
        hn2, _ = _rms(x1, gpre_ref[...])
        hn2_ref[...] = hn2.astype(BF16)

    outs = [(D_MODEL, BF16), (D_MODEL, BF16), (D_MODEL, BF16), (D_MODEL, F32), (D_MODEL, F32), (D_MODEL, BF16),
            (SSM_W, BF16)]
    return pl.pallas_call(
        body, name="merge_fwd", grid=(l // tl,),
        in_specs=[_row(tl, D_MODEL), _row(tl, 2 * D_MODEL), _row(tl, HP), _row(tl, SSM_W),
                  _const((HP, D_MODEL)), _const((SSM_W, D_MODEL)), _const((SSM_W, SSM_W)), _const((1, SSM_W)),
                  _const((D_MODEL, D_MODEL)), _const((1, D_MODEL)), _const((1, D_MODEL))],
        out_specs=[_row(tl, n) for n, _ in outs],
        out_shape=[jax.ShapeDtypeStruct((l, n), dt) for n, dt in outs],
        compiler_params=_params(("parallel",)),
    )(x, gl, attn, y1, wba, wbs, wglu, bglu, wout, gpost, gpre)


def _merge_bwd(dhn2, x1, dx2, z, gl, a, sm, y1, wba, wbs, wglu, bglu, wout, gpost, gpre, tl):
    l = x1.shape[0]

    def body(dhn2_ref, x1_ref, dx2_ref, z_ref, gl_ref, a_ref, sm_ref, y1_ref,
             wba_ref, wbs_ref, wglu_ref, bglu_ref, wout_ref, gpost_ref, gpre_ref,
             dx1_ref, dz_ref, dbra_ref, dbrs_ref, dgl_ref, dat_ref, dy1_ref, dt_ref, y2_ref,
             dgpre_ref, dgpost_ref, dbg_ref, dbglu_ref):
        @pl.when(pl.program_id(0) == 0)
        def _():
            for ref in (dgpre_ref, dgpost_ref, dbg_ref, dbglu_ref):
                ref[...] = jnp.zeros_like(ref)

        dx1a, dgpre = _rms_bwd(dhn2_ref[...], x1_ref[...], gpre_ref[...])
        dgpre_ref[...] += dgpre
        dx1 = dx2_ref[...] + dx1a
        dx1_ref[...] = dx1
        dz, dgpost = _rms_bwd(dx1, z_ref[...], gpost_ref[...])
        dgpost_ref[...] += dgpost
        dzb = dz.astype(BF16)
        dz_ref[...] = dzb
        dm = _dot_nt(dzb, wout_ref[...])
        g = _sigmoid(gl_ref[...].astype(F32))
        g0 = g[:, :D_MODEL]
        g1 = g[:, D_MODEL:]
        dbra = (dm * g0).astype(BF16)
        dbrs = (dm * g1).astype(BF16)
        dbra_ref[...] = dbra
        dbrs_ref[...] = dbrs
        dgl0 = dm * a_ref[...].astype(F32) * g0 * (1.0 - g0)
        dgl1 = dm * sm_ref[...].astype(F32) * g1 * (1.0 - g1)
        dgl_ref[:, :D_MODEL] = dgl0.astype(BF16)
        dgl_ref[:, D_MODEL:] = dgl1.astype(BF16)
        dbg_ref[:, :D_MODEL] += jnp.sum(dgl0, axis=0, keepdims=True)
        dbg_ref[:, D_MODEL:] += jnp.sum(dgl1, axis=0, keepdims=True)
        dat_ref[...] = _dot_nt(dbra, wba_ref[...]).astype(BF16)
        dy3 = _dot_nt(dbrs, wbs_ref[...])
        y1v = y1_ref[...]
        y2 = _gelu(y1v)
        y2b = y2.astype(BF16)
        y2_ref[...] = y2b
        sg = _sigmoid(_dot(y2b, wglu_ref[...]) + bglu_ref[...])
        dt = dy3 * y2 * sg * (1.0 - sg)
        dtb = dt.astype(BF16)
        dt_ref[...] = dtb
        dbglu_ref[...] += jnp.sum(dt, axis=0, keepdims=True)
        dy2 = dy3 * sg + _dot_nt(dtb, wglu_ref[...])
        dy1_ref[...] = dy2 * _gelu_grad(y1v)

    outs = [(D_MODEL, F32), (D_MODEL, BF16), (D_MODEL, BF16), (D_MODEL, BF16), (2 * D_MODEL, BF16), (HP, BF16),
            (SSM_W, F32), (SSM_W, BF16), (SSM_W, BF16)]
    accs = [D_MODEL, D_MODEL, 2 * D_MODEL, SSM_W]
    return pl.pallas_call(
        body, name="merge_bwd", grid=(l // tl,),
        in_specs=[_row(tl, D_MODEL), _row(tl, D_MODEL), _row(tl, D_MODEL), _row(tl, D_MODEL),
                  _row(tl, 2 * D_MODEL), _row(tl, D_MODEL), _row(tl, D_MODEL), _row(tl, SSM_W),
                  _const((HP, D_MODEL)), _const((SSM_W, D_MODEL)), _const((SSM_W, SSM_W)), _const((1, SSM_W)),
                  _const((D_MODEL, D_MODEL)), _const((1, D_MODEL)), _const((1, D_MODEL))],
        out_specs=[_row(tl, n) for n, _ in outs] + [_const((1, n)) for n in accs],
        out_shape=[jax.ShapeDtypeStruct((l, n), dt) for n, dt in outs]
        + [jax.ShapeDtypeStruct((1, n), F32) for n in accs],
        compiler_params=_params(("arbitrary",)),
    )(dhn2, x1, dx2, z, gl, a, sm, y1, wba, wbs, wglu, bglu, wout, gpost, gpre)


def _proj_bwd(x, dx1, cq, ckv, dq, dk, dv, du, dgl, g1, win, gq, wuq, gkv, wukv, rc, rs, tl):
    l = x.shape[0]

    def body(x_ref, dx1_ref, cq_ref, ckv_ref, dq_ref, dk_ref, dv_ref, du_ref, dgl_ref,
             g1_ref, win_ref, gq_ref, wuq_ref, gkv_ref, wukv_ref, rc_ref, rs_ref,
             gx_ref, dql_ref, qn_ref, ckvn_ref, dproj_ref, dg1_ref, dgq_ref, dgkv_ref):
        @pl.when(pl.program_id(0) == 0)
        def _():
            for ref in (dg1_ref, dgq_ref, dgkv_ref):
                ref[...] = jnp.zeros_like(ref)

        c1 = rc_ref[...]
        s1 = rs_ref[...]
        dql = _rope_bwd(dq_ref[...], jnp.tile(c1, (1, N_HEADS)), jnp.tile(s1, (1, N_HEADS))).astype(BF16)
        dql_ref[...] = dql
        dqn = _dot_nt(dql, wuq_ref[...])
        cq = cq_ref[...]
        qn, _ = _rms(cq, gq_ref[...])
        qn_ref[...] = qn.astype(BF16)
        dcq, dgq = _rms_bwd(dqn, cq, gq_ref[...])
        dgq_ref[...] += dgq
        dkb = dk_ref[...]
        dvb = dv_ref[...]
        dkf = dkb.astype(F32)
        dkr = dkf[:, 0:HEAD_SLOT]
        for h in range(1, N_HEADS):
            dkr = dkr + dkf[:, h * HEAD_SLOT:(h + 1) * HEAD_SLOT]
        dkr = _rope_bwd(dkr, c1, s1)
        dckvn = _dot_nt(dkb, wukv_ref[:, :HP]) + _dot_nt(dvb, wukv_ref[:, HP:])
        ckv = ckv_ref[...]
        ckvn, _ = _rms(ckv, gkv_ref[...])
        ckvn_ref[...] = ckvn.astype(BF16)
        dckv, dgkv = _rms_bwd(dckvn, ckv, gkv_ref[...])
        dgkv_ref[...] += dgkv
        dproj_ref[:, P_CQ:P_CKV] = dcq.astype(BF16)
        dproj_ref[:, P_CKV:P_KR] = dckv.astype(BF16)
        dproj_ref[:, P_KR:P_U] = dkr.astype(BF16)
        dproj_ref[:, P_U:P_GL] = du_ref[...].astype(BF16)
        dproj_ref[:, P_GL:P_END] = dgl_ref[...]
        dhn = _dot_nt(dproj_ref[...], win_ref[...])
        dxa, dg1 = _rms_bwd(dhn, x_ref[...], g1_ref[...])
        dg1_ref[...] += dg1
        gx_ref[...] = dx1_ref[...] + dxa

    outs = [(D_MODEL, F32), (HP, BF16), (Q_RANK, BF16), (KV_RANK, BF16), (P_END, BF16)]
    accs = [D_MODEL, Q_RANK, KV_RANK]
    return pl.pallas_call(
        body, name="proj_bwd", grid=(l // tl,),
        in_specs=[_row(tl, D_MODEL), _row(tl, D_MODEL), _row(tl, Q_RANK), _row(tl, KV_RANK), _row(tl, HP),
                  _row(tl, HP), _row(tl, HP), _row(tl, SSM_W), _row(tl, 2 * D_MODEL),
                  _const((1, D_MODEL)), _const((D_MODEL, P_END)), _const((1, Q_RANK)), _const((Q_RANK, HP)),
                  _const((1, KV_RANK)), _const((KV_RANK, 2 * HP)), _row(tl, HEAD_SLOT), _row(tl, HEAD_SLOT)],
        out_specs=[_row(tl, n) for n, _ in outs] + [_const((1, n)) for n in accs],
        out_shape=[jax.ShapeDtypeStruct((l, n), dt) for n, dt in outs]
        + [jax.ShapeDtypeStruct((1, n), F32) for n in accs],
        compiler_params=_params(("arbitrary",)),
    )(x, dx1, cq, ckv, dq, dk, dv, du, dgl, g1, win, gq, wuq, gkv, wukv, rc, rs)


CONV_CB = 256
CONV_NB = D_FF // CONV_CB
CONV_ROWS = 16


def _conv3(h, halo, w, b):
    return b + w[0:1, :] * _shift_down(h, 2, halo) + w[1:2, :] * _shift_down(h, 1, halo) + w[2:3, :] * h


def _conv_fwd(h, cw, cb, tl):
    l = h.shape[0]

    def body(hg_ref, hv_ref, wg_ref, wv_ref, bg_ref, bv_ref, act_ref, halo_ref):
        @pl.when(pl.program_id(1) == 0)
        def _():
            halo_ref[...] = jnp.zeros_like(halo_ref)

        hg = hg_ref[...]
        hv = hv_ref[...]
        cg = _conv3(hg, halo_ref[0:SUBLANES, :], wg_ref[...], bg_ref[...])
        cv = _conv3(hv, halo_ref[SUBLANES:, :], wv_ref[...], bv_ref[...])
        act_ref[...] = (_gelu(cg) * cv).astype(BF16)
        halo_ref[0:SUBLANES, :] = hg[tl - SUBLANES:, :]
        halo_ref[SUBLANES:, :] = hv[tl - SUBLANES:, :]

    gmap = lambda c, r: (r, c)
    vmap = lambda c, r: (r, CONV_NB + c)
    return pl.pallas_call(
        body, name="conv_fwd", grid=(CONV_NB, l // tl),
        in_specs=[pl.BlockSpec((tl, CONV_CB), gmap), pl.BlockSpec((tl, CONV_CB), vmap),
                  pl.BlockSpec((3, CONV_CB), lambda c, r: (0, c)), pl.BlockSpec((3, CONV_CB), lambda c, r: (0, CONV_NB + c)),
                  pl.BlockSpec((1, CONV_CB), lambda c, r: (0, c)), pl.BlockSpec((1, CONV_CB), lambda c, r: (0, CONV_NB + c))],
        out_specs=pl.BlockSpec((tl, CONV_CB), gmap),
        out_shape=jax.ShapeDtypeStruct((l, D_FF), BF16),
        scratch_shapes=[pltpu.VMEM((2 * SUBLANES, CONV_CB), F32)],
        compiler_params=_params(("parallel", "arbitrary")),
    )(h, h, cw, cw, cb, cb)


def _conv_bwd(h, dact, cw, cb, tl):
    l = h.shape[0]
    nr = l // tl
    tpb = tl // SUBLANES

    def body(hg_ref, hv_ref, hgh_ref, hvh_ref, da_ref, wg_ref, wv_ref, bg_ref, bv_ref,
             dh_ref, dwg_ref, dwv_ref, dbg_ref, dbv_ref, car_ref):
        r = pl.program_id(1)

        @pl.when(r == 0)
        def _():
            for ref in (car_ref, dwg_ref, dwv_ref, dbg_ref, dbv_ref):
                ref[...] = jnp.zeros_like(ref)

        keep = jnp.where(r == nr - 1, 0.0, 1.0)
        wg, wv, bg, bv = wg_ref[...], wv_ref[...], bg_ref[...], bv_ref[...]
        nch = tl // CONV_ROWS

        def fold(x):
            s = x[0:SUBLANES, :]
            for k in range(1, CONV_ROWS // SUBLANES):
                s = s + x[k * SUBLANES:(k + 1) * SUBLANES, :]
            return s

        def chunk(n, carry):
            ncg, ncv, acc = carry
            idx = nch - 1 - n
            r0 = pl.multiple_of(idx * CONV_ROWS, CONV_ROWS)
            rows = pl.ds(r0, CONV_ROWS)
            before = pl.ds(pl.multiple_of(jnp.maximum(r0 - SUBLANES, 0), SUBLANES), SUBLANES)
            in_tile = idx > 0
            da = da_ref[rows, :].astype(F32)

            def half(h_ref, halo_ref, w, b):
                hh = h_ref[rows, :]
                prev = jnp.where(in_tile, h_ref[before, :], halo_ref[...] * keep)
                h1 = _shift_down(hh, 1, prev)
                h2 = _shift_down(hh, 2, prev)
                return hh, h1, h2, b + w[0:1, :] * h2 + w[1:2, :] * h1 + w[2:3, :] * hh

            hg, hg1, hg2, cg = half(hg_ref, hgh_ref, wg, bg)
            hv, hv1, hv2, cv = half(hv_ref, hvh_ref, wv, bv)
            dcg = da * cv * _gelu_grad(cg)
            dcv = da * _gelu(cg)

            def back(dc, hh, h1, h2, w, nxt, part):
                dh = w[2:3, :] * dc + w[1:2, :] * _shift_up(dc, 1, nxt) + w[0:1, :] * _shift_up(dc, 2, nxt)
                dh_ref[part, rows, :] = dh.astype(BF16)
                return [fold(dc * h2), fold(dc * h1), fold(dc * hh), fold(dc)]

            sums = back(dcg, hg, hg1, hg2, wg, ncg, 0) + back(dcv, hv, hv1, hv2, wv, ncv, 1)
            return dcg[0:SUBLANES, :], dcv[0:SUBLANES, :], [a + s for a, s in zip(acc, sums)]

        zero = jnp.zeros((SUBLANES, CONV_CB), F32)
        ncg, ncv, acc = lax.fori_loop(0, nch, chunk, (car_ref[0:SUBLANES, :], car_ref[SUBLANES:, :], [zero] * 8))
        car_ref[0:SUBLANES, :] = ncg
        car_ref[SUBLANES:, :] = ncv
        for half_acc, dw_ref, db_ref in ((acc[0:4], dwg_ref, dbg_ref), (acc[4:8], dwv_ref, dbv_ref)):
            for k in range(3):
                dw_ref[k:k + 1, :] += jnp.sum(half_acc[k], axis=0, keepdims=True)
            db_ref[...] += jnp.sum(half_acc[3], axis=0, keepdims=True)

    grev = lambda c, r: (nr - 1 - r, c)
    vrev = lambda c, r: (nr - 1 - r, CONV_NB + c)
    ghalo = lambda c, r: (jnp.maximum((nr - 1 - r) * tpb - 1, 0), c)
    vhalo = lambda c, r: (jnp.maximum((nr - 1 - r) * tpb - 1, 0), CONV_NB + c)
    colg = lambda c, r: (0, c)
    colv = lambda c, r: (0, CONV_NB + c)
    return pl.pallas_call(
        body, name="conv_bwd", grid=(CONV_NB, nr),
        in_specs=[pl.BlockSpec((tl, CONV_CB), grev), pl.BlockSpec((tl, CONV_CB), vrev),
                  pl.BlockSpec((SUBLANES, CONV_CB), ghalo), pl.BlockSpec((SUBLANES, CONV_CB), vhalo),
                  pl.BlockSpec((tl, CONV_CB), grev),
                  pl.BlockSpec((3, CONV_CB), colg), pl.BlockSpec((3, CONV_CB), colv),
                  pl.BlockSpec((1, CONV_CB), colg), pl.BlockSpec((1, CONV_CB), colv)],
        out_specs=[pl.BlockSpec((2, tl, CONV_CB), lambda c, r: (0, nr - 1 - r, c)),
                   pl.BlockSpec((3, CONV_CB), colg), pl.BlockSpec((3, CONV_CB), colg),
                   pl.BlockSpec((1, CONV_CB), colg), pl.BlockSpec((1, CONV_CB), colg)],
        out_shape=[jax.ShapeDtypeStruct((2, l, D_FF), BF16),
                   jax.ShapeDtypeStruct((3, D_FF), F32), jax.ShapeDtypeStruct((3, D_FF), F32),
                   jax.ShapeDtypeStruct((1, D_FF), F32), jax.ShapeDtypeStruct((1, D_FF), F32)],
        scratch_shapes=[pltpu.VMEM((2 * SUBLANES, CONV_CB), F32)],
        compiler_params=_params(("parallel", "arbitrary")),
    )(h, h, h, h, dact, cw, cw, cb, cb)


def _loss_head(ff, x1, tgt, g, tl):
    l = ff.shape[0]

    def body(ff_ref, x1_ref, tg_ref, g_ref, loss_ref, dx2_ref, dff_ref, dg_ref):
        @pl.when(pl.program_id(0) == 0)
        def _():
            loss_ref[...] = jnp.zeros_like(loss_ref)
            dg_ref[...] = jnp.zeros_like(dg_ref)

        f = ff_ref[...]
        gv = g_ref[...]
        n, _ = _rms(f, gv)
        e = x1_ref[...] + n - tg_ref[...]
        loss_ref[...] += 0.5 * jnp.sum(jnp.mean(e * e, axis=-1, keepdims=True), axis=0, keepdims=True)
        dx2 = e * (1.0 / D_MODEL)
        dx2_ref[...] = dx2
        dff, dg = _rms_bwd(dx2, f, gv)
        dff_ref[...] = dff.astype(BF16)
        dg_ref[...] += dg

    return pl.pallas_call(
        body, name="loss_head", grid=(l // tl,),
        in_specs=[_row(tl, D_MODEL), _row(tl, D_MODEL), _row(tl, D_MODEL), _const((1, D_MODEL))],
        out_specs=[_const((1, LANES)), _row(tl, D_MODEL), _row(tl, D_MODEL), _const((1, D_MODEL))],
        out_shape=[jax.ShapeDtypeStruct((1, LANES), F32), jax.ShapeDtypeStruct((l, D_MODEL), F32),
                   jax.ShapeDtypeStruct((l, D_MODEL), BF16), jax.ShapeDtypeStruct((1, D_MODEL), F32)],
        compiler_params=_params(("arbitrary",)),
    )(ff, x1, tgt, g)


def _ssm_disc(lam_re, lam_im, log_dt, b_re, b_im):
    dt = jnp.exp(log_dt)[:, None]
    mag = jnp.exp(lam_re * dt)
    ang = lam_im * dt
    a_re, a_im = mag * jnp.cos(ang), mag * jnp.sin(ang)
    den = lam_re * lam_re + lam_im * lam_im
    n_re, n_im = a_re - 1.0, a_im
    z_re = (n_re * lam_re + n_im * lam_im) / den
    z_im = (n_im * lam_re - n_re * lam_im) / den
    bb_re = z_re[..., None] * b_re - z_im[..., None] * b_im
    bb_im = z_re[..., None] * b_im + z_im[..., None] * b_re
    return a_re, a_im, bb_re, bb_im


_GPB = SSM_CB // SSM_P


def _embed_b(bb):
    t = bb.transpose(0, 2, 1).reshape(SSM_NB, _GPB, SSM_H, SSM_P)
    return jnp.einsum('mjhp,jk->mjhkp', t, jnp.eye(_GPB, dtype=bb.dtype)).reshape(SSM_NB, SSM_UB, SSM_CB)


def _extract_b(d):
    t = d.reshape(SSM_NB, _GPB, SSM_H, _GPB, SSM_P)
    t = jnp.einsum('mjhkp,jk->mjhp', t, jnp.eye(_GPB, dtype=d.dtype))
    return t.reshape(SSM_G, SSM_H, SSM_P).transpose(0, 2, 1)


def _embed_c(c):
    t = c.transpose(0, 2, 1).reshape(SSM_NB, _GPB, SSM_P, SSM_H)
    return jnp.einsum('mjph,jk->mjpkh', t, jnp.eye(_GPB, dtype=c.dtype)).reshape(SSM_NB, SSM_CB, SSM_UB)


def _extract_c(d):
    t = d.reshape(SSM_NB, _GPB, SSM_P, _GPB, SSM_H)
    t = jnp.einsum('mjpkh,jk->mjph', t, jnp.eye(_GPB, dtype=d.dtype))
    return t.reshape(SSM_G, SSM_P, SSM_H).transpose(0, 2, 1)


def _scan_tables(a_re, a_im, reverse):
    ar = a_re.reshape(1, SSM_CH)
    ai = (-a_im if reverse else a_im).reshape(1, SSM_CH)
    pr, pi = [ar], [ai]
    for _ in range(SUBLANES - 1):
        pr, pi = pr + [pr[-1] * ar - pi[-1] * ai], pi + [pr[-1] * ai + pi[-1] * ar]
    rows = jnp.arange(SUBLANES)[:, None]
    out = []
    for k in (1, 2, 4):
        valid = (rows + k <= SUBLANES - 1) if reverse else (rows >= k)
        out += [jnp.where(valid, pr[k - 1], 0.0), jnp.where(valid, pi[k - 1], 0.0)]
    order = list(range(SUBLANES - 1, -1, -1)) if reverse else list(range(SUBLANES))
    out += [jnp.concatenate([pr[n] for n in order], axis=0), jnp.concatenate([pi[n] for n in order], axis=0)]
    return jnp.stack(out).astype(F32)


def _pad_heads(w, d):
    lead = w.shape[:-1]
    w = w.reshape(lead + (N_HEADS, d))
    w = jnp.pad(w, [(0, 0)] * len(lead) + [(0, 0), (0, HEAD_SLOT - d)])
    return w.reshape(lead + (HP,))


def _unpad_heads(w, d):
    lead = w.shape[:-1]
    return w.reshape(lead + (N_HEADS, HEAD_SLOT))[..., :d].reshape(lead + (N_HEADS * d,))


def _chip_major(w, axis):
    k, n = w.shape
    if axis == 0:
        return w.reshape(N_CHIPS, k // N_CHIPS, n)
    return w.reshape(k, N_CHIPS, n // N_CHIPS).transpose(1, 0, 2)


def _from_chip_major(w, axis):
    if axis == 0:
        return w.reshape(-1, w.shape[2])
    return w.transpose(1, 0, 2).reshape(w.shape[1], -1)


def _pad_w_in(w):
    z = lambda n: jnp.zeros((w.shape[0], n), w.dtype)
    return jnp.concatenate([w[:, :640], z(KR_LANE), w[:, 640:672], z(HEAD_SLOT - KR_LANE - QK_ROPE), w[:, 672:]], axis=1)


def _unpad_w_in(w):
    return jnp.concatenate([w[:, :640], w[:, P_KR + KR_LANE:P_KR + KR_LANE + QK_ROPE], w[:, P_U:]], axis=1)


def _local_step(x, positions, tgt, wts, sp):
    l = x.shape[0]
    tl = min(512, l)
    tb = min(256, l)
    ta = min(512, l)
    ts = min(2048, l)

    inv_freq = ROPE_THETA ** (-jnp.arange(0, QK_ROPE, 2, dtype=F32) / QK_ROPE)
    ang = positions.astype(F32)[:, None] * inv_freq
    cos, sin = jnp.cos(ang), jnp.sin(ang)
    one = jnp.ones((l, KR_LANE), F32)
    rc = jnp.concatenate([one, cos, cos, jnp.ones((l, HEAD_SLOT - KR_LANE - QK_ROPE), F32)], axis=1)
    rs = jnp.concatenate([0 * one, -sin, sin, jnp.zeros((l, HEAD_SLOT - KR_LANE - QK_ROPE), F32)], axis=1)

    win = _pad_w_in(wts["w_in"])
    wuq = _pad_heads(wts["w_uq"], QK_HEAD)
    wukv = jnp.concatenate([_pad_heads(wts["w_uk"], QK_NOPE), _pad_heads(wts["w_uv"], V_HEAD)], axis=1)

    disc_in = (sp["ssm_lambda_re"], sp["ssm_lambda_im"], sp["ssm_log_dt"], sp["ssm_b_re"], sp["ssm_b_im"])
    (a_re, a_im, bb_re, bb_im), disc_vjp = jax.vjp(_ssm_disc, *disc_in)
    bre, bim = _embed_b(bb_re).astype(BF16), _embed_b(bb_im).astype(BF16)
    cre, cim = _embed_c(sp["ssm_c_re"]).astype(BF16), _embed_c(sp["ssm_c_im"]).astype(BF16)
    dvec = sp["ssm_d"].reshape(1, SSM_W)
    tab_f = _scan_tables(a_re, a_im, False)
    tab_r = _scan_tables(a_re, a_im, True)

    g1, gq, gkv = sp["mix_norm_pre"], sp["q_norm"], sp["kv_norm"]
    gpost, gpre, gfin = sp["mix_norm_post"], sp["ffn_norm_pre"], sp["ffn_norm_post"]
    bgate, bglu, convb = sp["b_gate"], sp["b_glu"], sp["conv_b"]

    hn, cq, ckv, q, k, v, u, gl = _proj_fwd(x, g1, win, gq, wuq, gkv, wukv, rc, rs, bgate, tl)
    attn, lse = _attn_fwd(q, k, v, ta, FWD_HEADS)
    y1, sre, sim = _ssm_fwd(u, bre, bim, cre, cim, dvec, tab_f, ts)
    wba = jnp.pad(wts["w_branch_attn"].reshape(N_HEADS, V_HEAD, D_MODEL),
                  ((0, 0), (0, HEAD_SLOT - V_HEAD), (0, 0))).reshape(HP, D_MODEL)
    wbs, wglu, wout = wts["w_branch_ssm"], wts["w_glu"], wts["w_out"]
    a, sm, merged, z, x1, hn2, y3 = _merge_fwd(x, gl, attn, y1, wba, wbs, wglu, bglu, wout, gpost, gpre, tl)
    late = wts["late"](x1)
    wup, wdown, convw = late["w_up"], late["w_down"], late["conv_w"]
    h = _mm(hn2, wup, "ffn_up", tm_cap=2048)
    act = _conv_fwd(h, convw, convb, l)
    ff = _mm(act, wdown, "ffn_down")
    loss, dx2, dff, dgfin = _loss_head(ff, x1, tgt, gfin, ta)

    dact = _mm(dff, wdown, "ffn_down_dx", out_dtype=BF16, bt=True, tm_cap=2048)
    d_wdown = _mm_tn(act, dff, "ffn_down_dw", tk_cap=D_FF // 2)
    dh, dwg, dwv, dbg, dbv = _conv_bwd(h, dact, convw, convb, l)
    d_convw = jnp.concatenate([dwg, dwv], axis=1)
    d_convb = jnp.concatenate([dbg, dbv], axis=1)
    dhn2 = _mm(dh, wup, "ffn_up_dx", bt=True)
    d_wup = _mm_tn(hn2, dh, "ffn_up_dw", chips=True)
    behind = wts["send_grads"]("ffn", {"w_up": d_wup, "w_down": _chip_major(d_wdown, 0)})
    (dx1, dz, dbra, dbrs, dgl, dattn, dy1, dt, y2, dgpre, dgpost, dbgate, dbglu) = _merge_bwd(
        dhn2, x1, dx2, z, gl, a, sm, y1, wba, wbs, wglu, bglu, wout, gpost, gpre + behind, tb)
    d_wout = _mm_tn(merged, dz, "w_out_dw")
    d_wba = _mm_tn(attn, dbra, "w_branch_attn_dw", chips=True)
    d_wbs = _mm_tn(y3, dbrs, "w_branch_ssm_dw", chips=True)
    d_wglu = _mm_tn(y2, dt, "w_glu_dw")
    ncol = D_MODEL // N_CHIPS
    behind = wts["send_grads"]("mix", {
        "w_glu": _chip_major(d_wglu, 0),
        "w_branch_attn": d_wba.reshape(N_CHIPS, N_HEADS, HEAD_SLOT, ncol)[:, :, :V_HEAD].reshape(
            N_CHIPS, N_HEADS * V_HEAD, ncol),
        "w_branch_ssm": d_wbs,
        "w_out": _chip_major(d_wout, 0)}, after=d_wglu)
    dq, dk, dv = _attn_bwd(q, k, v, dattn, lse + behind, _attn_delta(attn, dattn, min(2048, l), BWD_HEADS), ta,
                            BWD_HEADS)
    du, dbre, dbim, dcre, dcim, dare, daim, dd = _ssm_bwd(dy1, u, sre, sim, bre, bim, cre, cim, dvec, tab_r, ts)
    behind = wts["send_grads"]("none", {}, after=du)
    gx, dql, qn, ckvn, dproj, dg1, dgq, dgkv = _proj_bwd(
        x, dx1, cq, ckv, dq, dk, dv, du, dgl, g1 + behind, win, gq, wuq, gkv, wukv, rc, rs, tl)
    d_win = _mm_tn(hn, dproj, "w_in_dw")
    d_wuq = _mm_tn(qn, dql, "w_uq_dw")
    d_wuk = _mm_tn(ckvn, dk, "w_uk_dw")
    d_wuv = _mm_tn(ckvn, dv, "w_uv_dw")

    d_lre, d_lim, d_ldt, d_bre, d_bim = disc_vjp((dare.reshape(SSM_G, SSM_P), daim.reshape(SSM_G, SSM_P),
                                                  _extract_b(dbre), _extract_b(dbim)))
    big = {
        "w_in": _chip_major(_unpad_w_in(d_win), 1),
        "w_uq": _chip_major(_unpad_heads(d_wuq, QK_HEAD), 1),
        "w_uk": _chip_major(_unpad_heads(d_wuk, QK_NOPE), 1),
        "w_uv": _chip_major(_unpad_heads(d_wuv, V_HEAD), 1),
    }
    small = {
        "conv_w": d_convw,
        "mix_norm_pre": dg1, "q_norm": dgq, "kv_norm": dgkv,
        "ssm_lambda_re": d_lre, "ssm_lambda_im": d_lim, "ssm_log_dt": d_ldt,
        "ssm_b_re": d_bre, "ssm_b_im": d_bim,
        "ssm_c_re": _extract_c(dcre), "ssm_c_im": _extract_c(dcim),
        "ssm_d": dd.reshape(SSM_G, SSM_H), "b_glu": dbglu, "b_gate": dbgate,
        "mix_norm_post": dgpost, "ffn_norm_pre": dgpre, "conv_b": d_convb, "ffn_norm_post": dgfin,
    }
    return loss[0, 0], gx, big, small


_ANY = pl.BlockSpec(memory_space=pl.ANY)


ROW_TILE = 16


def _place():
    x, y, c = lax.axis_index("x"), lax.axis_index("y"), lax.axis_index("c")
    return x, y, c, 2 * x + y, [(1 - x, y), (x, 1 - y), (1 - x, 1 - y)]


def _half(rows, which):
    hr = rows // 2
    return pl.ds(pl.multiple_of(which * hr, ROW_TILE), hr)


def _remote(src, dst, send_sems, recv_sems, n, dev):
    return pltpu.make_async_remote_copy(src_ref=src, dst_ref=dst, send_sem=send_sems.at[n], recv_sem=recv_sems.at[n],
                                        device_id=dev, device_id_type=MESH)


def _gather_big(shards):
    nw = len(shards)
    rows = [s.shape[0] for s in shards]

    def body(*refs):
        ins, outs = refs[:nw], refs[nw:2 * nw]
        ici_send, ici_recv, d2d_send, d2d_recv = refs[2 * nw:]
        x, y, c, me, peers = _place()
        sent = []
        for i in range(nw):
            for p, (px, py) in enumerate(peers):
                cp = _remote(ins[i].at[_half(rows[i], c)], outs[i].at[me, _half(rows[i], c)], ici_send, ici_recv,
                             3 * i + p, (px, py, c))
                cp.start()
                sent.append(cp)
        for i in range(nw):
            for p, (px, py) in enumerate(peers):
                blk = outs[i].at[2 * px + py, _half(rows[i], c)]
                _remote(blk, blk, ici_send, ici_recv, 3 * i + p, (px, py, c)).wait_recv()
                cp = _remote(blk, blk, d2d_send, d2d_recv, 3 * i + p, (x, y, 1 - c))
                cp.start()
                sent.append(cp)
        for p, (px, py) in enumerate(peers):
            for i in range(nw):
                blk = outs[i].at[2 * px + py, _half(rows[i], 1 - c)]
                _remote(blk, blk, d2d_send, d2d_recv, 3 * i + p, (x, y, 1 - c)).wait_recv()
        for cp in sent:
            cp.wait_send()

    dma = pltpu.SemaphoreType.DMA
    return pl.pallas_call(
        body, name="gather_weights", in_specs=[_ANY] * nw, out_specs=[_ANY] * nw,
        out_shape=[jax.ShapeDtypeStruct((N_CHIPS,) + s.shape, s.dtype) for s in shards],
        scratch_shapes=[dma((3 * nw,)), dma((3 * nw,)), dma((3 * nw,)), dma((3 * nw,))],
    )(*shards)


_HBM = pl.BlockSpec(memory_space=pltpu.HBM)
_SEM = pl.BlockSpec(memory_space=pltpu.SEMAPHORE)
_DATAFLOW = pltpu.SideEffectType.DATAFLOW_SIDE_EFFECTING


def _exchange_start(shards, name, scatter):
    nw = len(shards)
    lands = [lax.empty(s.shape if scatter else (N_CHIPS,) + s.shape, s.dtype) for s in shards]

    def body(*refs):
        ins, zones = refs[:nw], refs[nw:2 * nw]
        send_sems, recv_sems, token = refs[2 * nw], refs[2 * nw + 1], refs[-1]
        x, y, c, me, peers = _place()
        for i in range(nw):
            for p, (px, py) in enumerate(peers):
                src = ins[i].at[2 * px + py] if scatter else ins[i]
                _remote(src, zones[i].at[me], send_sems, recv_sems, 3 * i + p, (px, py, c)).start()
        token[...] = jnp.zeros_like(token)

    thru = [pltpu.HBM(a.shape, a.dtype) for a in list(shards) + lands]
    dma = pltpu.SemaphoreType.DMA
    outs = pl.pallas_call(
        body, name=name,
        out_shape=(dma((3 * nw,)), dma((3 * nw,)), *thru, jax.ShapeDtypeStruct((SUBLANES, LANES), F32)),
        in_specs=[_HBM] * (2 * nw),
        out_specs=(_SEM, _SEM, *([_HBM] * (2 * nw)), pl.BlockSpec(memory_space=pltpu.VMEM)),
        input_output_aliases={i: 2 + i for i in range(2 * nw)},
        compiler_params=pltpu.CompilerParams(has_side_effects=_DATAFLOW),
    )(*[pltpu.with_memory_space_constraint(a, pltpu.HBM) for a in list(shards) + lands])
    return outs[0], outs[1], list(outs[2:2 + nw]), list(outs[2 + nw:2 + 2 * nw]), outs[-1]


def _exchange_wait(send_sems, recv_sems, shards, lands, after, name, scatter):
    nw = len(shards)

    def body(*refs):
        ins, zones = refs[:nw], refs[nw:2 * nw]
        send_sems, recv_sems = refs[2 * nw], refs[2 * nw + 1]
        x, y, c, me, peers = _place()
        for i in range(nw):
            for p, (px, py) in enumerate(peers):
                src = ins[i].at[2 * px + py] if scatter else ins[i]
                cp = _remote(src, zones[i].at[2 * px + py], send_sems, recv_sems, 3 * i + p, (px, py, c))
                cp.wait_send()
                cp.wait_recv()

    both = list(shards) + list(lands)
    outs = pl.pallas_call(
        body, name=name,
        out_shape=tuple(pltpu.HBM(a.shape, a.dtype) for a in both),
        in_specs=(*([_HBM] * (2 * nw)), _SEM, _SEM, _ANY), out_specs=[_HBM] * (2 * nw),
        input_output_aliases={i: i for i in range(2 * nw)},
        compiler_params=pltpu.CompilerParams(has_side_effects=_DATAFLOW),
    )(*both, send_sems, recv_sems, after)
    return list(outs[:nw]), list(outs[nw:])


def _sibling_start(grads, name):
    nw = len(grads)
    lands = [lax.empty((N_CHIPS, g.shape[1] // 2, g.shape[2]), g.dtype) for g in grads]

    def body(*refs):
        ins, zones = refs[:nw], refs[nw:2 * nw]
        send_sems, recv_sems, token = refs[2 * nw], refs[2 * nw + 1], refs[-1]
        x, y, c, _, _ = _place()
        for i in range(nw):
            _remote(ins[i].at[pl.ds(0, N_CHIPS), _half(grads[i].shape[1], 1 - c)], zones[i], send_sems, recv_sems,
                    i, (x, y, 1 - c)).start()
        token[...] = jnp.zeros_like(token)

    thru = [pltpu.HBM(a.shape, a.dtype) for a in list(grads) + lands]
    dma = pltpu.SemaphoreType.DMA
    outs = pl.pallas_call(
        body, name=name,
        out_shape=(dma((nw,)), dma((nw,)), *thru, jax.ShapeDtypeStruct((SUBLANES, LANES), F32)),
        in_specs=[_HBM] * (2 * nw),
        out_specs=(_SEM, _SEM, *([_HBM] * (2 * nw)), pl.BlockSpec(memory_space=pltpu.VMEM)),
        input_output_aliases={i: 2 + i for i in range(2 * nw)},
        compiler_params=pltpu.CompilerParams(has_side_effects=_DATAFLOW),
    )(*[pltpu.with_memory_space_constraint(a, pltpu.HBM) for a in list(grads) + lands])
    return outs[0], outs[1], list(outs[2:2 + nw]), list(outs[2 + nw:2 + 2 * nw]), outs[-1]


def _sibling_wait(send_sems, recv_sems, grads, lands, after, name):
    nw = len(grads)

    def body(*refs):
        ins, zones = refs[:nw], refs[nw:2 * nw]
        send_sems, recv_sems = refs[2 * nw], refs[2 * nw + 1]
        x, y, c, _, _ = _place()
        for i in range(nw):
            cp = _remote(ins[i].at[pl.ds(0, N_CHIPS), _half(grads[i].shape[1], 1 - c)], zones[i], send_sems, recv_sems,
                         i, (x, y, 1 - c))
            cp.wait_send()
            cp.wait_recv()

    both = list(grads) + list(lands)
    outs = pl.pallas_call(
        body, name=name,
        out_shape=tuple(pltpu.HBM(a.shape, a.dtype) for a in both),
        in_specs=(*([_HBM] * (2 * nw)), _SEM, _SEM, _ANY), out_specs=[_HBM] * (2 * nw),
        input_output_aliases={i: i for i in range(2 * nw)},
        compiler_params=pltpu.CompilerParams(has_side_effects=_DATAFLOW),
    )(*both, send_sems, recv_sems, after)
    return list(outs[:nw]), list(outs[nw:])


def _reduce_to_sibling(grads, name):
    nw = len(grads)

    def body(*refs):
        ins, outs = refs[:nw], refs[nw:2 * nw]
        send_sems, recv_sems = refs[2 * nw:]
        x, y, c, _, _ = _place()
        sent = []
        for i in range(nw):
            cp = _remote(ins[i].at[pl.ds(0, N_CHIPS), _half(grads[i].shape[1], 1 - c)], outs[i], send_sems, recv_sems,
                         i, (x, y, 1 - c))
            cp.start()
            sent.append(cp)
        for cp in sent:
            cp.wait()

    dma = pltpu.SemaphoreType.DMA
    return pl.pallas_call(
        body, name=name, in_specs=[_ANY] * nw, out_specs=[_ANY] * nw,
        out_shape=[jax.ShapeDtypeStruct((N_CHIPS, g.shape[1] // 2, g.shape[2]), g.dtype) for g in grads],
        scratch_shapes=[dma((nw,)), dma((nw,))],
    )(*grads)


def _reduce_back(totals, name):
    nw = len(totals)

    def body(*refs):
        outs = refs[nw:2 * nw]
        send_sems, recv_sems = refs[2 * nw:]
        x, y, c, _, _ = _place()
        sent = []
        for i in range(nw):
            blk = outs[i].at[_half(totals[i].shape[0], c)]
            cp = _remote(blk, blk, send_sems, recv_sems, i, (x, y, 1 - c))
            cp.start()
            sent.append(cp)
        for i in range(nw):
            blk = outs[i].at[_half(totals[i].shape[0], 1 - c)]
            _remote(blk, blk, send_sems, recv_sems, i, (x, y, 1 - c)).wait_recv()
        for cp in sent:
            cp.wait_send()

    dma = pltpu.SemaphoreType.DMA
    return pl.pallas_call(
        body, name=name, in_specs=[_ANY] * nw, out_specs=[_ANY] * nw,
        out_shape=[jax.ShapeDtypeStruct(t.shape, t.dtype) for t in totals],
        input_output_aliases={i: i for i in range(nw)},
        scratch_shapes=[dma((nw,)), dma((nw,))],
    )(*totals)


def _all_reduce_small(v, name):
    rows, w = v.shape
    hr = rows // 2
    assert hr % SUBLANES == 0

    def body(v_ref, out_ref, sib_ref, half_ref, chips_ref, send_sems, recv_sems):
        x, y, c, me, peers = _place()
        sibling = (x, y, 1 - c)
        mine = pl.ds(pl.multiple_of(c * hr, SUBLANES), hr)
        other = pl.ds(pl.multiple_of((1 - c) * hr, SUBLANES), hr)
        cp = _remote(v_ref, sib_ref, send_sems, recv_sems, 0, sibling)
        cp.start()
        cp.wait()
        half_ref[...] = v_ref[mine, :] + sib_ref[mine, :]
        sent = []
        for p, (px, py) in enumerate(peers):
            cp = _remote(half_ref, chips_ref.at[me], send_sems, recv_sems, 1 + p, (px, py, c))
            cp.start()
            sent.append(cp)
        chips_ref[me] = half_ref[...]
        for p, (px, py) in enumerate(peers):
            _remote(half_ref, chips_ref.at[2 * px + py], send_sems, recv_sems, 1 + p, (px, py, c)).wait_recv()
        for cp in sent:
            cp.wait_send()
        out_ref[mine, :] = ((chips_ref[0] + chips_ref[1]) + chips_ref[2]) + chips_ref[3]
        cp = _remote(out_ref.at[mine], out_ref.at[mine], send_sems, recv_sems, 4, sibling)
        cp.start()
        _remote(out_ref.at[other], out_ref.at[other], send_sems, recv_sems, 4, sibling).wait_recv()
        cp.wait_send()

    vm = pl.BlockSpec(memory_space=pltpu.VMEM)
    return pl.pallas_call(
        body, name=name, in_specs=[vm], out_specs=vm,
        out_shape=jax.ShapeDtypeStruct((rows, w), F32),
        scratch_shapes=[pltpu.VMEM((rows, w), F32), pltpu.VMEM((hr, w), F32), pltpu.VMEM((N_CHIPS, hr, w), F32),
                        pltpu.SemaphoreType.DMA((5,)), pltpu.SemaphoreType.DMA((5,))],
        compiler_params=pltpu.CompilerParams(vmem_limit_bytes=VMEM_LIMIT),
    )(v)


ELEMENTWISE_BLOCK = 512 * 1024


def _rows_tile(rows, cols, cap=ELEMENTWISE_BLOCK):
    best = None
    for t in range(SUBLANES, rows + 1, SUBLANES):
        if rows % t == 0 and t * cols <= cap:
            best = t
    return rows if best is None else best


def _add_pair(g, t, core, name):
    nb, n, w = t.shape
    tr = _rows_tile(n, w, 2 * ELEMENTWISE_BLOCK)
    steps = n // tr

    def body(core_ref, g_ref, t_ref, o_ref):
        o_ref[...] = (g_ref[...] + t_ref[...]).astype(BF16)

    spec = pl.BlockSpec((1, tr, w), lambda j, i, core_ref: (j, i, 0))
    return pl.pallas_call(
        body, name=name,
        grid_spec=pltpu.PrefetchScalarGridSpec(
            num_scalar_prefetch=1, grid=(nb, steps),
            in_specs=[pl.BlockSpec((1, tr, w), lambda j, i, core_ref: (j, core_ref[0] * steps + i, 0)), spec],
            out_specs=spec),
        out_shape=jax.ShapeDtypeStruct(t.shape, BF16),
        compiler_params=_params(("parallel", "parallel")))(core, g, t)


def _add_chips(landed, pairs, place, name):
    nb, n, w = landed.shape
    tr = _rows_tile(n, w, 2 * ELEMENTWISE_BLOCK)
    steps = n // tr

    def body(place_ref, r_ref, own_ref, o_ref):
        me = place_ref[0]
        acc = None
        for k in range(nb):
            blk = jnp.where(me == k, own_ref[0], r_ref[k]).astype(F32)
            acc = blk if acc is None else acc + blk
        o_ref[...] = acc

    return pl.pallas_call(
        body, name=name,
        grid_spec=pltpu.PrefetchScalarGridSpec(
            num_scalar_prefetch=1, grid=(steps,),
            in_specs=[pl.BlockSpec((nb, tr, w), lambda i, place_ref: (0, i, 0)),
                      pl.BlockSpec((1, tr, w), lambda i, place_ref: (place_ref[0], i, 0))],
            out_specs=pl.BlockSpec((tr, w), lambda i, place_ref: (place_ref[1] * steps + i, 0))),
        out_shape=jax.ShapeDtypeStruct((2 * n, w), F32),
        compiler_params=_params(("parallel",)))(place, landed, pairs)


def _adamw(w, g, m, v, name):
    rows, wd = w.shape
    tr = _rows_tile(rows, wd)
    c1 = 1.0 - ADAM_B1 ** ADAM_STEP
    c2 = 1.0 - ADAM_B2 ** ADAM_STEP

    def body(w_ref, g_ref, m_ref, v_ref, go_ref, d_ref, mo_ref, vo_ref):
        gv = g_ref[...]
        go_ref[...] = gv
        m2 = ADAM_B1 * m_ref[...] + (1.0 - ADAM_B1) * gv
        v2 = ADAM_B2 * v_ref[...] + (1.0 - ADAM_B2) * (gv * gv)
        mo_ref[...] = m2
        vo_ref[...] = v2
        d_ref[...] = -ADAM_LR * ((m2 / c1) / (jnp.sqrt(v2 / c2) + ADAM_EPS) + ADAM_WD * w_ref[...])

    spec = pl.BlockSpec((tr, wd), lambda i: (i, 0))
    shp = jax.ShapeDtypeStruct((rows, wd), F32)
    return pl.pallas_call(body, name=name, grid=(rows // tr,), in_specs=[spec] * 4, out_specs=[spec] * 4,
                          out_shape=[shp] * 4, compiler_params=_params(("parallel",)))(w, g, m, v)


BIG = [("w_in", (1024, 3232), 1), ("w_uq", (384, 768), 1), ("w_uk", (256, 512), 1), ("w_uv", (256, 512), 1),
       ("w_glu", (512, 512), 0), ("w_branch_attn", (512, 1024), 1), ("w_branch_ssm", (512, 1024), 1),
       ("w_out", (1024, 1024), 0), ("w_up", (1024, 5632), 1), ("conv_w", (3, 5632), 1), ("w_down", (2816, 1024), 0)]
SMALL = [("mix_norm_pre", (1024,)), ("q_norm", (384,)), ("kv_norm", (256,)), ("ssm_lambda_re", (32, 64)),
         ("ssm_lambda_im", (32, 64)), ("ssm_log_dt", (32,)), ("ssm_b_re", (32, 64, 16)), ("ssm_b_im", (32, 64, 16)),
         ("ssm_c_re", (32, 16, 64)), ("ssm_c_im", (32, 16, 64)), ("ssm_d", (32, 16)), ("b_glu", (512,)),
         ("b_gate", (2048,)), ("mix_norm_post", (1024,)), ("ffn_norm_pre", (1024,)), ("conv_b", (5632,)),
         ("ffn_norm_post", (1024,))]
MATMUL_W = [b for b in BIG if b[0] != "conv_w"]
LATE_W = ("w_up", "w_down", "conv_w")
CONV_W_SHAPE = (3, 2 * D_FF)
CONV_W_SHARD = (3, 2 * D_FF // N_CHIPS)
SMALL_SUM = [("loss", (1,))] + SMALL + [("conv_w", CONV_W_SHAPE)]
SMALL_ADAM = SMALL + [("conv_w", CONV_W_SHARD)]


def _pack_flat(layout, vals):
    flat = jnp.concatenate([vals[n].astype(F32).reshape(-1) for n, _ in layout])
    rows = -(-(-(-flat.shape[0] // FLAT_W)) // (2 * SUBLANES)) * 2 * SUBLANES
    return jnp.pad(flat, (0, rows * FLAT_W - flat.shape[0])).reshape(rows, FLAT_W)


def _unpack_flat(layout, flat):
    flat = flat.reshape(-1)
    out = {}
    o = 0
    for name, shape in layout:
        n = math.prod(shape)
        out[name] = flat[o:o + n].reshape(shape)
        o += n
    return out


_ARG_NAMES = ["x", "positions"] + [n for n in (
    "mix_norm_pre", "w_in", "q_norm", "w_uq", "kv_norm", "w_uk", "w_uv", "ssm_lambda_re", "ssm_lambda_im", "ssm_log_dt",
    "ssm_b_re", "ssm_b_im", "ssm_c_re", "ssm_c_im", "ssm_d", "w_glu", "b_glu", "w_branch_attn", "w_branch_ssm",
    "b_gate", "w_out", "mix_norm_post", "ffn_norm_pre", "w_up", "conv_w", "conv_b", "w_down", "ffn_norm_post")]
_WEIGHTS = _ARG_NAMES[2:]


def _gather_weights(w):
    early = [b for b in MATMUL_W if b[0] not in LATE_W]
    late = [b for b in BIG if b[0] in LATE_W]
    own = (jnp.arange(N_CHIPS) == 2 * lax.axis_index("x") + lax.axis_index("y"))[:, None, None]

    def whole(layout, mine, gathered):
        return {name: _from_chip_major(jnp.where(own, s[None], g), axis)
                for (name, _, axis), s, g in zip(layout, mine, gathered)}

    mine = [w[name].astype(BF16) for name, _, _ in early]
    gathered = _gather_big(mine)
    full = whole(early, mine, gathered)
    mine_late = [w[name].astype(F32 if name == "conv_w" else BF16) for name, _, _ in late]
    _, mine_late = lax.optimization_barrier((gathered[0], mine_late))
    send_sems, recv_sems, shards_thru, lands_thru, token = _exchange_start(mine_late, "gather_late_start", scatter=False)

    def late_weights(after):
        shards, lands = _exchange_wait(send_sems, recv_sems, shards_thru, lands_thru, after, "gather_late_wait",
                                       scatter=False)
        return whole(late, shards, lands)

    full["late"] = late_weights
    full["token"] = token[0, 0]
    return full


def _pair_sums(names, grads, tag):
    core = lax.axis_index("c").astype(jnp.int32).reshape(1)
    theirs = _reduce_to_sibling(grads, "reduce_grads_d2d" + tag)
    return [_add_pair(g, t, core, "reduce_pair_" + n) for n, g, t in zip(names, grads, theirs)]


def _send_grads(tag, grads, after, flying, pending):
    token = jnp.zeros((), F32)
    if flying:
        tag0, names0, state0 = flying.pop()
        core = lax.axis_index("c").astype(jnp.int32).reshape(1)
        mine, theirs = _sibling_wait(*state0, after, "reduce_" + tag0 + "_d2d_wait")
        pairs = [_add_pair(g, t, core, "reduce_pair_" + n) for n, g, t in zip(names0, mine, theirs)]
        send_sems, recv_sems, pairs_thru, lands_thru, tok = _exchange_start(pairs, "reduce_" + tag0 + "_start",
                                                                          scatter=True)
        pending.append((tag0, names0, send_sems, recv_sems, pairs_thru, lands_thru))
        token = token + tok[0, 0]
    if grads:
        names = list(grads)
        send_sems, recv_sems, grads_thru, lands_thru, tok = _sibling_start([grads[n] for n in names],
                                                                          "reduce_" + tag + "_d2d_start")
        flying.append((tag, names, (send_sems, recv_sems, grads_thru, lands_thru)))
        token = token + tok[0, 0]
    return token


def _reduce_grads(gbig, pending, loss, gsmall, use_sent):
    core = lax.axis_index("c").astype(jnp.int32).reshape(1)
    chip = (2 * lax.axis_index("x") + lax.axis_index("y")).astype(jnp.int32).reshape(1)
    place = jnp.concatenate([chip, core])

    def finish(names, pairs, landed, name):
        totals = [_add_chips(r, p, place, "reduce_chips_" + n) for n, r, p in zip(names, landed, pairs)]
        return dict(zip(names, _reduce_back(totals, name)))

    names = list(gbig)
    pairs = _pair_sums(names, [gbig[n] for n in names], "")
    send_sems, recv_sems, pairs_thru, lands_thru, token = _exchange_start(pairs, "reduce_last_start", scatter=True)
    sent_names, sent_pairs, sent_landed = [], [], []
    for tag, group, g_send, g_recv, g_pairs, g_lands in pending:
        got_pairs, got_landed = _exchange_wait(g_send, g_recv, g_pairs, g_lands, token, "reduce_" + tag + "_wait",
                                               scatter=True)
        sent_names, sent_pairs, sent_landed = sent_names + group, sent_pairs + got_pairs, sent_landed + got_landed
    g_sent = finish(sent_names, sent_pairs, sent_landed, "reduce_back_sent")
    vals = dict(gsmall)
    vals["loss"] = loss
    small_red = _unpack_flat(SMALL_SUM, _all_reduce_small(_pack_flat(SMALL_SUM, vals), "reduce_small"))
    after = use_sent(g_sent, small_red)
    pairs, landed = _exchange_wait(send_sems, recv_sems, pairs_thru, lands_thru, after, "reduce_last_wait", scatter=True)
    return finish(names, pairs, landed, "reduce_back_last"), small_red


def _step(args):
    x = args["x"][0]
    positions = args["positions"][0]
    tgt = args["loss_target"][0]
    w = {n: args[n][0] for n in _WEIGHTS}
    m = {n: args["m_" + n][0] for n in _WEIGHTS}
    v = {n: args["v_" + n][0] for n in _WEIGHTS}

    full = _gather_weights(w)
    sp = {n: w[n].reshape(s) for n, s in SMALL}
    for n in ("mix_norm_pre", "q_norm", "kv_norm", "b_glu", "b_gate", "mix_norm_post", "ffn_norm_pre", "conv_b",
              "ffn_norm_post"):
        sp[n] = sp[n].reshape(1, -1)
    sp["mix_norm_pre"] = sp["mix_norm_pre"] + full.pop("token")
    pending, flying = [], []
    full["send_grads"] = lambda tag, grads, after=None: _send_grads(tag, grads, after, flying, pending)
    loss, gx, gbig, gsmall = _local_step(x, positions, tgt, full, sp)
    outs = {}

    def adam_big(g_red):
        for name in g_red:
            g2, d, m2, v2 = _adamw(w[name], g_red[name], m[name], v[name], "adamw_" + name)
            outs["grad_" + name], outs["delta_" + name], outs["new_m_" + name], outs["new_v_" + name] = g2, d, m2, v2
        return v2

    def use_sent(g_sent, small_red):
        chip = 2 * lax.axis_index("x") + lax.axis_index("y")
        grads = dict(small_red)
        grads["conv_w"] = lax.dynamic_slice_in_dim(small_red["conv_w"], chip * CONV_W_SHARD[1], CONV_W_SHARD[1], axis=1)
        outs.update({"grad_" + n: grads[n] for n, _ in SMALL_ADAM})
        _, d_sm, m_sm, v_sm = _adamw(_pack_flat(SMALL_ADAM, w), _pack_flat(SMALL_ADAM, grads),
                                     _pack_flat(SMALL_ADAM, m), _pack_flat(SMALL_ADAM, v), "adamw_small")
        for prefix, flat in (("delta_", d_sm), ("new_m_", m_sm), ("new_v_", v_sm)):
            for n, val in _unpack_flat(SMALL_ADAM, flat).items():
                outs[prefix + n] = val
        return adam_big(g_sent)

    g_last, small_red = _reduce_grads(gbig, pending, loss, gsmall, use_sent)
    adam_big(g_last)
    outs = {n: val.reshape(args[n.split("_", 1)[1] if not n.startswith("new_") else n[6:]].shape)
            for n, val in outs.items()}
    res = [small_red["loss"][0], gx[None]]
    for prefix in ("grad_", "delta_", "new_m_", "new_v_"):
        res += [outs[prefix + n] for n in _WEIGHTS]
    return tuple(res)


def kernel(x, positions, mix_norm_pre, w_in, q_norm, w_uq, kv_norm, w_uk, w_uv, ssm_lambda_re, ssm_lambda_im, ssm_log_dt, ssm_b_re, ssm_b_im, ssm_c_re, ssm_c_im, ssm_d, w_glu, b_glu, w_branch_attn, w_branch_ssm, b_gate, w_out, mix_norm_post, ffn_norm_pre, w_up, conv_w, conv_b, w_down, ffn_norm_post, loss_target, m_mix_norm_pre, m_w_in, m_q_norm, m_w_uq, m_kv_norm, m_w_uk, m_w_uv, m_ssm_lambda_re, m_ssm_lambda_im, m_ssm_log_dt, m_ssm_b_re, m_ssm_b_im, m_ssm_c_re, m_ssm_c_im, m_ssm_d, m_w_glu, m_b_glu, m_w_branch_attn, m_w_branch_ssm, m_b_gate, m_w_out, m_mix_norm_post, m_ffn_norm_pre, m_w_up, m_conv_w, m_conv_b, m_w_down, m_ffn_norm_post, v_mix_norm_pre, v_w_in, v_q_norm, v_w_uq, v_kv_norm, v_w_uk, v_w_uv, v_ssm_lambda_re, v_ssm_lambda_im, v_ssm_log_dt, v_ssm_b_re, v_ssm_b_im, v_ssm_c_re, v_ssm_c_im, v_ssm_d, v_w_glu, v_b_glu, v_w_branch_attn, v_w_branch_ssm, v_b_gate, v_w_out, v_mix_norm_post, v_ffn_norm_pre, v_w_up, v_conv_w, v_conv_b, v_w_down, v_ffn_norm_post):
    given = dict(locals())
    return _step(given)
```

```python
import math

import jax
import jax.numpy as jnp
from jax import lax
from jax.experimental import pallas as pl
from jax.experimental.pallas import tpu as pltpu

F32 = jnp.float32
BF16 = jnp.bfloat16
MESH = pl.DeviceIdType.MESH

D_MODEL = 1024
N_HEADS = 8
QK_NOPE = 64
QK_ROPE = 32
QK_HEAD = QK_NOPE + QK_ROPE
V_HEAD = 64
Q_RANK = 384
KV_RANK = 256
ROPE_THETA = 10000.0
SSM_W = 512
SSM_H = 16
SSM_G = 32
SSM_P = 64
SSM_CH = SSM_G * SSM_P
D_FF = 2816
EPS = 1e-6
ADAM_LR = 0.001
ADAM_B1 = 0.9
ADAM_B2 = 0.999
ADAM_EPS = 1e-08
ADAM_WD = 0.01
ADAM_STEP = 10

LANES = 128
SUBLANES = 8
VMEM_LIMIT = 56 * 1024 * 1024

HEAD_SLOT = LANES
HP = N_HEADS * HEAD_SLOT
P_CQ, P_CKV, P_KR, P_U, P_GL, P_END = 0, 384, 640, 768, 1280, 3328
KR_LANE = 64

FLAT_W = 1024
N_CHIPS = 4


def _tile(n, cap):
    if n <= cap:
        return n
    best = None
    for t in range(LANES, cap + 1, LANES):
        if n % t == 0:
            best = t
    assert best is not None, (n, cap)
    return best


def _params(sem):
    return pltpu.CompilerParams(dimension_semantics=sem, vmem_limit_bytes=VMEM_LIMIT)


def _dot(a, b):
    return jnp.dot(a, b, preferred_element_type=F32)


def _dot_nt(a, b):
    return lax.dot_general(a, b, (((1,), (1,)), ((), ())), preferred_element_type=F32)


def _dot_tn(a, b):
    return lax.dot_general(a, b, (((0,), (0,)), ((), ())), preferred_element_type=F32)


def _rms(x, g):
    r = lax.rsqrt(jnp.mean(x * x, axis=-1, keepdims=True) + EPS)
    return x * r * g, r


def _rms_bwd(dy, x, g):
    r = lax.rsqrt(jnp.mean(x * x, axis=-1, keepdims=True) + EPS)
    dyg = dy * g
    dx = r * dyg - x * (r * r * r) * jnp.mean(dyg * x, axis=-1, keepdims=True)
    dg = jnp.sum(dy * x * r, axis=0, keepdims=True)
    return dx, dg


_GELU_K0 = math.sqrt(2.0 / math.pi)
_GELU_K1 = 0.044715


def _gelu(x):
    th = jnp.tanh(_GELU_K0 * (x + _GELU_K1 * x * x * x))
    return 0.5 * x * (1.0 + th)


def _gelu_grad(x):
    th = jnp.tanh(_GELU_K0 * (x + _GELU_K1 * x * x * x))
    return 0.5 * (1.0 + th) + 0.5 * x * (1.0 - th * th) * _GELU_K0 * (1.0 + 3.0 * _GELU_K1 * x * x)


def _sigmoid(x):
    return 1.0 / (1.0 + jnp.exp(-x))


def _rope(q, c, s):
    n = q.shape[1]
    lane = lax.broadcasted_iota(jnp.int32, q.shape, 1) % HEAD_SLOT
    sw = jnp.where(lane < KR_LANE + QK_ROPE // 2, pltpu.roll(q, n - QK_ROPE // 2, 1), pltpu.roll(q, QK_ROPE // 2, 1))
    return q * c + sw * s


def _rope_bwd(dy, c, s):
    n = dy.shape[1]
    t = dy * s
    lane = lax.broadcasted_iota(jnp.int32, dy.shape, 1) % HEAD_SLOT
    sw = jnp.where(lane < KR_LANE + QK_ROPE // 2, pltpu.roll(t, n - QK_ROPE // 2, 1), pltpu.roll(t, QK_ROPE // 2, 1))
    rope_lane = jnp.logical_and(lane >= KR_LANE, lane < KR_LANE + QK_ROPE)
    return dy * c + jnp.where(rope_lane, sw, 0.0)


def _shift_down(x, k, halo):
    xs = pltpu.roll(x, k, 0)
    hs = pltpu.roll(halo, k, 0)
    rows = lax.broadcasted_iota(jnp.int32, halo.shape, 0)
    top = jnp.where(rows < k, hs, xs[0:SUBLANES])
    return jnp.concatenate([top, xs[SUBLANES:]], axis=0)


def _shift_up(x, k, halo):
    t = x.shape[0]
    xs = pltpu.roll(x, t - k, 0)
    hs = pltpu.roll(halo, SUBLANES - k, 0)
    rows = lax.broadcasted_iota(jnp.int32, halo.shape, 0)
    bot = jnp.where(rows >= SUBLANES - k, hs, xs[t - SUBLANES:])
    return jnp.concatenate([xs[:t - SUBLANES], bot], axis=0)


def _mm(a, b, name, out_dtype=F32, bt=False, tm_cap=1024, tn_cap=1408):
    m, k = a.shape[-2:]
    parts = a.shape[0] if a.ndim == 3 else 1
    n = b.shape[0] if bt else b.shape[1]
    tm = min(tm_cap, m)
    tn = _tile(n, tn_cap)

    def body(a_ref, b_ref, o_ref):
        if not bt:
            o_ref[...] = _dot(a_ref[...], b_ref[...]).astype(out_dtype)
        elif parts == 1:
            o_ref[...] = _dot_nt(a_ref[...], b_ref[...]).astype(out_dtype)
        else:
            @pl.when(pl.program_id(2) == 0)
            def _():
                o_ref[...] = jnp.zeros_like(o_ref)

            o_ref[...] += _dot_nt(a_ref[...], b_ref[...])

    if bt:
        b_spec = pl.BlockSpec((tn, k), lambda j, i, s: (j, s))
    else:
        b_spec = pl.BlockSpec((k, tn), lambda j, i, s: (0, j))
    if a.ndim == 3:
        assert bt and out_dtype == F32
        a_spec = pl.BlockSpec((None, tm, k), lambda j, i, s: (s, i, 0))
    else:
        a_spec = pl.BlockSpec((tm, k), lambda j, i, s: (i, 0))
    return pl.pallas_call(
        body, name=name, grid=(n // tn, m // tm, parts),
        in_specs=[a_spec, b_spec],
        out_specs=pl.BlockSpec((tm, tn), lambda j, i, s: (i, j)),
        out_shape=jax.ShapeDtypeStruct((m, n), out_dtype),
        compiler_params=_params(("parallel", "parallel", "arbitrary")),
    )(a, b)


def _mm_tn(a, b, name, tk_cap=1024, tn_cap=1664, tl_cap=2048, chips=False):
    l, k = a.shape
    tk = _tile(k, tk_cap)
    tl = min(tl_cap, l)

    def body(a_ref, b_ref, o_ref):
        @pl.when(pl.program_id(2) == 0)
        def _():
            o_ref[...] = jnp.zeros_like(o_ref)

        o_ref[...] += _dot_tn(a_ref[...], b_ref[...])

    if chips:
        n = b.shape[-1] * (b.shape[0] if b.ndim == 3 else 1)
        tn = n // N_CHIPS
        assert tn % LANES == 0
        if b.ndim == 3:
            per = N_CHIPS // b.shape[0]
            b_spec = pl.BlockSpec((None, tl, tn), lambda i, j, r: (j // per, r, j % per))
        else:
            b_spec = pl.BlockSpec((tl, tn), lambda i, j, r: (r, j))
        out_spec = pl.BlockSpec((None, tk, tn), lambda i, j, r: (j, i, 0))
        out_shape = jax.ShapeDtypeStruct((N_CHIPS, k, tn), F32)
    else:
        n = b.shape[1]
        tn = _tile(n, tn_cap)
        b_spec = pl.BlockSpec((tl, tn), lambda i, j, r: (r, j))
        out_spec = pl.BlockSpec((tk, tn), lambda i, j, r: (i, j))
        out_shape = jax.ShapeDtypeStruct((k, n), F32)
    return pl.pallas_call(
        body, name=name, grid=(k // tk, n // tn, l // tl),
        in_specs=[pl.BlockSpec((tl, tk), lambda i, j, r: (r, i)), b_spec],
        out_specs=out_spec, out_shape=out_shape,
        compiler_params=_params(("parallel", "parallel", "arbitrary")),
    )(a, b)


def _row(tl, n):
    return pl.BlockSpec((tl, n), lambda i: (i, 0))


def _const(shape):
    return pl.BlockSpec(shape, lambda i: tuple(0 for _ in shape))


def _proj_fwd(x, g1, win, gq, wuq, gkv, wukv, rc, rs, bg, tl):
    l = x.shape[0]

    def body(x_ref, g1_ref, win_ref, gq_ref, wuq_ref, gkv_ref, wukv_ref, rc_ref, rs_ref, bg_ref,
             hn_ref, cq_ref, ckv_ref, q_ref, k_ref, v_ref, u_ref, gl_ref):
        hn, _ = _rms(x_ref[...], g1_ref[...])
        hnb = hn.astype(BF16)
        hn_ref[...] = hnb
        proj = _dot(hnb, win_ref[...])
        cq = proj[:, P_CQ:P_CKV]
        ckv = proj[:, P_CKV:P_KR]
        kr = proj[:, P_KR:P_U]
        cq_ref[...] = cq
        ckv_ref[...] = ckv
        u_ref[...] = proj[:, P_U:P_GL]
        gl_ref[...] = (proj[:, P_GL:P_END] + bg_ref[...]).astype(BF16)
        qn, _ = _rms(cq, gq_ref[...])
        q = _dot(qn.astype(BF16), wuq_ref[...])
        c1 = rc_ref[...]
        s1 = rs_ref[...]
        q_ref[...] = (_rope(q, jnp.tile(c1, (1, N_HEADS)), jnp.tile(s1, (1, N_HEADS))) * Q_PRESCALE).astype(BF16)
        ckvn, _ = _rms(ckv, gkv_ref[...])
        kv = _dot(ckvn.astype(BF16), wukv_ref[...])
        krr = _rope(kr, c1, s1)
        k_ref[...] = (kv[:, :HP] + jnp.tile(krr, (1, N_HEADS))).astype(BF16)
        v_ref[...] = kv[:, HP:].astype(BF16)

    outs = [(D_MODEL, BF16), (Q_RANK, F32), (KV_RANK, F32), (HP, BF16), (HP, BF16), (HP, BF16),
            (SSM_W, F32), (2 * D_MODEL, BF16)]
    return pl.pallas_call(
        body, name="proj_fwd", grid=(l // tl,),
        in_specs=[_row(tl, D_MODEL), _const((1, D_MODEL)), _const((D_MODEL, P_END)), _const((1, Q_RANK)),
                  _const((Q_RANK, HP)), _const((1, KV_RANK)), _const((KV_RANK, 2 * HP)),
                  _row(tl, HEAD_SLOT), _row(tl, HEAD_SLOT), _const((1, 2 * D_MODEL))],
        out_specs=[_row(tl, n) for n, _ in outs],
        out_shape=[jax.ShapeDtypeStruct((l, n), dt) for n, dt in outs],
        compiler_params=_params(("parallel",)),
    )(x, g1, win, gq, wuq, gkv, wukv, rc, rs, bg)


_NEG = -1e30


LOG2E = 1.0 / math.log(2.0)
LN2 = math.log(2.0)
ATTN_SCALE = 1.0 / math.sqrt(QK_HEAD)
Q_PRESCALE = ATTN_SCALE * LOG2E
FWD_HEADS = 8
BWD_HEADS = 8


def _causal_pairs(nq, by_query):
    if by_query:
        pairs = [(i, j) for i in range(nq) for j in range(i + 1)]
    else:
        pairs = [(i, j) for j in range(nq) for i in range(j, nq)]
    return jnp.array([p[0] for p in pairs], jnp.int32), jnp.array([p[1] for p in pairs], jnp.int32)


def _diag_mask_t(s):
    rows = lax.broadcasted_iota(jnp.int32, s.shape, 0)
    cols = lax.broadcasted_iota(jnp.int32, s.shape, 1)
    return jnp.where(rows <= cols, s, _NEG)


def _attn_fwd(q, k, v, tq, heads):
    l = q.shape[0]
    nq = l // tq
    it, jt = _causal_pairs(nq, True)

    def body(it_ref, jt_ref, q_ref, k_ref, v_ref, o_ref, lse_ref, m_ref, l_ref, acc_ref):
        t = pl.program_id(1)
        i = it_ref[t]
        j = jt_ref[t]

        @pl.when(j == 0)
        def _():
            m_ref[...] = jnp.full_like(m_ref, _NEG)
            l_ref[...] = jnp.zeros_like(l_ref)
            acc_ref[...] = jnp.zeros_like(acc_ref)

        def update(on_diagonal):
            for hh in range(heads):
                sl = slice(hh * HEAD_SLOT, (hh + 1) * HEAD_SLOT)
                s = _dot_nt(k_ref[:, sl], q_ref[:, sl])
                if on_diagonal:
                    s = _diag_mask_t(s)
                m_old = m_ref[hh]
                m_new = jnp.maximum(m_old, jnp.max(s, axis=0, keepdims=True))
                p = jnp.exp2(s - m_new)
                alpha = jnp.exp2(m_old - m_new)
                l_ref[hh] = alpha * l_ref[hh] + jnp.sum(p, axis=0, keepdims=True)
                acc_ref[hh] = alpha * acc_ref[hh] + _dot_tn(v_ref[:, sl], p.astype(BF16))
                m_ref[hh] = m_new

        @pl.when(j < i)
        def _():
            update(False)

        @pl.when(j == i)
        def _():
            update(True)
            for hh in range(heads):
                sl = slice(hh * HEAD_SLOT, (hh + 1) * HEAD_SLOT)
                o_ref[:, sl] = (acc_ref[hh] / l_ref[hh]).T.astype(BF16)
                lse_ref[hh] = m_ref[hh] + jnp.log(l_ref[hh]) * LOG2E

    blk = (tq, (heads * HEAD_SLOT))
    qmap = lambda h, t, it_ref, jt_ref: (it_ref[t], h)
    kmap = lambda h, t, it_ref, jt_ref: (jt_ref[t], h)
    row = pl.BlockSpec((heads, 1, tq), lambda h, t, it_ref, jt_ref: (h, 0, it_ref[t]))
    return pl.pallas_call(
        body, name="attn_fwd",
        grid_spec=pltpu.PrefetchScalarGridSpec(
            num_scalar_prefetch=2, grid=(N_HEADS // heads, it.shape[0]),
            in_specs=[pl.BlockSpec(blk, qmap), pl.BlockSpec(blk, kmap), pl.BlockSpec(blk, kmap)],
            out_specs=[pl.BlockSpec(blk, qmap), row],
            scratch_shapes=[pltpu.VMEM((heads, 1, tq), F32), pltpu.VMEM((heads, 1, tq), F32),
                            pltpu.VMEM((heads, HEAD_SLOT, tq), F32)]),
        out_shape=[jax.ShapeDtypeStruct((l, HP), BF16), jax.ShapeDtypeStruct((N_HEADS, 1, l), F32)],
        compiler_params=_params(("parallel", "arbitrary")),
    )(it, jt, q, k, v)


def _attn_delta(o, do, tq, heads):
    l = o.shape[0]

    def body(o_ref, do_ref, d_ref):
        prod = o_ref[...].astype(F32) * do_ref[...].astype(F32)
        for hh in range(heads):
            d_ref[hh] = jnp.sum(prod[:, hh * HEAD_SLOT:(hh + 1) * HEAD_SLOT].T, axis=0, keepdims=True)

    blk = pl.BlockSpec((tq, (heads * HEAD_SLOT)), lambda h, i: (i, h))
    return pl.pallas_call(
        body, name="attn_delta", grid=(N_HEADS // heads, l // tq), in_specs=[blk, blk],
        out_specs=pl.BlockSpec((heads, 1, tq), lambda h, i: (h, 0, i)),
        out_shape=jax.ShapeDtypeStruct((N_HEADS, 1, l), F32),
        compiler_params=_params(("parallel", "parallel")),
    )(o, do)


def _attn_bwd(q, k, v, do, lse, delta, tq, heads):
    l = q.shape[0]
    nq = l // tq
    it, jt = _causal_pairs(nq, False)

    def body(it_ref, jt_ref, q_ref, k_ref, v_ref, do_ref, lse_ref, dl_ref, dq_ref, dk_ref, dv_ref, dka_ref, dva_ref):
        t = pl.program_id(1)
        i = it_ref[t]
        j = jt_ref[t]

        @pl.when(t == 0)
        def _():
            dq_ref[...] = jnp.zeros_like(dq_ref)

        @pl.when(i == j)
        def _():
            dka_ref[...] = jnp.zeros_like(dka_ref)
            dva_ref[...] = jnp.zeros_like(dva_ref)

        def update(on_diagonal):
            r0 = pl.multiple_of(i * tq, tq)
            for hh in range(heads):
                sl = slice(hh * HEAD_SLOT, (hh + 1) * HEAD_SLOT)
                qb = q_ref[:, sl]
                kb = k_ref[:, sl]
                dob = do_ref[:, sl]
                s = _dot_nt(kb, qb)
                if on_diagonal:
                    s = _diag_mask_t(s)
                p = jnp.exp2(s - lse_ref[hh])
                dva_ref[:, sl] += _dot(p.astype(BF16), dob)
                dp = _dot_nt(v_ref[:, sl], dob)
                ds = (p * (dp - dl_ref[hh])).astype(BF16)
                dka_ref[:, sl] += _dot(ds, qb)
                dq_ref[pl.ds(r0, tq), sl] += ATTN_SCALE * _dot_tn(ds, kb)

        @pl.when(j < i)
        def _():
            update(False)

        @pl.when(j == i)
        def _():
            update(True)

        @pl.when(i == nq - 1)
        def _():
            dk_ref[...] = (dka_ref[...] * LN2).astype(BF16)
            dv_ref[...] = dva_ref[...].astype(BF16)

    blk = (tq, (heads * HEAD_SLOT))
    qmap = lambda h, t, it_ref, jt_ref: (it_ref[t], h)
    kmap = lambda h, t, it_ref, jt_ref: (jt_ref[t], h)
    row = pl.BlockSpec((heads, 1, tq), lambda h, t, it_ref, jt_ref: (h, 0, it_ref[t]))
    return pl.pallas_call(
        body, name="attn_bwd",
        grid_spec=pltpu.PrefetchScalarGridSpec(
            num_scalar_prefetch=2, grid=(N_HEADS // heads, it.shape[0]),
            in_specs=[pl.BlockSpec(blk, qmap), pl.BlockSpec(blk, kmap), pl.BlockSpec(blk, kmap),
                      pl.BlockSpec(blk, qmap), row, row],
            out_specs=[pl.BlockSpec((l, (heads * HEAD_SLOT)), lambda h, t, it_ref, jt_ref: (0, h)), pl.BlockSpec(blk, kmap),
                       pl.BlockSpec(blk, kmap)],
            scratch_shapes=[pltpu.VMEM(blk, F32), pltpu.VMEM(blk, F32)]),
        out_shape=[jax.ShapeDtypeStruct((l, HP), F32), jax.ShapeDtypeStruct((l, HP), BF16),
                   jax.ShapeDtypeStruct((l, HP), BF16)],
        compiler_params=_params(("parallel", "arbitrary")),
    )(it, jt, q, k, v, do, lse, delta)


SSM_CB = 512
SSM_UB = 128
SSM_NB = SSM_CH // SSM_CB


def _scan_tiles(re_ref, im_ref, tab, carry, n_tiles, reverse):
    group = 2
    assert n_tiles % group == 0
    pr, pi = tab[6], tab[7]

    def inside(sr, si):
        for step, k in enumerate((1, 2, 4)):
            mr, mi = tab[2 * step], tab[2 * step + 1]
            sh = (SUBLANES - k) if reverse else k
            rr = pltpu.roll(sr, sh, 0)
            ri = pltpu.roll(si, sh, 0)
            sr, si = sr + mr * rr - mi * ri, si + mr * ri + mi * rr
        return sr, si

    def body(n, c):
        cr, ci = c
        first = (n_tiles - group * (n + 1)) if reverse else group * n
        r0 = pl.multiple_of(first * SUBLANES, group * SUBLANES)
        rows = [pl.ds(r0 + g * SUBLANES, SUBLANES) for g in range(group)]
        tiles = [inside(re_ref[r, :], im_ref[r, :]) for r in rows]
        for g in (range(group - 1, -1, -1) if reverse else range(group)):
            sr, si = tiles[g]
            sr, si = sr + pr * cr - pi * ci, si + pr * ci + pi * cr
            re_ref[rows[g], :] = sr
            im_ref[rows[g], :] = si
            edge = slice(0, 1) if reverse else slice(SUBLANES - 1, SUBLANES)
            cr, ci = sr[edge, :], si[edge, :]
        return cr, ci

    return lax.fori_loop(0, n_tiles // group, body, carry)


def _ssm_fwd(u, bre, bim, cre, cim, dvec, tab, tt):
    l = u.shape[0]
    nt = l // tt

    def body(u_ref, bre_ref, bim_ref, cre_ref, cim_ref, d_ref, tab_ref, y_ref, sre_ref, sim_ref, car_ref):
        @pl.when(pl.program_id(1) == 0)
        def _():
            car_ref[...] = jnp.zeros_like(car_ref)

        uf = u_ref[...]
        ub = uf.astype(BF16)
        sre_ref[...] = _dot(ub, bre_ref[0])
        sim_ref[...] = _dot(ub, bim_ref[0])
        tab_v = [tab_ref[n] for n in range(8)]
        cr, ci = _scan_tiles(sre_ref, sim_ref, tab_v, (car_ref[0:1, :], car_ref[8:9, :]), tt // SUBLANES, False)
        car_ref[0:1, :] = cr
        car_ref[8:9, :] = ci
        y_ref[...] = (_dot(sre_ref[...].astype(BF16), cre_ref[0]) - _dot(sim_ref[...].astype(BF16), cim_ref[0])
                      + d_ref[...] * uf)

    return pl.pallas_call(
        body, name="ssm_fwd", grid=(SSM_NB, nt),
        in_specs=[pl.BlockSpec((tt, SSM_UB), lambda m, t: (t, m)),
                  pl.BlockSpec((1, SSM_UB, SSM_CB), lambda m, t: (m, 0, 0)),
                  pl.BlockSpec((1, SSM_UB, SSM_CB), lambda m, t: (m, 0, 0)),
                  pl.BlockSpec((1, SSM_CB, SSM_UB), lambda m, t: (m, 0, 0)),
                  pl.BlockSpec((1, SSM_CB, SSM_UB), lambda m, t: (m, 0, 0)),
                  pl.BlockSpec((1, SSM_UB), lambda m, t: (0, m)),
                  pl.BlockSpec((8, SUBLANES, SSM_CB), lambda m, t: (0, 0, m))],
        out_specs=[pl.BlockSpec((tt, SSM_UB), lambda m, t: (t, m)),
                   pl.BlockSpec((tt, SSM_CB), lambda m, t: (t, m)),
                   pl.BlockSpec((tt, SSM_CB), lambda m, t: (t, m))],
        out_shape=[jax.ShapeDtypeStruct((l, SSM_W), F32), jax.ShapeDtypeStruct((l, SSM_CH), F32),
                   jax.ShapeDtypeStruct((l, SSM_CH), F32)],
        scratch_shapes=[pltpu.VMEM((2 * SUBLANES, SSM_CB), F32)],
        compiler_params=_params(("parallel", "arbitrary")),
    )(u, bre, bim, cre, cim, dvec, tab)


def _ssm_bwd(dy, u, sre, sim, bre, bim, cre, cim, dvec, tab, tt):
    l = u.shape[0]
    nt = l // tt
    tpb = tt // SUBLANES

    def body(dy_ref, u_ref, sre_ref, sim_ref, hre_ref, him_ref, bre_ref, bim_ref, cre_ref, cim_ref, d_ref, tab_ref,
             du_ref, dbre_ref, dbim_ref, dcre_ref, dcim_ref, dare_ref, daim_ref, dd_ref, lr_ref, li_ref, car_ref):
        t = pl.program_id(1)

        @pl.when(t == 0)
        def _():
            car_ref[...] = jnp.zeros_like(car_ref)
            for ref in (dbre_ref, dbim_ref, dcre_ref, dcim_ref, dare_ref, daim_ref, dd_ref):
                ref[...] = jnp.zeros_like(ref)

        dyf = dy_ref[...]
        dyb = dyf.astype(BF16)
        uf = u_ref[...]
        s_re = sre_ref[...]
        s_im = sim_ref[...]
        lr_ref[...] = _dot_nt(dyb, cre_ref[0])
        li_ref[...] = -_dot_nt(dyb, cim_ref[0])
        dcre_ref[0] += _dot_tn(s_re.astype(BF16), dyb)
        dcim_ref[0] -= _dot_tn(s_im.astype(BF16), dyb)
        tab_v = [tab_ref[n] for n in range(8)]
        cr, ci = _scan_tiles(lr_ref, li_ref, tab_v, (car_ref[0:1, :], car_ref[8:9, :]), tpb, True)
        car_ref[0:1, :] = cr
        car_ref[8:9, :] = ci
        lam_r = lr_ref[...]
        lam_i = li_ref[...]
        keep = jnp.where(t == nt - 1, 0.0, 1.0)
        sp_r = _shift_down(s_re, 1, hre_ref[...] * keep)
        sp_i = _shift_down(s_im, 1, him_ref[...] * keep)
        dare_ref[...] += jnp.sum(lam_r * sp_r + lam_i * sp_i, axis=0, keepdims=True)
        daim_ref[...] += jnp.sum(lam_i * sp_r - lam_r * sp_i, axis=0, keepdims=True)
        lrb = lam_r.astype(BF16)
        lib = lam_i.astype(BF16)
        du_ref[...] = _dot_nt(lrb, bre_ref[0]) + _dot_nt(lib, bim_ref[0]) + dyf * d_ref[...]
        ub = uf.astype(BF16)
        dbre_ref[0] += _dot_tn(ub, lrb)
        dbim_ref[0] += _dot_tn(ub, lib)
        dd_ref[...] += jnp.sum(dyf * uf, axis=0, keepdims=True)

    rev = lambda m, t: (nt - 1 - t, m)
    halo = lambda m, t: (jnp.maximum((nt - 1 - t) * tpb - 1, 0), m)
    wb = pl.BlockSpec((1, SSM_UB, SSM_CB), lambda m, t: (m, 0, 0))
    wc = pl.BlockSpec((1, SSM_CB, SSM_UB), lambda m, t: (m, 0, 0))
    vec_c = pl.BlockSpec((1, SSM_CB), lambda m, t: (0, m))
    vec_u = pl.BlockSpec((1, SSM_UB), lambda m, t: (0, m))
    return pl.pallas_call(
        body, name="ssm_bwd", grid=(SSM_NB, nt),
        in_specs=[pl.BlockSpec((tt, SSM_UB), rev), pl.BlockSpec((tt, SSM_UB), rev),
                  pl.BlockSpec((tt, SSM_CB), rev), pl.BlockSpec((tt, SSM_CB), rev),
                  pl.BlockSpec((SUBLANES, SSM_CB), halo), pl.BlockSpec((SUBLANES, SSM_CB), halo),
                  wb, wb, wc, wc, vec_u,
                  pl.BlockSpec((8, SUBLANES, SSM_CB), lambda m, t: (0, 0, m))],
        out_specs=[pl.BlockSpec((tt, SSM_UB), rev), wb, wb, wc, wc, vec_c, vec_c, vec_u],
        out_shape=[jax.ShapeDtypeStruct((l, SSM_W), F32),
                   jax.ShapeDtypeStruct((SSM_NB, SSM_UB, SSM_CB), F32), jax.ShapeDtypeStruct((SSM_NB, SSM_UB, SSM_CB), F32),
                   jax.ShapeDtypeStruct((SSM_NB, SSM_CB, SSM_UB), F32), jax.ShapeDtypeStruct((SSM_NB, SSM_CB, SSM_UB), F32),
                   jax.ShapeDtypeStruct((1, SSM_CH), F32), jax.ShapeDtypeStruct((1, SSM_CH), F32),
                   jax.ShapeDtypeStruct((1, SSM_W), F32)],
        scratch_shapes=[pltpu.VMEM((tt, SSM_CB), F32), pltpu.VMEM((tt, SSM_CB), F32),
                        pltpu.VMEM((2 * SUBLANES, SSM_CB), F32)],
        compiler_params=_params(("parallel", "arbitrary")),
    )(dy, u, sre, sim, sre, sim, bre, bim, cre, cim, dvec, tab)


def _merge_fwd(x, gl, attn, y1, wba, wbs, wglu, bglu, wout, gpost, gpre, tl):
    l = x.shape[0]

    def body(x_ref, gl_ref, at_ref, y1_ref, wba_ref, wbs_ref, wglu_ref, bglu_ref, wout_ref, gpost_ref, gpre_ref,
             a_ref, sm_ref, mg_ref, z_ref, x1_ref, hn2_ref, y3_ref):
        y2 = _gelu(y1_ref[...])
        sg = _sigmoid(_dot(y2.astype(BF16), wglu_ref[...]) + bglu_ref[...])
        y3 = (y2 * sg).astype(BF16)
        y3_ref[...] = y3
        a = _dot(at_ref[...], wba_ref[...])
        sm = _dot(y3, wbs_ref[...])
        a_ref[...] = a.astype(BF16)
        sm_ref[...] = sm.astype(BF16)
        g = _sigmoid(gl_ref[...].astype(F32))
        merged = (g[:, :D_MODEL] * a + g[:, D_MODEL:] * sm).astype(BF16)
        mg_ref[...] = merged
        z = _dot(merged, wout_ref[...])
        z_ref[...] = z
        n, _ = _rms(z, gpost_ref[...])
        x1 = x_ref[...] + n
        x1_ref[...] = x1
        hn2, _ = _rms(x1, gpre_ref[...])
        hn2_ref[...] = hn2.astype(BF16)

    outs = [(D_MODEL, BF16), (D_MODEL, BF16), (D_MODEL, BF16), (D_MODEL, F32), (D_MODEL, F32), (D_MODEL, BF16),
            (SSM_W, BF16)]
    return pl.pallas_call(
        body, name="merge_fwd", grid=(l // tl,),
        in_specs=[_row(tl, D_MODEL), _row(tl, 2 * D_MODEL), _row(tl, HP), _row(tl, SSM_W),
                  _const((HP, D_MODEL)), _const((SSM_W, D_MODEL)), _const((SSM_W, SSM_W)), _const((1, SSM_W)),
                  _const((D_MODEL, D_MODEL)), _const((1, D_MODEL)), _const((1, D_MODEL))],
        out_specs=[_row(tl, n) for n, _ in outs],
        out_shape=[jax.ShapeDtypeStruct((l, n), dt) for n, dt in outs],
        compiler_params=_params(("parallel",)),
    )(x, gl, attn, y1, wba, wbs, wglu, bglu, wout, gpost, gpre)


def _merge_bwd(dhn2, x1, dx2, z, gl, a, sm, y1, wba, wbs, wglu, bglu, wout, gpost, gpre, tl):
    l = x1.shape[0]

    def body(dhn2_ref, x1_ref, dx2_ref, z_ref, gl_ref, a_ref, sm_ref, y1_ref,
             wba_ref, wbs_ref, wglu_ref, bglu_ref, wout_ref, gpost_ref, gpre_ref,
             dx1_ref, dz_ref, dbra_ref, dbrs_ref, dgl_ref, dat_ref, dy1_ref, dt_ref, y2_ref,
             dgpre_ref, dgpost_ref, dbg_ref, dbglu_ref):
        @pl.when(pl.program_id(0) == 0)
        def _():
            for ref in (dgpre_ref, dgpost_ref, dbg_ref, dbglu_ref):
                ref[...] = jnp.zeros_like(ref)

        dx1a, dgpre = _rms_bwd(dhn2_ref[...], x1_ref[...], gpre_ref[...])
        dgpre_ref[...] += dgpre
        dx1 = dx2_ref[...] + dx1a
        dx1_ref[...] = dx1
        dz, dgpost = _rms_bwd(dx1, z_ref[...], gpost_ref[...])
        dgpost_ref[...] += dgpost
        dzb = dz.astype(BF16)
        dz_ref[...] = dzb
        dm = _dot_nt(dzb, wout_ref[...])
        g = _sigmoid(gl_ref[...].astype(F32))
        g0 = g[:, :D_MODEL]
        g1 = g[:, D_MODEL:]
        dbra = (dm * g0).astype(BF16)
        dbrs = (dm * g1).astype(BF16)
        dbra_ref[...] = dbra
        dbrs_ref[...] = dbrs
        dgl0 = dm * a_ref[...].astype(F32) * g0 * (1.0 - g0)
        dgl1 = dm * sm_ref[...].astype(F32) * g1 * (1.0 - g1)
        dgl_ref[:, :D_MODEL] = dgl0.astype(BF16)
        dgl_ref[:, D_MODEL:] = dgl1.astype(BF16)
        dbg_ref[:, :D_MODEL] += jnp.sum(dgl0, axis=0, keepdims=True)
        dbg_ref[:, D_MODEL:] += jnp.sum(dgl1, axis=0, keepdims=True)
        dat_ref[...] = _dot_nt(dbra, wba_ref[...]).astype(BF16)
        dy3 = _dot_nt(dbrs, wbs_ref[...])
        y1v = y1_ref[...]
        y2 = _gelu(y1v)
        y2b = y2.astype(BF16)
        y2_ref[...] = y2b
        sg = _sigmoid(_dot(y2b, wglu_ref[...]) + bglu_ref[...])
        dt = dy3 * y2 * sg * (1.0 - sg)
        dtb = dt.astype(BF16)
        dt_ref[...] = dtb
        dbglu_ref[...] += jnp.sum(dt, axis=0, keepdims=True)
        dy2 = dy3 * sg + _dot_nt(dtb, wglu_ref[...])
        dy1_ref[...] = dy2 * _gelu_grad(y1v)

    outs = [(D_MODEL, F32), (D_MODEL, BF16), (D_MODEL, BF16), (D_MODEL, BF16), (2 * D_MODEL, BF16), (HP, BF16),
            (SSM_W, F32), (SSM_W, BF16), (SSM_W, BF16)]
    accs = [D_MODEL, D_MODEL, 2 * D_MODEL, SSM_W]
    return pl.pallas_call(
        body, name="merge_bwd", grid=(l // tl,),
        in_specs=[_row(tl, D_MODEL), _row(tl, D_MODEL), _row(tl, D_MODEL), _row(tl, D_MODEL),
                  _row(tl, 2 * D_MODEL), _row(tl, D_MODEL), _row(tl, D_MODEL), _row(tl, SSM_W),
                  _const((HP, D_MODEL)), _const((SSM_W, D_MODEL)), _const((SSM_W, SSM_W)), _const((1, SSM_W)),
                  _const((D_MODEL, D_MODEL)), _const((1, D_MODEL)), _const((1, D_MODEL))],
        out_specs=[_row(tl, n) for n, _ in outs] + [_const((1, n)) for n in accs],
        out_shape=[jax.ShapeDtypeStruct((l, n), dt) for n, dt in outs]
        + [jax.ShapeDtypeStruct((1, n), F32) for n in accs],
        compiler_params=_params(("arbitrary",)),
    )(dhn2, x1, dx2, z, gl, a, sm, y1, wba, wbs, wglu, bglu, wout, gpost, gpre)


def _proj_bwd(x, dx1, cq, ckv, dq, dk, dv, du, dgl, g1, win, gq, wuq, gkv, wukv, rc, rs, tl):
    l = x.shape[0]

    def body(x_ref, dx1_ref, cq_ref, ckv_ref, dq_ref, dk_ref, dv_ref, du_ref, dgl_ref,
             g1_ref, win_ref, gq_ref, wuq_ref, gkv_ref, wukv_ref, rc_ref, rs_ref,
             gx_ref, dql_ref, qn_ref, ckvn_ref, dproj_ref, dg1_ref, dgq_ref, dgkv_ref):
        @pl.when(pl.program_id(0) == 0)
        def _():
            for ref in (dg1_ref, dgq_ref, dgkv_ref):
                ref[...] = jnp.zeros_like(ref)

        c1 = rc_ref[...]
        s1 = rs_ref[...]
        dql = _rope_bwd(dq_ref[...], jnp.tile(c1, (1, N_HEADS)), jnp.tile(s1, (1, N_HEADS))).astype(BF16)
        dql_ref[...] = dql
        dqn = _dot_nt(dql, wuq_ref[...])
        cq = cq_ref[...]
        qn, _ = _rms(cq, gq_ref[...])
        qn_ref[...] = qn.astype(BF16)
        dcq, dgq = _rms_bwd(dqn, cq, gq_ref[...])
        dgq_ref[...] += dgq
        dkb = dk_ref[...]
        dvb = dv_ref[...]
        dkf = dkb.astype(F32)
        dkr = dkf[:, 0:HEAD_SLOT]
        for h in range(1, N_HEADS):
            dkr = dkr + dkf[:, h * HEAD_SLOT:(h + 1) * HEAD_SLOT]
        dkr = _rope_bwd(dkr, c1, s1)
        dckvn = _dot_nt(dkb, wukv_ref[:, :HP]) + _dot_nt(dvb, wukv_ref[:, HP:])
        ckv = ckv_ref[...]
        ckvn, _ = _rms(ckv, gkv_ref[...])
        ckvn_ref[...] = ckvn.astype(BF16)
        dckv, dgkv = _rms_bwd(dckvn, ckv, gkv_ref[...])
        dgkv_ref[...] += dgkv
        dproj_ref[:, P_CQ:P_CKV] = dcq.astype(BF16)
        dproj_ref[:, P_CKV:P_KR] = dckv.astype(BF16)
        dproj_ref[:, P_KR:P_U] = dkr.astype(BF16)
        dproj_ref[:, P_U:P_GL] = du_ref[...].astype(BF16)
        dproj_ref[:, P_GL:P_END] = dgl_ref[...]
        dhn = _dot_nt(dproj_ref[...], win_ref[...])
        dxa, dg1 = _rms_bwd(dhn, x_ref[...], g1_ref[...])
        dg1_ref[...] += dg1
        gx_ref[...] = dx1_ref[...] + dxa

    outs = [(D_MODEL, F32), (HP, BF16), (Q_RANK, BF16), (KV_RANK, BF16), (P_END, BF16)]
    accs = [D_MODEL, Q_RANK, KV_RANK]
    return pl.pallas_call(
        body, name="proj_bwd", grid=(l // tl,),
        in_specs=[_row(tl, D_MODEL), _row(tl, D_MODEL), _row(tl, Q_RANK), _row(tl, KV_RANK), _row(tl, HP),
                  _row(tl, HP), _row(tl, HP), _row(tl, SSM_W), _row(tl, 2 * D_MODEL),
                  _const((1, D_MODEL)), _const((D_MODEL, P_END)), _const((1, Q_RANK)), _const((Q_RANK, HP)),
                  _const((1, KV_RANK)), _const((KV_RANK, 2 * HP)), _row(tl, HEAD_SLOT), _row(tl, HEAD_SLOT)],
        out_specs=[_row(tl, n) for n, _ in outs] + [_const((1, n)) for n in accs],
        out_shape=[jax.ShapeDtypeStruct((l, n), dt) for n, dt in outs]
        + [jax.ShapeDtypeStruct((1, n), F32) for n in accs],
        compiler_params=_params(("arbitrary",)),
    )(x, dx1, cq, ckv, dq, dk, dv, du, dgl, g1, win, gq, wuq, gkv, wukv, rc, rs)


CONV_CB = 256
CONV_NB = D_FF // CONV_CB
CONV_ROWS = 16


def _conv3(h, halo, w, b):
    return b + w[0:1, :] * _shift_down(h, 2, halo) + w[1:2, :] * _shift_down(h, 1, halo) + w[2:3, :] * h


def _conv_fwd(h, cw, cb, tl):
    l = h.shape[0]

    def body(hg_ref, hv_ref, wg_ref, wv_ref, bg_ref, bv_ref, act_ref, halo_ref):
        @pl.when(pl.program_id(1) == 0)
        def _():
            halo_ref[...] = jnp.zeros_like(halo_ref)

        hg = hg_ref[...]
        hv = hv_ref[...]
        cg = _conv3(hg, halo_ref[0:SUBLANES, :], wg_ref[...], bg_ref[...])
        cv = _conv3(hv, halo_ref[SUBLANES:, :], wv_ref[...], bv_ref[...])
        act_ref[...] = (_gelu(cg) * cv).astype(BF16)
        halo_ref[0:SUBLANES, :] = hg[tl - SUBLANES:, :]
        halo_ref[SUBLANES:, :] = hv[tl - SUBLANES:, :]

    gmap = lambda c, r: (r, c)
    vmap = lambda c, r: (r, CONV_NB + c)
    return pl.pallas_call(
        body, name="conv_fwd", grid=(CONV_NB, l // tl),
        in_specs=[pl.BlockSpec((tl, CONV_CB), gmap), pl.BlockSpec((tl, CONV_CB), vmap),
                  pl.BlockSpec((3, CONV_CB), lambda c, r: (0, c)), pl.BlockSpec((3, CONV_CB), lambda c, r: (0, CONV_NB + c)),
                  pl.BlockSpec((1, CONV_CB), lambda c, r: (0, c)), pl.BlockSpec((1, CONV_CB), lambda c, r: (0, CONV_NB + c))],
        out_specs=pl.BlockSpec((tl, CONV_CB), gmap),
        out_shape=jax.ShapeDtypeStruct((l, D_FF), BF16),
        scratch_shapes=[pltpu.VMEM((2 * SUBLANES, CONV_CB), F32)],
        compiler_params=_params(("parallel", "arbitrary")),
    )(h, h, cw, cw, cb, cb)


def _conv_bwd(h, dact, cw, cb, tl):
    l = h.shape[0]
    nr = l // tl
    tpb = tl // SUBLANES

    def body(hg_ref, hv_ref, hgh_ref, hvh_ref, da_ref, wg_ref, wv_ref, bg_ref, bv_ref,
             dh_ref, dwg_ref, dwv_ref, dbg_ref, dbv_ref, car_ref):
        r = pl.program_id(1)

        @pl.when(r == 0)
        def _():
            for ref in (car_ref, dwg_ref, dwv_ref, dbg_ref, dbv_ref):
                ref[...] = jnp.zeros_like(ref)

        keep = jnp.where(r == nr - 1, 0.0, 1.0)
        wg, wv, bg, bv = wg_ref[...], wv_ref[...], bg_ref[...], bv_ref[...]
        nch = tl // CONV_ROWS

        def fold(x):
            s = x[0:SUBLANES, :]
            for k in range(1, CONV_ROWS // SUBLANES):
                s = s + x[k * SUBLANES:(k + 1) * SUBLANES, :]
            return s

        def chunk(n, carry):
            ncg, ncv, acc = carry
            idx = nch - 1 - n
            r0 = pl.multiple_of(idx * CONV_ROWS, CONV_ROWS)
            rows = pl.ds(r0, CONV_ROWS)
            before = pl.ds(pl.multiple_of(jnp.maximum(r0 - SUBLANES, 0), SUBLANES), SUBLANES)
            in_tile = idx > 0
            da = da_ref[rows, :].astype(F32)

            def half(h_ref, halo_ref, w, b):
                hh = h_ref[rows, :]
                prev = jnp.where(in_tile, h_ref[before, :], halo_ref[...] * keep)
                h1 = _shift_down(hh, 1, prev)
                h2 = _shift_down(hh, 2, prev)
                return hh, h1, h2, b + w[0:1, :] * h2 + w[1:2, :] * h1 + w[2:3, :] * hh

            hg, hg1, hg2, cg = half(hg_ref, hgh_ref, wg, bg)
            hv, hv1, hv2, cv = half(hv_ref, hvh_ref, wv, bv)
            dcg = da * cv * _gelu_grad(cg)
            dcv = da * _gelu(cg)

            def back(dc, hh, h1, h2, w, nxt, part):
                dh = w[2:3, :] * dc + w[1:2, :] * _shift_up(dc, 1, nxt) + w[0:1, :] * _shift_up(dc, 2, nxt)
                dh_ref[part, rows, :] = dh.astype(BF16)
                return [fold(dc * h2), fold(dc * h1), fold(dc * hh), fold(dc)]

            sums = back(dcg, hg, hg1, hg2, wg, ncg, 0) + back(dcv, hv, hv1, hv2, wv, ncv, 1)
            return dcg[0:SUBLANES, :], dcv[0:SUBLANES, :], [a + s for a, s in zip(acc, sums)]

        zero = jnp.zeros((SUBLANES, CONV_CB), F32)
        ncg, ncv, acc = lax.fori_loop(0, nch, chunk, (car_ref[0:SUBLANES, :], car_ref[SUBLANES:, :], [zero] * 8))
        car_ref[0:SUBLANES, :] = ncg
        car_ref[SUBLANES:, :] = ncv
        for half_acc, dw_ref, db_ref in ((acc[0:4], dwg_ref, dbg_ref), (acc[4:8], dwv_ref, dbv_ref)):
            for k in range(3):
                dw_ref[k:k + 1, :] += jnp.sum(half_acc[k], axis=0, keepdims=True)
            db_ref[...] += jnp.sum(half_acc[3], axis=0, keepdims=True)

    grev = lambda c, r: (nr - 1 - r, c)
    vrev = lambda c, r: (nr - 1 - r, CONV_NB + c)
    ghalo = lambda c, r: (jnp.maximum((nr - 1 - r) * tpb - 1, 0), c)
    vhalo = lambda c, r: (jnp.maximum((nr - 1 - r) * tpb - 1, 0), CONV_NB + c)
    colg = lambda c, r: (0, c)
    colv = lambda c, r: (0, CONV_NB + c)
    return pl.pallas_call(
        body, name="conv_bwd", grid=(CONV_NB, nr),
        in_specs=[pl.BlockSpec((tl, CONV_CB), grev), pl.BlockSpec((tl, CONV_CB), vrev),
                  pl.BlockSpec((SUBLANES, CONV_CB), ghalo), pl.BlockSpec((SUBLANES, CONV_CB), vhalo),
                  pl.BlockSpec((tl, CONV_CB), grev),
                  pl.BlockSpec((3, CONV_CB), colg), pl.BlockSpec((3, CONV_CB), colv),
                  pl.BlockSpec((1, CONV_CB), colg), pl.BlockSpec((1, CONV_CB), colv)],
        out_specs=[pl.BlockSpec((2, tl, CONV_CB), lambda c, r: (0, nr - 1 - r, c)),
                   pl.BlockSpec((3, CONV_CB), colg), pl.BlockSpec((3, CONV_CB), colg),
                   pl.BlockSpec((1, CONV_CB), colg), pl.BlockSpec((1, CONV_CB), colg)],
        out_shape=[jax.ShapeDtypeStruct((2, l, D_FF), BF16),
                   jax.ShapeDtypeStruct((3, D_FF), F32), jax.ShapeDtypeStruct((3, D_FF), F32),
                   jax.ShapeDtypeStruct((1, D_FF), F32), jax.ShapeDtypeStruct((1, D_FF), F32)],
        scratch_shapes=[pltpu.VMEM((2 * SUBLANES, CONV_CB), F32)],
        compiler_params=_params(("parallel", "arbitrary")),
    )(h, h, h, h, dact, cw, cw, cb, cb)


def _loss_head(ff, x1, tgt, g, tl):
    l = ff.shape[0]

    def body(ff_ref, x1_ref, tg_ref, g_ref, loss_ref, dx2_ref, dff_ref, dg_ref):
        @pl.when(pl.program_id(0) == 0)
        def _():
            loss_ref[...] = jnp.zeros_like(loss_ref)
            dg_ref[...] = jnp.zeros_like(dg_ref)

        f = ff_ref[...]
        gv = g_ref[...]
        n, _ = _rms(f, gv)
        e = x1_ref[...] + n - tg_ref[...]
        loss_ref[...] += 0.5 * jnp.sum(jnp.mean(e * e, axis=-1, keepdims=True), axis=0, keepdims=True)
        dx2 = e * (1.0 / D_MODEL)
        dx2_ref[...] = dx2
        dff, dg = _rms_bwd(dx2, f, gv)
        dff_ref[...] = dff.astype(BF16)
        dg_ref[...] += dg

    return pl.pallas_call(
        body, name="loss_head", grid=(l // tl,),
        in_specs=[_row(tl, D_MODEL), _row(tl, D_MODEL), _row(tl, D_MODEL), _const((1, D_MODEL))],
        out_specs=[_const((1, LANES)), _row(tl, D_MODEL), _row(tl, D_MODEL), _const((1, D_MODEL))],
        out_shape=[jax.ShapeDtypeStruct((1, LANES), F32), jax.ShapeDtypeStruct((l, D_MODEL), F32),
                   jax.ShapeDtypeStruct((l, D_MODEL), BF16), jax.ShapeDtypeStruct((1, D_MODEL), F32)],
        compiler_params=_params(("arbitrary",)),
    )(ff, x1, tgt, g)


def _ssm_disc(lam_re, lam_im, log_dt, b_re, b_im):
    dt = jnp.exp(log_dt)[:, None]
    mag = jnp.exp(lam_re * dt)
    ang = lam_im * dt
    a_re, a_im = mag * jnp.cos(ang), mag * jnp.sin(ang)
    den = lam_re * lam_re + lam_im * lam_im
    n_re, n_im = a_re - 1.0, a_im
    z_re = (n_re * lam_re + n_im * lam_im) / den
    z_im = (n_im * lam_re - n_re * lam_im) / den
    bb_re = z_re[..., None] * b_re - z_im[..., None] * b_im
    bb_im = z_re[..., None] * b_im + z_im[..., None] * b_re
    return a_re, a_im, bb_re, bb_im


_GPB = SSM_CB // SSM_P


def _embed_b(bb):
    t = bb.transpose(0, 2, 1).reshape(SSM_NB, _GPB, SSM_H, SSM_P)
    return jnp.einsum('mjhp,jk->mjhkp', t, jnp.eye(_GPB, dtype=bb.dtype)).reshape(SSM_NB, SSM_UB, SSM_CB)


def _extract_b(d):
    t = d.reshape(SSM_NB, _GPB, SSM_H, _GPB, SSM_P)
    t = jnp.einsum('mjhkp,jk->mjhp', t, jnp.eye(_GPB, dtype=d.dtype))
    return t.reshape(SSM_G, SSM_H, SSM_P).transpose(0, 2, 1)


def _embed_c(c):
    t = c.transpose(0, 2, 1).reshape(SSM_NB, _GPB, SSM_P, SSM_H)
    return jnp.einsum('mjph,jk->mjpkh', t, jnp.eye(_GPB, dtype=c.dtype)).reshape(SSM_NB, SSM_CB, SSM_UB)


def _extract_c(d):
    t = d.reshape(SSM_NB, _GPB, SSM_P, _GPB, SSM_H)
    t = jnp.einsum('mjpkh,jk->mjph', t, jnp.eye(_GPB, dtype=d.dtype))
    return t.reshape(SSM_G, SSM_P, SSM_H).transpose(0, 2, 1)


def _scan_tables(a_re, a_im, reverse):
    ar = a_re.reshape(1, SSM_CH)
    ai = (-a_im if reverse else a_im).reshape(1, SSM_CH)
    pr, pi = [ar], [ai]
    for _ in range(SUBLANES - 1):
        pr, pi = pr + [pr[-1] * ar - pi[-1] * ai], pi + [pr[-1] * ai + pi[-1] * ar]
    rows = jnp.arange(SUBLANES)[:, None]
    out = []
    for k in (1, 2, 4):
        valid = (rows + k <= SUBLANES - 1) if reverse else (rows >= k)
        out += [jnp.where(valid, pr[k - 1], 0.0), jnp.where(valid, pi[k - 1], 0.0)]
    order = list(range(SUBLANES - 1, -1, -1)) if reverse else list(range(SUBLANES))
    out += [jnp.concatenate([pr[n] for n in order], axis=0), jnp.concatenate([pi[n] for n in order], axis=0)]
    return jnp.stack(out).astype(F32)


def _pad_heads(w, d):
    lead = w.shape[:-1]
    w = w.reshape(lead + (N_HEADS, d))
    w = jnp.pad(w, [(0, 0)] * len(lead) + [(0, 0), (0, HEAD_SLOT - d)])
    return w.reshape(lead + (HP,))


def _unpad_heads(w, d):
    lead = w.shape[:-1]
    return w.reshape(lead + (N_HEADS, HEAD_SLOT))[..., :d].reshape(lead + (N_HEADS * d,))


def _chip_major(w, axis):
    k, n = w.shape
    if axis == 0:
        return w.reshape(N_CHIPS, k // N_CHIPS, n)
    return w.reshape(k, N_CHIPS, n // N_CHIPS).transpose(1, 0, 2)


def _from_chip_major(w, axis):
    if axis == 0:
        return w.reshape(-1, w.shape[2])
    return w.transpose(1, 0, 2).reshape(w.shape[1], -1)


def _pad_w_in(w):
    z = lambda n: jnp.zeros((w.shape[0], n), w.dtype)
    return jnp.concatenate([w[:, :640], z(KR_LANE), w[:, 640:672], z(HEAD_SLOT - KR_LANE - QK_ROPE), w[:, 672:]], axis=1)


def _unpad_w_in(w):
    return jnp.concatenate([w[:, :640], w[:, P_KR + KR_LANE:P_KR + KR_LANE + QK_ROPE], w[:, P_U:]], axis=1)


def _local_step(x, positions, tgt, wts, sp):
    l = x.shape[0]
    tl = min(512, l)
    tb = min(256, l)
    ta = min(512, l)
    ts = min(2048, l)

    inv_freq = ROPE_THETA ** (-jnp.arange(0, QK_ROPE, 2, dtype=F32) / QK_ROPE)
    ang = positions.astype(F32)[:, None] * inv_freq
    cos, sin = jnp.cos(ang), jnp.sin(ang)
    one = jnp.ones((l, KR_LANE), F32)
    rc = jnp.concatenate([one, cos, cos, jnp.ones((l, HEAD_SLOT - KR_LANE - QK_ROPE), F32)], axis=1)
    rs = jnp.concatenate([0 * one, -sin, sin, jnp.zeros((l, HEAD_SLOT - KR_LANE - QK_ROPE), F32)], axis=1)

    win = _pad_w_in(wts["w_in"])
    wuq = _pad_heads(wts["w_uq"], QK_HEAD)
    wukv = jnp.concatenate([_pad_heads(wts["w_uk"], QK_NOPE), _pad_heads(wts["w_uv"], V_HEAD)], axis=1)

    disc_in = (sp["ssm_lambda_re"], sp["ssm_lambda_im"], sp["ssm_log_dt"], sp["ssm_b_re"], sp["ssm_b_im"])
    (a_re, a_im, bb_re, bb_im), disc_vjp = jax.vjp(_ssm_disc, *disc_in)
    bre, bim = _embed_b(bb_re).astype(BF16), _embed_b(bb_im).astype(BF16)
    cre, cim = _embed_c(sp["ssm_c_re"]).astype(BF16), _embed_c(sp["ssm_c_im"]).astype(BF16)
    dvec = sp["ssm_d"].reshape(1, SSM_W)
    tab_f = _scan_tables(a_re, a_im, False)
    tab_r = _scan_tables(a_re, a_im, True)

    g1, gq, gkv = sp["mix_norm_pre"], sp["q_norm"], sp["kv_norm"]
    gpost, gpre, gfin = sp["mix_norm_post"], sp["ffn_norm_pre"], sp["ffn_norm_post"]
    bgate, bglu, convb = sp["b_gate"], sp["b_glu"], sp["conv_b"]

    hn, cq, ckv, q, k, v, u, gl = _proj_fwd(x, g1, win, gq, wuq, gkv, wukv, rc, rs, bgate, tl)
    attn, lse = _attn_fwd(q, k, v, ta, FWD_HEADS)
    y1, sre, sim = _ssm_fwd(u, bre, bim, cre, cim, dvec, tab_f, ts)
    wba = jnp.pad(wts["w_branch_attn"].reshape(N_HEADS, V_HEAD, D_MODEL),
                  ((0, 0), (0, HEAD_SLOT - V_HEAD), (0, 0))).reshape(HP, D_MODEL)
    wbs, wglu, wout = wts["w_branch_ssm"], wts["w_glu"], wts["w_out"]
    a, sm, merged, z, x1, hn2, y3 = _merge_fwd(x, gl, attn, y1, wba, wbs, wglu, bglu, wout, gpost, gpre, tl)
    late = wts["late"](x1)
    wup, wdown, convw = late["w_up"], late["w_down"], late["conv_w"]
    h = _mm(hn2, wup, "ffn_up", tm_cap=2048)
    act = _conv_fwd(h, convw, convb, l)
    ff = _mm(act, wdown, "ffn_down")
    loss, dx2, dff, dgfin = _loss_head(ff, x1, tgt, gfin, ta)

    dact = _mm(dff, wdown, "ffn_down_dx", out_dtype=BF16, bt=True, tm_cap=2048)
    d_wdown = _mm_tn(act, dff, "ffn_down_dw", tk_cap=D_FF // 2)
    dh, dwg, dwv, dbg, dbv = _conv_bwd(h, dact, convw, convb, l)
    d_convw = jnp.concatenate([dwg, dwv], axis=1)
    d_convb = jnp.concatenate([dbg, dbv], axis=1)
    dhn2 = _mm(dh, wup, "ffn_up_dx", bt=True)
    d_wup = _mm_tn(hn2, dh, "ffn_up_dw", chips=True)
    behind = wts["send_grads"]("ffn", {"w_up": d_wup, "w_down": _chip_major(d_wdown, 0)})
    (dx1, dz, dbra, dbrs, dgl, dattn, dy1, dt, y2, dgpre, dgpost, dbgate, dbglu) = _merge_bwd(
        dhn2, x1, dx2, z, gl, a, sm, y1, wba, wbs, wglu, bglu, wout, gpost, gpre + behind, tb)
    d_wout = _mm_tn(merged, dz, "w_out_dw")
    d_wba = _mm_tn(attn, dbra, "w_branch_attn_dw", chips=True)
    d_wbs = _mm_tn(y3, dbrs, "w_branch_ssm_dw", chips=True)
    d_wglu = _mm_tn(y2, dt, "w_glu_dw")
    ncol = D_MODEL // N_CHIPS
    behind = wts["send_grads"]("mix", {
        "w_glu": _chip_major(d_wglu, 0),
        "w_branch_attn": d_wba.reshape(N_CHIPS, N_HEADS, HEAD_SLOT, ncol)[:, :, :V_HEAD].reshape(
            N_CHIPS, N_HEADS * V_HEAD, ncol),
        "w_branch_ssm": d_wbs,
        "w_out": _chip_major(d_wout, 0)}, after=d_wglu)
    dq, dk, dv = _attn_bwd(q, k, v, dattn, lse + behind, _attn_delta(attn, dattn, min(2048, l), BWD_HEADS), ta,
                            BWD_HEADS)
    behind = wts["send_grads"]("none", {}, after=dq)
    du, dbre, dbim, dcre, dcim, dare, daim, dd = _ssm_bwd(dy1, u, sre, sim, bre, bim, cre, cim, dvec + behind, tab_r,
                                                          ts)
    gx, dql, qn, ckvn, dproj, dg1, dgq, dgkv = _proj_bwd(
        x, dx1, cq, ckv, dq, dk, dv, du, dgl, g1, win, gq, wuq, gkv, wukv, rc, rs, tl)
    d_win = _mm_tn(hn, dproj, "w_in_dw")
    d_wuq = _mm_tn(qn, dql, "w_uq_dw")
    d_wuk = _mm_tn(ckvn, dk, "w_uk_dw")
    d_wuv = _mm_tn(ckvn, dv, "w_uv_dw")

    d_lre, d_lim, d_ldt, d_bre, d_bim = disc_vjp((dare.reshape(SSM_G, SSM_P), daim.reshape(SSM_G, SSM_P),
                                                  _extract_b(dbre), _extract_b(dbim)))
    big = {
        "w_in": _chip_major(_unpad_w_in(d_win), 1),
        "w_uq": _chip_major(_unpad_heads(d_wuq, QK_HEAD), 1),
        "w_uk": _chip_major(_unpad_heads(d_wuk, QK_NOPE), 1),
        "w_uv": _chip_major(_unpad_heads(d_wuv, V_HEAD), 1),
    }
    small = {
        "conv_w": d_convw,
        "mix_norm_pre": dg1, "q_norm": dgq, "kv_norm": dgkv,
        "ssm_lambda_re": d_lre, "ssm_lambda_im": d_lim, "ssm_log_dt": d_ldt,
        "ssm_b_re": d_bre, "ssm_b_im": d_bim,
        "ssm_c_re": _extract_c(dcre), "ssm_c_im": _extract_c(dcim),
        "ssm_d": dd.reshape(SSM_G, SSM_H), "b_glu": dbglu, "b_gate": dbgate,
        "mix_norm_post": dgpost, "ffn_norm_pre": dgpre, "conv_b": d_convb, "ffn_norm_post": dgfin,
    }
    return loss[0, 0], gx, big, small


_ANY = pl.BlockSpec(memory_space=pl.ANY)


ROW_TILE = 16


def _place():
    x, y, c = lax.axis_index("x"), lax.axis_index("y"), lax.axis_index("c")
    return x, y, c, 2 * x + y, [(1 - x, y), (x, 1 - y), (1 - x, 1 - y)]


def _half(rows, which):
    hr = rows // 2
    return pl.ds(pl.multiple_of(which * hr, ROW_TILE), hr)


def _remote(src, dst, send_sems, recv_sems, n, dev):
    return pltpu.make_async_remote_copy(src_ref=src, dst_ref=dst, send_sem=send_sems.at[n], recv_sem=recv_sems.at[n],
                                        device_id=dev, device_id_type=MESH)


def _gather_big(shards):
    nw = len(shards)
    rows = [s.shape[0] for s in shards]

    def body(*refs):
        ins, outs = refs[:nw], refs[nw:2 * nw]
        ici_send, ici_recv, d2d_send, d2d_recv = refs[2 * nw:]
        x, y, c, me, peers = _place()
        sent = []
        for i in range(nw):
            for p, (px, py) in enumerate(peers):
                cp = _remote(ins[i].at[_half(rows[i], c)], outs[i].at[me, _half(rows[i], c)], ici_send, ici_recv,
                             3 * i + p, (px, py, c))
                cp.start()
                sent.append(cp)
        for i in range(nw):
            for p, (px, py) in enumerate(peers):
                blk = outs[i].at[2 * px + py, _half(rows[i], c)]
                _remote(blk, blk, ici_send, ici_recv, 3 * i + p, (px, py, c)).wait_recv()
                cp = _remote(blk, blk, d2d_send, d2d_recv, 3 * i + p, (x, y, 1 - c))
                cp.start()
                sent.append(cp)
        for p, (px, py) in enumerate(peers):
            for i in range(nw):
                blk = outs[i].at[2 * px + py, _half(rows[i], 1 - c)]
                _remote(blk, blk, d2d_send, d2d_recv, 3 * i + p, (x, y, 1 - c)).wait_recv()
        for cp in sent:
            cp.wait_send()

    dma = pltpu.SemaphoreType.DMA
    return pl.pallas_call(
        body, name="gather_weights", in_specs=[_ANY] * nw, out_specs=[_ANY] * nw,
        out_shape=[jax.ShapeDtypeStruct((N_CHIPS,) + s.shape, s.dtype) for s in shards],
        scratch_shapes=[dma((3 * nw,)), dma((3 * nw,)), dma((3 * nw,)), dma((3 * nw,))],
    )(*shards)


_HBM = pl.BlockSpec(memory_space=pltpu.HBM)
_SEM = pl.BlockSpec(memory_space=pltpu.SEMAPHORE)
_DATAFLOW = pltpu.SideEffectType.DATAFLOW_SIDE_EFFECTING


def _exchange_start(shards, name, scatter):
    nw = len(shards)
    lands = [lax.empty(s.shape if scatter else (N_CHIPS,) + s.shape, s.dtype) for s in shards]

    def body(*refs):
        ins, zones = refs[:nw], refs[nw:2 * nw]
        send_sems, recv_sems, token = refs[2 * nw], refs[2 * nw + 1], refs[-1]
        x, y, c, me, peers = _place()
        for i in range(nw):
            for p, (px, py) in enumerate(peers):
                src = ins[i].at[2 * px + py] if scatter else ins[i]
                _remote(src, zones[i].at[me], send_sems, recv_sems, 3 * i + p, (px, py, c)).start()
        token[...] = jnp.zeros_like(token)

    thru = [pltpu.HBM(a.shape, a.dtype) for a in list(shards) + lands]
    dma = pltpu.SemaphoreType.DMA
    outs = pl.pallas_call(
        body, name=name,
        out_shape=(dma((3 * nw,)), dma((3 * nw,)), *thru, jax.ShapeDtypeStruct((SUBLANES, LANES), F32)),
        in_specs=[_HBM] * (2 * nw),
        out_specs=(_SEM, _SEM, *([_HBM] * (2 * nw)), pl.BlockSpec(memory_space=pltpu.VMEM)),
        input_output_aliases={i: 2 + i for i in range(2 * nw)},
        compiler_params=pltpu.CompilerParams(has_side_effects=_DATAFLOW),
    )(*[pltpu.with_memory_space_constraint(a, pltpu.HBM) for a in list(shards) + lands])
    return outs[0], outs[1], list(outs[2:2 + nw]), list(outs[2 + nw:2 + 2 * nw]), outs[-1]


def _exchange_wait(send_sems, recv_sems, shards, lands, after, name, scatter):
    nw = len(shards)

    def body(*refs):
        ins, zones = refs[:nw], refs[nw:2 * nw]
        send_sems, recv_sems = refs[2 * nw], refs[2 * nw + 1]
        x, y, c, me, peers = _place()
        for i in range(nw):
            for p, (px, py) in enumerate(peers):
                src = ins[i].at[2 * px + py] if scatter else ins[i]
                cp = _remote(src, zones[i].at[2 * px + py], send_sems, recv_sems, 3 * i + p, (px, py, c))
                cp.wait_send()
                cp.wait_recv()

    both = list(shards) + list(lands)
    outs = pl.pallas_call(
        body, name=name,
        out_shape=tuple(pltpu.HBM(a.shape, a.dtype) for a in both),
        in_specs=(*([_HBM] * (2 * nw)), _SEM, _SEM, _ANY), out_specs=[_HBM] * (2 * nw),
        input_output_aliases={i: i for i in range(2 * nw)},
        compiler_params=pltpu.CompilerParams(has_side_effects=_DATAFLOW),
    )(*both, send_sems, recv_sems, after)
    return list(outs[:nw]), list(outs[nw:])


def _sibling_start(grads, name):
    nw = len(grads)
    lands = [lax.empty((N_CHIPS, g.shape[1] // 2, g.shape[2]), g.dtype) for g in grads]

    def body(*refs):
        ins, zones = refs[:nw], refs[nw:2 * nw]
        send_sems, recv_sems, token = refs[2 * nw], refs[2 * nw + 1], refs[-1]
        x, y, c, _, _ = _place()
        for i in range(nw):
            _remote(ins[i].at[pl.ds(0, N_CHIPS), _half(grads[i].shape[1], 1 - c)], zones[i], send_sems, recv_sems,
                    i, (x, y, 1 - c)).start()
        token[...] = jnp.zeros_like(token)

    thru = [pltpu.HBM(a.shape, a.dtype) for a in list(grads) + lands]
    dma = pltpu.SemaphoreType.DMA
    outs = pl.pallas_call(
        body, name=name,
        out_shape=(dma((nw,)), dma((nw,)), *thru, jax.ShapeDtypeStruct((SUBLANES, LANES), F32)),
        in_specs=[_HBM] * (2 * nw),
        out_specs=(_SEM, _SEM, *([_HBM] * (2 * nw)), pl.BlockSpec(memory_space=pltpu.VMEM)),
        input_output_aliases={i: 2 + i for i in range(2 * nw)},
        compiler_params=pltpu.CompilerParams(has_side_effects=_DATAFLOW),
    )(*[pltpu.with_memory_space_constraint(a, pltpu.HBM) for a in list(grads) + lands])
    return outs[0], outs[1], list(outs[2:2 + nw]), list(outs[2 + nw:2 + 2 * nw]), outs[-1]


def _sibling_wait(send_sems, recv_sems, grads, lands, after, name):
    nw = len(grads)

    def body(*refs):
        ins, zones = refs[:nw], refs[nw:2 * nw]
        send_sems, recv_sems = refs[2 * nw], refs[2 * nw + 1]
        x, y, c, _, _ = _place()
        for i in range(nw):
            cp = _remote(ins[i].at[pl.ds(0, N_CHIPS), _half(grads[i].shape[1], 1 - c)], zones[i], send_sems, recv_sems,
                         i, (x, y, 1 - c))
            cp.wait_send()
            cp.wait_recv()

    both = list(grads) + list(lands)
    outs = pl.pallas_call(
        body, name=name,
        out_shape=tuple(pltpu.HBM(a.shape, a.dtype) for a in both),
        in_specs=(*([_HBM] * (2 * nw)), _SEM, _SEM, _ANY), out_specs=[_HBM] * (2 * nw),
        input_output_aliases={i: i for i in range(2 * nw)},
        compiler_params=pltpu.CompilerParams(has_side_effects=_DATAFLOW),
    )(*both, send_sems, recv_sems, after)
    return list(outs[:nw]), list(outs[nw:])


def _reduce_to_sibling(grads, name):
    nw = len(grads)

    def body(*refs):
        ins, outs = refs[:nw], refs[nw:2 * nw]
        send_sems, recv_sems = refs[2 * nw:]
        x, y, c, _, _ = _place()
        sent = []
        for i in range(nw):
            cp = _remote(ins[i].at[pl.ds(0, N_CHIPS), _half(grads[i].shape[1], 1 - c)], outs[i], send_sems, recv_sems,
                         i, (x, y, 1 - c))
            cp.start()
            sent.append(cp)
        for cp in sent:
            cp.wait()

    dma = pltpu.SemaphoreType.DMA
    return pl.pallas_call(
        body, name=name, in_specs=[_ANY] * nw, out_specs=[_ANY] * nw,
        out_shape=[jax.ShapeDtypeStruct((N_CHIPS, g.shape[1] // 2, g.shape[2]), g.dtype) for g in grads],
        scratch_shapes=[dma((nw,)), dma((nw,))],
    )(*grads)


def _reduce_back(totals, name):
    nw = len(totals)

    def body(*refs):
        outs = refs[nw:2 * nw]
        send_sems, recv_sems = refs[2 * nw:]
        x, y, c, _, _ = _place()
        sent = []
        for i in range(nw):
            blk = outs[i].at[_half(totals[i].shape[0], c)]
            cp = _remote(blk, blk, send_sems, recv_sems, i, (x, y, 1 - c))
            cp.start()
            sent.append(cp)
        for i in range(nw):
            blk = outs[i].at[_half(totals[i].shape[0], 1 - c)]
            _remote(blk, blk, send_sems, recv_sems, i, (x, y, 1 - c)).wait_recv()
        for cp in sent:
            cp.wait_send()

    dma = pltpu.SemaphoreType.DMA
    return pl.pallas_call(
        body, name=name, in_specs=[_ANY] * nw, out_specs=[_ANY] * nw,
        out_shape=[jax.ShapeDtypeStruct(t.shape, t.dtype) for t in totals],
        input_output_aliases={i: i for i in range(nw)},
        scratch_shapes=[dma((nw,)), dma((nw,))],
    )(*totals)


def _all_reduce_small(v, name):
    rows, w = v.shape
    hr = rows // 2
    assert hr % SUBLANES == 0

    def body(v_ref, out_ref, sib_ref, half_ref, chips_ref, send_sems, recv_sems):
        x, y, c, me, peers = _place()
        sibling = (x, y, 1 - c)
        mine = pl.ds(pl.multiple_of(c * hr, SUBLANES), hr)
        other = pl.ds(pl.multiple_of((1 - c) * hr, SUBLANES), hr)
        cp = _remote(v_ref, sib_ref, send_sems, recv_sems, 0, sibling)
        cp.start()
        cp.wait()
        half_ref[...] = v_ref[mine, :] + sib_ref[mine, :]
        sent = []
        for p, (px, py) in enumerate(peers):
            cp = _remote(half_ref, chips_ref.at[me], send_sems, recv_sems, 1 + p, (px, py, c))
            cp.start()
            sent.append(cp)
        chips_ref[me] = half_ref[...]
        for p, (px, py) in enumerate(peers):
            _remote(half_ref, chips_ref.at[2 * px + py], send_sems, recv_sems, 1 + p, (px, py, c)).wait_recv()
        for cp in sent:
            cp.wait_send()
        out_ref[mine, :] = ((chips_ref[0] + chips_ref[1]) + chips_ref[2]) + chips_ref[3]
        cp = _remote(out_ref.at[mine], out_ref.at[mine], send_sems, recv_sems, 4, sibling)
        cp.start()
        _remote(out_ref.at[other], out_ref.at[other], send_sems, recv_sems, 4, sibling).wait_recv()
        cp.wait_send()

    vm = pl.BlockSpec(memory_space=pltpu.VMEM)
    return pl.pallas_call(
        body, name=name, in_specs=[vm], out_specs=vm,
        out_shape=jax.ShapeDtypeStruct((rows, w), F32),
        scratch_shapes=[pltpu.VMEM((rows, w), F32), pltpu.VMEM((hr, w), F32), pltpu.VMEM((N_CHIPS, hr, w), F32),
                        pltpu.SemaphoreType.DMA((5,)), pltpu.SemaphoreType.DMA((5,))],
        compiler_params=pltpu.CompilerParams(vmem_limit_bytes=VMEM_LIMIT),
    )(v)


ELEMENTWISE_BLOCK = 512 * 1024


def _rows_tile(rows, cols, cap=ELEMENTWISE_BLOCK):
    best = None
    for t in range(SUBLANES, rows + 1, SUBLANES):
        if rows % t == 0 and t * cols <= cap:
            best = t
    return rows if best is None else best


def _add_pair(g, t, core, name):
    nb, n, w = t.shape
    tr = _rows_tile(n, w, 2 * ELEMENTWISE_BLOCK)
    steps = n // tr

    def body(core_ref, g_ref, t_ref, o_ref):
        o_ref[...] = (g_ref[...] + t_ref[...]).astype(BF16)

    spec = pl.BlockSpec((1, tr, w), lambda j, i, core_ref: (j, i, 0))
    return pl.pallas_call(
        body, name=name,
        grid_spec=pltpu.PrefetchScalarGridSpec(
            num_scalar_prefetch=1, grid=(nb, steps),
            in_specs=[pl.BlockSpec((1, tr, w), lambda j, i, core_ref: (j, core_ref[0] * steps + i, 0)), spec],
            out_specs=spec),
        out_shape=jax.ShapeDtypeStruct(t.shape, BF16),
        compiler_params=_params(("parallel", "parallel")))(core, g, t)


def _add_chips(landed, pairs, place, name):
    nb, n, w = landed.shape
    tr = _rows_tile(n, w, 2 * ELEMENTWISE_BLOCK)
    steps = n // tr

    def body(place_ref, r_ref, own_ref, o_ref):
        me = place_ref[0]
        acc = None
        for k in range(nb):
            blk = jnp.where(me == k, own_ref[0], r_ref[k]).astype(F32)
            acc = blk if acc is None else acc + blk
        o_ref[...] = acc

    return pl.pallas_call(
        body, name=name,
        grid_spec=pltpu.PrefetchScalarGridSpec(
            num_scalar_prefetch=1, grid=(steps,),
            in_specs=[pl.BlockSpec((nb, tr, w), lambda i, place_ref: (0, i, 0)),
                      pl.BlockSpec((1, tr, w), lambda i, place_ref: (place_ref[0], i, 0))],
            out_specs=pl.BlockSpec((tr, w), lambda i, place_ref: (place_ref[1] * steps + i, 0))),
        out_shape=jax.ShapeDtypeStruct((2 * n, w), F32),
        compiler_params=_params(("parallel",)))(place, landed, pairs)


def _adamw(w, g, m, v, name):
    rows, wd = w.shape
    tr = _rows_tile(rows, wd)
    c1 = 1.0 - ADAM_B1 ** ADAM_STEP
    c2 = 1.0 - ADAM_B2 ** ADAM_STEP

    def body(w_ref, g_ref, m_ref, v_ref, go_ref, d_ref, mo_ref, vo_ref):
        gv = g_ref[...]
        go_ref[...] = gv
        m2 = ADAM_B1 * m_ref[...] + (1.0 - ADAM_B1) * gv
        v2 = ADAM_B2 * v_ref[...] + (1.0 - ADAM_B2) * (gv * gv)
        mo_ref[...] = m2
        vo_ref[...] = v2
        d_ref[...] = -ADAM_LR * ((m2 / c1) / (jnp.sqrt(v2 / c2) + ADAM_EPS) + ADAM_WD * w_ref[...])

    spec = pl.BlockSpec((tr, wd), lambda i: (i, 0))
    shp = jax.ShapeDtypeStruct((rows, wd), F32)
    return pl.pallas_call(body, name=name, grid=(rows // tr,), in_specs=[spec] * 4, out_specs=[spec] * 4,
                          out_shape=[shp] * 4, compiler_params=_params(("parallel",)))(w, g, m, v)


BIG = [("w_in", (1024, 3232), 1), ("w_uq", (384, 768), 1), ("w_uk", (256, 512), 1), ("w_uv", (256, 512), 1),
       ("w_glu", (512, 512), 0), ("w_branch_attn", (512, 1024), 1), ("w_branch_ssm", (512, 1024), 1),
       ("w_out", (1024, 1024), 0), ("w_up", (1024, 5632), 1), ("conv_w", (3, 5632), 1), ("w_down", (2816, 1024), 0)]
SMALL = [("mix_norm_pre", (1024,)), ("q_norm", (384,)), ("kv_norm", (256,)), ("ssm_lambda_re", (32, 64)),
         ("ssm_lambda_im", (32, 64)), ("ssm_log_dt", (32,)), ("ssm_b_re", (32, 64, 16)), ("ssm_b_im", (32, 64, 16)),
         ("ssm_c_re", (32, 16, 64)), ("ssm_c_im", (32, 16, 64)), ("ssm_d", (32, 16)), ("b_glu", (512,)),
         ("b_gate", (2048,)), ("mix_norm_post", (1024,)), ("ffn_norm_pre", (1024,)), ("conv_b", (5632,)),
         ("ffn_norm_post", (1024,))]
MATMUL_W = [b for b in BIG if b[0] != "conv_w"]
LATE_W = ("w_up", "w_down", "conv_w")
CONV_W_SHAPE = (3, 2 * D_FF)
CONV_W_SHARD = (3, 2 * D_FF // N_CHIPS)
SMALL_SUM = [("loss", (1,))] + SMALL + [("conv_w", CONV_W_SHAPE)]
SMALL_ADAM = SMALL + [("conv_w", CONV_W_SHARD)]


def _pack_flat(layout, vals):
    flat = jnp.concatenate([vals[n].astype(F32).reshape(-1) for n, _ in layout])
    rows = -(-(-(-flat.shape[0] // FLAT_W)) // (2 * SUBLANES)) * 2 * SUBLANES
    return jnp.pad(flat, (0, rows * FLAT_W - flat.shape[0])).reshape(rows, FLAT_W)


def _unpack_flat(layout, flat):
    flat = flat.reshape(-1)
    out = {}
    o = 0
    for name, shape in layout:
        n = math.prod(shape)
        out[name] = flat[o:o + n].reshape(shape)
        o += n
    return out


_ARG_NAMES = ["x", "positions"] + [n for n in (
    "mix_norm_pre", "w_in", "q_norm", "w_uq", "kv_norm", "w_uk", "w_uv", "ssm_lambda_re", "ssm_lambda_im", "ssm_log_dt",
    "ssm_b_re", "ssm_b_im", "ssm_c_re", "ssm_c_im", "ssm_d", "w_glu", "b_glu", "w_branch_attn", "w_branch_ssm",
    "b_gate", "w_out", "mix_norm_post", "ffn_norm_pre", "w_up", "conv_w", "conv_b", "w_down", "ffn_norm_post")]
_WEIGHTS = _ARG_NAMES[2:]


def _gather_weights(w):
    early = [b for b in MATMUL_W if b[0] not in LATE_W]
    late = [b for b in BIG if b[0] in LATE_W]
    own = (jnp.arange(N_CHIPS) == 2 * lax.axis_index("x") + lax.axis_index("y"))[:, None, None]

    def whole(layout, mine, gathered):
        return {name: _from_chip_major(jnp.where(own, s[None], g), axis)
                for (name, _, axis), s, g in zip(layout, mine, gathered)}

    mine = [w[name].astype(BF16) for name, _, _ in early]
    gathered = _gather_big(mine)
    full = whole(early, mine, gathered)
    mine_late = [w[name].astype(F32 if name == "conv_w" else BF16) for name, _, _ in late]
    _, mine_late = lax.optimization_barrier((gathered[0], mine_late))
    send_sems, recv_sems, shards_thru, lands_thru, token = _exchange_start(mine_late, "gather_late_start", scatter=False)

    def late_weights(after):
        shards, lands = _exchange_wait(send_sems, recv_sems, shards_thru, lands_thru, after, "gather_late_wait",
                                       scatter=False)
        return whole(late, shards, lands)

    full["late"] = late_weights
    full["token"] = token[0, 0]
    return full


def _pair_sums(names, grads, tag):
    core = lax.axis_index("c").astype(jnp.int32).reshape(1)
    theirs = _reduce_to_sibling(grads, "reduce_grads_d2d" + tag)
    return [_add_pair(g, t, core, "reduce_pair_" + n) for n, g, t in zip(names, grads, theirs)]


def _send_grads(tag, grads, after, flying, pending):
    token = jnp.zeros((), F32)
    if flying:
        tag0, names0, state0 = flying.pop()
        core = lax.axis_index("c").astype(jnp.int32).reshape(1)
        mine, theirs = _sibling_wait(*state0, after, "reduce_" + tag0 + "_d2d_wait")
        pairs = [_add_pair(g, t, core, "reduce_pair_" + n) for n, g, t in zip(names0, mine, theirs)]
        send_sems, recv_sems, pairs_thru, lands_thru, tok = _exchange_start(pairs, "reduce_" + tag0 + "_start",
                                                                          scatter=True)
        pending.append((tag0, names0, send_sems, recv_sems, pairs_thru, lands_thru))
        token = token + tok[0, 0]
    if grads:
        names = list(grads)
        send_sems, recv_sems, grads_thru, lands_thru, tok = _sibling_start([grads[n] for n in names],
                                                                          "reduce_" + tag + "_d2d_start")
        flying.append((tag, names, (send_sems, recv_sems, grads_thru, lands_thru)))
        token = token + tok[0, 0]
    return token


def _reduce_grads(gbig, pending, loss, gsmall, use_sent):
    core = lax.axis_index("c").astype(jnp.int32).reshape(1)
    chip = (2 * lax.axis_index("x") + lax.axis_index("y")).astype(jnp.int32).reshape(1)
    place = jnp.concatenate([chip, core])

    def finish(names, pairs, landed, name):
        totals = [_add_chips(r, p, place, "reduce_chips_" + n) for n, r, p in zip(names, landed, pairs)]
        return dict(zip(names, _reduce_back(totals, name)))

    names = list(gbig)
    pairs = _pair_sums(names, [gbig[n] for n in names], "")
    send_sems, recv_sems, pairs_thru, lands_thru, token = _exchange_start(pairs, "reduce_last_start", scatter=True)
    sent_names, sent_pairs, sent_landed = [], [], []
    for tag, group, g_send, g_recv, g_pairs, g_lands in pending:
        got_pairs, got_landed = _exchange_wait(g_send, g_recv, g_pairs, g_lands, token, "reduce_" + tag + "_wait",
                                               scatter=True)
        sent_names, sent_pairs, sent_landed = sent_names + group, sent_pairs + got_pairs, sent_landed + got_landed
    g_sent = finish(sent_names, sent_pairs, sent_landed, "reduce_back_sent")
    vals = dict(gsmall)
    vals["loss"] = loss
    small_red = _unpack_flat(SMALL_SUM, _all_reduce_small(_pack_flat(SMALL_SUM, vals), "reduce_small"))
    after = use_sent(g_sent, small_red)
    pairs, landed = _exchange_wait(send_sems, recv_sems, pairs_thru, lands_thru, after, "reduce_last_wait", scatter=True)
    return finish(names, pairs, landed, "reduce_back_last"), small_red


def _step(args):
    x = args["x"][0]
    positions = args["positions"][0]
    tgt = args["loss_target"][0]
    w = {n: args[n][0] for n in _WEIGHTS}
    m = {n: args["m_" + n][0] for n in _WEIGHTS}
    v = {n: args["v_" + n][0] for n in _WEIGHTS}

    full = _gather_weights(w)
    sp = {n: w[n].reshape(s) for n, s in SMALL}
    for n in ("mix_norm_pre", "q_norm", "kv_norm", "b_glu", "b_gate", "mix_norm_post", "ffn_norm_pre", "conv_b",
              "ffn_norm_post"):
        sp[n] = sp[n].reshape(1, -1)
    sp["mix_norm_pre"] = sp["mix_norm_pre"] + full.pop("token")
    pending, flying = [], []
    full["send_grads"] = lambda tag, grads, after=None: _send_grads(tag, grads, after, flying, pending)
    loss, gx, gbig, gsmall = _local_step(x, positions, tgt, full, sp)
    outs = {}

    def adam_big(g_red):
        for name in g_red:
            g2, d, m2, v2 = _adamw(w[name], g_red[name], m[name], v[name], "adamw_" + name)
            outs["grad_" + name], outs["delta_" + name], outs["new_m_" + name], outs["new_v_" + name] = g2, d, m2, v2
        return v2

    def use_sent(g_sent, small_red):
        chip = 2 * lax.axis_index("x") + lax.axis_index("y")
        grads = dict(small_red)
        grads["conv_w"] = lax.dynamic_slice_in_dim(small_red["conv_w"], chip * CONV_W_SHARD[1], CONV_W_SHARD[1], axis=1)
        outs.update({"grad_" + n: grads[n] for n, _ in SMALL_ADAM})
        _, d_sm, m_sm, v_sm = _adamw(_pack_flat(SMALL_ADAM, w), _pack_flat(SMALL_ADAM, grads),
                                     _pack_flat(SMALL_ADAM, m), _pack_flat(SMALL_ADAM, v), "adamw_small")
        for prefix, flat in (("delta_", d_sm), ("new_m_", m_sm), ("new_v_", v_sm)):
            for n, val in _unpack_flat(SMALL_ADAM, flat).items():
                outs[prefix + n] = val
        return adam_big(g_sent)

    g_last, small_red = _reduce_grads(gbig, pending, loss, gsmall, use_sent)
    adam_big(g_last)
    outs = {n: val.reshape(args[n.split("_", 1)[1] if not n.startswith("new_") else n[6:]].shape)
            for n, val in outs.items()}
    res = [small_red["loss"][0], gx[None]]
    for prefix in ("grad_", "delta_", "new_m_", "new_v_"):
        res += [outs[prefix + n] for n in _WEIGHTS]
    return tuple(res)


def kernel(x, positions, mix_norm_pre, w_in, q_norm, w_uq, kv_norm, w_uk, w_uv, ssm_lambda_re, ssm_lambda_im, ssm_log_dt, ssm_b_re, ssm_b_im, ssm_c_re, ssm_c_im, ssm_d, w_glu, b_glu, w_branch_attn, w_branch_ssm, b_gate, w_out, mix_norm_post, ffn_norm_pre, w_up, conv_w, conv_b, w_down, ffn_norm_post, loss_target, m_mix_norm_pre, m_w_in, m_q_norm, m_w_uq, m_kv_norm, m_w_uk, m_w_uv, m_ssm_lambda_re, m_ssm_lambda_im, m_ssm_log_dt, m_ssm_b_re, m_ssm_b_im, m_ssm_c_re, m_ssm_c_im, m_ssm_d, m_w_glu, m_b_glu, m_w_branch_attn, m_w_branch_ssm, m_b_gate, m_w_out, m_mix_norm_post, m_ffn_norm_pre, m_w_up, m_conv_w, m_conv_b, m_w_down, m_ffn_norm_post, v_mix_norm_pre, v_w_in, v_q_norm, v_w_uq, v_kv_norm, v_w_uk, v_w_uv, v_ssm_lambda_re, v_ssm_lambda_im, v_ssm_log_dt, v_ssm_b_re, v_ssm_b_im, v_ssm_c_re, v_ssm_c_im, v_ssm_d, v_w_glu, v_b_glu, v_w_branch_attn, v_w_branch_ssm, v_b_gate, v_w_out, v_mix_norm_post, v_ffn_norm_pre, v_w_up, v_conv_w, v_conv_b, v_w_down, v_ffn_norm_post):
    given = dict(locals())
    return _step(given)
```
